```python
import jax, jax.numpy as jnp
from jax import lax
import numpy as np

D_MODEL = 1024
BATCH = 8
SEQ = 4096
DEPTH = 1

MLA_HEADS = 4
V_HEAD_DIM = D_MODEL // (2 * MLA_HEADS)
QK_NOPE_DIM = V_HEAD_DIM
QK_ROPE_DIM = V_HEAD_DIM // 2
Q_LORA_RANK = D_MODEL // 4
KV_LORA_RANK = D_MODEL // 8
MLA_WIDTH = MLA_HEADS * V_HEAD_DIM
POOL_WIDTH = D_MODEL - MLA_WIDTH
POOL_WINDOWS = (2, 4, 8, 16)
POOL_GROUPS = len(POOL_WINDOWS)
POOL_CH = POOL_WIDTH // POOL_GROUPS
IN_PROJ_DIM = Q_LORA_RANK + KV_LORA_RANK + QK_ROPE_DIM + POOL_WIDTH
D_FF = ((8 * D_MODEL // 3 + 255) // 256) * 256
N_MEM = 256
MEM_HEADS = 4
MEM_HEAD_DIM = D_MODEL // MEM_HEADS
ROPE_BASE = 10000.0
RMS_EPS = 1e-6
BLOCK_Q = 128

kernel_name = "hymba_mla_pool_macaron_memxattn"


def rmsnorm(x, g):
    xf = x.astype(jnp.float32)
    y = xf * lax.rsqrt(jnp.mean(xf * xf, axis=-1, keepdims=True) + RMS_EPS)
    return (y * g.astype(jnp.float32)).astype(x.dtype)


def swiglu(x, w_gate, w_up, w_down):
    return (jax.nn.silu(x @ w_gate) * (x @ w_up)) @ w_down


def rope(x, positions):
    d = x.shape[-1]
    freqs = 1.0 / (ROPE_BASE ** (jnp.arange(0, d, 2, dtype=jnp.float32) / d))
    ang = positions.astype(jnp.float32)[..., None] * freqs
    cos = jnp.cos(ang)[:, :, None, :]
    sin = jnp.sin(ang)[:, :, None, :]
    xf = x.astype(jnp.float32)
    x1, x2 = xf[..., : d // 2], xf[..., d // 2:]
    out = jnp.concatenate([x1 * cos - x2 * sin, x1 * sin + x2 * cos], axis=-1)
    return out.astype(x.dtype)


def causal_block_attention(q, k, v, scale):
    B, S, H, Dqk = q.shape
    nblk = S // BLOCK_Q
    qb = q.reshape(B, nblk, BLOCK_Q, H, Dqk).transpose(1, 0, 2, 3, 4)
    key_pos = jnp.arange(S)

    def one_block(args):
        qi, i = args
        s = jnp.einsum('bqhd,bkhd->bhqk', qi, k).astype(jnp.float32) * scale
        q_pos = i * BLOCK_Q + jnp.arange(BLOCK_Q)
        mask = key_pos[None, :] <= q_pos[:, None]
        s = jnp.where(mask[None, None], s, -jnp.inf)
        p = jax.nn.softmax(s, axis=-1).astype(v.dtype)
        return jnp.einsum('bhqk,bkhd->bqhd', p, v)

    out = lax.map(one_block, (qb, jnp.arange(nblk)))
    return out.transpose(1, 0, 2, 3, 4).reshape(B, S, H, v.shape[-1])


def mla_group(c_q, c_kv, k_rope, positions, q_norm, w_q_up, kv_norm, w_kv_up):
    B, S, _ = c_q.shape
    q = (rmsnorm(c_q, q_norm) @ w_q_up).reshape(B, S, MLA_HEADS, QK_NOPE_DIM + QK_ROPE_DIM)
    q_nope, q_pe = q[..., :QK_NOPE_DIM], q[..., QK_NOPE_DIM:]
    q_pe = rope(q_pe, positions)
    kv = (rmsnorm(c_kv, kv_norm) @ w_kv_up).reshape(B, S, MLA_HEADS, QK_NOPE_DIM + V_HEAD_DIM)
    k_nope, v = kv[..., :QK_NOPE_DIM], kv[..., QK_NOPE_DIM:]
    k_pe = rope(k_rope[:, :, None, :], positions)
    k_pe = jnp.broadcast_to(k_pe, (B, S, MLA_HEADS, QK_ROPE_DIM))
    q_full = jnp.concatenate([q_nope, q_pe], axis=-1)
    k_full = jnp.concatenate([k_nope, k_pe], axis=-1)
    scale = (QK_NOPE_DIM + QK_ROPE_DIM) ** -0.5
    o = causal_block_attention(q_full, k_full, v, scale)
    return o.reshape(B, S, MLA_WIDTH)


def pool_group(z, pool_w, pool_scale):
    B, S, _ = z.shape
    zg = z.reshape(B, S, POOL_GROUPS, POOL_CH)
    cs = jnp.cumsum(zg.astype(jnp.float32), axis=1)
    cs = jnp.pad(cs, ((0, 0), (1, 0), (0, 0), (0, 0)))
    t1 = jnp.arange(1, S + 1, dtype=jnp.float32)
    pooled = []
    for g, w in enumerate(POOL_WINDOWS):
        hi = cs[:, 1:, g]
        lo = jnp.pad(cs[:, : S + 1 - w, g], ((0, 0), (w - 1, 0), (0, 0)))
        count = jnp.minimum(t1, float(w))[None, :, None]
        pooled.append((hi - lo) / count)
    mean = jnp.stack(pooled, axis=2).astype(z.dtype)
    y = jnp.einsum('bsgc,gcd->bsgd', mean - zg, pool_w).reshape(B, S, POOL_WIDTH)
    return y * pool_scale


def memory_cross_attention(hn, memn, w_mq, w_mkv, w_mo):
    B, S, _ = hn.shape
    q = (hn @ w_mq).reshape(B, S, MEM_HEADS, MEM_HEAD_DIM)
    kv = (memn @ w_mkv).reshape(B, N_MEM, 2, MEM_HEADS, MEM_HEAD_DIM)
    k, v = kv[:, :, 0], kv[:, :, 1]
    s = jnp.einsum('bshd,bmhd->bhsm', q, k).astype(jnp.float32) * (MEM_HEAD_DIM ** -0.5)
    p = jax.nn.softmax(s, axis=-1).astype(v.dtype)
    o = jnp.einsum('bhsm,bmhd->bshd', p, v).reshape(B, S, D_MODEL)
    return o @ w_mo


def _fwd_setup_inputs(seed: int = 0) -> dict:
    key = jax.random.key(seed)
    ks = iter(jax.random.split(key, 32))

    def dense(shape, fan_in, mult=1.0):
        return jax.random.normal(next(ks), shape, jnp.float32) * (mult * fan_in ** -0.5)

    def gain(shape):
        return 1.0 + 0.02 * jax.random.normal(next(ks), shape, jnp.float32)

    L = DEPTH
    x = jax.random.normal(next(ks), (BATCH, SEQ, D_MODEL), jnp.float32)
    mem = jax.random.normal(next(ks), (BATCH, N_MEM, D_MODEL), jnp.float32)
    offset = jax.random.randint(next(ks), (BATCH, 1), 0, 1024, dtype=jnp.int32)
    positions = (jnp.arange(SEQ, dtype=jnp.int32)[None, :] + offset).astype(jnp.int32)
    return {
        "x": x,
        "mem": mem,
        "positions": positions,
        "ffn1_norm": gain((L, D_MODEL)),
        "ffn1_w_gate": dense((L, D_MODEL, D_FF), D_MODEL),
        "ffn1_w_up": dense((L, D_MODEL, D_FF), D_MODEL),
        "ffn1_w_down": dense((L, D_FF, D_MODEL), D_FF),
        "mix_norm": gain((L, D_MODEL)),
        "w_in": dense((L, D_MODEL, IN_PROJ_DIM), D_MODEL),
        "q_norm": gain((L, Q_LORA_RANK)),
        "w_q_up": dense((L, Q_LORA_RANK, MLA_HEADS * (QK_NOPE_DIM + QK_ROPE_DIM)), Q_LORA_RANK),
        "kv_norm": gain((L, KV_LORA_RANK)),
        "w_kv_up": dense((L, KV_LORA_RANK, MLA_HEADS * (QK_NOPE_DIM + V_HEAD_DIM)), KV_LORA_RANK),
        "pool_w": dense((L, POOL_GROUPS, POOL_CH, POOL_CH), POOL_CH),
        "pool_scale": 0.5 + 0.05 * jax.random.normal(next(ks), (L, POOL_WIDTH), jnp.float32),
        "w_out": dense((L, MLA_WIDTH + POOL_WIDTH, D_MODEL), MLA_WIDTH + POOL_WIDTH),
        "xattn_norm": gain((L, D_MODEL)),
        "mem_norm": gain((L, D_MODEL)),
        "w_mq": dense((L, D_MODEL, D_MODEL), D_MODEL),
        "w_mkv": dense((L, D_MODEL, 2 * D_MODEL), D_MODEL),
        "w_mo": dense((L, D_MODEL, D_MODEL), D_MODEL),
        "ffn2_norm": gain((L, D_MODEL)),
        "ffn2_w_gate": dense((L, D_MODEL, D_FF), D_MODEL),
        "ffn2_w_up": dense((L, D_MODEL, D_FF), D_MODEL),
        "ffn2_w_down": dense((L, D_FF, D_MODEL), D_FF),
        "final_norm": gain((D_MODEL,)),
    }


def _fwd_reference(x, mem, positions, ffn1_norm, ffn1_w_gate, ffn1_w_up, ffn1_w_down, mix_norm, w_in,
              q_norm, w_q_up, kv_norm, w_kv_up, pool_w, pool_scale, w_out, xattn_norm, mem_norm,
              w_mq, w_mkv, w_mo, ffn2_norm, ffn2_w_gate, ffn2_w_up, ffn2_w_down, final_norm):
    h = x
    split_pts = [Q_LORA_RANK, Q_LORA_RANK + KV_LORA_RANK, Q_LORA_RANK + KV_LORA_RANK + QK_ROPE_DIM]
    for l in range(DEPTH):
        h = h + 0.5 * swiglu(rmsnorm(h, ffn1_norm[l]), ffn1_w_gate[l], ffn1_w_up[l], ffn1_w_down[l])
        z = rmsnorm(h, mix_norm[l]) @ w_in[l]
        c_q, c_kv, k_rope, z_pool = jnp.split(z, split_pts, axis=-1)
        a = mla_group(c_q, c_kv, k_rope, positions, q_norm[l], w_q_up[l], kv_norm[l], w_kv_up[l])
        p = pool_group(z_pool, pool_w[l], pool_scale[l])
        h = h + jnp.concatenate([a, p], axis=-1) @ w_out[l]
        h = h + memory_cross_attention(rmsnorm(h, xattn_norm[l]), rmsnorm(mem, mem_norm[l]),
                                       w_mq[l], w_mkv[l], w_mo[l])
        h = h + 0.5 * swiglu(rmsnorm(h, ffn2_norm[l]), ffn2_w_gate[l], ffn2_w_up[l], ffn2_w_down[l])
    return rmsnorm(h, final_norm)


import jax as _jax
import jax.numpy as _jnp

TWIN_FORMAT = 'train_step'
FWD_PARAMS = ['x', 'mem', 'positions', 'ffn1_norm', 'ffn1_w_gate', 'ffn1_w_up', 'ffn1_w_down', 'mix_norm', 'w_in', 'q_norm', 'w_q_up', 'kv_norm', 'w_kv_up', 'pool_w', 'pool_scale', 'w_out', 'xattn_norm', 'mem_norm', 'w_mq', 'w_mkv', 'w_mo', 'ffn2_norm', 'ffn2_w_gate', 'ffn2_w_up', 'ffn2_w_down', 'final_norm']
TWIN_WEIGHTS = ['ffn1_norm', 'ffn1_w_gate', 'ffn1_w_up', 'ffn1_w_down', 'mix_norm', 'w_in', 'q_norm', 'w_q_up', 'kv_norm', 'w_kv_up', 'pool_w', 'pool_scale', 'w_out', 'xattn_norm', 'mem_norm', 'w_mq', 'w_mkv', 'w_mo', 'ffn2_norm', 'ffn2_w_gate', 'ffn2_w_up', 'ffn2_w_down', 'final_norm']
TWIN_DIFF_INPUT = 'x'
TWIN_INPUTS = ['x', 'mem', 'positions', 'ffn1_norm', 'ffn1_w_gate', 'ffn1_w_up', 'ffn1_w_down', 'mix_norm', 'w_in', 'q_norm', 'w_q_up', 'kv_norm', 'w_kv_up', 'pool_w', 'pool_scale', 'w_out', 'xattn_norm', 'mem_norm', 'w_mq', 'w_mkv', 'w_mo', 'ffn2_norm', 'ffn2_w_gate', 'ffn2_w_up', 'ffn2_w_down', 'final_norm', 'loss_target', 'm_ffn1_norm', 'm_ffn1_w_gate', 'm_ffn1_w_up', 'm_ffn1_w_down', 'm_mix_norm', 'm_w_in', 'm_q_norm', 'm_w_q_up', 'm_kv_norm', 'm_w_kv_up', 'm_pool_w', 'm_pool_scale', 'm_w_out', 'm_xattn_norm', 'm_mem_norm', 'm_w_mq', 'm_w_mkv', 'm_w_mo', 'm_ffn2_norm', 'm_ffn2_w_gate', 'm_ffn2_w_up', 'm_ffn2_w_down', 'm_final_norm', 'v_ffn1_norm', 'v_ffn1_w_gate', 'v_ffn1_w_up', 'v_ffn1_w_down', 'v_mix_norm', 'v_w_in', 'v_q_norm', 'v_w_q_up', 'v_kv_norm', 'v_w_kv_up', 'v_pool_w', 'v_pool_scale', 'v_w_out', 'v_xattn_norm', 'v_mem_norm', 'v_w_mq', 'v_w_mkv', 'v_w_mo', 'v_ffn2_norm', 'v_ffn2_w_gate', 'v_ffn2_w_up', 'v_ffn2_w_down', 'v_final_norm']
TWIN_OUTPUTS = ['loss', 'grad_x', 'grad_ffn1_norm', 'grad_ffn1_w_gate', 'grad_ffn1_w_up', 'grad_ffn1_w_down', 'grad_mix_norm', 'grad_w_in', 'grad_q_norm', 'grad_w_q_up', 'grad_kv_norm', 'grad_w_kv_up', 'grad_pool_w', 'grad_pool_scale', 'grad_w_out', 'grad_xattn_norm', 'grad_mem_norm', 'grad_w_mq', 'grad_w_mkv', 'grad_w_mo', 'grad_ffn2_norm', 'grad_ffn2_w_gate', 'grad_ffn2_w_up', 'grad_ffn2_w_down', 'grad_final_norm', 'delta_ffn1_norm', 'delta_ffn1_w_gate', 'delta_ffn1_w_up', 'delta_ffn1_w_down', 'delta_mix_norm', 'delta_w_in', 'delta_q_norm', 'delta_w_q_up', 'delta_kv_norm', 'delta_w_kv_up', 'delta_pool_w', 'delta_pool_scale', 'delta_w_out', 'delta_xattn_norm', 'delta_mem_norm', 'delta_w_mq', 'delta_w_mkv', 'delta_w_mo', 'delta_ffn2_norm', 'delta_ffn2_w_gate', 'delta_ffn2_w_up', 'delta_ffn2_w_down', 'delta_final_norm', 'new_m_ffn1_norm', 'new_m_ffn1_w_gate', 'new_m_ffn1_w_up', 'new_m_ffn1_w_down', 'new_m_mix_norm', 'new_m_w_in', 'new_m_q_norm', 'new_m_w_q_up', 'new_m_kv_norm', 'new_m_w_kv_up', 'new_m_pool_w', 'new_m_pool_scale', 'new_m_w_out', 'new_m_xattn_norm', 'new_m_mem_norm', 'new_m_w_mq', 'new_m_w_mkv', 'new_m_w_mo', 'new_m_ffn2_norm', 'new_m_ffn2_w_gate', 'new_m_ffn2_w_up', 'new_m_ffn2_w_down', 'new_m_final_norm', 'new_v_ffn1_norm', 'new_v_ffn1_w_gate', 'new_v_ffn1_w_up', 'new_v_ffn1_w_down', 'new_v_mix_norm', 'new_v_w_in', 'new_v_q_norm', 'new_v_w_q_up', 'new_v_kv_norm', 'new_v_w_kv_up', 'new_v_pool_w', 'new_v_pool_scale', 'new_v_w_out', 'new_v_xattn_norm', 'new_v_mem_norm', 'new_v_w_mq', 'new_v_w_mkv', 'new_v_w_mo', 'new_v_ffn2_norm', 'new_v_ffn2_w_gate', 'new_v_ffn2_w_up', 'new_v_ffn2_w_down', 'new_v_final_norm']
TWIN_LEAF_KINDS = {'loss': 'loss', 'grad_x': 'grad_x', 'grad_ffn1_norm': 'grad_w', 'grad_ffn1_w_gate': 'grad_w', 'grad_ffn1_w_up': 'grad_w', 'grad_ffn1_w_down': 'grad_w', 'grad_mix_norm': 'grad_w', 'grad_w_in': 'grad_w', 'grad_q_norm': 'grad_w', 'grad_w_q_up': 'grad_w', 'grad_kv_norm': 'grad_w', 'grad_w_kv_up': 'grad_w', 'grad_pool_w': 'grad_w', 'grad_pool_scale': 'grad_w', 'grad_w_out': 'grad_w', 'grad_xattn_norm': 'grad_w', 'grad_mem_norm': 'grad_w', 'grad_w_mq': 'grad_w', 'grad_w_mkv': 'grad_w', 'grad_w_mo': 'grad_w', 'grad_ffn2_norm': 'grad_w', 'grad_ffn2_w_gate': 'grad_w', 'grad_ffn2_w_up': 'grad_w', 'grad_ffn2_w_down': 'grad_w', 'grad_final_norm': 'grad_w', 'delta_ffn1_norm': 'delta_w', 'delta_ffn1_w_gate': 'delta_w', 'delta_ffn1_w_up': 'delta_w', 'delta_ffn1_w_down': 'delta_w', 'delta_mix_norm': 'delta_w', 'delta_w_in': 'delta_w', 'delta_q_norm': 'delta_w', 'delta_w_q_up': 'delta_w', 'delta_kv_norm': 'delta_w', 'delta_w_kv_up': 'delta_w', 'delta_pool_w': 'delta_w', 'delta_pool_scale': 'delta_w', 'delta_w_out': 'delta_w', 'delta_xattn_norm': 'delta_w', 'delta_mem_norm': 'delta_w', 'delta_w_mq': 'delta_w', 'delta_w_mkv': 'delta_w', 'delta_w_mo': 'delta_w', 'delta_ffn2_norm': 'delta_w', 'delta_ffn2_w_gate': 'delta_w', 'delta_ffn2_w_up': 'delta_w', 'delta_ffn2_w_down': 'delta_w', 'delta_final_norm': 'delta_w', 'new_m_ffn1_norm': 'new_m', 'new_m_ffn1_w_gate': 'new_m', 'new_m_ffn1_w_up': 'new_m', 'new_m_ffn1_w_down': 'new_m', 'new_m_mix_norm': 'new_m', 'new_m_w_in': 'new_m', 'new_m_q_norm': 'new_m', 'new_m_w_q_up': 'new_m', 'new_m_kv_norm': 'new_m', 'new_m_w_kv_up': 'new_m', 'new_m_pool_w': 'new_m', 'new_m_pool_scale': 'new_m', 'new_m_w_out': 'new_m', 'new_m_xattn_norm': 'new_m', 'new_m_mem_norm': 'new_m', 'new_m_w_mq': 'new_m', 'new_m_w_mkv': 'new_m', 'new_m_w_mo': 'new_m', 'new_m_ffn2_norm': 'new_m', 'new_m_ffn2_w_gate': 'new_m', 'new_m_ffn2_w_up': 'new_m', 'new_m_ffn2_w_down': 'new_m', 'new_m_final_norm': 'new_m', 'new_v_ffn1_norm': 'new_v', 'new_v_ffn1_w_gate': 'new_v', 'new_v_ffn1_w_up': 'new_v', 'new_v_ffn1_w_down': 'new_v', 'new_v_mix_norm': 'new_v', 'new_v_w_in': 'new_v', 'new_v_q_norm': 'new_v', 'new_v_w_q_up': 'new_v', 'new_v_kv_norm': 'new_v', 'new_v_w_kv_up': 'new_v', 'new_v_pool_w': 'new_v', 'new_v_pool_scale': 'new_v', 'new_v_w_out': 'new_v', 'new_v_xattn_norm': 'new_v', 'new_v_mem_norm': 'new_v', 'new_v_w_mq': 'new_v', 'new_v_w_mkv': 'new_v', 'new_v_w_mo': 'new_v', 'new_v_ffn2_norm': 'new_v', 'new_v_ffn2_w_gate': 'new_v', 'new_v_ffn2_w_up': 'new_v', 'new_v_ffn2_w_down': 'new_v', 'new_v_final_norm': 'new_v'}


def _forward(args):
    return _fwd_reference(*[args[k] for k in FWD_PARAMS])


def _output_shape():
    def fwd():
        inp = _fwd_setup_inputs(0)
        return _fwd_reference(*[inp[k] for k in FWD_PARAMS])
    out = _jax.eval_shape(fwd)
    return out.shape, out.dtype

N_MICROBATCH = 1
ADAM_LR = 0.001
ADAM_B1 = 0.9
ADAM_B2 = 0.999
ADAM_EPS = 1e-08
ADAM_WD = 0.01
ADAM_STEP = 10
PER_EXAMPLE_BATCH_AXIS = {'x': 0, 'mem': 0, 'positions': 0, 'loss_target': 0}
SHARED_INPUTS = []
_WEIGHT_DTYPES = {'ffn1_norm': _jnp.float32, 'ffn1_w_gate': _jnp.float32, 'ffn1_w_up': _jnp.float32, 'ffn1_w_down': _jnp.float32, 'mix_norm': _jnp.float32, 'w_in': _jnp.float32, 'q_norm': _jnp.float32, 'w_q_up': _jnp.float32, 'kv_norm': _jnp.float32, 'w_kv_up': _jnp.float32, 'pool_w': _jnp.float32, 'pool_scale': _jnp.float32, 'w_out': _jnp.float32, 'xattn_norm': _jnp.float32, 'mem_norm': _jnp.float32, 'w_mq': _jnp.float32, 'w_mkv': _jnp.float32, 'w_mo': _jnp.float32, 'ffn2_norm': _jnp.float32, 'ffn2_w_gate': _jnp.float32, 'ffn2_w_up': _jnp.float32, 'ffn2_w_down': _jnp.float32, 'final_norm': _jnp.float32}
MOMENT_SCALE = {'ffn1_norm': 7.855725e-02, 'ffn1_w_gate': 3.290371e-02, 'ffn1_w_up': 3.191869e-02, 'ffn1_w_down': 5.293351e-02, 'mix_norm': 6.320889e-02, 'w_in': 6.705520e-02, 'q_norm': 4.449869e-02, 'w_q_up': 2.515412e-02, 'kv_norm': 1.018134e-01, 'w_kv_up': 3.180212e-02, 'pool_w': 7.541789e-02, 'pool_scale': 1.759898e-01, 'w_out': 5.814012e-02, 'xattn_norm': 1.926912e-02, 'mem_norm': 2.858513e-02, 'w_mq': 1.770204e-02, 'w_mkv': 1.784017e-02, 'w_mo': 1.789090e-02, 'ffn2_norm': 6.858749e-02, 'ffn2_w_gate': 2.905666e-02, 'ffn2_w_up': 2.817373e-02, 'ffn2_w_down': 4.687664e-02, 'final_norm': 3.199521e+01}


def _to_microbatches(a, axis):
    t = _jnp.moveaxis(a, axis, 0)
    t = t.reshape((N_MICROBATCH, t.shape[0] // N_MICROBATCH) + t.shape[1:])
    return _jnp.moveaxis(t, 1, axis + 1)


def setup_inputs(seed: int = 0) -> dict:
    inp = _fwd_setup_inputs(seed)
    key = _jax.random.fold_in(_jax.random.key(seed), 7919)
    shape, _ = _output_shape()
    out = dict(inp)
    out["loss_target"] = _jax.random.normal(_jax.random.fold_in(key, 0), shape, _jnp.float32)
    for i, name in enumerate(TWIN_WEIGHTS):
        w = inp[name].astype(_jnp.float32)
        if MOMENT_SCALE is None:
            s = _jnp.sqrt(_jnp.mean(_jnp.square(w)) + 1e-30)
        else:
            s = MOMENT_SCALE[name]
        km, kv = _jax.random.split(_jax.random.fold_in(key, i + 1))
        out[name] = w
        out["m_" + name] = s * _jax.random.normal(km, w.shape, _jnp.float32)
        out["v_" + name] = (s * s) * _jax.random.uniform(kv, w.shape, _jnp.float32, 0.5, 1.5)
    if N_MICROBATCH > 1:
        for name, axis in PER_EXAMPLE_BATCH_AXIS.items():
            out[name] = _to_microbatches(out[name], axis)
    return {'x': out['x'], 'mem': out['mem'], 'positions': out['positions'], 'ffn1_norm': out['ffn1_norm'], 'ffn1_w_gate': out['ffn1_w_gate'], 'ffn1_w_up': out['ffn1_w_up'], 'ffn1_w_down': out['ffn1_w_down'], 'mix_norm': out['mix_norm'], 'w_in': out['w_in'], 'q_norm': out['q_norm'], 'w_q_up': out['w_q_up'], 'kv_norm': out['kv_norm'], 'w_kv_up': out['w_kv_up'], 'pool_w': out['pool_w'], 'pool_scale': out['pool_scale'], 'w_out': out['w_out'], 'xattn_norm': out['xattn_norm'], 'mem_norm': out['mem_norm'], 'w_mq': out['w_mq'], 'w_mkv': out['w_mkv'], 'w_mo': out['w_mo'], 'ffn2_norm': out['ffn2_norm'], 'ffn2_w_gate': out['ffn2_w_gate'], 'ffn2_w_up': out['ffn2_w_up'], 'ffn2_w_down': out['ffn2_w_down'], 'final_norm': out['final_norm'], 'loss_target': out['loss_target'], 'm_ffn1_norm': out['m_ffn1_norm'], 'm_ffn1_w_gate': out['m_ffn1_w_gate'], 'm_ffn1_w_up': out['m_ffn1_w_up'], 'm_ffn1_w_down': out['m_ffn1_w_down'], 'm_mix_norm': out['m_mix_norm'], 'm_w_in': out['m_w_in'], 'm_q_norm': out['m_q_norm'], 'm_w_q_up': out['m_w_q_up'], 'm_kv_norm': out['m_kv_norm'], 'm_w_kv_up': out['m_w_kv_up'], 'm_pool_w': out['m_pool_w'], 'm_pool_scale': out['m_pool_scale'], 'm_w_out': out['m_w_out'], 'm_xattn_norm': out['m_xattn_norm'], 'm_mem_norm': out['m_mem_norm'], 'm_w_mq': out['m_w_mq'], 'm_w_mkv': out['m_w_mkv'], 'm_w_mo': out['m_w_mo'], 'm_ffn2_norm': out['m_ffn2_norm'], 'm_ffn2_w_gate': out['m_ffn2_w_gate'], 'm_ffn2_w_up': out['m_ffn2_w_up'], 'm_ffn2_w_down': out['m_ffn2_w_down'], 'm_final_norm': out['m_final_norm'], 'v_ffn1_norm': out['v_ffn1_norm'], 'v_ffn1_w_gate': out['v_ffn1_w_gate'], 'v_ffn1_w_up': out['v_ffn1_w_up'], 'v_ffn1_w_down': out['v_ffn1_w_down'], 'v_mix_norm': out['v_mix_norm'], 'v_w_in': out['v_w_in'], 'v_q_norm': out['v_q_norm'], 'v_w_q_up': out['v_w_q_up'], 'v_kv_norm': out['v_kv_norm'], 'v_w_kv_up': out['v_w_kv_up'], 'v_pool_w': out['v_pool_w'], 'v_pool_scale': out['v_pool_scale'], 'v_w_out': out['v_w_out'], 'v_xattn_norm': out['v_xattn_norm'], 'v_mem_norm': out['v_mem_norm'], 'v_w_mq': out['v_w_mq'], 'v_w_mkv': out['v_w_mkv'], 'v_w_mo': out['v_w_mo'], 'v_ffn2_norm': out['v_ffn2_norm'], 'v_ffn2_w_gate': out['v_ffn2_w_gate'], 'v_ffn2_w_up': out['v_ffn2_w_up'], 'v_ffn2_w_down': out['v_ffn2_w_down'], 'v_final_norm': out['v_final_norm']}


def _loss(weights, diff, rest, loss_target):
    with _jax.named_scope("forward"):
        args = {**rest, TWIN_DIFF_INPUT: diff, **{k: w.astype(_WEIGHT_DTYPES[k]) for k, w in weights.items()}}
        y = _forward(args)
    with _jax.named_scope("loss_head"):
        err = _jnp.square(y.astype(_jnp.float32) - loss_target)
        return 0.5 * _jnp.sum(_jnp.mean(err, axis=-1)) if err.ndim else 0.5 * err


def _adamw(w, g, m, v):
    m = ADAM_B1 * m + (1.0 - ADAM_B1) * g
    v = ADAM_B2 * v + (1.0 - ADAM_B2) * _jnp.square(g)
    m_hat = m / (1.0 - ADAM_B1 ** ADAM_STEP)
    v_hat = v / (1.0 - ADAM_B2 ** ADAM_STEP)
    delta = -ADAM_LR * (m_hat / (_jnp.sqrt(v_hat) + ADAM_EPS) + ADAM_WD * w)
    return delta, m, v


def reference(x, mem, positions, ffn1_norm, ffn1_w_gate, ffn1_w_up, ffn1_w_down, mix_norm, w_in, q_norm, w_q_up, kv_norm, w_kv_up, pool_w, pool_scale, w_out, xattn_norm, mem_norm, w_mq, w_mkv, w_mo, ffn2_norm, ffn2_w_gate, ffn2_w_up, ffn2_w_down, final_norm, loss_target, m_ffn1_norm, m_ffn1_w_gate, m_ffn1_w_up, m_ffn1_w_down, m_mix_norm, m_w_in, m_q_norm, m_w_q_up, m_kv_norm, m_w_kv_up, m_pool_w, m_pool_scale, m_w_out, m_xattn_norm, m_mem_norm, m_w_mq, m_w_mkv, m_w_mo, m_ffn2_norm, m_ffn2_w_gate, m_ffn2_w_up, m_ffn2_w_down, m_final_norm, v_ffn1_norm, v_ffn1_w_gate, v_ffn1_w_up, v_ffn1_w_down, v_mix_norm, v_w_in, v_q_norm, v_w_q_up, v_kv_norm, v_w_kv_up, v_pool_w, v_pool_scale, v_w_out, v_xattn_norm, v_mem_norm, v_w_mq, v_w_mkv, v_w_mo, v_ffn2_norm, v_ffn2_w_gate, v_ffn2_w_up, v_ffn2_w_down, v_final_norm):
    given = dict(x=x, mem=mem, positions=positions, ffn1_norm=ffn1_norm, ffn1_w_gate=ffn1_w_gate, ffn1_w_up=ffn1_w_up, ffn1_w_down=ffn1_w_down, mix_norm=mix_norm, w_in=w_in, q_norm=q_norm, w_q_up=w_q_up, kv_norm=kv_norm, w_kv_up=w_kv_up, pool_w=pool_w, pool_scale=pool_scale, w_out=w_out, xattn_norm=xattn_norm, mem_norm=mem_norm, w_mq=w_mq, w_mkv=w_mkv, w_mo=w_mo, ffn2_norm=ffn2_norm, ffn2_w_gate=ffn2_w_gate, ffn2_w_up=ffn2_w_up, ffn2_w_down=ffn2_w_down, final_norm=final_norm, loss_target=loss_target, m_ffn1_norm=m_ffn1_norm, m_ffn1_w_gate=m_ffn1_w_gate, m_ffn1_w_up=m_ffn1_w_up, m_ffn1_w_down=m_ffn1_w_down, m_mix_norm=m_mix_norm, m_w_in=m_w_in, m_q_norm=m_q_norm, m_w_q_up=m_w_q_up, m_kv_norm=m_kv_norm, m_w_kv_up=m_w_kv_up, m_pool_w=m_pool_w, m_pool_scale=m_pool_scale, m_w_out=m_w_out, m_xattn_norm=m_xattn_norm, m_mem_norm=m_mem_norm, m_w_mq=m_w_mq, m_w_mkv=m_w_mkv, m_w_mo=m_w_mo, m_ffn2_norm=m_ffn2_norm, m_ffn2_w_gate=m_ffn2_w_gate, m_ffn2_w_up=m_ffn2_w_up, m_ffn2_w_down=m_ffn2_w_down, m_final_norm=m_final_norm, v_ffn1_norm=v_ffn1_norm, v_ffn1_w_gate=v_ffn1_w_gate, v_ffn1_w_up=v_ffn1_w_up, v_ffn1_w_down=v_ffn1_w_down, v_mix_norm=v_mix_norm, v_w_in=v_w_in, v_q_norm=v_q_norm, v_w_q_up=v_w_q_up, v_kv_norm=v_kv_norm, v_w_kv_up=v_w_kv_up, v_pool_w=v_pool_w, v_pool_scale=v_pool_scale, v_w_out=v_w_out, v_xattn_norm=v_xattn_norm, v_mem_norm=v_mem_norm, v_w_mq=v_w_mq, v_w_mkv=v_w_mkv, v_w_mo=v_w_mo, v_ffn2_norm=v_ffn2_norm, v_ffn2_w_gate=v_ffn2_w_gate, v_ffn2_w_up=v_ffn2_w_up, v_ffn2_w_down=v_ffn2_w_down, v_final_norm=v_final_norm)
    weights = {n: given[n] for n in TWIN_WEIGHTS}
    shared = {n: given[n] for n in SHARED_INPUTS}
    per_example = {n: given[n] for n in ['x', 'mem', 'positions']}
    grad_fn = _jax.value_and_grad(_loss, argnums=(0, 1))

    def one_microbatch(ex, loss_target):
        ex = dict(ex)
        diff = ex.pop(TWIN_DIFF_INPUT)
        return grad_fn(weights, diff, {**shared, **ex}, loss_target)

    if N_MICROBATCH == 1:
        loss, (grad_w, grad_x) = one_microbatch(per_example, given["loss_target"])
    else:
        def body(carry, xs):
            loss_sum, grad_sum = carry
            l_k, (gw_k, gx_k) = one_microbatch(xs[0], xs[1])
            with _jax.named_scope("update"):
                return (loss_sum + l_k, _jax.tree.map(_jnp.add, grad_sum, gw_k)), gx_k

        init = (_jnp.zeros((), _jnp.float32), _jax.tree.map(_jnp.zeros_like, weights))
        (loss, grad_w), grad_x = _jax.lax.scan(body, init, (per_example, given["loss_target"]))
    with _jax.named_scope("update"):
        delta_w, new_m, new_v = {}, {}, {}
        for n in TWIN_WEIGHTS:
            delta_w[n], new_m[n], new_v[n] = _adamw(weights[n], grad_w[n], given["m_" + n], given["v_" + n])
    return (loss, grad_x, *[grad_w[n] for n in TWIN_WEIGHTS], *[delta_w[n] for n in TWIN_WEIGHTS],
            *[new_m[n] for n in TWIN_WEIGHTS], *[new_v[n] for n in TWIN_WEIGHTS])
```

```python
import numpy as np

import jax
import jax.numpy as jnp
from jax import lax
from jax.experimental import pallas as pl
from jax.experimental.pallas import tpu as pltpu

F32 = jnp.float32
BF16 = jnp.bfloat16

N_DEV = 8
D_MODEL = 1024
D_FF = 2816
MLA_HEADS = 4
NOPE = 128
ROPE = 64
HEAD_PAD = 256
V_DIM = 128
Q_RANK = 256
KV_RANK = 128
POOL_WINDOWS = (2, 4, 8, 16)
POOL_CH = 128
POOL_HALO = 16
N_MEM = 256
MEM_HEADS = 4
MEM_HD = 256
ROPE_BASE = 10000.0
RMS_EPS = 1e-6
ATTN_SCALE = (NOPE + ROPE) ** -0.5
MEM_SCALE = MEM_HD ** -0.5
NEG_BIG = -1e30

ADAM_LR = 0.001
ADAM_B1 = 0.9
ADAM_B2 = 0.999
ADAM_EPS = 1e-08
ADAM_WD = 0.01
ADAM_STEP = 10
ADAM_C1 = 1.0 - ADAM_B1 ** ADAM_STEP
ADAM_C2 = 1.0 - ADAM_B2 ** ADAM_STEP

VMEM_LIMIT_BYTES = 52 * 1024 * 1024
BF16_ROWS = 16

SEG_ROWS = (("ffn1_g", 352), ("ffn1_u", 352), ("ffn1_d", 352), ("ffn2_g", 352), ("ffn2_u", 352), ("ffn2_d", 352),
            ("w_in", 128), ("w_out", 128), ("w_mq", 128), ("w_mo", 128), ("w_mkv", 256), ("w_q", 32), ("w_kv", 16))
SEG_OFF = {}
_o = 0
for _n, _r in SEG_ROWS:
    SEG_OFF[_n] = (_o, _r)
    _o += _r
PACK_ROWS = _o

SMALL_ROWS = (("ffn1_norm", 8), ("mix_norm", 8), ("q_norm", 8), ("kv_norm", 8), ("pool_w", 512), ("pool_scale", 8),
              ("xattn_norm", 8), ("mem_norm", 8), ("ffn2_norm", 8), ("final_norm", 8), ("loss", 8))
SMALL_OFF = {}
_o = 0
for _n, _r in SMALL_ROWS:
    SMALL_OFF[_n] = (_o, _r)
    _o += _r


def _cparams(**kw):
    return pltpu.CompilerParams(vmem_limit_bytes=VMEM_LIMIT_BYTES, **kw)


def _row_tile(rows, limit):
    best = None
    for cand in range(BF16_ROWS, min(rows, limit) + 1, BF16_ROWS):
        if rows % cand == 0:
            best = cand
    assert best is not None, rows
    return best


def _dot_nn(a, b):
    return lax.dot_general(a, b, (((1,), (0,)), ((), ())), preferred_element_type=F32)


def _dot_nt(a, b):
    return lax.dot_general(a, b, (((1,), (1,)), ((), ())), preferred_element_type=F32)


def _dot_tn(a, b):
    return lax.dot_general(a, b, (((0,), (0,)), ((), ())), preferred_element_type=F32)


def _rms_fwd(x, g):
    r = lax.rsqrt(jnp.mean(x * x, axis=-1, keepdims=True) + RMS_EPS)
    return x * r * g, r


def _rms_bwd(dy, x, g, r):
    xhat = x * r
    dyg = dy * g
    dx = r * (dyg - xhat * jnp.mean(dyg * xhat, axis=-1, keepdims=True))
    dg = jnp.sum(dy * xhat, axis=0, keepdims=True)
    return dx, dg


def _accumulate(ref, val, first):
    @pl.when(first)
    def _():
        ref[...] = val

    @pl.when(jnp.logical_not(first))
    def _():
        ref[...] += val


def _rope_tables(pos_col, tab):
    ang = pos_col.astype(F32) * tab[0:1, :]
    return jnp.cos(ang), jnp.sin(ang) * tab[1:2, :]


def _swap_halves(x):
    lane = lax.broadcasted_iota(jnp.int32, x.shape, 1)
    return jnp.where((lane % 64) < 32, pltpu.roll(x, 96, 1), pltpu.roll(x, 32, 1))


def _rope_apply(x, cos_t, sin_t):
    return x * cos_t + _swap_halves(x) * sin_t


def _rope_apply_t(dy, cos_t, sin_t):
    return dy * cos_t + _swap_halves(dy * sin_t)


def _ffn_fwd(h, g, wg_t, wu_t, wd, name):
    t, d = h.shape
    f = wg_t.shape[0]
    tm, tf = min(1024, t), 256
    nf = f // tf

    def body(h_ref, g_ref, wg_ref, wu_ref, wd_ref, ho_ref, n_ref, gate_ref, up_ref, nb_sc, acc_sc):
        j = pl.program_id(1)

        @pl.when(j == 0)
        def _():
            y, _ = _rms_fwd(h_ref[...], g_ref[...])
            nb = y.astype(BF16)
            nb_sc[...] = nb
            n_ref[...] = nb
            acc_sc[...] = jnp.zeros_like(acc_sc)

        nb = nb_sc[...]
        gt = _dot_nt(nb, wg_ref[...])
        ut = _dot_nt(nb, wu_ref[...])
        gate_ref[...] = gt.astype(BF16)
        up_ref[...] = ut.astype(BF16)
        act = (gt * jax.nn.sigmoid(gt)) * ut
        acc_sc[...] += _dot_nn(act.astype(BF16), wd_ref[...])

        @pl.when(j == nf - 1)
        def _():
            ho_ref[...] = h_ref[...] + 0.5 * acc_sc[...]

    return pl.pallas_call(
        body, name=name, grid=(t // tm, nf),
        in_specs=[pl.BlockSpec((tm, d), lambda i, j: (i, 0)),
                  pl.BlockSpec((1, d), lambda i, j: (0, 0)),
                  pl.BlockSpec((tf, d), lambda i, j: (j, 0)),
                  pl.BlockSpec((tf, d), lambda i, j: (j, 0)),
                  pl.BlockSpec((tf, d), lambda i, j: (j, 0))],
        out_specs=[pl.BlockSpec((tm, d), lambda i, j: (i, 0)),
                   pl.BlockSpec((tm, d), lambda i, j: (i, 0)),
                   pl.BlockSpec((tm, tf), lambda i, j: (i, j)),
                   pl.BlockSpec((tm, tf), lambda i, j: (i, j))],
        out_shape=[jax.ShapeDtypeStruct((t, d), F32), jax.ShapeDtypeStruct((t, d), BF16),
                   jax.ShapeDtypeStruct((t, f), BF16), jax.ShapeDtypeStruct((t, f), BF16)],
        scratch_shapes=[pltpu.VMEM((tm, d), BF16), pltpu.VMEM((tm, d), F32)],
        compiler_params=_cparams(),
    )(h, g, wg_t, wu_t, wd)


def _ffn_bwd_data(dho, h, g, gate, up, wg_t, wu_t, wd, name):
    t, d = h.shape
    f = wg_t.shape[0]
    tm, tf = min(1024, t), 256
    nf = f // tf

    def body(dho_ref, h_ref, g_ref, gate_ref, up_ref, wg_ref, wu_ref, wd_ref,
             dh_ref, dgate_ref, dup_ref, act_ref, dg_ref, dhb_sc, acc_sc):
        i, j = pl.program_id(0), pl.program_id(1)

        @pl.when(j == 0)
        def _():
            dhb_sc[...] = (0.5 * dho_ref[...]).astype(BF16)
            acc_sc[...] = jnp.zeros_like(acc_sc)

        dact = _dot_nt(dhb_sc[...], wd_ref[...])
        gt = gate_ref[...].astype(F32)
        ut = up_ref[...].astype(F32)
        sg = jax.nn.sigmoid(gt)
        silu = gt * sg
        dgb = (dact * ut * (sg * (1.0 + gt * (1.0 - sg)))).astype(BF16)
        dub = (dact * silu).astype(BF16)
        act_ref[...] = (silu * ut).astype(BF16)
        dgate_ref[...] = dgb
        dup_ref[...] = dub
        acc_sc[...] += _dot_nn(dgb, wg_ref[...]) + _dot_nn(dub, wu_ref[...])

        @pl.when(j == nf - 1)
        def _():
            x = h_ref[...]
            gg = g_ref[...]
            _, r = _rms_fwd(x, gg)
            dx, dg = _rms_bwd(acc_sc[...], x, gg, r)
            dh_ref[...] = dho_ref[...] + dx
            _accumulate(dg_ref, dg, i == 0)

    return pl.pallas_call(
        body, name=name, grid=(t // tm, nf),
        in_specs=[pl.BlockSpec((tm, d), lambda i, j: (i, 0)),
                  pl.BlockSpec((tm, d), lambda i, j: (i, 0)),
                  pl.BlockSpec((1, d), lambda i, j: (0, 0)),
                  pl.BlockSpec((tm, tf), lambda i, j: (i, j)),
                  pl.BlockSpec((tm, tf), lambda i, j: (i, j)),
                  pl.BlockSpec((tf, d), lambda i, j: (j, 0)),
                  pl.BlockSpec((tf, d), lambda i, j: (j, 0)),
                  pl.BlockSpec((tf, d), lambda i, j: (j, 0))],
        out_specs=[pl.BlockSpec((tm, d), lambda i, j: (i, 0)),
                   pl.BlockSpec((tm, tf), lambda i, j: (i, j)),
                   pl.BlockSpec((tm, tf), lambda i, j: (i, j)),
                   pl.BlockSpec((tm, tf), lambda i, j: (i, j)),
                   pl.BlockSpec((1, d), lambda i, j: (0, 0))],
        out_shape=[jax.ShapeDtypeStruct((t, d), F32), jax.ShapeDtypeStruct((t, f), BF16),
                   jax.ShapeDtypeStruct((t, f), BF16), jax.ShapeDtypeStruct((t, f), BF16),
                   jax.ShapeDtypeStruct((1, d), F32)],
        scratch_shapes=[pltpu.VMEM((tm, d), BF16), pltpu.VMEM((tm, d), F32)],
        compiler_params=_cparams(),
    )(dho, h, g, gate, up, wg_t, wu_t, wd)


def _tn_matmul(a, b, name, scale=1.0, tmm=None):
    t, m = a.shape
    n = b.shape[1]
    tmm = m if tmm is None else tmm
    tk = min(1024, t)

    def body(a_ref, b_ref, o_ref):
        k = pl.program_id(1)
        prod = _dot_tn(a_ref[...].astype(BF16), b_ref[...].astype(BF16))
        if scale != 1.0:
            prod = prod * scale
        _accumulate(o_ref, prod, k == 0)

    return pl.pallas_call(
        body, name=name, grid=(m // tmm, t // tk),
        in_specs=[pl.BlockSpec((tk, tmm), lambda i, k: (k, i)),
                  pl.BlockSpec((tk, n), lambda i, k: (k, 0))],
        out_specs=pl.BlockSpec((tmm, n), lambda i, k: (i, 0)),
        out_shape=jax.ShapeDtypeStruct((m, n), F32),
        compiler_params=_cparams(),
    )(a, b)


def _loss_head(h, target, g):
    t, d = h.shape
    tm = min(512, t)

    def body(h_ref, t_ref, g_ref, loss_ref, dh_ref, dg_ref):
        i = pl.program_id(0)
        x = h_ref[...]
        gg = g_ref[...]
        y, r = _rms_fwd(x, gg)
        err = y - t_ref[...]
        part = 0.5 * jnp.sum(jnp.mean(err * err, axis=-1, keepdims=True), axis=0, keepdims=True)
        dx, dg = _rms_bwd(err * (1.0 / d), x, gg, r)
        dh_ref[...] = dx
        _accumulate(loss_ref, jnp.broadcast_to(part, loss_ref.shape), i == 0)
        _accumulate(dg_ref, dg, i == 0)

    return pl.pallas_call(
        body, name="loss_head", grid=(t // tm,),
        in_specs=[pl.BlockSpec((tm, d), lambda i: (i, 0)),
                  pl.BlockSpec((tm, d), lambda i: (i, 0)),
                  pl.BlockSpec((1, d), lambda i: (0, 0))],
        out_specs=[pl.BlockSpec((8, 128), lambda i: (0, 0)),
                   pl.BlockSpec((tm, d), lambda i: (i, 0)),
                   pl.BlockSpec((1, d), lambda i: (0, 0))],
        out_shape=[jax.ShapeDtypeStruct((8, 128), F32), jax.ShapeDtypeStruct((t, d), F32),
                   jax.ShapeDtypeStruct((1, d), F32)],
        compiler_params=_cparams(),
    )(h, target, g)


def _mix_prep(h1, mix_norm, w_in, q_norm, wq_t, kv_norm, wkv, pos, rope_tab):
    t, d = h1.shape
    tm = min(512, t)

    def body(h_ref, gm_ref, win_ref, gq_ref, wq_ref, gkv_ref, wkv_ref, pos_ref, tab_ref,
             u_ref, z_ref, qn_ref, kvn_ref, q_ref, k_ref, v_ref):
        u, _ = _rms_fwd(h_ref[...], gm_ref[...])
        ub = u.astype(BF16)
        u_ref[...] = ub
        z = _dot_nn(ub, win_ref[...])
        z_ref[...] = z
        cos_t, sin_t = _rope_tables(pos_ref[...], tab_ref[...])
        qn, _ = _rms_fwd(z[:, 0:Q_RANK], gq_ref[...])
        qnb = qn.astype(BF16)
        qn_ref[...] = qnb
        q = _dot_nt(qnb, wq_ref[...])
        kvn, _ = _rms_fwd(z[:, Q_RANK:Q_RANK + KV_RANK], gkv_ref[...])
        kvnb = kvn.astype(BF16)
        kvn_ref[...] = kvnb
        kv = _dot_nn(kvnb, wkv_ref[...])
        k_pe = _rope_apply(z[:, Q_RANK + KV_RANK:Q_RANK + KV_RANK + 128], cos_t, sin_t)
        for hh in range(MLA_HEADS):
            b = hh * HEAD_PAD
            q_pe = _rope_apply(q[:, b + NOPE:b + HEAD_PAD], cos_t, sin_t)
            q_ref[hh] = jnp.concatenate([q[:, b:b + NOPE], q_pe], axis=-1).astype(BF16)
            k_ref[hh] = jnp.concatenate([kv[:, b:b + NOPE], k_pe], axis=-1).astype(BF16)
            v_ref[hh] = kv[:, b + NOPE:b + HEAD_PAD].astype(BF16)

    full = lambda shape: pl.BlockSpec(shape, lambda i: (0,) * len(shape))
    return pl.pallas_call(
        body, name="mix_prep", grid=(t // tm,),
        in_specs=[pl.BlockSpec((tm, d), lambda i: (i, 0)), full((1, d)), full(w_in.shape), full((1, Q_RANK)),
                  full(wq_t.shape), full((1, KV_RANK)), full(wkv.shape),
                  pl.BlockSpec((tm, 1), lambda i: (i, 0)), full(rope_tab.shape)],
        out_specs=[pl.BlockSpec((tm, d), lambda i: (i, 0)),
                   pl.BlockSpec((tm, d), lambda i: (i, 0)),
                   pl.BlockSpec((tm, Q_RANK), lambda i: (i, 0)),
                   pl.BlockSpec((tm, KV_RANK), lambda i: (i, 0)),
                   pl.BlockSpec((MLA_HEADS, tm, HEAD_PAD), lambda i: (0, i, 0)),
                   pl.BlockSpec((MLA_HEADS, tm, HEAD_PAD), lambda i: (0, i, 0)),
                   pl.BlockSpec((MLA_HEADS, tm, V_DIM), lambda i: (0, i, 0))],
        out_shape=[jax.ShapeDtypeStruct((t, d), BF16), jax.ShapeDtypeStruct((t, d), F32),
                   jax.ShapeDtypeStruct((t, Q_RANK), BF16), jax.ShapeDtypeStruct((t, KV_RANK), BF16),
                   jax.ShapeDtypeStruct((MLA_HEADS, t, HEAD_PAD), BF16),
                   jax.ShapeDtypeStruct((MLA_HEADS, t, HEAD_PAD), BF16),
                   jax.ShapeDtypeStruct((MLA_HEADS, t, V_DIM), BF16)],
        compiler_params=_cparams(),
    )(h1, mix_norm, w_in, q_norm, wq_t, kv_norm, wkv, pos, rope_tab)


def _causal_mask(s, qi, kj, tq, tk):
    row = lax.broadcasted_iota(jnp.int32, s.shape, 0) + qi * tq
    col = lax.broadcasted_iota(jnp.int32, s.shape, 1) + kj * tk
    return jnp.where(col <= row, s, NEG_BIG)


def _attn_fwd(q, k, v):
    nh, t, _ = q.shape
    tq = tk = min(512, t)
    nq, nk = t // tq, t // tk

    def body(q_ref, k_ref, v_ref, o_ref, lse_ref, m_sc, l_sc, acc_sc):
        i, j = pl.program_id(1), pl.program_id(2)

        @pl.when(j == 0)
        def _():
            m_sc[...] = jnp.full_like(m_sc, NEG_BIG)
            l_sc[...] = jnp.zeros_like(l_sc)
            acc_sc[...] = jnp.zeros_like(acc_sc)

        @pl.when(j <= i)
        def _():
            s = _causal_mask(_dot_nt(q_ref[0], k_ref[0]) * ATTN_SCALE, i, j, tq, tk)
            m_old = m_sc[...]
            m_new = jnp.maximum(m_old, jnp.max(s, axis=-1, keepdims=True))
            alpha = jnp.exp(m_old - m_new)
            p = jnp.exp(s - m_new)
            l_sc[...] = alpha * l_sc[...] + jnp.sum(p, axis=-1, keepdims=True)
            acc_sc[...] = alpha * acc_sc[...] + _dot_nn(p.astype(BF16), v_ref[0])
            m_sc[...] = m_new

        @pl.when(j == nk - 1)
        def _():
            l = l_sc[...]
            o_ref[...] = (acc_sc[...] / l).astype(BF16)
            lse_ref[0] = m_sc[...] + jnp.log(l)

    return pl.pallas_call(
        body, name="attn_fwd", grid=(nh, nq, nk),
        in_specs=[pl.BlockSpec((1, tq, HEAD_PAD), lambda h, i, j: (h, i, 0)),
                  pl.BlockSpec((1, tk, HEAD_PAD), lambda h, i, j: (h, jnp.minimum(j, i), 0)),
                  pl.BlockSpec((1, tk, V_DIM), lambda h, i, j: (h, jnp.minimum(j, i), 0))],
        out_specs=[pl.BlockSpec((tq, V_DIM), lambda h, i, j: (i, h)),
                   pl.BlockSpec((1, tq, 1), lambda h, i, j: (h, i, 0))],
        out_shape=[jax.ShapeDtypeStruct((t, nh * V_DIM), BF16), jax.ShapeDtypeStruct((nh, t, 1), F32)],
        scratch_shapes=[pltpu.VMEM((tq, 1), F32), pltpu.VMEM((tq, 1), F32), pltpu.VMEM((tq, V_DIM), F32)],
        compiler_params=_cparams(),
    )(q, k, v)


def _attn_bwd(q, k, v, o, do, lse):
    nh, t, _ = q.shape
    tq = tk = min(512, t)
    nq, nk = t // tq, t // tk

    def body(q_ref, k_ref, v_ref, o_ref, do_ref, lse_ref, dq_ref, dk_ref, dv_ref):
        j, i = pl.program_id(1), pl.program_id(2)

        @pl.when(i >= j)
        def _():
            qq, kk, dob = q_ref[0], k_ref[0], do_ref[...]
            s = _causal_mask(_dot_nt(qq, kk) * ATTN_SCALE, i, j, tq, tk)
            p = jnp.exp(s - lse_ref[0])
            dpp = _dot_nt(dob, v_ref[0])
            delta = jnp.sum(dob.astype(F32) * o_ref[...].astype(F32), axis=-1, keepdims=True)
            dsb = (p * (dpp - delta) * ATTN_SCALE).astype(BF16)
            _accumulate(dv_ref.at[0], _dot_tn(p.astype(BF16), dob), i == j)
            _accumulate(dk_ref.at[0], _dot_tn(dsb, qq), i == j)
            _accumulate(dq_ref.at[0, pl.ds(pl.multiple_of(i * tq, tq), tq), :], _dot_nn(dsb, kk), j == 0)

    qmap = lambda h, j, i: (h, jnp.maximum(i, j), 0)
    return pl.pallas_call(
        body, name="attn_bwd", grid=(nh, nk, nq),
        in_specs=[pl.BlockSpec((1, tq, HEAD_PAD), qmap),
                  pl.BlockSpec((1, tk, HEAD_PAD), lambda h, j, i: (h, j, 0)),
                  pl.BlockSpec((1, tk, V_DIM), lambda h, j, i: (h, j, 0)),
                  pl.BlockSpec((tq, V_DIM), lambda h, j, i: (jnp.maximum(i, j), h)),
                  pl.BlockSpec((tq, V_DIM), lambda h, j, i: (jnp.maximum(i, j), h)),
                  pl.BlockSpec((1, tq, 1), qmap)],
        out_specs=[pl.BlockSpec((1, t, HEAD_PAD), lambda h, j, i: (h, 0, 0)),
                   pl.BlockSpec((1, tk, HEAD_PAD), lambda h, j, i: (h, j, 0)),
                   pl.BlockSpec((1, tk, V_DIM), lambda h, j, i: (h, j, 0))],
        out_shape=[jax.ShapeDtypeStruct((nh, t, HEAD_PAD), F32), jax.ShapeDtypeStruct((nh, t, HEAD_PAD), F32),
                   jax.ShapeDtypeStruct((nh, t, V_DIM), F32)],
        compiler_params=_cparams(),
    )(q, k, v, o, do, lse)


def _pool_counts(first_token, rows, w):
    tok = lax.broadcasted_iota(jnp.int32, (rows, POOL_CH), 0) + first_token
    return jnp.minimum(tok + 1, w).astype(F32)


def _pool_centered(zbuf, g, w, i, tm):
    lanes = pl.ds(g * POOL_CH, POOL_CH)
    cur = zbuf[pl.ds(POOL_HALO, tm), lanes]
    win = cur
    for s in range(1, w):
        win = win + zbuf[pl.ds(POOL_HALO - s, tm), lanes]
    return win / _pool_counts(i * tm, tm, w) - cur


def _pool_load(zbuf, z_ref, halo_ref, i, tm):
    @pl.when(i == 0)
    def _():
        zbuf[pl.ds(0, POOL_HALO), :] = jnp.zeros((POOL_HALO, zbuf.shape[1]), F32)

    @pl.when(i > 0)
    def _():
        zbuf[pl.ds(0, POOL_HALO), :] = halo_ref[...]

    zbuf[pl.ds(POOL_HALO, tm), :] = z_ref[...]


def _pool_fwd(z, pool_w, pool_scale):
    t = z.shape[0]
    pw = len(POOL_WINDOWS) * POOL_CH
    tm = min(512, t)
    hb = tm // POOL_HALO

    def body(z_ref, halo_ref, w_ref, sc_ref, p_ref, zbuf):
        i = pl.program_id(0)
        _pool_load(zbuf, z_ref, halo_ref, i, tm)
        for g, w in enumerate(POOL_WINDOWS):
            c = _pool_centered(zbuf, g, w, i, tm)
            y = _dot_nn(c.astype(BF16), w_ref[g]) * sc_ref[:, g * POOL_CH:(g + 1) * POOL_CH]
            p_ref[:, g * POOL_CH:(g + 1) * POOL_CH] = y.astype(BF16)

    return pl.pallas_call(
        body, name="pool_fwd", grid=(t // tm,),
        in_specs=[pl.BlockSpec((tm, pw), lambda i: (i, 1)),
                  pl.BlockSpec((POOL_HALO, pw), lambda i: (jnp.maximum(i * hb - 1, 0), 1)),
                  pl.BlockSpec(pool_w.shape, lambda i: (0, 0, 0)),
                  pl.BlockSpec((1, pw), lambda i: (0, 0))],
        out_specs=pl.BlockSpec((tm, pw), lambda i: (i, 0)),
        out_shape=jax.ShapeDtypeStruct((t, pw), BF16),
        scratch_shapes=[pltpu.VMEM((POOL_HALO + tm, pw), F32)],
        compiler_params=_cparams(),
    )(z, z, pool_w, pool_scale)


def _pool_bwd(dp, z, pool_w, pool_scale):
    t = z.shape[0]
    ng = len(POOL_WINDOWS)
    pw = ng * POOL_CH
    tm = min(512, t)
    hb = tm // POOL_HALO
    nt = t // tm

    def body(dp_ref, dpn_ref, z_ref, halo_ref, w_ref, sc_ref, dz_ref, dw_ref, dsc_ref, zbuf, dbuf):
        i = pl.program_id(0)
        _pool_load(zbuf, z_ref, halo_ref, i, tm)
        nxt_ok = (i < nt - 1).astype(F32)
        for g, w in enumerate(POOL_WINDOWS):
            lanes = pl.ds(g * POOL_CH, POOL_CH)
            cols = slice(g * POOL_CH, (g + 1) * POOL_CH)
            sc = sc_ref[:, cols]
            wg = w_ref[g]
            c = _pool_centered(zbuf, g, w, i, tm).astype(BF16)
            ypre = _dot_nn(c, wg)
            dpg = dp_ref[:, cols].astype(F32)
            _accumulate(dsc_ref.at[:, lanes], jnp.sum(dpg * ypre, axis=0, keepdims=True), i == 0)
            dyb = (dpg * sc).astype(BF16)
            _accumulate(dw_ref.at[g], _dot_tn(c, dyb), i == 0)
            dd = _dot_nt(dyb, wg)
            dyn = (dpn_ref[:, cols].astype(F32) * sc).astype(BF16)
            ddn = _dot_nt(dyn, wg) * nxt_ok
            dbuf[pl.ds(0, tm), lanes] = dd / _pool_counts(i * tm, tm, w)
            dbuf[pl.ds(tm, POOL_HALO), lanes] = ddn / _pool_counts((i + 1) * tm, POOL_HALO, w)
            acc = -dd
            for s in range(w):
                acc = acc + dbuf[pl.ds(s, tm), lanes]
            dz_ref[:, cols] = acc

    return pl.pallas_call(
        body, name="pool_bwd", grid=(nt,),
        in_specs=[pl.BlockSpec((tm, pw), lambda i: (i, 0)),
                  pl.BlockSpec((POOL_HALO, pw), lambda i: (jnp.minimum((i + 1) * hb, t // POOL_HALO - 1), 0)),
                  pl.BlockSpec((tm, pw), lambda i: (i, 1)),
                  pl.BlockSpec((POOL_HALO, pw), lambda i: (jnp.maximum(i * hb - 1, 0), 1)),
                  pl.BlockSpec(pool_w.shape, lambda i: (0, 0, 0)),
                  pl.BlockSpec((1, pw), lambda i: (0, 0))],
        out_specs=[pl.BlockSpec((tm, pw), lambda i: (i, 0)),
                   pl.BlockSpec((ng, POOL_CH, POOL_CH), lambda i: (0, 0, 0)),
                   pl.BlockSpec((1, pw), lambda i: (0, 0))],
        out_shape=[jax.ShapeDtypeStruct((t, pw), F32), jax.ShapeDtypeStruct((ng, POOL_CH, POOL_CH), F32),
                   jax.ShapeDtypeStruct((1, pw), F32)],
        scratch_shapes=[pltpu.VMEM((POOL_HALO + tm, pw), F32), pltpu.VMEM((tm + POOL_HALO, pw), F32)],
        compiler_params=_cparams(),
    )(dp, dp, z, z, pool_w, pool_scale)


def _mla_bwd(dq_h, dk_h, dv_h, z, dz_pool, h1, dh2, mix_norm, w_in, q_norm, wq_t, kv_norm, wkv, pos, rope_tab):
    t, d = h1.shape
    tm = min(256, t)

    def body(dqh_ref, dkh_ref, dvh_ref, z_ref, dzp_ref, h_ref, dh2_ref, gm_ref, win_ref, gq_ref, wq_ref, gkv_ref,
             wkv_ref, pos_ref, tab_ref, dh1_ref, dq_ref, dkv_ref, dz_ref, dgq_ref, dgkv_ref, dgm_ref):
        i = pl.program_id(0)
        first = i == 0
        cos_t, sin_t = _rope_tables(pos_ref[...], tab_ref[...])
        dq_parts, dkv_parts = [], []
        dk_pe = jnp.zeros((tm, 128), F32)
        for hh in range(MLA_HEADS):
            dqh = dqh_ref[hh]
            dq_parts += [dqh[:, 0:NOPE], _rope_apply_t(dqh[:, NOPE:HEAD_PAD], cos_t, sin_t)]
            dkh = dkh_ref[hh]
            dkv_parts += [dkh[:, 0:NOPE], dvh_ref[hh]]
            dk_pe = dk_pe + dkh[:, NOPE:HEAD_PAD]
        dqb = jnp.concatenate(dq_parts, axis=-1).astype(BF16)
        dkvb = jnp.concatenate(dkv_parts, axis=-1).astype(BF16)
        dq_ref[...] = dqb
        dkv_ref[...] = dkvb
        z = z_ref[...]
        c_q = z[:, 0:Q_RANK]
        gq = gq_ref[...]
        _, rq = _rms_fwd(c_q, gq)
        dcq, dgq = _rms_bwd(_dot_nn(dqb, wq_ref[...]), c_q, gq, rq)
        c_kv = z[:, Q_RANK:Q_RANK + KV_RANK]
        gkv = gkv_ref[...]
        _, rkv = _rms_fwd(c_kv, gkv)
        dckv, dgkv = _rms_bwd(_dot_nt(dkvb, wkv_ref[...]), c_kv, gkv, rkv)
        dkr = _rope_apply_t(dk_pe, cos_t, sin_t)
        dzb = jnp.concatenate([dcq, dckv, dkr, dzp_ref[...]], axis=-1).astype(BF16)
        dz_ref[...] = dzb
        x = h_ref[...]
        gm = gm_ref[...]
        _, rm = _rms_fwd(x, gm)
        dx, dgm = _rms_bwd(_dot_nt(dzb, win_ref[...]), x, gm, rm)
        dh1_ref[...] = dh2_ref[...] + dx
        _accumulate(dgq_ref, dgq, first)
        _accumulate(dgkv_ref, dgkv, first)
        _accumulate(dgm_ref, dgm, first)

    full = lambda shape: pl.BlockSpec(shape, lambda i: (0,) * len(shape))
    row = lambda w: pl.BlockSpec((tm, w), lambda i: (i, 0))
    head = lambda w: pl.BlockSpec((MLA_HEADS, tm, w), lambda i: (0, i, 0))
    pw = len(POOL_WINDOWS) * POOL_CH
    return pl.pallas_call(
        body, name="mla_bwd", grid=(t // tm,),
        in_specs=[head(HEAD_PAD), head(HEAD_PAD), head(V_DIM), row(d), row(pw), row(d), row(d),
                  full((1, d)), full(w_in.shape), full((1, Q_RANK)), full(wq_t.shape), full((1, KV_RANK)),
                  full(wkv.shape), row(1), full(rope_tab.shape)],
        out_specs=[row(d), row(d), row(d), row(d), full((1, Q_RANK)), full((1, KV_RANK)), full((1, d))],
        out_shape=[jax.ShapeDtypeStruct((t, d), F32), jax.ShapeDtypeStruct((t, d), BF16),
                   jax.ShapeDtypeStruct((t, d), BF16), jax.ShapeDtypeStruct((t, d), BF16),
                   jax.ShapeDtypeStruct((1, Q_RANK), F32), jax.ShapeDtypeStruct((1, KV_RANK), F32),
                   jax.ShapeDtypeStruct((1, d), F32)],
        compiler_params=_cparams(),
    )(dq_h, dk_h, dv_h, z, dz_pool, h1, dh2, mix_norm, w_in, q_norm, wq_t, kv_norm, wkv, pos, rope_tab)


def _mem_kv(mem, mem_norm, wmkv):
    n, d = mem.shape

    def body(mem_ref, g_ref, w_ref, memn_ref, k_ref, v_ref):
        y, _ = _rms_fwd(mem_ref[...], g_ref[...])
        yb = y.astype(BF16)
        memn_ref[...] = yb
        for hh in range(MEM_HEADS):
            k_ref[hh] = _dot_nn(yb, w_ref[hh]).astype(BF16)
            v_ref[hh] = _dot_nn(yb, w_ref[MEM_HEADS + hh]).astype(BF16)

    return pl.pallas_call(
        body, name="mem_kv",
        out_shape=[jax.ShapeDtypeStruct((n, d), BF16), jax.ShapeDtypeStruct((MEM_HEADS, n, MEM_HD), BF16),
                   jax.ShapeDtypeStruct((MEM_HEADS, n, MEM_HD), BF16)],
        compiler_params=_cparams(),
    )(mem, mem_norm, wmkv)


def _mem_softmax(qb, km):
    s = _dot_nt(qb, km) * MEM_SCALE
    e = jnp.exp(s - jnp.max(s, axis=-1, keepdims=True))
    return e / jnp.sum(e, axis=-1, keepdims=True)


def _xattn_fwd(h1, a, p, w_out, g, wmq, km, vm, wmo):
    t, d = h1.shape
    tm = min(512, t)
    half = a.shape[1]

    def body(h_ref, a_ref, p_ref, wo_ref, g_ref, wmq_ref, km_ref, vm_ref, wmo_ref,
             h2_ref, h3_ref, hn_ref, q_ref, o_ref):
        h2 = h_ref[...] + _dot_nn(a_ref[...], wo_ref[0:half, :]) + _dot_nn(p_ref[...], wo_ref[half:2 * half, :])
        h2_ref[...] = h2
        hn, _ = _rms_fwd(h2, g_ref[...])
        hnb = hn.astype(BF16)
        hn_ref[...] = hnb
        qb = _dot_nn(hnb, wmq_ref[...]).astype(BF16)
        q_ref[...] = qb
        outs = []
        for hh in range(MEM_HEADS):
            pr = _mem_softmax(qb[:, hh * MEM_HD:(hh + 1) * MEM_HD], km_ref[hh])
            outs.append(_dot_nn(pr.astype(BF16), vm_ref[hh]))
        ob = jnp.concatenate(outs, axis=-1).astype(BF16)
        o_ref[...] = ob
        h3_ref[...] = h2 + _dot_nn(ob, wmo_ref[...])

    full = lambda shape: pl.BlockSpec(shape, lambda i: (0,) * len(shape))
    row = lambda w: pl.BlockSpec((tm, w), lambda i: (i, 0))
    return pl.pallas_call(
        body, name="xattn_fwd", grid=(t // tm,),
        in_specs=[row(d), row(half), row(half), full(w_out.shape), full((1, d)), full(wmq.shape),
                  full(km.shape), full(vm.shape), full(wmo.shape)],
        out_specs=[row(d), row(d), row(d), row(d), row(d)],
        out_shape=[jax.ShapeDtypeStruct((t, d), F32), jax.ShapeDtypeStruct((t, d), F32),
                   jax.ShapeDtypeStruct((t, d), BF16), jax.ShapeDtypeStruct((t, d), BF16),
                   jax.ShapeDtypeStruct((t, d), BF16)],
        compiler_params=_cparams(),
    )(h1, a, p, w_out, g, wmq, km, vm, wmo)


def _xattn_bwd(dh3, h2, qm, g, wmq, km, vm, wmo, w_out):
    t, d = h2.shape
    tm = min(256, t)
    half = d // 2

    def body(dh3_ref, h2_ref, q_ref, g_ref, wmq_ref, km_ref, vm_ref, wmo_ref, wo_ref,
             dh2_ref, dq_ref, da_ref, dp_ref, dk_ref, dv_ref, dg_ref):
        i = pl.program_id(0)
        first = i == 0
        dh3 = dh3_ref[...]
        dob = _dot_nt(dh3.astype(BF16), wmo_ref[...]).astype(BF16)
        qb = q_ref[...]
        dq_parts = []
        for hh in range(MEM_HEADS):
            cols = slice(hh * MEM_HD, (hh + 1) * MEM_HD)
            kk, vv = km_ref[hh], vm_ref[hh]
            pr = _mem_softmax(qb[:, cols], kk)
            doh = dob[:, cols]
            _accumulate(dv_ref.at[hh], _dot_tn(pr.astype(BF16), doh), first)
            dpp = _dot_nt(doh, vv)
            dsb = (pr * (dpp - jnp.sum(dpp * pr, axis=-1, keepdims=True)) * MEM_SCALE).astype(BF16)
            dq_parts.append(_dot_nn(dsb, kk))
            _accumulate(dk_ref.at[hh], _dot_tn(dsb, qb[:, cols]), first)
        dqb = jnp.concatenate(dq_parts, axis=-1).astype(BF16)
        dq_ref[...] = dqb
        x = h2_ref[...]
        gg = g_ref[...]
        _, r = _rms_fwd(x, gg)
        dx, dg = _rms_bwd(_dot_nt(dqb, wmq_ref[...]), x, gg, r)
        dh2 = dh3 + dx
        dh2_ref[...] = dh2
        dap = _dot_nt(dh2.astype(BF16), wo_ref[...])
        da_ref[...] = dap[:, 0:half].astype(BF16)
        dp_ref[...] = dap[:, half:d].astype(BF16)
        _accumulate(dg_ref, dg, first)

    full = lambda shape: pl.BlockSpec(shape, lambda i: (0,) * len(shape))
    row = lambda w: pl.BlockSpec((tm, w), lambda i: (i, 0))
    return pl.pallas_call(
        body, name="xattn_bwd", grid=(t // tm,),
        in_specs=[row(d), row(d), row(d), full((1, d)), full(wmq.shape), full(km.shape), full(vm.shape),
                  full(wmo.shape), full(w_out.shape)],
        out_specs=[row(d), row(d), row(half), row(half), full(km.shape), full(vm.shape), full((1, d))],
        out_shape=[jax.ShapeDtypeStruct((t, d), F32), jax.ShapeDtypeStruct((t, d), BF16),
                   jax.ShapeDtypeStruct((t, half), BF16), jax.ShapeDtypeStruct((t, half), BF16),
                   jax.ShapeDtypeStruct(km.shape, F32), jax.ShapeDtypeStruct(vm.shape, F32),
                   jax.ShapeDtypeStruct((1, d), F32)],
        compiler_params=_cparams(),
    )(dh3, h2, qm, g, wmq, km, vm, wmo, w_out)


def _mem_kv_bwd(dkm, dvm, memn, mem, mem_norm, wmkv):
    n, d = mem.shape

    def body(dk_ref, dv_ref, memn_ref, mem_ref, g_ref, w_ref, dw_ref, dg_ref):
        memn = memn_ref[...]
        dmemn = jnp.zeros((n, d), F32)
        for s in range(2 * MEM_HEADS):
            src = dk_ref[s] if s < MEM_HEADS else dv_ref[s - MEM_HEADS]
            db = src.astype(BF16)
            dw_ref[s] = _dot_tn(memn, db)
            dmemn = dmemn + _dot_nt(db, w_ref[s])
        x = mem_ref[...]
        gg = g_ref[...]
        _, r = _rms_fwd(x, gg)
        _, dg = _rms_bwd(dmemn, x, gg, r)
        dg_ref[...] = dg

    return pl.pallas_call(
        body, name="mem_kv_bwd",
        out_shape=[jax.ShapeDtypeStruct(wmkv.shape, F32), jax.ShapeDtypeStruct((1, d), F32)],
        compiler_params=_cparams(),
    )(dkm, dvm, memn, mem, mem_norm, wmkv)


MESH_ID = pl.DeviceIdType.MESH
ANY = pl.BlockSpec(memory_space=pl.ANY)


def _coords():
    return lax.axis_index("x"), lax.axis_index("y"), lax.axis_index("c")


def _other_chips(x, y):
    return [(1 - x, y), (x, 1 - y), (1 - x, 1 - y)]


def _all_gather_rows(shard):
    r, w = shard.shape

    def body(x_ref, out_ref, send_sems, recv_sems, local_sem):
        x, y, c = _coords()
        me, sibling = (x, y, c), (x, y, 1 - c)
        chips = _other_chips(x, y)

        def block(px, py, pc):
            return out_ref.at[4 * px + 2 * py + pc]

        def copy(k, blk, to, src=None):
            return pltpu.make_async_remote_copy(
                src_ref=block(*blk) if src is None else src, dst_ref=block(*blk),
                send_sem=send_sems.at[k], recv_sem=recv_sems.at[k], device_id=to, device_id_type=MESH_ID)

        mine = pltpu.make_async_copy(x_ref, block(*me), local_sem)
        mine.start()
        first = [copy(0, me, sibling, src=x_ref)]
        first += [copy(1 + j, me, (*chip, c), src=x_ref) for j, chip in enumerate(chips)]
        for cp in first:
            cp.start()
        passed = [copy(4 + j, (*chip, c), sibling) for j, chip in enumerate(chips)]
        for j, chip in enumerate(chips):
            copy(1 + j, (*chip, c), me).wait_recv()
            passed[j].start()
        copy(0, sibling, me).wait_recv()
        for j, chip in enumerate(chips):
            copy(4 + j, (*chip, 1 - c), me).wait_recv()
        for cp in first + passed:
            cp.wait_send()
        mine.wait()

    return pl.pallas_call(
        body, name="all_gather_rows",
        out_shape=jax.ShapeDtypeStruct((N_DEV, r, w), shard.dtype),
        in_specs=[ANY], out_specs=ANY,
        scratch_shapes=[pltpu.SemaphoreType.DMA((7,)), pltpu.SemaphoreType.DMA((7,)), pltpu.SemaphoreType.DMA],
    )(shard)


def _all_gather_direct(shard):
    r, w = shard.shape

    def body(x_ref, out_ref, send_sems, recv_sems, local_sem):
        x, y, c = _coords()
        mine = pltpu.make_async_copy(x_ref, out_ref.at[4 * x + 2 * y + c], local_sem)
        mine.start()
        arrivals = []
        for k in range(1, N_DEV):
            px, py, pc = x ^ ((k >> 2) & 1), y ^ ((k >> 1) & 1), c ^ (k & 1)
            pltpu.make_async_remote_copy(
                src_ref=x_ref, dst_ref=out_ref.at[4 * x + 2 * y + c],
                send_sem=send_sems.at[k - 1], recv_sem=recv_sems.at[k - 1],
                device_id=(px, py, pc), device_id_type=MESH_ID).start()
            arrivals.append(pltpu.make_async_remote_copy(
                src_ref=x_ref, dst_ref=out_ref.at[4 * px + 2 * py + pc],
                send_sem=send_sems.at[k - 1], recv_sem=recv_sems.at[k - 1],
                device_id=(px, py, pc), device_id_type=MESH_ID))
        for cp in arrivals:
            cp.wait_recv()
        for cp in arrivals:
            cp.wait_send()
        mine.wait()

    return pl.pallas_call(
        body, name="all_gather_direct",
        out_shape=jax.ShapeDtypeStruct((N_DEV, r, w), shard.dtype),
        in_specs=[pl.BlockSpec(memory_space=pltpu.VMEM)], out_specs=pl.BlockSpec(memory_space=pltpu.VMEM),
        scratch_shapes=[pltpu.SemaphoreType.DMA((7,)), pltpu.SemaphoreType.DMA((7,)), pltpu.SemaphoreType.DMA],
        compiler_params=_cparams(),
    )(shard)


def _exchange_core(g):
    _, r, w = g.shape

    def body(g_ref, land_ref, send_sems, recv_sems):
        x, y, c = _coords()
        copies = []
        for chip in range(4):
            copies.append(pltpu.make_async_remote_copy(
                src_ref=g_ref.at[2 * chip + (1 - c)], dst_ref=land_ref.at[chip],
                send_sem=send_sems.at[chip], recv_sem=recv_sems.at[chip],
                device_id=(x, y, 1 - c), device_id_type=MESH_ID))
        for cp in copies:
            cp.start()
        for cp in copies:
            cp.wait_recv()
        for cp in copies:
            cp.wait_send()

    return pl.pallas_call(
        body, name="exchange_core",
        out_shape=jax.ShapeDtypeStruct((4, r, w), g.dtype),
        in_specs=[ANY], out_specs=ANY,
        scratch_shapes=[pltpu.SemaphoreType.DMA((4,)), pltpu.SemaphoreType.DMA((4,))],
    )(g)


def _chip_partial(g, land, cidx):
    _, r, w = g.shape
    tr = _row_tile(r, 1024)
    g4 = g.reshape(4, 2, r, w)

    def body(c_ref, g_ref, l_ref, o_ref):
        o_ref[0] = (g_ref[0, 0].astype(F32) + l_ref[0].astype(F32)).astype(o_ref.dtype)

    return pl.pallas_call(
        body, name="chip_partial",
        grid_spec=pltpu.PrefetchScalarGridSpec(
            num_scalar_prefetch=1, grid=(4, r // tr),
            in_specs=[pl.BlockSpec((1, 1, tr, w), lambda i, j, s: (i, s[0], j, 0)),
                      pl.BlockSpec((1, tr, w), lambda i, j, s: (i, j, 0))],
            out_specs=pl.BlockSpec((1, tr, w), lambda i, j, s: (i, j, 0))),
        out_shape=jax.ShapeDtypeStruct((4, r, w), g.dtype),
        compiler_params=_cparams(),
    )(cidx, g4, land)


def _exchange_chips(part):
    _, r, w = part.shape

    def body(p_ref, land_ref, send_sems, recv_sems):
        x, y, c = _coords()
        copies = []
        for j, (px, py) in enumerate(_other_chips(x, y)):
            copies.append(pltpu.make_async_remote_copy(
                src_ref=p_ref.at[2 * px + py], dst_ref=land_ref.at[j],
                send_sem=send_sems.at[j], recv_sem=recv_sems.at[j],
                device_id=(px, py, c), device_id_type=MESH_ID))
        for cp in copies:
            cp.start()
        for cp in copies:
            cp.wait_recv()
        for cp in copies:
            cp.wait_send()

    return pl.pallas_call(
        body, name="exchange_chips",
        out_shape=jax.ShapeDtypeStruct((3, r, w), part.dtype),
        in_specs=[ANY], out_specs=ANY,
        scratch_shapes=[pltpu.SemaphoreType.DMA((3,)), pltpu.SemaphoreType.DMA((3,))],
    )(part)


def _adamw(w, g, m, v):
    m = ADAM_B1 * m + (1.0 - ADAM_B1) * g
    v = ADAM_B2 * v + (1.0 - ADAM_B2) * (g * g)
    m_hat = m / ADAM_C1
    v_hat = v / ADAM_C2
    delta = -ADAM_LR * (m_hat / (jnp.sqrt(v_hat) + ADAM_EPS) + ADAM_WD * w)
    return delta, m, v


def _adam_big(part, land, w, m, v, chip_idx):
    r, wd = w.shape
    tr, tw = _row_tile(r, 1024), 256

    def body(s_ref, p_ref, l_ref, w_ref, m_ref, v_ref, g_ref, d_ref, mo_ref, vo_ref):
        g = p_ref[0].astype(F32)
        for j in range(3):
            g = g + l_ref[j].astype(F32)
        delta, mn, vn = _adamw(w_ref[...], g, m_ref[...], v_ref[...])
        g_ref[...] = g
        d_ref[...] = delta
        mo_ref[...] = mn
        vo_ref[...] = vn

    row = pl.BlockSpec((tr, tw), lambda i, j, s: (i, j))
    return pl.pallas_call(
        body, name="adam_big",
        grid_spec=pltpu.PrefetchScalarGridSpec(
            num_scalar_prefetch=1, grid=(r // tr, wd // tw),
            in_specs=[pl.BlockSpec((1, tr, tw), lambda i, j, s: (s[0], i, j)),
                      pl.BlockSpec((3, tr, tw), lambda i, j, s: (0, i, j)), row, row, row],
            out_specs=[row, row, row, row]),
        out_shape=[jax.ShapeDtypeStruct((r, wd), F32)] * 4,
        compiler_params=_cparams(),
    )(chip_idx, part, land, w, m, v)


def _adam_small(parts, w, m, v):
    _, r, wd = parts.shape

    def body(p_ref, w_ref, m_ref, v_ref, g_ref, d_ref, mo_ref, vo_ref):
        g = p_ref[0]
        for k in range(1, N_DEV):
            g = g + p_ref[k]
        delta, mn, vn = _adamw(w_ref[...], g, m_ref[...], v_ref[...])
        g_ref[...] = g
        d_ref[...] = delta
        mo_ref[...] = mn
        vo_ref[...] = vn

    return pl.pallas_call(
        body, name="adam_small",
        out_shape=[jax.ShapeDtypeStruct((r, wd), F32)] * 4,
        compiler_params=_cparams(),
    )(parts, w, m, v)


def _pad_rows(a, rows):
    return jnp.pad(a, ((0, rows - a.shape[0]), (0, 0)))


def _pad_w_in(w):
    cut = Q_RANK + KV_RANK + ROPE
    return jnp.concatenate([w[:, :cut], jnp.zeros((w.shape[0], 64), w.dtype), w[:, cut:]], axis=1)


def _unpad_w_in(w):
    cut = Q_RANK + KV_RANK + ROPE
    return jnp.concatenate([w[:, :cut], w[:, cut + 64:]], axis=1)


def _pack_big(p):
    parts = [p["ffn1_w_gate"][0].T, p["ffn1_w_up"][0].T, p["ffn1_w_down"][0],
             p["ffn2_w_gate"][0].T, p["ffn2_w_up"][0].T, p["ffn2_w_down"][0],
             _pad_w_in(p["w_in"][0]), p["w_out"][0], p["w_mq"][0], p["w_mo"][0],
             p["w_mkv"][0].reshape(256, D_MODEL),
             _pad_rows(p["w_q_up"][0].T.reshape(24, D_MODEL), 32),
             p["w_kv_up"][0].reshape(16, D_MODEL)]
    return jnp.concatenate(parts, axis=0)


def _unpack_big(a):
    seg = lambda n: a[SEG_OFF[n][0]:SEG_OFF[n][0] + SEG_OFF[n][1]]
    return {
        "ffn1_w_gate": seg("ffn1_g").T[None], "ffn1_w_up": seg("ffn1_u").T[None], "ffn1_w_down": seg("ffn1_d")[None],
        "ffn2_w_gate": seg("ffn2_g").T[None], "ffn2_w_up": seg("ffn2_u").T[None], "ffn2_w_down": seg("ffn2_d")[None],
        "w_in": _unpad_w_in(seg("w_in"))[None], "w_out": seg("w_out")[None], "w_mq": seg("w_mq")[None],
        "w_mo": seg("w_mo")[None], "w_mkv": seg("w_mkv").reshape(D_MODEL, 256)[None],
        "w_q_up": seg("w_q")[:24].reshape(96, Q_RANK).T[None],
        "w_kv_up": seg("w_kv").reshape(KV_RANK, 128)[None],
    }


def _unpack_gathered(full):
    seg = lambda n: full[:, SEG_OFF[n][0]:SEG_OFF[n][0] + SEG_OFF[n][1]]
    rows = lambda n: seg(n).reshape(-1, D_MODEL)
    wq_t = seg("w_q")[:, :24].reshape(MLA_HEADS, NOPE + ROPE, Q_RANK)
    wq_t = jnp.pad(wq_t, ((0, 0), (0, HEAD_PAD - NOPE - ROPE), (0, 0))).reshape(MLA_HEADS * HEAD_PAD, Q_RANK)
    wkv = seg("w_kv").reshape(N_DEV, KV_RANK, 128).transpose(1, 0, 2).reshape(KV_RANK, N_DEV * 128)
    return {
        "ffn1_g": rows("ffn1_g"), "ffn1_u": rows("ffn1_u"), "ffn1_d": rows("ffn1_d"),
        "ffn2_g": rows("ffn2_g"), "ffn2_u": rows("ffn2_u"), "ffn2_d": rows("ffn2_d"),
        "w_in": rows("w_in"), "w_out": rows("w_out"), "w_mq": rows("w_mq"), "w_mo": rows("w_mo"),
        "w_mkv": seg("w_mkv").reshape(N_DEV, D_MODEL, 256), "w_q": wq_t, "w_kv": wkv,
    }


def _pack_grads(gr):
    blk = lambda a: a.reshape(N_DEV, -1, D_MODEL)
    dwq = gr["w_q"].reshape(MLA_HEADS, HEAD_PAD, Q_RANK)[:, :NOPE + ROPE].reshape(N_DEV, 24, D_MODEL)
    dwq = jnp.pad(dwq, ((0, 0), (0, 8), (0, 0)))
    dwkv = gr["w_kv"].reshape(KV_RANK, N_DEV, 128).transpose(1, 0, 2).reshape(N_DEV, 16, D_MODEL)
    parts = [blk(gr["ffn1_g"]), blk(gr["ffn1_u"]), blk(gr["ffn1_d"]), blk(gr["ffn2_g"]), blk(gr["ffn2_u"]),
             blk(gr["ffn2_d"]), blk(gr["w_in"]), blk(gr["w_out"]), blk(gr["w_mq"]), blk(gr["w_mo"]),
             gr["w_mkv"].reshape(N_DEV, 256, D_MODEL), dwq, dwkv]
    return jnp.concatenate([a.astype(BF16) for a in parts], axis=1)


def _pack_small(vals):
    parts = []
    for n, r in SMALL_ROWS:
        parts.append(_pad_rows(vals[n].reshape(-1, 128), r) if n in vals else jnp.zeros((r, 128), F32))
    return jnp.concatenate(parts, axis=0)


def _unpack_small(a, shapes):
    out = {}
    for n, shape in shapes.items():
        o = SMALL_OFF[n][0]
        out[n] = a[o:o + int(np.prod(shape)) // 128].reshape(shape)
    return out


BIG_NAMES = ("ffn1_w_gate", "ffn1_w_up", "ffn1_w_down", "w_in", "w_q_up", "w_kv_up", "w_out", "w_mq", "w_mkv",
             "w_mo", "ffn2_w_gate", "ffn2_w_up", "ffn2_w_down")
SMALL_NAMES = ("ffn1_norm", "mix_norm", "q_norm", "kv_norm", "pool_w", "pool_scale", "xattn_norm", "mem_norm",
               "ffn2_norm", "final_norm")
WEIGHT_ORDER = ("ffn1_norm", "ffn1_w_gate", "ffn1_w_up", "ffn1_w_down", "mix_norm", "w_in", "q_norm", "w_q_up",
                "kv_norm", "w_kv_up", "pool_w", "pool_scale", "w_out", "xattn_norm", "mem_norm", "w_mq", "w_mkv",
                "w_mo", "ffn2_norm", "ffn2_w_gate", "ffn2_w_up", "ffn2_w_down", "final_norm")


def _rope_table():
    lane = np.arange(128)
    freqs = (1.0 / (ROPE_BASE ** (np.arange(0, ROPE, 2, dtype=np.float32) / ROPE))).astype(np.float32)
    tab = np.zeros((8, 128), np.float32)
    tab[0] = np.where(lane < ROPE, freqs[lane % (ROPE // 2)], 0.0)
    tab[1] = np.where(lane < ROPE // 2, -1.0, np.where(lane < ROPE, 1.0, 0.0))
    return jnp.asarray(tab)


def kernel(x, mem, positions, ffn1_norm, ffn1_w_gate, ffn1_w_up, ffn1_w_down, mix_norm, w_in, q_norm, w_q_up, kv_norm, w_kv_up, pool_w, pool_scale, w_out, xattn_norm, mem_norm, w_mq, w_mkv, w_mo, ffn2_norm, ffn2_w_gate, ffn2_w_up, ffn2_w_down, final_norm, loss_target, m_ffn1_norm, m_ffn1_w_gate, m_ffn1_w_up, m_ffn1_w_down, m_mix_norm, m_w_in, m_q_norm, m_w_q_up, m_kv_norm, m_w_kv_up, m_pool_w, m_pool_scale, m_w_out, m_xattn_norm, m_mem_norm, m_w_mq, m_w_mkv, m_w_mo, m_ffn2_norm, m_ffn2_w_gate, m_ffn2_w_up, m_ffn2_w_down, m_final_norm, v_ffn1_norm, v_ffn1_w_gate, v_ffn1_w_up, v_ffn1_w_down, v_mix_norm, v_w_in, v_q_norm, v_w_q_up, v_kv_norm, v_w_kv_up, v_pool_w, v_pool_scale, v_w_out, v_xattn_norm, v_mem_norm, v_w_mq, v_w_mkv, v_w_mo, v_ffn2_norm, v_ffn2_w_gate, v_ffn2_w_up, v_ffn2_w_down, v_final_norm):
    wts = dict(ffn1_norm=ffn1_norm, ffn1_w_gate=ffn1_w_gate, ffn1_w_up=ffn1_w_up, ffn1_w_down=ffn1_w_down,
               mix_norm=mix_norm, w_in=w_in, q_norm=q_norm, w_q_up=w_q_up, kv_norm=kv_norm, w_kv_up=w_kv_up,
               pool_w=pool_w, pool_scale=pool_scale, w_out=w_out, xattn_norm=xattn_norm, mem_norm=mem_norm,
               w_mq=w_mq, w_mkv=w_mkv, w_mo=w_mo, ffn2_norm=ffn2_norm, ffn2_w_gate=ffn2_w_gate,
               ffn2_w_up=ffn2_w_up, ffn2_w_down=ffn2_w_down, final_norm=final_norm)
    mom = dict(ffn1_norm=m_ffn1_norm, ffn1_w_gate=m_ffn1_w_gate, ffn1_w_up=m_ffn1_w_up, ffn1_w_down=m_ffn1_w_down,
               mix_norm=m_mix_norm, w_in=m_w_in, q_norm=m_q_norm, w_q_up=m_w_q_up, kv_norm=m_kv_norm,
               w_kv_up=m_w_kv_up, pool_w=m_pool_w, pool_scale=m_pool_scale, w_out=m_w_out, xattn_norm=m_xattn_norm,
               mem_norm=m_mem_norm, w_mq=m_w_mq, w_mkv=m_w_mkv, w_mo=m_w_mo, ffn2_norm=m_ffn2_norm,
               ffn2_w_gate=m_ffn2_w_gate, ffn2_w_up=m_ffn2_w_up, ffn2_w_down=m_ffn2_w_down, final_norm=m_final_norm)
    var = dict(ffn1_norm=v_ffn1_norm, ffn1_w_gate=v_ffn1_w_gate, ffn1_w_up=v_ffn1_w_up, ffn1_w_down=v_ffn1_w_down,
               mix_norm=v_mix_norm, w_in=v_w_in, q_norm=v_q_norm, w_q_up=v_w_q_up, kv_norm=v_kv_norm,
               w_kv_up=v_w_kv_up, pool_w=v_pool_w, pool_scale=v_pool_scale, w_out=v_w_out, xattn_norm=v_xattn_norm,
               mem_norm=v_mem_norm, w_mq=v_w_mq, w_mkv=v_w_mkv, w_mo=v_w_mo, ffn2_norm=v_ffn2_norm,
               ffn2_w_gate=v_ffn2_w_gate, ffn2_w_up=v_ffn2_w_up, ffn2_w_down=v_ffn2_w_down, final_norm=v_final_norm)

    t = x.shape[1]
    xs = x[0]
    mems = mem[0]
    target = loss_target[0]
    pos = positions.reshape(t, 1)
    row = lambda a: a.reshape(1, -1)
    rope_tab = _rope_table()

    w_pack = _pack_big(wts)
    fw = _unpack_gathered(_all_gather_rows(w_pack.astype(BF16)))
    g_ffn1, g_mix, g_q, g_kv = row(ffn1_norm), row(mix_norm), row(q_norm), row(kv_norm)
    g_x, g_mem, g_ffn2, g_fin = row(xattn_norm), row(mem_norm), row(ffn2_norm), row(final_norm)
    pool_wb = pool_w[0].astype(BF16)
    pool_sc = row(pool_scale)

    h1, n1, gate1, up1 = _ffn_fwd(xs, g_ffn1, fw["ffn1_g"], fw["ffn1_u"], fw["ffn1_d"], "ffn1_fwd")
    u, z, qn, kvn, qh, kh, vh = _mix_prep(h1, g_mix, fw["w_in"], g_q, fw["w_q"], g_kv, fw["w_kv"], pos, rope_tab)
    a, lse = _attn_fwd(qh, kh, vh)
    p = _pool_fwd(z, pool_wb, pool_sc)
    memn, km, vm = _mem_kv(mems, g_mem, fw["w_mkv"])
    h2, h3, hn, qm, om = _xattn_fwd(h1, a, p, fw["w_out"], g_x, fw["w_mq"], km, vm, fw["w_mo"])
    h4, n2, gate2, up2 = _ffn_fwd(h3, g_ffn2, fw["ffn2_g"], fw["ffn2_u"], fw["ffn2_d"], "ffn2_fwd")
    loss_part, dh4, dg_fin = _loss_head(h4, target, g_fin)

    gr = {}
    dh3, dgate2, dup2, act2, dg_ffn2 = _ffn_bwd_data(dh4, h3, g_ffn2, gate2, up2, fw["ffn2_g"], fw["ffn2_u"],
                                                     fw["ffn2_d"], "ffn2_bwd")
    gr["ffn2_g"] = _tn_matmul(dgate2, n2, "ffn2_dwg", tmm=1408)
    gr["ffn2_u"] = _tn_matmul(dup2, n2, "ffn2_dwu", tmm=1408)
    gr["ffn2_d"] = _tn_matmul(act2, dh4, "ffn2_dwd", scale=0.5, tmm=1408)
    dh2, dqm, da, dp, dkm, dvm, dg_x = _xattn_bwd(dh3, h2, qm, g_x, fw["w_mq"], km, vm, fw["w_mo"], fw["w_out"])
    gr["w_mo"] = _tn_matmul(om, dh3, "dw_mo", tmm=512)
    gr["w_mq"] = _tn_matmul(hn, dqm, "dw_mq", tmm=512)
    gr["w_out"] = jnp.concatenate([_tn_matmul(a, dh2, "dw_out_a"), _tn_matmul(p, dh2, "dw_out_p")], axis=0)
    gr["w_mkv"], dg_mem = _mem_kv_bwd(dkm, dvm, memn, mems, g_mem, fw["w_mkv"])
    dz_pool, d_pool_w, d_pool_sc = _pool_bwd(dp, z, pool_wb, pool_sc)
    dqh, dkh, dvh = _attn_bwd(qh, kh, vh, a, da, lse)
    dh1, dq, dkv, dz, dg_q, dg_kv, dg_mix = _mla_bwd(dqh, dkh, dvh, z, dz_pool, h1, dh2, g_mix, fw["w_in"], g_q,
                                                     fw["w_q"], g_kv, fw["w_kv"], pos, rope_tab)
    gr["w_q"] = _tn_matmul(dq, qn, "dw_q", tmm=512)
    gr["w_kv"] = _tn_matmul(kvn, dkv, "dw_kv")
    gr["w_in"] = _tn_matmul(u, dz, "dw_in", tmm=512)
    dx, dgate1, dup1, act1, dg_ffn1 = _ffn_bwd_data(dh1, xs, g_ffn1, gate1, up1, fw["ffn1_g"], fw["ffn1_u"],
                                                    fw["ffn1_d"], "ffn1_bwd")
    gr["ffn1_g"] = _tn_matmul(dgate1, n1, "ffn1_dwg", tmm=1408)
    gr["ffn1_u"] = _tn_matmul(dup1, n1, "ffn1_dwu", tmm=1408)
    gr["ffn1_d"] = _tn_matmul(act1, dh1, "ffn1_dwd", scale=0.5, tmm=1408)

    cx, cy, cc = _coords()
    g_pack = _pack_grads(gr)
    land_core = _exchange_core(g_pack)
    part = _chip_partial(g_pack, land_core, cc.astype(jnp.int32).reshape(1))
    land_chip = _exchange_chips(part)
    big = _adam_big(part, land_chip, w_pack, _pack_big(mom), _pack_big(var),
                    (2 * cx + cy).astype(jnp.int32).reshape(1))
    big = [_unpack_big(b) for b in big]

    small_g = dict(ffn1_norm=dg_ffn1, mix_norm=dg_mix, q_norm=dg_q, kv_norm=dg_kv, pool_w=d_pool_w,
                   pool_scale=d_pool_sc, xattn_norm=dg_x, mem_norm=dg_mem, ffn2_norm=dg_ffn2, final_norm=dg_fin,
                   loss=loss_part)
    parts = _all_gather_direct(_pack_small(small_g))
    small = _adam_small(parts, _pack_small({n: wts[n] for n in SMALL_NAMES}),
                        _pack_small({n: mom[n] for n in SMALL_NAMES}), _pack_small({n: var[n] for n in SMALL_NAMES}))
    loss = small[0][SMALL_OFF["loss"][0], 0]
    shapes = {n: wts[n].shape for n in SMALL_NAMES}
    small = [_unpack_small(s, shapes) for s in small]

    outs = [loss, dx[None]]
    for k in range(4):
        for n in WEIGHT_ORDER:
            outs.append(big[k][n] if n in BIG_NAMES else small[k][n])
    return tuple(outs)
```

```python
import numpy as np

import jax
import jax.numpy as jnp
from jax import lax
from jax.experimental import pallas as pl
from jax.experimental.pallas import tpu as pltpu

F32 = jnp.float32
BF16 = jnp.bfloat16

N_DEV = 8
D_MODEL = 1024
D_FF = 2816
MLA_HEADS = 4
NOPE = 128
ROPE = 64
HEAD_PAD = 256
V_DIM = 128
Q_RANK = 256
KV_RANK = 128
POOL_WINDOWS = (2, 4, 8, 16)
POOL_CH = 128
POOL_HALO = 16
N_MEM = 256
MEM_HEADS = 4
MEM_HD = 256
ROPE_BASE = 10000.0
RMS_EPS = 1e-6
ATTN_SCALE = (NOPE + ROPE) ** -0.5
MEM_SCALE = MEM_HD ** -0.5
NEG_BIG = -1e30

ADAM_LR = 0.001
ADAM_B1 = 0.9
ADAM_B2 = 0.999
ADAM_EPS = 1e-08
ADAM_WD = 0.01
ADAM_STEP = 10
ADAM_C1 = 1.0 - ADAM_B1 ** ADAM_STEP
ADAM_C2 = 1.0 - ADAM_B2 ** ADAM_STEP

VMEM_LIMIT_BYTES = 52 * 1024 * 1024
BF16_ROWS = 16

SEG_ROWS = (("ffn1_g", 352), ("ffn1_u", 352), ("ffn1_d", 352), ("ffn2_g", 352), ("ffn2_u", 352), ("ffn2_d", 352),
            ("w_in", 128), ("w_out", 128), ("w_mq", 128), ("w_mo", 128), ("w_mkv", 256), ("w_q", 32), ("w_kv", 16))
SEG_OFF = {}
_o = 0
for _n, _r in SEG_ROWS:
    SEG_OFF[_n] = (_o, _r)
    _o += _r
PACK_ROWS = _o

SMALL_ROWS = (("ffn1_norm", 8), ("mix_norm", 8), ("q_norm", 8), ("kv_norm", 8), ("pool_w", 512), ("pool_scale", 8),
              ("xattn_norm", 8), ("mem_norm", 8), ("ffn2_norm", 8), ("final_norm", 8), ("loss", 8))
SMALL_OFF = {}
_o = 0
for _n, _r in SMALL_ROWS:
    SMALL_OFF[_n] = (_o, _r)
    _o += _r


def _cparams(**kw):
    return pltpu.CompilerParams(vmem_limit_bytes=VMEM_LIMIT_BYTES, **kw)


def _row_tile(rows, limit):
    best = None
    for cand in range(BF16_ROWS, min(rows, limit) + 1, BF16_ROWS):
        if rows % cand == 0:
            best = cand
    assert best is not None, rows
    return best


def _dot_nn(a, b):
    return lax.dot_general(a, b, (((1,), (0,)), ((), ())), preferred_element_type=F32)


def _dot_nt(a, b):
    return lax.dot_general(a, b, (((1,), (1,)), ((), ())), preferred_element_type=F32)


def _dot_tn(a, b):
    return lax.dot_general(a, b, (((0,), (0,)), ((), ())), preferred_element_type=F32)


def _rms_fwd(x, g):
    r = lax.rsqrt(jnp.mean(x * x, axis=-1, keepdims=True) + RMS_EPS)
    return x * r * g, r


def _rms_bwd(dy, x, g, r):
    xhat = x * r
    dyg = dy * g
    dx = r * (dyg - xhat * jnp.mean(dyg * xhat, axis=-1, keepdims=True))
    dg = jnp.sum(dy * xhat, axis=0, keepdims=True)
    return dx, dg


def _accumulate(ref, val, first):
    if isinstance(first, bool):
        if first:
            ref[...] = val
        else:
            ref[...] += val
        return

    @pl.when(first)
    def _():
        ref[...] = val

    @pl.when(jnp.logical_not(first))
    def _():
        ref[...] += val


def _rope_tables(pos_col, tab):
    ang = pos_col.astype(F32) * tab[0:1, :]
    return jnp.cos(ang), jnp.sin(ang) * tab[1:2, :]


def _swap_halves(x):
    lane = lax.broadcasted_iota(jnp.int32, x.shape, 1)
    return jnp.where((lane % 64) < 32, pltpu.roll(x, 96, 1), pltpu.roll(x, 32, 1))


def _rope_apply(x, cos_t, sin_t):
    return x * cos_t + _swap_halves(x) * sin_t


def _rope_apply_t(dy, cos_t, sin_t):
    return dy * cos_t + _swap_halves(dy * sin_t)


def _ffn_fwd(h, g, wg_t, wu_t, wd, name):
    t, d = h.shape
    f = wg_t.shape[0]
    tm, tf = min(1024, t), 256
    nf = f // tf

    def body(h_ref, g_ref, wg_ref, wu_ref, wd_ref, ho_ref, n_ref, gate_ref, up_ref, nb_sc, acc_sc):
        j = pl.program_id(1)

        @pl.when(j == 0)
        def _():
            y, _ = _rms_fwd(h_ref[...], g_ref[...])
            nb = y.astype(BF16)
            nb_sc[...] = nb
            n_ref[...] = nb
            acc_sc[...] = jnp.zeros_like(acc_sc)

        nb = nb_sc[...]
        gt = _dot_nt(nb, wg_ref[...])
        ut = _dot_nt(nb, wu_ref[...])
        gate_ref[...] = gt.astype(BF16)
        up_ref[...] = ut.astype(BF16)
        act = (gt * jax.nn.sigmoid(gt)) * ut
        acc_sc[...] += _dot_nn(act.astype(BF16), wd_ref[...])

        @pl.when(j == nf - 1)
        def _():
            ho_ref[...] = h_ref[...] + 0.5 * acc_sc[...]

    return pl.pallas_call(
        body, name=name, grid=(t // tm, nf),
        in_specs=[pl.BlockSpec((tm, d), lambda i, j: (i, 0)),
                  pl.BlockSpec((1, d), lambda i, j: (0, 0)),
                  pl.BlockSpec((tf, d), lambda i, j: (j, 0)),
                  pl.BlockSpec((tf, d), lambda i, j: (j, 0)),
                  pl.BlockSpec((tf, d), lambda i, j: (j, 0))],
        out_specs=[pl.BlockSpec((tm, d), lambda i, j: (i, 0)),
                   pl.BlockSpec((tm, d), lambda i, j: (i, 0)),
                   pl.BlockSpec((tm, tf), lambda i, j: (i, j)),
                   pl.BlockSpec((tm, tf), lambda i, j: (i, j))],
        out_shape=[jax.ShapeDtypeStruct((t, d), F32), jax.ShapeDtypeStruct((t, d), BF16),
                   jax.ShapeDtypeStruct((t, f), BF16), jax.ShapeDtypeStruct((t, f), BF16)],
        scratch_shapes=[pltpu.VMEM((tm, d), BF16), pltpu.VMEM((tm, d), F32)],
        compiler_params=_cparams(),
    )(h, g, wg_t, wu_t, wd)


def _ffn_bwd_data(dho, h, g, gate, up, wg_t, wu_t, wd, name):
    t, d = h.shape
    f = wg_t.shape[0]
    tm, tf = min(1024, t), 256
    nf = f // tf

    def body(dho_ref, h_ref, g_ref, gate_ref, up_ref, wg_ref, wu_ref, wd_ref,
             dh_ref, dgate_ref, dup_ref, act_ref, dg_ref, dhb_sc, acc_sc):
        i, j = pl.program_id(0), pl.program_id(1)

        @pl.when(j == 0)
        def _():
            dhb_sc[...] = (0.5 * dho_ref[...]).astype(BF16)
            acc_sc[...] = jnp.zeros_like(acc_sc)

        dact = _dot_nt(dhb_sc[...], wd_ref[...])
        gt = gate_ref[...].astype(F32)
        ut = up_ref[...].astype(F32)
        sg = jax.nn.sigmoid(gt)
        silu = gt * sg
        dgb = (dact * ut * (sg * (1.0 + gt * (1.0 - sg)))).astype(BF16)
        dub = (dact * silu).astype(BF16)
        act_ref[...] = (silu * ut).astype(BF16)
        dgate_ref[...] = dgb
        dup_ref[...] = dub
        acc_sc[...] += _dot_nn(dgb, wg_ref[...]) + _dot_nn(dub, wu_ref[...])

        @pl.when(j == nf - 1)
        def _():
            x = h_ref[...]
            gg = g_ref[...]
            _, r = _rms_fwd(x, gg)
            dx, dg = _rms_bwd(acc_sc[...], x, gg, r)
            dh_ref[...] = dho_ref[...] + dx
            _accumulate(dg_ref, dg, i == 0)

    return pl.pallas_call(
        body, name=name, grid=(t // tm, nf),
        in_specs=[pl.BlockSpec((tm, d), lambda i, j: (i, 0)),
                  pl.BlockSpec((tm, d), lambda i, j: (i, 0)),
                  pl.BlockSpec((1, d), lambda i, j: (0, 0)),
                  pl.BlockSpec((tm, tf), lambda i, j: (i, j)),
                  pl.BlockSpec((tm, tf), lambda i, j: (i, j)),
                  pl.BlockSpec((tf, d), lambda i, j: (j, 0)),
                  pl.BlockSpec((tf, d), lambda i, j: (j, 0)),
                  pl.BlockSpec((tf, d), lambda i, j: (j, 0))],
        out_specs=[pl.BlockSpec((tm, d), lambda i, j: (i, 0)),
                   pl.BlockSpec((tm, tf), lambda i, j: (i, j)),
                   pl.BlockSpec((tm, tf), lambda i, j: (i, j)),
                   pl.BlockSpec((tm, tf), lambda i, j: (i, j)),
                   pl.BlockSpec((1, d), lambda i, j: (0, 0))],
        out_shape=[jax.ShapeDtypeStruct((t, d), F32), jax.ShapeDtypeStruct((t, f), BF16),
                   jax.ShapeDtypeStruct((t, f), BF16), jax.ShapeDtypeStruct((t, f), BF16),
                   jax.ShapeDtypeStruct((1, d), F32)],
        scratch_shapes=[pltpu.VMEM((tm, d), BF16), pltpu.VMEM((tm, d), F32)],
        compiler_params=_cparams(),
    )(dho, h, g, gate, up, wg_t, wu_t, wd)


def _tn_matmul(a, b, name, scale=1.0, tmm=None):
    t, m = a.shape
    n = b.shape[1]
    tmm = m if tmm is None else tmm
    tk = min(1024, t)

    def body(a_ref, b_ref, o_ref):
        k = pl.program_id(1)
        prod = _dot_tn(a_ref[...].astype(BF16), b_ref[...].astype(BF16))
        if scale != 1.0:
            prod = prod * scale
        _accumulate(o_ref, prod, k == 0)

    return pl.pallas_call(
        body, name=name, grid=(m // tmm, t // tk),
        in_specs=[pl.BlockSpec((tk, tmm), lambda i, k: (k, i)),
                  pl.BlockSpec((tk, n), lambda i, k: (k, 0))],
        out_specs=pl.BlockSpec((tmm, n), lambda i, k: (i, 0)),
        out_shape=jax.ShapeDtypeStruct((m, n), F32),
        compiler_params=_cparams(),
    )(a, b)


def _loss_head(h, target, g):
    t, d = h.shape
    tm = min(512, t)

    def body(h_ref, t_ref, g_ref, loss_ref, dh_ref, dg_ref):
        i = pl.program_id(0)
        x = h_ref[...]
        gg = g_ref[...]
        y, r = _rms_fwd(x, gg)
        err = y - t_ref[...]
        part = 0.5 * jnp.sum(jnp.mean(err * err, axis=-1, keepdims=True), axis=0, keepdims=True)
        dx, dg = _rms_bwd(err * (1.0 / d), x, gg, r)
        dh_ref[...] = dx
        _accumulate(loss_ref, jnp.broadcast_to(part, loss_ref.shape), i == 0)
        _accumulate(dg_ref, dg, i == 0)

    return pl.pallas_call(
        body, name="loss_head", grid=(t // tm,),
        in_specs=[pl.BlockSpec((tm, d), lambda i: (i, 0)),
                  pl.BlockSpec((tm, d), lambda i: (i, 0)),
                  pl.BlockSpec((1, d), lambda i: (0, 0))],
        out_specs=[pl.BlockSpec((8, 128), lambda i: (0, 0)),
                   pl.BlockSpec((tm, d), lambda i: (i, 0)),
                   pl.BlockSpec((1, d), lambda i: (0, 0))],
        out_shape=[jax.ShapeDtypeStruct((8, 128), F32), jax.ShapeDtypeStruct((t, d), F32),
                   jax.ShapeDtypeStruct((1, d), F32)],
        compiler_params=_cparams(),
    )(h, target, g)


def _mix_prep(h1, mix_norm, w_in, q_norm, wq_t, kv_norm, wkv, pos, rope_tab):
    t, d = h1.shape
    tm = min(512, t)

    def body(h_ref, gm_ref, win_ref, gq_ref, wq_ref, gkv_ref, wkv_ref, pos_ref, tab_ref,
             u_ref, z_ref, qn_ref, kvn_ref, q_ref, k_ref, v_ref):
        u, _ = _rms_fwd(h_ref[...], gm_ref[...])
        ub = u.astype(BF16)
        u_ref[...] = ub
        z = _dot_nn(ub, win_ref[...])
        z_ref[...] = z
        cos_t, sin_t = _rope_tables(pos_ref[...], tab_ref[...])
        qn, _ = _rms_fwd(z[:, 0:Q_RANK], gq_ref[...])
        qnb = qn.astype(BF16)
        qn_ref[...] = qnb
        q = _dot_nt(qnb, wq_ref[...])
        kvn, _ = _rms_fwd(z[:, Q_RANK:Q_RANK + KV_RANK], gkv_ref[...])
        kvnb = kvn.astype(BF16)
        kvn_ref[...] = kvnb
        kv = _dot_nn(kvnb, wkv_ref[...])
        k_pe = _rope_apply(z[:, Q_RANK + KV_RANK:Q_RANK + KV_RANK + 128], cos_t, sin_t)
        ones = jnp.ones((tm, V_DIM), F32)
        for hh in range(MLA_HEADS):
            b = hh * HEAD_PAD
            q_pe = _rope_apply(q[:, b + NOPE:b + HEAD_PAD], cos_t, sin_t)
            q_ref[hh] = jnp.concatenate([q[:, b:b + NOPE], q_pe], axis=-1).astype(BF16)
            k_ref[hh] = jnp.concatenate([kv[:, b:b + NOPE], k_pe], axis=-1).astype(BF16)
            v_ref[hh] = jnp.concatenate([kv[:, b + NOPE:b + HEAD_PAD], ones], axis=-1).astype(BF16)

    full = lambda shape: pl.BlockSpec(shape, lambda i: (0,) * len(shape))
    return pl.pallas_call(
        body, name="mix_prep", grid=(t // tm,),
        in_specs=[pl.BlockSpec((tm, d), lambda i: (i, 0)), full((1, d)), full(w_in.shape), full((1, Q_RANK)),
                  full(wq_t.shape), full((1, KV_RANK)), full(wkv.shape),
                  pl.BlockSpec((tm, 1), lambda i: (i, 0)), full(rope_tab.shape)],
        out_specs=[pl.BlockSpec((tm, d), lambda i: (i, 0)),
                   pl.BlockSpec((tm, d), lambda i: (i, 0)),
                   pl.BlockSpec((tm, Q_RANK), lambda i: (i, 0)),
                   pl.BlockSpec((tm, KV_RANK), lambda i: (i, 0)),
                   pl.BlockSpec((MLA_HEADS, tm, HEAD_PAD), lambda i: (0, i, 0)),
                   pl.BlockSpec((MLA_HEADS, tm, HEAD_PAD), lambda i: (0, i, 0)),
                   pl.BlockSpec((MLA_HEADS, tm, 2 * V_DIM), lambda i: (0, i, 0))],
        out_shape=[jax.ShapeDtypeStruct((t, d), BF16), jax.ShapeDtypeStruct((t, d), F32),
                   jax.ShapeDtypeStruct((t, Q_RANK), BF16), jax.ShapeDtypeStruct((t, KV_RANK), BF16),
                   jax.ShapeDtypeStruct((MLA_HEADS, t, HEAD_PAD), BF16),
                   jax.ShapeDtypeStruct((MLA_HEADS, t, HEAD_PAD), BF16),
                   jax.ShapeDtypeStruct((MLA_HEADS, t, 2 * V_DIM), BF16)],
        compiler_params=_cparams(),
    )(h1, mix_norm, w_in, q_norm, wq_t, kv_norm, wkv, pos, rope_tab)


def _causal_mask(s):
    row = lax.broadcasted_iota(jnp.int32, s.shape, 0)
    col = lax.broadcasted_iota(jnp.int32, s.shape, 1)
    return jnp.where(col <= row, s, NEG_BIG)


def _attn_fwd(q, k, v):
    nh, t, _ = q.shape
    tq = tk = min(512, t)
    nq, nk = t // tq, t // tk

    def body(q_ref, k_ref, v_ref, o_ref, lse_ref, m_sc, acc_sc):
        i, j = pl.program_id(0), pl.program_id(1)

        @pl.when(j == 0)
        def _():
            m_sc[...] = jnp.full_like(m_sc, NEG_BIG)
            acc_sc[...] = jnp.zeros_like(acc_sc)

        def step(diagonal):
            for hh in range(nh):
                s = _dot_nt(q_ref[hh], k_ref[hh]) * ATTN_SCALE
                if diagonal:
                    s = _causal_mask(s)
                m_old = m_sc[hh]
                m_new = jnp.maximum(m_old, jnp.max(s, axis=-1, keepdims=True))
                p = jnp.exp(s - m_new).astype(BF16)
                acc_sc[hh] = jnp.exp(m_old - m_new) * acc_sc[hh] + _dot_nn(p, v_ref[hh])
                m_sc[hh] = m_new

        @pl.when(j < i)
        def _():
            step(False)

        @pl.when(j == i)
        def _():
            step(True)
            for hh in range(nh):
                acc = acc_sc[hh]
                l = acc[:, V_DIM:2 * V_DIM]
                o_ref[:, hh * V_DIM:(hh + 1) * V_DIM] = (acc[:, 0:V_DIM] / l).astype(BF16)
                lse_ref[hh] = m_sc[hh] + jnp.log(l[:, 0:1])

    kv_map = lambda i, j: (0, jnp.minimum(j, i), 0)
    return pl.pallas_call(
        body, name="attn_fwd", grid=(nq, nk),
        in_specs=[pl.BlockSpec((nh, tq, HEAD_PAD), lambda i, j: (0, i, 0)),
                  pl.BlockSpec((nh, tk, HEAD_PAD), kv_map),
                  pl.BlockSpec((nh, tk, 2 * V_DIM), kv_map)],
        out_specs=[pl.BlockSpec((tq, nh * V_DIM), lambda i, j: (i, 0)),
                   pl.BlockSpec((nh, tq, 1), lambda i, j: (0, i, 0))],
        out_shape=[jax.ShapeDtypeStruct((t, nh * V_DIM), BF16), jax.ShapeDtypeStruct((nh, t, 1), F32)],
        scratch_shapes=[pltpu.VMEM((nh, tq, 1), F32), pltpu.VMEM((nh, tq, 2 * V_DIM), F32)],
        compiler_params=_cparams(),
    )(q, k, v)


def _attn_delta(o, do):
    t, w = o.shape
    nh = w // V_DIM
    tm = min(512, t)

    def body(o_ref, do_ref, d_ref):
        prod = o_ref[...].astype(F32) * do_ref[...].astype(F32)
        for hh in range(nh):
            d_ref[hh] = jnp.sum(prod[:, hh * V_DIM:(hh + 1) * V_DIM], axis=-1, keepdims=True)

    return pl.pallas_call(
        body, name="attn_delta", grid=(t // tm,),
        in_specs=[pl.BlockSpec((tm, w), lambda i: (i, 0)), pl.BlockSpec((tm, w), lambda i: (i, 0))],
        out_specs=pl.BlockSpec((nh, tm, 1), lambda i: (0, i, 0)),
        out_shape=jax.ShapeDtypeStruct((nh, t, 1), F32),
        compiler_params=_cparams(),
    )(o, do)


ATTN_BWD_HEADS = 2


def _attn_bwd(q, k, v, do, lse, delta):
    nh, t, _ = q.shape
    hp = ATTN_BWD_HEADS
    tq = tk = min(512, t)
    nq, nk = t // tq, t // tk

    def body(q_ref, k_ref, v_ref, do_ref, lse_ref, dlt_ref, dq_ref, dk_ref, dv_ref):
        j, i = pl.program_id(1), pl.program_id(2)

        def step(diagonal):
            for hh in range(hp):
                qq, kk = q_ref[hh], k_ref[hh]
                dob = do_ref[:, hh * V_DIM:(hh + 1) * V_DIM]
                s = _dot_nt(qq, kk) * ATTN_SCALE
                if diagonal:
                    s = _causal_mask(s)
                p = jnp.exp(s - lse_ref[hh])
                dpp = _dot_nt(dob, v_ref[hh])
                dsb = (p * (dpp - dlt_ref[hh]) * ATTN_SCALE).astype(BF16)
                _accumulate(dv_ref.at[hh], _dot_tn(p.astype(BF16), dob), diagonal)
                _accumulate(dk_ref.at[hh], _dot_tn(dsb, qq), diagonal)
                _accumulate(dq_ref.at[hh, pl.ds(pl.multiple_of(i * tq, tq), tq), :], _dot_nn(dsb, kk), j == 0)

        @pl.when(i > j)
        def _():
            step(False)

        @pl.when(i == j)
        def _():
            step(True)

    qmap = lambda h, j, i: (h, jnp.maximum(i, j), 0)
    return pl.pallas_call(
        body, name="attn_bwd", grid=(nh // hp, nk, nq),
        in_specs=[pl.BlockSpec((hp, tq, HEAD_PAD), qmap),
                  pl.BlockSpec((hp, tk, HEAD_PAD), lambda h, j, i: (h, j, 0)),
                  pl.BlockSpec((hp, tk, V_DIM), lambda h, j, i: (h, j, 0)),
                  pl.BlockSpec((tq, hp * V_DIM), lambda h, j, i: (jnp.maximum(i, j), h)),
                  pl.BlockSpec((hp, tq, 1), qmap),
                  pl.BlockSpec((hp, tq, 1), qmap)],
        out_specs=[pl.BlockSpec((hp, t, HEAD_PAD), lambda h, j, i: (h, 0, 0)),
                   pl.BlockSpec((hp, tk, HEAD_PAD), lambda h, j, i: (h, j, 0)),
                   pl.BlockSpec((hp, tk, V_DIM), lambda h, j, i: (h, j, 0))],
        out_shape=[jax.ShapeDtypeStruct((nh, t, HEAD_PAD), F32), jax.ShapeDtypeStruct((nh, t, HEAD_PAD), F32),
                   jax.ShapeDtypeStruct((nh, t, V_DIM), F32)],
        compiler_params=_cparams(),
    )(q, k, v, do, lse, delta)


def _pool_counts(first_token, rows, w):
    tok = lax.broadcasted_iota(jnp.int32, (rows, POOL_CH), 0) + first_token
    return jnp.minimum(tok + 1, w).astype(F32)


def _pool_centered(zbuf, g, w, i, tm):
    lanes = pl.ds(g * POOL_CH, POOL_CH)
    cur = zbuf[pl.ds(POOL_HALO, tm), lanes]
    win = cur
    for s in range(1, w):
        win = win + zbuf[pl.ds(POOL_HALO - s, tm), lanes]
    return win / _pool_counts(i * tm, tm, w) - cur


def _pool_load(zbuf, z_ref, halo_ref, i, tm):
    @pl.when(i == 0)
    def _():
        zbuf[pl.ds(0, POOL_HALO), :] = jnp.zeros((POOL_HALO, zbuf.shape[1]), F32)

    @pl.when(i > 0)
    def _():
        zbuf[pl.ds(0, POOL_HALO), :] = halo_ref[...]

    zbuf[pl.ds(POOL_HALO, tm), :] = z_ref[...]


def _pool_fwd(z, pool_w, pool_scale):
    t = z.shape[0]
    pw = len(POOL_WINDOWS) * POOL_CH
    tm = min(512, t)
    hb = tm // POOL_HALO

    def body(z_ref, halo_ref, w_ref, sc_ref, p_ref, zbuf):
        i = pl.program_id(0)
        _pool_load(zbuf, z_ref, halo_ref, i, tm)
        for g, w in enumerate(POOL_WINDOWS):
            c = _pool_centered(zbuf, g, w, i, tm)
            y = _dot_nn(c.astype(BF16), w_ref[g]) * sc_ref[:, g * POOL_CH:(g + 1) * POOL_CH]
            p_ref[:, g * POOL_CH:(g + 1) * POOL_CH] = y.astype(BF16)

    return pl.pallas_call(
        body, name="pool_fwd", grid=(t // tm,),
        in_specs=[pl.BlockSpec((tm, pw), lambda i: (i, 1)),
                  pl.BlockSpec((POOL_HALO, pw), lambda i: (jnp.maximum(i * hb - 1, 0), 1)),
                  pl.BlockSpec(pool_w.shape, lambda i: (0, 0, 0)),
                  pl.BlockSpec((1, pw), lambda i: (0, 0))],
        out_specs=pl.BlockSpec((tm, pw), lambda i: (i, 0)),
        out_shape=jax.ShapeDtypeStruct((t, pw), BF16),
        scratch_shapes=[pltpu.VMEM((POOL_HALO + tm, pw), F32)],
        compiler_params=_cparams(),
    )(z, z, pool_w, pool_scale)


def _pool_bwd(dp, z, pool_w, pool_scale):
    t = z.shape[0]
    ng = len(POOL_WINDOWS)
    pw = ng * POOL_CH
    tm = min(512, t)
    hb = tm // POOL_HALO
    nt = t // tm

    def body(dp_ref, dpn_ref, z_ref, halo_ref, w_ref, sc_ref, dz_ref, dw_ref, dsc_ref, zbuf, dbuf):
        i = pl.program_id(0)
        _pool_load(zbuf, z_ref, halo_ref, i, tm)
        nxt_ok = (i < nt - 1).astype(F32)
        for g, w in enumerate(POOL_WINDOWS):
            lanes = pl.ds(g * POOL_CH, POOL_CH)
            cols = slice(g * POOL_CH, (g + 1) * POOL_CH)
            sc = sc_ref[:, cols]
            wg = w_ref[g]
            c = _pool_centered(zbuf, g, w, i, tm).astype(BF16)
            ypre = _dot_nn(c, wg)
            dpg = dp_ref[:, cols].astype(F32)
            _accumulate(dsc_ref.at[:, lanes], jnp.sum(dpg * ypre, axis=0, keepdims=True), i == 0)
            dyb = (dpg * sc).astype(BF16)
            _accumulate(dw_ref.at[g], _dot_tn(c, dyb), i == 0)
            dd = _dot_nt(dyb, wg)
            dyn = (dpn_ref[:, cols].astype(F32) * sc).astype(BF16)
            ddn = _dot_nt(dyn, wg) * nxt_ok
            dbuf[pl.ds(0, tm), lanes] = dd / _pool_counts(i * tm, tm, w)
            dbuf[pl.ds(tm, POOL_HALO), lanes] = ddn / _pool_counts((i + 1) * tm, POOL_HALO, w)
            acc = -dd
            for s in range(w):
                acc = acc + dbuf[pl.ds(s, tm), lanes]
            dz_ref[:, cols] = acc

    return pl.pallas_call(
        body, name="pool_bwd", grid=(nt,),
        in_specs=[pl.BlockSpec((tm, pw), lambda i: (i, 0)),
                  pl.BlockSpec((POOL_HALO, pw), lambda i: (jnp.minimum((i + 1) * hb, t // POOL_HALO - 1), 0)),
                  pl.BlockSpec((tm, pw), lambda i: (i, 1)),
                  pl.BlockSpec((POOL_HALO, pw), lambda i: (jnp.maximum(i * hb - 1, 0), 1)),
                  pl.BlockSpec(pool_w.shape, lambda i: (0, 0, 0)),
                  pl.BlockSpec((1, pw), lambda i: (0, 0))],
        out_specs=[pl.BlockSpec((tm, pw), lambda i: (i, 0)),
                   pl.BlockSpec((ng, POOL_CH, POOL_CH), lambda i: (0, 0, 0)),
                   pl.BlockSpec((1, pw), lambda i: (0, 0))],
        out_shape=[jax.ShapeDtypeStruct((t, pw), F32), jax.ShapeDtypeStruct((ng, POOL_CH, POOL_CH), F32),
                   jax.ShapeDtypeStruct((1, pw), F32)],
        scratch_shapes=[pltpu.VMEM((POOL_HALO + tm, pw), F32), pltpu.VMEM((tm + POOL_HALO, pw), F32)],
        compiler_params=_cparams(),
    )(dp, dp, z, z, pool_w, pool_scale)


def _mla_bwd(dq_h, dk_h, dv_h, z, dz_pool, h1, dh2, mix_norm, w_in, q_norm, wq_t, kv_norm, wkv, pos, rope_tab):
    t, d = h1.shape
    tm = min(256, t)

    def body(dqh_ref, dkh_ref, dvh_ref, z_ref, dzp_ref, h_ref, dh2_ref, gm_ref, win_ref, gq_ref, wq_ref, gkv_ref,
             wkv_ref, pos_ref, tab_ref, dh1_ref, dq_ref, dkv_ref, dz_ref, dgq_ref, dgkv_ref, dgm_ref):
        i = pl.program_id(0)
        first = i == 0
        cos_t, sin_t = _rope_tables(pos_ref[...], tab_ref[...])
        dq_parts, dkv_parts = [], []
        dk_pe = jnp.zeros((tm, 128), F32)
        for hh in range(MLA_HEADS):
            dqh = dqh_ref[hh]
            dq_parts += [dqh[:, 0:NOPE], _rope_apply_t(dqh[:, NOPE:HEAD_PAD], cos_t, sin_t)]
            dkh = dkh_ref[hh]
            dkv_parts += [dkh[:, 0:NOPE], dvh_ref[hh]]
            dk_pe = dk_pe + dkh[:, NOPE:HEAD_PAD]
        dqb = jnp.concatenate(dq_parts, axis=-1).astype(BF16)
        dkvb = jnp.concatenate(dkv_parts, axis=-1).astype(BF16)
        dq_ref[...] = dqb
        dkv_ref[...] = dkvb
        z = z_ref[...]
        c_q = z[:, 0:Q_RANK]
        gq = gq_ref[...]
        _, rq = _rms_fwd(c_q, gq)
        dcq, dgq = _rms_bwd(_dot_nn(dqb, wq_ref[...]), c_q, gq, rq)
        c_kv = z[:, Q_RANK:Q_RANK + KV_RANK]
        gkv = gkv_ref[...]
        _, rkv = _rms_fwd(c_kv, gkv)
        dckv, dgkv = _rms_bwd(_dot_nt(dkvb, wkv_ref[...]), c_kv, gkv, rkv)
        dkr = _rope_apply_t(dk_pe, cos_t, sin_t)
        dzb = jnp.concatenate([dcq, dckv, dkr, dzp_ref[...]], axis=-1).astype(BF16)
        dz_ref[...] = dzb
        x = h_ref[...]
        gm = gm_ref[...]
        _, rm = _rms_fwd(x, gm)
        dx, dgm = _rms_bwd(_dot_nt(dzb, win_ref[...]), x, gm, rm)
        dh1_ref[...] = dh2_ref[...] + dx
        _accumulate(dgq_ref, dgq, first)
        _accumulate(dgkv_ref, dgkv, first)
        _accumulate(dgm_ref, dgm, first)

    full = lambda shape: pl.BlockSpec(shape, lambda i: (0,) * len(shape))
    row = lambda w: pl.BlockSpec((tm, w), lambda i: (i, 0))
    head = lambda w: pl.BlockSpec((MLA_HEADS, tm, w), lambda i: (0, i, 0))
    pw = len(POOL_WINDOWS) * POOL_CH
    return pl.pallas_call(
        body, name="mla_bwd", grid=(t // tm,),
        in_specs=[head(HEAD_PAD), head(HEAD_PAD), head(V_DIM), row(d), row(pw), row(d), row(d),
                  full((1, d)), full(w_in.shape), full((1, Q_RANK)), full(wq_t.shape), full((1, KV_RANK)),
                  full(wkv.shape), row(1), full(rope_tab.shape)],
        out_specs=[row(d), row(d), row(d), row(d), full((1, Q_RANK)), full((1, KV_RANK)), full((1, d))],
        out_shape=[jax.ShapeDtypeStruct((t, d), F32), jax.ShapeDtypeStruct((t, d), BF16),
                   jax.ShapeDtypeStruct((t, d), BF16), jax.ShapeDtypeStruct((t, d), BF16),
                   jax.ShapeDtypeStruct((1, Q_RANK), F32), jax.ShapeDtypeStruct((1, KV_RANK), F32),
                   jax.ShapeDtypeStruct((1, d), F32)],
        compiler_params=_cparams(),
    )(dq_h, dk_h, dv_h, z, dz_pool, h1, dh2, mix_norm, w_in, q_norm, wq_t, kv_norm, wkv, pos, rope_tab)


def _mem_kv(mem, mem_norm, wmkv):
    n, d = mem.shape

    def body(mem_ref, g_ref, w_ref, memn_ref, k_ref, v_ref):
        y, _ = _rms_fwd(mem_ref[...], g_ref[...])
        yb = y.astype(BF16)
        memn_ref[...] = yb
        for hh in range(MEM_HEADS):
            k_ref[hh] = _dot_nn(yb, w_ref[hh]).astype(BF16)
            v_ref[hh] = _dot_nn(yb, w_ref[MEM_HEADS + hh]).astype(BF16)

    return pl.pallas_call(
        body, name="mem_kv",
        out_shape=[jax.ShapeDtypeStruct((n, d), BF16), jax.ShapeDtypeStruct((MEM_HEADS, n, MEM_HD), BF16),
                   jax.ShapeDtypeStruct((MEM_HEADS, n, MEM_HD), BF16)],
        compiler_params=_cparams(),
    )(mem, mem_norm, wmkv)


def _mem_softmax(qb, km):
    s = _dot_nt(qb, km) * MEM_SCALE
    e = jnp.exp(s - jnp.max(s, axis=-1, keepdims=True))
    return e / jnp.sum(e, axis=-1, keepdims=True)


def _xattn_fwd(h1, a, p, w_out, g, wmq, km, vm, wmo):
    t, d = h1.shape
    tm = min(512, t)
    half = a.shape[1]

    def body(h_ref, a_ref, p_ref, wo_ref, g_ref, wmq_ref, km_ref, vm_ref, wmo_ref,
             h2_ref, h3_ref, hn_ref, q_ref, o_ref):
        h2 = h_ref[...] + _dot_nn(a_ref[...], wo_ref[0:half, :]) + _dot_nn(p_ref[...], wo_ref[half:2 * half, :])
        h2_ref[...] = h2
        hn, _ = _rms_fwd(h2, g_ref[...])
        hnb = hn.astype(BF16)
        hn_ref[...] = hnb
        qb = _dot_nn(hnb, wmq_ref[...]).astype(BF16)
        q_ref[...] = qb
        outs = []
        for hh in range(MEM_HEADS):
            pr = _mem_softmax(qb[:, hh * MEM_HD:(hh + 1) * MEM_HD], km_ref[hh])
            outs.append(_dot_nn(pr.astype(BF16), vm_ref[hh]))
        ob = jnp.concatenate(outs, axis=-1).astype(BF16)
        o_ref[...] = ob
        h3_ref[...] = h2 + _dot_nn(ob, wmo_ref[...])

    full = lambda shape: pl.BlockSpec(shape, lambda i: (0,) * len(shape))
    row = lambda w: pl.BlockSpec((tm, w), lambda i: (i, 0))
    return pl.pallas_call(
        body, name="xattn_fwd", grid=(t // tm,),
        in_specs=[row(d), row(half), row(half), full(w_out.shape), full((1, d)), full(wmq.shape),
                  full(km.shape), full(vm.shape), full(wmo.shape)],
        out_specs=[row(d), row(d), row(d), row(d), row(d)],
        out_shape=[jax.ShapeDtypeStruct((t, d), F32), jax.ShapeDtypeStruct((t, d), F32),
                   jax.ShapeDtypeStruct((t, d), BF16), jax.ShapeDtypeStruct((t, d), BF16),
                   jax.ShapeDtypeStruct((t, d), BF16)],
        compiler_params=_cparams(),
    )(h1, a, p, w_out, g, wmq, km, vm, wmo)


def _xattn_bwd(dh3, h2, qm, g, wmq, km, vm, wmo, w_out):
    t, d = h2.shape
    tm = min(256, t)
    half = d // 2

    def body(dh3_ref, h2_ref, q_ref, g_ref, wmq_ref, km_ref, vm_ref, wmo_ref, wo_ref,
             dh2_ref, dq_ref, da_ref, dp_ref, dk_ref, dv_ref, dg_ref):
        i = pl.program_id(0)
        first = i == 0
        dh3 = dh3_ref[...]
        dob = _dot_nt(dh3.astype(BF16), wmo_ref[...]).astype(BF16)
        qb = q_ref[...]
        dq_parts = []
        for hh in range(MEM_HEADS):
            cols = slice(hh * MEM_HD, (hh + 1) * MEM_HD)
            kk, vv = km_ref[hh], vm_ref[hh]
            pr = _mem_softmax(qb[:, cols], kk)
            doh = dob[:, cols]
            _accumulate(dv_ref.at[hh], _dot_tn(pr.astype(BF16), doh), first)
            dpp = _dot_nt(doh, vv)
            dsb = (pr * (dpp - jnp.sum(dpp * pr, axis=-1, keepdims=True)) * MEM_SCALE).astype(BF16)
            dq_parts.append(_dot_nn(dsb, kk))
            _accumulate(dk_ref.at[hh], _dot_tn(dsb, qb[:, cols]), first)
        dqb = jnp.concatenate(dq_parts, axis=-1).astype(BF16)
        dq_ref[...] = dqb
        x = h2_ref[...]
        gg = g_ref[...]
        _, r = _rms_fwd(x, gg)
        dx, dg = _rms_bwd(_dot_nt(dqb, wmq_ref[...]), x, gg, r)
        dh2 = dh3 + dx
        dh2_ref[...] = dh2
        dap = _dot_nt(dh2.astype(BF16), wo_ref[...])
        da_ref[...] = dap[:, 0:half].astype(BF16)
        dp_ref[...] = dap[:, half:d].astype(BF16)
        _accumulate(dg_ref, dg, first)

    full = lambda shape: pl.BlockSpec(shape, lambda i: (0,) * len(shape))
    row = lambda w: pl.BlockSpec((tm, w), lambda i: (i, 0))
    return pl.pallas_call(
        body, name="xattn_bwd", grid=(t // tm,),
        in_specs=[row(d), row(d), row(d), full((1, d)), full(wmq.shape), full(km.shape), full(vm.shape),
                  full(wmo.shape), full(w_out.shape)],
        out_specs=[row(d), row(d), row(half), row(half), full(km.shape), full(vm.shape), full((1, d))],
        out_shape=[jax.ShapeDtypeStruct((t, d), F32), jax.ShapeDtypeStruct((t, d), BF16),
                   jax.ShapeDtypeStruct((t, half), BF16), jax.ShapeDtypeStruct((t, half), BF16),
                   jax.ShapeDtypeStruct(km.shape, F32), jax.ShapeDtypeStruct(vm.shape, F32),
                   jax.ShapeDtypeStruct((1, d), F32)],
        compiler_params=_cparams(),
    )(dh3, h2, qm, g, wmq, km, vm, wmo, w_out)


def _mem_kv_bwd(dkm, dvm, memn, mem, mem_norm, wmkv):
    n, d = mem.shape

    def body(dk_ref, dv_ref, memn_ref, mem_ref, g_ref, w_ref, dw_ref, dg_ref):
        memn = memn_ref[...]
        dmemn = jnp.zeros((n, d), F32)
        for s in range(2 * MEM_HEADS):
            src = dk_ref[s] if s < MEM_HEADS else dv_ref[s - MEM_HEADS]
            db = src.astype(BF16)
            dw_ref[s] = _dot_tn(memn, db)
            dmemn = dmemn + _dot_nt(db, w_ref[s])
        x = mem_ref[...]
        gg = g_ref[...]
        _, r = _rms_fwd(x, gg)
        _, dg = _rms_bwd(dmemn, x, gg, r)
        dg_ref[...] = dg

    return pl.pallas_call(
        body, name="mem_kv_bwd",
        out_shape=[jax.ShapeDtypeStruct(wmkv.shape, F32), jax.ShapeDtypeStruct((1, d), F32)],
        compiler_params=_cparams(),
    )(dkm, dvm, memn, mem, mem_norm, wmkv)


MESH_ID = pl.DeviceIdType.MESH
ANY = pl.BlockSpec(memory_space=pl.ANY)


def _coords():
    return lax.axis_index("x"), lax.axis_index("y"), lax.axis_index("c")


def _other_chips(x, y):
    return [(1 - x, y), (x, 1 - y), (1 - x, 1 - y)]


def _all_gather_rows(shard):
    r, w = shard.shape

    def body(x_ref, out_ref, send_sems, recv_sems, local_sem):
        x, y, c = _coords()
        me, sibling = (x, y, c), (x, y, 1 - c)
        chips = _other_chips(x, y)

        def block(px, py, pc):
            return out_ref.at[4 * px + 2 * py + pc]

        def copy(k, blk, to, src=None):
            return pltpu.make_async_remote_copy(
                src_ref=block(*blk) if src is None else src, dst_ref=block(*blk),
                send_sem=send_sems.at[k], recv_sem=recv_sems.at[k], device_id=to, device_id_type=MESH_ID)

        mine = pltpu.make_async_copy(x_ref, block(*me), local_sem)
        mine.start()
        first = [copy(0, me, sibling, src=x_ref)]
        first += [copy(1 + j, me, (*chip, c), src=x_ref) for j, chip in enumerate(chips)]
        for cp in first:
            cp.start()
        passed = [copy(4 + j, (*chip, c), sibling) for j, chip in enumerate(chips)]
        for j, chip in enumerate(chips):
            copy(1 + j, (*chip, c), me).wait_recv()
            passed[j].start()
        copy(0, sibling, me).wait_recv()
        for j, chip in enumerate(chips):
            copy(4 + j, (*chip, 1 - c), me).wait_recv()
        for cp in first + passed:
            cp.wait_send()
        mine.wait()

    return pl.pallas_call(
        body, name="all_gather_rows",
        out_shape=jax.ShapeDtypeStruct((N_DEV, r, w), shard.dtype),
        in_specs=[ANY], out_specs=ANY,
        scratch_shapes=[pltpu.SemaphoreType.DMA((7,)), pltpu.SemaphoreType.DMA((7,)), pltpu.SemaphoreType.DMA],
    )(shard)


def _all_gather_direct(shard):
    r, w = shard.shape

    def body(x_ref, out_ref, send_sems, recv_sems, local_sem):
        x, y, c = _coords()
        mine = pltpu.make_async_copy(x_ref, out_ref.at[4 * x + 2 * y + c], local_sem)
        mine.start()
        arrivals = []
        for k in range(1, N_DEV):
            px, py, pc = x ^ ((k >> 2) & 1), y ^ ((k >> 1) & 1), c ^ (k & 1)
            pltpu.make_async_remote_copy(
                src_ref=x_ref, dst_ref=out_ref.at[4 * x + 2 * y + c],
                send_sem=send_sems.at[k - 1], recv_sem=recv_sems.at[k - 1],
                device_id=(px, py, pc), device_id_type=MESH_ID).start()
            arrivals.append(pltpu.make_async_remote_copy(
                src_ref=x_ref, dst_ref=out_ref.at[4 * px + 2 * py + pc],
                send_sem=send_sems.at[k - 1], recv_sem=recv_sems.at[k - 1],
                device_id=(px, py, pc), device_id_type=MESH_ID))
        for cp in arrivals:
            cp.wait_recv()
        for cp in arrivals:
            cp.wait_send()
        mine.wait()

    return pl.pallas_call(
        body, name="all_gather_direct",
        out_shape=jax.ShapeDtypeStruct((N_DEV, r, w), shard.dtype),
        in_specs=[pl.BlockSpec(memory_space=pltpu.VMEM)], out_specs=pl.BlockSpec(memory_space=pltpu.VMEM),
        scratch_shapes=[pltpu.SemaphoreType.DMA((7,)), pltpu.SemaphoreType.DMA((7,)), pltpu.SemaphoreType.DMA],
        compiler_params=_cparams(),
    )(shard)


def _exchange_core(g):
    _, r, w = g.shape

    def body(g_ref, land_ref, send_sems, recv_sems):
        x, y, c = _coords()
        copies = []
        for chip in range(4):
            copies.append(pltpu.make_async_remote_copy(
                src_ref=g_ref.at[2 * chip + (1 - c)], dst_ref=land_ref.at[chip],
                send_sem=send_sems.at[chip], recv_sem=recv_sems.at[chip],
                device_id=(x, y, 1 - c), device_id_type=MESH_ID))
        for cp in copies:
            cp.start()
        for cp in copies:
            cp.wait_recv()
        for cp in copies:
            cp.wait_send()

    return pl.pallas_call(
        body, name="exchange_core",
        out_shape=jax.ShapeDtypeStruct((4, r, w), g.dtype),
        in_specs=[ANY], out_specs=ANY,
        scratch_shapes=[pltpu.SemaphoreType.DMA((4,)), pltpu.SemaphoreType.DMA((4,))],
    )(g)


def _chip_partial(g, land, cidx):
    _, r, w = g.shape
    tr = _row_tile(r, 1024)
    g4 = g.reshape(4, 2, r, w)

    def body(c_ref, g_ref, l_ref, o_ref):
        o_ref[0] = (g_ref[0, 0].astype(F32) + l_ref[0].astype(F32)).astype(o_ref.dtype)

    return pl.pallas_call(
        body, name="chip_partial",
        grid_spec=pltpu.PrefetchScalarGridSpec(
            num_scalar_prefetch=1, grid=(4, r // tr),
            in_specs=[pl.BlockSpec((1, 1, tr, w), lambda i, j, s: (i, s[0], j, 0)),
                      pl.BlockSpec((1, tr, w), lambda i, j, s: (i, j, 0))],
            out_specs=pl.BlockSpec((1, tr, w), lambda i, j, s: (i, j, 0))),
        out_shape=jax.ShapeDtypeStruct((4, r, w), g.dtype),
        compiler_params=_cparams(),
    )(cidx, g4, land)


def _exchange_chips(part):
    _, r, w = part.shape

    def body(p_ref, land_ref, send_sems, recv_sems):
        x, y, c = _coords()
        copies = []
        for j, (px, py) in enumerate(_other_chips(x, y)):
            copies.append(pltpu.make_async_remote_copy(
                src_ref=p_ref.at[2 * px + py], dst_ref=land_ref.at[j],
                send_sem=send_sems.at[j], recv_sem=recv_sems.at[j],
                device_id=(px, py, c), device_id_type=MESH_ID))
        for cp in copies:
            cp.start()
        for cp in copies:
            cp.wait_recv()
        for cp in copies:
            cp.wait_send()

    return pl.pallas_call(
        body, name="exchange_chips",
        out_shape=jax.ShapeDtypeStruct((3, r, w), part.dtype),
        in_specs=[ANY], out_specs=ANY,
        scratch_shapes=[pltpu.SemaphoreType.DMA((3,)), pltpu.SemaphoreType.DMA((3,))],
    )(part)


def _adamw(w, g, m, v):
    m = ADAM_B1 * m + (1.0 - ADAM_B1) * g
    v = ADAM_B2 * v + (1.0 - ADAM_B2) * (g * g)
    m_hat = m / ADAM_C1
    v_hat = v / ADAM_C2
    delta = -ADAM_LR * (m_hat / (jnp.sqrt(v_hat) + ADAM_EPS) + ADAM_WD * w)
    return delta, m, v


def _adam_big(part, land, w, m, v, chip_idx):
    r, wd = w.shape
    tr, tw = _row_tile(r, 1024), 256

    def body(s_ref, p_ref, l_ref, w_ref, m_ref, v_ref, g_ref, d_ref, mo_ref, vo_ref):
        g = p_ref[0].astype(F32)
        for j in range(3):
            g = g + l_ref[j].astype(F32)
        delta, mn, vn = _adamw(w_ref[...], g, m_ref[...], v_ref[...])
        g_ref[...] = g
        d_ref[...] = delta
        mo_ref[...] = mn
        vo_ref[...] = vn

    row = pl.BlockSpec((tr, tw), lambda i, j, s: (i, j))
    return pl.pallas_call(
        body, name="adam_big",
        grid_spec=pltpu.PrefetchScalarGridSpec(
            num_scalar_prefetch=1, grid=(r // tr, wd // tw),
            in_specs=[pl.BlockSpec((1, tr, tw), lambda i, j, s: (s[0], i, j)),
                      pl.BlockSpec((3, tr, tw), lambda i, j, s: (0, i, j)), row, row, row],
            out_specs=[row, row, row, row]),
        out_shape=[jax.ShapeDtypeStruct((r, wd), F32)] * 4,
        compiler_params=_cparams(),
    )(chip_idx, part, land, w, m, v)


def _adam_small(parts, w, m, v):
    _, r, wd = parts.shape

    def body(p_ref, w_ref, m_ref, v_ref, g_ref, d_ref, mo_ref, vo_ref):
        g = p_ref[0]
        for k in range(1, N_DEV):
            g = g + p_ref[k]
        delta, mn, vn = _adamw(w_ref[...], g, m_ref[...], v_ref[...])
        g_ref[...] = g
        d_ref[...] = delta
        mo_ref[...] = mn
        vo_ref[...] = vn

    return pl.pallas_call(
        body, name="adam_small",
        out_shape=[jax.ShapeDtypeStruct((r, wd), F32)] * 4,
        compiler_params=_cparams(),
    )(parts, w, m, v)


def _pad_rows(a, rows):
    return jnp.pad(a, ((0, rows - a.shape[0]), (0, 0)))


def _pad_w_in(w):
    cut = Q_RANK + KV_RANK + ROPE
    return jnp.concatenate([w[:, :cut], jnp.zeros((w.shape[0], 64), w.dtype), w[:, cut:]], axis=1)


def _unpad_w_in(w):
    cut = Q_RANK + KV_RANK + ROPE
    return jnp.concatenate([w[:, :cut], w[:, cut + 64:]], axis=1)


def _pack_big(p):
    parts = [p["ffn1_w_gate"][0].T, p["ffn1_w_up"][0].T, p["ffn1_w_down"][0],
             p["ffn2_w_gate"][0].T, p["ffn2_w_up"][0].T, p["ffn2_w_down"][0],
             _pad_w_in(p["w_in"][0]), p["w_out"][0], p["w_mq"][0], p["w_mo"][0],
             p["w_mkv"][0].reshape(256, D_MODEL),
             _pad_rows(p["w_q_up"][0].T.reshape(24, D_MODEL), 32),
             p["w_kv_up"][0].reshape(16, D_MODEL)]
    return jnp.concatenate(parts, axis=0)


def _unpack_big(a):
    seg = lambda n: a[SEG_OFF[n][0]:SEG_OFF[n][0] + SEG_OFF[n][1]]
    return {
        "ffn1_w_gate": seg("ffn1_g").T[None], "ffn1_w_up": seg("ffn1_u").T[None], "ffn1_w_down": seg("ffn1_d")[None],
        "ffn2_w_gate": seg("ffn2_g").T[None], "ffn2_w_up": seg("ffn2_u").T[None], "ffn2_w_down": seg("ffn2_d")[None],
        "w_in": _unpad_w_in(seg("w_in"))[None], "w_out": seg("w_out")[None], "w_mq": seg("w_mq")[None],
        "w_mo": seg("w_mo")[None], "w_mkv": seg("w_mkv").reshape(D_MODEL, 256)[None],
        "w_q_up": seg("w_q")[:24].reshape(96, Q_RANK).T[None],
        "w_kv_up": seg("w_kv").reshape(KV_RANK, 128)[None],
    }


def _unpack_gathered(full):
    seg = lambda n: full[:, SEG_OFF[n][0]:SEG_OFF[n][0] + SEG_OFF[n][1]]
    rows = lambda n: seg(n).reshape(-1, D_MODEL)
    wq_t = seg("w_q")[:, :24].reshape(MLA_HEADS, NOPE + ROPE, Q_RANK)
    wq_t = jnp.pad(wq_t, ((0, 0), (0, HEAD_PAD - NOPE - ROPE), (0, 0))).reshape(MLA_HEADS * HEAD_PAD, Q_RANK)
    wkv = seg("w_kv").reshape(N_DEV, KV_RANK, 128).transpose(1, 0, 2).reshape(KV_RANK, N_DEV * 128)
    return {
        "ffn1_g": rows("ffn1_g"), "ffn1_u": rows("ffn1_u"), "ffn1_d": rows("ffn1_d"),
        "ffn2_g": rows("ffn2_g"), "ffn2_u": rows("ffn2_u"), "ffn2_d": rows("ffn2_d"),
        "w_in": rows("w_in"), "w_out": rows("w_out"), "w_mq": rows("w_mq"), "w_mo": rows("w_mo"),
        "w_mkv": seg("w_mkv").reshape(N_DEV, D_MODEL, 256), "w_q": wq_t, "w_kv": wkv,
    }


def _pack_grads(gr):
    blk = lambda a: a.reshape(N_DEV, -1, D_MODEL)
    dwq = gr["w_q"].reshape(MLA_HEADS, HEAD_PAD, Q_RANK)[:, :NOPE + ROPE].reshape(N_DEV, 24, D_MODEL)
    dwq = jnp.pad(dwq, ((0, 0), (0, 8), (0, 0)))
    dwkv = gr["w_kv"].reshape(KV_RANK, N_DEV, 128).transpose(1, 0, 2).reshape(N_DEV, 16, D_MODEL)
    parts = [blk(gr["ffn1_g"]), blk(gr["ffn1_u"]), blk(gr["ffn1_d"]), blk(gr["ffn2_g"]), blk(gr["ffn2_u"]),
             blk(gr["ffn2_d"]), blk(gr["w_in"]), blk(gr["w_out"]), blk(gr["w_mq"]), blk(gr["w_mo"]),
             gr["w_mkv"].reshape(N_DEV, 256, D_MODEL), dwq, dwkv]
    return jnp.concatenate([a.astype(BF16) for a in parts], axis=1)


def _pack_small(vals):
    parts = []
    for n, r in SMALL_ROWS:
        parts.append(_pad_rows(vals[n].reshape(-1, 128), r) if n in vals else jnp.zeros((r, 128), F32))
    return jnp.concatenate(parts, axis=0)


def _unpack_small(a, shapes):
    out = {}
    for n, shape in shapes.items():
        o = SMALL_OFF[n][0]
        out[n] = a[o:o + int(np.prod(shape)) // 128].reshape(shape)
    return out


BIG_NAMES = ("ffn1_w_gate", "ffn1_w_up", "ffn1_w_down", "w_in", "w_q_up", "w_kv_up", "w_out", "w_mq", "w_mkv",
             "w_mo", "ffn2_w_gate", "ffn2_w_up", "ffn2_w_down")
SMALL_NAMES = ("ffn1_norm", "mix_norm", "q_norm", "kv_norm", "pool_w", "pool_scale", "xattn_norm", "mem_norm",
               "ffn2_norm", "final_norm")
WEIGHT_ORDER = ("ffn1_norm", "ffn1_w_gate", "ffn1_w_up", "ffn1_w_down", "mix_norm", "w_in", "q_norm", "w_q_up",
                "kv_norm", "w_kv_up", "pool_w", "pool_scale", "w_out", "xattn_norm", "mem_norm", "w_mq", "w_mkv",
                "w_mo", "ffn2_norm", "ffn2_w_gate", "ffn2_w_up", "ffn2_w_down", "final_norm")


def _rope_table():
    lane = np.arange(128)
    freqs = (1.0 / (ROPE_BASE ** (np.arange(0, ROPE, 2, dtype=np.float32) / ROPE))).astype(np.float32)
    tab = np.zeros((8, 128), np.float32)
    tab[0] = np.where(lane < ROPE, freqs[lane % (ROPE // 2)], 0.0)
    tab[1] = np.where(lane < ROPE // 2, -1.0, np.where(lane < ROPE, 1.0, 0.0))
    return jnp.asarray(tab)


def kernel(x, mem, positions, ffn1_norm, ffn1_w_gate, ffn1_w_up, ffn1_w_down, mix_norm, w_in, q_norm, w_q_up, kv_norm, w_kv_up, pool_w, pool_scale, w_out, xattn_norm, mem_norm, w_mq, w_mkv, w_mo, ffn2_norm, ffn2_w_gate, ffn2_w_up, ffn2_w_down, final_norm, loss_target, m_ffn1_norm, m_ffn1_w_gate, m_ffn1_w_up, m_ffn1_w_down, m_mix_norm, m_w_in, m_q_norm, m_w_q_up, m_kv_norm, m_w_kv_up, m_pool_w, m_pool_scale, m_w_out, m_xattn_norm, m_mem_norm, m_w_mq, m_w_mkv, m_w_mo, m_ffn2_norm, m_ffn2_w_gate, m_ffn2_w_up, m_ffn2_w_down, m_final_norm, v_ffn1_norm, v_ffn1_w_gate, v_ffn1_w_up, v_ffn1_w_down, v_mix_norm, v_w_in, v_q_norm, v_w_q_up, v_kv_norm, v_w_kv_up, v_pool_w, v_pool_scale, v_w_out, v_xattn_norm, v_mem_norm, v_w_mq, v_w_mkv, v_w_mo, v_ffn2_norm, v_ffn2_w_gate, v_ffn2_w_up, v_ffn2_w_down, v_final_norm):
    wts = dict(ffn1_norm=ffn1_norm, ffn1_w_gate=ffn1_w_gate, ffn1_w_up=ffn1_w_up, ffn1_w_down=ffn1_w_down,
               mix_norm=mix_norm, w_in=w_in, q_norm=q_norm, w_q_up=w_q_up, kv_norm=kv_norm, w_kv_up=w_kv_up,
               pool_w=pool_w, pool_scale=pool_scale, w_out=w_out, xattn_norm=xattn_norm, mem_norm=mem_norm,
               w_mq=w_mq, w_mkv=w_mkv, w_mo=w_mo, ffn2_norm=ffn2_norm, ffn2_w_gate=ffn2_w_gate,
               ffn2_w_up=ffn2_w_up, ffn2_w_down=ffn2_w_down, final_norm=final_norm)
    mom = dict(ffn1_norm=m_ffn1_norm, ffn1_w_gate=m_ffn1_w_gate, ffn1_w_up=m_ffn1_w_up, ffn1_w_down=m_ffn1_w_down,
               mix_norm=m_mix_norm, w_in=m_w_in, q_norm=m_q_norm, w_q_up=m_w_q_up, kv_norm=m_kv_norm,
               w_kv_up=m_w_kv_up, pool_w=m_pool_w, pool_scale=m_pool_scale, w_out=m_w_out, xattn_norm=m_xattn_norm,
               mem_norm=m_mem_norm, w_mq=m_w_mq, w_mkv=m_w_mkv, w_mo=m_w_mo, ffn2_norm=m_ffn2_norm,
               ffn2_w_gate=m_ffn2_w_gate, ffn2_w_up=m_ffn2_w_up, ffn2_w_down=m_ffn2_w_down, final_norm=m_final_norm)
    var = dict(ffn1_norm=v_ffn1_norm, ffn1_w_gate=v_ffn1_w_gate, ffn1_w_up=v_ffn1_w_up, ffn1_w_down=v_ffn1_w_down,
               mix_norm=v_mix_norm, w_in=v_w_in, q_norm=v_q_norm, w_q_up=v_w_q_up, kv_norm=v_kv_norm,
               w_kv_up=v_w_kv_up, pool_w=v_pool_w, pool_scale=v_pool_scale, w_out=v_w_out, xattn_norm=v_xattn_norm,
               mem_norm=v_mem_norm, w_mq=v_w_mq, w_mkv=v_w_mkv, w_mo=v_w_mo, ffn2_norm=v_ffn2_norm,
               ffn2_w_gate=v_ffn2_w_gate, ffn2_w_up=v_ffn2_w_up, ffn2_w_down=v_ffn2_w_down, final_norm=v_final_norm)

    t = x.shape[1]
    xs = x[0]
    mems = mem[0]
    target = loss_target[0]
    pos = positions.reshape(t, 1)
    row = lambda a: a.reshape(1, -1)
    rope_tab = _rope_table()

    w_pack = _pack_big(wts)
    fw = _unpack_gathered(_all_gather_rows(w_pack.astype(BF16)))
    g_ffn1, g_mix, g_q, g_kv = row(ffn1_norm), row(mix_norm), row(q_norm), row(kv_norm)
    g_x, g_mem, g_ffn2, g_fin = row(xattn_norm), row(mem_norm), row(ffn2_norm), row(final_norm)
    pool_wb = pool_w[0].astype(BF16)
    pool_sc = row(pool_scale)

    h1, n1, gate1, up1 = _ffn_fwd(xs, g_ffn1, fw["ffn1_g"], fw["ffn1_u"], fw["ffn1_d"], "ffn1_fwd")
    u, z, qn, kvn, qh, kh, vh = _mix_prep(h1, g_mix, fw["w_in"], g_q, fw["w_q"], g_kv, fw["w_kv"], pos, rope_tab)
    a, lse = _attn_fwd(qh, kh, vh)
    p = _pool_fwd(z, pool_wb, pool_sc)
    memn, km, vm = _mem_kv(mems, g_mem, fw["w_mkv"])
    h2, h3, hn, qm, om = _xattn_fwd(h1, a, p, fw["w_out"], g_x, fw["w_mq"], km, vm, fw["w_mo"])
    h4, n2, gate2, up2 = _ffn_fwd(h3, g_ffn2, fw["ffn2_g"], fw["ffn2_u"], fw["ffn2_d"], "ffn2_fwd")
    loss_part, dh4, dg_fin = _loss_head(h4, target, g_fin)

    gr = {}
    dh3, dgate2, dup2, act2, dg_ffn2 = _ffn_bwd_data(dh4, h3, g_ffn2, gate2, up2, fw["ffn2_g"], fw["ffn2_u"],
                                                     fw["ffn2_d"], "ffn2_bwd")
    gr["ffn2_g"] = _tn_matmul(dgate2, n2, "ffn2_dwg", tmm=1408)
    gr["ffn2_u"] = _tn_matmul(dup2, n2, "ffn2_dwu", tmm=1408)
    gr["ffn2_d"] = _tn_matmul(act2, dh4, "ffn2_dwd", scale=0.5, tmm=1408)
    dh2, dqm, da, dp, dkm, dvm, dg_x = _xattn_bwd(dh3, h2, qm, g_x, fw["w_mq"], km, vm, fw["w_mo"], fw["w_out"])
    gr["w_mo"] = _tn_matmul(om, dh3, "dw_mo", tmm=512)
    gr["w_mq"] = _tn_matmul(hn, dqm, "dw_mq", tmm=512)
    gr["w_out"] = jnp.concatenate([_tn_matmul(a, dh2, "dw_out_a"), _tn_matmul(p, dh2, "dw_out_p")], axis=0)
    gr["w_mkv"], dg_mem = _mem_kv_bwd(dkm, dvm, memn, mems, g_mem, fw["w_mkv"])
    dz_pool, d_pool_w, d_pool_sc = _pool_bwd(dp, z, pool_wb, pool_sc)
    dqh, dkh, dvh = _attn_bwd(qh, kh, vh, da, lse, _attn_delta(a, da))
    dh1, dq, dkv, dz, dg_q, dg_kv, dg_mix = _mla_bwd(dqh, dkh, dvh, z, dz_pool, h1, dh2, g_mix, fw["w_in"], g_q,
                                                     fw["w_q"], g_kv, fw["w_kv"], pos, rope_tab)
    gr["w_q"] = _tn_matmul(dq, qn, "dw_q", tmm=512)
    gr["w_kv"] = _tn_matmul(kvn, dkv, "dw_kv")
    gr["w_in"] = _tn_matmul(u, dz, "dw_in", tmm=512)
    dx, dgate1, dup1, act1, dg_ffn1 = _ffn_bwd_data(dh1, xs, g_ffn1, gate1, up1, fw["ffn1_g"], fw["ffn1_u"],
                                                    fw["ffn1_d"], "ffn1_bwd")
    gr["ffn1_g"] = _tn_matmul(dgate1, n1, "ffn1_dwg", tmm=1408)
    gr["ffn1_u"] = _tn_matmul(dup1, n1, "ffn1_dwu", tmm=1408)
    gr["ffn1_d"] = _tn_matmul(act1, dh1, "ffn1_dwd", scale=0.5, tmm=1408)

    cx, cy, cc = _coords()
    g_pack = _pack_grads(gr)
    land_core = _exchange_core(g_pack)
    part = _chip_partial(g_pack, land_core, cc.astype(jnp.int32).reshape(1))
    land_chip = _exchange_chips(part)
    big = _adam_big(part, land_chip, w_pack, _pack_big(mom), _pack_big(var),
                    (2 * cx + cy).astype(jnp.int32).reshape(1))
    big = [_unpack_big(b) for b in big]

    small_g = dict(ffn1_norm=dg_ffn1, mix_norm=dg_mix, q_norm=dg_q, kv_norm=dg_kv, pool_w=d_pool_w,
                   pool_scale=d_pool_sc, xattn_norm=dg_x, mem_norm=dg_mem, ffn2_norm=dg_ffn2, final_norm=dg_fin,
                   loss=loss_part)
    parts = _all_gather_direct(_pack_small(small_g))
    small = _adam_small(parts, _pack_small({n: wts[n] for n in SMALL_NAMES}),
                        _pack_small({n: mom[n] for n in SMALL_NAMES}), _pack_small({n: var[n] for n in SMALL_NAMES}))
    loss = small[0][SMALL_OFF["loss"][0], 0]
    shapes = {n: wts[n].shape for n in SMALL_NAMES}
    small = [_unpack_small(s, shapes) for s in small]

    outs = [loss, dx[None]]
    for k in range(4):
        for n in WEIGHT_ORDER:
            outs.append(big[k][n] if n in BIG_NAMES else small[k][n])
    return tuple(outs)
```

```python
import numpy as np

import jax
import jax.numpy as jnp
from jax import lax
from jax.experimental import pallas as pl
from jax.experimental.pallas import tpu as pltpu

F32 = jnp.float32
BF16 = jnp.bfloat16

N_DEV = 8
D_MODEL = 1024
D_FF = 2816
MLA_HEADS = 4
NOPE = 128
ROPE = 64
HEAD_PAD = 256
V_DIM = 128
Q_RANK = 256
KV_RANK = 128
POOL_WINDOWS = (2, 4, 8, 16)
POOL_CH = 128
POOL_HALO = 16
N_MEM = 256
MEM_HEADS = 4
MEM_HD = 256
ROPE_BASE = 10000.0
RMS_EPS = 1e-6
ATTN_SCALE = (NOPE + ROPE) ** -0.5
MEM_SCALE = MEM_HD ** -0.5
NEG_BIG = -1e30

ADAM_LR = 0.001
ADAM_B1 = 0.9
ADAM_B2 = 0.999
ADAM_EPS = 1e-08
ADAM_WD = 0.01
ADAM_STEP = 10
ADAM_C1 = 1.0 - ADAM_B1 ** ADAM_STEP
ADAM_C2 = 1.0 - ADAM_B2 ** ADAM_STEP

VMEM_LIMIT_BYTES = 52 * 1024 * 1024
BF16_ROWS = 16

GROUP_SEGS = {
    "ffn1": (("ffn1_g", 352), ("ffn1_u", 352), ("ffn1_d", 352)),
    "mid": (("w_in", 128), ("w_out", 128), ("w_mq", 128), ("w_mo", 128), ("w_mkv", 256), ("w_q", 32), ("w_kv", 16)),
    "ffn2": (("ffn2_g", 352), ("ffn2_u", 352), ("ffn2_d", 352)),
}
SEG_OFF = {}
GROUP_ROWS = {}
for _g, _segs in GROUP_SEGS.items():
    _o = 0
    for _n, _r in _segs:
        SEG_OFF[_n] = (_o, _r)
        _o += _r
    GROUP_ROWS[_g] = _o

SMALL_ROWS = (("ffn1_norm", 8), ("mix_norm", 8), ("q_norm", 8), ("kv_norm", 8), ("pool_w", 512), ("pool_scale", 8),
              ("xattn_norm", 8), ("mem_norm", 8), ("ffn2_norm", 8), ("final_norm", 8), ("loss", 8))
SMALL_OFF = {}
_o = 0
for _n, _r in SMALL_ROWS:
    SMALL_OFF[_n] = (_o, _r)
    _o += _r


def _cparams(**kw):
    return pltpu.CompilerParams(vmem_limit_bytes=VMEM_LIMIT_BYTES, **kw)


def _row_tile(rows, limit):
    best = None
    for cand in range(BF16_ROWS, min(rows, limit) + 1, BF16_ROWS):
        if rows % cand == 0:
            best = cand
    assert best is not None, rows
    return best


def _dot_nn(a, b):
    return lax.dot_general(a, b, (((1,), (0,)), ((), ())), preferred_element_type=F32)


def _dot_nt(a, b):
    return lax.dot_general(a, b, (((1,), (1,)), ((), ())), preferred_element_type=F32)


def _dot_tn(a, b):
    return lax.dot_general(a, b, (((0,), (0,)), ((), ())), preferred_element_type=F32)


def _rms_fwd(x, g):
    r = lax.rsqrt(jnp.mean(x * x, axis=-1, keepdims=True) + RMS_EPS)
    return x * r * g, r


def _rms_bwd(dy, x, g, r):
    xhat = x * r
    dyg = dy * g
    dx = r * (dyg - xhat * jnp.mean(dyg * xhat, axis=-1, keepdims=True))
    dg = jnp.sum(dy * xhat, axis=0, keepdims=True)
    return dx, dg


def _accumulate(ref, val, first):
    if isinstance(first, bool):
        if first:
            ref[...] = val
        else:
            ref[...] += val
        return

    @pl.when(first)
    def _():
        ref[...] = val

    @pl.when(jnp.logical_not(first))
    def _():
        ref[...] += val


def _call_after(token, body, in_specs, args, **kw):
    if token is not None:
        inner = body
        body = lambda tok_ref, *refs: inner(*refs)
        in_specs = [pl.BlockSpec((8, 128), lambda *_: (0, 0))] + list(in_specs)
        args = (token,) + tuple(args)
    return pl.pallas_call(body, in_specs=in_specs, **kw)(*args)


def _rope_tables(pos_col, tab):
    ang = pos_col.astype(F32) * tab[0:1, :]
    return jnp.cos(ang), jnp.sin(ang) * tab[1:2, :]


def _swap_halves(x):
    lane = lax.broadcasted_iota(jnp.int32, x.shape, 1)
    return jnp.where((lane % 64) < 32, pltpu.roll(x, 96, 1), pltpu.roll(x, 32, 1))


def _rope_apply(x, cos_t, sin_t):
    return x * cos_t + _swap_halves(x) * sin_t


def _rope_apply_t(dy, cos_t, sin_t):
    return dy * cos_t + _swap_halves(dy * sin_t)


def _ffn_fwd(h, g, wg_t, wu_t, wd, name, token=None):
    t, d = h.shape
    f = wg_t.shape[0]
    tm, tf = min(1024, t), 256
    nf = f // tf

    def body(h_ref, g_ref, wg_ref, wu_ref, wd_ref, ho_ref, n_ref, gate_ref, up_ref, nb_sc, acc_sc):
        j = pl.program_id(1)

        @pl.when(j == 0)
        def _():
            y, _ = _rms_fwd(h_ref[...], g_ref[...])
            nb = y.astype(BF16)
            nb_sc[...] = nb
            n_ref[...] = nb
            acc_sc[...] = jnp.zeros_like(acc_sc)

        nb = nb_sc[...]
        gt = _dot_nt(nb, wg_ref[...])
        ut = _dot_nt(nb, wu_ref[...])
        gate_ref[...] = gt.astype(BF16)
        up_ref[...] = ut.astype(BF16)
        act = (gt * jax.nn.sigmoid(gt)) * ut
        acc_sc[...] += _dot_nn(act.astype(BF16), wd_ref[...])

        @pl.when(j == nf - 1)
        def _():
            ho_ref[...] = h_ref[...] + 0.5 * acc_sc[...]

    return _call_after(
        token, body,
        [pl.BlockSpec((tm, d), lambda i, j: (i, 0)),
         pl.BlockSpec((1, d), lambda i, j: (0, 0)),
         pl.BlockSpec((tf, d), lambda i, j: (j, 0)),
         pl.BlockSpec((tf, d), lambda i, j: (j, 0)),
         pl.BlockSpec((tf, d), lambda i, j: (j, 0))],
        (h, g, wg_t, wu_t, wd),
        name=name, grid=(t // tm, nf),
        out_specs=[pl.BlockSpec((tm, d), lambda i, j: (i, 0)),
                   pl.BlockSpec((tm, d), lambda i, j: (i, 0)),
                   pl.BlockSpec((tm, tf), lambda i, j: (i, j)),
                   pl.BlockSpec((tm, tf), lambda i, j: (i, j))],
        out_shape=[jax.ShapeDtypeStruct((t, d), F32), jax.ShapeDtypeStruct((t, d), BF16),
                   jax.ShapeDtypeStruct((t, f), BF16), jax.ShapeDtypeStruct((t, f), BF16)],
        scratch_shapes=[pltpu.VMEM((tm, d), BF16), pltpu.VMEM((tm, d), F32)],
        compiler_params=_cparams(),
    )


def _ffn_bwd_data(dho, h, g, gate, up, wg_t, wu_t, wd, name, token=None):
    t, d = h.shape
    f = wg_t.shape[0]
    tm, tf = min(1024, t), 256
    nf = f // tf

    def body(dho_ref, h_ref, g_ref, gate_ref, up_ref, wg_ref, wu_ref, wd_ref,
             dh_ref, dgate_ref, dup_ref, act_ref, dg_ref, dhb_sc, acc_sc):
        i, j = pl.program_id(0), pl.program_id(1)

        @pl.when(j == 0)
        def _():
            dhb_sc[...] = (0.5 * dho_ref[...]).astype(BF16)
            acc_sc[...] = jnp.zeros_like(acc_sc)

        dact = _dot_nt(dhb_sc[...], wd_ref[...])
        gt = gate_ref[...].astype(F32)
        ut = up_ref[...].astype(F32)
        sg = jax.nn.sigmoid(gt)
        silu = gt * sg
        dgb = (dact * ut * (sg * (1.0 + gt * (1.0 - sg)))).astype(BF16)
        dub = (dact * silu).astype(BF16)
        act_ref[...] = (silu * ut).astype(BF16)
        dgate_ref[...] = dgb
        dup_ref[...] = dub
        acc_sc[...] += _dot_nn(dgb, wg_ref[...]) + _dot_nn(dub, wu_ref[...])

        @pl.when(j == nf - 1)
        def _():
            x = h_ref[...]
            gg = g_ref[...]
            _, r = _rms_fwd(x, gg)
            dx, dg = _rms_bwd(acc_sc[...], x, gg, r)
            dh_ref[...] = dho_ref[...] + dx
            _accumulate(dg_ref, dg, i == 0)

    return _call_after(
        token, body,
        [pl.BlockSpec((tm, d), lambda i, j: (i, 0)),
         pl.BlockSpec((tm, d), lambda i, j: (i, 0)),
         pl.BlockSpec((1, d), lambda i, j: (0, 0)),
         pl.BlockSpec((tm, tf), lambda i, j: (i, j)),
         pl.BlockSpec((tm, tf), lambda i, j: (i, j)),
         pl.BlockSpec((tf, d), lambda i, j: (j, 0)),
         pl.BlockSpec((tf, d), lambda i, j: (j, 0)),
         pl.BlockSpec((tf, d), lambda i, j: (j, 0))],
        (dho, h, g, gate, up, wg_t, wu_t, wd),
        name=name, grid=(t // tm, nf),
        out_specs=[pl.BlockSpec((tm, d), lambda i, j: (i, 0)),
                   pl.BlockSpec((tm, tf), lambda i, j: (i, j)),
                   pl.BlockSpec((tm, tf), lambda i, j: (i, j)),
                   pl.BlockSpec((tm, tf), lambda i, j: (i, j)),
                   pl.BlockSpec((1, d), lambda i, j: (0, 0))],
        out_shape=[jax.ShapeDtypeStruct((t, d), F32), jax.ShapeDtypeStruct((t, f), BF16),
                   jax.ShapeDtypeStruct((t, f), BF16), jax.ShapeDtypeStruct((t, f), BF16),
                   jax.ShapeDtypeStruct((1, d), F32)],
        scratch_shapes=[pltpu.VMEM((tm, d), BF16), pltpu.VMEM((tm, d), F32)],
        compiler_params=_cparams(),
    )


def _tn_matmul(a, b, name, scale=1.0, tmm=None):
    t, m = a.shape
    n = b.shape[1]
    tmm = m if tmm is None else tmm
    tk = min(1024, t)

    def body(a_ref, b_ref, o_ref):
        k = pl.program_id(1)
        prod = _dot_tn(a_ref[...].astype(BF16), b_ref[...].astype(BF16))
        if scale != 1.0:
            prod = prod * scale
        _accumulate(o_ref, prod, k == 0)

    return pl.pallas_call(
        body, name=name, grid=(m // tmm, t // tk),
        in_specs=[pl.BlockSpec((tk, tmm), lambda i, k: (k, i)),
                  pl.BlockSpec((tk, n), lambda i, k: (k, 0))],
        out_specs=pl.BlockSpec((tmm, n), lambda i, k: (i, 0)),
        out_shape=jax.ShapeDtypeStruct((m, n), F32),
        compiler_params=_cparams(),
    )(a, b)


def _loss_head(h, target, g):
    t, d = h.shape
    tm = min(512, t)

    def body(h_ref, t_ref, g_ref, loss_ref, dh_ref, dg_ref):
        i = pl.program_id(0)
        x = h_ref[...]
        gg = g_ref[...]
        y, r = _rms_fwd(x, gg)
        err = y - t_ref[...]
        part = 0.5 * jnp.sum(jnp.mean(err * err, axis=-1, keepdims=True), axis=0, keepdims=True)
        dx, dg = _rms_bwd(err * (1.0 / d), x, gg, r)
        dh_ref[...] = dx
        _accumulate(loss_ref, jnp.broadcast_to(part, loss_ref.shape), i == 0)
        _accumulate(dg_ref, dg, i == 0)

    return pl.pallas_call(
        body, name="loss_head", grid=(t // tm,),
        in_specs=[pl.BlockSpec((tm, d), lambda i: (i, 0)),
                  pl.BlockSpec((tm, d), lambda i: (i, 0)),
                  pl.BlockSpec((1, d), lambda i: (0, 0))],
        out_specs=[pl.BlockSpec((8, 128), lambda i: (0, 0)),
                   pl.BlockSpec((tm, d), lambda i: (i, 0)),
                   pl.BlockSpec((1, d), lambda i: (0, 0))],
        out_shape=[jax.ShapeDtypeStruct((8, 128), F32), jax.ShapeDtypeStruct((t, d), F32),
                   jax.ShapeDtypeStruct((1, d), F32)],
        compiler_params=_cparams(),
    )(h, target, g)


def _mix_prep(h1, mix_norm, w_in, q_norm, wq_t, kv_norm, wkv, pos, rope_tab, token=None):
    t, d = h1.shape
    tm = min(512, t)

    def body(h_ref, gm_ref, win_ref, gq_ref, wq_ref, gkv_ref, wkv_ref, pos_ref, tab_ref,
             u_ref, z_ref, qn_ref, kvn_ref, q_ref, k_ref, v_ref):
        u, _ = _rms_fwd(h_ref[...], gm_ref[...])
        ub = u.astype(BF16)
        u_ref[...] = ub
        z = _dot_nn(ub, win_ref[...])
        z_ref[...] = z
        cos_t, sin_t = _rope_tables(pos_ref[...], tab_ref[...])
        qn, _ = _rms_fwd(z[:, 0:Q_RANK], gq_ref[...])
        qnb = qn.astype(BF16)
        qn_ref[...] = qnb
        q = _dot_nt(qnb, wq_ref[...])
        kvn, _ = _rms_fwd(z[:, Q_RANK:Q_RANK + KV_RANK], gkv_ref[...])
        kvnb = kvn.astype(BF16)
        kvn_ref[...] = kvnb
        kv = _dot_nn(kvnb, wkv_ref[...])
        k_pe = _rope_apply(z[:, Q_RANK + KV_RANK:Q_RANK + KV_RANK + 128], cos_t, sin_t)
        ones = jnp.ones((tm, V_DIM), F32)
        for hh in range(MLA_HEADS):
            b = hh * HEAD_PAD
            q_pe = _rope_apply(q[:, b + NOPE:b + HEAD_PAD], cos_t, sin_t)
            q_ref[hh] = jnp.concatenate([q[:, b:b + NOPE], q_pe], axis=-1).astype(BF16)
            k_ref[hh] = jnp.concatenate([kv[:, b:b + NOPE], k_pe], axis=-1).astype(BF16)
            v_ref[hh] = jnp.concatenate([kv[:, b + NOPE:b + HEAD_PAD], ones], axis=-1).astype(BF16)

    full = lambda shape: pl.BlockSpec(shape, lambda i: (0,) * len(shape))
    return _call_after(
        token, body,
        [pl.BlockSpec((tm, d), lambda i: (i, 0)), full((1, d)), full(w_in.shape), full((1, Q_RANK)),
         full(wq_t.shape), full((1, KV_RANK)), full(wkv.shape),
         pl.BlockSpec((tm, 1), lambda i: (i, 0)), full(rope_tab.shape)],
        (h1, mix_norm, w_in, q_norm, wq_t, kv_norm, wkv, pos, rope_tab),
        name="mix_prep", grid=(t // tm,),
        out_specs=[pl.BlockSpec((tm, d), lambda i: (i, 0)),
                   pl.BlockSpec((tm, d), lambda i: (i, 0)),
                   pl.BlockSpec((tm, Q_RANK), lambda i: (i, 0)),
                   pl.BlockSpec((tm, KV_RANK), lambda i: (i, 0)),
                   pl.BlockSpec((MLA_HEADS, tm, HEAD_PAD), lambda i: (0, i, 0)),
                   pl.BlockSpec((MLA_HEADS, tm, HEAD_PAD), lambda i: (0, i, 0)),
                   pl.BlockSpec((MLA_HEADS, tm, 2 * V_DIM), lambda i: (0, i, 0))],
        out_shape=[jax.ShapeDtypeStruct((t, d), BF16), jax.ShapeDtypeStruct((t, d), F32),
                   jax.ShapeDtypeStruct((t, Q_RANK), BF16), jax.ShapeDtypeStruct((t, KV_RANK), BF16),
                   jax.ShapeDtypeStruct((MLA_HEADS, t, HEAD_PAD), BF16),
                   jax.ShapeDtypeStruct((MLA_HEADS, t, HEAD_PAD), BF16),
                   jax.ShapeDtypeStruct((MLA_HEADS, t, 2 * V_DIM), BF16)],
        compiler_params=_cparams(),
    )


def _causal_mask(s):
    row = lax.broadcasted_iota(jnp.int32, s.shape, 0)
    col = lax.broadcasted_iota(jnp.int32, s.shape, 1)
    return jnp.where(col <= row, s, NEG_BIG)


def _attn_fwd(q, k, v):
    nh, t, _ = q.shape
    tq = tk = min(512, t)
    nq, nk = t // tq, t // tk

    def body(q_ref, k_ref, v_ref, o_ref, lse_ref, m_sc, acc_sc):
        i, j = pl.program_id(0), pl.program_id(1)

        @pl.when(j == 0)
        def _():
            m_sc[...] = jnp.full_like(m_sc, NEG_BIG)
            acc_sc[...] = jnp.zeros_like(acc_sc)

        def step(diagonal):
            for hh in range(nh):
                s = _dot_nt(q_ref[hh], k_ref[hh]) * ATTN_SCALE
                if diagonal:
                    s = _causal_mask(s)
                m_old = m_sc[hh]
                m_new = jnp.maximum(m_old, jnp.max(s, axis=-1, keepdims=True))
                p = jnp.exp(s - m_new).astype(BF16)
                acc_sc[hh] = jnp.exp(m_old - m_new) * acc_sc[hh] + _dot_nn(p, v_ref[hh])
                m_sc[hh] = m_new

        @pl.when(j < i)
        def _():
            step(False)

        @pl.when(j == i)
        def _():
            step(True)
            for hh in range(nh):
                acc = acc_sc[hh]
                l = acc[:, V_DIM:2 * V_DIM]
                o_ref[:, hh * V_DIM:(hh + 1) * V_DIM] = (acc[:, 0:V_DIM] / l).astype(BF16)
                lse_ref[hh] = m_sc[hh] + jnp.log(l[:, 0:1])

    kv_map = lambda i, j: (0, jnp.minimum(j, i), 0)
    return pl.pallas_call(
        body, name="attn_fwd", grid=(nq, nk),
        in_specs=[pl.BlockSpec((nh, tq, HEAD_PAD), lambda i, j: (0, i, 0)),
                  pl.BlockSpec((nh, tk, HEAD_PAD), kv_map),
                  pl.BlockSpec((nh, tk, 2 * V_DIM), kv_map)],
        out_specs=[pl.BlockSpec((tq, nh * V_DIM), lambda i, j: (i, 0)),
                   pl.BlockSpec((nh, tq, 1), lambda i, j: (0, i, 0))],
        out_shape=[jax.ShapeDtypeStruct((t, nh * V_DIM), BF16), jax.ShapeDtypeStruct((nh, t, 1), F32)],
        scratch_shapes=[pltpu.VMEM((nh, tq, 1), F32), pltpu.VMEM((nh, tq, 2 * V_DIM), F32)],
        compiler_params=_cparams(),
    )(q, k, v)


def _attn_delta(o, do):
    t, w = o.shape
    nh = w // V_DIM
    tm = min(512, t)

    def body(o_ref, do_ref, d_ref):
        prod = o_ref[...].astype(F32) * do_ref[...].astype(F32)
        for hh in range(nh):
            d_ref[hh] = jnp.sum(prod[:, hh * V_DIM:(hh + 1) * V_DIM], axis=-1, keepdims=True)

    return pl.pallas_call(
        body, name="attn_delta", grid=(t // tm,),
        in_specs=[pl.BlockSpec((tm, w), lambda i: (i, 0)), pl.BlockSpec((tm, w), lambda i: (i, 0))],
        out_specs=pl.BlockSpec((nh, tm, 1), lambda i: (0, i, 0)),
        out_shape=jax.ShapeDtypeStruct((nh, t, 1), F32),
        compiler_params=_cparams(),
    )(o, do)


ATTN_BWD_HEADS = 2


def _attn_bwd(q, k, v, do, lse, delta):
    nh, t, _ = q.shape
    hp = ATTN_BWD_HEADS
    tq = tk = min(512, t)
    nq, nk = t // tq, t // tk

    def body(q_ref, k_ref, v_ref, do_ref, lse_ref, dlt_ref, dq_ref, dk_ref, dv_ref):
        j, i = pl.program_id(1), pl.program_id(2)

        def step(diagonal):
            for hh in range(hp):
                qq, kk = q_ref[hh], k_ref[hh]
                dob = do_ref[:, hh * V_DIM:(hh + 1) * V_DIM]
                s = _dot_nt(qq, kk) * ATTN_SCALE
                if diagonal:
                    s = _causal_mask(s)
                p = jnp.exp(s - lse_ref[hh])
                dpp = _dot_nt(dob, v_ref[hh])
                dsb = (p * (dpp - dlt_ref[hh]) * ATTN_SCALE).astype(BF16)
                _accumulate(dv_ref.at[hh], _dot_tn(p.astype(BF16), dob), diagonal)
                _accumulate(dk_ref.at[hh], _dot_tn(dsb, qq), diagonal)
                _accumulate(dq_ref.at[hh, pl.ds(pl.multiple_of(i * tq, tq), tq), :], _dot_nn(dsb, kk), j == 0)

        @pl.when(i > j)
        def _():
            step(False)

        @pl.when(i == j)
        def _():
            step(True)

    qmap = lambda h, j, i: (h, jnp.maximum(i, j), 0)
    return pl.pallas_call(
        body, name="attn_bwd", grid=(nh // hp, nk, nq),
        in_specs=[pl.BlockSpec((hp, tq, HEAD_PAD), qmap),
                  pl.BlockSpec((hp, tk, HEAD_PAD), lambda h, j, i: (h, j, 0)),
                  pl.BlockSpec((hp, tk, V_DIM), lambda h, j, i: (h, j, 0)),
                  pl.BlockSpec((tq, hp * V_DIM), lambda h, j, i: (jnp.maximum(i, j), h)),
                  pl.BlockSpec((hp, tq, 1), qmap),
                  pl.BlockSpec((hp, tq, 1), qmap)],
        out_specs=[pl.BlockSpec((hp, t, HEAD_PAD), lambda h, j, i: (h, 0, 0)),
                   pl.BlockSpec((hp, tk, HEAD_PAD), lambda h, j, i: (h, j, 0)),
                   pl.BlockSpec((hp, tk, V_DIM), lambda h, j, i: (h, j, 0))],
        out_shape=[jax.ShapeDtypeStruct((nh, t, HEAD_PAD), F32), jax.ShapeDtypeStruct((nh, t, HEAD_PAD), F32),
                   jax.ShapeDtypeStruct((nh, t, V_DIM), F32)],
        compiler_params=_cparams(),
    )(q, k, v, do, lse, delta)


def _pool_counts(first_token, rows, w):
    tok = lax.broadcasted_iota(jnp.int32, (rows, POOL_CH), 0) + first_token
    return jnp.minimum(tok + 1, w).astype(F32)


def _pool_centered(zbuf, g, w, i, tm):
    lanes = pl.ds(g * POOL_CH, POOL_CH)
    cur = zbuf[pl.ds(POOL_HALO, tm), lanes]
    win = cur
    for s in range(1, w):
        win = win + zbuf[pl.ds(POOL_HALO - s, tm), lanes]
    return win / _pool_counts(i * tm, tm, w) - cur


def _pool_load(zbuf, z_ref, halo_ref, i, tm):
    @pl.when(i == 0)
    def _():
        zbuf[pl.ds(0, POOL_HALO), :] = jnp.zeros((POOL_HALO, zbuf.shape[1]), F32)

    @pl.when(i > 0)
    def _():
        zbuf[pl.ds(0, POOL_HALO), :] = halo_ref[...]

    zbuf[pl.ds(POOL_HALO, tm), :] = z_ref[...]


def _pool_fwd(z, pool_w, pool_scale):
    t = z.shape[0]
    pw = len(POOL_WINDOWS) * POOL_CH
    tm = min(512, t)
    hb = tm // POOL_HALO

    def body(z_ref, halo_ref, w_ref, sc_ref, p_ref, zbuf):
        i = pl.program_id(0)
        _pool_load(zbuf, z_ref, halo_ref, i, tm)
        for g, w in enumerate(POOL_WINDOWS):
            c = _pool_centered(zbuf, g, w, i, tm)
            y = _dot_nn(c.astype(BF16), w_ref[g]) * sc_ref[:, g * POOL_CH:(g + 1) * POOL_CH]
            p_ref[:, g * POOL_CH:(g + 1) * POOL_CH] = y.astype(BF16)

    return pl.pallas_call(
        body, name="pool_fwd", grid=(t // tm,),
        in_specs=[pl.BlockSpec((tm, pw), lambda i: (i, 1)),
                  pl.BlockSpec((POOL_HALO, pw), lambda i: (jnp.maximum(i * hb - 1, 0), 1)),
                  pl.BlockSpec(pool_w.shape, lambda i: (0, 0, 0)),
                  pl.BlockSpec((1, pw), lambda i: (0, 0))],
        out_specs=pl.BlockSpec((tm, pw), lambda i: (i, 0)),
        out_shape=jax.ShapeDtypeStruct((t, pw), BF16),
        scratch_shapes=[pltpu.VMEM((POOL_HALO + tm, pw), F32)],
        compiler_params=_cparams(),
    )(z, z, pool_w, pool_scale)


def _pool_bwd(dp, z, pool_w, pool_scale):
    t = z.shape[0]
    ng = len(POOL_WINDOWS)
    pw = ng * POOL_CH
    tm = min(512, t)
    hb = tm // POOL_HALO
    nt = t // tm

    def body(dp_ref, dpn_ref, z_ref, halo_ref, w_ref, sc_ref, dz_ref, dw_ref, dsc_ref, zbuf, dbuf):
        i = pl.program_id(0)
        _pool_load(zbuf, z_ref, halo_ref, i, tm)
        nxt_ok = (i < nt - 1).astype(F32)
        for g, w in enumerate(POOL_WINDOWS):
            lanes = pl.ds(g * POOL_CH, POOL_CH)
            cols = slice(g * POOL_CH, (g + 1) * POOL_CH)
            sc = sc_ref[:, cols]
            wg = w_ref[g]
            c = _pool_centered(zbuf, g, w, i, tm).astype(BF16)
            ypre = _dot_nn(c, wg)
            dpg = dp_ref[:, cols].astype(F32)
            _accumulate(dsc_ref.at[:, lanes], jnp.sum(dpg * ypre, axis=0, keepdims=True), i == 0)
            dyb = (dpg * sc).astype(BF16)
            _accumulate(dw_ref.at[g], _dot_tn(c, dyb), i == 0)
            dd = _dot_nt(dyb, wg)
            dyn = (dpn_ref[:, cols].astype(F32) * sc).astype(BF16)
            ddn = _dot_nt(dyn, wg) * nxt_ok
            dbuf[pl.ds(0, tm), lanes] = dd / _pool_counts(i * tm, tm, w)
            dbuf[pl.ds(tm, POOL_HALO), lanes] = ddn / _pool_counts((i + 1) * tm, POOL_HALO, w)
            acc = -dd
            for s in range(w):
                acc = acc + dbuf[pl.ds(s, tm), lanes]
            dz_ref[:, cols] = acc

    return pl.pallas_call(
        body, name="pool_bwd", grid=(nt,),
        in_specs=[pl.BlockSpec((tm, pw), lambda i: (i, 0)),
                  pl.BlockSpec((POOL_HALO, pw), lambda i: (jnp.minimum((i + 1) * hb, t // POOL_HALO - 1), 0)),
                  pl.BlockSpec((tm, pw), lambda i: (i, 1)),
                  pl.BlockSpec((POOL_HALO, pw), lambda i: (jnp.maximum(i * hb - 1, 0), 1)),
                  pl.BlockSpec(pool_w.shape, lambda i: (0, 0, 0)),
                  pl.BlockSpec((1, pw), lambda i: (0, 0))],
        out_specs=[pl.BlockSpec((tm, pw), lambda i: (i, 0)),
                   pl.BlockSpec((ng, POOL_CH, POOL_CH), lambda i: (0, 0, 0)),
                   pl.BlockSpec((1, pw), lambda i: (0, 0))],
        out_shape=[jax.ShapeDtypeStruct((t, pw), F32), jax.ShapeDtypeStruct((ng, POOL_CH, POOL_CH), F32),
                   jax.ShapeDtypeStruct((1, pw), F32)],
        scratch_shapes=[pltpu.VMEM((POOL_HALO + tm, pw), F32), pltpu.VMEM((tm + POOL_HALO, pw), F32)],
        compiler_params=_cparams(),
    )(dp, dp, z, z, pool_w, pool_scale)


def _mla_bwd(dq_h, dk_h, dv_h, z, dz_pool, h1, dh2, mix_norm, w_in, q_norm, wq_t, kv_norm, wkv, pos, rope_tab):
    t, d = h1.shape
    tm = min(256, t)

    def body(dqh_ref, dkh_ref, dvh_ref, z_ref, dzp_ref, h_ref, dh2_ref, gm_ref, win_ref, gq_ref, wq_ref, gkv_ref,
             wkv_ref, pos_ref, tab_ref, dh1_ref, dq_ref, dkv_ref, dz_ref, dgq_ref, dgkv_ref, dgm_ref):
        i = pl.program_id(0)
        first = i == 0
        cos_t, sin_t = _rope_tables(pos_ref[...], tab_ref[...])
        dq_parts, dkv_parts = [], []
        dk_pe = jnp.zeros((tm, 128), F32)
        for hh in range(MLA_HEADS):
            dqh = dqh_ref[hh]
            dq_parts += [dqh[:, 0:NOPE], _rope_apply_t(dqh[:, NOPE:HEAD_PAD], cos_t, sin_t)]
            dkh = dkh_ref[hh]
            dkv_parts += [dkh[:, 0:NOPE], dvh_ref[hh]]
            dk_pe = dk_pe + dkh[:, NOPE:HEAD_PAD]
        dqb = jnp.concatenate(dq_parts, axis=-1).astype(BF16)
        dkvb = jnp.concatenate(dkv_parts, axis=-1).astype(BF16)
        dq_ref[...] = dqb
        dkv_ref[...] = dkvb
        z = z_ref[...]
        c_q = z[:, 0:Q_RANK]
        gq = gq_ref[...]
        _, rq = _rms_fwd(c_q, gq)
        dcq, dgq = _rms_bwd(_dot_nn(dqb, wq_ref[...]), c_q, gq, rq)
        c_kv = z[:, Q_RANK:Q_RANK + KV_RANK]
        gkv = gkv_ref[...]
        _, rkv = _rms_fwd(c_kv, gkv)
        dckv, dgkv = _rms_bwd(_dot_nt(dkvb, wkv_ref[...]), c_kv, gkv, rkv)
        dkr = _rope_apply_t(dk_pe, cos_t, sin_t)
        dzb = jnp.concatenate([dcq, dckv, dkr, dzp_ref[...]], axis=-1).astype(BF16)
        dz_ref[...] = dzb
        x = h_ref[...]
        gm = gm_ref[...]
        _, rm = _rms_fwd(x, gm)
        dx, dgm = _rms_bwd(_dot_nt(dzb, win_ref[...]), x, gm, rm)
        dh1_ref[...] = dh2_ref[...] + dx
        _accumulate(dgq_ref, dgq, first)
        _accumulate(dgkv_ref, dgkv, first)
        _accumulate(dgm_ref, dgm, first)

    full = lambda shape: pl.BlockSpec(shape, lambda i: (0,) * len(shape))
    row = lambda w: pl.BlockSpec((tm, w), lambda i: (i, 0))
    head = lambda w: pl.BlockSpec((MLA_HEADS, tm, w), lambda i: (0, i, 0))
    pw = len(POOL_WINDOWS) * POOL_CH
    return pl.pallas_call(
        body, name="mla_bwd", grid=(t // tm,),
        in_specs=[head(HEAD_PAD), head(HEAD_PAD), head(V_DIM), row(d), row(pw), row(d), row(d),
                  full((1, d)), full(w_in.shape), full((1, Q_RANK)), full(wq_t.shape), full((1, KV_RANK)),
                  full(wkv.shape), row(1), full(rope_tab.shape)],
        out_specs=[row(d), row(d), row(d), row(d), full((1, Q_RANK)), full((1, KV_RANK)), full((1, d))],
        out_shape=[jax.ShapeDtypeStruct((t, d), F32), jax.ShapeDtypeStruct((t, d), BF16),
                   jax.ShapeDtypeStruct((t, d), BF16), jax.ShapeDtypeStruct((t, d), BF16),
                   jax.ShapeDtypeStruct((1, Q_RANK), F32), jax.ShapeDtypeStruct((1, KV_RANK), F32),
                   jax.ShapeDtypeStruct((1, d), F32)],
        compiler_params=_cparams(),
    )(dq_h, dk_h, dv_h, z, dz_pool, h1, dh2, mix_norm, w_in, q_norm, wq_t, kv_norm, wkv, pos, rope_tab)


def _mem_kv(mem, mem_norm, wmkv):
    n, d = mem.shape

    def body(mem_ref, g_ref, w_ref, memn_ref, k_ref, v_ref):
        y, _ = _rms_fwd(mem_ref[...], g_ref[...])
        yb = y.astype(BF16)
        memn_ref[...] = yb
        for hh in range(MEM_HEADS):
            k_ref[hh] = _dot_nn(yb, w_ref[hh]).astype(BF16)
            v_ref[hh] = _dot_nn(yb, w_ref[MEM_HEADS + hh]).astype(BF16)

    return pl.pallas_call(
        body, name="mem_kv",
        out_shape=[jax.ShapeDtypeStruct((n, d), BF16), jax.ShapeDtypeStruct((MEM_HEADS, n, MEM_HD), BF16),
                   jax.ShapeDtypeStruct((MEM_HEADS, n, MEM_HD), BF16)],
        compiler_params=_cparams(),
    )(mem, mem_norm, wmkv)


def _mem_softmax(qb, km):
    s = _dot_nt(qb, km) * MEM_SCALE
    e = jnp.exp(s - jnp.max(s, axis=-1, keepdims=True))
    return e / jnp.sum(e, axis=-1, keepdims=True)


def _xattn_fwd(h1, a, p, w_out, g, wmq, km, vm, wmo):
    t, d = h1.shape
    tm = min(512, t)
    half = a.shape[1]

    def body(h_ref, a_ref, p_ref, wo_ref, g_ref, wmq_ref, km_ref, vm_ref, wmo_ref,
             h2_ref, h3_ref, hn_ref, q_ref, o_ref):
        h2 = h_ref[...] + _dot_nn(a_ref[...], wo_ref[0:half, :]) + _dot_nn(p_ref[...], wo_ref[half:2 * half, :])
        h2_ref[...] = h2
        hn, _ = _rms_fwd(h2, g_ref[...])
        hnb = hn.astype(BF16)
        hn_ref[...] = hnb
        qb = _dot_nn(hnb, wmq_ref[...]).astype(BF16)
        q_ref[...] = qb
        outs = []
        for hh in range(MEM_HEADS):
            pr = _mem_softmax(qb[:, hh * MEM_HD:(hh + 1) * MEM_HD], km_ref[hh])
            outs.append(_dot_nn(pr.astype(BF16), vm_ref[hh]))
        ob = jnp.concatenate(outs, axis=-1).astype(BF16)
        o_ref[...] = ob
        h3_ref[...] = h2 + _dot_nn(ob, wmo_ref[...])

    full = lambda shape: pl.BlockSpec(shape, lambda i: (0,) * len(shape))
    row = lambda w: pl.BlockSpec((tm, w), lambda i: (i, 0))
    return pl.pallas_call(
        body, name="xattn_fwd", grid=(t // tm,),
        in_specs=[row(d), row(half), row(half), full(w_out.shape), full((1, d)), full(wmq.shape),
                  full(km.shape), full(vm.shape), full(wmo.shape)],
        out_specs=[row(d), row(d), row(d), row(d), row(d)],
        out_shape=[jax.ShapeDtypeStruct((t, d), F32), jax.ShapeDtypeStruct((t, d), F32),
                   jax.ShapeDtypeStruct((t, d), BF16), jax.ShapeDtypeStruct((t, d), BF16),
                   jax.ShapeDtypeStruct((t, d), BF16)],
        compiler_params=_cparams(),
    )(h1, a, p, w_out, g, wmq, km, vm, wmo)


def _xattn_bwd(dh3, h2, qm, g, wmq, km, vm, wmo, w_out, token=None):
    t, d = h2.shape
    tm = min(256, t)
    half = d // 2

    def body(dh3_ref, h2_ref, q_ref, g_ref, wmq_ref, km_ref, vm_ref, wmo_ref, wo_ref,
             dh2_ref, dq_ref, da_ref, dp_ref, dk_ref, dv_ref, dg_ref):
        i = pl.program_id(0)
        first = i == 0
        dh3 = dh3_ref[...]
        dob = _dot_nt(dh3.astype(BF16), wmo_ref[...]).astype(BF16)
        qb = q_ref[...]
        dq_parts = []
        for hh in range(MEM_HEADS):
            cols = slice(hh * MEM_HD, (hh + 1) * MEM_HD)
            kk, vv = km_ref[hh], vm_ref[hh]
            pr = _mem_softmax(qb[:, cols], kk)
            doh = dob[:, cols]
            _accumulate(dv_ref.at[hh], _dot_tn(pr.astype(BF16), doh), first)
            dpp = _dot_nt(doh, vv)
            dsb = (pr * (dpp - jnp.sum(dpp * pr, axis=-1, keepdims=True)) * MEM_SCALE).astype(BF16)
            dq_parts.append(_dot_nn(dsb, kk))
            _accumulate(dk_ref.at[hh], _dot_tn(dsb, qb[:, cols]), first)
        dqb = jnp.concatenate(dq_parts, axis=-1).astype(BF16)
        dq_ref[...] = dqb
        x = h2_ref[...]
        gg = g_ref[...]
        _, r = _rms_fwd(x, gg)
        dx, dg = _rms_bwd(_dot_nt(dqb, wmq_ref[...]), x, gg, r)
        dh2 = dh3 + dx
        dh2_ref[...] = dh2
        dap = _dot_nt(dh2.astype(BF16), wo_ref[...])
        da_ref[...] = dap[:, 0:half].astype(BF16)
        dp_ref[...] = dap[:, half:d].astype(BF16)
        _accumulate(dg_ref, dg, first)

    full = lambda shape: pl.BlockSpec(shape, lambda i: (0,) * len(shape))
    row = lambda w: pl.BlockSpec((tm, w), lambda i: (i, 0))
    return _call_after(
        token, body,
        [row(d), row(d), row(d), full((1, d)), full(wmq.shape), full(km.shape), full(vm.shape),
         full(wmo.shape), full(w_out.shape)],
        (dh3, h2, qm, g, wmq, km, vm, wmo, w_out),
        name="xattn_bwd", grid=(t // tm,),
        out_specs=[row(d), row(d), row(half), row(half), full(km.shape), full(vm.shape), full((1, d))],
        out_shape=[jax.ShapeDtypeStruct((t, d), F32), jax.ShapeDtypeStruct((t, d), BF16),
                   jax.ShapeDtypeStruct((t, half), BF16), jax.ShapeDtypeStruct((t, half), BF16),
                   jax.ShapeDtypeStruct(km.shape, F32), jax.ShapeDtypeStruct(vm.shape, F32),
                   jax.ShapeDtypeStruct((1, d), F32)],
        compiler_params=_cparams(),
    )


def _mem_kv_bwd(dkm, dvm, memn, mem, mem_norm, wmkv):
    n, d = mem.shape

    def body(dk_ref, dv_ref, memn_ref, mem_ref, g_ref, w_ref, dw_ref, dg_ref):
        memn = memn_ref[...]
        dmemn = jnp.zeros((n, d), F32)
        for s in range(2 * MEM_HEADS):
            src = dk_ref[s] if s < MEM_HEADS else dv_ref[s - MEM_HEADS]
            db = src.astype(BF16)
            dw_ref[s] = _dot_tn(memn, db)
            dmemn = dmemn + _dot_nt(db, w_ref[s])
        x = mem_ref[...]
        gg = g_ref[...]
        _, r = _rms_fwd(x, gg)
        _, dg = _rms_bwd(dmemn, x, gg, r)
        dg_ref[...] = dg

    return pl.pallas_call(
        body, name="mem_kv_bwd",
        out_shape=[jax.ShapeDtypeStruct(wmkv.shape, F32), jax.ShapeDtypeStruct((1, d), F32)],
        compiler_params=_cparams(),
    )(dkm, dvm, memn, mem, mem_norm, wmkv)


MESH_ID = pl.DeviceIdType.MESH
ANY = pl.BlockSpec(memory_space=pl.ANY)


def _coords():
    return lax.axis_index("x"), lax.axis_index("y"), lax.axis_index("c")


def _other_chips(x, y):
    return [(1 - x, y), (x, 1 - y), (1 - x, 1 - y)]


def _all_gather_rows(shard):
    r, w = shard.shape

    def body(x_ref, out_ref, send_sems, recv_sems, local_sem):
        x, y, c = _coords()
        me, sibling = (x, y, c), (x, y, 1 - c)
        chips = _other_chips(x, y)

        def block(px, py, pc):
            return out_ref.at[4 * px + 2 * py + pc]

        def copy(k, blk, to, src=None):
            return pltpu.make_async_remote_copy(
                src_ref=block(*blk) if src is None else src, dst_ref=block(*blk),
                send_sem=send_sems.at[k], recv_sem=recv_sems.at[k], device_id=to, device_id_type=MESH_ID)

        mine = pltpu.make_async_copy(x_ref, block(*me), local_sem)
        mine.start()
        first = [copy(0, me, sibling, src=x_ref)]
        first += [copy(1 + j, me, (*chip, c), src=x_ref) for j, chip in enumerate(chips)]
        for cp in first:
            cp.start()
        passed = [copy(4 + j, (*chip, c), sibling) for j, chip in enumerate(chips)]
        for j, chip in enumerate(chips):
            copy(1 + j, (*chip, c), me).wait_recv()
            passed[j].start()
        copy(0, sibling, me).wait_recv()
        for j, chip in enumerate(chips):
            copy(4 + j, (*chip, 1 - c), me).wait_recv()
        for cp in first + passed:
            cp.wait_send()
        mine.wait()

    return pl.pallas_call(
        body, name="all_gather_rows",
        out_shape=jax.ShapeDtypeStruct((N_DEV, r, w), shard.dtype),
        in_specs=[ANY], out_specs=ANY,
        scratch_shapes=[pltpu.SemaphoreType.DMA((7,)), pltpu.SemaphoreType.DMA((7,)), pltpu.SemaphoreType.DMA],
    )(shard)


def _all_gather_direct(shard):
    r, w = shard.shape

    def body(x_ref, out_ref, send_sems, recv_sems, local_sem):
        x, y, c = _coords()
        mine = pltpu.make_async_copy(x_ref, out_ref.at[4 * x + 2 * y + c], local_sem)
        mine.start()
        arrivals = []
        for k in range(1, N_DEV):
            px, py, pc = x ^ ((k >> 2) & 1), y ^ ((k >> 1) & 1), c ^ (k & 1)
            pltpu.make_async_remote_copy(
                src_ref=x_ref, dst_ref=out_ref.at[4 * x + 2 * y + c],
                send_sem=send_sems.at[k - 1], recv_sem=recv_sems.at[k - 1],
                device_id=(px, py, pc), device_id_type=MESH_ID).start()
            arrivals.append(pltpu.make_async_remote_copy(
                src_ref=x_ref, dst_ref=out_ref.at[4 * px + 2 * py + pc],
                send_sem=send_sems.at[k - 1], recv_sem=recv_sems.at[k - 1],
                device_id=(px, py, pc), device_id_type=MESH_ID))
        for cp in arrivals:
            cp.wait_recv()
        for cp in arrivals:
            cp.wait_send()
        mine.wait()

    return pl.pallas_call(
        body, name="all_gather_direct",
        out_shape=jax.ShapeDtypeStruct((N_DEV, r, w), shard.dtype),
        in_specs=[pl.BlockSpec(memory_space=pltpu.VMEM)], out_specs=pl.BlockSpec(memory_space=pltpu.VMEM),
        scratch_shapes=[pltpu.SemaphoreType.DMA((7,)), pltpu.SemaphoreType.DMA((7,)), pltpu.SemaphoreType.DMA],
        compiler_params=_cparams(),
    )(shard)


def _exchange_core(g, tag):
    _, r, w = g.shape

    def body(g_ref, land_ref, send_sems, recv_sems):
        x, y, c = _coords()
        copies = []
        for chip in range(4):
            copies.append(pltpu.make_async_remote_copy(
                src_ref=g_ref.at[2 * chip + (1 - c)], dst_ref=land_ref.at[chip],
                send_sem=send_sems.at[chip], recv_sem=recv_sems.at[chip],
                device_id=(x, y, 1 - c), device_id_type=MESH_ID))
        for cp in copies:
            cp.start()
        for cp in copies:
            cp.wait_recv()
        for cp in copies:
            cp.wait_send()

    return pl.pallas_call(
        body, name="exchange_core_" + tag,
        out_shape=jax.ShapeDtypeStruct((4, r, w), g.dtype),
        in_specs=[ANY], out_specs=ANY,
        scratch_shapes=[pltpu.SemaphoreType.DMA((4,)), pltpu.SemaphoreType.DMA((4,))],
    )(g)


def _chip_partial(g, land, cidx, tag):
    _, r, w = g.shape
    tr = _row_tile(r, 1024)
    g4 = g.reshape(4, 2, r, w)

    def body(c_ref, g_ref, l_ref, o_ref):
        o_ref[0] = (g_ref[0, 0].astype(F32) + l_ref[0].astype(F32)).astype(o_ref.dtype)

    return pl.pallas_call(
        body, name="chip_partial_" + tag,
        grid_spec=pltpu.PrefetchScalarGridSpec(
            num_scalar_prefetch=1, grid=(4, r // tr),
            in_specs=[pl.BlockSpec((1, 1, tr, w), lambda i, j, s: (i, s[0], j, 0)),
                      pl.BlockSpec((1, tr, w), lambda i, j, s: (i, j, 0))],
            out_specs=pl.BlockSpec((1, tr, w), lambda i, j, s: (i, j, 0))),
        out_shape=jax.ShapeDtypeStruct((4, r, w), g.dtype),
        compiler_params=_cparams(),
    )(cidx, g4, land)


def _exchange_chips(part):
    _, r, w = part.shape

    def body(p_ref, land_ref, send_sems, recv_sems):
        x, y, c = _coords()
        copies = []
        for j, (px, py) in enumerate(_other_chips(x, y)):
            copies.append(pltpu.make_async_remote_copy(
                src_ref=p_ref.at[2 * px + py], dst_ref=land_ref.at[j],
                send_sem=send_sems.at[j], recv_sem=recv_sems.at[j],
                device_id=(px, py, c), device_id_type=MESH_ID))
        for cp in copies:
            cp.start()
        for cp in copies:
            cp.wait_recv()
        for cp in copies:
            cp.wait_send()

    return pl.pallas_call(
        body, name="exchange_chips",
        out_shape=jax.ShapeDtypeStruct((3, r, w), part.dtype),
        in_specs=[ANY], out_specs=ANY,
        scratch_shapes=[pltpu.SemaphoreType.DMA((3,)), pltpu.SemaphoreType.DMA((3,))],
    )(part)


HBM_SPEC = pl.BlockSpec(memory_space=pltpu.HBM)
SEM_SPEC = pl.BlockSpec(memory_space=pltpu.SEMAPHORE)
SPLIT_EFFECT = pltpu.SideEffectType.DATAFLOW_SIDE_EFFECTING


def _ici_block(src_ref, px, py):
    return src_ref if len(src_ref.shape) == 2 else src_ref.at[2 * px + py]


def _ici_start(src, after, name):
    r, w = src.shape[-2:]
    land_shape = (3, r, w)

    def body(src_ref, land_ref, after_ref, send_sems, recv_sems, src_thru, land_thru, token):
        x, y, c = _coords()
        for j, (px, py) in enumerate(_other_chips(x, y)):
            pltpu.make_async_remote_copy(
                src_ref=_ici_block(src_ref, px, py), dst_ref=land_ref.at[j],
                send_sem=send_sems.at[j], recv_sem=recv_sems.at[j],
                device_id=(px, py, c), device_id_type=MESH_ID).start()
        token[...] = jnp.zeros_like(token)

    return pl.pallas_call(
        body, name=name,
        out_shape=(pltpu.SemaphoreType.DMA((3,)), pltpu.SemaphoreType.DMA((3,)), pltpu.HBM(src.shape, src.dtype),
                   pltpu.HBM(land_shape, src.dtype), jax.ShapeDtypeStruct((8, 128), F32)),
        in_specs=(HBM_SPEC, HBM_SPEC, ANY),
        out_specs=(SEM_SPEC, SEM_SPEC, HBM_SPEC, HBM_SPEC, pl.BlockSpec(memory_space=pltpu.VMEM)),
        input_output_aliases={0: 2, 1: 3},
        compiler_params=pltpu.CompilerParams(has_side_effects=SPLIT_EFFECT),
    )(pltpu.with_memory_space_constraint(src, pltpu.HBM),
      pltpu.with_memory_space_constraint(lax.empty(land_shape, src.dtype), pltpu.HBM), after)


def _ici_wait(started, after, name):
    send_sems, recv_sems, src_thru, land_thru, _ = started

    def body(src_ref, land_ref, send_sems, recv_sems, after_ref, src_dead, got_ref):
        x, y, c = _coords()
        for j, (px, py) in enumerate(_other_chips(x, y)):
            copy = pltpu.make_async_remote_copy(
                src_ref=_ici_block(src_ref, px, py), dst_ref=land_ref.at[j],
                send_sem=send_sems.at[j], recv_sem=recv_sems.at[j],
                device_id=(px, py, c), device_id_type=MESH_ID)
            copy.wait_send()
            copy.wait_recv()

    return pl.pallas_call(
        body, name=name,
        out_shape=(pltpu.HBM(src_thru.shape, src_thru.dtype), pltpu.HBM(land_thru.shape, land_thru.dtype)),
        in_specs=(HBM_SPEC, HBM_SPEC, SEM_SPEC, SEM_SPEC, ANY),
        out_specs=(HBM_SPEC, HBM_SPEC), input_output_aliases={0: 0, 1: 1},
        compiler_params=pltpu.CompilerParams(has_side_effects=SPLIT_EFFECT),
    )(src_thru, land_thru, send_sems, recv_sems, after)


def _core_share(own, land, name):
    r, w = own.shape

    def body(own_ref, land_ref, out_ref, send_sems, recv_sems, local_sems):
        x, y, c = _coords()
        sibling = (x, y, 1 - c)
        chips = [(x, y)] + _other_chips(x, y)
        srcs = [own_ref] + [land_ref.at[j] for j in range(3)]
        local, sent, arriving = [], [], []
        for k, ((px, py), src) in enumerate(zip(chips, srcs)):
            local.append(pltpu.make_async_copy(src, out_ref.at[4 * px + 2 * py + c], local_sems.at[k]))
            sent.append(pltpu.make_async_remote_copy(
                src_ref=src, dst_ref=out_ref.at[4 * px + 2 * py + c],
                send_sem=send_sems.at[k], recv_sem=recv_sems.at[k], device_id=sibling, device_id_type=MESH_ID))
            arriving.append(pltpu.make_async_remote_copy(
                src_ref=src, dst_ref=out_ref.at[4 * px + 2 * py + (1 - c)],
                send_sem=send_sems.at[k], recv_sem=recv_sems.at[k], device_id=sibling, device_id_type=MESH_ID))
        for cp in local + sent:
            cp.start()
        for cp in arriving:
            cp.wait_recv()
        for cp in sent:
            cp.wait_send()
        for cp in local:
            cp.wait()

    return pl.pallas_call(
        body, name=name,
        out_shape=jax.ShapeDtypeStruct((N_DEV, r, w), own.dtype),
        in_specs=[ANY, ANY], out_specs=ANY,
        scratch_shapes=[pltpu.SemaphoreType.DMA((4,)), pltpu.SemaphoreType.DMA((4,)), pltpu.SemaphoreType.DMA((4,))],
    )(own, land)


def _adamw(w, g, m, v):
    m = ADAM_B1 * m + (1.0 - ADAM_B1) * g
    v = ADAM_B2 * v + (1.0 - ADAM_B2) * (g * g)
    m_hat = m / ADAM_C1
    v_hat = v / ADAM_C2
    delta = -ADAM_LR * (m_hat / (jnp.sqrt(v_hat) + ADAM_EPS) + ADAM_WD * w)
    return delta, m, v


def _adam_big(part, land, w, m, v, chip_idx, tag):
    r, wd = w.shape
    tr, tw = _row_tile(r, 1024), 256

    def body(s_ref, p_ref, l_ref, w_ref, m_ref, v_ref, g_ref, d_ref, mo_ref, vo_ref):
        g = p_ref[0].astype(F32)
        for j in range(3):
            g = g + l_ref[j].astype(F32)
        delta, mn, vn = _adamw(w_ref[...], g, m_ref[...], v_ref[...])
        g_ref[...] = g
        d_ref[...] = delta
        mo_ref[...] = mn
        vo_ref[...] = vn

    row = pl.BlockSpec((tr, tw), lambda i, j, s: (i, j))
    return pl.pallas_call(
        body, name="adam_big_" + tag,
        grid_spec=pltpu.PrefetchScalarGridSpec(
            num_scalar_prefetch=1, grid=(r // tr, wd // tw),
            in_specs=[pl.BlockSpec((1, tr, tw), lambda i, j, s: (s[0], i, j)),
                      pl.BlockSpec((3, tr, tw), lambda i, j, s: (0, i, j)), row, row, row],
            out_specs=[row, row, row, row]),
        out_shape=[jax.ShapeDtypeStruct((r, wd), F32)] * 4,
        compiler_params=_cparams(),
    )(chip_idx, part, land, w, m, v)


def _adam_small(parts, w, m, v):
    _, r, wd = parts.shape

    def body(p_ref, w_ref, m_ref, v_ref, g_ref, d_ref, mo_ref, vo_ref):
        g = p_ref[0]
        for k in range(1, N_DEV):
            g = g + p_ref[k]
        delta, mn, vn = _adamw(w_ref[...], g, m_ref[...], v_ref[...])
        g_ref[...] = g
        d_ref[...] = delta
        mo_ref[...] = mn
        vo_ref[...] = vn

    return pl.pallas_call(
        body, name="adam_small",
        out_shape=[jax.ShapeDtypeStruct((r, wd), F32)] * 4,
        compiler_params=_cparams(),
    )(parts, w, m, v)


def _pad_rows(a, rows):
    return jnp.pad(a, ((0, rows - a.shape[0]), (0, 0)))


def _pad_w_in(w):
    cut = Q_RANK + KV_RANK + ROPE
    return jnp.concatenate([w[:, :cut], jnp.zeros((w.shape[0], 64), w.dtype), w[:, cut:]], axis=1)


def _unpad_w_in(w):
    cut = Q_RANK + KV_RANK + ROPE
    return jnp.concatenate([w[:, :cut], w[:, cut + 64:]], axis=1)


def _pack_big(p, group):
    if group == "mid":
        parts = [_pad_w_in(p["w_in"][0]), p["w_out"][0], p["w_mq"][0], p["w_mo"][0],
                 p["w_mkv"][0].reshape(256, D_MODEL),
                 _pad_rows(p["w_q_up"][0].T.reshape(24, D_MODEL), 32),
                 p["w_kv_up"][0].reshape(16, D_MODEL)]
    else:
        parts = [p[group + "_w_gate"][0].T, p[group + "_w_up"][0].T, p[group + "_w_down"][0]]
    return jnp.concatenate(parts, axis=0)


def _unpack_big(a, group):
    seg = lambda n: a[SEG_OFF[n][0]:SEG_OFF[n][0] + SEG_OFF[n][1]]
    if group == "mid":
        return {"w_in": _unpad_w_in(seg("w_in"))[None], "w_out": seg("w_out")[None], "w_mq": seg("w_mq")[None],
                "w_mo": seg("w_mo")[None], "w_mkv": seg("w_mkv").reshape(D_MODEL, 256)[None],
                "w_q_up": seg("w_q")[:24].reshape(96, Q_RANK).T[None],
                "w_kv_up": seg("w_kv").reshape(KV_RANK, 128)[None]}
    return {group + "_w_gate": seg(group + "_g").T[None], group + "_w_up": seg(group + "_u").T[None],
            group + "_w_down": seg(group + "_d")[None]}


def _unpack_gathered(full, group):
    seg = lambda n: full[:, SEG_OFF[n][0]:SEG_OFF[n][0] + SEG_OFF[n][1]]
    rows = lambda n: seg(n).reshape(-1, D_MODEL)
    if group != "mid":
        return {n: rows(n) for n, _ in GROUP_SEGS[group]}
    wq_t = seg("w_q")[:, :24].reshape(MLA_HEADS, NOPE + ROPE, Q_RANK)
    wq_t = jnp.pad(wq_t, ((0, 0), (0, HEAD_PAD - NOPE - ROPE), (0, 0))).reshape(MLA_HEADS * HEAD_PAD, Q_RANK)
    wkv = seg("w_kv").reshape(N_DEV, KV_RANK, 128).transpose(1, 0, 2).reshape(KV_RANK, N_DEV * 128)
    return {"w_in": rows("w_in"), "w_out": rows("w_out"), "w_mq": rows("w_mq"), "w_mo": rows("w_mo"),
            "w_mkv": seg("w_mkv").reshape(N_DEV, D_MODEL, 256), "w_q": wq_t, "w_kv": wkv}


def _pack_grads(gr, group):
    blk = lambda a: a.reshape(N_DEV, -1, D_MODEL)
    if group == "mid":
        dwq = gr["w_q"].reshape(MLA_HEADS, HEAD_PAD, Q_RANK)[:, :NOPE + ROPE].reshape(N_DEV, 24, D_MODEL)
        dwq = jnp.pad(dwq, ((0, 0), (0, 8), (0, 0)))
        dwkv = gr["w_kv"].reshape(KV_RANK, N_DEV, 128).transpose(1, 0, 2).reshape(N_DEV, 16, D_MODEL)
        parts = [blk(gr["w_in"]), blk(gr["w_out"]), blk(gr["w_mq"]), blk(gr["w_mo"]),
                 gr["w_mkv"].reshape(N_DEV, 256, D_MODEL), dwq, dwkv]
    else:
        parts = [blk(gr[n]) for n, _ in GROUP_SEGS[group]]
    return jnp.concatenate([a.astype(BF16) for a in parts], axis=1)


def _pack_small(vals):
    parts = []
    for n, r in SMALL_ROWS:
        parts.append(_pad_rows(vals[n].reshape(-1, 128), r) if n in vals else jnp.zeros((r, 128), F32))
    return jnp.concatenate(parts, axis=0)


def _unpack_small(a, shapes):
    out = {}
    for n, shape in shapes.items():
        o = SMALL_OFF[n][0]
        out[n] = a[o:o + int(np.prod(shape)) // 128].reshape(shape)
    return out


BIG_NAMES = ("ffn1_w_gate", "ffn1_w_up", "ffn1_w_down", "w_in", "w_q_up", "w_kv_up", "w_out", "w_mq", "w_mkv",
             "w_mo", "ffn2_w_gate", "ffn2_w_up", "ffn2_w_down")
SMALL_NAMES = ("ffn1_norm", "mix_norm", "q_norm", "kv_norm", "pool_w", "pool_scale", "xattn_norm", "mem_norm",
               "ffn2_norm", "final_norm")
WEIGHT_ORDER = ("ffn1_norm", "ffn1_w_gate", "ffn1_w_up", "ffn1_w_down", "mix_norm", "w_in", "q_norm", "w_q_up",
                "kv_norm", "w_kv_up", "pool_w", "pool_scale", "w_out", "xattn_norm", "mem_norm", "w_mq", "w_mkv",
                "w_mo", "ffn2_norm", "ffn2_w_gate", "ffn2_w_up", "ffn2_w_down", "final_norm")


def _rope_table():
    lane = np.arange(128)
    freqs = (1.0 / (ROPE_BASE ** (np.arange(0, ROPE, 2, dtype=np.float32) / ROPE))).astype(np.float32)
    tab = np.zeros((8, 128), np.float32)
    tab[0] = np.where(lane < ROPE, freqs[lane % (ROPE // 2)], 0.0)
    tab[1] = np.where(lane < ROPE // 2, -1.0, np.where(lane < ROPE, 1.0, 0.0))
    return jnp.asarray(tab)


def kernel(x, mem, positions, ffn1_norm, ffn1_w_gate, ffn1_w_up, ffn1_w_down, mix_norm, w_in, q_norm, w_q_up, kv_norm, w_kv_up, pool_w, pool_scale, w_out, xattn_norm, mem_norm, w_mq, w_mkv, w_mo, ffn2_norm, ffn2_w_gate, ffn2_w_up, ffn2_w_down, final_norm, loss_target, m_ffn1_norm, m_ffn1_w_gate, m_ffn1_w_up, m_ffn1_w_down, m_mix_norm, m_w_in, m_q_norm, m_w_q_up, m_kv_norm, m_w_kv_up, m_pool_w, m_pool_scale, m_w_out, m_xattn_norm, m_mem_norm, m_w_mq, m_w_mkv, m_w_mo, m_ffn2_norm, m_ffn2_w_gate, m_ffn2_w_up, m_ffn2_w_down, m_final_norm, v_ffn1_norm, v_ffn1_w_gate, v_ffn1_w_up, v_ffn1_w_down, v_mix_norm, v_w_in, v_q_norm, v_w_q_up, v_kv_norm, v_w_kv_up, v_pool_w, v_pool_scale, v_w_out, v_xattn_norm, v_mem_norm, v_w_mq, v_w_mkv, v_w_mo, v_ffn2_norm, v_ffn2_w_gate, v_ffn2_w_up, v_ffn2_w_down, v_final_norm):
    wts = dict(ffn1_norm=ffn1_norm, ffn1_w_gate=ffn1_w_gate, ffn1_w_up=ffn1_w_up, ffn1_w_down=ffn1_w_down,
               mix_norm=mix_norm, w_in=w_in, q_norm=q_norm, w_q_up=w_q_up, kv_norm=kv_norm, w_kv_up=w_kv_up,
               pool_w=pool_w, pool_scale=pool_scale, w_out=w_out, xattn_norm=xattn_norm, mem_norm=mem_norm,
               w_mq=w_mq, w_mkv=w_mkv, w_mo=w_mo, ffn2_norm=ffn2_norm, ffn2_w_gate=ffn2_w_gate,
               ffn2_w_up=ffn2_w_up, ffn2_w_down=ffn2_w_down, final_norm=final_norm)
    mom = dict(ffn1_norm=m_ffn1_norm, ffn1_w_gate=m_ffn1_w_gate, ffn1_w_up=m_ffn1_w_up, ffn1_w_down=m_ffn1_w_down,
               mix_norm=m_mix_norm, w_in=m_w_in, q_norm=m_q_norm, w_q_up=m_w_q_up, kv_norm=m_kv_norm,
               w_kv_up=m_w_kv_up, pool_w=m_pool_w, pool_scale=m_pool_scale, w_out=m_w_out, xattn_norm=m_xattn_norm,
               mem_norm=m_mem_norm, w_mq=m_w_mq, w_mkv=m_w_mkv, w_mo=m_w_mo, ffn2_norm=m_ffn2_norm,
               ffn2_w_gate=m_ffn2_w_gate, ffn2_w_up=m_ffn2_w_up, ffn2_w_down=m_ffn2_w_down, final_norm=m_final_norm)
    var = dict(ffn1_norm=v_ffn1_norm, ffn1_w_gate=v_ffn1_w_gate, ffn1_w_up=v_ffn1_w_up, ffn1_w_down=v_ffn1_w_down,
               mix_norm=v_mix_norm, w_in=v_w_in, q_norm=v_q_norm, w_q_up=v_w_q_up, kv_norm=v_kv_norm,
               w_kv_up=v_w_kv_up, pool_w=v_pool_w, pool_scale=v_pool_scale, w_out=v_w_out, xattn_norm=v_xattn_norm,
               mem_norm=v_mem_norm, w_mq=v_w_mq, w_mkv=v_w_mkv, w_mo=v_w_mo, ffn2_norm=v_ffn2_norm,
               ffn2_w_gate=v_ffn2_w_gate, ffn2_w_up=v_ffn2_w_up, ffn2_w_down=v_ffn2_w_down, final_norm=v_final_norm)

    t = x.shape[1]
    xs = x[0]
    mems = mem[0]
    target = loss_target[0]
    pos = positions.reshape(t, 1)
    row = lambda a: a.reshape(1, -1)
    rope_tab = _rope_table()

    cx, cy, cc = _coords()
    core_idx = cc.astype(jnp.int32).reshape(1)
    chip_idx = (2 * cx + cy).astype(jnp.int32).reshape(1)

    w_pack = {g: _pack_big(wts, g) for g in GROUP_SEGS}
    wb = {g: w_pack[g].astype(BF16) for g in GROUP_SEGS}
    full_ffn1 = _all_gather_rows(wb["ffn1"])
    fw = _unpack_gathered(full_ffn1, "ffn1")
    ag_mid = _ici_start(wb["mid"], full_ffn1, "ag_mid_start")
    g_ffn1, g_mix, g_q, g_kv = row(ffn1_norm), row(mix_norm), row(q_norm), row(kv_norm)
    g_x, g_mem, g_ffn2, g_fin = row(xattn_norm), row(mem_norm), row(ffn2_norm), row(final_norm)
    pool_wb = pool_w[0].astype(BF16)
    pool_sc = row(pool_scale)

    h1, n1, gate1, up1 = _ffn_fwd(xs, g_ffn1, fw["ffn1_g"], fw["ffn1_u"], fw["ffn1_d"], "ffn1_fwd", token=ag_mid[4])
    own_mid, land_mid = _ici_wait(ag_mid, h1, "ag_mid_wait")
    full_mid = _core_share(own_mid, land_mid, "ag_mid_share")
    fw.update(_unpack_gathered(full_mid, "mid"))
    ag_ffn2 = _ici_start(wb["ffn2"], full_mid, "ag_ffn2_start")
    u, z, qn, kvn, qh, kh, vh = _mix_prep(h1, g_mix, fw["w_in"], g_q, fw["w_q"], g_kv, fw["w_kv"], pos, rope_tab,
                                          token=ag_ffn2[4])
    a, lse = _attn_fwd(qh, kh, vh)
    p = _pool_fwd(z, pool_wb, pool_sc)
    memn, km, vm = _mem_kv(mems, g_mem, fw["w_mkv"])
    h2, h3, hn, qm, om = _xattn_fwd(h1, a, p, fw["w_out"], g_x, fw["w_mq"], km, vm, fw["w_mo"])
    own_ffn2, land_ffn2 = _ici_wait(ag_ffn2, h3, "ag_ffn2_wait")
    fw.update(_unpack_gathered(_core_share(own_ffn2, land_ffn2, "ag_ffn2_share"), "ffn2"))
    h4, n2, gate2, up2 = _ffn_fwd(h3, g_ffn2, fw["ffn2_g"], fw["ffn2_u"], fw["ffn2_d"], "ffn2_fwd")
    loss_part, dh4, dg_fin = _loss_head(h4, target, g_fin)

    gr = {}
    dh3, dgate2, dup2, act2, dg_ffn2 = _ffn_bwd_data(dh4, h3, g_ffn2, gate2, up2, fw["ffn2_g"], fw["ffn2_u"],
                                                     fw["ffn2_d"], "ffn2_bwd")
    gr["ffn2_g"] = _tn_matmul(dgate2, n2, "ffn2_dwg", tmm=1408)
    gr["ffn2_u"] = _tn_matmul(dup2, n2, "ffn2_dwu", tmm=1408)
    gr["ffn2_d"] = _tn_matmul(act2, dh4, "ffn2_dwd", scale=0.5, tmm=1408)
    g_ffn2_pack = _pack_grads(gr, "ffn2")
    part_ffn2 = _chip_partial(g_ffn2_pack, _exchange_core(g_ffn2_pack, "ffn2"), core_idx, "ffn2")
    rs_ffn2 = _ici_start(part_ffn2, g_ffn2_pack, "rs_ffn2_start")
    dh2, dqm, da, dp, dkm, dvm, dg_x = _xattn_bwd(dh3, h2, qm, g_x, fw["w_mq"], km, vm, fw["w_mo"], fw["w_out"],
                                                  token=rs_ffn2[4])
    gr["w_mo"] = _tn_matmul(om, dh3, "dw_mo", tmm=512)
    gr["w_mq"] = _tn_matmul(hn, dqm, "dw_mq", tmm=512)
    gr["w_out"] = jnp.concatenate([_tn_matmul(a, dh2, "dw_out_a"), _tn_matmul(p, dh2, "dw_out_p")], axis=0)
    gr["w_mkv"], dg_mem = _mem_kv_bwd(dkm, dvm, memn, mems, g_mem, fw["w_mkv"])
    dz_pool, d_pool_w, d_pool_sc = _pool_bwd(dp, z, pool_wb, pool_sc)
    dqh, dkh, dvh = _attn_bwd(qh, kh, vh, da, lse, _attn_delta(a, da))
    dh1, dq, dkv, dz, dg_q, dg_kv, dg_mix = _mla_bwd(dqh, dkh, dvh, z, dz_pool, h1, dh2, g_mix, fw["w_in"], g_q,
                                                     fw["w_q"], g_kv, fw["w_kv"], pos, rope_tab)
    gr["w_q"] = _tn_matmul(dq, qn, "dw_q", tmm=512)
    gr["w_kv"] = _tn_matmul(kvn, dkv, "dw_kv")
    gr["w_in"] = _tn_matmul(u, dz, "dw_in", tmm=512)
    g_mid_pack = _pack_grads(gr, "mid")
    part_mid = _chip_partial(g_mid_pack, _exchange_core(g_mid_pack, "mid"), core_idx, "mid")
    part_ffn2, land_ffn2_g = _ici_wait(rs_ffn2, part_mid, "rs_ffn2_wait")
    rs_mid = _ici_start(part_mid, land_ffn2_g, "rs_mid_start")
    dx, dgate1, dup1, act1, dg_ffn1 = _ffn_bwd_data(dh1, xs, g_ffn1, gate1, up1, fw["ffn1_g"], fw["ffn1_u"],
                                                    fw["ffn1_d"], "ffn1_bwd", token=rs_mid[4])
    part_mid, land_mid_g = _ici_wait(rs_mid, dx, "rs_mid_wait")
    gr["ffn1_g"] = _tn_matmul(dgate1, n1, "ffn1_dwg", tmm=1408)
    gr["ffn1_u"] = _tn_matmul(dup1, n1, "ffn1_dwu", tmm=1408)
    gr["ffn1_d"] = _tn_matmul(act1, dh1, "ffn1_dwd", scale=0.5, tmm=1408)
    g_ffn1_pack = _pack_grads(gr, "ffn1")
    part_ffn1 = _chip_partial(g_ffn1_pack, _exchange_core(g_ffn1_pack, "ffn1"), core_idx, "ffn1")
    land_ffn1_g = _exchange_chips(part_ffn1)

    big = {}
    for grp, part, land in (("ffn1", part_ffn1, land_ffn1_g), ("mid", part_mid, land_mid_g),
                            ("ffn2", part_ffn2, land_ffn2_g)):
        res = _adam_big(part, land, w_pack[grp], _pack_big(mom, grp), _pack_big(var, grp), chip_idx, grp)
        for k, packed in enumerate(res):
            big.setdefault(k, {}).update(_unpack_big(packed, grp))

    small_g = dict(ffn1_norm=dg_ffn1, mix_norm=dg_mix, q_norm=dg_q, kv_norm=dg_kv, pool_w=d_pool_w,
                   pool_scale=d_pool_sc, xattn_norm=dg_x, mem_norm=dg_mem, ffn2_norm=dg_ffn2, final_norm=dg_fin,
                   loss=loss_part)
    parts = _all_gather_direct(_pack_small(small_g))
    small = _adam_small(parts, _pack_small({n: wts[n] for n in SMALL_NAMES}),
                        _pack_small({n: mom[n] for n in SMALL_NAMES}), _pack_small({n: var[n] for n in SMALL_NAMES}))
    loss = small[0][SMALL_OFF["loss"][0], 0]
    shapes = {n: wts[n].shape for n in SMALL_NAMES}
    small = [_unpack_small(s, shapes) for s in small]

    outs = [loss, dx[None]]
    for k in range(4):
        for n in WEIGHT_ORDER:
            outs.append(big[k][n] if n in BIG_NAMES else small[k][n])
    return tuple(outs)
```

```python
import numpy as np

import jax
import jax.numpy as jnp
from jax import lax
from jax.experimental import pallas as pl
from jax.experimental.pallas import tpu as pltpu

F32 = jnp.float32
BF16 = jnp.bfloat16

N_DEV = 8
D_MODEL = 1024
D_FF = 2816
MLA_HEADS = 4
NOPE = 128
ROPE = 64
HEAD_PAD = 256
V_DIM = 128
Q_RANK = 256
KV_RANK = 128
POOL_WINDOWS = (2, 4, 8, 16)
POOL_CH = 128
POOL_HALO = 16
N_MEM = 256
MEM_HEADS = 4
MEM_HD = 256
ROPE_BASE = 10000.0
RMS_EPS = 1e-6
ATTN_SCALE = (NOPE + ROPE) ** -0.5
MEM_SCALE = MEM_HD ** -0.5
NEG_BIG = -1e30

ADAM_LR = 0.001
ADAM_B1 = 0.9
ADAM_B2 = 0.999
ADAM_EPS = 1e-08
ADAM_WD = 0.01
ADAM_STEP = 10
ADAM_C1 = 1.0 - ADAM_B1 ** ADAM_STEP
ADAM_C2 = 1.0 - ADAM_B2 ** ADAM_STEP

VMEM_LIMIT_BYTES = 52 * 1024 * 1024
BF16_ROWS = 16

GROUP_SEGS = {
    "ffn1": (("ffn1_g", 352), ("ffn1_u", 352), ("ffn1_d", 352)),
    "mid": (("w_in", 128), ("w_out", 128), ("w_mq", 128), ("w_mo", 128), ("w_mkv", 256), ("w_q", 32), ("w_kv", 16)),
    "ffn2": (("ffn2_g", 352), ("ffn2_u", 352), ("ffn2_d", 352)),
}
SEG_OFF = {}
GROUP_ROWS = {}
for _g, _segs in GROUP_SEGS.items():
    _o = 0
    for _n, _r in _segs:
        SEG_OFF[_n] = (_o, _r)
        _o += _r
    GROUP_ROWS[_g] = _o

SMALL_ROWS = (("ffn1_norm", 8), ("mix_norm", 8), ("q_norm", 8), ("kv_norm", 8), ("pool_w", 512), ("pool_scale", 8),
              ("xattn_norm", 8), ("mem_norm", 8), ("ffn2_norm", 8), ("final_norm", 8), ("loss", 8))
SMALL_OFF = {}
_o = 0
for _n, _r in SMALL_ROWS:
    SMALL_OFF[_n] = (_o, _r)
    _o += _r


def _cparams(**kw):
    return pltpu.CompilerParams(vmem_limit_bytes=VMEM_LIMIT_BYTES, **kw)


def _row_tile(rows, limit):
    best = None
    for cand in range(BF16_ROWS, min(rows, limit) + 1, BF16_ROWS):
        if rows % cand == 0:
            best = cand
    assert best is not None, rows
    return best


def _dot_nn(a, b):
    return lax.dot_general(a, b, (((1,), (0,)), ((), ())), preferred_element_type=F32)


def _dot_nt(a, b):
    return lax.dot_general(a, b, (((1,), (1,)), ((), ())), preferred_element_type=F32)


def _dot_tn(a, b):
    return lax.dot_general(a, b, (((0,), (0,)), ((), ())), preferred_element_type=F32)


def _rms_fwd(x, g):
    r = lax.rsqrt(jnp.mean(x * x, axis=-1, keepdims=True) + RMS_EPS)
    return x * r * g, r


def _rms_bwd(dy, x, g, r):
    xhat = x * r
    dyg = dy * g
    dx = r * (dyg - xhat * jnp.mean(dyg * xhat, axis=-1, keepdims=True))
    dg = jnp.sum(dy * xhat, axis=0, keepdims=True)
    return dx, dg


def _accumulate(ref, val, first):
    if isinstance(first, bool):
        if first:
            ref[...] = val
        else:
            ref[...] += val
        return

    @pl.when(first)
    def _():
        ref[...] = val

    @pl.when(jnp.logical_not(first))
    def _():
        ref[...] += val


def _call_after(token, body, in_specs, args, **kw):
    if token is not None:
        inner = body
        body = lambda tok_ref, *refs: inner(*refs)
        in_specs = [pl.BlockSpec((8, 128), lambda *_: (0, 0))] + list(in_specs)
        args = (token,) + tuple(args)
    return pl.pallas_call(body, in_specs=in_specs, **kw)(*args)


def _rope_tables(pos_col, tab):
    ang = pos_col.astype(F32) * tab[0:1, :]
    return jnp.cos(ang), jnp.sin(ang) * tab[1:2, :]


def _swap_halves(x):
    lane = lax.broadcasted_iota(jnp.int32, x.shape, 1)
    return jnp.where((lane % 64) < 32, pltpu.roll(x, 96, 1), pltpu.roll(x, 32, 1))


def _rope_apply(x, cos_t, sin_t):
    return x * cos_t + _swap_halves(x) * sin_t


def _rope_apply_t(dy, cos_t, sin_t):
    return dy * cos_t + _swap_halves(dy * sin_t)


def _ffn_fwd(h, g, wg_t, wu_t, wd, name, token=None):
    t, d = h.shape
    f = wg_t.shape[0]
    tm, tf = min(1024, t), 256
    nf = f // tf

    def body(h_ref, g_ref, wg_ref, wu_ref, wd_ref, ho_ref, n_ref, gate_ref, up_ref, nb_sc, acc_sc):
        j = pl.program_id(1)

        @pl.when(j == 0)
        def _():
            y, _ = _rms_fwd(h_ref[...], g_ref[...])
            nb = y.astype(BF16)
            nb_sc[...] = nb
            n_ref[...] = nb
            acc_sc[...] = jnp.zeros_like(acc_sc)

        nb = nb_sc[...]
        gt = _dot_nt(nb, wg_ref[...])
        ut = _dot_nt(nb, wu_ref[...])
        gate_ref[...] = gt.astype(BF16)
        up_ref[...] = ut.astype(BF16)
        act = (gt * jax.nn.sigmoid(gt)) * ut
        acc_sc[...] += _dot_nn(act.astype(BF16), wd_ref[...])

        @pl.when(j == nf - 1)
        def _():
            ho_ref[...] = h_ref[...] + 0.5 * acc_sc[...]

    return _call_after(
        token, body,
        [pl.BlockSpec((tm, d), lambda i, j: (i, 0)),
         pl.BlockSpec((1, d), lambda i, j: (0, 0)),
         pl.BlockSpec((tf, d), lambda i, j: (j, 0)),
         pl.BlockSpec((tf, d), lambda i, j: (j, 0)),
         pl.BlockSpec((tf, d), lambda i, j: (j, 0))],
        (h, g, wg_t, wu_t, wd),
        name=name, grid=(t // tm, nf),
        out_specs=[pl.BlockSpec((tm, d), lambda i, j: (i, 0)),
                   pl.BlockSpec((tm, d), lambda i, j: (i, 0)),
                   pl.BlockSpec((tm, tf), lambda i, j: (i, j)),
                   pl.BlockSpec((tm, tf), lambda i, j: (i, j))],
        out_shape=[jax.ShapeDtypeStruct((t, d), F32), jax.ShapeDtypeStruct((t, d), BF16),
                   jax.ShapeDtypeStruct((t, f), BF16), jax.ShapeDtypeStruct((t, f), BF16)],
        scratch_shapes=[pltpu.VMEM((tm, d), BF16), pltpu.VMEM((tm, d), F32)],
        compiler_params=_cparams(),
    )


def _ffn_bwd_data(dho, h, g, gate, up, wg_t, wu_t, wd, name, token=None):
    t, d = h.shape
    f = wg_t.shape[0]
    tm, tf = min(1024, t), 256
    nf = f // tf

    def body(dho_ref, h_ref, g_ref, gate_ref, up_ref, wg_ref, wu_ref, wd_ref,
             dh_ref, dgate_ref, dup_ref, act_ref, dg_ref, dhb_sc, acc_sc):
        i, j = pl.program_id(0), pl.program_id(1)

        @pl.when(j == 0)
        def _():
            dhb_sc[...] = (0.5 * dho_ref[...]).astype(BF16)
            acc_sc[...] = jnp.zeros_like(acc_sc)

        dact = _dot_nt(dhb_sc[...], wd_ref[...])
        gt = gate_ref[...].astype(F32)
        ut = up_ref[...].astype(F32)
        sg = jax.nn.sigmoid(gt)
        silu = gt * sg
        dgb = (dact * ut * (sg * (1.0 + gt * (1.0 - sg)))).astype(BF16)
        dub = (dact * silu).astype(BF16)
        act_ref[...] = (silu * ut).astype(BF16)
        dgate_ref[...] = dgb
        dup_ref[...] = dub
        acc_sc[...] += _dot_nn(dgb, wg_ref[...]) + _dot_nn(dub, wu_ref[...])

        @pl.when(j == nf - 1)
        def _():
            x = h_ref[...]
            gg = g_ref[...]
            _, r = _rms_fwd(x, gg)
            dx, dg = _rms_bwd(acc_sc[...], x, gg, r)
            dh_ref[...] = dho_ref[...] + dx
            _accumulate(dg_ref, dg, i == 0)

    return _call_after(
        token, body,
        [pl.BlockSpec((tm, d), lambda i, j: (i, 0)),
         pl.BlockSpec((tm, d), lambda i, j: (i, 0)),
         pl.BlockSpec((1, d), lambda i, j: (0, 0)),
         pl.BlockSpec((tm, tf), lambda i, j: (i, j)),
         pl.BlockSpec((tm, tf), lambda i, j: (i, j)),
         pl.BlockSpec((tf, d), lambda i, j: (j, 0)),
         pl.BlockSpec((tf, d), lambda i, j: (j, 0)),
         pl.BlockSpec((tf, d), lambda i, j: (j, 0))],
        (dho, h, g, gate, up, wg_t, wu_t, wd),
        name=name, grid=(t // tm, nf),
        out_specs=[pl.BlockSpec((tm, d), lambda i, j: (i, 0)),
                   pl.BlockSpec((tm, tf), lambda i, j: (i, j)),
                   pl.BlockSpec((tm, tf), lambda i, j: (i, j)),
                   pl.BlockSpec((tm, tf), lambda i, j: (i, j)),
                   pl.BlockSpec((1, d), lambda i, j: (0, 0))],
        out_shape=[jax.ShapeDtypeStruct((t, d), F32), jax.ShapeDtypeStruct((t, f), BF16),
                   jax.ShapeDtypeStruct((t, f), BF16), jax.ShapeDtypeStruct((t, f), BF16),
                   jax.ShapeDtypeStruct((1, d), F32)],
        scratch_shapes=[pltpu.VMEM((tm, d), BF16), pltpu.VMEM((tm, d), F32)],
        compiler_params=_cparams(),
    )


def _tn_matmul(a, b, name, scale=1.0, tmm=None):
    t, m = a.shape
    n = b.shape[1]
    tmm = m if tmm is None else tmm
    tk = min(1024, t)

    def body(a_ref, b_ref, o_ref):
        k = pl.program_id(1)
        prod = _dot_tn(a_ref[...].astype(BF16), b_ref[...].astype(BF16))
        if scale != 1.0:
            prod = prod * scale
        _accumulate(o_ref, prod, k == 0)

    return pl.pallas_call(
        body, name=name, grid=(m // tmm, t // tk),
        in_specs=[pl.BlockSpec((tk, tmm), lambda i, k: (k, i)),
                  pl.BlockSpec((tk, n), lambda i, k: (k, 0))],
        out_specs=pl.BlockSpec((tmm, n), lambda i, k: (i, 0)),
        out_shape=jax.ShapeDtypeStruct((m, n), F32),
        compiler_params=_cparams(),
    )(a, b)


def _loss_head(h, target, g):
    t, d = h.shape
    tm = min(512, t)

    def body(h_ref, t_ref, g_ref, loss_ref, dh_ref, dg_ref):
        i = pl.program_id(0)
        x = h_ref[...]
        gg = g_ref[...]
        y, r = _rms_fwd(x, gg)
        err = y - t_ref[...]
        part = 0.5 * jnp.sum(jnp.mean(err * err, axis=-1, keepdims=True), axis=0, keepdims=True)
        dx, dg = _rms_bwd(err * (1.0 / d), x, gg, r)
        dh_ref[...] = dx
        _accumulate(loss_ref, jnp.broadcast_to(part, loss_ref.shape), i == 0)
        _accumulate(dg_ref, dg, i == 0)

    return pl.pallas_call(
        body, name="loss_head", grid=(t // tm,),
        in_specs=[pl.BlockSpec((tm, d), lambda i: (i, 0)),
                  pl.BlockSpec((tm, d), lambda i: (i, 0)),
                  pl.BlockSpec((1, d), lambda i: (0, 0))],
        out_specs=[pl.BlockSpec((8, 128), lambda i: (0, 0)),
                   pl.BlockSpec((tm, d), lambda i: (i, 0)),
                   pl.BlockSpec((1, d), lambda i: (0, 0))],
        out_shape=[jax.ShapeDtypeStruct((8, 128), F32), jax.ShapeDtypeStruct((t, d), F32),
                   jax.ShapeDtypeStruct((1, d), F32)],
        compiler_params=_cparams(),
    )(h, target, g)


def _mix_prep(h1, mix_norm, w_in, q_norm, wq_t, kv_norm, wkv, pos, rope_tab, token=None):
    t, d = h1.shape
    tm = min(512, t)

    def body(h_ref, gm_ref, win_ref, gq_ref, wq_ref, gkv_ref, wkv_ref, pos_ref, tab_ref,
             u_ref, z_ref, qn_ref, kvn_ref, q_ref, k_ref, v_ref):
        u, _ = _rms_fwd(h_ref[...], gm_ref[...])
        ub = u.astype(BF16)
        u_ref[...] = ub
        z = _dot_nn(ub, win_ref[...])
        z_ref[...] = z
        cos_t, sin_t = _rope_tables(pos_ref[...], tab_ref[...])
        qn, _ = _rms_fwd(z[:, 0:Q_RANK], gq_ref[...])
        qnb = qn.astype(BF16)
        qn_ref[...] = qnb
        q = _dot_nt(qnb, wq_ref[...])
        kvn, _ = _rms_fwd(z[:, Q_RANK:Q_RANK + KV_RANK], gkv_ref[...])
        kvnb = kvn.astype(BF16)
        kvn_ref[...] = kvnb
        kv = _dot_nn(kvnb, wkv_ref[...])
        k_pe = _rope_apply(z[:, Q_RANK + KV_RANK:Q_RANK + KV_RANK + 128], cos_t, sin_t)
        ones = jnp.ones((tm, V_DIM), F32)
        for hh in range(MLA_HEADS):
            b = hh * HEAD_PAD
            q_pe = _rope_apply(q[:, b + NOPE:b + HEAD_PAD], cos_t, sin_t)
            q_ref[hh] = jnp.concatenate([q[:, b:b + NOPE], q_pe], axis=-1).astype(BF16)
            k_ref[hh] = jnp.concatenate([kv[:, b:b + NOPE], k_pe], axis=-1).astype(BF16)
            v_ref[hh] = jnp.concatenate([kv[:, b + NOPE:b + HEAD_PAD], ones], axis=-1).astype(BF16)

    full = lambda shape: pl.BlockSpec(shape, lambda i: (0,) * len(shape))
    return _call_after(
        token, body,
        [pl.BlockSpec((tm, d), lambda i: (i, 0)), full((1, d)), full(w_in.shape), full((1, Q_RANK)),
         full(wq_t.shape), full((1, KV_RANK)), full(wkv.shape),
         pl.BlockSpec((tm, 1), lambda i: (i, 0)), full(rope_tab.shape)],
        (h1, mix_norm, w_in, q_norm, wq_t, kv_norm, wkv, pos, rope_tab),
        name="mix_prep", grid=(t // tm,),
        out_specs=[pl.BlockSpec((tm, d), lambda i: (i, 0)),
                   pl.BlockSpec((tm, d), lambda i: (i, 0)),
                   pl.BlockSpec((tm, Q_RANK), lambda i: (i, 0)),
                   pl.BlockSpec((tm, KV_RANK), lambda i: (i, 0)),
                   pl.BlockSpec((MLA_HEADS, tm, HEAD_PAD), lambda i: (0, i, 0)),
                   pl.BlockSpec((MLA_HEADS, tm, HEAD_PAD), lambda i: (0, i, 0)),
                   pl.BlockSpec((MLA_HEADS, tm, 2 * V_DIM), lambda i: (0, i, 0))],
        out_shape=[jax.ShapeDtypeStruct((t, d), BF16), jax.ShapeDtypeStruct((t, d), F32),
                   jax.ShapeDtypeStruct((t, Q_RANK), BF16), jax.ShapeDtypeStruct((t, KV_RANK), BF16),
                   jax.ShapeDtypeStruct((MLA_HEADS, t, HEAD_PAD), BF16),
                   jax.ShapeDtypeStruct((MLA_HEADS, t, HEAD_PAD), BF16),
                   jax.ShapeDtypeStruct((MLA_HEADS, t, 2 * V_DIM), BF16)],
        compiler_params=_cparams(),
    )


def _causal_mask(s):
    row = lax.broadcasted_iota(jnp.int32, s.shape, 0)
    col = lax.broadcasted_iota(jnp.int32, s.shape, 1)
    return jnp.where(col <= row, s, NEG_BIG)


def _attn_fwd(q, k, v):
    nh, t, _ = q.shape
    tq = tk = min(512, t)
    nq, nk = t // tq, t // tk

    def body(q_ref, k_ref, v_ref, o_ref, lse_ref, m_sc, acc_sc):
        i, j = pl.program_id(0), pl.program_id(1)

        @pl.when(j == 0)
        def _():
            m_sc[...] = jnp.full_like(m_sc, NEG_BIG)
            acc_sc[...] = jnp.zeros_like(acc_sc)

        def step(diagonal):
            for hh in range(nh):
                s = _dot_nt(q_ref[hh], k_ref[hh]) * ATTN_SCALE
                if diagonal:
                    s = _causal_mask(s)
                m_old = m_sc[hh]
                m_new = jnp.maximum(m_old, jnp.max(s, axis=-1, keepdims=True))
                p = jnp.exp(s - m_new).astype(BF16)
                acc_sc[hh] = jnp.exp(m_old - m_new) * acc_sc[hh] + _dot_nn(p, v_ref[hh])
                m_sc[hh] = m_new

        @pl.when(j < i)
        def _():
            step(False)

        @pl.when(j == i)
        def _():
            step(True)
            for hh in range(nh):
                acc = acc_sc[hh]
                l = acc[:, V_DIM:2 * V_DIM]
                o_ref[:, hh * V_DIM:(hh + 1) * V_DIM] = (acc[:, 0:V_DIM] / l).astype(BF16)
                lse_ref[hh] = m_sc[hh] + jnp.log(l[:, 0:1])

    kv_map = lambda i, j: (0, jnp.minimum(j, i), 0)
    return pl.pallas_call(
        body, name="attn_fwd", grid=(nq, nk),
        in_specs=[pl.BlockSpec((nh, tq, HEAD_PAD), lambda i, j: (0, i, 0)),
                  pl.BlockSpec((nh, tk, HEAD_PAD), kv_map),
                  pl.BlockSpec((nh, tk, 2 * V_DIM), kv_map)],
        out_specs=[pl.BlockSpec((tq, nh * V_DIM), lambda i, j: (i, 0)),
                   pl.BlockSpec((nh, tq, 1), lambda i, j: (0, i, 0))],
        out_shape=[jax.ShapeDtypeStruct((t, nh * V_DIM), BF16), jax.ShapeDtypeStruct((nh, t, 1), F32)],
        scratch_shapes=[pltpu.VMEM((nh, tq, 1), F32), pltpu.VMEM((nh, tq, 2 * V_DIM), F32)],
        compiler_params=_cparams(),
    )(q, k, v)


def _attn_delta(o, do):
    t, w = o.shape
    nh = w // V_DIM
    tm = min(512, t)

    def body(o_ref, do_ref, d_ref):
        prod = o_ref[...].astype(F32) * do_ref[...].astype(F32)
        for hh in range(nh):
            d_ref[hh] = jnp.sum(prod[:, hh * V_DIM:(hh + 1) * V_DIM], axis=-1, keepdims=True)

    return pl.pallas_call(
        body, name="attn_delta", grid=(t // tm,),
        in_specs=[pl.BlockSpec((tm, w), lambda i: (i, 0)), pl.BlockSpec((tm, w), lambda i: (i, 0))],
        out_specs=pl.BlockSpec((nh, tm, 1), lambda i: (0, i, 0)),
        out_shape=jax.ShapeDtypeStruct((nh, t, 1), F32),
        compiler_params=_cparams(),
    )(o, do)


ATTN_BWD_HEADS = 2


def _attn_bwd(q, k, v, do, lse, delta):
    nh, t, _ = q.shape
    hp = ATTN_BWD_HEADS
    tq = tk = min(512, t)
    nq, nk = t // tq, t // tk

    def body(q_ref, k_ref, v_ref, do_ref, lse_ref, dlt_ref, dq_ref, dk_ref, dv_ref):
        j, i = pl.program_id(1), pl.program_id(2)

        def step(diagonal):
            for hh in range(hp):
                qq, kk = q_ref[hh], k_ref[hh]
                dob = do_ref[:, hh * V_DIM:(hh + 1) * V_DIM]
                s = _dot_nt(qq, kk) * ATTN_SCALE
                if diagonal:
                    s = _causal_mask(s)
                p = jnp.exp(s - lse_ref[hh])
                dpp = _dot_nt(dob, v_ref[hh])
                dsb = (p * (dpp - dlt_ref[hh]) * ATTN_SCALE).astype(BF16)
                _accumulate(dv_ref.at[hh], _dot_tn(p.astype(BF16), dob), diagonal)
                _accumulate(dk_ref.at[hh], _dot_tn(dsb, qq), diagonal)
                _accumulate(dq_ref.at[hh, pl.ds(pl.multiple_of(i * tq, tq), tq), :], _dot_nn(dsb, kk), j == 0)

        @pl.when(i > j)
        def _():
            step(False)

        @pl.when(i == j)
        def _():
            step(True)

    qmap = lambda h, j, i: (h, jnp.maximum(i, j), 0)
    return pl.pallas_call(
        body, name="attn_bwd", grid=(nh // hp, nk, nq),
        in_specs=[pl.BlockSpec((hp, tq, HEAD_PAD), qmap),
                  pl.BlockSpec((hp, tk, HEAD_PAD), lambda h, j, i: (h, j, 0)),
                  pl.BlockSpec((hp, tk, V_DIM), lambda h, j, i: (h, j, 0)),
                  pl.BlockSpec((tq, hp * V_DIM), lambda h, j, i: (jnp.maximum(i, j), h)),
                  pl.BlockSpec((hp, tq, 1), qmap),
                  pl.BlockSpec((hp, tq, 1), qmap)],
        out_specs=[pl.BlockSpec((hp, t, HEAD_PAD), lambda h, j, i: (h, 0, 0)),
                   pl.BlockSpec((hp, tk, HEAD_PAD), lambda h, j, i: (h, j, 0)),
                   pl.BlockSpec((hp, tk, V_DIM), lambda h, j, i: (h, j, 0))],
        out_shape=[jax.ShapeDtypeStruct((nh, t, HEAD_PAD), F32), jax.ShapeDtypeStruct((nh, t, HEAD_PAD), F32),
                   jax.ShapeDtypeStruct((nh, t, V_DIM), F32)],
        compiler_params=_cparams(),
    )(q, k, v, do, lse, delta)


def _pool_counts(first_token, rows, w):
    tok = lax.broadcasted_iota(jnp.int32, (rows, POOL_CH), 0) + first_token
    return jnp.minimum(tok + 1, w).astype(F32)


def _pool_centered(zbuf, g, w, i, tm):
    lanes = pl.ds(g * POOL_CH, POOL_CH)
    cur = zbuf[pl.ds(POOL_HALO, tm), lanes]
    win = cur
    for s in range(1, w):
        win = win + zbuf[pl.ds(POOL_HALO - s, tm), lanes]
    return win / _pool_counts(i * tm, tm, w) - cur


def _pool_load(zbuf, z_ref, halo_ref, i, tm):
    @pl.when(i == 0)
    def _():
        zbuf[pl.ds(0, POOL_HALO), :] = jnp.zeros((POOL_HALO, zbuf.shape[1]), F32)

    @pl.when(i > 0)
    def _():
        zbuf[pl.ds(0, POOL_HALO), :] = halo_ref[...]

    zbuf[pl.ds(POOL_HALO, tm), :] = z_ref[...]


def _pool_fwd(z, pool_w, pool_scale):
    t = z.shape[0]
    pw = len(POOL_WINDOWS) * POOL_CH
    tm = min(512, t)
    hb = tm // POOL_HALO

    def body(z_ref, halo_ref, w_ref, sc_ref, p_ref, zbuf):
        i = pl.program_id(0)
        _pool_load(zbuf, z_ref, halo_ref, i, tm)
        for g, w in enumerate(POOL_WINDOWS):
            c = _pool_centered(zbuf, g, w, i, tm)
            y = _dot_nn(c.astype(BF16), w_ref[g]) * sc_ref[:, g * POOL_CH:(g + 1) * POOL_CH]
            p_ref[:, g * POOL_CH:(g + 1) * POOL_CH] = y.astype(BF16)

    return pl.pallas_call(
        body, name="pool_fwd", grid=(t // tm,),
        in_specs=[pl.BlockSpec((tm, pw), lambda i: (i, 1)),
                  pl.BlockSpec((POOL_HALO, pw), lambda i: (jnp.maximum(i * hb - 1, 0), 1)),
                  pl.BlockSpec(pool_w.shape, lambda i: (0, 0, 0)),
                  pl.BlockSpec((1, pw), lambda i: (0, 0))],
        out_specs=pl.BlockSpec((tm, pw), lambda i: (i, 0)),
        out_shape=jax.ShapeDtypeStruct((t, pw), BF16),
        scratch_shapes=[pltpu.VMEM((POOL_HALO + tm, pw), F32)],
        compiler_params=_cparams(),
    )(z, z, pool_w, pool_scale)


def _pool_bwd(dp, z, pool_w, pool_scale):
    t = z.shape[0]
    ng = len(POOL_WINDOWS)
    pw = ng * POOL_CH
    tm = min(512, t)
    hb = tm // POOL_HALO
    nt = t // tm

    def body(dp_ref, dpn_ref, z_ref, halo_ref, w_ref, sc_ref, dz_ref, dw_ref, dsc_ref, zbuf, dbuf):
        i = pl.program_id(0)
        _pool_load(zbuf, z_ref, halo_ref, i, tm)
        nxt_ok = (i < nt - 1).astype(F32)
        for g, w in enumerate(POOL_WINDOWS):
            lanes = pl.ds(g * POOL_CH, POOL_CH)
            cols = slice(g * POOL_CH, (g + 1) * POOL_CH)
            sc = sc_ref[:, cols]
            wg = w_ref[g]
            c = _pool_centered(zbuf, g, w, i, tm).astype(BF16)
            ypre = _dot_nn(c, wg)
            dpg = dp_ref[:, cols].astype(F32)
            _accumulate(dsc_ref.at[:, lanes], jnp.sum(dpg * ypre, axis=0, keepdims=True), i == 0)
            dyb = (dpg * sc).astype(BF16)
            _accumulate(dw_ref.at[g], _dot_tn(c, dyb), i == 0)
            dd = _dot_nt(dyb, wg)
            dyn = (dpn_ref[:, cols].astype(F32) * sc).astype(BF16)
            ddn = _dot_nt(dyn, wg) * nxt_ok
            dbuf[pl.ds(0, tm), lanes] = dd / _pool_counts(i * tm, tm, w)
            dbuf[pl.ds(tm, POOL_HALO), lanes] = ddn / _pool_counts((i + 1) * tm, POOL_HALO, w)
            acc = -dd
            for s in range(w):
                acc = acc + dbuf[pl.ds(s, tm), lanes]
            dz_ref[:, cols] = acc

    return pl.pallas_call(
        body, name="pool_bwd", grid=(nt,),
        in_specs=[pl.BlockSpec((tm, pw), lambda i: (i, 0)),
                  pl.BlockSpec((POOL_HALO, pw), lambda i: (jnp.minimum((i + 1) * hb, t // POOL_HALO - 1), 0)),
                  pl.BlockSpec((tm, pw), lambda i: (i, 1)),
                  pl.BlockSpec((POOL_HALO, pw), lambda i: (jnp.maximum(i * hb - 1, 0), 1)),
                  pl.BlockSpec(pool_w.shape, lambda i: (0, 0, 0)),
                  pl.BlockSpec((1, pw), lambda i: (0, 0))],
        out_specs=[pl.BlockSpec((tm, pw), lambda i: (i, 0)),
                   pl.BlockSpec((ng, POOL_CH, POOL_CH), lambda i: (0, 0, 0)),
                   pl.BlockSpec((1, pw), lambda i: (0, 0))],
        out_shape=[jax.ShapeDtypeStruct((t, pw), F32), jax.ShapeDtypeStruct((ng, POOL_CH, POOL_CH), F32),
                   jax.ShapeDtypeStruct((1, pw), F32)],
        scratch_shapes=[pltpu.VMEM((POOL_HALO + tm, pw), F32), pltpu.VMEM((tm + POOL_HALO, pw), F32)],
        compiler_params=_cparams(),
    )(dp, dp, z, z, pool_w, pool_scale)


def _mla_bwd(dq_h, dk_h, dv_h, z, dz_pool, h1, dh2, mix_norm, w_in, q_norm, wq_t, kv_norm, wkv, pos, rope_tab):
    t, d = h1.shape
    tm = min(256, t)

    def body(dqh_ref, dkh_ref, dvh_ref, z_ref, dzp_ref, h_ref, dh2_ref, gm_ref, win_ref, gq_ref, wq_ref, gkv_ref,
             wkv_ref, pos_ref, tab_ref, dh1_ref, dq_ref, dkv_ref, dz_ref, dgq_ref, dgkv_ref, dgm_ref):
        i = pl.program_id(0)
        first = i == 0
        cos_t, sin_t = _rope_tables(pos_ref[...], tab_ref[...])
        dq_parts, dkv_parts = [], []
        dk_pe = jnp.zeros((tm, 128), F32)
        for hh in range(MLA_HEADS):
            dqh = dqh_ref[hh]
            dq_parts += [dqh[:, 0:NOPE], _rope_apply_t(dqh[:, NOPE:HEAD_PAD], cos_t, sin_t)]
            dkh = dkh_ref[hh]
            dkv_parts += [dkh[:, 0:NOPE], dvh_ref[hh]]
            dk_pe = dk_pe + dkh[:, NOPE:HEAD_PAD]
        dqb = jnp.concatenate(dq_parts, axis=-1).astype(BF16)
        dkvb = jnp.concatenate(dkv_parts, axis=-1).astype(BF16)
        dq_ref[...] = dqb
        dkv_ref[...] = dkvb
        z = z_ref[...]
        c_q = z[:, 0:Q_RANK]
        gq = gq_ref[...]
        _, rq = _rms_fwd(c_q, gq)
        dcq, dgq = _rms_bwd(_dot_nn(dqb, wq_ref[...]), c_q, gq, rq)
        c_kv = z[:, Q_RANK:Q_RANK + KV_RANK]
        gkv = gkv_ref[...]
        _, rkv = _rms_fwd(c_kv, gkv)
        dckv, dgkv = _rms_bwd(_dot_nt(dkvb, wkv_ref[...]), c_kv, gkv, rkv)
        dkr = _rope_apply_t(dk_pe, cos_t, sin_t)
        dzb = jnp.concatenate([dcq, dckv, dkr, dzp_ref[...]], axis=-1).astype(BF16)
        dz_ref[...] = dzb
        x = h_ref[...]
        gm = gm_ref[...]
        _, rm = _rms_fwd(x, gm)
        dx, dgm = _rms_bwd(_dot_nt(dzb, win_ref[...]), x, gm, rm)
        dh1_ref[...] = dh2_ref[...] + dx
        _accumulate(dgq_ref, dgq, first)
        _accumulate(dgkv_ref, dgkv, first)
        _accumulate(dgm_ref, dgm, first)

    full = lambda shape: pl.BlockSpec(shape, lambda i: (0,) * len(shape))
    row = lambda w: pl.BlockSpec((tm, w), lambda i: (i, 0))
    head = lambda w: pl.BlockSpec((MLA_HEADS, tm, w), lambda i: (0, i, 0))
    pw = len(POOL_WINDOWS) * POOL_CH
    return pl.pallas_call(
        body, name="mla_bwd", grid=(t // tm,),
        in_specs=[head(HEAD_PAD), head(HEAD_PAD), head(V_DIM), row(d), row(pw), row(d), row(d),
                  full((1, d)), full(w_in.shape), full((1, Q_RANK)), full(wq_t.shape), full((1, KV_RANK)),
                  full(wkv.shape), row(1), full(rope_tab.shape)],
        out_specs=[row(d), row(d), row(d), row(d), full((1, Q_RANK)), full((1, KV_RANK)), full((1, d))],
        out_shape=[jax.ShapeDtypeStruct((t, d), F32), jax.ShapeDtypeStruct((t, d), BF16),
                   jax.ShapeDtypeStruct((t, d), BF16), jax.ShapeDtypeStruct((t, d), BF16),
                   jax.ShapeDtypeStruct((1, Q_RANK), F32), jax.ShapeDtypeStruct((1, KV_RANK), F32),
                   jax.ShapeDtypeStruct((1, d), F32)],
        compiler_params=_cparams(),
    )(dq_h, dk_h, dv_h, z, dz_pool, h1, dh2, mix_norm, w_in, q_norm, wq_t, kv_norm, wkv, pos, rope_tab)


def _mem_kv(mem, mem_norm, wmkv):
    n, d = mem.shape

    def body(mem_ref, g_ref, w_ref, memn_ref, k_ref, v_ref):
        y, _ = _rms_fwd(mem_ref[...], g_ref[...])
        yb = y.astype(BF16)
        memn_ref[...] = yb
        for hh in range(MEM_HEADS):
            k_ref[hh] = _dot_nn(yb, w_ref[hh]).astype(BF16)
            v_ref[hh] = _dot_nn(yb, w_ref[MEM_HEADS + hh]).astype(BF16)

    return pl.pallas_call(
        body, name="mem_kv",
        out_shape=[jax.ShapeDtypeStruct((n, d), BF16), jax.ShapeDtypeStruct((MEM_HEADS, n, MEM_HD), BF16),
                   jax.ShapeDtypeStruct((MEM_HEADS, n, MEM_HD), BF16)],
        compiler_params=_cparams(),
    )(mem, mem_norm, wmkv)


def _mem_softmax(qb, km):
    s = _dot_nt(qb, km) * MEM_SCALE
    e = jnp.exp(s - jnp.max(s, axis=-1, keepdims=True))
    return e / jnp.sum(e, axis=-1, keepdims=True)


def _xattn_fwd(h1, a, p, w_out, g, wmq, km, vm, wmo):
    t, d = h1.shape
    tm = min(512, t)
    half = a.shape[1]

    def body(h_ref, a_ref, p_ref, wo_ref, g_ref, wmq_ref, km_ref, vm_ref, wmo_ref,
             h2_ref, h3_ref, hn_ref, q_ref, o_ref):
        h2 = h_ref[...] + _dot_nn(a_ref[...], wo_ref[0:half, :]) + _dot_nn(p_ref[...], wo_ref[half:2 * half, :])
        h2_ref[...] = h2
        hn, _ = _rms_fwd(h2, g_ref[...])
        hnb = hn.astype(BF16)
        hn_ref[...] = hnb
        qb = _dot_nn(hnb, wmq_ref[...]).astype(BF16)
        q_ref[...] = qb
        outs = []
        for hh in range(MEM_HEADS):
            pr = _mem_softmax(qb[:, hh * MEM_HD:(hh + 1) * MEM_HD], km_ref[hh])
            outs.append(_dot_nn(pr.astype(BF16), vm_ref[hh]))
        ob = jnp.concatenate(outs, axis=-1).astype(BF16)
        o_ref[...] = ob
        h3_ref[...] = h2 + _dot_nn(ob, wmo_ref[...])

    full = lambda shape: pl.BlockSpec(shape, lambda i: (0,) * len(shape))
    row = lambda w: pl.BlockSpec((tm, w), lambda i: (i, 0))
    return pl.pallas_call(
        body, name="xattn_fwd", grid=(t // tm,),
        in_specs=[row(d), row(half), row(half), full(w_out.shape), full((1, d)), full(wmq.shape),
                  full(km.shape), full(vm.shape), full(wmo.shape)],
        out_specs=[row(d), row(d), row(d), row(d), row(d)],
        out_shape=[jax.ShapeDtypeStruct((t, d), F32), jax.ShapeDtypeStruct((t, d), F32),
                   jax.ShapeDtypeStruct((t, d), BF16), jax.ShapeDtypeStruct((t, d), BF16),
                   jax.ShapeDtypeStruct((t, d), BF16)],
        compiler_params=_cparams(),
    )(h1, a, p, w_out, g, wmq, km, vm, wmo)


def _xattn_bwd(dh3, h2, qm, g, wmq, km, vm, wmo, w_out, token=None):
    t, d = h2.shape
    tm = min(256, t)
    half = d // 2

    def body(dh3_ref, h2_ref, q_ref, g_ref, wmq_ref, km_ref, vm_ref, wmo_ref, wo_ref,
             dh2_ref, dq_ref, da_ref, dp_ref, dk_ref, dv_ref, dg_ref):
        i = pl.program_id(0)
        first = i == 0
        dh3 = dh3_ref[...]
        dob = _dot_nt(dh3.astype(BF16), wmo_ref[...]).astype(BF16)
        qb = q_ref[...]
        dq_parts = []
        for hh in range(MEM_HEADS):
            cols = slice(hh * MEM_HD, (hh + 1) * MEM_HD)
            kk, vv = km_ref[hh], vm_ref[hh]
            pr = _mem_softmax(qb[:, cols], kk)
            doh = dob[:, cols]
            _accumulate(dv_ref.at[hh], _dot_tn(pr.astype(BF16), doh), first)
            dpp = _dot_nt(doh, vv)
            dsb = (pr * (dpp - jnp.sum(dpp * pr, axis=-1, keepdims=True)) * MEM_SCALE).astype(BF16)
            dq_parts.append(_dot_nn(dsb, kk))
            _accumulate(dk_ref.at[hh], _dot_tn(dsb, qb[:, cols]), first)
        dqb = jnp.concatenate(dq_parts, axis=-1).astype(BF16)
        dq_ref[...] = dqb
        x = h2_ref[...]
        gg = g_ref[...]
        _, r = _rms_fwd(x, gg)
        dx, dg = _rms_bwd(_dot_nt(dqb, wmq_ref[...]), x, gg, r)
        dh2 = dh3 + dx
        dh2_ref[...] = dh2
        dap = _dot_nt(dh2.astype(BF16), wo_ref[...])
        da_ref[...] = dap[:, 0:half].astype(BF16)
        dp_ref[...] = dap[:, half:d].astype(BF16)
        _accumulate(dg_ref, dg, first)

    full = lambda shape: pl.BlockSpec(shape, lambda i: (0,) * len(shape))
    row = lambda w: pl.BlockSpec((tm, w), lambda i: (i, 0))
    return _call_after(
        token, body,
        [row(d), row(d), row(d), full((1, d)), full(wmq.shape), full(km.shape), full(vm.shape),
         full(wmo.shape), full(w_out.shape)],
        (dh3, h2, qm, g, wmq, km, vm, wmo, w_out),
        name="xattn_bwd", grid=(t // tm,),
        out_specs=[row(d), row(d), row(half), row(half), full(km.shape), full(vm.shape), full((1, d))],
        out_shape=[jax.ShapeDtypeStruct((t, d), F32), jax.ShapeDtypeStruct((t, d), BF16),
                   jax.ShapeDtypeStruct((t, half), BF16), jax.ShapeDtypeStruct((t, half), BF16),
                   jax.ShapeDtypeStruct(km.shape, F32), jax.ShapeDtypeStruct(vm.shape, F32),
                   jax.ShapeDtypeStruct((1, d), F32)],
        compiler_params=_cparams(),
    )


def _mem_kv_bwd(dkm, dvm, memn, mem, mem_norm, wmkv):
    n, d = mem.shape

    def body(dk_ref, dv_ref, memn_ref, mem_ref, g_ref, w_ref, dw_ref, dg_ref):
        memn = memn_ref[...]
        dmemn = jnp.zeros((n, d), F32)
        for s in range(2 * MEM_HEADS):
            src = dk_ref[s] if s < MEM_HEADS else dv_ref[s - MEM_HEADS]
            db = src.astype(BF16)
            dw_ref[s] = _dot_tn(memn, db)
            dmemn = dmemn + _dot_nt(db, w_ref[s])
        x = mem_ref[...]
        gg = g_ref[...]
        _, r = _rms_fwd(x, gg)
        _, dg = _rms_bwd(dmemn, x, gg, r)
        dg_ref[...] = dg

    return pl.pallas_call(
        body, name="mem_kv_bwd",
        out_shape=[jax.ShapeDtypeStruct(wmkv.shape, F32), jax.ShapeDtypeStruct((1, d), F32)],
        compiler_params=_cparams(),
    )(dkm, dvm, memn, mem, mem_norm, wmkv)


MESH_ID = pl.DeviceIdType.MESH
ANY = pl.BlockSpec(memory_space=pl.ANY)


def _coords():
    return lax.axis_index("x"), lax.axis_index("y"), lax.axis_index("c")


def _other_chips(x, y):
    return [(1 - x, y), (x, 1 - y), (1 - x, 1 - y)]


def _all_gather_rows(shard):
    r, w = shard.shape

    def body(x_ref, out_ref, send_sems, recv_sems, local_sem):
        x, y, c = _coords()
        me, sibling = (x, y, c), (x, y, 1 - c)
        chips = _other_chips(x, y)

        def block(px, py, pc):
            return out_ref.at[4 * px + 2 * py + pc]

        def copy(k, blk, to, src=None):
            return pltpu.make_async_remote_copy(
                src_ref=block(*blk) if src is None else src, dst_ref=block(*blk),
                send_sem=send_sems.at[k], recv_sem=recv_sems.at[k], device_id=to, device_id_type=MESH_ID)

        mine = pltpu.make_async_copy(x_ref, block(*me), local_sem)
        mine.start()
        first = [copy(0, me, sibling, src=x_ref)]
        first += [copy(1 + j, me, (*chip, c), src=x_ref) for j, chip in enumerate(chips)]
        for cp in first:
            cp.start()
        passed = [copy(4 + j, (*chip, c), sibling) for j, chip in enumerate(chips)]
        for j, chip in enumerate(chips):
            copy(1 + j, (*chip, c), me).wait_recv()
            passed[j].start()
        copy(0, sibling, me).wait_recv()
        for j, chip in enumerate(chips):
            copy(4 + j, (*chip, 1 - c), me).wait_recv()
        for cp in first + passed:
            cp.wait_send()
        mine.wait()

    return pl.pallas_call(
        body, name="all_gather_rows",
        out_shape=jax.ShapeDtypeStruct((N_DEV, r, w), shard.dtype),
        in_specs=[ANY], out_specs=ANY,
        scratch_shapes=[pltpu.SemaphoreType.DMA((7,)), pltpu.SemaphoreType.DMA((7,)), pltpu.SemaphoreType.DMA],
    )(shard)


def _all_gather_direct(shard):
    r, w = shard.shape

    def body(x_ref, out_ref, send_sems, recv_sems, local_sem):
        x, y, c = _coords()
        mine = pltpu.make_async_copy(x_ref, out_ref.at[4 * x + 2 * y + c], local_sem)
        mine.start()
        arrivals = []
        for k in range(1, N_DEV):
            px, py, pc = x ^ ((k >> 2) & 1), y ^ ((k >> 1) & 1), c ^ (k & 1)
            pltpu.make_async_remote_copy(
                src_ref=x_ref, dst_ref=out_ref.at[4 * x + 2 * y + c],
                send_sem=send_sems.at[k - 1], recv_sem=recv_sems.at[k - 1],
                device_id=(px, py, pc), device_id_type=MESH_ID).start()
            arrivals.append(pltpu.make_async_remote_copy(
                src_ref=x_ref, dst_ref=out_ref.at[4 * px + 2 * py + pc],
                send_sem=send_sems.at[k - 1], recv_sem=recv_sems.at[k - 1],
                device_id=(px, py, pc), device_id_type=MESH_ID))
        for cp in arrivals:
            cp.wait_recv()
        for cp in arrivals:
            cp.wait_send()
        mine.wait()

    return pl.pallas_call(
        body, name="all_gather_direct",
        out_shape=jax.ShapeDtypeStruct((N_DEV, r, w), shard.dtype),
        in_specs=[pl.BlockSpec(memory_space=pltpu.VMEM)], out_specs=pl.BlockSpec(memory_space=pltpu.VMEM),
        scratch_shapes=[pltpu.SemaphoreType.DMA((7,)), pltpu.SemaphoreType.DMA((7,)), pltpu.SemaphoreType.DMA],
        compiler_params=_cparams(),
    )(shard)


def _exchange_core(g, tag):
    _, r, w = g.shape

    def body(g_ref, land_ref, send_sems, recv_sems):
        x, y, c = _coords()
        copies = []
        for chip in range(4):
            copies.append(pltpu.make_async_remote_copy(
                src_ref=g_ref.at[2 * chip + (1 - c)], dst_ref=land_ref.at[chip],
                send_sem=send_sems.at[chip], recv_sem=recv_sems.at[chip],
                device_id=(x, y, 1 - c), device_id_type=MESH_ID))
        for cp in copies:
            cp.start()
        for cp in copies:
            cp.wait_recv()
        for cp in copies:
            cp.wait_send()

    return pl.pallas_call(
        body, name="exchange_core_" + tag,
        out_shape=jax.ShapeDtypeStruct((4, r, w), g.dtype),
        in_specs=[ANY], out_specs=ANY,
        scratch_shapes=[pltpu.SemaphoreType.DMA((4,)), pltpu.SemaphoreType.DMA((4,))],
    )(g)


def _chip_partial(g, land, cidx, tag):
    _, r, w = g.shape
    tr = _row_tile(r, 1024)
    g4 = g.reshape(4, 2, r, w)

    def body(c_ref, g_ref, l_ref, o_ref):
        o_ref[0] = (g_ref[0, 0].astype(F32) + l_ref[0].astype(F32)).astype(o_ref.dtype)

    return pl.pallas_call(
        body, name="chip_partial_" + tag,
        grid_spec=pltpu.PrefetchScalarGridSpec(
            num_scalar_prefetch=1, grid=(4, r // tr),
            in_specs=[pl.BlockSpec((1, 1, tr, w), lambda i, j, s: (i, s[0], j, 0)),
                      pl.BlockSpec((1, tr, w), lambda i, j, s: (i, j, 0))],
            out_specs=pl.BlockSpec((1, tr, w), lambda i, j, s: (i, j, 0))),
        out_shape=jax.ShapeDtypeStruct((4, r, w), g.dtype),
        compiler_params=_cparams(),
    )(cidx, g4, land)


def _exchange_chips(part):
    _, r, w = part.shape

    def body(p_ref, land_ref, send_sems, recv_sems):
        x, y, c = _coords()
        copies = []
        for j, (px, py) in enumerate(_other_chips(x, y)):
            copies.append(pltpu.make_async_remote_copy(
                src_ref=p_ref.at[2 * px + py], dst_ref=land_ref.at[j],
                send_sem=send_sems.at[j], recv_sem=recv_sems.at[j],
                device_id=(px, py, c), device_id_type=MESH_ID))
        for cp in copies:
            cp.start()
        for cp in copies:
            cp.wait_recv()
        for cp in copies:
            cp.wait_send()

    return pl.pallas_call(
        body, name="exchange_chips",
        out_shape=jax.ShapeDtypeStruct((3, r, w), part.dtype),
        in_specs=[ANY], out_specs=ANY,
        scratch_shapes=[pltpu.SemaphoreType.DMA((3,)), pltpu.SemaphoreType.DMA((3,))],
    )(part)


HBM_SPEC = pl.BlockSpec(memory_space=pltpu.HBM)
SEM_SPEC = pl.BlockSpec(memory_space=pltpu.SEMAPHORE)
SPLIT_EFFECT = pltpu.SideEffectType.DATAFLOW_SIDE_EFFECTING


def _ici_block(src_ref, px, py):
    return src_ref if len(src_ref.shape) == 2 else src_ref.at[2 * px + py]


def _ici_slot(gather, j, px, py, c):
    return 4 * px + 2 * py + c if gather else j


def _ici_start(src, after, name):
    r, w = src.shape[-2:]
    gather = len(src.shape) == 2
    land_shape = (N_DEV if gather else 3, r, w)

    def body(src_ref, land_ref, after_ref, send_sems, recv_sems, src_thru, land_thru, token):
        x, y, c = _coords()
        for j, (px, py) in enumerate(_other_chips(x, y)):
            pltpu.make_async_remote_copy(
                src_ref=_ici_block(src_ref, px, py), dst_ref=land_ref.at[_ici_slot(gather, j, x, y, c)],
                send_sem=send_sems.at[j], recv_sem=recv_sems.at[j],
                device_id=(px, py, c), device_id_type=MESH_ID).start()
        token[...] = jnp.zeros_like(token)

    return pl.pallas_call(
        body, name=name,
        out_shape=(pltpu.SemaphoreType.DMA((3,)), pltpu.SemaphoreType.DMA((3,)), pltpu.HBM(src.shape, src.dtype),
                   pltpu.HBM(land_shape, src.dtype), jax.ShapeDtypeStruct((8, 128), F32)),
        in_specs=(HBM_SPEC, HBM_SPEC, ANY),
        out_specs=(SEM_SPEC, SEM_SPEC, HBM_SPEC, HBM_SPEC, pl.BlockSpec(memory_space=pltpu.VMEM)),
        input_output_aliases={0: 2, 1: 3},
        compiler_params=pltpu.CompilerParams(has_side_effects=SPLIT_EFFECT),
    )(pltpu.with_memory_space_constraint(src, pltpu.HBM),
      pltpu.with_memory_space_constraint(lax.empty(land_shape, src.dtype), pltpu.HBM), after)


def _ici_wait(started, after, name):
    send_sems, recv_sems, src_thru, land_thru, _ = started
    gather = len(src_thru.shape) == 2

    def body(src_ref, land_ref, send_sems, recv_sems, after_ref, src_dead, got_ref):
        x, y, c = _coords()
        for j, (px, py) in enumerate(_other_chips(x, y)):
            copy = pltpu.make_async_remote_copy(
                src_ref=_ici_block(src_ref, px, py), dst_ref=land_ref.at[_ici_slot(gather, j, px, py, c)],
                send_sem=send_sems.at[j], recv_sem=recv_sems.at[j],
                device_id=(px, py, c), device_id_type=MESH_ID)
            copy.wait_send()
            copy.wait_recv()

    return pl.pallas_call(
        body, name=name,
        out_shape=(pltpu.HBM(src_thru.shape, src_thru.dtype), pltpu.HBM(land_thru.shape, land_thru.dtype)),
        in_specs=(HBM_SPEC, HBM_SPEC, SEM_SPEC, SEM_SPEC, ANY),
        out_specs=(HBM_SPEC, HBM_SPEC), input_output_aliases={0: 0, 1: 1},
        compiler_params=pltpu.CompilerParams(has_side_effects=SPLIT_EFFECT),
    )(src_thru, land_thru, send_sems, recv_sems, after)


def _core_share(own, gathered, name):
    r, w = own.shape

    def body(own_ref, gin_ref, out_ref, stage, send_sems, recv_sems, local_sem):
        x, y, c = _coords()
        sibling = (x, y, 1 - c)
        chips = [(x, y)] + _other_chips(x, y)
        stage_in = pltpu.make_async_copy(own_ref, stage, local_sem)
        stage_in.start()
        sent, arriving = [], []
        for k, (px, py) in enumerate(chips):
            rows = out_ref.at[4 * px + 2 * py + c]
            sent.append(pltpu.make_async_remote_copy(
                src_ref=own_ref if k == 0 else rows, dst_ref=rows,
                send_sem=send_sems.at[k], recv_sem=recv_sems.at[k], device_id=sibling, device_id_type=MESH_ID))
            arriving.append(pltpu.make_async_remote_copy(
                src_ref=own_ref, dst_ref=out_ref.at[4 * px + 2 * py + (1 - c)],
                send_sem=send_sems.at[k], recv_sem=recv_sems.at[k], device_id=sibling, device_id_type=MESH_ID))
        for cp in sent:
            cp.start()
        stage_in.wait()
        stage_out = pltpu.make_async_copy(stage, out_ref.at[4 * x + 2 * y + c], local_sem)
        stage_out.start()
        for cp in arriving:
            cp.wait_recv()
        for cp in sent:
            cp.wait_send()
        stage_out.wait()

    return pl.pallas_call(
        body, name=name,
        out_shape=jax.ShapeDtypeStruct((N_DEV, r, w), own.dtype),
        in_specs=[ANY, ANY], out_specs=ANY, input_output_aliases={1: 0},
        scratch_shapes=[pltpu.VMEM((r, w), own.dtype), pltpu.SemaphoreType.DMA((4,)),
                        pltpu.SemaphoreType.DMA((4,)), pltpu.SemaphoreType.DMA],
    )(own, gathered)


def _adamw(w, g, m, v):
    m = ADAM_B1 * m + (1.0 - ADAM_B1) * g
    v = ADAM_B2 * v + (1.0 - ADAM_B2) * (g * g)
    m_hat = m / ADAM_C1
    v_hat = v / ADAM_C2
    delta = -ADAM_LR * (m_hat / (jnp.sqrt(v_hat) + ADAM_EPS) + ADAM_WD * w)
    return delta, m, v


def _adam_big(part, land, w, m, v, chip_idx, tag):
    r, wd = w.shape
    tr, tw = _row_tile(r, 1024), 256

    def body(s_ref, p_ref, l_ref, w_ref, m_ref, v_ref, g_ref, d_ref, mo_ref, vo_ref):
        g = p_ref[0].astype(F32)
        for j in range(3):
            g = g + l_ref[j].astype(F32)
        delta, mn, vn = _adamw(w_ref[...], g, m_ref[...], v_ref[...])
        g_ref[...] = g
        d_ref[...] = delta
        mo_ref[...] = mn
        vo_ref[...] = vn

    row = pl.BlockSpec((tr, tw), lambda i, j, s: (i, j))
    return pl.pallas_call(
        body, name="adam_big_" + tag,
        grid_spec=pltpu.PrefetchScalarGridSpec(
            num_scalar_prefetch=1, grid=(r // tr, wd // tw),
            in_specs=[pl.BlockSpec((1, tr, tw), lambda i, j, s: (s[0], i, j)),
                      pl.BlockSpec((3, tr, tw), lambda i, j, s: (0, i, j)), row, row, row],
            out_specs=[row, row, row, row]),
        out_shape=[jax.ShapeDtypeStruct((r, wd), F32)] * 4,
        compiler_params=_cparams(),
    )(chip_idx, part, land, w, m, v)


def _adam_small(parts, w, m, v):
    _, r, wd = parts.shape

    def body(p_ref, w_ref, m_ref, v_ref, g_ref, d_ref, mo_ref, vo_ref):
        g = p_ref[0]
        for k in range(1, N_DEV):
            g = g + p_ref[k]
        delta, mn, vn = _adamw(w_ref[...], g, m_ref[...], v_ref[...])
        g_ref[...] = g
        d_ref[...] = delta
        mo_ref[...] = mn
        vo_ref[...] = vn

    return pl.pallas_call(
        body, name="adam_small",
        out_shape=[jax.ShapeDtypeStruct((r, wd), F32)] * 4,
        compiler_params=_cparams(),
    )(parts, w, m, v)


def _pad_rows(a, rows):
    return jnp.pad(a, ((0, rows - a.shape[0]), (0, 0)))


def _pad_w_in(w):
    cut = Q_RANK + KV_RANK + ROPE
    return jnp.concatenate([w[:, :cut], jnp.zeros((w.shape[0], 64), w.dtype), w[:, cut:]], axis=1)


def _unpad_w_in(w):
    cut = Q_RANK + KV_RANK + ROPE
    return jnp.concatenate([w[:, :cut], w[:, cut + 64:]], axis=1)


def _pack_big(p, group):
    if group == "mid":
        parts = [_pad_w_in(p["w_in"][0]), p["w_out"][0], p["w_mq"][0], p["w_mo"][0],
                 p["w_mkv"][0].reshape(256, D_MODEL),
                 _pad_rows(p["w_q_up"][0].T.reshape(24, D_MODEL), 32),
                 p["w_kv_up"][0].reshape(16, D_MODEL)]
    else:
        parts = [p[group + "_w_gate"][0].T, p[group + "_w_up"][0].T, p[group + "_w_down"][0]]
    return jnp.concatenate(parts, axis=0)


def _unpack_big(a, group):
    seg = lambda n: a[SEG_OFF[n][0]:SEG_OFF[n][0] + SEG_OFF[n][1]]
    if group == "mid":
        return {"w_in": _unpad_w_in(seg("w_in"))[None], "w_out": seg("w_out")[None], "w_mq": seg("w_mq")[None],
                "w_mo": seg("w_mo")[None], "w_mkv": seg("w_mkv").reshape(D_MODEL, 256)[None],
                "w_q_up": seg("w_q")[:24].reshape(96, Q_RANK).T[None],
                "w_kv_up": seg("w_kv").reshape(KV_RANK, 128)[None]}
    return {group + "_w_gate": seg(group + "_g").T[None], group + "_w_up": seg(group + "_u").T[None],
            group + "_w_down": seg(group + "_d")[None]}


def _unpack_gathered(full, group):
    seg = lambda n: full[:, SEG_OFF[n][0]:SEG_OFF[n][0] + SEG_OFF[n][1]]
    rows = lambda n: seg(n).reshape(-1, D_MODEL)
    if group != "mid":
        return {n: rows(n) for n, _ in GROUP_SEGS[group]}
    wq_t = seg("w_q")[:, :24].reshape(MLA_HEADS, NOPE + ROPE, Q_RANK)
    wq_t = jnp.pad(wq_t, ((0, 0), (0, HEAD_PAD - NOPE - ROPE), (0, 0))).reshape(MLA_HEADS * HEAD_PAD, Q_RANK)
    wkv = seg("w_kv").reshape(N_DEV, KV_RANK, 128).transpose(1, 0, 2).reshape(KV_RANK, N_DEV * 128)
    return {"w_in": rows("w_in"), "w_out": rows("w_out"), "w_mq": rows("w_mq"), "w_mo": rows("w_mo"),
            "w_mkv": seg("w_mkv").reshape(N_DEV, D_MODEL, 256), "w_q": wq_t, "w_kv": wkv}


def _pack_grads(gr, group):
    blk = lambda a: a.reshape(N_DEV, -1, D_MODEL)
    if group == "mid":
        dwq = gr["w_q"].reshape(MLA_HEADS, HEAD_PAD, Q_RANK)[:, :NOPE + ROPE].reshape(N_DEV, 24, D_MODEL)
        dwq = jnp.pad(dwq, ((0, 0), (0, 8), (0, 0)))
        dwkv = gr["w_kv"].reshape(KV_RANK, N_DEV, 128).transpose(1, 0, 2).reshape(N_DEV, 16, D_MODEL)
        parts = [blk(gr["w_in"]), blk(gr["w_out"]), blk(gr["w_mq"]), blk(gr["w_mo"]),
                 gr["w_mkv"].reshape(N_DEV, 256, D_MODEL), dwq, dwkv]
    else:
        parts = [blk(gr[n]) for n, _ in GROUP_SEGS[group]]
    return jnp.concatenate([a.astype(BF16) for a in parts], axis=1)


def _pack_small(vals):
    parts = []
    for n, r in SMALL_ROWS:
        parts.append(_pad_rows(vals[n].reshape(-1, 128), r) if n in vals else jnp.zeros((r, 128), F32))
    return jnp.concatenate(parts, axis=0)


def _unpack_small(a, shapes):
    out = {}
    for n, shape in shapes.items():
        o = SMALL_OFF[n][0]
        out[n] = a[o:o + int(np.prod(shape)) // 128].reshape(shape)
    return out


BIG_NAMES = ("ffn1_w_gate", "ffn1_w_up", "ffn1_w_down", "w_in", "w_q_up", "w_kv_up", "w_out", "w_mq", "w_mkv",
             "w_mo", "ffn2_w_gate", "ffn2_w_up", "ffn2_w_down")
SMALL_NAMES = ("ffn1_norm", "mix_norm", "q_norm", "kv_norm", "pool_w", "pool_scale", "xattn_norm", "mem_norm",
               "ffn2_norm", "final_norm")
WEIGHT_ORDER = ("ffn1_norm", "ffn1_w_gate", "ffn1_w_up", "ffn1_w_down", "mix_norm", "w_in", "q_norm", "w_q_up",
                "kv_norm", "w_kv_up", "pool_w", "pool_scale", "w_out", "xattn_norm", "mem_norm", "w_mq", "w_mkv",
                "w_mo", "ffn2_norm", "ffn2_w_gate", "ffn2_w_up", "ffn2_w_down", "final_norm")


def _rope_table():
    lane = np.arange(128)
    freqs = (1.0 / (ROPE_BASE ** (np.arange(0, ROPE, 2, dtype=np.float32) / ROPE))).astype(np.float32)
    tab = np.zeros((8, 128), np.float32)
    tab[0] = np.where(lane < ROPE, freqs[lane % (ROPE // 2)], 0.0)
    tab[1] = np.where(lane < ROPE // 2, -1.0, np.where(lane < ROPE, 1.0, 0.0))
    return jnp.asarray(tab)


def kernel(x, mem, positions, ffn1_norm, ffn1_w_gate, ffn1_w_up, ffn1_w_down, mix_norm, w_in, q_norm, w_q_up, kv_norm, w_kv_up, pool_w, pool_scale, w_out, xattn_norm, mem_norm, w_mq, w_mkv, w_mo, ffn2_norm, ffn2_w_gate, ffn2_w_up, ffn2_w_down, final_norm, loss_target, m_ffn1_norm, m_ffn1_w_gate, m_ffn1_w_up, m_ffn1_w_down, m_mix_norm, m_w_in, m_q_norm, m_w_q_up, m_kv_norm, m_w_kv_up, m_pool_w, m_pool_scale, m_w_out, m_xattn_norm, m_mem_norm, m_w_mq, m_w_mkv, m_w_mo, m_ffn2_norm, m_ffn2_w_gate, m_ffn2_w_up, m_ffn2_w_down, m_final_norm, v_ffn1_norm, v_ffn1_w_gate, v_ffn1_w_up, v_ffn1_w_down, v_mix_norm, v_w_in, v_q_norm, v_w_q_up, v_kv_norm, v_w_kv_up, v_pool_w, v_pool_scale, v_w_out, v_xattn_norm, v_mem_norm, v_w_mq, v_w_mkv, v_w_mo, v_ffn2_norm, v_ffn2_w_gate, v_ffn2_w_up, v_ffn2_w_down, v_final_norm):
    wts = dict(ffn1_norm=ffn1_norm, ffn1_w_gate=ffn1_w_gate, ffn1_w_up=ffn1_w_up, ffn1_w_down=ffn1_w_down,
               mix_norm=mix_norm, w_in=w_in, q_norm=q_norm, w_q_up=w_q_up, kv_norm=kv_norm, w_kv_up=w_kv_up,
               pool_w=pool_w, pool_scale=pool_scale, w_out=w_out, xattn_norm=xattn_norm, mem_norm=mem_norm,
               w_mq=w_mq, w_mkv=w_mkv, w_mo=w_mo, ffn2_norm=ffn2_norm, ffn2_w_gate=ffn2_w_gate,
               ffn2_w_up=ffn2_w_up, ffn2_w_down=ffn2_w_down, final_norm=final_norm)
    mom = dict(ffn1_norm=m_ffn1_norm, ffn1_w_gate=m_ffn1_w_gate, ffn1_w_up=m_ffn1_w_up, ffn1_w_down=m_ffn1_w_down,
               mix_norm=m_mix_norm, w_in=m_w_in, q_norm=m_q_norm, w_q_up=m_w_q_up, kv_norm=m_kv_norm,
               w_kv_up=m_w_kv_up, pool_w=m_pool_w, pool_scale=m_pool_scale, w_out=m_w_out, xattn_norm=m_xattn_norm,
               mem_norm=m_mem_norm, w_mq=m_w_mq, w_mkv=m_w_mkv, w_mo=m_w_mo, ffn2_norm=m_ffn2_norm,
               ffn2_w_gate=m_ffn2_w_gate, ffn2_w_up=m_ffn2_w_up, ffn2_w_down=m_ffn2_w_down, final_norm=m_final_norm)
    var = dict(ffn1_norm=v_ffn1_norm, ffn1_w_gate=v_ffn1_w_gate, ffn1_w_up=v_ffn1_w_up, ffn1_w_down=v_ffn1_w_down,
               mix_norm=v_mix_norm, w_in=v_w_in, q_norm=v_q_norm, w_q_up=v_w_q_up, kv_norm=v_kv_norm,
               w_kv_up=v_w_kv_up, pool_w=v_pool_w, pool_scale=v_pool_scale, w_out=v_w_out, xattn_norm=v_xattn_norm,
               mem_norm=v_mem_norm, w_mq=v_w_mq, w_mkv=v_w_mkv, w_mo=v_w_mo, ffn2_norm=v_ffn2_norm,
               ffn2_w_gate=v_ffn2_w_gate, ffn2_w_up=v_ffn2_w_up, ffn2_w_down=v_ffn2_w_down, final_norm=v_final_norm)

    t = x.shape[1]
    xs = x[0]
    mems = mem[0]
    target = loss_target[0]
    pos = positions.reshape(t, 1)
    row = lambda a: a.reshape(1, -1)
    rope_tab = _rope_table()

    cx, cy, cc = _coords()
    core_idx = cc.astype(jnp.int32).reshape(1)
    chip_idx = (2 * cx + cy).astype(jnp.int32).reshape(1)

    w_pack = {g: _pack_big(wts, g) for g in GROUP_SEGS}
    wb = {g: w_pack[g].astype(BF16) for g in GROUP_SEGS}
    full_ffn1 = _all_gather_rows(wb["ffn1"])
    fw = _unpack_gathered(full_ffn1, "ffn1")
    ag_mid = _ici_start(wb["mid"], full_ffn1, "ag_mid_start")
    g_ffn1, g_mix, g_q, g_kv = row(ffn1_norm), row(mix_norm), row(q_norm), row(kv_norm)
    g_x, g_mem, g_ffn2, g_fin = row(xattn_norm), row(mem_norm), row(ffn2_norm), row(final_norm)
    pool_wb = pool_w[0].astype(BF16)
    pool_sc = row(pool_scale)

    h1, n1, gate1, up1 = _ffn_fwd(xs, g_ffn1, fw["ffn1_g"], fw["ffn1_u"], fw["ffn1_d"], "ffn1_fwd", token=ag_mid[4])
    own_mid, land_mid = _ici_wait(ag_mid, h1, "ag_mid_wait")
    full_mid = _core_share(own_mid, land_mid, "ag_mid_share")
    fw.update(_unpack_gathered(full_mid, "mid"))
    ag_ffn2 = _ici_start(wb["ffn2"], full_mid, "ag_ffn2_start")
    u, z, qn, kvn, qh, kh, vh = _mix_prep(h1, g_mix, fw["w_in"], g_q, fw["w_q"], g_kv, fw["w_kv"], pos, rope_tab,
                                          token=ag_ffn2[4])
    a, lse = _attn_fwd(qh, kh, vh)
    p = _pool_fwd(z, pool_wb, pool_sc)
    memn, km, vm = _mem_kv(mems, g_mem, fw["w_mkv"])
    h2, h3, hn, qm, om = _xattn_fwd(h1, a, p, fw["w_out"], g_x, fw["w_mq"], km, vm, fw["w_mo"])
    own_ffn2, land_ffn2 = _ici_wait(ag_ffn2, h3, "ag_ffn2_wait")
    fw.update(_unpack_gathered(_core_share(own_ffn2, land_ffn2, "ag_ffn2_share"), "ffn2"))
    h4, n2, gate2, up2 = _ffn_fwd(h3, g_ffn2, fw["ffn2_g"], fw["ffn2_u"], fw["ffn2_d"], "ffn2_fwd")
    loss_part, dh4, dg_fin = _loss_head(h4, target, g_fin)

    gr = {}
    dh3, dgate2, dup2, act2, dg_ffn2 = _ffn_bwd_data(dh4, h3, g_ffn2, gate2, up2, fw["ffn2_g"], fw["ffn2_u"],
                                                     fw["ffn2_d"], "ffn2_bwd")
    gr["ffn2_g"] = _tn_matmul(dgate2, n2, "ffn2_dwg", tmm=1408)
    gr["ffn2_u"] = _tn_matmul(dup2, n2, "ffn2_dwu", tmm=1408)
    gr["ffn2_d"] = _tn_matmul(act2, dh4, "ffn2_dwd", scale=0.5, tmm=1408)
    g_ffn2_pack = _pack_grads(gr, "ffn2")
    part_ffn2 = _chip_partial(g_ffn2_pack, _exchange_core(g_ffn2_pack, "ffn2"), core_idx, "ffn2")
    rs_ffn2 = _ici_start(part_ffn2, g_ffn2_pack, "rs_ffn2_start")
    dh2, dqm, da, dp, dkm, dvm, dg_x = _xattn_bwd(dh3, h2, qm, g_x, fw["w_mq"], km, vm, fw["w_mo"], fw["w_out"],
                                                  token=rs_ffn2[4])
    gr["w_mo"] = _tn_matmul(om, dh3, "dw_mo", tmm=512)
    gr["w_mq"] = _tn_matmul(hn, dqm, "dw_mq", tmm=512)
    gr["w_out"] = jnp.concatenate([_tn_matmul(a, dh2, "dw_out_a"), _tn_matmul(p, dh2, "dw_out_p")], axis=0)
    gr["w_mkv"], dg_mem = _mem_kv_bwd(dkm, dvm, memn, mems, g_mem, fw["w_mkv"])
    dz_pool, d_pool_w, d_pool_sc = _pool_bwd(dp, z, pool_wb, pool_sc)
    dqh, dkh, dvh = _attn_bwd(qh, kh, vh, da, lse, _attn_delta(a, da))
    dh1, dq, dkv, dz, dg_q, dg_kv, dg_mix = _mla_bwd(dqh, dkh, dvh, z, dz_pool, h1, dh2, g_mix, fw["w_in"], g_q,
                                                     fw["w_q"], g_kv, fw["w_kv"], pos, rope_tab)
    gr["w_q"] = _tn_matmul(dq, qn, "dw_q", tmm=512)
    gr["w_kv"] = _tn_matmul(kvn, dkv, "dw_kv")
    gr["w_in"] = _tn_matmul(u, dz, "dw_in", tmm=512)
    g_mid_pack = _pack_grads(gr, "mid")
    part_mid = _chip_partial(g_mid_pack, _exchange_core(g_mid_pack, "mid"), core_idx, "mid")
    part_ffn2, land_ffn2_g = _ici_wait(rs_ffn2, part_mid, "rs_ffn2_wait")
    rs_mid = _ici_start(part_mid, land_ffn2_g, "rs_mid_start")
    dx, dgate1, dup1, act1, dg_ffn1 = _ffn_bwd_data(dh1, xs, g_ffn1, gate1, up1, fw["ffn1_g"], fw["ffn1_u"],
                                                    fw["ffn1_d"], "ffn1_bwd", token=rs_mid[4])
    part_mid, land_mid_g = _ici_wait(rs_mid, dx, "rs_mid_wait")
    gr["ffn1_g"] = _tn_matmul(dgate1, n1, "ffn1_dwg", tmm=1408)
    gr["ffn1_u"] = _tn_matmul(dup1, n1, "ffn1_dwu", tmm=1408)
    gr["ffn1_d"] = _tn_matmul(act1, dh1, "ffn1_dwd", scale=0.5, tmm=1408)
    g_ffn1_pack = _pack_grads(gr, "ffn1")
    part_ffn1 = _chip_partial(g_ffn1_pack, _exchange_core(g_ffn1_pack, "ffn1"), core_idx, "ffn1")
    land_ffn1_g = _exchange_chips(part_ffn1)

    big = {}
    for grp, part, land in (("ffn1", part_ffn1, land_ffn1_g), ("mid", part_mid, land_mid_g),
                            ("ffn2", part_ffn2, land_ffn2_g)):
        res = _adam_big(part, land, w_pack[grp], _pack_big(mom, grp), _pack_big(var, grp), chip_idx, grp)
        for k, packed in enumerate(res):
            big.setdefault(k, {}).update(_unpack_big(packed, grp))

    small_g = dict(ffn1_norm=dg_ffn1, mix_norm=dg_mix, q_norm=dg_q, kv_norm=dg_kv, pool_w=d_pool_w,
                   pool_scale=d_pool_sc, xattn_norm=dg_x, mem_norm=dg_mem, ffn2_norm=dg_ffn2, final_norm=dg_fin,
                   loss=loss_part)
    parts = _all_gather_direct(_pack_small(small_g))
    small = _adam_small(parts, _pack_small({n: wts[n] for n in SMALL_NAMES}),
                        _pack_small({n: mom[n] for n in SMALL_NAMES}), _pack_small({n: var[n] for n in SMALL_NAMES}))
    loss = small[0][SMALL_OFF["loss"][0], 0]
    shapes = {n: wts[n].shape for n in SMALL_NAMES}
    small = [_unpack_small(s, shapes) for s in small]

    outs = [loss, dx[None]]
    for k in range(4):
        for n in WEIGHT_ORDER:
            outs.append(big[k][n] if n in BIG_NAMES else small[k][n])
    return tuple(outs)
```

```python
import numpy as np

import jax
import jax.numpy as jnp
from jax import lax
from jax.experimental import pallas as pl
from jax.experimental.pallas import tpu as pltpu

F32 = jnp.float32
BF16 = jnp.bfloat16

N_DEV = 8
D_MODEL = 1024
D_FF = 2816
MLA_HEADS = 4
NOPE = 128
ROPE = 64
HEAD_PAD = 256
V_DIM = 128
Q_RANK = 256
KV_RANK = 128
POOL_WINDOWS = (2, 4, 8, 16)
POOL_CH = 128
POOL_HALO = 16
N_MEM = 256
MEM_HEADS = 4
MEM_HD = 256
ROPE_BASE = 10000.0
RMS_EPS = 1e-6
ATTN_SCALE = (NOPE + ROPE) ** -0.5
MEM_SCALE = MEM_HD ** -0.5
NEG_BIG = -1e30

ADAM_LR = 0.001
ADAM_B1 = 0.9
ADAM_B2 = 0.999
ADAM_EPS = 1e-08
ADAM_WD = 0.01
ADAM_STEP = 10
ADAM_C1 = 1.0 - ADAM_B1 ** ADAM_STEP
ADAM_C2 = 1.0 - ADAM_B2 ** ADAM_STEP

VMEM_LIMIT_BYTES = 52 * 1024 * 1024
BF16_ROWS = 16

GROUP_SEGS = {
    "ffn1": (("ffn1_g", 352), ("ffn1_u", 352), ("ffn1_d", 352)),
    "mid": (("w_in", 128), ("w_out", 128), ("w_mq", 128), ("w_mo", 128), ("w_mkv", 256), ("w_q", 32), ("w_kv", 16)),
    "ffn2": (("ffn2_g", 352), ("ffn2_u", 352), ("ffn2_d", 352)),
}
SEG_OFF = {}
GROUP_ROWS = {}
for _g, _segs in GROUP_SEGS.items():
    _o = 0
    for _n, _r in _segs:
        SEG_OFF[_n] = (_o, _r)
        _o += _r
    GROUP_ROWS[_g] = _o

SMALL_ROWS = (("ffn1_norm", 8), ("mix_norm", 8), ("q_norm", 8), ("kv_norm", 8), ("pool_w", 512), ("pool_scale", 8),
              ("xattn_norm", 8), ("mem_norm", 8), ("ffn2_norm", 8), ("final_norm", 8), ("loss", 8))
SMALL_OFF = {}
_o = 0
for _n, _r in SMALL_ROWS:
    SMALL_OFF[_n] = (_o, _r)
    _o += _r


def _cparams(**kw):
    return pltpu.CompilerParams(vmem_limit_bytes=VMEM_LIMIT_BYTES, **kw)


def _row_tile(rows, limit):
    best = None
    for cand in range(BF16_ROWS, min(rows, limit) + 1, BF16_ROWS):
        if rows % cand == 0:
            best = cand
    assert best is not None, rows
    return best


def _dot_nn(a, b):
    return lax.dot_general(a, b, (((1,), (0,)), ((), ())), preferred_element_type=F32)


def _dot_nt(a, b):
    return lax.dot_general(a, b, (((1,), (1,)), ((), ())), preferred_element_type=F32)


def _dot_tn(a, b):
    return lax.dot_general(a, b, (((0,), (0,)), ((), ())), preferred_element_type=F32)


def _rms_fwd(x, g):
    r = lax.rsqrt(jnp.mean(x * x, axis=-1, keepdims=True) + RMS_EPS)
    return x * r * g, r


def _rms_bwd(dy, x, g, r):
    xhat = x * r
    dyg = dy * g
    dx = r * (dyg - xhat * jnp.mean(dyg * xhat, axis=-1, keepdims=True))
    dg = jnp.sum(dy * xhat, axis=0, keepdims=True)
    return dx, dg


def _accumulate(ref, val, first):
    if isinstance(first, bool):
        if first:
            ref[...] = val
        else:
            ref[...] += val
        return

    @pl.when(first)
    def _():
        ref[...] = val

    @pl.when(jnp.logical_not(first))
    def _():
        ref[...] += val


def _call_after(token, body, in_specs, args, **kw):
    if token is not None:
        inner = body
        body = lambda tok_ref, *refs: inner(*refs)
        in_specs = [pl.BlockSpec((8, 128), lambda *_: (0, 0))] + list(in_specs)
        args = (token,) + tuple(args)
    return pl.pallas_call(body, in_specs=in_specs, **kw)(*args)


def _rope_tables(pos_col, tab):
    ang = pos_col.astype(F32) * tab[0:1, :]
    return jnp.cos(ang), jnp.sin(ang) * tab[1:2, :]


def _swap_halves(x):
    lane = lax.broadcasted_iota(jnp.int32, x.shape, 1)
    return jnp.where((lane % 64) < 32, pltpu.roll(x, 96, 1), pltpu.roll(x, 32, 1))


def _rope_apply(x, cos_t, sin_t):
    return x * cos_t + _swap_halves(x) * sin_t


def _rope_apply_t(dy, cos_t, sin_t):
    return dy * cos_t + _swap_halves(dy * sin_t)


def _ffn_fwd(h, g, wg_t, wu_t, wd, name, token=None):
    t, d = h.shape
    f = wg_t.shape[0]
    tm, tf = min(1024, t), 256
    nf = f // tf

    def body(h_ref, g_ref, wg_ref, wu_ref, wd_ref, ho_ref, n_ref, gate_ref, up_ref, nb_sc, acc_sc):
        j = pl.program_id(1)

        @pl.when(j == 0)
        def _():
            y, _ = _rms_fwd(h_ref[...], g_ref[...])
            nb = y.astype(BF16)
            nb_sc[...] = nb
            n_ref[...] = nb
            acc_sc[...] = jnp.zeros_like(acc_sc)

        nb = nb_sc[...]
        gt = _dot_nt(nb, wg_ref[...])
        ut = _dot_nt(nb, wu_ref[...])
        gate_ref[...] = gt.astype(BF16)
        up_ref[...] = ut.astype(BF16)
        act = (gt * jax.nn.sigmoid(gt)) * ut
        acc_sc[...] += _dot_nn(act.astype(BF16), wd_ref[...])

        @pl.when(j == nf - 1)
        def _():
            ho_ref[...] = h_ref[...] + 0.5 * acc_sc[...]

    return _call_after(
        token, body,
        [pl.BlockSpec((tm, d), lambda i, j: (i, 0)),
         pl.BlockSpec((1, d), lambda i, j: (0, 0)),
         pl.BlockSpec((tf, d), lambda i, j: (j, 0)),
         pl.BlockSpec((tf, d), lambda i, j: (j, 0)),
         pl.BlockSpec((tf, d), lambda i, j: (j, 0))],
        (h, g, wg_t, wu_t, wd),
        name=name, grid=(t // tm, nf),
        out_specs=[pl.BlockSpec((tm, d), lambda i, j: (i, 0)),
                   pl.BlockSpec((tm, d), lambda i, j: (i, 0)),
                   pl.BlockSpec((tm, tf), lambda i, j: (i, j)),
                   pl.BlockSpec((tm, tf), lambda i, j: (i, j))],
        out_shape=[jax.ShapeDtypeStruct((t, d), F32), jax.ShapeDtypeStruct((t, d), BF16),
                   jax.ShapeDtypeStruct((t, f), BF16), jax.ShapeDtypeStruct((t, f), BF16)],
        scratch_shapes=[pltpu.VMEM((tm, d), BF16), pltpu.VMEM((tm, d), F32)],
        compiler_params=_cparams(),
    )


def _ffn_bwd_data(dho, h, g, gate, up, wg_t, wu_t, wd, name, token=None):
    t, d = h.shape
    f = wg_t.shape[0]
    tm, tf = min(1024, t), 256
    nf = f // tf

    def body(dho_ref, h_ref, g_ref, gate_ref, up_ref, wg_ref, wu_ref, wd_ref,
             dh_ref, dgate_ref, dup_ref, act_ref, dg_ref, dhb_sc, acc_sc):
        i, j = pl.program_id(0), pl.program_id(1)

        @pl.when(j == 0)
        def _():
            dhb_sc[...] = (0.5 * dho_ref[...]).astype(BF16)
            acc_sc[...] = jnp.zeros_like(acc_sc)

        dact = _dot_nt(dhb_sc[...], wd_ref[...])
        gt = gate_ref[...].astype(F32)
        ut = up_ref[...].astype(F32)
        sg = jax.nn.sigmoid(gt)
        silu = gt * sg
        dgb = (dact * ut * (sg * (1.0 + gt * (1.0 - sg)))).astype(BF16)
        dub = (dact * silu).astype(BF16)
        act_ref[...] = (silu * ut).astype(BF16)
        dgate_ref[...] = dgb
        dup_ref[...] = dub
        acc_sc[...] += _dot_nn(dgb, wg_ref[...]) + _dot_nn(dub, wu_ref[...])

        @pl.when(j == nf - 1)
        def _():
            x = h_ref[...]
            gg = g_ref[...]
            _, r = _rms_fwd(x, gg)
            dx, dg = _rms_bwd(acc_sc[...], x, gg, r)
            dh_ref[...] = dho_ref[...] + dx
            _accumulate(dg_ref, dg, i == 0)

    return _call_after(
        token, body,
        [pl.BlockSpec((tm, d), lambda i, j: (i, 0)),
         pl.BlockSpec((tm, d), lambda i, j: (i, 0)),
         pl.BlockSpec((1, d), lambda i, j: (0, 0)),
         pl.BlockSpec((tm, tf), lambda i, j: (i, j)),
         pl.BlockSpec((tm, tf), lambda i, j: (i, j)),
         pl.BlockSpec((tf, d), lambda i, j: (j, 0)),
         pl.BlockSpec((tf, d), lambda i, j: (j, 0)),
         pl.BlockSpec((tf, d), lambda i, j: (j, 0))],
        (dho, h, g, gate, up, wg_t, wu_t, wd),
        name=name, grid=(t // tm, nf),
        out_specs=[pl.BlockSpec((tm, d), lambda i, j: (i, 0)),
                   pl.BlockSpec((tm, tf), lambda i, j: (i, j)),
                   pl.BlockSpec((tm, tf), lambda i, j: (i, j)),
                   pl.BlockSpec((tm, tf), lambda i, j: (i, j)),
                   pl.BlockSpec((1, d), lambda i, j: (0, 0))],
        out_shape=[jax.ShapeDtypeStruct((t, d), F32), jax.ShapeDtypeStruct((t, f), BF16),
                   jax.ShapeDtypeStruct((t, f), BF16), jax.ShapeDtypeStruct((t, f), BF16),
                   jax.ShapeDtypeStruct((1, d), F32)],
        scratch_shapes=[pltpu.VMEM((tm, d), BF16), pltpu.VMEM((tm, d), F32)],
        compiler_params=_cparams(),
    )


def _tn_matmul(a, b, name, scale=1.0, tmm=None):
    t, m = a.shape
    n = b.shape[1]
    tmm = m if tmm is None else tmm
    tk = min(1024, t)

    def body(a_ref, b_ref, o_ref):
        k = pl.program_id(1)
        prod = _dot_tn(a_ref[...].astype(BF16), b_ref[...].astype(BF16))
        if scale != 1.0:
            prod = prod * scale
        _accumulate(o_ref, prod, k == 0)

    return pl.pallas_call(
        body, name=name, grid=(m // tmm, t // tk),
        in_specs=[pl.BlockSpec((tk, tmm), lambda i, k: (k, i)),
                  pl.BlockSpec((tk, n), lambda i, k: (k, 0))],
        out_specs=pl.BlockSpec((tmm, n), lambda i, k: (i, 0)),
        out_shape=jax.ShapeDtypeStruct((m, n), F32),
        compiler_params=_cparams(),
    )(a, b)


def _loss_head(h, target, g):
    t, d = h.shape
    tm = min(512, t)

    def body(h_ref, t_ref, g_ref, loss_ref, dh_ref, dg_ref):
        i = pl.program_id(0)
        x = h_ref[...]
        gg = g_ref[...]
        y, r = _rms_fwd(x, gg)
        err = y - t_ref[...]
        part = 0.5 * jnp.sum(jnp.mean(err * err, axis=-1, keepdims=True), axis=0, keepdims=True)
        dx, dg = _rms_bwd(err * (1.0 / d), x, gg, r)
        dh_ref[...] = dx
        _accumulate(loss_ref, jnp.broadcast_to(part, loss_ref.shape), i == 0)
        _accumulate(dg_ref, dg, i == 0)

    return pl.pallas_call(
        body, name="loss_head", grid=(t // tm,),
        in_specs=[pl.BlockSpec((tm, d), lambda i: (i, 0)),
                  pl.BlockSpec((tm, d), lambda i: (i, 0)),
                  pl.BlockSpec((1, d), lambda i: (0, 0))],
        out_specs=[pl.BlockSpec((8, 128), lambda i: (0, 0)),
                   pl.BlockSpec((tm, d), lambda i: (i, 0)),
                   pl.BlockSpec((1, d), lambda i: (0, 0))],
        out_shape=[jax.ShapeDtypeStruct((8, 128), F32), jax.ShapeDtypeStruct((t, d), F32),
                   jax.ShapeDtypeStruct((1, d), F32)],
        compiler_params=_cparams(),
    )(h, target, g)


def _mix_prep(h1, mix_norm, w_in, q_norm, wq_t, kv_norm, wkv, pos, rope_tab, token=None):
    t, d = h1.shape
    tm = min(512, t)

    def body(h_ref, gm_ref, win_ref, gq_ref, wq_ref, gkv_ref, wkv_ref, pos_ref, tab_ref,
             u_ref, z_ref, qn_ref, kvn_ref, q_ref, k_ref, v_ref):
        u, _ = _rms_fwd(h_ref[...], gm_ref[...])
        ub = u.astype(BF16)
        u_ref[...] = ub
        z = _dot_nn(ub, win_ref[...])
        z_ref[...] = z
        cos_t, sin_t = _rope_tables(pos_ref[...], tab_ref[...])
        qn, _ = _rms_fwd(z[:, 0:Q_RANK], gq_ref[...])
        qnb = qn.astype(BF16)
        qn_ref[...] = qnb
        q = _dot_nt(qnb, wq_ref[...])
        kvn, _ = _rms_fwd(z[:, Q_RANK:Q_RANK + KV_RANK], gkv_ref[...])
        kvnb = kvn.astype(BF16)
        kvn_ref[...] = kvnb
        kv = _dot_nn(kvnb, wkv_ref[...])
        k_pe = _rope_apply(z[:, Q_RANK + KV_RANK:Q_RANK + KV_RANK + 128], cos_t, sin_t)
        ones = jnp.ones((tm, V_DIM), F32)
        for hh in range(MLA_HEADS):
            b = hh * HEAD_PAD
            q_pe = _rope_apply(q[:, b + NOPE:b + HEAD_PAD], cos_t, sin_t)
            q_ref[hh] = jnp.concatenate([q[:, b:b + NOPE], q_pe], axis=-1).astype(BF16)
            k_ref[hh] = jnp.concatenate([kv[:, b:b + NOPE], k_pe], axis=-1).astype(BF16)
            v_ref[hh] = jnp.concatenate([kv[:, b + NOPE:b + HEAD_PAD], ones], axis=-1).astype(BF16)

    full = lambda shape: pl.BlockSpec(shape, lambda i: (0,) * len(shape))
    return _call_after(
        token, body,
        [pl.BlockSpec((tm, d), lambda i: (i, 0)), full((1, d)), full(w_in.shape), full((1, Q_RANK)),
         full(wq_t.shape), full((1, KV_RANK)), full(wkv.shape),
         pl.BlockSpec((tm, 1), lambda i: (i, 0)), full(rope_tab.shape)],
        (h1, mix_norm, w_in, q_norm, wq_t, kv_norm, wkv, pos, rope_tab),
        name="mix_prep", grid=(t // tm,),
        out_specs=[pl.BlockSpec((tm, d), lambda i: (i, 0)),
                   pl.BlockSpec((tm, d), lambda i: (i, 0)),
                   pl.BlockSpec((tm, Q_RANK), lambda i: (i, 0)),
                   pl.BlockSpec((tm, KV_RANK), lambda i: (i, 0)),
                   pl.BlockSpec((MLA_HEADS, tm, HEAD_PAD), lambda i: (0, i, 0)),
                   pl.BlockSpec((MLA_HEADS, tm, HEAD_PAD), lambda i: (0, i, 0)),
                   pl.BlockSpec((MLA_HEADS, tm, 2 * V_DIM), lambda i: (0, i, 0))],
        out_shape=[jax.ShapeDtypeStruct((t, d), BF16), jax.ShapeDtypeStruct((t, d), F32),
                   jax.ShapeDtypeStruct((t, Q_RANK), BF16), jax.ShapeDtypeStruct((t, KV_RANK), BF16),
                   jax.ShapeDtypeStruct((MLA_HEADS, t, HEAD_PAD), BF16),
                   jax.ShapeDtypeStruct((MLA_HEADS, t, HEAD_PAD), BF16),
                   jax.ShapeDtypeStruct((MLA_HEADS, t, 2 * V_DIM), BF16)],
        compiler_params=_cparams(),
    )


def _causal_mask(s):
    row = lax.broadcasted_iota(jnp.int32, s.shape, 0)
    col = lax.broadcasted_iota(jnp.int32, s.shape, 1)
    return jnp.where(col <= row, s, NEG_BIG)


def _attn_fwd(q, k, v):
    nh, t, _ = q.shape
    tq = tk = min(512, t)
    nq, nk = t // tq, t // tk

    def body(q_ref, k_ref, v_ref, o_ref, lse_ref, m_sc, acc_sc):
        i, j = pl.program_id(0), pl.program_id(1)

        @pl.when(j == 0)
        def _():
            m_sc[...] = jnp.full_like(m_sc, NEG_BIG)
            acc_sc[...] = jnp.zeros_like(acc_sc)

        def step(diagonal):
            for hh in range(nh):
                s = _dot_nt(q_ref[hh], k_ref[hh]) * ATTN_SCALE
                if diagonal:
                    s = _causal_mask(s)
                m_old = m_sc[hh]
                m_new = jnp.maximum(m_old, jnp.max(s, axis=-1, keepdims=True))
                p = jnp.exp(s - m_new).astype(BF16)
                acc_sc[hh] = jnp.exp(m_old - m_new) * acc_sc[hh] + _dot_nn(p, v_ref[hh])
                m_sc[hh] = m_new

        @pl.when(j < i)
        def _():
            step(False)

        @pl.when(j == i)
        def _():
            step(True)
            for hh in range(nh):
                acc = acc_sc[hh]
                l = acc[:, V_DIM:2 * V_DIM]
                o_ref[:, hh * V_DIM:(hh + 1) * V_DIM] = (acc[:, 0:V_DIM] / l).astype(BF16)
                lse_ref[hh] = m_sc[hh] + jnp.log(l[:, 0:1])

    kv_map = lambda i, j: (0, jnp.minimum(j, i), 0)
    return pl.pallas_call(
        body, name="attn_fwd", grid=(nq, nk),
        in_specs=[pl.BlockSpec((nh, tq, HEAD_PAD), lambda i, j: (0, i, 0)),
                  pl.BlockSpec((nh, tk, HEAD_PAD), kv_map),
                  pl.BlockSpec((nh, tk, 2 * V_DIM), kv_map)],
        out_specs=[pl.BlockSpec((tq, nh * V_DIM), lambda i, j: (i, 0)),
                   pl.BlockSpec((nh, tq, 1), lambda i, j: (0, i, 0))],
        out_shape=[jax.ShapeDtypeStruct((t, nh * V_DIM), BF16), jax.ShapeDtypeStruct((nh, t, 1), F32)],
        scratch_shapes=[pltpu.VMEM((nh, tq, 1), F32), pltpu.VMEM((nh, tq, 2 * V_DIM), F32)],
        compiler_params=_cparams(),
    )(q, k, v)


def _attn_delta(o, do):
    t, w = o.shape
    nh = w // V_DIM
    tm = min(512, t)

    def body(o_ref, do_ref, d_ref):
        prod = o_ref[...].astype(F32) * do_ref[...].astype(F32)
        for hh in range(nh):
            d_ref[hh] = jnp.sum(prod[:, hh * V_DIM:(hh + 1) * V_DIM], axis=-1, keepdims=True)

    return pl.pallas_call(
        body, name="attn_delta", grid=(t // tm,),
        in_specs=[pl.BlockSpec((tm, w), lambda i: (i, 0)), pl.BlockSpec((tm, w), lambda i: (i, 0))],
        out_specs=pl.BlockSpec((nh, tm, 1), lambda i: (0, i, 0)),
        out_shape=jax.ShapeDtypeStruct((nh, t, 1), F32),
        compiler_params=_cparams(),
    )(o, do)


ATTN_BWD_HEADS = 2


def _attn_bwd(q, k, v, do, lse, delta):
    nh, t, _ = q.shape
    hp = ATTN_BWD_HEADS
    tq = tk = min(512, t)
    nq, nk = t // tq, t // tk

    def body(q_ref, k_ref, v_ref, do_ref, lse_ref, dlt_ref, dq_ref, dk_ref, dv_ref):
        j, i = pl.program_id(1), pl.program_id(2)

        @pl.when(jnp.logical_and(j == 0, i == 0))
        def _():
            dq_ref[...] = jnp.zeros_like(dq_ref)

        def step(diagonal):
            for hh in range(hp):
                qq, kk = q_ref[hh], k_ref[hh]
                dob = do_ref[:, hh * V_DIM:(hh + 1) * V_DIM]
                s = _dot_nt(qq, kk) * ATTN_SCALE
                if diagonal:
                    s = _causal_mask(s)
                p = jnp.exp(s - lse_ref[hh])
                dpp = _dot_nt(dob, v_ref[hh])
                dsb = (p * (dpp - dlt_ref[hh]) * ATTN_SCALE).astype(BF16)
                _accumulate(dv_ref.at[hh], _dot_tn(p.astype(BF16), dob), diagonal)
                _accumulate(dk_ref.at[hh], _dot_tn(dsb, qq), diagonal)
                dq_ref[hh, pl.ds(pl.multiple_of(i * tq, tq), tq), :] += _dot_nn(dsb, kk)

        @pl.when(i > j)
        def _():
            step(False)

        @pl.when(i == j)
        def _():
            step(True)

    qmap = lambda h, j, i: (h, jnp.maximum(i, j), 0)
    return pl.pallas_call(
        body, name="attn_bwd", grid=(nh // hp, nk, nq),
        in_specs=[pl.BlockSpec((hp, tq, HEAD_PAD), qmap),
                  pl.BlockSpec((hp, tk, HEAD_PAD), lambda h, j, i: (h, j, 0)),
                  pl.BlockSpec((hp, tk, V_DIM), lambda h, j, i: (h, j, 0)),
                  pl.BlockSpec((tq, hp * V_DIM), lambda h, j, i: (jnp.maximum(i, j), h)),
                  pl.BlockSpec((hp, tq, 1), qmap),
                  pl.BlockSpec((hp, tq, 1), qmap)],
        out_specs=[pl.BlockSpec((hp, t, HEAD_PAD), lambda h, j, i: (h, 0, 0)),
                   pl.BlockSpec((hp, tk, HEAD_PAD), lambda h, j, i: (h, j, 0)),
                   pl.BlockSpec((hp, tk, V_DIM), lambda h, j, i: (h, j, 0))],
        out_shape=[jax.ShapeDtypeStruct((nh, t, HEAD_PAD), F32), jax.ShapeDtypeStruct((nh, t, HEAD_PAD), F32),
                   jax.ShapeDtypeStruct((nh, t, V_DIM), F32)],
        compiler_params=_cparams(),
    )(q, k, v, do, lse, delta)


def _pool_counts(first_token, rows, w):
    tok = lax.broadcasted_iota(jnp.int32, (rows, POOL_CH), 0) + first_token
    return jnp.minimum(tok + 1, w).astype(F32)


def _pool_centered(zbuf, g, w, i, tm):
    lanes = pl.ds(g * POOL_CH, POOL_CH)
    cur = zbuf[pl.ds(POOL_HALO, tm), lanes]
    win = cur
    for s in range(1, w):
        win = win + zbuf[pl.ds(POOL_HALO - s, tm), lanes]
    return win / _pool_counts(i * tm, tm, w) - cur


def _pool_load(zbuf, z_ref, halo_ref, i, tm):
    @pl.when(i == 0)
    def _():
        zbuf[pl.ds(0, POOL_HALO), :] = jnp.zeros((POOL_HALO, zbuf.shape[1]), F32)

    @pl.when(i > 0)
    def _():
        zbuf[pl.ds(0, POOL_HALO), :] = halo_ref[...]

    zbuf[pl.ds(POOL_HALO, tm), :] = z_ref[...]


def _pool_fwd(z, pool_w, pool_scale):
    t = z.shape[0]
    pw = len(POOL_WINDOWS) * POOL_CH
    tm = min(512, t)
    hb = tm // POOL_HALO

    def body(z_ref, halo_ref, w_ref, sc_ref, p_ref, zbuf):
        i = pl.program_id(0)
        _pool_load(zbuf, z_ref, halo_ref, i, tm)
        for g, w in enumerate(POOL_WINDOWS):
            c = _pool_centered(zbuf, g, w, i, tm)
            y = _dot_nn(c.astype(BF16), w_ref[g]) * sc_ref[:, g * POOL_CH:(g + 1) * POOL_CH]
            p_ref[:, g * POOL_CH:(g + 1) * POOL_CH] = y.astype(BF16)

    return pl.pallas_call(
        body, name="pool_fwd", grid=(t // tm,),
        in_specs=[pl.BlockSpec((tm, pw), lambda i: (i, 1)),
                  pl.BlockSpec((POOL_HALO, pw), lambda i: (jnp.maximum(i * hb - 1, 0), 1)),
                  pl.BlockSpec(pool_w.shape, lambda i: (0, 0, 0)),
                  pl.BlockSpec((1, pw), lambda i: (0, 0))],
        out_specs=pl.BlockSpec((tm, pw), lambda i: (i, 0)),
        out_shape=jax.ShapeDtypeStruct((t, pw), BF16),
        scratch_shapes=[pltpu.VMEM((POOL_HALO + tm, pw), F32)],
        compiler_params=_cparams(),
    )(z, z, pool_w, pool_scale)


def _pool_bwd(dp, z, pool_w, pool_scale):
    t = z.shape[0]
    ng = len(POOL_WINDOWS)
    pw = ng * POOL_CH
    tm = min(512, t)
    hb = tm // POOL_HALO
    nt = t // tm

    def body(dp_ref, dpn_ref, z_ref, halo_ref, w_ref, sc_ref, dz_ref, dw_ref, dsc_ref, zbuf, dbuf):
        i = pl.program_id(0)
        _pool_load(zbuf, z_ref, halo_ref, i, tm)

        @pl.when(i == 0)
        def _():
            dw_ref[...] = jnp.zeros_like(dw_ref)
            dsc_ref[...] = jnp.zeros_like(dsc_ref)

        nxt_ok = (i < nt - 1).astype(F32)
        for g, w in enumerate(POOL_WINDOWS):
            lanes = pl.ds(g * POOL_CH, POOL_CH)
            cols = slice(g * POOL_CH, (g + 1) * POOL_CH)
            sc = sc_ref[:, cols]
            wg = w_ref[g]
            c = _pool_centered(zbuf, g, w, i, tm).astype(BF16)
            ypre = _dot_nn(c, wg)
            dpg = dp_ref[:, cols].astype(F32)
            dsc_ref[:, cols] += jnp.sum(dpg * ypre, axis=0, keepdims=True)
            dyb = (dpg * sc).astype(BF16)
            dw_ref[g] += _dot_tn(c, dyb)
            dd = _dot_nt(dyb, wg)
            dyn = (dpn_ref[:, cols].astype(F32) * sc).astype(BF16)
            ddn = _dot_nt(dyn, wg) * nxt_ok
            dbuf[pl.ds(0, tm), lanes] = dd / _pool_counts(i * tm, tm, w)
            dbuf[pl.ds(tm, POOL_HALO), lanes] = ddn / _pool_counts((i + 1) * tm, POOL_HALO, w)
            acc = -dd
            for s in range(w):
                acc = acc + dbuf[pl.ds(s, tm), lanes]
            dz_ref[:, cols] = acc

    return pl.pallas_call(
        body, name="pool_bwd", grid=(nt,),
        in_specs=[pl.BlockSpec((tm, pw), lambda i: (i, 0)),
                  pl.BlockSpec((POOL_HALO, pw), lambda i: (jnp.minimum((i + 1) * hb, t // POOL_HALO - 1), 0)),
                  pl.BlockSpec((tm, pw), lambda i: (i, 1)),
                  pl.BlockSpec((POOL_HALO, pw), lambda i: (jnp.maximum(i * hb - 1, 0), 1)),
                  pl.BlockSpec(pool_w.shape, lambda i: (0, 0, 0)),
                  pl.BlockSpec((1, pw), lambda i: (0, 0))],
        out_specs=[pl.BlockSpec((tm, pw), lambda i: (i, 0)),
                   pl.BlockSpec((ng, POOL_CH, POOL_CH), lambda i: (0, 0, 0)),
                   pl.BlockSpec((1, pw), lambda i: (0, 0))],
        out_shape=[jax.ShapeDtypeStruct((t, pw), F32), jax.ShapeDtypeStruct((ng, POOL_CH, POOL_CH), F32),
                   jax.ShapeDtypeStruct((1, pw), F32)],
        scratch_shapes=[pltpu.VMEM((POOL_HALO + tm, pw), F32), pltpu.VMEM((tm + POOL_HALO, pw), F32)],
        compiler_params=_cparams(),
    )(dp, dp, z, z, pool_w, pool_scale)


def _mla_bwd(dq_h, dk_h, dv_h, z, dz_pool, h1, dh2, mix_norm, w_in, q_norm, wq_t, kv_norm, wkv, pos, rope_tab):
    t, d = h1.shape
    tm = min(256, t)

    def body(dqh_ref, dkh_ref, dvh_ref, z_ref, dzp_ref, h_ref, dh2_ref, gm_ref, win_ref, gq_ref, wq_ref, gkv_ref,
             wkv_ref, pos_ref, tab_ref, dh1_ref, dq_ref, dkv_ref, dz_ref, dgq_ref, dgkv_ref, dgm_ref):
        i = pl.program_id(0)
        first = i == 0
        cos_t, sin_t = _rope_tables(pos_ref[...], tab_ref[...])
        dq_parts, dkv_parts = [], []
        dk_pe = jnp.zeros((tm, 128), F32)
        for hh in range(MLA_HEADS):
            dqh = dqh_ref[hh]
            dq_parts += [dqh[:, 0:NOPE], _rope_apply_t(dqh[:, NOPE:HEAD_PAD], cos_t, sin_t)]
            dkh = dkh_ref[hh]
            dkv_parts += [dkh[:, 0:NOPE], dvh_ref[hh]]
            dk_pe = dk_pe + dkh[:, NOPE:HEAD_PAD]
        dqb = jnp.concatenate(dq_parts, axis=-1).astype(BF16)
        dkvb = jnp.concatenate(dkv_parts, axis=-1).astype(BF16)
        dq_ref[...] = dqb
        dkv_ref[...] = dkvb
        z = z_ref[...]
        c_q = z[:, 0:Q_RANK]
        gq = gq_ref[...]
        _, rq = _rms_fwd(c_q, gq)
        dcq, dgq = _rms_bwd(_dot_nn(dqb, wq_ref[...]), c_q, gq, rq)
        c_kv = z[:, Q_RANK:Q_RANK + KV_RANK]
        gkv = gkv_ref[...]
        _, rkv = _rms_fwd(c_kv, gkv)
        dckv, dgkv = _rms_bwd(_dot_nt(dkvb, wkv_ref[...]), c_kv, gkv, rkv)
        dkr = _rope_apply_t(dk_pe, cos_t, sin_t)
        dzb = jnp.concatenate([dcq, dckv, dkr, dzp_ref[...]], axis=-1).astype(BF16)
        dz_ref[...] = dzb
        x = h_ref[...]
        gm = gm_ref[...]
        _, rm = _rms_fwd(x, gm)
        dx, dgm = _rms_bwd(_dot_nt(dzb, win_ref[...]), x, gm, rm)
        dh1_ref[...] = dh2_ref[...] + dx
        _accumulate(dgq_ref, dgq, first)
        _accumulate(dgkv_ref, dgkv, first)
        _accumulate(dgm_ref, dgm, first)

    full = lambda shape: pl.BlockSpec(shape, lambda i: (0,) * len(shape))
    row = lambda w: pl.BlockSpec((tm, w), lambda i: (i, 0))
    head = lambda w: pl.BlockSpec((MLA_HEADS, tm, w), lambda i: (0, i, 0))
    pw = len(POOL_WINDOWS) * POOL_CH
    return pl.pallas_call(
        body, name="mla_bwd", grid=(t // tm,),
        in_specs=[head(HEAD_PAD), head(HEAD_PAD), head(V_DIM), row(d), row(pw), row(d), row(d),
                  full((1, d)), full(w_in.shape), full((1, Q_RANK)), full(wq_t.shape), full((1, KV_RANK)),
                  full(wkv.shape), row(1), full(rope_tab.shape)],
        out_specs=[row(d), row(d), row(d), row(d), full((1, Q_RANK)), full((1, KV_RANK)), full((1, d))],
        out_shape=[jax.ShapeDtypeStruct((t, d), F32), jax.ShapeDtypeStruct((t, d), BF16),
                   jax.ShapeDtypeStruct((t, d), BF16), jax.ShapeDtypeStruct((t, d), BF16),
                   jax.ShapeDtypeStruct((1, Q_RANK), F32), jax.ShapeDtypeStruct((1, KV_RANK), F32),
                   jax.ShapeDtypeStruct((1, d), F32)],
        compiler_params=_cparams(),
    )(dq_h, dk_h, dv_h, z, dz_pool, h1, dh2, mix_norm, w_in, q_norm, wq_t, kv_norm, wkv, pos, rope_tab)


def _mem_kv(mem, mem_norm, wmkv):
    n, d = mem.shape

    def body(mem_ref, g_ref, w_ref, memn_ref, k_ref, v_ref):
        y, _ = _rms_fwd(mem_ref[...], g_ref[...])
        yb = y.astype(BF16)
        memn_ref[...] = yb
        for hh in range(MEM_HEADS):
            k_ref[hh] = _dot_nn(yb, w_ref[hh]).astype(BF16)
            v_ref[hh] = _dot_nn(yb, w_ref[MEM_HEADS + hh]).astype(BF16)

    return pl.pallas_call(
        body, name="mem_kv",
        out_shape=[jax.ShapeDtypeStruct((n, d), BF16), jax.ShapeDtypeStruct((MEM_HEADS, n, MEM_HD), BF16),
                   jax.ShapeDtypeStruct((MEM_HEADS, n, MEM_HD), BF16)],
        compiler_params=_cparams(),
    )(mem, mem_norm, wmkv)


def _mem_softmax(qb, km):
    s = _dot_nt(qb, km) * MEM_SCALE
    e = jnp.exp(s - jnp.max(s, axis=-1, keepdims=True))
    return e / jnp.sum(e, axis=-1, keepdims=True)


def _xattn_fwd(h1, a, p, w_out, g, wmq, km, vm, wmo):
    t, d = h1.shape
    tm = min(512, t)
    half = a.shape[1]

    def body(h_ref, a_ref, p_ref, wo_ref, g_ref, wmq_ref, km_ref, vm_ref, wmo_ref,
             h2_ref, h3_ref, hn_ref, q_ref, o_ref):
        h2 = h_ref[...] + _dot_nn(a_ref[...], wo_ref[0:half, :]) + _dot_nn(p_ref[...], wo_ref[half:2 * half, :])
        h2_ref[...] = h2
        hn, _ = _rms_fwd(h2, g_ref[...])
        hnb = hn.astype(BF16)
        hn_ref[...] = hnb
        qb = _dot_nn(hnb, wmq_ref[...]).astype(BF16)
        q_ref[...] = qb
        outs = []
        for hh in range(MEM_HEADS):
            pr = _mem_softmax(qb[:, hh * MEM_HD:(hh + 1) * MEM_HD], km_ref[hh])
            outs.append(_dot_nn(pr.astype(BF16), vm_ref[hh]))
        ob = jnp.concatenate(outs, axis=-1).astype(BF16)
        o_ref[...] = ob
        h3_ref[...] = h2 + _dot_nn(ob, wmo_ref[...])

    full = lambda shape: pl.BlockSpec(shape, lambda i: (0,) * len(shape))
    row = lambda w: pl.BlockSpec((tm, w), lambda i: (i, 0))
    return pl.pallas_call(
        body, name="xattn_fwd", grid=(t // tm,),
        in_specs=[row(d), row(half), row(half), full(w_out.shape), full((1, d)), full(wmq.shape),
                  full(km.shape), full(vm.shape), full(wmo.shape)],
        out_specs=[row(d), row(d), row(d), row(d), row(d)],
        out_shape=[jax.ShapeDtypeStruct((t, d), F32), jax.ShapeDtypeStruct((t, d), F32),
                   jax.ShapeDtypeStruct((t, d), BF16), jax.ShapeDtypeStruct((t, d), BF16),
                   jax.ShapeDtypeStruct((t, d), BF16)],
        compiler_params=_cparams(),
    )(h1, a, p, w_out, g, wmq, km, vm, wmo)


def _xattn_bwd(dh3, h2, qm, g, wmq, km, vm, wmo, w_out, token=None):
    t, d = h2.shape
    tm = min(256, t)
    half = d // 2

    def body(dh3_ref, h2_ref, q_ref, g_ref, wmq_ref, km_ref, vm_ref, wmo_ref, wo_ref,
             dh2_ref, dq_ref, da_ref, dp_ref, dk_ref, dv_ref, dg_ref):
        i = pl.program_id(0)
        first = i == 0

        @pl.when(first)
        def _():
            dk_ref[...] = jnp.zeros_like(dk_ref)
            dv_ref[...] = jnp.zeros_like(dv_ref)

        dh3 = dh3_ref[...]
        dob = _dot_nt(dh3.astype(BF16), wmo_ref[...]).astype(BF16)
        qb = q_ref[...]
        dq_parts = []
        for hh in range(MEM_HEADS):
            cols = slice(hh * MEM_HD, (hh + 1) * MEM_HD)
            kk, vv = km_ref[hh], vm_ref[hh]
            pr = _mem_softmax(qb[:, cols], kk)
            doh = dob[:, cols]
            dv_ref[hh] += _dot_tn(pr.astype(BF16), doh)
            dpp = _dot_nt(doh, vv)
            dsb = (pr * (dpp - jnp.sum(dpp * pr, axis=-1, keepdims=True)) * MEM_SCALE).astype(BF16)
            dq_parts.append(_dot_nn(dsb, kk))
            dk_ref[hh] += _dot_tn(dsb, qb[:, cols])
        dqb = jnp.concatenate(dq_parts, axis=-1).astype(BF16)
        dq_ref[...] = dqb
        x = h2_ref[...]
        gg = g_ref[...]
        _, r = _rms_fwd(x, gg)
        dx, dg = _rms_bwd(_dot_nt(dqb, wmq_ref[...]), x, gg, r)
        dh2 = dh3 + dx
        dh2_ref[...] = dh2
        dap = _dot_nt(dh2.astype(BF16), wo_ref[...])
        da_ref[...] = dap[:, 0:half].astype(BF16)
        dp_ref[...] = dap[:, half:d].astype(BF16)
        _accumulate(dg_ref, dg, first)

    full = lambda shape: pl.BlockSpec(shape, lambda i: (0,) * len(shape))
    row = lambda w: pl.BlockSpec((tm, w), lambda i: (i, 0))
    return _call_after(
        token, body,
        [row(d), row(d), row(d), full((1, d)), full(wmq.shape), full(km.shape), full(vm.shape),
         full(wmo.shape), full(w_out.shape)],
        (dh3, h2, qm, g, wmq, km, vm, wmo, w_out),
        name="xattn_bwd", grid=(t // tm,),
        out_specs=[row(d), row(d), row(half), row(half), full(km.shape), full(vm.shape), full((1, d))],
        out_shape=[jax.ShapeDtypeStruct((t, d), F32), jax.ShapeDtypeStruct((t, d), BF16),
                   jax.ShapeDtypeStruct((t, half), BF16), jax.ShapeDtypeStruct((t, half), BF16),
                   jax.ShapeDtypeStruct(km.shape, F32), jax.ShapeDtypeStruct(vm.shape, F32),
                   jax.ShapeDtypeStruct((1, d), F32)],
        compiler_params=_cparams(),
    )


def _mem_kv_bwd(dkm, dvm, memn, mem, mem_norm, wmkv):
    n, d = mem.shape

    def body(dk_ref, dv_ref, memn_ref, mem_ref, g_ref, w_ref, dw_ref, dg_ref):
        memn = memn_ref[...]
        dmemn = jnp.zeros((n, d), F32)
        for s in range(2 * MEM_HEADS):
            src = dk_ref[s] if s < MEM_HEADS else dv_ref[s - MEM_HEADS]
            db = src.astype(BF16)
            dw_ref[s] = _dot_tn(memn, db)
            dmemn = dmemn + _dot_nt(db, w_ref[s])
        x = mem_ref[...]
        gg = g_ref[...]
        _, r = _rms_fwd(x, gg)
        _, dg = _rms_bwd(dmemn, x, gg, r)
        dg_ref[...] = dg

    return pl.pallas_call(
        body, name="mem_kv_bwd",
        out_shape=[jax.ShapeDtypeStruct(wmkv.shape, F32), jax.ShapeDtypeStruct((1, d), F32)],
        compiler_params=_cparams(),
    )(dkm, dvm, memn, mem, mem_norm, wmkv)


MESH_ID = pl.DeviceIdType.MESH
ANY = pl.BlockSpec(memory_space=pl.ANY)


def _coords():
    return lax.axis_index("x"), lax.axis_index("y"), lax.axis_index("c")


def _other_chips(x, y):
    return [(1 - x, y), (x, 1 - y), (1 - x, 1 - y)]


def _all_gather_direct(shard, token):
    r, w = shard.shape

    def body(tok_ref, x_ref, out_ref, send_sems, recv_sems, local_sem):
        x, y, c = _coords()
        mine = pltpu.make_async_copy(x_ref, out_ref.at[4 * x + 2 * y + c], local_sem)
        mine.start()
        arrivals = []
        for k in range(1, N_DEV):
            px, py, pc = x ^ ((k >> 2) & 1), y ^ ((k >> 1) & 1), c ^ (k & 1)
            pltpu.make_async_remote_copy(
                src_ref=x_ref, dst_ref=out_ref.at[4 * x + 2 * y + c],
                send_sem=send_sems.at[k - 1], recv_sem=recv_sems.at[k - 1],
                device_id=(px, py, pc), device_id_type=MESH_ID).start()
            arrivals.append(pltpu.make_async_remote_copy(
                src_ref=x_ref, dst_ref=out_ref.at[4 * px + 2 * py + pc],
                send_sem=send_sems.at[k - 1], recv_sem=recv_sems.at[k - 1],
                device_id=(px, py, pc), device_id_type=MESH_ID))
        for cp in arrivals:
            cp.wait_recv()
        for cp in arrivals:
            cp.wait_send()
        mine.wait()

    return pl.pallas_call(
        body, name="all_gather_direct",
        out_shape=jax.ShapeDtypeStruct((N_DEV, r, w), shard.dtype),
        in_specs=[pl.BlockSpec(memory_space=pltpu.VMEM)] * 2, out_specs=pl.BlockSpec(memory_space=pltpu.VMEM),
        scratch_shapes=[pltpu.SemaphoreType.DMA((7,)), pltpu.SemaphoreType.DMA((7,)), pltpu.SemaphoreType.DMA],
        compiler_params=_cparams(),
    )(token, shard)


def _exchange_core(g, tag):
    _, r, w = g.shape

    def body(g_ref, land_ref, send_sems, recv_sems):
        x, y, c = _coords()
        copies = []
        for chip in range(4):
            copies.append(pltpu.make_async_remote_copy(
                src_ref=g_ref.at[2 * chip + (1 - c)], dst_ref=land_ref.at[chip],
                send_sem=send_sems.at[chip], recv_sem=recv_sems.at[chip],
                device_id=(x, y, 1 - c), device_id_type=MESH_ID))
        for cp in copies:
            cp.start()
        for cp in copies:
            cp.wait_recv()
        for cp in copies:
            cp.wait_send()

    return pl.pallas_call(
        body, name="exchange_core_" + tag,
        out_shape=jax.ShapeDtypeStruct((4, r, w), g.dtype),
        in_specs=[ANY], out_specs=ANY,
        scratch_shapes=[pltpu.SemaphoreType.DMA((4,)), pltpu.SemaphoreType.DMA((4,))],
    )(g)


def _chip_partial(g, land, cidx, tag):
    _, r, w = g.shape
    tr = _row_tile(r, 1024)
    g4 = g.reshape(4, 2, r, w)

    def body(c_ref, g_ref, l_ref, o_ref):
        o_ref[0] = (g_ref[0, 0].astype(F32) + l_ref[0].astype(F32)).astype(o_ref.dtype)

    return pl.pallas_call(
        body, name="chip_partial_" + tag,
        grid_spec=pltpu.PrefetchScalarGridSpec(
            num_scalar_prefetch=1, grid=(4, r // tr),
            in_specs=[pl.BlockSpec((1, 1, tr, w), lambda i, j, s: (i, s[0], j, 0)),
                      pl.BlockSpec((1, tr, w), lambda i, j, s: (i, j, 0))],
            out_specs=pl.BlockSpec((1, tr, w), lambda i, j, s: (i, j, 0))),
        out_shape=jax.ShapeDtypeStruct((4, r, w), g.dtype),
        compiler_params=_cparams(),
    )(cidx, g4, land)


HBM_SPEC = pl.BlockSpec(memory_space=pltpu.HBM)
SEM_SPEC = pl.BlockSpec(memory_space=pltpu.SEMAPHORE)
SPLIT_EFFECT = pltpu.SideEffectType.DATAFLOW_SIDE_EFFECTING


def _ici_block(src_ref, px, py):
    return src_ref if len(src_ref.shape) == 2 else src_ref.at[2 * px + py]


def _ici_slot(gather, j, px, py, c):
    return 4 * px + 2 * py + c if gather else j


def _ici_start(src, after, name):
    r, w = src.shape[-2:]
    gather = len(src.shape) == 2
    land_shape = (N_DEV if gather else 3, r, w)

    def body(src_ref, land_ref, after_ref, send_sems, recv_sems, src_thru, land_thru, token):
        x, y, c = _coords()
        for j, (px, py) in enumerate(_other_chips(x, y)):
            pltpu.make_async_remote_copy(
                src_ref=_ici_block(src_ref, px, py), dst_ref=land_ref.at[_ici_slot(gather, j, x, y, c)],
                send_sem=send_sems.at[j], recv_sem=recv_sems.at[j],
                device_id=(px, py, c), device_id_type=MESH_ID).start()
        token[...] = jnp.zeros_like(token)

    return pl.pallas_call(
        body, name=name,
        out_shape=(pltpu.SemaphoreType.DMA((3,)), pltpu.SemaphoreType.DMA((3,)), pltpu.HBM(src.shape, src.dtype),
                   pltpu.HBM(land_shape, src.dtype), jax.ShapeDtypeStruct((8, 128), F32)),
        in_specs=(HBM_SPEC, HBM_SPEC, ANY),
        out_specs=(SEM_SPEC, SEM_SPEC, HBM_SPEC, HBM_SPEC, pl.BlockSpec(memory_space=pltpu.VMEM)),
        input_output_aliases={0: 2, 1: 3},
        compiler_params=pltpu.CompilerParams(has_side_effects=SPLIT_EFFECT),
    )(pltpu.with_memory_space_constraint(src, pltpu.HBM),
      pltpu.with_memory_space_constraint(lax.empty(land_shape, src.dtype), pltpu.HBM), after)


def _ici_wait(started, after, name):
    send_sems, recv_sems, src_thru, land_thru, _ = started
    gather = len(src_thru.shape) == 2

    def body(src_ref, land_ref, send_sems, recv_sems, after_ref, src_dead, got_ref):
        x, y, c = _coords()
        for j, (px, py) in enumerate(_other_chips(x, y)):
            copy = pltpu.make_async_remote_copy(
                src_ref=_ici_block(src_ref, px, py), dst_ref=land_ref.at[_ici_slot(gather, j, px, py, c)],
                send_sem=send_sems.at[j], recv_sem=recv_sems.at[j],
                device_id=(px, py, c), device_id_type=MESH_ID)
            copy.wait_send()
            copy.wait_recv()

    return pl.pallas_call(
        body, name=name,
        out_shape=(pltpu.HBM(src_thru.shape, src_thru.dtype), pltpu.HBM(land_thru.shape, land_thru.dtype)),
        in_specs=(HBM_SPEC, HBM_SPEC, SEM_SPEC, SEM_SPEC, ANY),
        out_specs=(HBM_SPEC, HBM_SPEC), input_output_aliases={0: 0, 1: 1},
        compiler_params=pltpu.CompilerParams(has_side_effects=SPLIT_EFFECT),
    )(src_thru, land_thru, send_sems, recv_sems, after)


def _core_share(own, gathered, name):
    r, w = own.shape

    def body(own_ref, gin_ref, out_ref, stage, send_sems, recv_sems, local_sem):
        x, y, c = _coords()
        sibling = (x, y, 1 - c)
        chips = [(x, y)] + _other_chips(x, y)
        stage_in = pltpu.make_async_copy(own_ref, stage, local_sem)
        stage_in.start()
        sent, arriving = [], []
        for k, (px, py) in enumerate(chips):
            rows = out_ref.at[4 * px + 2 * py + c]
            sent.append(pltpu.make_async_remote_copy(
                src_ref=own_ref if k == 0 else rows, dst_ref=rows,
                send_sem=send_sems.at[k], recv_sem=recv_sems.at[k], device_id=sibling, device_id_type=MESH_ID))
            arriving.append(pltpu.make_async_remote_copy(
                src_ref=own_ref, dst_ref=out_ref.at[4 * px + 2 * py + (1 - c)],
                send_sem=send_sems.at[k], recv_sem=recv_sems.at[k], device_id=sibling, device_id_type=MESH_ID))
        for cp in sent:
            cp.start()
        stage_in.wait()
        stage_out = pltpu.make_async_copy(stage, out_ref.at[4 * x + 2 * y + c], local_sem)
        stage_out.start()
        for cp in arriving:
            cp.wait_recv()
        for cp in sent:
            cp.wait_send()
        stage_out.wait()

    return pl.pallas_call(
        body, name=name,
        out_shape=jax.ShapeDtypeStruct((N_DEV, r, w), own.dtype),
        in_specs=[ANY, ANY], out_specs=ANY, input_output_aliases={1: 0},
        scratch_shapes=[pltpu.VMEM((r, w), own.dtype), pltpu.SemaphoreType.DMA((4,)),
                        pltpu.SemaphoreType.DMA((4,)), pltpu.SemaphoreType.DMA],
    )(own, gathered)


def _adamw(w, g, m, v):
    m = ADAM_B1 * m + (1.0 - ADAM_B1) * g
    v = ADAM_B2 * v + (1.0 - ADAM_B2) * (g * g)
    m_hat = m / ADAM_C1
    v_hat = v / ADAM_C2
    delta = -ADAM_LR * (m_hat / (jnp.sqrt(v_hat) + ADAM_EPS) + ADAM_WD * w)
    return delta, m, v


def _adam_big(part, land, w, m, v, chip_idx, tag):
    r, wd = w.shape
    tr, tw = _row_tile(r, 1024), 256

    def body(s_ref, p_ref, l_ref, w_ref, m_ref, v_ref, g_ref, d_ref, mo_ref, vo_ref):
        g = p_ref[0].astype(F32)
        for j in range(3):
            g = g + l_ref[j].astype(F32)
        delta, mn, vn = _adamw(w_ref[...], g, m_ref[...], v_ref[...])
        g_ref[...] = g
        d_ref[...] = delta
        mo_ref[...] = mn
        vo_ref[...] = vn

    row = pl.BlockSpec((tr, tw), lambda i, j, s: (i, j))
    return pl.pallas_call(
        body, name="adam_big_" + tag,
        grid_spec=pltpu.PrefetchScalarGridSpec(
            num_scalar_prefetch=1, grid=(r // tr, wd // tw),
            in_specs=[pl.BlockSpec((1, tr, tw), lambda i, j, s: (s[0], i, j)),
                      pl.BlockSpec((3, tr, tw), lambda i, j, s: (0, i, j)), row, row, row],
            out_specs=[row, row, row, row]),
        out_shape=[jax.ShapeDtypeStruct((r, wd), F32)] * 4,
        compiler_params=_cparams(),
    )(chip_idx, part, land, w, m, v)


def _adam_small(parts, w, m, v):
    _, r, wd = parts.shape

    def body(p_ref, w_ref, m_ref, v_ref, g_ref, d_ref, mo_ref, vo_ref):
        g = p_ref[0]
        for k in range(1, N_DEV):
            g = g + p_ref[k]
        delta, mn, vn = _adamw(w_ref[...], g, m_ref[...], v_ref[...])
        g_ref[...] = g
        d_ref[...] = delta
        mo_ref[...] = mn
        vo_ref[...] = vn

    return pl.pallas_call(
        body, name="adam_small",
        out_shape=[jax.ShapeDtypeStruct((r, wd), F32)] * 4,
        compiler_params=_cparams(),
    )(parts, w, m, v)


def _pad_rows(a, rows):
    return jnp.pad(a, ((0, rows - a.shape[0]), (0, 0)))


def _pad_w_in(w):
    cut = Q_RANK + KV_RANK + ROPE
    return jnp.concatenate([w[:, :cut], jnp.zeros((w.shape[0], 64), w.dtype), w[:, cut:]], axis=1)


def _unpad_w_in(w):
    cut = Q_RANK + KV_RANK + ROPE
    return jnp.concatenate([w[:, :cut], w[:, cut + 64:]], axis=1)


def _pack_big(p, group):
    if group == "mid":
        parts = [_pad_w_in(p["w_in"][0]), p["w_out"][0], p["w_mq"][0], p["w_mo"][0],
                 p["w_mkv"][0].reshape(256, D_MODEL),
                 _pad_rows(p["w_q_up"][0].T.reshape(24, D_MODEL), 32),
                 p["w_kv_up"][0].reshape(16, D_MODEL)]
    else:
        parts = [p[group + "_w_gate"][0].T, p[group + "_w_up"][0].T, p[group + "_w_down"][0]]
    return jnp.concatenate(parts, axis=0)


def _unpack_big(a, group):
    seg = lambda n: a[SEG_OFF[n][0]:SEG_OFF[n][0] + SEG_OFF[n][1]]
    if group == "mid":
        return {"w_in": _unpad_w_in(seg("w_in"))[None], "w_out": seg("w_out")[None], "w_mq": seg("w_mq")[None],
                "w_mo": seg("w_mo")[None], "w_mkv": seg("w_mkv").reshape(D_MODEL, 256)[None],
                "w_q_up": seg("w_q")[:24].reshape(96, Q_RANK).T[None],
                "w_kv_up": seg("w_kv").reshape(KV_RANK, 128)[None]}
    return {group + "_w_gate": seg(group + "_g").T[None], group + "_w_up": seg(group + "_u").T[None],
            group + "_w_down": seg(group + "_d")[None]}


def _unpack_gathered(full, group):
    seg = lambda n: full[:, SEG_OFF[n][0]:SEG_OFF[n][0] + SEG_OFF[n][1]]
    rows = lambda n: seg(n).reshape(-1, D_MODEL)
    if group != "mid":
        return {n: rows(n) for n, _ in GROUP_SEGS[group]}
    wq_t = seg("w_q")[:, :24].reshape(MLA_HEADS, NOPE + ROPE, Q_RANK)
    wq_t = jnp.pad(wq_t, ((0, 0), (0, HEAD_PAD - NOPE - ROPE), (0, 0))).reshape(MLA_HEADS * HEAD_PAD, Q_RANK)
    wkv = seg("w_kv").reshape(N_DEV, KV_RANK, 128).transpose(1, 0, 2).reshape(KV_RANK, N_DEV * 128)
    return {"w_in": rows("w_in"), "w_out": rows("w_out"), "w_mq": rows("w_mq"), "w_mo": rows("w_mo"),
            "w_mkv": seg("w_mkv").reshape(N_DEV, D_MODEL, 256), "w_q": wq_t, "w_kv": wkv}


def _pack_grads(gr, group):
    blk = lambda a: a.reshape(N_DEV, -1, D_MODEL)
    if group == "mid":
        dwq = gr["w_q"].reshape(MLA_HEADS, HEAD_PAD, Q_RANK)[:, :NOPE + ROPE].reshape(N_DEV, 24, D_MODEL)
        dwq = jnp.pad(dwq, ((0, 0), (0, 8), (0, 0)))
        dwkv = gr["w_kv"].reshape(KV_RANK, N_DEV, 128).transpose(1, 0, 2).reshape(N_DEV, 16, D_MODEL)
        parts = [blk(gr["w_in"]), blk(gr["w_out"]), blk(gr["w_mq"]), blk(gr["w_mo"]),
                 gr["w_mkv"].reshape(N_DEV, 256, D_MODEL), dwq, dwkv]
    else:
        parts = [blk(gr[n]) for n, _ in GROUP_SEGS[group]]
    return jnp.concatenate([a.astype(BF16) for a in parts], axis=1)


def _pack_small(vals):
    parts = []
    for n, r in SMALL_ROWS:
        parts.append(_pad_rows(vals[n].reshape(-1, 128), r) if n in vals else jnp.zeros((r, 128), F32))
    return jnp.concatenate(parts, axis=0)


def _unpack_small(a, shapes):
    out = {}
    for n, shape in shapes.items():
        o = SMALL_OFF[n][0]
        out[n] = a[o:o + int(np.prod(shape)) // 128].reshape(shape)
    return out


BIG_NAMES = ("ffn1_w_gate", "ffn1_w_up", "ffn1_w_down", "w_in", "w_q_up", "w_kv_up", "w_out", "w_mq", "w_mkv",
             "w_mo", "ffn2_w_gate", "ffn2_w_up", "ffn2_w_down")
SMALL_NAMES = ("ffn1_norm", "mix_norm", "q_norm", "kv_norm", "pool_w", "pool_scale", "xattn_norm", "mem_norm",
               "ffn2_norm", "final_norm")
WEIGHT_ORDER = ("ffn1_norm", "ffn1_w_gate", "ffn1_w_up", "ffn1_w_down", "mix_norm", "w_in", "q_norm", "w_q_up",
                "kv_norm", "w_kv_up", "pool_w", "pool_scale", "w_out", "xattn_norm", "mem_norm", "w_mq", "w_mkv",
                "w_mo", "ffn2_norm", "ffn2_w_gate", "ffn2_w_up", "ffn2_w_down", "final_norm")


def _rope_table():
    lane = np.arange(128)
    freqs = (1.0 / (ROPE_BASE ** (np.arange(0, ROPE, 2, dtype=np.float32) / ROPE))).astype(np.float32)
    tab = np.zeros((8, 128), np.float32)
    tab[0] = np.where(lane < ROPE, freqs[lane % (ROPE // 2)], 0.0)
    tab[1] = np.where(lane < ROPE // 2, -1.0, np.where(lane < ROPE, 1.0, 0.0))
    return jnp.asarray(tab)


def kernel(x, mem, positions, ffn1_norm, ffn1_w_gate, ffn1_w_up, ffn1_w_down, mix_norm, w_in, q_norm, w_q_up, kv_norm, w_kv_up, pool_w, pool_scale, w_out, xattn_norm, mem_norm, w_mq, w_mkv, w_mo, ffn2_norm, ffn2_w_gate, ffn2_w_up, ffn2_w_down, final_norm, loss_target, m_ffn1_norm, m_ffn1_w_gate, m_ffn1_w_up, m_ffn1_w_down, m_mix_norm, m_w_in, m_q_norm, m_w_q_up, m_kv_norm, m_w_kv_up, m_pool_w, m_pool_scale, m_w_out, m_xattn_norm, m_mem_norm, m_w_mq, m_w_mkv, m_w_mo, m_ffn2_norm, m_ffn2_w_gate, m_ffn2_w_up, m_ffn2_w_down, m_final_norm, v_ffn1_norm, v_ffn1_w_gate, v_ffn1_w_up, v_ffn1_w_down, v_mix_norm, v_w_in, v_q_norm, v_w_q_up, v_kv_norm, v_w_kv_up, v_pool_w, v_pool_scale, v_w_out, v_xattn_norm, v_mem_norm, v_w_mq, v_w_mkv, v_w_mo, v_ffn2_norm, v_ffn2_w_gate, v_ffn2_w_up, v_ffn2_w_down, v_final_norm):
    wts = dict(ffn1_norm=ffn1_norm, ffn1_w_gate=ffn1_w_gate, ffn1_w_up=ffn1_w_up, ffn1_w_down=ffn1_w_down,
               mix_norm=mix_norm, w_in=w_in, q_norm=q_norm, w_q_up=w_q_up, kv_norm=kv_norm, w_kv_up=w_kv_up,
               pool_w=pool_w, pool_scale=pool_scale, w_out=w_out, xattn_norm=xattn_norm, mem_norm=mem_norm,
               w_mq=w_mq, w_mkv=w_mkv, w_mo=w_mo, ffn2_norm=ffn2_norm, ffn2_w_gate=ffn2_w_gate,
               ffn2_w_up=ffn2_w_up, ffn2_w_down=ffn2_w_down, final_norm=final_norm)
    mom = dict(ffn1_norm=m_ffn1_norm, ffn1_w_gate=m_ffn1_w_gate, ffn1_w_up=m_ffn1_w_up, ffn1_w_down=m_ffn1_w_down,
               mix_norm=m_mix_norm, w_in=m_w_in, q_norm=m_q_norm, w_q_up=m_w_q_up, kv_norm=m_kv_norm,
               w_kv_up=m_w_kv_up, pool_w=m_pool_w, pool_scale=m_pool_scale, w_out=m_w_out, xattn_norm=m_xattn_norm,
               mem_norm=m_mem_norm, w_mq=m_w_mq, w_mkv=m_w_mkv, w_mo=m_w_mo, ffn2_norm=m_ffn2_norm,
               ffn2_w_gate=m_ffn2_w_gate, ffn2_w_up=m_ffn2_w_up, ffn2_w_down=m_ffn2_w_down, final_norm=m_final_norm)
    var = dict(ffn1_norm=v_ffn1_norm, ffn1_w_gate=v_ffn1_w_gate, ffn1_w_up=v_ffn1_w_up, ffn1_w_down=v_ffn1_w_down,
               mix_norm=v_mix_norm, w_in=v_w_in, q_norm=v_q_norm, w_q_up=v_w_q_up, kv_norm=v_kv_norm,
               w_kv_up=v_w_kv_up, pool_w=v_pool_w, pool_scale=v_pool_scale, w_out=v_w_out, xattn_norm=v_xattn_norm,
               mem_norm=v_mem_norm, w_mq=v_w_mq, w_mkv=v_w_mkv, w_mo=v_w_mo, ffn2_norm=v_ffn2_norm,
               ffn2_w_gate=v_ffn2_w_gate, ffn2_w_up=v_ffn2_w_up, ffn2_w_down=v_ffn2_w_down, final_norm=v_final_norm)

    t = x.shape[1]
    xs = x[0]
    mems = mem[0]
    target = loss_target[0]
    pos = positions.reshape(t, 1)
    row = lambda a: a.reshape(1, -1)
    rope_tab = _rope_table()

    cx, cy, cc = _coords()
    core_idx = cc.astype(jnp.int32).reshape(1)
    chip_idx = (2 * cx + cy).astype(jnp.int32).reshape(1)

    w_pack, wb = {}, {}
    for grp in ("ffn1", "mid", "ffn2"):
        w_pack[grp] = _pack_big(wts, grp)
        wb[grp] = w_pack[grp].astype(BF16)
        if grp == "ffn1":
            ag_ffn1 = _ici_start(wb["ffn1"], pos, "ag_ffn1_start")
    own_ffn1, land_ffn1 = _ici_wait(ag_ffn1, wb["ffn2"], "ag_ffn1_wait")
    full_ffn1 = _core_share(own_ffn1, land_ffn1, "ag_ffn1_share")
    fw = _unpack_gathered(full_ffn1, "ffn1")
    ag_mid = _ici_start(wb["mid"], full_ffn1, "ag_mid_start")
    g_ffn1, g_mix, g_q, g_kv = row(ffn1_norm), row(mix_norm), row(q_norm), row(kv_norm)
    g_x, g_mem, g_ffn2, g_fin = row(xattn_norm), row(mem_norm), row(ffn2_norm), row(final_norm)
    pool_wb = pool_w[0].astype(BF16)
    pool_sc = row(pool_scale)

    h1, n1, gate1, up1 = _ffn_fwd(xs, g_ffn1, fw["ffn1_g"], fw["ffn1_u"], fw["ffn1_d"], "ffn1_fwd", token=ag_mid[4])
    own_mid, land_mid = _ici_wait(ag_mid, h1, "ag_mid_wait")
    full_mid = _core_share(own_mid, land_mid, "ag_mid_share")
    fw.update(_unpack_gathered(full_mid, "mid"))
    ag_ffn2 = _ici_start(wb["ffn2"], full_mid, "ag_ffn2_start")
    u, z, qn, kvn, qh, kh, vh = _mix_prep(h1, g_mix, fw["w_in"], g_q, fw["w_q"], g_kv, fw["w_kv"], pos, rope_tab,
                                          token=ag_ffn2[4])
    a, lse = _attn_fwd(qh, kh, vh)
    p = _pool_fwd(z, pool_wb, pool_sc)
    memn, km, vm = _mem_kv(mems, g_mem, fw["w_mkv"])
    h2, h3, hn, qm, om = _xattn_fwd(h1, a, p, fw["w_out"], g_x, fw["w_mq"], km, vm, fw["w_mo"])
    own_ffn2, land_ffn2 = _ici_wait(ag_ffn2, h3, "ag_ffn2_wait")
    fw.update(_unpack_gathered(_core_share(own_ffn2, land_ffn2, "ag_ffn2_share"), "ffn2"))
    h4, n2, gate2, up2 = _ffn_fwd(h3, g_ffn2, fw["ffn2_g"], fw["ffn2_u"], fw["ffn2_d"], "ffn2_fwd")
    loss_part, dh4, dg_fin = _loss_head(h4, target, g_fin)

    gr = {}
    dh3, dgate2, dup2, act2, dg_ffn2 = _ffn_bwd_data(dh4, h3, g_ffn2, gate2, up2, fw["ffn2_g"], fw["ffn2_u"],
                                                     fw["ffn2_d"], "ffn2_bwd")
    gr["ffn2_g"] = _tn_matmul(dgate2, n2, "ffn2_dwg", tmm=1408)
    gr["ffn2_u"] = _tn_matmul(dup2, n2, "ffn2_dwu", tmm=1408)
    gr["ffn2_d"] = _tn_matmul(act2, dh4, "ffn2_dwd", scale=0.5, tmm=1408)
    g_ffn2_pack = _pack_grads(gr, "ffn2")
    part_ffn2 = _chip_partial(g_ffn2_pack, _exchange_core(g_ffn2_pack, "ffn2"), core_idx, "ffn2")
    rs_ffn2 = _ici_start(part_ffn2, g_ffn2_pack, "rs_ffn2_start")
    dh2, dqm, da, dp, dkm, dvm, dg_x = _xattn_bwd(dh3, h2, qm, g_x, fw["w_mq"], km, vm, fw["w_mo"], fw["w_out"],
                                                  token=rs_ffn2[4])
    gr["w_mo"] = _tn_matmul(om, dh3, "dw_mo", tmm=512)
    gr["w_mq"] = _tn_matmul(hn, dqm, "dw_mq", tmm=512)
    gr["w_out"] = jnp.concatenate([_tn_matmul(a, dh2, "dw_out_a"), _tn_matmul(p, dh2, "dw_out_p")], axis=0)
    gr["w_mkv"], dg_mem = _mem_kv_bwd(dkm, dvm, memn, mems, g_mem, fw["w_mkv"])
    dz_pool, d_pool_w, d_pool_sc = _pool_bwd(dp, z, pool_wb, pool_sc)
    dqh, dkh, dvh = _attn_bwd(qh, kh, vh, da, lse, _attn_delta(a, da))
    dh1, dq, dkv, dz, dg_q, dg_kv, dg_mix = _mla_bwd(dqh, dkh, dvh, z, dz_pool, h1, dh2, g_mix, fw["w_in"], g_q,
                                                     fw["w_q"], g_kv, fw["w_kv"], pos, rope_tab)
    gr["w_q"] = _tn_matmul(dq, qn, "dw_q", tmm=512)
    gr["w_kv"] = _tn_matmul(kvn, dkv, "dw_kv")
    gr["w_in"] = _tn_matmul(u, dz, "dw_in", tmm=512)
    g_mid_pack = _pack_grads(gr, "mid")
    part_mid = _chip_partial(g_mid_pack, _exchange_core(g_mid_pack, "mid"), core_idx, "mid")
    part_ffn2, land_ffn2_g = _ici_wait(rs_ffn2, part_mid, "rs_ffn2_wait")
    rs_mid = _ici_start(part_mid, land_ffn2_g, "rs_mid_start")
    dx, dgate1, dup1, act1, dg_ffn1 = _ffn_bwd_data(dh1, xs, g_ffn1, gate1, up1, fw["ffn1_g"], fw["ffn1_u"],
                                                    fw["ffn1_d"], "ffn1_bwd", token=rs_mid[4])
    part_mid, land_mid_g = _ici_wait(rs_mid, dx, "rs_mid_wait")
    gr["ffn1_g"] = _tn_matmul(dgate1, n1, "ffn1_dwg", tmm=1408)
    gr["ffn1_u"] = _tn_matmul(dup1, n1, "ffn1_dwu", tmm=1408)
    gr["ffn1_d"] = _tn_matmul(act1, dh1, "ffn1_dwd", scale=0.5, tmm=1408)
    g_ffn1_pack = _pack_grads(gr, "ffn1")
    part_ffn1 = _chip_partial(g_ffn1_pack, _exchange_core(g_ffn1_pack, "ffn1"), core_idx, "ffn1")
    rs_ffn1 = _ici_start(part_ffn1, g_ffn1_pack, "rs_ffn1_start")

    small_g = dict(ffn1_norm=dg_ffn1, mix_norm=dg_mix, q_norm=dg_q, kv_norm=dg_kv, pool_w=d_pool_w,
                   pool_scale=d_pool_sc, xattn_norm=dg_x, mem_norm=dg_mem, ffn2_norm=dg_ffn2, final_norm=dg_fin,
                   loss=loss_part)
    parts = _all_gather_direct(_pack_small(small_g), rs_ffn1[4])
    small = _adam_small(parts, _pack_small({n: wts[n] for n in SMALL_NAMES}),
                        _pack_small({n: mom[n] for n in SMALL_NAMES}), _pack_small({n: var[n] for n in SMALL_NAMES}))
    loss = small[0][SMALL_OFF["loss"][0], 0]
    shapes = {n: wts[n].shape for n in SMALL_NAMES}
    small = [_unpack_small(s, shapes) for s in small]

    big = {}

    def adam_group(grp, part, land):
        res = _adam_big(part, land, w_pack[grp], _pack_big(mom, grp), _pack_big(var, grp), chip_idx, grp)
        for k, packed in enumerate(res):
            big.setdefault(k, {}).update(_unpack_big(packed, grp))
        return res[0]

    adam_group("mid", part_mid, land_mid_g)
    last = adam_group("ffn2", part_ffn2, land_ffn2_g)
    part_ffn1, land_ffn1_g = _ici_wait(rs_ffn1, last, "rs_ffn1_wait")
    adam_group("ffn1", part_ffn1, land_ffn1_g)

    outs = [loss, dx[None]]
    for k in range(4):
        for n in WEIGHT_ORDER:
            outs.append(big[k][n] if n in BIG_NAMES else small[k][n])
    return tuple(outs)
```

```python
import numpy as np

import jax
import jax.numpy as jnp
from jax import lax
from jax.experimental import pallas as pl
from jax.experimental.pallas import tpu as pltpu

F32 = jnp.float32
BF16 = jnp.bfloat16

N_DEV = 8
D_MODEL = 1024
D_FF = 2816
MLA_HEADS = 4
NOPE = 128
ROPE = 64
HEAD_PAD = 256
V_DIM = 128
Q_RANK = 256
KV_RANK = 128
POOL_WINDOWS = (2, 4, 8, 16)
POOL_CH = 128
POOL_HALO = 16
N_MEM = 256
MEM_HEADS = 4
MEM_HD = 256
ROPE_BASE = 10000.0
RMS_EPS = 1e-6
ATTN_SCALE = (NOPE + ROPE) ** -0.5
MEM_SCALE = MEM_HD ** -0.5
NEG_BIG = -1e30

ADAM_LR = 0.001
ADAM_B1 = 0.9
ADAM_B2 = 0.999
ADAM_EPS = 1e-08
ADAM_WD = 0.01
ADAM_STEP = 10
ADAM_C1 = 1.0 - ADAM_B1 ** ADAM_STEP
ADAM_C2 = 1.0 - ADAM_B2 ** ADAM_STEP

VMEM_LIMIT_BYTES = 52 * 1024 * 1024
BF16_ROWS = 16

GROUP_SEGS = {
    "ffn1": (("ffn1_g", 352), ("ffn1_u", 352), ("ffn1_d", 352)),
    "mid": (("w_in", 128), ("w_out", 128), ("w_mq", 128), ("w_mo", 128), ("w_mkv", 256), ("w_q", 32), ("w_kv", 16)),
    "ffn2": (("ffn2_g", 352), ("ffn2_u", 352), ("ffn2_d", 352)),
}
SEG_OFF = {}
GROUP_ROWS = {}
for _g, _segs in GROUP_SEGS.items():
    _o = 0
    for _n, _r in _segs:
        SEG_OFF[_n] = (_o, _r)
        _o += _r
    GROUP_ROWS[_g] = _o

SMALL_ROWS = (("ffn1_norm", 8), ("mix_norm", 8), ("q_norm", 8), ("kv_norm", 8), ("pool_w", 512), ("pool_scale", 8),
              ("xattn_norm", 8), ("mem_norm", 8), ("ffn2_norm", 8), ("final_norm", 8), ("loss", 8))
SMALL_OFF = {}
_o = 0
for _n, _r in SMALL_ROWS:
    SMALL_OFF[_n] = (_o, _r)
    _o += _r


def _cparams(**kw):
    return pltpu.CompilerParams(vmem_limit_bytes=VMEM_LIMIT_BYTES, **kw)


def _row_tile(rows, limit):
    best = None
    for cand in range(BF16_ROWS, min(rows, limit) + 1, BF16_ROWS):
        if rows % cand == 0:
            best = cand
    assert best is not None, rows
    return best


def _dot_nn(a, b):
    return lax.dot_general(a, b, (((1,), (0,)), ((), ())), preferred_element_type=F32)


def _dot_nt(a, b):
    return lax.dot_general(a, b, (((1,), (1,)), ((), ())), preferred_element_type=F32)


def _dot_tn(a, b):
    return lax.dot_general(a, b, (((0,), (0,)), ((), ())), preferred_element_type=F32)


def _rms_fwd(x, g):
    r = lax.rsqrt(jnp.mean(x * x, axis=-1, keepdims=True) + RMS_EPS)
    return x * r * g, r


def _rms_bwd(dy, x, g, r):
    xhat = x * r
    dyg = dy * g
    dx = r * (dyg - xhat * jnp.mean(dyg * xhat, axis=-1, keepdims=True))
    dg = jnp.sum(dy * xhat, axis=0, keepdims=True)
    return dx, dg


def _accumulate(ref, val, first):
    if isinstance(first, bool):
        if first:
            ref[...] = val
        else:
            ref[...] += val
        return

    @pl.when(first)
    def _():
        ref[...] = val

    @pl.when(jnp.logical_not(first))
    def _():
        ref[...] += val


def _call_after(token, body, in_specs, args, **kw):
    if token is not None:
        inner = body
        body = lambda tok_ref, *refs: inner(*refs)
        in_specs = [pl.BlockSpec((8, 128), lambda *_: (0, 0))] + list(in_specs)
        args = (token,) + tuple(args)
    return pl.pallas_call(body, in_specs=in_specs, **kw)(*args)


def _rope_tables(pos_col, tab):
    ang = pos_col.astype(F32) * tab[0:1, :]
    return jnp.cos(ang), jnp.sin(ang) * tab[1:2, :]


def _swap_halves(x):
    lane = lax.broadcasted_iota(jnp.int32, x.shape, 1)
    return jnp.where((lane % 64) < 32, pltpu.roll(x, 96, 1), pltpu.roll(x, 32, 1))


def _rope_apply(x, cos_t, sin_t):
    return x * cos_t + _swap_halves(x) * sin_t


def _rope_apply_t(dy, cos_t, sin_t):
    return dy * cos_t + _swap_halves(dy * sin_t)


def _ffn_fwd(h, g, wg_t, wu_t, wd, name, token=None):
    t, d = h.shape
    f = wg_t.shape[0]
    tm, tf = min(1024, t), 256
    nf = f // tf

    def body(h_ref, g_ref, wg_ref, wu_ref, wd_ref, ho_ref, n_ref, gate_ref, up_ref, nb_sc, acc_sc):
        j = pl.program_id(1)

        @pl.when(j == 0)
        def _():
            y, _ = _rms_fwd(h_ref[...], g_ref[...])
            nb = y.astype(BF16)
            nb_sc[...] = nb
            n_ref[...] = nb
            acc_sc[...] = jnp.zeros_like(acc_sc)

        nb = nb_sc[...]
        gt = _dot_nt(nb, wg_ref[...])
        ut = _dot_nt(nb, wu_ref[...])
        gate_ref[...] = gt.astype(BF16)
        up_ref[...] = ut.astype(BF16)
        act = (gt * jax.nn.sigmoid(gt)) * ut
        acc_sc[...] += _dot_nn(act.astype(BF16), wd_ref[...])

        @pl.when(j == nf - 1)
        def _():
            ho_ref[...] = h_ref[...] + 0.5 * acc_sc[...]

    return _call_after(
        token, body,
        [pl.BlockSpec((tm, d), lambda i, j: (i, 0)),
         pl.BlockSpec((1, d), lambda i, j: (0, 0)),
         pl.BlockSpec((tf, d), lambda i, j: (j, 0)),
         pl.BlockSpec((tf, d), lambda i, j: (j, 0)),
         pl.BlockSpec((tf, d), lambda i, j: (j, 0))],
        (h, g, wg_t, wu_t, wd),
        name=name, grid=(t // tm, nf),
        out_specs=[pl.BlockSpec((tm, d), lambda i, j: (i, 0)),
                   pl.BlockSpec((tm, d), lambda i, j: (i, 0)),
                   pl.BlockSpec((tm, tf), lambda i, j: (i, j)),
                   pl.BlockSpec((tm, tf), lambda i, j: (i, j))],
        out_shape=[jax.ShapeDtypeStruct((t, d), F32), jax.ShapeDtypeStruct((t, d), BF16),
                   jax.ShapeDtypeStruct((t, f), BF16), jax.ShapeDtypeStruct((t, f), BF16)],
        scratch_shapes=[pltpu.VMEM((tm, d), BF16), pltpu.VMEM((tm, d), F32)],
        compiler_params=_cparams(),
    )


def _ffn_bwd_data(dho, h, g, gate, up, wg_t, wu_t, wd, name, token=None):
    t, d = h.shape
    f = wg_t.shape[0]
    tm, tf = min(1024, t), 256
    nf = f // tf

    def body(dho_ref, h_ref, g_ref, gate_ref, up_ref, wg_ref, wu_ref, wd_ref,
             dh_ref, dgate_ref, dup_ref, act_ref, dg_ref, dhb_sc, acc_sc):
        i, j = pl.program_id(0), pl.program_id(1)

        @pl.when(j == 0)
        def _():
            dhb_sc[...] = (0.5 * dho_ref[...]).astype(BF16)
            acc_sc[...] = jnp.zeros_like(acc_sc)

        dact = _dot_nt(dhb_sc[...], wd_ref[...])
        gt = gate_ref[...].astype(F32)
        ut = up_ref[...].astype(F32)
        sg = jax.nn.sigmoid(gt)
        silu = gt * sg
        dgb = (dact * ut * (sg * (1.0 + gt * (1.0 - sg)))).astype(BF16)
        dub = (dact * silu).astype(BF16)
        act_ref[...] = (silu * ut).astype(BF16)
        dgate_ref[...] = dgb
        dup_ref[...] = dub
        acc_sc[...] += _dot_nn(dgb, wg_ref[...]) + _dot_nn(dub, wu_ref[...])

        @pl.when(j == nf - 1)
        def _():
            x = h_ref[...]
            gg = g_ref[...]
            _, r = _rms_fwd(x, gg)
            dx, dg = _rms_bwd(acc_sc[...], x, gg, r)
            dh_ref[...] = dho_ref[...] + dx
            _accumulate(dg_ref, dg, i == 0)

    return _call_after(
        token, body,
        [pl.BlockSpec((tm, d), lambda i, j: (i, 0)),
         pl.BlockSpec((tm, d), lambda i, j: (i, 0)),
         pl.BlockSpec((1, d), lambda i, j: (0, 0)),
         pl.BlockSpec((tm, tf), lambda i, j: (i, j)),
         pl.BlockSpec((tm, tf), lambda i, j: (i, j)),
         pl.BlockSpec((tf, d), lambda i, j: (j, 0)),
         pl.BlockSpec((tf, d), lambda i, j: (j, 0)),
         pl.BlockSpec((tf, d), lambda i, j: (j, 0))],
        (dho, h, g, gate, up, wg_t, wu_t, wd),
        name=name, grid=(t // tm, nf),
        out_specs=[pl.BlockSpec((tm, d), lambda i, j: (i, 0)),
                   pl.BlockSpec((tm, tf), lambda i, j: (i, j)),
                   pl.BlockSpec((tm, tf), lambda i, j: (i, j)),
                   pl.BlockSpec((tm, tf), lambda i, j: (i, j)),
                   pl.BlockSpec((1, d), lambda i, j: (0, 0))],
        out_shape=[jax.ShapeDtypeStruct((t, d), F32), jax.ShapeDtypeStruct((t, f), BF16),
                   jax.ShapeDtypeStruct((t, f), BF16), jax.ShapeDtypeStruct((t, f), BF16),
                   jax.ShapeDtypeStruct((1, d), F32)],
        scratch_shapes=[pltpu.VMEM((tm, d), BF16), pltpu.VMEM((tm, d), F32)],
        compiler_params=_cparams(),
    )


def _tn_matmul(a, b, name, scale=1.0, tmm=None):
    t, m = a.shape
    n = b.shape[1]
    tmm = m if tmm is None else tmm
    tk = min(1024, t)

    def body(a_ref, b_ref, o_ref):
        k = pl.program_id(1)
        prod = _dot_tn(a_ref[...].astype(BF16), b_ref[...].astype(BF16))
        if scale != 1.0:
            prod = prod * scale
        _accumulate(o_ref, prod, k == 0)

    return pl.pallas_call(
        body, name=name, grid=(m // tmm, t // tk),
        in_specs=[pl.BlockSpec((tk, tmm), lambda i, k: (k, i)),
                  pl.BlockSpec((tk, n), lambda i, k: (k, 0))],
        out_specs=pl.BlockSpec((tmm, n), lambda i, k: (i, 0)),
        out_shape=jax.ShapeDtypeStruct((m, n), F32),
        compiler_params=_cparams(),
    )(a, b)


def _loss_head(h, target, g):
    t, d = h.shape
    tm = min(512, t)

    def body(h_ref, t_ref, g_ref, loss_ref, dh_ref, dg_ref):
        i = pl.program_id(0)
        x = h_ref[...]
        gg = g_ref[...]
        y, r = _rms_fwd(x, gg)
        err = y - t_ref[...]
        part = 0.5 * jnp.sum(jnp.mean(err * err, axis=-1, keepdims=True), axis=0, keepdims=True)
        dx, dg = _rms_bwd(err * (1.0 / d), x, gg, r)
        dh_ref[...] = dx
        _accumulate(loss_ref, jnp.broadcast_to(part, loss_ref.shape), i == 0)
        _accumulate(dg_ref, dg, i == 0)

    return pl.pallas_call(
        body, name="loss_head", grid=(t // tm,),
        in_specs=[pl.BlockSpec((tm, d), lambda i: (i, 0)),
                  pl.BlockSpec((tm, d), lambda i: (i, 0)),
                  pl.BlockSpec((1, d), lambda i: (0, 0))],
        out_specs=[pl.BlockSpec((8, 128), lambda i: (0, 0)),
                   pl.BlockSpec((tm, d), lambda i: (i, 0)),
                   pl.BlockSpec((1, d), lambda i: (0, 0))],
        out_shape=[jax.ShapeDtypeStruct((8, 128), F32), jax.ShapeDtypeStruct((t, d), F32),
                   jax.ShapeDtypeStruct((1, d), F32)],
        compiler_params=_cparams(),
    )(h, target, g)


def _mix_prep(h1, mix_norm, w_in, q_norm, wq_t, kv_norm, wkv, pos, rope_tab, token=None):
    t, d = h1.shape
    tm = min(512, t)

    def body(h_ref, gm_ref, win_ref, gq_ref, wq_ref, gkv_ref, wkv_ref, pos_ref, tab_ref,
             u_ref, z_ref, qn_ref, kvn_ref, q_ref, k_ref, v_ref):
        u, _ = _rms_fwd(h_ref[...], gm_ref[...])
        ub = u.astype(BF16)
        u_ref[...] = ub
        z = _dot_nn(ub, win_ref[...])
        z_ref[...] = z
        cos_t, sin_t = _rope_tables(pos_ref[...], tab_ref[...])
        qn, _ = _rms_fwd(z[:, 0:Q_RANK], gq_ref[...])
        qnb = qn.astype(BF16)
        qn_ref[...] = qnb
        q = _dot_nt(qnb, wq_ref[...])
        kvn, _ = _rms_fwd(z[:, Q_RANK:Q_RANK + KV_RANK], gkv_ref[...])
        kvnb = kvn.astype(BF16)
        kvn_ref[...] = kvnb
        kv = _dot_nn(kvnb, wkv_ref[...])
        k_pe = _rope_apply(z[:, Q_RANK + KV_RANK:Q_RANK + KV_RANK + 128], cos_t, sin_t)
        ones = jnp.ones((tm, V_DIM), F32)
        for hh in range(MLA_HEADS):
            b = hh * HEAD_PAD
            q_pe = _rope_apply(q[:, b + NOPE:b + HEAD_PAD], cos_t, sin_t)
            q_ref[hh] = jnp.concatenate([q[:, b:b + NOPE], q_pe], axis=-1).astype(BF16)
            k_ref[hh] = jnp.concatenate([kv[:, b:b + NOPE], k_pe], axis=-1).astype(BF16)
            v_ref[hh] = jnp.concatenate([kv[:, b + NOPE:b + HEAD_PAD], ones], axis=-1).astype(BF16)

    full = lambda shape: pl.BlockSpec(shape, lambda i: (0,) * len(shape))
    return _call_after(
        token, body,
        [pl.BlockSpec((tm, d), lambda i: (i, 0)), full((1, d)), full(w_in.shape), full((1, Q_RANK)),
         full(wq_t.shape), full((1, KV_RANK)), full(wkv.shape),
         pl.BlockSpec((tm, 1), lambda i: (i, 0)), full(rope_tab.shape)],
        (h1, mix_norm, w_in, q_norm, wq_t, kv_norm, wkv, pos, rope_tab),
        name="mix_prep", grid=(t // tm,),
        out_specs=[pl.BlockSpec((tm, d), lambda i: (i, 0)),
                   pl.BlockSpec((tm, d), lambda i: (i, 0)),
                   pl.BlockSpec((tm, Q_RANK), lambda i: (i, 0)),
                   pl.BlockSpec((tm, KV_RANK), lambda i: (i, 0)),
                   pl.BlockSpec((MLA_HEADS, tm, HEAD_PAD), lambda i: (0, i, 0)),
                   pl.BlockSpec((MLA_HEADS, tm, HEAD_PAD), lambda i: (0, i, 0)),
                   pl.BlockSpec((MLA_HEADS, tm, 2 * V_DIM), lambda i: (0, i, 0))],
        out_shape=[jax.ShapeDtypeStruct((t, d), BF16), jax.ShapeDtypeStruct((t, d), F32),
                   jax.ShapeDtypeStruct((t, Q_RANK), BF16), jax.ShapeDtypeStruct((t, KV_RANK), BF16),
                   jax.ShapeDtypeStruct((MLA_HEADS, t, HEAD_PAD), BF16),
                   jax.ShapeDtypeStruct((MLA_HEADS, t, HEAD_PAD), BF16),
                   jax.ShapeDtypeStruct((MLA_HEADS, t, 2 * V_DIM), BF16)],
        compiler_params=_cparams(),
    )


def _causal_mask(s):
    row = lax.broadcasted_iota(jnp.int32, s.shape, 0)
    col = lax.broadcasted_iota(jnp.int32, s.shape, 1)
    return jnp.where(col <= row, s, NEG_BIG)


def _attn_fwd(q, k, v):
    nh, t, _ = q.shape
    tq = tk = min(512, t)
    nq, nk = t // tq, t // tk

    def body(q_ref, k_ref, v_ref, o_ref, lse_ref, m_sc, acc_sc):
        i, j = pl.program_id(0), pl.program_id(1)

        @pl.when(j == 0)
        def _():
            m_sc[...] = jnp.full_like(m_sc, NEG_BIG)
            acc_sc[...] = jnp.zeros_like(acc_sc)

        def step(diagonal):
            for hh in range(nh):
                s = _dot_nt(q_ref[hh], k_ref[hh]) * ATTN_SCALE
                if diagonal:
                    s = _causal_mask(s)
                m_old = m_sc[hh]
                m_new = jnp.maximum(m_old, jnp.max(s, axis=-1, keepdims=True))
                p = jnp.exp(s - m_new).astype(BF16)
                acc_sc[hh] = jnp.exp(m_old - m_new) * acc_sc[hh] + _dot_nn(p, v_ref[hh])
                m_sc[hh] = m_new

        @pl.when(j < i)
        def _():
            step(False)

        @pl.when(j == i)
        def _():
            step(True)
            for hh in range(nh):
                acc = acc_sc[hh]
                l = acc[:, V_DIM:2 * V_DIM]
                o_ref[:, hh * V_DIM:(hh + 1) * V_DIM] = (acc[:, 0:V_DIM] / l).astype(BF16)
                lse_ref[hh] = m_sc[hh] + jnp.log(l[:, 0:1])

    kv_map = lambda i, j: (0, jnp.minimum(j, i), 0)
    return pl.pallas_call(
        body, name="attn_fwd", grid=(nq, nk),
        in_specs=[pl.BlockSpec((nh, tq, HEAD_PAD), lambda i, j: (0, i, 0)),
                  pl.BlockSpec((nh, tk, HEAD_PAD), kv_map),
                  pl.BlockSpec((nh, tk, 2 * V_DIM), kv_map)],
        out_specs=[pl.BlockSpec((tq, nh * V_DIM), lambda i, j: (i, 0)),
                   pl.BlockSpec((nh, tq, 1), lambda i, j: (0, i, 0))],
        out_shape=[jax.ShapeDtypeStruct((t, nh * V_DIM), BF16), jax.ShapeDtypeStruct((nh, t, 1), F32)],
        scratch_shapes=[pltpu.VMEM((nh, tq, 1), F32), pltpu.VMEM((nh, tq, 2 * V_DIM), F32)],
        compiler_params=_cparams(),
    )(q, k, v)


def _attn_delta(o, do):
    t, w = o.shape
    nh = w // V_DIM
    tm = min(512, t)

    def body(o_ref, do_ref, d_ref):
        prod = o_ref[...].astype(F32) * do_ref[...].astype(F32)
        for hh in range(nh):
            d_ref[hh] = jnp.sum(prod[:, hh * V_DIM:(hh + 1) * V_DIM], axis=-1, keepdims=True)

    return pl.pallas_call(
        body, name="attn_delta", grid=(t // tm,),
        in_specs=[pl.BlockSpec((tm, w), lambda i: (i, 0)), pl.BlockSpec((tm, w), lambda i: (i, 0))],
        out_specs=pl.BlockSpec((nh, tm, 1), lambda i: (0, i, 0)),
        out_shape=jax.ShapeDtypeStruct((nh, t, 1), F32),
        compiler_params=_cparams(),
    )(o, do)


ATTN_BWD_HEADS = 2


def _attn_bwd(q, k, v, do, lse, delta):
    nh, t, _ = q.shape
    hp = ATTN_BWD_HEADS
    tq = tk = min(512, t)
    nq, nk = t // tq, t // tk

    def body(q_ref, k_ref, v_ref, do_ref, lse_ref, dlt_ref, dq_ref, dk_ref, dv_ref):
        j, i = pl.program_id(1), pl.program_id(2)

        @pl.when(jnp.logical_and(j == 0, i == 0))
        def _():
            dq_ref[...] = jnp.zeros_like(dq_ref)

        def step(diagonal):
            for hh in range(hp):
                qq, kk = q_ref[hh], k_ref[hh]
                dob = do_ref[:, hh * V_DIM:(hh + 1) * V_DIM]
                s = _dot_nt(qq, kk) * ATTN_SCALE
                if diagonal:
                    s = _causal_mask(s)
                p = jnp.exp(s - lse_ref[hh])
                dpp = _dot_nt(dob, v_ref[hh])
                dsb = (p * (dpp - dlt_ref[hh]) * ATTN_SCALE).astype(BF16)
                _accumulate(dv_ref.at[hh], _dot_tn(p.astype(BF16), dob), diagonal)
                _accumulate(dk_ref.at[hh], _dot_tn(dsb, qq), diagonal)
                dq_ref[hh, pl.ds(pl.multiple_of(i * tq, tq), tq), :] += _dot_nn(dsb, kk)

        @pl.when(i > j)
        def _():
            step(False)

        @pl.when(i == j)
        def _():
            step(True)

    qmap = lambda h, j, i: (h, jnp.maximum(i, j), 0)
    return pl.pallas_call(
        body, name="attn_bwd", grid=(nh // hp, nk, nq),
        in_specs=[pl.BlockSpec((hp, tq, HEAD_PAD), qmap),
                  pl.BlockSpec((hp, tk, HEAD_PAD), lambda h, j, i: (h, j, 0)),
                  pl.BlockSpec((hp, tk, V_DIM), lambda h, j, i: (h, j, 0)),
                  pl.BlockSpec((tq, hp * V_DIM), lambda h, j, i: (jnp.maximum(i, j), h)),
                  pl.BlockSpec((hp, tq, 1), qmap),
                  pl.BlockSpec((hp, tq, 1), qmap)],
        out_specs=[pl.BlockSpec((hp, t, HEAD_PAD), lambda h, j, i: (h, 0, 0)),
                   pl.BlockSpec((hp, tk, HEAD_PAD), lambda h, j, i: (h, j, 0)),
                   pl.BlockSpec((hp, tk, V_DIM), lambda h, j, i: (h, j, 0))],
        out_shape=[jax.ShapeDtypeStruct((nh, t, HEAD_PAD), F32), jax.ShapeDtypeStruct((nh, t, HEAD_PAD), F32),
                   jax.ShapeDtypeStruct((nh, t, V_DIM), F32)],
        compiler_params=_cparams(),
    )(q, k, v, do, lse, delta)


def _pool_counts(first_token, rows, w):
    tok = lax.broadcasted_iota(jnp.int32, (rows, POOL_CH), 0) + first_token
    return jnp.minimum(tok + 1, w).astype(F32)


def _pool_centered(zbuf, g, w, i, tm):
    lanes = pl.ds(g * POOL_CH, POOL_CH)
    cur = zbuf[pl.ds(POOL_HALO, tm), lanes]
    win = cur
    for s in range(1, w):
        win = win + zbuf[pl.ds(POOL_HALO - s, tm), lanes]
    return win / _pool_counts(i * tm, tm, w) - cur


def _pool_load(zbuf, z_ref, halo_ref, i, tm):
    @pl.when(i == 0)
    def _():
        zbuf[pl.ds(0, POOL_HALO), :] = jnp.zeros((POOL_HALO, zbuf.shape[1]), F32)

    @pl.when(i > 0)
    def _():
        zbuf[pl.ds(0, POOL_HALO), :] = halo_ref[...]

    zbuf[pl.ds(POOL_HALO, tm), :] = z_ref[...]


def _pool_fwd(z, pool_w, pool_scale):
    t = z.shape[0]
    pw = len(POOL_WINDOWS) * POOL_CH
    tm = min(512, t)
    hb = tm // POOL_HALO

    def body(z_ref, halo_ref, w_ref, sc_ref, p_ref, zbuf):
        i = pl.program_id(0)
        _pool_load(zbuf, z_ref, halo_ref, i, tm)
        for g, w in enumerate(POOL_WINDOWS):
            c = _pool_centered(zbuf, g, w, i, tm)
            y = _dot_nn(c.astype(BF16), w_ref[g]) * sc_ref[:, g * POOL_CH:(g + 1) * POOL_CH]
            p_ref[:, g * POOL_CH:(g + 1) * POOL_CH] = y.astype(BF16)

    return pl.pallas_call(
        body, name="pool_fwd", grid=(t // tm,),
        in_specs=[pl.BlockSpec((tm, pw), lambda i: (i, 1)),
                  pl.BlockSpec((POOL_HALO, pw), lambda i: (jnp.maximum(i * hb - 1, 0), 1)),
                  pl.BlockSpec(pool_w.shape, lambda i: (0, 0, 0)),
                  pl.BlockSpec((1, pw), lambda i: (0, 0))],
        out_specs=pl.BlockSpec((tm, pw), lambda i: (i, 0)),
        out_shape=jax.ShapeDtypeStruct((t, pw), BF16),
        scratch_shapes=[pltpu.VMEM((POOL_HALO + tm, pw), F32)],
        compiler_params=_cparams(),
    )(z, z, pool_w, pool_scale)


def _pool_bwd(dp, z, pool_w, pool_scale):
    t = z.shape[0]
    ng = len(POOL_WINDOWS)
    pw = ng * POOL_CH
    tm = min(512, t)
    hb = tm // POOL_HALO
    nt = t // tm

    def body(dp_ref, dpn_ref, z_ref, halo_ref, w_ref, sc_ref, dz_ref, dw_ref, dsc_ref, zbuf, dbuf):
        i = pl.program_id(0)
        _pool_load(zbuf, z_ref, halo_ref, i, tm)

        @pl.when(i == 0)
        def _():
            dw_ref[...] = jnp.zeros_like(dw_ref)
            dsc_ref[...] = jnp.zeros_like(dsc_ref)

        nxt_ok = (i < nt - 1).astype(F32)
        for g, w in enumerate(POOL_WINDOWS):
            lanes = pl.ds(g * POOL_CH, POOL_CH)
            cols = slice(g * POOL_CH, (g + 1) * POOL_CH)
            sc = sc_ref[:, cols]
            wg = w_ref[g]
            c = _pool_centered(zbuf, g, w, i, tm).astype(BF16)
            ypre = _dot_nn(c, wg)
            dpg = dp_ref[:, cols].astype(F32)
            dsc_ref[:, cols] += jnp.sum(dpg * ypre, axis=0, keepdims=True)
            dyb = (dpg * sc).astype(BF16)
            dw_ref[g] += _dot_tn(c, dyb)
            dd = _dot_nt(dyb, wg)
            dyn = (dpn_ref[:, cols].astype(F32) * sc).astype(BF16)
            ddn = _dot_nt(dyn, wg) * nxt_ok
            dbuf[pl.ds(0, tm), lanes] = dd / _pool_counts(i * tm, tm, w)
            dbuf[pl.ds(tm, POOL_HALO), lanes] = ddn / _pool_counts((i + 1) * tm, POOL_HALO, w)
            acc = -dd
            for s in range(w):
                acc = acc + dbuf[pl.ds(s, tm), lanes]
            dz_ref[:, cols] = acc

    return pl.pallas_call(
        body, name="pool_bwd", grid=(nt,),
        in_specs=[pl.BlockSpec((tm, pw), lambda i: (i, 0)),
                  pl.BlockSpec((POOL_HALO, pw), lambda i: (jnp.minimum((i + 1) * hb, t // POOL_HALO - 1), 0)),
                  pl.BlockSpec((tm, pw), lambda i: (i, 1)),
                  pl.BlockSpec((POOL_HALO, pw), lambda i: (jnp.maximum(i * hb - 1, 0), 1)),
                  pl.BlockSpec(pool_w.shape, lambda i: (0, 0, 0)),
                  pl.BlockSpec((1, pw), lambda i: (0, 0))],
        out_specs=[pl.BlockSpec((tm, pw), lambda i: (i, 0)),
                   pl.BlockSpec((ng, POOL_CH, POOL_CH), lambda i: (0, 0, 0)),
                   pl.BlockSpec((1, pw), lambda i: (0, 0))],
        out_shape=[jax.ShapeDtypeStruct((t, pw), F32), jax.ShapeDtypeStruct((ng, POOL_CH, POOL_CH), F32),
                   jax.ShapeDtypeStruct((1, pw), F32)],
        scratch_shapes=[pltpu.VMEM((POOL_HALO + tm, pw), F32), pltpu.VMEM((tm + POOL_HALO, pw), F32)],
        compiler_params=_cparams(),
    )(dp, dp, z, z, pool_w, pool_scale)


def _mla_bwd(dq_h, dk_h, dv_h, z, dz_pool, h1, dh2, mix_norm, w_in, q_norm, wq_t, kv_norm, wkv, pos, rope_tab):
    t, d = h1.shape
    tm = min(256, t)

    def body(dqh_ref, dkh_ref, dvh_ref, z_ref, dzp_ref, h_ref, dh2_ref, gm_ref, win_ref, gq_ref, wq_ref, gkv_ref,
             wkv_ref, pos_ref, tab_ref, dh1_ref, dq_ref, dkv_ref, dz_ref, dgq_ref, dgkv_ref, dgm_ref):
        i = pl.program_id(0)
        first = i == 0
        cos_t, sin_t = _rope_tables(pos_ref[...], tab_ref[...])
        dq_parts, dkv_parts = [], []
        dk_pe = jnp.zeros((tm, 128), F32)
        for hh in range(MLA_HEADS):
            dqh = dqh_ref[hh]
            dq_parts += [dqh[:, 0:NOPE], _rope_apply_t(dqh[:, NOPE:HEAD_PAD], cos_t, sin_t)]
            dkh = dkh_ref[hh]
            dkv_parts += [dkh[:, 0:NOPE], dvh_ref[hh]]
            dk_pe = dk_pe + dkh[:, NOPE:HEAD_PAD]
        dqb = jnp.concatenate(dq_parts, axis=-1).astype(BF16)
        dkvb = jnp.concatenate(dkv_parts, axis=-1).astype(BF16)
        dq_ref[...] = dqb
        dkv_ref[...] = dkvb
        z = z_ref[...]
        c_q = z[:, 0:Q_RANK]
        gq = gq_ref[...]
        _, rq = _rms_fwd(c_q, gq)
        dcq, dgq = _rms_bwd(_dot_nn(dqb, wq_ref[...]), c_q, gq, rq)
        c_kv = z[:, Q_RANK:Q_RANK + KV_RANK]
        gkv = gkv_ref[...]
        _, rkv = _rms_fwd(c_kv, gkv)
        dckv, dgkv = _rms_bwd(_dot_nt(dkvb, wkv_ref[...]), c_kv, gkv, rkv)
        dkr = _rope_apply_t(dk_pe, cos_t, sin_t)
        dzb = jnp.concatenate([dcq, dckv, dkr, dzp_ref[...]], axis=-1).astype(BF16)
        dz_ref[...] = dzb
        x = h_ref[...]
        gm = gm_ref[...]
        _, rm = _rms_fwd(x, gm)
        dx, dgm = _rms_bwd(_dot_nt(dzb, win_ref[...]), x, gm, rm)
        dh1_ref[...] = dh2_ref[...] + dx
        _accumulate(dgq_ref, dgq, first)
        _accumulate(dgkv_ref, dgkv, first)
        _accumulate(dgm_ref, dgm, first)

    full = lambda shape: pl.BlockSpec(shape, lambda i: (0,) * len(shape))
    row = lambda w: pl.BlockSpec((tm, w), lambda i: (i, 0))
    head = lambda w: pl.BlockSpec((MLA_HEADS, tm, w), lambda i: (0, i, 0))
    pw = len(POOL_WINDOWS) * POOL_CH
    return pl.pallas_call(
        body, name="mla_bwd", grid=(t // tm,),
        in_specs=[head(HEAD_PAD), head(HEAD_PAD), head(V_DIM), row(d), row(pw), row(d), row(d),
                  full((1, d)), full(w_in.shape), full((1, Q_RANK)), full(wq_t.shape), full((1, KV_RANK)),
                  full(wkv.shape), row(1), full(rope_tab.shape)],
        out_specs=[row(d), row(d), row(d), row(d), full((1, Q_RANK)), full((1, KV_RANK)), full((1, d))],
        out_shape=[jax.ShapeDtypeStruct((t, d), F32), jax.ShapeDtypeStruct((t, d), BF16),
                   jax.ShapeDtypeStruct((t, d), BF16), jax.ShapeDtypeStruct((t, d), BF16),
                   jax.ShapeDtypeStruct((1, Q_RANK), F32), jax.ShapeDtypeStruct((1, KV_RANK), F32),
                   jax.ShapeDtypeStruct((1, d), F32)],
        compiler_params=_cparams(),
    )(dq_h, dk_h, dv_h, z, dz_pool, h1, dh2, mix_norm, w_in, q_norm, wq_t, kv_norm, wkv, pos, rope_tab)


def _mem_kv(mem, mem_norm, wmkv):
    n, d = mem.shape

    def body(mem_ref, g_ref, w_ref, memn_ref, k_ref, v_ref):
        y, _ = _rms_fwd(mem_ref[...], g_ref[...])
        yb = y.astype(BF16)
        memn_ref[...] = yb
        for hh in range(MEM_HEADS):
            k_ref[hh] = _dot_nn(yb, w_ref[hh]).astype(BF16)
            v_ref[hh] = _dot_nn(yb, w_ref[MEM_HEADS + hh]).astype(BF16)

    return pl.pallas_call(
        body, name="mem_kv",
        out_shape=[jax.ShapeDtypeStruct((n, d), BF16), jax.ShapeDtypeStruct((MEM_HEADS, n, MEM_HD), BF16),
                   jax.ShapeDtypeStruct((MEM_HEADS, n, MEM_HD), BF16)],
        compiler_params=_cparams(),
    )(mem, mem_norm, wmkv)


def _mem_softmax(qb, km):
    s = _dot_nt(qb, km) * MEM_SCALE
    e = jnp.exp(s - jnp.max(s, axis=-1, keepdims=True))
    return e / jnp.sum(e, axis=-1, keepdims=True)


def _xattn_fwd(h1, a, p, w_out, g, wmq, km, vm, wmo):
    t, d = h1.shape
    tm = min(512, t)
    half = a.shape[1]

    def body(h_ref, a_ref, p_ref, wo_ref, g_ref, wmq_ref, km_ref, vm_ref, wmo_ref,
             h2_ref, h3_ref, hn_ref, q_ref, o_ref):
        h2 = h_ref[...] + _dot_nn(a_ref[...], wo_ref[0:half, :]) + _dot_nn(p_ref[...], wo_ref[half:2 * half, :])
        h2_ref[...] = h2
        hn, _ = _rms_fwd(h2, g_ref[...])
        hnb = hn.astype(BF16)
        hn_ref[...] = hnb
        qb = _dot_nn(hnb, wmq_ref[...]).astype(BF16)
        q_ref[...] = qb
        outs = []
        for hh in range(MEM_HEADS):
            pr = _mem_softmax(qb[:, hh * MEM_HD:(hh + 1) * MEM_HD], km_ref[hh])
            outs.append(_dot_nn(pr.astype(BF16), vm_ref[hh]))
        ob = jnp.concatenate(outs, axis=-1).astype(BF16)
        o_ref[...] = ob
        h3_ref[...] = h2 + _dot_nn(ob, wmo_ref[...])

    full = lambda shape: pl.BlockSpec(shape, lambda i: (0,) * len(shape))
    row = lambda w: pl.BlockSpec((tm, w), lambda i: (i, 0))
    return pl.pallas_call(
        body, name="xattn_fwd", grid=(t // tm,),
        in_specs=[row(d), row(half), row(half), full(w_out.shape), full((1, d)), full(wmq.shape),
                  full(km.shape), full(vm.shape), full(wmo.shape)],
        out_specs=[row(d), row(d), row(d), row(d), row(d)],
        out_shape=[jax.ShapeDtypeStruct((t, d), F32), jax.ShapeDtypeStruct((t, d), F32),
                   jax.ShapeDtypeStruct((t, d), BF16), jax.ShapeDtypeStruct((t, d), BF16),
                   jax.ShapeDtypeStruct((t, d), BF16)],
        compiler_params=_cparams(),
    )(h1, a, p, w_out, g, wmq, km, vm, wmo)


def _xattn_bwd(dh3, h2, qm, g, wmq, km, vm, wmo, w_out, token=None):
    t, d = h2.shape
    tm = min(256, t)
    half = d // 2

    def body(dh3_ref, h2_ref, q_ref, g_ref, wmq_ref, km_ref, vm_ref, wmo_ref, wo_ref,
             dh2_ref, dq_ref, da_ref, dp_ref, dk_ref, dv_ref, dg_ref):
        i = pl.program_id(0)
        first = i == 0

        @pl.when(first)
        def _():
            dk_ref[...] = jnp.zeros_like(dk_ref)
            dv_ref[...] = jnp.zeros_like(dv_ref)

        dh3 = dh3_ref[...]
        dob = _dot_nt(dh3.astype(BF16), wmo_ref[...]).astype(BF16)
        qb = q_ref[...]
        dq_parts = []
        for hh in range(MEM_HEADS):
            cols = slice(hh * MEM_HD, (hh + 1) * MEM_HD)
            kk, vv = km_ref[hh], vm_ref[hh]
            pr = _mem_softmax(qb[:, cols], kk)
            doh = dob[:, cols]
            dv_ref[hh] += _dot_tn(pr.astype(BF16), doh)
            dpp = _dot_nt(doh, vv)
            dsb = (pr * (dpp - jnp.sum(dpp * pr, axis=-1, keepdims=True)) * MEM_SCALE).astype(BF16)
            dq_parts.append(_dot_nn(dsb, kk))
            dk_ref[hh] += _dot_tn(dsb, qb[:, cols])
        dqb = jnp.concatenate(dq_parts, axis=-1).astype(BF16)
        dq_ref[...] = dqb
        x = h2_ref[...]
        gg = g_ref[...]
        _, r = _rms_fwd(x, gg)
        dx, dg = _rms_bwd(_dot_nt(dqb, wmq_ref[...]), x, gg, r)
        dh2 = dh3 + dx
        dh2_ref[...] = dh2
        dap = _dot_nt(dh2.astype(BF16), wo_ref[...])
        da_ref[...] = dap[:, 0:half].astype(BF16)
        dp_ref[...] = dap[:, half:d].astype(BF16)
        _accumulate(dg_ref, dg, first)

    full = lambda shape: pl.BlockSpec(shape, lambda i: (0,) * len(shape))
    row = lambda w: pl.BlockSpec((tm, w), lambda i: (i, 0))
    return _call_after(
        token, body,
        [row(d), row(d), row(d), full((1, d)), full(wmq.shape), full(km.shape), full(vm.shape),
         full(wmo.shape), full(w_out.shape)],
        (dh3, h2, qm, g, wmq, km, vm, wmo, w_out),
        name="xattn_bwd", grid=(t // tm,),
        out_specs=[row(d), row(d), row(half), row(half), full(km.shape), full(vm.shape), full((1, d))],
        out_shape=[jax.ShapeDtypeStruct((t, d), F32), jax.ShapeDtypeStruct((t, d), BF16),
                   jax.ShapeDtypeStruct((t, half), BF16), jax.ShapeDtypeStruct((t, half), BF16),
                   jax.ShapeDtypeStruct(km.shape, F32), jax.ShapeDtypeStruct(vm.shape, F32),
                   jax.ShapeDtypeStruct((1, d), F32)],
        compiler_params=_cparams(),
    )


def _mem_kv_bwd(dkm, dvm, memn, mem, mem_norm, wmkv):
    n, d = mem.shape

    def body(dk_ref, dv_ref, memn_ref, mem_ref, g_ref, w_ref, dw_ref, dg_ref):
        memn = memn_ref[...]
        dmemn = jnp.zeros((n, d), F32)
        for s in range(2 * MEM_HEADS):
            src = dk_ref[s] if s < MEM_HEADS else dv_ref[s - MEM_HEADS]
            db = src.astype(BF16)
            dw_ref[s] = _dot_tn(memn, db)
            dmemn = dmemn + _dot_nt(db, w_ref[s])
        x = mem_ref[...]
        gg = g_ref[...]
        _, r = _rms_fwd(x, gg)
        _, dg = _rms_bwd(dmemn, x, gg, r)
        dg_ref[...] = dg

    return pl.pallas_call(
        body, name="mem_kv_bwd",
        out_shape=[jax.ShapeDtypeStruct(wmkv.shape, F32), jax.ShapeDtypeStruct((1, d), F32)],
        compiler_params=_cparams(),
    )(dkm, dvm, memn, mem, mem_norm, wmkv)


MESH_ID = pl.DeviceIdType.MESH
ANY = pl.BlockSpec(memory_space=pl.ANY)


def _coords():
    return lax.axis_index("x"), lax.axis_index("y"), lax.axis_index("c")


def _other_chips(x, y):
    return [(1 - x, y), (x, 1 - y), (1 - x, 1 - y)]


def _all_gather_direct(shard, token):
    r, w = shard.shape

    def body(tok_ref, x_ref, out_ref, send_sems, recv_sems, local_sem):
        x, y, c = _coords()
        mine = pltpu.make_async_copy(x_ref, out_ref.at[4 * x + 2 * y + c], local_sem)
        mine.start()
        arrivals = []
        for k in range(1, N_DEV):
            px, py, pc = x ^ ((k >> 2) & 1), y ^ ((k >> 1) & 1), c ^ (k & 1)
            pltpu.make_async_remote_copy(
                src_ref=x_ref, dst_ref=out_ref.at[4 * x + 2 * y + c],
                send_sem=send_sems.at[k - 1], recv_sem=recv_sems.at[k - 1],
                device_id=(px, py, pc), device_id_type=MESH_ID).start()
            arrivals.append(pltpu.make_async_remote_copy(
                src_ref=x_ref, dst_ref=out_ref.at[4 * px + 2 * py + pc],
                send_sem=send_sems.at[k - 1], recv_sem=recv_sems.at[k - 1],
                device_id=(px, py, pc), device_id_type=MESH_ID))
        for cp in arrivals:
            cp.wait_recv()
        for cp in arrivals:
            cp.wait_send()
        mine.wait()

    return pl.pallas_call(
        body, name="all_gather_direct",
        out_shape=jax.ShapeDtypeStruct((N_DEV, r, w), shard.dtype),
        in_specs=[pl.BlockSpec(memory_space=pltpu.VMEM)] * 2, out_specs=pl.BlockSpec(memory_space=pltpu.VMEM),
        scratch_shapes=[pltpu.SemaphoreType.DMA((7,)), pltpu.SemaphoreType.DMA((7,)), pltpu.SemaphoreType.DMA],
        compiler_params=_cparams(),
    )(token, shard)


def _exchange_core(g, tag):
    _, r, w = g.shape

    def body(g_ref, land_ref, send_sems, recv_sems):
        x, y, c = _coords()
        copies = []
        for chip in range(4):
            copies.append(pltpu.make_async_remote_copy(
                src_ref=g_ref.at[2 * chip + (1 - c)], dst_ref=land_ref.at[chip],
                send_sem=send_sems.at[chip], recv_sem=recv_sems.at[chip],
                device_id=(x, y, 1 - c), device_id_type=MESH_ID))
        for cp in copies:
            cp.start()
        for cp in copies:
            cp.wait_recv()
        for cp in copies:
            cp.wait_send()

    return pl.pallas_call(
        body, name="exchange_core_" + tag,
        out_shape=jax.ShapeDtypeStruct((4, r, w), g.dtype),
        in_specs=[ANY], out_specs=ANY,
        scratch_shapes=[pltpu.SemaphoreType.DMA((4,)), pltpu.SemaphoreType.DMA((4,))],
    )(g)


def _chip_partial(g, land, cidx, tag):
    _, r, w = g.shape
    tr = _row_tile(r, 1024)
    g4 = g.reshape(4, 2, r, w)

    def body(c_ref, g_ref, l_ref, o_ref):
        o_ref[0] = (g_ref[0, 0].astype(F32) + l_ref[0].astype(F32)).astype(o_ref.dtype)

    return pl.pallas_call(
        body, name="chip_partial_" + tag,
        grid_spec=pltpu.PrefetchScalarGridSpec(
            num_scalar_prefetch=1, grid=(4, r // tr),
            in_specs=[pl.BlockSpec((1, 1, tr, w), lambda i, j, s: (i, s[0], j, 0)),
                      pl.BlockSpec((1, tr, w), lambda i, j, s: (i, j, 0))],
            out_specs=pl.BlockSpec((1, tr, w), lambda i, j, s: (i, j, 0))),
        out_shape=jax.ShapeDtypeStruct((4, r, w), g.dtype),
        compiler_params=_cparams(),
    )(cidx, g4, land)


HBM_SPEC = pl.BlockSpec(memory_space=pltpu.HBM)
SEM_SPEC = pl.BlockSpec(memory_space=pltpu.SEMAPHORE)
SPLIT_EFFECT = pltpu.SideEffectType.DATAFLOW_SIDE_EFFECTING


def _ici_block(src_ref, px, py):
    return src_ref if len(src_ref.shape) == 2 else src_ref.at[2 * px + py]


def _ici_slot(gather, j, px, py, c):
    return 4 * px + 2 * py + c if gather else j


def _ici_start(src, after, name):
    r, w = src.shape[-2:]
    gather = len(src.shape) == 2
    land_shape = (N_DEV if gather else 3, r, w)

    def body(src_ref, land_ref, after_ref, send_sems, recv_sems, src_thru, land_thru, token):
        x, y, c = _coords()
        for j, (px, py) in enumerate(_other_chips(x, y)):
            pltpu.make_async_remote_copy(
                src_ref=_ici_block(src_ref, px, py), dst_ref=land_ref.at[_ici_slot(gather, j, x, y, c)],
                send_sem=send_sems.at[j], recv_sem=recv_sems.at[j],
                device_id=(px, py, c), device_id_type=MESH_ID).start()
        token[...] = jnp.zeros_like(token)

    return pl.pallas_call(
        body, name=name,
        out_shape=(pltpu.SemaphoreType.DMA((3,)), pltpu.SemaphoreType.DMA((3,)), pltpu.HBM(src.shape, src.dtype),
                   pltpu.HBM(land_shape, src.dtype), jax.ShapeDtypeStruct((8, 128), F32)),
        in_specs=(HBM_SPEC, HBM_SPEC, ANY),
        out_specs=(SEM_SPEC, SEM_SPEC, HBM_SPEC, HBM_SPEC, pl.BlockSpec(memory_space=pltpu.VMEM)),
        input_output_aliases={0: 2, 1: 3},
        compiler_params=pltpu.CompilerParams(has_side_effects=SPLIT_EFFECT),
    )(pltpu.with_memory_space_constraint(src, pltpu.HBM),
      pltpu.with_memory_space_constraint(lax.empty(land_shape, src.dtype), pltpu.HBM), after)


def _ici_wait(started, after, name):
    send_sems, recv_sems, src_thru, land_thru, _ = started
    gather = len(src_thru.shape) == 2

    def body(src_ref, land_ref, send_sems, recv_sems, after_ref, src_dead, got_ref):
        x, y, c = _coords()
        for j, (px, py) in enumerate(_other_chips(x, y)):
            copy = pltpu.make_async_remote_copy(
                src_ref=_ici_block(src_ref, px, py), dst_ref=land_ref.at[_ici_slot(gather, j, px, py, c)],
                send_sem=send_sems.at[j], recv_sem=recv_sems.at[j],
                device_id=(px, py, c), device_id_type=MESH_ID)
            copy.wait_send()
            copy.wait_recv()

    return pl.pallas_call(
        body, name=name,
        out_shape=(pltpu.HBM(src_thru.shape, src_thru.dtype), pltpu.HBM(land_thru.shape, land_thru.dtype)),
        in_specs=(HBM_SPEC, HBM_SPEC, SEM_SPEC, SEM_SPEC, ANY),
        out_specs=(HBM_SPEC, HBM_SPEC), input_output_aliases={0: 0, 1: 1},
        compiler_params=pltpu.CompilerParams(has_side_effects=SPLIT_EFFECT),
    )(src_thru, land_thru, send_sems, recv_sems, after)


def _core_share(own, gathered, name):
    r, w = own.shape

    def body(own_ref, gin_ref, out_ref, stage, send_sems, recv_sems, local_sem):
        x, y, c = _coords()
        sibling = (x, y, 1 - c)
        chips = [(x, y)] + _other_chips(x, y)
        stage_in = pltpu.make_async_copy(own_ref, stage, local_sem)
        stage_in.start()
        sent, arriving = [], []
        for k, (px, py) in enumerate(chips):
            rows = out_ref.at[4 * px + 2 * py + c]
            sent.append(pltpu.make_async_remote_copy(
                src_ref=own_ref if k == 0 else rows, dst_ref=rows,
                send_sem=send_sems.at[k], recv_sem=recv_sems.at[k], device_id=sibling, device_id_type=MESH_ID))
            arriving.append(pltpu.make_async_remote_copy(
                src_ref=own_ref, dst_ref=out_ref.at[4 * px + 2 * py + (1 - c)],
                send_sem=send_sems.at[k], recv_sem=recv_sems.at[k], device_id=sibling, device_id_type=MESH_ID))
        for cp in sent:
            cp.start()
        stage_in.wait()
        stage_out = pltpu.make_async_copy(stage, out_ref.at[4 * x + 2 * y + c], local_sem)
        stage_out.start()
        for cp in arriving:
            cp.wait_recv()
        for cp in sent:
            cp.wait_send()
        stage_out.wait()

    return pl.pallas_call(
        body, name=name,
        out_shape=jax.ShapeDtypeStruct((N_DEV, r, w), own.dtype),
        in_specs=[ANY, ANY], out_specs=ANY, input_output_aliases={1: 0},
        scratch_shapes=[pltpu.VMEM((r, w), own.dtype), pltpu.SemaphoreType.DMA((4,)),
                        pltpu.SemaphoreType.DMA((4,)), pltpu.SemaphoreType.DMA],
    )(own, gathered)


def _adamw(w, g, m, v):
    m = ADAM_B1 * m + (1.0 - ADAM_B1) * g
    v = ADAM_B2 * v + (1.0 - ADAM_B2) * (g * g)
    m_hat = m / ADAM_C1
    v_hat = v / ADAM_C2
    delta = -ADAM_LR * (m_hat / (jnp.sqrt(v_hat) + ADAM_EPS) + ADAM_WD * w)
    return delta, m, v


def _adam_big(part, land, w, m, v, chip_idx, tag, token):
    r, wd = w.shape
    tr, tw = _row_tile(r, 1024), 256

    def body(s_ref, tok_ref, p_ref, l_ref, w_ref, m_ref, v_ref, g_ref, d_ref, mo_ref, vo_ref):
        g = p_ref[0].astype(F32)
        for j in range(3):
            g = g + l_ref[j].astype(F32)
        delta, mn, vn = _adamw(w_ref[...], g, m_ref[...], v_ref[...])
        g_ref[...] = g
        d_ref[...] = delta
        mo_ref[...] = mn
        vo_ref[...] = vn

    row = pl.BlockSpec((tr, tw), lambda i, j, s: (i, j))
    return pl.pallas_call(
        body, name="adam_big_" + tag,
        grid_spec=pltpu.PrefetchScalarGridSpec(
            num_scalar_prefetch=1, grid=(r // tr, wd // tw),
            in_specs=[pl.BlockSpec((8, 128), lambda i, j, s: (0, 0)),
                      pl.BlockSpec((1, tr, tw), lambda i, j, s: (s[0], i, j)),
                      pl.BlockSpec((3, tr, tw), lambda i, j, s: (0, i, j)), row, row, row],
            out_specs=[row, row, row, row]),
        out_shape=[jax.ShapeDtypeStruct((r, wd), F32)] * 4,
        compiler_params=_cparams(),
    )(chip_idx, token, part, land, w, m, v)


def _adam_small(parts, w, m, v):
    _, r, wd = parts.shape

    def body(p_ref, w_ref, m_ref, v_ref, g_ref, d_ref, mo_ref, vo_ref):
        g = p_ref[0]
        for k in range(1, N_DEV):
            g = g + p_ref[k]
        delta, mn, vn = _adamw(w_ref[...], g, m_ref[...], v_ref[...])
        g_ref[...] = g
        d_ref[...] = delta
        mo_ref[...] = mn
        vo_ref[...] = vn

    return pl.pallas_call(
        body, name="adam_small",
        out_shape=[jax.ShapeDtypeStruct((r, wd), F32)] * 4,
        compiler_params=_cparams(),
    )(parts, w, m, v)


def _pad_rows(a, rows):
    return jnp.pad(a, ((0, rows - a.shape[0]), (0, 0)))


def _pad_w_in(w):
    cut = Q_RANK + KV_RANK + ROPE
    return jnp.concatenate([w[:, :cut], jnp.zeros((w.shape[0], 64), w.dtype), w[:, cut:]], axis=1)


def _unpad_w_in(w):
    cut = Q_RANK + KV_RANK + ROPE
    return jnp.concatenate([w[:, :cut], w[:, cut + 64:]], axis=1)


def _pack_big(p, group):
    if group == "mid":
        parts = [_pad_w_in(p["w_in"][0]), p["w_out"][0], p["w_mq"][0], p["w_mo"][0],
                 p["w_mkv"][0].reshape(256, D_MODEL),
                 _pad_rows(p["w_q_up"][0].T.reshape(24, D_MODEL), 32),
                 p["w_kv_up"][0].reshape(16, D_MODEL)]
    else:
        parts = [p[group + "_w_gate"][0].T, p[group + "_w_up"][0].T, p[group + "_w_down"][0]]
    return jnp.concatenate(parts, axis=0)


def _unpack_big(a, group):
    seg = lambda n: a[SEG_OFF[n][0]:SEG_OFF[n][0] + SEG_OFF[n][1]]
    if group == "mid":
        return {"w_in": _unpad_w_in(seg("w_in"))[None], "w_out": seg("w_out")[None], "w_mq": seg("w_mq")[None],
                "w_mo": seg("w_mo")[None], "w_mkv": seg("w_mkv").reshape(D_MODEL, 256)[None],
                "w_q_up": seg("w_q")[:24].reshape(96, Q_RANK).T[None],
                "w_kv_up": seg("w_kv").reshape(KV_RANK, 128)[None]}
    return {group + "_w_gate": seg(group + "_g").T[None], group + "_w_up": seg(group + "_u").T[None],
            group + "_w_down": seg(group + "_d")[None]}


def _unpack_gathered(full, group):
    seg = lambda n: full[:, SEG_OFF[n][0]:SEG_OFF[n][0] + SEG_OFF[n][1]]
    rows = lambda n: seg(n).reshape(-1, D_MODEL)
    if group != "mid":
        return {n: rows(n) for n, _ in GROUP_SEGS[group]}
    wq_t = seg("w_q")[:, :24].reshape(MLA_HEADS, NOPE + ROPE, Q_RANK)
    wq_t = jnp.pad(wq_t, ((0, 0), (0, HEAD_PAD - NOPE - ROPE), (0, 0))).reshape(MLA_HEADS * HEAD_PAD, Q_RANK)
    wkv = seg("w_kv").reshape(N_DEV, KV_RANK, 128).transpose(1, 0, 2).reshape(KV_RANK, N_DEV * 128)
    return {"w_in": rows("w_in"), "w_out": rows("w_out"), "w_mq": rows("w_mq"), "w_mo": rows("w_mo"),
            "w_mkv": seg("w_mkv").reshape(N_DEV, D_MODEL, 256), "w_q": wq_t, "w_kv": wkv}


def _pack_grads(gr, group):
    blk = lambda a: a.reshape(N_DEV, -1, D_MODEL)
    if group == "mid":
        dwq = gr["w_q"].reshape(MLA_HEADS, HEAD_PAD, Q_RANK)[:, :NOPE + ROPE].reshape(N_DEV, 24, D_MODEL)
        dwq = jnp.pad(dwq, ((0, 0), (0, 8), (0, 0)))
        dwkv = gr["w_kv"].reshape(KV_RANK, N_DEV, 128).transpose(1, 0, 2).reshape(N_DEV, 16, D_MODEL)
        parts = [blk(gr["w_in"]), blk(gr["w_out"]), blk(gr["w_mq"]), blk(gr["w_mo"]),
                 gr["w_mkv"].reshape(N_DEV, 256, D_MODEL), dwq, dwkv]
    else:
        parts = [blk(gr[n]) for n, _ in GROUP_SEGS[group]]
    return jnp.concatenate([a.astype(BF16) for a in parts], axis=1)


def _pack_small(vals):
    parts = []
    for n, r in SMALL_ROWS:
        parts.append(_pad_rows(vals[n].reshape(-1, 128), r) if n in vals else jnp.zeros((r, 128), F32))
    return jnp.concatenate(parts, axis=0)


def _unpack_small(a, shapes):
    out = {}
    for n, shape in shapes.items():
        o = SMALL_OFF[n][0]
        out[n] = a[o:o + int(np.prod(shape)) // 128].reshape(shape)
    return out


BIG_NAMES = ("ffn1_w_gate", "ffn1_w_up", "ffn1_w_down", "w_in", "w_q_up", "w_kv_up", "w_out", "w_mq", "w_mkv",
             "w_mo", "ffn2_w_gate", "ffn2_w_up", "ffn2_w_down")
SMALL_NAMES = ("ffn1_norm", "mix_norm", "q_norm", "kv_norm", "pool_w", "pool_scale", "xattn_norm", "mem_norm",
               "ffn2_norm", "final_norm")
WEIGHT_ORDER = ("ffn1_norm", "ffn1_w_gate", "ffn1_w_up", "ffn1_w_down", "mix_norm", "w_in", "q_norm", "w_q_up",
                "kv_norm", "w_kv_up", "pool_w", "pool_scale", "w_out", "xattn_norm", "mem_norm", "w_mq", "w_mkv",
                "w_mo", "ffn2_norm", "ffn2_w_gate", "ffn2_w_up", "ffn2_w_down", "final_norm")


def _rope_table():
    lane = np.arange(128)
    freqs = (1.0 / (ROPE_BASE ** (np.arange(0, ROPE, 2, dtype=np.float32) / ROPE))).astype(np.float32)
    tab = np.zeros((8, 128), np.float32)
    tab[0] = np.where(lane < ROPE, freqs[lane % (ROPE // 2)], 0.0)
    tab[1] = np.where(lane < ROPE // 2, -1.0, np.where(lane < ROPE, 1.0, 0.0))
    return jnp.asarray(tab)


def kernel(x, mem, positions, ffn1_norm, ffn1_w_gate, ffn1_w_up, ffn1_w_down, mix_norm, w_in, q_norm, w_q_up, kv_norm, w_kv_up, pool_w, pool_scale, w_out, xattn_norm, mem_norm, w_mq, w_mkv, w_mo, ffn2_norm, ffn2_w_gate, ffn2_w_up, ffn2_w_down, final_norm, loss_target, m_ffn1_norm, m_ffn1_w_gate, m_ffn1_w_up, m_ffn1_w_down, m_mix_norm, m_w_in, m_q_norm, m_w_q_up, m_kv_norm, m_w_kv_up, m_pool_w, m_pool_scale, m_w_out, m_xattn_norm, m_mem_norm, m_w_mq, m_w_mkv, m_w_mo, m_ffn2_norm, m_ffn2_w_gate, m_ffn2_w_up, m_ffn2_w_down, m_final_norm, v_ffn1_norm, v_ffn1_w_gate, v_ffn1_w_up, v_ffn1_w_down, v_mix_norm, v_w_in, v_q_norm, v_w_q_up, v_kv_norm, v_w_kv_up, v_pool_w, v_pool_scale, v_w_out, v_xattn_norm, v_mem_norm, v_w_mq, v_w_mkv, v_w_mo, v_ffn2_norm, v_ffn2_w_gate, v_ffn2_w_up, v_ffn2_w_down, v_final_norm):
    wts = dict(ffn1_norm=ffn1_norm, ffn1_w_gate=ffn1_w_gate, ffn1_w_up=ffn1_w_up, ffn1_w_down=ffn1_w_down,
               mix_norm=mix_norm, w_in=w_in, q_norm=q_norm, w_q_up=w_q_up, kv_norm=kv_norm, w_kv_up=w_kv_up,
               pool_w=pool_w, pool_scale=pool_scale, w_out=w_out, xattn_norm=xattn_norm, mem_norm=mem_norm,
               w_mq=w_mq, w_mkv=w_mkv, w_mo=w_mo, ffn2_norm=ffn2_norm, ffn2_w_gate=ffn2_w_gate,
               ffn2_w_up=ffn2_w_up, ffn2_w_down=ffn2_w_down, final_norm=final_norm)
    mom = dict(ffn1_norm=m_ffn1_norm, ffn1_w_gate=m_ffn1_w_gate, ffn1_w_up=m_ffn1_w_up, ffn1_w_down=m_ffn1_w_down,
               mix_norm=m_mix_norm, w_in=m_w_in, q_norm=m_q_norm, w_q_up=m_w_q_up, kv_norm=m_kv_norm,
               w_kv_up=m_w_kv_up, pool_w=m_pool_w, pool_scale=m_pool_scale, w_out=m_w_out, xattn_norm=m_xattn_norm,
               mem_norm=m_mem_norm, w_mq=m_w_mq, w_mkv=m_w_mkv, w_mo=m_w_mo, ffn2_norm=m_ffn2_norm,
               ffn2_w_gate=m_ffn2_w_gate, ffn2_w_up=m_ffn2_w_up, ffn2_w_down=m_ffn2_w_down, final_norm=m_final_norm)
    var = dict(ffn1_norm=v_ffn1_norm, ffn1_w_gate=v_ffn1_w_gate, ffn1_w_up=v_ffn1_w_up, ffn1_w_down=v_ffn1_w_down,
               mix_norm=v_mix_norm, w_in=v_w_in, q_norm=v_q_norm, w_q_up=v_w_q_up, kv_norm=v_kv_norm,
               w_kv_up=v_w_kv_up, pool_w=v_pool_w, pool_scale=v_pool_scale, w_out=v_w_out, xattn_norm=v_xattn_norm,
               mem_norm=v_mem_norm, w_mq=v_w_mq, w_mkv=v_w_mkv, w_mo=v_w_mo, ffn2_norm=v_ffn2_norm,
               ffn2_w_gate=v_ffn2_w_gate, ffn2_w_up=v_ffn2_w_up, ffn2_w_down=v_ffn2_w_down, final_norm=v_final_norm)

    t = x.shape[1]
    xs = x[0]
    mems = mem[0]
    target = loss_target[0]
    pos = positions.reshape(t, 1)
    row = lambda a: a.reshape(1, -1)
    rope_tab = _rope_table()

    cx, cy, cc = _coords()
    core_idx = cc.astype(jnp.int32).reshape(1)
    chip_idx = (2 * cx + cy).astype(jnp.int32).reshape(1)

    w_pack, wb = {}, {}
    for grp in ("ffn1", "mid", "ffn2"):
        w_pack[grp] = _pack_big(wts, grp)
        wb[grp] = w_pack[grp].astype(BF16)
        if grp == "ffn1":
            ag_ffn1 = _ici_start(wb["ffn1"], pos, "ag_ffn1_start")
    own_ffn1, land_ffn1 = _ici_wait(ag_ffn1, wb["ffn2"], "ag_ffn1_wait")
    full_ffn1 = _core_share(own_ffn1, land_ffn1, "ag_ffn1_share")
    fw = _unpack_gathered(full_ffn1, "ffn1")
    ag_mid = _ici_start(wb["mid"], full_ffn1, "ag_mid_start")
    g_ffn1, g_mix, g_q, g_kv = row(ffn1_norm), row(mix_norm), row(q_norm), row(kv_norm)
    g_x, g_mem, g_ffn2, g_fin = row(xattn_norm), row(mem_norm), row(ffn2_norm), row(final_norm)
    pool_wb = pool_w[0].astype(BF16)
    pool_sc = row(pool_scale)

    h1, n1, gate1, up1 = _ffn_fwd(xs, g_ffn1, fw["ffn1_g"], fw["ffn1_u"], fw["ffn1_d"], "ffn1_fwd", token=ag_mid[4])
    own_mid, land_mid = _ici_wait(ag_mid, h1, "ag_mid_wait")
    full_mid = _core_share(own_mid, land_mid, "ag_mid_share")
    fw.update(_unpack_gathered(full_mid, "mid"))
    ag_ffn2 = _ici_start(wb["ffn2"], full_mid, "ag_ffn2_start")
    u, z, qn, kvn, qh, kh, vh = _mix_prep(h1, g_mix, fw["w_in"], g_q, fw["w_q"], g_kv, fw["w_kv"], pos, rope_tab,
                                          token=ag_ffn2[4])
    a, lse = _attn_fwd(qh, kh, vh)
    p = _pool_fwd(z, pool_wb, pool_sc)
    memn, km, vm = _mem_kv(mems, g_mem, fw["w_mkv"])
    h2, h3, hn, qm, om = _xattn_fwd(h1, a, p, fw["w_out"], g_x, fw["w_mq"], km, vm, fw["w_mo"])
    own_ffn2, land_ffn2 = _ici_wait(ag_ffn2, h3, "ag_ffn2_wait")
    fw.update(_unpack_gathered(_core_share(own_ffn2, land_ffn2, "ag_ffn2_share"), "ffn2"))
    h4, n2, gate2, up2 = _ffn_fwd(h3, g_ffn2, fw["ffn2_g"], fw["ffn2_u"], fw["ffn2_d"], "ffn2_fwd")
    loss_part, dh4, dg_fin = _loss_head(h4, target, g_fin)

    gr = {}
    dh3, dgate2, dup2, act2, dg_ffn2 = _ffn_bwd_data(dh4, h3, g_ffn2, gate2, up2, fw["ffn2_g"], fw["ffn2_u"],
                                                     fw["ffn2_d"], "ffn2_bwd")
    gr["ffn2_g"] = _tn_matmul(dgate2, n2, "ffn2_dwg", tmm=1408)
    gr["ffn2_u"] = _tn_matmul(dup2, n2, "ffn2_dwu", tmm=1408)
    gr["ffn2_d"] = _tn_matmul(act2, dh4, "ffn2_dwd", scale=0.5, tmm=1408)
    g_ffn2_pack = _pack_grads(gr, "ffn2")
    part_ffn2 = _chip_partial(g_ffn2_pack, _exchange_core(g_ffn2_pack, "ffn2"), core_idx, "ffn2")
    rs_ffn2 = _ici_start(part_ffn2, g_ffn2_pack, "rs_ffn2_start")
    dh2, dqm, da, dp, dkm, dvm, dg_x = _xattn_bwd(dh3, h2, qm, g_x, fw["w_mq"], km, vm, fw["w_mo"], fw["w_out"],
                                                  token=rs_ffn2[4])
    gr["w_mo"] = _tn_matmul(om, dh3, "dw_mo", tmm=512)
    gr["w_mq"] = _tn_matmul(hn, dqm, "dw_mq", tmm=512)
    gr["w_out"] = jnp.concatenate([_tn_matmul(a, dh2, "dw_out_a"), _tn_matmul(p, dh2, "dw_out_p")], axis=0)
    gr["w_mkv"], dg_mem = _mem_kv_bwd(dkm, dvm, memn, mems, g_mem, fw["w_mkv"])
    dz_pool, d_pool_w, d_pool_sc = _pool_bwd(dp, z, pool_wb, pool_sc)
    dqh, dkh, dvh = _attn_bwd(qh, kh, vh, da, lse, _attn_delta(a, da))
    dh1, dq, dkv, dz, dg_q, dg_kv, dg_mix = _mla_bwd(dqh, dkh, dvh, z, dz_pool, h1, dh2, g_mix, fw["w_in"], g_q,
                                                     fw["w_q"], g_kv, fw["w_kv"], pos, rope_tab)
    gr["w_q"] = _tn_matmul(dq, qn, "dw_q", tmm=512)
    gr["w_kv"] = _tn_matmul(kvn, dkv, "dw_kv")
    gr["w_in"] = _tn_matmul(u, dz, "dw_in", tmm=512)
    g_mid_pack = _pack_grads(gr, "mid")
    part_mid = _chip_partial(g_mid_pack, _exchange_core(g_mid_pack, "mid"), core_idx, "mid")
    part_ffn2, land_ffn2_g = _ici_wait(rs_ffn2, part_mid, "rs_ffn2_wait")
    rs_mid = _ici_start(part_mid, land_ffn2_g, "rs_mid_start")
    dx, dgate1, dup1, act1, dg_ffn1 = _ffn_bwd_data(dh1, xs, g_ffn1, gate1, up1, fw["ffn1_g"], fw["ffn1_u"],
                                                    fw["ffn1_d"], "ffn1_bwd", token=rs_mid[4])
    part_mid, land_mid_g = _ici_wait(rs_mid, dx, "rs_mid_wait")
    gr["ffn1_g"] = _tn_matmul(dgate1, n1, "ffn1_dwg", tmm=1408)
    gr["ffn1_u"] = _tn_matmul(dup1, n1, "ffn1_dwu", tmm=1408)
    gr["ffn1_d"] = _tn_matmul(act1, dh1, "ffn1_dwd", scale=0.5, tmm=1408)
    g_ffn1_pack = _pack_grads(gr, "ffn1")
    part_ffn1 = _chip_partial(g_ffn1_pack, _exchange_core(g_ffn1_pack, "ffn1"), core_idx, "ffn1")
    rs_ffn1 = _ici_start(part_ffn1, g_ffn1_pack, "rs_ffn1_start")

    small_g = dict(ffn1_norm=dg_ffn1, mix_norm=dg_mix, q_norm=dg_q, kv_norm=dg_kv, pool_w=d_pool_w,
                   pool_scale=d_pool_sc, xattn_norm=dg_x, mem_norm=dg_mem, ffn2_norm=dg_ffn2, final_norm=dg_fin,
                   loss=loss_part)
    parts = _all_gather_direct(_pack_small(small_g), rs_ffn1[4])
    small = _adam_small(parts, _pack_small({n: wts[n] for n in SMALL_NAMES}),
                        _pack_small({n: mom[n] for n in SMALL_NAMES}), _pack_small({n: var[n] for n in SMALL_NAMES}))
    small_sum = small[0]
    loss = small_sum[SMALL_OFF["loss"][0], 0]
    shapes = {n: wts[n].shape for n in SMALL_NAMES}
    small = [_unpack_small(s, shapes) for s in small]

    big = {}

    def adam_group(grp, part, land, token):
        res = _adam_big(part, land, w_pack[grp], _pack_big(mom, grp), _pack_big(var, grp), chip_idx, grp, token)
        for k, packed in enumerate(res):
            big.setdefault(k, {}).update(_unpack_big(packed, grp))
        return res[0]

    done = adam_group("mid", part_mid, land_mid_g, small_sum)
    done = adam_group("ffn2", part_ffn2, land_ffn2_g, done)
    part_ffn1, land_ffn1_g = _ici_wait(rs_ffn1, done, "rs_ffn1_wait")
    adam_group("ffn1", part_ffn1, land_ffn1_g, done)

    outs = [loss, dx[None]]
    for k in range(4):
        for n in WEIGHT_ORDER:
            outs.append(big[k][n] if n in BIG_NAMES else small[k][n])
    return tuple(outs)
```

```python
import numpy as np

import jax
import jax.numpy as jnp
from jax import lax
from jax.experimental import pallas as pl
from jax.experimental.pallas import tpu as pltpu

F32 = jnp.float32
BF16 = jnp.bfloat16

N_DEV = 8
D_MODEL = 1024
D_FF = 2816
MLA_HEADS = 4
NOPE = 128
ROPE = 64
HEAD_PAD = 256
V_DIM = 128
Q_RANK = 256
KV_RANK = 128
POOL_WINDOWS = (2, 4, 8, 16)
POOL_CH = 128
POOL_HALO = 16
N_MEM = 256
MEM_HEADS = 4
MEM_HD = 256
ROPE_BASE = 10000.0
RMS_EPS = 1e-6
ATTN_SCALE = (NOPE + ROPE) ** -0.5
MEM_SCALE = MEM_HD ** -0.5
NEG_BIG = -1e30

ADAM_LR = 0.001
ADAM_B1 = 0.9
ADAM_B2 = 0.999
ADAM_EPS = 1e-08
ADAM_WD = 0.01
ADAM_STEP = 10
ADAM_C1 = 1.0 - ADAM_B1 ** ADAM_STEP
ADAM_C2 = 1.0 - ADAM_B2 ** ADAM_STEP

VMEM_LIMIT_BYTES = 52 * 1024 * 1024
BF16_ROWS = 16

GROUP_SEGS = {
    "ffn1": (("ffn1_g", 352), ("ffn1_u", 352), ("ffn1_d", 352)),
    "mid": (("w_in", 128), ("w_out", 128), ("w_mq", 128), ("w_mo", 128), ("w_mkv", 256), ("w_q", 32), ("w_kv", 16)),
    "ffn2": (("ffn2_g", 352), ("ffn2_u", 352), ("ffn2_d", 352)),
}
SEG_OFF = {}
GROUP_ROWS = {}
for _g, _segs in GROUP_SEGS.items():
    _o = 0
    for _n, _r in _segs:
        SEG_OFF[_n] = (_o, _r)
        _o += _r
    GROUP_ROWS[_g] = _o

SMALL_ROWS = (("ffn1_norm", 8), ("mix_norm", 8), ("q_norm", 8), ("kv_norm", 8), ("pool_w", 512), ("pool_scale", 8),
              ("xattn_norm", 8), ("mem_norm", 8), ("ffn2_norm", 8), ("final_norm", 8), ("loss", 8))
SMALL_OFF = {}
_o = 0
for _n, _r in SMALL_ROWS:
    SMALL_OFF[_n] = (_o, _r)
    _o += _r


def _cparams(**kw):
    return pltpu.CompilerParams(vmem_limit_bytes=VMEM_LIMIT_BYTES, **kw)


def _row_tile(rows, limit):
    best = None
    for cand in range(BF16_ROWS, min(rows, limit) + 1, BF16_ROWS):
        if rows % cand == 0:
            best = cand
    assert best is not None, rows
    return best


def _dot_nn(a, b):
    return lax.dot_general(a, b, (((1,), (0,)), ((), ())), preferred_element_type=F32)


def _dot_nt(a, b):
    return lax.dot_general(a, b, (((1,), (1,)), ((), ())), preferred_element_type=F32)


def _dot_tn(a, b):
    return lax.dot_general(a, b, (((0,), (0,)), ((), ())), preferred_element_type=F32)


def _rms_fwd(x, g):
    r = lax.rsqrt(jnp.mean(x * x, axis=-1, keepdims=True) + RMS_EPS)
    return x * r * g, r


def _rms_bwd(dy, x, g, r):
    xhat = x * r
    dyg = dy * g
    dx = r * (dyg - xhat * jnp.mean(dyg * xhat, axis=-1, keepdims=True))
    dg = jnp.sum(dy * xhat, axis=0, keepdims=True)
    return dx, dg


def _accumulate(ref, val, first):
    if isinstance(first, bool):
        if first:
            ref[...] = val
        else:
            ref[...] += val
        return

    @pl.when(first)
    def _():
        ref[...] = val

    @pl.when(jnp.logical_not(first))
    def _():
        ref[...] += val


def _call_after(token, body, in_specs, args, **kw):
    if token is not None:
        inner = body
        body = lambda tok_ref, *refs: inner(*refs)
        in_specs = [pl.BlockSpec((8, 128), lambda *_: (0, 0))] + list(in_specs)
        args = (token,) + tuple(args)
    return pl.pallas_call(body, in_specs=in_specs, **kw)(*args)


def _rope_tables(pos_col, tab):
    ang = pos_col.astype(F32) * tab[0:1, :]
    return jnp.cos(ang), jnp.sin(ang) * tab[1:2, :]


def _swap_halves(x):
    lane = lax.broadcasted_iota(jnp.int32, x.shape, 1)
    return jnp.where((lane % 64) < 32, pltpu.roll(x, 96, 1), pltpu.roll(x, 32, 1))


def _rope_apply(x, cos_t, sin_t):
    return x * cos_t + _swap_halves(x) * sin_t


def _rope_apply_t(dy, cos_t, sin_t):
    return dy * cos_t + _swap_halves(dy * sin_t)


def _ffn_fwd(h, g, wg_t, wu_t, wd, name, token=None):
    t, d = h.shape
    f = wg_t.shape[0]
    tm, tf = min(1024, t), 256
    nf = f // tf

    def body(h_ref, g_ref, wg_ref, wu_ref, wd_ref, ho_ref, n_ref, gate_ref, up_ref, nb_sc, acc_sc):
        j = pl.program_id(1)

        @pl.when(j == 0)
        def _():
            y, _ = _rms_fwd(h_ref[...], g_ref[...])
            nb = y.astype(BF16)
            nb_sc[...] = nb
            n_ref[...] = nb
            acc_sc[...] = jnp.zeros_like(acc_sc)

        nb = nb_sc[...]
        gt = _dot_nt(nb, wg_ref[...])
        ut = _dot_nt(nb, wu_ref[...])
        gate_ref[...] = gt.astype(BF16)
        up_ref[...] = ut.astype(BF16)
        act = (gt * jax.nn.sigmoid(gt)) * ut
        acc_sc[...] += _dot_nn(act.astype(BF16), wd_ref[...])

        @pl.when(j == nf - 1)
        def _():
            ho_ref[...] = h_ref[...] + 0.5 * acc_sc[...]

    return _call_after(
        token, body,
        [pl.BlockSpec((tm, d), lambda i, j: (i, 0)),
         pl.BlockSpec((1, d), lambda i, j: (0, 0)),
         pl.BlockSpec((tf, d), lambda i, j: (j, 0)),
         pl.BlockSpec((tf, d), lambda i, j: (j, 0)),
         pl.BlockSpec((tf, d), lambda i, j: (j, 0))],
        (h, g, wg_t, wu_t, wd),
        name=name, grid=(t // tm, nf),
        out_specs=[pl.BlockSpec((tm, d), lambda i, j: (i, 0)),
                   pl.BlockSpec((tm, d), lambda i, j: (i, 0)),
                   pl.BlockSpec((tm, tf), lambda i, j: (i, j)),
                   pl.BlockSpec((tm, tf), lambda i, j: (i, j))],
        out_shape=[jax.ShapeDtypeStruct((t, d), F32), jax.ShapeDtypeStruct((t, d), BF16),
                   jax.ShapeDtypeStruct((t, f), BF16), jax.ShapeDtypeStruct((t, f), BF16)],
        scratch_shapes=[pltpu.VMEM((tm, d), BF16), pltpu.VMEM((tm, d), F32)],
        compiler_params=_cparams(),
    )


def _ffn_bwd_data(dho, h, g, gate, up, wg_t, wu_t, wd, name, token=None):
    t, d = h.shape
    f = wg_t.shape[0]
    tm, tf = min(1024, t), 256
    nf = f // tf

    def body(dho_ref, h_ref, g_ref, gate_ref, up_ref, wg_ref, wu_ref, wd_ref,
             dh_ref, dgate_ref, dup_ref, act_ref, dg_ref, dhb_sc, acc_sc):
        i, j = pl.program_id(0), pl.program_id(1)

        @pl.when(j == 0)
        def _():
            dhb_sc[...] = (0.5 * dho_ref[...]).astype(BF16)
            acc_sc[...] = jnp.zeros_like(acc_sc)

        dact = _dot_nt(dhb_sc[...], wd_ref[...])
        gt = gate_ref[...].astype(F32)
        ut = up_ref[...].astype(F32)
        sg = jax.nn.sigmoid(gt)
        silu = gt * sg
        dgb = (dact * ut * (sg * (1.0 + gt * (1.0 - sg)))).astype(BF16)
        dub = (dact * silu).astype(BF16)
        act_ref[...] = (silu * ut).astype(BF16)
        dgate_ref[...] = dgb
        dup_ref[...] = dub
        acc_sc[...] += _dot_nn(dgb, wg_ref[...]) + _dot_nn(dub, wu_ref[...])

        @pl.when(j == nf - 1)
        def _():
            x = h_ref[...]
            gg = g_ref[...]
            _, r = _rms_fwd(x, gg)
            dx, dg = _rms_bwd(acc_sc[...], x, gg, r)
            dh_ref[...] = dho_ref[...] + dx
            _accumulate(dg_ref, dg, i == 0)

    return _call_after(
        token, body,
        [pl.BlockSpec((tm, d), lambda i, j: (i, 0)),
         pl.BlockSpec((tm, d), lambda i, j: (i, 0)),
         pl.BlockSpec((1, d), lambda i, j: (0, 0)),
         pl.BlockSpec((tm, tf), lambda i, j: (i, j)),
         pl.BlockSpec((tm, tf), lambda i, j: (i, j)),
         pl.BlockSpec((tf, d), lambda i, j: (j, 0)),
         pl.BlockSpec((tf, d), lambda i, j: (j, 0)),
         pl.BlockSpec((tf, d), lambda i, j: (j, 0))],
        (dho, h, g, gate, up, wg_t, wu_t, wd),
        name=name, grid=(t // tm, nf),
        out_specs=[pl.BlockSpec((tm, d), lambda i, j: (i, 0)),
                   pl.BlockSpec((tm, tf), lambda i, j: (i, j)),
                   pl.BlockSpec((tm, tf), lambda i, j: (i, j)),
                   pl.BlockSpec((tm, tf), lambda i, j: (i, j)),
                   pl.BlockSpec((1, d), lambda i, j: (0, 0))],
        out_shape=[jax.ShapeDtypeStruct((t, d), F32), jax.ShapeDtypeStruct((t, f), BF16),
                   jax.ShapeDtypeStruct((t, f), BF16), jax.ShapeDtypeStruct((t, f), BF16),
                   jax.ShapeDtypeStruct((1, d), F32)],
        scratch_shapes=[pltpu.VMEM((tm, d), BF16), pltpu.VMEM((tm, d), F32)],
        compiler_params=_cparams(),
    )


def _tn_matmul(a, b, name, scale=1.0, tmm=None, out_dtype=F32, token=None):
    t, m = a.shape
    n = b.shape[1]
    tmm = m if tmm is None else tmm
    tk = min(1024, t)
    nk = t // tk

    def product(a_ref, b_ref):
        prod = _dot_tn(a_ref[...].astype(BF16), b_ref[...].astype(BF16))
        return prod * scale if scale != 1.0 else prod

    def body_f32(a_ref, b_ref, o_ref):
        _accumulate(o_ref, product(a_ref, b_ref), pl.program_id(1) == 0)

    def body_cast(a_ref, b_ref, o_ref, acc_sc):
        k = pl.program_id(1)
        _accumulate(acc_sc, product(a_ref, b_ref), k == 0)

        @pl.when(k == nk - 1)
        def _():
            o_ref[...] = acc_sc[...].astype(out_dtype)

    direct = out_dtype == F32
    return _call_after(
        token, body_f32 if direct else body_cast,
        [pl.BlockSpec((tk, tmm), lambda i, k: (k, i)),
         pl.BlockSpec((tk, n), lambda i, k: (k, 0))],
        (a, b),
        name=name, grid=(m // tmm, nk),
        out_specs=pl.BlockSpec((tmm, n), lambda i, k: (i, 0)),
        out_shape=jax.ShapeDtypeStruct((m, n), out_dtype),
        scratch_shapes=[] if direct else [pltpu.VMEM((tmm, n), F32)],
        compiler_params=_cparams(),
    )


def _loss_head(h, target, g):
    t, d = h.shape
    tm = min(512, t)

    def body(h_ref, t_ref, g_ref, loss_ref, dh_ref, dg_ref):
        i = pl.program_id(0)
        x = h_ref[...]
        gg = g_ref[...]
        y, r = _rms_fwd(x, gg)
        err = y - t_ref[...]
        part = 0.5 * jnp.sum(jnp.mean(err * err, axis=-1, keepdims=True), axis=0, keepdims=True)
        dx, dg = _rms_bwd(err * (1.0 / d), x, gg, r)
        dh_ref[...] = dx
        _accumulate(loss_ref, jnp.broadcast_to(part, loss_ref.shape), i == 0)
        _accumulate(dg_ref, dg, i == 0)

    return pl.pallas_call(
        body, name="loss_head", grid=(t // tm,),
        in_specs=[pl.BlockSpec((tm, d), lambda i: (i, 0)),
                  pl.BlockSpec((tm, d), lambda i: (i, 0)),
                  pl.BlockSpec((1, d), lambda i: (0, 0))],
        out_specs=[pl.BlockSpec((8, 128), lambda i: (0, 0)),
                   pl.BlockSpec((tm, d), lambda i: (i, 0)),
                   pl.BlockSpec((1, d), lambda i: (0, 0))],
        out_shape=[jax.ShapeDtypeStruct((8, 128), F32), jax.ShapeDtypeStruct((t, d), F32),
                   jax.ShapeDtypeStruct((1, d), F32)],
        compiler_params=_cparams(),
    )(h, target, g)


def _mix_prep(h1, mix_norm, w_in, q_norm, wq_t, kv_norm, wkv, pos, rope_tab, token=None):
    t, d = h1.shape
    tm = min(512, t)

    def body(h_ref, gm_ref, win_ref, gq_ref, wq_ref, gkv_ref, wkv_ref, pos_ref, tab_ref,
             u_ref, z_ref, qn_ref, kvn_ref, q_ref, k_ref, v_ref):
        u, _ = _rms_fwd(h_ref[...], gm_ref[...])
        ub = u.astype(BF16)
        u_ref[...] = ub
        z = _dot_nn(ub, win_ref[...])
        z_ref[...] = z
        cos_t, sin_t = _rope_tables(pos_ref[...], tab_ref[...])
        qn, _ = _rms_fwd(z[:, 0:Q_RANK], gq_ref[...])
        qnb = qn.astype(BF16)
        qn_ref[...] = qnb
        q = _dot_nt(qnb, wq_ref[...])
        kvn, _ = _rms_fwd(z[:, Q_RANK:Q_RANK + KV_RANK], gkv_ref[...])
        kvnb = kvn.astype(BF16)
        kvn_ref[...] = kvnb
        kv = _dot_nn(kvnb, wkv_ref[...])
        k_pe = _rope_apply(z[:, Q_RANK + KV_RANK:Q_RANK + KV_RANK + 128], cos_t, sin_t)
        ones = jnp.ones((tm, V_DIM), F32)
        for hh in range(MLA_HEADS):
            b = hh * HEAD_PAD
            q_pe = _rope_apply(q[:, b + NOPE:b + HEAD_PAD], cos_t, sin_t)
            q_ref[hh] = jnp.concatenate([q[:, b:b + NOPE], q_pe], axis=-1).astype(BF16)
            k_ref[hh] = jnp.concatenate([kv[:, b:b + NOPE], k_pe], axis=-1).astype(BF16)
            v_ref[hh] = jnp.concatenate([kv[:, b + NOPE:b + HEAD_PAD], ones], axis=-1).astype(BF16)

    full = lambda shape: pl.BlockSpec(shape, lambda i: (0,) * len(shape))
    return _call_after(
        token, body,
        [pl.BlockSpec((tm, d), lambda i: (i, 0)), full((1, d)), full(w_in.shape), full((1, Q_RANK)),
         full(wq_t.shape), full((1, KV_RANK)), full(wkv.shape),
         pl.BlockSpec((tm, 1), lambda i: (i, 0)), full(rope_tab.shape)],
        (h1, mix_norm, w_in, q_norm, wq_t, kv_norm, wkv, pos, rope_tab),
        name="mix_prep", grid=(t // tm,),
        out_specs=[pl.BlockSpec((tm, d), lambda i: (i, 0)),
                   pl.BlockSpec((tm, d), lambda i: (i, 0)),
                   pl.BlockSpec((tm, Q_RANK), lambda i: (i, 0)),
                   pl.BlockSpec((tm, KV_RANK), lambda i: (i, 0)),
                   pl.BlockSpec((MLA_HEADS, tm, HEAD_PAD), lambda i: (0, i, 0)),
                   pl.BlockSpec((MLA_HEADS, tm, HEAD_PAD), lambda i: (0, i, 0)),
                   pl.BlockSpec((MLA_HEADS, tm, 2 * V_DIM), lambda i: (0, i, 0))],
        out_shape=[jax.ShapeDtypeStruct((t, d), BF16), jax.ShapeDtypeStruct((t, d), F32),
                   jax.ShapeDtypeStruct((t, Q_RANK), BF16), jax.ShapeDtypeStruct((t, KV_RANK), BF16),
                   jax.ShapeDtypeStruct((MLA_HEADS, t, HEAD_PAD), BF16),
                   jax.ShapeDtypeStruct((MLA_HEADS, t, HEAD_PAD), BF16),
                   jax.ShapeDtypeStruct((MLA_HEADS, t, 2 * V_DIM), BF16)],
        compiler_params=_cparams(),
    )


def _causal_mask(s):
    row = lax.broadcasted_iota(jnp.int32, s.shape, 0)
    col = lax.broadcasted_iota(jnp.int32, s.shape, 1)
    return jnp.where(col <= row, s, NEG_BIG)


def _attn_fwd(q, k, v):
    nh, t, _ = q.shape
    tq = tk = min(512, t)
    nq, nk = t // tq, t // tk

    def body(q_ref, k_ref, v_ref, o_ref, lse_ref, m_sc, acc_sc):
        i, j = pl.program_id(0), pl.program_id(1)

        @pl.when(j == 0)
        def _():
            m_sc[...] = jnp.full_like(m_sc, NEG_BIG)
            acc_sc[...] = jnp.zeros_like(acc_sc)

        def step(diagonal):
            for hh in range(nh):
                s = _dot_nt(q_ref[hh], k_ref[hh]) * ATTN_SCALE
                if diagonal:
                    s = _causal_mask(s)
                m_old = m_sc[hh]
                m_new = jnp.maximum(m_old, jnp.max(s, axis=-1, keepdims=True))
                p = jnp.exp(s - m_new).astype(BF16)
                acc_sc[hh] = jnp.exp(m_old - m_new) * acc_sc[hh] + _dot_nn(p, v_ref[hh])
                m_sc[hh] = m_new

        @pl.when(j < i)
        def _():
            step(False)

        @pl.when(j == i)
        def _():
            step(True)
            for hh in range(nh):
                acc = acc_sc[hh]
                l = acc[:, V_DIM:2 * V_DIM]
                o_ref[:, hh * V_DIM:(hh + 1) * V_DIM] = (acc[:, 0:V_DIM] / l).astype(BF16)
                lse_ref[hh] = m_sc[hh] + jnp.log(l[:, 0:1])

    kv_map = lambda i, j: (0, jnp.minimum(j, i), 0)
    return pl.pallas_call(
        body, name="attn_fwd", grid=(nq, nk),
        in_specs=[pl.BlockSpec((nh, tq, HEAD_PAD), lambda i, j: (0, i, 0)),
                  pl.BlockSpec((nh, tk, HEAD_PAD), kv_map),
                  pl.BlockSpec((nh, tk, 2 * V_DIM), kv_map)],
        out_specs=[pl.BlockSpec((tq, nh * V_DIM), lambda i, j: (i, 0)),
                   pl.BlockSpec((nh, tq, 1), lambda i, j: (0, i, 0))],
        out_shape=[jax.ShapeDtypeStruct((t, nh * V_DIM), BF16), jax.ShapeDtypeStruct((nh, t, 1), F32)],
        scratch_shapes=[pltpu.VMEM((nh, tq, 1), F32), pltpu.VMEM((nh, tq, 2 * V_DIM), F32)],
        compiler_params=_cparams(),
    )(q, k, v)


def _attn_delta(o, do):
    t, w = o.shape
    nh = w // V_DIM
    tm = min(512, t)

    def body(o_ref, do_ref, d_ref):
        prod = o_ref[...].astype(F32) * do_ref[...].astype(F32)
        for hh in range(nh):
            d_ref[hh] = jnp.sum(prod[:, hh * V_DIM:(hh + 1) * V_DIM], axis=-1, keepdims=True)

    return pl.pallas_call(
        body, name="attn_delta", grid=(t // tm,),
        in_specs=[pl.BlockSpec((tm, w), lambda i: (i, 0)), pl.BlockSpec((tm, w), lambda i: (i, 0))],
        out_specs=pl.BlockSpec((nh, tm, 1), lambda i: (0, i, 0)),
        out_shape=jax.ShapeDtypeStruct((nh, t, 1), F32),
        compiler_params=_cparams(),
    )(o, do)


ATTN_BWD_HEADS = 2


def _attn_bwd(q, k, v, do, lse, delta):
    nh, t, _ = q.shape
    hp = ATTN_BWD_HEADS
    tq = tk = min(512, t)
    nq, nk = t // tq, t // tk

    def body(q_ref, k_ref, v_ref, do_ref, lse_ref, dlt_ref, dq_ref, dk_ref, dv_ref):
        j, i = pl.program_id(1), pl.program_id(2)

        @pl.when(jnp.logical_and(j == 0, i == 0))
        def _():
            dq_ref[...] = jnp.zeros_like(dq_ref)

        def step(diagonal):
            for hh in range(hp):
                qq, kk = q_ref[hh], k_ref[hh]
                dob = do_ref[:, hh * V_DIM:(hh + 1) * V_DIM]
                s = _dot_nt(qq, kk) * ATTN_SCALE
                if diagonal:
                    s = _causal_mask(s)
                p = jnp.exp(s - lse_ref[hh])
                dpp = _dot_nt(dob, v_ref[hh])
                dsb = (p * (dpp - dlt_ref[hh]) * ATTN_SCALE).astype(BF16)
                _accumulate(dv_ref.at[hh], _dot_tn(p.astype(BF16), dob), diagonal)
                _accumulate(dk_ref.at[hh], _dot_tn(dsb, qq), diagonal)
                dq_ref[hh, pl.ds(pl.multiple_of(i * tq, tq), tq), :] += _dot_nn(dsb, kk)

        @pl.when(i > j)
        def _():
            step(False)

        @pl.when(i == j)
        def _():
            step(True)

    qmap = lambda h, j, i: (h, jnp.maximum(i, j), 0)
    return pl.pallas_call(
        body, name="attn_bwd", grid=(nh // hp, nk, nq),
        in_specs=[pl.BlockSpec((hp, tq, HEAD_PAD), qmap),
                  pl.BlockSpec((hp, tk, HEAD_PAD), lambda h, j, i: (h, j, 0)),
                  pl.BlockSpec((hp, tk, V_DIM), lambda h, j, i: (h, j, 0)),
                  pl.BlockSpec((tq, hp * V_DIM), lambda h, j, i: (jnp.maximum(i, j), h)),
                  pl.BlockSpec((hp, tq, 1), qmap),
                  pl.BlockSpec((hp, tq, 1), qmap)],
        out_specs=[pl.BlockSpec((hp, t, HEAD_PAD), lambda h, j, i: (h, 0, 0)),
                   pl.BlockSpec((hp, tk, HEAD_PAD), lambda h, j, i: (h, j, 0)),
                   pl.BlockSpec((hp, tk, V_DIM), lambda h, j, i: (h, j, 0))],
        out_shape=[jax.ShapeDtypeStruct((nh, t, HEAD_PAD), F32), jax.ShapeDtypeStruct((nh, t, HEAD_PAD), F32),
                   jax.ShapeDtypeStruct((nh, t, V_DIM), F32)],
        compiler_params=_cparams(),
    )(q, k, v, do, lse, delta)


def _pool_counts(first_token, rows, w):
    tok = lax.broadcasted_iota(jnp.int32, (rows, POOL_CH), 0) + first_token
    return jnp.minimum(tok + 1, w).astype(F32)


def _pool_centered(zbuf, g, w, i, tm):
    lanes = pl.ds(g * POOL_CH, POOL_CH)
    cur = zbuf[pl.ds(POOL_HALO, tm), lanes]
    win = cur
    for s in range(1, w):
        win = win + zbuf[pl.ds(POOL_HALO - s, tm), lanes]
    return win / _pool_counts(i * tm, tm, w) - cur


def _pool_load(zbuf, z_ref, halo_ref, i, tm):
    @pl.when(i == 0)
    def _():
        zbuf[pl.ds(0, POOL_HALO), :] = jnp.zeros((POOL_HALO, zbuf.shape[1]), F32)

    @pl.when(i > 0)
    def _():
        zbuf[pl.ds(0, POOL_HALO), :] = halo_ref[...]

    zbuf[pl.ds(POOL_HALO, tm), :] = z_ref[...]


def _pool_fwd(z, pool_w, pool_scale):
    t = z.shape[0]
    pw = len(POOL_WINDOWS) * POOL_CH
    tm = min(512, t)
    hb = tm // POOL_HALO

    def body(z_ref, halo_ref, w_ref, sc_ref, p_ref, zbuf):
        i = pl.program_id(0)
        _pool_load(zbuf, z_ref, halo_ref, i, tm)
        for g, w in enumerate(POOL_WINDOWS):
            c = _pool_centered(zbuf, g, w, i, tm)
            y = _dot_nn(c.astype(BF16), w_ref[g]) * sc_ref[:, g * POOL_CH:(g + 1) * POOL_CH]
            p_ref[:, g * POOL_CH:(g + 1) * POOL_CH] = y.astype(BF16)

    return pl.pallas_call(
        body, name="pool_fwd", grid=(t // tm,),
        in_specs=[pl.BlockSpec((tm, pw), lambda i: (i, 1)),
                  pl.BlockSpec((POOL_HALO, pw), lambda i: (jnp.maximum(i * hb - 1, 0), 1)),
                  pl.BlockSpec(pool_w.shape, lambda i: (0, 0, 0)),
                  pl.BlockSpec((1, pw), lambda i: (0, 0))],
        out_specs=pl.BlockSpec((tm, pw), lambda i: (i, 0)),
        out_shape=jax.ShapeDtypeStruct((t, pw), BF16),
        scratch_shapes=[pltpu.VMEM((POOL_HALO + tm, pw), F32)],
        compiler_params=_cparams(),
    )(z, z, pool_w, pool_scale)


def _pool_bwd(dp, z, pool_w, pool_scale):
    t = z.shape[0]
    ng = len(POOL_WINDOWS)
    pw = ng * POOL_CH
    tm = min(512, t)
    hb = tm // POOL_HALO
    nt = t // tm

    def body(dp_ref, dpn_ref, z_ref, halo_ref, w_ref, sc_ref, dz_ref, dw_ref, dsc_ref, zbuf, dbuf):
        i = pl.program_id(0)
        _pool_load(zbuf, z_ref, halo_ref, i, tm)

        @pl.when(i == 0)
        def _():
            dw_ref[...] = jnp.zeros_like(dw_ref)
            dsc_ref[...] = jnp.zeros_like(dsc_ref)

        nxt_ok = (i < nt - 1).astype(F32)
        for g, w in enumerate(POOL_WINDOWS):
            lanes = pl.ds(g * POOL_CH, POOL_CH)
            cols = slice(g * POOL_CH, (g + 1) * POOL_CH)
            sc = sc_ref[:, cols]
            wg = w_ref[g]
            c = _pool_centered(zbuf, g, w, i, tm).astype(BF16)
            ypre = _dot_nn(c, wg)
            dpg = dp_ref[:, cols].astype(F32)
            dsc_ref[:, cols] += jnp.sum(dpg * ypre, axis=0, keepdims=True)
            dyb = (dpg * sc).astype(BF16)
            dw_ref[g] += _dot_tn(c, dyb)
            dd = _dot_nt(dyb, wg)
            dyn = (dpn_ref[:, cols].astype(F32) * sc).astype(BF16)
            ddn = _dot_nt(dyn, wg) * nxt_ok
            dbuf[pl.ds(0, tm), lanes] = dd / _pool_counts(i * tm, tm, w)
            dbuf[pl.ds(tm, POOL_HALO), lanes] = ddn / _pool_counts((i + 1) * tm, POOL_HALO, w)
            acc = -dd
            for s in range(w):
                acc = acc + dbuf[pl.ds(s, tm), lanes]
            dz_ref[:, cols] = acc

    return pl.pallas_call(
        body, name="pool_bwd", grid=(nt,),
        in_specs=[pl.BlockSpec((tm, pw), lambda i: (i, 0)),
                  pl.BlockSpec((POOL_HALO, pw), lambda i: (jnp.minimum((i + 1) * hb, t // POOL_HALO - 1), 0)),
                  pl.BlockSpec((tm, pw), lambda i: (i, 1)),
                  pl.BlockSpec((POOL_HALO, pw), lambda i: (jnp.maximum(i * hb - 1, 0), 1)),
                  pl.BlockSpec(pool_w.shape, lambda i: (0, 0, 0)),
                  pl.BlockSpec((1, pw), lambda i: (0, 0))],
        out_specs=[pl.BlockSpec((tm, pw), lambda i: (i, 0)),
                   pl.BlockSpec((ng, POOL_CH, POOL_CH), lambda i: (0, 0, 0)),
                   pl.BlockSpec((1, pw), lambda i: (0, 0))],
        out_shape=[jax.ShapeDtypeStruct((t, pw), F32), jax.ShapeDtypeStruct((ng, POOL_CH, POOL_CH), F32),
                   jax.ShapeDtypeStruct((1, pw), F32)],
        scratch_shapes=[pltpu.VMEM((POOL_HALO + tm, pw), F32), pltpu.VMEM((tm + POOL_HALO, pw), F32)],
        compiler_params=_cparams(),
    )(dp, dp, z, z, pool_w, pool_scale)


def _mla_bwd(dq_h, dk_h, dv_h, z, dz_pool, h1, dh2, mix_norm, w_in, q_norm, wq_t, kv_norm, wkv, pos, rope_tab):
    t, d = h1.shape
    tm = min(256, t)

    def body(dqh_ref, dkh_ref, dvh_ref, z_ref, dzp_ref, h_ref, dh2_ref, gm_ref, win_ref, gq_ref, wq_ref, gkv_ref,
             wkv_ref, pos_ref, tab_ref, dh1_ref, dq_ref, dkv_ref, dz_ref, dgq_ref, dgkv_ref, dgm_ref):
        i = pl.program_id(0)
        first = i == 0
        cos_t, sin_t = _rope_tables(pos_ref[...], tab_ref[...])
        dq_parts, dkv_parts = [], []
        dk_pe = jnp.zeros((tm, 128), F32)
        for hh in range(MLA_HEADS):
            dqh = dqh_ref[hh]
            dq_parts += [dqh[:, 0:NOPE], _rope_apply_t(dqh[:, NOPE:HEAD_PAD], cos_t, sin_t)]
            dkh = dkh_ref[hh]
            dkv_parts += [dkh[:, 0:NOPE], dvh_ref[hh]]
            dk_pe = dk_pe + dkh[:, NOPE:HEAD_PAD]
        dqb = jnp.concatenate(dq_parts, axis=-1).astype(BF16)
        dkvb = jnp.concatenate(dkv_parts, axis=-1).astype(BF16)
        dq_ref[...] = dqb
        dkv_ref[...] = dkvb
        z = z_ref[...]
        c_q = z[:, 0:Q_RANK]
        gq = gq_ref[...]
        _, rq = _rms_fwd(c_q, gq)
        dcq, dgq = _rms_bwd(_dot_nn(dqb, wq_ref[...]), c_q, gq, rq)
        c_kv = z[:, Q_RANK:Q_RANK + KV_RANK]
        gkv = gkv_ref[...]
        _, rkv = _rms_fwd(c_kv, gkv)
        dckv, dgkv = _rms_bwd(_dot_nt(dkvb, wkv_ref[...]), c_kv, gkv, rkv)
        dkr = _rope_apply_t(dk_pe, cos_t, sin_t)
        dzb = jnp.concatenate([dcq, dckv, dkr, dzp_ref[...]], axis=-1).astype(BF16)
        dz_ref[...] = dzb
        x = h_ref[...]
        gm = gm_ref[...]
        _, rm = _rms_fwd(x, gm)
        dx, dgm = _rms_bwd(_dot_nt(dzb, win_ref[...]), x, gm, rm)
        dh1_ref[...] = dh2_ref[...] + dx
        _accumulate(dgq_ref, dgq, first)
        _accumulate(dgkv_ref, dgkv, first)
        _accumulate(dgm_ref, dgm, first)

    full = lambda shape: pl.BlockSpec(shape, lambda i: (0,) * len(shape))
    row = lambda w: pl.BlockSpec((tm, w), lambda i: (i, 0))
    head = lambda w: pl.BlockSpec((MLA_HEADS, tm, w), lambda i: (0, i, 0))
    pw = len(POOL_WINDOWS) * POOL_CH
    return pl.pallas_call(
        body, name="mla_bwd", grid=(t // tm,),
        in_specs=[head(HEAD_PAD), head(HEAD_PAD), head(V_DIM), row(d), row(pw), row(d), row(d),
                  full((1, d)), full(w_in.shape), full((1, Q_RANK)), full(wq_t.shape), full((1, KV_RANK)),
                  full(wkv.shape), row(1), full(rope_tab.shape)],
        out_specs=[row(d), row(d), row(d), row(d), full((1, Q_RANK)), full((1, KV_RANK)), full((1, d))],
        out_shape=[jax.ShapeDtypeStruct((t, d), F32), jax.ShapeDtypeStruct((t, d), BF16),
                   jax.ShapeDtypeStruct((t, d), BF16), jax.ShapeDtypeStruct((t, d), BF16),
                   jax.ShapeDtypeStruct((1, Q_RANK), F32), jax.ShapeDtypeStruct((1, KV_RANK), F32),
                   jax.ShapeDtypeStruct((1, d), F32)],
        compiler_params=_cparams(),
    )(dq_h, dk_h, dv_h, z, dz_pool, h1, dh2, mix_norm, w_in, q_norm, wq_t, kv_norm, wkv, pos, rope_tab)


def _mem_kv(mem, mem_norm, wmkv):
    n, d = mem.shape

    def body(mem_ref, g_ref, w_ref, memn_ref, k_ref, v_ref):
        y, _ = _rms_fwd(mem_ref[...], g_ref[...])
        yb = y.astype(BF16)
        memn_ref[...] = yb
        for hh in range(MEM_HEADS):
            k_ref[hh] = _dot_nn(yb, w_ref[hh]).astype(BF16)
            v_ref[hh] = _dot_nn(yb, w_ref[MEM_HEADS + hh]).astype(BF16)

    return pl.pallas_call(
        body, name="mem_kv",
        out_shape=[jax.ShapeDtypeStruct((n, d), BF16), jax.ShapeDtypeStruct((MEM_HEADS, n, MEM_HD), BF16),
                   jax.ShapeDtypeStruct((MEM_HEADS, n, MEM_HD), BF16)],
        compiler_params=_cparams(),
    )(mem, mem_norm, wmkv)


def _mem_softmax(qb, km):
    s = _dot_nt(qb, km) * MEM_SCALE
    e = jnp.exp(s - jnp.max(s, axis=-1, keepdims=True))
    return e / jnp.sum(e, axis=-1, keepdims=True)


def _xattn_fwd(h1, a, p, w_out, g, wmq, km, vm, wmo):
    t, d = h1.shape
    tm = min(512, t)
    half = a.shape[1]

    def body(h_ref, a_ref, p_ref, wo_ref, g_ref, wmq_ref, km_ref, vm_ref, wmo_ref,
             h2_ref, h3_ref, hn_ref, q_ref, o_ref):
        h2 = h_ref[...] + _dot_nn(a_ref[...], wo_ref[0:half, :]) + _dot_nn(p_ref[...], wo_ref[half:2 * half, :])
        h2_ref[...] = h2
        hn, _ = _rms_fwd(h2, g_ref[...])
        hnb = hn.astype(BF16)
        hn_ref[...] = hnb
        qb = _dot_nn(hnb, wmq_ref[...]).astype(BF16)
        q_ref[...] = qb
        outs = []
        for hh in range(MEM_HEADS):
            pr = _mem_softmax(qb[:, hh * MEM_HD:(hh + 1) * MEM_HD], km_ref[hh])
            outs.append(_dot_nn(pr.astype(BF16), vm_ref[hh]))
        ob = jnp.concatenate(outs, axis=-1).astype(BF16)
        o_ref[...] = ob
        h3_ref[...] = h2 + _dot_nn(ob, wmo_ref[...])

    full = lambda shape: pl.BlockSpec(shape, lambda i: (0,) * len(shape))
    row = lambda w: pl.BlockSpec((tm, w), lambda i: (i, 0))
    return pl.pallas_call(
        body, name="xattn_fwd", grid=(t // tm,),
        in_specs=[row(d), row(half), row(half), full(w_out.shape), full((1, d)), full(wmq.shape),
                  full(km.shape), full(vm.shape), full(wmo.shape)],
        out_specs=[row(d), row(d), row(d), row(d), row(d)],
        out_shape=[jax.ShapeDtypeStruct((t, d), F32), jax.ShapeDtypeStruct((t, d), F32),
                   jax.ShapeDtypeStruct((t, d), BF16), jax.ShapeDtypeStruct((t, d), BF16),
                   jax.ShapeDtypeStruct((t, d), BF16)],
        compiler_params=_cparams(),
    )(h1, a, p, w_out, g, wmq, km, vm, wmo)


def _xattn_bwd(dh3, h2, qm, g, wmq, km, vm, wmo, w_out, token=None):
    t, d = h2.shape
    tm = min(256, t)
    half = d // 2

    def body(dh3_ref, h2_ref, q_ref, g_ref, wmq_ref, km_ref, vm_ref, wmo_ref, wo_ref,
             dh2_ref, dq_ref, da_ref, dp_ref, dk_ref, dv_ref, dg_ref):
        i = pl.program_id(0)
        first = i == 0

        @pl.when(first)
        def _():
            dk_ref[...] = jnp.zeros_like(dk_ref)
            dv_ref[...] = jnp.zeros_like(dv_ref)

        dh3 = dh3_ref[...]
        dob = _dot_nt(dh3.astype(BF16), wmo_ref[...]).astype(BF16)
        qb = q_ref[...]
        dq_parts = []
        for hh in range(MEM_HEADS):
            cols = slice(hh * MEM_HD, (hh + 1) * MEM_HD)
            kk, vv = km_ref[hh], vm_ref[hh]
            pr = _mem_softmax(qb[:, cols], kk)
            doh = dob[:, cols]
            dv_ref[hh] += _dot_tn(pr.astype(BF16), doh)
            dpp = _dot_nt(doh, vv)
            dsb = (pr * (dpp - jnp.sum(dpp * pr, axis=-1, keepdims=True)) * MEM_SCALE).astype(BF16)
            dq_parts.append(_dot_nn(dsb, kk))
            dk_ref[hh] += _dot_tn(dsb, qb[:, cols])
        dqb = jnp.concatenate(dq_parts, axis=-1).astype(BF16)
        dq_ref[...] = dqb
        x = h2_ref[...]
        gg = g_ref[...]
        _, r = _rms_fwd(x, gg)
        dx, dg = _rms_bwd(_dot_nt(dqb, wmq_ref[...]), x, gg, r)
        dh2 = dh3 + dx
        dh2_ref[...] = dh2
        dap = _dot_nt(dh2.astype(BF16), wo_ref[...])
        da_ref[...] = dap[:, 0:half].astype(BF16)
        dp_ref[...] = dap[:, half:d].astype(BF16)
        _accumulate(dg_ref, dg, first)

    full = lambda shape: pl.BlockSpec(shape, lambda i: (0,) * len(shape))
    row = lambda w: pl.BlockSpec((tm, w), lambda i: (i, 0))
    return _call_after(
        token, body,
        [row(d), row(d), row(d), full((1, d)), full(wmq.shape), full(km.shape), full(vm.shape),
         full(wmo.shape), full(w_out.shape)],
        (dh3, h2, qm, g, wmq, km, vm, wmo, w_out),
        name="xattn_bwd", grid=(t // tm,),
        out_specs=[row(d), row(d), row(half), row(half), full(km.shape), full(vm.shape), full((1, d))],
        out_shape=[jax.ShapeDtypeStruct((t, d), F32), jax.ShapeDtypeStruct((t, d), BF16),
                   jax.ShapeDtypeStruct((t, half), BF16), jax.ShapeDtypeStruct((t, half), BF16),
                   jax.ShapeDtypeStruct(km.shape, F32), jax.ShapeDtypeStruct(vm.shape, F32),
                   jax.ShapeDtypeStruct((1, d), F32)],
        compiler_params=_cparams(),
    )


def _mem_kv_bwd(dkm, dvm, memn, mem, mem_norm, wmkv):
    n, d = mem.shape

    def body(dk_ref, dv_ref, memn_ref, mem_ref, g_ref, w_ref, dw_ref, dg_ref):
        memn = memn_ref[...]
        dmemn = jnp.zeros((n, d), F32)
        for s in range(2 * MEM_HEADS):
            src = dk_ref[s] if s < MEM_HEADS else dv_ref[s - MEM_HEADS]
            db = src.astype(BF16)
            dw_ref[s] = _dot_tn(memn, db)
            dmemn = dmemn + _dot_nt(db, w_ref[s])
        x = mem_ref[...]
        gg = g_ref[...]
        _, r = _rms_fwd(x, gg)
        _, dg = _rms_bwd(dmemn, x, gg, r)
        dg_ref[...] = dg

    return pl.pallas_call(
        body, name="mem_kv_bwd",
        out_shape=[jax.ShapeDtypeStruct(wmkv.shape, F32), jax.ShapeDtypeStruct((1, d), F32)],
        compiler_params=_cparams(),
    )(dkm, dvm, memn, mem, mem_norm, wmkv)


MESH_ID = pl.DeviceIdType.MESH
ANY = pl.BlockSpec(memory_space=pl.ANY)


def _coords():
    return lax.axis_index("x"), lax.axis_index("y"), lax.axis_index("c")


def _other_chips(x, y):
    return [(1 - x, y), (x, 1 - y), (1 - x, 1 - y)]


def _all_gather_direct(shard, token):
    r, w = shard.shape

    def body(tok_ref, x_ref, out_ref, send_sems, recv_sems, local_sem):
        x, y, c = _coords()
        mine = pltpu.make_async_copy(x_ref, out_ref.at[4 * x + 2 * y + c], local_sem)
        mine.start()
        arrivals = []
        for k in range(1, N_DEV):
            px, py, pc = x ^ ((k >> 2) & 1), y ^ ((k >> 1) & 1), c ^ (k & 1)
            pltpu.make_async_remote_copy(
                src_ref=x_ref, dst_ref=out_ref.at[4 * x + 2 * y + c],
                send_sem=send_sems.at[k - 1], recv_sem=recv_sems.at[k - 1],
                device_id=(px, py, pc), device_id_type=MESH_ID).start()
            arrivals.append(pltpu.make_async_remote_copy(
                src_ref=x_ref, dst_ref=out_ref.at[4 * px + 2 * py + pc],
                send_sem=send_sems.at[k - 1], recv_sem=recv_sems.at[k - 1],
                device_id=(px, py, pc), device_id_type=MESH_ID))
        for cp in arrivals:
            cp.wait_recv()
        for cp in arrivals:
            cp.wait_send()
        mine.wait()

    return pl.pallas_call(
        body, name="all_gather_direct",
        out_shape=jax.ShapeDtypeStruct((N_DEV, r, w), shard.dtype),
        in_specs=[ANY, pl.BlockSpec(memory_space=pltpu.VMEM)], out_specs=pl.BlockSpec(memory_space=pltpu.VMEM),
        scratch_shapes=[pltpu.SemaphoreType.DMA((7,)), pltpu.SemaphoreType.DMA((7,)), pltpu.SemaphoreType.DMA],
        compiler_params=_cparams(),
    )(token, shard)


def _exchange_core(g, tag):
    _, r, w = g.shape

    def body(g_ref, land_ref, send_sems, recv_sems):
        x, y, c = _coords()
        copies = []
        for chip in range(4):
            copies.append(pltpu.make_async_remote_copy(
                src_ref=g_ref.at[2 * chip + (1 - c)], dst_ref=land_ref.at[chip],
                send_sem=send_sems.at[chip], recv_sem=recv_sems.at[chip],
                device_id=(x, y, 1 - c), device_id_type=MESH_ID))
        for cp in copies:
            cp.start()
        for cp in copies:
            cp.wait_recv()
        for cp in copies:
            cp.wait_send()

    return pl.pallas_call(
        body, name="exchange_core_" + tag,
        out_shape=jax.ShapeDtypeStruct((4, r, w), g.dtype),
        in_specs=[ANY], out_specs=ANY,
        scratch_shapes=[pltpu.SemaphoreType.DMA((4,)), pltpu.SemaphoreType.DMA((4,))],
    )(g)


def _chip_partial(g, land, cidx, tag):
    _, r, w = g.shape
    tr = _row_tile(r, 1024)
    g4 = g.reshape(4, 2, r, w)

    def body(c_ref, g_ref, l_ref, o_ref):
        o_ref[0] = (g_ref[0, 0].astype(F32) + l_ref[0].astype(F32)).astype(o_ref.dtype)

    return pl.pallas_call(
        body, name="chip_partial_" + tag,
        grid_spec=pltpu.PrefetchScalarGridSpec(
            num_scalar_prefetch=1, grid=(4, r // tr),
            in_specs=[pl.BlockSpec((1, 1, tr, w), lambda i, j, s: (i, s[0], j, 0)),
                      pl.BlockSpec((1, tr, w), lambda i, j, s: (i, j, 0))],
            out_specs=pl.BlockSpec((1, tr, w), lambda i, j, s: (i, j, 0))),
        out_shape=jax.ShapeDtypeStruct((4, r, w), g.dtype),
        compiler_params=_cparams(),
    )(cidx, g4, land)


HBM_SPEC = pl.BlockSpec(memory_space=pltpu.HBM)
SEM_SPEC = pl.BlockSpec(memory_space=pltpu.SEMAPHORE)
SPLIT_EFFECT = pltpu.SideEffectType.DATAFLOW_SIDE_EFFECTING


def _ici_block(src_ref, px, py):
    return src_ref if len(src_ref.shape) == 2 else src_ref.at[2 * px + py]


def _ici_slot(gather, j, px, py, c):
    return 4 * px + 2 * py + c if gather else j


def _ici_start(src, after, name):
    r, w = src.shape[-2:]
    gather = len(src.shape) == 2
    land_shape = (N_DEV if gather else 3, r, w)

    def body(src_ref, land_ref, after_ref, send_sems, recv_sems, src_thru, land_thru, token):
        x, y, c = _coords()
        for j, (px, py) in enumerate(_other_chips(x, y)):
            pltpu.make_async_remote_copy(
                src_ref=_ici_block(src_ref, px, py), dst_ref=land_ref.at[_ici_slot(gather, j, x, y, c)],
                send_sem=send_sems.at[j], recv_sem=recv_sems.at[j],
                device_id=(px, py, c), device_id_type=MESH_ID).start()
        token[...] = jnp.zeros_like(token)

    return pl.pallas_call(
        body, name=name,
        out_shape=(pltpu.SemaphoreType.DMA((3,)), pltpu.SemaphoreType.DMA((3,)), pltpu.HBM(src.shape, src.dtype),
                   pltpu.HBM(land_shape, src.dtype), jax.ShapeDtypeStruct((8, 128), F32)),
        in_specs=(HBM_SPEC, HBM_SPEC, ANY),
        out_specs=(SEM_SPEC, SEM_SPEC, HBM_SPEC, HBM_SPEC, pl.BlockSpec(memory_space=pltpu.VMEM)),
        input_output_aliases={0: 2, 1: 3},
        compiler_params=pltpu.CompilerParams(has_side_effects=SPLIT_EFFECT),
    )(pltpu.with_memory_space_constraint(src, pltpu.HBM),
      pltpu.with_memory_space_constraint(lax.empty(land_shape, src.dtype), pltpu.HBM), after)


def _ici_wait(started, after, name):
    send_sems, recv_sems, src_thru, land_thru, _ = started
    gather = len(src_thru.shape) == 2

    def body(src_ref, land_ref, send_sems, recv_sems, after_ref, src_dead, got_ref):
        x, y, c = _coords()
        for j, (px, py) in enumerate(_other_chips(x, y)):
            copy = pltpu.make_async_remote_copy(
                src_ref=_ici_block(src_ref, px, py), dst_ref=land_ref.at[_ici_slot(gather, j, px, py, c)],
                send_sem=send_sems.at[j], recv_sem=recv_sems.at[j],
                device_id=(px, py, c), device_id_type=MESH_ID)
            copy.wait_send()
            copy.wait_recv()

    return pl.pallas_call(
        body, name=name,
        out_shape=(pltpu.HBM(src_thru.shape, src_thru.dtype), pltpu.HBM(land_thru.shape, land_thru.dtype)),
        in_specs=(HBM_SPEC, HBM_SPEC, SEM_SPEC, SEM_SPEC, ANY),
        out_specs=(HBM_SPEC, HBM_SPEC), input_output_aliases={0: 0, 1: 1},
        compiler_params=pltpu.CompilerParams(has_side_effects=SPLIT_EFFECT),
    )(src_thru, land_thru, send_sems, recv_sems, after)


def _core_share(own, gathered, name):
    r, w = own.shape

    def body(own_ref, gin_ref, out_ref, stage, send_sems, recv_sems, local_sem):
        x, y, c = _coords()
        sibling = (x, y, 1 - c)
        chips = [(x, y)] + _other_chips(x, y)
        stage_in = pltpu.make_async_copy(own_ref, stage, local_sem)
        stage_in.start()
        sent, arriving = [], []
        for k, (px, py) in enumerate(chips):
            rows = out_ref.at[4 * px + 2 * py + c]
            sent.append(pltpu.make_async_remote_copy(
                src_ref=own_ref if k == 0 else rows, dst_ref=rows,
                send_sem=send_sems.at[k], recv_sem=recv_sems.at[k], device_id=sibling, device_id_type=MESH_ID))
            arriving.append(pltpu.make_async_remote_copy(
                src_ref=own_ref, dst_ref=out_ref.at[4 * px + 2 * py + (1 - c)],
                send_sem=send_sems.at[k], recv_sem=recv_sems.at[k], device_id=sibling, device_id_type=MESH_ID))
        for cp in sent:
            cp.start()
        stage_in.wait()
        stage_out = pltpu.make_async_copy(stage, out_ref.at[4 * x + 2 * y + c], local_sem)
        stage_out.start()
        for cp in arriving:
            cp.wait_recv()
        for cp in sent:
            cp.wait_send()
        stage_out.wait()

    return pl.pallas_call(
        body, name=name,
        out_shape=jax.ShapeDtypeStruct((N_DEV, r, w), own.dtype),
        in_specs=[ANY, ANY], out_specs=ANY, input_output_aliases={1: 0},
        scratch_shapes=[pltpu.VMEM((r, w), own.dtype), pltpu.SemaphoreType.DMA((4,)),
                        pltpu.SemaphoreType.DMA((4,)), pltpu.SemaphoreType.DMA],
    )(own, gathered)


def _adamw(w, g, m, v):
    m = ADAM_B1 * m + (1.0 - ADAM_B1) * g
    v = ADAM_B2 * v + (1.0 - ADAM_B2) * (g * g)
    m_hat = m / ADAM_C1
    v_hat = v / ADAM_C2
    delta = -ADAM_LR * (m_hat / (jnp.sqrt(v_hat) + ADAM_EPS) + ADAM_WD * w)
    return delta, m, v


def _adam_big(part, land, w, m, v, chip_idx, tag, token):
    r, wd = w.shape
    tr, tw = _row_tile(r, 1024), 256

    def body(s_ref, tok_ref, p_ref, l_ref, w_ref, m_ref, v_ref, g_ref, d_ref, mo_ref, vo_ref):
        g = p_ref[0].astype(F32)
        for j in range(3):
            g = g + l_ref[j].astype(F32)
        delta, mn, vn = _adamw(w_ref[...], g, m_ref[...], v_ref[...])
        g_ref[...] = g
        d_ref[...] = delta
        mo_ref[...] = mn
        vo_ref[...] = vn

    row = pl.BlockSpec((tr, tw), lambda i, j, s: (i, j))
    return pl.pallas_call(
        body, name="adam_big_" + tag,
        grid_spec=pltpu.PrefetchScalarGridSpec(
            num_scalar_prefetch=1, grid=(r // tr, wd // tw),
            in_specs=[pl.BlockSpec((8, 128), lambda i, j, s: (0, 0)),
                      pl.BlockSpec((1, tr, tw), lambda i, j, s: (s[0], i, j)),
                      pl.BlockSpec((3, tr, tw), lambda i, j, s: (0, i, j)), row, row, row],
            out_specs=[row, row, row, row]),
        out_shape=[jax.ShapeDtypeStruct((r, wd), F32)] * 4,
        compiler_params=_cparams(),
    )(chip_idx, token, part, land, w, m, v)


def _adam_small(parts, w, m, v):
    _, r, wd = parts.shape

    def body(p_ref, w_ref, m_ref, v_ref, g_ref, d_ref, mo_ref, vo_ref):
        g = p_ref[0]
        for k in range(1, N_DEV):
            g = g + p_ref[k]
        delta, mn, vn = _adamw(w_ref[...], g, m_ref[...], v_ref[...])
        g_ref[...] = g
        d_ref[...] = delta
        mo_ref[...] = mn
        vo_ref[...] = vn

    return pl.pallas_call(
        body, name="adam_small",
        out_shape=[jax.ShapeDtypeStruct((r, wd), F32)] * 4,
        compiler_params=_cparams(),
    )(parts, w, m, v)


def _pad_rows(a, rows):
    return jnp.pad(a, ((0, rows - a.shape[0]), (0, 0)))


def _pad_w_in(w):
    cut = Q_RANK + KV_RANK + ROPE
    return jnp.concatenate([w[:, :cut], jnp.zeros((w.shape[0], 64), w.dtype), w[:, cut:]], axis=1)


def _unpad_w_in(w):
    cut = Q_RANK + KV_RANK + ROPE
    return jnp.concatenate([w[:, :cut], w[:, cut + 64:]], axis=1)


def _pack_big(p, group):
    if group == "mid":
        parts = [_pad_w_in(p["w_in"][0]), p["w_out"][0], p["w_mq"][0], p["w_mo"][0],
                 p["w_mkv"][0].reshape(256, D_MODEL),
                 _pad_rows(p["w_q_up"][0].T.reshape(24, D_MODEL), 32),
                 p["w_kv_up"][0].reshape(16, D_MODEL)]
    else:
        parts = [p[group + "_w_gate"][0].T, p[group + "_w_up"][0].T, p[group + "_w_down"][0]]
    return jnp.concatenate(parts, axis=0)


UNIT_WEIGHT = {"ffn1_g": ("ffn1_w_gate", True), "ffn1_u": ("ffn1_w_up", True), "ffn1_d": ("ffn1_w_down", False),
               "ffn2_g": ("ffn2_w_gate", True), "ffn2_u": ("ffn2_w_up", True), "ffn2_d": ("ffn2_w_down", False)}


def _pack_unit(p, unit):
    if unit == "mid":
        return _pack_big(p, "mid")
    name, transposed = UNIT_WEIGHT[unit]
    return p[name][0].T if transposed else p[name][0]


def _unpack_unit(a, unit):
    if unit != "mid":
        name, transposed = UNIT_WEIGHT[unit]
        return {name: (a.T if transposed else a)[None]}
    seg = lambda n: a[SEG_OFF[n][0]:SEG_OFF[n][0] + SEG_OFF[n][1]]
    return {"w_in": _unpad_w_in(seg("w_in"))[None], "w_out": seg("w_out")[None], "w_mq": seg("w_mq")[None],
            "w_mo": seg("w_mo")[None], "w_mkv": seg("w_mkv").reshape(D_MODEL, 256)[None],
            "w_q_up": seg("w_q")[:24].reshape(96, Q_RANK).T[None],
            "w_kv_up": seg("w_kv").reshape(KV_RANK, 128)[None]}


def _unpack_gathered(full, group):
    seg = lambda n: full[:, SEG_OFF[n][0]:SEG_OFF[n][0] + SEG_OFF[n][1]]
    rows = lambda n: seg(n).reshape(-1, D_MODEL)
    if group != "mid":
        return {n: rows(n) for n, _ in GROUP_SEGS[group]}
    wq_t = seg("w_q")[:, :24].reshape(MLA_HEADS, NOPE + ROPE, Q_RANK)
    wq_t = jnp.pad(wq_t, ((0, 0), (0, HEAD_PAD - NOPE - ROPE), (0, 0))).reshape(MLA_HEADS * HEAD_PAD, Q_RANK)
    wkv = seg("w_kv").reshape(N_DEV, KV_RANK, 128).transpose(1, 0, 2).reshape(KV_RANK, N_DEV * 128)
    return {"w_in": rows("w_in"), "w_out": rows("w_out"), "w_mq": rows("w_mq"), "w_mo": rows("w_mo"),
            "w_mkv": seg("w_mkv").reshape(N_DEV, D_MODEL, 256), "w_q": wq_t, "w_kv": wkv}


def _pack_grads(gr):
    blk = lambda a: a.reshape(N_DEV, -1, D_MODEL)
    dwq = gr["w_q"].reshape(MLA_HEADS, HEAD_PAD, Q_RANK)[:, :NOPE + ROPE].reshape(N_DEV, 24, D_MODEL)
    dwq = jnp.pad(dwq, ((0, 0), (0, 8), (0, 0)))
    dwkv = gr["w_kv"].reshape(KV_RANK, N_DEV, 128).transpose(1, 0, 2).reshape(N_DEV, 16, D_MODEL)
    parts = [blk(gr["w_in"]), blk(gr["w_out"]), blk(gr["w_mq"]), blk(gr["w_mo"]),
             gr["w_mkv"].reshape(N_DEV, 256, D_MODEL), dwq, dwkv]
    return jnp.concatenate([a.astype(BF16) for a in parts], axis=1)


def _pack_small(vals):
    parts = []
    for n, r in SMALL_ROWS:
        parts.append(_pad_rows(vals[n].reshape(-1, 128), r) if n in vals else jnp.zeros((r, 128), F32))
    return jnp.concatenate(parts, axis=0)


def _unpack_small(a, shapes):
    out = {}
    for n, shape in shapes.items():
        o = SMALL_OFF[n][0]
        out[n] = a[o:o + int(np.prod(shape)) // 128].reshape(shape)
    return out


BIG_NAMES = ("ffn1_w_gate", "ffn1_w_up", "ffn1_w_down", "w_in", "w_q_up", "w_kv_up", "w_out", "w_mq", "w_mkv",
             "w_mo", "ffn2_w_gate", "ffn2_w_up", "ffn2_w_down")
SMALL_NAMES = ("ffn1_norm", "mix_norm", "q_norm", "kv_norm", "pool_w", "pool_scale", "xattn_norm", "mem_norm",
               "ffn2_norm", "final_norm")
WEIGHT_ORDER = ("ffn1_norm", "ffn1_w_gate", "ffn1_w_up", "ffn1_w_down", "mix_norm", "w_in", "q_norm", "w_q_up",
                "kv_norm", "w_kv_up", "pool_w", "pool_scale", "w_out", "xattn_norm", "mem_norm", "w_mq", "w_mkv",
                "w_mo", "ffn2_norm", "ffn2_w_gate", "ffn2_w_up", "ffn2_w_down", "final_norm")


def _rope_table():
    lane = np.arange(128)
    freqs = (1.0 / (ROPE_BASE ** (np.arange(0, ROPE, 2, dtype=np.float32) / ROPE))).astype(np.float32)
    tab = np.zeros((8, 128), np.float32)
    tab[0] = np.where(lane < ROPE, freqs[lane % (ROPE // 2)], 0.0)
    tab[1] = np.where(lane < ROPE // 2, -1.0, np.where(lane < ROPE, 1.0, 0.0))
    return jnp.asarray(tab)


def kernel(x, mem, positions, ffn1_norm, ffn1_w_gate, ffn1_w_up, ffn1_w_down, mix_norm, w_in, q_norm, w_q_up, kv_norm, w_kv_up, pool_w, pool_scale, w_out, xattn_norm, mem_norm, w_mq, w_mkv, w_mo, ffn2_norm, ffn2_w_gate, ffn2_w_up, ffn2_w_down, final_norm, loss_target, m_ffn1_norm, m_ffn1_w_gate, m_ffn1_w_up, m_ffn1_w_down, m_mix_norm, m_w_in, m_q_norm, m_w_q_up, m_kv_norm, m_w_kv_up, m_pool_w, m_pool_scale, m_w_out, m_xattn_norm, m_mem_norm, m_w_mq, m_w_mkv, m_w_mo, m_ffn2_norm, m_ffn2_w_gate, m_ffn2_w_up, m_ffn2_w_down, m_final_norm, v_ffn1_norm, v_ffn1_w_gate, v_ffn1_w_up, v_ffn1_w_down, v_mix_norm, v_w_in, v_q_norm, v_w_q_up, v_kv_norm, v_w_kv_up, v_pool_w, v_pool_scale, v_w_out, v_xattn_norm, v_mem_norm, v_w_mq, v_w_mkv, v_w_mo, v_ffn2_norm, v_ffn2_w_gate, v_ffn2_w_up, v_ffn2_w_down, v_final_norm):
    wts = dict(ffn1_norm=ffn1_norm, ffn1_w_gate=ffn1_w_gate, ffn1_w_up=ffn1_w_up, ffn1_w_down=ffn1_w_down,
               mix_norm=mix_norm, w_in=w_in, q_norm=q_norm, w_q_up=w_q_up, kv_norm=kv_norm, w_kv_up=w_kv_up,
               pool_w=pool_w, pool_scale=pool_scale, w_out=w_out, xattn_norm=xattn_norm, mem_norm=mem_norm,
               w_mq=w_mq, w_mkv=w_mkv, w_mo=w_mo, ffn2_norm=ffn2_norm, ffn2_w_gate=ffn2_w_gate,
               ffn2_w_up=ffn2_w_up, ffn2_w_down=ffn2_w_down, final_norm=final_norm)
    mom = dict(ffn1_norm=m_ffn1_norm, ffn1_w_gate=m_ffn1_w_gate, ffn1_w_up=m_ffn1_w_up, ffn1_w_down=m_ffn1_w_down,
               mix_norm=m_mix_norm, w_in=m_w_in, q_norm=m_q_norm, w_q_up=m_w_q_up, kv_norm=m_kv_norm,
               w_kv_up=m_w_kv_up, pool_w=m_pool_w, pool_scale=m_pool_scale, w_out=m_w_out, xattn_norm=m_xattn_norm,
               mem_norm=m_mem_norm, w_mq=m_w_mq, w_mkv=m_w_mkv, w_mo=m_w_mo, ffn2_norm=m_ffn2_norm,
               ffn2_w_gate=m_ffn2_w_gate, ffn2_w_up=m_ffn2_w_up, ffn2_w_down=m_ffn2_w_down, final_norm=m_final_norm)
    var = dict(ffn1_norm=v_ffn1_norm, ffn1_w_gate=v_ffn1_w_gate, ffn1_w_up=v_ffn1_w_up, ffn1_w_down=v_ffn1_w_down,
               mix_norm=v_mix_norm, w_in=v_w_in, q_norm=v_q_norm, w_q_up=v_w_q_up, kv_norm=v_kv_norm,
               w_kv_up=v_w_kv_up, pool_w=v_pool_w, pool_scale=v_pool_scale, w_out=v_w_out, xattn_norm=v_xattn_norm,
               mem_norm=v_mem_norm, w_mq=v_w_mq, w_mkv=v_w_mkv, w_mo=v_w_mo, ffn2_norm=v_ffn2_norm,
               ffn2_w_gate=v_ffn2_w_gate, ffn2_w_up=v_ffn2_w_up, ffn2_w_down=v_ffn2_w_down, final_norm=v_final_norm)

    t = x.shape[1]
    xs = x[0]
    mems = mem[0]
    target = loss_target[0]
    pos = positions.reshape(t, 1)
    row = lambda a: a.reshape(1, -1)
    rope_tab = _rope_table()

    cx, cy, cc = _coords()
    core_idx = cc.astype(jnp.int32).reshape(1)
    chip_idx = (2 * cx + cy).astype(jnp.int32).reshape(1)

    w_pack, wb = {}, {}
    for grp in ("ffn1", "mid", "ffn2"):
        w_pack[grp] = _pack_big(wts, grp)
        wb[grp] = w_pack[grp].astype(BF16)
        if grp == "ffn1":
            ag_ffn1 = _ici_start(wb["ffn1"], pos, "ag_ffn1_start")
    own_ffn1, land_ffn1 = _ici_wait(ag_ffn1, wb["ffn2"], "ag_ffn1_wait")
    full_ffn1 = _core_share(own_ffn1, land_ffn1, "ag_ffn1_share")
    fw = _unpack_gathered(full_ffn1, "ffn1")
    ag_mid = _ici_start(wb["mid"], full_ffn1, "ag_mid_start")
    g_ffn1, g_mix, g_q, g_kv = row(ffn1_norm), row(mix_norm), row(q_norm), row(kv_norm)
    g_x, g_mem, g_ffn2, g_fin = row(xattn_norm), row(mem_norm), row(ffn2_norm), row(final_norm)
    pool_wb = pool_w[0].astype(BF16)
    pool_sc = row(pool_scale)

    h1, n1, gate1, up1 = _ffn_fwd(xs, g_ffn1, fw["ffn1_g"], fw["ffn1_u"], fw["ffn1_d"], "ffn1_fwd", token=ag_mid[4])
    own_mid, land_mid = _ici_wait(ag_mid, h1, "ag_mid_wait")
    full_mid = _core_share(own_mid, land_mid, "ag_mid_share")
    fw.update(_unpack_gathered(full_mid, "mid"))
    ag_ffn2 = _ici_start(wb["ffn2"], full_mid, "ag_ffn2_start")
    u, z, qn, kvn, qh, kh, vh = _mix_prep(h1, g_mix, fw["w_in"], g_q, fw["w_q"], g_kv, fw["w_kv"], pos, rope_tab,
                                          token=ag_ffn2[4])
    a, lse = _attn_fwd(qh, kh, vh)
    p = _pool_fwd(z, pool_wb, pool_sc)
    memn, km, vm = _mem_kv(mems, g_mem, fw["w_mkv"])
    h2, h3, hn, qm, om = _xattn_fwd(h1, a, p, fw["w_out"], g_x, fw["w_mq"], km, vm, fw["w_mo"])
    own_ffn2, land_ffn2 = _ici_wait(ag_ffn2, h3, "ag_ffn2_wait")
    fw.update(_unpack_gathered(_core_share(own_ffn2, land_ffn2, "ag_ffn2_share"), "ffn2"))
    h4, n2, gate2, up2 = _ffn_fwd(h3, g_ffn2, fw["ffn2_g"], fw["ffn2_u"], fw["ffn2_d"], "ffn2_fwd")
    loss_part, dh4, dg_fin = _loss_head(h4, target, g_fin)

    def reduce_start(g8, unit):
        part = _chip_partial(g8, _exchange_core(g8, unit), core_idx, unit)
        return _ici_start(part, g8, "rs_" + unit + "_start")

    def by_device(g):
        return g.reshape(N_DEV, -1, D_MODEL)

    rs = {}
    dh3, dgate2, dup2, act2, dg_ffn2 = _ffn_bwd_data(dh4, h3, g_ffn2, gate2, up2, fw["ffn2_g"], fw["ffn2_u"],
                                                     fw["ffn2_d"], "ffn2_bwd")
    rs["ffn2_g"] = reduce_start(by_device(_tn_matmul(dgate2, n2, "ffn2_dwg", tmm=1408, out_dtype=BF16)), "ffn2_g")
    rs["ffn2_u"] = reduce_start(by_device(_tn_matmul(dup2, n2, "ffn2_dwu", tmm=1408, out_dtype=BF16,
                                                     token=rs["ffn2_g"][4])), "ffn2_u")
    rs["ffn2_d"] = reduce_start(by_device(_tn_matmul(act2, dh4, "ffn2_dwd", scale=0.5, tmm=1408, out_dtype=BF16,
                                                     token=rs["ffn2_u"][4])), "ffn2_d")
    dh2, dqm, da, dp, dkm, dvm, dg_x = _xattn_bwd(dh3, h2, qm, g_x, fw["w_mq"], km, vm, fw["w_mo"], fw["w_out"],
                                                  token=rs["ffn2_d"][4])
    gr = {}
    gr["w_mo"] = _tn_matmul(om, dh3, "dw_mo", tmm=512, out_dtype=BF16)
    gr["w_mq"] = _tn_matmul(hn, dqm, "dw_mq", tmm=512, out_dtype=BF16)
    gr["w_out"] = jnp.concatenate([_tn_matmul(a, dh2, "dw_out_a", out_dtype=BF16),
                                   _tn_matmul(p, dh2, "dw_out_p", out_dtype=BF16)], axis=0)
    gr["w_mkv"], dg_mem = _mem_kv_bwd(dkm, dvm, memn, mems, g_mem, fw["w_mkv"])
    dz_pool, d_pool_w, d_pool_sc = _pool_bwd(dp, z, pool_wb, pool_sc)
    dqh, dkh, dvh = _attn_bwd(qh, kh, vh, da, lse, _attn_delta(a, da))
    dh1, dq, dkv, dz, dg_q, dg_kv, dg_mix = _mla_bwd(dqh, dkh, dvh, z, dz_pool, h1, dh2, g_mix, fw["w_in"], g_q,
                                                     fw["w_q"], g_kv, fw["w_kv"], pos, rope_tab)
    gr["w_q"] = _tn_matmul(dq, qn, "dw_q", tmm=512, out_dtype=BF16)
    gr["w_kv"] = _tn_matmul(kvn, dkv, "dw_kv", out_dtype=BF16)
    gr["w_in"] = _tn_matmul(u, dz, "dw_in", tmm=512, out_dtype=BF16)
    g_mid = _pack_grads(gr)
    part_mid = _chip_partial(g_mid, _exchange_core(g_mid, "mid"), core_idx, "mid")
    got = {}
    after = part_mid
    for unit in ("ffn2_g", "ffn2_u", "ffn2_d"):
        got[unit] = _ici_wait(rs[unit], after, "rs_" + unit + "_wait")
        after = got[unit][1]
    rs["mid"] = _ici_start(part_mid, after, "rs_mid_start")
    dx, dgate1, dup1, act1, dg_ffn1 = _ffn_bwd_data(dh1, xs, g_ffn1, gate1, up1, fw["ffn1_g"], fw["ffn1_u"],
                                                    fw["ffn1_d"], "ffn1_bwd", token=rs["mid"][4])
    got["mid"] = _ici_wait(rs["mid"], dx, "rs_mid_wait")

    small_g = dict(ffn1_norm=dg_ffn1, mix_norm=dg_mix, q_norm=dg_q, kv_norm=dg_kv, pool_w=d_pool_w,
                   pool_scale=d_pool_sc, xattn_norm=dg_x, mem_norm=dg_mem, ffn2_norm=dg_ffn2, final_norm=dg_fin,
                   loss=loss_part)
    parts = _all_gather_direct(_pack_small(small_g), got["mid"][1])
    small = _adam_small(parts, _pack_small({n: wts[n] for n in SMALL_NAMES}),
                        _pack_small({n: mom[n] for n in SMALL_NAMES}), _pack_small({n: var[n] for n in SMALL_NAMES}))
    small_sum = small[0]
    loss = small_sum[SMALL_OFF["loss"][0], 0]
    shapes = {n: wts[n].shape for n in SMALL_NAMES}
    small = [_unpack_small(s, shapes) for s in small]

    rs["ffn1_g"] = reduce_start(by_device(_tn_matmul(dgate1, n1, "ffn1_dwg", tmm=1408, out_dtype=BF16,
                                                     token=small_sum)), "ffn1_g")
    rs["ffn1_u"] = reduce_start(by_device(_tn_matmul(dup1, n1, "ffn1_dwu", tmm=1408, out_dtype=BF16,
                                                     token=rs["ffn1_g"][4])), "ffn1_u")
    rs["ffn1_d"] = reduce_start(by_device(_tn_matmul(act1, dh1, "ffn1_dwd", scale=0.5, tmm=1408, out_dtype=BF16,
                                                     token=rs["ffn1_u"][4])), "ffn1_d")

    big = {}

    def adam_unit(unit, token):
        part, land = got[unit]
        res = _adam_big(part, land, _pack_unit(wts, unit), _pack_unit(mom, unit), _pack_unit(var, unit),
                        chip_idx, unit, token)
        for k, packed in enumerate(res):
            big.setdefault(k, {}).update(_unpack_unit(packed, unit))
        return res[0]

    done = rs["ffn1_d"][4]
    for unit in ("mid", "ffn2_g", "ffn2_u", "ffn2_d"):
        done = adam_unit(unit, done)
    for unit in ("ffn1_g", "ffn1_u", "ffn1_d"):
        got[unit] = _ici_wait(rs[unit], done, "rs_" + unit + "_wait")
        done = adam_unit(unit, done)

    outs = [loss, dx[None]]
    for k in range(4):
        for n in WEIGHT_ORDER:
            outs.append(big[k][n] if n in BIG_NAMES else small[k][n])
    return tuple(outs)
```

```python
import numpy as np

import jax
import jax.numpy as jnp
from jax import lax
from jax.experimental import pallas as pl
from jax.experimental.pallas import tpu as pltpu

F32 = jnp.float32
BF16 = jnp.bfloat16

N_DEV = 8
D_MODEL = 1024
D_FF = 2816
MLA_HEADS = 4
NOPE = 128
ROPE = 64
HEAD_PAD = 256
V_DIM = 128
Q_RANK = 256
KV_RANK = 128
POOL_WINDOWS = (2, 4, 8, 16)
POOL_CH = 128
POOL_HALO = 16
N_MEM = 256
MEM_HEADS = 4
MEM_HD = 256
ROPE_BASE = 10000.0
RMS_EPS = 1e-6
ATTN_SCALE = (NOPE + ROPE) ** -0.5
MEM_SCALE = MEM_HD ** -0.5
NEG_BIG = -1e30

ADAM_LR = 0.001
ADAM_B1 = 0.9
ADAM_B2 = 0.999
ADAM_EPS = 1e-08
ADAM_WD = 0.01
ADAM_STEP = 10
ADAM_C1 = 1.0 - ADAM_B1 ** ADAM_STEP
ADAM_C2 = 1.0 - ADAM_B2 ** ADAM_STEP

VMEM_LIMIT_BYTES = 56 * 1024 * 1024
BF16_ROWS = 16

GROUP_SEGS = {
    "ffn1": (("ffn1_g", 352), ("ffn1_u", 352), ("ffn1_d", 352)),
    "mid": (("w_in", 128), ("w_out", 128), ("w_mq", 128), ("w_mo", 128), ("w_mkv", 256), ("w_q", 32), ("w_kv", 16)),
    "ffn2": (("ffn2_g", 352), ("ffn2_u", 352), ("ffn2_d", 352)),
}
SEG_OFF = {}
GROUP_ROWS = {}
for _g, _segs in GROUP_SEGS.items():
    _o = 0
    for _n, _r in _segs:
        SEG_OFF[_n] = (_o, _r)
        _o += _r
    GROUP_ROWS[_g] = _o

SMALL_ROWS = (("ffn1_norm", 8), ("mix_norm", 8), ("q_norm", 8), ("kv_norm", 8), ("pool_w", 512), ("pool_scale", 8),
              ("xattn_norm", 8), ("mem_norm", 8), ("ffn2_norm", 8), ("final_norm", 8), ("loss", 8))
SMALL_OFF = {}
_o = 0
for _n, _r in SMALL_ROWS:
    SMALL_OFF[_n] = (_o, _r)
    _o += _r


def _cparams(**kw):
    return pltpu.CompilerParams(vmem_limit_bytes=VMEM_LIMIT_BYTES, **kw)


def _row_tile(rows, limit):
    best = None
    for cand in range(BF16_ROWS, min(rows, limit) + 1, BF16_ROWS):
        if rows % cand == 0:
            best = cand
    assert best is not None, rows
    return best


def _dot_nn(a, b):
    return lax.dot_general(a, b, (((1,), (0,)), ((), ())), preferred_element_type=F32)


def _dot_nt(a, b):
    return lax.dot_general(a, b, (((1,), (1,)), ((), ())), preferred_element_type=F32)


def _dot_tn(a, b):
    return lax.dot_general(a, b, (((0,), (0,)), ((), ())), preferred_element_type=F32)


def _rms_fwd(x, g):
    r = lax.rsqrt(jnp.mean(x * x, axis=-1, keepdims=True) + RMS_EPS)
    return x * r * g, r


def _rms_bwd(dy, x, g, r):
    xhat = x * r
    dyg = dy * g
    dx = r * (dyg - xhat * jnp.mean(dyg * xhat, axis=-1, keepdims=True))
    dg = jnp.sum(dy * xhat, axis=0, keepdims=True)
    return dx, dg


def _accumulate(ref, val, first):
    if isinstance(first, bool):
        if first:
            ref[...] = val
        else:
            ref[...] += val
        return

    @pl.when(first)
    def _():
        ref[...] = val

    @pl.when(jnp.logical_not(first))
    def _():
        ref[...] += val


def _call_after(token, body, in_specs, args, **kw):
    if token is not None:
        inner = body
        body = lambda tok_ref, *refs: inner(*refs)
        in_specs = [pl.BlockSpec((8, 128), lambda *_: (0, 0))] + list(in_specs)
        args = (token,) + tuple(args)
    return pl.pallas_call(body, in_specs=in_specs, **kw)(*args)


def _rope_tables(pos_col, tab):
    ang = pos_col.astype(F32) * tab[0:1, :]
    return jnp.cos(ang), jnp.sin(ang) * tab[1:2, :]


def _swap_halves(x):
    lane = lax.broadcasted_iota(jnp.int32, x.shape, 1)
    return jnp.where((lane % 64) < 32, pltpu.roll(x, 96, 1), pltpu.roll(x, 32, 1))


def _rope_apply(x, cos_t, sin_t):
    return x * cos_t + _swap_halves(x) * sin_t


def _rope_apply_t(dy, cos_t, sin_t):
    return dy * cos_t + _swap_halves(dy * sin_t)


def _ffn_fwd(h, g, wg_t, wu_t, wd, name, token=None):
    t, d = h.shape
    f = wg_t.shape[0]
    tm, tf = min(1024, t), 256
    parts = 2 if tm % 512 == 0 else 1
    tp = tm // parts
    nf = f // tf

    def body(h_ref, g_ref, wg_ref, wu_ref, wd_ref, ho_ref, n_ref, gate_ref, up_ref, nb_sc, acc_sc):
        j = pl.program_id(1)

        @pl.when(j == 0)
        def _():
            y, _ = _rms_fwd(h_ref[...], g_ref[...])
            nb = y.astype(BF16)
            nb_sc[...] = nb
            n_ref[...] = nb
            acc_sc[...] = jnp.zeros_like(acc_sc)

        for r in range(parts):
            rows = pl.ds(r * tp, tp)
            nb = nb_sc[rows, :]
            gt = _dot_nt(nb, wg_ref[...])
            ut = _dot_nt(nb, wu_ref[...])
            gate_ref[rows, :] = gt.astype(BF16)
            up_ref[rows, :] = ut.astype(BF16)
            act = (gt * jax.nn.sigmoid(gt)) * ut
            acc_sc[rows, :] += _dot_nn(act.astype(BF16), wd_ref[...])

        @pl.when(j == nf - 1)
        def _():
            ho_ref[...] = h_ref[...] + 0.5 * acc_sc[...]

    return _call_after(
        token, body,
        [pl.BlockSpec((tm, d), lambda i, j: (i, 0)),
         pl.BlockSpec((1, d), lambda i, j: (0, 0)),
         pl.BlockSpec((tf, d), lambda i, j: (j, 0)),
         pl.BlockSpec((tf, d), lambda i, j: (j, 0)),
         pl.BlockSpec((tf, d), lambda i, j: (j, 0))],
        (h, g, wg_t, wu_t, wd),
        name=name, grid=(t // tm, nf),
        out_specs=[pl.BlockSpec((tm, d), lambda i, j: (i, 0)),
                   pl.BlockSpec((tm, d), lambda i, j: (i, 0)),
                   pl.BlockSpec((tm, tf), lambda i, j: (i, j)),
                   pl.BlockSpec((tm, tf), lambda i, j: (i, j))],
        out_shape=[jax.ShapeDtypeStruct((t, d), F32), jax.ShapeDtypeStruct((t, d), BF16),
                   jax.ShapeDtypeStruct((t, f), BF16), jax.ShapeDtypeStruct((t, f), BF16)],
        scratch_shapes=[pltpu.VMEM((tm, d), BF16), pltpu.VMEM((tm, d), F32)],
        compiler_params=_cparams(),
    )


def _ffn_bwd_data(dho, h, g, gate, up, wg_t, wu_t, wd, name, token=None):
    t, d = h.shape
    f = wg_t.shape[0]
    tm, tf = min(1024, t), 256
    parts = 2 if tm % 512 == 0 else 1
    tp = tm // parts
    nf = f // tf

    def body(dho_ref, h_ref, g_ref, gate_ref, up_ref, wg_ref, wu_ref, wd_ref,
             dh_ref, dgate_ref, dup_ref, act_ref, dg_ref, dhb_sc, acc_sc):
        i, j = pl.program_id(0), pl.program_id(1)

        @pl.when(j == 0)
        def _():
            dhb_sc[...] = (0.5 * dho_ref[...]).astype(BF16)
            acc_sc[...] = jnp.zeros_like(acc_sc)

        for r in range(parts):
            rows = pl.ds(r * tp, tp)
            dact = _dot_nt(dhb_sc[rows, :], wd_ref[...])
            gt = gate_ref[rows, :].astype(F32)
            ut = up_ref[rows, :].astype(F32)
            sg = jax.nn.sigmoid(gt)
            silu = gt * sg
            dgb = (dact * ut * (sg * (1.0 + gt * (1.0 - sg)))).astype(BF16)
            dub = (dact * silu).astype(BF16)
            act_ref[rows, :] = (silu * ut).astype(BF16)
            dgate_ref[rows, :] = dgb
            dup_ref[rows, :] = dub
            acc_sc[rows, :] += _dot_nn(dgb, wg_ref[...]) + _dot_nn(dub, wu_ref[...])

        @pl.when(j == nf - 1)
        def _():
            x = h_ref[...]
            gg = g_ref[...]
            _, r = _rms_fwd(x, gg)
            dx, dg = _rms_bwd(acc_sc[...], x, gg, r)
            dh_ref[...] = dho_ref[...] + dx
            _accumulate(dg_ref, dg, i == 0)

    return _call_after(
        token, body,
        [pl.BlockSpec((tm, d), lambda i, j: (i, 0)),
         pl.BlockSpec((tm, d), lambda i, j: (i, 0)),
         pl.BlockSpec((1, d), lambda i, j: (0, 0)),
         pl.BlockSpec((tm, tf), lambda i, j: (i, j)),
         pl.BlockSpec((tm, tf), lambda i, j: (i, j)),
         pl.BlockSpec((tf, d), lambda i, j: (j, 0)),
         pl.BlockSpec((tf, d), lambda i, j: (j, 0)),
         pl.BlockSpec((tf, d), lambda i, j: (j, 0))],
        (dho, h, g, gate, up, wg_t, wu_t, wd),
        name=name, grid=(t // tm, nf),
        out_specs=[pl.BlockSpec((tm, d), lambda i, j: (i, 0)),
                   pl.BlockSpec((tm, tf), lambda i, j: (i, j)),
                   pl.BlockSpec((tm, tf), lambda i, j: (i, j)),
                   pl.BlockSpec((tm, tf), lambda i, j: (i, j)),
                   pl.BlockSpec((1, d), lambda i, j: (0, 0))],
        out_shape=[jax.ShapeDtypeStruct((t, d), F32), jax.ShapeDtypeStruct((t, f), BF16),
                   jax.ShapeDtypeStruct((t, f), BF16), jax.ShapeDtypeStruct((t, f), BF16),
                   jax.ShapeDtypeStruct((1, d), F32)],
        scratch_shapes=[pltpu.VMEM((tm, d), BF16), pltpu.VMEM((tm, d), F32)],
        compiler_params=_cparams(),
    )


def _tn_matmul(a, b, name, scale=1.0, tmm=None, out_dtype=F32, token=None):
    t, m = a.shape
    n = b.shape[1]
    tmm = m if tmm is None else tmm
    tk = min(1024, t)
    nk = t // tk

    def product(a_ref, b_ref):
        prod = _dot_tn(a_ref[...].astype(BF16), b_ref[...].astype(BF16))
        return prod * scale if scale != 1.0 else prod

    def body_f32(a_ref, b_ref, o_ref):
        _accumulate(o_ref, product(a_ref, b_ref), pl.program_id(1) == 0)

    def body_cast(a_ref, b_ref, o_ref, acc_sc):
        k = pl.program_id(1)
        _accumulate(acc_sc, product(a_ref, b_ref), k == 0)

        @pl.when(k == nk - 1)
        def _():
            o_ref[...] = acc_sc[...].astype(out_dtype)

    direct = out_dtype == F32
    return _call_after(
        token, body_f32 if direct else body_cast,
        [pl.BlockSpec((tk, tmm), lambda i, k: (k, i)),
         pl.BlockSpec((tk, n), lambda i, k: (k, 0))],
        (a, b),
        name=name, grid=(m // tmm, nk),
        out_specs=pl.BlockSpec((tmm, n), lambda i, k: (i, 0)),
        out_shape=jax.ShapeDtypeStruct((m, n), out_dtype),
        scratch_shapes=[] if direct else [pltpu.VMEM((tmm, n), F32)],
        compiler_params=_cparams(),
    )


def _loss_head(h, target, g):
    t, d = h.shape
    tm = min(512, t)

    def body(h_ref, t_ref, g_ref, loss_ref, dh_ref, dg_ref):
        i = pl.program_id(0)
        x = h_ref[...]
        gg = g_ref[...]
        y, r = _rms_fwd(x, gg)
        err = y - t_ref[...]
        part = 0.5 * jnp.sum(jnp.mean(err * err, axis=-1, keepdims=True), axis=0, keepdims=True)
        dx, dg = _rms_bwd(err * (1.0 / d), x, gg, r)
        dh_ref[...] = dx
        _accumulate(loss_ref, jnp.broadcast_to(part, loss_ref.shape), i == 0)
        _accumulate(dg_ref, dg, i == 0)

    return pl.pallas_call(
        body, name="loss_head", grid=(t // tm,),
        in_specs=[pl.BlockSpec((tm, d), lambda i: (i, 0)),
                  pl.BlockSpec((tm, d), lambda i: (i, 0)),
                  pl.BlockSpec((1, d), lambda i: (0, 0))],
        out_specs=[pl.BlockSpec((8, 128), lambda i: (0, 0)),
                   pl.BlockSpec((tm, d), lambda i: (i, 0)),
                   pl.BlockSpec((1, d), lambda i: (0, 0))],
        out_shape=[jax.ShapeDtypeStruct((8, 128), F32), jax.ShapeDtypeStruct((t, d), F32),
                   jax.ShapeDtypeStruct((1, d), F32)],
        compiler_params=_cparams(),
    )(h, target, g)


def _mix_prep(h1, mix_norm, w_in, q_norm, wq_t, kv_norm, wkv, pos, rope_tab, token=None):
    t, d = h1.shape
    tm = min(512, t)

    def body(h_ref, gm_ref, win_ref, gq_ref, wq_ref, gkv_ref, wkv_ref, pos_ref, tab_ref,
             u_ref, z_ref, qn_ref, kvn_ref, q_ref, k_ref, v_ref):
        u, _ = _rms_fwd(h_ref[...], gm_ref[...])
        ub = u.astype(BF16)
        u_ref[...] = ub
        z = _dot_nn(ub, win_ref[...])
        z_ref[...] = z
        cos_t, sin_t = _rope_tables(pos_ref[...], tab_ref[...])
        qn, _ = _rms_fwd(z[:, 0:Q_RANK], gq_ref[...])
        qnb = qn.astype(BF16)
        qn_ref[...] = qnb
        q = _dot_nt(qnb, wq_ref[...])
        kvn, _ = _rms_fwd(z[:, Q_RANK:Q_RANK + KV_RANK], gkv_ref[...])
        kvnb = kvn.astype(BF16)
        kvn_ref[...] = kvnb
        kv = _dot_nn(kvnb, wkv_ref[...])
        k_pe = _rope_apply(z[:, Q_RANK + KV_RANK:Q_RANK + KV_RANK + 128], cos_t, sin_t)
        ones = jnp.ones((tm, V_DIM), F32)
        for hh in range(MLA_HEADS):
            b = hh * HEAD_PAD
            q_pe = _rope_apply(q[:, b + NOPE:b + HEAD_PAD], cos_t, sin_t)
            q_ref[hh] = jnp.concatenate([q[:, b:b + NOPE], q_pe], axis=-1).astype(BF16)
            k_ref[hh] = jnp.concatenate([kv[:, b:b + NOPE], k_pe], axis=-1).astype(BF16)
            v_ref[hh] = jnp.concatenate([kv[:, b + NOPE:b + HEAD_PAD], ones], axis=-1).astype(BF16)

    full = lambda shape: pl.BlockSpec(shape, lambda i: (0,) * len(shape))
    return _call_after(
        token, body,
        [pl.BlockSpec((tm, d), lambda i: (i, 0)), full((1, d)), full(w_in.shape), full((1, Q_RANK)),
         full(wq_t.shape), full((1, KV_RANK)), full(wkv.shape),
         pl.BlockSpec((tm, 1), lambda i: (i, 0)), full(rope_tab.shape)],
        (h1, mix_norm, w_in, q_norm, wq_t, kv_norm, wkv, pos, rope_tab),
        name="mix_prep", grid=(t // tm,),
        out_specs=[pl.BlockSpec((tm, d), lambda i: (i, 0)),
                   pl.BlockSpec((tm, d), lambda i: (i, 0)),
                   pl.BlockSpec((tm, Q_RANK), lambda i: (i, 0)),
                   pl.BlockSpec((tm, KV_RANK), lambda i: (i, 0)),
                   pl.BlockSpec((MLA_HEADS, tm, HEAD_PAD), lambda i: (0, i, 0)),
                   pl.BlockSpec((MLA_HEADS, tm, HEAD_PAD), lambda i: (0, i, 0)),
                   pl.BlockSpec((MLA_HEADS, tm, 2 * V_DIM), lambda i: (0, i, 0))],
        out_shape=[jax.ShapeDtypeStruct((t, d), BF16), jax.ShapeDtypeStruct((t, d), F32),
                   jax.ShapeDtypeStruct((t, Q_RANK), BF16), jax.ShapeDtypeStruct((t, KV_RANK), BF16),
                   jax.ShapeDtypeStruct((MLA_HEADS, t, HEAD_PAD), BF16),
                   jax.ShapeDtypeStruct((MLA_HEADS, t, HEAD_PAD), BF16),
                   jax.ShapeDtypeStruct((MLA_HEADS, t, 2 * V_DIM), BF16)],
        compiler_params=_cparams(),
    )


def _causal_mask(s):
    row = lax.broadcasted_iota(jnp.int32, s.shape, 0)
    col = lax.broadcasted_iota(jnp.int32, s.shape, 1)
    return jnp.where(col <= row, s, NEG_BIG)


def _attn_fwd(q, k, v):
    nh, t, _ = q.shape
    tq = tk = min(512, t)
    nq, nk = t // tq, t // tk

    def body(q_ref, k_ref, v_ref, o_ref, lse_ref, m_sc, acc_sc):
        i, j = pl.program_id(0), pl.program_id(1)

        @pl.when(j == 0)
        def _():
            m_sc[...] = jnp.full_like(m_sc, NEG_BIG)
            acc_sc[...] = jnp.zeros_like(acc_sc)

        def step(diagonal):
            for hh in range(nh):
                s = _dot_nt(q_ref[hh], k_ref[hh]) * ATTN_SCALE
                if diagonal:
                    s = _causal_mask(s)
                m_old = m_sc[hh]
                m_new = jnp.maximum(m_old, jnp.max(s, axis=-1, keepdims=True))
                p = jnp.exp(s - m_new).astype(BF16)
                acc_sc[hh] = jnp.exp(m_old - m_new) * acc_sc[hh] + _dot_nn(p, v_ref[hh])
                m_sc[hh] = m_new

        @pl.when(j < i)
        def _():
            step(False)

        @pl.when(j == i)
        def _():
            step(True)
            for hh in range(nh):
                acc = acc_sc[hh]
                l = acc[:, V_DIM:2 * V_DIM]
                o_ref[:, hh * V_DIM:(hh + 1) * V_DIM] = (acc[:, 0:V_DIM] / l).astype(BF16)
                lse_ref[hh] = m_sc[hh] + jnp.log(l[:, 0:1])

    kv_map = lambda i, j: (0, jnp.minimum(j, i), 0)
    return pl.pallas_call(
        body, name="attn_fwd", grid=(nq, nk),
        in_specs=[pl.BlockSpec((nh, tq, HEAD_PAD), lambda i, j: (0, i, 0)),
                  pl.BlockSpec((nh, tk, HEAD_PAD), kv_map),
                  pl.BlockSpec((nh, tk, 2 * V_DIM), kv_map)],
        out_specs=[pl.BlockSpec((tq, nh * V_DIM), lambda i, j: (i, 0)),
                   pl.BlockSpec((nh, tq, 1), lambda i, j: (0, i, 0))],
        out_shape=[jax.ShapeDtypeStruct((t, nh * V_DIM), BF16), jax.ShapeDtypeStruct((nh, t, 1), F32)],
        scratch_shapes=[pltpu.VMEM((nh, tq, 1), F32), pltpu.VMEM((nh, tq, 2 * V_DIM), F32)],
        compiler_params=_cparams(),
    )(q, k, v)


def _attn_delta(o, do):
    t, w = o.shape
    nh = w // V_DIM
    tm = min(512, t)

    def body(o_ref, do_ref, d_ref):
        prod = o_ref[...].astype(F32) * do_ref[...].astype(F32)
        for hh in range(nh):
            d_ref[hh] = jnp.sum(prod[:, hh * V_DIM:(hh + 1) * V_DIM], axis=-1, keepdims=True)

    return pl.pallas_call(
        body, name="attn_delta", grid=(t // tm,),
        in_specs=[pl.BlockSpec((tm, w), lambda i: (i, 0)), pl.BlockSpec((tm, w), lambda i: (i, 0))],
        out_specs=pl.BlockSpec((nh, tm, 1), lambda i: (0, i, 0)),
        out_shape=jax.ShapeDtypeStruct((nh, t, 1), F32),
        compiler_params=_cparams(),
    )(o, do)


ATTN_BWD_HEADS = 2


def _attn_bwd(q, k, v, do, lse, delta):
    nh, t, _ = q.shape
    hp = ATTN_BWD_HEADS
    tq = tk = min(512, t)
    nq, nk = t // tq, t // tk

    def body(q_ref, k_ref, v_ref, do_ref, lse_ref, dlt_ref, dq_ref, dk_ref, dv_ref):
        j, i = pl.program_id(1), pl.program_id(2)

        @pl.when(jnp.logical_and(j == 0, i == 0))
        def _():
            dq_ref[...] = jnp.zeros_like(dq_ref)

        def step(diagonal):
            for hh in range(hp):
                qq, kk = q_ref[hh], k_ref[hh]
                dob = do_ref[:, hh * V_DIM:(hh + 1) * V_DIM]
                s = _dot_nt(qq, kk) * ATTN_SCALE
                if diagonal:
                    s = _causal_mask(s)
                p = jnp.exp(s - lse_ref[hh])
                dpp = _dot_nt(dob, v_ref[hh])
                dsb = (p * (dpp - dlt_ref[hh]) * ATTN_SCALE).astype(BF16)
                _accumulate(dv_ref.at[hh], _dot_tn(p.astype(BF16), dob), diagonal)
                _accumulate(dk_ref.at[hh], _dot_tn(dsb, qq), diagonal)
                dq_ref[hh, pl.ds(pl.multiple_of(i * tq, tq), tq), :] += _dot_nn(dsb, kk)

        @pl.when(i > j)
        def _():
            step(False)

        @pl.when(i == j)
        def _():
            step(True)

    qmap = lambda h, j, i: (h, jnp.maximum(i, j), 0)
    return pl.pallas_call(
        body, name="attn_bwd", grid=(nh // hp, nk, nq),
        in_specs=[pl.BlockSpec((hp, tq, HEAD_PAD), qmap),
                  pl.BlockSpec((hp, tk, HEAD_PAD), lambda h, j, i: (h, j, 0)),
                  pl.BlockSpec((hp, tk, V_DIM), lambda h, j, i: (h, j, 0)),
                  pl.BlockSpec((tq, hp * V_DIM), lambda h, j, i: (jnp.maximum(i, j), h)),
                  pl.BlockSpec((hp, tq, 1), qmap),
                  pl.BlockSpec((hp, tq, 1), qmap)],
        out_specs=[pl.BlockSpec((hp, t, HEAD_PAD), lambda h, j, i: (h, 0, 0)),
                   pl.BlockSpec((hp, tk, HEAD_PAD), lambda h, j, i: (h, j, 0)),
                   pl.BlockSpec((hp, tk, V_DIM), lambda h, j, i: (h, j, 0))],
        out_shape=[jax.ShapeDtypeStruct((nh, t, HEAD_PAD), F32), jax.ShapeDtypeStruct((nh, t, HEAD_PAD), F32),
                   jax.ShapeDtypeStruct((nh, t, V_DIM), F32)],
        compiler_params=_cparams(),
    )(q, k, v, do, lse, delta)


def _pool_counts(first_token, rows, w):
    tok = lax.broadcasted_iota(jnp.int32, (rows, POOL_CH), 0) + first_token
    return jnp.minimum(tok + 1, w).astype(F32)


def _pool_centered(zbuf, g, w, i, tm):
    lanes = pl.ds(g * POOL_CH, POOL_CH)
    cur = zbuf[pl.ds(POOL_HALO, tm), lanes]
    win = cur
    for s in range(1, w):
        win = win + zbuf[pl.ds(POOL_HALO - s, tm), lanes]
    return win / _pool_counts(i * tm, tm, w) - cur


def _pool_load(zbuf, z_ref, halo_ref, i, tm):
    @pl.when(i == 0)
    def _():
        zbuf[pl.ds(0, POOL_HALO), :] = jnp.zeros((POOL_HALO, zbuf.shape[1]), F32)

    @pl.when(i > 0)
    def _():
        zbuf[pl.ds(0, POOL_HALO), :] = halo_ref[...]

    zbuf[pl.ds(POOL_HALO, tm), :] = z_ref[...]


def _pool_fwd(z, pool_w, pool_scale):
    t = z.shape[0]
    pw = len(POOL_WINDOWS) * POOL_CH
    tm = min(512, t)
    hb = tm // POOL_HALO

    def body(z_ref, halo_ref, w_ref, sc_ref, p_ref, zbuf):
        i = pl.program_id(0)
        _pool_load(zbuf, z_ref, halo_ref, i, tm)
        for g, w in enumerate(POOL_WINDOWS):
            c = _pool_centered(zbuf, g, w, i, tm)
            y = _dot_nn(c.astype(BF16), w_ref[g]) * sc_ref[:, g * POOL_CH:(g + 1) * POOL_CH]
            p_ref[:, g * POOL_CH:(g + 1) * POOL_CH] = y.astype(BF16)

    return pl.pallas_call(
        body, name="pool_fwd", grid=(t // tm,),
        in_specs=[pl.BlockSpec((tm, pw), lambda i: (i, 1)),
                  pl.BlockSpec((POOL_HALO, pw), lambda i: (jnp.maximum(i * hb - 1, 0), 1)),
                  pl.BlockSpec(pool_w.shape, lambda i: (0, 0, 0)),
                  pl.BlockSpec((1, pw), lambda i: (0, 0))],
        out_specs=pl.BlockSpec((tm, pw), lambda i: (i, 0)),
        out_shape=jax.ShapeDtypeStruct((t, pw), BF16),
        scratch_shapes=[pltpu.VMEM((POOL_HALO + tm, pw), F32)],
        compiler_params=_cparams(),
    )(z, z, pool_w, pool_scale)


def _pool_bwd(dp, z, pool_w, pool_scale):
    t = z.shape[0]
    ng = len(POOL_WINDOWS)
    pw = ng * POOL_CH
    tm = min(512, t)
    hb = tm // POOL_HALO
    nt = t // tm

    def body(dp_ref, dpn_ref, z_ref, halo_ref, w_ref, sc_ref, dz_ref, dw_ref, dsc_ref, zbuf, dbuf):
        i = pl.program_id(0)
        _pool_load(zbuf, z_ref, halo_ref, i, tm)

        @pl.when(i == 0)
        def _():
            dw_ref[...] = jnp.zeros_like(dw_ref)
            dsc_ref[...] = jnp.zeros_like(dsc_ref)

        nxt_ok = (i < nt - 1).astype(F32)
        for g, w in enumerate(POOL_WINDOWS):
            lanes = pl.ds(g * POOL_CH, POOL_CH)
            cols = slice(g * POOL_CH, (g + 1) * POOL_CH)
            sc = sc_ref[:, cols]
            wg = w_ref[g]
            c = _pool_centered(zbuf, g, w, i, tm).astype(BF16)
            ypre = _dot_nn(c, wg)
            dpg = dp_ref[:, cols].astype(F32)
            dsc_ref[:, cols] += jnp.sum(dpg * ypre, axis=0, keepdims=True)
            dyb = (dpg * sc).astype(BF16)
            dw_ref[g] += _dot_tn(c, dyb)
            dd = _dot_nt(dyb, wg)
            dyn = (dpn_ref[:, cols].astype(F32) * sc).astype(BF16)
            ddn = _dot_nt(dyn, wg) * nxt_ok
            dbuf[pl.ds(0, tm), lanes] = dd / _pool_counts(i * tm, tm, w)
            dbuf[pl.ds(tm, POOL_HALO), lanes] = ddn / _pool_counts((i + 1) * tm, POOL_HALO, w)
            acc = -dd
            for s in range(w):
                acc = acc + dbuf[pl.ds(s, tm), lanes]
            dz_ref[:, cols] = acc

    return pl.pallas_call(
        body, name="pool_bwd", grid=(nt,),
        in_specs=[pl.BlockSpec((tm, pw), lambda i: (i, 0)),
                  pl.BlockSpec((POOL_HALO, pw), lambda i: (jnp.minimum((i + 1) * hb, t // POOL_HALO - 1), 0)),
                  pl.BlockSpec((tm, pw), lambda i: (i, 1)),
                  pl.BlockSpec((POOL_HALO, pw), lambda i: (jnp.maximum(i * hb - 1, 0), 1)),
                  pl.BlockSpec(pool_w.shape, lambda i: (0, 0, 0)),
                  pl.BlockSpec((1, pw), lambda i: (0, 0))],
        out_specs=[pl.BlockSpec((tm, pw), lambda i: (i, 0)),
                   pl.BlockSpec((ng, POOL_CH, POOL_CH), lambda i: (0, 0, 0)),
                   pl.BlockSpec((1, pw), lambda i: (0, 0))],
        out_shape=[jax.ShapeDtypeStruct((t, pw), F32), jax.ShapeDtypeStruct((ng, POOL_CH, POOL_CH), F32),
                   jax.ShapeDtypeStruct((1, pw), F32)],
        scratch_shapes=[pltpu.VMEM((POOL_HALO + tm, pw), F32), pltpu.VMEM((tm + POOL_HALO, pw), F32)],
        compiler_params=_cparams(),
    )(dp, dp, z, z, pool_w, pool_scale)


def _mla_bwd(dq_h, dk_h, dv_h, z, dz_pool, h1, dh2, mix_norm, w_in, q_norm, wq_t, kv_norm, wkv, pos, rope_tab):
    t, d = h1.shape
    tm = min(256, t)

    def body(dqh_ref, dkh_ref, dvh_ref, z_ref, dzp_ref, h_ref, dh2_ref, gm_ref, win_ref, gq_ref, wq_ref, gkv_ref,
             wkv_ref, pos_ref, tab_ref, dh1_ref, dq_ref, dkv_ref, dz_ref, dgq_ref, dgkv_ref, dgm_ref):
        i = pl.program_id(0)
        first = i == 0
        cos_t, sin_t = _rope_tables(pos_ref[...], tab_ref[...])
        dq_parts, dkv_parts = [], []
        dk_pe = jnp.zeros((tm, 128), F32)
        for hh in range(MLA_HEADS):
            dqh = dqh_ref[hh]
            dq_parts += [dqh[:, 0:NOPE], _rope_apply_t(dqh[:, NOPE:HEAD_PAD], cos_t, sin_t)]
            dkh = dkh_ref[hh]
            dkv_parts += [dkh[:, 0:NOPE], dvh_ref[hh]]
            dk_pe = dk_pe + dkh[:, NOPE:HEAD_PAD]
        dqb = jnp.concatenate(dq_parts, axis=-1).astype(BF16)
        dkvb = jnp.concatenate(dkv_parts, axis=-1).astype(BF16)
        dq_ref[...] = dqb
        dkv_ref[...] = dkvb
        z = z_ref[...]
        c_q = z[:, 0:Q_RANK]
        gq = gq_ref[...]
        _, rq = _rms_fwd(c_q, gq)
        dcq, dgq = _rms_bwd(_dot_nn(dqb, wq_ref[...]), c_q, gq, rq)
        c_kv = z[:, Q_RANK:Q_RANK + KV_RANK]
        gkv = gkv_ref[...]
        _, rkv = _rms_fwd(c_kv, gkv)
        dckv, dgkv = _rms_bwd(_dot_nt(dkvb, wkv_ref[...]), c_kv, gkv, rkv)
        dkr = _rope_apply_t(dk_pe, cos_t, sin_t)
        dzb = jnp.concatenate([dcq, dckv, dkr, dzp_ref[...]], axis=-1).astype(BF16)
        dz_ref[...] = dzb
        x = h_ref[...]
        gm = gm_ref[...]
        _, rm = _rms_fwd(x, gm)
        dx, dgm = _rms_bwd(_dot_nt(dzb, win_ref[...]), x, gm, rm)
        dh1_ref[...] = dh2_ref[...] + dx
        _accumulate(dgq_ref, dgq, first)
        _accumulate(dgkv_ref, dgkv, first)
        _accumulate(dgm_ref, dgm, first)

    full = lambda shape: pl.BlockSpec(shape, lambda i: (0,) * len(shape))
    row = lambda w: pl.BlockSpec((tm, w), lambda i: (i, 0))
    head = lambda w: pl.BlockSpec((MLA_HEADS, tm, w), lambda i: (0, i, 0))
    pw = len(POOL_WINDOWS) * POOL_CH
    return pl.pallas_call(
        body, name="mla_bwd", grid=(t // tm,),
        in_specs=[head(HEAD_PAD), head(HEAD_PAD), head(V_DIM), row(d), row(pw), row(d), row(d),
                  full((1, d)), full(w_in.shape), full((1, Q_RANK)), full(wq_t.shape), full((1, KV_RANK)),
                  full(wkv.shape), row(1), full(rope_tab.shape)],
        out_specs=[row(d), row(d), row(d), row(d), full((1, Q_RANK)), full((1, KV_RANK)), full((1, d))],
        out_shape=[jax.ShapeDtypeStruct((t, d), F32), jax.ShapeDtypeStruct((t, d), BF16),
                   jax.ShapeDtypeStruct((t, d), BF16), jax.ShapeDtypeStruct((t, d), BF16),
                   jax.ShapeDtypeStruct((1, Q_RANK), F32), jax.ShapeDtypeStruct((1, KV_RANK), F32),
                   jax.ShapeDtypeStruct((1, d), F32)],
        compiler_params=_cparams(),
    )(dq_h, dk_h, dv_h, z, dz_pool, h1, dh2, mix_norm, w_in, q_norm, wq_t, kv_norm, wkv, pos, rope_tab)


def _mem_kv(mem, mem_norm, wmkv):
    n, d = mem.shape

    def body(mem_ref, g_ref, w_ref, memn_ref, k_ref, v_ref):
        y, _ = _rms_fwd(mem_ref[...], g_ref[...])
        yb = y.astype(BF16)
        memn_ref[...] = yb
        for hh in range(MEM_HEADS):
            k_ref[hh] = _dot_nn(yb, w_ref[hh]).astype(BF16)
            v_ref[hh] = _dot_nn(yb, w_ref[MEM_HEADS + hh]).astype(BF16)

    return pl.pallas_call(
        body, name="mem_kv",
        out_shape=[jax.ShapeDtypeStruct((n, d), BF16), jax.ShapeDtypeStruct((MEM_HEADS, n, MEM_HD), BF16),
                   jax.ShapeDtypeStruct((MEM_HEADS, n, MEM_HD), BF16)],
        compiler_params=_cparams(),
    )(mem, mem_norm, wmkv)


def _mem_softmax(qb, km):
    s = _dot_nt(qb, km) * MEM_SCALE
    e = jnp.exp(s - jnp.max(s, axis=-1, keepdims=True))
    return e / jnp.sum(e, axis=-1, keepdims=True)


def _xattn_fwd(h1, a, p, w_out, g, wmq, km, vm, wmo):
    t, d = h1.shape
    tm = min(512, t)
    half = a.shape[1]

    def body(h_ref, a_ref, p_ref, wo_ref, g_ref, wmq_ref, km_ref, vm_ref, wmo_ref,
             h2_ref, h3_ref, hn_ref, q_ref, o_ref):
        h2 = h_ref[...] + _dot_nn(a_ref[...], wo_ref[0:half, :]) + _dot_nn(p_ref[...], wo_ref[half:2 * half, :])
        h2_ref[...] = h2
        hn, _ = _rms_fwd(h2, g_ref[...])
        hnb = hn.astype(BF16)
        hn_ref[...] = hnb
        qb = _dot_nn(hnb, wmq_ref[...]).astype(BF16)
        q_ref[...] = qb
        outs = []
        for hh in range(MEM_HEADS):
            pr = _mem_softmax(qb[:, hh * MEM_HD:(hh + 1) * MEM_HD], km_ref[hh])
            outs.append(_dot_nn(pr.astype(BF16), vm_ref[hh]))
        ob = jnp.concatenate(outs, axis=-1).astype(BF16)
        o_ref[...] = ob
        h3_ref[...] = h2 + _dot_nn(ob, wmo_ref[...])

    full = lambda shape: pl.BlockSpec(shape, lambda i: (0,) * len(shape))
    row = lambda w: pl.BlockSpec((tm, w), lambda i: (i, 0))
    return pl.pallas_call(
        body, name="xattn_fwd", grid=(t // tm,),
        in_specs=[row(d), row(half), row(half), full(w_out.shape), full((1, d)), full(wmq.shape),
                  full(km.shape), full(vm.shape), full(wmo.shape)],
        out_specs=[row(d), row(d), row(d), row(d), row(d)],
        out_shape=[jax.ShapeDtypeStruct((t, d), F32), jax.ShapeDtypeStruct((t, d), F32),
                   jax.ShapeDtypeStruct((t, d), BF16), jax.ShapeDtypeStruct((t, d), BF16),
                   jax.ShapeDtypeStruct((t, d), BF16)],
        compiler_params=_cparams(),
    )(h1, a, p, w_out, g, wmq, km, vm, wmo)


def _xattn_bwd(dh3, h2, qm, g, wmq, km, vm, wmo, w_out, token=None):
    t, d = h2.shape
    tm = min(256, t)
    half = d // 2

    def body(dh3_ref, h2_ref, q_ref, g_ref, wmq_ref, km_ref, vm_ref, wmo_ref, wo_ref,
             dh2_ref, dq_ref, da_ref, dp_ref, dk_ref, dv_ref, dg_ref):
        i = pl.program_id(0)
        first = i == 0

        @pl.when(first)
        def _():
            dk_ref[...] = jnp.zeros_like(dk_ref)
            dv_ref[...] = jnp.zeros_like(dv_ref)

        dh3 = dh3_ref[...]
        dob = _dot_nt(dh3.astype(BF16), wmo_ref[...]).astype(BF16)
        qb = q_ref[...]
        dq_parts = []
        for hh in range(MEM_HEADS):
            cols = slice(hh * MEM_HD, (hh + 1) * MEM_HD)
            kk, vv = km_ref[hh], vm_ref[hh]
            pr = _mem_softmax(qb[:, cols], kk)
            doh = dob[:, cols]
            dv_ref[hh] += _dot_tn(pr.astype(BF16), doh)
            dpp = _dot_nt(doh, vv)
            dsb = (pr * (dpp - jnp.sum(dpp * pr, axis=-1, keepdims=True)) * MEM_SCALE).astype(BF16)
            dq_parts.append(_dot_nn(dsb, kk))
            dk_ref[hh] += _dot_tn(dsb, qb[:, cols])
        dqb = jnp.concatenate(dq_parts, axis=-1).astype(BF16)
        dq_ref[...] = dqb
        x = h2_ref[...]
        gg = g_ref[...]
        _, r = _rms_fwd(x, gg)
        dx, dg = _rms_bwd(_dot_nt(dqb, wmq_ref[...]), x, gg, r)
        dh2 = dh3 + dx
        dh2_ref[...] = dh2
        dap = _dot_nt(dh2.astype(BF16), wo_ref[...])
        da_ref[...] = dap[:, 0:half].astype(BF16)
        dp_ref[...] = dap[:, half:d].astype(BF16)
        _accumulate(dg_ref, dg, first)

    full = lambda shape: pl.BlockSpec(shape, lambda i: (0,) * len(shape))
    row = lambda w: pl.BlockSpec((tm, w), lambda i: (i, 0))
    return _call_after(
        token, body,
        [row(d), row(d), row(d), full((1, d)), full(wmq.shape), full(km.shape), full(vm.shape),
         full(wmo.shape), full(w_out.shape)],
        (dh3, h2, qm, g, wmq, km, vm, wmo, w_out),
        name="xattn_bwd", grid=(t // tm,),
        out_specs=[row(d), row(d), row(half), row(half), full(km.shape), full(vm.shape), full((1, d))],
        out_shape=[jax.ShapeDtypeStruct((t, d), F32), jax.ShapeDtypeStruct((t, d), BF16),
                   jax.ShapeDtypeStruct((t, half), BF16), jax.ShapeDtypeStruct((t, half), BF16),
                   jax.ShapeDtypeStruct(km.shape, F32), jax.ShapeDtypeStruct(vm.shape, F32),
                   jax.ShapeDtypeStruct((1, d), F32)],
        compiler_params=_cparams(),
    )


def _mem_kv_bwd(dkm, dvm, memn, mem, mem_norm, wmkv):
    n, d = mem.shape

    def body(dk_ref, dv_ref, memn_ref, mem_ref, g_ref, w_ref, dw_ref, dg_ref):
        memn = memn_ref[...]
        dmemn = jnp.zeros((n, d), F32)
        for s in range(2 * MEM_HEADS):
            src = dk_ref[s] if s < MEM_HEADS else dv_ref[s - MEM_HEADS]
            db = src.astype(BF16)
            dw_ref[s] = _dot_tn(memn, db)
            dmemn = dmemn + _dot_nt(db, w_ref[s])
        x = mem_ref[...]
        gg = g_ref[...]
        _, r = _rms_fwd(x, gg)
        _, dg = _rms_bwd(dmemn, x, gg, r)
        dg_ref[...] = dg

    return pl.pallas_call(
        body, name="mem_kv_bwd",
        out_shape=[jax.ShapeDtypeStruct(wmkv.shape, F32), jax.ShapeDtypeStruct((1, d), F32)],
        compiler_params=_cparams(),
    )(dkm, dvm, memn, mem, mem_norm, wmkv)


MESH_ID = pl.DeviceIdType.MESH
ANY = pl.BlockSpec(memory_space=pl.ANY)


def _coords():
    return lax.axis_index("x"), lax.axis_index("y"), lax.axis_index("c")


def _other_chips(x, y):
    return [(1 - x, y), (x, 1 - y), (1 - x, 1 - y)]


def _all_gather_direct(shard, token):
    r, w = shard.shape

    def body(tok_ref, x_ref, out_ref, send_sems, recv_sems, local_sem):
        x, y, c = _coords()
        mine = pltpu.make_async_copy(x_ref, out_ref.at[4 * x + 2 * y + c], local_sem)
        mine.start()
        arrivals = []
        for k in range(1, N_DEV):
            px, py, pc = x ^ ((k >> 2) & 1), y ^ ((k >> 1) & 1), c ^ (k & 1)
            pltpu.make_async_remote_copy(
                src_ref=x_ref, dst_ref=out_ref.at[4 * x + 2 * y + c],
                send_sem=send_sems.at[k - 1], recv_sem=recv_sems.at[k - 1],
                device_id=(px, py, pc), device_id_type=MESH_ID).start()
            arrivals.append(pltpu.make_async_remote_copy(
                src_ref=x_ref, dst_ref=out_ref.at[4 * px + 2 * py + pc],
                send_sem=send_sems.at[k - 1], recv_sem=recv_sems.at[k - 1],
                device_id=(px, py, pc), device_id_type=MESH_ID))
        for cp in arrivals:
            cp.wait_recv()
        for cp in arrivals:
            cp.wait_send()
        mine.wait()

    return pl.pallas_call(
        body, name="all_gather_direct",
        out_shape=jax.ShapeDtypeStruct((N_DEV, r, w), shard.dtype),
        in_specs=[ANY, pl.BlockSpec(memory_space=pltpu.VMEM)], out_specs=pl.BlockSpec(memory_space=pltpu.VMEM),
        scratch_shapes=[pltpu.SemaphoreType.DMA((7,)), pltpu.SemaphoreType.DMA((7,)), pltpu.SemaphoreType.DMA],
        compiler_params=_cparams(),
    )(token, shard)


def _core_reduce(g, tag):
    _, r, w = g.shape

    def body(g_ref, part_ref, own_sc, recv_sc, send_sems, recv_sems, local_sems):
        x, y, c = _coords()
        sent, local = [], []
        for chip in range(4):
            sent.append(pltpu.make_async_remote_copy(
                src_ref=g_ref.at[2 * chip + (1 - c)], dst_ref=recv_sc.at[chip],
                send_sem=send_sems.at[chip], recv_sem=recv_sems.at[chip],
                device_id=(x, y, 1 - c), device_id_type=MESH_ID))
            local.append(pltpu.make_async_copy(g_ref.at[2 * chip + c], own_sc.at[chip], local_sems.at[chip]))
        for cp in sent + local:
            cp.start()
        for chip in range(4):
            local[chip].wait()
            sent[chip].wait_recv()
            part_ref[chip] = (own_sc[chip].astype(F32) + recv_sc[chip].astype(F32)).astype(part_ref.dtype)
        for cp in sent:
            cp.wait_send()

    return pl.pallas_call(
        body, name="core_reduce_" + tag,
        out_shape=jax.ShapeDtypeStruct((4, r, w), g.dtype),
        in_specs=[ANY], out_specs=pl.BlockSpec(memory_space=pltpu.VMEM),
        scratch_shapes=[pltpu.VMEM((4, r, w), g.dtype), pltpu.VMEM((4, r, w), g.dtype),
                        pltpu.SemaphoreType.DMA((4,)), pltpu.SemaphoreType.DMA((4,)), pltpu.SemaphoreType.DMA((4,))],
        compiler_params=_cparams(),
    )(g)


HBM_SPEC = pl.BlockSpec(memory_space=pltpu.HBM)
SEM_SPEC = pl.BlockSpec(memory_space=pltpu.SEMAPHORE)
SPLIT_EFFECT = pltpu.SideEffectType.DATAFLOW_SIDE_EFFECTING


def _ici_block(src_ref, px, py):
    return src_ref if len(src_ref.shape) == 2 else src_ref.at[2 * px + py]


def _ici_slot(gather, j, px, py, c):
    return 4 * px + 2 * py + c if gather else j


def _ici_start(src, after, name):
    r, w = src.shape[-2:]
    gather = len(src.shape) == 2
    land_shape = (N_DEV if gather else 3, r, w)

    def body(src_ref, land_ref, after_ref, send_sems, recv_sems, src_thru, land_thru, token):
        x, y, c = _coords()
        for j, (px, py) in enumerate(_other_chips(x, y)):
            pltpu.make_async_remote_copy(
                src_ref=_ici_block(src_ref, px, py), dst_ref=land_ref.at[_ici_slot(gather, j, x, y, c)],
                send_sem=send_sems.at[j], recv_sem=recv_sems.at[j],
                device_id=(px, py, c), device_id_type=MESH_ID).start()
        token[...] = jnp.zeros_like(token)

    return pl.pallas_call(
        body, name=name,
        out_shape=(pltpu.SemaphoreType.DMA((3,)), pltpu.SemaphoreType.DMA((3,)), pltpu.HBM(src.shape, src.dtype),
                   pltpu.HBM(land_shape, src.dtype), jax.ShapeDtypeStruct((8, 128), F32)),
        in_specs=(HBM_SPEC, HBM_SPEC, ANY),
        out_specs=(SEM_SPEC, SEM_SPEC, HBM_SPEC, HBM_SPEC, pl.BlockSpec(memory_space=pltpu.VMEM)),
        input_output_aliases={0: 2, 1: 3},
        compiler_params=pltpu.CompilerParams(has_side_effects=SPLIT_EFFECT),
    )(pltpu.with_memory_space_constraint(src, pltpu.HBM),
      pltpu.with_memory_space_constraint(lax.empty(land_shape, src.dtype), pltpu.HBM), after)


def _ici_wait(started, after, name):
    send_sems, recv_sems, src_thru, land_thru, _ = started
    gather = len(src_thru.shape) == 2

    def body(src_ref, land_ref, send_sems, recv_sems, after_ref, src_dead, got_ref):
        x, y, c = _coords()
        for j, (px, py) in enumerate(_other_chips(x, y)):
            copy = pltpu.make_async_remote_copy(
                src_ref=_ici_block(src_ref, px, py), dst_ref=land_ref.at[_ici_slot(gather, j, px, py, c)],
                send_sem=send_sems.at[j], recv_sem=recv_sems.at[j],
                device_id=(px, py, c), device_id_type=MESH_ID)
            copy.wait_send()
            copy.wait_recv()

    return pl.pallas_call(
        body, name=name,
        out_shape=(pltpu.HBM(src_thru.shape, src_thru.dtype), pltpu.HBM(land_thru.shape, land_thru.dtype)),
        in_specs=(HBM_SPEC, HBM_SPEC, SEM_SPEC, SEM_SPEC, ANY),
        out_specs=(HBM_SPEC, HBM_SPEC), input_output_aliases={0: 0, 1: 1},
        compiler_params=pltpu.CompilerParams(has_side_effects=SPLIT_EFFECT),
    )(src_thru, land_thru, send_sems, recv_sems, after)


def _core_share(own, gathered, name):
    r, w = own.shape

    def body(own_ref, gin_ref, out_ref, stage, send_sems, recv_sems, local_sem):
        x, y, c = _coords()
        sibling = (x, y, 1 - c)
        chips = [(x, y)] + _other_chips(x, y)
        stage_in = pltpu.make_async_copy(own_ref, stage, local_sem)
        stage_in.start()
        sent, arriving = [], []
        for k, (px, py) in enumerate(chips):
            rows = out_ref.at[4 * px + 2 * py + c]
            sent.append(pltpu.make_async_remote_copy(
                src_ref=own_ref if k == 0 else rows, dst_ref=rows,
                send_sem=send_sems.at[k], recv_sem=recv_sems.at[k], device_id=sibling, device_id_type=MESH_ID))
            arriving.append(pltpu.make_async_remote_copy(
                src_ref=own_ref, dst_ref=out_ref.at[4 * px + 2 * py + (1 - c)],
                send_sem=send_sems.at[k], recv_sem=recv_sems.at[k], device_id=sibling, device_id_type=MESH_ID))
        for cp in sent:
            cp.start()
        stage_in.wait()
        stage_out = pltpu.make_async_copy(stage, out_ref.at[4 * x + 2 * y + c], local_sem)
        stage_out.start()
        for cp in arriving:
            cp.wait_recv()
        for cp in sent:
            cp.wait_send()
        stage_out.wait()

    return pl.pallas_call(
        body, name=name,
        out_shape=jax.ShapeDtypeStruct((N_DEV, r, w), own.dtype),
        in_specs=[ANY, ANY], out_specs=ANY, input_output_aliases={1: 0},
        scratch_shapes=[pltpu.VMEM((r, w), own.dtype), pltpu.SemaphoreType.DMA((4,)),
                        pltpu.SemaphoreType.DMA((4,)), pltpu.SemaphoreType.DMA],
    )(own, gathered)


def _adamw(w, g, m, v):
    m = ADAM_B1 * m + (1.0 - ADAM_B1) * g
    v = ADAM_B2 * v + (1.0 - ADAM_B2) * (g * g)
    m_hat = m / ADAM_C1
    v_hat = v / ADAM_C2
    delta = -ADAM_LR * (m_hat / (jnp.sqrt(v_hat) + ADAM_EPS) + ADAM_WD * w)
    return delta, m, v


def _adam_big(part, land, w, m, v, chip_idx, tag, token):
    r, wd = w.shape
    tr, tw = _row_tile(r, 1024), 256

    def body(s_ref, tok_ref, p_ref, l_ref, w_ref, m_ref, v_ref, g_ref, d_ref, mo_ref, vo_ref):
        g = p_ref[0].astype(F32)
        for j in range(3):
            g = g + l_ref[j].astype(F32)
        delta, mn, vn = _adamw(w_ref[...], g, m_ref[...], v_ref[...])
        g_ref[...] = g
        d_ref[...] = delta
        mo_ref[...] = mn
        vo_ref[...] = vn

    row = pl.BlockSpec((tr, tw), lambda i, j, s: (i, j))
    return pl.pallas_call(
        body, name="adam_big_" + tag,
        grid_spec=pltpu.PrefetchScalarGridSpec(
            num_scalar_prefetch=1, grid=(r // tr, wd // tw),
            in_specs=[pl.BlockSpec((8, 128), lambda i, j, s: (0, 0)),
                      pl.BlockSpec((1, tr, tw), lambda i, j, s: (s[0], i, j)),
                      pl.BlockSpec((3, tr, tw), lambda i, j, s: (0, i, j)), row, row, row],
            out_specs=[row, row, row, row]),
        out_shape=[jax.ShapeDtypeStruct((r, wd), F32)] * 4,
        compiler_params=_cparams(),
    )(chip_idx, token, part, land, w, m, v)


def _adam_small(parts, w, m, v):
    _, r, wd = parts.shape

    def body(p_ref, w_ref, m_ref, v_ref, g_ref, d_ref, mo_ref, vo_ref):
        g = p_ref[0]
        for k in range(1, N_DEV):
            g = g + p_ref[k]
        delta, mn, vn = _adamw(w_ref[...], g, m_ref[...], v_ref[...])
        g_ref[...] = g
        d_ref[...] = delta
        mo_ref[...] = mn
        vo_ref[...] = vn

    return pl.pallas_call(
        body, name="adam_small",
        out_shape=[jax.ShapeDtypeStruct((r, wd), F32)] * 4,
        compiler_params=_cparams(),
    )(parts, w, m, v)


def _pad_rows(a, rows):
    return jnp.pad(a, ((0, rows - a.shape[0]), (0, 0)))


def _pad_w_in(w):
    cut = Q_RANK + KV_RANK + ROPE
    return jnp.concatenate([w[:, :cut], jnp.zeros((w.shape[0], 64), w.dtype), w[:, cut:]], axis=1)


def _unpad_w_in(w):
    cut = Q_RANK + KV_RANK + ROPE
    return jnp.concatenate([w[:, :cut], w[:, cut + 64:]], axis=1)


def _pack_big(p, group):
    if group == "mid":
        parts = [_pad_w_in(p["w_in"][0]), p["w_out"][0], p["w_mq"][0], p["w_mo"][0],
                 p["w_mkv"][0].reshape(256, D_MODEL),
                 _pad_rows(p["w_q_up"][0].T.reshape(24, D_MODEL), 32),
                 p["w_kv_up"][0].reshape(16, D_MODEL)]
    else:
        parts = [p[group + "_w_gate"][0].T, p[group + "_w_up"][0].T, p[group + "_w_down"][0]]
    return jnp.concatenate(parts, axis=0)


UNIT_WEIGHT = {"ffn1_g": ("ffn1_w_gate", True), "ffn1_u": ("ffn1_w_up", True), "ffn1_d": ("ffn1_w_down", False),
               "ffn2_g": ("ffn2_w_gate", True), "ffn2_u": ("ffn2_w_up", True), "ffn2_d": ("ffn2_w_down", False)}


def _pack_unit(p, unit):
    if unit == "mid":
        return _pack_big(p, "mid")
    name, transposed = UNIT_WEIGHT[unit]
    return p[name][0].T if transposed else p[name][0]


def _unpack_unit(a, unit):
    if unit != "mid":
        name, transposed = UNIT_WEIGHT[unit]
        return {name: (a.T if transposed else a)[None]}
    seg = lambda n: a[SEG_OFF[n][0]:SEG_OFF[n][0] + SEG_OFF[n][1]]
    return {"w_in": _unpad_w_in(seg("w_in"))[None], "w_out": seg("w_out")[None], "w_mq": seg("w_mq")[None],
            "w_mo": seg("w_mo")[None], "w_mkv": seg("w_mkv").reshape(D_MODEL, 256)[None],
            "w_q_up": seg("w_q")[:24].reshape(96, Q_RANK).T[None],
            "w_kv_up": seg("w_kv").reshape(KV_RANK, 128)[None]}


def _unpack_gathered(full, group):
    seg = lambda n: full[:, SEG_OFF[n][0]:SEG_OFF[n][0] + SEG_OFF[n][1]]
    rows = lambda n: seg(n).reshape(-1, D_MODEL)
    if group != "mid":
        return {n: rows(n) for n, _ in GROUP_SEGS[group]}
    wq_t = seg("w_q")[:, :24].reshape(MLA_HEADS, NOPE + ROPE, Q_RANK)
    wq_t = jnp.pad(wq_t, ((0, 0), (0, HEAD_PAD - NOPE - ROPE), (0, 0))).reshape(MLA_HEADS * HEAD_PAD, Q_RANK)
    wkv = seg("w_kv").reshape(N_DEV, KV_RANK, 128).transpose(1, 0, 2).reshape(KV_RANK, N_DEV * 128)
    return {"w_in": rows("w_in"), "w_out": rows("w_out"), "w_mq": rows("w_mq"), "w_mo": rows("w_mo"),
            "w_mkv": seg("w_mkv").reshape(N_DEV, D_MODEL, 256), "w_q": wq_t, "w_kv": wkv}


def _pack_grads(gr):
    blk = lambda a: a.reshape(N_DEV, -1, D_MODEL)
    dwq = gr["w_q"].reshape(MLA_HEADS, HEAD_PAD, Q_RANK)[:, :NOPE + ROPE].reshape(N_DEV, 24, D_MODEL)
    dwq = jnp.pad(dwq, ((0, 0), (0, 8), (0, 0)))
    dwkv = gr["w_kv"].reshape(KV_RANK, N_DEV, 128).transpose(1, 0, 2).reshape(N_DEV, 16, D_MODEL)
    parts = [blk(gr["w_in"]), blk(gr["w_out"]), blk(gr["w_mq"]), blk(gr["w_mo"]),
             gr["w_mkv"].reshape(N_DEV, 256, D_MODEL), dwq, dwkv]
    return jnp.concatenate([a.astype(BF16) for a in parts], axis=1)


def _pack_small(vals):
    parts = []
    for n, r in SMALL_ROWS:
        parts.append(_pad_rows(vals[n].reshape(-1, 128), r) if n in vals else jnp.zeros((r, 128), F32))
    return jnp.concatenate(parts, axis=0)


def _unpack_small(a, shapes):
    out = {}
    for n, shape in shapes.items():
        o = SMALL_OFF[n][0]
        out[n] = a[o:o + int(np.prod(shape)) // 128].reshape(shape)
    return out


BIG_NAMES = ("ffn1_w_gate", "ffn1_w_up", "ffn1_w_down", "w_in", "w_q_up", "w_kv_up", "w_out", "w_mq", "w_mkv",
             "w_mo", "ffn2_w_gate", "ffn2_w_up", "ffn2_w_down")
SMALL_NAMES = ("ffn1_norm", "mix_norm", "q_norm", "kv_norm", "pool_w", "pool_scale", "xattn_norm", "mem_norm",
               "ffn2_norm", "final_norm")
WEIGHT_ORDER = ("ffn1_norm", "ffn1_w_gate", "ffn1_w_up", "ffn1_w_down", "mix_norm", "w_in", "q_norm", "w_q_up",
                "kv_norm", "w_kv_up", "pool_w", "pool_scale", "w_out", "xattn_norm", "mem_norm", "w_mq", "w_mkv",
                "w_mo", "ffn2_norm", "ffn2_w_gate", "ffn2_w_up", "ffn2_w_down", "final_norm")


def _rope_table():
    lane = np.arange(128)
    freqs = (1.0 / (ROPE_BASE ** (np.arange(0, ROPE, 2, dtype=np.float32) / ROPE))).astype(np.float32)
    tab = np.zeros((8, 128), np.float32)
    tab[0] = np.where(lane < ROPE, freqs[lane % (ROPE // 2)], 0.0)
    tab[1] = np.where(lane < ROPE // 2, -1.0, np.where(lane < ROPE, 1.0, 0.0))
    return jnp.asarray(tab)


def kernel(x, mem, positions, ffn1_norm, ffn1_w_gate, ffn1_w_up, ffn1_w_down, mix_norm, w_in, q_norm, w_q_up, kv_norm, w_kv_up, pool_w, pool_scale, w_out, xattn_norm, mem_norm, w_mq, w_mkv, w_mo, ffn2_norm, ffn2_w_gate, ffn2_w_up, ffn2_w_down, final_norm, loss_target, m_ffn1_norm, m_ffn1_w_gate, m_ffn1_w_up, m_ffn1_w_down, m_mix_norm, m_w_in, m_q_norm, m_w_q_up, m_kv_norm, m_w_kv_up, m_pool_w, m_pool_scale, m_w_out, m_xattn_norm, m_mem_norm, m_w_mq, m_w_mkv, m_w_mo, m_ffn2_norm, m_ffn2_w_gate, m_ffn2_w_up, m_ffn2_w_down, m_final_norm, v_ffn1_norm, v_ffn1_w_gate, v_ffn1_w_up, v_ffn1_w_down, v_mix_norm, v_w_in, v_q_norm, v_w_q_up, v_kv_norm, v_w_kv_up, v_pool_w, v_pool_scale, v_w_out, v_xattn_norm, v_mem_norm, v_w_mq, v_w_mkv, v_w_mo, v_ffn2_norm, v_ffn2_w_gate, v_ffn2_w_up, v_ffn2_w_down, v_final_norm):
    wts = dict(ffn1_norm=ffn1_norm, ffn1_w_gate=ffn1_w_gate, ffn1_w_up=ffn1_w_up, ffn1_w_down=ffn1_w_down,
               mix_norm=mix_norm, w_in=w_in, q_norm=q_norm, w_q_up=w_q_up, kv_norm=kv_norm, w_kv_up=w_kv_up,
               pool_w=pool_w, pool_scale=pool_scale, w_out=w_out, xattn_norm=xattn_norm, mem_norm=mem_norm,
               w_mq=w_mq, w_mkv=w_mkv, w_mo=w_mo, ffn2_norm=ffn2_norm, ffn2_w_gate=ffn2_w_gate,
               ffn2_w_up=ffn2_w_up, ffn2_w_down=ffn2_w_down, final_norm=final_norm)
    mom = dict(ffn1_norm=m_ffn1_norm, ffn1_w_gate=m_ffn1_w_gate, ffn1_w_up=m_ffn1_w_up, ffn1_w_down=m_ffn1_w_down,
               mix_norm=m_mix_norm, w_in=m_w_in, q_norm=m_q_norm, w_q_up=m_w_q_up, kv_norm=m_kv_norm,
               w_kv_up=m_w_kv_up, pool_w=m_pool_w, pool_scale=m_pool_scale, w_out=m_w_out, xattn_norm=m_xattn_norm,
               mem_norm=m_mem_norm, w_mq=m_w_mq, w_mkv=m_w_mkv, w_mo=m_w_mo, ffn2_norm=m_ffn2_norm,
               ffn2_w_gate=m_ffn2_w_gate, ffn2_w_up=m_ffn2_w_up, ffn2_w_down=m_ffn2_w_down, final_norm=m_final_norm)
    var = dict(ffn1_norm=v_ffn1_norm, ffn1_w_gate=v_ffn1_w_gate, ffn1_w_up=v_ffn1_w_up, ffn1_w_down=v_ffn1_w_down,
               mix_norm=v_mix_norm, w_in=v_w_in, q_norm=v_q_norm, w_q_up=v_w_q_up, kv_norm=v_kv_norm,
               w_kv_up=v_w_kv_up, pool_w=v_pool_w, pool_scale=v_pool_scale, w_out=v_w_out, xattn_norm=v_xattn_norm,
               mem_norm=v_mem_norm, w_mq=v_w_mq, w_mkv=v_w_mkv, w_mo=v_w_mo, ffn2_norm=v_ffn2_norm,
               ffn2_w_gate=v_ffn2_w_gate, ffn2_w_up=v_ffn2_w_up, ffn2_w_down=v_ffn2_w_down, final_norm=v_final_norm)

    t = x.shape[1]
    xs = x[0]
    mems = mem[0]
    target = loss_target[0]
    pos = positions.reshape(t, 1)
    row = lambda a: a.reshape(1, -1)
    rope_tab = _rope_table()

    cx, cy, cc = _coords()
    chip_idx = (2 * cx + cy).astype(jnp.int32).reshape(1)

    w_pack, wb = {}, {}
    for grp in ("ffn1", "mid", "ffn2"):
        w_pack[grp] = _pack_big(wts, grp)
        wb[grp] = w_pack[grp].astype(BF16)
        if grp == "ffn1":
            ag_ffn1 = _ici_start(wb["ffn1"], pos, "ag_ffn1_start")
    own_ffn1, land_ffn1 = _ici_wait(ag_ffn1, wb["ffn2"], "ag_ffn1_wait")
    full_ffn1 = _core_share(own_ffn1, land_ffn1, "ag_ffn1_share")
    fw = _unpack_gathered(full_ffn1, "ffn1")
    ag_mid = _ici_start(wb["mid"], full_ffn1, "ag_mid_start")
    g_ffn1, g_mix, g_q, g_kv = row(ffn1_norm), row(mix_norm), row(q_norm), row(kv_norm)
    g_x, g_mem, g_ffn2, g_fin = row(xattn_norm), row(mem_norm), row(ffn2_norm), row(final_norm)
    pool_wb = pool_w[0].astype(BF16)
    pool_sc = row(pool_scale)

    h1, n1, gate1, up1 = _ffn_fwd(xs, g_ffn1, fw["ffn1_g"], fw["ffn1_u"], fw["ffn1_d"], "ffn1_fwd", token=ag_mid[4])
    own_mid, land_mid = _ici_wait(ag_mid, h1, "ag_mid_wait")
    full_mid = _core_share(own_mid, land_mid, "ag_mid_share")
    fw.update(_unpack_gathered(full_mid, "mid"))
    ag_ffn2 = _ici_start(wb["ffn2"], full_mid, "ag_ffn2_start")
    u, z, qn, kvn, qh, kh, vh = _mix_prep(h1, g_mix, fw["w_in"], g_q, fw["w_q"], g_kv, fw["w_kv"], pos, rope_tab,
                                          token=ag_ffn2[4])
    a, lse = _attn_fwd(qh, kh, vh)
    p = _pool_fwd(z, pool_wb, pool_sc)
    memn, km, vm = _mem_kv(mems, g_mem, fw["w_mkv"])
    h2, h3, hn, qm, om = _xattn_fwd(h1, a, p, fw["w_out"], g_x, fw["w_mq"], km, vm, fw["w_mo"])
    own_ffn2, land_ffn2 = _ici_wait(ag_ffn2, h3, "ag_ffn2_wait")
    fw.update(_unpack_gathered(_core_share(own_ffn2, land_ffn2, "ag_ffn2_share"), "ffn2"))
    h4, n2, gate2, up2 = _ffn_fwd(h3, g_ffn2, fw["ffn2_g"], fw["ffn2_u"], fw["ffn2_d"], "ffn2_fwd")
    loss_part, dh4, dg_fin = _loss_head(h4, target, g_fin)

    def reduce_start(g8, unit):
        part = _core_reduce(g8, unit)
        return _ici_start(part, g8, "rs_" + unit + "_start")

    def by_device(g):
        return g.reshape(N_DEV, -1, D_MODEL)

    rs = {}
    dh3, dgate2, dup2, act2, dg_ffn2 = _ffn_bwd_data(dh4, h3, g_ffn2, gate2, up2, fw["ffn2_g"], fw["ffn2_u"],
                                                     fw["ffn2_d"], "ffn2_bwd")
    rs["ffn2_g"] = reduce_start(by_device(_tn_matmul(dgate2, n2, "ffn2_dwg", tmm=1408, out_dtype=BF16)), "ffn2_g")
    rs["ffn2_u"] = reduce_start(by_device(_tn_matmul(dup2, n2, "ffn2_dwu", tmm=1408, out_dtype=BF16,
                                                     token=rs["ffn2_g"][4])), "ffn2_u")
    rs["ffn2_d"] = reduce_start(by_device(_tn_matmul(act2, dh4, "ffn2_dwd", scale=0.5, tmm=1408, out_dtype=BF16,
                                                     token=rs["ffn2_u"][4])), "ffn2_d")
    dh2, dqm, da, dp, dkm, dvm, dg_x = _xattn_bwd(dh3, h2, qm, g_x, fw["w_mq"], km, vm, fw["w_mo"], fw["w_out"],
                                                  token=rs["ffn2_d"][4])
    gr = {}
    gr["w_mo"] = _tn_matmul(om, dh3, "dw_mo", tmm=512, out_dtype=BF16)
    gr["w_mq"] = _tn_matmul(hn, dqm, "dw_mq", tmm=512, out_dtype=BF16)
    gr["w_out"] = jnp.concatenate([_tn_matmul(a, dh2, "dw_out_a", out_dtype=BF16),
                                   _tn_matmul(p, dh2, "dw_out_p", out_dtype=BF16)], axis=0)
    gr["w_mkv"], dg_mem = _mem_kv_bwd(dkm, dvm, memn, mems, g_mem, fw["w_mkv"])
    dz_pool, d_pool_w, d_pool_sc = _pool_bwd(dp, z, pool_wb, pool_sc)
    dqh, dkh, dvh = _attn_bwd(qh, kh, vh, da, lse, _attn_delta(a, da))
    dh1, dq, dkv, dz, dg_q, dg_kv, dg_mix = _mla_bwd(dqh, dkh, dvh, z, dz_pool, h1, dh2, g_mix, fw["w_in"], g_q,
                                                     fw["w_q"], g_kv, fw["w_kv"], pos, rope_tab)
    gr["w_q"] = _tn_matmul(dq, qn, "dw_q", tmm=512, out_dtype=BF16)
    gr["w_kv"] = _tn_matmul(kvn, dkv, "dw_kv", out_dtype=BF16)
    gr["w_in"] = _tn_matmul(u, dz, "dw_in", tmm=512, out_dtype=BF16)
    g_mid = _pack_grads(gr)
    part_mid = _core_reduce(g_mid, "mid")
    got = {}
    after = part_mid
    for unit in ("ffn2_g", "ffn2_u", "ffn2_d"):
        got[unit] = _ici_wait(rs[unit], after, "rs_" + unit + "_wait")
        after = got[unit][1]
    rs["mid"] = _ici_start(part_mid, after, "rs_mid_start")
    dx, dgate1, dup1, act1, dg_ffn1 = _ffn_bwd_data(dh1, xs, g_ffn1, gate1, up1, fw["ffn1_g"], fw["ffn1_u"],
                                                    fw["ffn1_d"], "ffn1_bwd", token=rs["mid"][4])
    got["mid"] = _ici_wait(rs["mid"], dx, "rs_mid_wait")

    small_g = dict(ffn1_norm=dg_ffn1, mix_norm=dg_mix, q_norm=dg_q, kv_norm=dg_kv, pool_w=d_pool_w,
                   pool_scale=d_pool_sc, xattn_norm=dg_x, mem_norm=dg_mem, ffn2_norm=dg_ffn2, final_norm=dg_fin,
                   loss=loss_part)
    parts = _all_gather_direct(_pack_small(small_g), got["mid"][1])
    small = _adam_small(parts, _pack_small({n: wts[n] for n in SMALL_NAMES}),
                        _pack_small({n: mom[n] for n in SMALL_NAMES}), _pack_small({n: var[n] for n in SMALL_NAMES}))
    small_sum = small[0]
    loss = small_sum[SMALL_OFF["loss"][0], 0]
    shapes = {n: wts[n].shape for n in SMALL_NAMES}
    small = [_unpack_small(s, shapes) for s in small]

    rs["ffn1_g"] = reduce_start(by_device(_tn_matmul(dgate1, n1, "ffn1_dwg", tmm=1408, out_dtype=BF16,
                                                     token=small_sum)), "ffn1_g")
    rs["ffn1_u"] = reduce_start(by_device(_tn_matmul(dup1, n1, "ffn1_dwu", tmm=1408, out_dtype=BF16,
                                                     token=rs["ffn1_g"][4])), "ffn1_u")
    rs["ffn1_d"] = reduce_start(by_device(_tn_matmul(act1, dh1, "ffn1_dwd", scale=0.5, tmm=1408, out_dtype=BF16,
                                                     token=rs["ffn1_u"][4])), "ffn1_d")

    big = {}

    def adam_unit(unit, token):
        part, land = got[unit]
        res = _adam_big(part, land, _pack_unit(wts, unit), _pack_unit(mom, unit), _pack_unit(var, unit),
                        chip_idx, unit, token)
        for k, packed in enumerate(res):
            big.setdefault(k, {}).update(_unpack_unit(packed, unit))
        return res[0]

    done = rs["ffn1_d"][4]
    for unit in ("mid", "ffn2_g", "ffn2_u", "ffn2_d"):
        done = adam_unit(unit, done)
    for unit in ("ffn1_g", "ffn1_u", "ffn1_d"):
        got[unit] = _ici_wait(rs[unit], done, "rs_" + unit + "_wait")
        done = adam_unit(unit, done)

    outs = [loss, dx[None]]
    for k in range(4):
        for n in WEIGHT_ORDER:
            outs.append(big[k][n] if n in BIG_NAMES else small[k][n])
    return tuple(outs)
```

```python
import numpy as np

import jax
import jax.numpy as jnp
from jax import lax
from jax.experimental import pallas as pl
from jax.experimental.pallas import tpu as pltpu

F32 = jnp.float32
BF16 = jnp.bfloat16

N_DEV = 8
D_MODEL = 1024
D_FF = 2816
MLA_HEADS = 4
NOPE = 128
ROPE = 64
HEAD_PAD = 256
V_DIM = 128
Q_RANK = 256
KV_RANK = 128
POOL_WINDOWS = (2, 4, 8, 16)
POOL_CH = 128
POOL_HALO = 16
N_MEM = 256
MEM_HEADS = 4
MEM_HD = 256
ROPE_BASE = 10000.0
RMS_EPS = 1e-6
ATTN_SCALE = (NOPE + ROPE) ** -0.5
MEM_SCALE = MEM_HD ** -0.5
NEG_BIG = -1e30

ADAM_LR = 0.001
ADAM_B1 = 0.9
ADAM_B2 = 0.999
ADAM_EPS = 1e-08
ADAM_WD = 0.01
ADAM_STEP = 10
ADAM_C1 = 1.0 - ADAM_B1 ** ADAM_STEP
ADAM_C2 = 1.0 - ADAM_B2 ** ADAM_STEP

VMEM_LIMIT_BYTES = 56 * 1024 * 1024
BF16_ROWS = 16

GROUP_SEGS = {
    "ffn1": (("ffn1_g", 352), ("ffn1_u", 352), ("ffn1_d", 352)),
    "mid": (("w_in", 128), ("w_out", 128), ("w_mq", 128), ("w_mo", 128), ("w_mkv", 256), ("w_q", 32), ("w_kv", 16)),
    "ffn2": (("ffn2_g", 352), ("ffn2_u", 352), ("ffn2_d", 352)),
}
SEG_OFF = {}
GROUP_ROWS = {}
for _g, _segs in GROUP_SEGS.items():
    _o = 0
    for _n, _r in _segs:
        SEG_OFF[_n] = (_o, _r)
        _o += _r
    GROUP_ROWS[_g] = _o

SMALL_ROWS = (("ffn1_norm", 8), ("mix_norm", 8), ("q_norm", 8), ("kv_norm", 8), ("pool_w", 512), ("pool_scale", 8),
              ("xattn_norm", 8), ("mem_norm", 8), ("ffn2_norm", 8), ("final_norm", 8), ("loss", 8))
SMALL_OFF = {}
_o = 0
for _n, _r in SMALL_ROWS:
    SMALL_OFF[_n] = (_o, _r)
    _o += _r


def _cparams(**kw):
    return pltpu.CompilerParams(vmem_limit_bytes=VMEM_LIMIT_BYTES, **kw)


def _row_tile(rows, limit):
    best = None
    for cand in range(BF16_ROWS, min(rows, limit) + 1, BF16_ROWS):
        if rows % cand == 0:
            best = cand
    assert best is not None, rows
    return best


def _dot_nn(a, b):
    return lax.dot_general(a, b, (((1,), (0,)), ((), ())), preferred_element_type=F32)


def _dot_nt(a, b):
    return lax.dot_general(a, b, (((1,), (1,)), ((), ())), preferred_element_type=F32)


def _dot_tn(a, b):
    return lax.dot_general(a, b, (((0,), (0,)), ((), ())), preferred_element_type=F32)


def _rms_fwd(x, g):
    r = lax.rsqrt(jnp.mean(x * x, axis=-1, keepdims=True) + RMS_EPS)
    return x * r * g, r


def _rms_bwd(dy, x, g, r):
    xhat = x * r
    dyg = dy * g
    dx = r * (dyg - xhat * jnp.mean(dyg * xhat, axis=-1, keepdims=True))
    dg = jnp.sum(dy * xhat, axis=0, keepdims=True)
    return dx, dg


def _accumulate(ref, val, first):
    if isinstance(first, bool):
        if first:
            ref[...] = val
        else:
            ref[...] += val
        return

    @pl.when(first)
    def _():
        ref[...] = val

    @pl.when(jnp.logical_not(first))
    def _():
        ref[...] += val


def _call_after(token, body, in_specs, args, **kw):
    if token is not None:
        inner = body
        body = lambda tok_ref, *refs: inner(*refs)
        in_specs = [pl.BlockSpec((8, 128), lambda *_: (0, 0))] + list(in_specs)
        args = (token,) + tuple(args)
    return pl.pallas_call(body, in_specs=in_specs, **kw)(*args)


def _resident(shape):
    return pl.BlockSpec(shape, lambda *_: (0,) * len(shape), pipeline_mode=pl.Buffered(1))


def _rope_tables(pos_col, tab):
    ang = pos_col.astype(F32) * tab[0:1, :]
    return jnp.cos(ang), jnp.sin(ang) * tab[1:2, :]


def _swap_halves(x):
    lane = lax.broadcasted_iota(jnp.int32, x.shape, 1)
    return jnp.where((lane % 64) < 32, pltpu.roll(x, 96, 1), pltpu.roll(x, 32, 1))


def _rope_apply(x, cos_t, sin_t):
    return x * cos_t + _swap_halves(x) * sin_t


def _rope_apply_t(dy, cos_t, sin_t):
    return dy * cos_t + _swap_halves(dy * sin_t)


def _ffn_fwd(h, g, wg_t, wu_t, wd, name, token=None):
    t, d = h.shape
    f = wg_t.shape[0]
    tm, tf = min(1024, t), 256
    parts = 2 if tm % 512 == 0 else 1
    tp = tm // parts
    nf = f // tf

    def body(h_ref, g_ref, wg_ref, wu_ref, wd_ref, ho_ref, n_ref, gate_ref, up_ref, nb_sc, acc_sc):
        j = pl.program_id(1)

        @pl.when(j == 0)
        def _():
            y, _ = _rms_fwd(h_ref[...], g_ref[...])
            nb = y.astype(BF16)
            nb_sc[...] = nb
            n_ref[...] = nb
            acc_sc[...] = jnp.zeros_like(acc_sc)

        for r in range(parts):
            rows = pl.ds(r * tp, tp)
            nb = nb_sc[rows, :]
            gt = _dot_nt(nb, wg_ref[...])
            ut = _dot_nt(nb, wu_ref[...])
            gate_ref[rows, :] = gt.astype(BF16)
            up_ref[rows, :] = ut.astype(BF16)
            act = (gt * jax.nn.sigmoid(gt)) * ut
            acc_sc[rows, :] += _dot_nn(act.astype(BF16), wd_ref[...])

        @pl.when(j == nf - 1)
        def _():
            ho_ref[...] = h_ref[...] + 0.5 * acc_sc[...]

    return _call_after(
        token, body,
        [pl.BlockSpec((tm, d), lambda i, j: (i, 0)),
         pl.BlockSpec((1, d), lambda i, j: (0, 0)),
         pl.BlockSpec((tf, d), lambda i, j: (j, 0)),
         pl.BlockSpec((tf, d), lambda i, j: (j, 0)),
         pl.BlockSpec((tf, d), lambda i, j: (j, 0))],
        (h, g, wg_t, wu_t, wd),
        name=name, grid=(t // tm, nf),
        out_specs=[pl.BlockSpec((tm, d), lambda i, j: (i, 0)),
                   pl.BlockSpec((tm, d), lambda i, j: (i, 0)),
                   pl.BlockSpec((tm, tf), lambda i, j: (i, j)),
                   pl.BlockSpec((tm, tf), lambda i, j: (i, j))],
        out_shape=[jax.ShapeDtypeStruct((t, d), F32), jax.ShapeDtypeStruct((t, d), BF16),
                   jax.ShapeDtypeStruct((t, f), BF16), jax.ShapeDtypeStruct((t, f), BF16)],
        scratch_shapes=[pltpu.VMEM((tm, d), BF16), pltpu.VMEM((tm, d), F32)],
        compiler_params=_cparams(),
    )


def _ffn_bwd_data(dho, h, g, gate, up, wg_t, wu_t, wd, name, token=None):
    t, d = h.shape
    f = wg_t.shape[0]
    tm, tf = min(1024, t), 256
    parts = 2 if tm % 512 == 0 else 1
    tp = tm // parts
    nf = f // tf

    def body(dho_ref, h_ref, g_ref, gate_ref, up_ref, wg_ref, wu_ref, wd_ref,
             dh_ref, dgate_ref, dup_ref, act_ref, dg_ref, dhb_sc, acc_sc):
        i, j = pl.program_id(0), pl.program_id(1)

        @pl.when(j == 0)
        def _():
            dhb_sc[...] = (0.5 * dho_ref[...]).astype(BF16)
            acc_sc[...] = jnp.zeros_like(acc_sc)

        for r in range(parts):
            rows = pl.ds(r * tp, tp)
            dact = _dot_nt(dhb_sc[rows, :], wd_ref[...])
            gt = gate_ref[rows, :].astype(F32)
            ut = up_ref[rows, :].astype(F32)
            sg = jax.nn.sigmoid(gt)
            silu = gt * sg
            dgb = (dact * ut * (sg * (1.0 + gt * (1.0 - sg)))).astype(BF16)
            dub = (dact * silu).astype(BF16)
            act_ref[rows, :] = (silu * ut).astype(BF16)
            dgate_ref[rows, :] = dgb
            dup_ref[rows, :] = dub
            acc_sc[rows, :] += _dot_nn(dgb, wg_ref[...]) + _dot_nn(dub, wu_ref[...])

        @pl.when(j == nf - 1)
        def _():
            x = h_ref[...]
            gg = g_ref[...]
            _, r = _rms_fwd(x, gg)
            dx, dg = _rms_bwd(acc_sc[...], x, gg, r)
            dh_ref[...] = dho_ref[...] + dx
            _accumulate(dg_ref, dg, i == 0)

    return _call_after(
        token, body,
        [pl.BlockSpec((tm, d), lambda i, j: (i, 0)),
         pl.BlockSpec((tm, d), lambda i, j: (i, 0)),
         pl.BlockSpec((1, d), lambda i, j: (0, 0)),
         pl.BlockSpec((tm, tf), lambda i, j: (i, j)),
         pl.BlockSpec((tm, tf), lambda i, j: (i, j)),
         pl.BlockSpec((tf, d), lambda i, j: (j, 0)),
         pl.BlockSpec((tf, d), lambda i, j: (j, 0)),
         pl.BlockSpec((tf, d), lambda i, j: (j, 0))],
        (dho, h, g, gate, up, wg_t, wu_t, wd),
        name=name, grid=(t // tm, nf),
        out_specs=[pl.BlockSpec((tm, d), lambda i, j: (i, 0)),
                   pl.BlockSpec((tm, tf), lambda i, j: (i, j)),
                   pl.BlockSpec((tm, tf), lambda i, j: (i, j)),
                   pl.BlockSpec((tm, tf), lambda i, j: (i, j)),
                   pl.BlockSpec((1, d), lambda i, j: (0, 0))],
        out_shape=[jax.ShapeDtypeStruct((t, d), F32), jax.ShapeDtypeStruct((t, f), BF16),
                   jax.ShapeDtypeStruct((t, f), BF16), jax.ShapeDtypeStruct((t, f), BF16),
                   jax.ShapeDtypeStruct((1, d), F32)],
        scratch_shapes=[pltpu.VMEM((tm, d), BF16), pltpu.VMEM((tm, d), F32)],
        compiler_params=_cparams(),
    )


def _tn_matmul(a, b, name, scale=1.0, tmm=None, out_dtype=F32, token=None):
    t, m = a.shape
    n = b.shape[1]
    tmm = m if tmm is None else tmm
    tk = min(1024, t)
    nk = t // tk

    def product(a_ref, b_ref):
        prod = _dot_tn(a_ref[...].astype(BF16), b_ref[...].astype(BF16))
        return prod * scale if scale != 1.0 else prod

    def body_f32(a_ref, b_ref, o_ref):
        _accumulate(o_ref, product(a_ref, b_ref), pl.program_id(1) == 0)

    def body_cast(a_ref, b_ref, o_ref, acc_sc):
        k = pl.program_id(1)
        _accumulate(acc_sc, product(a_ref, b_ref), k == 0)

        @pl.when(k == nk - 1)
        def _():
            o_ref[...] = acc_sc[...].astype(out_dtype)

    direct = out_dtype == F32
    return _call_after(
        token, body_f32 if direct else body_cast,
        [pl.BlockSpec((tk, tmm), lambda i, k: (k, i)),
         pl.BlockSpec((tk, n), lambda i, k: (k, 0))],
        (a, b),
        name=name, grid=(m // tmm, nk),
        out_specs=pl.BlockSpec((tmm, n), lambda i, k: (i, 0)),
        out_shape=jax.ShapeDtypeStruct((m, n), out_dtype),
        scratch_shapes=[] if direct else [pltpu.VMEM((tmm, n), F32)],
        compiler_params=_cparams(),
    )


def _loss_head(h, target, g):
    t, d = h.shape
    tm = min(512, t)

    def body(h_ref, t_ref, g_ref, loss_ref, dh_ref, dg_ref):
        i = pl.program_id(0)
        x = h_ref[...]
        gg = g_ref[...]
        y, r = _rms_fwd(x, gg)
        err = y - t_ref[...]
        part = 0.5 * jnp.sum(jnp.mean(err * err, axis=-1, keepdims=True), axis=0, keepdims=True)
        dx, dg = _rms_bwd(err * (1.0 / d), x, gg, r)
        dh_ref[...] = dx
        _accumulate(loss_ref, jnp.broadcast_to(part, loss_ref.shape), i == 0)
        _accumulate(dg_ref, dg, i == 0)

    return pl.pallas_call(
        body, name="loss_head", grid=(t // tm,),
        in_specs=[pl.BlockSpec((tm, d), lambda i: (i, 0)),
                  pl.BlockSpec((tm, d), lambda i: (i, 0)),
                  pl.BlockSpec((1, d), lambda i: (0, 0))],
        out_specs=[pl.BlockSpec((8, 128), lambda i: (0, 0)),
                   pl.BlockSpec((tm, d), lambda i: (i, 0)),
                   pl.BlockSpec((1, d), lambda i: (0, 0))],
        out_shape=[jax.ShapeDtypeStruct((8, 128), F32), jax.ShapeDtypeStruct((t, d), F32),
                   jax.ShapeDtypeStruct((1, d), F32)],
        compiler_params=_cparams(),
    )(h, target, g)


def _mix_prep(h1, mix_norm, w_in, q_norm, wq_t, kv_norm, wkv, pos, rope_tab, token=None):
    t, d = h1.shape
    tm = min(512, t)

    def body(h_ref, gm_ref, win_ref, gq_ref, wq_ref, gkv_ref, wkv_ref, pos_ref, tab_ref,
             u_ref, z_ref, qn_ref, kvn_ref, q_ref, k_ref, v_ref):
        u, _ = _rms_fwd(h_ref[...], gm_ref[...])
        ub = u.astype(BF16)
        u_ref[...] = ub
        z = _dot_nn(ub, win_ref[...])
        z_ref[...] = z
        cos_t, sin_t = _rope_tables(pos_ref[...], tab_ref[...])
        qn, _ = _rms_fwd(z[:, 0:Q_RANK], gq_ref[...])
        qnb = qn.astype(BF16)
        qn_ref[...] = qnb
        q = _dot_nt(qnb, wq_ref[...])
        kvn, _ = _rms_fwd(z[:, Q_RANK:Q_RANK + KV_RANK], gkv_ref[...])
        kvnb = kvn.astype(BF16)
        kvn_ref[...] = kvnb
        kv = _dot_nn(kvnb, wkv_ref[...])
        k_pe = _rope_apply(z[:, Q_RANK + KV_RANK:Q_RANK + KV_RANK + 128], cos_t, sin_t)
        ones = jnp.ones((tm, V_DIM), F32)
        for hh in range(MLA_HEADS):
            b = hh * HEAD_PAD
            q_pe = _rope_apply(q[:, b + NOPE:b + HEAD_PAD], cos_t, sin_t)
            q_ref[hh] = jnp.concatenate([q[:, b:b + NOPE], q_pe], axis=-1).astype(BF16)
            k_ref[hh] = jnp.concatenate([kv[:, b:b + NOPE], k_pe], axis=-1).astype(BF16)
            v_ref[hh] = jnp.concatenate([kv[:, b + NOPE:b + HEAD_PAD], ones], axis=-1).astype(BF16)

    full = lambda shape: pl.BlockSpec(shape, lambda i: (0,) * len(shape))
    return _call_after(
        token, body,
        [pl.BlockSpec((tm, d), lambda i: (i, 0)), _resident((1, d)), _resident(w_in.shape), _resident((1, Q_RANK)),
         _resident(wq_t.shape), _resident((1, KV_RANK)), _resident(wkv.shape),
         pl.BlockSpec((tm, 1), lambda i: (i, 0)), _resident(rope_tab.shape)],
        (h1, mix_norm, w_in, q_norm, wq_t, kv_norm, wkv, pos, rope_tab),
        name="mix_prep", grid=(t // tm,),
        out_specs=[pl.BlockSpec((tm, d), lambda i: (i, 0)),
                   pl.BlockSpec((tm, d), lambda i: (i, 0)),
                   pl.BlockSpec((tm, Q_RANK), lambda i: (i, 0)),
                   pl.BlockSpec((tm, KV_RANK), lambda i: (i, 0)),
                   pl.BlockSpec((MLA_HEADS, tm, HEAD_PAD), lambda i: (0, i, 0)),
                   pl.BlockSpec((MLA_HEADS, tm, HEAD_PAD), lambda i: (0, i, 0)),
                   pl.BlockSpec((MLA_HEADS, tm, 2 * V_DIM), lambda i: (0, i, 0))],
        out_shape=[jax.ShapeDtypeStruct((t, d), BF16), jax.ShapeDtypeStruct((t, d), F32),
                   jax.ShapeDtypeStruct((t, Q_RANK), BF16), jax.ShapeDtypeStruct((t, KV_RANK), BF16),
                   jax.ShapeDtypeStruct((MLA_HEADS, t, HEAD_PAD), BF16),
                   jax.ShapeDtypeStruct((MLA_HEADS, t, HEAD_PAD), BF16),
                   jax.ShapeDtypeStruct((MLA_HEADS, t, 2 * V_DIM), BF16)],
        compiler_params=_cparams(),
    )


def _causal_mask(s):
    row = lax.broadcasted_iota(jnp.int32, s.shape, 0)
    col = lax.broadcasted_iota(jnp.int32, s.shape, 1)
    return jnp.where(col <= row, s, NEG_BIG)


def _attn_fwd(q, k, v):
    nh, t, _ = q.shape
    tq = tk = min(512, t)
    nq, nk = t // tq, t // tk

    def body(q_ref, k_ref, v_ref, o_ref, lse_ref, m_sc, acc_sc):
        i, j = pl.program_id(0), pl.program_id(1)

        @pl.when(j == 0)
        def _():
            m_sc[...] = jnp.full_like(m_sc, NEG_BIG)
            acc_sc[...] = jnp.zeros_like(acc_sc)

        def step(diagonal):
            for hh in range(nh):
                s = _dot_nt(q_ref[hh], k_ref[hh]) * ATTN_SCALE
                if diagonal:
                    s = _causal_mask(s)
                m_old = m_sc[hh]
                m_new = jnp.maximum(m_old, jnp.max(s, axis=-1, keepdims=True))
                p = jnp.exp(s - m_new).astype(BF16)
                acc_sc[hh] = jnp.exp(m_old - m_new) * acc_sc[hh] + _dot_nn(p, v_ref[hh])
                m_sc[hh] = m_new

        @pl.when(j < i)
        def _():
            step(False)

        @pl.when(j == i)
        def _():
            step(True)
            for hh in range(nh):
                acc = acc_sc[hh]
                l = acc[:, V_DIM:2 * V_DIM]
                o_ref[:, hh * V_DIM:(hh + 1) * V_DIM] = (acc[:, 0:V_DIM] / l).astype(BF16)
                lse_ref[hh] = m_sc[hh] + jnp.log(l[:, 0:1])

    kv_map = lambda i, j: (0, jnp.minimum(j, i), 0)
    return pl.pallas_call(
        body, name="attn_fwd", grid=(nq, nk),
        in_specs=[pl.BlockSpec((nh, tq, HEAD_PAD), lambda i, j: (0, i, 0)),
                  pl.BlockSpec((nh, tk, HEAD_PAD), kv_map),
                  pl.BlockSpec((nh, tk, 2 * V_DIM), kv_map)],
        out_specs=[pl.BlockSpec((tq, nh * V_DIM), lambda i, j: (i, 0)),
                   pl.BlockSpec((nh, tq, 1), lambda i, j: (0, i, 0))],
        out_shape=[jax.ShapeDtypeStruct((t, nh * V_DIM), BF16), jax.ShapeDtypeStruct((nh, t, 1), F32)],
        scratch_shapes=[pltpu.VMEM((nh, tq, 1), F32), pltpu.VMEM((nh, tq, 2 * V_DIM), F32)],
        compiler_params=_cparams(),
    )(q, k, v)


def _attn_delta(o, do):
    t, w = o.shape
    nh = w // V_DIM
    tm = min(512, t)

    def body(o_ref, do_ref, d_ref):
        prod = o_ref[...].astype(F32) * do_ref[...].astype(F32)
        for hh in range(nh):
            d_ref[hh] = jnp.sum(prod[:, hh * V_DIM:(hh + 1) * V_DIM], axis=-1, keepdims=True)

    return pl.pallas_call(
        body, name="attn_delta", grid=(t // tm,),
        in_specs=[pl.BlockSpec((tm, w), lambda i: (i, 0)), pl.BlockSpec((tm, w), lambda i: (i, 0))],
        out_specs=pl.BlockSpec((nh, tm, 1), lambda i: (0, i, 0)),
        out_shape=jax.ShapeDtypeStruct((nh, t, 1), F32),
        compiler_params=_cparams(),
    )(o, do)


ATTN_BWD_HEADS = 2


def _attn_bwd(q, k, v, do, lse, delta):
    nh, t, _ = q.shape
    hp = ATTN_BWD_HEADS
    tq = tk = min(512, t)
    nq, nk = t // tq, t // tk

    def body(q_ref, k_ref, v_ref, do_ref, lse_ref, dlt_ref, dq_ref, dk_ref, dv_ref):
        j, i = pl.program_id(1), pl.program_id(2)

        @pl.when(jnp.logical_and(j == 0, i == 0))
        def _():
            dq_ref[...] = jnp.zeros_like(dq_ref)

        def step(diagonal):
            for hh in range(hp):
                qq, kk = q_ref[hh], k_ref[hh]
                dob = do_ref[:, hh * V_DIM:(hh + 1) * V_DIM]
                s = _dot_nt(qq, kk) * ATTN_SCALE
                if diagonal:
                    s = _causal_mask(s)
                p = jnp.exp(s - lse_ref[hh])
                dpp = _dot_nt(dob, v_ref[hh])
                dsb = (p * (dpp - dlt_ref[hh]) * ATTN_SCALE).astype(BF16)
                _accumulate(dv_ref.at[hh], _dot_tn(p.astype(BF16), dob), diagonal)
                _accumulate(dk_ref.at[hh], _dot_tn(dsb, qq), diagonal)
                dq_ref[hh, pl.ds(pl.multiple_of(i * tq, tq), tq), :] += _dot_nn(dsb, kk)

        @pl.when(i > j)
        def _():
            step(False)

        @pl.when(i == j)
        def _():
            step(True)

    qmap = lambda h, j, i: (h, jnp.maximum(i, j), 0)
    return pl.pallas_call(
        body, name="attn_bwd", grid=(nh // hp, nk, nq),
        in_specs=[pl.BlockSpec((hp, tq, HEAD_PAD), qmap),
                  pl.BlockSpec((hp, tk, HEAD_PAD), lambda h, j, i: (h, j, 0)),
                  pl.BlockSpec((hp, tk, V_DIM), lambda h, j, i: (h, j, 0)),
                  pl.BlockSpec((tq, hp * V_DIM), lambda h, j, i: (jnp.maximum(i, j), h)),
                  pl.BlockSpec((hp, tq, 1), qmap),
                  pl.BlockSpec((hp, tq, 1), qmap)],
        out_specs=[pl.BlockSpec((hp, t, HEAD_PAD), lambda h, j, i: (h, 0, 0)),
                   pl.BlockSpec((hp, tk, HEAD_PAD), lambda h, j, i: (h, j, 0)),
                   pl.BlockSpec((hp, tk, V_DIM), lambda h, j, i: (h, j, 0))],
        out_shape=[jax.ShapeDtypeStruct((nh, t, HEAD_PAD), F32), jax.ShapeDtypeStruct((nh, t, HEAD_PAD), F32),
                   jax.ShapeDtypeStruct((nh, t, V_DIM), F32)],
        compiler_params=_cparams(),
    )(q, k, v, do, lse, delta)


def _pool_counts(first_token, rows, w):
    tok = lax.broadcasted_iota(jnp.int32, (rows, POOL_CH), 0) + first_token
    return jnp.minimum(tok + 1, w).astype(F32)


def _pool_centered(zbuf, g, w, i, tm):
    lanes = pl.ds(g * POOL_CH, POOL_CH)
    cur = zbuf[pl.ds(POOL_HALO, tm), lanes]
    win = cur
    for s in range(1, w):
        win = win + zbuf[pl.ds(POOL_HALO - s, tm), lanes]
    return win / _pool_counts(i * tm, tm, w) - cur


def _pool_load(zbuf, z_ref, halo_ref, i, tm):
    @pl.when(i == 0)
    def _():
        zbuf[pl.ds(0, POOL_HALO), :] = jnp.zeros((POOL_HALO, zbuf.shape[1]), F32)

    @pl.when(i > 0)
    def _():
        zbuf[pl.ds(0, POOL_HALO), :] = halo_ref[...]

    zbuf[pl.ds(POOL_HALO, tm), :] = z_ref[...]


def _pool_fwd(z, pool_w, pool_scale):
    t = z.shape[0]
    pw = len(POOL_WINDOWS) * POOL_CH
    tm = min(512, t)
    hb = tm // POOL_HALO

    def body(z_ref, halo_ref, w_ref, sc_ref, p_ref, zbuf):
        i = pl.program_id(0)
        _pool_load(zbuf, z_ref, halo_ref, i, tm)
        for g, w in enumerate(POOL_WINDOWS):
            c = _pool_centered(zbuf, g, w, i, tm)
            y = _dot_nn(c.astype(BF16), w_ref[g]) * sc_ref[:, g * POOL_CH:(g + 1) * POOL_CH]
            p_ref[:, g * POOL_CH:(g + 1) * POOL_CH] = y.astype(BF16)

    return pl.pallas_call(
        body, name="pool_fwd", grid=(t // tm,),
        in_specs=[pl.BlockSpec((tm, pw), lambda i: (i, 1)),
                  pl.BlockSpec((POOL_HALO, pw), lambda i: (jnp.maximum(i * hb - 1, 0), 1)),
                  pl.BlockSpec(pool_w.shape, lambda i: (0, 0, 0)),
                  pl.BlockSpec((1, pw), lambda i: (0, 0))],
        out_specs=pl.BlockSpec((tm, pw), lambda i: (i, 0)),
        out_shape=jax.ShapeDtypeStruct((t, pw), BF16),
        scratch_shapes=[pltpu.VMEM((POOL_HALO + tm, pw), F32)],
        compiler_params=_cparams(),
    )(z, z, pool_w, pool_scale)


def _pool_bwd(dp, z, pool_w, pool_scale):
    t = z.shape[0]
    ng = len(POOL_WINDOWS)
    pw = ng * POOL_CH
    tm = min(512, t)
    hb = tm // POOL_HALO
    nt = t // tm

    def body(dp_ref, dpn_ref, z_ref, halo_ref, w_ref, sc_ref, dz_ref, dw_ref, dsc_ref, zbuf, dbuf):
        i = pl.program_id(0)
        _pool_load(zbuf, z_ref, halo_ref, i, tm)

        @pl.when(i == 0)
        def _():
            dw_ref[...] = jnp.zeros_like(dw_ref)
            dsc_ref[...] = jnp.zeros_like(dsc_ref)

        nxt_ok = (i < nt - 1).astype(F32)
        for g, w in enumerate(POOL_WINDOWS):
            lanes = pl.ds(g * POOL_CH, POOL_CH)
            cols = slice(g * POOL_CH, (g + 1) * POOL_CH)
            sc = sc_ref[:, cols]
            wg = w_ref[g]
            c = _pool_centered(zbuf, g, w, i, tm).astype(BF16)
            ypre = _dot_nn(c, wg)
            dpg = dp_ref[:, cols].astype(F32)
            dsc_ref[:, cols] += jnp.sum(dpg * ypre, axis=0, keepdims=True)
            dyb = (dpg * sc).astype(BF16)
            dw_ref[g] += _dot_tn(c, dyb)
            dd = _dot_nt(dyb, wg)
            dyn = (dpn_ref[:, cols].astype(F32) * sc).astype(BF16)
            ddn = _dot_nt(dyn, wg) * nxt_ok
            dbuf[pl.ds(0, tm), lanes] = dd / _pool_counts(i * tm, tm, w)
            dbuf[pl.ds(tm, POOL_HALO), lanes] = ddn / _pool_counts((i + 1) * tm, POOL_HALO, w)
            acc = -dd
            for s in range(w):
                acc = acc + dbuf[pl.ds(s, tm), lanes]
            dz_ref[:, cols] = acc

    return pl.pallas_call(
        body, name="pool_bwd", grid=(nt,),
        in_specs=[pl.BlockSpec((tm, pw), lambda i: (i, 0)),
                  pl.BlockSpec((POOL_HALO, pw), lambda i: (jnp.minimum((i + 1) * hb, t // POOL_HALO - 1), 0)),
                  pl.BlockSpec((tm, pw), lambda i: (i, 1)),
                  pl.BlockSpec((POOL_HALO, pw), lambda i: (jnp.maximum(i * hb - 1, 0), 1)),
                  pl.BlockSpec(pool_w.shape, lambda i: (0, 0, 0)),
                  pl.BlockSpec((1, pw), lambda i: (0, 0))],
        out_specs=[pl.BlockSpec((tm, pw), lambda i: (i, 0)),
                   pl.BlockSpec((ng, POOL_CH, POOL_CH), lambda i: (0, 0, 0)),
                   pl.BlockSpec((1, pw), lambda i: (0, 0))],
        out_shape=[jax.ShapeDtypeStruct((t, pw), F32), jax.ShapeDtypeStruct((ng, POOL_CH, POOL_CH), F32),
                   jax.ShapeDtypeStruct((1, pw), F32)],
        scratch_shapes=[pltpu.VMEM((POOL_HALO + tm, pw), F32), pltpu.VMEM((tm + POOL_HALO, pw), F32)],
        compiler_params=_cparams(),
    )(dp, dp, z, z, pool_w, pool_scale)


def _mla_bwd(dq_h, dk_h, dv_h, z, dz_pool, h1, dh2, mix_norm, w_in, q_norm, wq_t, kv_norm, wkv, pos, rope_tab):
    t, d = h1.shape
    tm = min(512, t)

    def body(dqh_ref, dkh_ref, dvh_ref, z_ref, dzp_ref, h_ref, dh2_ref, gm_ref, win_ref, gq_ref, wq_ref, gkv_ref,
             wkv_ref, pos_ref, tab_ref, dh1_ref, dq_ref, dkv_ref, dz_ref, dgq_ref, dgkv_ref, dgm_ref):
        i = pl.program_id(0)
        first = i == 0
        cos_t, sin_t = _rope_tables(pos_ref[...], tab_ref[...])
        dq_parts, dkv_parts = [], []
        dk_pe = jnp.zeros((tm, 128), F32)
        for hh in range(MLA_HEADS):
            dqh = dqh_ref[hh]
            dq_parts += [dqh[:, 0:NOPE], _rope_apply_t(dqh[:, NOPE:HEAD_PAD], cos_t, sin_t)]
            dkh = dkh_ref[hh]
            dkv_parts += [dkh[:, 0:NOPE], dvh_ref[hh]]
            dk_pe = dk_pe + dkh[:, NOPE:HEAD_PAD]
        dqb = jnp.concatenate(dq_parts, axis=-1).astype(BF16)
        dkvb = jnp.concatenate(dkv_parts, axis=-1).astype(BF16)
        dq_ref[...] = dqb
        dkv_ref[...] = dkvb
        z = z_ref[...]
        c_q = z[:, 0:Q_RANK]
        gq = gq_ref[...]
        _, rq = _rms_fwd(c_q, gq)
        dcq, dgq = _rms_bwd(_dot_nn(dqb, wq_ref[...]), c_q, gq, rq)
        c_kv = z[:, Q_RANK:Q_RANK + KV_RANK]
        gkv = gkv_ref[...]
        _, rkv = _rms_fwd(c_kv, gkv)
        dckv, dgkv = _rms_bwd(_dot_nt(dkvb, wkv_ref[...]), c_kv, gkv, rkv)
        dkr = _rope_apply_t(dk_pe, cos_t, sin_t)
        dzb = jnp.concatenate([dcq, dckv, dkr, dzp_ref[...]], axis=-1).astype(BF16)
        dz_ref[...] = dzb
        x = h_ref[...]
        gm = gm_ref[...]
        _, rm = _rms_fwd(x, gm)
        dx, dgm = _rms_bwd(_dot_nt(dzb, win_ref[...]), x, gm, rm)
        dh1_ref[...] = dh2_ref[...] + dx
        _accumulate(dgq_ref, dgq, first)
        _accumulate(dgkv_ref, dgkv, first)
        _accumulate(dgm_ref, dgm, first)

    full = lambda shape: pl.BlockSpec(shape, lambda i: (0,) * len(shape))
    row = lambda w: pl.BlockSpec((tm, w), lambda i: (i, 0))
    head = lambda w: pl.BlockSpec((MLA_HEADS, tm, w), lambda i: (0, i, 0))
    pw = len(POOL_WINDOWS) * POOL_CH
    return pl.pallas_call(
        body, name="mla_bwd", grid=(t // tm,),
        in_specs=[head(HEAD_PAD), head(HEAD_PAD), head(V_DIM), row(d), row(pw), row(d), row(d),
                  _resident((1, d)), _resident(w_in.shape), _resident((1, Q_RANK)), _resident(wq_t.shape),
                  _resident((1, KV_RANK)), _resident(wkv.shape), row(1), _resident(rope_tab.shape)],
        out_specs=[row(d), row(d), row(d), row(d), full((1, Q_RANK)), full((1, KV_RANK)), full((1, d))],
        out_shape=[jax.ShapeDtypeStruct((t, d), F32), jax.ShapeDtypeStruct((t, d), BF16),
                   jax.ShapeDtypeStruct((t, d), BF16), jax.ShapeDtypeStruct((t, d), BF16),
                   jax.ShapeDtypeStruct((1, Q_RANK), F32), jax.ShapeDtypeStruct((1, KV_RANK), F32),
                   jax.ShapeDtypeStruct((1, d), F32)],
        compiler_params=_cparams(),
    )(dq_h, dk_h, dv_h, z, dz_pool, h1, dh2, mix_norm, w_in, q_norm, wq_t, kv_norm, wkv, pos, rope_tab)


def _mem_kv(mem, mem_norm, wmkv):
    n, d = mem.shape

    def body(mem_ref, g_ref, w_ref, memn_ref, k_ref, v_ref):
        y, _ = _rms_fwd(mem_ref[...], g_ref[...])
        yb = y.astype(BF16)
        memn_ref[...] = yb
        for hh in range(MEM_HEADS):
            k_ref[hh] = _dot_nn(yb, w_ref[hh]).astype(BF16)
            v_ref[hh] = _dot_nn(yb, w_ref[MEM_HEADS + hh]).astype(BF16)

    return pl.pallas_call(
        body, name="mem_kv",
        out_shape=[jax.ShapeDtypeStruct((n, d), BF16), jax.ShapeDtypeStruct((MEM_HEADS, n, MEM_HD), BF16),
                   jax.ShapeDtypeStruct((MEM_HEADS, n, MEM_HD), BF16)],
        compiler_params=_cparams(),
    )(mem, mem_norm, wmkv)


def _mem_softmax(qb, km):
    s = _dot_nt(qb, km) * MEM_SCALE
    e = jnp.exp(s - jnp.max(s, axis=-1, keepdims=True))
    return e / jnp.sum(e, axis=-1, keepdims=True)


def _xattn_fwd(h1, a, p, w_out, g, wmq, km, vm, wmo):
    t, d = h1.shape
    tm = min(512, t)
    half = a.shape[1]

    def body(h_ref, a_ref, p_ref, wo_ref, g_ref, wmq_ref, km_ref, vm_ref, wmo_ref,
             h2_ref, h3_ref, hn_ref, q_ref, o_ref):
        h2 = h_ref[...] + _dot_nn(a_ref[...], wo_ref[0:half, :]) + _dot_nn(p_ref[...], wo_ref[half:2 * half, :])
        h2_ref[...] = h2
        hn, _ = _rms_fwd(h2, g_ref[...])
        hnb = hn.astype(BF16)
        hn_ref[...] = hnb
        qb = _dot_nn(hnb, wmq_ref[...]).astype(BF16)
        q_ref[...] = qb
        outs = []
        for hh in range(MEM_HEADS):
            pr = _mem_softmax(qb[:, hh * MEM_HD:(hh + 1) * MEM_HD], km_ref[hh])
            outs.append(_dot_nn(pr.astype(BF16), vm_ref[hh]))
        ob = jnp.concatenate(outs, axis=-1).astype(BF16)
        o_ref[...] = ob
        h3_ref[...] = h2 + _dot_nn(ob, wmo_ref[...])

    full = lambda shape: pl.BlockSpec(shape, lambda i: (0,) * len(shape))
    row = lambda w: pl.BlockSpec((tm, w), lambda i: (i, 0))
    return pl.pallas_call(
        body, name="xattn_fwd", grid=(t // tm,),
        in_specs=[row(d), row(half), row(half), _resident(w_out.shape), _resident((1, d)), _resident(wmq.shape),
                  _resident(km.shape), _resident(vm.shape), _resident(wmo.shape)],
        out_specs=[row(d), row(d), row(d), row(d), row(d)],
        out_shape=[jax.ShapeDtypeStruct((t, d), F32), jax.ShapeDtypeStruct((t, d), F32),
                   jax.ShapeDtypeStruct((t, d), BF16), jax.ShapeDtypeStruct((t, d), BF16),
                   jax.ShapeDtypeStruct((t, d), BF16)],
        compiler_params=_cparams(),
    )(h1, a, p, w_out, g, wmq, km, vm, wmo)


def _xattn_bwd(dh3, h2, qm, g, wmq, km, vm, wmo, w_out, token=None):
    t, d = h2.shape
    tm = min(512, t)
    half = d // 2

    def body(dh3_ref, h2_ref, q_ref, g_ref, wmq_ref, km_ref, vm_ref, wmo_ref, wo_ref,
             dh2_ref, dq_ref, da_ref, dp_ref, dk_ref, dv_ref, dg_ref):
        i = pl.program_id(0)
        first = i == 0

        @pl.when(first)
        def _():
            dk_ref[...] = jnp.zeros_like(dk_ref)
            dv_ref[...] = jnp.zeros_like(dv_ref)

        dh3 = dh3_ref[...]
        dob = _dot_nt(dh3.astype(BF16), wmo_ref[...]).astype(BF16)
        qb = q_ref[...]
        dq_parts = []
        for hh in range(MEM_HEADS):
            cols = slice(hh * MEM_HD, (hh + 1) * MEM_HD)
            kk, vv = km_ref[hh], vm_ref[hh]
            pr = _mem_softmax(qb[:, cols], kk)
            doh = dob[:, cols]
            dv_ref[hh] += _dot_tn(pr.astype(BF16), doh)
            dpp = _dot_nt(doh, vv)
            dsb = (pr * (dpp - jnp.sum(dpp * pr, axis=-1, keepdims=True)) * MEM_SCALE).astype(BF16)
            dq_parts.append(_dot_nn(dsb, kk))
            dk_ref[hh] += _dot_tn(dsb, qb[:, cols])
        dqb = jnp.concatenate(dq_parts, axis=-1).astype(BF16)
        dq_ref[...] = dqb
        x = h2_ref[...]
        gg = g_ref[...]
        _, r = _rms_fwd(x, gg)
        dx, dg = _rms_bwd(_dot_nt(dqb, wmq_ref[...]), x, gg, r)
        dh2 = dh3 + dx
        dh2_ref[...] = dh2
        dap = _dot_nt(dh2.astype(BF16), wo_ref[...])
        da_ref[...] = dap[:, 0:half].astype(BF16)
        dp_ref[...] = dap[:, half:d].astype(BF16)
        _accumulate(dg_ref, dg, first)

    full = lambda shape: pl.BlockSpec(shape, lambda i: (0,) * len(shape))
    row = lambda w: pl.BlockSpec((tm, w), lambda i: (i, 0))
    return _call_after(
        token, body,
        [row(d), row(d), row(d), _resident((1, d)), _resident(wmq.shape), _resident(km.shape), _resident(vm.shape),
         _resident(wmo.shape), _resident(w_out.shape)],
        (dh3, h2, qm, g, wmq, km, vm, wmo, w_out),
        name="xattn_bwd", grid=(t // tm,),
        out_specs=[row(d), row(d), row(half), row(half), full(km.shape), full(vm.shape), full((1, d))],
        out_shape=[jax.ShapeDtypeStruct((t, d), F32), jax.ShapeDtypeStruct((t, d), BF16),
                   jax.ShapeDtypeStruct((t, half), BF16), jax.ShapeDtypeStruct((t, half), BF16),
                   jax.ShapeDtypeStruct(km.shape, F32), jax.ShapeDtypeStruct(vm.shape, F32),
                   jax.ShapeDtypeStruct((1, d), F32)],
        compiler_params=_cparams(),
    )


def _mem_kv_bwd(dkm, dvm, memn, mem, mem_norm, wmkv):
    n, d = mem.shape

    def body(dk_ref, dv_ref, memn_ref, mem_ref, g_ref, w_ref, dw_ref, dg_ref):
        memn = memn_ref[...]
        dmemn = jnp.zeros((n, d), F32)
        for s in range(2 * MEM_HEADS):
            src = dk_ref[s] if s < MEM_HEADS else dv_ref[s - MEM_HEADS]
            db = src.astype(BF16)
            dw_ref[s] = _dot_tn(memn, db)
            dmemn = dmemn + _dot_nt(db, w_ref[s])
        x = mem_ref[...]
        gg = g_ref[...]
        _, r = _rms_fwd(x, gg)
        _, dg = _rms_bwd(dmemn, x, gg, r)
        dg_ref[...] = dg

    return pl.pallas_call(
        body, name="mem_kv_bwd",
        out_shape=[jax.ShapeDtypeStruct(wmkv.shape, F32), jax.ShapeDtypeStruct((1, d), F32)],
        compiler_params=_cparams(),
    )(dkm, dvm, memn, mem, mem_norm, wmkv)


MESH_ID = pl.DeviceIdType.MESH
ANY = pl.BlockSpec(memory_space=pl.ANY)


def _coords():
    return lax.axis_index("x"), lax.axis_index("y"), lax.axis_index("c")


def _other_chips(x, y):
    return [(1 - x, y), (x, 1 - y), (1 - x, 1 - y)]


def _core_reduce(g, tag):
    _, r, w = g.shape

    def body(g_ref, part_ref, own_sc, recv_sc, send_sems, recv_sems, local_sems):
        x, y, c = _coords()
        sent, local = [], []
        for chip in range(4):
            sent.append(pltpu.make_async_remote_copy(
                src_ref=g_ref.at[2 * chip + (1 - c)], dst_ref=recv_sc.at[chip],
                send_sem=send_sems.at[chip], recv_sem=recv_sems.at[chip],
                device_id=(x, y, 1 - c), device_id_type=MESH_ID))
            local.append(pltpu.make_async_copy(g_ref.at[2 * chip + c], own_sc.at[chip], local_sems.at[chip]))
        for cp in sent + local:
            cp.start()
        for chip in range(4):
            local[chip].wait()
            sent[chip].wait_recv()
            part_ref[chip] = (own_sc[chip].astype(F32) + recv_sc[chip].astype(F32)).astype(part_ref.dtype)
        for cp in sent:
            cp.wait_send()

    return pl.pallas_call(
        body, name="core_reduce_" + tag,
        out_shape=jax.ShapeDtypeStruct((4, r, w), g.dtype),
        in_specs=[ANY], out_specs=pl.BlockSpec(memory_space=pltpu.VMEM),
        scratch_shapes=[pltpu.VMEM((4, r, w), g.dtype), pltpu.VMEM((4, r, w), g.dtype),
                        pltpu.SemaphoreType.DMA((4,)), pltpu.SemaphoreType.DMA((4,)), pltpu.SemaphoreType.DMA((4,))],
        compiler_params=_cparams(),
    )(g)


HBM_SPEC = pl.BlockSpec(memory_space=pltpu.HBM)
SEM_SPEC = pl.BlockSpec(memory_space=pltpu.SEMAPHORE)
SPLIT_EFFECT = pltpu.SideEffectType.DATAFLOW_SIDE_EFFECTING


def _ici_refs(gather, src_ref, land_ref, j, px, py, slot_chip, c):
    if gather:
        return src_ref, land_ref.at[:, 4 * slot_chip[0] + 2 * slot_chip[1] + c]
    return src_ref.at[2 * px + py], land_ref.at[j]


def _ici_start(src, after, name, gather):
    r, w = src.shape[-2:]
    land_shape = (src.shape[0], N_DEV, r, w) if gather else (3, r, w)

    def body(src_ref, land_ref, after_ref, send_sems, recv_sems, src_thru, land_thru, token):
        x, y, c = _coords()
        for j, (px, py) in enumerate(_other_chips(x, y)):
            s_ref, d_ref = _ici_refs(gather, src_ref, land_ref, j, px, py, (x, y), c)
            pltpu.make_async_remote_copy(
                src_ref=s_ref, dst_ref=d_ref, send_sem=send_sems.at[j], recv_sem=recv_sems.at[j],
                device_id=(px, py, c), device_id_type=MESH_ID).start()
        token[...] = jnp.zeros_like(token)

    return pl.pallas_call(
        body, name=name,
        out_shape=(pltpu.SemaphoreType.DMA((3,)), pltpu.SemaphoreType.DMA((3,)), pltpu.HBM(src.shape, src.dtype),
                   pltpu.HBM(land_shape, src.dtype), jax.ShapeDtypeStruct((8, 128), F32)),
        in_specs=(HBM_SPEC, HBM_SPEC, ANY),
        out_specs=(SEM_SPEC, SEM_SPEC, HBM_SPEC, HBM_SPEC, pl.BlockSpec(memory_space=pltpu.VMEM)),
        input_output_aliases={0: 2, 1: 3},
        compiler_params=pltpu.CompilerParams(has_side_effects=SPLIT_EFFECT),
    )(pltpu.with_memory_space_constraint(src, pltpu.HBM),
      pltpu.with_memory_space_constraint(lax.empty(land_shape, src.dtype), pltpu.HBM), after)


def _ici_wait(started, after, name, gather):
    send_sems, recv_sems, src_thru, land_thru, _ = started

    def body(src_ref, land_ref, send_sems, recv_sems, after_ref, src_dead, got_ref):
        x, y, c = _coords()
        for j, (px, py) in enumerate(_other_chips(x, y)):
            s_ref, d_ref = _ici_refs(gather, src_ref, land_ref, j, px, py, (px, py), c)
            copy = pltpu.make_async_remote_copy(
                src_ref=s_ref, dst_ref=d_ref, send_sem=send_sems.at[j], recv_sem=recv_sems.at[j],
                device_id=(px, py, c), device_id_type=MESH_ID)
            copy.wait_send()
            copy.wait_recv()

    return pl.pallas_call(
        body, name=name,
        out_shape=(pltpu.HBM(src_thru.shape, src_thru.dtype), pltpu.HBM(land_thru.shape, land_thru.dtype)),
        in_specs=(HBM_SPEC, HBM_SPEC, SEM_SPEC, SEM_SPEC, ANY),
        out_specs=(HBM_SPEC, HBM_SPEC), input_output_aliases={0: 0, 1: 1},
        compiler_params=pltpu.CompilerParams(has_side_effects=SPLIT_EFFECT),
    )(src_thru, land_thru, send_sems, recv_sems, after)


def _peer(k, x, y, c):
    return x ^ ((k >> 2) & 1), y ^ ((k >> 1) & 1), c ^ (k & 1)


def _peers_start(src, after, name):
    r, w = src.shape
    x, y, c = _coords()
    land = lax.dynamic_update_slice(jnp.zeros((N_DEV, r, w), src.dtype), src[None], (4 * x + 2 * y + c, 0, 0))

    def body(src_ref, land_ref, after_ref, send_sems, recv_sems, src_thru, land_thru, token):
        x, y, c = _coords()
        for k in range(1, N_DEV):
            pltpu.make_async_remote_copy(
                src_ref=src_ref, dst_ref=land_ref.at[4 * x + 2 * y + c],
                send_sem=send_sems.at[k - 1], recv_sem=recv_sems.at[k - 1],
                device_id=_peer(k, x, y, c), device_id_type=MESH_ID).start()
        token[...] = jnp.zeros_like(token)

    return pl.pallas_call(
        body, name=name,
        out_shape=(pltpu.SemaphoreType.DMA((N_DEV - 1,)), pltpu.SemaphoreType.DMA((N_DEV - 1,)),
                   pltpu.HBM(src.shape, src.dtype), pltpu.HBM(land.shape, src.dtype),
                   jax.ShapeDtypeStruct((8, 128), F32)),
        in_specs=(HBM_SPEC, HBM_SPEC, ANY),
        out_specs=(SEM_SPEC, SEM_SPEC, HBM_SPEC, HBM_SPEC, pl.BlockSpec(memory_space=pltpu.VMEM)),
        input_output_aliases={0: 2, 1: 3},
        compiler_params=pltpu.CompilerParams(has_side_effects=SPLIT_EFFECT),
    )(pltpu.with_memory_space_constraint(src, pltpu.HBM), pltpu.with_memory_space_constraint(land, pltpu.HBM), after)


def _peers_wait(started, after, name):
    send_sems, recv_sems, src_thru, land_thru, _ = started

    def body(src_ref, land_ref, send_sems, recv_sems, after_ref, src_dead, got_ref):
        x, y, c = _coords()
        for k in range(1, N_DEV):
            px, py, pc = _peer(k, x, y, c)
            copy = pltpu.make_async_remote_copy(
                src_ref=src_ref, dst_ref=land_ref.at[4 * px + 2 * py + pc],
                send_sem=send_sems.at[k - 1], recv_sem=recv_sems.at[k - 1],
                device_id=(px, py, pc), device_id_type=MESH_ID)
            copy.wait_send()
            copy.wait_recv()

    return pl.pallas_call(
        body, name=name,
        out_shape=(pltpu.HBM(src_thru.shape, src_thru.dtype), pltpu.HBM(land_thru.shape, land_thru.dtype)),
        in_specs=(HBM_SPEC, HBM_SPEC, SEM_SPEC, SEM_SPEC, ANY),
        out_specs=(HBM_SPEC, HBM_SPEC), input_output_aliases={0: 0, 1: 1},
        compiler_params=pltpu.CompilerParams(has_side_effects=SPLIT_EFFECT),
    )(src_thru, land_thru, send_sems, recv_sems, after)


def _core_share(own, gathered, name):
    def body(own_ref, gin_ref, out_ref, stage, send_sems, recv_sems, local_sem):
        x, y, c = _coords()
        sibling = (x, y, 1 - c)
        chips = [(x, y)] + _other_chips(x, y)
        stage_in = pltpu.make_async_copy(own_ref, stage, local_sem)
        stage_in.start()
        sent, arriving = [], []
        for k, (px, py) in enumerate(chips):
            slot = out_ref.at[:, 4 * px + 2 * py + c]
            sent.append(pltpu.make_async_remote_copy(
                src_ref=own_ref if k == 0 else slot, dst_ref=slot,
                send_sem=send_sems.at[k], recv_sem=recv_sems.at[k], device_id=sibling, device_id_type=MESH_ID))
            arriving.append(pltpu.make_async_remote_copy(
                src_ref=own_ref, dst_ref=out_ref.at[:, 4 * px + 2 * py + (1 - c)],
                send_sem=send_sems.at[k], recv_sem=recv_sems.at[k], device_id=sibling, device_id_type=MESH_ID))
        for cp in sent:
            cp.start()
        stage_in.wait()
        stage_out = pltpu.make_async_copy(stage, out_ref.at[:, 4 * x + 2 * y + c], local_sem)
        stage_out.start()
        for cp in arriving:
            cp.wait_recv()
        for cp in sent:
            cp.wait_send()
        stage_out.wait()

    return pl.pallas_call(
        body, name=name,
        out_shape=jax.ShapeDtypeStruct(gathered.shape, own.dtype),
        in_specs=[ANY, ANY], out_specs=ANY, input_output_aliases={1: 0},
        scratch_shapes=[pltpu.VMEM(own.shape, own.dtype), pltpu.SemaphoreType.DMA((4,)),
                        pltpu.SemaphoreType.DMA((4,)), pltpu.SemaphoreType.DMA],
    )(own, gathered)


def _adamw(w, g, m, v):
    m = ADAM_B1 * m + (1.0 - ADAM_B1) * g
    v = ADAM_B2 * v + (1.0 - ADAM_B2) * (g * g)
    m_hat = m / ADAM_C1
    v_hat = v / ADAM_C2
    delta = -ADAM_LR * (m_hat / (jnp.sqrt(v_hat) + ADAM_EPS) + ADAM_WD * w)
    return delta, m, v


def _adam_big(part, land, w, m, v, chip_idx, tag, token):
    r, wd = w.shape
    tr, tw = _row_tile(r, 1024), 256

    def body(s_ref, tok_ref, p_ref, l_ref, w_ref, m_ref, v_ref, g_ref, d_ref, mo_ref, vo_ref):
        g = p_ref[0].astype(F32)
        for j in range(3):
            g = g + l_ref[j].astype(F32)
        delta, mn, vn = _adamw(w_ref[...], g, m_ref[...], v_ref[...])
        g_ref[...] = g
        d_ref[...] = delta
        mo_ref[...] = mn
        vo_ref[...] = vn

    row = pl.BlockSpec((tr, tw), lambda i, j, s: (i, j))
    return pl.pallas_call(
        body, name="adam_big_" + tag,
        grid_spec=pltpu.PrefetchScalarGridSpec(
            num_scalar_prefetch=1, grid=(r // tr, wd // tw),
            in_specs=[pl.BlockSpec((8, 128), lambda i, j, s: (0, 0)),
                      pl.BlockSpec((1, tr, tw), lambda i, j, s: (s[0], i, j)),
                      pl.BlockSpec((3, tr, tw), lambda i, j, s: (0, i, j)), row, row, row],
            out_specs=[row, row, row, row]),
        out_shape=[jax.ShapeDtypeStruct((r, wd), F32)] * 4,
        compiler_params=_cparams(),
    )(chip_idx, token, part, land, w, m, v)


def _adam_small(parts, w, m, v):
    _, r, wd = parts.shape

    def body(p_ref, w_ref, m_ref, v_ref, g_ref, d_ref, mo_ref, vo_ref):
        g = p_ref[0]
        for k in range(1, N_DEV):
            g = g + p_ref[k]
        delta, mn, vn = _adamw(w_ref[...], g, m_ref[...], v_ref[...])
        g_ref[...] = g
        d_ref[...] = delta
        mo_ref[...] = mn
        vo_ref[...] = vn

    return pl.pallas_call(
        body, name="adam_small",
        out_shape=[jax.ShapeDtypeStruct((r, wd), F32)] * 4,
        compiler_params=_cparams(),
    )(parts, w, m, v)


def _pad_rows(a, rows):
    return jnp.pad(a, ((0, rows - a.shape[0]), (0, 0)))


def _pad_w_in(w):
    cut = Q_RANK + KV_RANK + ROPE
    return jnp.concatenate([w[:, :cut], jnp.zeros((w.shape[0], 64), w.dtype), w[:, cut:]], axis=1)


def _unpad_w_in(w):
    cut = Q_RANK + KV_RANK + ROPE
    return jnp.concatenate([w[:, :cut], w[:, cut + 64:]], axis=1)


def _pack_mid(p):
    parts = [_pad_w_in(p["w_in"][0]), p["w_out"][0], p["w_mq"][0], p["w_mo"][0],
             p["w_mkv"][0].reshape(256, D_MODEL),
             _pad_rows(p["w_q_up"][0].T.reshape(24, D_MODEL), 32),
             p["w_kv_up"][0].reshape(16, D_MODEL)]
    return jnp.concatenate(parts, axis=0)


def _pack_segments(p, group):
    if group == "mid":
        return _pack_mid(p)[None]
    return jnp.stack([p[group + "_w_gate"][0].T, p[group + "_w_up"][0].T, p[group + "_w_down"][0]])


UNIT_WEIGHT = {"ffn1_g": ("ffn1_w_gate", True), "ffn1_u": ("ffn1_w_up", True), "ffn1_d": ("ffn1_w_down", False),
               "ffn2_g": ("ffn2_w_gate", True), "ffn2_u": ("ffn2_w_up", True), "ffn2_d": ("ffn2_w_down", False)}


def _pack_unit(p, unit):
    if unit == "mid":
        return _pack_mid(p)
    name, transposed = UNIT_WEIGHT[unit]
    return p[name][0].T if transposed else p[name][0]


def _unpack_unit(a, unit):
    if unit != "mid":
        name, transposed = UNIT_WEIGHT[unit]
        return {name: (a.T if transposed else a)[None]}
    seg = lambda n: a[SEG_OFF[n][0]:SEG_OFF[n][0] + SEG_OFF[n][1]]
    return {"w_in": _unpad_w_in(seg("w_in"))[None], "w_out": seg("w_out")[None], "w_mq": seg("w_mq")[None],
            "w_mo": seg("w_mo")[None], "w_mkv": seg("w_mkv").reshape(D_MODEL, 256)[None],
            "w_q_up": seg("w_q")[:24].reshape(96, Q_RANK).T[None],
            "w_kv_up": seg("w_kv").reshape(KV_RANK, 128)[None]}


def _unpack_gathered(full, group):
    if group != "mid":
        return {n: full[k].reshape(-1, D_MODEL) for k, (n, _) in enumerate(GROUP_SEGS[group])}
    full = full[0]
    seg = lambda n: full[:, SEG_OFF[n][0]:SEG_OFF[n][0] + SEG_OFF[n][1]]
    rows = lambda n: seg(n).reshape(-1, D_MODEL)
    wq_t = seg("w_q")[:, :24].reshape(MLA_HEADS, NOPE + ROPE, Q_RANK)
    wq_t = jnp.pad(wq_t, ((0, 0), (0, HEAD_PAD - NOPE - ROPE), (0, 0))).reshape(MLA_HEADS * HEAD_PAD, Q_RANK)
    wkv = seg("w_kv").reshape(N_DEV, KV_RANK, 128).transpose(1, 0, 2).reshape(KV_RANK, N_DEV * 128)
    return {"w_in": rows("w_in"), "w_out": rows("w_out"), "w_mq": rows("w_mq"), "w_mo": rows("w_mo"),
            "w_mkv": seg("w_mkv").reshape(N_DEV, D_MODEL, 256), "w_q": wq_t, "w_kv": wkv}


def _pack_grads(gr):
    blk = lambda a: a.reshape(N_DEV, -1, D_MODEL)
    dwq = gr["w_q"].reshape(MLA_HEADS, HEAD_PAD, Q_RANK)[:, :NOPE + ROPE].reshape(N_DEV, 24, D_MODEL)
    dwq = jnp.pad(dwq, ((0, 0), (0, 8), (0, 0)))
    dwkv = gr["w_kv"].reshape(KV_RANK, N_DEV, 128).transpose(1, 0, 2).reshape(N_DEV, 16, D_MODEL)
    parts = [blk(gr["w_in"]), blk(gr["w_out"]), blk(gr["w_mq"]), blk(gr["w_mo"]),
             gr["w_mkv"].reshape(N_DEV, 256, D_MODEL), dwq, dwkv]
    return jnp.concatenate([a.astype(BF16) for a in parts], axis=1)


def _pack_small(vals):
    parts = []
    for n, r in SMALL_ROWS:
        parts.append(_pad_rows(vals[n].reshape(-1, 128), r) if n in vals else jnp.zeros((r, 128), F32))
    return jnp.concatenate(parts, axis=0)


def _unpack_small(a, shapes):
    out = {}
    for n, shape in shapes.items():
        o = SMALL_OFF[n][0]
        out[n] = a[o:o + int(np.prod(shape)) // 128].reshape(shape)
    return out


BIG_NAMES = ("ffn1_w_gate", "ffn1_w_up", "ffn1_w_down", "w_in", "w_q_up", "w_kv_up", "w_out", "w_mq", "w_mkv",
             "w_mo", "ffn2_w_gate", "ffn2_w_up", "ffn2_w_down")
SMALL_NAMES = ("ffn1_norm", "mix_norm", "q_norm", "kv_norm", "pool_w", "pool_scale", "xattn_norm", "mem_norm",
               "ffn2_norm", "final_norm")
WEIGHT_ORDER = ("ffn1_norm", "ffn1_w_gate", "ffn1_w_up", "ffn1_w_down", "mix_norm", "w_in", "q_norm", "w_q_up",
                "kv_norm", "w_kv_up", "pool_w", "pool_scale", "w_out", "xattn_norm", "mem_norm", "w_mq", "w_mkv",
                "w_mo", "ffn2_norm", "ffn2_w_gate", "ffn2_w_up", "ffn2_w_down", "final_norm")


def _rope_table():
    lane = np.arange(128)
    freqs = (1.0 / (ROPE_BASE ** (np.arange(0, ROPE, 2, dtype=np.float32) / ROPE))).astype(np.float32)
    tab = np.zeros((8, 128), np.float32)
    tab[0] = np.where(lane < ROPE, freqs[lane % (ROPE // 2)], 0.0)
    tab[1] = np.where(lane < ROPE // 2, -1.0, np.where(lane < ROPE, 1.0, 0.0))
    return jnp.asarray(tab)


def kernel(x, mem, positions, ffn1_norm, ffn1_w_gate, ffn1_w_up, ffn1_w_down, mix_norm, w_in, q_norm, w_q_up, kv_norm, w_kv_up, pool_w, pool_scale, w_out, xattn_norm, mem_norm, w_mq, w_mkv, w_mo, ffn2_norm, ffn2_w_gate, ffn2_w_up, ffn2_w_down, final_norm, loss_target, m_ffn1_norm, m_ffn1_w_gate, m_ffn1_w_up, m_ffn1_w_down, m_mix_norm, m_w_in, m_q_norm, m_w_q_up, m_kv_norm, m_w_kv_up, m_pool_w, m_pool_scale, m_w_out, m_xattn_norm, m_mem_norm, m_w_mq, m_w_mkv, m_w_mo, m_ffn2_norm, m_ffn2_w_gate, m_ffn2_w_up, m_ffn2_w_down, m_final_norm, v_ffn1_norm, v_ffn1_w_gate, v_ffn1_w_up, v_ffn1_w_down, v_mix_norm, v_w_in, v_q_norm, v_w_q_up, v_kv_norm, v_w_kv_up, v_pool_w, v_pool_scale, v_w_out, v_xattn_norm, v_mem_norm, v_w_mq, v_w_mkv, v_w_mo, v_ffn2_norm, v_ffn2_w_gate, v_ffn2_w_up, v_ffn2_w_down, v_final_norm):
    wts = dict(ffn1_norm=ffn1_norm, ffn1_w_gate=ffn1_w_gate, ffn1_w_up=ffn1_w_up, ffn1_w_down=ffn1_w_down,
               mix_norm=mix_norm, w_in=w_in, q_norm=q_norm, w_q_up=w_q_up, kv_norm=kv_norm, w_kv_up=w_kv_up,
               pool_w=pool_w, pool_scale=pool_scale, w_out=w_out, xattn_norm=xattn_norm, mem_norm=mem_norm,
               w_mq=w_mq, w_mkv=w_mkv, w_mo=w_mo, ffn2_norm=ffn2_norm, ffn2_w_gate=ffn2_w_gate,
               ffn2_w_up=ffn2_w_up, ffn2_w_down=ffn2_w_down, final_norm=final_norm)
    mom = dict(ffn1_norm=m_ffn1_norm, ffn1_w_gate=m_ffn1_w_gate, ffn1_w_up=m_ffn1_w_up, ffn1_w_down=m_ffn1_w_down,
               mix_norm=m_mix_norm, w_in=m_w_in, q_norm=m_q_norm, w_q_up=m_w_q_up, kv_norm=m_kv_norm,
               w_kv_up=m_w_kv_up, pool_w=m_pool_w, pool_scale=m_pool_scale, w_out=m_w_out, xattn_norm=m_xattn_norm,
               mem_norm=m_mem_norm, w_mq=m_w_mq, w_mkv=m_w_mkv, w_mo=m_w_mo, ffn2_norm=m_ffn2_norm,
               ffn2_w_gate=m_ffn2_w_gate, ffn2_w_up=m_ffn2_w_up, ffn2_w_down=m_ffn2_w_down, final_norm=m_final_norm)
    var = dict(ffn1_norm=v_ffn1_norm, ffn1_w_gate=v_ffn1_w_gate, ffn1_w_up=v_ffn1_w_up, ffn1_w_down=v_ffn1_w_down,
               mix_norm=v_mix_norm, w_in=v_w_in, q_norm=v_q_norm, w_q_up=v_w_q_up, kv_norm=v_kv_norm,
               w_kv_up=v_w_kv_up, pool_w=v_pool_w, pool_scale=v_pool_scale, w_out=v_w_out, xattn_norm=v_xattn_norm,
               mem_norm=v_mem_norm, w_mq=v_w_mq, w_mkv=v_w_mkv, w_mo=v_w_mo, ffn2_norm=v_ffn2_norm,
               ffn2_w_gate=v_ffn2_w_gate, ffn2_w_up=v_ffn2_w_up, ffn2_w_down=v_ffn2_w_down, final_norm=v_final_norm)

    t = x.shape[1]
    xs = x[0]
    mems = mem[0]
    target = loss_target[0]
    pos = positions.reshape(t, 1)
    row = lambda a: a.reshape(1, -1)
    rope_tab = _rope_table()

    cx, cy, cc = _coords()
    chip_idx = (2 * cx + cy).astype(jnp.int32).reshape(1)

    wb = {}
    for grp in ("ffn1", "mid", "ffn2"):
        wb[grp] = _pack_segments(wts, grp).astype(BF16)
        if grp == "ffn1":
            ag_ffn1 = _ici_start(wb["ffn1"], pos, "ag_ffn1_start", True)
    own_ffn1, land_ffn1 = _ici_wait(ag_ffn1, wb["ffn2"], "ag_ffn1_wait", True)
    full_ffn1 = _core_share(own_ffn1, land_ffn1, "ag_ffn1_share")
    fw = _unpack_gathered(full_ffn1, "ffn1")
    ag_mid = _ici_start(wb["mid"], full_ffn1, "ag_mid_start", True)
    g_ffn1, g_mix, g_q, g_kv = row(ffn1_norm), row(mix_norm), row(q_norm), row(kv_norm)
    g_x, g_mem, g_ffn2, g_fin = row(xattn_norm), row(mem_norm), row(ffn2_norm), row(final_norm)
    pool_wb = pool_w[0].astype(BF16)
    pool_sc = row(pool_scale)

    h1, n1, gate1, up1 = _ffn_fwd(xs, g_ffn1, fw["ffn1_g"], fw["ffn1_u"], fw["ffn1_d"], "ffn1_fwd", token=ag_mid[4])
    own_mid, land_mid = _ici_wait(ag_mid, h1, "ag_mid_wait", True)
    full_mid = _core_share(own_mid, land_mid, "ag_mid_share")
    fw.update(_unpack_gathered(full_mid, "mid"))
    ag_ffn2 = _ici_start(wb["ffn2"], full_mid, "ag_ffn2_start", True)
    u, z, qn, kvn, qh, kh, vh = _mix_prep(h1, g_mix, fw["w_in"], g_q, fw["w_q"], g_kv, fw["w_kv"], pos, rope_tab,
                                          token=ag_ffn2[4])
    a, lse = _attn_fwd(qh, kh, vh)
    p = _pool_fwd(z, pool_wb, pool_sc)
    memn, km, vm = _mem_kv(mems, g_mem, fw["w_mkv"])
    h2, h3, hn, qm, om = _xattn_fwd(h1, a, p, fw["w_out"], g_x, fw["w_mq"], km, vm, fw["w_mo"])
    own_ffn2, land_ffn2 = _ici_wait(ag_ffn2, h3, "ag_ffn2_wait", True)
    fw.update(_unpack_gathered(_core_share(own_ffn2, land_ffn2, "ag_ffn2_share"), "ffn2"))
    h4, n2, gate2, up2 = _ffn_fwd(h3, g_ffn2, fw["ffn2_g"], fw["ffn2_u"], fw["ffn2_d"], "ffn2_fwd")
    loss_part, dh4, dg_fin = _loss_head(h4, target, g_fin)

    def reduce_start(g8, unit):
        part = _core_reduce(g8, unit)
        return _ici_start(part, g8, "rs_" + unit + "_start", False)

    def by_device(g):
        return g.reshape(N_DEV, -1, D_MODEL)

    rs = {}
    dh3, dgate2, dup2, act2, dg_ffn2 = _ffn_bwd_data(dh4, h3, g_ffn2, gate2, up2, fw["ffn2_g"], fw["ffn2_u"],
                                                     fw["ffn2_d"], "ffn2_bwd")
    rs["ffn2_g"] = reduce_start(by_device(_tn_matmul(dgate2, n2, "ffn2_dwg", tmm=1408, out_dtype=BF16)), "ffn2_g")
    rs["ffn2_u"] = reduce_start(by_device(_tn_matmul(dup2, n2, "ffn2_dwu", tmm=1408, out_dtype=BF16,
                                                     token=rs["ffn2_g"][4])), "ffn2_u")
    rs["ffn2_d"] = reduce_start(by_device(_tn_matmul(act2, dh4, "ffn2_dwd", scale=0.5, tmm=1408, out_dtype=BF16,
                                                     token=rs["ffn2_u"][4])), "ffn2_d")
    dh2, dqm, da, dp, dkm, dvm, dg_x = _xattn_bwd(dh3, h2, qm, g_x, fw["w_mq"], km, vm, fw["w_mo"], fw["w_out"],
                                                  token=rs["ffn2_d"][4])
    gr = {}
    gr["w_mo"] = _tn_matmul(om, dh3, "dw_mo", tmm=512, out_dtype=BF16)
    gr["w_mq"] = _tn_matmul(hn, dqm, "dw_mq", tmm=512, out_dtype=BF16)
    gr["w_out"] = jnp.concatenate([_tn_matmul(a, dh2, "dw_out_a", out_dtype=BF16),
                                   _tn_matmul(p, dh2, "dw_out_p", out_dtype=BF16)], axis=0)
    gr["w_mkv"], dg_mem = _mem_kv_bwd(dkm, dvm, memn, mems, g_mem, fw["w_mkv"])
    dz_pool, d_pool_w, d_pool_sc = _pool_bwd(dp, z, pool_wb, pool_sc)
    dqh, dkh, dvh = _attn_bwd(qh, kh, vh, da, lse, _attn_delta(a, da))
    dh1, dq, dkv, dz, dg_q, dg_kv, dg_mix = _mla_bwd(dqh, dkh, dvh, z, dz_pool, h1, dh2, g_mix, fw["w_in"], g_q,
                                                     fw["w_q"], g_kv, fw["w_kv"], pos, rope_tab)
    gr["w_q"] = _tn_matmul(dq, qn, "dw_q", tmm=512, out_dtype=BF16)
    gr["w_kv"] = _tn_matmul(kvn, dkv, "dw_kv", out_dtype=BF16)
    gr["w_in"] = _tn_matmul(u, dz, "dw_in", tmm=512, out_dtype=BF16)
    g_mid = _pack_grads(gr)
    part_mid = _core_reduce(g_mid, "mid")
    got = {}
    after = part_mid
    for unit in ("ffn2_g", "ffn2_u", "ffn2_d"):
        got[unit] = _ici_wait(rs[unit], after, "rs_" + unit + "_wait", False)
        after = got[unit][1]
    rs["mid"] = _ici_start(part_mid, after, "rs_mid_start", False)
    dx, dgate1, dup1, act1, dg_ffn1 = _ffn_bwd_data(dh1, xs, g_ffn1, gate1, up1, fw["ffn1_g"], fw["ffn1_u"],
                                                    fw["ffn1_d"], "ffn1_bwd", token=rs["mid"][4])
    got["mid"] = _ici_wait(rs["mid"], dx, "rs_mid_wait", False)

    small_g = dict(ffn1_norm=dg_ffn1, mix_norm=dg_mix, q_norm=dg_q, kv_norm=dg_kv, pool_w=d_pool_w,
                   pool_scale=d_pool_sc, xattn_norm=dg_x, mem_norm=dg_mem, ffn2_norm=dg_ffn2, final_norm=dg_fin,
                   loss=loss_part)
    small_ag = _peers_start(_pack_small(small_g), got["mid"][1], "small_ag_start")
    rs["ffn1_g"] = reduce_start(by_device(_tn_matmul(dgate1, n1, "ffn1_dwg", tmm=1408, out_dtype=BF16,
                                                     token=small_ag[4])), "ffn1_g")
    _, parts = _peers_wait(small_ag, rs["ffn1_g"][4], "small_ag_wait")
    small = _adam_small(parts, _pack_small({n: wts[n] for n in SMALL_NAMES}),
                        _pack_small({n: mom[n] for n in SMALL_NAMES}), _pack_small({n: var[n] for n in SMALL_NAMES}))
    small_sum = small[0]
    loss = small_sum[SMALL_OFF["loss"][0], 0]
    shapes = {n: wts[n].shape for n in SMALL_NAMES}
    small = [_unpack_small(s, shapes) for s in small]

    rs["ffn1_u"] = reduce_start(by_device(_tn_matmul(dup1, n1, "ffn1_dwu", tmm=1408, out_dtype=BF16,
                                                     token=small_sum)), "ffn1_u")
    rs["ffn1_d"] = reduce_start(by_device(_tn_matmul(act1, dh1, "ffn1_dwd", scale=0.5, tmm=1408, out_dtype=BF16,
                                                     token=rs["ffn1_u"][4])), "ffn1_d")

    big = {}

    def adam_unit(unit, token):
        part, land = got[unit]
        res = _adam_big(part, land, _pack_unit(wts, unit), _pack_unit(mom, unit), _pack_unit(var, unit),
                        chip_idx, unit, token)
        for k, packed in enumerate(res):
            big.setdefault(k, {}).update(_unpack_unit(packed, unit))
        return res[0]

    done = rs["ffn1_d"][4]
    for unit in ("mid", "ffn2_g", "ffn2_u", "ffn2_d"):
        done = adam_unit(unit, done)
    for unit in ("ffn1_g", "ffn1_u", "ffn1_d"):
        got[unit] = _ici_wait(rs[unit], done, "rs_" + unit + "_wait", False)
        done = adam_unit(unit, done)

    outs = [loss, dx[None]]
    for k in range(4):
        for n in WEIGHT_ORDER:
            outs.append(big[k][n] if n in BIG_NAMES else small[k][n])
    return tuple(outs)
```

```python
import numpy as np

import jax
import jax.numpy as jnp
from jax import lax
from jax.experimental import pallas as pl
from jax.experimental.pallas import tpu as pltpu

F32 = jnp.float32
BF16 = jnp.bfloat16

N_DEV = 8
D_MODEL = 1024
D_FF = 2816
MLA_HEADS = 4
NOPE = 128
ROPE = 64
HEAD_PAD = 256
V_DIM = 128
Q_RANK = 256
KV_RANK = 128
POOL_WINDOWS = (2, 4, 8, 16)
POOL_CH = 128
POOL_HALO = 16
N_MEM = 256
MEM_HEADS = 4
MEM_HD = 256
ROPE_BASE = 10000.0
RMS_EPS = 1e-6
ATTN_SCALE = (NOPE + ROPE) ** -0.5
MEM_SCALE = MEM_HD ** -0.5
NEG_BIG = -1e30

ADAM_LR = 0.001
ADAM_B1 = 0.9
ADAM_B2 = 0.999
ADAM_EPS = 1e-08
ADAM_WD = 0.01
ADAM_STEP = 10
ADAM_C1 = 1.0 - ADAM_B1 ** ADAM_STEP
ADAM_C2 = 1.0 - ADAM_B2 ** ADAM_STEP

VMEM_LIMIT_BYTES = 56 * 1024 * 1024
BF16_ROWS = 16

GROUP_SEGS = {
    "ffn1": (("ffn1_g", 352), ("ffn1_u", 352), ("ffn1_d", 352)),
    "mid": (("w_in", 128), ("w_out", 128), ("w_mq", 128), ("w_mo", 128), ("w_mkv", 256), ("w_q", 32), ("w_kv", 16)),
    "ffn2": (("ffn2_g", 352), ("ffn2_u", 352), ("ffn2_d", 352)),
}
SEG_OFF = {}
GROUP_ROWS = {}
for _g, _segs in GROUP_SEGS.items():
    _o = 0
    for _n, _r in _segs:
        SEG_OFF[_n] = (_o, _r)
        _o += _r
    GROUP_ROWS[_g] = _o

SMALL_ROWS = (("ffn1_norm", 8), ("mix_norm", 8), ("q_norm", 8), ("kv_norm", 8), ("pool_w", 512), ("pool_scale", 8),
              ("xattn_norm", 8), ("mem_norm", 8), ("ffn2_norm", 8), ("final_norm", 8), ("loss", 8))
SMALL_OFF = {}
_o = 0
for _n, _r in SMALL_ROWS:
    SMALL_OFF[_n] = (_o, _r)
    _o += _r


def _cparams(**kw):
    return pltpu.CompilerParams(vmem_limit_bytes=VMEM_LIMIT_BYTES, **kw)


def _row_tile(rows, limit):
    best = None
    for cand in range(BF16_ROWS, min(rows, limit) + 1, BF16_ROWS):
        if rows % cand == 0:
            best = cand
    assert best is not None, rows
    return best


def _dot_nn(a, b):
    return lax.dot_general(a, b, (((1,), (0,)), ((), ())), preferred_element_type=F32)


def _dot_nt(a, b):
    return lax.dot_general(a, b, (((1,), (1,)), ((), ())), preferred_element_type=F32)


def _dot_tn(a, b):
    return lax.dot_general(a, b, (((0,), (0,)), ((), ())), preferred_element_type=F32)


def _rms_fwd(x, g):
    r = lax.rsqrt(jnp.mean(x * x, axis=-1, keepdims=True) + RMS_EPS)
    return x * r * g, r


def _rms_bwd(dy, x, g, r):
    xhat = x * r
    dyg = dy * g
    dx = r * (dyg - xhat * jnp.mean(dyg * xhat, axis=-1, keepdims=True))
    dg = jnp.sum(dy * xhat, axis=0, keepdims=True)
    return dx, dg


def _accumulate(ref, val, first):
    if isinstance(first, bool):
        if first:
            ref[...] = val
        else:
            ref[...] += val
        return

    @pl.when(first)
    def _():
        ref[...] = val

    @pl.when(jnp.logical_not(first))
    def _():
        ref[...] += val


def _call_after(token, body, in_specs, args, **kw):
    if token is not None:
        inner = body
        body = lambda tok_ref, *refs: inner(*refs)
        in_specs = [pl.BlockSpec((8, 128), lambda *_: (0, 0))] + list(in_specs)
        args = (token,) + tuple(args)
    return pl.pallas_call(body, in_specs=in_specs, **kw)(*args)


def _resident(shape):
    return pl.BlockSpec(shape, lambda *_: (0,) * len(shape), pipeline_mode=pl.Buffered(1))


def _rope_tables(pos_col, tab):
    ang = pos_col.astype(F32) * tab[0:1, :]
    return jnp.cos(ang), jnp.sin(ang) * tab[1:2, :]


def _swap_halves(x):
    lane = lax.broadcasted_iota(jnp.int32, x.shape, 1)
    return jnp.where((lane % 64) < 32, pltpu.roll(x, 96, 1), pltpu.roll(x, 32, 1))


def _rope_apply(x, cos_t, sin_t):
    return x * cos_t + _swap_halves(x) * sin_t


def _rope_apply_t(dy, cos_t, sin_t):
    return dy * cos_t + _swap_halves(dy * sin_t)


def _ffn_fwd(h, g, wg_t, wu_t, wd, name, token=None):
    t, d = h.shape
    f = wg_t.shape[0]
    tm, tf = min(1024, t), 256
    parts = 2 if tm % 512 == 0 else 1
    tp = tm // parts
    nf = f // tf

    def body(h_ref, g_ref, wg_ref, wu_ref, wd_ref, ho_ref, n_ref, gate_ref, up_ref, nb_sc, acc_sc):
        j = pl.program_id(1)

        @pl.when(j == 0)
        def _():
            y, _ = _rms_fwd(h_ref[...], g_ref[...])
            nb = y.astype(BF16)
            nb_sc[...] = nb
            n_ref[...] = nb
            acc_sc[...] = jnp.zeros_like(acc_sc)

        for r in range(parts):
            rows = pl.ds(r * tp, tp)
            nb = nb_sc[rows, :]
            gt = _dot_nt(nb, wg_ref[...])
            ut = _dot_nt(nb, wu_ref[...])
            gate_ref[0, rows, :] = gt.astype(BF16)
            up_ref[0, rows, :] = ut.astype(BF16)
            act = (gt * jax.nn.sigmoid(gt)) * ut
            acc_sc[rows, :] += _dot_nn(act.astype(BF16), wd_ref[...])

        @pl.when(j == nf - 1)
        def _():
            ho_ref[...] = h_ref[...] + 0.5 * acc_sc[...]

    return _call_after(
        token, body,
        [pl.BlockSpec((tm, d), lambda i, j: (i, 0)),
         pl.BlockSpec((1, d), lambda i, j: (0, 0)),
         pl.BlockSpec((tf, d), lambda i, j: (j, 0)),
         pl.BlockSpec((tf, d), lambda i, j: (j, 0)),
         pl.BlockSpec((tf, d), lambda i, j: (j, 0))],
        (h, g, wg_t, wu_t, wd),
        name=name, grid=(t // tm, nf),
        out_specs=[pl.BlockSpec((tm, d), lambda i, j: (i, 0)),
                   pl.BlockSpec((tm, d), lambda i, j: (i, 0)),
                   pl.BlockSpec((1, tm, tf), lambda i, j: (j, i, 0)),
                   pl.BlockSpec((1, tm, tf), lambda i, j: (j, i, 0))],
        out_shape=[jax.ShapeDtypeStruct((t, d), F32), jax.ShapeDtypeStruct((t, d), BF16),
                   jax.ShapeDtypeStruct((nf, t, tf), BF16), jax.ShapeDtypeStruct((nf, t, tf), BF16)],
        scratch_shapes=[pltpu.VMEM((tm, d), BF16), pltpu.VMEM((tm, d), F32)],
        compiler_params=_cparams(),
    )


def _ffn_bwd_data(dho, h, g, gate, up, wg_t, wu_t, wd, name, token=None):
    t, d = h.shape
    f = wg_t.shape[0]
    tm, tf = min(1024, t), 256
    parts = 2 if tm % 512 == 0 else 1
    tp = tm // parts
    nf = f // tf

    def body(dho_ref, h_ref, g_ref, gate_ref, up_ref, wg_ref, wu_ref, wd_ref,
             dh_ref, dgate_ref, dup_ref, act_ref, dg_ref, dhb_sc, acc_sc):
        i, j = pl.program_id(0), pl.program_id(1)

        @pl.when(j == 0)
        def _():
            dhb_sc[...] = (0.5 * dho_ref[...]).astype(BF16)
            acc_sc[...] = jnp.zeros_like(acc_sc)

        for r in range(parts):
            rows = pl.ds(r * tp, tp)
            dact = _dot_nt(dhb_sc[rows, :], wd_ref[...])
            gt = gate_ref[0, rows, :].astype(F32)
            ut = up_ref[0, rows, :].astype(F32)
            sg = jax.nn.sigmoid(gt)
            silu = gt * sg
            dgb = (dact * ut * (sg * (1.0 + gt * (1.0 - sg)))).astype(BF16)
            dub = (dact * silu).astype(BF16)
            act_ref[rows, :] = (silu * ut).astype(BF16)
            dgate_ref[rows, :] = dgb
            dup_ref[rows, :] = dub
            acc_sc[rows, :] += _dot_nn(dgb, wg_ref[...]) + _dot_nn(dub, wu_ref[...])

        @pl.when(j == nf - 1)
        def _():
            x = h_ref[...]
            gg = g_ref[...]
            _, r = _rms_fwd(x, gg)
            dx, dg = _rms_bwd(acc_sc[...], x, gg, r)
            dh_ref[...] = dho_ref[...] + dx
            _accumulate(dg_ref, dg, i == 0)

    return _call_after(
        token, body,
        [pl.BlockSpec((tm, d), lambda i, j: (i, 0)),
         pl.BlockSpec((tm, d), lambda i, j: (i, 0)),
         pl.BlockSpec((1, d), lambda i, j: (0, 0)),
         pl.BlockSpec((1, tm, tf), lambda i, j: (j, i, 0)),
         pl.BlockSpec((1, tm, tf), lambda i, j: (j, i, 0)),
         pl.BlockSpec((tf, d), lambda i, j: (j, 0)),
         pl.BlockSpec((tf, d), lambda i, j: (j, 0)),
         pl.BlockSpec((tf, d), lambda i, j: (j, 0))],
        (dho, h, g, gate, up, wg_t, wu_t, wd),
        name=name, grid=(t // tm, nf),
        out_specs=[pl.BlockSpec((tm, d), lambda i, j: (i, 0)),
                   pl.BlockSpec((tm, tf), lambda i, j: (i, j)),
                   pl.BlockSpec((tm, tf), lambda i, j: (i, j)),
                   pl.BlockSpec((tm, tf), lambda i, j: (i, j)),
                   pl.BlockSpec((1, d), lambda i, j: (0, 0))],
        out_shape=[jax.ShapeDtypeStruct((t, d), F32), jax.ShapeDtypeStruct((t, f), BF16),
                   jax.ShapeDtypeStruct((t, f), BF16), jax.ShapeDtypeStruct((t, f), BF16),
                   jax.ShapeDtypeStruct((1, d), F32)],
        scratch_shapes=[pltpu.VMEM((tm, d), BF16), pltpu.VMEM((tm, d), F32)],
        compiler_params=_cparams(),
    )


def _tn_matmul(a, b, name, scale=1.0, tmm=None, out_dtype=F32, token=None):
    t, m = a.shape
    n = b.shape[1]
    tmm = m if tmm is None else tmm
    tk = min(1024, t)
    nk = t // tk

    def product(a_ref, b_ref):
        prod = _dot_tn(a_ref[...].astype(BF16), b_ref[...].astype(BF16))
        return prod * scale if scale != 1.0 else prod

    def body_f32(a_ref, b_ref, o_ref):
        _accumulate(o_ref, product(a_ref, b_ref), pl.program_id(1) == 0)

    def body_cast(a_ref, b_ref, o_ref, acc_sc):
        k = pl.program_id(1)
        _accumulate(acc_sc, product(a_ref, b_ref), k == 0)

        @pl.when(k == nk - 1)
        def _():
            o_ref[...] = acc_sc[...].astype(out_dtype)

    direct = out_dtype == F32
    return _call_after(
        token, body_f32 if direct else body_cast,
        [pl.BlockSpec((tk, tmm), lambda i, k: (k, i)),
         pl.BlockSpec((tk, n), lambda i, k: (k, 0))],
        (a, b),
        name=name, grid=(m // tmm, nk),
        out_specs=pl.BlockSpec((tmm, n), lambda i, k: (i, 0)),
        out_shape=jax.ShapeDtypeStruct((m, n), out_dtype),
        scratch_shapes=[] if direct else [pltpu.VMEM((tmm, n), F32)],
        compiler_params=_cparams(),
    )


def _loss_head(h, target, g):
    t, d = h.shape
    tm = min(512, t)

    def body(h_ref, t_ref, g_ref, loss_ref, dh_ref, dg_ref):
        i = pl.program_id(0)
        x = h_ref[...]
        gg = g_ref[...]
        y, r = _rms_fwd(x, gg)
        err = y - t_ref[...]
        part = 0.5 * jnp.sum(jnp.mean(err * err, axis=-1, keepdims=True), axis=0, keepdims=True)
        dx, dg = _rms_bwd(err * (1.0 / d), x, gg, r)
        dh_ref[...] = dx
        _accumulate(loss_ref, jnp.broadcast_to(part, loss_ref.shape), i == 0)
        _accumulate(dg_ref, dg, i == 0)

    return pl.pallas_call(
        body, name="loss_head", grid=(t // tm,),
        in_specs=[pl.BlockSpec((tm, d), lambda i: (i, 0)),
                  pl.BlockSpec((tm, d), lambda i: (i, 0)),
                  pl.BlockSpec((1, d), lambda i: (0, 0))],
        out_specs=[pl.BlockSpec((8, 128), lambda i: (0, 0)),
                   pl.BlockSpec((tm, d), lambda i: (i, 0)),
                   pl.BlockSpec((1, d), lambda i: (0, 0))],
        out_shape=[jax.ShapeDtypeStruct((8, 128), F32), jax.ShapeDtypeStruct((t, d), F32),
                   jax.ShapeDtypeStruct((1, d), F32)],
        compiler_params=_cparams(),
    )(h, target, g)


def _mix_prep(h1, mix_norm, w_in, q_norm, wq_t, kv_norm, wkv, pos, rope_tab, token=None):
    t, d = h1.shape
    tm = min(512, t)

    def body(h_ref, gm_ref, win_ref, gq_ref, wq_ref, gkv_ref, wkv_ref, pos_ref, tab_ref,
             u_ref, z_ref, qn_ref, kvn_ref, q_ref, k_ref, v_ref):
        u, _ = _rms_fwd(h_ref[...], gm_ref[...])
        ub = u.astype(BF16)
        u_ref[...] = ub
        z = _dot_nn(ub, win_ref[...])
        z_ref[...] = z
        cos_t, sin_t = _rope_tables(pos_ref[...], tab_ref[...])
        qn, _ = _rms_fwd(z[:, 0:Q_RANK], gq_ref[...])
        qnb = qn.astype(BF16)
        qn_ref[...] = qnb
        q = _dot_nt(qnb, wq_ref[...])
        kvn, _ = _rms_fwd(z[:, Q_RANK:Q_RANK + KV_RANK], gkv_ref[...])
        kvnb = kvn.astype(BF16)
        kvn_ref[...] = kvnb
        kv = _dot_nn(kvnb, wkv_ref[...])
        k_pe = _rope_apply(z[:, Q_RANK + KV_RANK:Q_RANK + KV_RANK + 128], cos_t, sin_t)
        ones = jnp.ones((tm, V_DIM), F32)
        for hh in range(MLA_HEADS):
            b = hh * HEAD_PAD
            q_pe = _rope_apply(q[:, b + NOPE:b + HEAD_PAD], cos_t, sin_t)
            q_ref[hh] = jnp.concatenate([q[:, b:b + NOPE], q_pe], axis=-1).astype(BF16)
            k_ref[hh] = jnp.concatenate([kv[:, b:b + NOPE], k_pe], axis=-1).astype(BF16)
            v_ref[hh] = jnp.concatenate([kv[:, b + NOPE:b + HEAD_PAD], ones], axis=-1).astype(BF16)

    full = lambda shape: pl.BlockSpec(shape, lambda i: (0,) * len(shape))
    return _call_after(
        token, body,
        [pl.BlockSpec((tm, d), lambda i: (i, 0)), _resident((1, d)), _resident(w_in.shape), _resident((1, Q_RANK)),
         _resident(wq_t.shape), _resident((1, KV_RANK)), _resident(wkv.shape),
         pl.BlockSpec((tm, 1), lambda i: (i, 0)), _resident(rope_tab.shape)],
        (h1, mix_norm, w_in, q_norm, wq_t, kv_norm, wkv, pos, rope_tab),
        name="mix_prep", grid=(t // tm,),
        out_specs=[pl.BlockSpec((tm, d), lambda i: (i, 0)),
                   pl.BlockSpec((tm, d), lambda i: (i, 0)),
                   pl.BlockSpec((tm, Q_RANK), lambda i: (i, 0)),
                   pl.BlockSpec((tm, KV_RANK), lambda i: (i, 0)),
                   pl.BlockSpec((MLA_HEADS, tm, HEAD_PAD), lambda i: (0, i, 0)),
                   pl.BlockSpec((MLA_HEADS, tm, HEAD_PAD), lambda i: (0, i, 0)),
                   pl.BlockSpec((MLA_HEADS, tm, 2 * V_DIM), lambda i: (0, i, 0))],
        out_shape=[jax.ShapeDtypeStruct((t, d), BF16), jax.ShapeDtypeStruct((t, d), F32),
                   jax.ShapeDtypeStruct((t, Q_RANK), BF16), jax.ShapeDtypeStruct((t, KV_RANK), BF16),
                   jax.ShapeDtypeStruct((MLA_HEADS, t, HEAD_PAD), BF16),
                   jax.ShapeDtypeStruct((MLA_HEADS, t, HEAD_PAD), BF16),
                   jax.ShapeDtypeStruct((MLA_HEADS, t, 2 * V_DIM), BF16)],
        compiler_params=_cparams(),
    )


def _causal_mask(s):
    row = lax.broadcasted_iota(jnp.int32, s.shape, 0)
    col = lax.broadcasted_iota(jnp.int32, s.shape, 1)
    return jnp.where(col <= row, s, NEG_BIG)


def _attn_fwd(q, k, v):
    nh, t, _ = q.shape
    tq = tk = min(512, t)
    nq, nk = t // tq, t // tk

    def body(q_ref, k_ref, v_ref, o_ref, lse_ref, m_sc, acc_sc):
        i, j = pl.program_id(0), pl.program_id(1)

        @pl.when(j == 0)
        def _():
            m_sc[...] = jnp.full_like(m_sc, NEG_BIG)
            acc_sc[...] = jnp.zeros_like(acc_sc)

        def step(diagonal):
            for hh in range(nh):
                s = _dot_nt(q_ref[hh], k_ref[hh]) * ATTN_SCALE
                if diagonal:
                    s = _causal_mask(s)
                m_old = m_sc[hh]
                m_new = jnp.maximum(m_old, jnp.max(s, axis=-1, keepdims=True))
                p = jnp.exp(s - m_new).astype(BF16)
                acc_sc[hh] = jnp.exp(m_old - m_new) * acc_sc[hh] + _dot_nn(p, v_ref[hh])
                m_sc[hh] = m_new

        @pl.when(j < i)
        def _():
            step(False)

        @pl.when(j == i)
        def _():
            step(True)
            for hh in range(nh):
                acc = acc_sc[hh]
                l = acc[:, V_DIM:2 * V_DIM]
                o_ref[:, hh * V_DIM:(hh + 1) * V_DIM] = (acc[:, 0:V_DIM] / l).astype(BF16)
                lse_ref[hh] = m_sc[hh] + jnp.log(l[:, 0:1])

    kv_map = lambda i, j: (0, jnp.minimum(j, i), 0)
    return pl.pallas_call(
        body, name="attn_fwd", grid=(nq, nk),
        in_specs=[pl.BlockSpec((nh, tq, HEAD_PAD), lambda i, j: (0, i, 0)),
                  pl.BlockSpec((nh, tk, HEAD_PAD), kv_map),
                  pl.BlockSpec((nh, tk, 2 * V_DIM), kv_map)],
        out_specs=[pl.BlockSpec((tq, nh * V_DIM), lambda i, j: (i, 0)),
                   pl.BlockSpec((nh, tq, 1), lambda i, j: (0, i, 0))],
        out_shape=[jax.ShapeDtypeStruct((t, nh * V_DIM), BF16), jax.ShapeDtypeStruct((nh, t, 1), F32)],
        scratch_shapes=[pltpu.VMEM((nh, tq, 1), F32), pltpu.VMEM((nh, tq, 2 * V_DIM), F32)],
        compiler_params=_cparams(),
    )(q, k, v)


def _attn_delta(o, do):
    t, w = o.shape
    nh = w // V_DIM
    tm = min(512, t)

    def body(o_ref, do_ref, d_ref):
        prod = o_ref[...].astype(F32) * do_ref[...].astype(F32)
        for hh in range(nh):
            d_ref[hh] = jnp.sum(prod[:, hh * V_DIM:(hh + 1) * V_DIM], axis=-1, keepdims=True)

    return pl.pallas_call(
        body, name="attn_delta", grid=(t // tm,),
        in_specs=[pl.BlockSpec((tm, w), lambda i: (i, 0)), pl.BlockSpec((tm, w), lambda i: (i, 0))],
        out_specs=pl.BlockSpec((nh, tm, 1), lambda i: (0, i, 0)),
        out_shape=jax.ShapeDtypeStruct((nh, t, 1), F32),
        compiler_params=_cparams(),
    )(o, do)


ATTN_BWD_HEADS = 2


def _attn_bwd(q, k, v, do, lse, delta):
    nh, t, _ = q.shape
    hp = ATTN_BWD_HEADS
    tq = tk = min(512, t)
    nq, nk = t // tq, t // tk

    def body(q_ref, k_ref, v_ref, do_ref, lse_ref, dlt_ref, dq_ref, dk_ref, dv_ref):
        j, i = pl.program_id(1), pl.program_id(2)

        @pl.when(jnp.logical_and(j == 0, i == 0))
        def _():
            dq_ref[...] = jnp.zeros_like(dq_ref)

        def step(diagonal):
            for hh in range(hp):
                qq, kk = q_ref[hh], k_ref[hh]
                dob = do_ref[:, hh * V_DIM:(hh + 1) * V_DIM]
                s = _dot_nt(qq, kk) * ATTN_SCALE
                if diagonal:
                    s = _causal_mask(s)
                p = jnp.exp(s - lse_ref[hh])
                dpp = _dot_nt(dob, v_ref[hh])
                dsb = (p * (dpp - dlt_ref[hh]) * ATTN_SCALE).astype(BF16)
                _accumulate(dv_ref.at[hh], _dot_tn(p.astype(BF16), dob), diagonal)
                _accumulate(dk_ref.at[hh], _dot_tn(dsb, qq), diagonal)
                dq_ref[hh, pl.ds(pl.multiple_of(i * tq, tq), tq), :] += _dot_nn(dsb, kk)

        @pl.when(i > j)
        def _():
            step(False)

        @pl.when(i == j)
        def _():
            step(True)

    qmap = lambda h, j, i: (h, jnp.maximum(i, j), 0)
    return pl.pallas_call(
        body, name="attn_bwd", grid=(nh // hp, nk, nq),
        in_specs=[pl.BlockSpec((hp, tq, HEAD_PAD), qmap),
                  pl.BlockSpec((hp, tk, HEAD_PAD), lambda h, j, i: (h, j, 0)),
                  pl.BlockSpec((hp, tk, V_DIM), lambda h, j, i: (h, j, 0)),
                  pl.BlockSpec((tq, hp * V_DIM), lambda h, j, i: (jnp.maximum(i, j), h)),
                  pl.BlockSpec((hp, tq, 1), qmap),
                  pl.BlockSpec((hp, tq, 1), qmap)],
        out_specs=[pl.BlockSpec((hp, t, HEAD_PAD), lambda h, j, i: (h, 0, 0)),
                   pl.BlockSpec((hp, tk, HEAD_PAD), lambda h, j, i: (h, j, 0)),
                   pl.BlockSpec((hp, tk, V_DIM), lambda h, j, i: (h, j, 0))],
        out_shape=[jax.ShapeDtypeStruct((nh, t, HEAD_PAD), F32), jax.ShapeDtypeStruct((nh, t, HEAD_PAD), F32),
                   jax.ShapeDtypeStruct((nh, t, V_DIM), F32)],
        compiler_params=_cparams(),
    )(q, k, v, do, lse, delta)


def _pool_counts(first_token, rows, w):
    tok = lax.broadcasted_iota(jnp.int32, (rows, POOL_CH), 0) + first_token
    return jnp.minimum(tok + 1, w).astype(F32)


def _pool_centered(zbuf, g, w, i, tm):
    lanes = pl.ds(g * POOL_CH, POOL_CH)
    cur = zbuf[pl.ds(POOL_HALO, tm), lanes]
    win = cur
    for s in range(1, w):
        win = win + zbuf[pl.ds(POOL_HALO - s, tm), lanes]
    return win / _pool_counts(i * tm, tm, w) - cur


def _pool_load(zbuf, z_ref, halo_ref, i, tm):
    @pl.when(i == 0)
    def _():
        zbuf[pl.ds(0, POOL_HALO), :] = jnp.zeros((POOL_HALO, zbuf.shape[1]), F32)

    @pl.when(i > 0)
    def _():
        zbuf[pl.ds(0, POOL_HALO), :] = halo_ref[...]

    zbuf[pl.ds(POOL_HALO, tm), :] = z_ref[...]


def _pool_fwd(z, pool_w, pool_scale):
    t = z.shape[0]
    pw = len(POOL_WINDOWS) * POOL_CH
    tm = min(512, t)
    hb = tm // POOL_HALO

    def body(z_ref, halo_ref, w_ref, sc_ref, p_ref, zbuf):
        i = pl.program_id(0)
        _pool_load(zbuf, z_ref, halo_ref, i, tm)
        for g, w in enumerate(POOL_WINDOWS):
            c = _pool_centered(zbuf, g, w, i, tm)
            y = _dot_nn(c.astype(BF16), w_ref[g]) * sc_ref[:, g * POOL_CH:(g + 1) * POOL_CH]
            p_ref[:, g * POOL_CH:(g + 1) * POOL_CH] = y.astype(BF16)

    return pl.pallas_call(
        body, name="pool_fwd", grid=(t // tm,),
        in_specs=[pl.BlockSpec((tm, pw), lambda i: (i, 1)),
                  pl.BlockSpec((POOL_HALO, pw), lambda i: (jnp.maximum(i * hb - 1, 0), 1)),
                  pl.BlockSpec(pool_w.shape, lambda i: (0, 0, 0)),
                  pl.BlockSpec((1, pw), lambda i: (0, 0))],
        out_specs=pl.BlockSpec((tm, pw), lambda i: (i, 0)),
        out_shape=jax.ShapeDtypeStruct((t, pw), BF16),
        scratch_shapes=[pltpu.VMEM((POOL_HALO + tm, pw), F32)],
        compiler_params=_cparams(),
    )(z, z, pool_w, pool_scale)


def _pool_bwd(dp, z, pool_w, pool_scale):
    t = z.shape[0]
    ng = len(POOL_WINDOWS)
    pw = ng * POOL_CH
    tm = min(512, t)
    hb = tm // POOL_HALO
    nt = t // tm

    def body(dp_ref, dpn_ref, z_ref, halo_ref, w_ref, sc_ref, dz_ref, dw_ref, dsc_ref, zbuf, dbuf):
        i = pl.program_id(0)
        _pool_load(zbuf, z_ref, halo_ref, i, tm)

        @pl.when(i == 0)
        def _():
            dw_ref[...] = jnp.zeros_like(dw_ref)
            dsc_ref[...] = jnp.zeros_like(dsc_ref)

        nxt_ok = (i < nt - 1).astype(F32)
        for g, w in enumerate(POOL_WINDOWS):
            lanes = pl.ds(g * POOL_CH, POOL_CH)
            cols = slice(g * POOL_CH, (g + 1) * POOL_CH)
            sc = sc_ref[:, cols]
            wg = w_ref[g]
            c = _pool_centered(zbuf, g, w, i, tm).astype(BF16)
            ypre = _dot_nn(c, wg)
            dpg = dp_ref[:, cols].astype(F32)
            dsc_ref[:, cols] += jnp.sum(dpg * ypre, axis=0, keepdims=True)
            dyb = (dpg * sc).astype(BF16)
            dw_ref[g] += _dot_tn(c, dyb)
            dd = _dot_nt(dyb, wg)
            dyn = (dpn_ref[:, cols].astype(F32) * sc).astype(BF16)
            ddn = _dot_nt(dyn, wg) * nxt_ok
            dbuf[pl.ds(0, tm), lanes] = dd / _pool_counts(i * tm, tm, w)
            dbuf[pl.ds(tm, POOL_HALO), lanes] = ddn / _pool_counts((i + 1) * tm, POOL_HALO, w)
            acc = -dd
            for s in range(w):
                acc = acc + dbuf[pl.ds(s, tm), lanes]
            dz_ref[:, cols] = acc

    return pl.pallas_call(
        body, name="pool_bwd", grid=(nt,),
        in_specs=[pl.BlockSpec((tm, pw), lambda i: (i, 0)),
                  pl.BlockSpec((POOL_HALO, pw), lambda i: (jnp.minimum((i + 1) * hb, t // POOL_HALO - 1), 0)),
                  pl.BlockSpec((tm, pw), lambda i: (i, 1)),
                  pl.BlockSpec((POOL_HALO, pw), lambda i: (jnp.maximum(i * hb - 1, 0), 1)),
                  pl.BlockSpec(pool_w.shape, lambda i: (0, 0, 0)),
                  pl.BlockSpec((1, pw), lambda i: (0, 0))],
        out_specs=[pl.BlockSpec((tm, pw), lambda i: (i, 0)),
                   pl.BlockSpec((ng, POOL_CH, POOL_CH), lambda i: (0, 0, 0)),
                   pl.BlockSpec((1, pw), lambda i: (0, 0))],
        out_shape=[jax.ShapeDtypeStruct((t, pw), F32), jax.ShapeDtypeStruct((ng, POOL_CH, POOL_CH), F32),
                   jax.ShapeDtypeStruct((1, pw), F32)],
        scratch_shapes=[pltpu.VMEM((POOL_HALO + tm, pw), F32), pltpu.VMEM((tm + POOL_HALO, pw), F32)],
        compiler_params=_cparams(),
    )(dp, dp, z, z, pool_w, pool_scale)


def _mla_bwd(dq_h, dk_h, dv_h, z, dz_pool, h1, dh2, mix_norm, w_in, q_norm, wq_t, kv_norm, wkv, pos, rope_tab):
    t, d = h1.shape
    tm = min(512, t)

    def body(dqh_ref, dkh_ref, dvh_ref, z_ref, dzp_ref, h_ref, dh2_ref, gm_ref, win_ref, gq_ref, wq_ref, gkv_ref,
             wkv_ref, pos_ref, tab_ref, dh1_ref, dq_ref, dkv_ref, dz_ref, dgq_ref, dgkv_ref, dgm_ref):
        i = pl.program_id(0)
        first = i == 0
        cos_t, sin_t = _rope_tables(pos_ref[...], tab_ref[...])
        dq_parts, dkv_parts = [], []
        dk_pe = jnp.zeros((tm, 128), F32)
        for hh in range(MLA_HEADS):
            dqh = dqh_ref[hh]
            dq_parts += [dqh[:, 0:NOPE], _rope_apply_t(dqh[:, NOPE:HEAD_PAD], cos_t, sin_t)]
            dkh = dkh_ref[hh]
            dkv_parts += [dkh[:, 0:NOPE], dvh_ref[hh]]
            dk_pe = dk_pe + dkh[:, NOPE:HEAD_PAD]
        dqb = jnp.concatenate(dq_parts, axis=-1).astype(BF16)
        dkvb = jnp.concatenate(dkv_parts, axis=-1).astype(BF16)
        dq_ref[...] = dqb
        dkv_ref[...] = dkvb
        z = z_ref[...]
        c_q = z[:, 0:Q_RANK]
        gq = gq_ref[...]
        _, rq = _rms_fwd(c_q, gq)
        dcq, dgq = _rms_bwd(_dot_nn(dqb, wq_ref[...]), c_q, gq, rq)
        c_kv = z[:, Q_RANK:Q_RANK + KV_RANK]
        gkv = gkv_ref[...]
        _, rkv = _rms_fwd(c_kv, gkv)
        dckv, dgkv = _rms_bwd(_dot_nt(dkvb, wkv_ref[...]), c_kv, gkv, rkv)
        dkr = _rope_apply_t(dk_pe, cos_t, sin_t)
        dzb = jnp.concatenate([dcq, dckv, dkr, dzp_ref[...]], axis=-1).astype(BF16)
        dz_ref[...] = dzb
        x = h_ref[...]
        gm = gm_ref[...]
        _, rm = _rms_fwd(x, gm)
        dx, dgm = _rms_bwd(_dot_nt(dzb, win_ref[...]), x, gm, rm)
        dh1_ref[...] = dh2_ref[...] + dx
        _accumulate(dgq_ref, dgq, first)
        _accumulate(dgkv_ref, dgkv, first)
        _accumulate(dgm_ref, dgm, first)

    full = lambda shape: pl.BlockSpec(shape, lambda i: (0,) * len(shape))
    row = lambda w: pl.BlockSpec((tm, w), lambda i: (i, 0))
    head = lambda w: pl.BlockSpec((MLA_HEADS, tm, w), lambda i: (0, i, 0))
    pw = len(POOL_WINDOWS) * POOL_CH
    return pl.pallas_call(
        body, name="mla_bwd", grid=(t // tm,),
        in_specs=[head(HEAD_PAD), head(HEAD_PAD), head(V_DIM), row(d), row(pw), row(d), row(d),
                  _resident((1, d)), _resident(w_in.shape), _resident((1, Q_RANK)), _resident(wq_t.shape),
                  _resident((1, KV_RANK)), _resident(wkv.shape), row(1), _resident(rope_tab.shape)],
        out_specs=[row(d), row(d), row(d), row(d), full((1, Q_RANK)), full((1, KV_RANK)), full((1, d))],
        out_shape=[jax.ShapeDtypeStruct((t, d), F32), jax.ShapeDtypeStruct((t, d), BF16),
                   jax.ShapeDtypeStruct((t, d), BF16), jax.ShapeDtypeStruct((t, d), BF16),
                   jax.ShapeDtypeStruct((1, Q_RANK), F32), jax.ShapeDtypeStruct((1, KV_RANK), F32),
                   jax.ShapeDtypeStruct((1, d), F32)],
        compiler_params=_cparams(),
    )(dq_h, dk_h, dv_h, z, dz_pool, h1, dh2, mix_norm, w_in, q_norm, wq_t, kv_norm, wkv, pos, rope_tab)


def _mem_kv(mem, mem_norm, wmkv):
    n, d = mem.shape

    def body(mem_ref, g_ref, w_ref, memn_ref, k_ref, v_ref):
        y, _ = _rms_fwd(mem_ref[...], g_ref[...])
        yb = y.astype(BF16)
        memn_ref[...] = yb
        for hh in range(MEM_HEADS):
            k_ref[hh] = _dot_nn(yb, w_ref[hh]).astype(BF16)
            v_ref[hh] = _dot_nn(yb, w_ref[MEM_HEADS + hh]).astype(BF16)

    return pl.pallas_call(
        body, name="mem_kv",
        out_shape=[jax.ShapeDtypeStruct((n, d), BF16), jax.ShapeDtypeStruct((MEM_HEADS, n, MEM_HD), BF16),
                   jax.ShapeDtypeStruct((MEM_HEADS, n, MEM_HD), BF16)],
        compiler_params=_cparams(),
    )(mem, mem_norm, wmkv)


def _mem_softmax(qb, km):
    s = _dot_nt(qb, km) * MEM_SCALE
    e = jnp.exp(s - jnp.max(s, axis=-1, keepdims=True))
    return e / jnp.sum(e, axis=-1, keepdims=True)


def _xattn_fwd(h1, a, p, w_out, g, wmq, km, vm, wmo):
    t, d = h1.shape
    tm = min(512, t)
    half = a.shape[1]

    def body(h_ref, a_ref, p_ref, wo_ref, g_ref, wmq_ref, km_ref, vm_ref, wmo_ref,
             h2_ref, h3_ref, hn_ref, q_ref, o_ref):
        h2 = h_ref[...] + _dot_nn(a_ref[...], wo_ref[0:half, :]) + _dot_nn(p_ref[...], wo_ref[half:2 * half, :])
        h2_ref[...] = h2
        hn, _ = _rms_fwd(h2, g_ref[...])
        hnb = hn.astype(BF16)
        hn_ref[...] = hnb
        qb = _dot_nn(hnb, wmq_ref[...]).astype(BF16)
        q_ref[...] = qb
        outs = []
        for hh in range(MEM_HEADS):
            pr = _mem_softmax(qb[:, hh * MEM_HD:(hh + 1) * MEM_HD], km_ref[hh])
            outs.append(_dot_nn(pr.astype(BF16), vm_ref[hh]))
        ob = jnp.concatenate(outs, axis=-1).astype(BF16)
        o_ref[...] = ob
        h3_ref[...] = h2 + _dot_nn(ob, wmo_ref[...])

    full = lambda shape: pl.BlockSpec(shape, lambda i: (0,) * len(shape))
    row = lambda w: pl.BlockSpec((tm, w), lambda i: (i, 0))
    return pl.pallas_call(
        body, name="xattn_fwd", grid=(t // tm,),
        in_specs=[row(d), row(half), row(half), _resident(w_out.shape), _resident((1, d)), _resident(wmq.shape),
                  _resident(km.shape), _resident(vm.shape), _resident(wmo.shape)],
        out_specs=[row(d), row(d), row(d), row(d), row(d)],
        out_shape=[jax.ShapeDtypeStruct((t, d), F32), jax.ShapeDtypeStruct((t, d), F32),
                   jax.ShapeDtypeStruct((t, d), BF16), jax.ShapeDtypeStruct((t, d), BF16),
                   jax.ShapeDtypeStruct((t, d), BF16)],
        compiler_params=_cparams(),
    )(h1, a, p, w_out, g, wmq, km, vm, wmo)


def _xattn_bwd(dh3, h2, qm, g, wmq, km, vm, wmo, w_out, token=None):
    t, d = h2.shape
    tm = min(512, t)
    half = d // 2

    def body(dh3_ref, h2_ref, q_ref, g_ref, wmq_ref, km_ref, vm_ref, wmo_ref, wo_ref,
             dh2_ref, dq_ref, da_ref, dp_ref, dk_ref, dv_ref, dg_ref):
        i = pl.program_id(0)
        first = i == 0

        @pl.when(first)
        def _():
            dk_ref[...] = jnp.zeros_like(dk_ref)
            dv_ref[...] = jnp.zeros_like(dv_ref)

        dh3 = dh3_ref[...]
        dob = _dot_nt(dh3.astype(BF16), wmo_ref[...]).astype(BF16)
        qb = q_ref[...]
        dq_parts = []
        for hh in range(MEM_HEADS):
            cols = slice(hh * MEM_HD, (hh + 1) * MEM_HD)
            kk, vv = km_ref[hh], vm_ref[hh]
            pr = _mem_softmax(qb[:, cols], kk)
            doh = dob[:, cols]
            dv_ref[hh] += _dot_tn(pr.astype(BF16), doh)
            dpp = _dot_nt(doh, vv)
            dsb = (pr * (dpp - jnp.sum(dpp * pr, axis=-1, keepdims=True)) * MEM_SCALE).astype(BF16)
            dq_parts.append(_dot_nn(dsb, kk))
            dk_ref[hh] += _dot_tn(dsb, qb[:, cols])
        dqb = jnp.concatenate(dq_parts, axis=-1).astype(BF16)
        dq_ref[...] = dqb
        x = h2_ref[...]
        gg = g_ref[...]
        _, r = _rms_fwd(x, gg)
        dx, dg = _rms_bwd(_dot_nt(dqb, wmq_ref[...]), x, gg, r)
        dh2 = dh3 + dx
        dh2_ref[...] = dh2
        dap = _dot_nt(dh2.astype(BF16), wo_ref[...])
        da_ref[...] = dap[:, 0:half].astype(BF16)
        dp_ref[...] = dap[:, half:d].astype(BF16)
        _accumulate(dg_ref, dg, first)

    full = lambda shape: pl.BlockSpec(shape, lambda i: (0,) * len(shape))
    row = lambda w: pl.BlockSpec((tm, w), lambda i: (i, 0))
    return _call_after(
        token, body,
        [row(d), row(d), row(d), _resident((1, d)), _resident(wmq.shape), _resident(km.shape), _resident(vm.shape),
         _resident(wmo.shape), _resident(w_out.shape)],
        (dh3, h2, qm, g, wmq, km, vm, wmo, w_out),
        name="xattn_bwd", grid=(t // tm,),
        out_specs=[row(d), row(d), row(half), row(half), full(km.shape), full(vm.shape), full((1, d))],
        out_shape=[jax.ShapeDtypeStruct((t, d), F32), jax.ShapeDtypeStruct((t, d), BF16),
                   jax.ShapeDtypeStruct((t, half), BF16), jax.ShapeDtypeStruct((t, half), BF16),
                   jax.ShapeDtypeStruct(km.shape, F32), jax.ShapeDtypeStruct(vm.shape, F32),
                   jax.ShapeDtypeStruct((1, d), F32)],
        compiler_params=_cparams(),
    )


def _mem_kv_bwd(dkm, dvm, memn, mem, mem_norm, wmkv):
    n, d = mem.shape

    def body(dk_ref, dv_ref, memn_ref, mem_ref, g_ref, w_ref, dw_ref, dg_ref):
        memn = memn_ref[...]
        dmemn = jnp.zeros((n, d), F32)
        for s in range(2 * MEM_HEADS):
            src = dk_ref[s] if s < MEM_HEADS else dv_ref[s - MEM_HEADS]
            db = src.astype(BF16)
            dw_ref[s] = _dot_tn(memn, db)
            dmemn = dmemn + _dot_nt(db, w_ref[s])
        x = mem_ref[...]
        gg = g_ref[...]
        _, r = _rms_fwd(x, gg)
        _, dg = _rms_bwd(dmemn, x, gg, r)
        dg_ref[...] = dg

    return pl.pallas_call(
        body, name="mem_kv_bwd",
        out_shape=[jax.ShapeDtypeStruct(wmkv.shape, F32), jax.ShapeDtypeStruct((1, d), F32)],
        compiler_params=_cparams(),
    )(dkm, dvm, memn, mem, mem_norm, wmkv)


MESH_ID = pl.DeviceIdType.MESH
ANY = pl.BlockSpec(memory_space=pl.ANY)


def _coords():
    return lax.axis_index("x"), lax.axis_index("y"), lax.axis_index("c")


def _other_chips(x, y):
    return [(1 - x, y), (x, 1 - y), (1 - x, 1 - y)]


def _core_reduce(g, tag):
    _, r, w = g.shape

    def body(g_ref, part_ref, own_sc, recv_sc, send_sems, recv_sems, local_sems):
        x, y, c = _coords()
        sent, local = [], []
        for chip in range(4):
            sent.append(pltpu.make_async_remote_copy(
                src_ref=g_ref.at[2 * chip + (1 - c)], dst_ref=recv_sc.at[chip],
                send_sem=send_sems.at[chip], recv_sem=recv_sems.at[chip],
                device_id=(x, y, 1 - c), device_id_type=MESH_ID))
            local.append(pltpu.make_async_copy(g_ref.at[2 * chip + c], own_sc.at[chip], local_sems.at[chip]))
        for cp in sent + local:
            cp.start()
        for chip in range(4):
            local[chip].wait()
            sent[chip].wait_recv()
            part_ref[chip] = (own_sc[chip].astype(F32) + recv_sc[chip].astype(F32)).astype(part_ref.dtype)
        for cp in sent:
            cp.wait_send()

    return pl.pallas_call(
        body, name="core_reduce_" + tag,
        out_shape=jax.ShapeDtypeStruct((4, r, w), g.dtype),
        in_specs=[ANY], out_specs=pl.BlockSpec(memory_space=pltpu.VMEM),
        scratch_shapes=[pltpu.VMEM((4, r, w), g.dtype), pltpu.VMEM((4, r, w), g.dtype),
                        pltpu.SemaphoreType.DMA((4,)), pltpu.SemaphoreType.DMA((4,)), pltpu.SemaphoreType.DMA((4,))],
        compiler_params=_cparams(),
    )(g)


HBM_SPEC = pl.BlockSpec(memory_space=pltpu.HBM)
SEM_SPEC = pl.BlockSpec(memory_space=pltpu.SEMAPHORE)
SPLIT_EFFECT = pltpu.SideEffectType.DATAFLOW_SIDE_EFFECTING


def _ici_refs(gather, src_ref, land_ref, j, px, py, slot_chip, c):
    if gather:
        return src_ref, land_ref.at[:, 4 * slot_chip[0] + 2 * slot_chip[1] + c]
    return src_ref.at[2 * px + py], land_ref.at[j]


def _ici_start(src, after, name, gather):
    r, w = src.shape[-2:]
    land_shape = (src.shape[0], N_DEV, r, w) if gather else (3, r, w)

    def body(src_ref, land_ref, after_ref, send_sems, recv_sems, src_thru, land_thru, token):
        x, y, c = _coords()
        for j, (px, py) in enumerate(_other_chips(x, y)):
            s_ref, d_ref = _ici_refs(gather, src_ref, land_ref, j, px, py, (x, y), c)
            pltpu.make_async_remote_copy(
                src_ref=s_ref, dst_ref=d_ref, send_sem=send_sems.at[j], recv_sem=recv_sems.at[j],
                device_id=(px, py, c), device_id_type=MESH_ID).start()
        token[...] = jnp.zeros_like(token)

    return pl.pallas_call(
        body, name=name,
        out_shape=(pltpu.SemaphoreType.DMA((3,)), pltpu.SemaphoreType.DMA((3,)), pltpu.HBM(src.shape, src.dtype),
                   pltpu.HBM(land_shape, src.dtype), jax.ShapeDtypeStruct((8, 128), F32)),
        in_specs=(HBM_SPEC, HBM_SPEC, ANY),
        out_specs=(SEM_SPEC, SEM_SPEC, HBM_SPEC, HBM_SPEC, pl.BlockSpec(memory_space=pltpu.VMEM)),
        input_output_aliases={0: 2, 1: 3},
        compiler_params=pltpu.CompilerParams(has_side_effects=SPLIT_EFFECT),
    )(pltpu.with_memory_space_constraint(src, pltpu.HBM),
      pltpu.with_memory_space_constraint(lax.empty(land_shape, src.dtype), pltpu.HBM), after)


def _ici_wait(started, after, name, gather):
    send_sems, recv_sems, src_thru, land_thru, _ = started

    def body(src_ref, land_ref, send_sems, recv_sems, after_ref, src_dead, got_ref):
        x, y, c = _coords()
        for j, (px, py) in enumerate(_other_chips(x, y)):
            s_ref, d_ref = _ici_refs(gather, src_ref, land_ref, j, px, py, (px, py), c)
            copy = pltpu.make_async_remote_copy(
                src_ref=s_ref, dst_ref=d_ref, send_sem=send_sems.at[j], recv_sem=recv_sems.at[j],
                device_id=(px, py, c), device_id_type=MESH_ID)
            copy.wait_send()
            copy.wait_recv()

    return pl.pallas_call(
        body, name=name,
        out_shape=(pltpu.HBM(src_thru.shape, src_thru.dtype), pltpu.HBM(land_thru.shape, land_thru.dtype)),
        in_specs=(HBM_SPEC, HBM_SPEC, SEM_SPEC, SEM_SPEC, ANY),
        out_specs=(HBM_SPEC, HBM_SPEC), input_output_aliases={0: 0, 1: 1},
        compiler_params=pltpu.CompilerParams(has_side_effects=SPLIT_EFFECT),
    )(src_thru, land_thru, send_sems, recv_sems, after)


def _peer(k, x, y, c):
    return x ^ ((k >> 2) & 1), y ^ ((k >> 1) & 1), c ^ (k & 1)


def _peers_start(src, after, name):
    r, w = src.shape
    x, y, c = _coords()
    land = lax.dynamic_update_slice(jnp.zeros((N_DEV, r, w), src.dtype), src[None], (4 * x + 2 * y + c, 0, 0))

    def body(src_ref, land_ref, after_ref, send_sems, recv_sems, src_thru, land_thru, token):
        x, y, c = _coords()
        for k in range(1, N_DEV):
            pltpu.make_async_remote_copy(
                src_ref=src_ref, dst_ref=land_ref.at[4 * x + 2 * y + c],
                send_sem=send_sems.at[k - 1], recv_sem=recv_sems.at[k - 1],
                device_id=_peer(k, x, y, c), device_id_type=MESH_ID).start()
        token[...] = jnp.zeros_like(token)

    return pl.pallas_call(
        body, name=name,
        out_shape=(pltpu.SemaphoreType.DMA((N_DEV - 1,)), pltpu.SemaphoreType.DMA((N_DEV - 1,)),
                   pltpu.HBM(src.shape, src.dtype), pltpu.HBM(land.shape, src.dtype),
                   jax.ShapeDtypeStruct((8, 128), F32)),
        in_specs=(HBM_SPEC, HBM_SPEC, ANY),
        out_specs=(SEM_SPEC, SEM_SPEC, HBM_SPEC, HBM_SPEC, pl.BlockSpec(memory_space=pltpu.VMEM)),
        input_output_aliases={0: 2, 1: 3},
        compiler_params=pltpu.CompilerParams(has_side_effects=SPLIT_EFFECT),
    )(pltpu.with_memory_space_constraint(src, pltpu.HBM), pltpu.with_memory_space_constraint(land, pltpu.HBM), after)


def _peers_wait(started, after, name):
    send_sems, recv_sems, src_thru, land_thru, _ = started

    def body(src_ref, land_ref, send_sems, recv_sems, after_ref, src_dead, got_ref):
        x, y, c = _coords()
        for k in range(1, N_DEV):
            px, py, pc = _peer(k, x, y, c)
            copy = pltpu.make_async_remote_copy(
                src_ref=src_ref, dst_ref=land_ref.at[4 * px + 2 * py + pc],
                send_sem=send_sems.at[k - 1], recv_sem=recv_sems.at[k - 1],
                device_id=(px, py, pc), device_id_type=MESH_ID)
            copy.wait_send()
            copy.wait_recv()

    return pl.pallas_call(
        body, name=name,
        out_shape=(pltpu.HBM(src_thru.shape, src_thru.dtype), pltpu.HBM(land_thru.shape, land_thru.dtype)),
        in_specs=(HBM_SPEC, HBM_SPEC, SEM_SPEC, SEM_SPEC, ANY),
        out_specs=(HBM_SPEC, HBM_SPEC), input_output_aliases={0: 0, 1: 1},
        compiler_params=pltpu.CompilerParams(has_side_effects=SPLIT_EFFECT),
    )(src_thru, land_thru, send_sems, recv_sems, after)


def _core_share(own, gathered, name):
    def body(own_ref, gin_ref, out_ref, stage, send_sems, recv_sems, local_sem):
        x, y, c = _coords()
        sibling = (x, y, 1 - c)
        chips = [(x, y)] + _other_chips(x, y)
        stage_in = pltpu.make_async_copy(own_ref, stage, local_sem)
        stage_in.start()
        sent, arriving = [], []
        for k, (px, py) in enumerate(chips):
            slot = out_ref.at[:, 4 * px + 2 * py + c]
            sent.append(pltpu.make_async_remote_copy(
                src_ref=own_ref if k == 0 else slot, dst_ref=slot,
                send_sem=send_sems.at[k], recv_sem=recv_sems.at[k], device_id=sibling, device_id_type=MESH_ID))
            arriving.append(pltpu.make_async_remote_copy(
                src_ref=own_ref, dst_ref=out_ref.at[:, 4 * px + 2 * py + (1 - c)],
                send_sem=send_sems.at[k], recv_sem=recv_sems.at[k], device_id=sibling, device_id_type=MESH_ID))
        for cp in sent:
            cp.start()
        stage_in.wait()
        stage_out = pltpu.make_async_copy(stage, out_ref.at[:, 4 * x + 2 * y + c], local_sem)
        stage_out.start()
        for cp in arriving:
            cp.wait_recv()
        for cp in sent:
            cp.wait_send()
        stage_out.wait()

    return pl.pallas_call(
        body, name=name,
        out_shape=jax.ShapeDtypeStruct(gathered.shape, own.dtype),
        in_specs=[ANY, ANY], out_specs=ANY, input_output_aliases={1: 0},
        scratch_shapes=[pltpu.VMEM(own.shape, own.dtype), pltpu.SemaphoreType.DMA((4,)),
                        pltpu.SemaphoreType.DMA((4,)), pltpu.SemaphoreType.DMA],
    )(own, gathered)


def _adamw(w, g, m, v):
    m = ADAM_B1 * m + (1.0 - ADAM_B1) * g
    v = ADAM_B2 * v + (1.0 - ADAM_B2) * (g * g)
    m_hat = m / ADAM_C1
    v_hat = v / ADAM_C2
    delta = -ADAM_LR * (m_hat / (jnp.sqrt(v_hat) + ADAM_EPS) + ADAM_WD * w)
    return delta, m, v


def _adam_big(part, land, w, m, v, chip_idx, tag, token):
    r, wd = w.shape
    tr, tw = _row_tile(r, 1024), 256

    def body(s_ref, tok_ref, p_ref, l_ref, w_ref, m_ref, v_ref, g_ref, d_ref, mo_ref, vo_ref):
        g = p_ref[0].astype(F32)
        for j in range(3):
            g = g + l_ref[j].astype(F32)
        delta, mn, vn = _adamw(w_ref[...], g, m_ref[...], v_ref[...])
        g_ref[...] = g
        d_ref[...] = delta
        mo_ref[...] = mn
        vo_ref[...] = vn

    row = pl.BlockSpec((tr, tw), lambda i, j, s: (i, j))
    return pl.pallas_call(
        body, name="adam_big_" + tag,
        grid_spec=pltpu.PrefetchScalarGridSpec(
            num_scalar_prefetch=1, grid=(r // tr, wd // tw),
            in_specs=[pl.BlockSpec((8, 128), lambda i, j, s: (0, 0)),
                      pl.BlockSpec((1, tr, tw), lambda i, j, s: (s[0], i, j)),
                      pl.BlockSpec((3, tr, tw), lambda i, j, s: (0, i, j)), row, row, row],
            out_specs=[row, row, row, row]),
        out_shape=[jax.ShapeDtypeStruct((r, wd), F32)] * 4,
        compiler_params=_cparams(),
    )(chip_idx, token, part, land, w, m, v)


def _adam_small(parts, w, m, v):
    _, r, wd = parts.shape

    def body(p_ref, w_ref, m_ref, v_ref, g_ref, d_ref, mo_ref, vo_ref):
        g = p_ref[0]
        for k in range(1, N_DEV):
            g = g + p_ref[k]
        delta, mn, vn = _adamw(w_ref[...], g, m_ref[...], v_ref[...])
        g_ref[...] = g
        d_ref[...] = delta
        mo_ref[...] = mn
        vo_ref[...] = vn

    return pl.pallas_call(
        body, name="adam_small",
        out_shape=[jax.ShapeDtypeStruct((r, wd), F32)] * 4,
        compiler_params=_cparams(),
    )(parts, w, m, v)


def _pad_rows(a, rows):
    return jnp.pad(a, ((0, rows - a.shape[0]), (0, 0)))


def _pad_w_in(w):
    cut = Q_RANK + KV_RANK + ROPE
    return jnp.concatenate([w[:, :cut], jnp.zeros((w.shape[0], 64), w.dtype), w[:, cut:]], axis=1)


def _unpad_w_in(w):
    cut = Q_RANK + KV_RANK + ROPE
    return jnp.concatenate([w[:, :cut], w[:, cut + 64:]], axis=1)


def _pack_mid(p):
    parts = [_pad_w_in(p["w_in"][0]), p["w_out"][0], p["w_mq"][0], p["w_mo"][0],
             p["w_mkv"][0].reshape(256, D_MODEL),
             _pad_rows(p["w_q_up"][0].T.reshape(24, D_MODEL), 32),
             p["w_kv_up"][0].reshape(16, D_MODEL)]
    return jnp.concatenate(parts, axis=0)


def _pack_segments(p, group):
    if group == "mid":
        return _pack_mid(p)[None]
    return jnp.stack([p[group + "_w_gate"][0].T, p[group + "_w_up"][0].T, p[group + "_w_down"][0]])


UNIT_WEIGHT = {"ffn1_g": ("ffn1_w_gate", True), "ffn1_u": ("ffn1_w_up", True), "ffn1_d": ("ffn1_w_down", False),
               "ffn2_g": ("ffn2_w_gate", True), "ffn2_u": ("ffn2_w_up", True), "ffn2_d": ("ffn2_w_down", False)}


def _pack_unit(p, unit):
    if unit == "mid":
        return _pack_mid(p)
    name, transposed = UNIT_WEIGHT[unit]
    return p[name][0].T if transposed else p[name][0]


def _unpack_unit(a, unit):
    if unit != "mid":
        name, transposed = UNIT_WEIGHT[unit]
        return {name: (a.T if transposed else a)[None]}
    seg = lambda n: a[SEG_OFF[n][0]:SEG_OFF[n][0] + SEG_OFF[n][1]]
    return {"w_in": _unpad_w_in(seg("w_in"))[None], "w_out": seg("w_out")[None], "w_mq": seg("w_mq")[None],
            "w_mo": seg("w_mo")[None], "w_mkv": seg("w_mkv").reshape(D_MODEL, 256)[None],
            "w_q_up": seg("w_q")[:24].reshape(96, Q_RANK).T[None],
            "w_kv_up": seg("w_kv").reshape(KV_RANK, 128)[None]}


def _unpack_gathered(full, group):
    if group != "mid":
        return {n: full[k].reshape(-1, D_MODEL) for k, (n, _) in enumerate(GROUP_SEGS[group])}
    full = full[0]
    seg = lambda n: full[:, SEG_OFF[n][0]:SEG_OFF[n][0] + SEG_OFF[n][1]]
    rows = lambda n: seg(n).reshape(-1, D_MODEL)
    wq_t = seg("w_q")[:, :24].reshape(MLA_HEADS, NOPE + ROPE, Q_RANK)
    wq_t = jnp.pad(wq_t, ((0, 0), (0, HEAD_PAD - NOPE - ROPE), (0, 0))).reshape(MLA_HEADS * HEAD_PAD, Q_RANK)
    wkv = seg("w_kv").reshape(N_DEV, KV_RANK, 128).transpose(1, 0, 2).reshape(KV_RANK, N_DEV * 128)
    return {"w_in": rows("w_in"), "w_out": rows("w_out"), "w_mq": rows("w_mq"), "w_mo": rows("w_mo"),
            "w_mkv": seg("w_mkv").reshape(N_DEV, D_MODEL, 256), "w_q": wq_t, "w_kv": wkv}


def _pack_grads(gr):
    blk = lambda a: a.reshape(N_DEV, -1, D_MODEL)
    dwq = gr["w_q"].reshape(MLA_HEADS, HEAD_PAD, Q_RANK)[:, :NOPE + ROPE].reshape(N_DEV, 24, D_MODEL)
    dwq = jnp.pad(dwq, ((0, 0), (0, 8), (0, 0)))
    dwkv = gr["w_kv"].reshape(KV_RANK, N_DEV, 128).transpose(1, 0, 2).reshape(N_DEV, 16, D_MODEL)
    parts = [blk(gr["w_in"]), blk(gr["w_out"]), blk(gr["w_mq"]), blk(gr["w_mo"]),
             gr["w_mkv"].reshape(N_DEV, 256, D_MODEL), dwq, dwkv]
    return jnp.concatenate([a.astype(BF16) for a in parts], axis=1)


def _pack_small(vals):
    parts = []
    for n, r in SMALL_ROWS:
        parts.append(_pad_rows(vals[n].reshape(-1, 128), r) if n in vals else jnp.zeros((r, 128), F32))
    return jnp.concatenate(parts, axis=0)


def _unpack_small(a, shapes):
    out = {}
    for n, shape in shapes.items():
        o = SMALL_OFF[n][0]
        out[n] = a[o:o + int(np.prod(shape)) // 128].reshape(shape)
    return out


BIG_NAMES = ("ffn1_w_gate", "ffn1_w_up", "ffn1_w_down", "w_in", "w_q_up", "w_kv_up", "w_out", "w_mq", "w_mkv",
             "w_mo", "ffn2_w_gate", "ffn2_w_up", "ffn2_w_down")
SMALL_NAMES = ("ffn1_norm", "mix_norm", "q_norm", "kv_norm", "pool_w", "pool_scale", "xattn_norm", "mem_norm",
               "ffn2_norm", "final_norm")
WEIGHT_ORDER = ("ffn1_norm", "ffn1_w_gate", "ffn1_w_up", "ffn1_w_down", "mix_norm", "w_in", "q_norm", "w_q_up",
                "kv_norm", "w_kv_up", "pool_w", "pool_scale", "w_out", "xattn_norm", "mem_norm", "w_mq", "w_mkv",
                "w_mo", "ffn2_norm", "ffn2_w_gate", "ffn2_w_up", "ffn2_w_down", "final_norm")


def _rope_table():
    lane = np.arange(128)
    freqs = (1.0 / (ROPE_BASE ** (np.arange(0, ROPE, 2, dtype=np.float32) / ROPE))).astype(np.float32)
    tab = np.zeros((8, 128), np.float32)
    tab[0] = np.where(lane < ROPE, freqs[lane % (ROPE // 2)], 0.0)
    tab[1] = np.where(lane < ROPE // 2, -1.0, np.where(lane < ROPE, 1.0, 0.0))
    return jnp.asarray(tab)


def kernel(x, mem, positions, ffn1_norm, ffn1_w_gate, ffn1_w_up, ffn1_w_down, mix_norm, w_in, q_norm, w_q_up, kv_norm, w_kv_up, pool_w, pool_scale, w_out, xattn_norm, mem_norm, w_mq, w_mkv, w_mo, ffn2_norm, ffn2_w_gate, ffn2_w_up, ffn2_w_down, final_norm, loss_target, m_ffn1_norm, m_ffn1_w_gate, m_ffn1_w_up, m_ffn1_w_down, m_mix_norm, m_w_in, m_q_norm, m_w_q_up, m_kv_norm, m_w_kv_up, m_pool_w, m_pool_scale, m_w_out, m_xattn_norm, m_mem_norm, m_w_mq, m_w_mkv, m_w_mo, m_ffn2_norm, m_ffn2_w_gate, m_ffn2_w_up, m_ffn2_w_down, m_final_norm, v_ffn1_norm, v_ffn1_w_gate, v_ffn1_w_up, v_ffn1_w_down, v_mix_norm, v_w_in, v_q_norm, v_w_q_up, v_kv_norm, v_w_kv_up, v_pool_w, v_pool_scale, v_w_out, v_xattn_norm, v_mem_norm, v_w_mq, v_w_mkv, v_w_mo, v_ffn2_norm, v_ffn2_w_gate, v_ffn2_w_up, v_ffn2_w_down, v_final_norm):
    wts = dict(ffn1_norm=ffn1_norm, ffn1_w_gate=ffn1_w_gate, ffn1_w_up=ffn1_w_up, ffn1_w_down=ffn1_w_down,
               mix_norm=mix_norm, w_in=w_in, q_norm=q_norm, w_q_up=w_q_up, kv_norm=kv_norm, w_kv_up=w_kv_up,
               pool_w=pool_w, pool_scale=pool_scale, w_out=w_out, xattn_norm=xattn_norm, mem_norm=mem_norm,
               w_mq=w_mq, w_mkv=w_mkv, w_mo=w_mo, ffn2_norm=ffn2_norm, ffn2_w_gate=ffn2_w_gate,
               ffn2_w_up=ffn2_w_up, ffn2_w_down=ffn2_w_down, final_norm=final_norm)
    mom = dict(ffn1_norm=m_ffn1_norm, ffn1_w_gate=m_ffn1_w_gate, ffn1_w_up=m_ffn1_w_up, ffn1_w_down=m_ffn1_w_down,
               mix_norm=m_mix_norm, w_in=m_w_in, q_norm=m_q_norm, w_q_up=m_w_q_up, kv_norm=m_kv_norm,
               w_kv_up=m_w_kv_up, pool_w=m_pool_w, pool_scale=m_pool_scale, w_out=m_w_out, xattn_norm=m_xattn_norm,
               mem_norm=m_mem_norm, w_mq=m_w_mq, w_mkv=m_w_mkv, w_mo=m_w_mo, ffn2_norm=m_ffn2_norm,
               ffn2_w_gate=m_ffn2_w_gate, ffn2_w_up=m_ffn2_w_up, ffn2_w_down=m_ffn2_w_down, final_norm=m_final_norm)
    var = dict(ffn1_norm=v_ffn1_norm, ffn1_w_gate=v_ffn1_w_gate, ffn1_w_up=v_ffn1_w_up, ffn1_w_down=v_ffn1_w_down,
               mix_norm=v_mix_norm, w_in=v_w_in, q_norm=v_q_norm, w_q_up=v_w_q_up, kv_norm=v_kv_norm,
               w_kv_up=v_w_kv_up, pool_w=v_pool_w, pool_scale=v_pool_scale, w_out=v_w_out, xattn_norm=v_xattn_norm,
               mem_norm=v_mem_norm, w_mq=v_w_mq, w_mkv=v_w_mkv, w_mo=v_w_mo, ffn2_norm=v_ffn2_norm,
               ffn2_w_gate=v_ffn2_w_gate, ffn2_w_up=v_ffn2_w_up, ffn2_w_down=v_ffn2_w_down, final_norm=v_final_norm)

    t = x.shape[1]
    xs = x[0]
    mems = mem[0]
    target = loss_target[0]
    pos = positions.reshape(t, 1)
    row = lambda a: a.reshape(1, -1)
    rope_tab = _rope_table()

    cx, cy, cc = _coords()
    chip_idx = (2 * cx + cy).astype(jnp.int32).reshape(1)

    wb = {}
    for grp in ("ffn1", "mid", "ffn2"):
        wb[grp] = _pack_segments(wts, grp).astype(BF16)
        if grp == "ffn1":
            ag_ffn1 = _ici_start(wb["ffn1"], pos, "ag_ffn1_start", True)
    own_ffn1, land_ffn1 = _ici_wait(ag_ffn1, wb["ffn2"], "ag_ffn1_wait", True)
    full_ffn1 = _core_share(own_ffn1, land_ffn1, "ag_ffn1_share")
    fw = _unpack_gathered(full_ffn1, "ffn1")
    ag_mid = _ici_start(wb["mid"], full_ffn1, "ag_mid_start", True)
    g_ffn1, g_mix, g_q, g_kv = row(ffn1_norm), row(mix_norm), row(q_norm), row(kv_norm)
    g_x, g_mem, g_ffn2, g_fin = row(xattn_norm), row(mem_norm), row(ffn2_norm), row(final_norm)
    pool_wb = pool_w[0].astype(BF16)
    pool_sc = row(pool_scale)

    h1, n1, gate1, up1 = _ffn_fwd(xs, g_ffn1, fw["ffn1_g"], fw["ffn1_u"], fw["ffn1_d"], "ffn1_fwd", token=ag_mid[4])
    own_mid, land_mid = _ici_wait(ag_mid, h1, "ag_mid_wait", True)
    full_mid = _core_share(own_mid, land_mid, "ag_mid_share")
    fw.update(_unpack_gathered(full_mid, "mid"))
    ag_ffn2 = _ici_start(wb["ffn2"], full_mid, "ag_ffn2_start", True)
    u, z, qn, kvn, qh, kh, vh = _mix_prep(h1, g_mix, fw["w_in"], g_q, fw["w_q"], g_kv, fw["w_kv"], pos, rope_tab,
                                          token=ag_ffn2[4])
    a, lse = _attn_fwd(qh, kh, vh)
    p = _pool_fwd(z, pool_wb, pool_sc)
    memn, km, vm = _mem_kv(mems, g_mem, fw["w_mkv"])
    h2, h3, hn, qm, om = _xattn_fwd(h1, a, p, fw["w_out"], g_x, fw["w_mq"], km, vm, fw["w_mo"])
    own_ffn2, land_ffn2 = _ici_wait(ag_ffn2, h3, "ag_ffn2_wait", True)
    fw.update(_unpack_gathered(_core_share(own_ffn2, land_ffn2, "ag_ffn2_share"), "ffn2"))
    h4, n2, gate2, up2 = _ffn_fwd(h3, g_ffn2, fw["ffn2_g"], fw["ffn2_u"], fw["ffn2_d"], "ffn2_fwd")
    loss_part, dh4, dg_fin = _loss_head(h4, target, g_fin)

    def reduce_start(g8, unit):
        part = _core_reduce(g8, unit)
        return _ici_start(part, g8, "rs_" + unit + "_start", False)

    def by_device(g):
        return g.reshape(N_DEV, -1, D_MODEL)

    rs = {}
    dh3, dgate2, dup2, act2, dg_ffn2 = _ffn_bwd_data(dh4, h3, g_ffn2, gate2, up2, fw["ffn2_g"], fw["ffn2_u"],
                                                     fw["ffn2_d"], "ffn2_bwd")
    rs["ffn2_g"] = reduce_start(by_device(_tn_matmul(dgate2, n2, "ffn2_dwg", tmm=1408, out_dtype=BF16)), "ffn2_g")
    rs["ffn2_u"] = reduce_start(by_device(_tn_matmul(dup2, n2, "ffn2_dwu", tmm=1408, out_dtype=BF16,
                                                     token=rs["ffn2_g"][4])), "ffn2_u")
    rs["ffn2_d"] = reduce_start(by_device(_tn_matmul(act2, dh4, "ffn2_dwd", scale=0.5, tmm=1408, out_dtype=BF16,
                                                     token=rs["ffn2_u"][4])), "ffn2_d")
    dh2, dqm, da, dp, dkm, dvm, dg_x = _xattn_bwd(dh3, h2, qm, g_x, fw["w_mq"], km, vm, fw["w_mo"], fw["w_out"],
                                                  token=rs["ffn2_d"][4])
    gr = {}
    gr["w_mo"] = _tn_matmul(om, dh3, "dw_mo", out_dtype=BF16)
    gr["w_mq"] = _tn_matmul(hn, dqm, "dw_mq", out_dtype=BF16)
    gr["w_out"] = jnp.concatenate([_tn_matmul(a, dh2, "dw_out_a", out_dtype=BF16),
                                   _tn_matmul(p, dh2, "dw_out_p", out_dtype=BF16)], axis=0)
    gr["w_mkv"], dg_mem = _mem_kv_bwd(dkm, dvm, memn, mems, g_mem, fw["w_mkv"])
    dz_pool, d_pool_w, d_pool_sc = _pool_bwd(dp, z, pool_wb, pool_sc)
    dqh, dkh, dvh = _attn_bwd(qh, kh, vh, da, lse, _attn_delta(a, da))
    dh1, dq, dkv, dz, dg_q, dg_kv, dg_mix = _mla_bwd(dqh, dkh, dvh, z, dz_pool, h1, dh2, g_mix, fw["w_in"], g_q,
                                                     fw["w_q"], g_kv, fw["w_kv"], pos, rope_tab)
    gr["w_q"] = _tn_matmul(dq, qn, "dw_q", out_dtype=BF16)
    gr["w_kv"] = _tn_matmul(kvn, dkv, "dw_kv", out_dtype=BF16)
    gr["w_in"] = _tn_matmul(u, dz, "dw_in", out_dtype=BF16)
    g_mid = _pack_grads(gr)
    part_mid = _core_reduce(g_mid, "mid")
    got = {}
    after = part_mid
    for unit in ("ffn2_g", "ffn2_u", "ffn2_d"):
        got[unit] = _ici_wait(rs[unit], after, "rs_" + unit + "_wait", False)
        after = got[unit][1]
    rs["mid"] = _ici_start(part_mid, after, "rs_mid_start", False)
    dx, dgate1, dup1, act1, dg_ffn1 = _ffn_bwd_data(dh1, xs, g_ffn1, gate1, up1, fw["ffn1_g"], fw["ffn1_u"],
                                                    fw["ffn1_d"], "ffn1_bwd", token=rs["mid"][4])
    got["mid"] = _ici_wait(rs["mid"], dx, "rs_mid_wait", False)

    small_g = dict(ffn1_norm=dg_ffn1, mix_norm=dg_mix, q_norm=dg_q, kv_norm=dg_kv, pool_w=d_pool_w,
                   pool_scale=d_pool_sc, xattn_norm=dg_x, mem_norm=dg_mem, ffn2_norm=dg_ffn2, final_norm=dg_fin,
                   loss=loss_part)
    small_ag = _peers_start(_pack_small(small_g), got["mid"][1], "small_ag_start")
    rs["ffn1_g"] = reduce_start(by_device(_tn_matmul(dgate1, n1, "ffn1_dwg", tmm=1408, out_dtype=BF16,
                                                     token=small_ag[4])), "ffn1_g")
    _, parts = _peers_wait(small_ag, rs["ffn1_g"][4], "small_ag_wait")
    small = _adam_small(parts, _pack_small({n: wts[n] for n in SMALL_NAMES}),
                        _pack_small({n: mom[n] for n in SMALL_NAMES}), _pack_small({n: var[n] for n in SMALL_NAMES}))
    small_sum = small[0]
    loss = small_sum[SMALL_OFF["loss"][0], 0]
    shapes = {n: wts[n].shape for n in SMALL_NAMES}
    small = [_unpack_small(s, shapes) for s in small]

    rs["ffn1_u"] = reduce_start(by_device(_tn_matmul(dup1, n1, "ffn1_dwu", tmm=1408, out_dtype=BF16,
                                                     token=small_sum)), "ffn1_u")
    rs["ffn1_d"] = reduce_start(by_device(_tn_matmul(act1, dh1, "ffn1_dwd", scale=0.5, tmm=1408, out_dtype=BF16,
                                                     token=rs["ffn1_u"][4])), "ffn1_d")

    big = {}

    def adam_unit(unit, token):
        part, land = got[unit]
        res = _adam_big(part, land, _pack_unit(wts, unit), _pack_unit(mom, unit), _pack_unit(var, unit),
                        chip_idx, unit, token)
        for k, packed in enumerate(res):
            big.setdefault(k, {}).update(_unpack_unit(packed, unit))
        return res[0]

    done = rs["ffn1_d"][4]
    for unit in ("mid", "ffn2_g", "ffn2_u", "ffn2_d"):
        done = adam_unit(unit, done)
    for unit in ("ffn1_g", "ffn1_u", "ffn1_d"):
        got[unit] = _ici_wait(rs[unit], done, "rs_" + unit + "_wait", False)
        done = adam_unit(unit, done)

    outs = [loss, dx[None]]
    for k in range(4):
        for n in WEIGHT_ORDER:
            outs.append(big[k][n] if n in BIG_NAMES else small[k][n])
    return tuple(outs)
```

```python
import numpy as np

import jax
import jax.numpy as jnp
from jax import lax
from jax.experimental import pallas as pl
from jax.experimental.pallas import tpu as pltpu

F32 = jnp.float32
BF16 = jnp.bfloat16

N_DEV = 8
D_MODEL = 1024
D_FF = 2816
MLA_HEADS = 4
NOPE = 128
ROPE = 64
HEAD_PAD = 256
V_DIM = 128
Q_RANK = 256
KV_RANK = 128
POOL_WINDOWS = (2, 4, 8, 16)
POOL_CH = 128
POOL_HALO = 16
N_MEM = 256
MEM_HEADS = 4
MEM_HD = 256
ROPE_BASE = 10000.0
RMS_EPS = 1e-6
ATTN_SCALE = (NOPE + ROPE) ** -0.5
MEM_SCALE = MEM_HD ** -0.5
NEG_BIG = -1e30

ADAM_LR = 0.001
ADAM_B1 = 0.9
ADAM_B2 = 0.999
ADAM_EPS = 1e-08
ADAM_WD = 0.01
ADAM_STEP = 10
ADAM_C1 = 1.0 - ADAM_B1 ** ADAM_STEP
ADAM_C2 = 1.0 - ADAM_B2 ** ADAM_STEP

VMEM_LIMIT_BYTES = 56 * 1024 * 1024
BF16_ROWS = 16

GROUP_SEGS = {
    "ffn1": (("ffn1_g", 352), ("ffn1_u", 352), ("ffn1_d", 352)),
    "mid": (("w_in", 128), ("w_out", 128), ("w_mq", 128), ("w_mo", 128), ("w_mkv", 256), ("w_q", 32), ("w_kv", 16)),
    "ffn2": (("ffn2_g", 352), ("ffn2_u", 352), ("ffn2_d", 352)),
}
SEG_OFF = {}
GROUP_ROWS = {}
for _g, _segs in GROUP_SEGS.items():
    _o = 0
    for _n, _r in _segs:
        SEG_OFF[_n] = (_o, _r)
        _o += _r
    GROUP_ROWS[_g] = _o

SMALL_ROWS = (("ffn1_norm", 8), ("mix_norm", 8), ("q_norm", 8), ("kv_norm", 8), ("pool_w", 512), ("pool_scale", 8),
              ("xattn_norm", 8), ("mem_norm", 8), ("ffn2_norm", 8), ("final_norm", 8), ("loss", 8))
SMALL_OFF = {}
_o = 0
for _n, _r in SMALL_ROWS:
    SMALL_OFF[_n] = (_o, _r)
    _o += _r


def _cparams(**kw):
    return pltpu.CompilerParams(vmem_limit_bytes=VMEM_LIMIT_BYTES, **kw)


def _row_tile(rows, limit):
    best = None
    for cand in range(BF16_ROWS, min(rows, limit) + 1, BF16_ROWS):
        if rows % cand == 0:
            best = cand
    assert best is not None, rows
    return best


def _dot_nn(a, b):
    return lax.dot_general(a, b, (((1,), (0,)), ((), ())), preferred_element_type=F32)


def _dot_nt(a, b):
    return lax.dot_general(a, b, (((1,), (1,)), ((), ())), preferred_element_type=F32)


def _dot_tn(a, b):
    return lax.dot_general(a, b, (((0,), (0,)), ((), ())), preferred_element_type=F32)


def _rms_fwd(x, g):
    r = lax.rsqrt(jnp.mean(x * x, axis=-1, keepdims=True) + RMS_EPS)
    return x * r * g, r


def _rms_bwd(dy, x, g, r):
    xhat = x * r
    dyg = dy * g
    dx = r * (dyg - xhat * jnp.mean(dyg * xhat, axis=-1, keepdims=True))
    dg = jnp.sum(dy * xhat, axis=0, keepdims=True)
    return dx, dg


def _accumulate(ref, val, first):
    if isinstance(first, bool):
        if first:
            ref[...] = val
        else:
            ref[...] += val
        return

    @pl.when(first)
    def _():
        ref[...] = val

    @pl.when(jnp.logical_not(first))
    def _():
        ref[...] += val


def _call_after(token, body, in_specs, args, **kw):
    if token is not None:
        inner = body
        body = lambda tok_ref, *refs: inner(*refs)
        in_specs = [pl.BlockSpec((8, 128), lambda *_: (0, 0))] + list(in_specs)
        args = (token,) + tuple(args)
    return pl.pallas_call(body, in_specs=in_specs, **kw)(*args)


def _resident(shape):
    return pl.BlockSpec(shape, lambda *_: (0,) * len(shape), pipeline_mode=pl.Buffered(1))


def _rope_tables(pos_col, tab):
    ang = pos_col.astype(F32) * tab[0:1, :]
    return jnp.cos(ang), jnp.sin(ang) * tab[1:2, :]


def _swap_halves(x):
    lane = lax.broadcasted_iota(jnp.int32, x.shape, 1)
    return jnp.where((lane % 64) < 32, pltpu.roll(x, 96, 1), pltpu.roll(x, 32, 1))


def _rope_apply(x, cos_t, sin_t):
    return x * cos_t + _swap_halves(x) * sin_t


def _rope_apply_t(dy, cos_t, sin_t):
    return dy * cos_t + _swap_halves(dy * sin_t)


def _ffn_fwd(h, g, wg_t, wu_t, wd, name, token=None):
    t, d = h.shape
    f = wg_t.shape[0]
    tm, tf = min(512, t), 256
    nf = f // tf

    def body(h_ref, g_ref, wg_ref, wu_ref, wd_ref, ho_ref, n_ref, gate_ref, up_ref, nb_sc, acc_sc):
        y, _ = _rms_fwd(h_ref[...], g_ref[...])
        nb = y.astype(BF16)
        nb_sc[...] = nb
        n_ref[...] = nb
        acc_sc[...] = jnp.zeros_like(acc_sc)

        def f_tile(j):
            rows = pl.ds(pl.multiple_of(j * tf, tf), tf)
            nb = nb_sc[...]
            gt = _dot_nt(nb, wg_ref[rows, :])
            ut = _dot_nt(nb, wu_ref[rows, :])
            gate_ref[j] = gt.astype(BF16)
            up_ref[j] = ut.astype(BF16)
            act = (gt * jax.nn.sigmoid(gt)) * ut
            return _dot_nn(act.astype(BF16), wd_ref[rows, :])

        def pair(p, carry):
            acc_sc[...] += f_tile(2 * p) + f_tile(2 * p + 1)
            return carry

        lax.fori_loop(0, nf // 2, pair, 0)
        if nf % 2:
            acc_sc[...] += f_tile(nf - 1)
        ho_ref[...] = h_ref[...] + 0.5 * acc_sc[...]

    return _call_after(
        token, body,
        [pl.BlockSpec((tm, d), lambda i: (i, 0)), _resident((1, d)), _resident((f, d)), _resident((f, d)),
         _resident((f, d))],
        (h, g, wg_t, wu_t, wd),
        name=name, grid=(t // tm,),
        out_specs=[pl.BlockSpec((tm, d), lambda i: (i, 0)),
                   pl.BlockSpec((tm, d), lambda i: (i, 0)),
                   pl.BlockSpec((nf, tm, tf), lambda i: (0, i, 0)),
                   pl.BlockSpec((nf, tm, tf), lambda i: (0, i, 0))],
        out_shape=[jax.ShapeDtypeStruct((t, d), F32), jax.ShapeDtypeStruct((t, d), BF16),
                   jax.ShapeDtypeStruct((nf, t, tf), BF16), jax.ShapeDtypeStruct((nf, t, tf), BF16)],
        scratch_shapes=[pltpu.VMEM((tm, d), BF16), pltpu.VMEM((tm, d), F32)],
        compiler_params=_cparams(),
    )


def _ffn_bwd_data(dho, h, g, gate, up, wg_t, wu_t, wd, name, token=None):
    t, d = h.shape
    f = wg_t.shape[0]
    tm, tf = min(1024, t), 256
    parts = 2 if tm % 512 == 0 else 1
    tp = tm // parts
    nf = f // tf

    def body(dho_ref, h_ref, g_ref, gate_ref, up_ref, wg_ref, wu_ref, wd_ref,
             dh_ref, dgate_ref, dup_ref, act_ref, dg_ref, dhb_sc, acc_sc):
        i, j = pl.program_id(0), pl.program_id(1)

        @pl.when(j == 0)
        def _():
            dhb_sc[...] = (0.5 * dho_ref[...]).astype(BF16)
            acc_sc[...] = jnp.zeros_like(acc_sc)

        for r in range(parts):
            rows = pl.ds(r * tp, tp)
            dact = _dot_nt(dhb_sc[rows, :], wd_ref[...])
            gt = gate_ref[0, rows, :].astype(F32)
            ut = up_ref[0, rows, :].astype(F32)
            sg = jax.nn.sigmoid(gt)
            silu = gt * sg
            dgb = (dact * ut * (sg * (1.0 + gt * (1.0 - sg)))).astype(BF16)
            dub = (dact * silu).astype(BF16)
            act_ref[rows, :] = (silu * ut).astype(BF16)
            dgate_ref[rows, :] = dgb
            dup_ref[rows, :] = dub
            acc_sc[rows, :] += _dot_nn(dgb, wg_ref[...]) + _dot_nn(dub, wu_ref[...])

        @pl.when(j == nf - 1)
        def _():
            x = h_ref[...]
            gg = g_ref[...]
            _, r = _rms_fwd(x, gg)
            dx, dg = _rms_bwd(acc_sc[...], x, gg, r)
            dh_ref[...] = dho_ref[...] + dx
            _accumulate(dg_ref, dg, i == 0)

    return _call_after(
        token, body,
        [pl.BlockSpec((tm, d), lambda i, j: (i, 0)),
         pl.BlockSpec((tm, d), lambda i, j: (i, 0)),
         pl.BlockSpec((1, d), lambda i, j: (0, 0)),
         pl.BlockSpec((1, tm, tf), lambda i, j: (j, i, 0)),
         pl.BlockSpec((1, tm, tf), lambda i, j: (j, i, 0)),
         pl.BlockSpec((tf, d), lambda i, j: (j, 0)),
         pl.BlockSpec((tf, d), lambda i, j: (j, 0)),
         pl.BlockSpec((tf, d), lambda i, j: (j, 0))],
        (dho, h, g, gate, up, wg_t, wu_t, wd),
        name=name, grid=(t // tm, nf),
        out_specs=[pl.BlockSpec((tm, d), lambda i, j: (i, 0)),
                   pl.BlockSpec((tm, tf), lambda i, j: (i, j)),
                   pl.BlockSpec((tm, tf), lambda i, j: (i, j)),
                   pl.BlockSpec((tm, tf), lambda i, j: (i, j)),
                   pl.BlockSpec((1, d), lambda i, j: (0, 0))],
        out_shape=[jax.ShapeDtypeStruct((t, d), F32), jax.ShapeDtypeStruct((t, f), BF16),
                   jax.ShapeDtypeStruct((t, f), BF16), jax.ShapeDtypeStruct((t, f), BF16),
                   jax.ShapeDtypeStruct((1, d), F32)],
        scratch_shapes=[pltpu.VMEM((tm, d), BF16), pltpu.VMEM((tm, d), F32)],
        compiler_params=_cparams(),
    )


def _tn_matmul(a, b, name, scale=1.0, tmm=None, out_dtype=F32, token=None):
    t, m = a.shape
    n = b.shape[1]
    tmm = m if tmm is None else tmm
    tk = min(1024, t)
    nk = t // tk

    def product(a_ref, b_ref):
        prod = _dot_tn(a_ref[...].astype(BF16), b_ref[...].astype(BF16))
        return prod * scale if scale != 1.0 else prod

    def body_f32(a_ref, b_ref, o_ref):
        _accumulate(o_ref, product(a_ref, b_ref), pl.program_id(1) == 0)

    def body_cast(a_ref, b_ref, o_ref, acc_sc):
        k = pl.program_id(1)
        _accumulate(acc_sc, product(a_ref, b_ref), k == 0)

        @pl.when(k == nk - 1)
        def _():
            o_ref[...] = acc_sc[...].astype(out_dtype)

    direct = out_dtype == F32
    return _call_after(
        token, body_f32 if direct else body_cast,
        [pl.BlockSpec((tk, tmm), lambda i, k: (k, i)),
         pl.BlockSpec((tk, n), lambda i, k: (k, 0))],
        (a, b),
        name=name, grid=(m // tmm, nk),
        out_specs=pl.BlockSpec((tmm, n), lambda i, k: (i, 0)),
        out_shape=jax.ShapeDtypeStruct((m, n), out_dtype),
        scratch_shapes=[] if direct else [pltpu.VMEM((tmm, n), F32)],
        compiler_params=_cparams(),
    )


def _loss_head(h, target, g):
    t, d = h.shape
    tm = min(512, t)

    def body(h_ref, t_ref, g_ref, loss_ref, dh_ref, dg_ref):
        i = pl.program_id(0)
        x = h_ref[...]
        gg = g_ref[...]
        y, r = _rms_fwd(x, gg)
        err = y - t_ref[...]
        part = 0.5 * jnp.sum(jnp.mean(err * err, axis=-1, keepdims=True), axis=0, keepdims=True)
        dx, dg = _rms_bwd(err * (1.0 / d), x, gg, r)
        dh_ref[...] = dx
        _accumulate(loss_ref, jnp.broadcast_to(part, loss_ref.shape), i == 0)
        _accumulate(dg_ref, dg, i == 0)

    return pl.pallas_call(
        body, name="loss_head", grid=(t // tm,),
        in_specs=[pl.BlockSpec((tm, d), lambda i: (i, 0)),
                  pl.BlockSpec((tm, d), lambda i: (i, 0)),
                  pl.BlockSpec((1, d), lambda i: (0, 0))],
        out_specs=[pl.BlockSpec((8, 128), lambda i: (0, 0)),
                   pl.BlockSpec((tm, d), lambda i: (i, 0)),
                   pl.BlockSpec((1, d), lambda i: (0, 0))],
        out_shape=[jax.ShapeDtypeStruct((8, 128), F32), jax.ShapeDtypeStruct((t, d), F32),
                   jax.ShapeDtypeStruct((1, d), F32)],
        compiler_params=_cparams(),
    )(h, target, g)


def _mix_prep(h1, mix_norm, w_in, q_norm, wq_t, kv_norm, wkv, pos, rope_tab, token=None):
    t, d = h1.shape
    tm = min(512, t)

    def body(h_ref, gm_ref, win_ref, gq_ref, wq_ref, gkv_ref, wkv_ref, pos_ref, tab_ref,
             u_ref, z_ref, qn_ref, kvn_ref, q_ref, k_ref, v_ref):
        u, _ = _rms_fwd(h_ref[...], gm_ref[...])
        ub = u.astype(BF16)
        u_ref[...] = ub
        z = _dot_nn(ub, win_ref[...])
        z_ref[...] = z
        cos_t, sin_t = _rope_tables(pos_ref[...], tab_ref[...])
        qn, _ = _rms_fwd(z[:, 0:Q_RANK], gq_ref[...])
        qnb = qn.astype(BF16)
        qn_ref[...] = qnb
        q = _dot_nt(qnb, wq_ref[...])
        kvn, _ = _rms_fwd(z[:, Q_RANK:Q_RANK + KV_RANK], gkv_ref[...])
        kvnb = kvn.astype(BF16)
        kvn_ref[...] = kvnb
        kv = _dot_nn(kvnb, wkv_ref[...])
        k_pe = _rope_apply(z[:, Q_RANK + KV_RANK:Q_RANK + KV_RANK + 128], cos_t, sin_t)
        ones = jnp.ones((tm, V_DIM), F32)
        for hh in range(MLA_HEADS):
            b = hh * HEAD_PAD
            q_pe = _rope_apply(q[:, b + NOPE:b + HEAD_PAD], cos_t, sin_t)
            q_ref[hh] = jnp.concatenate([q[:, b:b + NOPE], q_pe], axis=-1).astype(BF16)
            k_ref[hh] = jnp.concatenate([kv[:, b:b + NOPE], k_pe], axis=-1).astype(BF16)
            v_ref[hh] = jnp.concatenate([kv[:, b + NOPE:b + HEAD_PAD], ones], axis=-1).astype(BF16)

    full = lambda shape: pl.BlockSpec(shape, lambda i: (0,) * len(shape))
    return _call_after(
        token, body,
        [pl.BlockSpec((tm, d), lambda i: (i, 0)), _resident((1, d)), _resident(w_in.shape), _resident((1, Q_RANK)),
         _resident(wq_t.shape), _resident((1, KV_RANK)), _resident(wkv.shape),
         pl.BlockSpec((tm, 1), lambda i: (i, 0)), _resident(rope_tab.shape)],
        (h1, mix_norm, w_in, q_norm, wq_t, kv_norm, wkv, pos, rope_tab),
        name="mix_prep", grid=(t // tm,),
        out_specs=[pl.BlockSpec((tm, d), lambda i: (i, 0)),
                   pl.BlockSpec((tm, d), lambda i: (i, 0)),
                   pl.BlockSpec((tm, Q_RANK), lambda i: (i, 0)),
                   pl.BlockSpec((tm, KV_RANK), lambda i: (i, 0)),
                   pl.BlockSpec((MLA_HEADS, tm, HEAD_PAD), lambda i: (0, i, 0)),
                   pl.BlockSpec((MLA_HEADS, tm, HEAD_PAD), lambda i: (0, i, 0)),
                   pl.BlockSpec((MLA_HEADS, tm, 2 * V_DIM), lambda i: (0, i, 0))],
        out_shape=[jax.ShapeDtypeStruct((t, d), BF16), jax.ShapeDtypeStruct((t, d), F32),
                   jax.ShapeDtypeStruct((t, Q_RANK), BF16), jax.ShapeDtypeStruct((t, KV_RANK), BF16),
                   jax.ShapeDtypeStruct((MLA_HEADS, t, HEAD_PAD), BF16),
                   jax.ShapeDtypeStruct((MLA_HEADS, t, HEAD_PAD), BF16),
                   jax.ShapeDtypeStruct((MLA_HEADS, t, 2 * V_DIM), BF16)],
        compiler_params=_cparams(),
    )


def _causal_mask(s):
    row = lax.broadcasted_iota(jnp.int32, s.shape, 0)
    col = lax.broadcasted_iota(jnp.int32, s.shape, 1)
    return jnp.where(col <= row, s, NEG_BIG)


def _attn_fwd(q, k, v):
    nh, t, _ = q.shape
    tq = tk = min(512, t)
    nq, nk = t // tq, t // tk

    pairs = [(i, j) for i in range(nq) for j in range(i + 1)]
    qi = jnp.asarray(np.array([i for i, _ in pairs], np.int32))
    kj = jnp.asarray(np.array([j for _, j in pairs], np.int32))

    def body(qi_ref, kj_ref, q_ref, k_ref, v_ref, o_ref, lse_ref, m_sc, acc_sc):
        n = pl.program_id(0)
        i, j = qi_ref[n], kj_ref[n]

        @pl.when(j == 0)
        def _():
            m_sc[...] = jnp.full_like(m_sc, NEG_BIG)
            acc_sc[...] = jnp.zeros_like(acc_sc)

        def step(diagonal):
            for hh in range(nh):
                s = _dot_nt(q_ref[hh], k_ref[hh]) * ATTN_SCALE
                if diagonal:
                    s = _causal_mask(s)
                m_old = m_sc[hh]
                m_new = jnp.maximum(m_old, jnp.max(s, axis=-1, keepdims=True))
                p = jnp.exp(s - m_new).astype(BF16)
                acc_sc[hh] = jnp.exp(m_old - m_new) * acc_sc[hh] + _dot_nn(p, v_ref[hh])
                m_sc[hh] = m_new

        @pl.when(j < i)
        def _():
            step(False)

        @pl.when(j == i)
        def _():
            step(True)
            for hh in range(nh):
                acc = acc_sc[hh]
                l = acc[:, V_DIM:2 * V_DIM]
                o_ref[:, hh * V_DIM:(hh + 1) * V_DIM] = (acc[:, 0:V_DIM] / l).astype(BF16)
                lse_ref[hh] = m_sc[hh] + jnp.log(l[:, 0:1])

    q_map = lambda n, qi_ref, kj_ref: (0, qi_ref[n], 0)
    kv_map = lambda n, qi_ref, kj_ref: (0, kj_ref[n], 0)
    return pl.pallas_call(
        body, name="attn_fwd",
        grid_spec=pltpu.PrefetchScalarGridSpec(
            num_scalar_prefetch=2, grid=(len(pairs),),
            in_specs=[pl.BlockSpec((nh, tq, HEAD_PAD), q_map),
                      pl.BlockSpec((nh, tk, HEAD_PAD), kv_map),
                      pl.BlockSpec((nh, tk, 2 * V_DIM), kv_map)],
            out_specs=[pl.BlockSpec((tq, nh * V_DIM), lambda n, qi_ref, kj_ref: (qi_ref[n], 0)),
                       pl.BlockSpec((nh, tq, 1), q_map)],
            scratch_shapes=[pltpu.VMEM((nh, tq, 1), F32), pltpu.VMEM((nh, tq, 2 * V_DIM), F32)]),
        out_shape=[jax.ShapeDtypeStruct((t, nh * V_DIM), BF16), jax.ShapeDtypeStruct((nh, t, 1), F32)],
        compiler_params=_cparams(),
    )(qi, kj, q, k, v)


def _attn_delta(o, do):
    t, w = o.shape
    nh = w // V_DIM
    tm = min(512, t)

    def body(o_ref, do_ref, d_ref):
        prod = o_ref[...].astype(F32) * do_ref[...].astype(F32)
        for hh in range(nh):
            d_ref[hh] = jnp.sum(prod[:, hh * V_DIM:(hh + 1) * V_DIM], axis=-1, keepdims=True)

    return pl.pallas_call(
        body, name="attn_delta", grid=(t // tm,),
        in_specs=[pl.BlockSpec((tm, w), lambda i: (i, 0)), pl.BlockSpec((tm, w), lambda i: (i, 0))],
        out_specs=pl.BlockSpec((nh, tm, 1), lambda i: (0, i, 0)),
        out_shape=jax.ShapeDtypeStruct((nh, t, 1), F32),
        compiler_params=_cparams(),
    )(o, do)


ATTN_BWD_HEADS = 2


def _attn_bwd(q, k, v, do, lse, delta):
    nh, t, _ = q.shape
    hp = ATTN_BWD_HEADS
    tq = tk = min(512, t)
    nq, nk = t // tq, t // tk

    pairs = [(j, i) for j in range(nk) for i in range(j, nq)]
    kj = jnp.asarray(np.array([j for j, _ in pairs], np.int32))
    qi = jnp.asarray(np.array([i for _, i in pairs], np.int32))

    def body(kj_ref, qi_ref, q_ref, k_ref, v_ref, do_ref, lse_ref, dlt_ref, dq_ref, dk_ref, dv_ref):
        n = pl.program_id(1)
        j, i = kj_ref[n], qi_ref[n]

        @pl.when(n == 0)
        def _():
            dq_ref[...] = jnp.zeros_like(dq_ref)

        def step(diagonal):
            for hh in range(hp):
                qq, kk = q_ref[hh], k_ref[hh]
                dob = do_ref[:, hh * V_DIM:(hh + 1) * V_DIM]
                s = _dot_nt(qq, kk) * ATTN_SCALE
                if diagonal:
                    s = _causal_mask(s)
                p = jnp.exp(s - lse_ref[hh])
                dpp = _dot_nt(dob, v_ref[hh])
                dsb = (p * (dpp - dlt_ref[hh]) * ATTN_SCALE).astype(BF16)
                _accumulate(dv_ref.at[hh], _dot_tn(p.astype(BF16), dob), diagonal)
                _accumulate(dk_ref.at[hh], _dot_tn(dsb, qq), diagonal)
                dq_ref[hh, pl.ds(pl.multiple_of(i * tq, tq), tq), :] += _dot_nn(dsb, kk)

        @pl.when(i > j)
        def _():
            step(False)

        @pl.when(i == j)
        def _():
            step(True)

    q_map = lambda h, n, kj_ref, qi_ref: (h, qi_ref[n], 0)
    k_map = lambda h, n, kj_ref, qi_ref: (h, kj_ref[n], 0)
    return pl.pallas_call(
        body, name="attn_bwd",
        grid_spec=pltpu.PrefetchScalarGridSpec(
            num_scalar_prefetch=2, grid=(nh // hp, len(pairs)),
            in_specs=[pl.BlockSpec((hp, tq, HEAD_PAD), q_map),
                      pl.BlockSpec((hp, tk, HEAD_PAD), k_map),
                      pl.BlockSpec((hp, tk, V_DIM), k_map),
                      pl.BlockSpec((tq, hp * V_DIM), lambda h, n, kj_ref, qi_ref: (qi_ref[n], h)),
                      pl.BlockSpec((hp, tq, 1), q_map),
                      pl.BlockSpec((hp, tq, 1), q_map)],
            out_specs=[pl.BlockSpec((hp, t, HEAD_PAD), lambda h, n, kj_ref, qi_ref: (h, 0, 0)),
                       pl.BlockSpec((hp, tk, HEAD_PAD), k_map),
                       pl.BlockSpec((hp, tk, V_DIM), k_map)]),
        out_shape=[jax.ShapeDtypeStruct((nh, t, HEAD_PAD), F32), jax.ShapeDtypeStruct((nh, t, HEAD_PAD), F32),
                   jax.ShapeDtypeStruct((nh, t, V_DIM), F32)],
        compiler_params=_cparams(),
    )(kj, qi, q, k, v, do, lse, delta)


def _pool_counts(first_token, rows, w):
    tok = lax.broadcasted_iota(jnp.int32, (rows, POOL_CH), 0) + first_token
    return jnp.minimum(tok + 1, w).astype(F32)


def _pool_centered(zbuf, g, w, i, tm):
    lanes = pl.ds(g * POOL_CH, POOL_CH)
    cur = zbuf[pl.ds(POOL_HALO, tm), lanes]
    win = cur
    for s in range(1, w):
        win = win + zbuf[pl.ds(POOL_HALO - s, tm), lanes]
    return win / _pool_counts(i * tm, tm, w) - cur


def _pool_load(zbuf, z_ref, halo_ref, i, tm):
    @pl.when(i == 0)
    def _():
        zbuf[pl.ds(0, POOL_HALO), :] = jnp.zeros((POOL_HALO, zbuf.shape[1]), F32)

    @pl.when(i > 0)
    def _():
        zbuf[pl.ds(0, POOL_HALO), :] = halo_ref[...]

    zbuf[pl.ds(POOL_HALO, tm), :] = z_ref[...]


def _pool_fwd(z, pool_w, pool_scale):
    t = z.shape[0]
    pw = len(POOL_WINDOWS) * POOL_CH
    tm = min(512, t)
    hb = tm // POOL_HALO

    def body(z_ref, halo_ref, w_ref, sc_ref, p_ref, zbuf):
        i = pl.program_id(0)
        _pool_load(zbuf, z_ref, halo_ref, i, tm)
        for g, w in enumerate(POOL_WINDOWS):
            c = _pool_centered(zbuf, g, w, i, tm)
            y = _dot_nn(c.astype(BF16), w_ref[g]) * sc_ref[:, g * POOL_CH:(g + 1) * POOL_CH]
            p_ref[:, g * POOL_CH:(g + 1) * POOL_CH] = y.astype(BF16)

    return pl.pallas_call(
        body, name="pool_fwd", grid=(t // tm,),
        in_specs=[pl.BlockSpec((tm, pw), lambda i: (i, 1)),
                  pl.BlockSpec((POOL_HALO, pw), lambda i: (jnp.maximum(i * hb - 1, 0), 1)),
                  pl.BlockSpec(pool_w.shape, lambda i: (0, 0, 0)),
                  pl.BlockSpec((1, pw), lambda i: (0, 0))],
        out_specs=pl.BlockSpec((tm, pw), lambda i: (i, 0)),
        out_shape=jax.ShapeDtypeStruct((t, pw), BF16),
        scratch_shapes=[pltpu.VMEM((POOL_HALO + tm, pw), F32)],
        compiler_params=_cparams(),
    )(z, z, pool_w, pool_scale)


def _pool_bwd(dp, z, pool_w, pool_scale):
    t = z.shape[0]
    ng = len(POOL_WINDOWS)
    pw = ng * POOL_CH
    tm = min(512, t)
    hb = tm // POOL_HALO
    nt = t // tm

    def body(dp_ref, dpn_ref, z_ref, halo_ref, w_ref, sc_ref, dz_ref, dw_ref, dsc_ref, zbuf, dbuf):
        i = pl.program_id(0)
        _pool_load(zbuf, z_ref, halo_ref, i, tm)

        @pl.when(i == 0)
        def _():
            dw_ref[...] = jnp.zeros_like(dw_ref)
            dsc_ref[...] = jnp.zeros_like(dsc_ref)

        nxt_ok = (i < nt - 1).astype(F32)
        for g, w in enumerate(POOL_WINDOWS):
            lanes = pl.ds(g * POOL_CH, POOL_CH)
            cols = slice(g * POOL_CH, (g + 1) * POOL_CH)
            sc = sc_ref[:, cols]
            wg = w_ref[g]
            c = _pool_centered(zbuf, g, w, i, tm).astype(BF16)
            ypre = _dot_nn(c, wg)
            dpg = dp_ref[:, cols].astype(F32)
            dsc_ref[:, cols] += jnp.sum(dpg * ypre, axis=0, keepdims=True)
            dyb = (dpg * sc).astype(BF16)
            dw_ref[g] += _dot_tn(c, dyb)
            dd = _dot_nt(dyb, wg)
            dyn = (dpn_ref[:, cols].astype(F32) * sc).astype(BF16)
            ddn = _dot_nt(dyn, wg) * nxt_ok
            dbuf[pl.ds(0, tm), lanes] = dd / _pool_counts(i * tm, tm, w)
            dbuf[pl.ds(tm, POOL_HALO), lanes] = ddn / _pool_counts((i + 1) * tm, POOL_HALO, w)
            acc = -dd
            for s in range(w):
                acc = acc + dbuf[pl.ds(s, tm), lanes]
            dz_ref[:, cols] = acc

    return pl.pallas_call(
        body, name="pool_bwd", grid=(nt,),
        in_specs=[pl.BlockSpec((tm, pw), lambda i: (i, 0)),
                  pl.BlockSpec((POOL_HALO, pw), lambda i: (jnp.minimum((i + 1) * hb, t // POOL_HALO - 1), 0)),
                  pl.BlockSpec((tm, pw), lambda i: (i, 1)),
                  pl.BlockSpec((POOL_HALO, pw), lambda i: (jnp.maximum(i * hb - 1, 0), 1)),
                  pl.BlockSpec(pool_w.shape, lambda i: (0, 0, 0)),
                  pl.BlockSpec((1, pw), lambda i: (0, 0))],
        out_specs=[pl.BlockSpec((tm, pw), lambda i: (i, 0)),
                   pl.BlockSpec((ng, POOL_CH, POOL_CH), lambda i: (0, 0, 0)),
                   pl.BlockSpec((1, pw), lambda i: (0, 0))],
        out_shape=[jax.ShapeDtypeStruct((t, pw), F32), jax.ShapeDtypeStruct((ng, POOL_CH, POOL_CH), F32),
                   jax.ShapeDtypeStruct((1, pw), F32)],
        scratch_shapes=[pltpu.VMEM((POOL_HALO + tm, pw), F32), pltpu.VMEM((tm + POOL_HALO, pw), F32)],
        compiler_params=_cparams(),
    )(dp, dp, z, z, pool_w, pool_scale)


def _mla_bwd(dq_h, dk_h, dv_h, z, dz_pool, h1, dh2, mix_norm, w_in, q_norm, wq_t, kv_norm, wkv, pos, rope_tab):
    t, d = h1.shape
    tm = min(512, t)

    def body(dqh_ref, dkh_ref, dvh_ref, z_ref, dzp_ref, h_ref, dh2_ref, gm_ref, win_ref, gq_ref, wq_ref, gkv_ref,
             wkv_ref, pos_ref, tab_ref, dh1_ref, dq_ref, dkv_ref, dz_ref, dgq_ref, dgkv_ref, dgm_ref):
        i = pl.program_id(0)
        first = i == 0
        cos_t, sin_t = _rope_tables(pos_ref[...], tab_ref[...])
        dq_parts, dkv_parts = [], []
        dk_pe = jnp.zeros((tm, 128), F32)
        for hh in range(MLA_HEADS):
            dqh = dqh_ref[hh]
            dq_parts += [dqh[:, 0:NOPE], _rope_apply_t(dqh[:, NOPE:HEAD_PAD], cos_t, sin_t)]
            dkh = dkh_ref[hh]
            dkv_parts += [dkh[:, 0:NOPE], dvh_ref[hh]]
            dk_pe = dk_pe + dkh[:, NOPE:HEAD_PAD]
        dqb = jnp.concatenate(dq_parts, axis=-1).astype(BF16)
        dkvb = jnp.concatenate(dkv_parts, axis=-1).astype(BF16)
        dq_ref[...] = dqb
        dkv_ref[...] = dkvb
        z = z_ref[...]
        c_q = z[:, 0:Q_RANK]
        gq = gq_ref[...]
        _, rq = _rms_fwd(c_q, gq)
        dcq, dgq = _rms_bwd(_dot_nn(dqb, wq_ref[...]), c_q, gq, rq)
        c_kv = z[:, Q_RANK:Q_RANK + KV_RANK]
        gkv = gkv_ref[...]
        _, rkv = _rms_fwd(c_kv, gkv)
        dckv, dgkv = _rms_bwd(_dot_nt(dkvb, wkv_ref[...]), c_kv, gkv, rkv)
        dkr = _rope_apply_t(dk_pe, cos_t, sin_t)
        dzb = jnp.concatenate([dcq, dckv, dkr, dzp_ref[...]], axis=-1).astype(BF16)
        dz_ref[...] = dzb
        x = h_ref[...]
        gm = gm_ref[...]
        _, rm = _rms_fwd(x, gm)
        dx, dgm = _rms_bwd(_dot_nt(dzb, win_ref[...]), x, gm, rm)
        dh1_ref[...] = dh2_ref[...] + dx
        _accumulate(dgq_ref, dgq, first)
        _accumulate(dgkv_ref, dgkv, first)
        _accumulate(dgm_ref, dgm, first)

    full = lambda shape: pl.BlockSpec(shape, lambda i: (0,) * len(shape))
    row = lambda w: pl.BlockSpec((tm, w), lambda i: (i, 0))
    head = lambda w: pl.BlockSpec((MLA_HEADS, tm, w), lambda i: (0, i, 0))
    pw = len(POOL_WINDOWS) * POOL_CH
    return pl.pallas_call(
        body, name="mla_bwd", grid=(t // tm,),
        in_specs=[head(HEAD_PAD), head(HEAD_PAD), head(V_DIM), row(d), row(pw), row(d), row(d),
                  _resident((1, d)), _resident(w_in.shape), _resident((1, Q_RANK)), _resident(wq_t.shape),
                  _resident((1, KV_RANK)), _resident(wkv.shape), row(1), _resident(rope_tab.shape)],
        out_specs=[row(d), row(d), row(d), row(d), full((1, Q_RANK)), full((1, KV_RANK)), full((1, d))],
        out_shape=[jax.ShapeDtypeStruct((t, d), F32), jax.ShapeDtypeStruct((t, d), BF16),
                   jax.ShapeDtypeStruct((t, d), BF16), jax.ShapeDtypeStruct((t, d), BF16),
                   jax.ShapeDtypeStruct((1, Q_RANK), F32), jax.ShapeDtypeStruct((1, KV_RANK), F32),
                   jax.ShapeDtypeStruct((1, d), F32)],
        compiler_params=_cparams(),
    )(dq_h, dk_h, dv_h, z, dz_pool, h1, dh2, mix_norm, w_in, q_norm, wq_t, kv_norm, wkv, pos, rope_tab)


def _mem_kv(mem, mem_norm, wmkv):
    n, d = mem.shape

    def body(mem_ref, g_ref, w_ref, memn_ref, k_ref, v_ref):
        y, _ = _rms_fwd(mem_ref[...], g_ref[...])
        yb = y.astype(BF16)
        memn_ref[...] = yb
        for hh in range(MEM_HEADS):
            k_ref[hh] = _dot_nn(yb, w_ref[hh]).astype(BF16)
            v_ref[hh] = _dot_nn(yb, w_ref[MEM_HEADS + hh]).astype(BF16)

    return pl.pallas_call(
        body, name="mem_kv",
        out_shape=[jax.ShapeDtypeStruct((n, d), BF16), jax.ShapeDtypeStruct((MEM_HEADS, n, MEM_HD), BF16),
                   jax.ShapeDtypeStruct((MEM_HEADS, n, MEM_HD), BF16)],
        compiler_params=_cparams(),
    )(mem, mem_norm, wmkv)


def _mem_softmax(qb, km):
    s = _dot_nt(qb, km) * MEM_SCALE
    e = jnp.exp(s - jnp.max(s, axis=-1, keepdims=True))
    return e / jnp.sum(e, axis=-1, keepdims=True)


def _xattn_fwd(h1, a, p, w_out, g, wmq, km, vm, wmo):
    t, d = h1.shape
    tm = min(512, t)
    half = a.shape[1]

    def body(h_ref, a_ref, p_ref, wo_ref, g_ref, wmq_ref, km_ref, vm_ref, wmo_ref,
             h2_ref, h3_ref, hn_ref, q_ref, o_ref):
        h2 = h_ref[...] + _dot_nn(a_ref[...], wo_ref[0:half, :]) + _dot_nn(p_ref[...], wo_ref[half:2 * half, :])
        h2_ref[...] = h2
        hn, _ = _rms_fwd(h2, g_ref[...])
        hnb = hn.astype(BF16)
        hn_ref[...] = hnb
        qb = _dot_nn(hnb, wmq_ref[...]).astype(BF16)
        q_ref[...] = qb
        outs = []
        for hh in range(MEM_HEADS):
            pr = _mem_softmax(qb[:, hh * MEM_HD:(hh + 1) * MEM_HD], km_ref[hh])
            outs.append(_dot_nn(pr.astype(BF16), vm_ref[hh]))
        ob = jnp.concatenate(outs, axis=-1).astype(BF16)
        o_ref[...] = ob
        h3_ref[...] = h2 + _dot_nn(ob, wmo_ref[...])

    full = lambda shape: pl.BlockSpec(shape, lambda i: (0,) * len(shape))
    row = lambda w: pl.BlockSpec((tm, w), lambda i: (i, 0))
    return pl.pallas_call(
        body, name="xattn_fwd", grid=(t // tm,),
        in_specs=[row(d), row(half), row(half), _resident(w_out.shape), _resident((1, d)), _resident(wmq.shape),
                  _resident(km.shape), _resident(vm.shape), _resident(wmo.shape)],
        out_specs=[row(d), row(d), row(d), row(d), row(d)],
        out_shape=[jax.ShapeDtypeStruct((t, d), F32), jax.ShapeDtypeStruct((t, d), F32),
                   jax.ShapeDtypeStruct((t, d), BF16), jax.ShapeDtypeStruct((t, d), BF16),
                   jax.ShapeDtypeStruct((t, d), BF16)],
        compiler_params=_cparams(),
    )(h1, a, p, w_out, g, wmq, km, vm, wmo)


def _xattn_bwd(dh3, h2, qm, g, wmq, km, vm, wmo, w_out, token=None):
    t, d = h2.shape
    tm = min(512, t)
    half = d // 2

    def body(dh3_ref, h2_ref, q_ref, g_ref, wmq_ref, km_ref, vm_ref, wmo_ref, wo_ref,
             dh2_ref, dq_ref, da_ref, dp_ref, dk_ref, dv_ref, dg_ref):
        i = pl.program_id(0)
        first = i == 0

        @pl.when(first)
        def _():
            dk_ref[...] = jnp.zeros_like(dk_ref)
            dv_ref[...] = jnp.zeros_like(dv_ref)

        dh3 = dh3_ref[...]
        dob = _dot_nt(dh3.astype(BF16), wmo_ref[...]).astype(BF16)
        qb = q_ref[...]
        dq_parts = []
        for hh in range(MEM_HEADS):
            cols = slice(hh * MEM_HD, (hh + 1) * MEM_HD)
            kk, vv = km_ref[hh], vm_ref[hh]
            pr = _mem_softmax(qb[:, cols], kk)
            doh = dob[:, cols]
            dv_ref[hh] += _dot_tn(pr.astype(BF16), doh)
            dpp = _dot_nt(doh, vv)
            dsb = (pr * (dpp - jnp.sum(dpp * pr, axis=-1, keepdims=True)) * MEM_SCALE).astype(BF16)
            dq_parts.append(_dot_nn(dsb, kk))
            dk_ref[hh] += _dot_tn(dsb, qb[:, cols])
        dqb = jnp.concatenate(dq_parts, axis=-1).astype(BF16)
        dq_ref[...] = dqb
        x = h2_ref[...]
        gg = g_ref[...]
        _, r = _rms_fwd(x, gg)
        dx, dg = _rms_bwd(_dot_nt(dqb, wmq_ref[...]), x, gg, r)
        dh2 = dh3 + dx
        dh2_ref[...] = dh2
        dap = _dot_nt(dh2.astype(BF16), wo_ref[...])
        da_ref[...] = dap[:, 0:half].astype(BF16)
        dp_ref[...] = dap[:, half:d].astype(BF16)
        _accumulate(dg_ref, dg, first)

    full = lambda shape: pl.BlockSpec(shape, lambda i: (0,) * len(shape))
    row = lambda w: pl.BlockSpec((tm, w), lambda i: (i, 0))
    return _call_after(
        token, body,
        [row(d), row(d), row(d), _resident((1, d)), _resident(wmq.shape), _resident(km.shape), _resident(vm.shape),
         _resident(wmo.shape), _resident(w_out.shape)],
        (dh3, h2, qm, g, wmq, km, vm, wmo, w_out),
        name="xattn_bwd", grid=(t // tm,),
        out_specs=[row(d), row(d), row(half), row(half), full(km.shape), full(vm.shape), full((1, d))],
        out_shape=[jax.ShapeDtypeStruct((t, d), F32), jax.ShapeDtypeStruct((t, d), BF16),
                   jax.ShapeDtypeStruct((t, half), BF16), jax.ShapeDtypeStruct((t, half), BF16),
                   jax.ShapeDtypeStruct(km.shape, F32), jax.ShapeDtypeStruct(vm.shape, F32),
                   jax.ShapeDtypeStruct((1, d), F32)],
        compiler_params=_cparams(),
    )


def _mem_kv_bwd(dkm, dvm, memn, mem, mem_norm, wmkv):
    n, d = mem.shape

    def body(dk_ref, dv_ref, memn_ref, mem_ref, g_ref, w_ref, dw_ref, dg_ref):
        memn = memn_ref[...]
        dmemn = jnp.zeros((n, d), F32)
        for s in range(2 * MEM_HEADS):
            src = dk_ref[s] if s < MEM_HEADS else dv_ref[s - MEM_HEADS]
            db = src.astype(BF16)
            dw_ref[s] = _dot_tn(memn, db)
            dmemn = dmemn + _dot_nt(db, w_ref[s])
        x = mem_ref[...]
        gg = g_ref[...]
        _, r = _rms_fwd(x, gg)
        _, dg = _rms_bwd(dmemn, x, gg, r)
        dg_ref[...] = dg

    return pl.pallas_call(
        body, name="mem_kv_bwd",
        out_shape=[jax.ShapeDtypeStruct(wmkv.shape, F32), jax.ShapeDtypeStruct((1, d), F32)],
        compiler_params=_cparams(),
    )(dkm, dvm, memn, mem, mem_norm, wmkv)


MESH_ID = pl.DeviceIdType.MESH
ANY = pl.BlockSpec(memory_space=pl.ANY)


def _coords():
    return lax.axis_index("x"), lax.axis_index("y"), lax.axis_index("c")


def _other_chips(x, y):
    return [(1 - x, y), (x, 1 - y), (1 - x, 1 - y)]


def _core_reduce(g, tag):
    _, r, w = g.shape

    def body(g_ref, part_ref, own_sc, recv_sc, send_sems, recv_sems, local_sems):
        x, y, c = _coords()
        sent, local = [], []
        for chip in range(4):
            sent.append(pltpu.make_async_remote_copy(
                src_ref=g_ref.at[2 * chip + (1 - c)], dst_ref=recv_sc.at[chip],
                send_sem=send_sems.at[chip], recv_sem=recv_sems.at[chip],
                device_id=(x, y, 1 - c), device_id_type=MESH_ID))
            local.append(pltpu.make_async_copy(g_ref.at[2 * chip + c], own_sc.at[chip], local_sems.at[chip]))
        for cp in sent + local:
            cp.start()
        for chip in range(4):
            local[chip].wait()
            sent[chip].wait_recv()
            part_ref[chip] = (own_sc[chip].astype(F32) + recv_sc[chip].astype(F32)).astype(part_ref.dtype)
        for cp in sent:
            cp.wait_send()

    return pl.pallas_call(
        body, name="core_reduce_" + tag,
        out_shape=jax.ShapeDtypeStruct((4, r, w), g.dtype),
        in_specs=[ANY], out_specs=pl.BlockSpec(memory_space=pltpu.VMEM),
        scratch_shapes=[pltpu.VMEM((4, r, w), g.dtype), pltpu.VMEM((4, r, w), g.dtype),
                        pltpu.SemaphoreType.DMA((4,)), pltpu.SemaphoreType.DMA((4,)), pltpu.SemaphoreType.DMA((4,))],
        compiler_params=_cparams(),
    )(g)


HBM_SPEC = pl.BlockSpec(memory_space=pltpu.HBM)
SEM_SPEC = pl.BlockSpec(memory_space=pltpu.SEMAPHORE)
SPLIT_EFFECT = pltpu.SideEffectType.DATAFLOW_SIDE_EFFECTING


def _ici_refs(gather, src_ref, land_ref, j, px, py, slot_chip, c):
    if gather:
        return src_ref, land_ref.at[:, 4 * slot_chip[0] + 2 * slot_chip[1] + c]
    return src_ref.at[2 * px + py], land_ref.at[j]


def _ici_start(src, after, name, gather):
    r, w = src.shape[-2:]
    land_shape = (src.shape[0], N_DEV, r, w) if gather else (3, r, w)

    def body(src_ref, land_ref, after_ref, send_sems, recv_sems, src_thru, land_thru, token):
        x, y, c = _coords()
        for j, (px, py) in enumerate(_other_chips(x, y)):
            s_ref, d_ref = _ici_refs(gather, src_ref, land_ref, j, px, py, (x, y), c)
            pltpu.make_async_remote_copy(
                src_ref=s_ref, dst_ref=d_ref, send_sem=send_sems.at[j], recv_sem=recv_sems.at[j],
                device_id=(px, py, c), device_id_type=MESH_ID).start()
        token[...] = jnp.zeros_like(token)

    return pl.pallas_call(
        body, name=name,
        out_shape=(pltpu.SemaphoreType.DMA((3,)), pltpu.SemaphoreType.DMA((3,)), pltpu.HBM(src.shape, src.dtype),
                   pltpu.HBM(land_shape, src.dtype), jax.ShapeDtypeStruct((8, 128), F32)),
        in_specs=(HBM_SPEC, HBM_SPEC, ANY),
        out_specs=(SEM_SPEC, SEM_SPEC, HBM_SPEC, HBM_SPEC, pl.BlockSpec(memory_space=pltpu.VMEM)),
        input_output_aliases={0: 2, 1: 3},
        compiler_params=pltpu.CompilerParams(has_side_effects=SPLIT_EFFECT),
    )(pltpu.with_memory_space_constraint(src, pltpu.HBM),
      pltpu.with_memory_space_constraint(lax.empty(land_shape, src.dtype), pltpu.HBM), after)


def _ici_wait(started, after, name, gather):
    send_sems, recv_sems, src_thru, land_thru, _ = started

    def body(src_ref, land_ref, send_sems, recv_sems, after_ref, src_dead, got_ref):
        x, y, c = _coords()
        for j, (px, py) in enumerate(_other_chips(x, y)):
            s_ref, d_ref = _ici_refs(gather, src_ref, land_ref, j, px, py, (px, py), c)
            copy = pltpu.make_async_remote_copy(
                src_ref=s_ref, dst_ref=d_ref, send_sem=send_sems.at[j], recv_sem=recv_sems.at[j],
                device_id=(px, py, c), device_id_type=MESH_ID)
            copy.wait_send()
            copy.wait_recv()

    return pl.pallas_call(
        body, name=name,
        out_shape=(pltpu.HBM(src_thru.shape, src_thru.dtype), pltpu.HBM(land_thru.shape, land_thru.dtype)),
        in_specs=(HBM_SPEC, HBM_SPEC, SEM_SPEC, SEM_SPEC, ANY),
        out_specs=(HBM_SPEC, HBM_SPEC), input_output_aliases={0: 0, 1: 1},
        compiler_params=pltpu.CompilerParams(has_side_effects=SPLIT_EFFECT),
    )(src_thru, land_thru, send_sems, recv_sems, after)


def _peer(k, x, y, c):
    return x ^ ((k >> 2) & 1), y ^ ((k >> 1) & 1), c ^ (k & 1)


def _peers_start(src, after, name):
    r, w = src.shape
    x, y, c = _coords()
    land = lax.dynamic_update_slice(jnp.zeros((N_DEV, r, w), src.dtype), src[None], (4 * x + 2 * y + c, 0, 0))

    def body(src_ref, land_ref, after_ref, send_sems, recv_sems, src_thru, land_thru, token):
        x, y, c = _coords()
        for k in range(1, N_DEV):
            pltpu.make_async_remote_copy(
                src_ref=src_ref, dst_ref=land_ref.at[4 * x + 2 * y + c],
                send_sem=send_sems.at[k - 1], recv_sem=recv_sems.at[k - 1],
                device_id=_peer(k, x, y, c), device_id_type=MESH_ID).start()
        token[...] = jnp.zeros_like(token)

    return pl.pallas_call(
        body, name=name,
        out_shape=(pltpu.SemaphoreType.DMA((N_DEV - 1,)), pltpu.SemaphoreType.DMA((N_DEV - 1,)),
                   pltpu.HBM(src.shape, src.dtype), pltpu.HBM(land.shape, src.dtype),
                   jax.ShapeDtypeStruct((8, 128), F32)),
        in_specs=(HBM_SPEC, HBM_SPEC, ANY),
        out_specs=(SEM_SPEC, SEM_SPEC, HBM_SPEC, HBM_SPEC, pl.BlockSpec(memory_space=pltpu.VMEM)),
        input_output_aliases={0: 2, 1: 3},
        compiler_params=pltpu.CompilerParams(has_side_effects=SPLIT_EFFECT),
    )(pltpu.with_memory_space_constraint(src, pltpu.HBM), pltpu.with_memory_space_constraint(land, pltpu.HBM), after)


def _peers_wait(started, after, name):
    send_sems, recv_sems, src_thru, land_thru, _ = started

    def body(src_ref, land_ref, send_sems, recv_sems, after_ref, src_dead, got_ref):
        x, y, c = _coords()
        for k in range(1, N_DEV):
            px, py, pc = _peer(k, x, y, c)
            copy = pltpu.make_async_remote_copy(
                src_ref=src_ref, dst_ref=land_ref.at[4 * px + 2 * py + pc],
                send_sem=send_sems.at[k - 1], recv_sem=recv_sems.at[k - 1],
                device_id=(px, py, pc), device_id_type=MESH_ID)
            copy.wait_send()
            copy.wait_recv()

    return pl.pallas_call(
        body, name=name,
        out_shape=(pltpu.HBM(src_thru.shape, src_thru.dtype), pltpu.HBM(land_thru.shape, land_thru.dtype)),
        in_specs=(HBM_SPEC, HBM_SPEC, SEM_SPEC, SEM_SPEC, ANY),
        out_specs=(HBM_SPEC, HBM_SPEC), input_output_aliases={0: 0, 1: 1},
        compiler_params=pltpu.CompilerParams(has_side_effects=SPLIT_EFFECT),
    )(src_thru, land_thru, send_sems, recv_sems, after)


def _core_share(own, gathered, name):
    def body(own_ref, gin_ref, out_ref, stage, send_sems, recv_sems, local_sem):
        x, y, c = _coords()
        sibling = (x, y, 1 - c)
        chips = [(x, y)] + _other_chips(x, y)
        stage_in = pltpu.make_async_copy(own_ref, stage, local_sem)
        stage_in.start()
        sent, arriving = [], []
        for k, (px, py) in enumerate(chips):
            slot = out_ref.at[:, 4 * px + 2 * py + c]
            sent.append(pltpu.make_async_remote_copy(
                src_ref=own_ref if k == 0 else slot, dst_ref=slot,
                send_sem=send_sems.at[k], recv_sem=recv_sems.at[k], device_id=sibling, device_id_type=MESH_ID))
            arriving.append(pltpu.make_async_remote_copy(
                src_ref=own_ref, dst_ref=out_ref.at[:, 4 * px + 2 * py + (1 - c)],
                send_sem=send_sems.at[k], recv_sem=recv_sems.at[k], device_id=sibling, device_id_type=MESH_ID))
        for cp in sent:
            cp.start()
        stage_in.wait()
        stage_out = pltpu.make_async_copy(stage, out_ref.at[:, 4 * x + 2 * y + c], local_sem)
        stage_out.start()
        for cp in arriving:
            cp.wait_recv()
        for cp in sent:
            cp.wait_send()
        stage_out.wait()

    return pl.pallas_call(
        body, name=name,
        out_shape=jax.ShapeDtypeStruct(gathered.shape, own.dtype),
        in_specs=[ANY, ANY], out_specs=ANY, input_output_aliases={1: 0},
        scratch_shapes=[pltpu.VMEM(own.shape, own.dtype), pltpu.SemaphoreType.DMA((4,)),
                        pltpu.SemaphoreType.DMA((4,)), pltpu.SemaphoreType.DMA],
    )(own, gathered)


def _adamw(w, g, m, v):
    m = ADAM_B1 * m + (1.0 - ADAM_B1) * g
    v = ADAM_B2 * v + (1.0 - ADAM_B2) * (g * g)
    m_hat = m / ADAM_C1
    v_hat = v / ADAM_C2
    delta = -ADAM_LR * (m_hat / (jnp.sqrt(v_hat) + ADAM_EPS) + ADAM_WD * w)
    return delta, m, v


def _adam_big(part, land, w, m, v, chip_idx, tag, token):
    r, wd = w.shape
    tr, tw = _row_tile(r, 1024), 256

    def body(s_ref, tok_ref, p_ref, l_ref, w_ref, m_ref, v_ref, g_ref, d_ref, mo_ref, vo_ref):
        g = p_ref[0].astype(F32)
        for j in range(3):
            g = g + l_ref[j].astype(F32)
        delta, mn, vn = _adamw(w_ref[...], g, m_ref[...], v_ref[...])
        g_ref[...] = g
        d_ref[...] = delta
        mo_ref[...] = mn
        vo_ref[...] = vn

    row = pl.BlockSpec((tr, tw), lambda i, j, s: (i, j))
    return pl.pallas_call(
        body, name="adam_big_" + tag,
        grid_spec=pltpu.PrefetchScalarGridSpec(
            num_scalar_prefetch=1, grid=(r // tr, wd // tw),
            in_specs=[pl.BlockSpec((8, 128), lambda i, j, s: (0, 0)),
                      pl.BlockSpec((1, tr, tw), lambda i, j, s: (s[0], i, j)),
                      pl.BlockSpec((3, tr, tw), lambda i, j, s: (0, i, j)), row, row, row],
            out_specs=[row, row, row, row]),
        out_shape=[jax.ShapeDtypeStruct((r, wd), F32)] * 4,
        compiler_params=_cparams(),
    )(chip_idx, token, part, land, w, m, v)


def _adam_small(parts, w, m, v):
    _, r, wd = parts.shape

    def body(p_ref, w_ref, m_ref, v_ref, g_ref, d_ref, mo_ref, vo_ref):
        g = p_ref[0]
        for k in range(1, N_DEV):
            g = g + p_ref[k]
        delta, mn, vn = _adamw(w_ref[...], g, m_ref[...], v_ref[...])
        g_ref[...] = g
        d_ref[...] = delta
        mo_ref[...] = mn
        vo_ref[...] = vn

    return pl.pallas_call(
        body, name="adam_small",
        out_shape=[jax.ShapeDtypeStruct((r, wd), F32)] * 4,
        compiler_params=_cparams(),
    )(parts, w, m, v)


def _pad_rows(a, rows):
    return jnp.pad(a, ((0, rows - a.shape[0]), (0, 0)))


def _pad_w_in(w):
    cut = Q_RANK + KV_RANK + ROPE
    return jnp.concatenate([w[:, :cut], jnp.zeros((w.shape[0], 64), w.dtype), w[:, cut:]], axis=1)


def _unpad_w_in(w):
    cut = Q_RANK + KV_RANK + ROPE
    return jnp.concatenate([w[:, :cut], w[:, cut + 64:]], axis=1)


def _pack_mid(p):
    parts = [_pad_w_in(p["w_in"][0]), p["w_out"][0], p["w_mq"][0], p["w_mo"][0],
             p["w_mkv"][0].reshape(256, D_MODEL),
             _pad_rows(p["w_q_up"][0].T.reshape(24, D_MODEL), 32),
             p["w_kv_up"][0].reshape(16, D_MODEL)]
    return jnp.concatenate(parts, axis=0)


def _pack_segments(p, group):
    if group == "mid":
        return _pack_mid(p)[None]
    return jnp.stack([p[group + "_w_gate"][0].T, p[group + "_w_up"][0].T, p[group + "_w_down"][0]])


UNIT_WEIGHT = {"ffn1_g": ("ffn1_w_gate", True), "ffn1_u": ("ffn1_w_up", True), "ffn1_d": ("ffn1_w_down", False),
               "ffn2_g": ("ffn2_w_gate", True), "ffn2_u": ("ffn2_w_up", True), "ffn2_d": ("ffn2_w_down", False)}


def _pack_unit(p, unit):
    if unit == "mid":
        return _pack_mid(p)
    name, transposed = UNIT_WEIGHT[unit]
    return p[name][0].T if transposed else p[name][0]


def _unpack_unit(a, unit):
    if unit != "mid":
        name, transposed = UNIT_WEIGHT[unit]
        return {name: (a.T if transposed else a)[None]}
    seg = lambda n: a[SEG_OFF[n][0]:SEG_OFF[n][0] + SEG_OFF[n][1]]
    return {"w_in": _unpad_w_in(seg("w_in"))[None], "w_out": seg("w_out")[None], "w_mq": seg("w_mq")[None],
            "w_mo": seg("w_mo")[None], "w_mkv": seg("w_mkv").reshape(D_MODEL, 256)[None],
            "w_q_up": seg("w_q")[:24].reshape(96, Q_RANK).T[None],
            "w_kv_up": seg("w_kv").reshape(KV_RANK, 128)[None]}


def _unpack_gathered(full, group):
    if group != "mid":
        return {n: full[k].reshape(-1, D_MODEL) for k, (n, _) in enumerate(GROUP_SEGS[group])}
    full = full[0]
    seg = lambda n: full[:, SEG_OFF[n][0]:SEG_OFF[n][0] + SEG_OFF[n][1]]
    rows = lambda n: seg(n).reshape(-1, D_MODEL)
    wq_t = seg("w_q")[:, :24].reshape(MLA_HEADS, NOPE + ROPE, Q_RANK)
    wq_t = jnp.pad(wq_t, ((0, 0), (0, HEAD_PAD - NOPE - ROPE), (0, 0))).reshape(MLA_HEADS * HEAD_PAD, Q_RANK)
    wkv = seg("w_kv").reshape(N_DEV, KV_RANK, 128).transpose(1, 0, 2).reshape(KV_RANK, N_DEV * 128)
    return {"w_in": rows("w_in"), "w_out": rows("w_out"), "w_mq": rows("w_mq"), "w_mo": rows("w_mo"),
            "w_mkv": seg("w_mkv").reshape(N_DEV, D_MODEL, 256), "w_q": wq_t, "w_kv": wkv}


def _pack_grads(gr):
    blk = lambda a: a.reshape(N_DEV, -1, D_MODEL)
    dwq = gr["w_q"].reshape(MLA_HEADS, HEAD_PAD, Q_RANK)[:, :NOPE + ROPE].reshape(N_DEV, 24, D_MODEL)
    dwq = jnp.pad(dwq, ((0, 0), (0, 8), (0, 0)))
    dwkv = gr["w_kv"].reshape(KV_RANK, N_DEV, 128).transpose(1, 0, 2).reshape(N_DEV, 16, D_MODEL)
    parts = [blk(gr["w_in"]), blk(gr["w_out"]), blk(gr["w_mq"]), blk(gr["w_mo"]),
             gr["w_mkv"].reshape(N_DEV, 256, D_MODEL), dwq, dwkv]
    return jnp.concatenate([a.astype(BF16) for a in parts], axis=1)


def _pack_small(vals):
    parts = []
    for n, r in SMALL_ROWS:
        parts.append(_pad_rows(vals[n].reshape(-1, 128), r) if n in vals else jnp.zeros((r, 128), F32))
    return jnp.concatenate(parts, axis=0)


def _unpack_small(a, shapes):
    out = {}
    for n, shape in shapes.items():
        o = SMALL_OFF[n][0]
        out[n] = a[o:o + int(np.prod(shape)) // 128].reshape(shape)
    return out


BIG_NAMES = ("ffn1_w_gate", "ffn1_w_up", "ffn1_w_down", "w_in", "w_q_up", "w_kv_up", "w_out", "w_mq", "w_mkv",
             "w_mo", "ffn2_w_gate", "ffn2_w_up", "ffn2_w_down")
SMALL_NAMES = ("ffn1_norm", "mix_norm", "q_norm", "kv_norm", "pool_w", "pool_scale", "xattn_norm", "mem_norm",
               "ffn2_norm", "final_norm")
WEIGHT_ORDER = ("ffn1_norm", "ffn1_w_gate", "ffn1_w_up", "ffn1_w_down", "mix_norm", "w_in", "q_norm", "w_q_up",
                "kv_norm", "w_kv_up", "pool_w", "pool_scale", "w_out", "xattn_norm", "mem_norm", "w_mq", "w_mkv",
                "w_mo", "ffn2_norm", "ffn2_w_gate", "ffn2_w_up", "ffn2_w_down", "final_norm")


def _rope_table():
    lane = np.arange(128)
    freqs = (1.0 / (ROPE_BASE ** (np.arange(0, ROPE, 2, dtype=np.float32) / ROPE))).astype(np.float32)
    tab = np.zeros((8, 128), np.float32)
    tab[0] = np.where(lane < ROPE, freqs[lane % (ROPE // 2)], 0.0)
    tab[1] = np.where(lane < ROPE // 2, -1.0, np.where(lane < ROPE, 1.0, 0.0))
    return jnp.asarray(tab)


def kernel(x, mem, positions, ffn1_norm, ffn1_w_gate, ffn1_w_up, ffn1_w_down, mix_norm, w_in, q_norm, w_q_up, kv_norm, w_kv_up, pool_w, pool_scale, w_out, xattn_norm, mem_norm, w_mq, w_mkv, w_mo, ffn2_norm, ffn2_w_gate, ffn2_w_up, ffn2_w_down, final_norm, loss_target, m_ffn1_norm, m_ffn1_w_gate, m_ffn1_w_up, m_ffn1_w_down, m_mix_norm, m_w_in, m_q_norm, m_w_q_up, m_kv_norm, m_w_kv_up, m_pool_w, m_pool_scale, m_w_out, m_xattn_norm, m_mem_norm, m_w_mq, m_w_mkv, m_w_mo, m_ffn2_norm, m_ffn2_w_gate, m_ffn2_w_up, m_ffn2_w_down, m_final_norm, v_ffn1_norm, v_ffn1_w_gate, v_ffn1_w_up, v_ffn1_w_down, v_mix_norm, v_w_in, v_q_norm, v_w_q_up, v_kv_norm, v_w_kv_up, v_pool_w, v_pool_scale, v_w_out, v_xattn_norm, v_mem_norm, v_w_mq, v_w_mkv, v_w_mo, v_ffn2_norm, v_ffn2_w_gate, v_ffn2_w_up, v_ffn2_w_down, v_final_norm):
    wts = dict(ffn1_norm=ffn1_norm, ffn1_w_gate=ffn1_w_gate, ffn1_w_up=ffn1_w_up, ffn1_w_down=ffn1_w_down,
               mix_norm=mix_norm, w_in=w_in, q_norm=q_norm, w_q_up=w_q_up, kv_norm=kv_norm, w_kv_up=w_kv_up,
               pool_w=pool_w, pool_scale=pool_scale, w_out=w_out, xattn_norm=xattn_norm, mem_norm=mem_norm,
               w_mq=w_mq, w_mkv=w_mkv, w_mo=w_mo, ffn2_norm=ffn2_norm, ffn2_w_gate=ffn2_w_gate,
               ffn2_w_up=ffn2_w_up, ffn2_w_down=ffn2_w_down, final_norm=final_norm)
    mom = dict(ffn1_norm=m_ffn1_norm, ffn1_w_gate=m_ffn1_w_gate, ffn1_w_up=m_ffn1_w_up, ffn1_w_down=m_ffn1_w_down,
               mix_norm=m_mix_norm, w_in=m_w_in, q_norm=m_q_norm, w_q_up=m_w_q_up, kv_norm=m_kv_norm,
               w_kv_up=m_w_kv_up, pool_w=m_pool_w, pool_scale=m_pool_scale, w_out=m_w_out, xattn_norm=m_xattn_norm,
               mem_norm=m_mem_norm, w_mq=m_w_mq, w_mkv=m_w_mkv, w_mo=m_w_mo, ffn2_norm=m_ffn2_norm,
               ffn2_w_gate=m_ffn2_w_gate, ffn2_w_up=m_ffn2_w_up, ffn2_w_down=m_ffn2_w_down, final_norm=m_final_norm)
    var = dict(ffn1_norm=v_ffn1_norm, ffn1_w_gate=v_ffn1_w_gate, ffn1_w_up=v_ffn1_w_up, ffn1_w_down=v_ffn1_w_down,
               mix_norm=v_mix_norm, w_in=v_w_in, q_norm=v_q_norm, w_q_up=v_w_q_up, kv_norm=v_kv_norm,
               w_kv_up=v_w_kv_up, pool_w=v_pool_w, pool_scale=v_pool_scale, w_out=v_w_out, xattn_norm=v_xattn_norm,
               mem_norm=v_mem_norm, w_mq=v_w_mq, w_mkv=v_w_mkv, w_mo=v_w_mo, ffn2_norm=v_ffn2_norm,
               ffn2_w_gate=v_ffn2_w_gate, ffn2_w_up=v_ffn2_w_up, ffn2_w_down=v_ffn2_w_down, final_norm=v_final_norm)

    t = x.shape[1]
    xs = x[0]
    mems = mem[0]
    target = loss_target[0]
    pos = positions.reshape(t, 1)
    row = lambda a: a.reshape(1, -1)
    rope_tab = _rope_table()

    cx, cy, cc = _coords()
    chip_idx = (2 * cx + cy).astype(jnp.int32).reshape(1)

    wb = {}
    for grp in ("ffn1", "mid", "ffn2"):
        wb[grp] = _pack_segments(wts, grp).astype(BF16)
        if grp == "ffn1":
            ag_ffn1 = _ici_start(wb["ffn1"], pos, "ag_ffn1_start", True)
    own_ffn1, land_ffn1 = _ici_wait(ag_ffn1, wb["ffn2"], "ag_ffn1_wait", True)
    full_ffn1 = _core_share(own_ffn1, land_ffn1, "ag_ffn1_share")
    fw = _unpack_gathered(full_ffn1, "ffn1")
    ag_mid = _ici_start(wb["mid"], full_ffn1, "ag_mid_start", True)
    g_ffn1, g_mix, g_q, g_kv = row(ffn1_norm), row(mix_norm), row(q_norm), row(kv_norm)
    g_x, g_mem, g_ffn2, g_fin = row(xattn_norm), row(mem_norm), row(ffn2_norm), row(final_norm)
    pool_wb = pool_w[0].astype(BF16)
    pool_sc = row(pool_scale)

    h1, n1, gate1, up1 = _ffn_fwd(xs, g_ffn1, fw["ffn1_g"], fw["ffn1_u"], fw["ffn1_d"], "ffn1_fwd", token=ag_mid[4])
    own_mid, land_mid = _ici_wait(ag_mid, h1, "ag_mid_wait", True)
    full_mid = _core_share(own_mid, land_mid, "ag_mid_share")
    fw.update(_unpack_gathered(full_mid, "mid"))
    ag_ffn2 = _ici_start(wb["ffn2"], full_mid, "ag_ffn2_start", True)
    u, z, qn, kvn, qh, kh, vh = _mix_prep(h1, g_mix, fw["w_in"], g_q, fw["w_q"], g_kv, fw["w_kv"], pos, rope_tab,
                                          token=ag_ffn2[4])
    a, lse = _attn_fwd(qh, kh, vh)
    p = _pool_fwd(z, pool_wb, pool_sc)
    memn, km, vm = _mem_kv(mems, g_mem, fw["w_mkv"])
    h2, h3, hn, qm, om = _xattn_fwd(h1, a, p, fw["w_out"], g_x, fw["w_mq"], km, vm, fw["w_mo"])
    own_ffn2, land_ffn2 = _ici_wait(ag_ffn2, h3, "ag_ffn2_wait", True)
    fw.update(_unpack_gathered(_core_share(own_ffn2, land_ffn2, "ag_ffn2_share"), "ffn2"))
    h4, n2, gate2, up2 = _ffn_fwd(h3, g_ffn2, fw["ffn2_g"], fw["ffn2_u"], fw["ffn2_d"], "ffn2_fwd")
    loss_part, dh4, dg_fin = _loss_head(h4, target, g_fin)

    def reduce_start(g8, unit):
        part = _core_reduce(g8, unit)
        return _ici_start(part, g8, "rs_" + unit + "_start", False)

    def by_device(g):
        return g.reshape(N_DEV, -1, D_MODEL)

    rs = {}
    dh3, dgate2, dup2, act2, dg_ffn2 = _ffn_bwd_data(dh4, h3, g_ffn2, gate2, up2, fw["ffn2_g"], fw["ffn2_u"],
                                                     fw["ffn2_d"], "ffn2_bwd")
    rs["ffn2_g"] = reduce_start(by_device(_tn_matmul(dgate2, n2, "ffn2_dwg", tmm=1408, out_dtype=BF16)), "ffn2_g")
    rs["ffn2_u"] = reduce_start(by_device(_tn_matmul(dup2, n2, "ffn2_dwu", tmm=1408, out_dtype=BF16,
                                                     token=rs["ffn2_g"][4])), "ffn2_u")
    rs["ffn2_d"] = reduce_start(by_device(_tn_matmul(act2, dh4, "ffn2_dwd", scale=0.5, tmm=1408, out_dtype=BF16,
                                                     token=rs["ffn2_u"][4])), "ffn2_d")
    dh2, dqm, da, dp, dkm, dvm, dg_x = _xattn_bwd(dh3, h2, qm, g_x, fw["w_mq"], km, vm, fw["w_mo"], fw["w_out"],
                                                  token=rs["ffn2_d"][4])
    gr = {}
    gr["w_mo"] = _tn_matmul(om, dh3, "dw_mo", out_dtype=BF16)
    gr["w_mq"] = _tn_matmul(hn, dqm, "dw_mq", out_dtype=BF16)
    gr["w_out"] = jnp.concatenate([_tn_matmul(a, dh2, "dw_out_a", out_dtype=BF16),
                                   _tn_matmul(p, dh2, "dw_out_p", out_dtype=BF16)], axis=0)
    gr["w_mkv"], dg_mem = _mem_kv_bwd(dkm, dvm, memn, mems, g_mem, fw["w_mkv"])
    dz_pool, d_pool_w, d_pool_sc = _pool_bwd(dp, z, pool_wb, pool_sc)
    dqh, dkh, dvh = _attn_bwd(qh, kh, vh, da, lse, _attn_delta(a, da))
    dh1, dq, dkv, dz, dg_q, dg_kv, dg_mix = _mla_bwd(dqh, dkh, dvh, z, dz_pool, h1, dh2, g_mix, fw["w_in"], g_q,
                                                     fw["w_q"], g_kv, fw["w_kv"], pos, rope_tab)
    gr["w_q"] = _tn_matmul(dq, qn, "dw_q", out_dtype=BF16)
    gr["w_kv"] = _tn_matmul(kvn, dkv, "dw_kv", out_dtype=BF16)
    gr["w_in"] = _tn_matmul(u, dz, "dw_in", out_dtype=BF16)
    g_mid = _pack_grads(gr)
    part_mid = _core_reduce(g_mid, "mid")
    got = {}
    after = part_mid
    for unit in ("ffn2_g", "ffn2_u", "ffn2_d"):
        got[unit] = _ici_wait(rs[unit], after, "rs_" + unit + "_wait", False)
        after = got[unit][1]
    rs["mid"] = _ici_start(part_mid, after, "rs_mid_start", False)
    dx, dgate1, dup1, act1, dg_ffn1 = _ffn_bwd_data(dh1, xs, g_ffn1, gate1, up1, fw["ffn1_g"], fw["ffn1_u"],
                                                    fw["ffn1_d"], "ffn1_bwd", token=rs["mid"][4])
    got["mid"] = _ici_wait(rs["mid"], dx, "rs_mid_wait", False)

    small_g = dict(ffn1_norm=dg_ffn1, mix_norm=dg_mix, q_norm=dg_q, kv_norm=dg_kv, pool_w=d_pool_w,
                   pool_scale=d_pool_sc, xattn_norm=dg_x, mem_norm=dg_mem, ffn2_norm=dg_ffn2, final_norm=dg_fin,
                   loss=loss_part)
    small_ag = _peers_start(_pack_small(small_g), got["mid"][1], "small_ag_start")
    rs["ffn1_g"] = reduce_start(by_device(_tn_matmul(dgate1, n1, "ffn1_dwg", tmm=1408, out_dtype=BF16,
                                                     token=small_ag[4])), "ffn1_g")
    _, parts = _peers_wait(small_ag, rs["ffn1_g"][4], "small_ag_wait")
    small = _adam_small(parts, _pack_small({n: wts[n] for n in SMALL_NAMES}),
                        _pack_small({n: mom[n] for n in SMALL_NAMES}), _pack_small({n: var[n] for n in SMALL_NAMES}))
    small_sum = small[0]
    loss = small_sum[SMALL_OFF["loss"][0], 0]
    shapes = {n: wts[n].shape for n in SMALL_NAMES}
    small = [_unpack_small(s, shapes) for s in small]

    rs["ffn1_u"] = reduce_start(by_device(_tn_matmul(dup1, n1, "ffn1_dwu", tmm=1408, out_dtype=BF16,
                                                     token=small_sum)), "ffn1_u")
    rs["ffn1_d"] = reduce_start(by_device(_tn_matmul(act1, dh1, "ffn1_dwd", scale=0.5, tmm=1408, out_dtype=BF16,
                                                     token=rs["ffn1_u"][4])), "ffn1_d")

    big = {}

    def adam_unit(unit, token):
        part, land = got[unit]
        res = _adam_big(part, land, _pack_unit(wts, unit), _pack_unit(mom, unit), _pack_unit(var, unit),
                        chip_idx, unit, token)
        for k, packed in enumerate(res):
            big.setdefault(k, {}).update(_unpack_unit(packed, unit))
        return res[0]

    done = rs["ffn1_d"][4]
    for unit in ("mid", "ffn2_g", "ffn2_u", "ffn2_d"):
        done = adam_unit(unit, done)
    for unit in ("ffn1_g", "ffn1_u", "ffn1_d"):
        got[unit] = _ici_wait(rs[unit], done, "rs_" + unit + "_wait", False)
        done = adam_unit(unit, done)

    outs = [loss, dx[None]]
    for k in range(4):
        for n in WEIGHT_ORDER:
            outs.append(big[k][n] if n in BIG_NAMES else small[k][n])
    return tuple(outs)
```

```python
import numpy as np

import jax
import jax.numpy as jnp
from jax import lax
from jax.experimental import pallas as pl
from jax.experimental.pallas import tpu as pltpu

F32 = jnp.float32
BF16 = jnp.bfloat16

N_DEV = 8
D_MODEL = 1024
D_FF = 2816
MLA_HEADS = 4
NOPE = 128
ROPE = 64
HEAD_PAD = 256
V_DIM = 128
Q_RANK = 256
KV_RANK = 128
POOL_WINDOWS = (2, 4, 8, 16)
POOL_CH = 128
POOL_HALO = 16
N_MEM = 256
MEM_HEADS = 4
MEM_HD = 256
ROPE_BASE = 10000.0
RMS_EPS = 1e-6
ATTN_SCALE = (NOPE + ROPE) ** -0.5
MEM_SCALE = MEM_HD ** -0.5
NEG_BIG = -1e30

ADAM_LR = 0.001
ADAM_B1 = 0.9
ADAM_B2 = 0.999
ADAM_EPS = 1e-08
ADAM_WD = 0.01
ADAM_STEP = 10
ADAM_C1 = 1.0 - ADAM_B1 ** ADAM_STEP
ADAM_C2 = 1.0 - ADAM_B2 ** ADAM_STEP

VMEM_LIMIT_BYTES = 56 * 1024 * 1024
BF16_ROWS = 16

GROUP_SEGS = {
    "ffn1": (("ffn1_g", 352), ("ffn1_u", 352), ("ffn1_d", 352)),
    "mid": (("w_in", 128), ("w_out", 128), ("w_mq", 128), ("w_mo", 128), ("w_mkv", 256), ("w_q", 32), ("w_kv", 16)),
    "ffn2": (("ffn2_g", 352), ("ffn2_u", 352), ("ffn2_d", 352)),
}
SEG_OFF = {}
GROUP_ROWS = {}
for _g, _segs in GROUP_SEGS.items():
    _o = 0
    for _n, _r in _segs:
        SEG_OFF[_n] = (_o, _r)
        _o += _r
    GROUP_ROWS[_g] = _o

SMALL_ROWS = (("ffn1_norm", 8), ("mix_norm", 8), ("q_norm", 8), ("kv_norm", 8), ("pool_w", 512), ("pool_scale", 8),
              ("xattn_norm", 8), ("mem_norm", 8), ("ffn2_norm", 8), ("final_norm", 8), ("loss", 8))
SMALL_OFF = {}
_o = 0
for _n, _r in SMALL_ROWS:
    SMALL_OFF[_n] = (_o, _r)
    _o += _r


def _cparams(**kw):
    return pltpu.CompilerParams(vmem_limit_bytes=VMEM_LIMIT_BYTES, **kw)


def _row_tile(rows, limit):
    best = None
    for cand in range(BF16_ROWS, min(rows, limit) + 1, BF16_ROWS):
        if rows % cand == 0:
            best = cand
    assert best is not None, rows
    return best


def _dot_nn(a, b):
    return lax.dot_general(a, b, (((1,), (0,)), ((), ())), preferred_element_type=F32)


def _dot_nt(a, b):
    return lax.dot_general(a, b, (((1,), (1,)), ((), ())), preferred_element_type=F32)


def _dot_tn(a, b):
    return lax.dot_general(a, b, (((0,), (0,)), ((), ())), preferred_element_type=F32)


def _rms_fwd(x, g):
    r = lax.rsqrt(jnp.mean(x * x, axis=-1, keepdims=True) + RMS_EPS)
    return x * r * g, r


def _rms_bwd(dy, x, g, r):
    xhat = x * r
    dyg = dy * g
    dx = r * (dyg - xhat * jnp.mean(dyg * xhat, axis=-1, keepdims=True))
    dg = jnp.sum(dy * xhat, axis=0, keepdims=True)
    return dx, dg


def _accumulate(ref, val, first):
    if isinstance(first, bool):
        if first:
            ref[...] = val
        else:
            ref[...] += val
        return

    @pl.when(first)
    def _():
        ref[...] = val

    @pl.when(jnp.logical_not(first))
    def _():
        ref[...] += val


def _call_after(token, body, in_specs, args, **kw):
    if token is not None:
        inner = body
        body = lambda tok_ref, *refs: inner(*refs)
        in_specs = [pl.BlockSpec((8, 128), lambda *_: (0, 0))] + list(in_specs)
        args = (token,) + tuple(args)
    return pl.pallas_call(body, in_specs=in_specs, **kw)(*args)


def _resident(shape):
    return pl.BlockSpec(shape, lambda *_: (0,) * len(shape), pipeline_mode=pl.Buffered(1))


def _rope_tables(pos_col, tab):
    ang = pos_col.astype(F32) * tab[0:1, :]
    return jnp.cos(ang), jnp.sin(ang) * tab[1:2, :]


def _swap_halves(x):
    lane = lax.broadcasted_iota(jnp.int32, x.shape, 1)
    return jnp.where((lane % 64) < 32, pltpu.roll(x, 96, 1), pltpu.roll(x, 32, 1))


def _rope_apply(x, cos_t, sin_t):
    return x * cos_t + _swap_halves(x) * sin_t


def _rope_apply_t(dy, cos_t, sin_t):
    return dy * cos_t + _swap_halves(dy * sin_t)


def _ffn_fwd(h, g, wg_t, wu_t, wd, name, token=None):
    t, d = h.shape
    f = wg_t.shape[0]
    tm, tf = min(512, t), 256
    nf = f // tf

    def body(h_ref, g_ref, wg_ref, wu_ref, wd_ref, ho_ref, n_ref, gate_ref, up_ref, nb_sc, acc_sc):
        y, _ = _rms_fwd(h_ref[...], g_ref[...])
        nb = y.astype(BF16)
        nb_sc[...] = nb
        n_ref[...] = nb
        acc_sc[...] = jnp.zeros_like(acc_sc)

        def f_tile(j):
            rows = pl.ds(pl.multiple_of(j * tf, tf), tf)
            nb = nb_sc[...]
            gt = _dot_nn(nb, wg_ref[j])
            ut = _dot_nn(nb, wu_ref[j])
            gate_ref[j] = gt.astype(BF16)
            up_ref[j] = ut.astype(BF16)
            act = (gt * jax.nn.sigmoid(gt)) * ut
            return _dot_nn(act.astype(BF16), wd_ref[rows, :])

        def pair(p, carry):
            acc_sc[...] += f_tile(2 * p) + f_tile(2 * p + 1)
            return carry

        lax.fori_loop(0, nf // 2, pair, 0)
        if nf % 2:
            acc_sc[...] += f_tile(nf - 1)
        ho_ref[...] = h_ref[...] + 0.5 * acc_sc[...]

    return _call_after(
        token, body,
        [pl.BlockSpec((tm, d), lambda i: (i, 0)), _resident((1, d)), _resident((nf, d, tf)), _resident((nf, d, tf)),
         _resident((f, d))],
        (h, g, wg_t.reshape(nf, tf, d).transpose(0, 2, 1), wu_t.reshape(nf, tf, d).transpose(0, 2, 1), wd),
        name=name, grid=(t // tm,),
        out_specs=[pl.BlockSpec((tm, d), lambda i: (i, 0)),
                   pl.BlockSpec((tm, d), lambda i: (i, 0)),
                   pl.BlockSpec((nf, tm, tf), lambda i: (0, i, 0)),
                   pl.BlockSpec((nf, tm, tf), lambda i: (0, i, 0))],
        out_shape=[jax.ShapeDtypeStruct((t, d), F32), jax.ShapeDtypeStruct((t, d), BF16),
                   jax.ShapeDtypeStruct((nf, t, tf), BF16), jax.ShapeDtypeStruct((nf, t, tf), BF16)],
        scratch_shapes=[pltpu.VMEM((tm, d), BF16), pltpu.VMEM((tm, d), F32)],
        compiler_params=_cparams(),
    )


def _ffn_bwd_data(dho, h, g, gate, up, wg_t, wu_t, wd, name, token=None):
    t, d = h.shape
    f = wg_t.shape[0]
    tm, tf = min(1024, t), 256
    parts = 2 if tm % 512 == 0 else 1
    tp = tm // parts
    nf = f // tf

    def body(dho_ref, h_ref, g_ref, gate_ref, up_ref, wg_ref, wu_ref, wd_ref,
             dh_ref, dgate_ref, dup_ref, act_ref, dg_ref, dhb_sc, acc_sc):
        i, j = pl.program_id(0), pl.program_id(1)

        @pl.when(j == 0)
        def _():
            dhb_sc[...] = (0.5 * dho_ref[...]).astype(BF16)
            acc_sc[...] = jnp.zeros_like(acc_sc)

        for r in range(parts):
            rows = pl.ds(r * tp, tp)
            dact = _dot_nn(dhb_sc[rows, :], wd_ref[0])
            gt = gate_ref[0, rows, :].astype(F32)
            ut = up_ref[0, rows, :].astype(F32)
            sg = jax.nn.sigmoid(gt)
            silu = gt * sg
            dgb = (dact * ut * (sg * (1.0 + gt * (1.0 - sg)))).astype(BF16)
            dub = (dact * silu).astype(BF16)
            act_ref[rows, :] = (silu * ut).astype(BF16)
            dgate_ref[rows, :] = dgb
            dup_ref[rows, :] = dub
            acc_sc[rows, :] += _dot_nn(dgb, wg_ref[...]) + _dot_nn(dub, wu_ref[...])

        @pl.when(j == nf - 1)
        def _():
            x = h_ref[...]
            gg = g_ref[...]
            _, r = _rms_fwd(x, gg)
            dx, dg = _rms_bwd(acc_sc[...], x, gg, r)
            dh_ref[...] = dho_ref[...] + dx
            _accumulate(dg_ref, dg, i == 0)

    return _call_after(
        token, body,
        [pl.BlockSpec((tm, d), lambda i, j: (i, 0)),
         pl.BlockSpec((tm, d), lambda i, j: (i, 0)),
         pl.BlockSpec((1, d), lambda i, j: (0, 0)),
         pl.BlockSpec((1, tm, tf), lambda i, j: (j, i, 0)),
         pl.BlockSpec((1, tm, tf), lambda i, j: (j, i, 0)),
         pl.BlockSpec((tf, d), lambda i, j: (j, 0)),
         pl.BlockSpec((tf, d), lambda i, j: (j, 0)),
         pl.BlockSpec((1, d, tf), lambda i, j: (j, 0, 0))],
        (dho, h, g, gate, up, wg_t, wu_t, wd.reshape(nf, tf, d).transpose(0, 2, 1)),
        name=name, grid=(t // tm, nf),
        out_specs=[pl.BlockSpec((tm, d), lambda i, j: (i, 0)),
                   pl.BlockSpec((tm, tf), lambda i, j: (i, j)),
                   pl.BlockSpec((tm, tf), lambda i, j: (i, j)),
                   pl.BlockSpec((tm, tf), lambda i, j: (i, j)),
                   pl.BlockSpec((1, d), lambda i, j: (0, 0))],
        out_shape=[jax.ShapeDtypeStruct((t, d), F32), jax.ShapeDtypeStruct((t, f), BF16),
                   jax.ShapeDtypeStruct((t, f), BF16), jax.ShapeDtypeStruct((t, f), BF16),
                   jax.ShapeDtypeStruct((1, d), F32)],
        scratch_shapes=[pltpu.VMEM((tm, d), BF16), pltpu.VMEM((tm, d), F32)],
        compiler_params=_cparams(),
    )


def _tn_matmul(a, b, name, scale=1.0, tmm=None, out_dtype=F32, token=None):
    t, m = a.shape
    n = b.shape[1]
    tmm = m if tmm is None else tmm
    tk = min(1024, t)
    nk = t // tk

    def product(a_ref, b_ref):
        prod = _dot_tn(a_ref[...].astype(BF16), b_ref[...].astype(BF16))
        return prod * scale if scale != 1.0 else prod

    def body_f32(a_ref, b_ref, o_ref):
        _accumulate(o_ref, product(a_ref, b_ref), pl.program_id(1) == 0)

    def body_cast(a_ref, b_ref, o_ref, acc_sc):
        k = pl.program_id(1)
        _accumulate(acc_sc, product(a_ref, b_ref), k == 0)

        @pl.when(k == nk - 1)
        def _():
            o_ref[...] = acc_sc[...].astype(out_dtype)

    direct = out_dtype == F32
    return _call_after(
        token, body_f32 if direct else body_cast,
        [pl.BlockSpec((tk, tmm), lambda i, k: (k, i)),
         pl.BlockSpec((tk, n), lambda i, k: (k, 0))],
        (a, b),
        name=name, grid=(m // tmm, nk),
        out_specs=pl.BlockSpec((tmm, n), lambda i, k: (i, 0)),
        out_shape=jax.ShapeDtypeStruct((m, n), out_dtype),
        scratch_shapes=[] if direct else [pltpu.VMEM((tmm, n), F32)],
        compiler_params=_cparams(),
    )


def _loss_head(h, target, g):
    t, d = h.shape
    tm = min(512, t)

    def body(h_ref, t_ref, g_ref, loss_ref, dh_ref, dg_ref):
        i = pl.program_id(0)
        x = h_ref[...]
        gg = g_ref[...]
        y, r = _rms_fwd(x, gg)
        err = y - t_ref[...]
        part = 0.5 * jnp.sum(jnp.mean(err * err, axis=-1, keepdims=True), axis=0, keepdims=True)
        dx, dg = _rms_bwd(err * (1.0 / d), x, gg, r)
        dh_ref[...] = dx
        _accumulate(loss_ref, jnp.broadcast_to(part, loss_ref.shape), i == 0)
        _accumulate(dg_ref, dg, i == 0)

    return pl.pallas_call(
        body, name="loss_head", grid=(t // tm,),
        in_specs=[pl.BlockSpec((tm, d), lambda i: (i, 0)),
                  pl.BlockSpec((tm, d), lambda i: (i, 0)),
                  pl.BlockSpec((1, d), lambda i: (0, 0))],
        out_specs=[pl.BlockSpec((8, 128), lambda i: (0, 0)),
                   pl.BlockSpec((tm, d), lambda i: (i, 0)),
                   pl.BlockSpec((1, d), lambda i: (0, 0))],
        out_shape=[jax.ShapeDtypeStruct((8, 128), F32), jax.ShapeDtypeStruct((t, d), F32),
                   jax.ShapeDtypeStruct((1, d), F32)],
        compiler_params=_cparams(),
    )(h, target, g)


def _mix_prep(h1, mix_norm, w_in, q_norm, wq_t, kv_norm, wkv, pos, rope_tab, token=None):
    t, d = h1.shape
    tm = min(512, t)

    def body(h_ref, gm_ref, win_ref, gq_ref, wq_ref, gkv_ref, wkv_ref, pos_ref, tab_ref,
             u_ref, z_ref, qn_ref, kvn_ref, q_ref, k_ref, v_ref):
        u, _ = _rms_fwd(h_ref[...], gm_ref[...])
        ub = u.astype(BF16)
        u_ref[...] = ub
        z = _dot_nn(ub, win_ref[...])
        z_ref[...] = z
        cos_t, sin_t = _rope_tables(pos_ref[...], tab_ref[...])
        qn, _ = _rms_fwd(z[:, 0:Q_RANK], gq_ref[...])
        qnb = qn.astype(BF16)
        qn_ref[...] = qnb
        q = _dot_nt(qnb, wq_ref[...])
        kvn, _ = _rms_fwd(z[:, Q_RANK:Q_RANK + KV_RANK], gkv_ref[...])
        kvnb = kvn.astype(BF16)
        kvn_ref[...] = kvnb
        kv = _dot_nn(kvnb, wkv_ref[...])
        k_pe = _rope_apply(z[:, Q_RANK + KV_RANK:Q_RANK + KV_RANK + 128], cos_t, sin_t)
        ones = jnp.ones((tm, V_DIM), F32)
        for hh in range(MLA_HEADS):
            b = hh * HEAD_PAD
            q_pe = _rope_apply(q[:, b + NOPE:b + HEAD_PAD], cos_t, sin_t)
            q_ref[hh] = jnp.concatenate([q[:, b:b + NOPE], q_pe], axis=-1).astype(BF16)
            k_ref[hh] = jnp.concatenate([kv[:, b:b + NOPE], k_pe], axis=-1).astype(BF16)
            v_ref[hh] = jnp.concatenate([kv[:, b + NOPE:b + HEAD_PAD], ones], axis=-1).astype(BF16)

    full = lambda shape: pl.BlockSpec(shape, lambda i: (0,) * len(shape))
    return _call_after(
        token, body,
        [pl.BlockSpec((tm, d), lambda i: (i, 0)), _resident((1, d)), _resident(w_in.shape), _resident((1, Q_RANK)),
         _resident(wq_t.shape), _resident((1, KV_RANK)), _resident(wkv.shape),
         pl.BlockSpec((tm, 1), lambda i: (i, 0)), _resident(rope_tab.shape)],
        (h1, mix_norm, w_in, q_norm, wq_t, kv_norm, wkv, pos, rope_tab),
        name="mix_prep", grid=(t // tm,),
        out_specs=[pl.BlockSpec((tm, d), lambda i: (i, 0)),
                   pl.BlockSpec((tm, d), lambda i: (i, 0)),
                   pl.BlockSpec((tm, Q_RANK), lambda i: (i, 0)),
                   pl.BlockSpec((tm, KV_RANK), lambda i: (i, 0)),
                   pl.BlockSpec((MLA_HEADS, tm, HEAD_PAD), lambda i: (0, i, 0)),
                   pl.BlockSpec((MLA_HEADS, tm, HEAD_PAD), lambda i: (0, i, 0)),
                   pl.BlockSpec((MLA_HEADS, tm, 2 * V_DIM), lambda i: (0, i, 0))],
        out_shape=[jax.ShapeDtypeStruct((t, d), BF16), jax.ShapeDtypeStruct((t, d), F32),
                   jax.ShapeDtypeStruct((t, Q_RANK), BF16), jax.ShapeDtypeStruct((t, KV_RANK), BF16),
                   jax.ShapeDtypeStruct((MLA_HEADS, t, HEAD_PAD), BF16),
                   jax.ShapeDtypeStruct((MLA_HEADS, t, HEAD_PAD), BF16),
                   jax.ShapeDtypeStruct((MLA_HEADS, t, 2 * V_DIM), BF16)],
        compiler_params=_cparams(),
    )


def _causal_mask(s):
    row = lax.broadcasted_iota(jnp.int32, s.shape, 0)
    col = lax.broadcasted_iota(jnp.int32, s.shape, 1)
    return jnp.where(col <= row, s, NEG_BIG)


def _attn_fwd(q, k, v):
    nh, t, _ = q.shape
    tq = tk = min(512, t)
    nq, nk = t // tq, t // tk

    pairs = [(i, j) for i in range(nq) for j in range(i + 1)]
    qi = jnp.asarray(np.array([i for i, _ in pairs], np.int32))
    kj = jnp.asarray(np.array([j for _, j in pairs], np.int32))

    def body(qi_ref, kj_ref, q_ref, k_ref, v_ref, o_ref, lse_ref, m_sc, acc_sc):
        n = pl.program_id(0)
        i, j = qi_ref[n], kj_ref[n]

        @pl.when(j == 0)
        def _():
            m_sc[...] = jnp.full_like(m_sc, NEG_BIG)
            acc_sc[...] = jnp.zeros_like(acc_sc)

        def step(diagonal):
            for hh in range(nh):
                s = _dot_nt(q_ref[hh], k_ref[hh]) * ATTN_SCALE
                if diagonal:
                    s = _causal_mask(s)
                m_old = m_sc[hh]
                m_new = jnp.maximum(m_old, jnp.max(s, axis=-1, keepdims=True))
                p = jnp.exp(s - m_new).astype(BF16)
                acc_sc[hh] = jnp.exp(m_old - m_new) * acc_sc[hh] + _dot_nn(p, v_ref[hh])
                m_sc[hh] = m_new

        @pl.when(j < i)
        def _():
            step(False)

        @pl.when(j == i)
        def _():
            step(True)
            for hh in range(nh):
                acc = acc_sc[hh]
                l = acc[:, V_DIM:2 * V_DIM]
                o_ref[:, hh * V_DIM:(hh + 1) * V_DIM] = (acc[:, 0:V_DIM] / l).astype(BF16)
                lse_ref[hh] = m_sc[hh] + jnp.log(l[:, 0:1])

    q_map = lambda n, qi_ref, kj_ref: (0, qi_ref[n], 0)
    kv_map = lambda n, qi_ref, kj_ref: (0, kj_ref[n], 0)
    return pl.pallas_call(
        body, name="attn_fwd",
        grid_spec=pltpu.PrefetchScalarGridSpec(
            num_scalar_prefetch=2, grid=(len(pairs),),
            in_specs=[pl.BlockSpec((nh, tq, HEAD_PAD), q_map),
                      pl.BlockSpec((nh, tk, HEAD_PAD), kv_map),
                      pl.BlockSpec((nh, tk, 2 * V_DIM), kv_map)],
            out_specs=[pl.BlockSpec((tq, nh * V_DIM), lambda n, qi_ref, kj_ref: (qi_ref[n], 0)),
                       pl.BlockSpec((nh, tq, 1), q_map)],
            scratch_shapes=[pltpu.VMEM((nh, tq, 1), F32), pltpu.VMEM((nh, tq, 2 * V_DIM), F32)]),
        out_shape=[jax.ShapeDtypeStruct((t, nh * V_DIM), BF16), jax.ShapeDtypeStruct((nh, t, 1), F32)],
        compiler_params=_cparams(),
    )(qi, kj, q, k, v)


def _attn_delta(o, do):
    t, w = o.shape
    nh = w // V_DIM
    tm = min(512, t)

    def body(o_ref, do_ref, d_ref):
        prod = o_ref[...].astype(F32) * do_ref[...].astype(F32)
        for hh in range(nh):
            d_ref[hh] = jnp.sum(prod[:, hh * V_DIM:(hh + 1) * V_DIM], axis=-1, keepdims=True)

    return pl.pallas_call(
        body, name="attn_delta", grid=(t // tm,),
        in_specs=[pl.BlockSpec((tm, w), lambda i: (i, 0)), pl.BlockSpec((tm, w), lambda i: (i, 0))],
        out_specs=pl.BlockSpec((nh, tm, 1), lambda i: (0, i, 0)),
        out_shape=jax.ShapeDtypeStruct((nh, t, 1), F32),
        compiler_params=_cparams(),
    )(o, do)


ATTN_BWD_HEADS = 2


def _attn_bwd(q, k, v, do, lse, delta):
    nh, t, _ = q.shape
    hp = ATTN_BWD_HEADS
    tq = tk = min(512, t)
    nq, nk = t // tq, t // tk

    pairs = [(j, i) for j in range(nk) for i in range(j, nq)]
    kj = jnp.asarray(np.array([j for j, _ in pairs], np.int32))
    qi = jnp.asarray(np.array([i for _, i in pairs], np.int32))

    def body(kj_ref, qi_ref, q_ref, k_ref, v_ref, do_ref, lse_ref, dlt_ref, dq_ref, dk_ref, dv_ref):
        n = pl.program_id(1)
        j, i = kj_ref[n], qi_ref[n]

        @pl.when(n == 0)
        def _():
            dq_ref[...] = jnp.zeros_like(dq_ref)

        def step(diagonal):
            for hh in range(hp):
                qq, kk = q_ref[hh], k_ref[hh]
                dob = do_ref[:, hh * V_DIM:(hh + 1) * V_DIM]
                s = _dot_nt(qq, kk) * ATTN_SCALE
                if diagonal:
                    s = _causal_mask(s)
                p = jnp.exp(s - lse_ref[hh])
                dpp = _dot_nt(dob, v_ref[hh])
                dsb = (p * (dpp - dlt_ref[hh]) * ATTN_SCALE).astype(BF16)
                _accumulate(dv_ref.at[hh], _dot_tn(p.astype(BF16), dob), diagonal)
                _accumulate(dk_ref.at[hh], _dot_tn(dsb, qq), diagonal)
                dq_ref[hh, pl.ds(pl.multiple_of(i * tq, tq), tq), :] += _dot_nn(dsb, kk)

        @pl.when(i > j)
        def _():
            step(False)

        @pl.when(i == j)
        def _():
            step(True)

    q_map = lambda h, n, kj_ref, qi_ref: (h, qi_ref[n], 0)
    k_map = lambda h, n, kj_ref, qi_ref: (h, kj_ref[n], 0)
    return pl.pallas_call(
        body, name="attn_bwd",
        grid_spec=pltpu.PrefetchScalarGridSpec(
            num_scalar_prefetch=2, grid=(nh // hp, len(pairs)),
            in_specs=[pl.BlockSpec((hp, tq, HEAD_PAD), q_map),
                      pl.BlockSpec((hp, tk, HEAD_PAD), k_map),
                      pl.BlockSpec((hp, tk, V_DIM), k_map),
                      pl.BlockSpec((tq, hp * V_DIM), lambda h, n, kj_ref, qi_ref: (qi_ref[n], h)),
                      pl.BlockSpec((hp, tq, 1), q_map),
                      pl.BlockSpec((hp, tq, 1), q_map)],
            out_specs=[pl.BlockSpec((hp, t, HEAD_PAD), lambda h, n, kj_ref, qi_ref: (h, 0, 0)),
                       pl.BlockSpec((hp, tk, HEAD_PAD), k_map),
                       pl.BlockSpec((hp, tk, V_DIM), k_map)]),
        out_shape=[jax.ShapeDtypeStruct((nh, t, HEAD_PAD), F32), jax.ShapeDtypeStruct((nh, t, HEAD_PAD), F32),
                   jax.ShapeDtypeStruct((nh, t, V_DIM), F32)],
        compiler_params=_cparams(),
    )(kj, qi, q, k, v, do, lse, delta)


def _pool_counts(first_token, rows, w):
    tok = lax.broadcasted_iota(jnp.int32, (rows, POOL_CH), 0) + first_token
    return jnp.minimum(tok + 1, w).astype(F32)


def _pool_centered(zbuf, g, w, i, tm):
    lanes = pl.ds(g * POOL_CH, POOL_CH)
    cur = zbuf[pl.ds(POOL_HALO, tm), lanes]
    win = cur
    for s in range(1, w):
        win = win + zbuf[pl.ds(POOL_HALO - s, tm), lanes]
    return win / _pool_counts(i * tm, tm, w) - cur


def _pool_load(zbuf, z_ref, halo_ref, i, tm):
    @pl.when(i == 0)
    def _():
        zbuf[pl.ds(0, POOL_HALO), :] = jnp.zeros((POOL_HALO, zbuf.shape[1]), F32)

    @pl.when(i > 0)
    def _():
        zbuf[pl.ds(0, POOL_HALO), :] = halo_ref[...]

    zbuf[pl.ds(POOL_HALO, tm), :] = z_ref[...]


def _pool_fwd(z, pool_w, pool_scale):
    t = z.shape[0]
    pw = len(POOL_WINDOWS) * POOL_CH
    tm = min(512, t)
    hb = tm // POOL_HALO

    def body(z_ref, halo_ref, w_ref, sc_ref, p_ref, zbuf):
        i = pl.program_id(0)
        _pool_load(zbuf, z_ref, halo_ref, i, tm)
        for g, w in enumerate(POOL_WINDOWS):
            c = _pool_centered(zbuf, g, w, i, tm)
            y = _dot_nn(c.astype(BF16), w_ref[g]) * sc_ref[:, g * POOL_CH:(g + 1) * POOL_CH]
            p_ref[:, g * POOL_CH:(g + 1) * POOL_CH] = y.astype(BF16)

    return pl.pallas_call(
        body, name="pool_fwd", grid=(t // tm,),
        in_specs=[pl.BlockSpec((tm, pw), lambda i: (i, 1)),
                  pl.BlockSpec((POOL_HALO, pw), lambda i: (jnp.maximum(i * hb - 1, 0), 1)),
                  pl.BlockSpec(pool_w.shape, lambda i: (0, 0, 0)),
                  pl.BlockSpec((1, pw), lambda i: (0, 0))],
        out_specs=pl.BlockSpec((tm, pw), lambda i: (i, 0)),
        out_shape=jax.ShapeDtypeStruct((t, pw), BF16),
        scratch_shapes=[pltpu.VMEM((POOL_HALO + tm, pw), F32)],
        compiler_params=_cparams(),
    )(z, z, pool_w, pool_scale)


def _pool_bwd(dp, z, pool_w, pool_scale):
    t = z.shape[0]
    ng = len(POOL_WINDOWS)
    pw = ng * POOL_CH
    tm = min(512, t)
    hb = tm // POOL_HALO
    nt = t // tm

    def body(dp_ref, dpn_ref, z_ref, halo_ref, w_ref, sc_ref, dz_ref, dw_ref, dsc_ref, zbuf, dbuf):
        i = pl.program_id(0)
        _pool_load(zbuf, z_ref, halo_ref, i, tm)

        @pl.when(i == 0)
        def _():
            dw_ref[...] = jnp.zeros_like(dw_ref)
            dsc_ref[...] = jnp.zeros_like(dsc_ref)

        nxt_ok = (i < nt - 1).astype(F32)
        for g, w in enumerate(POOL_WINDOWS):
            lanes = pl.ds(g * POOL_CH, POOL_CH)
            cols = slice(g * POOL_CH, (g + 1) * POOL_CH)
            sc = sc_ref[:, cols]
            wg = w_ref[g]
            c = _pool_centered(zbuf, g, w, i, tm).astype(BF16)
            ypre = _dot_nn(c, wg)
            dpg = dp_ref[:, cols].astype(F32)
            dsc_ref[:, cols] += jnp.sum(dpg * ypre, axis=0, keepdims=True)
            dyb = (dpg * sc).astype(BF16)
            dw_ref[g] += _dot_tn(c, dyb)
            dd = _dot_nt(dyb, wg)
            dyn = (dpn_ref[:, cols].astype(F32) * sc).astype(BF16)
            ddn = _dot_nt(dyn, wg) * nxt_ok
            dbuf[pl.ds(0, tm), lanes] = dd / _pool_counts(i * tm, tm, w)
            dbuf[pl.ds(tm, POOL_HALO), lanes] = ddn / _pool_counts((i + 1) * tm, POOL_HALO, w)
            acc = -dd
            for s in range(w):
                acc = acc + dbuf[pl.ds(s, tm), lanes]
            dz_ref[:, cols] = acc

    return pl.pallas_call(
        body, name="pool_bwd", grid=(nt,),
        in_specs=[pl.BlockSpec((tm, pw), lambda i: (i, 0)),
                  pl.BlockSpec((POOL_HALO, pw), lambda i: (jnp.minimum((i + 1) * hb, t // POOL_HALO - 1), 0)),
                  pl.BlockSpec((tm, pw), lambda i: (i, 1)),
                  pl.BlockSpec((POOL_HALO, pw), lambda i: (jnp.maximum(i * hb - 1, 0), 1)),
                  pl.BlockSpec(pool_w.shape, lambda i: (0, 0, 0)),
                  pl.BlockSpec((1, pw), lambda i: (0, 0))],
        out_specs=[pl.BlockSpec((tm, pw), lambda i: (i, 0)),
                   pl.BlockSpec((ng, POOL_CH, POOL_CH), lambda i: (0, 0, 0)),
                   pl.BlockSpec((1, pw), lambda i: (0, 0))],
        out_shape=[jax.ShapeDtypeStruct((t, pw), F32), jax.ShapeDtypeStruct((ng, POOL_CH, POOL_CH), F32),
                   jax.ShapeDtypeStruct((1, pw), F32)],
        scratch_shapes=[pltpu.VMEM((POOL_HALO + tm, pw), F32), pltpu.VMEM((tm + POOL_HALO, pw), F32)],
        compiler_params=_cparams(),
    )(dp, dp, z, z, pool_w, pool_scale)


def _mla_bwd(dq_h, dk_h, dv_h, z, dz_pool, h1, dh2, mix_norm, w_in, q_norm, wq_t, kv_norm, wkv, pos, rope_tab):
    t, d = h1.shape
    tm = min(512, t)

    def body(dqh_ref, dkh_ref, dvh_ref, z_ref, dzp_ref, h_ref, dh2_ref, gm_ref, win_ref, gq_ref, wq_ref, gkv_ref,
             wkv_ref, pos_ref, tab_ref, dh1_ref, dq_ref, dkv_ref, dz_ref, dgq_ref, dgkv_ref, dgm_ref):
        i = pl.program_id(0)
        first = i == 0
        cos_t, sin_t = _rope_tables(pos_ref[...], tab_ref[...])
        dq_parts, dkv_parts = [], []
        dk_pe = jnp.zeros((tm, 128), F32)
        for hh in range(MLA_HEADS):
            dqh = dqh_ref[hh]
            dq_parts += [dqh[:, 0:NOPE], _rope_apply_t(dqh[:, NOPE:HEAD_PAD], cos_t, sin_t)]
            dkh = dkh_ref[hh]
            dkv_parts += [dkh[:, 0:NOPE], dvh_ref[hh]]
            dk_pe = dk_pe + dkh[:, NOPE:HEAD_PAD]
        dqb = jnp.concatenate(dq_parts, axis=-1).astype(BF16)
        dkvb = jnp.concatenate(dkv_parts, axis=-1).astype(BF16)
        dq_ref[...] = dqb
        dkv_ref[...] = dkvb
        z = z_ref[...]
        c_q = z[:, 0:Q_RANK]
        gq = gq_ref[...]
        _, rq = _rms_fwd(c_q, gq)
        dcq, dgq = _rms_bwd(_dot_nn(dqb, wq_ref[...]), c_q, gq, rq)
        c_kv = z[:, Q_RANK:Q_RANK + KV_RANK]
        gkv = gkv_ref[...]
        _, rkv = _rms_fwd(c_kv, gkv)
        dckv, dgkv = _rms_bwd(_dot_nt(dkvb, wkv_ref[...]), c_kv, gkv, rkv)
        dkr = _rope_apply_t(dk_pe, cos_t, sin_t)
        dzb = jnp.concatenate([dcq, dckv, dkr, dzp_ref[...]], axis=-1).astype(BF16)
        dz_ref[...] = dzb
        x = h_ref[...]
        gm = gm_ref[...]
        _, rm = _rms_fwd(x, gm)
        dx, dgm = _rms_bwd(_dot_nt(dzb, win_ref[...]), x, gm, rm)
        dh1_ref[...] = dh2_ref[...] + dx
        _accumulate(dgq_ref, dgq, first)
        _accumulate(dgkv_ref, dgkv, first)
        _accumulate(dgm_ref, dgm, first)

    full = lambda shape: pl.BlockSpec(shape, lambda i: (0,) * len(shape))
    row = lambda w: pl.BlockSpec((tm, w), lambda i: (i, 0))
    head = lambda w: pl.BlockSpec((MLA_HEADS, tm, w), lambda i: (0, i, 0))
    pw = len(POOL_WINDOWS) * POOL_CH
    return pl.pallas_call(
        body, name="mla_bwd", grid=(t // tm,),
        in_specs=[head(HEAD_PAD), head(HEAD_PAD), head(V_DIM), row(d), row(pw), row(d), row(d),
                  _resident((1, d)), _resident(w_in.shape), _resident((1, Q_RANK)), _resident(wq_t.shape),
                  _resident((1, KV_RANK)), _resident(wkv.shape), row(1), _resident(rope_tab.shape)],
        out_specs=[row(d), row(d), row(d), row(d), full((1, Q_RANK)), full((1, KV_RANK)), full((1, d))],
        out_shape=[jax.ShapeDtypeStruct((t, d), F32), jax.ShapeDtypeStruct((t, d), BF16),
                   jax.ShapeDtypeStruct((t, d), BF16), jax.ShapeDtypeStruct((t, d), BF16),
                   jax.ShapeDtypeStruct((1, Q_RANK), F32), jax.ShapeDtypeStruct((1, KV_RANK), F32),
                   jax.ShapeDtypeStruct((1, d), F32)],
        compiler_params=_cparams(),
    )(dq_h, dk_h, dv_h, z, dz_pool, h1, dh2, mix_norm, w_in, q_norm, wq_t, kv_norm, wkv, pos, rope_tab)


def _mem_kv(mem, mem_norm, wmkv):
    n, d = mem.shape

    def body(mem_ref, g_ref, w_ref, memn_ref, k_ref, v_ref):
        y, _ = _rms_fwd(mem_ref[...], g_ref[...])
        yb = y.astype(BF16)
        memn_ref[...] = yb
        for hh in range(MEM_HEADS):
            k_ref[hh] = _dot_nn(yb, w_ref[hh]).astype(BF16)
            v_ref[hh] = _dot_nn(yb, w_ref[MEM_HEADS + hh]).astype(BF16)

    return pl.pallas_call(
        body, name="mem_kv",
        out_shape=[jax.ShapeDtypeStruct((n, d), BF16), jax.ShapeDtypeStruct((MEM_HEADS, n, MEM_HD), BF16),
                   jax.ShapeDtypeStruct((MEM_HEADS, n, MEM_HD), BF16)],
        compiler_params=_cparams(),
    )(mem, mem_norm, wmkv)


def _mem_softmax(qb, km):
    s = _dot_nt(qb, km) * MEM_SCALE
    e = jnp.exp(s - jnp.max(s, axis=-1, keepdims=True))
    return e / jnp.sum(e, axis=-1, keepdims=True)


def _xattn_fwd(h1, a, p, w_out, g, wmq, km, vm, wmo):
    t, d = h1.shape
    tm = min(512, t)
    half = a.shape[1]

    def body(h_ref, a_ref, p_ref, wo_ref, g_ref, wmq_ref, km_ref, vm_ref, wmo_ref,
             h2_ref, h3_ref, hn_ref, q_ref, o_ref):
        h2 = h_ref[...] + _dot_nn(a_ref[...], wo_ref[0:half, :]) + _dot_nn(p_ref[...], wo_ref[half:2 * half, :])
        h2_ref[...] = h2
        hn, _ = _rms_fwd(h2, g_ref[...])
        hnb = hn.astype(BF16)
        hn_ref[...] = hnb
        qb = _dot_nn(hnb, wmq_ref[...]).astype(BF16)
        q_ref[...] = qb
        outs = []
        for hh in range(MEM_HEADS):
            pr = _mem_softmax(qb[:, hh * MEM_HD:(hh + 1) * MEM_HD], km_ref[hh])
            outs.append(_dot_nn(pr.astype(BF16), vm_ref[hh]))
        ob = jnp.concatenate(outs, axis=-1).astype(BF16)
        o_ref[...] = ob
        h3_ref[...] = h2 + _dot_nn(ob, wmo_ref[...])

    full = lambda shape: pl.BlockSpec(shape, lambda i: (0,) * len(shape))
    row = lambda w: pl.BlockSpec((tm, w), lambda i: (i, 0))
    return pl.pallas_call(
        body, name="xattn_fwd", grid=(t // tm,),
        in_specs=[row(d), row(half), row(half), _resident(w_out.shape), _resident((1, d)), _resident(wmq.shape),
                  _resident(km.shape), _resident(vm.shape), _resident(wmo.shape)],
        out_specs=[row(d), row(d), row(d), row(d), row(d)],
        out_shape=[jax.ShapeDtypeStruct((t, d), F32), jax.ShapeDtypeStruct((t, d), F32),
                   jax.ShapeDtypeStruct((t, d), BF16), jax.ShapeDtypeStruct((t, d), BF16),
                   jax.ShapeDtypeStruct((t, d), BF16)],
        compiler_params=_cparams(),
    )(h1, a, p, w_out, g, wmq, km, vm, wmo)


def _xattn_bwd(dh3, h2, qm, g, wmq, km, vm, wmo, w_out, token=None):
    t, d = h2.shape
    tm = min(512, t)
    half = d // 2

    def body(dh3_ref, h2_ref, q_ref, g_ref, wmq_ref, km_ref, vm_ref, wmo_ref, wo_ref,
             dh2_ref, dq_ref, da_ref, dp_ref, dk_ref, dv_ref, dg_ref):
        i = pl.program_id(0)
        first = i == 0

        @pl.when(first)
        def _():
            dk_ref[...] = jnp.zeros_like(dk_ref)
            dv_ref[...] = jnp.zeros_like(dv_ref)

        dh3 = dh3_ref[...]
        dob = _dot_nt(dh3.astype(BF16), wmo_ref[...]).astype(BF16)
        qb = q_ref[...]
        dq_parts = []
        for hh in range(MEM_HEADS):
            cols = slice(hh * MEM_HD, (hh + 1) * MEM_HD)
            kk, vv = km_ref[hh], vm_ref[hh]
            pr = _mem_softmax(qb[:, cols], kk)
            doh = dob[:, cols]
            dv_ref[hh] += _dot_tn(pr.astype(BF16), doh)
            dpp = _dot_nt(doh, vv)
            dsb = (pr * (dpp - jnp.sum(dpp * pr, axis=-1, keepdims=True)) * MEM_SCALE).astype(BF16)
            dq_parts.append(_dot_nn(dsb, kk))
            dk_ref[hh] += _dot_tn(dsb, qb[:, cols])
        dqb = jnp.concatenate(dq_parts, axis=-1).astype(BF16)
        dq_ref[...] = dqb
        x = h2_ref[...]
        gg = g_ref[...]
        _, r = _rms_fwd(x, gg)
        dx, dg = _rms_bwd(_dot_nt(dqb, wmq_ref[...]), x, gg, r)
        dh2 = dh3 + dx
        dh2_ref[...] = dh2
        dap = _dot_nt(dh2.astype(BF16), wo_ref[...])
        da_ref[...] = dap[:, 0:half].astype(BF16)
        dp_ref[...] = dap[:, half:d].astype(BF16)
        _accumulate(dg_ref, dg, first)

    full = lambda shape: pl.BlockSpec(shape, lambda i: (0,) * len(shape))
    row = lambda w: pl.BlockSpec((tm, w), lambda i: (i, 0))
    return _call_after(
        token, body,
        [row(d), row(d), row(d), _resident((1, d)), _resident(wmq.shape), _resident(km.shape), _resident(vm.shape),
         _resident(wmo.shape), _resident(w_out.shape)],
        (dh3, h2, qm, g, wmq, km, vm, wmo, w_out),
        name="xattn_bwd", grid=(t // tm,),
        out_specs=[row(d), row(d), row(half), row(half), full(km.shape), full(vm.shape), full((1, d))],
        out_shape=[jax.ShapeDtypeStruct((t, d), F32), jax.ShapeDtypeStruct((t, d), BF16),
                   jax.ShapeDtypeStruct((t, half), BF16), jax.ShapeDtypeStruct((t, half), BF16),
                   jax.ShapeDtypeStruct(km.shape, F32), jax.ShapeDtypeStruct(vm.shape, F32),
                   jax.ShapeDtypeStruct((1, d), F32)],
        compiler_params=_cparams(),
    )


def _mem_kv_bwd(dkm, dvm, memn, mem, mem_norm, wmkv):
    n, d = mem.shape

    def body(dk_ref, dv_ref, memn_ref, mem_ref, g_ref, w_ref, dw_ref, dg_ref):
        memn = memn_ref[...]
        dmemn = jnp.zeros((n, d), F32)
        for s in range(2 * MEM_HEADS):
            src = dk_ref[s] if s < MEM_HEADS else dv_ref[s - MEM_HEADS]
            db = src.astype(BF16)
            dw_ref[s] = _dot_tn(memn, db)
            dmemn = dmemn + _dot_nt(db, w_ref[s])
        x = mem_ref[...]
        gg = g_ref[...]
        _, r = _rms_fwd(x, gg)
        _, dg = _rms_bwd(dmemn, x, gg, r)
        dg_ref[...] = dg

    return pl.pallas_call(
        body, name="mem_kv_bwd",
        out_shape=[jax.ShapeDtypeStruct(wmkv.shape, F32), jax.ShapeDtypeStruct((1, d), F32)],
        compiler_params=_cparams(),
    )(dkm, dvm, memn, mem, mem_norm, wmkv)


MESH_ID = pl.DeviceIdType.MESH
ANY = pl.BlockSpec(memory_space=pl.ANY)


def _coords():
    return lax.axis_index("x"), lax.axis_index("y"), lax.axis_index("c")


def _other_chips(x, y):
    return [(1 - x, y), (x, 1 - y), (1 - x, 1 - y)]


def _core_reduce(g, tag):
    _, r, w = g.shape

    def body(g_ref, part_ref, own_sc, recv_sc, send_sems, recv_sems, local_sems):
        x, y, c = _coords()
        sent, local = [], []
        for chip in range(4):
            sent.append(pltpu.make_async_remote_copy(
                src_ref=g_ref.at[2 * chip + (1 - c)], dst_ref=recv_sc.at[chip],
                send_sem=send_sems.at[chip], recv_sem=recv_sems.at[chip],
                device_id=(x, y, 1 - c), device_id_type=MESH_ID))
            local.append(pltpu.make_async_copy(g_ref.at[2 * chip + c], own_sc.at[chip], local_sems.at[chip]))
        for cp in sent + local:
            cp.start()
        for chip in range(4):
            local[chip].wait()
            sent[chip].wait_recv()
            part_ref[chip] = (own_sc[chip].astype(F32) + recv_sc[chip].astype(F32)).astype(part_ref.dtype)
        for cp in sent:
            cp.wait_send()

    return pl.pallas_call(
        body, name="core_reduce_" + tag,
        out_shape=jax.ShapeDtypeStruct((4, r, w), g.dtype),
        in_specs=[ANY], out_specs=pl.BlockSpec(memory_space=pltpu.VMEM),
        scratch_shapes=[pltpu.VMEM((4, r, w), g.dtype), pltpu.VMEM((4, r, w), g.dtype),
                        pltpu.SemaphoreType.DMA((4,)), pltpu.SemaphoreType.DMA((4,)), pltpu.SemaphoreType.DMA((4,))],
        compiler_params=_cparams(),
    )(g)


HBM_SPEC = pl.BlockSpec(memory_space=pltpu.HBM)
SEM_SPEC = pl.BlockSpec(memory_space=pltpu.SEMAPHORE)
SPLIT_EFFECT = pltpu.SideEffectType.DATAFLOW_SIDE_EFFECTING


def _ici_refs(gather, src_ref, land_ref, j, px, py, slot_chip, c):
    if gather:
        return src_ref, land_ref.at[:, 4 * slot_chip[0] + 2 * slot_chip[1] + c]
    return src_ref.at[2 * px + py], land_ref.at[j]


def _ici_start(src, after, name, gather):
    r, w = src.shape[-2:]
    land_shape = (src.shape[0], N_DEV, r, w) if gather else (3, r, w)

    def body(src_ref, land_ref, after_ref, send_sems, recv_sems, src_thru, land_thru, token):
        x, y, c = _coords()
        for j, (px, py) in enumerate(_other_chips(x, y)):
            s_ref, d_ref = _ici_refs(gather, src_ref, land_ref, j, px, py, (x, y), c)
            pltpu.make_async_remote_copy(
                src_ref=s_ref, dst_ref=d_ref, send_sem=send_sems.at[j], recv_sem=recv_sems.at[j],
                device_id=(px, py, c), device_id_type=MESH_ID).start()
        token[...] = jnp.zeros_like(token)

    return pl.pallas_call(
        body, name=name,
        out_shape=(pltpu.SemaphoreType.DMA((3,)), pltpu.SemaphoreType.DMA((3,)), pltpu.HBM(src.shape, src.dtype),
                   pltpu.HBM(land_shape, src.dtype), jax.ShapeDtypeStruct((8, 128), F32)),
        in_specs=(HBM_SPEC, HBM_SPEC, ANY),
        out_specs=(SEM_SPEC, SEM_SPEC, HBM_SPEC, HBM_SPEC, pl.BlockSpec(memory_space=pltpu.VMEM)),
        input_output_aliases={0: 2, 1: 3},
        compiler_params=pltpu.CompilerParams(has_side_effects=SPLIT_EFFECT),
    )(pltpu.with_memory_space_constraint(src, pltpu.HBM),
      pltpu.with_memory_space_constraint(lax.empty(land_shape, src.dtype), pltpu.HBM), after)


def _ici_wait(started, after, name, gather):
    send_sems, recv_sems, src_thru, land_thru, _ = started

    def body(src_ref, land_ref, send_sems, recv_sems, after_ref, src_dead, got_ref):
        x, y, c = _coords()
        for j, (px, py) in enumerate(_other_chips(x, y)):
            s_ref, d_ref = _ici_refs(gather, src_ref, land_ref, j, px, py, (px, py), c)
            copy = pltpu.make_async_remote_copy(
                src_ref=s_ref, dst_ref=d_ref, send_sem=send_sems.at[j], recv_sem=recv_sems.at[j],
                device_id=(px, py, c), device_id_type=MESH_ID)
            copy.wait_send()
            copy.wait_recv()

    return pl.pallas_call(
        body, name=name,
        out_shape=(pltpu.HBM(src_thru.shape, src_thru.dtype), pltpu.HBM(land_thru.shape, land_thru.dtype)),
        in_specs=(HBM_SPEC, HBM_SPEC, SEM_SPEC, SEM_SPEC, ANY),
        out_specs=(HBM_SPEC, HBM_SPEC), input_output_aliases={0: 0, 1: 1},
        compiler_params=pltpu.CompilerParams(has_side_effects=SPLIT_EFFECT),
    )(src_thru, land_thru, send_sems, recv_sems, after)


def _peer(k, x, y, c):
    return x ^ ((k >> 2) & 1), y ^ ((k >> 1) & 1), c ^ (k & 1)


def _peers_start(src, after, name):
    r, w = src.shape
    x, y, c = _coords()
    land = lax.dynamic_update_slice(jnp.zeros((N_DEV, r, w), src.dtype), src[None], (4 * x + 2 * y + c, 0, 0))

    def body(src_ref, land_ref, after_ref, send_sems, recv_sems, src_thru, land_thru, token):
        x, y, c = _coords()
        for k in range(1, N_DEV):
            pltpu.make_async_remote_copy(
                src_ref=src_ref, dst_ref=land_ref.at[4 * x + 2 * y + c],
                send_sem=send_sems.at[k - 1], recv_sem=recv_sems.at[k - 1],
                device_id=_peer(k, x, y, c), device_id_type=MESH_ID).start()
        token[...] = jnp.zeros_like(token)

    return pl.pallas_call(
        body, name=name,
        out_shape=(pltpu.SemaphoreType.DMA((N_DEV - 1,)), pltpu.SemaphoreType.DMA((N_DEV - 1,)),
                   pltpu.HBM(src.shape, src.dtype), pltpu.HBM(land.shape, src.dtype),
                   jax.ShapeDtypeStruct((8, 128), F32)),
        in_specs=(HBM_SPEC, HBM_SPEC, ANY),
        out_specs=(SEM_SPEC, SEM_SPEC, HBM_SPEC, HBM_SPEC, pl.BlockSpec(memory_space=pltpu.VMEM)),
        input_output_aliases={0: 2, 1: 3},
        compiler_params=pltpu.CompilerParams(has_side_effects=SPLIT_EFFECT),
    )(pltpu.with_memory_space_constraint(src, pltpu.HBM), pltpu.with_memory_space_constraint(land, pltpu.HBM), after)


def _peers_wait(started, after, name):
    send_sems, recv_sems, src_thru, land_thru, _ = started

    def body(src_ref, land_ref, send_sems, recv_sems, after_ref, src_dead, got_ref):
        x, y, c = _coords()
        for k in range(1, N_DEV):
            px, py, pc = _peer(k, x, y, c)
            copy = pltpu.make_async_remote_copy(
                src_ref=src_ref, dst_ref=land_ref.at[4 * px + 2 * py + pc],
                send_sem=send_sems.at[k - 1], recv_sem=recv_sems.at[k - 1],
                device_id=(px, py, pc), device_id_type=MESH_ID)
            copy.wait_send()
            copy.wait_recv()

    return pl.pallas_call(
        body, name=name,
        out_shape=(pltpu.HBM(src_thru.shape, src_thru.dtype), pltpu.HBM(land_thru.shape, land_thru.dtype)),
        in_specs=(HBM_SPEC, HBM_SPEC, SEM_SPEC, SEM_SPEC, ANY),
        out_specs=(HBM_SPEC, HBM_SPEC), input_output_aliases={0: 0, 1: 1},
        compiler_params=pltpu.CompilerParams(has_side_effects=SPLIT_EFFECT),
    )(src_thru, land_thru, send_sems, recv_sems, after)


def _core_share(own, gathered, name):
    def body(own_ref, gin_ref, out_ref, stage, send_sems, recv_sems, local_sem):
        x, y, c = _coords()
        sibling = (x, y, 1 - c)
        chips = [(x, y)] + _other_chips(x, y)
        stage_in = pltpu.make_async_copy(own_ref, stage, local_sem)
        stage_in.start()
        sent, arriving = [], []
        for k, (px, py) in enumerate(chips):
            slot = out_ref.at[:, 4 * px + 2 * py + c]
            sent.append(pltpu.make_async_remote_copy(
                src_ref=own_ref if k == 0 else slot, dst_ref=slot,
                send_sem=send_sems.at[k], recv_sem=recv_sems.at[k], device_id=sibling, device_id_type=MESH_ID))
            arriving.append(pltpu.make_async_remote_copy(
                src_ref=own_ref, dst_ref=out_ref.at[:, 4 * px + 2 * py + (1 - c)],
                send_sem=send_sems.at[k], recv_sem=recv_sems.at[k], device_id=sibling, device_id_type=MESH_ID))
        for cp in sent:
            cp.start()
        stage_in.wait()
        stage_out = pltpu.make_async_copy(stage, out_ref.at[:, 4 * x + 2 * y + c], local_sem)
        stage_out.start()
        for cp in arriving:
            cp.wait_recv()
        for cp in sent:
            cp.wait_send()
        stage_out.wait()

    return pl.pallas_call(
        body, name=name,
        out_shape=jax.ShapeDtypeStruct(gathered.shape, own.dtype),
        in_specs=[ANY, ANY], out_specs=ANY, input_output_aliases={1: 0},
        scratch_shapes=[pltpu.VMEM(own.shape, own.dtype), pltpu.SemaphoreType.DMA((4,)),
                        pltpu.SemaphoreType.DMA((4,)), pltpu.SemaphoreType.DMA],
    )(own, gathered)


def _adamw(w, g, m, v):
    m = ADAM_B1 * m + (1.0 - ADAM_B1) * g
    v = ADAM_B2 * v + (1.0 - ADAM_B2) * (g * g)
    m_hat = m / ADAM_C1
    v_hat = v / ADAM_C2
    delta = -ADAM_LR * (m_hat / (jnp.sqrt(v_hat) + ADAM_EPS) + ADAM_WD * w)
    return delta, m, v


def _adam_big(part, land, w, m, v, chip_idx, tag, token):
    r, wd = w.shape
    tr, tw = _row_tile(r, 1024), 256

    def body(s_ref, tok_ref, p_ref, l_ref, w_ref, m_ref, v_ref, g_ref, d_ref, mo_ref, vo_ref):
        g = p_ref[0].astype(F32)
        for j in range(3):
            g = g + l_ref[j].astype(F32)
        delta, mn, vn = _adamw(w_ref[...], g, m_ref[...], v_ref[...])
        g_ref[...] = g
        d_ref[...] = delta
        mo_ref[...] = mn
        vo_ref[...] = vn

    row = pl.BlockSpec((tr, tw), lambda i, j, s: (i, j))
    return pl.pallas_call(
        body, name="adam_big_" + tag,
        grid_spec=pltpu.PrefetchScalarGridSpec(
            num_scalar_prefetch=1, grid=(r // tr, wd // tw),
            in_specs=[pl.BlockSpec((8, 128), lambda i, j, s: (0, 0)),
                      pl.BlockSpec((1, tr, tw), lambda i, j, s: (s[0], i, j)),
                      pl.BlockSpec((3, tr, tw), lambda i, j, s: (0, i, j)), row, row, row],
            out_specs=[row, row, row, row]),
        out_shape=[jax.ShapeDtypeStruct((r, wd), F32)] * 4,
        compiler_params=_cparams(),
    )(chip_idx, token, part, land, w, m, v)


def _adam_small(parts, w, m, v):
    _, r, wd = parts.shape

    def body(p_ref, w_ref, m_ref, v_ref, g_ref, d_ref, mo_ref, vo_ref):
        g = p_ref[0]
        for k in range(1, N_DEV):
            g = g + p_ref[k]
        delta, mn, vn = _adamw(w_ref[...], g, m_ref[...], v_ref[...])
        g_ref[...] = g
        d_ref[...] = delta
        mo_ref[...] = mn
        vo_ref[...] = vn

    return pl.pallas_call(
        body, name="adam_small",
        out_shape=[jax.ShapeDtypeStruct((r, wd), F32)] * 4,
        compiler_params=_cparams(),
    )(parts, w, m, v)


def _pad_rows(a, rows):
    return jnp.pad(a, ((0, rows - a.shape[0]), (0, 0)))


def _pad_w_in(w):
    cut = Q_RANK + KV_RANK + ROPE
    return jnp.concatenate([w[:, :cut], jnp.zeros((w.shape[0], 64), w.dtype), w[:, cut:]], axis=1)


def _unpad_w_in(w):
    cut = Q_RANK + KV_RANK + ROPE
    return jnp.concatenate([w[:, :cut], w[:, cut + 64:]], axis=1)


def _pack_mid(p):
    parts = [_pad_w_in(p["w_in"][0]), p["w_out"][0], p["w_mq"][0], p["w_mo"][0],
             p["w_mkv"][0].reshape(256, D_MODEL),
             _pad_rows(p["w_q_up"][0].T.reshape(24, D_MODEL), 32),
             p["w_kv_up"][0].reshape(16, D_MODEL)]
    return jnp.concatenate(parts, axis=0)


def _pack_segments(p, group):
    if group == "mid":
        return _pack_mid(p)[None]
    return jnp.stack([p[group + "_w_gate"][0].T, p[group + "_w_up"][0].T, p[group + "_w_down"][0]])


UNIT_WEIGHT = {"ffn1_g": ("ffn1_w_gate", True), "ffn1_u": ("ffn1_w_up", True), "ffn1_d": ("ffn1_w_down", False),
               "ffn2_g": ("ffn2_w_gate", True), "ffn2_u": ("ffn2_w_up", True), "ffn2_d": ("ffn2_w_down", False)}


def _pack_unit(p, unit):
    if unit == "mid":
        return _pack_mid(p)
    name, transposed = UNIT_WEIGHT[unit]
    return p[name][0].T if transposed else p[name][0]


def _unpack_unit(a, unit):
    if unit != "mid":
        name, transposed = UNIT_WEIGHT[unit]
        return {name: (a.T if transposed else a)[None]}
    seg = lambda n: a[SEG_OFF[n][0]:SEG_OFF[n][0] + SEG_OFF[n][1]]
    return {"w_in": _unpad_w_in(seg("w_in"))[None], "w_out": seg("w_out")[None], "w_mq": seg("w_mq")[None],
            "w_mo": seg("w_mo")[None], "w_mkv": seg("w_mkv").reshape(D_MODEL, 256)[None],
            "w_q_up": seg("w_q")[:24].reshape(96, Q_RANK).T[None],
            "w_kv_up": seg("w_kv").reshape(KV_RANK, 128)[None]}


def _unpack_gathered(full, group):
    if group != "mid":
        return {n: full[k].reshape(-1, D_MODEL) for k, (n, _) in enumerate(GROUP_SEGS[group])}
    full = full[0]
    seg = lambda n: full[:, SEG_OFF[n][0]:SEG_OFF[n][0] + SEG_OFF[n][1]]
    rows = lambda n: seg(n).reshape(-1, D_MODEL)
    wq_t = seg("w_q")[:, :24].reshape(MLA_HEADS, NOPE + ROPE, Q_RANK)
    wq_t = jnp.pad(wq_t, ((0, 0), (0, HEAD_PAD - NOPE - ROPE), (0, 0))).reshape(MLA_HEADS * HEAD_PAD, Q_RANK)
    wkv = seg("w_kv").reshape(N_DEV, KV_RANK, 128).transpose(1, 0, 2).reshape(KV_RANK, N_DEV * 128)
    return {"w_in": rows("w_in"), "w_out": rows("w_out"), "w_mq": rows("w_mq"), "w_mo": rows("w_mo"),
            "w_mkv": seg("w_mkv").reshape(N_DEV, D_MODEL, 256), "w_q": wq_t, "w_kv": wkv}


def _pack_grads(gr):
    blk = lambda a: a.reshape(N_DEV, -1, D_MODEL)
    dwq = gr["w_q"].reshape(MLA_HEADS, HEAD_PAD, Q_RANK)[:, :NOPE + ROPE].reshape(N_DEV, 24, D_MODEL)
    dwq = jnp.pad(dwq, ((0, 0), (0, 8), (0, 0)))
    dwkv = gr["w_kv"].reshape(KV_RANK, N_DEV, 128).transpose(1, 0, 2).reshape(N_DEV, 16, D_MODEL)
    parts = [blk(gr["w_in"]), blk(gr["w_out"]), blk(gr["w_mq"]), blk(gr["w_mo"]),
             gr["w_mkv"].reshape(N_DEV, 256, D_MODEL), dwq, dwkv]
    return jnp.concatenate([a.astype(BF16) for a in parts], axis=1)


def _pack_small(vals):
    parts = []
    for n, r in SMALL_ROWS:
        parts.append(_pad_rows(vals[n].reshape(-1, 128), r) if n in vals else jnp.zeros((r, 128), F32))
    return jnp.concatenate(parts, axis=0)


def _unpack_small(a, shapes):
    out = {}
    for n, shape in shapes.items():
        o = SMALL_OFF[n][0]
        out[n] = a[o:o + int(np.prod(shape)) // 128].reshape(shape)
    return out


BIG_NAMES = ("ffn1_w_gate", "ffn1_w_up", "ffn1_w_down", "w_in", "w_q_up", "w_kv_up", "w_out", "w_mq", "w_mkv",
             "w_mo", "ffn2_w_gate", "ffn2_w_up", "ffn2_w_down")
SMALL_NAMES = ("ffn1_norm", "mix_norm", "q_norm", "kv_norm", "pool_w", "pool_scale", "xattn_norm", "mem_norm",
               "ffn2_norm", "final_norm")
WEIGHT_ORDER = ("ffn1_norm", "ffn1_w_gate", "ffn1_w_up", "ffn1_w_down", "mix_norm", "w_in", "q_norm", "w_q_up",
                "kv_norm", "w_kv_up", "pool_w", "pool_scale", "w_out", "xattn_norm", "mem_norm", "w_mq", "w_mkv",
                "w_mo", "ffn2_norm", "ffn2_w_gate", "ffn2_w_up", "ffn2_w_down", "final_norm")


def _rope_table():
    lane = np.arange(128)
    freqs = (1.0 / (ROPE_BASE ** (np.arange(0, ROPE, 2, dtype=np.float32) / ROPE))).astype(np.float32)
    tab = np.zeros((8, 128), np.float32)
    tab[0] = np.where(lane < ROPE, freqs[lane % (ROPE // 2)], 0.0)
    tab[1] = np.where(lane < ROPE // 2, -1.0, np.where(lane < ROPE, 1.0, 0.0))
    return jnp.asarray(tab)


def kernel(x, mem, positions, ffn1_norm, ffn1_w_gate, ffn1_w_up, ffn1_w_down, mix_norm, w_in, q_norm, w_q_up, kv_norm, w_kv_up, pool_w, pool_scale, w_out, xattn_norm, mem_norm, w_mq, w_mkv, w_mo, ffn2_norm, ffn2_w_gate, ffn2_w_up, ffn2_w_down, final_norm, loss_target, m_ffn1_norm, m_ffn1_w_gate, m_ffn1_w_up, m_ffn1_w_down, m_mix_norm, m_w_in, m_q_norm, m_w_q_up, m_kv_norm, m_w_kv_up, m_pool_w, m_pool_scale, m_w_out, m_xattn_norm, m_mem_norm, m_w_mq, m_w_mkv, m_w_mo, m_ffn2_norm, m_ffn2_w_gate, m_ffn2_w_up, m_ffn2_w_down, m_final_norm, v_ffn1_norm, v_ffn1_w_gate, v_ffn1_w_up, v_ffn1_w_down, v_mix_norm, v_w_in, v_q_norm, v_w_q_up, v_kv_norm, v_w_kv_up, v_pool_w, v_pool_scale, v_w_out, v_xattn_norm, v_mem_norm, v_w_mq, v_w_mkv, v_w_mo, v_ffn2_norm, v_ffn2_w_gate, v_ffn2_w_up, v_ffn2_w_down, v_final_norm):
    wts = dict(ffn1_norm=ffn1_norm, ffn1_w_gate=ffn1_w_gate, ffn1_w_up=ffn1_w_up, ffn1_w_down=ffn1_w_down,
               mix_norm=mix_norm, w_in=w_in, q_norm=q_norm, w_q_up=w_q_up, kv_norm=kv_norm, w_kv_up=w_kv_up,
               pool_w=pool_w, pool_scale=pool_scale, w_out=w_out, xattn_norm=xattn_norm, mem_norm=mem_norm,
               w_mq=w_mq, w_mkv=w_mkv, w_mo=w_mo, ffn2_norm=ffn2_norm, ffn2_w_gate=ffn2_w_gate,
               ffn2_w_up=ffn2_w_up, ffn2_w_down=ffn2_w_down, final_norm=final_norm)
    mom = dict(ffn1_norm=m_ffn1_norm, ffn1_w_gate=m_ffn1_w_gate, ffn1_w_up=m_ffn1_w_up, ffn1_w_down=m_ffn1_w_down,
               mix_norm=m_mix_norm, w_in=m_w_in, q_norm=m_q_norm, w_q_up=m_w_q_up, kv_norm=m_kv_norm,
               w_kv_up=m_w_kv_up, pool_w=m_pool_w, pool_scale=m_pool_scale, w_out=m_w_out, xattn_norm=m_xattn_norm,
               mem_norm=m_mem_norm, w_mq=m_w_mq, w_mkv=m_w_mkv, w_mo=m_w_mo, ffn2_norm=m_ffn2_norm,
               ffn2_w_gate=m_ffn2_w_gate, ffn2_w_up=m_ffn2_w_up, ffn2_w_down=m_ffn2_w_down, final_norm=m_final_norm)
    var = dict(ffn1_norm=v_ffn1_norm, ffn1_w_gate=v_ffn1_w_gate, ffn1_w_up=v_ffn1_w_up, ffn1_w_down=v_ffn1_w_down,
               mix_norm=v_mix_norm, w_in=v_w_in, q_norm=v_q_norm, w_q_up=v_w_q_up, kv_norm=v_kv_norm,
               w_kv_up=v_w_kv_up, pool_w=v_pool_w, pool_scale=v_pool_scale, w_out=v_w_out, xattn_norm=v_xattn_norm,
               mem_norm=v_mem_norm, w_mq=v_w_mq, w_mkv=v_w_mkv, w_mo=v_w_mo, ffn2_norm=v_ffn2_norm,
               ffn2_w_gate=v_ffn2_w_gate, ffn2_w_up=v_ffn2_w_up, ffn2_w_down=v_ffn2_w_down, final_norm=v_final_norm)

    t = x.shape[1]
    xs = x[0]
    mems = mem[0]
    target = loss_target[0]
    pos = positions.reshape(t, 1)
    row = lambda a: a.reshape(1, -1)
    rope_tab = _rope_table()

    cx, cy, cc = _coords()
    chip_idx = (2 * cx + cy).astype(jnp.int32).reshape(1)

    wb = {}
    for grp in ("ffn1", "mid", "ffn2"):
        wb[grp] = _pack_segments(wts, grp).astype(BF16)
        if grp == "ffn1":
            ag_ffn1 = _ici_start(wb["ffn1"], pos, "ag_ffn1_start", True)
    own_ffn1, land_ffn1 = _ici_wait(ag_ffn1, wb["ffn2"], "ag_ffn1_wait", True)
    full_ffn1 = _core_share(own_ffn1, land_ffn1, "ag_ffn1_share")
    fw = _unpack_gathered(full_ffn1, "ffn1")
    ag_mid = _ici_start(wb["mid"], full_ffn1, "ag_mid_start", True)
    g_ffn1, g_mix, g_q, g_kv = row(ffn1_norm), row(mix_norm), row(q_norm), row(kv_norm)
    g_x, g_mem, g_ffn2, g_fin = row(xattn_norm), row(mem_norm), row(ffn2_norm), row(final_norm)
    pool_wb = pool_w[0].astype(BF16)
    pool_sc = row(pool_scale)

    h1, n1, gate1, up1 = _ffn_fwd(xs, g_ffn1, fw["ffn1_g"], fw["ffn1_u"], fw["ffn1_d"], "ffn1_fwd", token=ag_mid[4])
    own_mid, land_mid = _ici_wait(ag_mid, h1, "ag_mid_wait", True)
    full_mid = _core_share(own_mid, land_mid, "ag_mid_share")
    fw.update(_unpack_gathered(full_mid, "mid"))
    ag_ffn2 = _ici_start(wb["ffn2"], full_mid, "ag_ffn2_start", True)
    u, z, qn, kvn, qh, kh, vh = _mix_prep(h1, g_mix, fw["w_in"], g_q, fw["w_q"], g_kv, fw["w_kv"], pos, rope_tab,
                                          token=ag_ffn2[4])
    a, lse = _attn_fwd(qh, kh, vh)
    p = _pool_fwd(z, pool_wb, pool_sc)
    memn, km, vm = _mem_kv(mems, g_mem, fw["w_mkv"])
    h2, h3, hn, qm, om = _xattn_fwd(h1, a, p, fw["w_out"], g_x, fw["w_mq"], km, vm, fw["w_mo"])
    own_ffn2, land_ffn2 = _ici_wait(ag_ffn2, h3, "ag_ffn2_wait", True)
    fw.update(_unpack_gathered(_core_share(own_ffn2, land_ffn2, "ag_ffn2_share"), "ffn2"))
    h4, n2, gate2, up2 = _ffn_fwd(h3, g_ffn2, fw["ffn2_g"], fw["ffn2_u"], fw["ffn2_d"], "ffn2_fwd")
    loss_part, dh4, dg_fin = _loss_head(h4, target, g_fin)

    def reduce_start(g8, unit):
        part = _core_reduce(g8, unit)
        return _ici_start(part, g8, "rs_" + unit + "_start", False)

    def by_device(g):
        return g.reshape(N_DEV, -1, D_MODEL)

    rs = {}
    dh3, dgate2, dup2, act2, dg_ffn2 = _ffn_bwd_data(dh4, h3, g_ffn2, gate2, up2, fw["ffn2_g"], fw["ffn2_u"],
                                                     fw["ffn2_d"], "ffn2_bwd")
    rs["ffn2_g"] = reduce_start(by_device(_tn_matmul(dgate2, n2, "ffn2_dwg", tmm=1408, out_dtype=BF16)), "ffn2_g")
    rs["ffn2_u"] = reduce_start(by_device(_tn_matmul(dup2, n2, "ffn2_dwu", tmm=1408, out_dtype=BF16,
                                                     token=rs["ffn2_g"][4])), "ffn2_u")
    rs["ffn2_d"] = reduce_start(by_device(_tn_matmul(act2, dh4, "ffn2_dwd", scale=0.5, tmm=1408, out_dtype=BF16,
                                                     token=rs["ffn2_u"][4])), "ffn2_d")
    dh2, dqm, da, dp, dkm, dvm, dg_x = _xattn_bwd(dh3, h2, qm, g_x, fw["w_mq"], km, vm, fw["w_mo"], fw["w_out"],
                                                  token=rs["ffn2_d"][4])
    gr = {}
    gr["w_mo"] = _tn_matmul(om, dh3, "dw_mo", out_dtype=BF16)
    gr["w_mq"] = _tn_matmul(hn, dqm, "dw_mq", out_dtype=BF16)
    gr["w_out"] = jnp.concatenate([_tn_matmul(a, dh2, "dw_out_a", out_dtype=BF16),
                                   _tn_matmul(p, dh2, "dw_out_p", out_dtype=BF16)], axis=0)
    gr["w_mkv"], dg_mem = _mem_kv_bwd(dkm, dvm, memn, mems, g_mem, fw["w_mkv"])
    dz_pool, d_pool_w, d_pool_sc = _pool_bwd(dp, z, pool_wb, pool_sc)
    dqh, dkh, dvh = _attn_bwd(qh, kh, vh, da, lse, _attn_delta(a, da))
    dh1, dq, dkv, dz, dg_q, dg_kv, dg_mix = _mla_bwd(dqh, dkh, dvh, z, dz_pool, h1, dh2, g_mix, fw["w_in"], g_q,
                                                     fw["w_q"], g_kv, fw["w_kv"], pos, rope_tab)
    gr["w_q"] = _tn_matmul(dq, qn, "dw_q", out_dtype=BF16)
    gr["w_kv"] = _tn_matmul(kvn, dkv, "dw_kv", out_dtype=BF16)
    gr["w_in"] = _tn_matmul(u, dz, "dw_in", out_dtype=BF16)
    g_mid = _pack_grads(gr)
    part_mid = _core_reduce(g_mid, "mid")
    got = {}
    after = part_mid
    for unit in ("ffn2_g", "ffn2_u", "ffn2_d"):
        got[unit] = _ici_wait(rs[unit], after, "rs_" + unit + "_wait", False)
        after = got[unit][1]
    rs["mid"] = _ici_start(part_mid, after, "rs_mid_start", False)
    dx, dgate1, dup1, act1, dg_ffn1 = _ffn_bwd_data(dh1, xs, g_ffn1, gate1, up1, fw["ffn1_g"], fw["ffn1_u"],
                                                    fw["ffn1_d"], "ffn1_bwd", token=rs["mid"][4])
    got["mid"] = _ici_wait(rs["mid"], dx, "rs_mid_wait", False)

    small_g = dict(ffn1_norm=dg_ffn1, mix_norm=dg_mix, q_norm=dg_q, kv_norm=dg_kv, pool_w=d_pool_w,
                   pool_scale=d_pool_sc, xattn_norm=dg_x, mem_norm=dg_mem, ffn2_norm=dg_ffn2, final_norm=dg_fin,
                   loss=loss_part)
    small_ag = _peers_start(_pack_small(small_g), got["mid"][1], "small_ag_start")
    rs["ffn1_g"] = reduce_start(by_device(_tn_matmul(dgate1, n1, "ffn1_dwg", tmm=1408, out_dtype=BF16,
                                                     token=small_ag[4])), "ffn1_g")
    _, parts = _peers_wait(small_ag, rs["ffn1_g"][4], "small_ag_wait")
    small = _adam_small(parts, _pack_small({n: wts[n] for n in SMALL_NAMES}),
                        _pack_small({n: mom[n] for n in SMALL_NAMES}), _pack_small({n: var[n] for n in SMALL_NAMES}))
    small_sum = small[0]
    loss = small_sum[SMALL_OFF["loss"][0], 0]
    shapes = {n: wts[n].shape for n in SMALL_NAMES}
    small = [_unpack_small(s, shapes) for s in small]

    rs["ffn1_u"] = reduce_start(by_device(_tn_matmul(dup1, n1, "ffn1_dwu", tmm=1408, out_dtype=BF16,
                                                     token=small_sum)), "ffn1_u")
    rs["ffn1_d"] = reduce_start(by_device(_tn_matmul(act1, dh1, "ffn1_dwd", scale=0.5, tmm=1408, out_dtype=BF16,
                                                     token=rs["ffn1_u"][4])), "ffn1_d")

    big = {}

    def adam_unit(unit, token):
        part, land = got[unit]
        res = _adam_big(part, land, _pack_unit(wts, unit), _pack_unit(mom, unit), _pack_unit(var, unit),
                        chip_idx, unit, token)
        for k, packed in enumerate(res):
            big.setdefault(k, {}).update(_unpack_unit(packed, unit))
        return res[0]

    done = rs["ffn1_d"][4]
    for unit in ("mid", "ffn2_g", "ffn2_u", "ffn2_d"):
        done = adam_unit(unit, done)
    for unit in ("ffn1_g", "ffn1_u", "ffn1_d"):
        got[unit] = _ici_wait(rs[unit], done, "rs_" + unit + "_wait", False)
        done = adam_unit(unit, done)

    outs = [loss, dx[None]]
    for k in range(4):
        for n in WEIGHT_ORDER:
            outs.append(big[k][n] if n in BIG_NAMES else small[k][n])
    return tuple(outs)
```

```python
import numpy as np

import jax
import jax.numpy as jnp
from jax import lax
from jax.experimental import pallas as pl
from jax.experimental.pallas import tpu as pltpu

F32 = jnp.float32
BF16 = jnp.bfloat16

N_DEV = 8
D_MODEL = 1024
D_FF = 2816
MLA_HEADS = 4
NOPE = 128
ROPE = 64
HEAD_PAD = 256
V_DIM = 128
Q_RANK = 256
KV_RANK = 128
POOL_WINDOWS = (2, 4, 8, 16)
POOL_CH = 128
POOL_HALO = 16
N_MEM = 256
MEM_HEADS = 4
MEM_HD = 256
ROPE_BASE = 10000.0
RMS_EPS = 1e-6
ATTN_SCALE = (NOPE + ROPE) ** -0.5
MEM_SCALE = MEM_HD ** -0.5
NEG_BIG = -1e30

ADAM_LR = 0.001
ADAM_B1 = 0.9
ADAM_B2 = 0.999
ADAM_EPS = 1e-08
ADAM_WD = 0.01
ADAM_STEP = 10
ADAM_C1 = 1.0 - ADAM_B1 ** ADAM_STEP
ADAM_C2 = 1.0 - ADAM_B2 ** ADAM_STEP

VMEM_LIMIT_BYTES = 56 * 1024 * 1024
BF16_ROWS = 16

GROUP_SEGS = {
    "ffn1": (("ffn1_g", 352), ("ffn1_u", 352), ("ffn1_d", 352)),
    "mid": (("w_in", 128), ("w_out", 128), ("w_mq", 128), ("w_mo", 128), ("w_mkv", 256), ("w_q", 32), ("w_kv", 16)),
    "ffn2": (("ffn2_g", 352), ("ffn2_u", 352), ("ffn2_d", 352)),
}
SEG_OFF = {}
GROUP_ROWS = {}
for _g, _segs in GROUP_SEGS.items():
    _o = 0
    for _n, _r in _segs:
        SEG_OFF[_n] = (_o, _r)
        _o += _r
    GROUP_ROWS[_g] = _o

SMALL_ROWS = (("ffn1_norm", 8), ("mix_norm", 8), ("q_norm", 8), ("kv_norm", 8), ("pool_w", 512), ("pool_scale", 8),
              ("xattn_norm", 8), ("mem_norm", 8), ("ffn2_norm", 8), ("final_norm", 8), ("loss", 8))
SMALL_OFF = {}
_o = 0
for _n, _r in SMALL_ROWS:
    SMALL_OFF[_n] = (_o, _r)
    _o += _r


def _cparams(**kw):
    return pltpu.CompilerParams(vmem_limit_bytes=VMEM_LIMIT_BYTES, **kw)


def _row_tile(rows, limit):
    best = None
    for cand in range(BF16_ROWS, min(rows, limit) + 1, BF16_ROWS):
        if rows % cand == 0:
            best = cand
    assert best is not None, rows
    return best


def _dot_nn(a, b):
    return lax.dot_general(a, b, (((1,), (0,)), ((), ())), preferred_element_type=F32)


def _dot_nt(a, b):
    return lax.dot_general(a, b, (((1,), (1,)), ((), ())), preferred_element_type=F32)


def _dot_tn(a, b):
    return lax.dot_general(a, b, (((0,), (0,)), ((), ())), preferred_element_type=F32)


def _rms_fwd(x, g):
    r = lax.rsqrt(jnp.mean(x * x, axis=-1, keepdims=True) + RMS_EPS)
    return x * r * g, r


def _rms_bwd(dy, x, g, r):
    xhat = x * r
    dyg = dy * g
    dx = r * (dyg - xhat * jnp.mean(dyg * xhat, axis=-1, keepdims=True))
    dg = jnp.sum(dy * xhat, axis=0, keepdims=True)
    return dx, dg


def _accumulate(ref, val, first):
    if isinstance(first, bool):
        if first:
            ref[...] = val
        else:
            ref[...] += val
        return

    @pl.when(first)
    def _():
        ref[...] = val

    @pl.when(jnp.logical_not(first))
    def _():
        ref[...] += val


def _call_after(token, body, in_specs, args, **kw):
    if token is not None:
        inner = body
        body = lambda tok_ref, *refs: inner(*refs)
        in_specs = [pl.BlockSpec((8, 128), lambda *_: (0, 0))] + list(in_specs)
        args = (token,) + tuple(args)
    return pl.pallas_call(body, in_specs=in_specs, **kw)(*args)


def _resident(shape):
    return pl.BlockSpec(shape, lambda *_: (0,) * len(shape), pipeline_mode=pl.Buffered(1))


def _rope_tables(pos_col, tab):
    ang = pos_col.astype(F32) * tab[0:1, :]
    return jnp.cos(ang), jnp.sin(ang) * tab[1:2, :]


def _swap_halves(x):
    lane = lax.broadcasted_iota(jnp.int32, x.shape, 1)
    return jnp.where((lane % 64) < 32, pltpu.roll(x, 96, 1), pltpu.roll(x, 32, 1))


def _rope_apply(x, cos_t, sin_t):
    return x * cos_t + _swap_halves(x) * sin_t


def _rope_apply_t(dy, cos_t, sin_t):
    return dy * cos_t + _swap_halves(dy * sin_t)


def _ffn_fwd(h, g, wg_t, wu_t, wd, name, token=None, head=None):
    t, d = h.shape
    f = wg_t.shape[0]
    tm, tf = min(512, t), 256
    nf = f // tf
    n_in = 5 if head is None else 7

    def body(*refs):
        h_ref, g_ref, wg_ref, wu_ref, wd_ref = refs[:5]
        ho_ref, n_ref, gate_ref, up_ref = refs[n_in:n_in + 4]
        nb_sc, acc_sc = refs[-2:]
        y, _ = _rms_fwd(h_ref[...], g_ref[...])
        nb = y.astype(BF16)
        nb_sc[...] = nb
        n_ref[...] = nb
        acc_sc[...] = jnp.zeros_like(acc_sc)

        def f_tile(j):
            rows = pl.ds(pl.multiple_of(j * tf, tf), tf)
            nb = nb_sc[...]
            gt = _dot_nt(nb, wg_ref[rows, :])
            ut = _dot_nt(nb, wu_ref[rows, :])
            gate_ref[j] = gt.astype(BF16)
            up_ref[j] = ut.astype(BF16)
            act = (gt * jax.nn.sigmoid(gt)) * ut
            return _dot_nn(act.astype(BF16), wd_ref[rows, :])

        def pair(p, carry):
            acc_sc[...] += f_tile(2 * p) + f_tile(2 * p + 1)
            return carry

        lax.fori_loop(0, nf // 2, pair, 0)
        if nf % 2:
            acc_sc[...] += f_tile(nf - 1)
        ho = h_ref[...] + 0.5 * acc_sc[...]
        if head is None:
            ho_ref[...] = ho
            return
        t_ref, gf_ref = refs[5:7]
        loss_ref, dgf_ref = refs[n_in + 4:n_in + 6]
        gg = gf_ref[...]
        y, r = _rms_fwd(ho, gg)
        err = y - t_ref[...]
        part = 0.5 * jnp.sum(jnp.mean(err * err, axis=-1, keepdims=True), axis=0, keepdims=True)
        dx, dg = _rms_bwd(err * (1.0 / d), ho, gg, r)
        ho_ref[...] = dx
        first = pl.program_id(0) == 0
        _accumulate(loss_ref, jnp.broadcast_to(part, loss_ref.shape), first)
        _accumulate(dgf_ref, dg, first)

    row = pl.BlockSpec((tm, d), lambda i: (i, 0))
    tiles = pl.BlockSpec((nf, tm, tf), lambda i: (0, i, 0))
    in_specs = [row, _resident((1, d)), _resident((f, d)), _resident((f, d)), _resident((f, d))]
    args = (h, g, wg_t, wu_t, wd)
    out_specs = [row, row, tiles, tiles]
    out_shape = [jax.ShapeDtypeStruct((t, d), F32), jax.ShapeDtypeStruct((t, d), BF16),
                 jax.ShapeDtypeStruct((nf, t, tf), BF16), jax.ShapeDtypeStruct((nf, t, tf), BF16)]
    if head is not None:
        in_specs += [row, _resident((1, d))]
        args += tuple(head)
        out_specs += [pl.BlockSpec((8, 128), lambda i: (0, 0)), pl.BlockSpec((1, d), lambda i: (0, 0))]
        out_shape += [jax.ShapeDtypeStruct((8, 128), F32), jax.ShapeDtypeStruct((1, d), F32)]
    return _call_after(
        token, body, in_specs, args, name=name, grid=(t // tm,), out_specs=out_specs, out_shape=out_shape,
        scratch_shapes=[pltpu.VMEM((tm, d), BF16), pltpu.VMEM((tm, d), F32)],
        compiler_params=_cparams(),
    )


def _ffn_bwd_data(dho, h, g, gate, up, wg_t, wu_t, wd, name, token=None):
    t, d = h.shape
    f = wg_t.shape[0]
    tm, tf = min(1024, t), 256
    parts = 2 if tm % 512 == 0 else 1
    tp = tm // parts
    nf = f // tf

    def body(dho_ref, h_ref, g_ref, gate_ref, up_ref, wg_ref, wu_ref, wd_ref,
             dh_ref, dgate_ref, dup_ref, act_ref, dg_ref, dhb_sc, acc_sc):
        i, j = pl.program_id(0), pl.program_id(1)

        @pl.when(j == 0)
        def _():
            dhb_sc[...] = (0.5 * dho_ref[...]).astype(BF16)
            acc_sc[...] = jnp.zeros_like(acc_sc)

        for r in range(parts):
            rows = pl.ds(r * tp, tp)
            dact = _dot_nt(dhb_sc[rows, :], wd_ref[...])
            gt = gate_ref[0, rows, :].astype(F32)
            ut = up_ref[0, rows, :].astype(F32)
            sg = jax.nn.sigmoid(gt)
            silu = gt * sg
            dgb = (dact * ut * (sg * (1.0 + gt * (1.0 - sg)))).astype(BF16)
            dub = (dact * silu).astype(BF16)
            act_ref[rows, :] = (silu * ut).astype(BF16)
            dgate_ref[rows, :] = dgb
            dup_ref[rows, :] = dub
            acc_sc[rows, :] += _dot_nn(dgb, wg_ref[...]) + _dot_nn(dub, wu_ref[...])

        @pl.when(j == nf - 1)
        def _():
            x = h_ref[...]
            gg = g_ref[...]
            _, r = _rms_fwd(x, gg)
            dx, dg = _rms_bwd(acc_sc[...], x, gg, r)
            dh_ref[...] = dho_ref[...] + dx
            _accumulate(dg_ref, dg, i == 0)

    return _call_after(
        token, body,
        [pl.BlockSpec((tm, d), lambda i, j: (i, 0)),
         pl.BlockSpec((tm, d), lambda i, j: (i, 0)),
         pl.BlockSpec((1, d), lambda i, j: (0, 0)),
         pl.BlockSpec((1, tm, tf), lambda i, j: (j, i, 0)),
         pl.BlockSpec((1, tm, tf), lambda i, j: (j, i, 0)),
         pl.BlockSpec((tf, d), lambda i, j: (j, 0)),
         pl.BlockSpec((tf, d), lambda i, j: (j, 0)),
         pl.BlockSpec((tf, d), lambda i, j: (j, 0))],
        (dho, h, g, gate, up, wg_t, wu_t, wd),
        name=name, grid=(t // tm, nf),
        out_specs=[pl.BlockSpec((tm, d), lambda i, j: (i, 0)),
                   pl.BlockSpec((tm, tf), lambda i, j: (i, j)),
                   pl.BlockSpec((tm, tf), lambda i, j: (i, j)),
                   pl.BlockSpec((tm, tf), lambda i, j: (i, j)),
                   pl.BlockSpec((1, d), lambda i, j: (0, 0))],
        out_shape=[jax.ShapeDtypeStruct((t, d), F32), jax.ShapeDtypeStruct((t, f), BF16),
                   jax.ShapeDtypeStruct((t, f), BF16), jax.ShapeDtypeStruct((t, f), BF16),
                   jax.ShapeDtypeStruct((1, d), F32)],
        scratch_shapes=[pltpu.VMEM((tm, d), BF16), pltpu.VMEM((tm, d), F32)],
        compiler_params=_cparams(),
    )


def _tn_matmul(a, b, name, scale=1.0, tmm=None, out_dtype=F32, token=None):
    t, m = a.shape
    n = b.shape[1]
    tmm = m if tmm is None else tmm
    tk = min(1024, t)
    nk = t // tk

    def product(a_ref, b_ref):
        prod = _dot_tn(a_ref[...].astype(BF16), b_ref[...].astype(BF16))
        return prod * scale if scale != 1.0 else prod

    def body_f32(a_ref, b_ref, o_ref):
        _accumulate(o_ref, product(a_ref, b_ref), pl.program_id(1) == 0)

    def body_cast(a_ref, b_ref, o_ref, acc_sc):
        k = pl.program_id(1)
        _accumulate(acc_sc, product(a_ref, b_ref), k == 0)

        @pl.when(k == nk - 1)
        def _():
            o_ref[...] = acc_sc[...].astype(out_dtype)

    direct = out_dtype == F32
    return _call_after(
        token, body_f32 if direct else body_cast,
        [pl.BlockSpec((tk, tmm), lambda i, k: (k, i)),
         pl.BlockSpec((tk, n), lambda i, k: (k, 0))],
        (a, b),
        name=name, grid=(m // tmm, nk),
        out_specs=pl.BlockSpec((tmm, n), lambda i, k: (i, 0)),
        out_shape=jax.ShapeDtypeStruct((m, n), out_dtype),
        scratch_shapes=[] if direct else [pltpu.VMEM((tmm, n), F32)],
        compiler_params=_cparams(),
    )


def _mix_prep(h1, mix_norm, w_in, q_norm, wq_t, kv_norm, wkv, pos, rope_tab, token=None):
    t, d = h1.shape
    tm = min(512, t)

    def body(h_ref, gm_ref, win_ref, gq_ref, wq_ref, gkv_ref, wkv_ref, pos_ref, tab_ref,
             u_ref, z_ref, qn_ref, kvn_ref, q_ref, k_ref, v_ref):
        u, _ = _rms_fwd(h_ref[...], gm_ref[...])
        ub = u.astype(BF16)
        u_ref[...] = ub
        z = _dot_nn(ub, win_ref[...])
        z_ref[...] = z
        cos_t, sin_t = _rope_tables(pos_ref[...], tab_ref[...])
        qn, _ = _rms_fwd(z[:, 0:Q_RANK], gq_ref[...])
        qnb = qn.astype(BF16)
        qn_ref[...] = qnb
        q = _dot_nt(qnb, wq_ref[...])
        kvn, _ = _rms_fwd(z[:, Q_RANK:Q_RANK + KV_RANK], gkv_ref[...])
        kvnb = kvn.astype(BF16)
        kvn_ref[...] = kvnb
        kv = _dot_nn(kvnb, wkv_ref[...])
        k_pe = _rope_apply(z[:, Q_RANK + KV_RANK:Q_RANK + KV_RANK + 128], cos_t, sin_t)
        ones = jnp.ones((tm, V_DIM), F32)
        for hh in range(MLA_HEADS):
            b = hh * HEAD_PAD
            q_pe = _rope_apply(q[:, b + NOPE:b + HEAD_PAD], cos_t, sin_t)
            q_ref[hh] = jnp.concatenate([q[:, b:b + NOPE], q_pe], axis=-1).astype(BF16)
            k_ref[hh] = jnp.concatenate([kv[:, b:b + NOPE], k_pe], axis=-1).astype(BF16)
            v_ref[hh] = jnp.concatenate([kv[:, b + NOPE:b + HEAD_PAD], ones], axis=-1).astype(BF16)

    full = lambda shape: pl.BlockSpec(shape, lambda i: (0,) * len(shape))
    return _call_after(
        token, body,
        [pl.BlockSpec((tm, d), lambda i: (i, 0)), _resident((1, d)), _resident(w_in.shape), _resident((1, Q_RANK)),
         _resident(wq_t.shape), _resident((1, KV_RANK)), _resident(wkv.shape),
         pl.BlockSpec((tm, 1), lambda i: (i, 0)), _resident(rope_tab.shape)],
        (h1, mix_norm, w_in, q_norm, wq_t, kv_norm, wkv, pos, rope_tab),
        name="mix_prep", grid=(t // tm,),
        out_specs=[pl.BlockSpec((tm, d), lambda i: (i, 0)),
                   pl.BlockSpec((tm, d), lambda i: (i, 0)),
                   pl.BlockSpec((tm, Q_RANK), lambda i: (i, 0)),
                   pl.BlockSpec((tm, KV_RANK), lambda i: (i, 0)),
                   pl.BlockSpec((MLA_HEADS, tm, HEAD_PAD), lambda i: (0, i, 0)),
                   pl.BlockSpec((MLA_HEADS, tm, HEAD_PAD), lambda i: (0, i, 0)),
                   pl.BlockSpec((MLA_HEADS, tm, 2 * V_DIM), lambda i: (0, i, 0))],
        out_shape=[jax.ShapeDtypeStruct((t, d), BF16), jax.ShapeDtypeStruct((t, d), F32),
                   jax.ShapeDtypeStruct((t, Q_RANK), BF16), jax.ShapeDtypeStruct((t, KV_RANK), BF16),
                   jax.ShapeDtypeStruct((MLA_HEADS, t, HEAD_PAD), BF16),
                   jax.ShapeDtypeStruct((MLA_HEADS, t, HEAD_PAD), BF16),
                   jax.ShapeDtypeStruct((MLA_HEADS, t, 2 * V_DIM), BF16)],
        compiler_params=_cparams(),
    )


def _causal_mask(s):
    row = lax.broadcasted_iota(jnp.int32, s.shape, 0)
    col = lax.broadcasted_iota(jnp.int32, s.shape, 1)
    return jnp.where(col <= row, s, NEG_BIG)


def _attn_fwd(q, k, v):
    nh, t, _ = q.shape
    tq = tk = min(512, t)
    nq, nk = t // tq, t // tk

    pairs = [(i, j) for i in range(nq) for j in range(i + 1)]
    qi = jnp.asarray(np.array([i for i, _ in pairs], np.int32))
    kj = jnp.asarray(np.array([j for _, j in pairs], np.int32))

    def body(qi_ref, kj_ref, q_ref, k_ref, v_ref, o_ref, lse_ref, m_sc, acc_sc):
        n = pl.program_id(0)
        i, j = qi_ref[n], kj_ref[n]

        @pl.when(j == 0)
        def _():
            m_sc[...] = jnp.full_like(m_sc, NEG_BIG)
            acc_sc[...] = jnp.zeros_like(acc_sc)

        def step(diagonal):
            for hh in range(nh):
                s = _dot_nt(q_ref[hh], k_ref[hh]) * ATTN_SCALE
                if diagonal:
                    s = _causal_mask(s)
                m_old = m_sc[hh]
                m_new = jnp.maximum(m_old, jnp.max(s, axis=-1, keepdims=True))
                p = jnp.exp(s - m_new).astype(BF16)
                acc_sc[hh] = jnp.exp(m_old - m_new) * acc_sc[hh] + _dot_nn(p, v_ref[hh])
                m_sc[hh] = m_new

        @pl.when(j < i)
        def _():
            step(False)

        @pl.when(j == i)
        def _():
            step(True)
            for hh in range(nh):
                acc = acc_sc[hh]
                l = acc[:, V_DIM:2 * V_DIM]
                o_ref[:, hh * V_DIM:(hh + 1) * V_DIM] = (acc[:, 0:V_DIM] / l).astype(BF16)
                lse_ref[hh] = m_sc[hh] + jnp.log(l[:, 0:1])

    q_map = lambda n, qi_ref, kj_ref: (0, qi_ref[n], 0)
    kv_map = lambda n, qi_ref, kj_ref: (0, kj_ref[n], 0)
    return pl.pallas_call(
        body, name="attn_fwd",
        grid_spec=pltpu.PrefetchScalarGridSpec(
            num_scalar_prefetch=2, grid=(len(pairs),),
            in_specs=[pl.BlockSpec((nh, tq, HEAD_PAD), q_map),
                      pl.BlockSpec((nh, tk, HEAD_PAD), kv_map),
                      pl.BlockSpec((nh, tk, 2 * V_DIM), kv_map)],
            out_specs=[pl.BlockSpec((tq, nh * V_DIM), lambda n, qi_ref, kj_ref: (qi_ref[n], 0)),
                       pl.BlockSpec((nh, tq, 1), q_map)],
            scratch_shapes=[pltpu.VMEM((nh, tq, 1), F32), pltpu.VMEM((nh, tq, 2 * V_DIM), F32)]),
        out_shape=[jax.ShapeDtypeStruct((t, nh * V_DIM), BF16), jax.ShapeDtypeStruct((nh, t, 1), F32)],
        compiler_params=_cparams(),
    )(qi, kj, q, k, v)


def _attn_delta(o, do):
    t, w = o.shape
    nh = w // V_DIM
    tm = min(512, t)

    def body(o_ref, do_ref, d_ref):
        prod = o_ref[...].astype(F32) * do_ref[...].astype(F32)
        for hh in range(nh):
            d_ref[hh] = jnp.sum(prod[:, hh * V_DIM:(hh + 1) * V_DIM], axis=-1, keepdims=True)

    return pl.pallas_call(
        body, name="attn_delta", grid=(t // tm,),
        in_specs=[pl.BlockSpec((tm, w), lambda i: (i, 0)), pl.BlockSpec((tm, w), lambda i: (i, 0))],
        out_specs=pl.BlockSpec((nh, tm, 1), lambda i: (0, i, 0)),
        out_shape=jax.ShapeDtypeStruct((nh, t, 1), F32),
        compiler_params=_cparams(),
    )(o, do)


ATTN_BWD_HEADS = 2


def _attn_bwd(q, k, v, do, lse, delta):
    nh, t, _ = q.shape
    hp = ATTN_BWD_HEADS
    tq = tk = min(512, t)
    nq, nk = t // tq, t // tk

    pairs = [(j, i) for j in range(nk) for i in range(j, nq)]
    kj = jnp.asarray(np.array([j for j, _ in pairs], np.int32))
    qi = jnp.asarray(np.array([i for _, i in pairs], np.int32))

    def body(kj_ref, qi_ref, q_ref, k_ref, v_ref, do_ref, lse_ref, dlt_ref, dq_ref, dk_ref, dv_ref):
        n = pl.program_id(1)
        j, i = kj_ref[n], qi_ref[n]

        @pl.when(n == 0)
        def _():
            dq_ref[...] = jnp.zeros_like(dq_ref)

        def step(diagonal):
            for hh in range(hp):
                qq, kk = q_ref[hh], k_ref[hh]
                dob = do_ref[:, hh * V_DIM:(hh + 1) * V_DIM]
                s = _dot_nt(qq, kk) * ATTN_SCALE
                if diagonal:
                    s = _causal_mask(s)
                p = jnp.exp(s - lse_ref[hh])
                dpp = _dot_nt(dob, v_ref[hh])
                dsb = (p * (dpp - dlt_ref[hh]) * ATTN_SCALE).astype(BF16)
                _accumulate(dv_ref.at[hh], _dot_tn(p.astype(BF16), dob), diagonal)
                _accumulate(dk_ref.at[hh], _dot_tn(dsb, qq), diagonal)
                dq_ref[hh, pl.ds(pl.multiple_of(i * tq, tq), tq), :] += _dot_nn(dsb, kk)

        @pl.when(i > j)
        def _():
            step(False)

        @pl.when(i == j)
        def _():
            step(True)

    q_map = lambda h, n, kj_ref, qi_ref: (h, qi_ref[n], 0)
    k_map = lambda h, n, kj_ref, qi_ref: (h, kj_ref[n], 0)
    return pl.pallas_call(
        body, name="attn_bwd",
        grid_spec=pltpu.PrefetchScalarGridSpec(
            num_scalar_prefetch=2, grid=(nh // hp, len(pairs)),
            in_specs=[pl.BlockSpec((hp, tq, HEAD_PAD), q_map),
                      pl.BlockSpec((hp, tk, HEAD_PAD), k_map),
                      pl.BlockSpec((hp, tk, V_DIM), k_map),
                      pl.BlockSpec((tq, hp * V_DIM), lambda h, n, kj_ref, qi_ref: (qi_ref[n], h)),
                      pl.BlockSpec((hp, tq, 1), q_map),
                      pl.BlockSpec((hp, tq, 1), q_map)],
            out_specs=[pl.BlockSpec((hp, t, HEAD_PAD), lambda h, n, kj_ref, qi_ref: (h, 0, 0)),
                       pl.BlockSpec((hp, tk, HEAD_PAD), k_map),
                       pl.BlockSpec((hp, tk, V_DIM), k_map)]),
        out_shape=[jax.ShapeDtypeStruct((nh, t, HEAD_PAD), F32), jax.ShapeDtypeStruct((nh, t, HEAD_PAD), F32),
                   jax.ShapeDtypeStruct((nh, t, V_DIM), F32)],
        compiler_params=_cparams(),
    )(kj, qi, q, k, v, do, lse, delta)


def _pool_counts(first_token, rows, w):
    tok = lax.broadcasted_iota(jnp.int32, (rows, POOL_CH), 0) + first_token
    return jnp.minimum(tok + 1, w).astype(F32)


def _pool_centered(zbuf, g, w, i, tm):
    lanes = pl.ds(g * POOL_CH, POOL_CH)
    cur = zbuf[pl.ds(POOL_HALO, tm), lanes]
    win = cur
    for s in range(1, w):
        win = win + zbuf[pl.ds(POOL_HALO - s, tm), lanes]
    return win / _pool_counts(i * tm, tm, w) - cur


def _pool_load(zbuf, z_ref, halo_ref, i, tm):
    @pl.when(i == 0)
    def _():
        zbuf[pl.ds(0, POOL_HALO), :] = jnp.zeros((POOL_HALO, zbuf.shape[1]), F32)

    @pl.when(i > 0)
    def _():
        zbuf[pl.ds(0, POOL_HALO), :] = halo_ref[...]

    zbuf[pl.ds(POOL_HALO, tm), :] = z_ref[...]


def _pool_fwd(z, pool_w, pool_scale):
    t = z.shape[0]
    pw = len(POOL_WINDOWS) * POOL_CH
    tm = min(512, t)
    hb = tm // POOL_HALO

    def body(z_ref, halo_ref, w_ref, sc_ref, p_ref, zbuf):
        i = pl.program_id(0)
        _pool_load(zbuf, z_ref, halo_ref, i, tm)
        for g, w in enumerate(POOL_WINDOWS):
            c = _pool_centered(zbuf, g, w, i, tm)
            y = _dot_nn(c.astype(BF16), w_ref[g]) * sc_ref[:, g * POOL_CH:(g + 1) * POOL_CH]
            p_ref[:, g * POOL_CH:(g + 1) * POOL_CH] = y.astype(BF16)

    return pl.pallas_call(
        body, name="pool_fwd", grid=(t // tm,),
        in_specs=[pl.BlockSpec((tm, pw), lambda i: (i, 1)),
                  pl.BlockSpec((POOL_HALO, pw), lambda i: (jnp.maximum(i * hb - 1, 0), 1)),
                  pl.BlockSpec(pool_w.shape, lambda i: (0, 0, 0)),
                  pl.BlockSpec((1, pw), lambda i: (0, 0))],
        out_specs=pl.BlockSpec((tm, pw), lambda i: (i, 0)),
        out_shape=jax.ShapeDtypeStruct((t, pw), BF16),
        scratch_shapes=[pltpu.VMEM((POOL_HALO + tm, pw), F32)],
        compiler_params=_cparams(),
    )(z, z, pool_w, pool_scale)


def _pool_bwd(dp, z, pool_w, pool_scale):
    t = z.shape[0]
    ng = len(POOL_WINDOWS)
    pw = ng * POOL_CH
    tm = min(512, t)
    hb = tm // POOL_HALO
    nt = t // tm

    def body(dp_ref, dpn_ref, z_ref, halo_ref, w_ref, sc_ref, dz_ref, dw_ref, dsc_ref, zbuf, dbuf):
        i = pl.program_id(0)
        _pool_load(zbuf, z_ref, halo_ref, i, tm)

        @pl.when(i == 0)
        def _():
            dw_ref[...] = jnp.zeros_like(dw_ref)
            dsc_ref[...] = jnp.zeros_like(dsc_ref)

        nxt_ok = (i < nt - 1).astype(F32)
        for g, w in enumerate(POOL_WINDOWS):
            lanes = pl.ds(g * POOL_CH, POOL_CH)
            cols = slice(g * POOL_CH, (g + 1) * POOL_CH)
            sc = sc_ref[:, cols]
            wg = w_ref[g]
            c = _pool_centered(zbuf, g, w, i, tm).astype(BF16)
            ypre = _dot_nn(c, wg)
            dpg = dp_ref[:, cols].astype(F32)
            dsc_ref[:, cols] += jnp.sum(dpg * ypre, axis=0, keepdims=True)
            dyb = (dpg * sc).astype(BF16)
            dw_ref[g] += _dot_tn(c, dyb)
            dd = _dot_nt(dyb, wg)
            dyn = (dpn_ref[:, cols].astype(F32) * sc).astype(BF16)
            ddn = _dot_nt(dyn, wg) * nxt_ok
            dbuf[pl.ds(0, tm), lanes] = dd / _pool_counts(i * tm, tm, w)
            dbuf[pl.ds(tm, POOL_HALO), lanes] = ddn / _pool_counts((i + 1) * tm, POOL_HALO, w)
            acc = -dd
            for s in range(w):
                acc = acc + dbuf[pl.ds(s, tm), lanes]
            dz_ref[:, cols] = acc

    return pl.pallas_call(
        body, name="pool_bwd", grid=(nt,),
        in_specs=[pl.BlockSpec((tm, pw), lambda i: (i, 0)),
                  pl.BlockSpec((POOL_HALO, pw), lambda i: (jnp.minimum((i + 1) * hb, t // POOL_HALO - 1), 0)),
                  pl.BlockSpec((tm, pw), lambda i: (i, 1)),
                  pl.BlockSpec((POOL_HALO, pw), lambda i: (jnp.maximum(i * hb - 1, 0), 1)),
                  pl.BlockSpec(pool_w.shape, lambda i: (0, 0, 0)),
                  pl.BlockSpec((1, pw), lambda i: (0, 0))],
        out_specs=[pl.BlockSpec((tm, pw), lambda i: (i, 0)),
                   pl.BlockSpec((ng, POOL_CH, POOL_CH), lambda i: (0, 0, 0)),
                   pl.BlockSpec((1, pw), lambda i: (0, 0))],
        out_shape=[jax.ShapeDtypeStruct((t, pw), F32), jax.ShapeDtypeStruct((ng, POOL_CH, POOL_CH), F32),
                   jax.ShapeDtypeStruct((1, pw), F32)],
        scratch_shapes=[pltpu.VMEM((POOL_HALO + tm, pw), F32), pltpu.VMEM((tm + POOL_HALO, pw), F32)],
        compiler_params=_cparams(),
    )(dp, dp, z, z, pool_w, pool_scale)


def _mla_bwd(dq_h, dk_h, dv_h, z, dz_pool, h1, dh2, mix_norm, w_in, q_norm, wq_t, kv_norm, wkv, pos, rope_tab):
    t, d = h1.shape
    tm = min(512, t)

    def body(dqh_ref, dkh_ref, dvh_ref, z_ref, dzp_ref, h_ref, dh2_ref, gm_ref, win_ref, gq_ref, wq_ref, gkv_ref,
             wkv_ref, pos_ref, tab_ref, dh1_ref, dq_ref, dkv_ref, dz_ref, dgq_ref, dgkv_ref, dgm_ref):
        i = pl.program_id(0)
        first = i == 0
        cos_t, sin_t = _rope_tables(pos_ref[...], tab_ref[...])
        dq_parts, dkv_parts = [], []
        dk_pe = jnp.zeros((tm, 128), F32)
        for hh in range(MLA_HEADS):
            dqh = dqh_ref[hh]
            dq_parts += [dqh[:, 0:NOPE], _rope_apply_t(dqh[:, NOPE:HEAD_PAD], cos_t, sin_t)]
            dkh = dkh_ref[hh]
            dkv_parts += [dkh[:, 0:NOPE], dvh_ref[hh]]
            dk_pe = dk_pe + dkh[:, NOPE:HEAD_PAD]
        dqb = jnp.concatenate(dq_parts, axis=-1).astype(BF16)
        dkvb = jnp.concatenate(dkv_parts, axis=-1).astype(BF16)
        dq_ref[...] = dqb
        dkv_ref[...] = dkvb
        z = z_ref[...]
        c_q = z[:, 0:Q_RANK]
        gq = gq_ref[...]
        _, rq = _rms_fwd(c_q, gq)
        dcq, dgq = _rms_bwd(_dot_nn(dqb, wq_ref[...]), c_q, gq, rq)
        c_kv = z[:, Q_RANK:Q_RANK + KV_RANK]
        gkv = gkv_ref[...]
        _, rkv = _rms_fwd(c_kv, gkv)
        dckv, dgkv = _rms_bwd(_dot_nt(dkvb, wkv_ref[...]), c_kv, gkv, rkv)
        dkr = _rope_apply_t(dk_pe, cos_t, sin_t)
        dzb = jnp.concatenate([dcq, dckv, dkr, dzp_ref[...]], axis=-1).astype(BF16)
        dz_ref[...] = dzb
        x = h_ref[...]
        gm = gm_ref[...]
        _, rm = _rms_fwd(x, gm)
        dx, dgm = _rms_bwd(_dot_nt(dzb, win_ref[...]), x, gm, rm)
        dh1_ref[...] = dh2_ref[...] + dx
        _accumulate(dgq_ref, dgq, first)
        _accumulate(dgkv_ref, dgkv, first)
        _accumulate(dgm_ref, dgm, first)

    full = lambda shape: pl.BlockSpec(shape, lambda i: (0,) * len(shape))
    row = lambda w: pl.BlockSpec((tm, w), lambda i: (i, 0))
    head = lambda w: pl.BlockSpec((MLA_HEADS, tm, w), lambda i: (0, i, 0))
    pw = len(POOL_WINDOWS) * POOL_CH
    return pl.pallas_call(
        body, name="mla_bwd", grid=(t // tm,),
        in_specs=[head(HEAD_PAD), head(HEAD_PAD), head(V_DIM), row(d), row(pw), row(d), row(d),
                  _resident((1, d)), _resident(w_in.shape), _resident((1, Q_RANK)), _resident(wq_t.shape),
                  _resident((1, KV_RANK)), _resident(wkv.shape), row(1), _resident(rope_tab.shape)],
        out_specs=[row(d), row(d), row(d), row(d), full((1, Q_RANK)), full((1, KV_RANK)), full((1, d))],
        out_shape=[jax.ShapeDtypeStruct((t, d), F32), jax.ShapeDtypeStruct((t, d), BF16),
                   jax.ShapeDtypeStruct((t, d), BF16), jax.ShapeDtypeStruct((t, d), BF16),
                   jax.ShapeDtypeStruct((1, Q_RANK), F32), jax.ShapeDtypeStruct((1, KV_RANK), F32),
                   jax.ShapeDtypeStruct((1, d), F32)],
        compiler_params=_cparams(),
    )(dq_h, dk_h, dv_h, z, dz_pool, h1, dh2, mix_norm, w_in, q_norm, wq_t, kv_norm, wkv, pos, rope_tab)


def _mem_kv(mem, mem_norm, wmkv):
    n, d = mem.shape

    def body(mem_ref, g_ref, w_ref, memn_ref, k_ref, v_ref):
        y, _ = _rms_fwd(mem_ref[...], g_ref[...])
        yb = y.astype(BF16)
        memn_ref[...] = yb
        for hh in range(MEM_HEADS):
            k_ref[hh] = _dot_nn(yb, w_ref[hh]).astype(BF16)
            v_ref[hh] = _dot_nn(yb, w_ref[MEM_HEADS + hh]).astype(BF16)

    return pl.pallas_call(
        body, name="mem_kv",
        out_shape=[jax.ShapeDtypeStruct((n, d), BF16), jax.ShapeDtypeStruct((MEM_HEADS, n, MEM_HD), BF16),
                   jax.ShapeDtypeStruct((MEM_HEADS, n, MEM_HD), BF16)],
        compiler_params=_cparams(),
    )(mem, mem_norm, wmkv)


def _mem_softmax(qb, km):
    s = _dot_nt(qb, km) * MEM_SCALE
    e = jnp.exp(s - jnp.max(s, axis=-1, keepdims=True))
    return e / jnp.sum(e, axis=-1, keepdims=True)


def _xattn_fwd(h1, a, p, w_out, g, wmq, km, vm, wmo, token=None):
    t, d = h1.shape
    tm = min(512, t)
    half = a.shape[1]

    def body(h_ref, a_ref, p_ref, wo_ref, g_ref, wmq_ref, km_ref, vm_ref, wmo_ref,
             h2_ref, h3_ref, hn_ref, q_ref, o_ref):
        h2 = h_ref[...] + _dot_nn(a_ref[...], wo_ref[0:half, :]) + _dot_nn(p_ref[...], wo_ref[half:2 * half, :])
        h2_ref[...] = h2
        hn, _ = _rms_fwd(h2, g_ref[...])
        hnb = hn.astype(BF16)
        hn_ref[...] = hnb
        qb = _dot_nn(hnb, wmq_ref[...]).astype(BF16)
        q_ref[...] = qb
        outs = []
        for hh in range(MEM_HEADS):
            pr = _mem_softmax(qb[:, hh * MEM_HD:(hh + 1) * MEM_HD], km_ref[hh])
            outs.append(_dot_nn(pr.astype(BF16), vm_ref[hh]))
        ob = jnp.concatenate(outs, axis=-1).astype(BF16)
        o_ref[...] = ob
        h3_ref[...] = h2 + _dot_nn(ob, wmo_ref[...])

    full = lambda shape: pl.BlockSpec(shape, lambda i: (0,) * len(shape))
    row = lambda w: pl.BlockSpec((tm, w), lambda i: (i, 0))
    return _call_after(
        token, body,
        [row(d), row(half), row(half), _resident(w_out.shape), _resident((1, d)), _resident(wmq.shape),
         _resident(km.shape), _resident(vm.shape), _resident(wmo.shape)],
        (h1, a, p, w_out, g, wmq, km, vm, wmo),
        name="xattn_fwd", grid=(t // tm,),
        out_specs=[row(d), row(d), row(d), row(d), row(d)],
        out_shape=[jax.ShapeDtypeStruct((t, d), F32), jax.ShapeDtypeStruct((t, d), F32),
                   jax.ShapeDtypeStruct((t, d), BF16), jax.ShapeDtypeStruct((t, d), BF16),
                   jax.ShapeDtypeStruct((t, d), BF16)],
        compiler_params=_cparams(),
    )


def _xattn_bwd(dh3, h2, qm, g, wmq, km, vm, wmo, w_out, token=None):
    t, d = h2.shape
    tm = min(512, t)
    half = d // 2

    def body(dh3_ref, h2_ref, q_ref, g_ref, wmq_ref, km_ref, vm_ref, wmo_ref, wo_ref,
             dh2_ref, dq_ref, da_ref, dp_ref, dk_ref, dv_ref, dg_ref):
        i = pl.program_id(0)
        first = i == 0

        @pl.when(first)
        def _():
            dk_ref[...] = jnp.zeros_like(dk_ref)
            dv_ref[...] = jnp.zeros_like(dv_ref)

        dh3 = dh3_ref[...]
        dob = _dot_nt(dh3.astype(BF16), wmo_ref[...]).astype(BF16)
        qb = q_ref[...]
        dq_parts = []
        for hh in range(MEM_HEADS):
            cols = slice(hh * MEM_HD, (hh + 1) * MEM_HD)
            kk, vv = km_ref[hh], vm_ref[hh]
            pr = _mem_softmax(qb[:, cols], kk)
            doh = dob[:, cols]
            dv_ref[hh] += _dot_tn(pr.astype(BF16), doh)
            dpp = _dot_nt(doh, vv)
            dsb = (pr * (dpp - jnp.sum(dpp * pr, axis=-1, keepdims=True)) * MEM_SCALE).astype(BF16)
            dq_parts.append(_dot_nn(dsb, kk))
            dk_ref[hh] += _dot_tn(dsb, qb[:, cols])
        dqb = jnp.concatenate(dq_parts, axis=-1).astype(BF16)
        dq_ref[...] = dqb
        x = h2_ref[...]
        gg = g_ref[...]
        _, r = _rms_fwd(x, gg)
        dx, dg = _rms_bwd(_dot_nt(dqb, wmq_ref[...]), x, gg, r)
        dh2 = dh3 + dx
        dh2_ref[...] = dh2
        dap = _dot_nt(dh2.astype(BF16), wo_ref[...])
        da_ref[...] = dap[:, 0:half].astype(BF16)
        dp_ref[...] = dap[:, half:d].astype(BF16)
        _accumulate(dg_ref, dg, first)

    full = lambda shape: pl.BlockSpec(shape, lambda i: (0,) * len(shape))
    row = lambda w: pl.BlockSpec((tm, w), lambda i: (i, 0))
    return _call_after(
        token, body,
        [row(d), row(d), row(d), _resident((1, d)), _resident(wmq.shape), _resident(km.shape), _resident(vm.shape),
         _resident(wmo.shape), _resident(w_out.shape)],
        (dh3, h2, qm, g, wmq, km, vm, wmo, w_out),
        name="xattn_bwd", grid=(t // tm,),
        out_specs=[row(d), row(d), row(half), row(half), full(km.shape), full(vm.shape), full((1, d))],
        out_shape=[jax.ShapeDtypeStruct((t, d), F32), jax.ShapeDtypeStruct((t, d), BF16),
                   jax.ShapeDtypeStruct((t, half), BF16), jax.ShapeDtypeStruct((t, half), BF16),
                   jax.ShapeDtypeStruct(km.shape, F32), jax.ShapeDtypeStruct(vm.shape, F32),
                   jax.ShapeDtypeStruct((1, d), F32)],
        compiler_params=_cparams(),
    )


def _mem_kv_bwd(dkm, dvm, memn, mem, mem_norm, wmkv):
    n, d = mem.shape

    def body(dk_ref, dv_ref, memn_ref, mem_ref, g_ref, w_ref, dw_ref, dg_ref):
        memn = memn_ref[...]
        dmemn = jnp.zeros((n, d), F32)
        for s in range(2 * MEM_HEADS):
            src = dk_ref[s] if s < MEM_HEADS else dv_ref[s - MEM_HEADS]
            db = src.astype(BF16)
            dw_ref[s] = _dot_tn(memn, db)
            dmemn = dmemn + _dot_nt(db, w_ref[s])
        x = mem_ref[...]
        gg = g_ref[...]
        _, r = _rms_fwd(x, gg)
        _, dg = _rms_bwd(dmemn, x, gg, r)
        dg_ref[...] = dg

    return pl.pallas_call(
        body, name="mem_kv_bwd",
        out_shape=[jax.ShapeDtypeStruct(wmkv.shape, F32), jax.ShapeDtypeStruct((1, d), F32)],
        compiler_params=_cparams(),
    )(dkm, dvm, memn, mem, mem_norm, wmkv)


MESH_ID = pl.DeviceIdType.MESH
ANY = pl.BlockSpec(memory_space=pl.ANY)


def _coords():
    return lax.axis_index("x"), lax.axis_index("y"), lax.axis_index("c")


def _other_chips(x, y):
    return [(1 - x, y), (x, 1 - y), (1 - x, 1 - y)]


def _core_reduce(g, tag):
    _, r, w = g.shape

    def body(g_ref, part_ref, own_sc, recv_sc, send_sems, recv_sems, local_sems):
        x, y, c = _coords()
        sent, local = [], []
        for chip in range(4):
            sent.append(pltpu.make_async_remote_copy(
                src_ref=g_ref.at[2 * chip + (1 - c)], dst_ref=recv_sc.at[chip],
                send_sem=send_sems.at[chip], recv_sem=recv_sems.at[chip],
                device_id=(x, y, 1 - c), device_id_type=MESH_ID))
            local.append(pltpu.make_async_copy(g_ref.at[2 * chip + c], own_sc.at[chip], local_sems.at[chip]))
        for cp in sent + local:
            cp.start()
        for chip in range(4):
            local[chip].wait()
            sent[chip].wait_recv()
            part_ref[chip] = (own_sc[chip].astype(F32) + recv_sc[chip].astype(F32)).astype(part_ref.dtype)
        for cp in sent:
            cp.wait_send()

    return pl.pallas_call(
        body, name="core_reduce_" + tag,
        out_shape=jax.ShapeDtypeStruct((4, r, w), g.dtype),
        in_specs=[ANY], out_specs=pl.BlockSpec(memory_space=pltpu.VMEM),
        scratch_shapes=[pltpu.VMEM((4, r, w), g.dtype), pltpu.VMEM((4, r, w), g.dtype),
                        pltpu.SemaphoreType.DMA((4,)), pltpu.SemaphoreType.DMA((4,)), pltpu.SemaphoreType.DMA((4,))],
        compiler_params=_cparams(),
    )(g)


HBM_SPEC = pl.BlockSpec(memory_space=pltpu.HBM)
SEM_SPEC = pl.BlockSpec(memory_space=pltpu.SEMAPHORE)
SPLIT_EFFECT = pltpu.SideEffectType.DATAFLOW_SIDE_EFFECTING


def _ici_refs(gather, src_ref, land_ref, j, px, py, slot_chip, c):
    if gather:
        return src_ref, land_ref.at[:, 4 * slot_chip[0] + 2 * slot_chip[1] + c]
    return src_ref.at[2 * px + py], land_ref.at[j]


def _ici_start(src, after, name, gather):
    r, w = src.shape[-2:]
    land_shape = (src.shape[0], N_DEV, r, w) if gather else (3, r, w)

    def body(src_ref, land_ref, after_ref, send_sems, recv_sems, src_thru, land_thru, token):
        x, y, c = _coords()
        for j, (px, py) in enumerate(_other_chips(x, y)):
            s_ref, d_ref = _ici_refs(gather, src_ref, land_ref, j, px, py, (x, y), c)
            pltpu.make_async_remote_copy(
                src_ref=s_ref, dst_ref=d_ref, send_sem=send_sems.at[j], recv_sem=recv_sems.at[j],
                device_id=(px, py, c), device_id_type=MESH_ID).start()
        token[...] = jnp.zeros_like(token)

    return pl.pallas_call(
        body, name=name,
        out_shape=(pltpu.SemaphoreType.DMA((3,)), pltpu.SemaphoreType.DMA((3,)), pltpu.HBM(src.shape, src.dtype),
                   pltpu.HBM(land_shape, src.dtype), jax.ShapeDtypeStruct((8, 128), F32)),
        in_specs=(HBM_SPEC, HBM_SPEC, ANY),
        out_specs=(SEM_SPEC, SEM_SPEC, HBM_SPEC, HBM_SPEC, pl.BlockSpec(memory_space=pltpu.VMEM)),
        input_output_aliases={0: 2, 1: 3},
        compiler_params=pltpu.CompilerParams(has_side_effects=SPLIT_EFFECT),
    )(pltpu.with_memory_space_constraint(src, pltpu.HBM),
      pltpu.with_memory_space_constraint(lax.empty(land_shape, src.dtype), pltpu.HBM), after)


def _ici_wait(started, after, name, gather):
    send_sems, recv_sems, src_thru, land_thru, _ = started

    def body(src_ref, land_ref, send_sems, recv_sems, after_ref, src_dead, got_ref):
        x, y, c = _coords()
        for j, (px, py) in enumerate(_other_chips(x, y)):
            s_ref, d_ref = _ici_refs(gather, src_ref, land_ref, j, px, py, (px, py), c)
            copy = pltpu.make_async_remote_copy(
                src_ref=s_ref, dst_ref=d_ref, send_sem=send_sems.at[j], recv_sem=recv_sems.at[j],
                device_id=(px, py, c), device_id_type=MESH_ID)
            copy.wait_send()
            copy.wait_recv()

    return pl.pallas_call(
        body, name=name,
        out_shape=(pltpu.HBM(src_thru.shape, src_thru.dtype), pltpu.HBM(land_thru.shape, land_thru.dtype)),
        in_specs=(HBM_SPEC, HBM_SPEC, SEM_SPEC, SEM_SPEC, ANY),
        out_specs=(HBM_SPEC, HBM_SPEC), input_output_aliases={0: 0, 1: 1},
        compiler_params=pltpu.CompilerParams(has_side_effects=SPLIT_EFFECT),
    )(src_thru, land_thru, send_sems, recv_sems, after)


def _peer(k, x, y, c):
    return x ^ ((k >> 2) & 1), y ^ ((k >> 1) & 1), c ^ (k & 1)


def _peers_start(src, after, name):
    r, w = src.shape
    x, y, c = _coords()
    land = lax.dynamic_update_slice(jnp.zeros((N_DEV, r, w), src.dtype), src[None], (4 * x + 2 * y + c, 0, 0))

    def body(src_ref, land_ref, after_ref, send_sems, recv_sems, src_thru, land_thru, token):
        x, y, c = _coords()
        for k in range(1, N_DEV):
            pltpu.make_async_remote_copy(
                src_ref=src_ref, dst_ref=land_ref.at[4 * x + 2 * y + c],
                send_sem=send_sems.at[k - 1], recv_sem=recv_sems.at[k - 1],
                device_id=_peer(k, x, y, c), device_id_type=MESH_ID).start()
        token[...] = jnp.zeros_like(token)

    return pl.pallas_call(
        body, name=name,
        out_shape=(pltpu.SemaphoreType.DMA((N_DEV - 1,)), pltpu.SemaphoreType.DMA((N_DEV - 1,)),
                   pltpu.HBM(src.shape, src.dtype), pltpu.HBM(land.shape, src.dtype),
                   jax.ShapeDtypeStruct((8, 128), F32)),
        in_specs=(HBM_SPEC, HBM_SPEC, ANY),
        out_specs=(SEM_SPEC, SEM_SPEC, HBM_SPEC, HBM_SPEC, pl.BlockSpec(memory_space=pltpu.VMEM)),
        input_output_aliases={0: 2, 1: 3},
        compiler_params=pltpu.CompilerParams(has_side_effects=SPLIT_EFFECT),
    )(pltpu.with_memory_space_constraint(src, pltpu.HBM), pltpu.with_memory_space_constraint(land, pltpu.HBM), after)


def _peers_wait(started, after, name):
    send_sems, recv_sems, src_thru, land_thru, _ = started

    def body(src_ref, land_ref, send_sems, recv_sems, after_ref, src_dead, got_ref):
        x, y, c = _coords()
        for k in range(1, N_DEV):
            px, py, pc = _peer(k, x, y, c)
            copy = pltpu.make_async_remote_copy(
                src_ref=src_ref, dst_ref=land_ref.at[4 * px + 2 * py + pc],
                send_sem=send_sems.at[k - 1], recv_sem=recv_sems.at[k - 1],
                device_id=(px, py, pc), device_id_type=MESH_ID)
            copy.wait_send()
            copy.wait_recv()

    return pl.pallas_call(
        body, name=name,
        out_shape=(pltpu.HBM(src_thru.shape, src_thru.dtype), pltpu.HBM(land_thru.shape, land_thru.dtype)),
        in_specs=(HBM_SPEC, HBM_SPEC, SEM_SPEC, SEM_SPEC, ANY),
        out_specs=(HBM_SPEC, HBM_SPEC), input_output_aliases={0: 0, 1: 1},
        compiler_params=pltpu.CompilerParams(has_side_effects=SPLIT_EFFECT),
    )(src_thru, land_thru, send_sems, recv_sems, after)


def _share_refs(ref, k, x, y, c, sender_c):
    px, py = ([(x, y)] + _other_chips(x, y))[k]
    return ref.at[:, 4 * px + 2 * py + sender_c]


def _share_start(gathered, after, name):
    def body(g_ref, after_ref, send_sems, recv_sems, g_thru, token):
        x, y, c = _coords()
        for k in range(4):
            slot = _share_refs(g_ref, k, x, y, c, c)
            pltpu.make_async_remote_copy(
                src_ref=slot, dst_ref=slot, send_sem=send_sems.at[k], recv_sem=recv_sems.at[k],
                device_id=(x, y, 1 - c), device_id_type=MESH_ID).start()
        token[...] = jnp.zeros_like(token)

    return pl.pallas_call(
        body, name=name,
        out_shape=(pltpu.SemaphoreType.DMA((4,)), pltpu.SemaphoreType.DMA((4,)),
                   pltpu.HBM(gathered.shape, gathered.dtype), jax.ShapeDtypeStruct((8, 128), F32)),
        in_specs=(HBM_SPEC, ANY),
        out_specs=(SEM_SPEC, SEM_SPEC, HBM_SPEC, pl.BlockSpec(memory_space=pltpu.VMEM)),
        input_output_aliases={0: 2},
        compiler_params=pltpu.CompilerParams(has_side_effects=SPLIT_EFFECT),
    )(pltpu.with_memory_space_constraint(gathered, pltpu.HBM), after)


def _share_wait(started, after, name):
    send_sems, recv_sems, g_thru, _ = started

    def body(g_ref, send_sems, recv_sems, after_ref, got_ref):
        x, y, c = _coords()
        for k in range(4):
            copy = pltpu.make_async_remote_copy(
                src_ref=_share_refs(g_ref, k, x, y, c, c), dst_ref=_share_refs(g_ref, k, x, y, c, 1 - c),
                send_sem=send_sems.at[k], recv_sem=recv_sems.at[k],
                device_id=(x, y, 1 - c), device_id_type=MESH_ID)
            copy.wait_send()
            copy.wait_recv()

    return pl.pallas_call(
        body, name=name,
        out_shape=pltpu.HBM(g_thru.shape, g_thru.dtype),
        in_specs=(HBM_SPEC, SEM_SPEC, SEM_SPEC, ANY),
        out_specs=HBM_SPEC, input_output_aliases={0: 0},
        compiler_params=pltpu.CompilerParams(has_side_effects=SPLIT_EFFECT),
    )(g_thru, send_sems, recv_sems, after)


def _core_share(own, gathered, name):
    def body(own_ref, gin_ref, out_ref, stage, send_sems, recv_sems, local_sem):
        x, y, c = _coords()
        sibling = (x, y, 1 - c)
        chips = [(x, y)] + _other_chips(x, y)
        stage_in = pltpu.make_async_copy(own_ref, stage, local_sem)
        stage_in.start()
        sent, arriving = [], []
        for k, (px, py) in enumerate(chips):
            slot = out_ref.at[:, 4 * px + 2 * py + c]
            sent.append(pltpu.make_async_remote_copy(
                src_ref=own_ref if k == 0 else slot, dst_ref=slot,
                send_sem=send_sems.at[k], recv_sem=recv_sems.at[k], device_id=sibling, device_id_type=MESH_ID))
            arriving.append(pltpu.make_async_remote_copy(
                src_ref=own_ref, dst_ref=out_ref.at[:, 4 * px + 2 * py + (1 - c)],
                send_sem=send_sems.at[k], recv_sem=recv_sems.at[k], device_id=sibling, device_id_type=MESH_ID))
        for cp in sent:
            cp.start()
        stage_in.wait()
        stage_out = pltpu.make_async_copy(stage, out_ref.at[:, 4 * x + 2 * y + c], local_sem)
        stage_out.start()
        for cp in arriving:
            cp.wait_recv()
        for cp in sent:
            cp.wait_send()
        stage_out.wait()

    return pl.pallas_call(
        body, name=name,
        out_shape=jax.ShapeDtypeStruct(gathered.shape, own.dtype),
        in_specs=[ANY, ANY], out_specs=ANY, input_output_aliases={1: 0},
        scratch_shapes=[pltpu.VMEM(own.shape, own.dtype), pltpu.SemaphoreType.DMA((4,)),
                        pltpu.SemaphoreType.DMA((4,)), pltpu.SemaphoreType.DMA],
    )(own, gathered)


def _adamw(w, g, m, v):
    m = ADAM_B1 * m + (1.0 - ADAM_B1) * g
    v = ADAM_B2 * v + (1.0 - ADAM_B2) * (g * g)
    m_hat = m / ADAM_C1
    v_hat = v / ADAM_C2
    delta = -ADAM_LR * (m_hat / (jnp.sqrt(v_hat) + ADAM_EPS) + ADAM_WD * w)
    return delta, m, v


def _adam_big(part, land, w, m, v, chip_idx, tag, token):
    r, wd = w.shape
    tr, tw = _row_tile(r, 1024), 256

    def body(s_ref, tok_ref, p_ref, l_ref, w_ref, m_ref, v_ref, g_ref, d_ref, mo_ref, vo_ref):
        g = p_ref[0].astype(F32)
        for j in range(3):
            g = g + l_ref[j].astype(F32)
        delta, mn, vn = _adamw(w_ref[...], g, m_ref[...], v_ref[...])
        g_ref[...] = g
        d_ref[...] = delta
        mo_ref[...] = mn
        vo_ref[...] = vn

    row = pl.BlockSpec((tr, tw), lambda i, j, s: (i, j))
    return pl.pallas_call(
        body, name="adam_big_" + tag,
        grid_spec=pltpu.PrefetchScalarGridSpec(
            num_scalar_prefetch=1, grid=(r // tr, wd // tw),
            in_specs=[pl.BlockSpec((8, 128), lambda i, j, s: (0, 0)),
                      pl.BlockSpec((1, tr, tw), lambda i, j, s: (s[0], i, j)),
                      pl.BlockSpec((3, tr, tw), lambda i, j, s: (0, i, j)), row, row, row],
            out_specs=[row, row, row, row]),
        out_shape=[jax.ShapeDtypeStruct((r, wd), F32)] * 4,
        compiler_params=_cparams(),
    )(chip_idx, token, part, land, w, m, v)


def _adam_small(parts, w, m, v):
    _, r, wd = parts.shape

    def body(p_ref, w_ref, m_ref, v_ref, g_ref, d_ref, mo_ref, vo_ref):
        g = p_ref[0]
        for k in range(1, N_DEV):
            g = g + p_ref[k]
        delta, mn, vn = _adamw(w_ref[...], g, m_ref[...], v_ref[...])
        g_ref[...] = g
        d_ref[...] = delta
        mo_ref[...] = mn
        vo_ref[...] = vn

    return pl.pallas_call(
        body, name="adam_small",
        out_shape=[jax.ShapeDtypeStruct((r, wd), F32)] * 4,
        compiler_params=_cparams(),
    )(parts, w, m, v)


def _pad_rows(a, rows):
    return jnp.pad(a, ((0, rows - a.shape[0]), (0, 0)))


def _pad_w_in(w):
    cut = Q_RANK + KV_RANK + ROPE
    return jnp.concatenate([w[:, :cut], jnp.zeros((w.shape[0], 64), w.dtype), w[:, cut:]], axis=1)


def _unpad_w_in(w):
    cut = Q_RANK + KV_RANK + ROPE
    return jnp.concatenate([w[:, :cut], w[:, cut + 64:]], axis=1)


def _pack_mid(p):
    parts = [_pad_w_in(p["w_in"][0]), p["w_out"][0], p["w_mq"][0], p["w_mo"][0],
             p["w_mkv"][0].reshape(256, D_MODEL),
             _pad_rows(p["w_q_up"][0].T.reshape(24, D_MODEL), 32),
             p["w_kv_up"][0].reshape(16, D_MODEL)]
    return jnp.concatenate(parts, axis=0)


def _pack_segments(p, group):
    if group == "mid":
        return _pack_mid(p)[None]
    return jnp.stack([p[group + "_w_gate"][0].T, p[group + "_w_up"][0].T, p[group + "_w_down"][0]])


UNIT_WEIGHT = {"ffn1_g": ("ffn1_w_gate", True), "ffn1_u": ("ffn1_w_up", True), "ffn1_d": ("ffn1_w_down", False),
               "ffn2_g": ("ffn2_w_gate", True), "ffn2_u": ("ffn2_w_up", True), "ffn2_d": ("ffn2_w_down", False)}


def _pack_unit(p, unit):
    if unit == "mid":
        return _pack_mid(p)
    name, transposed = UNIT_WEIGHT[unit]
    return p[name][0].T if transposed else p[name][0]


def _unpack_unit(a, unit):
    if unit != "mid":
        name, transposed = UNIT_WEIGHT[unit]
        return {name: (a.T if transposed else a)[None]}
    seg = lambda n: a[SEG_OFF[n][0]:SEG_OFF[n][0] + SEG_OFF[n][1]]
    return {"w_in": _unpad_w_in(seg("w_in"))[None], "w_out": seg("w_out")[None], "w_mq": seg("w_mq")[None],
            "w_mo": seg("w_mo")[None], "w_mkv": seg("w_mkv").reshape(D_MODEL, 256)[None],
            "w_q_up": seg("w_q")[:24].reshape(96, Q_RANK).T[None],
            "w_kv_up": seg("w_kv").reshape(KV_RANK, 128)[None]}


def _unpack_gathered(full, group):
    if group != "mid":
        return {n: full[k].reshape(-1, D_MODEL) for k, (n, _) in enumerate(GROUP_SEGS[group])}
    full = full[0]
    seg = lambda n: full[:, SEG_OFF[n][0]:SEG_OFF[n][0] + SEG_OFF[n][1]]
    rows = lambda n: seg(n).reshape(-1, D_MODEL)
    wq_t = seg("w_q")[:, :24].reshape(MLA_HEADS, NOPE + ROPE, Q_RANK)
    wq_t = jnp.pad(wq_t, ((0, 0), (0, HEAD_PAD - NOPE - ROPE), (0, 0))).reshape(MLA_HEADS * HEAD_PAD, Q_RANK)
    wkv = seg("w_kv").reshape(N_DEV, KV_RANK, 128).transpose(1, 0, 2).reshape(KV_RANK, N_DEV * 128)
    return {"w_in": rows("w_in"), "w_out": rows("w_out"), "w_mq": rows("w_mq"), "w_mo": rows("w_mo"),
            "w_mkv": seg("w_mkv").reshape(N_DEV, D_MODEL, 256), "w_q": wq_t, "w_kv": wkv}


def _pack_grads(gr):
    blk = lambda a: a.reshape(N_DEV, -1, D_MODEL)
    dwq = gr["w_q"].reshape(MLA_HEADS, HEAD_PAD, Q_RANK)[:, :NOPE + ROPE].reshape(N_DEV, 24, D_MODEL)
    dwq = jnp.pad(dwq, ((0, 0), (0, 8), (0, 0)))
    dwkv = gr["w_kv"].reshape(KV_RANK, N_DEV, 128).transpose(1, 0, 2).reshape(N_DEV, 16, D_MODEL)
    parts = [blk(gr["w_in"]), blk(gr["w_out"]), blk(gr["w_mq"]), blk(gr["w_mo"]),
             gr["w_mkv"].reshape(N_DEV, 256, D_MODEL), dwq, dwkv]
    return jnp.concatenate([a.astype(BF16) for a in parts], axis=1)


def _pack_small(vals):
    parts = []
    for n, r in SMALL_ROWS:
        parts.append(_pad_rows(vals[n].reshape(-1, 128), r) if n in vals else jnp.zeros((r, 128), F32))
    return jnp.concatenate(parts, axis=0)


def _unpack_small(a, shapes):
    out = {}
    for n, shape in shapes.items():
        o = SMALL_OFF[n][0]
        out[n] = a[o:o + int(np.prod(shape)) // 128].reshape(shape)
    return out


BIG_NAMES = ("ffn1_w_gate", "ffn1_w_up", "ffn1_w_down", "w_in", "w_q_up", "w_kv_up", "w_out", "w_mq", "w_mkv",
             "w_mo", "ffn2_w_gate", "ffn2_w_up", "ffn2_w_down")
SMALL_NAMES = ("ffn1_norm", "mix_norm", "q_norm", "kv_norm", "pool_w", "pool_scale", "xattn_norm", "mem_norm",
               "ffn2_norm", "final_norm")
WEIGHT_ORDER = ("ffn1_norm", "ffn1_w_gate", "ffn1_w_up", "ffn1_w_down", "mix_norm", "w_in", "q_norm", "w_q_up",
                "kv_norm", "w_kv_up", "pool_w", "pool_scale", "w_out", "xattn_norm", "mem_norm", "w_mq", "w_mkv",
                "w_mo", "ffn2_norm", "ffn2_w_gate", "ffn2_w_up", "ffn2_w_down", "final_norm")


def _rope_table():
    lane = np.arange(128)
    freqs = (1.0 / (ROPE_BASE ** (np.arange(0, ROPE, 2, dtype=np.float32) / ROPE))).astype(np.float32)
    tab = np.zeros((8, 128), np.float32)
    tab[0] = np.where(lane < ROPE, freqs[lane % (ROPE // 2)], 0.0)
    tab[1] = np.where(lane < ROPE // 2, -1.0, np.where(lane < ROPE, 1.0, 0.0))
    return jnp.asarray(tab)


def kernel(x, mem, positions, ffn1_norm, ffn1_w_gate, ffn1_w_up, ffn1_w_down, mix_norm, w_in, q_norm, w_q_up, kv_norm, w_kv_up, pool_w, pool_scale, w_out, xattn_norm, mem_norm, w_mq, w_mkv, w_mo, ffn2_norm, ffn2_w_gate, ffn2_w_up, ffn2_w_down, final_norm, loss_target, m_ffn1_norm, m_ffn1_w_gate, m_ffn1_w_up, m_ffn1_w_down, m_mix_norm, m_w_in, m_q_norm, m_w_q_up, m_kv_norm, m_w_kv_up, m_pool_w, m_pool_scale, m_w_out, m_xattn_norm, m_mem_norm, m_w_mq, m_w_mkv, m_w_mo, m_ffn2_norm, m_ffn2_w_gate, m_ffn2_w_up, m_ffn2_w_down, m_final_norm, v_ffn1_norm, v_ffn1_w_gate, v_ffn1_w_up, v_ffn1_w_down, v_mix_norm, v_w_in, v_q_norm, v_w_q_up, v_kv_norm, v_w_kv_up, v_pool_w, v_pool_scale, v_w_out, v_xattn_norm, v_mem_norm, v_w_mq, v_w_mkv, v_w_mo, v_ffn2_norm, v_ffn2_w_gate, v_ffn2_w_up, v_ffn2_w_down, v_final_norm):
    wts = dict(ffn1_norm=ffn1_norm, ffn1_w_gate=ffn1_w_gate, ffn1_w_up=ffn1_w_up, ffn1_w_down=ffn1_w_down,
               mix_norm=mix_norm, w_in=w_in, q_norm=q_norm, w_q_up=w_q_up, kv_norm=kv_norm, w_kv_up=w_kv_up,
               pool_w=pool_w, pool_scale=pool_scale, w_out=w_out, xattn_norm=xattn_norm, mem_norm=mem_norm,
               w_mq=w_mq, w_mkv=w_mkv, w_mo=w_mo, ffn2_norm=ffn2_norm, ffn2_w_gate=ffn2_w_gate,
               ffn2_w_up=ffn2_w_up, ffn2_w_down=ffn2_w_down, final_norm=final_norm)
    mom = dict(ffn1_norm=m_ffn1_norm, ffn1_w_gate=m_ffn1_w_gate, ffn1_w_up=m_ffn1_w_up, ffn1_w_down=m_ffn1_w_down,
               mix_norm=m_mix_norm, w_in=m_w_in, q_norm=m_q_norm, w_q_up=m_w_q_up, kv_norm=m_kv_norm,
               w_kv_up=m_w_kv_up, pool_w=m_pool_w, pool_scale=m_pool_scale, w_out=m_w_out, xattn_norm=m_xattn_norm,
               mem_norm=m_mem_norm, w_mq=m_w_mq, w_mkv=m_w_mkv, w_mo=m_w_mo, ffn2_norm=m_ffn2_norm,
               ffn2_w_gate=m_ffn2_w_gate, ffn2_w_up=m_ffn2_w_up, ffn2_w_down=m_ffn2_w_down, final_norm=m_final_norm)
    var = dict(ffn1_norm=v_ffn1_norm, ffn1_w_gate=v_ffn1_w_gate, ffn1_w_up=v_ffn1_w_up, ffn1_w_down=v_ffn1_w_down,
               mix_norm=v_mix_norm, w_in=v_w_in, q_norm=v_q_norm, w_q_up=v_w_q_up, kv_norm=v_kv_norm,
               w_kv_up=v_w_kv_up, pool_w=v_pool_w, pool_scale=v_pool_scale, w_out=v_w_out, xattn_norm=v_xattn_norm,
               mem_norm=v_mem_norm, w_mq=v_w_mq, w_mkv=v_w_mkv, w_mo=v_w_mo, ffn2_norm=v_ffn2_norm,
               ffn2_w_gate=v_ffn2_w_gate, ffn2_w_up=v_ffn2_w_up, ffn2_w_down=v_ffn2_w_down, final_norm=v_final_norm)

    t = x.shape[1]
    xs = x[0]
    mems = mem[0]
    target = loss_target[0]
    pos = positions.reshape(t, 1)
    row = lambda a: a.reshape(1, -1)
    rope_tab = _rope_table()

    cx, cy, cc = _coords()
    chip_idx = (2 * cx + cy).astype(jnp.int32).reshape(1)

    wb = {}
    for grp in ("ffn1", "mid", "ffn2"):
        wb[grp] = _pack_segments(wts, grp).astype(BF16)
        if grp == "ffn1":
            ag_ffn1 = _ici_start(wb["ffn1"], pos, "ag_ffn1_start", True)
    own_ffn1, land_ffn1 = _ici_wait(ag_ffn1, wb["ffn2"], "ag_ffn1_wait", True)
    full_ffn1 = _core_share(own_ffn1, land_ffn1, "ag_ffn1_share")
    fw = _unpack_gathered(full_ffn1, "ffn1")
    ag_mid = _ici_start(wb["mid"], full_ffn1, "ag_mid_start", True)
    g_ffn1, g_mix, g_q, g_kv = row(ffn1_norm), row(mix_norm), row(q_norm), row(kv_norm)
    g_x, g_mem, g_ffn2, g_fin = row(xattn_norm), row(mem_norm), row(ffn2_norm), row(final_norm)
    pool_wb = pool_w[0].astype(BF16)
    pool_sc = row(pool_scale)

    h1, n1, gate1, up1 = _ffn_fwd(xs, g_ffn1, fw["ffn1_g"], fw["ffn1_u"], fw["ffn1_d"], "ffn1_fwd", token=ag_mid[4])
    own_mid, land_mid = _ici_wait(ag_mid, h1, "ag_mid_wait", True)
    full_mid = _core_share(own_mid, land_mid, "ag_mid_share")
    fw.update(_unpack_gathered(full_mid, "mid"))
    ag_ffn2 = _ici_start(wb["ffn2"], full_mid, "ag_ffn2_start", True)
    u, z, qn, kvn, qh, kh, vh = _mix_prep(h1, g_mix, fw["w_in"], g_q, fw["w_q"], g_kv, fw["w_kv"], pos, rope_tab,
                                          token=ag_ffn2[4])
    a, lse = _attn_fwd(qh, kh, vh)
    p = _pool_fwd(z, pool_wb, pool_sc)
    memn, km, vm = _mem_kv(mems, g_mem, fw["w_mkv"])
    own_ffn2, land_ffn2 = _ici_wait(ag_ffn2, a, "ag_ffn2_wait", True)
    land_ffn2 = lax.dynamic_update_slice(land_ffn2, own_ffn2[:, None], (0, 4 * cx + 2 * cy + cc, 0, 0))
    share_ffn2 = _share_start(land_ffn2, a, "ag_ffn2_share_start")
    h2, h3, hn, qm, om = _xattn_fwd(h1, a, p, fw["w_out"], g_x, fw["w_mq"], km, vm, fw["w_mo"], token=share_ffn2[3])
    fw.update(_unpack_gathered(_share_wait(share_ffn2, h3, "ag_ffn2_share_wait"), "ffn2"))
    dh4, n2, gate2, up2, loss_part, dg_fin = _ffn_fwd(h3, g_ffn2, fw["ffn2_g"], fw["ffn2_u"], fw["ffn2_d"],
                                                      "ffn2_fwd", head=(target, g_fin))

    def reduce_start(g8, unit):
        part = _core_reduce(g8, unit)
        return _ici_start(part, g8, "rs_" + unit + "_start", False)

    def by_device(g):
        return g.reshape(N_DEV, -1, D_MODEL)

    rs = {}
    dh3, dgate2, dup2, act2, dg_ffn2 = _ffn_bwd_data(dh4, h3, g_ffn2, gate2, up2, fw["ffn2_g"], fw["ffn2_u"],
                                                     fw["ffn2_d"], "ffn2_bwd")
    rs["ffn2_g"] = reduce_start(by_device(_tn_matmul(dgate2, n2, "ffn2_dwg", tmm=1408, out_dtype=BF16)), "ffn2_g")
    rs["ffn2_u"] = reduce_start(by_device(_tn_matmul(dup2, n2, "ffn2_dwu", tmm=1408, out_dtype=BF16,
                                                     token=rs["ffn2_g"][4])), "ffn2_u")
    rs["ffn2_d"] = reduce_start(by_device(_tn_matmul(act2, dh4, "ffn2_dwd", scale=0.5, tmm=1408, out_dtype=BF16,
                                                     token=rs["ffn2_u"][4])), "ffn2_d")
    dh2, dqm, da, dp, dkm, dvm, dg_x = _xattn_bwd(dh3, h2, qm, g_x, fw["w_mq"], km, vm, fw["w_mo"], fw["w_out"],
                                                  token=rs["ffn2_d"][4])
    gr = {}
    gr["w_mo"] = _tn_matmul(om, dh3, "dw_mo", out_dtype=BF16)
    gr["w_mq"] = _tn_matmul(hn, dqm, "dw_mq", out_dtype=BF16)
    gr["w_out"] = jnp.concatenate([_tn_matmul(a, dh2, "dw_out_a", out_dtype=BF16),
                                   _tn_matmul(p, dh2, "dw_out_p", out_dtype=BF16)], axis=0)
    gr["w_mkv"], dg_mem = _mem_kv_bwd(dkm, dvm, memn, mems, g_mem, fw["w_mkv"])
    dz_pool, d_pool_w, d_pool_sc = _pool_bwd(dp, z, pool_wb, pool_sc)
    dqh, dkh, dvh = _attn_bwd(qh, kh, vh, da, lse, _attn_delta(a, da))
    dh1, dq, dkv, dz, dg_q, dg_kv, dg_mix = _mla_bwd(dqh, dkh, dvh, z, dz_pool, h1, dh2, g_mix, fw["w_in"], g_q,
                                                     fw["w_q"], g_kv, fw["w_kv"], pos, rope_tab)
    gr["w_q"] = _tn_matmul(dq, qn, "dw_q", out_dtype=BF16)
    gr["w_kv"] = _tn_matmul(kvn, dkv, "dw_kv", out_dtype=BF16)
    gr["w_in"] = _tn_matmul(u, dz, "dw_in", out_dtype=BF16)
    g_mid = _pack_grads(gr)
    part_mid = _core_reduce(g_mid, "mid")
    got = {}
    after = part_mid
    for unit in ("ffn2_g", "ffn2_u", "ffn2_d"):
        got[unit] = _ici_wait(rs[unit], after, "rs_" + unit + "_wait", False)
        after = got[unit][1]
    rs["mid"] = _ici_start(part_mid, after, "rs_mid_start", False)
    dx, dgate1, dup1, act1, dg_ffn1 = _ffn_bwd_data(dh1, xs, g_ffn1, gate1, up1, fw["ffn1_g"], fw["ffn1_u"],
                                                    fw["ffn1_d"], "ffn1_bwd", token=rs["mid"][4])
    got["mid"] = _ici_wait(rs["mid"], dx, "rs_mid_wait", False)

    small_g = dict(ffn1_norm=dg_ffn1, mix_norm=dg_mix, q_norm=dg_q, kv_norm=dg_kv, pool_w=d_pool_w,
                   pool_scale=d_pool_sc, xattn_norm=dg_x, mem_norm=dg_mem, ffn2_norm=dg_ffn2, final_norm=dg_fin,
                   loss=loss_part)
    small_ag = _peers_start(_pack_small(small_g), got["mid"][1], "small_ag_start")
    rs["ffn1_g"] = reduce_start(by_device(_tn_matmul(dgate1, n1, "ffn1_dwg", tmm=1408, out_dtype=BF16,
                                                     token=small_ag[4])), "ffn1_g")
    _, parts = _peers_wait(small_ag, rs["ffn1_g"][4], "small_ag_wait")
    small = _adam_small(parts, _pack_small({n: wts[n] for n in SMALL_NAMES}),
                        _pack_small({n: mom[n] for n in SMALL_NAMES}), _pack_small({n: var[n] for n in SMALL_NAMES}))
    small_sum = small[0]
    loss = small_sum[SMALL_OFF["loss"][0], 0]
    shapes = {n: wts[n].shape for n in SMALL_NAMES}
    small = [_unpack_small(s, shapes) for s in small]

    rs["ffn1_u"] = reduce_start(by_device(_tn_matmul(dup1, n1, "ffn1_dwu", tmm=1408, out_dtype=BF16,
                                                     token=small_sum)), "ffn1_u")
    rs["ffn1_d"] = reduce_start(by_device(_tn_matmul(act1, dh1, "ffn1_dwd", scale=0.5, tmm=1408, out_dtype=BF16,
                                                     token=rs["ffn1_u"][4])), "ffn1_d")

    big = {}

    def adam_unit(unit, token):
        part, land = got[unit]
        res = _adam_big(part, land, _pack_unit(wts, unit), _pack_unit(mom, unit), _pack_unit(var, unit),
                        chip_idx, unit, token)
        for k, packed in enumerate(res):
            big.setdefault(k, {}).update(_unpack_unit(packed, unit))
        return res[0]

    done = rs["ffn1_d"][4]
    for unit in ("mid", "ffn2_g", "ffn2_u", "ffn2_d"):
        done = adam_unit(unit, done)
    for unit in ("ffn1_g", "ffn1_u", "ffn1_d"):
        got[unit] = _ici_wait(rs[unit], done, "rs_" + unit + "_wait", False)
        done = adam_unit(unit, done)

    outs = [loss, dx[None]]
    for k in range(4):
        for n in WEIGHT_ORDER:
            outs.append(big[k][n] if n in BIG_NAMES else small[k][n])
    return tuple(outs)
```

```python
import numpy as np

import jax
import jax.numpy as jnp
from jax import lax
from jax.experimental import pallas as pl
from jax.experimental.pallas import tpu as pltpu

F32 = jnp.float32
BF16 = jnp.bfloat16

N_DEV = 8
D_MODEL = 1024
D_FF = 2816
MLA_HEADS = 4
NOPE = 128
ROPE = 64
HEAD_PAD = 256
V_DIM = 128
Q_RANK = 256
KV_RANK = 128
POOL_WINDOWS = (2, 4, 8, 16)
POOL_CH = 128
POOL_HALO = 16
N_MEM = 256
MEM_HEADS = 4
MEM_HD = 256
ROPE_BASE = 10000.0
RMS_EPS = 1e-6
ATTN_SCALE = (NOPE + ROPE) ** -0.5
MEM_SCALE = MEM_HD ** -0.5
NEG_BIG = -1e30

ADAM_LR = 0.001
ADAM_B1 = 0.9
ADAM_B2 = 0.999
ADAM_EPS = 1e-08
ADAM_WD = 0.01
ADAM_STEP = 10
ADAM_C1 = 1.0 - ADAM_B1 ** ADAM_STEP
ADAM_C2 = 1.0 - ADAM_B2 ** ADAM_STEP

VMEM_LIMIT_BYTES = 56 * 1024 * 1024
BF16_ROWS = 16

GROUP_SEGS = {
    "ffn1": (("ffn1_g", 352), ("ffn1_u", 352), ("ffn1_d", 352)),
    "mid": (("w_in", 128), ("w_out", 128), ("w_mq", 128), ("w_mo", 128), ("w_mkv", 256), ("w_q", 32), ("w_kv", 16)),
    "ffn2": (("ffn2_g", 352), ("ffn2_u", 352), ("ffn2_d", 352)),
}
SEG_OFF = {}
GROUP_ROWS = {}
for _g, _segs in GROUP_SEGS.items():
    _o = 0
    for _n, _r in _segs:
        SEG_OFF[_n] = (_o, _r)
        _o += _r
    GROUP_ROWS[_g] = _o

SMALL_ROWS = (("ffn1_norm", 8), ("mix_norm", 8), ("q_norm", 8), ("kv_norm", 8), ("pool_w", 512), ("pool_scale", 8),
              ("xattn_norm", 8), ("mem_norm", 8), ("ffn2_norm", 8), ("final_norm", 8), ("loss", 8))
SMALL_OFF = {}
_o = 0
for _n, _r in SMALL_ROWS:
    SMALL_OFF[_n] = (_o, _r)
    _o += _r


def _cparams(**kw):
    return pltpu.CompilerParams(vmem_limit_bytes=VMEM_LIMIT_BYTES, **kw)


def _row_tile(rows, limit):
    best = None
    for cand in range(BF16_ROWS, min(rows, limit) + 1, BF16_ROWS):
        if rows % cand == 0:
            best = cand
    assert best is not None, rows
    return best


def _dot_nn(a, b):
    return lax.dot_general(a, b, (((1,), (0,)), ((), ())), preferred_element_type=F32)


def _dot_nt(a, b):
    return lax.dot_general(a, b, (((1,), (1,)), ((), ())), preferred_element_type=F32)


def _dot_tn(a, b):
    return lax.dot_general(a, b, (((0,), (0,)), ((), ())), preferred_element_type=F32)


def _rms_fwd(x, g):
    r = lax.rsqrt(jnp.mean(x * x, axis=-1, keepdims=True) + RMS_EPS)
    return x * r * g, r


def _rms_bwd(dy, x, g, r):
    xhat = x * r
    dyg = dy * g
    dx = r * (dyg - xhat * jnp.mean(dyg * xhat, axis=-1, keepdims=True))
    dg = jnp.sum(dy * xhat, axis=0, keepdims=True)
    return dx, dg


def _accumulate(ref, val, first):
    if isinstance(first, bool):
        if first:
            ref[...] = val
        else:
            ref[...] += val
        return

    @pl.when(first)
    def _():
        ref[...] = val

    @pl.when(jnp.logical_not(first))
    def _():
        ref[...] += val


def _call_after(token, body, in_specs, args, **kw):
    if token is not None:
        inner = body
        body = lambda tok_ref, *refs: inner(*refs)
        in_specs = [pl.BlockSpec((8, 128), lambda *_: (0, 0))] + list(in_specs)
        args = (token,) + tuple(args)
    return pl.pallas_call(body, in_specs=in_specs, **kw)(*args)


def _resident(shape):
    return pl.BlockSpec(shape, lambda *_: (0,) * len(shape), pipeline_mode=pl.Buffered(1))


def _rope_tables(pos_col, tab):
    ang = pos_col.astype(F32) * tab[0:1, :]
    return jnp.cos(ang), jnp.sin(ang) * tab[1:2, :]


def _swap_halves(x):
    lane = lax.broadcasted_iota(jnp.int32, x.shape, 1)
    return jnp.where((lane % 64) < 32, pltpu.roll(x, 96, 1), pltpu.roll(x, 32, 1))


def _rope_apply(x, cos_t, sin_t):
    return x * cos_t + _swap_halves(x) * sin_t


def _rope_apply_t(dy, cos_t, sin_t):
    return dy * cos_t + _swap_halves(dy * sin_t)


def _ffn_fwd(h, g, wg_t, wu_t, wd, name, token=None, head=None):
    t, d = h.shape
    f = wg_t.shape[0]
    tm, tf = min(512, t), 256
    nf = f // tf
    n_in = 5 if head is None else 7

    def body(*refs):
        h_ref, g_ref, wg_ref, wu_ref, wd_ref = refs[:5]
        ho_ref, n_ref, gate_ref, up_ref = refs[n_in:n_in + 4]
        nb_sc, acc_sc = refs[-2:]
        y, _ = _rms_fwd(h_ref[...], g_ref[...])
        nb = y.astype(BF16)
        nb_sc[...] = nb
        n_ref[...] = nb
        acc_sc[...] = jnp.zeros_like(acc_sc)

        def f_tile(j):
            rows = pl.ds(pl.multiple_of(j * tf, tf), tf)
            nb = nb_sc[...]
            gt = _dot_nt(nb, wg_ref[rows, :])
            ut = _dot_nt(nb, wu_ref[rows, :])
            gate_ref[j] = gt.astype(BF16)
            up_ref[j] = ut.astype(BF16)
            act = (gt * jax.nn.sigmoid(gt)) * ut
            return _dot_nn(act.astype(BF16), wd_ref[rows, :])

        def pair(p, carry):
            acc_sc[...] += f_tile(2 * p) + f_tile(2 * p + 1)
            return carry

        lax.fori_loop(0, nf // 2, pair, 0)
        if nf % 2:
            acc_sc[...] += f_tile(nf - 1)
        ho = h_ref[...] + 0.5 * acc_sc[...]
        if head is None:
            ho_ref[...] = ho
            return
        t_ref, gf_ref = refs[5:7]
        loss_ref, dgf_ref = refs[n_in + 4:n_in + 6]
        gg = gf_ref[...]
        y, r = _rms_fwd(ho, gg)
        err = y - t_ref[...]
        part = 0.5 * jnp.sum(jnp.mean(err * err, axis=-1, keepdims=True), axis=0, keepdims=True)
        dx, dg = _rms_bwd(err * (1.0 / d), ho, gg, r)
        ho_ref[...] = dx
        first = pl.program_id(0) == 0
        _accumulate(loss_ref, jnp.broadcast_to(part, loss_ref.shape), first)
        _accumulate(dgf_ref, dg, first)

    row = pl.BlockSpec((tm, d), lambda i: (i, 0))
    tiles = pl.BlockSpec((nf, tm, tf), lambda i: (0, i, 0))
    in_specs = [row, _resident((1, d)), _resident((f, d)), _resident((f, d)), _resident((f, d))]
    args = (h, g, wg_t, wu_t, wd)
    out_specs = [row, row, tiles, tiles]
    out_shape = [jax.ShapeDtypeStruct((t, d), F32), jax.ShapeDtypeStruct((t, d), BF16),
                 jax.ShapeDtypeStruct((nf, t, tf), BF16), jax.ShapeDtypeStruct((nf, t, tf), BF16)]
    if head is not None:
        in_specs += [row, _resident((1, d))]
        args += tuple(head)
        out_specs += [pl.BlockSpec((8, 128), lambda i: (0, 0)), pl.BlockSpec((1, d), lambda i: (0, 0))]
        out_shape += [jax.ShapeDtypeStruct((8, 128), F32), jax.ShapeDtypeStruct((1, d), F32)]
    return _call_after(
        token, body, in_specs, args, name=name, grid=(t // tm,), out_specs=out_specs, out_shape=out_shape,
        scratch_shapes=[pltpu.VMEM((tm, d), BF16), pltpu.VMEM((tm, d), F32)],
        compiler_params=_cparams(),
    )


def _ffn_bwd_data(dho, h, g, gate, up, wg_t, wu_t, wd, name, token=None):
    t, d = h.shape
    f = wg_t.shape[0]
    tm, tf = min(1024, t), 256
    parts = 2 if tm % 512 == 0 else 1
    tp = tm // parts
    nf = f // tf

    def body(dho_ref, h_ref, g_ref, gate_ref, up_ref, wg_ref, wu_ref, wd_ref,
             dh_ref, dgate_ref, dup_ref, act_ref, dg_ref, dhb_sc, acc_sc):
        i, j = pl.program_id(0), pl.program_id(1)

        @pl.when(j == 0)
        def _():
            dhb_sc[...] = (0.5 * dho_ref[...]).astype(BF16)
            acc_sc[...] = jnp.zeros_like(acc_sc)

        for r in range(parts):
            rows = pl.ds(r * tp, tp)
            dact = _dot_nt(dhb_sc[rows, :], wd_ref[...])
            gt = gate_ref[0, rows, :].astype(F32)
            ut = up_ref[0, rows, :].astype(F32)
            sg = jax.nn.sigmoid(gt)
            silu = gt * sg
            dgb = (dact * ut * (sg * (1.0 + gt * (1.0 - sg)))).astype(BF16)
            dub = (dact * silu).astype(BF16)
            act_ref[rows, :] = (silu * ut).astype(BF16)
            dgate_ref[rows, :] = dgb
            dup_ref[rows, :] = dub
            acc_sc[rows, :] += _dot_nn(dgb, wg_ref[...]) + _dot_nn(dub, wu_ref[...])

        @pl.when(j == nf - 1)
        def _():
            x = h_ref[...]
            gg = g_ref[...]
            _, r = _rms_fwd(x, gg)
            dx, dg = _rms_bwd(acc_sc[...], x, gg, r)
            dh_ref[...] = dho_ref[...] + dx
            _accumulate(dg_ref, dg, i == 0)

    return _call_after(
        token, body,
        [pl.BlockSpec((tm, d), lambda i, j: (i, 0)),
         pl.BlockSpec((tm, d), lambda i, j: (i, 0)),
         pl.BlockSpec((1, d), lambda i, j: (0, 0)),
         pl.BlockSpec((1, tm, tf), lambda i, j: (j, i, 0)),
         pl.BlockSpec((1, tm, tf), lambda i, j: (j, i, 0)),
         pl.BlockSpec((tf, d), lambda i, j: (j, 0)),
         pl.BlockSpec((tf, d), lambda i, j: (j, 0)),
         pl.BlockSpec((tf, d), lambda i, j: (j, 0))],
        (dho, h, g, gate, up, wg_t, wu_t, wd),
        name=name, grid=(t // tm, nf),
        out_specs=[pl.BlockSpec((tm, d), lambda i, j: (i, 0)),
                   pl.BlockSpec((tm, tf), lambda i, j: (i, j)),
                   pl.BlockSpec((tm, tf), lambda i, j: (i, j)),
                   pl.BlockSpec((tm, tf), lambda i, j: (i, j)),
                   pl.BlockSpec((1, d), lambda i, j: (0, 0))],
        out_shape=[jax.ShapeDtypeStruct((t, d), F32), jax.ShapeDtypeStruct((t, f), BF16),
                   jax.ShapeDtypeStruct((t, f), BF16), jax.ShapeDtypeStruct((t, f), BF16),
                   jax.ShapeDtypeStruct((1, d), F32)],
        scratch_shapes=[pltpu.VMEM((tm, d), BF16), pltpu.VMEM((tm, d), F32)],
        compiler_params=_cparams(),
    )


def _tn_matmul(a, b, name, scale=1.0, tmm=None, out_dtype=F32, token=None):
    t, m = a.shape
    n = b.shape[1]
    tmm = m if tmm is None else tmm
    tk = min(1024, t)
    nk = t // tk

    def product(a_ref, b_ref):
        prod = _dot_tn(a_ref[...].astype(BF16), b_ref[...].astype(BF16))
        return prod * scale if scale != 1.0 else prod

    def body_f32(a_ref, b_ref, o_ref):
        _accumulate(o_ref, product(a_ref, b_ref), pl.program_id(1) == 0)

    def body_cast(a_ref, b_ref, o_ref, acc_sc):
        k = pl.program_id(1)
        _accumulate(acc_sc, product(a_ref, b_ref), k == 0)

        @pl.when(k == nk - 1)
        def _():
            o_ref[...] = acc_sc[...].astype(out_dtype)

    direct = out_dtype == F32
    return _call_after(
        token, body_f32 if direct else body_cast,
        [pl.BlockSpec((tk, tmm), lambda i, k: (k, i)),
         pl.BlockSpec((tk, n), lambda i, k: (k, 0))],
        (a, b),
        name=name, grid=(m // tmm, nk),
        out_specs=pl.BlockSpec((tmm, n), lambda i, k: (i, 0)),
        out_shape=jax.ShapeDtypeStruct((m, n), out_dtype),
        scratch_shapes=[] if direct else [pltpu.VMEM((tmm, n), F32)],
        compiler_params=_cparams(),
    )


def _mix_prep(h1, mix_norm, w_in, q_norm, wq_t, kv_norm, wkv, pos, rope_tab, token=None):
    t, d = h1.shape
    tm = min(512, t)

    def body(h_ref, gm_ref, win_ref, gq_ref, wq_ref, gkv_ref, wkv_ref, pos_ref, tab_ref,
             u_ref, z_ref, qn_ref, kvn_ref, q_ref, k_ref, v_ref):
        u, _ = _rms_fwd(h_ref[...], gm_ref[...])
        ub = u.astype(BF16)
        u_ref[...] = ub
        z = _dot_nn(ub, win_ref[...])
        z_ref[...] = z
        cos_t, sin_t = _rope_tables(pos_ref[...], tab_ref[...])
        qn, _ = _rms_fwd(z[:, 0:Q_RANK], gq_ref[...])
        qnb = qn.astype(BF16)
        qn_ref[...] = qnb
        q = _dot_nt(qnb, wq_ref[...])
        kvn, _ = _rms_fwd(z[:, Q_RANK:Q_RANK + KV_RANK], gkv_ref[...])
        kvnb = kvn.astype(BF16)
        kvn_ref[...] = kvnb
        kv = _dot_nn(kvnb, wkv_ref[...])
        k_pe = _rope_apply(z[:, Q_RANK + KV_RANK:Q_RANK + KV_RANK + 128], cos_t, sin_t)
        ones = jnp.ones((tm, V_DIM), F32)
        for hh in range(MLA_HEADS):
            b = hh * HEAD_PAD
            q_pe = _rope_apply(q[:, b + NOPE:b + HEAD_PAD], cos_t, sin_t)
            q_ref[hh] = jnp.concatenate([q[:, b:b + NOPE], q_pe], axis=-1).astype(BF16)
            k_ref[hh] = jnp.concatenate([kv[:, b:b + NOPE], k_pe], axis=-1).astype(BF16)
            v_ref[hh] = jnp.concatenate([kv[:, b + NOPE:b + HEAD_PAD], ones], axis=-1).astype(BF16)

    full = lambda shape: pl.BlockSpec(shape, lambda i: (0,) * len(shape))
    return _call_after(
        token, body,
        [pl.BlockSpec((tm, d), lambda i: (i, 0)), _resident((1, d)), _resident(w_in.shape), _resident((1, Q_RANK)),
         _resident(wq_t.shape), _resident((1, KV_RANK)), _resident(wkv.shape),
         pl.BlockSpec((tm, 1), lambda i: (i, 0)), _resident(rope_tab.shape)],
        (h1, mix_norm, w_in, q_norm, wq_t, kv_norm, wkv, pos, rope_tab),
        name="mix_prep", grid=(t // tm,),
        out_specs=[pl.BlockSpec((tm, d), lambda i: (i, 0)),
                   pl.BlockSpec((tm, d), lambda i: (i, 0)),
                   pl.BlockSpec((tm, Q_RANK), lambda i: (i, 0)),
                   pl.BlockSpec((tm, KV_RANK), lambda i: (i, 0)),
                   pl.BlockSpec((MLA_HEADS, tm, HEAD_PAD), lambda i: (0, i, 0)),
                   pl.BlockSpec((MLA_HEADS, tm, HEAD_PAD), lambda i: (0, i, 0)),
                   pl.BlockSpec((MLA_HEADS, tm, 2 * V_DIM), lambda i: (0, i, 0))],
        out_shape=[jax.ShapeDtypeStruct((t, d), BF16), jax.ShapeDtypeStruct((t, d), F32),
                   jax.ShapeDtypeStruct((t, Q_RANK), BF16), jax.ShapeDtypeStruct((t, KV_RANK), BF16),
                   jax.ShapeDtypeStruct((MLA_HEADS, t, HEAD_PAD), BF16),
                   jax.ShapeDtypeStruct((MLA_HEADS, t, HEAD_PAD), BF16),
                   jax.ShapeDtypeStruct((MLA_HEADS, t, 2 * V_DIM), BF16)],
        compiler_params=_cparams(),
    )


def _causal_mask(s):
    row = lax.broadcasted_iota(jnp.int32, s.shape, 0)
    col = lax.broadcasted_iota(jnp.int32, s.shape, 1)
    return jnp.where(col <= row, s, NEG_BIG)


def _attn_fwd(q, k, v):
    nh, t, _ = q.shape
    tq = tk = min(512, t)
    nq, nk = t // tq, t // tk

    pairs = [(i, j) for i in range(nq) for j in range(i + 1)]
    qi = jnp.asarray(np.array([i for i, _ in pairs], np.int32))
    kj = jnp.asarray(np.array([j for _, j in pairs], np.int32))

    def body(qi_ref, kj_ref, q_ref, k_ref, v_ref, o_ref, lse_ref, m_sc, acc_sc):
        n = pl.program_id(0)
        i, j = qi_ref[n], kj_ref[n]

        @pl.when(j == 0)
        def _():
            m_sc[...] = jnp.full_like(m_sc, NEG_BIG)
            acc_sc[...] = jnp.zeros_like(acc_sc)

        def step(diagonal):
            for hh in range(nh):
                s = _dot_nt(q_ref[hh], k_ref[hh]) * ATTN_SCALE
                if diagonal:
                    s = _causal_mask(s)
                m_old = m_sc[hh]
                m_new = jnp.maximum(m_old, jnp.max(s, axis=-1, keepdims=True))
                p = jnp.exp(s - m_new).astype(BF16)
                acc_sc[hh] = jnp.exp(m_old - m_new) * acc_sc[hh] + _dot_nn(p, v_ref[hh])
                m_sc[hh] = m_new

        @pl.when(j < i)
        def _():
            step(False)

        @pl.when(j == i)
        def _():
            step(True)
            for hh in range(nh):
                acc = acc_sc[hh]
                l = acc[:, V_DIM:2 * V_DIM]
                o_ref[:, hh * V_DIM:(hh + 1) * V_DIM] = (acc[:, 0:V_DIM] / l).astype(BF16)
                lse_ref[hh] = m_sc[hh] + jnp.log(l[:, 0:1])

    q_map = lambda n, qi_ref, kj_ref: (0, qi_ref[n], 0)
    kv_map = lambda n, qi_ref, kj_ref: (0, kj_ref[n], 0)
    return pl.pallas_call(
        body, name="attn_fwd",
        grid_spec=pltpu.PrefetchScalarGridSpec(
            num_scalar_prefetch=2, grid=(len(pairs),),
            in_specs=[pl.BlockSpec((nh, tq, HEAD_PAD), q_map),
                      pl.BlockSpec((nh, tk, HEAD_PAD), kv_map),
                      pl.BlockSpec((nh, tk, 2 * V_DIM), kv_map)],
            out_specs=[pl.BlockSpec((tq, nh * V_DIM), lambda n, qi_ref, kj_ref: (qi_ref[n], 0)),
                       pl.BlockSpec((nh, tq, 1), q_map)],
            scratch_shapes=[pltpu.VMEM((nh, tq, 1), F32), pltpu.VMEM((nh, tq, 2 * V_DIM), F32)]),
        out_shape=[jax.ShapeDtypeStruct((t, nh * V_DIM), BF16), jax.ShapeDtypeStruct((nh, t, 1), F32)],
        compiler_params=_cparams(),
    )(qi, kj, q, k, v)


def _attn_delta(o, do):
    t, w = o.shape
    nh = w // V_DIM
    tm = min(512, t)

    def body(o_ref, do_ref, d_ref):
        prod = o_ref[...].astype(F32) * do_ref[...].astype(F32)
        for hh in range(nh):
            d_ref[hh] = jnp.sum(prod[:, hh * V_DIM:(hh + 1) * V_DIM], axis=-1, keepdims=True)

    return pl.pallas_call(
        body, name="attn_delta", grid=(t // tm,),
        in_specs=[pl.BlockSpec((tm, w), lambda i: (i, 0)), pl.BlockSpec((tm, w), lambda i: (i, 0))],
        out_specs=pl.BlockSpec((nh, tm, 1), lambda i: (0, i, 0)),
        out_shape=jax.ShapeDtypeStruct((nh, t, 1), F32),
        compiler_params=_cparams(),
    )(o, do)


ATTN_BWD_HEADS = 2


def _attn_bwd(q, k, v, do, lse, delta):
    nh, t, _ = q.shape
    hp = ATTN_BWD_HEADS
    tq = tk = min(512, t)
    nq, nk = t // tq, t // tk

    pairs = [(j, i) for j in range(nk) for i in range(j, nq)]
    kj = jnp.asarray(np.array([j for j, _ in pairs], np.int32))
    qi = jnp.asarray(np.array([i for _, i in pairs], np.int32))

    def body(kj_ref, qi_ref, q_ref, k_ref, v_ref, do_ref, lse_ref, dlt_ref, dq_ref, dk_ref, dv_ref):
        n = pl.program_id(1)
        j, i = kj_ref[n], qi_ref[n]

        @pl.when(n == 0)
        def _():
            dq_ref[...] = jnp.zeros_like(dq_ref)

        def step(diagonal):
            for hh in range(hp):
                qq, kk = q_ref[hh], k_ref[hh]
                dob = do_ref[:, hh * V_DIM:(hh + 1) * V_DIM]
                s = _dot_nt(qq, kk) * ATTN_SCALE
                if diagonal:
                    s = _causal_mask(s)
                p = jnp.exp(s - lse_ref[hh])
                dpp = _dot_nt(dob, v_ref[hh])
                dsb = (p * (dpp - dlt_ref[hh]) * ATTN_SCALE).astype(BF16)
                _accumulate(dv_ref.at[hh], _dot_tn(p.astype(BF16), dob), diagonal)
                _accumulate(dk_ref.at[hh], _dot_tn(dsb, qq), diagonal)
                dq_ref[hh, pl.ds(pl.multiple_of(i * tq, tq), tq), :] += _dot_nn(dsb, kk)

        @pl.when(i > j)
        def _():
            step(False)

        @pl.when(i == j)
        def _():
            step(True)

    q_map = lambda h, n, kj_ref, qi_ref: (h, qi_ref[n], 0)
    k_map = lambda h, n, kj_ref, qi_ref: (h, kj_ref[n], 0)
    return pl.pallas_call(
        body, name="attn_bwd",
        grid_spec=pltpu.PrefetchScalarGridSpec(
            num_scalar_prefetch=2, grid=(nh // hp, len(pairs)),
            in_specs=[pl.BlockSpec((hp, tq, HEAD_PAD), q_map),
                      pl.BlockSpec((hp, tk, HEAD_PAD), k_map),
                      pl.BlockSpec((hp, tk, V_DIM), k_map),
                      pl.BlockSpec((tq, hp * V_DIM), lambda h, n, kj_ref, qi_ref: (qi_ref[n], h)),
                      pl.BlockSpec((hp, tq, 1), q_map),
                      pl.BlockSpec((hp, tq, 1), q_map)],
            out_specs=[pl.BlockSpec((hp, t, HEAD_PAD), lambda h, n, kj_ref, qi_ref: (h, 0, 0)),
                       pl.BlockSpec((hp, tk, HEAD_PAD), k_map),
                       pl.BlockSpec((hp, tk, V_DIM), k_map)]),
        out_shape=[jax.ShapeDtypeStruct((nh, t, HEAD_PAD), F32), jax.ShapeDtypeStruct((nh, t, HEAD_PAD), F32),
                   jax.ShapeDtypeStruct((nh, t, V_DIM), F32)],
        compiler_params=_cparams(),
    )(kj, qi, q, k, v, do, lse, delta)


def _pool_counts(first_token, rows, w):
    tok = lax.broadcasted_iota(jnp.int32, (rows, POOL_CH), 0) + first_token
    return jnp.minimum(tok + 1, w).astype(F32)


def _pool_centered(zbuf, g, w, i, tm):
    lanes = pl.ds(g * POOL_CH, POOL_CH)
    cur = zbuf[pl.ds(POOL_HALO, tm), lanes]
    win = cur
    for s in range(1, w):
        win = win + zbuf[pl.ds(POOL_HALO - s, tm), lanes]
    return win / _pool_counts(i * tm, tm, w) - cur


def _pool_load(zbuf, z_ref, halo_ref, i, tm):
    @pl.when(i == 0)
    def _():
        zbuf[pl.ds(0, POOL_HALO), :] = jnp.zeros((POOL_HALO, zbuf.shape[1]), F32)

    @pl.when(i > 0)
    def _():
        zbuf[pl.ds(0, POOL_HALO), :] = halo_ref[...]

    zbuf[pl.ds(POOL_HALO, tm), :] = z_ref[...]


def _pool_fwd(z, pool_w, pool_scale):
    t = z.shape[0]
    pw = len(POOL_WINDOWS) * POOL_CH
    tm = min(512, t)
    hb = tm // POOL_HALO

    def body(z_ref, halo_ref, w_ref, sc_ref, p_ref, zbuf):
        i = pl.program_id(0)
        _pool_load(zbuf, z_ref, halo_ref, i, tm)
        for g, w in enumerate(POOL_WINDOWS):
            c = _pool_centered(zbuf, g, w, i, tm)
            y = _dot_nn(c.astype(BF16), w_ref[g]) * sc_ref[:, g * POOL_CH:(g + 1) * POOL_CH]
            p_ref[:, g * POOL_CH:(g + 1) * POOL_CH] = y.astype(BF16)

    return pl.pallas_call(
        body, name="pool_fwd", grid=(t // tm,),
        in_specs=[pl.BlockSpec((tm, pw), lambda i: (i, 1)),
                  pl.BlockSpec((POOL_HALO, pw), lambda i: (jnp.maximum(i * hb - 1, 0), 1)),
                  pl.BlockSpec(pool_w.shape, lambda i: (0, 0, 0)),
                  pl.BlockSpec((1, pw), lambda i: (0, 0))],
        out_specs=pl.BlockSpec((tm, pw), lambda i: (i, 0)),
        out_shape=jax.ShapeDtypeStruct((t, pw), BF16),
        scratch_shapes=[pltpu.VMEM((POOL_HALO + tm, pw), F32)],
        compiler_params=_cparams(),
    )(z, z, pool_w, pool_scale)


def _pool_bwd(dp, z, pool_w, pool_scale):
    t = z.shape[0]
    ng = len(POOL_WINDOWS)
    pw = ng * POOL_CH
    tm = min(512, t)
    hb = tm // POOL_HALO
    nt = t // tm

    def body(dp_ref, dpn_ref, z_ref, halo_ref, w_ref, sc_ref, dz_ref, dw_ref, dsc_ref, zbuf, dbuf):
        i = pl.program_id(0)
        _pool_load(zbuf, z_ref, halo_ref, i, tm)

        @pl.when(i == 0)
        def _():
            dw_ref[...] = jnp.zeros_like(dw_ref)
            dsc_ref[...] = jnp.zeros_like(dsc_ref)

        nxt_ok = (i < nt - 1).astype(F32)
        for g, w in enumerate(POOL_WINDOWS):
            lanes = pl.ds(g * POOL_CH, POOL_CH)
            cols = slice(g * POOL_CH, (g + 1) * POOL_CH)
            sc = sc_ref[:, cols]
            wg = w_ref[g]
            c = _pool_centered(zbuf, g, w, i, tm).astype(BF16)
            ypre = _dot_nn(c, wg)
            dpg = dp_ref[:, cols].astype(F32)
            dsc_ref[:, cols] += jnp.sum(dpg * ypre, axis=0, keepdims=True)
            dyb = (dpg * sc).astype(BF16)
            dw_ref[g] += _dot_tn(c, dyb)
            dd = _dot_nt(dyb, wg)
            dyn = (dpn_ref[:, cols].astype(F32) * sc).astype(BF16)
            ddn = _dot_nt(dyn, wg) * nxt_ok
            dbuf[pl.ds(0, tm), lanes] = dd / _pool_counts(i * tm, tm, w)
            dbuf[pl.ds(tm, POOL_HALO), lanes] = ddn / _pool_counts((i + 1) * tm, POOL_HALO, w)
            acc = -dd
            for s in range(w):
                acc = acc + dbuf[pl.ds(s, tm), lanes]
            dz_ref[:, cols] = acc

    return pl.pallas_call(
        body, name="pool_bwd", grid=(nt,),
        in_specs=[pl.BlockSpec((tm, pw), lambda i: (i, 0)),
                  pl.BlockSpec((POOL_HALO, pw), lambda i: (jnp.minimum((i + 1) * hb, t // POOL_HALO - 1), 0)),
                  pl.BlockSpec((tm, pw), lambda i: (i, 1)),
                  pl.BlockSpec((POOL_HALO, pw), lambda i: (jnp.maximum(i * hb - 1, 0), 1)),
                  pl.BlockSpec(pool_w.shape, lambda i: (0, 0, 0)),
                  pl.BlockSpec((1, pw), lambda i: (0, 0))],
        out_specs=[pl.BlockSpec((tm, pw), lambda i: (i, 0)),
                   pl.BlockSpec((ng, POOL_CH, POOL_CH), lambda i: (0, 0, 0)),
                   pl.BlockSpec((1, pw), lambda i: (0, 0))],
        out_shape=[jax.ShapeDtypeStruct((t, pw), F32), jax.ShapeDtypeStruct((ng, POOL_CH, POOL_CH), F32),
                   jax.ShapeDtypeStruct((1, pw), F32)],
        scratch_shapes=[pltpu.VMEM((POOL_HALO + tm, pw), F32), pltpu.VMEM((tm + POOL_HALO, pw), F32)],
        compiler_params=_cparams(),
    )(dp, dp, z, z, pool_w, pool_scale)


def _mla_bwd(dq_h, dk_h, dv_h, z, dz_pool, h1, dh2, mix_norm, w_in, q_norm, wq_t, kv_norm, wkv, pos, rope_tab):
    t, d = h1.shape
    tm = min(512, t)

    def body(dqh_ref, dkh_ref, dvh_ref, z_ref, dzp_ref, h_ref, dh2_ref, gm_ref, win_ref, gq_ref, wq_ref, gkv_ref,
             wkv_ref, pos_ref, tab_ref, dh1_ref, dq_ref, dkv_ref, dz_ref, dgq_ref, dgkv_ref, dgm_ref):
        i = pl.program_id(0)
        first = i == 0
        cos_t, sin_t = _rope_tables(pos_ref[...], tab_ref[...])
        dq_parts, dkv_parts = [], []
        dk_pe = jnp.zeros((tm, 128), F32)
        for hh in range(MLA_HEADS):
            dqh = dqh_ref[hh]
            dq_parts += [dqh[:, 0:NOPE], _rope_apply_t(dqh[:, NOPE:HEAD_PAD], cos_t, sin_t)]
            dkh = dkh_ref[hh]
            dkv_parts += [dkh[:, 0:NOPE], dvh_ref[hh]]
            dk_pe = dk_pe + dkh[:, NOPE:HEAD_PAD]
        dqb = jnp.concatenate(dq_parts, axis=-1).astype(BF16)
        dkvb = jnp.concatenate(dkv_parts, axis=-1).astype(BF16)
        dq_ref[...] = dqb
        dkv_ref[...] = dkvb
        z = z_ref[...]
        c_q = z[:, 0:Q_RANK]
        gq = gq_ref[...]
        _, rq = _rms_fwd(c_q, gq)
        dcq, dgq = _rms_bwd(_dot_nn(dqb, wq_ref[...]), c_q, gq, rq)
        c_kv = z[:, Q_RANK:Q_RANK + KV_RANK]
        gkv = gkv_ref[...]
        _, rkv = _rms_fwd(c_kv, gkv)
        dckv, dgkv = _rms_bwd(_dot_nt(dkvb, wkv_ref[...]), c_kv, gkv, rkv)
        dkr = _rope_apply_t(dk_pe, cos_t, sin_t)
        dzb = jnp.concatenate([dcq, dckv, dkr, dzp_ref[...]], axis=-1).astype(BF16)
        dz_ref[...] = dzb
        x = h_ref[...]
        gm = gm_ref[...]
        _, rm = _rms_fwd(x, gm)
        dx, dgm = _rms_bwd(_dot_nt(dzb, win_ref[...]), x, gm, rm)
        dh1_ref[...] = dh2_ref[...] + dx
        _accumulate(dgq_ref, dgq, first)
        _accumulate(dgkv_ref, dgkv, first)
        _accumulate(dgm_ref, dgm, first)

    full = lambda shape: pl.BlockSpec(shape, lambda i: (0,) * len(shape))
    row = lambda w: pl.BlockSpec((tm, w), lambda i: (i, 0))
    head = lambda w: pl.BlockSpec((MLA_HEADS, tm, w), lambda i: (0, i, 0))
    pw = len(POOL_WINDOWS) * POOL_CH
    return pl.pallas_call(
        body, name="mla_bwd", grid=(t // tm,),
        in_specs=[head(HEAD_PAD), head(HEAD_PAD), head(V_DIM), row(d), row(pw), row(d), row(d),
                  _resident((1, d)), _resident(w_in.shape), _resident((1, Q_RANK)), _resident(wq_t.shape),
                  _resident((1, KV_RANK)), _resident(wkv.shape), row(1), _resident(rope_tab.shape)],
        out_specs=[row(d), row(d), row(d), row(d), full((1, Q_RANK)), full((1, KV_RANK)), full((1, d))],
        out_shape=[jax.ShapeDtypeStruct((t, d), F32), jax.ShapeDtypeStruct((t, d), BF16),
                   jax.ShapeDtypeStruct((t, d), BF16), jax.ShapeDtypeStruct((t, d), BF16),
                   jax.ShapeDtypeStruct((1, Q_RANK), F32), jax.ShapeDtypeStruct((1, KV_RANK), F32),
                   jax.ShapeDtypeStruct((1, d), F32)],
        compiler_params=_cparams(),
    )(dq_h, dk_h, dv_h, z, dz_pool, h1, dh2, mix_norm, w_in, q_norm, wq_t, kv_norm, wkv, pos, rope_tab)


def _mem_kv(mem, mem_norm, wmkv):
    n, d = mem.shape

    def body(mem_ref, g_ref, w_ref, memn_ref, k_ref, v_ref):
        y, _ = _rms_fwd(mem_ref[...], g_ref[...])
        yb = y.astype(BF16)
        memn_ref[...] = yb
        for hh in range(MEM_HEADS):
            k_ref[hh] = _dot_nn(yb, w_ref[hh]).astype(BF16)
            v_ref[hh] = _dot_nn(yb, w_ref[MEM_HEADS + hh]).astype(BF16)

    return pl.pallas_call(
        body, name="mem_kv",
        out_shape=[jax.ShapeDtypeStruct((n, d), BF16), jax.ShapeDtypeStruct((MEM_HEADS, n, MEM_HD), BF16),
                   jax.ShapeDtypeStruct((MEM_HEADS, n, MEM_HD), BF16)],
        compiler_params=_cparams(),
    )(mem, mem_norm, wmkv)


def _mem_softmax(qb, km):
    s = _dot_nt(qb, km) * MEM_SCALE
    e = jnp.exp(s - jnp.max(s, axis=-1, keepdims=True))
    return e / jnp.sum(e, axis=-1, keepdims=True)


def _xattn_fwd(h1, a, p, w_out, g, wmq, km, vm, wmo, token=None):
    t, d = h1.shape
    tm = min(512, t)
    half = a.shape[1]

    def body(h_ref, a_ref, p_ref, wo_ref, g_ref, wmq_ref, km_ref, vm_ref, wmo_ref,
             h2_ref, h3_ref, hn_ref, q_ref, o_ref):
        h2 = h_ref[...] + _dot_nn(a_ref[...], wo_ref[0:half, :]) + _dot_nn(p_ref[...], wo_ref[half:2 * half, :])
        h2_ref[...] = h2
        hn, _ = _rms_fwd(h2, g_ref[...])
        hnb = hn.astype(BF16)
        hn_ref[...] = hnb
        qb = _dot_nn(hnb, wmq_ref[...]).astype(BF16)
        q_ref[...] = qb
        outs = []
        for hh in range(MEM_HEADS):
            pr = _mem_softmax(qb[:, hh * MEM_HD:(hh + 1) * MEM_HD], km_ref[hh])
            outs.append(_dot_nn(pr.astype(BF16), vm_ref[hh]))
        ob = jnp.concatenate(outs, axis=-1).astype(BF16)
        o_ref[...] = ob
        h3_ref[...] = h2 + _dot_nn(ob, wmo_ref[...])

    full = lambda shape: pl.BlockSpec(shape, lambda i: (0,) * len(shape))
    row = lambda w: pl.BlockSpec((tm, w), lambda i: (i, 0))
    return _call_after(
        token, body,
        [row(d), row(half), row(half), _resident(w_out.shape), _resident((1, d)), _resident(wmq.shape),
         _resident(km.shape), _resident(vm.shape), _resident(wmo.shape)],
        (h1, a, p, w_out, g, wmq, km, vm, wmo),
        name="xattn_fwd", grid=(t // tm,),
        out_specs=[row(d), row(d), row(d), row(d), row(d)],
        out_shape=[jax.ShapeDtypeStruct((t, d), F32), jax.ShapeDtypeStruct((t, d), F32),
                   jax.ShapeDtypeStruct((t, d), BF16), jax.ShapeDtypeStruct((t, d), BF16),
                   jax.ShapeDtypeStruct((t, d), BF16)],
        compiler_params=_cparams(),
    )


def _xattn_bwd(dh3, h2, qm, g, wmq, km, vm, wmo, w_out, token=None):
    t, d = h2.shape
    tm = min(512, t)
    half = d // 2

    def body(dh3_ref, h2_ref, q_ref, g_ref, wmq_ref, km_ref, vm_ref, wmo_ref, wo_ref,
             dh2_ref, dq_ref, da_ref, dp_ref, dk_ref, dv_ref, dg_ref):
        i = pl.program_id(0)
        first = i == 0

        @pl.when(first)
        def _():
            dk_ref[...] = jnp.zeros_like(dk_ref)
            dv_ref[...] = jnp.zeros_like(dv_ref)

        dh3 = dh3_ref[...]
        dob = _dot_nt(dh3.astype(BF16), wmo_ref[...]).astype(BF16)
        qb = q_ref[...]
        dq_parts = []
        for hh in range(MEM_HEADS):
            cols = slice(hh * MEM_HD, (hh + 1) * MEM_HD)
            kk, vv = km_ref[hh], vm_ref[hh]
            pr = _mem_softmax(qb[:, cols], kk)
            doh = dob[:, cols]
            dv_ref[hh] += _dot_tn(pr.astype(BF16), doh)
            dpp = _dot_nt(doh, vv)
            dsb = (pr * (dpp - jnp.sum(dpp * pr, axis=-1, keepdims=True)) * MEM_SCALE).astype(BF16)
            dq_parts.append(_dot_nn(dsb, kk))
            dk_ref[hh] += _dot_tn(dsb, qb[:, cols])
        dqb = jnp.concatenate(dq_parts, axis=-1).astype(BF16)
        dq_ref[...] = dqb
        x = h2_ref[...]
        gg = g_ref[...]
        _, r = _rms_fwd(x, gg)
        dx, dg = _rms_bwd(_dot_nt(dqb, wmq_ref[...]), x, gg, r)
        dh2 = dh3 + dx
        dh2_ref[...] = dh2
        dap = _dot_nt(dh2.astype(BF16), wo_ref[...])
        da_ref[...] = dap[:, 0:half].astype(BF16)
        dp_ref[...] = dap[:, half:d].astype(BF16)
        _accumulate(dg_ref, dg, first)

    full = lambda shape: pl.BlockSpec(shape, lambda i: (0,) * len(shape))
    row = lambda w: pl.BlockSpec((tm, w), lambda i: (i, 0))
    return _call_after(
        token, body,
        [row(d), row(d), row(d), _resident((1, d)), _resident(wmq.shape), _resident(km.shape), _resident(vm.shape),
         _resident(wmo.shape), _resident(w_out.shape)],
        (dh3, h2, qm, g, wmq, km, vm, wmo, w_out),
        name="xattn_bwd", grid=(t // tm,),
        out_specs=[row(d), row(d), row(half), row(half), full(km.shape), full(vm.shape), full((1, d))],
        out_shape=[jax.ShapeDtypeStruct((t, d), F32), jax.ShapeDtypeStruct((t, d), BF16),
                   jax.ShapeDtypeStruct((t, half), BF16), jax.ShapeDtypeStruct((t, half), BF16),
                   jax.ShapeDtypeStruct(km.shape, F32), jax.ShapeDtypeStruct(vm.shape, F32),
                   jax.ShapeDtypeStruct((1, d), F32)],
        compiler_params=_cparams(),
    )


def _mem_kv_bwd(dkm, dvm, memn, mem, mem_norm, wmkv):
    n, d = mem.shape

    def body(dk_ref, dv_ref, memn_ref, mem_ref, g_ref, w_ref, dw_ref, dg_ref):
        memn = memn_ref[...]
        dmemn = jnp.zeros((n, d), F32)
        for s in range(2 * MEM_HEADS):
            src = dk_ref[s] if s < MEM_HEADS else dv_ref[s - MEM_HEADS]
            db = src.astype(BF16)
            dw_ref[s] = _dot_tn(memn, db)
            dmemn = dmemn + _dot_nt(db, w_ref[s])
        x = mem_ref[...]
        gg = g_ref[...]
        _, r = _rms_fwd(x, gg)
        _, dg = _rms_bwd(dmemn, x, gg, r)
        dg_ref[...] = dg

    return pl.pallas_call(
        body, name="mem_kv_bwd",
        out_shape=[jax.ShapeDtypeStruct(wmkv.shape, F32), jax.ShapeDtypeStruct((1, d), F32)],
        compiler_params=_cparams(),
    )(dkm, dvm, memn, mem, mem_norm, wmkv)


MESH_ID = pl.DeviceIdType.MESH
ANY = pl.BlockSpec(memory_space=pl.ANY)


def _coords():
    return lax.axis_index("x"), lax.axis_index("y"), lax.axis_index("c")


def _other_chips(x, y):
    return [(1 - x, y), (x, 1 - y), (1 - x, 1 - y)]


def _core_reduce(g, tag):
    _, r, w = g.shape

    def body(g_ref, part_ref, own_sc, recv_sc, send_sems, recv_sems, local_sems):
        x, y, c = _coords()
        sent, local = [], []
        for chip in range(4):
            sent.append(pltpu.make_async_remote_copy(
                src_ref=g_ref.at[2 * chip + (1 - c)], dst_ref=recv_sc.at[chip],
                send_sem=send_sems.at[chip], recv_sem=recv_sems.at[chip],
                device_id=(x, y, 1 - c), device_id_type=MESH_ID))
            local.append(pltpu.make_async_copy(g_ref.at[2 * chip + c], own_sc.at[chip], local_sems.at[chip]))
        for cp in sent + local:
            cp.start()
        for chip in range(4):
            local[chip].wait()
            sent[chip].wait_recv()
            part_ref[chip] = (own_sc[chip].astype(F32) + recv_sc[chip].astype(F32)).astype(part_ref.dtype)
        for cp in sent:
            cp.wait_send()

    return pl.pallas_call(
        body, name="core_reduce_" + tag,
        out_shape=jax.ShapeDtypeStruct((4, r, w), g.dtype),
        in_specs=[ANY], out_specs=pl.BlockSpec(memory_space=pltpu.VMEM),
        scratch_shapes=[pltpu.VMEM((4, r, w), g.dtype), pltpu.VMEM((4, r, w), g.dtype),
                        pltpu.SemaphoreType.DMA((4,)), pltpu.SemaphoreType.DMA((4,)), pltpu.SemaphoreType.DMA((4,))],
        compiler_params=_cparams(),
    )(g)


HBM_SPEC = pl.BlockSpec(memory_space=pltpu.HBM)
SEM_SPEC = pl.BlockSpec(memory_space=pltpu.SEMAPHORE)
SPLIT_EFFECT = pltpu.SideEffectType.DATAFLOW_SIDE_EFFECTING


def _ici_refs(gather, src_ref, land_ref, j, px, py, slot_chip, c):
    if gather:
        return src_ref, land_ref.at[:, 4 * slot_chip[0] + 2 * slot_chip[1] + c]
    return src_ref.at[2 * px + py], land_ref.at[j]


def _ici_start(src, after, name, gather):
    r, w = src.shape[-2:]
    land_shape = (src.shape[0], N_DEV, r, w) if gather else (3, r, w)

    def body(src_ref, land_ref, after_ref, send_sems, recv_sems, src_thru, land_thru, token):
        x, y, c = _coords()
        for j, (px, py) in enumerate(_other_chips(x, y)):
            s_ref, d_ref = _ici_refs(gather, src_ref, land_ref, j, px, py, (x, y), c)
            pltpu.make_async_remote_copy(
                src_ref=s_ref, dst_ref=d_ref, send_sem=send_sems.at[j], recv_sem=recv_sems.at[j],
                device_id=(px, py, c), device_id_type=MESH_ID).start()
        token[...] = jnp.zeros_like(token)

    return pl.pallas_call(
        body, name=name,
        out_shape=(pltpu.SemaphoreType.DMA((3,)), pltpu.SemaphoreType.DMA((3,)), pltpu.HBM(src.shape, src.dtype),
                   pltpu.HBM(land_shape, src.dtype), jax.ShapeDtypeStruct((8, 128), F32)),
        in_specs=(HBM_SPEC, HBM_SPEC, ANY),
        out_specs=(SEM_SPEC, SEM_SPEC, HBM_SPEC, HBM_SPEC, pl.BlockSpec(memory_space=pltpu.VMEM)),
        input_output_aliases={0: 2, 1: 3},
        compiler_params=pltpu.CompilerParams(has_side_effects=SPLIT_EFFECT),
    )(pltpu.with_memory_space_constraint(src, pltpu.HBM),
      pltpu.with_memory_space_constraint(lax.empty(land_shape, src.dtype), pltpu.HBM), after)


def _ici_wait(started, after, name, gather):
    send_sems, recv_sems, src_thru, land_thru, _ = started

    def body(src_ref, land_ref, send_sems, recv_sems, after_ref, src_dead, got_ref):
        x, y, c = _coords()
        for j, (px, py) in enumerate(_other_chips(x, y)):
            s_ref, d_ref = _ici_refs(gather, src_ref, land_ref, j, px, py, (px, py), c)
            copy = pltpu.make_async_remote_copy(
                src_ref=s_ref, dst_ref=d_ref, send_sem=send_sems.at[j], recv_sem=recv_sems.at[j],
                device_id=(px, py, c), device_id_type=MESH_ID)
            copy.wait_send()
            copy.wait_recv()

    return pl.pallas_call(
        body, name=name,
        out_shape=(pltpu.HBM(src_thru.shape, src_thru.dtype), pltpu.HBM(land_thru.shape, land_thru.dtype)),
        in_specs=(HBM_SPEC, HBM_SPEC, SEM_SPEC, SEM_SPEC, ANY),
        out_specs=(HBM_SPEC, HBM_SPEC), input_output_aliases={0: 0, 1: 1},
        compiler_params=pltpu.CompilerParams(has_side_effects=SPLIT_EFFECT),
    )(src_thru, land_thru, send_sems, recv_sems, after)


def _neighbour(k, x, y):
    return (1 - x, y) if k == 0 else (x, 1 - y)


def _slot(ref, px, py, c):
    return ref.at[:, 4 * px + 2 * py + c]


def _near_start(src, after, name):
    land_shape = (src.shape[0], N_DEV) + src.shape[1:]

    def body(src_ref, land_ref, after_ref, send_sems, recv_sems, src_thru, land_thru, token):
        x, y, c = _coords()
        for k in range(2):
            px, py = _neighbour(k, x, y)
            pltpu.make_async_remote_copy(
                src_ref=src_ref, dst_ref=_slot(land_ref, x, y, c), send_sem=send_sems.at[k],
                recv_sem=recv_sems.at[k], device_id=(px, py, c), device_id_type=MESH_ID).start()
        token[...] = jnp.zeros_like(token)

    return pl.pallas_call(
        body, name=name,
        out_shape=(pltpu.SemaphoreType.DMA((2,)), pltpu.SemaphoreType.DMA((2,)), pltpu.HBM(src.shape, src.dtype),
                   pltpu.HBM(land_shape, src.dtype), jax.ShapeDtypeStruct((8, 128), F32)),
        in_specs=(HBM_SPEC, HBM_SPEC, ANY),
        out_specs=(SEM_SPEC, SEM_SPEC, HBM_SPEC, HBM_SPEC, pl.BlockSpec(memory_space=pltpu.VMEM)),
        input_output_aliases={0: 2, 1: 3},
        compiler_params=pltpu.CompilerParams(has_side_effects=SPLIT_EFFECT),
    )(pltpu.with_memory_space_constraint(src, pltpu.HBM),
      pltpu.with_memory_space_constraint(lax.empty(land_shape, src.dtype), pltpu.HBM), after)


def _near_wait(started, after, name):
    send_sems, recv_sems, src_thru, land_thru, _ = started

    def body(src_ref, land_ref, send_sems, recv_sems, after_ref, src_dead, got_ref):
        x, y, c = _coords()
        for k in range(2):
            px, py = _neighbour(k, x, y)
            copy = pltpu.make_async_remote_copy(
                src_ref=src_ref, dst_ref=_slot(land_ref, px, py, c), send_sem=send_sems.at[k],
                recv_sem=recv_sems.at[k], device_id=(px, py, c), device_id_type=MESH_ID)
            copy.wait_send()
            copy.wait_recv()

    return pl.pallas_call(
        body, name=name,
        out_shape=(pltpu.HBM(src_thru.shape, src_thru.dtype), pltpu.HBM(land_thru.shape, land_thru.dtype)),
        in_specs=(HBM_SPEC, HBM_SPEC, SEM_SPEC, SEM_SPEC, ANY),
        out_specs=(HBM_SPEC, HBM_SPEC), input_output_aliases={0: 0, 1: 1},
        compiler_params=pltpu.CompilerParams(has_side_effects=SPLIT_EFFECT),
    )(src_thru, land_thru, send_sems, recv_sems, after)


def _far_refs(land_ref, k, x, y, c, arriving):
    half = land_ref.shape[2] // 2
    rows = pl.ds(k * half, half)
    ox, oy = (1 - x, 1 - y) if arriving else _neighbour(k, x, y)
    return land_ref.at[:, 4 * ox + 2 * oy + c, rows]


def _far_start(land, after, name):
    def body(land_ref, after_ref, send_sems, recv_sems, land_thru, token):
        x, y, c = _coords()
        for k in range(2):
            block = _far_refs(land_ref, k, x, y, c, False)
            px, py = _neighbour(1 - k, x, y)
            pltpu.make_async_remote_copy(
                src_ref=block, dst_ref=block, send_sem=send_sems.at[k], recv_sem=recv_sems.at[k],
                device_id=(px, py, c), device_id_type=MESH_ID).start()
        token[...] = jnp.zeros_like(token)

    return pl.pallas_call(
        body, name=name,
        out_shape=(pltpu.SemaphoreType.DMA((2,)), pltpu.SemaphoreType.DMA((2,)),
                   pltpu.HBM(land.shape, land.dtype), jax.ShapeDtypeStruct((8, 128), F32)),
        in_specs=(HBM_SPEC, ANY),
        out_specs=(SEM_SPEC, SEM_SPEC, HBM_SPEC, pl.BlockSpec(memory_space=pltpu.VMEM)),
        input_output_aliases={0: 2},
        compiler_params=pltpu.CompilerParams(has_side_effects=SPLIT_EFFECT),
    )(pltpu.with_memory_space_constraint(land, pltpu.HBM), after)


def _far_wait(started, after, name):
    send_sems, recv_sems, land_thru, _ = started

    def body(land_ref, send_sems, recv_sems, after_ref, got_ref):
        x, y, c = _coords()
        for k in range(2):
            px, py = _neighbour(1 - k, x, y)
            copy = pltpu.make_async_remote_copy(
                src_ref=_far_refs(land_ref, k, x, y, c, False), dst_ref=_far_refs(land_ref, k, x, y, c, True),
                send_sem=send_sems.at[k], recv_sem=recv_sems.at[k], device_id=(px, py, c), device_id_type=MESH_ID)
            copy.wait_send()
            copy.wait_recv()

    return pl.pallas_call(
        body, name=name,
        out_shape=pltpu.HBM(land_thru.shape, land_thru.dtype),
        in_specs=(HBM_SPEC, SEM_SPEC, SEM_SPEC, ANY),
        out_specs=HBM_SPEC, input_output_aliases={0: 0},
        compiler_params=pltpu.CompilerParams(has_side_effects=SPLIT_EFFECT),
    )(land_thru, send_sems, recv_sems, after)


def _peer(k, x, y, c):
    return x ^ ((k >> 2) & 1), y ^ ((k >> 1) & 1), c ^ (k & 1)


def _peers_start(src, after, name):
    r, w = src.shape
    x, y, c = _coords()
    land = lax.dynamic_update_slice(jnp.zeros((N_DEV, r, w), src.dtype), src[None], (4 * x + 2 * y + c, 0, 0))

    def body(src_ref, land_ref, after_ref, send_sems, recv_sems, src_thru, land_thru, token):
        x, y, c = _coords()
        for k in range(1, N_DEV):
            pltpu.make_async_remote_copy(
                src_ref=src_ref, dst_ref=land_ref.at[4 * x + 2 * y + c],
                send_sem=send_sems.at[k - 1], recv_sem=recv_sems.at[k - 1],
                device_id=_peer(k, x, y, c), device_id_type=MESH_ID).start()
        token[...] = jnp.zeros_like(token)

    return pl.pallas_call(
        body, name=name,
        out_shape=(pltpu.SemaphoreType.DMA((N_DEV - 1,)), pltpu.SemaphoreType.DMA((N_DEV - 1,)),
                   pltpu.HBM(src.shape, src.dtype), pltpu.HBM(land.shape, src.dtype),
                   jax.ShapeDtypeStruct((8, 128), F32)),
        in_specs=(HBM_SPEC, HBM_SPEC, ANY),
        out_specs=(SEM_SPEC, SEM_SPEC, HBM_SPEC, HBM_SPEC, pl.BlockSpec(memory_space=pltpu.VMEM)),
        input_output_aliases={0: 2, 1: 3},
        compiler_params=pltpu.CompilerParams(has_side_effects=SPLIT_EFFECT),
    )(pltpu.with_memory_space_constraint(src, pltpu.HBM), pltpu.with_memory_space_constraint(land, pltpu.HBM), after)


def _peers_wait(started, after, name):
    send_sems, recv_sems, src_thru, land_thru, _ = started

    def body(src_ref, land_ref, send_sems, recv_sems, after_ref, src_dead, got_ref):
        x, y, c = _coords()
        for k in range(1, N_DEV):
            px, py, pc = _peer(k, x, y, c)
            copy = pltpu.make_async_remote_copy(
                src_ref=src_ref, dst_ref=land_ref.at[4 * px + 2 * py + pc],
                send_sem=send_sems.at[k - 1], recv_sem=recv_sems.at[k - 1],
                device_id=(px, py, pc), device_id_type=MESH_ID)
            copy.wait_send()
            copy.wait_recv()

    return pl.pallas_call(
        body, name=name,
        out_shape=(pltpu.HBM(src_thru.shape, src_thru.dtype), pltpu.HBM(land_thru.shape, land_thru.dtype)),
        in_specs=(HBM_SPEC, HBM_SPEC, SEM_SPEC, SEM_SPEC, ANY),
        out_specs=(HBM_SPEC, HBM_SPEC), input_output_aliases={0: 0, 1: 1},
        compiler_params=pltpu.CompilerParams(has_side_effects=SPLIT_EFFECT),
    )(src_thru, land_thru, send_sems, recv_sems, after)


def _share_refs(ref, k, x, y, c, sender_c):
    px, py = ([(x, y)] + _other_chips(x, y))[k]
    return ref.at[:, 4 * px + 2 * py + sender_c]


def _share_start(gathered, after, name):
    def body(g_ref, after_ref, send_sems, recv_sems, g_thru, token):
        x, y, c = _coords()
        for k in range(4):
            slot = _share_refs(g_ref, k, x, y, c, c)
            pltpu.make_async_remote_copy(
                src_ref=slot, dst_ref=slot, send_sem=send_sems.at[k], recv_sem=recv_sems.at[k],
                device_id=(x, y, 1 - c), device_id_type=MESH_ID).start()
        token[...] = jnp.zeros_like(token)

    return pl.pallas_call(
        body, name=name,
        out_shape=(pltpu.SemaphoreType.DMA((4,)), pltpu.SemaphoreType.DMA((4,)),
                   pltpu.HBM(gathered.shape, gathered.dtype), jax.ShapeDtypeStruct((8, 128), F32)),
        in_specs=(HBM_SPEC, ANY),
        out_specs=(SEM_SPEC, SEM_SPEC, HBM_SPEC, pl.BlockSpec(memory_space=pltpu.VMEM)),
        input_output_aliases={0: 2},
        compiler_params=pltpu.CompilerParams(has_side_effects=SPLIT_EFFECT),
    )(pltpu.with_memory_space_constraint(gathered, pltpu.HBM), after)


def _share_wait(started, after, name):
    send_sems, recv_sems, g_thru, _ = started

    def body(g_ref, send_sems, recv_sems, after_ref, got_ref):
        x, y, c = _coords()
        for k in range(4):
            copy = pltpu.make_async_remote_copy(
                src_ref=_share_refs(g_ref, k, x, y, c, c), dst_ref=_share_refs(g_ref, k, x, y, c, 1 - c),
                send_sem=send_sems.at[k], recv_sem=recv_sems.at[k],
                device_id=(x, y, 1 - c), device_id_type=MESH_ID)
            copy.wait_send()
            copy.wait_recv()

    return pl.pallas_call(
        body, name=name,
        out_shape=pltpu.HBM(g_thru.shape, g_thru.dtype),
        in_specs=(HBM_SPEC, SEM_SPEC, SEM_SPEC, ANY),
        out_specs=HBM_SPEC, input_output_aliases={0: 0},
        compiler_params=pltpu.CompilerParams(has_side_effects=SPLIT_EFFECT),
    )(g_thru, send_sems, recv_sems, after)


def _core_share(own, gathered, name):
    def body(own_ref, gin_ref, out_ref, stage, send_sems, recv_sems, local_sem):
        x, y, c = _coords()
        sibling = (x, y, 1 - c)
        chips = [(x, y)] + _other_chips(x, y)
        stage_in = pltpu.make_async_copy(own_ref, stage, local_sem)
        stage_in.start()
        sent, arriving = [], []
        for k, (px, py) in enumerate(chips):
            slot = out_ref.at[:, 4 * px + 2 * py + c]
            sent.append(pltpu.make_async_remote_copy(
                src_ref=own_ref if k == 0 else slot, dst_ref=slot,
                send_sem=send_sems.at[k], recv_sem=recv_sems.at[k], device_id=sibling, device_id_type=MESH_ID))
            arriving.append(pltpu.make_async_remote_copy(
                src_ref=own_ref, dst_ref=out_ref.at[:, 4 * px + 2 * py + (1 - c)],
                send_sem=send_sems.at[k], recv_sem=recv_sems.at[k], device_id=sibling, device_id_type=MESH_ID))
        for cp in sent:
            cp.start()
        stage_in.wait()
        stage_out = pltpu.make_async_copy(stage, out_ref.at[:, 4 * x + 2 * y + c], local_sem)
        stage_out.start()
        for cp in arriving:
            cp.wait_recv()
        for cp in sent:
            cp.wait_send()
        stage_out.wait()

    return pl.pallas_call(
        body, name=name,
        out_shape=jax.ShapeDtypeStruct(gathered.shape, own.dtype),
        in_specs=[ANY, ANY], out_specs=ANY, input_output_aliases={1: 0},
        scratch_shapes=[pltpu.VMEM(own.shape, own.dtype), pltpu.SemaphoreType.DMA((4,)),
                        pltpu.SemaphoreType.DMA((4,)), pltpu.SemaphoreType.DMA],
    )(own, gathered)


def _adamw(w, g, m, v):
    m = ADAM_B1 * m + (1.0 - ADAM_B1) * g
    v = ADAM_B2 * v + (1.0 - ADAM_B2) * (g * g)
    m_hat = m / ADAM_C1
    v_hat = v / ADAM_C2
    delta = -ADAM_LR * (m_hat / (jnp.sqrt(v_hat) + ADAM_EPS) + ADAM_WD * w)
    return delta, m, v


def _adam_big(part, land, w, m, v, chip_idx, tag, token):
    r, wd = w.shape
    tr, tw = _row_tile(r, 1024), 256

    def body(s_ref, tok_ref, p_ref, l_ref, w_ref, m_ref, v_ref, g_ref, d_ref, mo_ref, vo_ref):
        g = p_ref[0].astype(F32)
        for j in range(3):
            g = g + l_ref[j].astype(F32)
        delta, mn, vn = _adamw(w_ref[...], g, m_ref[...], v_ref[...])
        g_ref[...] = g
        d_ref[...] = delta
        mo_ref[...] = mn
        vo_ref[...] = vn

    row = pl.BlockSpec((tr, tw), lambda i, j, s: (i, j))
    return pl.pallas_call(
        body, name="adam_big_" + tag,
        grid_spec=pltpu.PrefetchScalarGridSpec(
            num_scalar_prefetch=1, grid=(r // tr, wd // tw),
            in_specs=[pl.BlockSpec((8, 128), lambda i, j, s: (0, 0)),
                      pl.BlockSpec((1, tr, tw), lambda i, j, s: (s[0], i, j)),
                      pl.BlockSpec((3, tr, tw), lambda i, j, s: (0, i, j)), row, row, row],
            out_specs=[row, row, row, row]),
        out_shape=[jax.ShapeDtypeStruct((r, wd), F32)] * 4,
        compiler_params=_cparams(),
    )(chip_idx, token, part, land, w, m, v)


def _adam_small(parts, w, m, v):
    _, r, wd = parts.shape

    def body(p_ref, w_ref, m_ref, v_ref, g_ref, d_ref, mo_ref, vo_ref):
        g = p_ref[0]
        for k in range(1, N_DEV):
            g = g + p_ref[k]
        delta, mn, vn = _adamw(w_ref[...], g, m_ref[...], v_ref[...])
        g_ref[...] = g
        d_ref[...] = delta
        mo_ref[...] = mn
        vo_ref[...] = vn

    return pl.pallas_call(
        body, name="adam_small",
        out_shape=[jax.ShapeDtypeStruct((r, wd), F32)] * 4,
        compiler_params=_cparams(),
    )(parts, w, m, v)


def _pad_rows(a, rows):
    return jnp.pad(a, ((0, rows - a.shape[0]), (0, 0)))


def _pad_w_in(w):
    cut = Q_RANK + KV_RANK + ROPE
    return jnp.concatenate([w[:, :cut], jnp.zeros((w.shape[0], 64), w.dtype), w[:, cut:]], axis=1)


def _unpad_w_in(w):
    cut = Q_RANK + KV_RANK + ROPE
    return jnp.concatenate([w[:, :cut], w[:, cut + 64:]], axis=1)


def _pack_mid(p):
    parts = [_pad_w_in(p["w_in"][0]), p["w_out"][0], p["w_mq"][0], p["w_mo"][0],
             p["w_mkv"][0].reshape(256, D_MODEL),
             _pad_rows(p["w_q_up"][0].T.reshape(24, D_MODEL), 32),
             p["w_kv_up"][0].reshape(16, D_MODEL)]
    return jnp.concatenate(parts, axis=0)


def _pack_segments(p, group):
    if group == "mid":
        return _pack_mid(p)[None]
    return jnp.stack([p[group + "_w_gate"][0].T, p[group + "_w_up"][0].T, p[group + "_w_down"][0]])


UNIT_WEIGHT = {"ffn1_g": ("ffn1_w_gate", True), "ffn1_u": ("ffn1_w_up", True), "ffn1_d": ("ffn1_w_down", False),
               "ffn2_g": ("ffn2_w_gate", True), "ffn2_u": ("ffn2_w_up", True), "ffn2_d": ("ffn2_w_down", False)}


def _pack_unit(p, unit):
    if unit == "mid":
        return _pack_mid(p)
    name, transposed = UNIT_WEIGHT[unit]
    return p[name][0].T if transposed else p[name][0]


def _unpack_unit(a, unit):
    if unit != "mid":
        name, transposed = UNIT_WEIGHT[unit]
        return {name: (a.T if transposed else a)[None]}
    seg = lambda n: a[SEG_OFF[n][0]:SEG_OFF[n][0] + SEG_OFF[n][1]]
    return {"w_in": _unpad_w_in(seg("w_in"))[None], "w_out": seg("w_out")[None], "w_mq": seg("w_mq")[None],
            "w_mo": seg("w_mo")[None], "w_mkv": seg("w_mkv").reshape(D_MODEL, 256)[None],
            "w_q_up": seg("w_q")[:24].reshape(96, Q_RANK).T[None],
            "w_kv_up": seg("w_kv").reshape(KV_RANK, 128)[None]}


def _unpack_gathered(full, group):
    if group != "mid":
        return {n: full[k].reshape(-1, D_MODEL) for k, (n, _) in enumerate(GROUP_SEGS[group])}
    full = full[0]
    seg = lambda n: full[:, SEG_OFF[n][0]:SEG_OFF[n][0] + SEG_OFF[n][1]]
    rows = lambda n: seg(n).reshape(-1, D_MODEL)
    wq_t = seg("w_q")[:, :24].reshape(MLA_HEADS, NOPE + ROPE, Q_RANK)
    wq_t = jnp.pad(wq_t, ((0, 0), (0, HEAD_PAD - NOPE - ROPE), (0, 0))).reshape(MLA_HEADS * HEAD_PAD, Q_RANK)
    wkv = seg("w_kv").reshape(N_DEV, KV_RANK, 128).transpose(1, 0, 2).reshape(KV_RANK, N_DEV * 128)
    return {"w_in": rows("w_in"), "w_out": rows("w_out"), "w_mq": rows("w_mq"), "w_mo": rows("w_mo"),
            "w_mkv": seg("w_mkv").reshape(N_DEV, D_MODEL, 256), "w_q": wq_t, "w_kv": wkv}


def _pack_grads(gr):
    blk = lambda a: a.reshape(N_DEV, -1, D_MODEL)
    dwq = gr["w_q"].reshape(MLA_HEADS, HEAD_PAD, Q_RANK)[:, :NOPE + ROPE].reshape(N_DEV, 24, D_MODEL)
    dwq = jnp.pad(dwq, ((0, 0), (0, 8), (0, 0)))
    dwkv = gr["w_kv"].reshape(KV_RANK, N_DEV, 128).transpose(1, 0, 2).reshape(N_DEV, 16, D_MODEL)
    parts = [blk(gr["w_in"]), blk(gr["w_out"]), blk(gr["w_mq"]), blk(gr["w_mo"]),
             gr["w_mkv"].reshape(N_DEV, 256, D_MODEL), dwq, dwkv]
    return jnp.concatenate([a.astype(BF16) for a in parts], axis=1)


def _pack_small(vals):
    parts = []
    for n, r in SMALL_ROWS:
        parts.append(_pad_rows(vals[n].reshape(-1, 128), r) if n in vals else jnp.zeros((r, 128), F32))
    return jnp.concatenate(parts, axis=0)


def _unpack_small(a, shapes):
    out = {}
    for n, shape in shapes.items():
        o = SMALL_OFF[n][0]
        out[n] = a[o:o + int(np.prod(shape)) // 128].reshape(shape)
    return out


BIG_NAMES = ("ffn1_w_gate", "ffn1_w_up", "ffn1_w_down", "w_in", "w_q_up", "w_kv_up", "w_out", "w_mq", "w_mkv",
             "w_mo", "ffn2_w_gate", "ffn2_w_up", "ffn2_w_down")
SMALL_NAMES = ("ffn1_norm", "mix_norm", "q_norm", "kv_norm", "pool_w", "pool_scale", "xattn_norm", "mem_norm",
               "ffn2_norm", "final_norm")
WEIGHT_ORDER = ("ffn1_norm", "ffn1_w_gate", "ffn1_w_up", "ffn1_w_down", "mix_norm", "w_in", "q_norm", "w_q_up",
                "kv_norm", "w_kv_up", "pool_w", "pool_scale", "w_out", "xattn_norm", "mem_norm", "w_mq", "w_mkv",
                "w_mo", "ffn2_norm", "ffn2_w_gate", "ffn2_w_up", "ffn2_w_down", "final_norm")


def _rope_table():
    lane = np.arange(128)
    freqs = (1.0 / (ROPE_BASE ** (np.arange(0, ROPE, 2, dtype=np.float32) / ROPE))).astype(np.float32)
    tab = np.zeros((8, 128), np.float32)
    tab[0] = np.where(lane < ROPE, freqs[lane % (ROPE // 2)], 0.0)
    tab[1] = np.where(lane < ROPE // 2, -1.0, np.where(lane < ROPE, 1.0, 0.0))
    return jnp.asarray(tab)


def kernel(x, mem, positions, ffn1_norm, ffn1_w_gate, ffn1_w_up, ffn1_w_down, mix_norm, w_in, q_norm, w_q_up, kv_norm, w_kv_up, pool_w, pool_scale, w_out, xattn_norm, mem_norm, w_mq, w_mkv, w_mo, ffn2_norm, ffn2_w_gate, ffn2_w_up, ffn2_w_down, final_norm, loss_target, m_ffn1_norm, m_ffn1_w_gate, m_ffn1_w_up, m_ffn1_w_down, m_mix_norm, m_w_in, m_q_norm, m_w_q_up, m_kv_norm, m_w_kv_up, m_pool_w, m_pool_scale, m_w_out, m_xattn_norm, m_mem_norm, m_w_mq, m_w_mkv, m_w_mo, m_ffn2_norm, m_ffn2_w_gate, m_ffn2_w_up, m_ffn2_w_down, m_final_norm, v_ffn1_norm, v_ffn1_w_gate, v_ffn1_w_up, v_ffn1_w_down, v_mix_norm, v_w_in, v_q_norm, v_w_q_up, v_kv_norm, v_w_kv_up, v_pool_w, v_pool_scale, v_w_out, v_xattn_norm, v_mem_norm, v_w_mq, v_w_mkv, v_w_mo, v_ffn2_norm, v_ffn2_w_gate, v_ffn2_w_up, v_ffn2_w_down, v_final_norm):
    wts = dict(ffn1_norm=ffn1_norm, ffn1_w_gate=ffn1_w_gate, ffn1_w_up=ffn1_w_up, ffn1_w_down=ffn1_w_down,
               mix_norm=mix_norm, w_in=w_in, q_norm=q_norm, w_q_up=w_q_up, kv_norm=kv_norm, w_kv_up=w_kv_up,
               pool_w=pool_w, pool_scale=pool_scale, w_out=w_out, xattn_norm=xattn_norm, mem_norm=mem_norm,
               w_mq=w_mq, w_mkv=w_mkv, w_mo=w_mo, ffn2_norm=ffn2_norm, ffn2_w_gate=ffn2_w_gate,
               ffn2_w_up=ffn2_w_up, ffn2_w_down=ffn2_w_down, final_norm=final_norm)
    mom = dict(ffn1_norm=m_ffn1_norm, ffn1_w_gate=m_ffn1_w_gate, ffn1_w_up=m_ffn1_w_up, ffn1_w_down=m_ffn1_w_down,
               mix_norm=m_mix_norm, w_in=m_w_in, q_norm=m_q_norm, w_q_up=m_w_q_up, kv_norm=m_kv_norm,
               w_kv_up=m_w_kv_up, pool_w=m_pool_w, pool_scale=m_pool_scale, w_out=m_w_out, xattn_norm=m_xattn_norm,
               mem_norm=m_mem_norm, w_mq=m_w_mq, w_mkv=m_w_mkv, w_mo=m_w_mo, ffn2_norm=m_ffn2_norm,
               ffn2_w_gate=m_ffn2_w_gate, ffn2_w_up=m_ffn2_w_up, ffn2_w_down=m_ffn2_w_down, final_norm=m_final_norm)
    var = dict(ffn1_norm=v_ffn1_norm, ffn1_w_gate=v_ffn1_w_gate, ffn1_w_up=v_ffn1_w_up, ffn1_w_down=v_ffn1_w_down,
               mix_norm=v_mix_norm, w_in=v_w_in, q_norm=v_q_norm, w_q_up=v_w_q_up, kv_norm=v_kv_norm,
               w_kv_up=v_w_kv_up, pool_w=v_pool_w, pool_scale=v_pool_scale, w_out=v_w_out, xattn_norm=v_xattn_norm,
               mem_norm=v_mem_norm, w_mq=v_w_mq, w_mkv=v_w_mkv, w_mo=v_w_mo, ffn2_norm=v_ffn2_norm,
               ffn2_w_gate=v_ffn2_w_gate, ffn2_w_up=v_ffn2_w_up, ffn2_w_down=v_ffn2_w_down, final_norm=v_final_norm)

    t = x.shape[1]
    xs = x[0]
    mems = mem[0]
    target = loss_target[0]
    pos = positions.reshape(t, 1)
    row = lambda a: a.reshape(1, -1)
    rope_tab = _rope_table()

    cx, cy, cc = _coords()
    chip_idx = (2 * cx + cy).astype(jnp.int32).reshape(1)

    wb = {}
    for grp in ("ffn1", "mid", "ffn2"):
        wb[grp] = _pack_segments(wts, grp).astype(BF16)
        if grp == "ffn1":
            near_ffn1 = _near_start(wb["ffn1"], pos, "ag_ffn1_near_start")
    own_ffn1, land_ffn1 = _near_wait(near_ffn1, wb["ffn2"], "ag_ffn1_near_wait")
    far_ffn1 = _far_start(land_ffn1, own_ffn1, "ag_ffn1_far_start")
    land_ffn1 = _far_wait(far_ffn1, wb["mid"], "ag_ffn1_far_wait")
    full_ffn1 = _core_share(own_ffn1, land_ffn1, "ag_ffn1_share")
    fw = _unpack_gathered(full_ffn1, "ffn1")
    ag_mid = _ici_start(wb["mid"], full_ffn1, "ag_mid_start", True)
    g_ffn1, g_mix, g_q, g_kv = row(ffn1_norm), row(mix_norm), row(q_norm), row(kv_norm)
    g_x, g_mem, g_ffn2, g_fin = row(xattn_norm), row(mem_norm), row(ffn2_norm), row(final_norm)
    pool_wb = pool_w[0].astype(BF16)
    pool_sc = row(pool_scale)

    h1, n1, gate1, up1 = _ffn_fwd(xs, g_ffn1, fw["ffn1_g"], fw["ffn1_u"], fw["ffn1_d"], "ffn1_fwd", token=ag_mid[4])
    own_mid, land_mid = _ici_wait(ag_mid, h1, "ag_mid_wait", True)
    full_mid = _core_share(own_mid, land_mid, "ag_mid_share")
    fw.update(_unpack_gathered(full_mid, "mid"))
    ag_ffn2 = _ici_start(wb["ffn2"], full_mid, "ag_ffn2_start", True)
    u, z, qn, kvn, qh, kh, vh = _mix_prep(h1, g_mix, fw["w_in"], g_q, fw["w_q"], g_kv, fw["w_kv"], pos, rope_tab,
                                          token=ag_ffn2[4])
    a, lse = _attn_fwd(qh, kh, vh)
    p = _pool_fwd(z, pool_wb, pool_sc)
    memn, km, vm = _mem_kv(mems, g_mem, fw["w_mkv"])
    own_ffn2, land_ffn2 = _ici_wait(ag_ffn2, a, "ag_ffn2_wait", True)
    land_ffn2 = lax.dynamic_update_slice(land_ffn2, own_ffn2[:, None], (0, 4 * cx + 2 * cy + cc, 0, 0))
    share_ffn2 = _share_start(land_ffn2, a, "ag_ffn2_share_start")
    h2, h3, hn, qm, om = _xattn_fwd(h1, a, p, fw["w_out"], g_x, fw["w_mq"], km, vm, fw["w_mo"], token=share_ffn2[3])
    fw.update(_unpack_gathered(_share_wait(share_ffn2, h3, "ag_ffn2_share_wait"), "ffn2"))
    dh4, n2, gate2, up2, loss_part, dg_fin = _ffn_fwd(h3, g_ffn2, fw["ffn2_g"], fw["ffn2_u"], fw["ffn2_d"],
                                                      "ffn2_fwd", head=(target, g_fin))

    def reduce_start(g8, unit):
        part = _core_reduce(g8, unit)
        return _ici_start(part, g8, "rs_" + unit + "_start", False)

    def by_device(g):
        return g.reshape(N_DEV, -1, D_MODEL)

    rs = {}
    dh3, dgate2, dup2, act2, dg_ffn2 = _ffn_bwd_data(dh4, h3, g_ffn2, gate2, up2, fw["ffn2_g"], fw["ffn2_u"],
                                                     fw["ffn2_d"], "ffn2_bwd")
    rs["ffn2_g"] = reduce_start(by_device(_tn_matmul(dgate2, n2, "ffn2_dwg", tmm=1408, out_dtype=BF16)), "ffn2_g")
    rs["ffn2_u"] = reduce_start(by_device(_tn_matmul(dup2, n2, "ffn2_dwu", tmm=1408, out_dtype=BF16,
                                                     token=rs["ffn2_g"][4])), "ffn2_u")
    rs["ffn2_d"] = reduce_start(by_device(_tn_matmul(act2, dh4, "ffn2_dwd", scale=0.5, tmm=1408, out_dtype=BF16,
                                                     token=rs["ffn2_u"][4])), "ffn2_d")
    dh2, dqm, da, dp, dkm, dvm, dg_x = _xattn_bwd(dh3, h2, qm, g_x, fw["w_mq"], km, vm, fw["w_mo"], fw["w_out"],
                                                  token=rs["ffn2_d"][4])
    gr = {}
    gr["w_mo"] = _tn_matmul(om, dh3, "dw_mo", out_dtype=BF16)
    gr["w_mq"] = _tn_matmul(hn, dqm, "dw_mq", out_dtype=BF16)
    gr["w_out"] = jnp.concatenate([_tn_matmul(a, dh2, "dw_out_a", out_dtype=BF16),
                                   _tn_matmul(p, dh2, "dw_out_p", out_dtype=BF16)], axis=0)
    gr["w_mkv"], dg_mem = _mem_kv_bwd(dkm, dvm, memn, mems, g_mem, fw["w_mkv"])
    dz_pool, d_pool_w, d_pool_sc = _pool_bwd(dp, z, pool_wb, pool_sc)
    dqh, dkh, dvh = _attn_bwd(qh, kh, vh, da, lse, _attn_delta(a, da))
    dh1, dq, dkv, dz, dg_q, dg_kv, dg_mix = _mla_bwd(dqh, dkh, dvh, z, dz_pool, h1, dh2, g_mix, fw["w_in"], g_q,
                                                     fw["w_q"], g_kv, fw["w_kv"], pos, rope_tab)
    gr["w_q"] = _tn_matmul(dq, qn, "dw_q", out_dtype=BF16)
    gr["w_kv"] = _tn_matmul(kvn, dkv, "dw_kv", out_dtype=BF16)
    gr["w_in"] = _tn_matmul(u, dz, "dw_in", out_dtype=BF16)
    g_mid = _pack_grads(gr)
    part_mid = _core_reduce(g_mid, "mid")
    got = {}
    after = part_mid
    for unit in ("ffn2_g", "ffn2_u", "ffn2_d"):
        got[unit] = _ici_wait(rs[unit], after, "rs_" + unit + "_wait", False)
        after = got[unit][1]
    rs["mid"] = _ici_start(part_mid, after, "rs_mid_start", False)
    dx, dgate1, dup1, act1, dg_ffn1 = _ffn_bwd_data(dh1, xs, g_ffn1, gate1, up1, fw["ffn1_g"], fw["ffn1_u"],
                                                    fw["ffn1_d"], "ffn1_bwd", token=rs["mid"][4])
    got["mid"] = _ici_wait(rs["mid"], dx, "rs_mid_wait", False)

    small_g = dict(ffn1_norm=dg_ffn1, mix_norm=dg_mix, q_norm=dg_q, kv_norm=dg_kv, pool_w=d_pool_w,
                   pool_scale=d_pool_sc, xattn_norm=dg_x, mem_norm=dg_mem, ffn2_norm=dg_ffn2, final_norm=dg_fin,
                   loss=loss_part)
    small_ag = _peers_start(_pack_small(small_g), got["mid"][1], "small_ag_start")
    rs["ffn1_g"] = reduce_start(by_device(_tn_matmul(dgate1, n1, "ffn1_dwg", tmm=1408, out_dtype=BF16,
                                                     token=small_ag[4])), "ffn1_g")
    _, parts = _peers_wait(small_ag, rs["ffn1_g"][4], "small_ag_wait")
    small = _adam_small(parts, _pack_small({n: wts[n] for n in SMALL_NAMES}),
                        _pack_small({n: mom[n] for n in SMALL_NAMES}), _pack_small({n: var[n] for n in SMALL_NAMES}))
    small_sum = small[0]
    loss = small_sum[SMALL_OFF["loss"][0], 0]
    shapes = {n: wts[n].shape for n in SMALL_NAMES}
    small = [_unpack_small(s, shapes) for s in small]

    rs["ffn1_u"] = reduce_start(by_device(_tn_matmul(dup1, n1, "ffn1_dwu", tmm=1408, out_dtype=BF16,
                                                     token=small_sum)), "ffn1_u")
    rs["ffn1_d"] = reduce_start(by_device(_tn_matmul(act1, dh1, "ffn1_dwd", scale=0.5, tmm=1408, out_dtype=BF16,
                                                     token=rs["ffn1_u"][4])), "ffn1_d")

    big = {}

    def adam_unit(unit, token):
        part, land = got[unit]
        res = _adam_big(part, land, _pack_unit(wts, unit), _pack_unit(mom, unit), _pack_unit(var, unit),
                        chip_idx, unit, token)
        for k, packed in enumerate(res):
            big.setdefault(k, {}).update(_unpack_unit(packed, unit))
        return res[0]

    done = rs["ffn1_d"][4]
    for unit in ("mid", "ffn2_g", "ffn2_u", "ffn2_d"):
        done = adam_unit(unit, done)
    for unit in ("ffn1_g", "ffn1_u", "ffn1_d"):
        got[unit] = _ici_wait(rs[unit], done, "rs_" + unit + "_wait", False)
        done = adam_unit(unit, done)

    outs = [loss, dx[None]]
    for k in range(4):
        for n in WEIGHT_ORDER:
            outs.append(big[k][n] if n in BIG_NAMES else small[k][n])
    return tuple(outs)
```

```python
import numpy as np

import jax
import jax.numpy as jnp
from jax import lax
from jax.experimental import pallas as pl
from jax.experimental.pallas import tpu as pltpu

F32 = jnp.float32
BF16 = jnp.bfloat16

N_DEV = 8
D_MODEL = 1024
D_FF = 2816
MLA_HEADS = 4
NOPE = 128
ROPE = 64
HEAD_PAD = 256
V_DIM = 128
Q_RANK = 256
KV_RANK = 128
POOL_WINDOWS = (2, 4, 8, 16)
POOL_CH = 128
POOL_HALO = 16
N_MEM = 256
MEM_HEADS = 4
MEM_HD = 256
ROPE_BASE = 10000.0
RMS_EPS = 1e-6
ATTN_SCALE = (NOPE + ROPE) ** -0.5
MEM_SCALE = MEM_HD ** -0.5
NEG_BIG = -1e30

ADAM_LR = 0.001
ADAM_B1 = 0.9
ADAM_B2 = 0.999
ADAM_EPS = 1e-08
ADAM_WD = 0.01
ADAM_STEP = 10
ADAM_C1 = 1.0 - ADAM_B1 ** ADAM_STEP
ADAM_C2 = 1.0 - ADAM_B2 ** ADAM_STEP

VMEM_LIMIT_BYTES = 56 * 1024 * 1024
BF16_ROWS = 16

GROUP_SEGS = {
    "ffn1": (("ffn1_g", 352), ("ffn1_u", 352), ("ffn1_d", 352)),
    "mid": (("w_in", 128), ("w_out", 128), ("w_mq", 128), ("w_mo", 128), ("w_mkv", 256), ("w_q", 32), ("w_kv", 16)),
    "ffn2": (("ffn2_g", 352), ("ffn2_u", 352), ("ffn2_d", 352)),
}
SEG_OFF = {}
GROUP_ROWS = {}
for _g, _segs in GROUP_SEGS.items():
    _o = 0
    for _n, _r in _segs:
        SEG_OFF[_n] = (_o, _r)
        _o += _r
    GROUP_ROWS[_g] = _o

SMALL_ROWS = (("ffn1_norm", 8), ("mix_norm", 8), ("q_norm", 8), ("kv_norm", 8), ("pool_w", 512), ("pool_scale", 8),
              ("xattn_norm", 8), ("mem_norm", 8), ("ffn2_norm", 8), ("final_norm", 8), ("loss", 8))
SMALL_OFF = {}
_o = 0
for _n, _r in SMALL_ROWS:
    SMALL_OFF[_n] = (_o, _r)
    _o += _r


def _cparams(**kw):
    return pltpu.CompilerParams(vmem_limit_bytes=VMEM_LIMIT_BYTES, **kw)


def _row_tile(rows, limit):
    best = None
    for cand in range(BF16_ROWS, min(rows, limit) + 1, BF16_ROWS):
        if rows % cand == 0:
            best = cand
    assert best is not None, rows
    return best


def _dot_nn(a, b):
    return lax.dot_general(a, b, (((1,), (0,)), ((), ())), preferred_element_type=F32)


def _dot_nt(a, b):
    return lax.dot_general(a, b, (((1,), (1,)), ((), ())), preferred_element_type=F32)


def _dot_tn(a, b):
    return lax.dot_general(a, b, (((0,), (0,)), ((), ())), preferred_element_type=F32)


def _rms_fwd(x, g):
    r = lax.rsqrt(jnp.mean(x * x, axis=-1, keepdims=True) + RMS_EPS)
    return x * r * g, r


def _rms_bwd(dy, x, g, r):
    xhat = x * r
    dyg = dy * g
    dx = r * (dyg - xhat * jnp.mean(dyg * xhat, axis=-1, keepdims=True))
    dg = jnp.sum(dy * xhat, axis=0, keepdims=True)
    return dx, dg


def _accumulate(ref, val, first):
    if isinstance(first, bool):
        if first:
            ref[...] = val
        else:
            ref[...] += val
        return

    @pl.when(first)
    def _():
        ref[...] = val

    @pl.when(jnp.logical_not(first))
    def _():
        ref[...] += val


def _call_after(token, body, in_specs, args, **kw):
    if token is not None:
        inner = body
        body = lambda tok_ref, *refs: inner(*refs)
        in_specs = [pl.BlockSpec((8, 128), lambda *_: (0, 0))] + list(in_specs)
        args = (token,) + tuple(args)
    return pl.pallas_call(body, in_specs=in_specs, **kw)(*args)


def _resident(shape):
    return pl.BlockSpec(shape, lambda *_: (0,) * len(shape), pipeline_mode=pl.Buffered(1))


def _rope_tables(pos_col, tab):
    ang = pos_col.astype(F32) * tab[0:1, :]
    return jnp.cos(ang), jnp.sin(ang) * tab[1:2, :]


def _swap_halves(x):
    lane = lax.broadcasted_iota(jnp.int32, x.shape, 1)
    return jnp.where((lane % 64) < 32, pltpu.roll(x, 96, 1), pltpu.roll(x, 32, 1))


def _rope_apply(x, cos_t, sin_t):
    return x * cos_t + _swap_halves(x) * sin_t


def _rope_apply_t(dy, cos_t, sin_t):
    return dy * cos_t + _swap_halves(dy * sin_t)


def _ffn_fwd(h, g, wg_t, wu_t, wd, name, token=None, head=None):
    t, d = h.shape
    f = wg_t.shape[0]
    tm, tf = min(512, t), 256
    nf = f // tf
    n_in = 5 if head is None else 7

    def body(*refs):
        h_ref, g_ref, wg_ref, wu_ref, wd_ref = refs[:5]
        ho_ref, n_ref, gate_ref, up_ref = refs[n_in:n_in + 4]
        nb_sc, acc_sc = refs[-2:]
        y, _ = _rms_fwd(h_ref[...], g_ref[...])
        nb = y.astype(BF16)
        nb_sc[...] = nb
        n_ref[...] = nb
        acc_sc[...] = jnp.zeros_like(acc_sc)

        def f_tile(j):
            rows = pl.ds(pl.multiple_of(j * tf, tf), tf)
            nb = nb_sc[...]
            gt = _dot_nt(nb, wg_ref[rows, :])
            ut = _dot_nt(nb, wu_ref[rows, :])
            gate_ref[j] = gt.astype(BF16)
            up_ref[j] = ut.astype(BF16)
            act = (gt * jax.nn.sigmoid(gt)) * ut
            return _dot_nn(act.astype(BF16), wd_ref[rows, :])

        def pair(p, carry):
            acc_sc[...] += f_tile(2 * p) + f_tile(2 * p + 1)
            return carry

        lax.fori_loop(0, nf // 2, pair, 0)
        if nf % 2:
            acc_sc[...] += f_tile(nf - 1)
        ho = h_ref[...] + 0.5 * acc_sc[...]
        if head is None:
            ho_ref[...] = ho
            return
        t_ref, gf_ref = refs[5:7]
        loss_ref, dgf_ref = refs[n_in + 4:n_in + 6]
        gg = gf_ref[...]
        y, r = _rms_fwd(ho, gg)
        err = y - t_ref[...]
        part = 0.5 * jnp.sum(jnp.mean(err * err, axis=-1, keepdims=True), axis=0, keepdims=True)
        dx, dg = _rms_bwd(err * (1.0 / d), ho, gg, r)
        ho_ref[...] = dx
        first = pl.program_id(0) == 0
        _accumulate(loss_ref, jnp.broadcast_to(part, loss_ref.shape), first)
        _accumulate(dgf_ref, dg, first)

    row = pl.BlockSpec((tm, d), lambda i: (i, 0))
    tiles = pl.BlockSpec((nf, tm, tf), lambda i: (0, i, 0))
    in_specs = [row, _resident((1, d)), _resident((f, d)), _resident((f, d)), _resident((f, d))]
    args = (h, g, wg_t, wu_t, wd)
    out_specs = [row, row, tiles, tiles]
    out_shape = [jax.ShapeDtypeStruct((t, d), F32), jax.ShapeDtypeStruct((t, d), BF16),
                 jax.ShapeDtypeStruct((nf, t, tf), BF16), jax.ShapeDtypeStruct((nf, t, tf), BF16)]
    if head is not None:
        in_specs += [row, _resident((1, d))]
        args += tuple(head)
        out_specs += [pl.BlockSpec((8, 128), lambda i: (0, 0)), pl.BlockSpec((1, d), lambda i: (0, 0))]
        out_shape += [jax.ShapeDtypeStruct((8, 128), F32), jax.ShapeDtypeStruct((1, d), F32)]
    return _call_after(
        token, body, in_specs, args, name=name, grid=(t // tm,), out_specs=out_specs, out_shape=out_shape,
        scratch_shapes=[pltpu.VMEM((tm, d), BF16), pltpu.VMEM((tm, d), F32)],
        compiler_params=_cparams(),
    )


def _ffn_bwd_data(dho, h, g, gate, up, wg_t, wu_t, wd, name, token=None):
    t, d = h.shape
    f = wg_t.shape[0]
    tm, tf = min(1024, t), 256
    parts = 2 if tm % 512 == 0 else 1
    tp = tm // parts
    nf = f // tf

    def body(dho_ref, h_ref, g_ref, gate_ref, up_ref, wg_ref, wu_ref, wd_ref,
             dh_ref, dgate_ref, dup_ref, act_ref, dg_ref, dhb_sc, acc_sc):
        i, j = pl.program_id(0), pl.program_id(1)

        @pl.when(j == 0)
        def _():
            dhb_sc[...] = (0.5 * dho_ref[...]).astype(BF16)
            acc_sc[...] = jnp.zeros_like(acc_sc)

        for r in range(parts):
            rows = pl.ds(r * tp, tp)
            dact = _dot_nt(dhb_sc[rows, :], wd_ref[...])
            gt = gate_ref[0, rows, :].astype(F32)
            ut = up_ref[0, rows, :].astype(F32)
            sg = jax.nn.sigmoid(gt)
            silu = gt * sg
            dgb = (dact * ut * (sg * (1.0 + gt * (1.0 - sg)))).astype(BF16)
            dub = (dact * silu).astype(BF16)
            act_ref[rows, :] = (silu * ut).astype(BF16)
            dgate_ref[rows, :] = dgb
            dup_ref[rows, :] = dub
            acc_sc[rows, :] += _dot_nn(dgb, wg_ref[...]) + _dot_nn(dub, wu_ref[...])

        @pl.when(j == nf - 1)
        def _():
            x = h_ref[...]
            gg = g_ref[...]
            _, r = _rms_fwd(x, gg)
            dx, dg = _rms_bwd(acc_sc[...], x, gg, r)
            dh_ref[...] = dho_ref[...] + dx
            _accumulate(dg_ref, dg, i == 0)

    return _call_after(
        token, body,
        [pl.BlockSpec((tm, d), lambda i, j: (i, 0)),
         pl.BlockSpec((tm, d), lambda i, j: (i, 0)),
         pl.BlockSpec((1, d), lambda i, j: (0, 0)),
         pl.BlockSpec((1, tm, tf), lambda i, j: (j, i, 0)),
         pl.BlockSpec((1, tm, tf), lambda i, j: (j, i, 0)),
         pl.BlockSpec((tf, d), lambda i, j: (j, 0)),
         pl.BlockSpec((tf, d), lambda i, j: (j, 0)),
         pl.BlockSpec((tf, d), lambda i, j: (j, 0))],
        (dho, h, g, gate, up, wg_t, wu_t, wd),
        name=name, grid=(t // tm, nf),
        out_specs=[pl.BlockSpec((tm, d), lambda i, j: (i, 0)),
                   pl.BlockSpec((tm, tf), lambda i, j: (i, j)),
                   pl.BlockSpec((tm, tf), lambda i, j: (i, j)),
                   pl.BlockSpec((tm, tf), lambda i, j: (i, j)),
                   pl.BlockSpec((1, d), lambda i, j: (0, 0))],
        out_shape=[jax.ShapeDtypeStruct((t, d), F32), jax.ShapeDtypeStruct((t, f), BF16),
                   jax.ShapeDtypeStruct((t, f), BF16), jax.ShapeDtypeStruct((t, f), BF16),
                   jax.ShapeDtypeStruct((1, d), F32)],
        scratch_shapes=[pltpu.VMEM((tm, d), BF16), pltpu.VMEM((tm, d), F32)],
        compiler_params=_cparams(),
    )


def _tn_matmul(a, b, name, scale=1.0, tmm=None, out_dtype=F32, token=None):
    t, m = a.shape
    n = b.shape[1]
    tmm = m if tmm is None else tmm
    tk = min(1024, t)
    nk = t // tk

    def product(a_ref, b_ref):
        prod = _dot_tn(a_ref[...].astype(BF16), b_ref[...].astype(BF16))
        return prod * scale if scale != 1.0 else prod

    def body_f32(a_ref, b_ref, o_ref):
        _accumulate(o_ref, product(a_ref, b_ref), pl.program_id(1) == 0)

    def body_cast(a_ref, b_ref, o_ref, acc_sc):
        k = pl.program_id(1)
        _accumulate(acc_sc, product(a_ref, b_ref), k == 0)

        @pl.when(k == nk - 1)
        def _():
            o_ref[...] = acc_sc[...].astype(out_dtype)

    direct = out_dtype == F32
    return _call_after(
        token, body_f32 if direct else body_cast,
        [pl.BlockSpec((tk, tmm), lambda i, k: (k, i)),
         pl.BlockSpec((tk, n), lambda i, k: (k, 0))],
        (a, b),
        name=name, grid=(m // tmm, nk),
        out_specs=pl.BlockSpec((tmm, n), lambda i, k: (i, 0)),
        out_shape=jax.ShapeDtypeStruct((m, n), out_dtype),
        scratch_shapes=[] if direct else [pltpu.VMEM((tmm, n), F32)],
        compiler_params=_cparams(),
    )


def _mix_prep(h1, mix_norm, w_in, q_norm, wq_t, kv_norm, wkv, pos, rope_tab, token=None):
    t, d = h1.shape
    tm = min(512, t)

    def body(h_ref, gm_ref, win_ref, gq_ref, wq_ref, gkv_ref, wkv_ref, pos_ref, tab_ref,
             u_ref, z_ref, qn_ref, kvn_ref, q_ref, k_ref, v_ref):
        u, _ = _rms_fwd(h_ref[...], gm_ref[...])
        ub = u.astype(BF16)
        u_ref[...] = ub
        z = _dot_nn(ub, win_ref[...])
        z_ref[...] = z
        cos_t, sin_t = _rope_tables(pos_ref[...], tab_ref[...])
        qn, _ = _rms_fwd(z[:, 0:Q_RANK], gq_ref[...])
        qnb = qn.astype(BF16)
        qn_ref[...] = qnb
        q = _dot_nt(qnb, wq_ref[...])
        kvn, _ = _rms_fwd(z[:, Q_RANK:Q_RANK + KV_RANK], gkv_ref[...])
        kvnb = kvn.astype(BF16)
        kvn_ref[...] = kvnb
        kv = _dot_nn(kvnb, wkv_ref[...])
        k_pe = _rope_apply(z[:, Q_RANK + KV_RANK:Q_RANK + KV_RANK + 128], cos_t, sin_t)
        ones = jnp.ones((tm, V_DIM), F32)
        for hh in range(MLA_HEADS):
            b = hh * HEAD_PAD
            q_pe = _rope_apply(q[:, b + NOPE:b + HEAD_PAD], cos_t, sin_t)
            q_ref[hh] = (jnp.concatenate([q[:, b:b + NOPE], q_pe], axis=-1) * ATTN_SCALE).astype(BF16)
            k_ref[hh] = jnp.concatenate([kv[:, b:b + NOPE], k_pe], axis=-1).astype(BF16)
            v_ref[hh] = jnp.concatenate([kv[:, b + NOPE:b + HEAD_PAD], ones], axis=-1).astype(BF16)

    full = lambda shape: pl.BlockSpec(shape, lambda i: (0,) * len(shape))
    return _call_after(
        token, body,
        [pl.BlockSpec((tm, d), lambda i: (i, 0)), _resident((1, d)), _resident(w_in.shape), _resident((1, Q_RANK)),
         _resident(wq_t.shape), _resident((1, KV_RANK)), _resident(wkv.shape),
         pl.BlockSpec((tm, 1), lambda i: (i, 0)), _resident(rope_tab.shape)],
        (h1, mix_norm, w_in, q_norm, wq_t, kv_norm, wkv, pos, rope_tab),
        name="mix_prep", grid=(t // tm,),
        out_specs=[pl.BlockSpec((tm, d), lambda i: (i, 0)),
                   pl.BlockSpec((tm, d), lambda i: (i, 0)),
                   pl.BlockSpec((tm, Q_RANK), lambda i: (i, 0)),
                   pl.BlockSpec((tm, KV_RANK), lambda i: (i, 0)),
                   pl.BlockSpec((MLA_HEADS, tm, HEAD_PAD), lambda i: (0, i, 0)),
                   pl.BlockSpec((MLA_HEADS, tm, HEAD_PAD), lambda i: (0, i, 0)),
                   pl.BlockSpec((MLA_HEADS, tm, 2 * V_DIM), lambda i: (0, i, 0))],
        out_shape=[jax.ShapeDtypeStruct((t, d), BF16), jax.ShapeDtypeStruct((t, d), F32),
                   jax.ShapeDtypeStruct((t, Q_RANK), BF16), jax.ShapeDtypeStruct((t, KV_RANK), BF16),
                   jax.ShapeDtypeStruct((MLA_HEADS, t, HEAD_PAD), BF16),
                   jax.ShapeDtypeStruct((MLA_HEADS, t, HEAD_PAD), BF16),
                   jax.ShapeDtypeStruct((MLA_HEADS, t, 2 * V_DIM), BF16)],
        compiler_params=_cparams(),
    )


def _causal_mask(s):
    row = lax.broadcasted_iota(jnp.int32, s.shape, 0)
    col = lax.broadcasted_iota(jnp.int32, s.shape, 1)
    return jnp.where(col <= row, s, NEG_BIG)


def _attn_fwd(q, k, v):
    nh, t, _ = q.shape
    tq = tk = min(512, t)
    nq, nk = t // tq, t // tk

    pairs = [(i, j) for i in range(nq) for j in range(i + 1)]
    qi = jnp.asarray(np.array([i for i, _ in pairs], np.int32))
    kj = jnp.asarray(np.array([j for _, j in pairs], np.int32))

    def body(qi_ref, kj_ref, q_ref, k_ref, v_ref, o_ref, lse_ref, m_sc, acc_sc):
        n = pl.program_id(0)
        i, j = qi_ref[n], kj_ref[n]

        @pl.when(j == 0)
        def _():
            m_sc[...] = jnp.full_like(m_sc, NEG_BIG)
            acc_sc[...] = jnp.zeros_like(acc_sc)

        def step(diagonal):
            for hh in range(nh):
                s = _dot_nt(q_ref[hh], k_ref[hh])
                if diagonal:
                    s = _causal_mask(s)
                m_old = m_sc[hh]
                m_new = jnp.maximum(m_old, jnp.max(s, axis=-1, keepdims=True))
                p = jnp.exp(s - m_new).astype(BF16)
                acc_sc[hh] = jnp.exp(m_old - m_new) * acc_sc[hh] + _dot_nn(p, v_ref[hh])
                m_sc[hh] = m_new

        @pl.when(j < i)
        def _():
            step(False)

        @pl.when(j == i)
        def _():
            step(True)
            for hh in range(nh):
                acc = acc_sc[hh]
                l = acc[:, V_DIM:2 * V_DIM]
                o_ref[:, hh * V_DIM:(hh + 1) * V_DIM] = (acc[:, 0:V_DIM] / l).astype(BF16)
                lse_ref[hh] = m_sc[hh] + jnp.log(l[:, 0:1])

    q_map = lambda n, qi_ref, kj_ref: (0, qi_ref[n], 0)
    kv_map = lambda n, qi_ref, kj_ref: (0, kj_ref[n], 0)
    return pl.pallas_call(
        body, name="attn_fwd",
        grid_spec=pltpu.PrefetchScalarGridSpec(
            num_scalar_prefetch=2, grid=(len(pairs),),
            in_specs=[pl.BlockSpec((nh, tq, HEAD_PAD), q_map),
                      pl.BlockSpec((nh, tk, HEAD_PAD), kv_map),
                      pl.BlockSpec((nh, tk, 2 * V_DIM), kv_map)],
            out_specs=[pl.BlockSpec((tq, nh * V_DIM), lambda n, qi_ref, kj_ref: (qi_ref[n], 0)),
                       pl.BlockSpec((nh, tq, 1), q_map)],
            scratch_shapes=[pltpu.VMEM((nh, tq, 1), F32), pltpu.VMEM((nh, tq, 2 * V_DIM), F32)]),
        out_shape=[jax.ShapeDtypeStruct((t, nh * V_DIM), BF16), jax.ShapeDtypeStruct((nh, t, 1), F32)],
        compiler_params=_cparams(),
    )(qi, kj, q, k, v)


def _attn_delta(o, do):
    t, w = o.shape
    nh = w // V_DIM
    tm = min(512, t)

    def body(o_ref, do_ref, d_ref):
        prod = o_ref[...].astype(F32) * do_ref[...].astype(F32)
        for hh in range(nh):
            d_ref[hh] = jnp.sum(prod[:, hh * V_DIM:(hh + 1) * V_DIM], axis=-1, keepdims=True)

    return pl.pallas_call(
        body, name="attn_delta", grid=(t // tm,),
        in_specs=[pl.BlockSpec((tm, w), lambda i: (i, 0)), pl.BlockSpec((tm, w), lambda i: (i, 0))],
        out_specs=pl.BlockSpec((nh, tm, 1), lambda i: (0, i, 0)),
        out_shape=jax.ShapeDtypeStruct((nh, t, 1), F32),
        compiler_params=_cparams(),
    )(o, do)


ATTN_BWD_HEADS = 2


def _attn_bwd(q, k, v, do, lse, delta):
    nh, t, _ = q.shape
    hp = ATTN_BWD_HEADS
    tq = tk = min(512, t)
    nq, nk = t // tq, t // tk

    pairs = [(j, i) for j in range(nk) for i in range(j, nq)]
    kj = jnp.asarray(np.array([j for j, _ in pairs], np.int32))
    qi = jnp.asarray(np.array([i for _, i in pairs], np.int32))

    def body(kj_ref, qi_ref, q_ref, k_ref, v_ref, do_ref, lse_ref, dlt_ref, dq_ref, dk_ref, dv_ref):
        n = pl.program_id(1)
        j, i = kj_ref[n], qi_ref[n]

        @pl.when(n == 0)
        def _():
            dq_ref[...] = jnp.zeros_like(dq_ref)

        def step(diagonal):
            for hh in range(hp):
                qq, kk = q_ref[hh], k_ref[hh]
                dob = do_ref[:, hh * V_DIM:(hh + 1) * V_DIM]
                s = _dot_nt(qq, kk)
                if diagonal:
                    s = _causal_mask(s)
                p = jnp.exp(s - lse_ref[hh])
                dpp = _dot_nt(dob, v_ref[hh])
                dsb = (p * (dpp - dlt_ref[hh])).astype(BF16)
                _accumulate(dv_ref.at[hh], _dot_tn(p.astype(BF16), dob), diagonal)
                _accumulate(dk_ref.at[hh], _dot_tn(dsb, qq), diagonal)
                dq_ref[hh, pl.ds(pl.multiple_of(i * tq, tq), tq), :] += _dot_nn(dsb, kk)

        @pl.when(i > j)
        def _():
            step(False)

        @pl.when(i == j)
        def _():
            step(True)

    q_map = lambda h, n, kj_ref, qi_ref: (h, qi_ref[n], 0)
    k_map = lambda h, n, kj_ref, qi_ref: (h, kj_ref[n], 0)
    return pl.pallas_call(
        body, name="attn_bwd",
        grid_spec=pltpu.PrefetchScalarGridSpec(
            num_scalar_prefetch=2, grid=(nh // hp, len(pairs)),
            in_specs=[pl.BlockSpec((hp, tq, HEAD_PAD), q_map),
                      pl.BlockSpec((hp, tk, HEAD_PAD), k_map),
                      pl.BlockSpec((hp, tk, V_DIM), k_map),
                      pl.BlockSpec((tq, hp * V_DIM), lambda h, n, kj_ref, qi_ref: (qi_ref[n], h)),
                      pl.BlockSpec((hp, tq, 1), q_map),
                      pl.BlockSpec((hp, tq, 1), q_map)],
            out_specs=[pl.BlockSpec((hp, t, HEAD_PAD), lambda h, n, kj_ref, qi_ref: (h, 0, 0)),
                       pl.BlockSpec((hp, tk, HEAD_PAD), k_map),
                       pl.BlockSpec((hp, tk, V_DIM), k_map)]),
        out_shape=[jax.ShapeDtypeStruct((nh, t, HEAD_PAD), F32), jax.ShapeDtypeStruct((nh, t, HEAD_PAD), F32),
                   jax.ShapeDtypeStruct((nh, t, V_DIM), F32)],
        compiler_params=_cparams(),
    )(kj, qi, q, k, v, do, lse, delta)


def _pool_counts(first_token, rows, w):
    tok = lax.broadcasted_iota(jnp.int32, (rows, POOL_CH), 0) + first_token
    return jnp.minimum(tok + 1, w).astype(F32)


def _pool_centered(zbuf, g, w, i, tm):
    lanes = pl.ds(g * POOL_CH, POOL_CH)
    cur = zbuf[pl.ds(POOL_HALO, tm), lanes]
    win = cur
    for s in range(1, w):
        win = win + zbuf[pl.ds(POOL_HALO - s, tm), lanes]
    return win / _pool_counts(i * tm, tm, w) - cur


def _pool_load(zbuf, z_ref, halo_ref, i, tm):
    @pl.when(i == 0)
    def _():
        zbuf[pl.ds(0, POOL_HALO), :] = jnp.zeros((POOL_HALO, zbuf.shape[1]), F32)

    @pl.when(i > 0)
    def _():
        zbuf[pl.ds(0, POOL_HALO), :] = halo_ref[...]

    zbuf[pl.ds(POOL_HALO, tm), :] = z_ref[...]


def _pool_fwd(z, pool_w, pool_scale):
    t = z.shape[0]
    pw = len(POOL_WINDOWS) * POOL_CH
    tm = min(512, t)
    hb = tm // POOL_HALO

    def body(z_ref, halo_ref, w_ref, sc_ref, p_ref, zbuf):
        i = pl.program_id(0)
        _pool_load(zbuf, z_ref, halo_ref, i, tm)
        for g, w in enumerate(POOL_WINDOWS):
            c = _pool_centered(zbuf, g, w, i, tm)
            y = _dot_nn(c.astype(BF16), w_ref[g]) * sc_ref[:, g * POOL_CH:(g + 1) * POOL_CH]
            p_ref[:, g * POOL_CH:(g + 1) * POOL_CH] = y.astype(BF16)

    return pl.pallas_call(
        body, name="pool_fwd", grid=(t // tm,),
        in_specs=[pl.BlockSpec((tm, pw), lambda i: (i, 1)),
                  pl.BlockSpec((POOL_HALO, pw), lambda i: (jnp.maximum(i * hb - 1, 0), 1)),
                  pl.BlockSpec(pool_w.shape, lambda i: (0, 0, 0)),
                  pl.BlockSpec((1, pw), lambda i: (0, 0))],
        out_specs=pl.BlockSpec((tm, pw), lambda i: (i, 0)),
        out_shape=jax.ShapeDtypeStruct((t, pw), BF16),
        scratch_shapes=[pltpu.VMEM((POOL_HALO + tm, pw), F32)],
        compiler_params=_cparams(),
    )(z, z, pool_w, pool_scale)


def _pool_bwd(dp, z, pool_w, pool_scale):
    t = z.shape[0]
    ng = len(POOL_WINDOWS)
    pw = ng * POOL_CH
    tm = min(512, t)
    hb = tm // POOL_HALO
    nt = t // tm

    def body(dp_ref, dpn_ref, z_ref, halo_ref, w_ref, sc_ref, dz_ref, dw_ref, dsc_ref, zbuf, dbuf):
        i = pl.program_id(0)
        _pool_load(zbuf, z_ref, halo_ref, i, tm)

        @pl.when(i == 0)
        def _():
            dw_ref[...] = jnp.zeros_like(dw_ref)
            dsc_ref[...] = jnp.zeros_like(dsc_ref)

        nxt_ok = (i < nt - 1).astype(F32)
        for g, w in enumerate(POOL_WINDOWS):
            lanes = pl.ds(g * POOL_CH, POOL_CH)
            cols = slice(g * POOL_CH, (g + 1) * POOL_CH)
            sc = sc_ref[:, cols]
            wg = w_ref[g]
            c = _pool_centered(zbuf, g, w, i, tm).astype(BF16)
            ypre = _dot_nn(c, wg)
            dpg = dp_ref[:, cols].astype(F32)
            dsc_ref[:, cols] += jnp.sum(dpg * ypre, axis=0, keepdims=True)
            dyb = (dpg * sc).astype(BF16)
            dw_ref[g] += _dot_tn(c, dyb)
            dd = _dot_nt(dyb, wg)
            dyn = (dpn_ref[:, cols].astype(F32) * sc).astype(BF16)
            ddn = _dot_nt(dyn, wg) * nxt_ok
            dbuf[pl.ds(0, tm), lanes] = dd / _pool_counts(i * tm, tm, w)
            dbuf[pl.ds(tm, POOL_HALO), lanes] = ddn / _pool_counts((i + 1) * tm, POOL_HALO, w)
            acc = -dd
            for s in range(w):
                acc = acc + dbuf[pl.ds(s, tm), lanes]
            dz_ref[:, cols] = acc

    return pl.pallas_call(
        body, name="pool_bwd", grid=(nt,),
        in_specs=[pl.BlockSpec((tm, pw), lambda i: (i, 0)),
                  pl.BlockSpec((POOL_HALO, pw), lambda i: (jnp.minimum((i + 1) * hb, t // POOL_HALO - 1), 0)),
                  pl.BlockSpec((tm, pw), lambda i: (i, 1)),
                  pl.BlockSpec((POOL_HALO, pw), lambda i: (jnp.maximum(i * hb - 1, 0), 1)),
                  pl.BlockSpec(pool_w.shape, lambda i: (0, 0, 0)),
                  pl.BlockSpec((1, pw), lambda i: (0, 0))],
        out_specs=[pl.BlockSpec((tm, pw), lambda i: (i, 0)),
                   pl.BlockSpec((ng, POOL_CH, POOL_CH), lambda i: (0, 0, 0)),
                   pl.BlockSpec((1, pw), lambda i: (0, 0))],
        out_shape=[jax.ShapeDtypeStruct((t, pw), F32), jax.ShapeDtypeStruct((ng, POOL_CH, POOL_CH), F32),
                   jax.ShapeDtypeStruct((1, pw), F32)],
        scratch_shapes=[pltpu.VMEM((POOL_HALO + tm, pw), F32), pltpu.VMEM((tm + POOL_HALO, pw), F32)],
        compiler_params=_cparams(),
    )(dp, dp, z, z, pool_w, pool_scale)


def _mla_bwd(dq_h, dk_h, dv_h, z, dz_pool, h1, dh2, mix_norm, w_in, q_norm, wq_t, kv_norm, wkv, pos, rope_tab):
    t, d = h1.shape
    tm = min(512, t)

    def body(dqh_ref, dkh_ref, dvh_ref, z_ref, dzp_ref, h_ref, dh2_ref, gm_ref, win_ref, gq_ref, wq_ref, gkv_ref,
             wkv_ref, pos_ref, tab_ref, dh1_ref, dq_ref, dkv_ref, dz_ref, dgq_ref, dgkv_ref, dgm_ref):
        i = pl.program_id(0)
        first = i == 0
        cos_t, sin_t = _rope_tables(pos_ref[...], tab_ref[...])
        dq_parts, dkv_parts = [], []
        dk_pe = jnp.zeros((tm, 128), F32)
        for hh in range(MLA_HEADS):
            dqh = dqh_ref[hh] * ATTN_SCALE
            dq_parts += [dqh[:, 0:NOPE], _rope_apply_t(dqh[:, NOPE:HEAD_PAD], cos_t, sin_t)]
            dkh = dkh_ref[hh]
            dkv_parts += [dkh[:, 0:NOPE], dvh_ref[hh]]
            dk_pe = dk_pe + dkh[:, NOPE:HEAD_PAD]
        dqb = jnp.concatenate(dq_parts, axis=-1).astype(BF16)
        dkvb = jnp.concatenate(dkv_parts, axis=-1).astype(BF16)
        dq_ref[...] = dqb
        dkv_ref[...] = dkvb
        z = z_ref[...]
        c_q = z[:, 0:Q_RANK]
        gq = gq_ref[...]
        _, rq = _rms_fwd(c_q, gq)
        dcq, dgq = _rms_bwd(_dot_nn(dqb, wq_ref[...]), c_q, gq, rq)
        c_kv = z[:, Q_RANK:Q_RANK + KV_RANK]
        gkv = gkv_ref[...]
        _, rkv = _rms_fwd(c_kv, gkv)
        dckv, dgkv = _rms_bwd(_dot_nt(dkvb, wkv_ref[...]), c_kv, gkv, rkv)
        dkr = _rope_apply_t(dk_pe, cos_t, sin_t)
        dzb = jnp.concatenate([dcq, dckv, dkr, dzp_ref[...]], axis=-1).astype(BF16)
        dz_ref[...] = dzb
        x = h_ref[...]
        gm = gm_ref[...]
        _, rm = _rms_fwd(x, gm)
        dx, dgm = _rms_bwd(_dot_nt(dzb, win_ref[...]), x, gm, rm)
        dh1_ref[...] = dh2_ref[...] + dx
        _accumulate(dgq_ref, dgq, first)
        _accumulate(dgkv_ref, dgkv, first)
        _accumulate(dgm_ref, dgm, first)

    full = lambda shape: pl.BlockSpec(shape, lambda i: (0,) * len(shape))
    row = lambda w: pl.BlockSpec((tm, w), lambda i: (i, 0))
    head = lambda w: pl.BlockSpec((MLA_HEADS, tm, w), lambda i: (0, i, 0))
    pw = len(POOL_WINDOWS) * POOL_CH
    return pl.pallas_call(
        body, name="mla_bwd", grid=(t // tm,),
        in_specs=[head(HEAD_PAD), head(HEAD_PAD), head(V_DIM), row(d), row(pw), row(d), row(d),
                  _resident((1, d)), _resident(w_in.shape), _resident((1, Q_RANK)), _resident(wq_t.shape),
                  _resident((1, KV_RANK)), _resident(wkv.shape), row(1), _resident(rope_tab.shape)],
        out_specs=[row(d), row(d), row(d), row(d), full((1, Q_RANK)), full((1, KV_RANK)), full((1, d))],
        out_shape=[jax.ShapeDtypeStruct((t, d), F32), jax.ShapeDtypeStruct((t, d), BF16),
                   jax.ShapeDtypeStruct((t, d), BF16), jax.ShapeDtypeStruct((t, d), BF16),
                   jax.ShapeDtypeStruct((1, Q_RANK), F32), jax.ShapeDtypeStruct((1, KV_RANK), F32),
                   jax.ShapeDtypeStruct((1, d), F32)],
        compiler_params=_cparams(),
    )(dq_h, dk_h, dv_h, z, dz_pool, h1, dh2, mix_norm, w_in, q_norm, wq_t, kv_norm, wkv, pos, rope_tab)


def _mem_kv(mem, mem_norm, wmkv):
    n, d = mem.shape

    def body(mem_ref, g_ref, w_ref, memn_ref, k_ref, v_ref):
        y, _ = _rms_fwd(mem_ref[...], g_ref[...])
        yb = y.astype(BF16)
        memn_ref[...] = yb
        for hh in range(MEM_HEADS):
            k_ref[hh] = _dot_nn(yb, w_ref[hh]).astype(BF16)
            v_ref[hh] = _dot_nn(yb, w_ref[MEM_HEADS + hh]).astype(BF16)

    return pl.pallas_call(
        body, name="mem_kv",
        out_shape=[jax.ShapeDtypeStruct((n, d), BF16), jax.ShapeDtypeStruct((MEM_HEADS, n, MEM_HD), BF16),
                   jax.ShapeDtypeStruct((MEM_HEADS, n, MEM_HD), BF16)],
        compiler_params=_cparams(),
    )(mem, mem_norm, wmkv)


def _mem_softmax(qb, km):
    s = _dot_nt(qb, km)
    e = jnp.exp(s - jnp.max(s, axis=-1, keepdims=True))
    return e / jnp.sum(e, axis=-1, keepdims=True)


def _xattn_fwd(h1, a, p, w_out, g, wmq, km, vm, wmo, token=None):
    t, d = h1.shape
    tm = min(512, t)
    half = a.shape[1]

    def body(h_ref, a_ref, p_ref, wo_ref, g_ref, wmq_ref, km_ref, vm_ref, wmo_ref,
             h2_ref, h3_ref, hn_ref, q_ref, o_ref):
        h2 = h_ref[...] + _dot_nn(a_ref[...], wo_ref[0:half, :]) + _dot_nn(p_ref[...], wo_ref[half:2 * half, :])
        h2_ref[...] = h2
        hn, _ = _rms_fwd(h2, g_ref[...])
        hnb = hn.astype(BF16)
        hn_ref[...] = hnb
        qb = (_dot_nn(hnb, wmq_ref[...]) * MEM_SCALE).astype(BF16)
        q_ref[...] = qb
        outs = []
        for hh in range(MEM_HEADS):
            pr = _mem_softmax(qb[:, hh * MEM_HD:(hh + 1) * MEM_HD], km_ref[hh])
            outs.append(_dot_nn(pr.astype(BF16), vm_ref[hh]))
        ob = jnp.concatenate(outs, axis=-1).astype(BF16)
        o_ref[...] = ob
        h3_ref[...] = h2 + _dot_nn(ob, wmo_ref[...])

    full = lambda shape: pl.BlockSpec(shape, lambda i: (0,) * len(shape))
    row = lambda w: pl.BlockSpec((tm, w), lambda i: (i, 0))
    return _call_after(
        token, body,
        [row(d), row(half), row(half), _resident(w_out.shape), _resident((1, d)), _resident(wmq.shape),
         _resident(km.shape), _resident(vm.shape), _resident(wmo.shape)],
        (h1, a, p, w_out, g, wmq, km, vm, wmo),
        name="xattn_fwd", grid=(t // tm,),
        out_specs=[row(d), row(d), row(d), row(d), row(d)],
        out_shape=[jax.ShapeDtypeStruct((t, d), F32), jax.ShapeDtypeStruct((t, d), F32),
                   jax.ShapeDtypeStruct((t, d), BF16), jax.ShapeDtypeStruct((t, d), BF16),
                   jax.ShapeDtypeStruct((t, d), BF16)],
        compiler_params=_cparams(),
    )


def _xattn_bwd(dh3, h2, qm, g, wmq, km, vm, wmo, w_out, token=None):
    t, d = h2.shape
    tm = min(512, t)
    half = d // 2

    def body(dh3_ref, h2_ref, q_ref, g_ref, wmq_ref, km_ref, vm_ref, wmo_ref, wo_ref,
             dh2_ref, dq_ref, da_ref, dp_ref, dk_ref, dv_ref, dg_ref):
        i = pl.program_id(0)
        first = i == 0

        @pl.when(first)
        def _():
            dk_ref[...] = jnp.zeros_like(dk_ref)
            dv_ref[...] = jnp.zeros_like(dv_ref)

        dh3 = dh3_ref[...]
        dob = _dot_nt(dh3.astype(BF16), wmo_ref[...]).astype(BF16)
        qb = q_ref[...]
        dq_parts = []
        for hh in range(MEM_HEADS):
            cols = slice(hh * MEM_HD, (hh + 1) * MEM_HD)
            kk, vv = km_ref[hh], vm_ref[hh]
            pr = _mem_softmax(qb[:, cols], kk)
            doh = dob[:, cols]
            dv_ref[hh] += _dot_tn(pr.astype(BF16), doh)
            dpp = _dot_nt(doh, vv)
            dsb = (pr * (dpp - jnp.sum(dpp * pr, axis=-1, keepdims=True))).astype(BF16)
            dq_parts.append(_dot_nn(dsb, kk))
            dk_ref[hh] += _dot_tn(dsb, qb[:, cols])
        dqb = (jnp.concatenate(dq_parts, axis=-1) * MEM_SCALE).astype(BF16)
        dq_ref[...] = dqb
        x = h2_ref[...]
        gg = g_ref[...]
        _, r = _rms_fwd(x, gg)
        dx, dg = _rms_bwd(_dot_nt(dqb, wmq_ref[...]), x, gg, r)
        dh2 = dh3 + dx
        dh2_ref[...] = dh2
        dap = _dot_nt(dh2.astype(BF16), wo_ref[...])
        da_ref[...] = dap[:, 0:half].astype(BF16)
        dp_ref[...] = dap[:, half:d].astype(BF16)
        _accumulate(dg_ref, dg, first)

    full = lambda shape: pl.BlockSpec(shape, lambda i: (0,) * len(shape))
    row = lambda w: pl.BlockSpec((tm, w), lambda i: (i, 0))
    return _call_after(
        token, body,
        [row(d), row(d), row(d), _resident((1, d)), _resident(wmq.shape), _resident(km.shape), _resident(vm.shape),
         _resident(wmo.shape), _resident(w_out.shape)],
        (dh3, h2, qm, g, wmq, km, vm, wmo, w_out),
        name="xattn_bwd", grid=(t // tm,),
        out_specs=[row(d), row(d), row(half), row(half), full(km.shape), full(vm.shape), full((1, d))],
        out_shape=[jax.ShapeDtypeStruct((t, d), F32), jax.ShapeDtypeStruct((t, d), BF16),
                   jax.ShapeDtypeStruct((t, half), BF16), jax.ShapeDtypeStruct((t, half), BF16),
                   jax.ShapeDtypeStruct(km.shape, F32), jax.ShapeDtypeStruct(vm.shape, F32),
                   jax.ShapeDtypeStruct((1, d), F32)],
        compiler_params=_cparams(),
    )


def _mem_kv_bwd(dkm, dvm, memn, mem, mem_norm, wmkv):
    n, d = mem.shape

    def body(dk_ref, dv_ref, memn_ref, mem_ref, g_ref, w_ref, dw_ref, dg_ref):
        memn = memn_ref[...]
        dmemn = jnp.zeros((n, d), F32)
        for s in range(2 * MEM_HEADS):
            src = dk_ref[s] if s < MEM_HEADS else dv_ref[s - MEM_HEADS]
            db = src.astype(BF16)
            dw_ref[s] = _dot_tn(memn, db)
            dmemn = dmemn + _dot_nt(db, w_ref[s])
        x = mem_ref[...]
        gg = g_ref[...]
        _, r = _rms_fwd(x, gg)
        _, dg = _rms_bwd(dmemn, x, gg, r)
        dg_ref[...] = dg

    return pl.pallas_call(
        body, name="mem_kv_bwd",
        out_shape=[jax.ShapeDtypeStruct(wmkv.shape, F32), jax.ShapeDtypeStruct((1, d), F32)],
        compiler_params=_cparams(),
    )(dkm, dvm, memn, mem, mem_norm, wmkv)


MESH_ID = pl.DeviceIdType.MESH
ANY = pl.BlockSpec(memory_space=pl.ANY)


def _coords():
    return lax.axis_index("x"), lax.axis_index("y"), lax.axis_index("c")


def _other_chips(x, y):
    return [(1 - x, y), (x, 1 - y), (1 - x, 1 - y)]


def _core_reduce(g, tag):
    _, r, w = g.shape

    def body(g_ref, part_ref, own_sc, recv_sc, send_sems, recv_sems, local_sems):
        x, y, c = _coords()
        sent, local = [], []
        for chip in range(4):
            sent.append(pltpu.make_async_remote_copy(
                src_ref=g_ref.at[2 * chip + (1 - c)], dst_ref=recv_sc.at[chip],
                send_sem=send_sems.at[chip], recv_sem=recv_sems.at[chip],
                device_id=(x, y, 1 - c), device_id_type=MESH_ID))
            local.append(pltpu.make_async_copy(g_ref.at[2 * chip + c], own_sc.at[chip], local_sems.at[chip]))
        for cp in sent + local:
            cp.start()
        for chip in range(4):
            local[chip].wait()
            sent[chip].wait_recv()
            part_ref[chip] = (own_sc[chip].astype(F32) + recv_sc[chip].astype(F32)).astype(part_ref.dtype)
        for cp in sent:
            cp.wait_send()

    return pl.pallas_call(
        body, name="core_reduce_" + tag,
        out_shape=jax.ShapeDtypeStruct((4, r, w), g.dtype),
        in_specs=[ANY], out_specs=pl.BlockSpec(memory_space=pltpu.VMEM),
        scratch_shapes=[pltpu.VMEM((4, r, w), g.dtype), pltpu.VMEM((4, r, w), g.dtype),
                        pltpu.SemaphoreType.DMA((4,)), pltpu.SemaphoreType.DMA((4,)), pltpu.SemaphoreType.DMA((4,))],
        compiler_params=_cparams(),
    )(g)


HBM_SPEC = pl.BlockSpec(memory_space=pltpu.HBM)
SEM_SPEC = pl.BlockSpec(memory_space=pltpu.SEMAPHORE)
SPLIT_EFFECT = pltpu.SideEffectType.DATAFLOW_SIDE_EFFECTING


def _ici_refs(gather, src_ref, land_ref, j, px, py, slot_chip, c):
    if gather:
        return src_ref, land_ref.at[:, 4 * slot_chip[0] + 2 * slot_chip[1] + c]
    return src_ref.at[2 * px + py], land_ref.at[j]


def _ici_start(src, after, name, gather):
    r, w = src.shape[-2:]
    land_shape = (src.shape[0], N_DEV, r, w) if gather else (3, r, w)

    def body(src_ref, land_ref, after_ref, send_sems, recv_sems, src_thru, land_thru, token):
        x, y, c = _coords()
        for j, (px, py) in enumerate(_other_chips(x, y)):
            s_ref, d_ref = _ici_refs(gather, src_ref, land_ref, j, px, py, (x, y), c)
            pltpu.make_async_remote_copy(
                src_ref=s_ref, dst_ref=d_ref, send_sem=send_sems.at[j], recv_sem=recv_sems.at[j],
                device_id=(px, py, c), device_id_type=MESH_ID).start()
        token[...] = jnp.zeros_like(token)

    return pl.pallas_call(
        body, name=name,
        out_shape=(pltpu.SemaphoreType.DMA((3,)), pltpu.SemaphoreType.DMA((3,)), pltpu.HBM(src.shape, src.dtype),
                   pltpu.HBM(land_shape, src.dtype), jax.ShapeDtypeStruct((8, 128), F32)),
        in_specs=(HBM_SPEC, HBM_SPEC, ANY),
        out_specs=(SEM_SPEC, SEM_SPEC, HBM_SPEC, HBM_SPEC, pl.BlockSpec(memory_space=pltpu.VMEM)),
        input_output_aliases={0: 2, 1: 3},
        compiler_params=pltpu.CompilerParams(has_side_effects=SPLIT_EFFECT),
    )(pltpu.with_memory_space_constraint(src, pltpu.HBM),
      pltpu.with_memory_space_constraint(lax.empty(land_shape, src.dtype), pltpu.HBM), after)


def _ici_wait(started, after, name, gather):
    send_sems, recv_sems, src_thru, land_thru, _ = started

    def body(src_ref, land_ref, send_sems, recv_sems, after_ref, src_dead, got_ref):
        x, y, c = _coords()
        for j, (px, py) in enumerate(_other_chips(x, y)):
            s_ref, d_ref = _ici_refs(gather, src_ref, land_ref, j, px, py, (px, py), c)
            copy = pltpu.make_async_remote_copy(
                src_ref=s_ref, dst_ref=d_ref, send_sem=send_sems.at[j], recv_sem=recv_sems.at[j],
                device_id=(px, py, c), device_id_type=MESH_ID)
            copy.wait_send()
            copy.wait_recv()

    return pl.pallas_call(
        body, name=name,
        out_shape=(pltpu.HBM(src_thru.shape, src_thru.dtype), pltpu.HBM(land_thru.shape, land_thru.dtype)),
        in_specs=(HBM_SPEC, HBM_SPEC, SEM_SPEC, SEM_SPEC, ANY),
        out_specs=(HBM_SPEC, HBM_SPEC), input_output_aliases={0: 0, 1: 1},
        compiler_params=pltpu.CompilerParams(has_side_effects=SPLIT_EFFECT),
    )(src_thru, land_thru, send_sems, recv_sems, after)


def _neighbour(k, x, y):
    return (1 - x, y) if k == 0 else (x, 1 - y)


def _slot(ref, px, py, c):
    return ref.at[:, 4 * px + 2 * py + c]


def _near_start(src, after, name):
    land_shape = (src.shape[0], N_DEV) + src.shape[1:]

    def body(src_ref, land_ref, after_ref, send_sems, recv_sems, src_thru, land_thru, token):
        x, y, c = _coords()
        for k in range(2):
            px, py = _neighbour(k, x, y)
            pltpu.make_async_remote_copy(
                src_ref=src_ref, dst_ref=_slot(land_ref, x, y, c), send_sem=send_sems.at[k],
                recv_sem=recv_sems.at[k], device_id=(px, py, c), device_id_type=MESH_ID).start()
        token[...] = jnp.zeros_like(token)

    return pl.pallas_call(
        body, name=name,
        out_shape=(pltpu.SemaphoreType.DMA((2,)), pltpu.SemaphoreType.DMA((2,)), pltpu.HBM(src.shape, src.dtype),
                   pltpu.HBM(land_shape, src.dtype), jax.ShapeDtypeStruct((8, 128), F32)),
        in_specs=(HBM_SPEC, HBM_SPEC, ANY),
        out_specs=(SEM_SPEC, SEM_SPEC, HBM_SPEC, HBM_SPEC, pl.BlockSpec(memory_space=pltpu.VMEM)),
        input_output_aliases={0: 2, 1: 3},
        compiler_params=pltpu.CompilerParams(has_side_effects=SPLIT_EFFECT),
    )(pltpu.with_memory_space_constraint(src, pltpu.HBM),
      pltpu.with_memory_space_constraint(lax.empty(land_shape, src.dtype), pltpu.HBM), after)


def _near_wait(started, after, name):
    send_sems, recv_sems, src_thru, land_thru, _ = started

    def body(src_ref, land_ref, send_sems, recv_sems, after_ref, src_dead, got_ref):
        x, y, c = _coords()
        for k in range(2):
            px, py = _neighbour(k, x, y)
            copy = pltpu.make_async_remote_copy(
                src_ref=src_ref, dst_ref=_slot(land_ref, px, py, c), send_sem=send_sems.at[k],
                recv_sem=recv_sems.at[k], device_id=(px, py, c), device_id_type=MESH_ID)
            copy.wait_send()
            copy.wait_recv()

    return pl.pallas_call(
        body, name=name,
        out_shape=(pltpu.HBM(src_thru.shape, src_thru.dtype), pltpu.HBM(land_thru.shape, land_thru.dtype)),
        in_specs=(HBM_SPEC, HBM_SPEC, SEM_SPEC, SEM_SPEC, ANY),
        out_specs=(HBM_SPEC, HBM_SPEC), input_output_aliases={0: 0, 1: 1},
        compiler_params=pltpu.CompilerParams(has_side_effects=SPLIT_EFFECT),
    )(src_thru, land_thru, send_sems, recv_sems, after)


def _far_refs(land_ref, k, x, y, c, arriving):
    half = land_ref.shape[2] // 2
    rows = pl.ds(k * half, half)
    ox, oy = (1 - x, 1 - y) if arriving else _neighbour(k, x, y)
    return land_ref.at[:, 4 * ox + 2 * oy + c, rows]


def _far_start(land, after, name):
    def body(land_ref, after_ref, send_sems, recv_sems, land_thru, token):
        x, y, c = _coords()
        for k in range(2):
            block = _far_refs(land_ref, k, x, y, c, False)
            px, py = _neighbour(1 - k, x, y)
            pltpu.make_async_remote_copy(
                src_ref=block, dst_ref=block, send_sem=send_sems.at[k], recv_sem=recv_sems.at[k],
                device_id=(px, py, c), device_id_type=MESH_ID).start()
        token[...] = jnp.zeros_like(token)

    return pl.pallas_call(
        body, name=name,
        out_shape=(pltpu.SemaphoreType.DMA((2,)), pltpu.SemaphoreType.DMA((2,)),
                   pltpu.HBM(land.shape, land.dtype), jax.ShapeDtypeStruct((8, 128), F32)),
        in_specs=(HBM_SPEC, ANY),
        out_specs=(SEM_SPEC, SEM_SPEC, HBM_SPEC, pl.BlockSpec(memory_space=pltpu.VMEM)),
        input_output_aliases={0: 2},
        compiler_params=pltpu.CompilerParams(has_side_effects=SPLIT_EFFECT),
    )(pltpu.with_memory_space_constraint(land, pltpu.HBM), after)


def _far_wait(started, after, name):
    send_sems, recv_sems, land_thru, _ = started

    def body(land_ref, send_sems, recv_sems, after_ref, got_ref):
        x, y, c = _coords()
        for k in range(2):
            px, py = _neighbour(1 - k, x, y)
            copy = pltpu.make_async_remote_copy(
                src_ref=_far_refs(land_ref, k, x, y, c, False), dst_ref=_far_refs(land_ref, k, x, y, c, True),
                send_sem=send_sems.at[k], recv_sem=recv_sems.at[k], device_id=(px, py, c), device_id_type=MESH_ID)
            copy.wait_send()
            copy.wait_recv()

    return pl.pallas_call(
        body, name=name,
        out_shape=pltpu.HBM(land_thru.shape, land_thru.dtype),
        in_specs=(HBM_SPEC, SEM_SPEC, SEM_SPEC, ANY),
        out_specs=HBM_SPEC, input_output_aliases={0: 0},
        compiler_params=pltpu.CompilerParams(has_side_effects=SPLIT_EFFECT),
    )(land_thru, send_sems, recv_sems, after)


def _peer(k, x, y, c):
    return x ^ ((k >> 2) & 1), y ^ ((k >> 1) & 1), c ^ (k & 1)


def _peers_start(src, after, name):
    r, w = src.shape
    x, y, c = _coords()
    land = lax.dynamic_update_slice(jnp.zeros((N_DEV, r, w), src.dtype), src[None], (4 * x + 2 * y + c, 0, 0))

    def body(src_ref, land_ref, after_ref, send_sems, recv_sems, src_thru, land_thru, token):
        x, y, c = _coords()
        for k in range(1, N_DEV):
            pltpu.make_async_remote_copy(
                src_ref=src_ref, dst_ref=land_ref.at[4 * x + 2 * y + c],
                send_sem=send_sems.at[k - 1], recv_sem=recv_sems.at[k - 1],
                device_id=_peer(k, x, y, c), device_id_type=MESH_ID).start()
        token[...] = jnp.zeros_like(token)

    return pl.pallas_call(
        body, name=name,
        out_shape=(pltpu.SemaphoreType.DMA((N_DEV - 1,)), pltpu.SemaphoreType.DMA((N_DEV - 1,)),
                   pltpu.HBM(src.shape, src.dtype), pltpu.HBM(land.shape, src.dtype),
                   jax.ShapeDtypeStruct((8, 128), F32)),
        in_specs=(HBM_SPEC, HBM_SPEC, ANY),
        out_specs=(SEM_SPEC, SEM_SPEC, HBM_SPEC, HBM_SPEC, pl.BlockSpec(memory_space=pltpu.VMEM)),
        input_output_aliases={0: 2, 1: 3},
        compiler_params=pltpu.CompilerParams(has_side_effects=SPLIT_EFFECT),
    )(pltpu.with_memory_space_constraint(src, pltpu.HBM), pltpu.with_memory_space_constraint(land, pltpu.HBM), after)


def _peers_wait(started, after, name):
    send_sems, recv_sems, src_thru, land_thru, _ = started

    def body(src_ref, land_ref, send_sems, recv_sems, after_ref, src_dead, got_ref):
        x, y, c = _coords()
        for k in range(1, N_DEV):
            px, py, pc = _peer(k, x, y, c)
            copy = pltpu.make_async_remote_copy(
                src_ref=src_ref, dst_ref=land_ref.at[4 * px + 2 * py + pc],
                send_sem=send_sems.at[k - 1], recv_sem=recv_sems.at[k - 1],
                device_id=(px, py, pc), device_id_type=MESH_ID)
            copy.wait_send()
            copy.wait_recv()

    return pl.pallas_call(
        body, name=name,
        out_shape=(pltpu.HBM(src_thru.shape, src_thru.dtype), pltpu.HBM(land_thru.shape, land_thru.dtype)),
        in_specs=(HBM_SPEC, HBM_SPEC, SEM_SPEC, SEM_SPEC, ANY),
        out_specs=(HBM_SPEC, HBM_SPEC), input_output_aliases={0: 0, 1: 1},
        compiler_params=pltpu.CompilerParams(has_side_effects=SPLIT_EFFECT),
    )(src_thru, land_thru, send_sems, recv_sems, after)


def _share_refs(ref, k, x, y, c, sender_c):
    px, py = ([(x, y)] + _other_chips(x, y))[k]
    return ref.at[:, 4 * px + 2 * py + sender_c]


def _share_start(gathered, after, name):
    def body(g_ref, after_ref, send_sems, recv_sems, g_thru, token):
        x, y, c = _coords()
        for k in range(4):
            slot = _share_refs(g_ref, k, x, y, c, c)
            pltpu.make_async_remote_copy(
                src_ref=slot, dst_ref=slot, send_sem=send_sems.at[k], recv_sem=recv_sems.at[k],
                device_id=(x, y, 1 - c), device_id_type=MESH_ID).start()
        token[...] = jnp.zeros_like(token)

    return pl.pallas_call(
        body, name=name,
        out_shape=(pltpu.SemaphoreType.DMA((4,)), pltpu.SemaphoreType.DMA((4,)),
                   pltpu.HBM(gathered.shape, gathered.dtype), jax.ShapeDtypeStruct((8, 128), F32)),
        in_specs=(HBM_SPEC, ANY),
        out_specs=(SEM_SPEC, SEM_SPEC, HBM_SPEC, pl.BlockSpec(memory_space=pltpu.VMEM)),
        input_output_aliases={0: 2},
        compiler_params=pltpu.CompilerParams(has_side_effects=SPLIT_EFFECT),
    )(pltpu.with_memory_space_constraint(gathered, pltpu.HBM), after)


def _share_wait(started, after, name):
    send_sems, recv_sems, g_thru, _ = started

    def body(g_ref, send_sems, recv_sems, after_ref, got_ref):
        x, y, c = _coords()
        for k in range(4):
            copy = pltpu.make_async_remote_copy(
                src_ref=_share_refs(g_ref, k, x, y, c, c), dst_ref=_share_refs(g_ref, k, x, y, c, 1 - c),
                send_sem=send_sems.at[k], recv_sem=recv_sems.at[k],
                device_id=(x, y, 1 - c), device_id_type=MESH_ID)
            copy.wait_send()
            copy.wait_recv()

    return pl.pallas_call(
        body, name=name,
        out_shape=pltpu.HBM(g_thru.shape, g_thru.dtype),
        in_specs=(HBM_SPEC, SEM_SPEC, SEM_SPEC, ANY),
        out_specs=HBM_SPEC, input_output_aliases={0: 0},
        compiler_params=pltpu.CompilerParams(has_side_effects=SPLIT_EFFECT),
    )(g_thru, send_sems, recv_sems, after)


def _core_share(own, gathered, name):
    def body(own_ref, gin_ref, out_ref, stage, send_sems, recv_sems, local_sem):
        x, y, c = _coords()
        sibling = (x, y, 1 - c)
        chips = [(x, y)] + _other_chips(x, y)
        stage_in = pltpu.make_async_copy(own_ref, stage, local_sem)
        stage_in.start()
        sent, arriving = [], []
        for k, (px, py) in enumerate(chips):
            slot = out_ref.at[:, 4 * px + 2 * py + c]
            sent.append(pltpu.make_async_remote_copy(
                src_ref=own_ref if k == 0 else slot, dst_ref=slot,
                send_sem=send_sems.at[k], recv_sem=recv_sems.at[k], device_id=sibling, device_id_type=MESH_ID))
            arriving.append(pltpu.make_async_remote_copy(
                src_ref=own_ref, dst_ref=out_ref.at[:, 4 * px + 2 * py + (1 - c)],
                send_sem=send_sems.at[k], recv_sem=recv_sems.at[k], device_id=sibling, device_id_type=MESH_ID))
        for cp in sent:
            cp.start()
        stage_in.wait()
        stage_out = pltpu.make_async_copy(stage, out_ref.at[:, 4 * x + 2 * y + c], local_sem)
        stage_out.start()
        for cp in arriving:
            cp.wait_recv()
        for cp in sent:
            cp.wait_send()
        stage_out.wait()

    return pl.pallas_call(
        body, name=name,
        out_shape=jax.ShapeDtypeStruct(gathered.shape, own.dtype),
        in_specs=[ANY, ANY], out_specs=ANY, input_output_aliases={1: 0},
        scratch_shapes=[pltpu.VMEM(own.shape, own.dtype), pltpu.SemaphoreType.DMA((4,)),
                        pltpu.SemaphoreType.DMA((4,)), pltpu.SemaphoreType.DMA],
    )(own, gathered)


def _adamw(w, g, m, v):
    m = ADAM_B1 * m + (1.0 - ADAM_B1) * g
    v = ADAM_B2 * v + (1.0 - ADAM_B2) * (g * g)
    m_hat = m / ADAM_C1
    v_hat = v / ADAM_C2
    delta = -ADAM_LR * (m_hat / (jnp.sqrt(v_hat) + ADAM_EPS) + ADAM_WD * w)
    return delta, m, v


def _adam_big(part, land, w, m, v, chip_idx, tag, token):
    r, wd = w.shape
    tr, tw = _row_tile(r, 1024), 256

    def body(s_ref, tok_ref, p_ref, l_ref, w_ref, m_ref, v_ref, g_ref, d_ref, mo_ref, vo_ref):
        g = p_ref[0].astype(F32)
        for j in range(3):
            g = g + l_ref[j].astype(F32)
        delta, mn, vn = _adamw(w_ref[...], g, m_ref[...], v_ref[...])
        g_ref[...] = g
        d_ref[...] = delta
        mo_ref[...] = mn
        vo_ref[...] = vn

    row = pl.BlockSpec((tr, tw), lambda i, j, s: (i, j))
    return pl.pallas_call(
        body, name="adam_big_" + tag,
        grid_spec=pltpu.PrefetchScalarGridSpec(
            num_scalar_prefetch=1, grid=(r // tr, wd // tw),
            in_specs=[pl.BlockSpec((8, 128), lambda i, j, s: (0, 0)),
                      pl.BlockSpec((1, tr, tw), lambda i, j, s: (s[0], i, j)),
                      pl.BlockSpec((3, tr, tw), lambda i, j, s: (0, i, j)), row, row, row],
            out_specs=[row, row, row, row]),
        out_shape=[jax.ShapeDtypeStruct((r, wd), F32)] * 4,
        compiler_params=_cparams(),
    )(chip_idx, token, part, land, w, m, v)


def _adam_small(parts, w, m, v):
    _, r, wd = parts.shape

    def body(p_ref, w_ref, m_ref, v_ref, g_ref, d_ref, mo_ref, vo_ref):
        g = p_ref[0]
        for k in range(1, N_DEV):
            g = g + p_ref[k]
        delta, mn, vn = _adamw(w_ref[...], g, m_ref[...], v_ref[...])
        g_ref[...] = g
        d_ref[...] = delta
        mo_ref[...] = mn
        vo_ref[...] = vn

    return pl.pallas_call(
        body, name="adam_small",
        out_shape=[jax.ShapeDtypeStruct((r, wd), F32)] * 4,
        compiler_params=_cparams(),
    )(parts, w, m, v)


def _pad_rows(a, rows):
    return jnp.pad(a, ((0, rows - a.shape[0]), (0, 0)))


def _pad_w_in(w):
    cut = Q_RANK + KV_RANK + ROPE
    return jnp.concatenate([w[:, :cut], jnp.zeros((w.shape[0], 64), w.dtype), w[:, cut:]], axis=1)


def _unpad_w_in(w):
    cut = Q_RANK + KV_RANK + ROPE
    return jnp.concatenate([w[:, :cut], w[:, cut + 64:]], axis=1)


def _pack_mid(p):
    parts = [_pad_w_in(p["w_in"][0]), p["w_out"][0], p["w_mq"][0], p["w_mo"][0],
             p["w_mkv"][0].reshape(256, D_MODEL),
             _pad_rows(p["w_q_up"][0].T.reshape(24, D_MODEL), 32),
             p["w_kv_up"][0].reshape(16, D_MODEL)]
    return jnp.concatenate(parts, axis=0)


def _pack_segments(p, group):
    if group == "mid":
        return _pack_mid(p)[None]
    return jnp.stack([p[group + "_w_gate"][0].T, p[group + "_w_up"][0].T, p[group + "_w_down"][0]])


UNIT_WEIGHT = {"ffn1_g": ("ffn1_w_gate", True), "ffn1_u": ("ffn1_w_up", True), "ffn1_d": ("ffn1_w_down", False),
               "ffn2_g": ("ffn2_w_gate", True), "ffn2_u": ("ffn2_w_up", True), "ffn2_d": ("ffn2_w_down", False)}


def _pack_unit(p, unit):
    if unit == "mid":
        return _pack_mid(p)
    name, transposed = UNIT_WEIGHT[unit]
    return p[name][0].T if transposed else p[name][0]


def _unpack_unit(a, unit):
    if unit != "mid":
        name, transposed = UNIT_WEIGHT[unit]
        return {name: (a.T if transposed else a)[None]}
    seg = lambda n: a[SEG_OFF[n][0]:SEG_OFF[n][0] + SEG_OFF[n][1]]
    return {"w_in": _unpad_w_in(seg("w_in"))[None], "w_out": seg("w_out")[None], "w_mq": seg("w_mq")[None],
            "w_mo": seg("w_mo")[None], "w_mkv": seg("w_mkv").reshape(D_MODEL, 256)[None],
            "w_q_up": seg("w_q")[:24].reshape(96, Q_RANK).T[None],
            "w_kv_up": seg("w_kv").reshape(KV_RANK, 128)[None]}


def _unpack_gathered(full, group):
    if group != "mid":
        return {n: full[k].reshape(-1, D_MODEL) for k, (n, _) in enumerate(GROUP_SEGS[group])}
    full = full[0]
    seg = lambda n: full[:, SEG_OFF[n][0]:SEG_OFF[n][0] + SEG_OFF[n][1]]
    rows = lambda n: seg(n).reshape(-1, D_MODEL)
    wq_t = seg("w_q")[:, :24].reshape(MLA_HEADS, NOPE + ROPE, Q_RANK)
    wq_t = jnp.pad(wq_t, ((0, 0), (0, HEAD_PAD - NOPE - ROPE), (0, 0))).reshape(MLA_HEADS * HEAD_PAD, Q_RANK)
    wkv = seg("w_kv").reshape(N_DEV, KV_RANK, 128).transpose(1, 0, 2).reshape(KV_RANK, N_DEV * 128)
    return {"w_in": rows("w_in"), "w_out": rows("w_out"), "w_mq": rows("w_mq"), "w_mo": rows("w_mo"),
            "w_mkv": seg("w_mkv").reshape(N_DEV, D_MODEL, 256), "w_q": wq_t, "w_kv": wkv}


def _pack_grads(gr):
    blk = lambda a: a.reshape(N_DEV, -1, D_MODEL)
    dwq = gr["w_q"].reshape(MLA_HEADS, HEAD_PAD, Q_RANK)[:, :NOPE + ROPE].reshape(N_DEV, 24, D_MODEL)
    dwq = jnp.pad(dwq, ((0, 0), (0, 8), (0, 0)))
    dwkv = gr["w_kv"].reshape(KV_RANK, N_DEV, 128).transpose(1, 0, 2).reshape(N_DEV, 16, D_MODEL)
    parts = [blk(gr["w_in"]), blk(gr["w_out"]), blk(gr["w_mq"]), blk(gr["w_mo"]),
             gr["w_mkv"].reshape(N_DEV, 256, D_MODEL), dwq, dwkv]
    return jnp.concatenate([a.astype(BF16) for a in parts], axis=1)


def _pack_small(vals):
    parts = []
    for n, r in SMALL_ROWS:
        parts.append(_pad_rows(vals[n].reshape(-1, 128), r) if n in vals else jnp.zeros((r, 128), F32))
    return jnp.concatenate(parts, axis=0)


def _unpack_small(a, shapes):
    out = {}
    for n, shape in shapes.items():
        o = SMALL_OFF[n][0]
        out[n] = a[o:o + int(np.prod(shape)) // 128].reshape(shape)
    return out


BIG_NAMES = ("ffn1_w_gate", "ffn1_w_up", "ffn1_w_down", "w_in", "w_q_up", "w_kv_up", "w_out", "w_mq", "w_mkv",
             "w_mo", "ffn2_w_gate", "ffn2_w_up", "ffn2_w_down")
SMALL_NAMES = ("ffn1_norm", "mix_norm", "q_norm", "kv_norm", "pool_w", "pool_scale", "xattn_norm", "mem_norm",
               "ffn2_norm", "final_norm")
WEIGHT_ORDER = ("ffn1_norm", "ffn1_w_gate", "ffn1_w_up", "ffn1_w_down", "mix_norm", "w_in", "q_norm", "w_q_up",
                "kv_norm", "w_kv_up", "pool_w", "pool_scale", "w_out", "xattn_norm", "mem_norm", "w_mq", "w_mkv",
                "w_mo", "ffn2_norm", "ffn2_w_gate", "ffn2_w_up", "ffn2_w_down", "final_norm")


def _rope_table():
    lane = np.arange(128)
    freqs = (1.0 / (ROPE_BASE ** (np.arange(0, ROPE, 2, dtype=np.float32) / ROPE))).astype(np.float32)
    tab = np.zeros((8, 128), np.float32)
    tab[0] = np.where(lane < ROPE, freqs[lane % (ROPE // 2)], 0.0)
    tab[1] = np.where(lane < ROPE // 2, -1.0, np.where(lane < ROPE, 1.0, 0.0))
    return jnp.asarray(tab)


def kernel(x, mem, positions, ffn1_norm, ffn1_w_gate, ffn1_w_up, ffn1_w_down, mix_norm, w_in, q_norm, w_q_up, kv_norm, w_kv_up, pool_w, pool_scale, w_out, xattn_norm, mem_norm, w_mq, w_mkv, w_mo, ffn2_norm, ffn2_w_gate, ffn2_w_up, ffn2_w_down, final_norm, loss_target, m_ffn1_norm, m_ffn1_w_gate, m_ffn1_w_up, m_ffn1_w_down, m_mix_norm, m_w_in, m_q_norm, m_w_q_up, m_kv_norm, m_w_kv_up, m_pool_w, m_pool_scale, m_w_out, m_xattn_norm, m_mem_norm, m_w_mq, m_w_mkv, m_w_mo, m_ffn2_norm, m_ffn2_w_gate, m_ffn2_w_up, m_ffn2_w_down, m_final_norm, v_ffn1_norm, v_ffn1_w_gate, v_ffn1_w_up, v_ffn1_w_down, v_mix_norm, v_w_in, v_q_norm, v_w_q_up, v_kv_norm, v_w_kv_up, v_pool_w, v_pool_scale, v_w_out, v_xattn_norm, v_mem_norm, v_w_mq, v_w_mkv, v_w_mo, v_ffn2_norm, v_ffn2_w_gate, v_ffn2_w_up, v_ffn2_w_down, v_final_norm):
    wts = dict(ffn1_norm=ffn1_norm, ffn1_w_gate=ffn1_w_gate, ffn1_w_up=ffn1_w_up, ffn1_w_down=ffn1_w_down,
               mix_norm=mix_norm, w_in=w_in, q_norm=q_norm, w_q_up=w_q_up, kv_norm=kv_norm, w_kv_up=w_kv_up,
               pool_w=pool_w, pool_scale=pool_scale, w_out=w_out, xattn_norm=xattn_norm, mem_norm=mem_norm,
               w_mq=w_mq, w_mkv=w_mkv, w_mo=w_mo, ffn2_norm=ffn2_norm, ffn2_w_gate=ffn2_w_gate,
               ffn2_w_up=ffn2_w_up, ffn2_w_down=ffn2_w_down, final_norm=final_norm)
    mom = dict(ffn1_norm=m_ffn1_norm, ffn1_w_gate=m_ffn1_w_gate, ffn1_w_up=m_ffn1_w_up, ffn1_w_down=m_ffn1_w_down,
               mix_norm=m_mix_norm, w_in=m_w_in, q_norm=m_q_norm, w_q_up=m_w_q_up, kv_norm=m_kv_norm,
               w_kv_up=m_w_kv_up, pool_w=m_pool_w, pool_scale=m_pool_scale, w_out=m_w_out, xattn_norm=m_xattn_norm,
               mem_norm=m_mem_norm, w_mq=m_w_mq, w_mkv=m_w_mkv, w_mo=m_w_mo, ffn2_norm=m_ffn2_norm,
               ffn2_w_gate=m_ffn2_w_gate, ffn2_w_up=m_ffn2_w_up, ffn2_w_down=m_ffn2_w_down, final_norm=m_final_norm)
    var = dict(ffn1_norm=v_ffn1_norm, ffn1_w_gate=v_ffn1_w_gate, ffn1_w_up=v_ffn1_w_up, ffn1_w_down=v_ffn1_w_down,
               mix_norm=v_mix_norm, w_in=v_w_in, q_norm=v_q_norm, w_q_up=v_w_q_up, kv_norm=v_kv_norm,
               w_kv_up=v_w_kv_up, pool_w=v_pool_w, pool_scale=v_pool_scale, w_out=v_w_out, xattn_norm=v_xattn_norm,
               mem_norm=v_mem_norm, w_mq=v_w_mq, w_mkv=v_w_mkv, w_mo=v_w_mo, ffn2_norm=v_ffn2_norm,
               ffn2_w_gate=v_ffn2_w_gate, ffn2_w_up=v_ffn2_w_up, ffn2_w_down=v_ffn2_w_down, final_norm=v_final_norm)

    t = x.shape[1]
    xs = x[0]
    mems = mem[0]
    target = loss_target[0]
    pos = positions.reshape(t, 1)
    row = lambda a: a.reshape(1, -1)
    rope_tab = _rope_table()

    cx, cy, cc = _coords()
    chip_idx = (2 * cx + cy).astype(jnp.int32).reshape(1)

    wb = {}
    for grp in ("ffn1", "mid", "ffn2"):
        wb[grp] = _pack_segments(wts, grp).astype(BF16)
        if grp == "ffn1":
            near_ffn1 = _near_start(wb["ffn1"], pos, "ag_ffn1_near_start")
    own_ffn1, land_ffn1 = _near_wait(near_ffn1, wb["ffn2"], "ag_ffn1_near_wait")
    far_ffn1 = _far_start(land_ffn1, own_ffn1, "ag_ffn1_far_start")
    land_ffn1 = _far_wait(far_ffn1, wb["mid"], "ag_ffn1_far_wait")
    full_ffn1 = _core_share(own_ffn1, land_ffn1, "ag_ffn1_share")
    fw = _unpack_gathered(full_ffn1, "ffn1")
    ag_mid = _ici_start(wb["mid"], full_ffn1, "ag_mid_start", True)
    g_ffn1, g_mix, g_q, g_kv = row(ffn1_norm), row(mix_norm), row(q_norm), row(kv_norm)
    g_x, g_mem, g_ffn2, g_fin = row(xattn_norm), row(mem_norm), row(ffn2_norm), row(final_norm)
    pool_wb = pool_w[0].astype(BF16)
    pool_sc = row(pool_scale)

    h1, n1, gate1, up1 = _ffn_fwd(xs, g_ffn1, fw["ffn1_g"], fw["ffn1_u"], fw["ffn1_d"], "ffn1_fwd", token=ag_mid[4])
    own_mid, land_mid = _ici_wait(ag_mid, h1, "ag_mid_wait", True)
    full_mid = _core_share(own_mid, land_mid, "ag_mid_share")
    fw.update(_unpack_gathered(full_mid, "mid"))
    ag_ffn2 = _ici_start(wb["ffn2"], full_mid, "ag_ffn2_start", True)
    u, z, qn, kvn, qh, kh, vh = _mix_prep(h1, g_mix, fw["w_in"], g_q, fw["w_q"], g_kv, fw["w_kv"], pos, rope_tab,
                                          token=ag_ffn2[4])
    a, lse = _attn_fwd(qh, kh, vh)
    p = _pool_fwd(z, pool_wb, pool_sc)
    memn, km, vm = _mem_kv(mems, g_mem, fw["w_mkv"])
    own_ffn2, land_ffn2 = _ici_wait(ag_ffn2, a, "ag_ffn2_wait", True)
    land_ffn2 = lax.dynamic_update_slice(land_ffn2, own_ffn2[:, None], (0, 4 * cx + 2 * cy + cc, 0, 0))
    share_ffn2 = _share_start(land_ffn2, a, "ag_ffn2_share_start")
    h2, h3, hn, qm, om = _xattn_fwd(h1, a, p, fw["w_out"], g_x, fw["w_mq"], km, vm, fw["w_mo"], token=share_ffn2[3])
    fw.update(_unpack_gathered(_share_wait(share_ffn2, h3, "ag_ffn2_share_wait"), "ffn2"))
    dh4, n2, gate2, up2, loss_part, dg_fin = _ffn_fwd(h3, g_ffn2, fw["ffn2_g"], fw["ffn2_u"], fw["ffn2_d"],
                                                      "ffn2_fwd", head=(target, g_fin))

    def reduce_start(g8, unit):
        part = _core_reduce(g8, unit)
        return _ici_start(part, g8, "rs_" + unit + "_start", False)

    def by_device(g):
        return g.reshape(N_DEV, -1, D_MODEL)

    rs = {}
    dh3, dgate2, dup2, act2, dg_ffn2 = _ffn_bwd_data(dh4, h3, g_ffn2, gate2, up2, fw["ffn2_g"], fw["ffn2_u"],
                                                     fw["ffn2_d"], "ffn2_bwd")
    rs["ffn2_g"] = reduce_start(by_device(_tn_matmul(dgate2, n2, "ffn2_dwg", tmm=1408, out_dtype=BF16)), "ffn2_g")
    rs["ffn2_u"] = reduce_start(by_device(_tn_matmul(dup2, n2, "ffn2_dwu", tmm=1408, out_dtype=BF16,
                                                     token=rs["ffn2_g"][4])), "ffn2_u")
    rs["ffn2_d"] = reduce_start(by_device(_tn_matmul(act2, dh4, "ffn2_dwd", scale=0.5, tmm=1408, out_dtype=BF16,
                                                     token=rs["ffn2_u"][4])), "ffn2_d")
    dh2, dqm, da, dp, dkm, dvm, dg_x = _xattn_bwd(dh3, h2, qm, g_x, fw["w_mq"], km, vm, fw["w_mo"], fw["w_out"],
                                                  token=rs["ffn2_d"][4])
    gr = {}
    gr["w_mo"] = _tn_matmul(om, dh3, "dw_mo", out_dtype=BF16)
    gr["w_mq"] = _tn_matmul(hn, dqm, "dw_mq", out_dtype=BF16)
    gr["w_out"] = jnp.concatenate([_tn_matmul(a, dh2, "dw_out_a", out_dtype=BF16),
                                   _tn_matmul(p, dh2, "dw_out_p", out_dtype=BF16)], axis=0)
    gr["w_mkv"], dg_mem = _mem_kv_bwd(dkm, dvm, memn, mems, g_mem, fw["w_mkv"])
    dz_pool, d_pool_w, d_pool_sc = _pool_bwd(dp, z, pool_wb, pool_sc)
    dqh, dkh, dvh = _attn_bwd(qh, kh, vh, da, lse, _attn_delta(a, da))
    dh1, dq, dkv, dz, dg_q, dg_kv, dg_mix = _mla_bwd(dqh, dkh, dvh, z, dz_pool, h1, dh2, g_mix, fw["w_in"], g_q,
                                                     fw["w_q"], g_kv, fw["w_kv"], pos, rope_tab)
    gr["w_q"] = _tn_matmul(dq, qn, "dw_q", out_dtype=BF16)
    gr["w_kv"] = _tn_matmul(kvn, dkv, "dw_kv", out_dtype=BF16)
    gr["w_in"] = _tn_matmul(u, dz, "dw_in", out_dtype=BF16)
    g_mid = _pack_grads(gr)
    part_mid = _core_reduce(g_mid, "mid")
    got = {}
    after = part_mid
    for unit in ("ffn2_g", "ffn2_u", "ffn2_d"):
        got[unit] = _ici_wait(rs[unit], after, "rs_" + unit + "_wait", False)
        after = got[unit][1]
    rs["mid"] = _ici_start(part_mid, after, "rs_mid_start", False)
    dx, dgate1, dup1, act1, dg_ffn1 = _ffn_bwd_data(dh1, xs, g_ffn1, gate1, up1, fw["ffn1_g"], fw["ffn1_u"],
                                                    fw["ffn1_d"], "ffn1_bwd", token=rs["mid"][4])
    got["mid"] = _ici_wait(rs["mid"], dx, "rs_mid_wait", False)

    small_g = dict(ffn1_norm=dg_ffn1, mix_norm=dg_mix, q_norm=dg_q, kv_norm=dg_kv, pool_w=d_pool_w,
                   pool_scale=d_pool_sc, xattn_norm=dg_x, mem_norm=dg_mem, ffn2_norm=dg_ffn2, final_norm=dg_fin,
                   loss=loss_part)
    small_ag = _peers_start(_pack_small(small_g), got["mid"][1], "small_ag_start")
    rs["ffn1_g"] = reduce_start(by_device(_tn_matmul(dgate1, n1, "ffn1_dwg", tmm=1408, out_dtype=BF16,
                                                     token=small_ag[4])), "ffn1_g")
    _, parts = _peers_wait(small_ag, rs["ffn1_g"][4], "small_ag_wait")
    small = _adam_small(parts, _pack_small({n: wts[n] for n in SMALL_NAMES}),
                        _pack_small({n: mom[n] for n in SMALL_NAMES}), _pack_small({n: var[n] for n in SMALL_NAMES}))
    small_sum = small[0]
    loss = small_sum[SMALL_OFF["loss"][0], 0]
    shapes = {n: wts[n].shape for n in SMALL_NAMES}
    small = [_unpack_small(s, shapes) for s in small]

    rs["ffn1_u"] = reduce_start(by_device(_tn_matmul(dup1, n1, "ffn1_dwu", tmm=1408, out_dtype=BF16,
                                                     token=small_sum)), "ffn1_u")
    rs["ffn1_d"] = reduce_start(by_device(_tn_matmul(act1, dh1, "ffn1_dwd", scale=0.5, tmm=1408, out_dtype=BF16,
                                                     token=rs["ffn1_u"][4])), "ffn1_d")

    big = {}

    def adam_unit(unit, token):
        part, land = got[unit]
        res = _adam_big(part, land, _pack_unit(wts, unit), _pack_unit(mom, unit), _pack_unit(var, unit),
                        chip_idx, unit, token)
        for k, packed in enumerate(res):
            big.setdefault(k, {}).update(_unpack_unit(packed, unit))
        return res[0]

    done = rs["ffn1_d"][4]
    for unit in ("mid", "ffn2_g", "ffn2_u", "ffn2_d"):
        done = adam_unit(unit, done)
    for unit in ("ffn1_g", "ffn1_u", "ffn1_d"):
        got[unit] = _ici_wait(rs[unit], done, "rs_" + unit + "_wait", False)
        done = adam_unit(unit, done)

    outs = [loss, dx[None]]
    for k in range(4):
        for n in WEIGHT_ORDER:
            outs.append(big[k][n] if n in BIG_NAMES else small[k][n])
    return tuple(outs)
```

```python
import numpy as np

import jax
import jax.numpy as jnp
from jax import lax
from jax.experimental import pallas as pl
from jax.experimental.pallas import tpu as pltpu

F32 = jnp.float32
BF16 = jnp.bfloat16

N_DEV = 8
D_MODEL = 1024
D_FF = 2816
MLA_HEADS = 4
NOPE = 128
ROPE = 64
HEAD_PAD = 256
V_DIM = 128
Q_RANK = 256
KV_RANK = 128
POOL_WINDOWS = (2, 4, 8, 16)
POOL_CH = 128
POOL_HALO = 16
N_MEM = 256
MEM_HEADS = 4
MEM_HD = 256
ROPE_BASE = 10000.0
RMS_EPS = 1e-6
ATTN_SCALE = (NOPE + ROPE) ** -0.5
MEM_SCALE = MEM_HD ** -0.5
NEG_BIG = -1e30

ADAM_LR = 0.001
ADAM_B1 = 0.9
ADAM_B2 = 0.999
ADAM_EPS = 1e-08
ADAM_WD = 0.01
ADAM_STEP = 10
ADAM_C1 = 1.0 - ADAM_B1 ** ADAM_STEP
ADAM_C2 = 1.0 - ADAM_B2 ** ADAM_STEP

VMEM_LIMIT_BYTES = 56 * 1024 * 1024
BF16_ROWS = 16

GROUP_SEGS = {
    "ffn1": (("ffn1_g", 352), ("ffn1_u", 352), ("ffn1_d", 352)),
    "mid": (("w_in", 128), ("w_out", 128), ("w_mq", 128), ("w_mo", 128), ("w_mkv", 256), ("w_q", 32), ("w_kv", 16)),
    "ffn2": (("ffn2_g", 352), ("ffn2_u", 352), ("ffn2_d", 352)),
}
SEG_OFF = {}
GROUP_ROWS = {}
for _g, _segs in GROUP_SEGS.items():
    _o = 0
    for _n, _r in _segs:
        SEG_OFF[_n] = (_o, _r)
        _o += _r
    GROUP_ROWS[_g] = _o

SMALL_ROWS = (("ffn1_norm", 8), ("mix_norm", 8), ("q_norm", 8), ("kv_norm", 8), ("pool_w", 512), ("pool_scale", 8),
              ("xattn_norm", 8), ("mem_norm", 8), ("ffn2_norm", 8), ("final_norm", 8), ("loss", 8))
SMALL_OFF = {}
_o = 0
for _n, _r in SMALL_ROWS:
    SMALL_OFF[_n] = (_o, _r)
    _o += _r


def _cparams(**kw):
    return pltpu.CompilerParams(vmem_limit_bytes=VMEM_LIMIT_BYTES, **kw)


def _row_tile(rows, limit):
    best = None
    for cand in range(BF16_ROWS, min(rows, limit) + 1, BF16_ROWS):
        if rows % cand == 0:
            best = cand
    assert best is not None, rows
    return best


def _dot_nn(a, b):
    return lax.dot_general(a, b, (((1,), (0,)), ((), ())), preferred_element_type=F32)


def _dot_nt(a, b):
    return lax.dot_general(a, b, (((1,), (1,)), ((), ())), preferred_element_type=F32)


def _dot_tn(a, b):
    return lax.dot_general(a, b, (((0,), (0,)), ((), ())), preferred_element_type=F32)


def _rms_fwd(x, g):
    r = lax.rsqrt(jnp.mean(x * x, axis=-1, keepdims=True) + RMS_EPS)
    return x * r * g, r


def _rms_bwd(dy, x, g, r):
    xhat = x * r
    dyg = dy * g
    dx = r * (dyg - xhat * jnp.mean(dyg * xhat, axis=-1, keepdims=True))
    dg = jnp.sum(dy * xhat, axis=0, keepdims=True)
    return dx, dg


def _accumulate(ref, val, first):
    if isinstance(first, bool):
        if first:
            ref[...] = val
        else:
            ref[...] += val
        return

    @pl.when(first)
    def _():
        ref[...] = val

    @pl.when(jnp.logical_not(first))
    def _():
        ref[...] += val


def _call_after(token, body, in_specs, args, **kw):
    if token is not None:
        inner = body
        body = lambda tok_ref, *refs: inner(*refs)
        in_specs = [pl.BlockSpec((8, 128), lambda *_: (0, 0))] + list(in_specs)
        args = (token,) + tuple(args)
    return pl.pallas_call(body, in_specs=in_specs, **kw)(*args)


def _resident(shape):
    return pl.BlockSpec(shape, lambda *_: (0,) * len(shape), pipeline_mode=pl.Buffered(1))


def _rope_tables(pos_col, tab):
    ang = pos_col.astype(F32) * tab[0:1, :]
    return jnp.cos(ang), jnp.sin(ang) * tab[1:2, :]


def _swap_halves(x):
    lane = lax.broadcasted_iota(jnp.int32, x.shape, 1)
    return jnp.where((lane % 64) < 32, pltpu.roll(x, 96, 1), pltpu.roll(x, 32, 1))


def _rope_apply(x, cos_t, sin_t):
    return x * cos_t + _swap_halves(x) * sin_t


def _rope_apply_t(dy, cos_t, sin_t):
    return dy * cos_t + _swap_halves(dy * sin_t)


def _ffn_fwd(h, g, wg_t, wu_t, wd, name, token=None, head=None):
    t, d = h.shape
    f = wg_t.shape[0]
    tm, tf = min(512, t), 256
    nf = f // tf
    n_in = 5 if head is None else 7

    def body(*refs):
        h_ref, g_ref, wg_ref, wu_ref, wd_ref = refs[:5]
        ho_ref, n_ref, gate_ref, up_ref = refs[n_in:n_in + 4]
        nb_sc, acc_sc = refs[-2:]
        y, _ = _rms_fwd(h_ref[...], g_ref[...])
        nb = y.astype(BF16)
        nb_sc[...] = nb
        n_ref[...] = nb
        acc_sc[...] = jnp.zeros_like(acc_sc)

        def f_tile(j):
            rows = pl.ds(pl.multiple_of(j * tf, tf), tf)
            nb = nb_sc[...]
            gt = _dot_nt(nb, wg_ref[rows, :])
            ut = _dot_nt(nb, wu_ref[rows, :])
            gate_ref[j] = gt.astype(BF16)
            up_ref[j] = ut.astype(BF16)
            act = (gt * jax.nn.sigmoid(gt)) * ut
            return _dot_nn(act.astype(BF16), wd_ref[rows, :])

        def pair(p, carry):
            acc_sc[...] += f_tile(2 * p) + f_tile(2 * p + 1)
            return carry

        lax.fori_loop(0, nf // 2, pair, 0)
        if nf % 2:
            acc_sc[...] += f_tile(nf - 1)
        ho = h_ref[...] + 0.5 * acc_sc[...]
        if head is None:
            ho_ref[...] = ho
            return
        t_ref, gf_ref = refs[5:7]
        loss_ref, dgf_ref = refs[n_in + 4:n_in + 6]
        gg = gf_ref[...]
        y, r = _rms_fwd(ho, gg)
        err = y - t_ref[...]
        part = 0.5 * jnp.sum(jnp.mean(err * err, axis=-1, keepdims=True), axis=0, keepdims=True)
        dx, dg = _rms_bwd(err * (1.0 / d), ho, gg, r)
        ho_ref[...] = dx
        first = pl.program_id(0) == 0
        _accumulate(loss_ref, jnp.broadcast_to(part, loss_ref.shape), first)
        _accumulate(dgf_ref, dg, first)

    row = pl.BlockSpec((tm, d), lambda i: (i, 0))
    tiles = pl.BlockSpec((nf, tm, tf), lambda i: (0, i, 0))
    in_specs = [row, _resident((1, d)), _resident((f, d)), _resident((f, d)), _resident((f, d))]
    args = (h, g, wg_t, wu_t, wd)
    out_specs = [row, row, tiles, tiles]
    out_shape = [jax.ShapeDtypeStruct((t, d), F32), jax.ShapeDtypeStruct((t, d), BF16),
                 jax.ShapeDtypeStruct((nf, t, tf), BF16), jax.ShapeDtypeStruct((nf, t, tf), BF16)]
    if head is not None:
        in_specs += [row, _resident((1, d))]
        args += tuple(head)
        out_specs += [pl.BlockSpec((8, 128), lambda i: (0, 0)), pl.BlockSpec((1, d), lambda i: (0, 0))]
        out_shape += [jax.ShapeDtypeStruct((8, 128), F32), jax.ShapeDtypeStruct((1, d), F32)]
    return _call_after(
        token, body, in_specs, args, name=name, grid=(t // tm,), out_specs=out_specs, out_shape=out_shape,
        scratch_shapes=[pltpu.VMEM((tm, d), BF16), pltpu.VMEM((tm, d), F32)],
        compiler_params=_cparams(),
    )


def _ffn_bwd_data(dho, h, g, gate, up, wg_t, wu_t, wd, name, token=None):
    t, d = h.shape
    f = wg_t.shape[0]
    tm, tf = min(1024, t), 256
    parts = 2 if tm % 512 == 0 else 1
    tp = tm // parts
    nf = f // tf

    def body(dho_ref, h_ref, g_ref, gate_ref, up_ref, wg_ref, wu_ref, wd_ref,
             dh_ref, dgate_ref, dup_ref, act_ref, dg_ref, dhb_sc, acc_sc):
        i, j = pl.program_id(0), pl.program_id(1)

        @pl.when(j == 0)
        def _():
            dhb_sc[...] = (0.5 * dho_ref[...]).astype(BF16)
            acc_sc[...] = jnp.zeros_like(acc_sc)

        for r in range(parts):
            rows = pl.ds(r * tp, tp)
            dact = _dot_nt(dhb_sc[rows, :], wd_ref[...])
            gt = gate_ref[0, rows, :].astype(F32)
            ut = up_ref[0, rows, :].astype(F32)
            sg = jax.nn.sigmoid(gt)
            silu = gt * sg
            dgb = (dact * ut * (sg * (1.0 + gt * (1.0 - sg)))).astype(BF16)
            dub = (dact * silu).astype(BF16)
            act_ref[rows, :] = (silu * ut).astype(BF16)
            dgate_ref[rows, :] = dgb
            dup_ref[rows, :] = dub
            acc_sc[rows, :] += _dot_nn(dgb, wg_ref[...]) + _dot_nn(dub, wu_ref[...])

        @pl.when(j == nf - 1)
        def _():
            x = h_ref[...]
            gg = g_ref[...]
            _, r = _rms_fwd(x, gg)
            dx, dg = _rms_bwd(acc_sc[...], x, gg, r)
            dh_ref[...] = dho_ref[...] + dx
            _accumulate(dg_ref, dg, i == 0)

    return _call_after(
        token, body,
        [pl.BlockSpec((tm, d), lambda i, j: (i, 0)),
         pl.BlockSpec((tm, d), lambda i, j: (i, 0)),
         pl.BlockSpec((1, d), lambda i, j: (0, 0)),
         pl.BlockSpec((1, tm, tf), lambda i, j: (j, i, 0)),
         pl.BlockSpec((1, tm, tf), lambda i, j: (j, i, 0)),
         pl.BlockSpec((tf, d), lambda i, j: (j, 0)),
         pl.BlockSpec((tf, d), lambda i, j: (j, 0)),
         pl.BlockSpec((tf, d), lambda i, j: (j, 0))],
        (dho, h, g, gate, up, wg_t, wu_t, wd),
        name=name, grid=(t // tm, nf),
        out_specs=[pl.BlockSpec((tm, d), lambda i, j: (i, 0)),
                   pl.BlockSpec((tm, tf), lambda i, j: (i, j)),
                   pl.BlockSpec((tm, tf), lambda i, j: (i, j)),
                   pl.BlockSpec((tm, tf), lambda i, j: (i, j)),
                   pl.BlockSpec((1, d), lambda i, j: (0, 0))],
        out_shape=[jax.ShapeDtypeStruct((t, d), F32), jax.ShapeDtypeStruct((t, f), BF16),
                   jax.ShapeDtypeStruct((t, f), BF16), jax.ShapeDtypeStruct((t, f), BF16),
                   jax.ShapeDtypeStruct((1, d), F32)],
        scratch_shapes=[pltpu.VMEM((tm, d), BF16), pltpu.VMEM((tm, d), F32)],
        compiler_params=_cparams(),
    )


def _tn_matmul(a, b, name, scale=1.0, tmm=None, out_dtype=F32, token=None):
    t, m = a.shape
    n = b.shape[1]
    tmm = m if tmm is None else tmm
    tk = min(2048, t)
    nk = t // tk

    def product(a_ref, b_ref):
        prod = _dot_tn(a_ref[...].astype(BF16), b_ref[...].astype(BF16))
        return prod * scale if scale != 1.0 else prod

    def body_f32(a_ref, b_ref, o_ref):
        _accumulate(o_ref, product(a_ref, b_ref), pl.program_id(1) == 0)

    def body_cast(a_ref, b_ref, o_ref, acc_sc):
        k = pl.program_id(1)
        _accumulate(acc_sc, product(a_ref, b_ref), k == 0)

        @pl.when(k == nk - 1)
        def _():
            o_ref[...] = acc_sc[...].astype(out_dtype)

    direct = out_dtype == F32
    return _call_after(
        token, body_f32 if direct else body_cast,
        [pl.BlockSpec((tk, tmm), lambda i, k: (k, i)),
         pl.BlockSpec((tk, n), lambda i, k: (k, 0))],
        (a, b),
        name=name, grid=(m // tmm, nk),
        out_specs=pl.BlockSpec((tmm, n), lambda i, k: (i, 0)),
        out_shape=jax.ShapeDtypeStruct((m, n), out_dtype),
        scratch_shapes=[] if direct else [pltpu.VMEM((tmm, n), F32)],
        compiler_params=_cparams(),
    )


def _mix_prep(h1, mix_norm, w_in, q_norm, wq_t, kv_norm, wkv, pos, rope_tab, token=None):
    t, d = h1.shape
    tm = min(512, t)

    def body(h_ref, gm_ref, win_ref, gq_ref, wq_ref, gkv_ref, wkv_ref, pos_ref, tab_ref,
             u_ref, z_ref, qn_ref, kvn_ref, q_ref, k_ref, v_ref):
        u, _ = _rms_fwd(h_ref[...], gm_ref[...])
        ub = u.astype(BF16)
        u_ref[...] = ub
        z = _dot_nn(ub, win_ref[...])
        z_ref[...] = z
        cos_t, sin_t = _rope_tables(pos_ref[...], tab_ref[...])
        qn, _ = _rms_fwd(z[:, 0:Q_RANK], gq_ref[...])
        qnb = qn.astype(BF16)
        qn_ref[...] = qnb
        q = _dot_nt(qnb, wq_ref[...])
        kvn, _ = _rms_fwd(z[:, Q_RANK:Q_RANK + KV_RANK], gkv_ref[...])
        kvnb = kvn.astype(BF16)
        kvn_ref[...] = kvnb
        kv = _dot_nn(kvnb, wkv_ref[...])
        k_pe = _rope_apply(z[:, Q_RANK + KV_RANK:Q_RANK + KV_RANK + 128], cos_t, sin_t)
        ones = jnp.ones((tm, V_DIM), F32)
        for hh in range(MLA_HEADS):
            b = hh * HEAD_PAD
            q_pe = _rope_apply(q[:, b + NOPE:b + HEAD_PAD], cos_t, sin_t)
            q_ref[hh] = jnp.concatenate([q[:, b:b + NOPE], q_pe], axis=-1).astype(BF16)
            k_ref[hh] = jnp.concatenate([kv[:, b:b + NOPE], k_pe], axis=-1).astype(BF16)
            v_ref[hh] = jnp.concatenate([kv[:, b + NOPE:b + HEAD_PAD], ones], axis=-1).astype(BF16)

    full = lambda shape: pl.BlockSpec(shape, lambda i: (0,) * len(shape))
    return _call_after(
        token, body,
        [pl.BlockSpec((tm, d), lambda i: (i, 0)), _resident((1, d)), _resident(w_in.shape), _resident((1, Q_RANK)),
         _resident(wq_t.shape), _resident((1, KV_RANK)), _resident(wkv.shape),
         pl.BlockSpec((tm, 1), lambda i: (i, 0)), _resident(rope_tab.shape)],
        (h1, mix_norm, w_in, q_norm, wq_t, kv_norm, wkv, pos, rope_tab),
        name="mix_prep", grid=(t // tm,),
        out_specs=[pl.BlockSpec((tm, d), lambda i: (i, 0)),
                   pl.BlockSpec((tm, d), lambda i: (i, 0)),
                   pl.BlockSpec((tm, Q_RANK), lambda i: (i, 0)),
                   pl.BlockSpec((tm, KV_RANK), lambda i: (i, 0)),
                   pl.BlockSpec((MLA_HEADS, tm, HEAD_PAD), lambda i: (0, i, 0)),
                   pl.BlockSpec((MLA_HEADS, tm, HEAD_PAD), lambda i: (0, i, 0)),
                   pl.BlockSpec((MLA_HEADS, tm, 2 * V_DIM), lambda i: (0, i, 0))],
        out_shape=[jax.ShapeDtypeStruct((t, d), BF16), jax.ShapeDtypeStruct((t, d), F32),
                   jax.ShapeDtypeStruct((t, Q_RANK), BF16), jax.ShapeDtypeStruct((t, KV_RANK), BF16),
                   jax.ShapeDtypeStruct((MLA_HEADS, t, HEAD_PAD), BF16),
                   jax.ShapeDtypeStruct((MLA_HEADS, t, HEAD_PAD), BF16),
                   jax.ShapeDtypeStruct((MLA_HEADS, t, 2 * V_DIM), BF16)],
        compiler_params=_cparams(),
    )


def _causal_mask(s):
    row = lax.broadcasted_iota(jnp.int32, s.shape, 0)
    col = lax.broadcasted_iota(jnp.int32, s.shape, 1)
    return jnp.where(col <= row, s, NEG_BIG)


def _attn_fwd(q, k, v):
    nh, t, _ = q.shape
    tq = tk = min(512, t)
    nq, nk = t // tq, t // tk

    pairs = [(i, j) for i in range(nq) for j in range(i + 1)]
    qi = jnp.asarray(np.array([i for i, _ in pairs], np.int32))
    kj = jnp.asarray(np.array([j for _, j in pairs], np.int32))

    def body(qi_ref, kj_ref, q_ref, k_ref, v_ref, o_ref, lse_ref, m_sc, acc_sc):
        n = pl.program_id(0)
        i, j = qi_ref[n], kj_ref[n]

        @pl.when(j == 0)
        def _():
            m_sc[...] = jnp.full_like(m_sc, NEG_BIG)
            acc_sc[...] = jnp.zeros_like(acc_sc)

        def step(diagonal):
            for hh in range(nh):
                s = _dot_nt(q_ref[hh], k_ref[hh]) * ATTN_SCALE
                if diagonal:
                    s = _causal_mask(s)
                m_old = m_sc[hh]
                m_new = jnp.maximum(m_old, jnp.max(s, axis=-1, keepdims=True))
                p = jnp.exp(s - m_new).astype(BF16)
                acc_sc[hh] = jnp.exp(m_old - m_new) * acc_sc[hh] + _dot_nn(p, v_ref[hh])
                m_sc[hh] = m_new

        @pl.when(j < i)
        def _():
            step(False)

        @pl.when(j == i)
        def _():
            step(True)
            for hh in range(nh):
                acc = acc_sc[hh]
                l = acc[:, V_DIM:2 * V_DIM]
                o_ref[:, hh * V_DIM:(hh + 1) * V_DIM] = (acc[:, 0:V_DIM] / l).astype(BF16)
                lse_ref[hh] = m_sc[hh] + jnp.log(l[:, 0:1])

    q_map = lambda n, qi_ref, kj_ref: (0, qi_ref[n], 0)
    kv_map = lambda n, qi_ref, kj_ref: (0, kj_ref[n], 0)
    return pl.pallas_call(
        body, name="attn_fwd",
        grid_spec=pltpu.PrefetchScalarGridSpec(
            num_scalar_prefetch=2, grid=(len(pairs),),
            in_specs=[pl.BlockSpec((nh, tq, HEAD_PAD), q_map),
                      pl.BlockSpec((nh, tk, HEAD_PAD), kv_map),
                      pl.BlockSpec((nh, tk, 2 * V_DIM), kv_map)],
            out_specs=[pl.BlockSpec((tq, nh * V_DIM), lambda n, qi_ref, kj_ref: (qi_ref[n], 0)),
                       pl.BlockSpec((nh, tq, 1), q_map)],
            scratch_shapes=[pltpu.VMEM((nh, tq, 1), F32), pltpu.VMEM((nh, tq, 2 * V_DIM), F32)]),
        out_shape=[jax.ShapeDtypeStruct((t, nh * V_DIM), BF16), jax.ShapeDtypeStruct((nh, t, 1), F32)],
        compiler_params=_cparams(),
    )(qi, kj, q, k, v)


def _attn_delta(o, do):
    t, w = o.shape
    nh = w // V_DIM
    tm = min(512, t)

    def body(o_ref, do_ref, d_ref):
        prod = o_ref[...].astype(F32) * do_ref[...].astype(F32)
        for hh in range(nh):
            d_ref[hh] = jnp.sum(prod[:, hh * V_DIM:(hh + 1) * V_DIM], axis=-1, keepdims=True)

    return pl.pallas_call(
        body, name="attn_delta", grid=(t // tm,),
        in_specs=[pl.BlockSpec((tm, w), lambda i: (i, 0)), pl.BlockSpec((tm, w), lambda i: (i, 0))],
        out_specs=pl.BlockSpec((nh, tm, 1), lambda i: (0, i, 0)),
        out_shape=jax.ShapeDtypeStruct((nh, t, 1), F32),
        compiler_params=_cparams(),
    )(o, do)


ATTN_BWD_HEADS = 2


def _attn_bwd(q, k, v, do, lse, delta):
    nh, t, _ = q.shape
    hp = ATTN_BWD_HEADS
    tq = tk = min(512, t)
    nq, nk = t // tq, t // tk

    pairs = [(j, i) for j in range(nk) for i in range(j, nq)]
    kj = jnp.asarray(np.array([j for j, _ in pairs], np.int32))
    qi = jnp.asarray(np.array([i for _, i in pairs], np.int32))

    def body(kj_ref, qi_ref, q_ref, k_ref, v_ref, do_ref, lse_ref, dlt_ref, dq_ref, dk_ref, dv_ref):
        n = pl.program_id(1)
        j, i = kj_ref[n], qi_ref[n]

        @pl.when(n == 0)
        def _():
            dq_ref[...] = jnp.zeros_like(dq_ref)

        def step(diagonal):
            for hh in range(hp):
                qq, kk = q_ref[hh], k_ref[hh]
                dob = do_ref[:, hh * V_DIM:(hh + 1) * V_DIM]
                s = _dot_nt(qq, kk) * ATTN_SCALE
                if diagonal:
                    s = _causal_mask(s)
                p = jnp.exp(s - lse_ref[hh])
                dpp = _dot_nt(dob, v_ref[hh])
                dsb = (p * (dpp - dlt_ref[hh]) * ATTN_SCALE).astype(BF16)
                _accumulate(dv_ref.at[hh], _dot_tn(p.astype(BF16), dob), diagonal)
                _accumulate(dk_ref.at[hh], _dot_tn(dsb, qq), diagonal)
                dq_ref[hh, pl.ds(pl.multiple_of(i * tq, tq), tq), :] += _dot_nn(dsb, kk)

        @pl.when(i > j)
        def _():
            step(False)

        @pl.when(i == j)
        def _():
            step(True)

    q_map = lambda h, n, kj_ref, qi_ref: (h, qi_ref[n], 0)
    k_map = lambda h, n, kj_ref, qi_ref: (h, kj_ref[n], 0)
    return pl.pallas_call(
        body, name="attn_bwd",
        grid_spec=pltpu.PrefetchScalarGridSpec(
            num_scalar_prefetch=2, grid=(nh // hp, len(pairs)),
            in_specs=[pl.BlockSpec((hp, tq, HEAD_PAD), q_map),
                      pl.BlockSpec((hp, tk, HEAD_PAD), k_map),
                      pl.BlockSpec((hp, tk, V_DIM), k_map),
                      pl.BlockSpec((tq, hp * V_DIM), lambda h, n, kj_ref, qi_ref: (qi_ref[n], h)),
                      pl.BlockSpec((hp, tq, 1), q_map),
                      pl.BlockSpec((hp, tq, 1), q_map)],
            out_specs=[pl.BlockSpec((hp, t, HEAD_PAD), lambda h, n, kj_ref, qi_ref: (h, 0, 0)),
                       pl.BlockSpec((hp, tk, HEAD_PAD), k_map),
                       pl.BlockSpec((hp, tk, V_DIM), k_map)]),
        out_shape=[jax.ShapeDtypeStruct((nh, t, HEAD_PAD), F32), jax.ShapeDtypeStruct((nh, t, HEAD_PAD), F32),
                   jax.ShapeDtypeStruct((nh, t, V_DIM), F32)],
        compiler_params=_cparams(),
    )(kj, qi, q, k, v, do, lse, delta)


def _pool_counts(first_token, rows, w):
    tok = lax.broadcasted_iota(jnp.int32, (rows, POOL_CH), 0) + first_token
    return jnp.minimum(tok + 1, w).astype(F32)


def _pool_centered(zbuf, g, w, i, tm):
    lanes = pl.ds(g * POOL_CH, POOL_CH)
    cur = zbuf[pl.ds(POOL_HALO, tm), lanes]
    win = cur
    for s in range(1, w):
        win = win + zbuf[pl.ds(POOL_HALO - s, tm), lanes]
    return win / _pool_counts(i * tm, tm, w) - cur


def _pool_load(zbuf, z_ref, halo_ref, i, tm):
    @pl.when(i == 0)
    def _():
        zbuf[pl.ds(0, POOL_HALO), :] = jnp.zeros((POOL_HALO, zbuf.shape[1]), F32)

    @pl.when(i > 0)
    def _():
        zbuf[pl.ds(0, POOL_HALO), :] = halo_ref[...]

    zbuf[pl.ds(POOL_HALO, tm), :] = z_ref[...]


def _pool_fwd(z, pool_w, pool_scale):
    t = z.shape[0]
    pw = len(POOL_WINDOWS) * POOL_CH
    tm = min(512, t)
    hb = tm // POOL_HALO

    def body(z_ref, halo_ref, w_ref, sc_ref, p_ref, zbuf):
        i = pl.program_id(0)
        _pool_load(zbuf, z_ref, halo_ref, i, tm)
        for g, w in enumerate(POOL_WINDOWS):
            c = _pool_centered(zbuf, g, w, i, tm)
            y = _dot_nn(c.astype(BF16), w_ref[g]) * sc_ref[:, g * POOL_CH:(g + 1) * POOL_CH]
            p_ref[:, g * POOL_CH:(g + 1) * POOL_CH] = y.astype(BF16)

    return pl.pallas_call(
        body, name="pool_fwd", grid=(t // tm,),
        in_specs=[pl.BlockSpec((tm, pw), lambda i: (i, 1)),
                  pl.BlockSpec((POOL_HALO, pw), lambda i: (jnp.maximum(i * hb - 1, 0), 1)),
                  pl.BlockSpec(pool_w.shape, lambda i: (0, 0, 0)),
                  pl.BlockSpec((1, pw), lambda i: (0, 0))],
        out_specs=pl.BlockSpec((tm, pw), lambda i: (i, 0)),
        out_shape=jax.ShapeDtypeStruct((t, pw), BF16),
        scratch_shapes=[pltpu.VMEM((POOL_HALO + tm, pw), F32)],
        compiler_params=_cparams(),
    )(z, z, pool_w, pool_scale)


def _pool_bwd(dp, z, pool_w, pool_scale):
    t = z.shape[0]
    ng = len(POOL_WINDOWS)
    pw = ng * POOL_CH
    tm = min(512, t)
    hb = tm // POOL_HALO
    nt = t // tm

    def body(dp_ref, dpn_ref, z_ref, halo_ref, w_ref, sc_ref, dz_ref, dw_ref, dsc_ref, zbuf, dbuf):
        i = pl.program_id(0)
        _pool_load(zbuf, z_ref, halo_ref, i, tm)

        @pl.when(i == 0)
        def _():
            dw_ref[...] = jnp.zeros_like(dw_ref)
            dsc_ref[...] = jnp.zeros_like(dsc_ref)

        nxt_ok = (i < nt - 1).astype(F32)
        for g, w in enumerate(POOL_WINDOWS):
            lanes = pl.ds(g * POOL_CH, POOL_CH)
            cols = slice(g * POOL_CH, (g + 1) * POOL_CH)
            sc = sc_ref[:, cols]
            wg = w_ref[g]
            c = _pool_centered(zbuf, g, w, i, tm).astype(BF16)
            ypre = _dot_nn(c, wg)
            dpg = dp_ref[:, cols].astype(F32)
            dsc_ref[:, cols] += jnp.sum(dpg * ypre, axis=0, keepdims=True)
            dyb = (dpg * sc).astype(BF16)
            dw_ref[g] += _dot_tn(c, dyb)
            dd = _dot_nt(dyb, wg)
            dyn = (dpn_ref[:, cols].astype(F32) * sc).astype(BF16)
            ddn = _dot_nt(dyn, wg) * nxt_ok
            dbuf[pl.ds(0, tm), lanes] = dd / _pool_counts(i * tm, tm, w)
            dbuf[pl.ds(tm, POOL_HALO), lanes] = ddn / _pool_counts((i + 1) * tm, POOL_HALO, w)
            acc = -dd
            for s in range(w):
                acc = acc + dbuf[pl.ds(s, tm), lanes]
            dz_ref[:, cols] = acc

    return pl.pallas_call(
        body, name="pool_bwd", grid=(nt,),
        in_specs=[pl.BlockSpec((tm, pw), lambda i: (i, 0)),
                  pl.BlockSpec((POOL_HALO, pw), lambda i: (jnp.minimum((i + 1) * hb, t // POOL_HALO - 1), 0)),
                  pl.BlockSpec((tm, pw), lambda i: (i, 1)),
                  pl.BlockSpec((POOL_HALO, pw), lambda i: (jnp.maximum(i * hb - 1, 0), 1)),
                  pl.BlockSpec(pool_w.shape, lambda i: (0, 0, 0)),
                  pl.BlockSpec((1, pw), lambda i: (0, 0))],
        out_specs=[pl.BlockSpec((tm, pw), lambda i: (i, 0)),
                   pl.BlockSpec((ng, POOL_CH, POOL_CH), lambda i: (0, 0, 0)),
                   pl.BlockSpec((1, pw), lambda i: (0, 0))],
        out_shape=[jax.ShapeDtypeStruct((t, pw), F32), jax.ShapeDtypeStruct((ng, POOL_CH, POOL_CH), F32),
                   jax.ShapeDtypeStruct((1, pw), F32)],
        scratch_shapes=[pltpu.VMEM((POOL_HALO + tm, pw), F32), pltpu.VMEM((tm + POOL_HALO, pw), F32)],
        compiler_params=_cparams(),
    )(dp, dp, z, z, pool_w, pool_scale)


def _mla_bwd(dq_h, dk_h, dv_h, z, dz_pool, h1, dh2, mix_norm, w_in, q_norm, wq_t, kv_norm, wkv, pos, rope_tab):
    t, d = h1.shape
    tm = min(512, t)

    def body(dqh_ref, dkh_ref, dvh_ref, z_ref, dzp_ref, h_ref, dh2_ref, gm_ref, win_ref, gq_ref, wq_ref, gkv_ref,
             wkv_ref, pos_ref, tab_ref, dh1_ref, dq_ref, dkv_ref, dz_ref, dgq_ref, dgkv_ref, dgm_ref):
        i = pl.program_id(0)
        first = i == 0
        cos_t, sin_t = _rope_tables(pos_ref[...], tab_ref[...])
        dq_parts, dkv_parts = [], []
        dk_pe = jnp.zeros((tm, 128), F32)
        for hh in range(MLA_HEADS):
            dqh = dqh_ref[hh]
            dq_parts += [dqh[:, 0:NOPE], _rope_apply_t(dqh[:, NOPE:HEAD_PAD], cos_t, sin_t)]
            dkh = dkh_ref[hh]
            dkv_parts += [dkh[:, 0:NOPE], dvh_ref[hh]]
            dk_pe = dk_pe + dkh[:, NOPE:HEAD_PAD]
        dqb = jnp.concatenate(dq_parts, axis=-1).astype(BF16)
        dkvb = jnp.concatenate(dkv_parts, axis=-1).astype(BF16)
        dq_ref[...] = dqb
        dkv_ref[...] = dkvb
        z = z_ref[...]
        c_q = z[:, 0:Q_RANK]
        gq = gq_ref[...]
        _, rq = _rms_fwd(c_q, gq)
        dcq, dgq = _rms_bwd(_dot_nn(dqb, wq_ref[...]), c_q, gq, rq)
        c_kv = z[:, Q_RANK:Q_RANK + KV_RANK]
        gkv = gkv_ref[...]
        _, rkv = _rms_fwd(c_kv, gkv)
        dckv, dgkv = _rms_bwd(_dot_nt(dkvb, wkv_ref[...]), c_kv, gkv, rkv)
        dkr = _rope_apply_t(dk_pe, cos_t, sin_t)
        dzb = jnp.concatenate([dcq, dckv, dkr, dzp_ref[...]], axis=-1).astype(BF16)
        dz_ref[...] = dzb
        x = h_ref[...]
        gm = gm_ref[...]
        _, rm = _rms_fwd(x, gm)
        dx, dgm = _rms_bwd(_dot_nt(dzb, win_ref[...]), x, gm, rm)
        dh1_ref[...] = dh2_ref[...] + dx
        _accumulate(dgq_ref, dgq, first)
        _accumulate(dgkv_ref, dgkv, first)
        _accumulate(dgm_ref, dgm, first)

    full = lambda shape: pl.BlockSpec(shape, lambda i: (0,) * len(shape))
    row = lambda w: pl.BlockSpec((tm, w), lambda i: (i, 0))
    head = lambda w: pl.BlockSpec((MLA_HEADS, tm, w), lambda i: (0, i, 0))
    pw = len(POOL_WINDOWS) * POOL_CH
    return pl.pallas_call(
        body, name="mla_bwd", grid=(t // tm,),
        in_specs=[head(HEAD_PAD), head(HEAD_PAD), head(V_DIM), row(d), row(pw), row(d), row(d),
                  _resident((1, d)), _resident(w_in.shape), _resident((1, Q_RANK)), _resident(wq_t.shape),
                  _resident((1, KV_RANK)), _resident(wkv.shape), row(1), _resident(rope_tab.shape)],
        out_specs=[row(d), row(d), row(d), row(d), full((1, Q_RANK)), full((1, KV_RANK)), full((1, d))],
        out_shape=[jax.ShapeDtypeStruct((t, d), F32), jax.ShapeDtypeStruct((t, d), BF16),
                   jax.ShapeDtypeStruct((t, d), BF16), jax.ShapeDtypeStruct((t, d), BF16),
                   jax.ShapeDtypeStruct((1, Q_RANK), F32), jax.ShapeDtypeStruct((1, KV_RANK), F32),
                   jax.ShapeDtypeStruct((1, d), F32)],
        compiler_params=_cparams(),
    )(dq_h, dk_h, dv_h, z, dz_pool, h1, dh2, mix_norm, w_in, q_norm, wq_t, kv_norm, wkv, pos, rope_tab)


def _mem_kv(mem, mem_norm, wmkv):
    n, d = mem.shape

    def body(mem_ref, g_ref, w_ref, memn_ref, k_ref, v_ref):
        y, _ = _rms_fwd(mem_ref[...], g_ref[...])
        yb = y.astype(BF16)
        memn_ref[...] = yb
        for hh in range(MEM_HEADS):
            k_ref[hh] = _dot_nn(yb, w_ref[hh]).astype(BF16)
            v_ref[hh] = _dot_nn(yb, w_ref[MEM_HEADS + hh]).astype(BF16)

    return pl.pallas_call(
        body, name="mem_kv",
        out_shape=[jax.ShapeDtypeStruct((n, d), BF16), jax.ShapeDtypeStruct((MEM_HEADS, n, MEM_HD), BF16),
                   jax.ShapeDtypeStruct((MEM_HEADS, n, MEM_HD), BF16)],
        compiler_params=_cparams(),
    )(mem, mem_norm, wmkv)


def _mem_softmax(qb, km):
    s = _dot_nt(qb, km) * MEM_SCALE
    e = jnp.exp(s - jnp.max(s, axis=-1, keepdims=True))
    return e / jnp.sum(e, axis=-1, keepdims=True)


def _xattn_fwd(h1, a, p, w_out, g, wmq, km, vm, wmo, token=None):
    t, d = h1.shape
    tm = min(512, t)
    half = a.shape[1]

    def body(h_ref, a_ref, p_ref, wo_ref, g_ref, wmq_ref, km_ref, vm_ref, wmo_ref,
             h2_ref, h3_ref, hn_ref, q_ref, o_ref):
        h2 = h_ref[...] + _dot_nn(a_ref[...], wo_ref[0:half, :]) + _dot_nn(p_ref[...], wo_ref[half:2 * half, :])
        h2_ref[...] = h2
        hn, _ = _rms_fwd(h2, g_ref[...])
        hnb = hn.astype(BF16)
        hn_ref[...] = hnb
        qb = _dot_nn(hnb, wmq_ref[...]).astype(BF16)
        q_ref[...] = qb
        outs = []
        for hh in range(MEM_HEADS):
            pr = _mem_softmax(qb[:, hh * MEM_HD:(hh + 1) * MEM_HD], km_ref[hh])
            outs.append(_dot_nn(pr.astype(BF16), vm_ref[hh]))
        ob = jnp.concatenate(outs, axis=-1).astype(BF16)
        o_ref[...] = ob
        h3_ref[...] = h2 + _dot_nn(ob, wmo_ref[...])

    full = lambda shape: pl.BlockSpec(shape, lambda i: (0,) * len(shape))
    row = lambda w: pl.BlockSpec((tm, w), lambda i: (i, 0))
    return _call_after(
        token, body,
        [row(d), row(half), row(half), _resident(w_out.shape), _resident((1, d)), _resident(wmq.shape),
         _resident(km.shape), _resident(vm.shape), _resident(wmo.shape)],
        (h1, a, p, w_out, g, wmq, km, vm, wmo),
        name="xattn_fwd", grid=(t // tm,),
        out_specs=[row(d), row(d), row(d), row(d), row(d)],
        out_shape=[jax.ShapeDtypeStruct((t, d), F32), jax.ShapeDtypeStruct((t, d), F32),
                   jax.ShapeDtypeStruct((t, d), BF16), jax.ShapeDtypeStruct((t, d), BF16),
                   jax.ShapeDtypeStruct((t, d), BF16)],
        compiler_params=_cparams(),
    )


def _xattn_bwd(dh3, h2, qm, g, wmq, km, vm, wmo, w_out, token=None):
    t, d = h2.shape
    tm = min(512, t)
    half = d // 2

    def body(dh3_ref, h2_ref, q_ref, g_ref, wmq_ref, km_ref, vm_ref, wmo_ref, wo_ref,
             dh2_ref, dq_ref, da_ref, dp_ref, dk_ref, dv_ref, dg_ref):
        i = pl.program_id(0)
        first = i == 0

        @pl.when(first)
        def _():
            dk_ref[...] = jnp.zeros_like(dk_ref)
            dv_ref[...] = jnp.zeros_like(dv_ref)

        dh3 = dh3_ref[...]
        dob = _dot_nt(dh3.astype(BF16), wmo_ref[...]).astype(BF16)
        qb = q_ref[...]
        dq_parts = []
        for hh in range(MEM_HEADS):
            cols = slice(hh * MEM_HD, (hh + 1) * MEM_HD)
            kk, vv = km_ref[hh], vm_ref[hh]
            pr = _mem_softmax(qb[:, cols], kk)
            doh = dob[:, cols]
            dv_ref[hh] += _dot_tn(pr.astype(BF16), doh)
            dpp = _dot_nt(doh, vv)
            dsb = (pr * (dpp - jnp.sum(dpp * pr, axis=-1, keepdims=True)) * MEM_SCALE).astype(BF16)
            dq_parts.append(_dot_nn(dsb, kk))
            dk_ref[hh] += _dot_tn(dsb, qb[:, cols])
        dqb = jnp.concatenate(dq_parts, axis=-1).astype(BF16)
        dq_ref[...] = dqb
        x = h2_ref[...]
        gg = g_ref[...]
        _, r = _rms_fwd(x, gg)
        dx, dg = _rms_bwd(_dot_nt(dqb, wmq_ref[...]), x, gg, r)
        dh2 = dh3 + dx
        dh2_ref[...] = dh2
        dap = _dot_nt(dh2.astype(BF16), wo_ref[...])
        da_ref[...] = dap[:, 0:half].astype(BF16)
        dp_ref[...] = dap[:, half:d].astype(BF16)
        _accumulate(dg_ref, dg, first)

    full = lambda shape: pl.BlockSpec(shape, lambda i: (0,) * len(shape))
    row = lambda w: pl.BlockSpec((tm, w), lambda i: (i, 0))
    return _call_after(
        token, body,
        [row(d), row(d), row(d), _resident((1, d)), _resident(wmq.shape), _resident(km.shape), _resident(vm.shape),
         _resident(wmo.shape), _resident(w_out.shape)],
        (dh3, h2, qm, g, wmq, km, vm, wmo, w_out),
        name="xattn_bwd", grid=(t // tm,),
        out_specs=[row(d), row(d), row(half), row(half), full(km.shape), full(vm.shape), full((1, d))],
        out_shape=[jax.ShapeDtypeStruct((t, d), F32), jax.ShapeDtypeStruct((t, d), BF16),
                   jax.ShapeDtypeStruct((t, half), BF16), jax.ShapeDtypeStruct((t, half), BF16),
                   jax.ShapeDtypeStruct(km.shape, F32), jax.ShapeDtypeStruct(vm.shape, F32),
                   jax.ShapeDtypeStruct((1, d), F32)],
        compiler_params=_cparams(),
    )


def _mem_kv_bwd(dkm, dvm, memn, mem, mem_norm, wmkv):
    n, d = mem.shape

    def body(dk_ref, dv_ref, memn_ref, mem_ref, g_ref, w_ref, dw_ref, dg_ref):
        memn = memn_ref[...]
        dmemn = jnp.zeros((n, d), F32)
        for s in range(2 * MEM_HEADS):
            src = dk_ref[s] if s < MEM_HEADS else dv_ref[s - MEM_HEADS]
            db = src.astype(BF16)
            dw_ref[s] = _dot_tn(memn, db)
            dmemn = dmemn + _dot_nt(db, w_ref[s])
        x = mem_ref[...]
        gg = g_ref[...]
        _, r = _rms_fwd(x, gg)
        _, dg = _rms_bwd(dmemn, x, gg, r)
        dg_ref[...] = dg

    return pl.pallas_call(
        body, name="mem_kv_bwd",
        out_shape=[jax.ShapeDtypeStruct(wmkv.shape, F32), jax.ShapeDtypeStruct((1, d), F32)],
        compiler_params=_cparams(),
    )(dkm, dvm, memn, mem, mem_norm, wmkv)


MESH_ID = pl.DeviceIdType.MESH
ANY = pl.BlockSpec(memory_space=pl.ANY)


def _coords():
    return lax.axis_index("x"), lax.axis_index("y"), lax.axis_index("c")


def _other_chips(x, y):
    return [(1 - x, y), (x, 1 - y), (1 - x, 1 - y)]


def _core_reduce(g, tag):
    _, r, w = g.shape

    def body(g_ref, part_ref, own_sc, recv_sc, send_sems, recv_sems, local_sems):
        x, y, c = _coords()
        sent, local = [], []
        for chip in range(4):
            sent.append(pltpu.make_async_remote_copy(
                src_ref=g_ref.at[2 * chip + (1 - c)], dst_ref=recv_sc.at[chip],
                send_sem=send_sems.at[chip], recv_sem=recv_sems.at[chip],
                device_id=(x, y, 1 - c), device_id_type=MESH_ID))
            local.append(pltpu.make_async_copy(g_ref.at[2 * chip + c], own_sc.at[chip], local_sems.at[chip]))
        for cp in sent + local:
            cp.start()
        for chip in range(4):
            local[chip].wait()
            sent[chip].wait_recv()
            part_ref[chip] = (own_sc[chip].astype(F32) + recv_sc[chip].astype(F32)).astype(part_ref.dtype)
        for cp in sent:
            cp.wait_send()

    return pl.pallas_call(
        body, name="core_reduce_" + tag,
        out_shape=jax.ShapeDtypeStruct((4, r, w), g.dtype),
        in_specs=[ANY], out_specs=pl.BlockSpec(memory_space=pltpu.VMEM),
        scratch_shapes=[pltpu.VMEM((4, r, w), g.dtype), pltpu.VMEM((4, r, w), g.dtype),
                        pltpu.SemaphoreType.DMA((4,)), pltpu.SemaphoreType.DMA((4,)), pltpu.SemaphoreType.DMA((4,))],
        compiler_params=_cparams(),
    )(g)


HBM_SPEC = pl.BlockSpec(memory_space=pltpu.HBM)
SEM_SPEC = pl.BlockSpec(memory_space=pltpu.SEMAPHORE)
SPLIT_EFFECT = pltpu.SideEffectType.DATAFLOW_SIDE_EFFECTING


def _ici_refs(gather, src_ref, land_ref, j, px, py, slot_chip, c):
    if gather:
        return src_ref, land_ref.at[:, 4 * slot_chip[0] + 2 * slot_chip[1] + c]
    return src_ref.at[2 * px + py], land_ref.at[j]


def _ici_start(src, after, name, gather):
    r, w = src.shape[-2:]
    land_shape = (src.shape[0], N_DEV, r, w) if gather else (3, r, w)

    def body(src_ref, land_ref, after_ref, send_sems, recv_sems, src_thru, land_thru, token):
        x, y, c = _coords()
        for j, (px, py) in enumerate(_other_chips(x, y)):
            s_ref, d_ref = _ici_refs(gather, src_ref, land_ref, j, px, py, (x, y), c)
            pltpu.make_async_remote_copy(
                src_ref=s_ref, dst_ref=d_ref, send_sem=send_sems.at[j], recv_sem=recv_sems.at[j],
                device_id=(px, py, c), device_id_type=MESH_ID).start()
        token[...] = jnp.zeros_like(token)

    return pl.pallas_call(
        body, name=name,
        out_shape=(pltpu.SemaphoreType.DMA((3,)), pltpu.SemaphoreType.DMA((3,)), pltpu.HBM(src.shape, src.dtype),
                   pltpu.HBM(land_shape, src.dtype), jax.ShapeDtypeStruct((8, 128), F32)),
        in_specs=(HBM_SPEC, HBM_SPEC, ANY),
        out_specs=(SEM_SPEC, SEM_SPEC, HBM_SPEC, HBM_SPEC, pl.BlockSpec(memory_space=pltpu.VMEM)),
        input_output_aliases={0: 2, 1: 3},
        compiler_params=pltpu.CompilerParams(has_side_effects=SPLIT_EFFECT),
    )(pltpu.with_memory_space_constraint(src, pltpu.HBM),
      pltpu.with_memory_space_constraint(lax.empty(land_shape, src.dtype), pltpu.HBM), after)


def _ici_wait(started, after, name, gather):
    send_sems, recv_sems, src_thru, land_thru, _ = started

    def body(src_ref, land_ref, send_sems, recv_sems, after_ref, src_dead, got_ref):
        x, y, c = _coords()
        for j, (px, py) in enumerate(_other_chips(x, y)):
            s_ref, d_ref = _ici_refs(gather, src_ref, land_ref, j, px, py, (px, py), c)
            copy = pltpu.make_async_remote_copy(
                src_ref=s_ref, dst_ref=d_ref, send_sem=send_sems.at[j], recv_sem=recv_sems.at[j],
                device_id=(px, py, c), device_id_type=MESH_ID)
            copy.wait_send()
            copy.wait_recv()

    return pl.pallas_call(
        body, name=name,
        out_shape=(pltpu.HBM(src_thru.shape, src_thru.dtype), pltpu.HBM(land_thru.shape, land_thru.dtype)),
        in_specs=(HBM_SPEC, HBM_SPEC, SEM_SPEC, SEM_SPEC, ANY),
        out_specs=(HBM_SPEC, HBM_SPEC), input_output_aliases={0: 0, 1: 1},
        compiler_params=pltpu.CompilerParams(has_side_effects=SPLIT_EFFECT),
    )(src_thru, land_thru, send_sems, recv_sems, after)


def _neighbour(k, x, y):
    return (1 - x, y) if k == 0 else (x, 1 - y)


def _slot(ref, px, py, c):
    return ref.at[:, 4 * px + 2 * py + c]


def _near_start(src, after, name):
    land_shape = (src.shape[0], N_DEV) + src.shape[1:]

    def body(src_ref, land_ref, after_ref, send_sems, recv_sems, src_thru, land_thru, token):
        x, y, c = _coords()
        for k in range(2):
            px, py = _neighbour(k, x, y)
            pltpu.make_async_remote_copy(
                src_ref=src_ref, dst_ref=_slot(land_ref, x, y, c), send_sem=send_sems.at[k],
                recv_sem=recv_sems.at[k], device_id=(px, py, c), device_id_type=MESH_ID).start()
        token[...] = jnp.zeros_like(token)

    return pl.pallas_call(
        body, name=name,
        out_shape=(pltpu.SemaphoreType.DMA((2,)), pltpu.SemaphoreType.DMA((2,)), pltpu.HBM(src.shape, src.dtype),
                   pltpu.HBM(land_shape, src.dtype), jax.ShapeDtypeStruct((8, 128), F32)),
        in_specs=(HBM_SPEC, HBM_SPEC, ANY),
        out_specs=(SEM_SPEC, SEM_SPEC, HBM_SPEC, HBM_SPEC, pl.BlockSpec(memory_space=pltpu.VMEM)),
        input_output_aliases={0: 2, 1: 3},
        compiler_params=pltpu.CompilerParams(has_side_effects=SPLIT_EFFECT),
    )(pltpu.with_memory_space_constraint(src, pltpu.HBM),
      pltpu.with_memory_space_constraint(lax.empty(land_shape, src.dtype), pltpu.HBM), after)


def _near_wait(started, after, name):
    send_sems, recv_sems, src_thru, land_thru, _ = started

    def body(src_ref, land_ref, send_sems, recv_sems, after_ref, src_dead, got_ref):
        x, y, c = _coords()
        for k in range(2):
            px, py = _neighbour(k, x, y)
            copy = pltpu.make_async_remote_copy(
                src_ref=src_ref, dst_ref=_slot(land_ref, px, py, c), send_sem=send_sems.at[k],
                recv_sem=recv_sems.at[k], device_id=(px, py, c), device_id_type=MESH_ID)
            copy.wait_send()
            copy.wait_recv()

    return pl.pallas_call(
        body, name=name,
        out_shape=(pltpu.HBM(src_thru.shape, src_thru.dtype), pltpu.HBM(land_thru.shape, land_thru.dtype)),
        in_specs=(HBM_SPEC, HBM_SPEC, SEM_SPEC, SEM_SPEC, ANY),
        out_specs=(HBM_SPEC, HBM_SPEC), input_output_aliases={0: 0, 1: 1},
        compiler_params=pltpu.CompilerParams(has_side_effects=SPLIT_EFFECT),
    )(src_thru, land_thru, send_sems, recv_sems, after)


def _far_refs(land_ref, k, x, y, c, arriving):
    half = land_ref.shape[2] // 2
    rows = pl.ds(k * half, half)
    ox, oy = (1 - x, 1 - y) if arriving else _neighbour(k, x, y)
    return land_ref.at[:, 4 * ox + 2 * oy + c, rows]


def _far_start(land, after, name):
    def body(land_ref, after_ref, send_sems, recv_sems, land_thru, token):
        x, y, c = _coords()
        for k in range(2):
            block = _far_refs(land_ref, k, x, y, c, False)
            px, py = _neighbour(1 - k, x, y)
            pltpu.make_async_remote_copy(
                src_ref=block, dst_ref=block, send_sem=send_sems.at[k], recv_sem=recv_sems.at[k],
                device_id=(px, py, c), device_id_type=MESH_ID).start()
        token[...] = jnp.zeros_like(token)

    return pl.pallas_call(
        body, name=name,
        out_shape=(pltpu.SemaphoreType.DMA((2,)), pltpu.SemaphoreType.DMA((2,)),
                   pltpu.HBM(land.shape, land.dtype), jax.ShapeDtypeStruct((8, 128), F32)),
        in_specs=(HBM_SPEC, ANY),
        out_specs=(SEM_SPEC, SEM_SPEC, HBM_SPEC, pl.BlockSpec(memory_space=pltpu.VMEM)),
        input_output_aliases={0: 2},
        compiler_params=pltpu.CompilerParams(has_side_effects=SPLIT_EFFECT),
    )(pltpu.with_memory_space_constraint(land, pltpu.HBM), after)


def _far_wait(started, after, name):
    send_sems, recv_sems, land_thru, _ = started

    def body(land_ref, send_sems, recv_sems, after_ref, got_ref):
        x, y, c = _coords()
        for k in range(2):
            px, py = _neighbour(1 - k, x, y)
            copy = pltpu.make_async_remote_copy(
                src_ref=_far_refs(land_ref, k, x, y, c, False), dst_ref=_far_refs(land_ref, k, x, y, c, True),
                send_sem=send_sems.at[k], recv_sem=recv_sems.at[k], device_id=(px, py, c), device_id_type=MESH_ID)
            copy.wait_send()
            copy.wait_recv()

    return pl.pallas_call(
        body, name=name,
        out_shape=pltpu.HBM(land_thru.shape, land_thru.dtype),
        in_specs=(HBM_SPEC, SEM_SPEC, SEM_SPEC, ANY),
        out_specs=HBM_SPEC, input_output_aliases={0: 0},
        compiler_params=pltpu.CompilerParams(has_side_effects=SPLIT_EFFECT),
    )(land_thru, send_sems, recv_sems, after)


def _peer(k, x, y, c):
    return x ^ ((k >> 2) & 1), y ^ ((k >> 1) & 1), c ^ (k & 1)


def _peers_start(src, after, name):
    r, w = src.shape
    x, y, c = _coords()
    land = lax.dynamic_update_slice(jnp.zeros((N_DEV, r, w), src.dtype), src[None], (4 * x + 2 * y + c, 0, 0))

    def body(src_ref, land_ref, after_ref, send_sems, recv_sems, src_thru, land_thru, token):
        x, y, c = _coords()
        for k in range(1, N_DEV):
            pltpu.make_async_remote_copy(
                src_ref=src_ref, dst_ref=land_ref.at[4 * x + 2 * y + c],
                send_sem=send_sems.at[k - 1], recv_sem=recv_sems.at[k - 1],
                device_id=_peer(k, x, y, c), device_id_type=MESH_ID).start()
        token[...] = jnp.zeros_like(token)

    return pl.pallas_call(
        body, name=name,
        out_shape=(pltpu.SemaphoreType.DMA((N_DEV - 1,)), pltpu.SemaphoreType.DMA((N_DEV - 1,)),
                   pltpu.HBM(src.shape, src.dtype), pltpu.HBM(land.shape, src.dtype),
                   jax.ShapeDtypeStruct((8, 128), F32)),
        in_specs=(HBM_SPEC, HBM_SPEC, ANY),
        out_specs=(SEM_SPEC, SEM_SPEC, HBM_SPEC, HBM_SPEC, pl.BlockSpec(memory_space=pltpu.VMEM)),
        input_output_aliases={0: 2, 1: 3},
        compiler_params=pltpu.CompilerParams(has_side_effects=SPLIT_EFFECT),
    )(pltpu.with_memory_space_constraint(src, pltpu.HBM), pltpu.with_memory_space_constraint(land, pltpu.HBM), after)


def _peers_wait(started, after, name):
    send_sems, recv_sems, src_thru, land_thru, _ = started

    def body(src_ref, land_ref, send_sems, recv_sems, after_ref, src_dead, got_ref):
        x, y, c = _coords()
        for k in range(1, N_DEV):
            px, py, pc = _peer(k, x, y, c)
            copy = pltpu.make_async_remote_copy(
                src_ref=src_ref, dst_ref=land_ref.at[4 * px + 2 * py + pc],
                send_sem=send_sems.at[k - 1], recv_sem=recv_sems.at[k - 1],
                device_id=(px, py, pc), device_id_type=MESH_ID)
            copy.wait_send()
            copy.wait_recv()

    return pl.pallas_call(
        body, name=name,
        out_shape=(pltpu.HBM(src_thru.shape, src_thru.dtype), pltpu.HBM(land_thru.shape, land_thru.dtype)),
        in_specs=(HBM_SPEC, HBM_SPEC, SEM_SPEC, SEM_SPEC, ANY),
        out_specs=(HBM_SPEC, HBM_SPEC), input_output_aliases={0: 0, 1: 1},
        compiler_params=pltpu.CompilerParams(has_side_effects=SPLIT_EFFECT),
    )(src_thru, land_thru, send_sems, recv_sems, after)


def _share_refs(ref, k, x, y, c, sender_c):
    px, py = ([(x, y)] + _other_chips(x, y))[k]
    return ref.at[:, 4 * px + 2 * py + sender_c]


def _share_start(gathered, after, name):
    def body(g_ref, after_ref, send_sems, recv_sems, g_thru, token):
        x, y, c = _coords()
        for k in range(4):
            slot = _share_refs(g_ref, k, x, y, c, c)
            pltpu.make_async_remote_copy(
                src_ref=slot, dst_ref=slot, send_sem=send_sems.at[k], recv_sem=recv_sems.at[k],
                device_id=(x, y, 1 - c), device_id_type=MESH_ID).start()
        token[...] = jnp.zeros_like(token)

    return pl.pallas_call(
        body, name=name,
        out_shape=(pltpu.SemaphoreType.DMA((4,)), pltpu.SemaphoreType.DMA((4,)),
                   pltpu.HBM(gathered.shape, gathered.dtype), jax.ShapeDtypeStruct((8, 128), F32)),
        in_specs=(HBM_SPEC, ANY),
        out_specs=(SEM_SPEC, SEM_SPEC, HBM_SPEC, pl.BlockSpec(memory_space=pltpu.VMEM)),
        input_output_aliases={0: 2},
        compiler_params=pltpu.CompilerParams(has_side_effects=SPLIT_EFFECT),
    )(pltpu.with_memory_space_constraint(gathered, pltpu.HBM), after)


def _share_wait(started, after, name):
    send_sems, recv_sems, g_thru, _ = started

    def body(g_ref, send_sems, recv_sems, after_ref, got_ref):
        x, y, c = _coords()
        for k in range(4):
            copy = pltpu.make_async_remote_copy(
                src_ref=_share_refs(g_ref, k, x, y, c, c), dst_ref=_share_refs(g_ref, k, x, y, c, 1 - c),
                send_sem=send_sems.at[k], recv_sem=recv_sems.at[k],
                device_id=(x, y, 1 - c), device_id_type=MESH_ID)
            copy.wait_send()
            copy.wait_recv()

    return pl.pallas_call(
        body, name=name,
        out_shape=pltpu.HBM(g_thru.shape, g_thru.dtype),
        in_specs=(HBM_SPEC, SEM_SPEC, SEM_SPEC, ANY),
        out_specs=HBM_SPEC, input_output_aliases={0: 0},
        compiler_params=pltpu.CompilerParams(has_side_effects=SPLIT_EFFECT),
    )(g_thru, send_sems, recv_sems, after)


def _core_share(own, gathered, name):
    def body(own_ref, gin_ref, out_ref, stage, send_sems, recv_sems, local_sem):
        x, y, c = _coords()
        sibling = (x, y, 1 - c)
        chips = [(x, y)] + _other_chips(x, y)
        stage_in = pltpu.make_async_copy(own_ref, stage, local_sem)
        stage_in.start()
        sent, arriving = [], []
        for k, (px, py) in enumerate(chips):
            slot = out_ref.at[:, 4 * px + 2 * py + c]
            sent.append(pltpu.make_async_remote_copy(
                src_ref=own_ref if k == 0 else slot, dst_ref=slot,
                send_sem=send_sems.at[k], recv_sem=recv_sems.at[k], device_id=sibling, device_id_type=MESH_ID))
            arriving.append(pltpu.make_async_remote_copy(
                src_ref=own_ref, dst_ref=out_ref.at[:, 4 * px + 2 * py + (1 - c)],
                send_sem=send_sems.at[k], recv_sem=recv_sems.at[k], device_id=sibling, device_id_type=MESH_ID))
        for cp in sent:
            cp.start()
        stage_in.wait()
        stage_out = pltpu.make_async_copy(stage, out_ref.at[:, 4 * x + 2 * y + c], local_sem)
        stage_out.start()
        for cp in arriving:
            cp.wait_recv()
        for cp in sent:
            cp.wait_send()
        stage_out.wait()

    return pl.pallas_call(
        body, name=name,
        out_shape=jax.ShapeDtypeStruct(gathered.shape, own.dtype),
        in_specs=[ANY, ANY], out_specs=ANY, input_output_aliases={1: 0},
        scratch_shapes=[pltpu.VMEM(own.shape, own.dtype), pltpu.SemaphoreType.DMA((4,)),
                        pltpu.SemaphoreType.DMA((4,)), pltpu.SemaphoreType.DMA],
    )(own, gathered)


def _adamw(w, g, m, v):
    m = ADAM_B1 * m + (1.0 - ADAM_B1) * g
    v = ADAM_B2 * v + (1.0 - ADAM_B2) * (g * g)
    m_hat = m / ADAM_C1
    v_hat = v / ADAM_C2
    delta = -ADAM_LR * (m_hat / (jnp.sqrt(v_hat) + ADAM_EPS) + ADAM_WD * w)
    return delta, m, v


def _adam_big(part, land, w, m, v, chip_idx, tag, token):
    r, wd = w.shape
    tr, tw = _row_tile(r, 1024), 256

    def body(s_ref, tok_ref, p_ref, l_ref, w_ref, m_ref, v_ref, g_ref, d_ref, mo_ref, vo_ref):
        g = p_ref[0].astype(F32)
        for j in range(3):
            g = g + l_ref[j].astype(F32)
        delta, mn, vn = _adamw(w_ref[...], g, m_ref[...], v_ref[...])
        g_ref[...] = g
        d_ref[...] = delta
        mo_ref[...] = mn
        vo_ref[...] = vn

    row = pl.BlockSpec((tr, tw), lambda i, j, s: (i, j))
    return pl.pallas_call(
        body, name="adam_big_" + tag,
        grid_spec=pltpu.PrefetchScalarGridSpec(
            num_scalar_prefetch=1, grid=(r // tr, wd // tw),
            in_specs=[pl.BlockSpec((8, 128), lambda i, j, s: (0, 0)),
                      pl.BlockSpec((1, tr, tw), lambda i, j, s: (s[0], i, j)),
                      pl.BlockSpec((3, tr, tw), lambda i, j, s: (0, i, j)), row, row, row],
            out_specs=[row, row, row, row]),
        out_shape=[jax.ShapeDtypeStruct((r, wd), F32)] * 4,
        compiler_params=_cparams(),
    )(chip_idx, token, part, land, w, m, v)


def _adam_small(parts, w, m, v):
    _, r, wd = parts.shape

    def body(p_ref, w_ref, m_ref, v_ref, g_ref, d_ref, mo_ref, vo_ref):
        g = p_ref[0]
        for k in range(1, N_DEV):
            g = g + p_ref[k]
        delta, mn, vn = _adamw(w_ref[...], g, m_ref[...], v_ref[...])
        g_ref[...] = g
        d_ref[...] = delta
        mo_ref[...] = mn
        vo_ref[...] = vn

    return pl.pallas_call(
        body, name="adam_small",
        out_shape=[jax.ShapeDtypeStruct((r, wd), F32)] * 4,
        compiler_params=_cparams(),
    )(parts, w, m, v)


def _pad_rows(a, rows):
    return jnp.pad(a, ((0, rows - a.shape[0]), (0, 0)))


def _pad_w_in(w):
    cut = Q_RANK + KV_RANK + ROPE
    return jnp.concatenate([w[:, :cut], jnp.zeros((w.shape[0], 64), w.dtype), w[:, cut:]], axis=1)


def _unpad_w_in(w):
    cut = Q_RANK + KV_RANK + ROPE
    return jnp.concatenate([w[:, :cut], w[:, cut + 64:]], axis=1)


def _pack_mid(p):
    parts = [_pad_w_in(p["w_in"][0]), p["w_out"][0], p["w_mq"][0], p["w_mo"][0],
             p["w_mkv"][0].reshape(256, D_MODEL),
             _pad_rows(p["w_q_up"][0].T.reshape(24, D_MODEL), 32),
             p["w_kv_up"][0].reshape(16, D_MODEL)]
    return jnp.concatenate(parts, axis=0)


def _pack_segments(p, group):
    if group == "mid":
        return _pack_mid(p)[None]
    return jnp.stack([p[group + "_w_gate"][0].T, p[group + "_w_up"][0].T, p[group + "_w_down"][0]])


UNIT_WEIGHT = {"ffn1_g": ("ffn1_w_gate", True), "ffn1_u": ("ffn1_w_up", True), "ffn1_d": ("ffn1_w_down", False),
               "ffn2_g": ("ffn2_w_gate", True), "ffn2_u": ("ffn2_w_up", True), "ffn2_d": ("ffn2_w_down", False)}


def _pack_unit(p, unit):
    if unit == "mid":
        return _pack_mid(p)
    name, transposed = UNIT_WEIGHT[unit]
    return p[name][0].T if transposed else p[name][0]


def _unpack_unit(a, unit):
    if unit != "mid":
        name, transposed = UNIT_WEIGHT[unit]
        return {name: (a.T if transposed else a)[None]}
    seg = lambda n: a[SEG_OFF[n][0]:SEG_OFF[n][0] + SEG_OFF[n][1]]
    return {"w_in": _unpad_w_in(seg("w_in"))[None], "w_out": seg("w_out")[None], "w_mq": seg("w_mq")[None],
            "w_mo": seg("w_mo")[None], "w_mkv": seg("w_mkv").reshape(D_MODEL, 256)[None],
            "w_q_up": seg("w_q")[:24].reshape(96, Q_RANK).T[None],
            "w_kv_up": seg("w_kv").reshape(KV_RANK, 128)[None]}


def _unpack_gathered(full, group):
    if group != "mid":
        return {n: full[k].reshape(-1, D_MODEL) for k, (n, _) in enumerate(GROUP_SEGS[group])}
    full = full[0]
    seg = lambda n: full[:, SEG_OFF[n][0]:SEG_OFF[n][0] + SEG_OFF[n][1]]
    rows = lambda n: seg(n).reshape(-1, D_MODEL)
    wq_t = seg("w_q")[:, :24].reshape(MLA_HEADS, NOPE + ROPE, Q_RANK)
    wq_t = jnp.pad(wq_t, ((0, 0), (0, HEAD_PAD - NOPE - ROPE), (0, 0))).reshape(MLA_HEADS * HEAD_PAD, Q_RANK)
    wkv = seg("w_kv").reshape(N_DEV, KV_RANK, 128).transpose(1, 0, 2).reshape(KV_RANK, N_DEV * 128)
    return {"w_in": rows("w_in"), "w_out": rows("w_out"), "w_mq": rows("w_mq"), "w_mo": rows("w_mo"),
            "w_mkv": seg("w_mkv").reshape(N_DEV, D_MODEL, 256), "w_q": wq_t, "w_kv": wkv}


def _pack_grads(gr):
    blk = lambda a: a.reshape(N_DEV, -1, D_MODEL)
    dwq = gr["w_q"].reshape(MLA_HEADS, HEAD_PAD, Q_RANK)[:, :NOPE + ROPE].reshape(N_DEV, 24, D_MODEL)
    dwq = jnp.pad(dwq, ((0, 0), (0, 8), (0, 0)))
    dwkv = gr["w_kv"].reshape(KV_RANK, N_DEV, 128).transpose(1, 0, 2).reshape(N_DEV, 16, D_MODEL)
    parts = [blk(gr["w_in"]), blk(gr["w_out"]), blk(gr["w_mq"]), blk(gr["w_mo"]),
             gr["w_mkv"].reshape(N_DEV, 256, D_MODEL), dwq, dwkv]
    return jnp.concatenate([a.astype(BF16) for a in parts], axis=1)


def _pack_small(vals):
    parts = []
    for n, r in SMALL_ROWS:
        parts.append(_pad_rows(vals[n].reshape(-1, 128), r) if n in vals else jnp.zeros((r, 128), F32))
    return jnp.concatenate(parts, axis=0)


def _unpack_small(a, shapes):
    out = {}
    for n, shape in shapes.items():
        o = SMALL_OFF[n][0]
        out[n] = a[o:o + int(np.prod(shape)) // 128].reshape(shape)
    return out


BIG_NAMES = ("ffn1_w_gate", "ffn1_w_up", "ffn1_w_down", "w_in", "w_q_up", "w_kv_up", "w_out", "w_mq", "w_mkv",
             "w_mo", "ffn2_w_gate", "ffn2_w_up", "ffn2_w_down")
SMALL_NAMES = ("ffn1_norm", "mix_norm", "q_norm", "kv_norm", "pool_w", "pool_scale", "xattn_norm", "mem_norm",
               "ffn2_norm", "final_norm")
WEIGHT_ORDER = ("ffn1_norm", "ffn1_w_gate", "ffn1_w_up", "ffn1_w_down", "mix_norm", "w_in", "q_norm", "w_q_up",
                "kv_norm", "w_kv_up", "pool_w", "pool_scale", "w_out", "xattn_norm", "mem_norm", "w_mq", "w_mkv",
                "w_mo", "ffn2_norm", "ffn2_w_gate", "ffn2_w_up", "ffn2_w_down", "final_norm")


def _rope_table():
    lane = np.arange(128)
    freqs = (1.0 / (ROPE_BASE ** (np.arange(0, ROPE, 2, dtype=np.float32) / ROPE))).astype(np.float32)
    tab = np.zeros((8, 128), np.float32)
    tab[0] = np.where(lane < ROPE, freqs[lane % (ROPE // 2)], 0.0)
    tab[1] = np.where(lane < ROPE // 2, -1.0, np.where(lane < ROPE, 1.0, 0.0))
    return jnp.asarray(tab)


def kernel(x, mem, positions, ffn1_norm, ffn1_w_gate, ffn1_w_up, ffn1_w_down, mix_norm, w_in, q_norm, w_q_up, kv_norm, w_kv_up, pool_w, pool_scale, w_out, xattn_norm, mem_norm, w_mq, w_mkv, w_mo, ffn2_norm, ffn2_w_gate, ffn2_w_up, ffn2_w_down, final_norm, loss_target, m_ffn1_norm, m_ffn1_w_gate, m_ffn1_w_up, m_ffn1_w_down, m_mix_norm, m_w_in, m_q_norm, m_w_q_up, m_kv_norm, m_w_kv_up, m_pool_w, m_pool_scale, m_w_out, m_xattn_norm, m_mem_norm, m_w_mq, m_w_mkv, m_w_mo, m_ffn2_norm, m_ffn2_w_gate, m_ffn2_w_up, m_ffn2_w_down, m_final_norm, v_ffn1_norm, v_ffn1_w_gate, v_ffn1_w_up, v_ffn1_w_down, v_mix_norm, v_w_in, v_q_norm, v_w_q_up, v_kv_norm, v_w_kv_up, v_pool_w, v_pool_scale, v_w_out, v_xattn_norm, v_mem_norm, v_w_mq, v_w_mkv, v_w_mo, v_ffn2_norm, v_ffn2_w_gate, v_ffn2_w_up, v_ffn2_w_down, v_final_norm):
    wts = dict(ffn1_norm=ffn1_norm, ffn1_w_gate=ffn1_w_gate, ffn1_w_up=ffn1_w_up, ffn1_w_down=ffn1_w_down,
               mix_norm=mix_norm, w_in=w_in, q_norm=q_norm, w_q_up=w_q_up, kv_norm=kv_norm, w_kv_up=w_kv_up,
               pool_w=pool_w, pool_scale=pool_scale, w_out=w_out, xattn_norm=xattn_norm, mem_norm=mem_norm,
               w_mq=w_mq, w_mkv=w_mkv, w_mo=w_mo, ffn2_norm=ffn2_norm, ffn2_w_gate=ffn2_w_gate,
               ffn2_w_up=ffn2_w_up, ffn2_w_down=ffn2_w_down, final_norm=final_norm)
    mom = dict(ffn1_norm=m_ffn1_norm, ffn1_w_gate=m_ffn1_w_gate, ffn1_w_up=m_ffn1_w_up, ffn1_w_down=m_ffn1_w_down,
               mix_norm=m_mix_norm, w_in=m_w_in, q_norm=m_q_norm, w_q_up=m_w_q_up, kv_norm=m_kv_norm,
               w_kv_up=m_w_kv_up, pool_w=m_pool_w, pool_scale=m_pool_scale, w_out=m_w_out, xattn_norm=m_xattn_norm,
               mem_norm=m_mem_norm, w_mq=m_w_mq, w_mkv=m_w_mkv, w_mo=m_w_mo, ffn2_norm=m_ffn2_norm,
               ffn2_w_gate=m_ffn2_w_gate, ffn2_w_up=m_ffn2_w_up, ffn2_w_down=m_ffn2_w_down, final_norm=m_final_norm)
    var = dict(ffn1_norm=v_ffn1_norm, ffn1_w_gate=v_ffn1_w_gate, ffn1_w_up=v_ffn1_w_up, ffn1_w_down=v_ffn1_w_down,
               mix_norm=v_mix_norm, w_in=v_w_in, q_norm=v_q_norm, w_q_up=v_w_q_up, kv_norm=v_kv_norm,
               w_kv_up=v_w_kv_up, pool_w=v_pool_w, pool_scale=v_pool_scale, w_out=v_w_out, xattn_norm=v_xattn_norm,
               mem_norm=v_mem_norm, w_mq=v_w_mq, w_mkv=v_w_mkv, w_mo=v_w_mo, ffn2_norm=v_ffn2_norm,
               ffn2_w_gate=v_ffn2_w_gate, ffn2_w_up=v_ffn2_w_up, ffn2_w_down=v_ffn2_w_down, final_norm=v_final_norm)

    t = x.shape[1]
    xs = x[0]
    mems = mem[0]
    target = loss_target[0]
    pos = positions.reshape(t, 1)
    row = lambda a: a.reshape(1, -1)
    rope_tab = _rope_table()

    cx, cy, cc = _coords()
    chip_idx = (2 * cx + cy).astype(jnp.int32).reshape(1)

    wb = {}
    for grp in ("ffn1", "mid", "ffn2"):
        wb[grp] = _pack_segments(wts, grp).astype(BF16)
        if grp == "ffn1":
            near_ffn1 = _near_start(wb["ffn1"], pos, "ag_ffn1_near_start")
    own_ffn1, land_ffn1 = _near_wait(near_ffn1, wb["ffn2"], "ag_ffn1_near_wait")
    far_ffn1 = _far_start(land_ffn1, own_ffn1, "ag_ffn1_far_start")
    land_ffn1 = _far_wait(far_ffn1, wb["mid"], "ag_ffn1_far_wait")
    full_ffn1 = _core_share(own_ffn1, land_ffn1, "ag_ffn1_share")
    fw = _unpack_gathered(full_ffn1, "ffn1")
    ag_mid = _ici_start(wb["mid"], full_ffn1, "ag_mid_start", True)
    g_ffn1, g_mix, g_q, g_kv = row(ffn1_norm), row(mix_norm), row(q_norm), row(kv_norm)
    g_x, g_mem, g_ffn2, g_fin = row(xattn_norm), row(mem_norm), row(ffn2_norm), row(final_norm)
    pool_wb = pool_w[0].astype(BF16)
    pool_sc = row(pool_scale)

    h1, n1, gate1, up1 = _ffn_fwd(xs, g_ffn1, fw["ffn1_g"], fw["ffn1_u"], fw["ffn1_d"], "ffn1_fwd", token=ag_mid[4])
    own_mid, land_mid = _ici_wait(ag_mid, h1, "ag_mid_wait", True)
    full_mid = _core_share(own_mid, land_mid, "ag_mid_share")
    fw.update(_unpack_gathered(full_mid, "mid"))
    ag_ffn2 = _ici_start(wb["ffn2"], full_mid, "ag_ffn2_start", True)
    u, z, qn, kvn, qh, kh, vh = _mix_prep(h1, g_mix, fw["w_in"], g_q, fw["w_q"], g_kv, fw["w_kv"], pos, rope_tab,
                                          token=ag_ffn2[4])
    a, lse = _attn_fwd(qh, kh, vh)
    p = _pool_fwd(z, pool_wb, pool_sc)
    memn, km, vm = _mem_kv(mems, g_mem, fw["w_mkv"])
    own_ffn2, land_ffn2 = _ici_wait(ag_ffn2, a, "ag_ffn2_wait", True)
    land_ffn2 = lax.dynamic_update_slice(land_ffn2, own_ffn2[:, None], (0, 4 * cx + 2 * cy + cc, 0, 0))
    share_ffn2 = _share_start(land_ffn2, a, "ag_ffn2_share_start")
    h2, h3, hn, qm, om = _xattn_fwd(h1, a, p, fw["w_out"], g_x, fw["w_mq"], km, vm, fw["w_mo"], token=share_ffn2[3])
    fw.update(_unpack_gathered(_share_wait(share_ffn2, h3, "ag_ffn2_share_wait"), "ffn2"))
    dh4, n2, gate2, up2, loss_part, dg_fin = _ffn_fwd(h3, g_ffn2, fw["ffn2_g"], fw["ffn2_u"], fw["ffn2_d"],
                                                      "ffn2_fwd", head=(target, g_fin))

    def reduce_start(g8, unit):
        part = _core_reduce(g8, unit)
        return _ici_start(part, g8, "rs_" + unit + "_start", False)

    def by_device(g):
        return g.reshape(N_DEV, -1, D_MODEL)

    rs = {}
    dh3, dgate2, dup2, act2, dg_ffn2 = _ffn_bwd_data(dh4, h3, g_ffn2, gate2, up2, fw["ffn2_g"], fw["ffn2_u"],
                                                     fw["ffn2_d"], "ffn2_bwd")
    rs["ffn2_g"] = reduce_start(by_device(_tn_matmul(dgate2, n2, "ffn2_dwg", tmm=1408, out_dtype=BF16)), "ffn2_g")
    rs["ffn2_u"] = reduce_start(by_device(_tn_matmul(dup2, n2, "ffn2_dwu", tmm=1408, out_dtype=BF16,
                                                     token=rs["ffn2_g"][4])), "ffn2_u")
    rs["ffn2_d"] = reduce_start(by_device(_tn_matmul(act2, dh4, "ffn2_dwd", scale=0.5, tmm=1408, out_dtype=BF16,
                                                     token=rs["ffn2_u"][4])), "ffn2_d")
    dh2, dqm, da, dp, dkm, dvm, dg_x = _xattn_bwd(dh3, h2, qm, g_x, fw["w_mq"], km, vm, fw["w_mo"], fw["w_out"],
                                                  token=rs["ffn2_d"][4])
    gr = {}
    gr["w_mo"] = _tn_matmul(om, dh3, "dw_mo", out_dtype=BF16)
    gr["w_mq"] = _tn_matmul(hn, dqm, "dw_mq", out_dtype=BF16)
    gr["w_out"] = jnp.concatenate([_tn_matmul(a, dh2, "dw_out_a", out_dtype=BF16),
                                   _tn_matmul(p, dh2, "dw_out_p", out_dtype=BF16)], axis=0)
    gr["w_mkv"], dg_mem = _mem_kv_bwd(dkm, dvm, memn, mems, g_mem, fw["w_mkv"])
    dz_pool, d_pool_w, d_pool_sc = _pool_bwd(dp, z, pool_wb, pool_sc)
    dqh, dkh, dvh = _attn_bwd(qh, kh, vh, da, lse, _attn_delta(a, da))
    dh1, dq, dkv, dz, dg_q, dg_kv, dg_mix = _mla_bwd(dqh, dkh, dvh, z, dz_pool, h1, dh2, g_mix, fw["w_in"], g_q,
                                                     fw["w_q"], g_kv, fw["w_kv"], pos, rope_tab)
    gr["w_q"] = _tn_matmul(dq, qn, "dw_q", out_dtype=BF16)
    gr["w_kv"] = _tn_matmul(kvn, dkv, "dw_kv", out_dtype=BF16)
    gr["w_in"] = _tn_matmul(u, dz, "dw_in", out_dtype=BF16)
    g_mid = _pack_grads(gr)
    part_mid = _core_reduce(g_mid, "mid")
    got = {}
    after = part_mid
    for unit in ("ffn2_g", "ffn2_u", "ffn2_d"):
        got[unit] = _ici_wait(rs[unit], after, "rs_" + unit + "_wait", False)
        after = got[unit][1]
    rs["mid"] = _ici_start(part_mid, after, "rs_mid_start", False)
    dx, dgate1, dup1, act1, dg_ffn1 = _ffn_bwd_data(dh1, xs, g_ffn1, gate1, up1, fw["ffn1_g"], fw["ffn1_u"],
                                                    fw["ffn1_d"], "ffn1_bwd", token=rs["mid"][4])
    got["mid"] = _ici_wait(rs["mid"], dx, "rs_mid_wait", False)

    small_g = dict(ffn1_norm=dg_ffn1, mix_norm=dg_mix, q_norm=dg_q, kv_norm=dg_kv, pool_w=d_pool_w,
                   pool_scale=d_pool_sc, xattn_norm=dg_x, mem_norm=dg_mem, ffn2_norm=dg_ffn2, final_norm=dg_fin,
                   loss=loss_part)
    small_ag = _peers_start(_pack_small(small_g), got["mid"][1], "small_ag_start")
    rs["ffn1_g"] = reduce_start(by_device(_tn_matmul(dgate1, n1, "ffn1_dwg", tmm=1408, out_dtype=BF16,
                                                     token=small_ag[4])), "ffn1_g")
    _, parts = _peers_wait(small_ag, rs["ffn1_g"][4], "small_ag_wait")
    small = _adam_small(parts, _pack_small({n: wts[n] for n in SMALL_NAMES}),
                        _pack_small({n: mom[n] for n in SMALL_NAMES}), _pack_small({n: var[n] for n in SMALL_NAMES}))
    small_sum = small[0]
    loss = small_sum[SMALL_OFF["loss"][0], 0]
    shapes = {n: wts[n].shape for n in SMALL_NAMES}
    small = [_unpack_small(s, shapes) for s in small]

    rs["ffn1_u"] = reduce_start(by_device(_tn_matmul(dup1, n1, "ffn1_dwu", tmm=1408, out_dtype=BF16,
                                                     token=small_sum)), "ffn1_u")
    rs["ffn1_d"] = reduce_start(by_device(_tn_matmul(act1, dh1, "ffn1_dwd", scale=0.5, tmm=1408, out_dtype=BF16,
                                                     token=rs["ffn1_u"][4])), "ffn1_d")

    big = {}

    def adam_unit(unit, token):
        part, land = got[unit]
        res = _adam_big(part, land, _pack_unit(wts, unit), _pack_unit(mom, unit), _pack_unit(var, unit),
                        chip_idx, unit, token)
        for k, packed in enumerate(res):
            big.setdefault(k, {}).update(_unpack_unit(packed, unit))
        return res[0]

    done = rs["ffn1_d"][4]
    for unit in ("mid", "ffn2_g", "ffn2_u", "ffn2_d"):
        done = adam_unit(unit, done)
    for unit in ("ffn1_g", "ffn1_u", "ffn1_d"):
        got[unit] = _ici_wait(rs[unit], done, "rs_" + unit + "_wait", False)
        done = adam_unit(unit, done)

    outs = [loss, dx[None]]
    for k in range(4):
        for n in WEIGHT_ORDER:
            outs.append(big[k][n] if n in BIG_NAMES else small[k][n])
    return tuple(outs)
```

```python
import numpy as np

import jax
import jax.numpy as jnp
from jax import lax
from jax.experimental import pallas as pl
from jax.experimental.pallas import tpu as pltpu

F32 = jnp.float32
BF16 = jnp.bfloat16

N_DEV = 8
D_MODEL = 1024
D_FF = 2816
MLA_HEADS = 4
NOPE = 128
ROPE = 64
HEAD_PAD = 256
V_DIM = 128
Q_RANK = 256
KV_RANK = 128
POOL_WINDOWS = (2, 4, 8, 16)
POOL_CH = 128
POOL_HALO = 16
N_MEM = 256
MEM_HEADS = 4
MEM_HD = 256
ROPE_BASE = 10000.0
RMS_EPS = 1e-6
ATTN_SCALE = (NOPE + ROPE) ** -0.5
MEM_SCALE = MEM_HD ** -0.5
NEG_BIG = -1e30

ADAM_LR = 0.001
ADAM_B1 = 0.9
ADAM_B2 = 0.999
ADAM_EPS = 1e-08
ADAM_WD = 0.01
ADAM_STEP = 10
ADAM_C1 = 1.0 - ADAM_B1 ** ADAM_STEP
ADAM_C2 = 1.0 - ADAM_B2 ** ADAM_STEP

VMEM_LIMIT_BYTES = 56 * 1024 * 1024
BF16_ROWS = 16

GROUP_SEGS = {
    "ffn1": (("ffn1_g", 352), ("ffn1_u", 352), ("ffn1_d", 352)),
    "mid": (("w_in", 128), ("w_out", 128), ("w_mq", 128), ("w_mo", 128), ("w_mkv", 256), ("w_q", 32), ("w_kv", 16)),
    "ffn2": (("ffn2_g", 352), ("ffn2_u", 352), ("ffn2_d", 352)),
}
SEG_OFF = {}
GROUP_ROWS = {}
for _g, _segs in GROUP_SEGS.items():
    _o = 0
    for _n, _r in _segs:
        SEG_OFF[_n] = (_o, _r)
        _o += _r
    GROUP_ROWS[_g] = _o

SMALL_ROWS = (("ffn1_norm", 8), ("mix_norm", 8), ("q_norm", 8), ("kv_norm", 8), ("pool_w", 512), ("pool_scale", 8),
              ("xattn_norm", 8), ("mem_norm", 8), ("ffn2_norm", 8), ("final_norm", 8), ("loss", 8))
SMALL_OFF = {}
_o = 0
for _n, _r in SMALL_ROWS:
    SMALL_OFF[_n] = (_o, _r)
    _o += _r


def _cparams(**kw):
    return pltpu.CompilerParams(vmem_limit_bytes=VMEM_LIMIT_BYTES, **kw)


def _row_tile(rows, limit):
    best = None
    for cand in range(BF16_ROWS, min(rows, limit) + 1, BF16_ROWS):
        if rows % cand == 0:
            best = cand
    assert best is not None, rows
    return best


def _dot_nn(a, b):
    return lax.dot_general(a, b, (((1,), (0,)), ((), ())), preferred_element_type=F32)


def _dot_nt(a, b):
    return lax.dot_general(a, b, (((1,), (1,)), ((), ())), preferred_element_type=F32)


def _dot_tn(a, b):
    return lax.dot_general(a, b, (((0,), (0,)), ((), ())), preferred_element_type=F32)


def _rms_fwd(x, g):
    r = lax.rsqrt(jnp.mean(x * x, axis=-1, keepdims=True) + RMS_EPS)
    return x * r * g, r


def _rms_bwd(dy, x, g, r):
    xhat = x * r
    dyg = dy * g
    dx = r * (dyg - xhat * jnp.mean(dyg * xhat, axis=-1, keepdims=True))
    dg = jnp.sum(dy * xhat, axis=0, keepdims=True)
    return dx, dg


def _accumulate(ref, val, first):
    if isinstance(first, bool):
        if first:
            ref[...] = val
        else:
            ref[...] += val
        return

    @pl.when(first)
    def _():
        ref[...] = val

    @pl.when(jnp.logical_not(first))
    def _():
        ref[...] += val


def _call_after(token, body, in_specs, args, **kw):
    if token is not None:
        inner = body
        body = lambda tok_ref, *refs: inner(*refs)
        in_specs = [pl.BlockSpec((8, 128), lambda *_: (0, 0))] + list(in_specs)
        args = (token,) + tuple(args)
    return pl.pallas_call(body, in_specs=in_specs, **kw)(*args)


def _resident(shape):
    return pl.BlockSpec(shape, lambda *_: (0,) * len(shape), pipeline_mode=pl.Buffered(1))


def _rope_tables(pos_col, tab):
    ang = pos_col.astype(F32) * tab[0:1, :]
    return jnp.cos(ang), jnp.sin(ang) * tab[1:2, :]


def _swap_halves(x):
    lane = lax.broadcasted_iota(jnp.int32, x.shape, 1)
    return jnp.where((lane % 64) < 32, pltpu.roll(x, 96, 1), pltpu.roll(x, 32, 1))


def _rope_apply(x, cos_t, sin_t):
    return x * cos_t + _swap_halves(x) * sin_t


def _rope_apply_t(dy, cos_t, sin_t):
    return dy * cos_t + _swap_halves(dy * sin_t)


def _ffn_fwd(h, g, wg_t, wu_t, wd, name, token=None, head=None):
    t, d = h.shape
    f = wg_t.shape[0]
    tm, tf = min(512, t), 256
    nf = f // tf
    n_in = 5 if head is None else 7

    def body(*refs):
        h_ref, g_ref, wg_ref, wu_ref, wd_ref = refs[:5]
        ho_ref, n_ref, gate_ref, up_ref = refs[n_in:n_in + 4]
        nb_sc, acc_sc = refs[-2:]
        y, _ = _rms_fwd(h_ref[...], g_ref[...])
        nb = y.astype(BF16)
        nb_sc[...] = nb
        n_ref[...] = nb
        acc_sc[...] = jnp.zeros_like(acc_sc)

        def f_tile(j):
            rows = pl.ds(pl.multiple_of(j * tf, tf), tf)
            nb = nb_sc[...]
            gt = _dot_nt(nb, wg_ref[rows, :])
            ut = _dot_nt(nb, wu_ref[rows, :])
            gate_ref[j] = gt.astype(BF16)
            up_ref[j] = ut.astype(BF16)
            act = (gt * jax.nn.sigmoid(gt)) * ut
            return _dot_nn(act.astype(BF16), wd_ref[rows, :])

        def pair(p, carry):
            acc_sc[...] += f_tile(2 * p) + f_tile(2 * p + 1)
            return carry

        lax.fori_loop(0, nf // 2, pair, 0)
        if nf % 2:
            acc_sc[...] += f_tile(nf - 1)
        ho = h_ref[...] + 0.5 * acc_sc[...]
        if head is None:
            ho_ref[...] = ho
            return
        t_ref, gf_ref = refs[5:7]
        loss_ref, dgf_ref = refs[n_in + 4:n_in + 6]
        gg = gf_ref[...]
        y, r = _rms_fwd(ho, gg)
        err = y - t_ref[...]
        part = 0.5 * jnp.sum(jnp.mean(err * err, axis=-1, keepdims=True), axis=0, keepdims=True)
        dx, dg = _rms_bwd(err * (1.0 / d), ho, gg, r)
        ho_ref[...] = dx
        first = pl.program_id(0) == 0
        _accumulate(loss_ref, jnp.broadcast_to(part, loss_ref.shape), first)
        _accumulate(dgf_ref, dg, first)

    row = pl.BlockSpec((tm, d), lambda i: (i, 0))
    tiles = pl.BlockSpec((nf, tm, tf), lambda i: (0, i, 0))
    in_specs = [row, _resident((1, d)), _resident((f, d)), _resident((f, d)), _resident((f, d))]
    args = (h, g, wg_t, wu_t, wd)
    out_specs = [row, row, tiles, tiles]
    out_shape = [jax.ShapeDtypeStruct((t, d), F32), jax.ShapeDtypeStruct((t, d), BF16),
                 jax.ShapeDtypeStruct((nf, t, tf), BF16), jax.ShapeDtypeStruct((nf, t, tf), BF16)]
    if head is not None:
        in_specs += [row, _resident((1, d))]
        args += tuple(head)
        out_specs += [pl.BlockSpec((8, 128), lambda i: (0, 0)), pl.BlockSpec((1, d), lambda i: (0, 0))]
        out_shape += [jax.ShapeDtypeStruct((8, 128), F32), jax.ShapeDtypeStruct((1, d), F32)]
    return _call_after(
        token, body, in_specs, args, name=name, grid=(t // tm,), out_specs=out_specs, out_shape=out_shape,
        scratch_shapes=[pltpu.VMEM((tm, d), BF16), pltpu.VMEM((tm, d), F32)],
        compiler_params=_cparams(),
    )


def _ffn_bwd_data(dho, h, g, gate, up, wg_t, wu_t, wd, name, token=None):
    t, d = h.shape
    f = wg_t.shape[0]
    tm, tf = min(1024, t), 256
    parts = 2 if tm % 512 == 0 else 1
    tp = tm // parts
    nf = f // tf

    def body(dho_ref, h_ref, g_ref, gate_ref, up_ref, wg_ref, wu_ref, wd_ref,
             dh_ref, dgate_ref, dup_ref, act_ref, dg_ref, dhb_sc, acc_sc):
        i, j = pl.program_id(0), pl.program_id(1)

        @pl.when(j == 0)
        def _():
            dhb_sc[...] = (0.5 * dho_ref[...]).astype(BF16)
            acc_sc[...] = jnp.zeros_like(acc_sc)

        for r in range(parts):
            rows = pl.ds(r * tp, tp)
            dact = _dot_nt(dhb_sc[rows, :], wd_ref[...])
            gt = gate_ref[0, rows, :].astype(F32)
            ut = up_ref[0, rows, :].astype(F32)
            sg = jax.nn.sigmoid(gt)
            silu = gt * sg
            dgb = (dact * ut * (sg * (1.0 + gt * (1.0 - sg)))).astype(BF16)
            dub = (dact * silu).astype(BF16)
            act_ref[rows, :] = (silu * ut).astype(BF16)
            dgate_ref[rows, :] = dgb
            dup_ref[rows, :] = dub
            acc_sc[rows, :] += _dot_nn(dgb, wg_ref[...]) + _dot_nn(dub, wu_ref[...])

        @pl.when(j == nf - 1)
        def _():
            x = h_ref[...]
            gg = g_ref[...]
            _, r = _rms_fwd(x, gg)
            dx, dg = _rms_bwd(acc_sc[...], x, gg, r)
            dh_ref[...] = dho_ref[...] + dx
            _accumulate(dg_ref, dg, i == 0)

    return _call_after(
        token, body,
        [pl.BlockSpec((tm, d), lambda i, j: (i, 0)),
         pl.BlockSpec((tm, d), lambda i, j: (i, 0)),
         pl.BlockSpec((1, d), lambda i, j: (0, 0)),
         pl.BlockSpec((1, tm, tf), lambda i, j: (j, i, 0)),
         pl.BlockSpec((1, tm, tf), lambda i, j: (j, i, 0)),
         pl.BlockSpec((tf, d), lambda i, j: (j, 0)),
         pl.BlockSpec((tf, d), lambda i, j: (j, 0)),
         pl.BlockSpec((tf, d), lambda i, j: (j, 0))],
        (dho, h, g, gate, up, wg_t, wu_t, wd),
        name=name, grid=(t // tm, nf),
        out_specs=[pl.BlockSpec((tm, d), lambda i, j: (i, 0)),
                   pl.BlockSpec((tm, tf), lambda i, j: (i, j)),
                   pl.BlockSpec((tm, tf), lambda i, j: (i, j)),
                   pl.BlockSpec((tm, tf), lambda i, j: (i, j)),
                   pl.BlockSpec((1, d), lambda i, j: (0, 0))],
        out_shape=[jax.ShapeDtypeStruct((t, d), F32), jax.ShapeDtypeStruct((t, f), BF16),
                   jax.ShapeDtypeStruct((t, f), BF16), jax.ShapeDtypeStruct((t, f), BF16),
                   jax.ShapeDtypeStruct((1, d), F32)],
        scratch_shapes=[pltpu.VMEM((tm, d), BF16), pltpu.VMEM((tm, d), F32)],
        compiler_params=_cparams(),
    )


def _tn_matmul(a, b, name, scale=1.0, tmm=None, out_dtype=F32, token=None):
    t, m = a.shape
    n = b.shape[1]
    tmm = m if tmm is None else tmm
    tk = min(1024, t)
    nk = t // tk

    def product(a_ref, b_ref):
        prod = _dot_tn(a_ref[...].astype(BF16), b_ref[...].astype(BF16))
        return prod * scale if scale != 1.0 else prod

    def body_f32(a_ref, b_ref, o_ref):
        _accumulate(o_ref, product(a_ref, b_ref), pl.program_id(1) == 0)

    def body_cast(a_ref, b_ref, o_ref, acc_sc):
        k = pl.program_id(1)
        _accumulate(acc_sc, product(a_ref, b_ref), k == 0)

        @pl.when(k == nk - 1)
        def _():
            o_ref[...] = acc_sc[...].astype(out_dtype)

    direct = out_dtype == F32
    return _call_after(
        token, body_f32 if direct else body_cast,
        [pl.BlockSpec((tk, tmm), lambda i, k: (k, i)),
         pl.BlockSpec((tk, n), lambda i, k: (k, 0))],
        (a, b),
        name=name, grid=(m // tmm, nk),
        out_specs=pl.BlockSpec((tmm, n), lambda i, k: (i, 0)),
        out_shape=jax.ShapeDtypeStruct((m, n), out_dtype),
        scratch_shapes=[] if direct else [pltpu.VMEM((tmm, n), F32)],
        compiler_params=_cparams(),
    )


def _mix_prep(h1, mix_norm, w_in, q_norm, wq_t, kv_norm, wkv, pos, rope_tab, token=None):
    t, d = h1.shape
    tm = min(512, t)

    def body(h_ref, gm_ref, win_ref, gq_ref, wq_ref, gkv_ref, wkv_ref, pos_ref, tab_ref,
             u_ref, z_ref, qn_ref, kvn_ref, q_ref, k_ref, v_ref):
        u, _ = _rms_fwd(h_ref[...], gm_ref[...])
        ub = u.astype(BF16)
        u_ref[...] = ub
        z = _dot_nn(ub, win_ref[...])
        z_ref[...] = z
        cos_t, sin_t = _rope_tables(pos_ref[...], tab_ref[...])
        qn, _ = _rms_fwd(z[:, 0:Q_RANK], gq_ref[...])
        qnb = qn.astype(BF16)
        qn_ref[...] = qnb
        q = _dot_nt(qnb, wq_ref[...])
        kvn, _ = _rms_fwd(z[:, Q_RANK:Q_RANK + KV_RANK], gkv_ref[...])
        kvnb = kvn.astype(BF16)
        kvn_ref[...] = kvnb
        kv = _dot_nn(kvnb, wkv_ref[...])
        k_pe = _rope_apply(z[:, Q_RANK + KV_RANK:Q_RANK + KV_RANK + 128], cos_t, sin_t)
        ones = jnp.ones((tm, V_DIM), F32)
        for hh in range(MLA_HEADS):
            b = hh * HEAD_PAD
            q_pe = _rope_apply(q[:, b + NOPE:b + HEAD_PAD], cos_t, sin_t)
            q_ref[hh] = jnp.concatenate([q[:, b:b + NOPE], q_pe], axis=-1).astype(BF16)
            k_ref[hh] = jnp.concatenate([kv[:, b:b + NOPE], k_pe], axis=-1).astype(BF16)
            v_ref[hh] = jnp.concatenate([kv[:, b + NOPE:b + HEAD_PAD], ones], axis=-1).astype(BF16)

    full = lambda shape: pl.BlockSpec(shape, lambda i: (0,) * len(shape))
    return _call_after(
        token, body,
        [pl.BlockSpec((tm, d), lambda i: (i, 0)), _resident((1, d)), _resident(w_in.shape), _resident((1, Q_RANK)),
         _resident(wq_t.shape), _resident((1, KV_RANK)), _resident(wkv.shape),
         pl.BlockSpec((tm, 1), lambda i: (i, 0)), _resident(rope_tab.shape)],
        (h1, mix_norm, w_in, q_norm, wq_t, kv_norm, wkv, pos, rope_tab),
        name="mix_prep", grid=(t // tm,),
        out_specs=[pl.BlockSpec((tm, d), lambda i: (i, 0)),
                   pl.BlockSpec((tm, d), lambda i: (i, 0)),
                   pl.BlockSpec((tm, Q_RANK), lambda i: (i, 0)),
                   pl.BlockSpec((tm, KV_RANK), lambda i: (i, 0)),
                   pl.BlockSpec((MLA_HEADS, tm, HEAD_PAD), lambda i: (0, i, 0)),
                   pl.BlockSpec((MLA_HEADS, tm, HEAD_PAD), lambda i: (0, i, 0)),
                   pl.BlockSpec((MLA_HEADS, tm, 2 * V_DIM), lambda i: (0, i, 0))],
        out_shape=[jax.ShapeDtypeStruct((t, d), BF16), jax.ShapeDtypeStruct((t, d), F32),
                   jax.ShapeDtypeStruct((t, Q_RANK), BF16), jax.ShapeDtypeStruct((t, KV_RANK), BF16),
                   jax.ShapeDtypeStruct((MLA_HEADS, t, HEAD_PAD), BF16),
                   jax.ShapeDtypeStruct((MLA_HEADS, t, HEAD_PAD), BF16),
                   jax.ShapeDtypeStruct((MLA_HEADS, t, 2 * V_DIM), BF16)],
        compiler_params=_cparams(),
    )


def _causal_mask(s):
    row = lax.broadcasted_iota(jnp.int32, s.shape, 0)
    col = lax.broadcasted_iota(jnp.int32, s.shape, 1)
    return jnp.where(col <= row, s, NEG_BIG)


def _attn_fwd(q, k, v):
    nh, t, _ = q.shape
    tq = tk = min(512, t)
    nq, nk = t // tq, t // tk

    pairs = [(i, j) for i in range(nq) for j in range(i + 1)]
    qi = jnp.asarray(np.array([i for i, _ in pairs], np.int32))
    kj = jnp.asarray(np.array([j for _, j in pairs], np.int32))

    def body(qi_ref, kj_ref, q_ref, k_ref, v_ref, o_ref, lse_ref, m_sc, acc_sc):
        n = pl.program_id(0)
        i, j = qi_ref[n], kj_ref[n]

        @pl.when(j == 0)
        def _():
            m_sc[...] = jnp.full_like(m_sc, NEG_BIG)
            acc_sc[...] = jnp.zeros_like(acc_sc)

        def step(diagonal):
            for hh in range(nh):
                s = _dot_nt(q_ref[hh], k_ref[hh]) * ATTN_SCALE
                if diagonal:
                    s = _causal_mask(s)
                m_old = m_sc[hh]
                m_new = jnp.maximum(m_old, jnp.max(s, axis=-1, keepdims=True))
                p = jnp.exp(s - m_new).astype(BF16)
                acc_sc[hh] = jnp.exp(m_old - m_new) * acc_sc[hh] + _dot_nn(p, v_ref[hh])
                m_sc[hh] = m_new

        @pl.when(j < i)
        def _():
            step(False)

        @pl.when(j == i)
        def _():
            step(True)
            for hh in range(nh):
                acc = acc_sc[hh]
                l = acc[:, V_DIM:2 * V_DIM]
                o_ref[:, hh * V_DIM:(hh + 1) * V_DIM] = (acc[:, 0:V_DIM] / l).astype(BF16)
                lse_ref[hh] = m_sc[hh] + jnp.log(l[:, 0:1])

    q_map = lambda n, qi_ref, kj_ref: (0, qi_ref[n], 0)
    kv_map = lambda n, qi_ref, kj_ref: (0, kj_ref[n], 0)
    return pl.pallas_call(
        body, name="attn_fwd",
        grid_spec=pltpu.PrefetchScalarGridSpec(
            num_scalar_prefetch=2, grid=(len(pairs),),
            in_specs=[pl.BlockSpec((nh, tq, HEAD_PAD), q_map),
                      pl.BlockSpec((nh, tk, HEAD_PAD), kv_map),
                      pl.BlockSpec((nh, tk, 2 * V_DIM), kv_map)],
            out_specs=[pl.BlockSpec((tq, nh * V_DIM), lambda n, qi_ref, kj_ref: (qi_ref[n], 0)),
                       pl.BlockSpec((nh, tq, 1), q_map)],
            scratch_shapes=[pltpu.VMEM((nh, tq, 1), F32), pltpu.VMEM((nh, tq, 2 * V_DIM), F32)]),
        out_shape=[jax.ShapeDtypeStruct((t, nh * V_DIM), BF16), jax.ShapeDtypeStruct((nh, t, 1), F32)],
        compiler_params=_cparams(),
    )(qi, kj, q, k, v)


def _attn_delta(o, do):
    t, w = o.shape
    nh = w // V_DIM
    tm = min(512, t)

    def body(o_ref, do_ref, d_ref):
        prod = o_ref[...].astype(F32) * do_ref[...].astype(F32)
        for hh in range(nh):
            d_ref[hh] = jnp.sum(prod[:, hh * V_DIM:(hh + 1) * V_DIM], axis=-1, keepdims=True)

    return pl.pallas_call(
        body, name="attn_delta", grid=(t // tm,),
        in_specs=[pl.BlockSpec((tm, w), lambda i: (i, 0)), pl.BlockSpec((tm, w), lambda i: (i, 0))],
        out_specs=pl.BlockSpec((nh, tm, 1), lambda i: (0, i, 0)),
        out_shape=jax.ShapeDtypeStruct((nh, t, 1), F32),
        compiler_params=_cparams(),
    )(o, do)


ATTN_BWD_HEADS = 2


def _attn_bwd(q, k, v, do, lse, delta):
    nh, t, _ = q.shape
    hp = ATTN_BWD_HEADS
    tq = tk = min(512, t)
    nq, nk = t // tq, t // tk

    pairs = [(j, i) for j in range(nk) for i in range(j, nq)]
    kj = jnp.asarray(np.array([j for j, _ in pairs], np.int32))
    qi = jnp.asarray(np.array([i for _, i in pairs], np.int32))

    def body(kj_ref, qi_ref, q_ref, k_ref, v_ref, do_ref, lse_ref, dlt_ref, dq_ref, dk_ref, dv_ref):
        n = pl.program_id(1)
        j, i = kj_ref[n], qi_ref[n]

        @pl.when(n == 0)
        def _():
            dq_ref[...] = jnp.zeros_like(dq_ref)

        def step(diagonal):
            for hh in range(hp):
                qq, kk = q_ref[hh], k_ref[hh]
                dob = do_ref[:, hh * V_DIM:(hh + 1) * V_DIM]
                s = _dot_nt(qq, kk) * ATTN_SCALE
                if diagonal:
                    s = _causal_mask(s)
                p = jnp.exp(s - lse_ref[hh])
                dpp = _dot_nt(dob, v_ref[hh])
                dsb = (p * (dpp - dlt_ref[hh]) * ATTN_SCALE).astype(BF16)
                _accumulate(dv_ref.at[hh], _dot_tn(p.astype(BF16), dob), diagonal)
                _accumulate(dk_ref.at[hh], _dot_tn(dsb, qq), diagonal)
                dq_ref[hh, pl.ds(pl.multiple_of(i * tq, tq), tq), :] += _dot_nn(dsb, kk)

        @pl.when(i > j)
        def _():
            step(False)

        @pl.when(i == j)
        def _():
            step(True)

    q_map = lambda h, n, kj_ref, qi_ref: (h, qi_ref[n], 0)
    k_map = lambda h, n, kj_ref, qi_ref: (h, kj_ref[n], 0)
    return pl.pallas_call(
        body, name="attn_bwd",
        grid_spec=pltpu.PrefetchScalarGridSpec(
            num_scalar_prefetch=2, grid=(nh // hp, len(pairs)),
            in_specs=[pl.BlockSpec((hp, tq, HEAD_PAD), q_map),
                      pl.BlockSpec((hp, tk, HEAD_PAD), k_map),
                      pl.BlockSpec((hp, tk, V_DIM), k_map),
                      pl.BlockSpec((tq, hp * V_DIM), lambda h, n, kj_ref, qi_ref: (qi_ref[n], h)),
                      pl.BlockSpec((hp, tq, 1), q_map),
                      pl.BlockSpec((hp, tq, 1), q_map)],
            out_specs=[pl.BlockSpec((hp, t, HEAD_PAD), lambda h, n, kj_ref, qi_ref: (h, 0, 0)),
                       pl.BlockSpec((hp, tk, HEAD_PAD), k_map),
                       pl.BlockSpec((hp, tk, V_DIM), k_map)]),
        out_shape=[jax.ShapeDtypeStruct((nh, t, HEAD_PAD), F32), jax.ShapeDtypeStruct((nh, t, HEAD_PAD), F32),
                   jax.ShapeDtypeStruct((nh, t, V_DIM), F32)],
        compiler_params=_cparams(),
    )(kj, qi, q, k, v, do, lse, delta)


def _pool_counts(first_token, rows, w):
    tok = lax.broadcasted_iota(jnp.int32, (rows, POOL_CH), 0) + first_token
    return jnp.minimum(tok + 1, w).astype(F32)


def _pool_centered(zbuf, g, w, i, tm):
    lanes = pl.ds(g * POOL_CH, POOL_CH)
    cur = zbuf[pl.ds(POOL_HALO, tm), lanes]
    win = cur
    for s in range(1, w):
        win = win + zbuf[pl.ds(POOL_HALO - s, tm), lanes]
    return win / _pool_counts(i * tm, tm, w) - cur


def _pool_load(zbuf, z_ref, halo_ref, i, tm):
    @pl.when(i == 0)
    def _():
        zbuf[pl.ds(0, POOL_HALO), :] = jnp.zeros((POOL_HALO, zbuf.shape[1]), F32)

    @pl.when(i > 0)
    def _():
        zbuf[pl.ds(0, POOL_HALO), :] = halo_ref[...]

    zbuf[pl.ds(POOL_HALO, tm), :] = z_ref[...]


def _pool_fwd(z, pool_w, pool_scale):
    t = z.shape[0]
    pw = len(POOL_WINDOWS) * POOL_CH
    tm = min(512, t)
    hb = tm // POOL_HALO

    def body(z_ref, halo_ref, w_ref, sc_ref, p_ref, zbuf):
        i = pl.program_id(0)
        _pool_load(zbuf, z_ref, halo_ref, i, tm)
        for g, w in enumerate(POOL_WINDOWS):
            c = _pool_centered(zbuf, g, w, i, tm)
            y = _dot_nn(c.astype(BF16), w_ref[g]) * sc_ref[:, g * POOL_CH:(g + 1) * POOL_CH]
            p_ref[:, g * POOL_CH:(g + 1) * POOL_CH] = y.astype(BF16)

    return pl.pallas_call(
        body, name="pool_fwd", grid=(t // tm,),
        in_specs=[pl.BlockSpec((tm, pw), lambda i: (i, 1)),
                  pl.BlockSpec((POOL_HALO, pw), lambda i: (jnp.maximum(i * hb - 1, 0), 1)),
                  pl.BlockSpec(pool_w.shape, lambda i: (0, 0, 0)),
                  pl.BlockSpec((1, pw), lambda i: (0, 0))],
        out_specs=pl.BlockSpec((tm, pw), lambda i: (i, 0)),
        out_shape=jax.ShapeDtypeStruct((t, pw), BF16),
        scratch_shapes=[pltpu.VMEM((POOL_HALO + tm, pw), F32)],
        compiler_params=_cparams(),
    )(z, z, pool_w, pool_scale)


def _pool_bwd(dp, z, pool_w, pool_scale):
    t = z.shape[0]
    ng = len(POOL_WINDOWS)
    pw = ng * POOL_CH
    tm = min(512, t)
    hb = tm // POOL_HALO
    nt = t // tm

    def body(dp_ref, dpn_ref, z_ref, halo_ref, w_ref, sc_ref, dz_ref, dw_ref, dsc_ref, zbuf, dbuf):
        i = pl.program_id(0)
        _pool_load(zbuf, z_ref, halo_ref, i, tm)

        @pl.when(i == 0)
        def _():
            dw_ref[...] = jnp.zeros_like(dw_ref)
            dsc_ref[...] = jnp.zeros_like(dsc_ref)

        nxt_ok = (i < nt - 1).astype(F32)
        for g, w in enumerate(POOL_WINDOWS):
            lanes = pl.ds(g * POOL_CH, POOL_CH)
            cols = slice(g * POOL_CH, (g + 1) * POOL_CH)
            sc = sc_ref[:, cols]
            wg = w_ref[g]
            c = _pool_centered(zbuf, g, w, i, tm).astype(BF16)
            ypre = _dot_nn(c, wg)
            dpg = dp_ref[:, cols].astype(F32)
            dsc_ref[:, cols] += jnp.sum(dpg * ypre, axis=0, keepdims=True)
            dyb = (dpg * sc).astype(BF16)
            dw_ref[g] += _dot_tn(c, dyb)
            dd = _dot_nt(dyb, wg)
            dyn = (dpn_ref[:, cols].astype(F32) * sc).astype(BF16)
            ddn = _dot_nt(dyn, wg) * nxt_ok
            dbuf[pl.ds(0, tm), lanes] = dd / _pool_counts(i * tm, tm, w)
            dbuf[pl.ds(tm, POOL_HALO), lanes] = ddn / _pool_counts((i + 1) * tm, POOL_HALO, w)
            acc = -dd
            for s in range(w):
                acc = acc + dbuf[pl.ds(s, tm), lanes]
            dz_ref[:, cols] = acc

    return pl.pallas_call(
        body, name="pool_bwd", grid=(nt,),
        in_specs=[pl.BlockSpec((tm, pw), lambda i: (i, 0)),
                  pl.BlockSpec((POOL_HALO, pw), lambda i: (jnp.minimum((i + 1) * hb, t // POOL_HALO - 1), 0)),
                  pl.BlockSpec((tm, pw), lambda i: (i, 1)),
                  pl.BlockSpec((POOL_HALO, pw), lambda i: (jnp.maximum(i * hb - 1, 0), 1)),
                  pl.BlockSpec(pool_w.shape, lambda i: (0, 0, 0)),
                  pl.BlockSpec((1, pw), lambda i: (0, 0))],
        out_specs=[pl.BlockSpec((tm, pw), lambda i: (i, 0)),
                   pl.BlockSpec((ng, POOL_CH, POOL_CH), lambda i: (0, 0, 0)),
                   pl.BlockSpec((1, pw), lambda i: (0, 0))],
        out_shape=[jax.ShapeDtypeStruct((t, pw), F32), jax.ShapeDtypeStruct((ng, POOL_CH, POOL_CH), F32),
                   jax.ShapeDtypeStruct((1, pw), F32)],
        scratch_shapes=[pltpu.VMEM((POOL_HALO + tm, pw), F32), pltpu.VMEM((tm + POOL_HALO, pw), F32)],
        compiler_params=_cparams(),
    )(dp, dp, z, z, pool_w, pool_scale)


def _mla_bwd(dq_h, dk_h, dv_h, z, dz_pool, h1, dh2, mix_norm, w_in, q_norm, wq_t, kv_norm, wkv, pos, rope_tab):
    t, d = h1.shape
    tm = min(512, t)

    def body(dqh_ref, dkh_ref, dvh_ref, z_ref, dzp_ref, h_ref, dh2_ref, gm_ref, win_ref, gq_ref, wq_ref, gkv_ref,
             wkv_ref, pos_ref, tab_ref, dh1_ref, dq_ref, dkv_ref, dz_ref, dgq_ref, dgkv_ref, dgm_ref):
        i = pl.program_id(0)
        first = i == 0
        cos_t, sin_t = _rope_tables(pos_ref[...], tab_ref[...])
        dq_parts, dkv_parts = [], []
        dk_pe = jnp.zeros((tm, 128), F32)
        for hh in range(MLA_HEADS):
            dqh = dqh_ref[hh]
            dq_parts += [dqh[:, 0:NOPE], _rope_apply_t(dqh[:, NOPE:HEAD_PAD], cos_t, sin_t)]
            dkh = dkh_ref[hh]
            dkv_parts += [dkh[:, 0:NOPE], dvh_ref[hh]]
            dk_pe = dk_pe + dkh[:, NOPE:HEAD_PAD]
        dqb = jnp.concatenate(dq_parts, axis=-1).astype(BF16)
        dkvb = jnp.concatenate(dkv_parts, axis=-1).astype(BF16)
        dq_ref[...] = dqb
        dkv_ref[...] = dkvb
        z = z_ref[...]
        c_q = z[:, 0:Q_RANK]
        gq = gq_ref[...]
        _, rq = _rms_fwd(c_q, gq)
        dcq, dgq = _rms_bwd(_dot_nn(dqb, wq_ref[...]), c_q, gq, rq)
        c_kv = z[:, Q_RANK:Q_RANK + KV_RANK]
        gkv = gkv_ref[...]
        _, rkv = _rms_fwd(c_kv, gkv)
        dckv, dgkv = _rms_bwd(_dot_nt(dkvb, wkv_ref[...]), c_kv, gkv, rkv)
        dkr = _rope_apply_t(dk_pe, cos_t, sin_t)
        dzb = jnp.concatenate([dcq, dckv, dkr, dzp_ref[...]], axis=-1).astype(BF16)
        dz_ref[...] = dzb
        x = h_ref[...]
        gm = gm_ref[...]
        _, rm = _rms_fwd(x, gm)
        dx, dgm = _rms_bwd(_dot_nt(dzb, win_ref[...]), x, gm, rm)
        dh1_ref[...] = dh2_ref[...] + dx
        _accumulate(dgq_ref, dgq, first)
        _accumulate(dgkv_ref, dgkv, first)
        _accumulate(dgm_ref, dgm, first)

    full = lambda shape: pl.BlockSpec(shape, lambda i: (0,) * len(shape))
    row = lambda w: pl.BlockSpec((tm, w), lambda i: (i, 0))
    head = lambda w: pl.BlockSpec((MLA_HEADS, tm, w), lambda i: (0, i, 0))
    pw = len(POOL_WINDOWS) * POOL_CH
    return pl.pallas_call(
        body, name="mla_bwd", grid=(t // tm,),
        in_specs=[head(HEAD_PAD), head(HEAD_PAD), head(V_DIM), row(d), row(pw), row(d), row(d),
                  _resident((1, d)), _resident(w_in.shape), _resident((1, Q_RANK)), _resident(wq_t.shape),
                  _resident((1, KV_RANK)), _resident(wkv.shape), row(1), _resident(rope_tab.shape)],
        out_specs=[row(d), row(d), row(d), row(d), full((1, Q_RANK)), full((1, KV_RANK)), full((1, d))],
        out_shape=[jax.ShapeDtypeStruct((t, d), F32), jax.ShapeDtypeStruct((t, d), BF16),
                   jax.ShapeDtypeStruct((t, d), BF16), jax.ShapeDtypeStruct((t, d), BF16),
                   jax.ShapeDtypeStruct((1, Q_RANK), F32), jax.ShapeDtypeStruct((1, KV_RANK), F32),
                   jax.ShapeDtypeStruct((1, d), F32)],
        compiler_params=_cparams(),
    )(dq_h, dk_h, dv_h, z, dz_pool, h1, dh2, mix_norm, w_in, q_norm, wq_t, kv_norm, wkv, pos, rope_tab)


def _mem_kv(mem, mem_norm, wmkv):
    n, d = mem.shape

    def body(mem_ref, g_ref, w_ref, memn_ref, k_ref, v_ref):
        y, _ = _rms_fwd(mem_ref[...], g_ref[...])
        yb = y.astype(BF16)
        memn_ref[...] = yb
        for hh in range(MEM_HEADS):
            k_ref[hh] = _dot_nn(yb, w_ref[hh]).astype(BF16)
            v_ref[hh] = _dot_nn(yb, w_ref[MEM_HEADS + hh]).astype(BF16)

    return pl.pallas_call(
        body, name="mem_kv",
        out_shape=[jax.ShapeDtypeStruct((n, d), BF16), jax.ShapeDtypeStruct((MEM_HEADS, n, MEM_HD), BF16),
                   jax.ShapeDtypeStruct((MEM_HEADS, n, MEM_HD), BF16)],
        compiler_params=_cparams(),
    )(mem, mem_norm, wmkv)


def _mem_softmax(qb, km):
    s = _dot_nt(qb, km) * MEM_SCALE
    e = jnp.exp(s - jnp.max(s, axis=-1, keepdims=True))
    return e / jnp.sum(e, axis=-1, keepdims=True)


def _xattn_fwd(h1, a, p, w_out, g, wmq, km, vm, wmo, token=None):
    t, d = h1.shape
    tm = min(512, t)
    half = a.shape[1]

    def body(h_ref, a_ref, p_ref, wo_ref, g_ref, wmq_ref, km_ref, vm_ref, wmo_ref,
             h2_ref, h3_ref, hn_ref, q_ref, o_ref):
        h2 = h_ref[...] + _dot_nn(a_ref[...], wo_ref[0:half, :]) + _dot_nn(p_ref[...], wo_ref[half:2 * half, :])
        h2_ref[...] = h2
        hn, _ = _rms_fwd(h2, g_ref[...])
        hnb = hn.astype(BF16)
        hn_ref[...] = hnb
        qb = _dot_nn(hnb, wmq_ref[...]).astype(BF16)
        q_ref[...] = qb
        outs = []
        for hh in range(MEM_HEADS):
            pr = _mem_softmax(qb[:, hh * MEM_HD:(hh + 1) * MEM_HD], km_ref[hh])
            outs.append(_dot_nn(pr.astype(BF16), vm_ref[hh]))
        ob = jnp.concatenate(outs, axis=-1).astype(BF16)
        o_ref[...] = ob
        h3_ref[...] = h2 + _dot_nn(ob, wmo_ref[...])

    full = lambda shape: pl.BlockSpec(shape, lambda i: (0,) * len(shape))
    row = lambda w: pl.BlockSpec((tm, w), lambda i: (i, 0))
    return _call_after(
        token, body,
        [row(d), row(half), row(half), _resident(w_out.shape), _resident((1, d)), _resident(wmq.shape),
         _resident(km.shape), _resident(vm.shape), _resident(wmo.shape)],
        (h1, a, p, w_out, g, wmq, km, vm, wmo),
        name="xattn_fwd", grid=(t // tm,),
        out_specs=[row(d), row(d), row(d), row(d), row(d)],
        out_shape=[jax.ShapeDtypeStruct((t, d), F32), jax.ShapeDtypeStruct((t, d), F32),
                   jax.ShapeDtypeStruct((t, d), BF16), jax.ShapeDtypeStruct((t, d), BF16),
                   jax.ShapeDtypeStruct((t, d), BF16)],
        compiler_params=_cparams(),
    )


def _xattn_bwd(dh3, h2, qm, g, wmq, km, vm, wmo, w_out, token=None):
    t, d = h2.shape
    tm = min(512, t)
    half = d // 2

    def body(dh3_ref, h2_ref, q_ref, g_ref, wmq_ref, km_ref, vm_ref, wmo_ref, wo_ref,
             dh2_ref, dq_ref, da_ref, dp_ref, dk_ref, dv_ref, dg_ref):
        i = pl.program_id(0)
        first = i == 0

        @pl.when(first)
        def _():
            dk_ref[...] = jnp.zeros_like(dk_ref)
            dv_ref[...] = jnp.zeros_like(dv_ref)

        dh3 = dh3_ref[...]
        dob = _dot_nt(dh3.astype(BF16), wmo_ref[...]).astype(BF16)
        qb = q_ref[...]
        dq_parts = []
        for hh in range(MEM_HEADS):
            cols = slice(hh * MEM_HD, (hh + 1) * MEM_HD)
            kk, vv = km_ref[hh], vm_ref[hh]
            pr = _mem_softmax(qb[:, cols], kk)
            doh = dob[:, cols]
            dv_ref[hh] += _dot_tn(pr.astype(BF16), doh)
            dpp = _dot_nt(doh, vv)
            dsb = (pr * (dpp - jnp.sum(dpp * pr, axis=-1, keepdims=True)) * MEM_SCALE).astype(BF16)
            dq_parts.append(_dot_nn(dsb, kk))
            dk_ref[hh] += _dot_tn(dsb, qb[:, cols])
        dqb = jnp.concatenate(dq_parts, axis=-1).astype(BF16)
        dq_ref[...] = dqb
        x = h2_ref[...]
        gg = g_ref[...]
        _, r = _rms_fwd(x, gg)
        dx, dg = _rms_bwd(_dot_nt(dqb, wmq_ref[...]), x, gg, r)
        dh2 = dh3 + dx
        dh2_ref[...] = dh2
        dap = _dot_nt(dh2.astype(BF16), wo_ref[...])
        da_ref[...] = dap[:, 0:half].astype(BF16)
        dp_ref[...] = dap[:, half:d].astype(BF16)
        _accumulate(dg_ref, dg, first)

    full = lambda shape: pl.BlockSpec(shape, lambda i: (0,) * len(shape))
    row = lambda w: pl.BlockSpec((tm, w), lambda i: (i, 0))
    return _call_after(
        token, body,
        [row(d), row(d), row(d), _resident((1, d)), _resident(wmq.shape), _resident(km.shape), _resident(vm.shape),
         _resident(wmo.shape), _resident(w_out.shape)],
        (dh3, h2, qm, g, wmq, km, vm, wmo, w_out),
        name="xattn_bwd", grid=(t // tm,),
        out_specs=[row(d), row(d), row(half), row(half), full(km.shape), full(vm.shape), full((1, d))],
        out_shape=[jax.ShapeDtypeStruct((t, d), F32), jax.ShapeDtypeStruct((t, d), BF16),
                   jax.ShapeDtypeStruct((t, half), BF16), jax.ShapeDtypeStruct((t, half), BF16),
                   jax.ShapeDtypeStruct(km.shape, F32), jax.ShapeDtypeStruct(vm.shape, F32),
                   jax.ShapeDtypeStruct((1, d), F32)],
        compiler_params=_cparams(),
    )


def _mem_kv_bwd(dkm, dvm, memn, mem, mem_norm, wmkv):
    n, d = mem.shape

    def body(dk_ref, dv_ref, memn_ref, mem_ref, g_ref, w_ref, dw_ref, dg_ref):
        memn = memn_ref[...]
        dmemn = jnp.zeros((n, d), F32)
        for s in range(2 * MEM_HEADS):
            src = dk_ref[s] if s < MEM_HEADS else dv_ref[s - MEM_HEADS]
            db = src.astype(BF16)
            dw_ref[s] = _dot_tn(memn, db)
            dmemn = dmemn + _dot_nt(db, w_ref[s])
        x = mem_ref[...]
        gg = g_ref[...]
        _, r = _rms_fwd(x, gg)
        _, dg = _rms_bwd(dmemn, x, gg, r)
        dg_ref[...] = dg

    return pl.pallas_call(
        body, name="mem_kv_bwd",
        out_shape=[jax.ShapeDtypeStruct(wmkv.shape, F32), jax.ShapeDtypeStruct((1, d), F32)],
        compiler_params=_cparams(),
    )(dkm, dvm, memn, mem, mem_norm, wmkv)


MESH_ID = pl.DeviceIdType.MESH
ANY = pl.BlockSpec(memory_space=pl.ANY)


def _coords():
    return lax.axis_index("x"), lax.axis_index("y"), lax.axis_index("c")


def _other_chips(x, y):
    return [(1 - x, y), (x, 1 - y), (1 - x, 1 - y)]


def _core_reduce(g, tag):
    _, r, w = g.shape

    def body(g_ref, part_ref, own_sc, recv_sc, send_sems, recv_sems, local_sems):
        x, y, c = _coords()
        sent, local = [], []
        for chip in range(4):
            sent.append(pltpu.make_async_remote_copy(
                src_ref=g_ref.at[2 * chip + (1 - c)], dst_ref=recv_sc.at[chip],
                send_sem=send_sems.at[chip], recv_sem=recv_sems.at[chip],
                device_id=(x, y, 1 - c), device_id_type=MESH_ID))
            local.append(pltpu.make_async_copy(g_ref.at[2 * chip + c], own_sc.at[chip], local_sems.at[chip]))
        for cp in sent + local:
            cp.start()
        for chip in range(4):
            local[chip].wait()
            sent[chip].wait_recv()
            part_ref[chip] = (own_sc[chip].astype(F32) + recv_sc[chip].astype(F32)).astype(part_ref.dtype)
        for cp in sent:
            cp.wait_send()

    return pl.pallas_call(
        body, name="core_reduce_" + tag,
        out_shape=jax.ShapeDtypeStruct((4, r, w), g.dtype),
        in_specs=[ANY], out_specs=pl.BlockSpec(memory_space=pltpu.VMEM),
        scratch_shapes=[pltpu.VMEM((4, r, w), g.dtype), pltpu.VMEM((4, r, w), g.dtype),
                        pltpu.SemaphoreType.DMA((4,)), pltpu.SemaphoreType.DMA((4,)), pltpu.SemaphoreType.DMA((4,))],
        compiler_params=_cparams(),
    )(g)


HBM_SPEC = pl.BlockSpec(memory_space=pltpu.HBM)
SEM_SPEC = pl.BlockSpec(memory_space=pltpu.SEMAPHORE)
SPLIT_EFFECT = pltpu.SideEffectType.DATAFLOW_SIDE_EFFECTING


def _ici_refs(gather, src_ref, land_ref, j, px, py, slot_chip, c):
    if gather:
        return src_ref, land_ref.at[:, 4 * slot_chip[0] + 2 * slot_chip[1] + c]
    return src_ref.at[2 * px + py], land_ref.at[j]


def _ici_start(src, after, name, gather):
    r, w = src.shape[-2:]
    land_shape = (src.shape[0], N_DEV, r, w) if gather else (3, r, w)

    def body(src_ref, land_ref, after_ref, send_sems, recv_sems, src_thru, land_thru, token):
        x, y, c = _coords()
        for j, (px, py) in enumerate(_other_chips(x, y)):
            s_ref, d_ref = _ici_refs(gather, src_ref, land_ref, j, px, py, (x, y), c)
            pltpu.make_async_remote_copy(
                src_ref=s_ref, dst_ref=d_ref, send_sem=send_sems.at[j], recv_sem=recv_sems.at[j],
                device_id=(px, py, c), device_id_type=MESH_ID).start()
        token[...] = jnp.zeros_like(token)

    return pl.pallas_call(
        body, name=name,
        out_shape=(pltpu.SemaphoreType.DMA((3,)), pltpu.SemaphoreType.DMA((3,)), pltpu.HBM(src.shape, src.dtype),
                   pltpu.HBM(land_shape, src.dtype), jax.ShapeDtypeStruct((8, 128), F32)),
        in_specs=(HBM_SPEC, HBM_SPEC, ANY),
        out_specs=(SEM_SPEC, SEM_SPEC, HBM_SPEC, HBM_SPEC, pl.BlockSpec(memory_space=pltpu.VMEM)),
        input_output_aliases={0: 2, 1: 3},
        compiler_params=pltpu.CompilerParams(has_side_effects=SPLIT_EFFECT),
    )(pltpu.with_memory_space_constraint(src, pltpu.HBM),
      pltpu.with_memory_space_constraint(lax.empty(land_shape, src.dtype), pltpu.HBM), after)


def _ici_wait(started, after, name, gather):
    send_sems, recv_sems, src_thru, land_thru, _ = started

    def body(src_ref, land_ref, send_sems, recv_sems, after_ref, src_dead, got_ref):
        x, y, c = _coords()
        for j, (px, py) in enumerate(_other_chips(x, y)):
            s_ref, d_ref = _ici_refs(gather, src_ref, land_ref, j, px, py, (px, py), c)
            copy = pltpu.make_async_remote_copy(
                src_ref=s_ref, dst_ref=d_ref, send_sem=send_sems.at[j], recv_sem=recv_sems.at[j],
                device_id=(px, py, c), device_id_type=MESH_ID)
            copy.wait_send()
            copy.wait_recv()

    return pl.pallas_call(
        body, name=name,
        out_shape=(pltpu.HBM(src_thru.shape, src_thru.dtype), pltpu.HBM(land_thru.shape, land_thru.dtype)),
        in_specs=(HBM_SPEC, HBM_SPEC, SEM_SPEC, SEM_SPEC, ANY),
        out_specs=(HBM_SPEC, HBM_SPEC), input_output_aliases={0: 0, 1: 1},
        compiler_params=pltpu.CompilerParams(has_side_effects=SPLIT_EFFECT),
    )(src_thru, land_thru, send_sems, recv_sems, after)


def _neighbour(k, x, y):
    return (1 - x, y) if k == 0 else (x, 1 - y)


def _slot(ref, px, py, c):
    return ref.at[:, 4 * px + 2 * py + c]


def _near_start(src, after, name):
    land_shape = (src.shape[0], N_DEV) + src.shape[1:]

    def body(src_ref, land_ref, after_ref, send_sems, recv_sems, src_thru, land_thru, token):
        x, y, c = _coords()
        for k in range(2):
            px, py = _neighbour(k, x, y)
            pltpu.make_async_remote_copy(
                src_ref=src_ref, dst_ref=_slot(land_ref, x, y, c), send_sem=send_sems.at[k],
                recv_sem=recv_sems.at[k], device_id=(px, py, c), device_id_type=MESH_ID).start()
        token[...] = jnp.zeros_like(token)

    return pl.pallas_call(
        body, name=name,
        out_shape=(pltpu.SemaphoreType.DMA((2,)), pltpu.SemaphoreType.DMA((2,)), pltpu.HBM(src.shape, src.dtype),
                   pltpu.HBM(land_shape, src.dtype), jax.ShapeDtypeStruct((8, 128), F32)),
        in_specs=(HBM_SPEC, HBM_SPEC, ANY),
        out_specs=(SEM_SPEC, SEM_SPEC, HBM_SPEC, HBM_SPEC, pl.BlockSpec(memory_space=pltpu.VMEM)),
        input_output_aliases={0: 2, 1: 3},
        compiler_params=pltpu.CompilerParams(has_side_effects=SPLIT_EFFECT),
    )(pltpu.with_memory_space_constraint(src, pltpu.HBM),
      pltpu.with_memory_space_constraint(lax.empty(land_shape, src.dtype), pltpu.HBM), after)


def _near_wait(started, after, name):
    send_sems, recv_sems, src_thru, land_thru, _ = started

    def body(src_ref, land_ref, send_sems, recv_sems, after_ref, src_dead, got_ref):
        x, y, c = _coords()
        for k in range(2):
            px, py = _neighbour(k, x, y)
            copy = pltpu.make_async_remote_copy(
                src_ref=src_ref, dst_ref=_slot(land_ref, px, py, c), send_sem=send_sems.at[k],
                recv_sem=recv_sems.at[k], device_id=(px, py, c), device_id_type=MESH_ID)
            copy.wait_send()
            copy.wait_recv()

    return pl.pallas_call(
        body, name=name,
        out_shape=(pltpu.HBM(src_thru.shape, src_thru.dtype), pltpu.HBM(land_thru.shape, land_thru.dtype)),
        in_specs=(HBM_SPEC, HBM_SPEC, SEM_SPEC, SEM_SPEC, ANY),
        out_specs=(HBM_SPEC, HBM_SPEC), input_output_aliases={0: 0, 1: 1},
        compiler_params=pltpu.CompilerParams(has_side_effects=SPLIT_EFFECT),
    )(src_thru, land_thru, send_sems, recv_sems, after)


def _far_refs(land_ref, k, x, y, c, arriving):
    half = land_ref.shape[2] // 2
    rows = pl.ds(k * half, half)
    ox, oy = (1 - x, 1 - y) if arriving else _neighbour(k, x, y)
    return land_ref.at[:, 4 * ox + 2 * oy + c, rows]


def _far_start(land, after, name):
    def body(land_ref, after_ref, send_sems, recv_sems, land_thru, token):
        x, y, c = _coords()
        for k in range(2):
            block = _far_refs(land_ref, k, x, y, c, False)
            px, py = _neighbour(1 - k, x, y)
            pltpu.make_async_remote_copy(
                src_ref=block, dst_ref=block, send_sem=send_sems.at[k], recv_sem=recv_sems.at[k],
                device_id=(px, py, c), device_id_type=MESH_ID).start()
        token[...] = jnp.zeros_like(token)

    return pl.pallas_call(
        body, name=name,
        out_shape=(pltpu.SemaphoreType.DMA((2,)), pltpu.SemaphoreType.DMA((2,)),
                   pltpu.HBM(land.shape, land.dtype), jax.ShapeDtypeStruct((8, 128), F32)),
        in_specs=(HBM_SPEC, ANY),
        out_specs=(SEM_SPEC, SEM_SPEC, HBM_SPEC, pl.BlockSpec(memory_space=pltpu.VMEM)),
        input_output_aliases={0: 2},
        compiler_params=pltpu.CompilerParams(has_side_effects=SPLIT_EFFECT),
    )(pltpu.with_memory_space_constraint(land, pltpu.HBM), after)


def _far_wait(started, after, name):
    send_sems, recv_sems, land_thru, _ = started

    def body(land_ref, send_sems, recv_sems, after_ref, got_ref):
        x, y, c = _coords()
        for k in range(2):
            px, py = _neighbour(1 - k, x, y)
            copy = pltpu.make_async_remote_copy(
                src_ref=_far_refs(land_ref, k, x, y, c, False), dst_ref=_far_refs(land_ref, k, x, y, c, True),
                send_sem=send_sems.at[k], recv_sem=recv_sems.at[k], device_id=(px, py, c), device_id_type=MESH_ID)
            copy.wait_send()
            copy.wait_recv()

    return pl.pallas_call(
        body, name=name,
        out_shape=pltpu.HBM(land_thru.shape, land_thru.dtype),
        in_specs=(HBM_SPEC, SEM_SPEC, SEM_SPEC, ANY),
        out_specs=HBM_SPEC, input_output_aliases={0: 0},
        compiler_params=pltpu.CompilerParams(has_side_effects=SPLIT_EFFECT),
    )(land_thru, send_sems, recv_sems, after)


def _peer(k, x, y, c):
    return x ^ ((k >> 2) & 1), y ^ ((k >> 1) & 1), c ^ (k & 1)


def _peers_start(src, after, name):
    r, w = src.shape
    x, y, c = _coords()
    land = lax.dynamic_update_slice(jnp.zeros((N_DEV, r, w), src.dtype), src[None], (4 * x + 2 * y + c, 0, 0))

    def body(src_ref, land_ref, after_ref, send_sems, recv_sems, src_thru, land_thru, token):
        x, y, c = _coords()
        for k in range(1, N_DEV):
            pltpu.make_async_remote_copy(
                src_ref=src_ref, dst_ref=land_ref.at[4 * x + 2 * y + c],
                send_sem=send_sems.at[k - 1], recv_sem=recv_sems.at[k - 1],
                device_id=_peer(k, x, y, c), device_id_type=MESH_ID).start()
        token[...] = jnp.zeros_like(token)

    return pl.pallas_call(
        body, name=name,
        out_shape=(pltpu.SemaphoreType.DMA((N_DEV - 1,)), pltpu.SemaphoreType.DMA((N_DEV - 1,)),
                   pltpu.HBM(src.shape, src.dtype), pltpu.HBM(land.shape, src.dtype),
                   jax.ShapeDtypeStruct((8, 128), F32)),
        in_specs=(HBM_SPEC, HBM_SPEC, ANY),
        out_specs=(SEM_SPEC, SEM_SPEC, HBM_SPEC, HBM_SPEC, pl.BlockSpec(memory_space=pltpu.VMEM)),
        input_output_aliases={0: 2, 1: 3},
        compiler_params=pltpu.CompilerParams(has_side_effects=SPLIT_EFFECT),
    )(pltpu.with_memory_space_constraint(src, pltpu.HBM), pltpu.with_memory_space_constraint(land, pltpu.HBM), after)


def _peers_wait(started, after, name):
    send_sems, recv_sems, src_thru, land_thru, _ = started

    def body(src_ref, land_ref, send_sems, recv_sems, after_ref, src_dead, got_ref):
        x, y, c = _coords()
        for k in range(1, N_DEV):
            px, py, pc = _peer(k, x, y, c)
            copy = pltpu.make_async_remote_copy(
                src_ref=src_ref, dst_ref=land_ref.at[4 * px + 2 * py + pc],
                send_sem=send_sems.at[k - 1], recv_sem=recv_sems.at[k - 1],
                device_id=(px, py, pc), device_id_type=MESH_ID)
            copy.wait_send()
            copy.wait_recv()

    return pl.pallas_call(
        body, name=name,
        out_shape=(pltpu.HBM(src_thru.shape, src_thru.dtype), pltpu.HBM(land_thru.shape, land_thru.dtype)),
        in_specs=(HBM_SPEC, HBM_SPEC, SEM_SPEC, SEM_SPEC, ANY),
        out_specs=(HBM_SPEC, HBM_SPEC), input_output_aliases={0: 0, 1: 1},
        compiler_params=pltpu.CompilerParams(has_side_effects=SPLIT_EFFECT),
    )(src_thru, land_thru, send_sems, recv_sems, after)


def _share_refs(ref, k, x, y, c, sender_c):
    px, py = ([(x, y)] + _other_chips(x, y))[k]
    return ref.at[:, 4 * px + 2 * py + sender_c]


def _share_start(gathered, after, name):
    def body(g_ref, after_ref, send_sems, recv_sems, g_thru, token):
        x, y, c = _coords()
        for k in range(4):
            slot = _share_refs(g_ref, k, x, y, c, c)
            pltpu.make_async_remote_copy(
                src_ref=slot, dst_ref=slot, send_sem=send_sems.at[k], recv_sem=recv_sems.at[k],
                device_id=(x, y, 1 - c), device_id_type=MESH_ID).start()
        token[...] = jnp.zeros_like(token)

    return pl.pallas_call(
        body, name=name,
        out_shape=(pltpu.SemaphoreType.DMA((4,)), pltpu.SemaphoreType.DMA((4,)),
                   pltpu.HBM(gathered.shape, gathered.dtype), jax.ShapeDtypeStruct((8, 128), F32)),
        in_specs=(HBM_SPEC, ANY),
        out_specs=(SEM_SPEC, SEM_SPEC, HBM_SPEC, pl.BlockSpec(memory_space=pltpu.VMEM)),
        input_output_aliases={0: 2},
        compiler_params=pltpu.CompilerParams(has_side_effects=SPLIT_EFFECT),
    )(pltpu.with_memory_space_constraint(gathered, pltpu.HBM), after)


def _share_wait(started, after, name):
    send_sems, recv_sems, g_thru, _ = started

    def body(g_ref, send_sems, recv_sems, after_ref, got_ref):
        x, y, c = _coords()
        for k in range(4):
            copy = pltpu.make_async_remote_copy(
                src_ref=_share_refs(g_ref, k, x, y, c, c), dst_ref=_share_refs(g_ref, k, x, y, c, 1 - c),
                send_sem=send_sems.at[k], recv_sem=recv_sems.at[k],
                device_id=(x, y, 1 - c), device_id_type=MESH_ID)
            copy.wait_send()
            copy.wait_recv()

    return pl.pallas_call(
        body, name=name,
        out_shape=pltpu.HBM(g_thru.shape, g_thru.dtype),
        in_specs=(HBM_SPEC, SEM_SPEC, SEM_SPEC, ANY),
        out_specs=HBM_SPEC, input_output_aliases={0: 0},
        compiler_params=pltpu.CompilerParams(has_side_effects=SPLIT_EFFECT),
    )(g_thru, send_sems, recv_sems, after)


def _core_share(own, gathered, name):
    def body(own_ref, gin_ref, out_ref, stage, send_sems, recv_sems, local_sem):
        x, y, c = _coords()
        sibling = (x, y, 1 - c)
        chips = [(x, y)] + _other_chips(x, y)
        stage_in = pltpu.make_async_copy(own_ref, stage, local_sem)
        stage_in.start()
        sent, arriving = [], []
        for k, (px, py) in enumerate(chips):
            slot = out_ref.at[:, 4 * px + 2 * py + c]
            sent.append(pltpu.make_async_remote_copy(
                src_ref=own_ref if k == 0 else slot, dst_ref=slot,
                send_sem=send_sems.at[k], recv_sem=recv_sems.at[k], device_id=sibling, device_id_type=MESH_ID))
            arriving.append(pltpu.make_async_remote_copy(
                src_ref=own_ref, dst_ref=out_ref.at[:, 4 * px + 2 * py + (1 - c)],
                send_sem=send_sems.at[k], recv_sem=recv_sems.at[k], device_id=sibling, device_id_type=MESH_ID))
        for cp in sent:
            cp.start()
        stage_in.wait()
        stage_out = pltpu.make_async_copy(stage, out_ref.at[:, 4 * x + 2 * y + c], local_sem)
        stage_out.start()
        for cp in arriving:
            cp.wait_recv()
        for cp in sent:
            cp.wait_send()
        stage_out.wait()

    return pl.pallas_call(
        body, name=name,
        out_shape=jax.ShapeDtypeStruct(gathered.shape, own.dtype),
        in_specs=[ANY, ANY], out_specs=ANY, input_output_aliases={1: 0},
        scratch_shapes=[pltpu.VMEM(own.shape, own.dtype), pltpu.SemaphoreType.DMA((4,)),
                        pltpu.SemaphoreType.DMA((4,)), pltpu.SemaphoreType.DMA],
    )(own, gathered)


def _adamw(w, g, m, v):
    m = ADAM_B1 * m + (1.0 - ADAM_B1) * g
    v = ADAM_B2 * v + (1.0 - ADAM_B2) * (g * g)
    m_hat = m / ADAM_C1
    v_hat = v / ADAM_C2
    delta = -ADAM_LR * (m_hat / (jnp.sqrt(v_hat) + ADAM_EPS) + ADAM_WD * w)
    return delta, m, v


def _adam_big(units, chip_idx, tag, token):
    n = len(units)
    r, wd = units[0][2].shape
    tr, tw = _row_tile(r, 1024), 256

    def body(s_ref, tok_ref, *refs):
        for u in range(n):
            p_ref, l_ref, w_ref, m_ref, v_ref = refs[5 * u:5 * u + 5]
            g_ref, d_ref, mo_ref, vo_ref = refs[5 * n + 4 * u:5 * n + 4 * u + 4]
            g = p_ref[0].astype(F32)
            for j in range(3):
                g = g + l_ref[j].astype(F32)
            delta, mn, vn = _adamw(w_ref[...], g, m_ref[...], v_ref[...])
            g_ref[...] = g
            d_ref[...] = delta
            mo_ref[...] = mn
            vo_ref[...] = vn

    row = pl.BlockSpec((tr, tw), lambda i, j, s: (i, j))
    unit_specs = [pl.BlockSpec((1, tr, tw), lambda i, j, s: (s[0], i, j)),
                  pl.BlockSpec((3, tr, tw), lambda i, j, s: (0, i, j)), row, row, row]
    outs = pl.pallas_call(
        body, name="adam_big_" + tag,
        grid_spec=pltpu.PrefetchScalarGridSpec(
            num_scalar_prefetch=1, grid=(r // tr, wd // tw),
            in_specs=[pl.BlockSpec((8, 128), lambda i, j, s: (0, 0))] + unit_specs * n,
            out_specs=[row] * (4 * n)),
        out_shape=[jax.ShapeDtypeStruct((r, wd), F32)] * (4 * n),
        compiler_params=_cparams(),
    )(chip_idx, token, *[a for unit in units for a in unit])
    return [outs[4 * u:4 * u + 4] for u in range(n)]


def _adam_small(parts, w, m, v):
    _, r, wd = parts.shape

    def body(p_ref, w_ref, m_ref, v_ref, g_ref, d_ref, mo_ref, vo_ref):
        g = p_ref[0]
        for k in range(1, N_DEV):
            g = g + p_ref[k]
        delta, mn, vn = _adamw(w_ref[...], g, m_ref[...], v_ref[...])
        g_ref[...] = g
        d_ref[...] = delta
        mo_ref[...] = mn
        vo_ref[...] = vn

    return pl.pallas_call(
        body, name="adam_small",
        out_shape=[jax.ShapeDtypeStruct((r, wd), F32)] * 4,
        compiler_params=_cparams(),
    )(parts, w, m, v)


def _pad_rows(a, rows):
    return jnp.pad(a, ((0, rows - a.shape[0]), (0, 0)))


def _pad_w_in(w):
    cut = Q_RANK + KV_RANK + ROPE
    return jnp.concatenate([w[:, :cut], jnp.zeros((w.shape[0], 64), w.dtype), w[:, cut:]], axis=1)


def _unpad_w_in(w):
    cut = Q_RANK + KV_RANK + ROPE
    return jnp.concatenate([w[:, :cut], w[:, cut + 64:]], axis=1)


def _pack_mid(p):
    parts = [_pad_w_in(p["w_in"][0]), p["w_out"][0], p["w_mq"][0], p["w_mo"][0],
             p["w_mkv"][0].reshape(256, D_MODEL),
             _pad_rows(p["w_q_up"][0].T.reshape(24, D_MODEL), 32),
             p["w_kv_up"][0].reshape(16, D_MODEL)]
    return jnp.concatenate(parts, axis=0)


def _pack_ffn(w_gate, w_up, w_down, name):
    d, rows = w_gate.shape[1:]

    def body(g_ref, u_ref, d_ref, o_ref):
        eye = (lax.broadcasted_iota(jnp.int32, (d, d), 0) == lax.broadcasted_iota(jnp.int32, (d, d), 1)).astype(BF16)
        o_ref[0] = _dot_tn(g_ref[0].astype(BF16), eye).astype(BF16)
        o_ref[1] = _dot_tn(u_ref[0].astype(BF16), eye).astype(BF16)
        o_ref[2] = d_ref[0].astype(BF16)

    return pl.pallas_call(
        body, name=name, out_shape=jax.ShapeDtypeStruct((3, rows, d), BF16), compiler_params=_cparams(),
    )(w_gate, w_up, w_down)


def _pack_segments(p, group):
    if group == "mid":
        return _pack_mid(p)[None].astype(BF16)
    return _pack_ffn(p[group + "_w_gate"], p[group + "_w_up"], p[group + "_w_down"], "pack_" + group)


UNIT_WEIGHT = {"ffn1_g": ("ffn1_w_gate", True), "ffn1_u": ("ffn1_w_up", True), "ffn1_d": ("ffn1_w_down", False),
               "ffn2_g": ("ffn2_w_gate", True), "ffn2_u": ("ffn2_w_up", True), "ffn2_d": ("ffn2_w_down", False)}


def _pack_unit(p, unit):
    if unit == "mid":
        return _pack_mid(p)
    name, transposed = UNIT_WEIGHT[unit]
    return p[name][0].T if transposed else p[name][0]


def _unpack_unit(a, unit):
    if unit != "mid":
        name, transposed = UNIT_WEIGHT[unit]
        return {name: (a.T if transposed else a)[None]}
    seg = lambda n: a[SEG_OFF[n][0]:SEG_OFF[n][0] + SEG_OFF[n][1]]
    return {"w_in": _unpad_w_in(seg("w_in"))[None], "w_out": seg("w_out")[None], "w_mq": seg("w_mq")[None],
            "w_mo": seg("w_mo")[None], "w_mkv": seg("w_mkv").reshape(D_MODEL, 256)[None],
            "w_q_up": seg("w_q")[:24].reshape(96, Q_RANK).T[None],
            "w_kv_up": seg("w_kv").reshape(KV_RANK, 128)[None]}


def _unpack_gathered(full, group):
    if group != "mid":
        return {n: full[k].reshape(-1, D_MODEL) for k, (n, _) in enumerate(GROUP_SEGS[group])}
    full = full[0]
    seg = lambda n: full[:, SEG_OFF[n][0]:SEG_OFF[n][0] + SEG_OFF[n][1]]
    rows = lambda n: seg(n).reshape(-1, D_MODEL)
    wq_t = seg("w_q")[:, :24].reshape(MLA_HEADS, NOPE + ROPE, Q_RANK)
    wq_t = jnp.pad(wq_t, ((0, 0), (0, HEAD_PAD - NOPE - ROPE), (0, 0))).reshape(MLA_HEADS * HEAD_PAD, Q_RANK)
    wkv = seg("w_kv").reshape(N_DEV, KV_RANK, 128).transpose(1, 0, 2).reshape(KV_RANK, N_DEV * 128)
    return {"w_in": rows("w_in"), "w_out": rows("w_out"), "w_mq": rows("w_mq"), "w_mo": rows("w_mo"),
            "w_mkv": seg("w_mkv").reshape(N_DEV, D_MODEL, 256), "w_q": wq_t, "w_kv": wkv}


def _pack_grads(gr):
    blk = lambda a: a.reshape(N_DEV, -1, D_MODEL)
    dwq = gr["w_q"].reshape(MLA_HEADS, HEAD_PAD, Q_RANK)[:, :NOPE + ROPE].reshape(N_DEV, 24, D_MODEL)
    dwq = jnp.pad(dwq, ((0, 0), (0, 8), (0, 0)))
    dwkv = gr["w_kv"].reshape(KV_RANK, N_DEV, 128).transpose(1, 0, 2).reshape(N_DEV, 16, D_MODEL)
    parts = [blk(gr["w_in"]), blk(gr["w_out"]), blk(gr["w_mq"]), blk(gr["w_mo"]),
             gr["w_mkv"].reshape(N_DEV, 256, D_MODEL), dwq, dwkv]
    return jnp.concatenate([a.astype(BF16) for a in parts], axis=1)


def _pack_small(vals):
    parts = []
    for n, r in SMALL_ROWS:
        parts.append(_pad_rows(vals[n].reshape(-1, 128), r) if n in vals else jnp.zeros((r, 128), F32))
    return jnp.concatenate(parts, axis=0)


def _unpack_small(a, shapes):
    out = {}
    for n, shape in shapes.items():
        o = SMALL_OFF[n][0]
        out[n] = a[o:o + int(np.prod(shape)) // 128].reshape(shape)
    return out


BIG_NAMES = ("ffn1_w_gate", "ffn1_w_up", "ffn1_w_down", "w_in", "w_q_up", "w_kv_up", "w_out", "w_mq", "w_mkv",
             "w_mo", "ffn2_w_gate", "ffn2_w_up", "ffn2_w_down")
SMALL_NAMES = ("ffn1_norm", "mix_norm", "q_norm", "kv_norm", "pool_w", "pool_scale", "xattn_norm", "mem_norm",
               "ffn2_norm", "final_norm")
WEIGHT_ORDER = ("ffn1_norm", "ffn1_w_gate", "ffn1_w_up", "ffn1_w_down", "mix_norm", "w_in", "q_norm", "w_q_up",
                "kv_norm", "w_kv_up", "pool_w", "pool_scale", "w_out", "xattn_norm", "mem_norm", "w_mq", "w_mkv",
                "w_mo", "ffn2_norm", "ffn2_w_gate", "ffn2_w_up", "ffn2_w_down", "final_norm")


def _rope_table():
    lane = np.arange(128)
    freqs = (1.0 / (ROPE_BASE ** (np.arange(0, ROPE, 2, dtype=np.float32) / ROPE))).astype(np.float32)
    tab = np.zeros((8, 128), np.float32)
    tab[0] = np.where(lane < ROPE, freqs[lane % (ROPE // 2)], 0.0)
    tab[1] = np.where(lane < ROPE // 2, -1.0, np.where(lane < ROPE, 1.0, 0.0))
    return jnp.asarray(tab)


def kernel(x, mem, positions, ffn1_norm, ffn1_w_gate, ffn1_w_up, ffn1_w_down, mix_norm, w_in, q_norm, w_q_up, kv_norm, w_kv_up, pool_w, pool_scale, w_out, xattn_norm, mem_norm, w_mq, w_mkv, w_mo, ffn2_norm, ffn2_w_gate, ffn2_w_up, ffn2_w_down, final_norm, loss_target, m_ffn1_norm, m_ffn1_w_gate, m_ffn1_w_up, m_ffn1_w_down, m_mix_norm, m_w_in, m_q_norm, m_w_q_up, m_kv_norm, m_w_kv_up, m_pool_w, m_pool_scale, m_w_out, m_xattn_norm, m_mem_norm, m_w_mq, m_w_mkv, m_w_mo, m_ffn2_norm, m_ffn2_w_gate, m_ffn2_w_up, m_ffn2_w_down, m_final_norm, v_ffn1_norm, v_ffn1_w_gate, v_ffn1_w_up, v_ffn1_w_down, v_mix_norm, v_w_in, v_q_norm, v_w_q_up, v_kv_norm, v_w_kv_up, v_pool_w, v_pool_scale, v_w_out, v_xattn_norm, v_mem_norm, v_w_mq, v_w_mkv, v_w_mo, v_ffn2_norm, v_ffn2_w_gate, v_ffn2_w_up, v_ffn2_w_down, v_final_norm):
    wts = dict(ffn1_norm=ffn1_norm, ffn1_w_gate=ffn1_w_gate, ffn1_w_up=ffn1_w_up, ffn1_w_down=ffn1_w_down,
               mix_norm=mix_norm, w_in=w_in, q_norm=q_norm, w_q_up=w_q_up, kv_norm=kv_norm, w_kv_up=w_kv_up,
               pool_w=pool_w, pool_scale=pool_scale, w_out=w_out, xattn_norm=xattn_norm, mem_norm=mem_norm,
               w_mq=w_mq, w_mkv=w_mkv, w_mo=w_mo, ffn2_norm=ffn2_norm, ffn2_w_gate=ffn2_w_gate,
               ffn2_w_up=ffn2_w_up, ffn2_w_down=ffn2_w_down, final_norm=final_norm)
    mom = dict(ffn1_norm=m_ffn1_norm, ffn1_w_gate=m_ffn1_w_gate, ffn1_w_up=m_ffn1_w_up, ffn1_w_down=m_ffn1_w_down,
               mix_norm=m_mix_norm, w_in=m_w_in, q_norm=m_q_norm, w_q_up=m_w_q_up, kv_norm=m_kv_norm,
               w_kv_up=m_w_kv_up, pool_w=m_pool_w, pool_scale=m_pool_scale, w_out=m_w_out, xattn_norm=m_xattn_norm,
               mem_norm=m_mem_norm, w_mq=m_w_mq, w_mkv=m_w_mkv, w_mo=m_w_mo, ffn2_norm=m_ffn2_norm,
               ffn2_w_gate=m_ffn2_w_gate, ffn2_w_up=m_ffn2_w_up, ffn2_w_down=m_ffn2_w_down, final_norm=m_final_norm)
    var = dict(ffn1_norm=v_ffn1_norm, ffn1_w_gate=v_ffn1_w_gate, ffn1_w_up=v_ffn1_w_up, ffn1_w_down=v_ffn1_w_down,
               mix_norm=v_mix_norm, w_in=v_w_in, q_norm=v_q_norm, w_q_up=v_w_q_up, kv_norm=v_kv_norm,
               w_kv_up=v_w_kv_up, pool_w=v_pool_w, pool_scale=v_pool_scale, w_out=v_w_out, xattn_norm=v_xattn_norm,
               mem_norm=v_mem_norm, w_mq=v_w_mq, w_mkv=v_w_mkv, w_mo=v_w_mo, ffn2_norm=v_ffn2_norm,
               ffn2_w_gate=v_ffn2_w_gate, ffn2_w_up=v_ffn2_w_up, ffn2_w_down=v_ffn2_w_down, final_norm=v_final_norm)

    t = x.shape[1]
    xs = x[0]
    mems = mem[0]
    target = loss_target[0]
    pos = positions.reshape(t, 1)
    row = lambda a: a.reshape(1, -1)
    rope_tab = _rope_table()

    cx, cy, cc = _coords()
    chip_idx = (2 * cx + cy).astype(jnp.int32).reshape(1)

    wb = {}
    for grp in ("ffn1", "mid", "ffn2"):
        wb[grp] = _pack_segments(wts, grp)
        if grp == "ffn1":
            near_ffn1 = _near_start(wb["ffn1"], pos, "ag_ffn1_near_start")
    own_ffn1, land_ffn1 = _near_wait(near_ffn1, wb["ffn2"], "ag_ffn1_near_wait")
    far_ffn1 = _far_start(land_ffn1, own_ffn1, "ag_ffn1_far_start")
    land_ffn1 = _far_wait(far_ffn1, wb["mid"], "ag_ffn1_far_wait")
    full_ffn1 = _core_share(own_ffn1, land_ffn1, "ag_ffn1_share")
    fw = _unpack_gathered(full_ffn1, "ffn1")
    ag_mid = _ici_start(wb["mid"], full_ffn1, "ag_mid_start", True)
    g_ffn1, g_mix, g_q, g_kv = row(ffn1_norm), row(mix_norm), row(q_norm), row(kv_norm)
    g_x, g_mem, g_ffn2, g_fin = row(xattn_norm), row(mem_norm), row(ffn2_norm), row(final_norm)
    pool_wb = pool_w[0].astype(BF16)
    pool_sc = row(pool_scale)

    h1, n1, gate1, up1 = _ffn_fwd(xs, g_ffn1, fw["ffn1_g"], fw["ffn1_u"], fw["ffn1_d"], "ffn1_fwd", token=ag_mid[4])
    own_mid, land_mid = _ici_wait(ag_mid, h1, "ag_mid_wait", True)
    full_mid = _core_share(own_mid, land_mid, "ag_mid_share")
    fw.update(_unpack_gathered(full_mid, "mid"))
    ag_ffn2 = _ici_start(wb["ffn2"], full_mid, "ag_ffn2_start", True)
    u, z, qn, kvn, qh, kh, vh = _mix_prep(h1, g_mix, fw["w_in"], g_q, fw["w_q"], g_kv, fw["w_kv"], pos, rope_tab,
                                          token=ag_ffn2[4])
    a, lse = _attn_fwd(qh, kh, vh)
    p = _pool_fwd(z, pool_wb, pool_sc)
    memn, km, vm = _mem_kv(mems, g_mem, fw["w_mkv"])
    own_ffn2, land_ffn2 = _ici_wait(ag_ffn2, a, "ag_ffn2_wait", True)
    land_ffn2 = lax.dynamic_update_slice(land_ffn2, own_ffn2[:, None], (0, 4 * cx + 2 * cy + cc, 0, 0))
    share_ffn2 = _share_start(land_ffn2, a, "ag_ffn2_share_start")
    h2, h3, hn, qm, om = _xattn_fwd(h1, a, p, fw["w_out"], g_x, fw["w_mq"], km, vm, fw["w_mo"], token=share_ffn2[3])
    fw.update(_unpack_gathered(_share_wait(share_ffn2, h3, "ag_ffn2_share_wait"), "ffn2"))
    dh4, n2, gate2, up2, loss_part, dg_fin = _ffn_fwd(h3, g_ffn2, fw["ffn2_g"], fw["ffn2_u"], fw["ffn2_d"],
                                                      "ffn2_fwd", head=(target, g_fin))

    def reduce_start(g8, unit):
        part = _core_reduce(g8, unit)
        return _ici_start(part, g8, "rs_" + unit + "_start", False)

    def by_device(g):
        return g.reshape(N_DEV, -1, D_MODEL)

    rs = {}
    dh3, dgate2, dup2, act2, dg_ffn2 = _ffn_bwd_data(dh4, h3, g_ffn2, gate2, up2, fw["ffn2_g"], fw["ffn2_u"],
                                                     fw["ffn2_d"], "ffn2_bwd")
    rs["ffn2_g"] = reduce_start(by_device(_tn_matmul(dgate2, n2, "ffn2_dwg", tmm=1408, out_dtype=BF16)), "ffn2_g")
    rs["ffn2_u"] = reduce_start(by_device(_tn_matmul(dup2, n2, "ffn2_dwu", tmm=1408, out_dtype=BF16,
                                                     token=rs["ffn2_g"][4])), "ffn2_u")
    rs["ffn2_d"] = reduce_start(by_device(_tn_matmul(act2, dh4, "ffn2_dwd", scale=0.5, tmm=1408, out_dtype=BF16,
                                                     token=rs["ffn2_u"][4])), "ffn2_d")
    dh2, dqm, da, dp, dkm, dvm, dg_x = _xattn_bwd(dh3, h2, qm, g_x, fw["w_mq"], km, vm, fw["w_mo"], fw["w_out"],
                                                  token=rs["ffn2_d"][4])
    gr = {}
    gr["w_mo"] = _tn_matmul(om, dh3, "dw_mo", out_dtype=BF16)
    gr["w_mq"] = _tn_matmul(hn, dqm, "dw_mq", out_dtype=BF16)
    gr["w_out"] = jnp.concatenate([_tn_matmul(a, dh2, "dw_out_a", out_dtype=BF16),
                                   _tn_matmul(p, dh2, "dw_out_p", out_dtype=BF16)], axis=0)
    gr["w_mkv"], dg_mem = _mem_kv_bwd(dkm, dvm, memn, mems, g_mem, fw["w_mkv"])
    dz_pool, d_pool_w, d_pool_sc = _pool_bwd(dp, z, pool_wb, pool_sc)
    dqh, dkh, dvh = _attn_bwd(qh, kh, vh, da, lse, _attn_delta(a, da))
    dh1, dq, dkv, dz, dg_q, dg_kv, dg_mix = _mla_bwd(dqh, dkh, dvh, z, dz_pool, h1, dh2, g_mix, fw["w_in"], g_q,
                                                     fw["w_q"], g_kv, fw["w_kv"], pos, rope_tab)
    gr["w_q"] = _tn_matmul(dq, qn, "dw_q", out_dtype=BF16)
    gr["w_kv"] = _tn_matmul(kvn, dkv, "dw_kv", out_dtype=BF16)
    gr["w_in"] = _tn_matmul(u, dz, "dw_in", out_dtype=BF16)
    g_mid = _pack_grads(gr)
    part_mid = _core_reduce(g_mid, "mid")
    got = {}
    after = part_mid
    for unit in ("ffn2_g", "ffn2_u", "ffn2_d"):
        got[unit] = _ici_wait(rs[unit], after, "rs_" + unit + "_wait", False)
        after = got[unit][1]
    rs["mid"] = _ici_start(part_mid, after, "rs_mid_start", False)
    dx, dgate1, dup1, act1, dg_ffn1 = _ffn_bwd_data(dh1, xs, g_ffn1, gate1, up1, fw["ffn1_g"], fw["ffn1_u"],
                                                    fw["ffn1_d"], "ffn1_bwd", token=rs["mid"][4])
    got["mid"] = _ici_wait(rs["mid"], dx, "rs_mid_wait", False)

    small_g = dict(ffn1_norm=dg_ffn1, mix_norm=dg_mix, q_norm=dg_q, kv_norm=dg_kv, pool_w=d_pool_w,
                   pool_scale=d_pool_sc, xattn_norm=dg_x, mem_norm=dg_mem, ffn2_norm=dg_ffn2, final_norm=dg_fin,
                   loss=loss_part)
    small_ag = _peers_start(_pack_small(small_g), got["mid"][1], "small_ag_start")
    rs["ffn1_g"] = reduce_start(by_device(_tn_matmul(dgate1, n1, "ffn1_dwg", tmm=1408, out_dtype=BF16,
                                                     token=small_ag[4])), "ffn1_g")
    _, parts = _peers_wait(small_ag, rs["ffn1_g"][4], "small_ag_wait")
    small = _adam_small(parts, _pack_small({n: wts[n] for n in SMALL_NAMES}),
                        _pack_small({n: mom[n] for n in SMALL_NAMES}), _pack_small({n: var[n] for n in SMALL_NAMES}))
    small_sum = small[0]
    loss = small_sum[SMALL_OFF["loss"][0], 0]
    shapes = {n: wts[n].shape for n in SMALL_NAMES}
    small = [_unpack_small(s, shapes) for s in small]

    rs["ffn1_u"] = reduce_start(by_device(_tn_matmul(dup1, n1, "ffn1_dwu", tmm=1408, out_dtype=BF16,
                                                     token=small_sum)), "ffn1_u")
    rs["ffn1_d"] = reduce_start(by_device(_tn_matmul(act1, dh1, "ffn1_dwd", scale=0.5, tmm=1408, out_dtype=BF16,
                                                     token=rs["ffn1_u"][4])), "ffn1_d")

    big = {}

    def adam_units(names, token):
        units = [got[u] + (_pack_unit(wts, u), _pack_unit(mom, u), _pack_unit(var, u)) for u in names]
        res = _adam_big(units, chip_idx, "_".join(names), token)
        for u, four in zip(names, res):
            for k, packed in enumerate(four):
                big.setdefault(k, {}).update(_unpack_unit(packed, u))
        return res[-1][0]

    done = adam_units(["mid"], rs["ffn1_d"][4])
    done = adam_units(["ffn2_g", "ffn2_u", "ffn2_d"], done)
    got["ffn1_g"] = _ici_wait(rs["ffn1_g"], done, "rs_ffn1_g_wait", False)
    got["ffn1_u"] = _ici_wait(rs["ffn1_u"], got["ffn1_g"][1], "rs_ffn1_u_wait", False)
    done = adam_units(["ffn1_g", "ffn1_u"], done)
    got["ffn1_d"] = _ici_wait(rs["ffn1_d"], done, "rs_ffn1_d_wait", False)
    adam_units(["ffn1_d"], done)

    outs = [loss, dx[None]]
    for k in range(4):
        for n in WEIGHT_ORDER:
            outs.append(big[k][n] if n in BIG_NAMES else small[k][n])
    return tuple(outs)
```

```python
import numpy as np

import jax
import jax.numpy as jnp
from jax import lax
from jax.experimental import pallas as pl
from jax.experimental.pallas import tpu as pltpu

F32 = jnp.float32
BF16 = jnp.bfloat16

N_DEV = 8
D_MODEL = 1024
D_FF = 2816
MLA_HEADS = 4
NOPE = 128
ROPE = 64
HEAD_PAD = 256
V_DIM = 128
Q_RANK = 256
KV_RANK = 128
POOL_WINDOWS = (2, 4, 8, 16)
POOL_CH = 128
POOL_HALO = 16
N_MEM = 256
MEM_HEADS = 4
MEM_HD = 256
ROPE_BASE = 10000.0
RMS_EPS = 1e-6
ATTN_SCALE = (NOPE + ROPE) ** -0.5
MEM_SCALE = MEM_HD ** -0.5
NEG_BIG = -1e30

ADAM_LR = 0.001
ADAM_B1 = 0.9
ADAM_B2 = 0.999
ADAM_EPS = 1e-08
ADAM_WD = 0.01
ADAM_STEP = 10
ADAM_C1 = 1.0 - ADAM_B1 ** ADAM_STEP
ADAM_C2 = 1.0 - ADAM_B2 ** ADAM_STEP

VMEM_LIMIT_BYTES = 56 * 1024 * 1024
BF16_ROWS = 16

GROUP_SEGS = {
    "ffn1": (("ffn1_g", 352), ("ffn1_u", 352), ("ffn1_d", 352)),
    "mid": (("w_in", 128), ("w_out", 128), ("w_mq", 128), ("w_mo", 128), ("w_mkv", 256), ("w_q", 32), ("w_kv", 16)),
    "ffn2": (("ffn2_g", 352), ("ffn2_u", 352), ("ffn2_d", 352)),
}
SEG_OFF = {}
GROUP_ROWS = {}
for _g, _segs in GROUP_SEGS.items():
    _o = 0
    for _n, _r in _segs:
        SEG_OFF[_n] = (_o, _r)
        _o += _r
    GROUP_ROWS[_g] = _o

SMALL_ROWS = (("ffn1_norm", 8), ("mix_norm", 8), ("q_norm", 8), ("kv_norm", 8), ("pool_w", 512), ("pool_scale", 8),
              ("xattn_norm", 8), ("mem_norm", 8), ("ffn2_norm", 8), ("final_norm", 8), ("loss", 8))
SMALL_OFF = {}
_o = 0
for _n, _r in SMALL_ROWS:
    SMALL_OFF[_n] = (_o, _r)
    _o += _r


def _cparams(**kw):
    return pltpu.CompilerParams(vmem_limit_bytes=VMEM_LIMIT_BYTES, **kw)


def _row_tile(rows, limit):
    best = None
    for cand in range(BF16_ROWS, min(rows, limit) + 1, BF16_ROWS):
        if rows % cand == 0:
            best = cand
    assert best is not None, rows
    return best


def _dot_nn(a, b):
    return lax.dot_general(a, b, (((1,), (0,)), ((), ())), preferred_element_type=F32)


def _dot_nt(a, b):
    return lax.dot_general(a, b, (((1,), (1,)), ((), ())), preferred_element_type=F32)


def _dot_tn(a, b):
    return lax.dot_general(a, b, (((0,), (0,)), ((), ())), preferred_element_type=F32)


def _rms_fwd(x, g):
    r = lax.rsqrt(jnp.mean(x * x, axis=-1, keepdims=True) + RMS_EPS)
    return x * r * g, r


def _rms_bwd(dy, x, g, r):
    xhat = x * r
    dyg = dy * g
    dx = r * (dyg - xhat * jnp.mean(dyg * xhat, axis=-1, keepdims=True))
    dg = jnp.sum(dy * xhat, axis=0, keepdims=True)
    return dx, dg


def _accumulate(ref, val, first):
    if isinstance(first, bool):
        if first:
            ref[...] = val
        else:
            ref[...] += val
        return

    @pl.when(first)
    def _():
        ref[...] = val

    @pl.when(jnp.logical_not(first))
    def _():
        ref[...] += val


def _call_after(token, body, in_specs, args, **kw):
    if token is not None:
        inner = body
        body = lambda tok_ref, *refs: inner(*refs)
        in_specs = [pl.BlockSpec((8, 128), lambda *_: (0, 0))] + list(in_specs)
        args = (token,) + tuple(args)
    return pl.pallas_call(body, in_specs=in_specs, **kw)(*args)


def _resident(shape):
    return pl.BlockSpec(shape, lambda *_: (0,) * len(shape), pipeline_mode=pl.Buffered(1))


def _rope_tables(pos_col, tab):
    ang = pos_col.astype(F32) * tab[0:1, :]
    return jnp.cos(ang), jnp.sin(ang) * tab[1:2, :]


def _swap_halves(x):
    lane = lax.broadcasted_iota(jnp.int32, x.shape, 1)
    return jnp.where((lane % 64) < 32, pltpu.roll(x, 96, 1), pltpu.roll(x, 32, 1))


def _rope_apply(x, cos_t, sin_t):
    return x * cos_t + _swap_halves(x) * sin_t


def _rope_apply_t(dy, cos_t, sin_t):
    return dy * cos_t + _swap_halves(dy * sin_t)


def _ffn_fwd(h, g, wg_t, wu_t, wd, name, token=None, head=None):
    t, d = h.shape
    f = wg_t.shape[0]
    tm, tf = min(512, t), 256
    nf = f // tf
    n_in = 5 if head is None else 7

    def body(*refs):
        h_ref, g_ref, wg_ref, wu_ref, wd_ref = refs[:5]
        ho_ref, n_ref, gate_ref, up_ref = refs[n_in:n_in + 4]
        nb_sc, acc_sc = refs[-2:]
        y, _ = _rms_fwd(h_ref[...], g_ref[...])
        nb = y.astype(BF16)
        nb_sc[...] = nb
        n_ref[...] = nb
        acc_sc[...] = jnp.zeros_like(acc_sc)

        def f_tile(j):
            rows = pl.ds(pl.multiple_of(j * tf, tf), tf)
            nb = nb_sc[...]
            gt = _dot_nt(nb, wg_ref[rows, :])
            ut = _dot_nt(nb, wu_ref[rows, :])
            gate_ref[j] = gt.astype(BF16)
            up_ref[j] = ut.astype(BF16)
            act = (gt * jax.nn.sigmoid(gt)) * ut
            return _dot_nn(act.astype(BF16), wd_ref[rows, :])

        def pair(p, carry):
            acc_sc[...] += f_tile(2 * p) + f_tile(2 * p + 1)
            return carry

        lax.fori_loop(0, nf // 2, pair, 0)
        if nf % 2:
            acc_sc[...] += f_tile(nf - 1)
        ho = h_ref[...] + 0.5 * acc_sc[...]
        if head is None:
            ho_ref[...] = ho
            return
        t_ref, gf_ref = refs[5:7]
        loss_ref, dgf_ref = refs[n_in + 4:n_in + 6]
        gg = gf_ref[...]
        y, r = _rms_fwd(ho, gg)
        err = y - t_ref[...]
        part = 0.5 * jnp.sum(jnp.mean(err * err, axis=-1, keepdims=True), axis=0, keepdims=True)
        dx, dg = _rms_bwd(err * (1.0 / d), ho, gg, r)
        ho_ref[...] = dx
        first = pl.program_id(0) == 0
        _accumulate(loss_ref, jnp.broadcast_to(part, loss_ref.shape), first)
        _accumulate(dgf_ref, dg, first)

    row = pl.BlockSpec((tm, d), lambda i: (i, 0))
    tiles = pl.BlockSpec((nf, tm, tf), lambda i: (0, i, 0))
    in_specs = [row, _resident((1, d)), _resident((f, d)), _resident((f, d)), _resident((f, d))]
    args = (h, g, wg_t, wu_t, wd)
    out_specs = [row, row, tiles, tiles]
    out_shape = [jax.ShapeDtypeStruct((t, d), F32), jax.ShapeDtypeStruct((t, d), BF16),
                 jax.ShapeDtypeStruct((nf, t, tf), BF16), jax.ShapeDtypeStruct((nf, t, tf), BF16)]
    if head is not None:
        in_specs += [row, _resident((1, d))]
        args += tuple(head)
        out_specs += [pl.BlockSpec((8, 128), lambda i: (0, 0)), pl.BlockSpec((1, d), lambda i: (0, 0))]
        out_shape += [jax.ShapeDtypeStruct((8, 128), F32), jax.ShapeDtypeStruct((1, d), F32)]
    return _call_after(
        token, body, in_specs, args, name=name, grid=(t // tm,), out_specs=out_specs, out_shape=out_shape,
        scratch_shapes=[pltpu.VMEM((tm, d), BF16), pltpu.VMEM((tm, d), F32)],
        compiler_params=_cparams(),
    )


def _ffn_bwd_data(dho, h, g, gate, up, wg_t, wu_t, wd, name, token=None):
    t, d = h.shape
    f = wg_t.shape[0]
    tm, tf = min(512, t), 256
    nf = f // tf
    npair, odd = nf // 2, nf % 2
    nsteps = npair + odd

    def body(dho_ref, h_ref, g_ref, gate2, up2, wg2, wu2, wd2, gate1, up1, wg1, wu1, wd1,
             dh_ref, dgate_ref, dup_ref, act_ref, dg_ref, dhb_sc, acc_sc):
        i, j = pl.program_id(0), pl.program_id(1)

        @pl.when(j == 0)
        def _():
            dhb_sc[...] = (0.5 * dho_ref[...]).astype(BF16)
            acc_sc[...] = jnp.zeros_like(acc_sc)

        def tiles(gt, ut, wg, wu, wd_, width):
            cols = slice(0, width)
            dact = _dot_nt(dhb_sc[...], wd_)
            sg = jax.nn.sigmoid(gt)
            silu = gt * sg
            dgb = (dact * ut * (sg * (1.0 + gt * (1.0 - sg)))).astype(BF16)
            dub = (dact * silu).astype(BF16)
            act_ref[:, cols] = (silu * ut).astype(BF16)
            dgate_ref[:, cols] = dgb
            dup_ref[:, cols] = dub
            return _dot_nn(dgb, wg) + _dot_nn(dub, wu)

        @pl.when(j < npair)
        def _():
            gt = jnp.concatenate([gate2[0], gate2[1]], axis=-1).astype(F32)
            ut = jnp.concatenate([up2[0], up2[1]], axis=-1).astype(F32)
            acc_sc[...] += tiles(gt, ut, wg2[...], wu2[...], wd2[...], 2 * tf)

        if odd:
            @pl.when(j == npair)
            def _():
                acc_sc[...] += tiles(gate1[0].astype(F32), up1[0].astype(F32), wg1[...], wu1[...], wd1[...], tf)

        @pl.when(j == nsteps - 1)
        def _():
            x = h_ref[...]
            gg = g_ref[...]
            _, r = _rms_fwd(x, gg)
            dx, dg = _rms_bwd(acc_sc[...], x, gg, r)
            dh_ref[...] = dho_ref[...] + dx
            _accumulate(dg_ref, dg, i == 0)

    last_pair = max(npair - 1, 0)
    pair = lambda j: jnp.minimum(j, last_pair)
    row = pl.BlockSpec((tm, d), lambda i, j: (i, 0))
    act2 = pl.BlockSpec((2, tm, tf), lambda i, j: (pair(j), i, 0))
    w2 = pl.BlockSpec((2 * tf, d), lambda i, j: (pair(j), 0))
    act1 = pl.BlockSpec((1, tm, tf), lambda i, j: (nf - 1, i, 0))
    w1 = pl.BlockSpec((tf, d), lambda i, j: (nf - 1, 0))
    out2 = pl.BlockSpec((tm, 2 * tf), lambda i, j: (i, j))
    padded = jax.ShapeDtypeStruct((t, 2 * tf * nsteps), BF16)
    return _call_after(
        token, body,
        [row, row, pl.BlockSpec((1, d), lambda i, j: (0, 0)), act2, act2, w2, w2, w2, act1, act1, w1, w1, w1],
        (dho, h, g, gate, up, wg_t, wu_t, wd, gate, up, wg_t, wu_t, wd),
        name=name, grid=(t // tm, nsteps),
        out_specs=[row, out2, out2, out2, pl.BlockSpec((1, d), lambda i, j: (0, 0))],
        out_shape=[jax.ShapeDtypeStruct((t, d), F32), padded, padded, padded, jax.ShapeDtypeStruct((1, d), F32)],
        scratch_shapes=[pltpu.VMEM((tm, d), BF16), pltpu.VMEM((tm, d), F32)],
        compiler_params=_cparams(),
    )


def _tn_matmul(a, b, name, scale=1.0, tmm=None, out_dtype=F32, token=None, m=None):
    t = a.shape[0]
    m = a.shape[1] if m is None else m
    n = b.shape[1]
    tmm = m if tmm is None else tmm
    tk = min(1024, t)
    nk = t // tk

    def product(a_ref, b_ref):
        prod = _dot_tn(a_ref[...].astype(BF16), b_ref[...].astype(BF16))
        return prod * scale if scale != 1.0 else prod

    def body_f32(a_ref, b_ref, o_ref):
        _accumulate(o_ref, product(a_ref, b_ref), pl.program_id(1) == 0)

    def body_cast(a_ref, b_ref, o_ref, acc_sc):
        k = pl.program_id(1)
        _accumulate(acc_sc, product(a_ref, b_ref), k == 0)

        @pl.when(k == nk - 1)
        def _():
            o_ref[...] = acc_sc[...].astype(out_dtype)

    direct = out_dtype == F32
    return _call_after(
        token, body_f32 if direct else body_cast,
        [pl.BlockSpec((tk, tmm), lambda i, k: (k, i)),
         pl.BlockSpec((tk, n), lambda i, k: (k, 0))],
        (a, b),
        name=name, grid=(m // tmm, nk),
        out_specs=pl.BlockSpec((tmm, n), lambda i, k: (i, 0)),
        out_shape=jax.ShapeDtypeStruct((m, n), out_dtype),
        scratch_shapes=[] if direct else [pltpu.VMEM((tmm, n), F32)],
        compiler_params=_cparams(),
    )


def _mix_prep(h1, mix_norm, w_in, q_norm, wq_t, kv_norm, wkv, pos, rope_tab, token=None):
    t, d = h1.shape
    tm = min(512, t)

    def body(h_ref, gm_ref, win_ref, gq_ref, wq_ref, gkv_ref, wkv_ref, pos_ref, tab_ref,
             u_ref, z_ref, qn_ref, kvn_ref, q_ref, k_ref, v_ref):
        u, _ = _rms_fwd(h_ref[...], gm_ref[...])
        ub = u.astype(BF16)
        u_ref[...] = ub
        z = _dot_nn(ub, win_ref[...])
        z_ref[...] = z
        cos_t, sin_t = _rope_tables(pos_ref[...], tab_ref[...])
        qn, _ = _rms_fwd(z[:, 0:Q_RANK], gq_ref[...])
        qnb = qn.astype(BF16)
        qn_ref[...] = qnb
        q = _dot_nt(qnb, wq_ref[...])
        kvn, _ = _rms_fwd(z[:, Q_RANK:Q_RANK + KV_RANK], gkv_ref[...])
        kvnb = kvn.astype(BF16)
        kvn_ref[...] = kvnb
        kv = _dot_nn(kvnb, wkv_ref[...])
        k_pe = _rope_apply(z[:, Q_RANK + KV_RANK:Q_RANK + KV_RANK + 128], cos_t, sin_t)
        ones = jnp.ones((tm, V_DIM), F32)
        for hh in range(MLA_HEADS):
            b = hh * HEAD_PAD
            q_pe = _rope_apply(q[:, b + NOPE:b + HEAD_PAD], cos_t, sin_t)
            q_ref[hh] = jnp.concatenate([q[:, b:b + NOPE], q_pe], axis=-1).astype(BF16)
            k_ref[hh] = jnp.concatenate([kv[:, b:b + NOPE], k_pe], axis=-1).astype(BF16)
            v_ref[hh] = jnp.concatenate([kv[:, b + NOPE:b + HEAD_PAD], ones], axis=-1).astype(BF16)

    full = lambda shape: pl.BlockSpec(shape, lambda i: (0,) * len(shape))
    return _call_after(
        token, body,
        [pl.BlockSpec((tm, d), lambda i: (i, 0)), _resident((1, d)), _resident(w_in.shape), _resident((1, Q_RANK)),
         _resident(wq_t.shape), _resident((1, KV_RANK)), _resident(wkv.shape),
         pl.BlockSpec((tm, 1), lambda i: (i, 0)), _resident(rope_tab.shape)],
        (h1, mix_norm, w_in, q_norm, wq_t, kv_norm, wkv, pos, rope_tab),
        name="mix_prep", grid=(t // tm,),
        out_specs=[pl.BlockSpec((tm, d), lambda i: (i, 0)),
                   pl.BlockSpec((tm, d), lambda i: (i, 0)),
                   pl.BlockSpec((tm, Q_RANK), lambda i: (i, 0)),
                   pl.BlockSpec((tm, KV_RANK), lambda i: (i, 0)),
                   pl.BlockSpec((MLA_HEADS, tm, HEAD_PAD), lambda i: (0, i, 0)),
                   pl.BlockSpec((MLA_HEADS, tm, HEAD_PAD), lambda i: (0, i, 0)),
                   pl.BlockSpec((MLA_HEADS, tm, 2 * V_DIM), lambda i: (0, i, 0))],
        out_shape=[jax.ShapeDtypeStruct((t, d), BF16), jax.ShapeDtypeStruct((t, d), F32),
                   jax.ShapeDtypeStruct((t, Q_RANK), BF16), jax.ShapeDtypeStruct((t, KV_RANK), BF16),
                   jax.ShapeDtypeStruct((MLA_HEADS, t, HEAD_PAD), BF16),
                   jax.ShapeDtypeStruct((MLA_HEADS, t, HEAD_PAD), BF16),
                   jax.ShapeDtypeStruct((MLA_HEADS, t, 2 * V_DIM), BF16)],
        compiler_params=_cparams(),
    )


def _causal_mask(s):
    row = lax.broadcasted_iota(jnp.int32, s.shape, 0)
    col = lax.broadcasted_iota(jnp.int32, s.shape, 1)
    return jnp.where(col <= row, s, NEG_BIG)


def _attn_fwd(q, k, v):
    nh, t, _ = q.shape
    tq = tk = min(512, t)
    nq, nk = t // tq, t // tk

    pairs = [(i, j) for i in range(nq) for j in range(i + 1)]
    qi = jnp.asarray(np.array([i for i, _ in pairs], np.int32))
    kj = jnp.asarray(np.array([j for _, j in pairs], np.int32))

    def body(qi_ref, kj_ref, q_ref, k_ref, v_ref, o_ref, lse_ref, m_sc, acc_sc):
        n = pl.program_id(0)
        i, j = qi_ref[n], kj_ref[n]

        @pl.when(j == 0)
        def _():
            m_sc[...] = jnp.full_like(m_sc, NEG_BIG)
            acc_sc[...] = jnp.zeros_like(acc_sc)

        def step(diagonal):
            for hh in range(nh):
                s = _dot_nt(q_ref[hh], k_ref[hh]) * ATTN_SCALE
                if diagonal:
                    s = _causal_mask(s)
                m_old = m_sc[hh]
                m_new = jnp.maximum(m_old, jnp.max(s, axis=-1, keepdims=True))
                p = jnp.exp(s - m_new).astype(BF16)
                acc_sc[hh] = jnp.exp(m_old - m_new) * acc_sc[hh] + _dot_nn(p, v_ref[hh])
                m_sc[hh] = m_new

        @pl.when(j < i)
        def _():
            step(False)

        @pl.when(j == i)
        def _():
            step(True)
            for hh in range(nh):
                acc = acc_sc[hh]
                l = acc[:, V_DIM:2 * V_DIM]
                o_ref[:, hh * V_DIM:(hh + 1) * V_DIM] = (acc[:, 0:V_DIM] / l).astype(BF16)
                lse_ref[hh] = m_sc[hh] + jnp.log(l[:, 0:1])

    q_map = lambda n, qi_ref, kj_ref: (0, qi_ref[n], 0)
    kv_map = lambda n, qi_ref, kj_ref: (0, kj_ref[n], 0)
    return pl.pallas_call(
        body, name="attn_fwd",
        grid_spec=pltpu.PrefetchScalarGridSpec(
            num_scalar_prefetch=2, grid=(len(pairs),),
            in_specs=[pl.BlockSpec((nh, tq, HEAD_PAD), q_map),
                      pl.BlockSpec((nh, tk, HEAD_PAD), kv_map),
                      pl.BlockSpec((nh, tk, 2 * V_DIM), kv_map)],
            out_specs=[pl.BlockSpec((tq, nh * V_DIM), lambda n, qi_ref, kj_ref: (qi_ref[n], 0)),
                       pl.BlockSpec((nh, tq, 1), q_map)],
            scratch_shapes=[pltpu.VMEM((nh, tq, 1), F32), pltpu.VMEM((nh, tq, 2 * V_DIM), F32)]),
        out_shape=[jax.ShapeDtypeStruct((t, nh * V_DIM), BF16), jax.ShapeDtypeStruct((nh, t, 1), F32)],
        compiler_params=_cparams(),
    )(qi, kj, q, k, v)


def _attn_delta(o, do):
    t, w = o.shape
    nh = w // V_DIM
    tm = min(512, t)

    def body(o_ref, do_ref, d_ref):
        prod = o_ref[...].astype(F32) * do_ref[...].astype(F32)
        for hh in range(nh):
            d_ref[hh] = jnp.sum(prod[:, hh * V_DIM:(hh + 1) * V_DIM], axis=-1, keepdims=True)

    return pl.pallas_call(
        body, name="attn_delta", grid=(t // tm,),
        in_specs=[pl.BlockSpec((tm, w), lambda i: (i, 0)), pl.BlockSpec((tm, w), lambda i: (i, 0))],
        out_specs=pl.BlockSpec((nh, tm, 1), lambda i: (0, i, 0)),
        out_shape=jax.ShapeDtypeStruct((nh, t, 1), F32),
        compiler_params=_cparams(),
    )(o, do)


ATTN_BWD_HEADS = 2


def _attn_bwd(q, k, v, do, lse, delta):
    nh, t, _ = q.shape
    hp = ATTN_BWD_HEADS
    tq = tk = min(512, t)
    nq, nk = t // tq, t // tk

    pairs = [(j, i) for j in range(nk) for i in range(j, nq)]
    kj = jnp.asarray(np.array([j for j, _ in pairs], np.int32))
    qi = jnp.asarray(np.array([i for _, i in pairs], np.int32))

    def body(kj_ref, qi_ref, q_ref, k_ref, v_ref, do_ref, lse_ref, dlt_ref, dq_ref, dk_ref, dv_ref):
        n = pl.program_id(1)
        j, i = kj_ref[n], qi_ref[n]

        @pl.when(n == 0)
        def _():
            dq_ref[...] = jnp.zeros_like(dq_ref)

        def step(diagonal):
            for hh in range(hp):
                qq, kk = q_ref[hh], k_ref[hh]
                dob = do_ref[:, hh * V_DIM:(hh + 1) * V_DIM]
                s = _dot_nt(qq, kk) * ATTN_SCALE
                if diagonal:
                    s = _causal_mask(s)
                p = jnp.exp(s - lse_ref[hh])
                dpp = _dot_nt(dob, v_ref[hh])
                dsb = (p * (dpp - dlt_ref[hh]) * ATTN_SCALE).astype(BF16)
                _accumulate(dv_ref.at[hh], _dot_tn(p.astype(BF16), dob), diagonal)
                _accumulate(dk_ref.at[hh], _dot_tn(dsb, qq), diagonal)
                dq_ref[hh, pl.ds(pl.multiple_of(i * tq, tq), tq), :] += _dot_nn(dsb, kk)

        @pl.when(i > j)
        def _():
            step(False)

        @pl.when(i == j)
        def _():
            step(True)

    q_map = lambda h, n, kj_ref, qi_ref: (h, qi_ref[n], 0)
    k_map = lambda h, n, kj_ref, qi_ref: (h, kj_ref[n], 0)
    return pl.pallas_call(
        body, name="attn_bwd",
        grid_spec=pltpu.PrefetchScalarGridSpec(
            num_scalar_prefetch=2, grid=(nh // hp, len(pairs)),
            in_specs=[pl.BlockSpec((hp, tq, HEAD_PAD), q_map),
                      pl.BlockSpec((hp, tk, HEAD_PAD), k_map),
                      pl.BlockSpec((hp, tk, V_DIM), k_map),
                      pl.BlockSpec((tq, hp * V_DIM), lambda h, n, kj_ref, qi_ref: (qi_ref[n], h)),
                      pl.BlockSpec((hp, tq, 1), q_map),
                      pl.BlockSpec((hp, tq, 1), q_map)],
            out_specs=[pl.BlockSpec((hp, t, HEAD_PAD), lambda h, n, kj_ref, qi_ref: (h, 0, 0)),
                       pl.BlockSpec((hp, tk, HEAD_PAD), k_map),
                       pl.BlockSpec((hp, tk, V_DIM), k_map)]),
        out_shape=[jax.ShapeDtypeStruct((nh, t, HEAD_PAD), F32), jax.ShapeDtypeStruct((nh, t, HEAD_PAD), F32),
                   jax.ShapeDtypeStruct((nh, t, V_DIM), F32)],
        compiler_params=_cparams(),
    )(kj, qi, q, k, v, do, lse, delta)


def _pool_counts(first_token, rows, w):
    tok = lax.broadcasted_iota(jnp.int32, (rows, POOL_CH), 0) + first_token
    return jnp.minimum(tok + 1, w).astype(F32)


def _pool_centered(zbuf, g, w, i, tm):
    lanes = pl.ds(g * POOL_CH, POOL_CH)
    cur = zbuf[pl.ds(POOL_HALO, tm), lanes]
    win = cur
    for s in range(1, w):
        win = win + zbuf[pl.ds(POOL_HALO - s, tm), lanes]
    return win / _pool_counts(i * tm, tm, w) - cur


def _pool_load(zbuf, z_ref, halo_ref, i, tm):
    @pl.when(i == 0)
    def _():
        zbuf[pl.ds(0, POOL_HALO), :] = jnp.zeros((POOL_HALO, zbuf.shape[1]), F32)

    @pl.when(i > 0)
    def _():
        zbuf[pl.ds(0, POOL_HALO), :] = halo_ref[...]

    zbuf[pl.ds(POOL_HALO, tm), :] = z_ref[...]


def _pool_fwd(z, pool_w, pool_scale):
    t = z.shape[0]
    pw = len(POOL_WINDOWS) * POOL_CH
    tm = min(512, t)
    hb = tm // POOL_HALO

    def body(z_ref, halo_ref, w_ref, sc_ref, p_ref, zbuf):
        i = pl.program_id(0)
        _pool_load(zbuf, z_ref, halo_ref, i, tm)
        for g, w in enumerate(POOL_WINDOWS):
            c = _pool_centered(zbuf, g, w, i, tm)
            y = _dot_nn(c.astype(BF16), w_ref[g]) * sc_ref[:, g * POOL_CH:(g + 1) * POOL_CH]
            p_ref[:, g * POOL_CH:(g + 1) * POOL_CH] = y.astype(BF16)

    return pl.pallas_call(
        body, name="pool_fwd", grid=(t // tm,),
        in_specs=[pl.BlockSpec((tm, pw), lambda i: (i, 1)),
                  pl.BlockSpec((POOL_HALO, pw), lambda i: (jnp.maximum(i * hb - 1, 0), 1)),
                  pl.BlockSpec(pool_w.shape, lambda i: (0, 0, 0)),
                  pl.BlockSpec((1, pw), lambda i: (0, 0))],
        out_specs=pl.BlockSpec((tm, pw), lambda i: (i, 0)),
        out_shape=jax.ShapeDtypeStruct((t, pw), BF16),
        scratch_shapes=[pltpu.VMEM((POOL_HALO + tm, pw), F32)],
        compiler_params=_cparams(),
    )(z, z, pool_w, pool_scale)


def _pool_bwd(dp, z, pool_w, pool_scale):
    t = z.shape[0]
    ng = len(POOL_WINDOWS)
    pw = ng * POOL_CH
    tm = min(512, t)
    hb = tm // POOL_HALO
    nt = t // tm

    def body(dp_ref, dpn_ref, z_ref, halo_ref, w_ref, sc_ref, dz_ref, dw_ref, dsc_ref, zbuf, dbuf):
        i = pl.program_id(0)
        _pool_load(zbuf, z_ref, halo_ref, i, tm)

        @pl.when(i == 0)
        def _():
            dw_ref[...] = jnp.zeros_like(dw_ref)
            dsc_ref[...] = jnp.zeros_like(dsc_ref)

        nxt_ok = (i < nt - 1).astype(F32)
        for g, w in enumerate(POOL_WINDOWS):
            lanes = pl.ds(g * POOL_CH, POOL_CH)
            cols = slice(g * POOL_CH, (g + 1) * POOL_CH)
            sc = sc_ref[:, cols]
            wg = w_ref[g]
            c = _pool_centered(zbuf, g, w, i, tm).astype(BF16)
            ypre = _dot_nn(c, wg)
            dpg = dp_ref[:, cols].astype(F32)
            dsc_ref[:, cols] += jnp.sum(dpg * ypre, axis=0, keepdims=True)
            dyb = (dpg * sc).astype(BF16)
            dw_ref[g] += _dot_tn(c, dyb)
            dd = _dot_nt(dyb, wg)
            dyn = (dpn_ref[:, cols].astype(F32) * sc).astype(BF16)
            ddn = _dot_nt(dyn, wg) * nxt_ok
            dbuf[pl.ds(0, tm), lanes] = dd / _pool_counts(i * tm, tm, w)
            dbuf[pl.ds(tm, POOL_HALO), lanes] = ddn / _pool_counts((i + 1) * tm, POOL_HALO, w)
            acc = -dd
            for s in range(w):
                acc = acc + dbuf[pl.ds(s, tm), lanes]
            dz_ref[:, cols] = acc

    return pl.pallas_call(
        body, name="pool_bwd", grid=(nt,),
        in_specs=[pl.BlockSpec((tm, pw), lambda i: (i, 0)),
                  pl.BlockSpec((POOL_HALO, pw), lambda i: (jnp.minimum((i + 1) * hb, t // POOL_HALO - 1), 0)),
                  pl.BlockSpec((tm, pw), lambda i: (i, 1)),
                  pl.BlockSpec((POOL_HALO, pw), lambda i: (jnp.maximum(i * hb - 1, 0), 1)),
                  pl.BlockSpec(pool_w.shape, lambda i: (0, 0, 0)),
                  pl.BlockSpec((1, pw), lambda i: (0, 0))],
        out_specs=[pl.BlockSpec((tm, pw), lambda i: (i, 0)),
                   pl.BlockSpec((ng, POOL_CH, POOL_CH), lambda i: (0, 0, 0)),
                   pl.BlockSpec((1, pw), lambda i: (0, 0))],
        out_shape=[jax.ShapeDtypeStruct((t, pw), F32), jax.ShapeDtypeStruct((ng, POOL_CH, POOL_CH), F32),
                   jax.ShapeDtypeStruct((1, pw), F32)],
        scratch_shapes=[pltpu.VMEM((POOL_HALO + tm, pw), F32), pltpu.VMEM((tm + POOL_HALO, pw), F32)],
        compiler_params=_cparams(),
    )(dp, dp, z, z, pool_w, pool_scale)


def _mla_bwd(dq_h, dk_h, dv_h, z, dz_pool, h1, dh2, mix_norm, w_in, q_norm, wq_t, kv_norm, wkv, pos, rope_tab):
    t, d = h1.shape
    tm = min(512, t)

    def body(dqh_ref, dkh_ref, dvh_ref, z_ref, dzp_ref, h_ref, dh2_ref, gm_ref, win_ref, gq_ref, wq_ref, gkv_ref,
             wkv_ref, pos_ref, tab_ref, dh1_ref, dq_ref, dkv_ref, dz_ref, dgq_ref, dgkv_ref, dgm_ref):
        i = pl.program_id(0)
        first = i == 0
        cos_t, sin_t = _rope_tables(pos_ref[...], tab_ref[...])
        dq_parts, dkv_parts = [], []
        dk_pe = jnp.zeros((tm, 128), F32)
        for hh in range(MLA_HEADS):
            dqh = dqh_ref[hh]
            dq_parts += [dqh[:, 0:NOPE], _rope_apply_t(dqh[:, NOPE:HEAD_PAD], cos_t, sin_t)]
            dkh = dkh_ref[hh]
            dkv_parts += [dkh[:, 0:NOPE], dvh_ref[hh]]
            dk_pe = dk_pe + dkh[:, NOPE:HEAD_PAD]
        dqb = jnp.concatenate(dq_parts, axis=-1).astype(BF16)
        dkvb = jnp.concatenate(dkv_parts, axis=-1).astype(BF16)
        dq_ref[...] = dqb
        dkv_ref[...] = dkvb
        z = z_ref[...]
        c_q = z[:, 0:Q_RANK]
        gq = gq_ref[...]
        _, rq = _rms_fwd(c_q, gq)
        dcq, dgq = _rms_bwd(_dot_nn(dqb, wq_ref[...]), c_q, gq, rq)
        c_kv = z[:, Q_RANK:Q_RANK + KV_RANK]
        gkv = gkv_ref[...]
        _, rkv = _rms_fwd(c_kv, gkv)
        dckv, dgkv = _rms_bwd(_dot_nt(dkvb, wkv_ref[...]), c_kv, gkv, rkv)
        dkr = _rope_apply_t(dk_pe, cos_t, sin_t)
        dzb = jnp.concatenate([dcq, dckv, dkr, dzp_ref[...]], axis=-1).astype(BF16)
        dz_ref[...] = dzb
        x = h_ref[...]
        gm = gm_ref[...]
        _, rm = _rms_fwd(x, gm)
        dx, dgm = _rms_bwd(_dot_nt(dzb, win_ref[...]), x, gm, rm)
        dh1_ref[...] = dh2_ref[...] + dx
        _accumulate(dgq_ref, dgq, first)
        _accumulate(dgkv_ref, dgkv, first)
        _accumulate(dgm_ref, dgm, first)

    full = lambda shape: pl.BlockSpec(shape, lambda i: (0,) * len(shape))
    row = lambda w: pl.BlockSpec((tm, w), lambda i: (i, 0))
    head = lambda w: pl.BlockSpec((MLA_HEADS, tm, w), lambda i: (0, i, 0))
    pw = len(POOL_WINDOWS) * POOL_CH
    return pl.pallas_call(
        body, name="mla_bwd", grid=(t // tm,),
        in_specs=[head(HEAD_PAD), head(HEAD_PAD), head(V_DIM), row(d), row(pw), row(d), row(d),
                  _resident((1, d)), _resident(w_in.shape), _resident((1, Q_RANK)), _resident(wq_t.shape),
                  _resident((1, KV_RANK)), _resident(wkv.shape), row(1), _resident(rope_tab.shape)],
        out_specs=[row(d), row(d), row(d), row(d), full((1, Q_RANK)), full((1, KV_RANK)), full((1, d))],
        out_shape=[jax.ShapeDtypeStruct((t, d), F32), jax.ShapeDtypeStruct((t, d), BF16),
                   jax.ShapeDtypeStruct((t, d), BF16), jax.ShapeDtypeStruct((t, d), BF16),
                   jax.ShapeDtypeStruct((1, Q_RANK), F32), jax.ShapeDtypeStruct((1, KV_RANK), F32),
                   jax.ShapeDtypeStruct((1, d), F32)],
        compiler_params=_cparams(),
    )(dq_h, dk_h, dv_h, z, dz_pool, h1, dh2, mix_norm, w_in, q_norm, wq_t, kv_norm, wkv, pos, rope_tab)


def _mem_kv(mem, mem_norm, wmkv):
    n, d = mem.shape

    def body(mem_ref, g_ref, w_ref, memn_ref, k_ref, v_ref):
        y, _ = _rms_fwd(mem_ref[...], g_ref[...])
        yb = y.astype(BF16)
        memn_ref[...] = yb
        for hh in range(MEM_HEADS):
            k_ref[hh] = _dot_nn(yb, w_ref[hh]).astype(BF16)
            v_ref[hh] = _dot_nn(yb, w_ref[MEM_HEADS + hh]).astype(BF16)

    return pl.pallas_call(
        body, name="mem_kv",
        out_shape=[jax.ShapeDtypeStruct((n, d), BF16), jax.ShapeDtypeStruct((MEM_HEADS, n, MEM_HD), BF16),
                   jax.ShapeDtypeStruct((MEM_HEADS, n, MEM_HD), BF16)],
        compiler_params=_cparams(),
    )(mem, mem_norm, wmkv)


def _mem_softmax(qb, km):
    s = _dot_nt(qb, km) * MEM_SCALE
    e = jnp.exp(s - jnp.max(s, axis=-1, keepdims=True))
    return e / jnp.sum(e, axis=-1, keepdims=True)


def _xattn_fwd(h1, a, p, w_out, g, wmq, km, vm, wmo, token=None):
    t, d = h1.shape
    tm = min(512, t)
    half = a.shape[1]

    def body(h_ref, a_ref, p_ref, wo_ref, g_ref, wmq_ref, km_ref, vm_ref, wmo_ref,
             h2_ref, h3_ref, hn_ref, q_ref, o_ref):
        h2 = h_ref[...] + _dot_nn(a_ref[...], wo_ref[0:half, :]) + _dot_nn(p_ref[...], wo_ref[half:2 * half, :])
        h2_ref[...] = h2
        hn, _ = _rms_fwd(h2, g_ref[...])
        hnb = hn.astype(BF16)
        hn_ref[...] = hnb
        qb = _dot_nn(hnb, wmq_ref[...]).astype(BF16)
        q_ref[...] = qb
        outs = []
        for hh in range(MEM_HEADS):
            pr = _mem_softmax(qb[:, hh * MEM_HD:(hh + 1) * MEM_HD], km_ref[hh])
            outs.append(_dot_nn(pr.astype(BF16), vm_ref[hh]))
        ob = jnp.concatenate(outs, axis=-1).astype(BF16)
        o_ref[...] = ob
        h3_ref[...] = h2 + _dot_nn(ob, wmo_ref[...])

    full = lambda shape: pl.BlockSpec(shape, lambda i: (0,) * len(shape))
    row = lambda w: pl.BlockSpec((tm, w), lambda i: (i, 0))
    return _call_after(
        token, body,
        [row(d), row(half), row(half), _resident(w_out.shape), _resident((1, d)), _resident(wmq.shape),
         _resident(km.shape), _resident(vm.shape), _resident(wmo.shape)],
        (h1, a, p, w_out, g, wmq, km, vm, wmo),
        name="xattn_fwd", grid=(t // tm,),
        out_specs=[row(d), row(d), row(d), row(d), row(d)],
        out_shape=[jax.ShapeDtypeStruct((t, d), F32), jax.ShapeDtypeStruct((t, d), F32),
                   jax.ShapeDtypeStruct((t, d), BF16), jax.ShapeDtypeStruct((t, d), BF16),
                   jax.ShapeDtypeStruct((t, d), BF16)],
        compiler_params=_cparams(),
    )


def _xattn_bwd(dh3, h2, qm, g, wmq, km, vm, wmo, w_out, token=None):
    t, d = h2.shape
    tm = min(512, t)
    half = d // 2

    def body(dh3_ref, h2_ref, q_ref, g_ref, wmq_ref, km_ref, vm_ref, wmo_ref, wo_ref,
             dh2_ref, dq_ref, da_ref, dp_ref, dk_ref, dv_ref, dg_ref):
        i = pl.program_id(0)
        first = i == 0

        @pl.when(first)
        def _():
            dk_ref[...] = jnp.zeros_like(dk_ref)
            dv_ref[...] = jnp.zeros_like(dv_ref)

        dh3 = dh3_ref[...]
        dob = _dot_nt(dh3.astype(BF16), wmo_ref[...]).astype(BF16)
        qb = q_ref[...]
        dq_parts = []
        for hh in range(MEM_HEADS):
            cols = slice(hh * MEM_HD, (hh + 1) * MEM_HD)
            kk, vv = km_ref[hh], vm_ref[hh]
            pr = _mem_softmax(qb[:, cols], kk)
            doh = dob[:, cols]
            dv_ref[hh] += _dot_tn(pr.astype(BF16), doh)
            dpp = _dot_nt(doh, vv)
            dsb = (pr * (dpp - jnp.sum(dpp * pr, axis=-1, keepdims=True)) * MEM_SCALE).astype(BF16)
            dq_parts.append(_dot_nn(dsb, kk))
            dk_ref[hh] += _dot_tn(dsb, qb[:, cols])
        dqb = jnp.concatenate(dq_parts, axis=-1).astype(BF16)
        dq_ref[...] = dqb
        x = h2_ref[...]
        gg = g_ref[...]
        _, r = _rms_fwd(x, gg)
        dx, dg = _rms_bwd(_dot_nt(dqb, wmq_ref[...]), x, gg, r)
        dh2 = dh3 + dx
        dh2_ref[...] = dh2
        dap = _dot_nt(dh2.astype(BF16), wo_ref[...])
        da_ref[...] = dap[:, 0:half].astype(BF16)
        dp_ref[...] = dap[:, half:d].astype(BF16)
        _accumulate(dg_ref, dg, first)

    full = lambda shape: pl.BlockSpec(shape, lambda i: (0,) * len(shape))
    row = lambda w: pl.BlockSpec((tm, w), lambda i: (i, 0))
    return _call_after(
        token, body,
        [row(d), row(d), row(d), _resident((1, d)), _resident(wmq.shape), _resident(km.shape), _resident(vm.shape),
         _resident(wmo.shape), _resident(w_out.shape)],
        (dh3, h2, qm, g, wmq, km, vm, wmo, w_out),
        name="xattn_bwd", grid=(t // tm,),
        out_specs=[row(d), row(d), row(half), row(half), full(km.shape), full(vm.shape), full((1, d))],
        out_shape=[jax.ShapeDtypeStruct((t, d), F32), jax.ShapeDtypeStruct((t, d), BF16),
                   jax.ShapeDtypeStruct((t, half), BF16), jax.ShapeDtypeStruct((t, half), BF16),
                   jax.ShapeDtypeStruct(km.shape, F32), jax.ShapeDtypeStruct(vm.shape, F32),
                   jax.ShapeDtypeStruct((1, d), F32)],
        compiler_params=_cparams(),
    )


def _mem_kv_bwd(dkm, dvm, memn, mem, mem_norm, wmkv):
    n, d = mem.shape

    def body(dk_ref, dv_ref, memn_ref, mem_ref, g_ref, w_ref, dw_ref, dg_ref):
        memn = memn_ref[...]
        dmemn = jnp.zeros((n, d), F32)
        for s in range(2 * MEM_HEADS):
            src = dk_ref[s] if s < MEM_HEADS else dv_ref[s - MEM_HEADS]
            db = src.astype(BF16)
            dw_ref[s] = _dot_tn(memn, db)
            dmemn = dmemn + _dot_nt(db, w_ref[s])
        x = mem_ref[...]
        gg = g_ref[...]
        _, r = _rms_fwd(x, gg)
        _, dg = _rms_bwd(dmemn, x, gg, r)
        dg_ref[...] = dg

    return pl.pallas_call(
        body, name="mem_kv_bwd",
        out_shape=[jax.ShapeDtypeStruct(wmkv.shape, F32), jax.ShapeDtypeStruct((1, d), F32)],
        compiler_params=_cparams(),
    )(dkm, dvm, memn, mem, mem_norm, wmkv)


MESH_ID = pl.DeviceIdType.MESH
ANY = pl.BlockSpec(memory_space=pl.ANY)


def _coords():
    return lax.axis_index("x"), lax.axis_index("y"), lax.axis_index("c")


def _other_chips(x, y):
    return [(1 - x, y), (x, 1 - y), (1 - x, 1 - y)]


def _core_reduce(g, tag):
    _, r, w = g.shape

    def body(g_ref, part_ref, own_sc, recv_sc, send_sems, recv_sems, local_sems):
        x, y, c = _coords()
        sent, local = [], []
        for chip in range(4):
            sent.append(pltpu.make_async_remote_copy(
                src_ref=g_ref.at[2 * chip + (1 - c)], dst_ref=recv_sc.at[chip],
                send_sem=send_sems.at[chip], recv_sem=recv_sems.at[chip],
                device_id=(x, y, 1 - c), device_id_type=MESH_ID))
            local.append(pltpu.make_async_copy(g_ref.at[2 * chip + c], own_sc.at[chip], local_sems.at[chip]))
        for cp in sent + local:
            cp.start()
        for chip in range(4):
            local[chip].wait()
            sent[chip].wait_recv()
            part_ref[chip] = (own_sc[chip].astype(F32) + recv_sc[chip].astype(F32)).astype(part_ref.dtype)
        for cp in sent:
            cp.wait_send()

    return pl.pallas_call(
        body, name="core_reduce_" + tag,
        out_shape=jax.ShapeDtypeStruct((4, r, w), g.dtype),
        in_specs=[ANY], out_specs=pl.BlockSpec(memory_space=pltpu.VMEM),
        scratch_shapes=[pltpu.VMEM((4, r, w), g.dtype), pltpu.VMEM((4, r, w), g.dtype),
                        pltpu.SemaphoreType.DMA((4,)), pltpu.SemaphoreType.DMA((4,)), pltpu.SemaphoreType.DMA((4,))],
        compiler_params=_cparams(),
    )(g)


HBM_SPEC = pl.BlockSpec(memory_space=pltpu.HBM)
SEM_SPEC = pl.BlockSpec(memory_space=pltpu.SEMAPHORE)
SPLIT_EFFECT = pltpu.SideEffectType.DATAFLOW_SIDE_EFFECTING


def _ici_refs(gather, src_ref, land_ref, j, px, py, slot_chip, c):
    if gather:
        return src_ref, land_ref.at[:, 4 * slot_chip[0] + 2 * slot_chip[1] + c]
    return src_ref.at[2 * px + py], land_ref.at[j]


def _ici_start(src, after, name, gather):
    r, w = src.shape[-2:]
    land_shape = (src.shape[0], N_DEV, r, w) if gather else (3, r, w)

    def body(src_ref, land_ref, after_ref, send_sems, recv_sems, src_thru, land_thru, token):
        x, y, c = _coords()
        for j, (px, py) in enumerate(_other_chips(x, y)):
            s_ref, d_ref = _ici_refs(gather, src_ref, land_ref, j, px, py, (x, y), c)
            pltpu.make_async_remote_copy(
                src_ref=s_ref, dst_ref=d_ref, send_sem=send_sems.at[j], recv_sem=recv_sems.at[j],
                device_id=(px, py, c), device_id_type=MESH_ID).start()
        token[...] = jnp.zeros_like(token)

    return pl.pallas_call(
        body, name=name,
        out_shape=(pltpu.SemaphoreType.DMA((3,)), pltpu.SemaphoreType.DMA((3,)), pltpu.HBM(src.shape, src.dtype),
                   pltpu.HBM(land_shape, src.dtype), jax.ShapeDtypeStruct((8, 128), F32)),
        in_specs=(HBM_SPEC, HBM_SPEC, ANY),
        out_specs=(SEM_SPEC, SEM_SPEC, HBM_SPEC, HBM_SPEC, pl.BlockSpec(memory_space=pltpu.VMEM)),
        input_output_aliases={0: 2, 1: 3},
        compiler_params=pltpu.CompilerParams(has_side_effects=SPLIT_EFFECT),
    )(pltpu.with_memory_space_constraint(src, pltpu.HBM),
      pltpu.with_memory_space_constraint(lax.empty(land_shape, src.dtype), pltpu.HBM), after)


def _ici_wait(started, after, name, gather):
    send_sems, recv_sems, src_thru, land_thru, _ = started

    def body(src_ref, land_ref, send_sems, recv_sems, after_ref, src_dead, got_ref):
        x, y, c = _coords()
        for j, (px, py) in enumerate(_other_chips(x, y)):
            s_ref, d_ref = _ici_refs(gather, src_ref, land_ref, j, px, py, (px, py), c)
            copy = pltpu.make_async_remote_copy(
                src_ref=s_ref, dst_ref=d_ref, send_sem=send_sems.at[j], recv_sem=recv_sems.at[j],
                device_id=(px, py, c), device_id_type=MESH_ID)
            copy.wait_send()
            copy.wait_recv()

    return pl.pallas_call(
        body, name=name,
        out_shape=(pltpu.HBM(src_thru.shape, src_thru.dtype), pltpu.HBM(land_thru.shape, land_thru.dtype)),
        in_specs=(HBM_SPEC, HBM_SPEC, SEM_SPEC, SEM_SPEC, ANY),
        out_specs=(HBM_SPEC, HBM_SPEC), input_output_aliases={0: 0, 1: 1},
        compiler_params=pltpu.CompilerParams(has_side_effects=SPLIT_EFFECT),
    )(src_thru, land_thru, send_sems, recv_sems, after)


def _neighbour(k, x, y):
    return (1 - x, y) if k == 0 else (x, 1 - y)


def _slot(ref, px, py, c):
    return ref.at[:, 4 * px + 2 * py + c]


def _near_start(src, after, name):
    land_shape = (src.shape[0], N_DEV) + src.shape[1:]

    def body(src_ref, land_ref, after_ref, send_sems, recv_sems, src_thru, land_thru, token):
        x, y, c = _coords()
        for k in range(2):
            px, py = _neighbour(k, x, y)
            pltpu.make_async_remote_copy(
                src_ref=src_ref, dst_ref=_slot(land_ref, x, y, c), send_sem=send_sems.at[k],
                recv_sem=recv_sems.at[k], device_id=(px, py, c), device_id_type=MESH_ID).start()
        token[...] = jnp.zeros_like(token)

    return pl.pallas_call(
        body, name=name,
        out_shape=(pltpu.SemaphoreType.DMA((2,)), pltpu.SemaphoreType.DMA((2,)), pltpu.HBM(src.shape, src.dtype),
                   pltpu.HBM(land_shape, src.dtype), jax.ShapeDtypeStruct((8, 128), F32)),
        in_specs=(HBM_SPEC, HBM_SPEC, ANY),
        out_specs=(SEM_SPEC, SEM_SPEC, HBM_SPEC, HBM_SPEC, pl.BlockSpec(memory_space=pltpu.VMEM)),
        input_output_aliases={0: 2, 1: 3},
        compiler_params=pltpu.CompilerParams(has_side_effects=SPLIT_EFFECT),
    )(pltpu.with_memory_space_constraint(src, pltpu.HBM),
      pltpu.with_memory_space_constraint(lax.empty(land_shape, src.dtype), pltpu.HBM), after)


def _near_wait(started, after, name):
    send_sems, recv_sems, src_thru, land_thru, _ = started

    def body(src_ref, land_ref, send_sems, recv_sems, after_ref, src_dead, got_ref):
        x, y, c = _coords()
        for k in range(2):
            px, py = _neighbour(k, x, y)
            copy = pltpu.make_async_remote_copy(
                src_ref=src_ref, dst_ref=_slot(land_ref, px, py, c), send_sem=send_sems.at[k],
                recv_sem=recv_sems.at[k], device_id=(px, py, c), device_id_type=MESH_ID)
            copy.wait_send()
            copy.wait_recv()

    return pl.pallas_call(
        body, name=name,
        out_shape=(pltpu.HBM(src_thru.shape, src_thru.dtype), pltpu.HBM(land_thru.shape, land_thru.dtype)),
        in_specs=(HBM_SPEC, HBM_SPEC, SEM_SPEC, SEM_SPEC, ANY),
        out_specs=(HBM_SPEC, HBM_SPEC), input_output_aliases={0: 0, 1: 1},
        compiler_params=pltpu.CompilerParams(has_side_effects=SPLIT_EFFECT),
    )(src_thru, land_thru, send_sems, recv_sems, after)


def _far_refs(land_ref, k, x, y, c, arriving):
    half = land_ref.shape[2] // 2
    rows = pl.ds(k * half, half)
    ox, oy = (1 - x, 1 - y) if arriving else _neighbour(k, x, y)
    return land_ref.at[:, 4 * ox + 2 * oy + c, rows]


def _far_start(land, after, name):
    def body(land_ref, after_ref, send_sems, recv_sems, land_thru, token):
        x, y, c = _coords()
        for k in range(2):
            block = _far_refs(land_ref, k, x, y, c, False)
            px, py = _neighbour(1 - k, x, y)
            pltpu.make_async_remote_copy(
                src_ref=block, dst_ref=block, send_sem=send_sems.at[k], recv_sem=recv_sems.at[k],
                device_id=(px, py, c), device_id_type=MESH_ID).start()
        token[...] = jnp.zeros_like(token)

    return pl.pallas_call(
        body, name=name,
        out_shape=(pltpu.SemaphoreType.DMA((2,)), pltpu.SemaphoreType.DMA((2,)),
                   pltpu.HBM(land.shape, land.dtype), jax.ShapeDtypeStruct((8, 128), F32)),
        in_specs=(HBM_SPEC, ANY),
        out_specs=(SEM_SPEC, SEM_SPEC, HBM_SPEC, pl.BlockSpec(memory_space=pltpu.VMEM)),
        input_output_aliases={0: 2},
        compiler_params=pltpu.CompilerParams(has_side_effects=SPLIT_EFFECT),
    )(pltpu.with_memory_space_constraint(land, pltpu.HBM), after)


def _far_wait(started, after, name):
    send_sems, recv_sems, land_thru, _ = started

    def body(land_ref, send_sems, recv_sems, after_ref, got_ref):
        x, y, c = _coords()
        for k in range(2):
            px, py = _neighbour(1 - k, x, y)
            copy = pltpu.make_async_remote_copy(
                src_ref=_far_refs(land_ref, k, x, y, c, False), dst_ref=_far_refs(land_ref, k, x, y, c, True),
                send_sem=send_sems.at[k], recv_sem=recv_sems.at[k], device_id=(px, py, c), device_id_type=MESH_ID)
            copy.wait_send()
            copy.wait_recv()

    return pl.pallas_call(
        body, name=name,
        out_shape=pltpu.HBM(land_thru.shape, land_thru.dtype),
        in_specs=(HBM_SPEC, SEM_SPEC, SEM_SPEC, ANY),
        out_specs=HBM_SPEC, input_output_aliases={0: 0},
        compiler_params=pltpu.CompilerParams(has_side_effects=SPLIT_EFFECT),
    )(land_thru, send_sems, recv_sems, after)


def _peer(k, x, y, c):
    return x ^ ((k >> 2) & 1), y ^ ((k >> 1) & 1), c ^ (k & 1)


def _peers_start(src, after, name):
    r, w = src.shape
    x, y, c = _coords()
    land = lax.dynamic_update_slice(jnp.zeros((N_DEV, r, w), src.dtype), src[None], (4 * x + 2 * y + c, 0, 0))

    def body(src_ref, land_ref, after_ref, send_sems, recv_sems, src_thru, land_thru, token):
        x, y, c = _coords()
        for k in range(1, N_DEV):
            pltpu.make_async_remote_copy(
                src_ref=src_ref, dst_ref=land_ref.at[4 * x + 2 * y + c],
                send_sem=send_sems.at[k - 1], recv_sem=recv_sems.at[k - 1],
                device_id=_peer(k, x, y, c), device_id_type=MESH_ID).start()
        token[...] = jnp.zeros_like(token)

    return pl.pallas_call(
        body, name=name,
        out_shape=(pltpu.SemaphoreType.DMA((N_DEV - 1,)), pltpu.SemaphoreType.DMA((N_DEV - 1,)),
                   pltpu.HBM(src.shape, src.dtype), pltpu.HBM(land.shape, src.dtype),
                   jax.ShapeDtypeStruct((8, 128), F32)),
        in_specs=(HBM_SPEC, HBM_SPEC, ANY),
        out_specs=(SEM_SPEC, SEM_SPEC, HBM_SPEC, HBM_SPEC, pl.BlockSpec(memory_space=pltpu.VMEM)),
        input_output_aliases={0: 2, 1: 3},
        compiler_params=pltpu.CompilerParams(has_side_effects=SPLIT_EFFECT),
    )(pltpu.with_memory_space_constraint(src, pltpu.HBM), pltpu.with_memory_space_constraint(land, pltpu.HBM), after)


def _peers_wait(started, after, name):
    send_sems, recv_sems, src_thru, land_thru, _ = started

    def body(src_ref, land_ref, send_sems, recv_sems, after_ref, src_dead, got_ref):
        x, y, c = _coords()
        for k in range(1, N_DEV):
            px, py, pc = _peer(k, x, y, c)
            copy = pltpu.make_async_remote_copy(
                src_ref=src_ref, dst_ref=land_ref.at[4 * px + 2 * py + pc],
                send_sem=send_sems.at[k - 1], recv_sem=recv_sems.at[k - 1],
                device_id=(px, py, pc), device_id_type=MESH_ID)
            copy.wait_send()
            copy.wait_recv()

    return pl.pallas_call(
        body, name=name,
        out_shape=(pltpu.HBM(src_thru.shape, src_thru.dtype), pltpu.HBM(land_thru.shape, land_thru.dtype)),
        in_specs=(HBM_SPEC, HBM_SPEC, SEM_SPEC, SEM_SPEC, ANY),
        out_specs=(HBM_SPEC, HBM_SPEC), input_output_aliases={0: 0, 1: 1},
        compiler_params=pltpu.CompilerParams(has_side_effects=SPLIT_EFFECT),
    )(src_thru, land_thru, send_sems, recv_sems, after)


def _share_refs(ref, k, x, y, c, sender_c):
    px, py = ([(x, y)] + _other_chips(x, y))[k]
    return ref.at[:, 4 * px + 2 * py + sender_c]


def _share_start(gathered, after, name):
    def body(g_ref, after_ref, send_sems, recv_sems, g_thru, token):
        x, y, c = _coords()
        for k in range(4):
            slot = _share_refs(g_ref, k, x, y, c, c)
            pltpu.make_async_remote_copy(
                src_ref=slot, dst_ref=slot, send_sem=send_sems.at[k], recv_sem=recv_sems.at[k],
                device_id=(x, y, 1 - c), device_id_type=MESH_ID).start()
        token[...] = jnp.zeros_like(token)

    return pl.pallas_call(
        body, name=name,
        out_shape=(pltpu.SemaphoreType.DMA((4,)), pltpu.SemaphoreType.DMA((4,)),
                   pltpu.HBM(gathered.shape, gathered.dtype), jax.ShapeDtypeStruct((8, 128), F32)),
        in_specs=(HBM_SPEC, ANY),
        out_specs=(SEM_SPEC, SEM_SPEC, HBM_SPEC, pl.BlockSpec(memory_space=pltpu.VMEM)),
        input_output_aliases={0: 2},
        compiler_params=pltpu.CompilerParams(has_side_effects=SPLIT_EFFECT),
    )(pltpu.with_memory_space_constraint(gathered, pltpu.HBM), after)


def _share_wait(started, after, name):
    send_sems, recv_sems, g_thru, _ = started

    def body(g_ref, send_sems, recv_sems, after_ref, got_ref):
        x, y, c = _coords()
        for k in range(4):
            copy = pltpu.make_async_remote_copy(
                src_ref=_share_refs(g_ref, k, x, y, c, c), dst_ref=_share_refs(g_ref, k, x, y, c, 1 - c),
                send_sem=send_sems.at[k], recv_sem=recv_sems.at[k],
                device_id=(x, y, 1 - c), device_id_type=MESH_ID)
            copy.wait_send()
            copy.wait_recv()

    return pl.pallas_call(
        body, name=name,
        out_shape=pltpu.HBM(g_thru.shape, g_thru.dtype),
        in_specs=(HBM_SPEC, SEM_SPEC, SEM_SPEC, ANY),
        out_specs=HBM_SPEC, input_output_aliases={0: 0},
        compiler_params=pltpu.CompilerParams(has_side_effects=SPLIT_EFFECT),
    )(g_thru, send_sems, recv_sems, after)


def _core_share(own, gathered, name):
    def body(own_ref, gin_ref, out_ref, stage, send_sems, recv_sems, local_sem):
        x, y, c = _coords()
        sibling = (x, y, 1 - c)
        chips = [(x, y)] + _other_chips(x, y)
        stage_in = pltpu.make_async_copy(own_ref, stage, local_sem)
        stage_in.start()
        sent, arriving = [], []
        for k, (px, py) in enumerate(chips):
            slot = out_ref.at[:, 4 * px + 2 * py + c]
            sent.append(pltpu.make_async_remote_copy(
                src_ref=own_ref if k == 0 else slot, dst_ref=slot,
                send_sem=send_sems.at[k], recv_sem=recv_sems.at[k], device_id=sibling, device_id_type=MESH_ID))
            arriving.append(pltpu.make_async_remote_copy(
                src_ref=own_ref, dst_ref=out_ref.at[:, 4 * px + 2 * py + (1 - c)],
                send_sem=send_sems.at[k], recv_sem=recv_sems.at[k], device_id=sibling, device_id_type=MESH_ID))
        for cp in sent:
            cp.start()
        stage_in.wait()
        stage_out = pltpu.make_async_copy(stage, out_ref.at[:, 4 * x + 2 * y + c], local_sem)
        stage_out.start()
        for cp in arriving:
            cp.wait_recv()
        for cp in sent:
            cp.wait_send()
        stage_out.wait()

    return pl.pallas_call(
        body, name=name,
        out_shape=jax.ShapeDtypeStruct(gathered.shape, own.dtype),
        in_specs=[ANY, ANY], out_specs=ANY, input_output_aliases={1: 0},
        scratch_shapes=[pltpu.VMEM(own.shape, own.dtype), pltpu.SemaphoreType.DMA((4,)),
                        pltpu.SemaphoreType.DMA((4,)), pltpu.SemaphoreType.DMA],
    )(own, gathered)


def _adamw(w, g, m, v):
    m = ADAM_B1 * m + (1.0 - ADAM_B1) * g
    v = ADAM_B2 * v + (1.0 - ADAM_B2) * (g * g)
    m_hat = m / ADAM_C1
    v_hat = v / ADAM_C2
    delta = -ADAM_LR * (m_hat / (jnp.sqrt(v_hat) + ADAM_EPS) + ADAM_WD * w)
    return delta, m, v


def _adam_big(units, chip_idx, tag, token):
    n = len(units)
    r, wd = units[0][2].shape
    tr, tw = _row_tile(r, 1024), 256

    def body(s_ref, tok_ref, *refs):
        for u in range(n):
            p_ref, l_ref, w_ref, m_ref, v_ref = refs[5 * u:5 * u + 5]
            g_ref, d_ref, mo_ref, vo_ref = refs[5 * n + 4 * u:5 * n + 4 * u + 4]
            g = p_ref[0].astype(F32)
            for j in range(3):
                g = g + l_ref[j].astype(F32)
            delta, mn, vn = _adamw(w_ref[...], g, m_ref[...], v_ref[...])
            g_ref[...] = g
            d_ref[...] = delta
            mo_ref[...] = mn
            vo_ref[...] = vn

    row = pl.BlockSpec((tr, tw), lambda i, j, s: (i, j))
    unit_specs = [pl.BlockSpec((1, tr, tw), lambda i, j, s: (s[0], i, j)),
                  pl.BlockSpec((3, tr, tw), lambda i, j, s: (0, i, j)), row, row, row]
    outs = pl.pallas_call(
        body, name="adam_big_" + tag,
        grid_spec=pltpu.PrefetchScalarGridSpec(
            num_scalar_prefetch=1, grid=(r // tr, wd // tw),
            in_specs=[pl.BlockSpec((8, 128), lambda i, j, s: (0, 0))] + unit_specs * n,
            out_specs=[row] * (4 * n)),
        out_shape=[jax.ShapeDtypeStruct((r, wd), F32)] * (4 * n),
        compiler_params=_cparams(),
    )(chip_idx, token, *[a for unit in units for a in unit])
    return [outs[4 * u:4 * u + 4] for u in range(n)]


def _adam_small(parts, w, m, v):
    _, r, wd = parts.shape

    def body(p_ref, w_ref, m_ref, v_ref, g_ref, d_ref, mo_ref, vo_ref):
        g = p_ref[0]
        for k in range(1, N_DEV):
            g = g + p_ref[k]
        delta, mn, vn = _adamw(w_ref[...], g, m_ref[...], v_ref[...])
        g_ref[...] = g
        d_ref[...] = delta
        mo_ref[...] = mn
        vo_ref[...] = vn

    return pl.pallas_call(
        body, name="adam_small",
        out_shape=[jax.ShapeDtypeStruct((r, wd), F32)] * 4,
        compiler_params=_cparams(),
    )(parts, w, m, v)


def _pad_rows(a, rows):
    return jnp.pad(a, ((0, rows - a.shape[0]), (0, 0)))


def _pad_w_in(w):
    cut = Q_RANK + KV_RANK + ROPE
    return jnp.concatenate([w[:, :cut], jnp.zeros((w.shape[0], 64), w.dtype), w[:, cut:]], axis=1)


def _unpad_w_in(w):
    cut = Q_RANK + KV_RANK + ROPE
    return jnp.concatenate([w[:, :cut], w[:, cut + 64:]], axis=1)


def _pack_mid(p):
    parts = [_pad_w_in(p["w_in"][0]), p["w_out"][0], p["w_mq"][0], p["w_mo"][0],
             p["w_mkv"][0].reshape(256, D_MODEL),
             _pad_rows(p["w_q_up"][0].T.reshape(24, D_MODEL), 32),
             p["w_kv_up"][0].reshape(16, D_MODEL)]
    return jnp.concatenate(parts, axis=0)


def _pack_ffn(w_gate, w_up, w_down, name):
    d, rows = w_gate.shape[1:]

    def body(g_ref, u_ref, d_ref, o_ref):
        eye = (lax.broadcasted_iota(jnp.int32, (d, d), 0) == lax.broadcasted_iota(jnp.int32, (d, d), 1)).astype(BF16)
        o_ref[0] = _dot_tn(g_ref[0].astype(BF16), eye).astype(BF16)
        o_ref[1] = _dot_tn(u_ref[0].astype(BF16), eye).astype(BF16)
        o_ref[2] = d_ref[0].astype(BF16)

    return pl.pallas_call(
        body, name=name, out_shape=jax.ShapeDtypeStruct((3, rows, d), BF16), compiler_params=_cparams(),
    )(w_gate, w_up, w_down)


def _pack_segments(p, group):
    if group == "mid":
        return _pack_mid(p)[None].astype(BF16)
    return _pack_ffn(p[group + "_w_gate"], p[group + "_w_up"], p[group + "_w_down"], "pack_" + group)


UNIT_WEIGHT = {"ffn1_g": ("ffn1_w_gate", True), "ffn1_u": ("ffn1_w_up", True), "ffn1_d": ("ffn1_w_down", False),
               "ffn2_g": ("ffn2_w_gate", True), "ffn2_u": ("ffn2_w_up", True), "ffn2_d": ("ffn2_w_down", False)}


def _pack_unit(p, unit):
    if unit == "mid":
        return _pack_mid(p)
    name, transposed = UNIT_WEIGHT[unit]
    return p[name][0].T if transposed else p[name][0]


def _unpack_unit(a, unit):
    if unit != "mid":
        name, transposed = UNIT_WEIGHT[unit]
        return {name: (a.T if transposed else a)[None]}
    seg = lambda n: a[SEG_OFF[n][0]:SEG_OFF[n][0] + SEG_OFF[n][1]]
    return {"w_in": _unpad_w_in(seg("w_in"))[None], "w_out": seg("w_out")[None], "w_mq": seg("w_mq")[None],
            "w_mo": seg("w_mo")[None], "w_mkv": seg("w_mkv").reshape(D_MODEL, 256)[None],
            "w_q_up": seg("w_q")[:24].reshape(96, Q_RANK).T[None],
            "w_kv_up": seg("w_kv").reshape(KV_RANK, 128)[None]}


def _unpack_gathered(full, group):
    if group != "mid":
        return {n: full[k].reshape(-1, D_MODEL) for k, (n, _) in enumerate(GROUP_SEGS[group])}
    full = full[0]
    seg = lambda n: full[:, SEG_OFF[n][0]:SEG_OFF[n][0] + SEG_OFF[n][1]]
    rows = lambda n: seg(n).reshape(-1, D_MODEL)
    wq_t = seg("w_q")[:, :24].reshape(MLA_HEADS, NOPE + ROPE, Q_RANK)
    wq_t = jnp.pad(wq_t, ((0, 0), (0, HEAD_PAD - NOPE - ROPE), (0, 0))).reshape(MLA_HEADS * HEAD_PAD, Q_RANK)
    wkv = seg("w_kv").reshape(N_DEV, KV_RANK, 128).transpose(1, 0, 2).reshape(KV_RANK, N_DEV * 128)
    return {"w_in": rows("w_in"), "w_out": rows("w_out"), "w_mq": rows("w_mq"), "w_mo": rows("w_mo"),
            "w_mkv": seg("w_mkv").reshape(N_DEV, D_MODEL, 256), "w_q": wq_t, "w_kv": wkv}


def _pack_grads(gr):
    blk = lambda a: a.reshape(N_DEV, -1, D_MODEL)
    dwq = gr["w_q"].reshape(MLA_HEADS, HEAD_PAD, Q_RANK)[:, :NOPE + ROPE].reshape(N_DEV, 24, D_MODEL)
    dwq = jnp.pad(dwq, ((0, 0), (0, 8), (0, 0)))
    dwkv = gr["w_kv"].reshape(KV_RANK, N_DEV, 128).transpose(1, 0, 2).reshape(N_DEV, 16, D_MODEL)
    parts = [blk(gr["w_in"]), blk(gr["w_out"]), blk(gr["w_mq"]), blk(gr["w_mo"]),
             gr["w_mkv"].reshape(N_DEV, 256, D_MODEL), dwq, dwkv]
    return jnp.concatenate([a.astype(BF16) for a in parts], axis=1)


def _pack_small(vals):
    parts = []
    for n, r in SMALL_ROWS:
        parts.append(_pad_rows(vals[n].reshape(-1, 128), r) if n in vals else jnp.zeros((r, 128), F32))
    return jnp.concatenate(parts, axis=0)


def _unpack_small(a, shapes):
    out = {}
    for n, shape in shapes.items():
        o = SMALL_OFF[n][0]
        out[n] = a[o:o + int(np.prod(shape)) // 128].reshape(shape)
    return out


BIG_NAMES = ("ffn1_w_gate", "ffn1_w_up", "ffn1_w_down", "w_in", "w_q_up", "w_kv_up", "w_out", "w_mq", "w_mkv",
             "w_mo", "ffn2_w_gate", "ffn2_w_up", "ffn2_w_down")
SMALL_NAMES = ("ffn1_norm", "mix_norm", "q_norm", "kv_norm", "pool_w", "pool_scale", "xattn_norm", "mem_norm",
               "ffn2_norm", "final_norm")
WEIGHT_ORDER = ("ffn1_norm", "ffn1_w_gate", "ffn1_w_up", "ffn1_w_down", "mix_norm", "w_in", "q_norm", "w_q_up",
                "kv_norm", "w_kv_up", "pool_w", "pool_scale", "w_out", "xattn_norm", "mem_norm", "w_mq", "w_mkv",
                "w_mo", "ffn2_norm", "ffn2_w_gate", "ffn2_w_up", "ffn2_w_down", "final_norm")


def _rope_table():
    lane = np.arange(128)
    freqs = (1.0 / (ROPE_BASE ** (np.arange(0, ROPE, 2, dtype=np.float32) / ROPE))).astype(np.float32)
    tab = np.zeros((8, 128), np.float32)
    tab[0] = np.where(lane < ROPE, freqs[lane % (ROPE // 2)], 0.0)
    tab[1] = np.where(lane < ROPE // 2, -1.0, np.where(lane < ROPE, 1.0, 0.0))
    return jnp.asarray(tab)


def kernel(x, mem, positions, ffn1_norm, ffn1_w_gate, ffn1_w_up, ffn1_w_down, mix_norm, w_in, q_norm, w_q_up, kv_norm, w_kv_up, pool_w, pool_scale, w_out, xattn_norm, mem_norm, w_mq, w_mkv, w_mo, ffn2_norm, ffn2_w_gate, ffn2_w_up, ffn2_w_down, final_norm, loss_target, m_ffn1_norm, m_ffn1_w_gate, m_ffn1_w_up, m_ffn1_w_down, m_mix_norm, m_w_in, m_q_norm, m_w_q_up, m_kv_norm, m_w_kv_up, m_pool_w, m_pool_scale, m_w_out, m_xattn_norm, m_mem_norm, m_w_mq, m_w_mkv, m_w_mo, m_ffn2_norm, m_ffn2_w_gate, m_ffn2_w_up, m_ffn2_w_down, m_final_norm, v_ffn1_norm, v_ffn1_w_gate, v_ffn1_w_up, v_ffn1_w_down, v_mix_norm, v_w_in, v_q_norm, v_w_q_up, v_kv_norm, v_w_kv_up, v_pool_w, v_pool_scale, v_w_out, v_xattn_norm, v_mem_norm, v_w_mq, v_w_mkv, v_w_mo, v_ffn2_norm, v_ffn2_w_gate, v_ffn2_w_up, v_ffn2_w_down, v_final_norm):
    wts = dict(ffn1_norm=ffn1_norm, ffn1_w_gate=ffn1_w_gate, ffn1_w_up=ffn1_w_up, ffn1_w_down=ffn1_w_down,
               mix_norm=mix_norm, w_in=w_in, q_norm=q_norm, w_q_up=w_q_up, kv_norm=kv_norm, w_kv_up=w_kv_up,
               pool_w=pool_w, pool_scale=pool_scale, w_out=w_out, xattn_norm=xattn_norm, mem_norm=mem_norm,
               w_mq=w_mq, w_mkv=w_mkv, w_mo=w_mo, ffn2_norm=ffn2_norm, ffn2_w_gate=ffn2_w_gate,
               ffn2_w_up=ffn2_w_up, ffn2_w_down=ffn2_w_down, final_norm=final_norm)
    mom = dict(ffn1_norm=m_ffn1_norm, ffn1_w_gate=m_ffn1_w_gate, ffn1_w_up=m_ffn1_w_up, ffn1_w_down=m_ffn1_w_down,
               mix_norm=m_mix_norm, w_in=m_w_in, q_norm=m_q_norm, w_q_up=m_w_q_up, kv_norm=m_kv_norm,
               w_kv_up=m_w_kv_up, pool_w=m_pool_w, pool_scale=m_pool_scale, w_out=m_w_out, xattn_norm=m_xattn_norm,
               mem_norm=m_mem_norm, w_mq=m_w_mq, w_mkv=m_w_mkv, w_mo=m_w_mo, ffn2_norm=m_ffn2_norm,
               ffn2_w_gate=m_ffn2_w_gate, ffn2_w_up=m_ffn2_w_up, ffn2_w_down=m_ffn2_w_down, final_norm=m_final_norm)
    var = dict(ffn1_norm=v_ffn1_norm, ffn1_w_gate=v_ffn1_w_gate, ffn1_w_up=v_ffn1_w_up, ffn1_w_down=v_ffn1_w_down,
               mix_norm=v_mix_norm, w_in=v_w_in, q_norm=v_q_norm, w_q_up=v_w_q_up, kv_norm=v_kv_norm,
               w_kv_up=v_w_kv_up, pool_w=v_pool_w, pool_scale=v_pool_scale, w_out=v_w_out, xattn_norm=v_xattn_norm,
               mem_norm=v_mem_norm, w_mq=v_w_mq, w_mkv=v_w_mkv, w_mo=v_w_mo, ffn2_norm=v_ffn2_norm,
               ffn2_w_gate=v_ffn2_w_gate, ffn2_w_up=v_ffn2_w_up, ffn2_w_down=v_ffn2_w_down, final_norm=v_final_norm)

    t = x.shape[1]
    xs = x[0]
    mems = mem[0]
    target = loss_target[0]
    pos = positions.reshape(t, 1)
    row = lambda a: a.reshape(1, -1)
    rope_tab = _rope_table()

    cx, cy, cc = _coords()
    chip_idx = (2 * cx + cy).astype(jnp.int32).reshape(1)

    wb = {}
    for grp in ("ffn1", "mid", "ffn2"):
        wb[grp] = _pack_segments(wts, grp)
        if grp == "ffn1":
            near_ffn1 = _near_start(wb["ffn1"], pos, "ag_ffn1_near_start")
    own_ffn1, land_ffn1 = _near_wait(near_ffn1, wb["ffn2"], "ag_ffn1_near_wait")
    far_ffn1 = _far_start(land_ffn1, own_ffn1, "ag_ffn1_far_start")
    land_ffn1 = _far_wait(far_ffn1, wb["mid"], "ag_ffn1_far_wait")
    full_ffn1 = _core_share(own_ffn1, land_ffn1, "ag_ffn1_share")
    fw = _unpack_gathered(full_ffn1, "ffn1")
    ag_mid = _ici_start(wb["mid"], full_ffn1, "ag_mid_start", True)
    g_ffn1, g_mix, g_q, g_kv = row(ffn1_norm), row(mix_norm), row(q_norm), row(kv_norm)
    g_x, g_mem, g_ffn2, g_fin = row(xattn_norm), row(mem_norm), row(ffn2_norm), row(final_norm)
    pool_wb = pool_w[0].astype(BF16)
    pool_sc = row(pool_scale)

    h1, n1, gate1, up1 = _ffn_fwd(xs, g_ffn1, fw["ffn1_g"], fw["ffn1_u"], fw["ffn1_d"], "ffn1_fwd", token=ag_mid[4])
    own_mid, land_mid = _ici_wait(ag_mid, h1, "ag_mid_wait", True)
    full_mid = _core_share(own_mid, land_mid, "ag_mid_share")
    fw.update(_unpack_gathered(full_mid, "mid"))
    ag_ffn2 = _ici_start(wb["ffn2"], full_mid, "ag_ffn2_start", True)
    u, z, qn, kvn, qh, kh, vh = _mix_prep(h1, g_mix, fw["w_in"], g_q, fw["w_q"], g_kv, fw["w_kv"], pos, rope_tab,
                                          token=ag_ffn2[4])
    a, lse = _attn_fwd(qh, kh, vh)
    p = _pool_fwd(z, pool_wb, pool_sc)
    memn, km, vm = _mem_kv(mems, g_mem, fw["w_mkv"])
    own_ffn2, land_ffn2 = _ici_wait(ag_ffn2, a, "ag_ffn2_wait", True)
    land_ffn2 = lax.dynamic_update_slice(land_ffn2, own_ffn2[:, None], (0, 4 * cx + 2 * cy + cc, 0, 0))
    share_ffn2 = _share_start(land_ffn2, a, "ag_ffn2_share_start")
    h2, h3, hn, qm, om = _xattn_fwd(h1, a, p, fw["w_out"], g_x, fw["w_mq"], km, vm, fw["w_mo"], token=share_ffn2[3])
    fw.update(_unpack_gathered(_share_wait(share_ffn2, h3, "ag_ffn2_share_wait"), "ffn2"))
    dh4, n2, gate2, up2, loss_part, dg_fin = _ffn_fwd(h3, g_ffn2, fw["ffn2_g"], fw["ffn2_u"], fw["ffn2_d"],
                                                      "ffn2_fwd", head=(target, g_fin))

    def reduce_start(g8, unit):
        part = _core_reduce(g8, unit)
        return _ici_start(part, g8, "rs_" + unit + "_start", False)

    def by_device(g):
        return g.reshape(N_DEV, -1, D_MODEL)

    rs = {}
    dh3, dgate2, dup2, act2, dg_ffn2 = _ffn_bwd_data(dh4, h3, g_ffn2, gate2, up2, fw["ffn2_g"], fw["ffn2_u"],
                                                     fw["ffn2_d"], "ffn2_bwd")
    rs["ffn2_g"] = reduce_start(by_device(_tn_matmul(dgate2, n2, "ffn2_dwg", tmm=1408, m=D_FF, out_dtype=BF16)), "ffn2_g")
    rs["ffn2_u"] = reduce_start(by_device(_tn_matmul(dup2, n2, "ffn2_dwu", tmm=1408, m=D_FF, out_dtype=BF16,
                                                     token=rs["ffn2_g"][4])), "ffn2_u")
    rs["ffn2_d"] = reduce_start(by_device(_tn_matmul(act2, dh4, "ffn2_dwd", scale=0.5, tmm=1408, m=D_FF, out_dtype=BF16,
                                                     token=rs["ffn2_u"][4])), "ffn2_d")
    dh2, dqm, da, dp, dkm, dvm, dg_x = _xattn_bwd(dh3, h2, qm, g_x, fw["w_mq"], km, vm, fw["w_mo"], fw["w_out"],
                                                  token=rs["ffn2_d"][4])
    gr = {}
    gr["w_mo"] = _tn_matmul(om, dh3, "dw_mo", out_dtype=BF16)
    gr["w_mq"] = _tn_matmul(hn, dqm, "dw_mq", out_dtype=BF16)
    gr["w_out"] = jnp.concatenate([_tn_matmul(a, dh2, "dw_out_a", out_dtype=BF16),
                                   _tn_matmul(p, dh2, "dw_out_p", out_dtype=BF16)], axis=0)
    gr["w_mkv"], dg_mem = _mem_kv_bwd(dkm, dvm, memn, mems, g_mem, fw["w_mkv"])
    dz_pool, d_pool_w, d_pool_sc = _pool_bwd(dp, z, pool_wb, pool_sc)
    dqh, dkh, dvh = _attn_bwd(qh, kh, vh, da, lse, _attn_delta(a, da))
    dh1, dq, dkv, dz, dg_q, dg_kv, dg_mix = _mla_bwd(dqh, dkh, dvh, z, dz_pool, h1, dh2, g_mix, fw["w_in"], g_q,
                                                     fw["w_q"], g_kv, fw["w_kv"], pos, rope_tab)
    gr["w_q"] = _tn_matmul(dq, qn, "dw_q", out_dtype=BF16)
    gr["w_kv"] = _tn_matmul(kvn, dkv, "dw_kv", out_dtype=BF16)
    gr["w_in"] = _tn_matmul(u, dz, "dw_in", out_dtype=BF16)
    g_mid = _pack_grads(gr)
    part_mid = _core_reduce(g_mid, "mid")
    got = {}
    after = part_mid
    for unit in ("ffn2_g", "ffn2_u", "ffn2_d"):
        got[unit] = _ici_wait(rs[unit], after, "rs_" + unit + "_wait", False)
        after = got[unit][1]
    rs["mid"] = _ici_start(part_mid, after, "rs_mid_start", False)
    dx, dgate1, dup1, act1, dg_ffn1 = _ffn_bwd_data(dh1, xs, g_ffn1, gate1, up1, fw["ffn1_g"], fw["ffn1_u"],
                                                    fw["ffn1_d"], "ffn1_bwd", token=rs["mid"][4])
    got["mid"] = _ici_wait(rs["mid"], dx, "rs_mid_wait", False)

    small_g = dict(ffn1_norm=dg_ffn1, mix_norm=dg_mix, q_norm=dg_q, kv_norm=dg_kv, pool_w=d_pool_w,
                   pool_scale=d_pool_sc, xattn_norm=dg_x, mem_norm=dg_mem, ffn2_norm=dg_ffn2, final_norm=dg_fin,
                   loss=loss_part)
    small_ag = _peers_start(_pack_small(small_g), got["mid"][1], "small_ag_start")
    rs["ffn1_g"] = reduce_start(by_device(_tn_matmul(dgate1, n1, "ffn1_dwg", tmm=1408, m=D_FF, out_dtype=BF16,
                                                     token=small_ag[4])), "ffn1_g")
    _, parts = _peers_wait(small_ag, rs["ffn1_g"][4], "small_ag_wait")
    small = _adam_small(parts, _pack_small({n: wts[n] for n in SMALL_NAMES}),
                        _pack_small({n: mom[n] for n in SMALL_NAMES}), _pack_small({n: var[n] for n in SMALL_NAMES}))
    small_sum = small[0]
    loss = small_sum[SMALL_OFF["loss"][0], 0]
    shapes = {n: wts[n].shape for n in SMALL_NAMES}
    small = [_unpack_small(s, shapes) for s in small]

    rs["ffn1_u"] = reduce_start(by_device(_tn_matmul(dup1, n1, "ffn1_dwu", tmm=1408, m=D_FF, out_dtype=BF16,
                                                     token=small_sum)), "ffn1_u")
    rs["ffn1_d"] = reduce_start(by_device(_tn_matmul(act1, dh1, "ffn1_dwd", scale=0.5, tmm=1408, m=D_FF, out_dtype=BF16,
                                                     token=rs["ffn1_u"][4])), "ffn1_d")

    big = {}

    def adam_units(names, token):
        units = [got[u] + (_pack_unit(wts, u), _pack_unit(mom, u), _pack_unit(var, u)) for u in names]
        res = _adam_big(units, chip_idx, "_".join(names), token)
        for u, four in zip(names, res):
            for k, packed in enumerate(four):
                big.setdefault(k, {}).update(_unpack_unit(packed, u))
        return res[-1][0]

    done = adam_units(["mid"], rs["ffn1_d"][4])
    done = adam_units(["ffn2_g", "ffn2_u", "ffn2_d"], done)
    got["ffn1_g"] = _ici_wait(rs["ffn1_g"], done, "rs_ffn1_g_wait", False)
    got["ffn1_u"] = _ici_wait(rs["ffn1_u"], got["ffn1_g"][1], "rs_ffn1_u_wait", False)
    done = adam_units(["ffn1_g", "ffn1_u"], done)
    got["ffn1_d"] = _ici_wait(rs["ffn1_d"], done, "rs_ffn1_d_wait", False)
    adam_units(["ffn1_d"], done)

    outs = [loss, dx[None]]
    for k in range(4):
        for n in WEIGHT_ORDER:
            outs.append(big[k][n] if n in BIG_NAMES else small[k][n])
    return tuple(outs)
```

```python
import numpy as np

import jax
import jax.numpy as jnp
from jax import lax
from jax.experimental import pallas as pl
from jax.experimental.pallas import tpu as pltpu

F32 = jnp.float32
BF16 = jnp.bfloat16

N_DEV = 8
D_MODEL = 1024
D_FF = 2816
MLA_HEADS = 4
NOPE = 128
ROPE = 64
HEAD_PAD = 256
V_DIM = 128
Q_RANK = 256
KV_RANK = 128
POOL_WINDOWS = (2, 4, 8, 16)
POOL_CH = 128
POOL_HALO = 16
N_MEM = 256
MEM_HEADS = 4
MEM_HD = 256
ROPE_BASE = 10000.0
RMS_EPS = 1e-6
ATTN_SCALE = (NOPE + ROPE) ** -0.5
MEM_SCALE = MEM_HD ** -0.5
NEG_BIG = -1e30

ADAM_LR = 0.001
ADAM_B1 = 0.9
ADAM_B2 = 0.999
ADAM_EPS = 1e-08
ADAM_WD = 0.01
ADAM_STEP = 10
ADAM_C1 = 1.0 - ADAM_B1 ** ADAM_STEP
ADAM_C2 = 1.0 - ADAM_B2 ** ADAM_STEP

VMEM_LIMIT_BYTES = 56 * 1024 * 1024
BF16_ROWS = 16

GROUP_SEGS = {
    "ffn1": (("ffn1_g", 352), ("ffn1_u", 352), ("ffn1_d", 352)),
    "mid": (("w_in", 128), ("w_out", 128), ("w_mq", 128), ("w_mo", 128), ("w_mkv", 256), ("w_q", 32), ("w_kv", 16)),
    "ffn2": (("ffn2_g", 352), ("ffn2_u", 352), ("ffn2_d", 352)),
}
SEG_OFF = {}
GROUP_ROWS = {}
for _g, _segs in GROUP_SEGS.items():
    _o = 0
    for _n, _r in _segs:
        SEG_OFF[_n] = (_o, _r)
        _o += _r
    GROUP_ROWS[_g] = _o

SMALL_ROWS = (("ffn1_norm", 8), ("mix_norm", 8), ("q_norm", 8), ("kv_norm", 8), ("pool_w", 512), ("pool_scale", 8),
              ("xattn_norm", 8), ("mem_norm", 8), ("ffn2_norm", 8), ("final_norm", 8), ("loss", 8))
SMALL_OFF = {}
_o = 0
for _n, _r in SMALL_ROWS:
    SMALL_OFF[_n] = (_o, _r)
    _o += _r


def _cparams(**kw):
    return pltpu.CompilerParams(vmem_limit_bytes=VMEM_LIMIT_BYTES, **kw)


def _row_tile(rows, limit):
    best = None
    for cand in range(BF16_ROWS, min(rows, limit) + 1, BF16_ROWS):
        if rows % cand == 0:
            best = cand
    assert best is not None, rows
    return best


def _dot_nn(a, b):
    return lax.dot_general(a, b, (((1,), (0,)), ((), ())), preferred_element_type=F32)


def _dot_nt(a, b):
    return lax.dot_general(a, b, (((1,), (1,)), ((), ())), preferred_element_type=F32)


def _dot_tn(a, b):
    return lax.dot_general(a, b, (((0,), (0,)), ((), ())), preferred_element_type=F32)


def _rms_fwd(x, g):
    r = lax.rsqrt(jnp.mean(x * x, axis=-1, keepdims=True) + RMS_EPS)
    return x * r * g, r


def _rms_bwd(dy, x, g, r):
    xhat = x * r
    dyg = dy * g
    dx = r * (dyg - xhat * jnp.mean(dyg * xhat, axis=-1, keepdims=True))
    dg = jnp.sum(dy * xhat, axis=0, keepdims=True)
    return dx, dg


def _accumulate(ref, val, first):
    if isinstance(first, bool):
        if first:
            ref[...] = val
        else:
            ref[...] += val
        return

    @pl.when(first)
    def _():
        ref[...] = val

    @pl.when(jnp.logical_not(first))
    def _():
        ref[...] += val


def _call_after(token, body, in_specs, args, **kw):
    if token is not None:
        inner = body
        body = lambda tok_ref, *refs: inner(*refs)
        in_specs = [pl.BlockSpec((8, 128), lambda *_: (0, 0))] + list(in_specs)
        args = (token,) + tuple(args)
    return pl.pallas_call(body, in_specs=in_specs, **kw)(*args)


def _resident(shape):
    return pl.BlockSpec(shape, lambda *_: (0,) * len(shape), pipeline_mode=pl.Buffered(1))


def _rope_tables(pos_col, tab):
    ang = pos_col.astype(F32) * tab[0:1, :]
    return jnp.cos(ang), jnp.sin(ang) * tab[1:2, :]


def _swap_halves(x):
    lane = lax.broadcasted_iota(jnp.int32, x.shape, 1)
    return jnp.where((lane % 64) < 32, pltpu.roll(x, 96, 1), pltpu.roll(x, 32, 1))


def _rope_apply(x, cos_t, sin_t):
    return x * cos_t + _swap_halves(x) * sin_t


def _rope_apply_t(dy, cos_t, sin_t):
    return dy * cos_t + _swap_halves(dy * sin_t)


def _ffn_fwd(h, g, wg_t, wu_t, wd, name, token=None, head=None):
    t, d = h.shape
    f = wg_t.shape[0]
    tm, tf = min(512, t), 256
    nf = f // tf
    n_in = 5 if head is None else 7

    def body(*refs):
        h_ref, g_ref, wg_ref, wu_ref, wd_ref = refs[:5]
        ho_ref, n_ref, gate_ref, up_ref = refs[n_in:n_in + 4]
        nb_sc, acc_sc = refs[-2:]
        y, _ = _rms_fwd(h_ref[...], g_ref[...])
        nb = y.astype(BF16)
        nb_sc[...] = nb
        n_ref[...] = nb
        acc_sc[...] = jnp.zeros_like(acc_sc)

        def f_tile(j):
            rows = pl.ds(pl.multiple_of(j * tf, tf), tf)
            nb = nb_sc[...]
            gt = _dot_nt(nb, wg_ref[rows, :])
            ut = _dot_nt(nb, wu_ref[rows, :])
            gate_ref[j] = gt.astype(BF16)
            up_ref[j] = ut.astype(BF16)
            act = (gt * jax.nn.sigmoid(gt)) * ut
            return _dot_nn(act.astype(BF16), wd_ref[rows, :])

        def pair(p, carry):
            acc_sc[...] += f_tile(2 * p) + f_tile(2 * p + 1)
            return carry

        lax.fori_loop(0, nf // 2, pair, 0)
        if nf % 2:
            acc_sc[...] += f_tile(nf - 1)
        ho = h_ref[...] + 0.5 * acc_sc[...]
        if head is None:
            ho_ref[...] = ho
            return
        t_ref, gf_ref = refs[5:7]
        loss_ref, dgf_ref = refs[n_in + 4:n_in + 6]
        gg = gf_ref[...]
        y, r = _rms_fwd(ho, gg)
        err = y - t_ref[...]
        part = 0.5 * jnp.sum(jnp.mean(err * err, axis=-1, keepdims=True), axis=0, keepdims=True)
        dx, dg = _rms_bwd(err * (1.0 / d), ho, gg, r)
        ho_ref[...] = dx
        first = pl.program_id(0) == 0
        _accumulate(loss_ref, jnp.broadcast_to(part, loss_ref.shape), first)
        _accumulate(dgf_ref, dg, first)

    row = pl.BlockSpec((tm, d), lambda i: (i, 0))
    tiles = pl.BlockSpec((nf, tm, tf), lambda i: (0, i, 0))
    in_specs = [row, _resident((1, d)), _resident((f, d)), _resident((f, d)), _resident((f, d))]
    args = (h, g, wg_t, wu_t, wd)
    out_specs = [row, row, tiles, tiles]
    out_shape = [jax.ShapeDtypeStruct((t, d), F32), jax.ShapeDtypeStruct((t, d), BF16),
                 jax.ShapeDtypeStruct((nf, t, tf), BF16), jax.ShapeDtypeStruct((nf, t, tf), BF16)]
    if head is not None:
        in_specs += [row, _resident((1, d))]
        args += tuple(head)
        out_specs += [pl.BlockSpec((8, 128), lambda i: (0, 0)), pl.BlockSpec((1, d), lambda i: (0, 0))]
        out_shape += [jax.ShapeDtypeStruct((8, 128), F32), jax.ShapeDtypeStruct((1, d), F32)]
    return _call_after(
        token, body, in_specs, args, name=name, grid=(t // tm,), out_specs=out_specs, out_shape=out_shape,
        scratch_shapes=[pltpu.VMEM((tm, d), BF16), pltpu.VMEM((tm, d), F32)],
        compiler_params=_cparams(),
    )


def _ffn_bwd_data(dho, h, g, gate, up, wg_t, wu_t, wd, name, token=None):
    t, d = h.shape
    f = wg_t.shape[0]
    tm, tf = min(1024, t), 256
    parts = 2 if tm % 512 == 0 else 1
    tp = tm // parts
    nf = f // tf
    npair, odd = nf // 2, nf % 2
    nsteps = npair + odd

    def body(dho_ref, h_ref, g_ref, gate_ref, up_ref, wg_ref, wu_ref, wd_ref,
             dh_ref, dgate_ref, dup_ref, act_ref, dg_ref, dhb_sc, acc_sc):
        i, j = pl.program_id(0), pl.program_id(1)

        @pl.when(j == 0)
        def _():
            dhb_sc[...] = (0.5 * dho_ref[...]).astype(BF16)
            acc_sc[...] = jnp.zeros_like(acc_sc)

        def slab(ntile):
            cols = pl.ds(0, ntile * tf)
            for r in range(parts):
                rows = pl.ds(r * tp, tp)
                gt = jnp.concatenate([gate_ref[k, rows, :] for k in range(ntile)], axis=-1).astype(F32)
                ut = jnp.concatenate([up_ref[k, rows, :] for k in range(ntile)], axis=-1).astype(F32)
                dact = _dot_nt(dhb_sc[rows, :], wd_ref[cols, :])
                sg = jax.nn.sigmoid(gt)
                silu = gt * sg
                dgb = (dact * ut * (sg * (1.0 + gt * (1.0 - sg)))).astype(BF16)
                dub = (dact * silu).astype(BF16)
                act_ref[rows, cols] = (silu * ut).astype(BF16)
                dgate_ref[rows, cols] = dgb
                dup_ref[rows, cols] = dub
                acc_sc[rows, :] += _dot_nn(dgb, wg_ref[cols, :]) + _dot_nn(dub, wu_ref[cols, :])

        pl.when(j < npair)(lambda: slab(2))
        if odd:
            pl.when(j == npair)(lambda: slab(1))

        @pl.when(j == nsteps - 1)
        def _():
            x = h_ref[...]
            gg = g_ref[...]
            _, r = _rms_fwd(x, gg)
            dx, dg = _rms_bwd(acc_sc[...], x, gg, r)
            dh_ref[...] = dho_ref[...] + dx
            _accumulate(dg_ref, dg, i == 0)

    row = pl.BlockSpec((tm, d), lambda i, j: (i, 0))
    acts = pl.BlockSpec((2, tm, tf), lambda i, j: (j, i, 0))
    weights = pl.BlockSpec((2 * tf, d), lambda i, j: (j, 0))
    outs = pl.BlockSpec((tm, 2 * tf), lambda i, j: (i, j))
    padded = jax.ShapeDtypeStruct((t, 2 * tf * nsteps), BF16)
    return _call_after(
        token, body,
        [row, row, pl.BlockSpec((1, d), lambda i, j: (0, 0)), acts, acts, weights, weights, weights],
        (dho, h, g, gate, up, wg_t, wu_t, wd),
        name=name, grid=(t // tm, nsteps),
        out_specs=[row, outs, outs, outs, pl.BlockSpec((1, d), lambda i, j: (0, 0))],
        out_shape=[jax.ShapeDtypeStruct((t, d), F32), padded, padded, padded, jax.ShapeDtypeStruct((1, d), F32)],
        scratch_shapes=[pltpu.VMEM((tm, d), BF16), pltpu.VMEM((tm, d), F32)],
        compiler_params=_cparams(),
    )


def _tn_matmul(a, b, name, scale=1.0, tmm=None, out_dtype=F32, token=None, m=None):
    t = a.shape[0]
    m = a.shape[1] if m is None else m
    n = b.shape[1]
    tmm = m if tmm is None else tmm
    tk = min(1024, t)
    nk = t // tk

    def product(a_ref, b_ref):
        prod = _dot_tn(a_ref[...].astype(BF16), b_ref[...].astype(BF16))
        return prod * scale if scale != 1.0 else prod

    def body_f32(a_ref, b_ref, o_ref):
        _accumulate(o_ref, product(a_ref, b_ref), pl.program_id(1) == 0)

    def body_cast(a_ref, b_ref, o_ref, acc_sc):
        k = pl.program_id(1)
        _accumulate(acc_sc, product(a_ref, b_ref), k == 0)

        @pl.when(k == nk - 1)
        def _():
            o_ref[...] = acc_sc[...].astype(out_dtype)

    direct = out_dtype == F32
    return _call_after(
        token, body_f32 if direct else body_cast,
        [pl.BlockSpec((tk, tmm), lambda i, k: (k, i)),
         pl.BlockSpec((tk, n), lambda i, k: (k, 0))],
        (a, b),
        name=name, grid=(m // tmm, nk),
        out_specs=pl.BlockSpec((tmm, n), lambda i, k: (i, 0)),
        out_shape=jax.ShapeDtypeStruct((m, n), out_dtype),
        scratch_shapes=[] if direct else [pltpu.VMEM((tmm, n), F32)],
        compiler_params=_cparams(),
    )


def _mix_prep(h1, mix_norm, w_in, q_norm, wq_t, kv_norm, wkv, pos, rope_tab, token=None):
    t, d = h1.shape
    tm = min(512, t)

    def body(h_ref, gm_ref, win_ref, gq_ref, wq_ref, gkv_ref, wkv_ref, pos_ref, tab_ref,
             u_ref, z_ref, qn_ref, kvn_ref, q_ref, k_ref, v_ref):
        u, _ = _rms_fwd(h_ref[...], gm_ref[...])
        ub = u.astype(BF16)
        u_ref[...] = ub
        z = _dot_nn(ub, win_ref[...])
        z_ref[...] = z
        cos_t, sin_t = _rope_tables(pos_ref[...], tab_ref[...])
        qn, _ = _rms_fwd(z[:, 0:Q_RANK], gq_ref[...])
        qnb = qn.astype(BF16)
        qn_ref[...] = qnb
        q = _dot_nt(qnb, wq_ref[...])
        kvn, _ = _rms_fwd(z[:, Q_RANK:Q_RANK + KV_RANK], gkv_ref[...])
        kvnb = kvn.astype(BF16)
        kvn_ref[...] = kvnb
        kv = _dot_nn(kvnb, wkv_ref[...])
        k_pe = _rope_apply(z[:, Q_RANK + KV_RANK:Q_RANK + KV_RANK + 128], cos_t, sin_t)
        ones = jnp.ones((tm, V_DIM), F32)
        for hh in range(MLA_HEADS):
            b = hh * HEAD_PAD
            q_pe = _rope_apply(q[:, b + NOPE:b + HEAD_PAD], cos_t, sin_t)
            q_ref[hh] = jnp.concatenate([q[:, b:b + NOPE], q_pe], axis=-1).astype(BF16)
            k_ref[hh] = jnp.concatenate([kv[:, b:b + NOPE], k_pe], axis=-1).astype(BF16)
            v_ref[hh] = jnp.concatenate([kv[:, b + NOPE:b + HEAD_PAD], ones], axis=-1).astype(BF16)

    full = lambda shape: pl.BlockSpec(shape, lambda i: (0,) * len(shape))
    return _call_after(
        token, body,
        [pl.BlockSpec((tm, d), lambda i: (i, 0)), _resident((1, d)), _resident(w_in.shape), _resident((1, Q_RANK)),
         _resident(wq_t.shape), _resident((1, KV_RANK)), _resident(wkv.shape),
         pl.BlockSpec((tm, 1), lambda i: (i, 0)), _resident(rope_tab.shape)],
        (h1, mix_norm, w_in, q_norm, wq_t, kv_norm, wkv, pos, rope_tab),
        name="mix_prep", grid=(t // tm,),
        out_specs=[pl.BlockSpec((tm, d), lambda i: (i, 0)),
                   pl.BlockSpec((tm, d), lambda i: (i, 0)),
                   pl.BlockSpec((tm, Q_RANK), lambda i: (i, 0)),
                   pl.BlockSpec((tm, KV_RANK), lambda i: (i, 0)),
                   pl.BlockSpec((MLA_HEADS, tm, HEAD_PAD), lambda i: (0, i, 0)),
                   pl.BlockSpec((MLA_HEADS, tm, HEAD_PAD), lambda i: (0, i, 0)),
                   pl.BlockSpec((MLA_HEADS, tm, 2 * V_DIM), lambda i: (0, i, 0))],
        out_shape=[jax.ShapeDtypeStruct((t, d), BF16), jax.ShapeDtypeStruct((t, d), F32),
                   jax.ShapeDtypeStruct((t, Q_RANK), BF16), jax.ShapeDtypeStruct((t, KV_RANK), BF16),
                   jax.ShapeDtypeStruct((MLA_HEADS, t, HEAD_PAD), BF16),
                   jax.ShapeDtypeStruct((MLA_HEADS, t, HEAD_PAD), BF16),
                   jax.ShapeDtypeStruct((MLA_HEADS, t, 2 * V_DIM), BF16)],
        compiler_params=_cparams(),
    )


def _causal_mask(s):
    row = lax.broadcasted_iota(jnp.int32, s.shape, 0)
    col = lax.broadcasted_iota(jnp.int32, s.shape, 1)
    return jnp.where(col <= row, s, NEG_BIG)


def _attn_fwd(q, k, v):
    nh, t, _ = q.shape
    tq = tk = min(512, t)
    nq, nk = t // tq, t // tk

    pairs = [(i, j) for i in range(nq) for j in range(i + 1)]
    qi = jnp.asarray(np.array([i for i, _ in pairs], np.int32))
    kj = jnp.asarray(np.array([j for _, j in pairs], np.int32))

    def body(qi_ref, kj_ref, q_ref, k_ref, v_ref, o_ref, lse_ref, m_sc, acc_sc):
        n = pl.program_id(0)
        i, j = qi_ref[n], kj_ref[n]

        @pl.when(j == 0)
        def _():
            m_sc[...] = jnp.full_like(m_sc, NEG_BIG)
            acc_sc[...] = jnp.zeros_like(acc_sc)

        def step(diagonal):
            for hh in range(nh):
                s = _dot_nt(q_ref[hh], k_ref[hh]) * ATTN_SCALE
                if diagonal:
                    s = _causal_mask(s)
                m_old = m_sc[hh]
                m_new = jnp.maximum(m_old, jnp.max(s, axis=-1, keepdims=True))
                p = jnp.exp(s - m_new).astype(BF16)
                acc_sc[hh] = jnp.exp(m_old - m_new) * acc_sc[hh] + _dot_nn(p, v_ref[hh])
                m_sc[hh] = m_new

        @pl.when(j < i)
        def _():
            step(False)

        @pl.when(j == i)
        def _():
            step(True)
            for hh in range(nh):
                acc = acc_sc[hh]
                l = acc[:, V_DIM:2 * V_DIM]
                o_ref[:, hh * V_DIM:(hh + 1) * V_DIM] = (acc[:, 0:V_DIM] / l).astype(BF16)
                lse_ref[hh] = m_sc[hh] + jnp.log(l[:, 0:1])

    q_map = lambda n, qi_ref, kj_ref: (0, qi_ref[n], 0)
    kv_map = lambda n, qi_ref, kj_ref: (0, kj_ref[n], 0)
    return pl.pallas_call(
        body, name="attn_fwd",
        grid_spec=pltpu.PrefetchScalarGridSpec(
            num_scalar_prefetch=2, grid=(len(pairs),),
            in_specs=[pl.BlockSpec((nh, tq, HEAD_PAD), q_map),
                      pl.BlockSpec((nh, tk, HEAD_PAD), kv_map),
                      pl.BlockSpec((nh, tk, 2 * V_DIM), kv_map)],
            out_specs=[pl.BlockSpec((tq, nh * V_DIM), lambda n, qi_ref, kj_ref: (qi_ref[n], 0)),
                       pl.BlockSpec((nh, tq, 1), q_map)],
            scratch_shapes=[pltpu.VMEM((nh, tq, 1), F32), pltpu.VMEM((nh, tq, 2 * V_DIM), F32)]),
        out_shape=[jax.ShapeDtypeStruct((t, nh * V_DIM), BF16), jax.ShapeDtypeStruct((nh, t, 1), F32)],
        compiler_params=_cparams(),
    )(qi, kj, q, k, v)


def _attn_delta(o, do):
    t, w = o.shape
    nh = w // V_DIM
    tm = min(512, t)

    def body(o_ref, do_ref, d_ref):
        prod = o_ref[...].astype(F32) * do_ref[...].astype(F32)
        for hh in range(nh):
            d_ref[hh] = jnp.sum(prod[:, hh * V_DIM:(hh + 1) * V_DIM], axis=-1, keepdims=True)

    return pl.pallas_call(
        body, name="attn_delta", grid=(t // tm,),
        in_specs=[pl.BlockSpec((tm, w), lambda i: (i, 0)), pl.BlockSpec((tm, w), lambda i: (i, 0))],
        out_specs=pl.BlockSpec((nh, tm, 1), lambda i: (0, i, 0)),
        out_shape=jax.ShapeDtypeStruct((nh, t, 1), F32),
        compiler_params=_cparams(),
    )(o, do)


ATTN_BWD_HEADS = 2


def _attn_bwd(q, k, v, do, lse, delta):
    nh, t, _ = q.shape
    hp = ATTN_BWD_HEADS
    tq = tk = min(512, t)
    nq, nk = t // tq, t // tk

    pairs = [(j, i) for j in range(nk) for i in range(j, nq)]
    kj = jnp.asarray(np.array([j for j, _ in pairs], np.int32))
    qi = jnp.asarray(np.array([i for _, i in pairs], np.int32))

    def body(kj_ref, qi_ref, q_ref, k_ref, v_ref, do_ref, lse_ref, dlt_ref, dq_ref, dk_ref, dv_ref):
        n = pl.program_id(1)
        j, i = kj_ref[n], qi_ref[n]

        @pl.when(n == 0)
        def _():
            dq_ref[...] = jnp.zeros_like(dq_ref)

        def step(diagonal):
            for hh in range(hp):
                qq, kk = q_ref[hh], k_ref[hh]
                dob = do_ref[:, hh * V_DIM:(hh + 1) * V_DIM]
                s = _dot_nt(qq, kk) * ATTN_SCALE
                if diagonal:
                    s = _causal_mask(s)
                p = jnp.exp(s - lse_ref[hh])
                dpp = _dot_nt(dob, v_ref[hh])
                dsb = (p * (dpp - dlt_ref[hh]) * ATTN_SCALE).astype(BF16)
                _accumulate(dv_ref.at[hh], _dot_tn(p.astype(BF16), dob), diagonal)
                _accumulate(dk_ref.at[hh], _dot_tn(dsb, qq), diagonal)
                dq_ref[hh, pl.ds(pl.multiple_of(i * tq, tq), tq), :] += _dot_nn(dsb, kk)

        @pl.when(i > j)
        def _():
            step(False)

        @pl.when(i == j)
        def _():
            step(True)

    q_map = lambda h, n, kj_ref, qi_ref: (h, qi_ref[n], 0)
    k_map = lambda h, n, kj_ref, qi_ref: (h, kj_ref[n], 0)
    return pl.pallas_call(
        body, name="attn_bwd",
        grid_spec=pltpu.PrefetchScalarGridSpec(
            num_scalar_prefetch=2, grid=(nh // hp, len(pairs)),
            in_specs=[pl.BlockSpec((hp, tq, HEAD_PAD), q_map),
                      pl.BlockSpec((hp, tk, HEAD_PAD), k_map),
                      pl.BlockSpec((hp, tk, V_DIM), k_map),
                      pl.BlockSpec((tq, hp * V_DIM), lambda h, n, kj_ref, qi_ref: (qi_ref[n], h)),
                      pl.BlockSpec((hp, tq, 1), q_map),
                      pl.BlockSpec((hp, tq, 1), q_map)],
            out_specs=[pl.BlockSpec((hp, t, HEAD_PAD), lambda h, n, kj_ref, qi_ref: (h, 0, 0)),
                       pl.BlockSpec((hp, tk, HEAD_PAD), k_map),
                       pl.BlockSpec((hp, tk, V_DIM), k_map)]),
        out_shape=[jax.ShapeDtypeStruct((nh, t, HEAD_PAD), F32), jax.ShapeDtypeStruct((nh, t, HEAD_PAD), F32),
                   jax.ShapeDtypeStruct((nh, t, V_DIM), F32)],
        compiler_params=_cparams(),
    )(kj, qi, q, k, v, do, lse, delta)


def _pool_counts(first_token, rows, w):
    tok = lax.broadcasted_iota(jnp.int32, (rows, POOL_CH), 0) + first_token
    return jnp.minimum(tok + 1, w).astype(F32)


def _pool_centered(zbuf, g, w, i, tm):
    lanes = pl.ds(g * POOL_CH, POOL_CH)
    cur = zbuf[pl.ds(POOL_HALO, tm), lanes]
    win = cur
    for s in range(1, w):
        win = win + zbuf[pl.ds(POOL_HALO - s, tm), lanes]
    return win / _pool_counts(i * tm, tm, w) - cur


def _pool_load(zbuf, z_ref, halo_ref, i, tm):
    @pl.when(i == 0)
    def _():
        zbuf[pl.ds(0, POOL_HALO), :] = jnp.zeros((POOL_HALO, zbuf.shape[1]), F32)

    @pl.when(i > 0)
    def _():
        zbuf[pl.ds(0, POOL_HALO), :] = halo_ref[...]

    zbuf[pl.ds(POOL_HALO, tm), :] = z_ref[...]


def _pool_fwd(z, pool_w, pool_scale):
    t = z.shape[0]
    pw = len(POOL_WINDOWS) * POOL_CH
    tm = min(512, t)
    hb = tm // POOL_HALO

    def body(z_ref, halo_ref, w_ref, sc_ref, p_ref, zbuf):
        i = pl.program_id(0)
        _pool_load(zbuf, z_ref, halo_ref, i, tm)
        for g, w in enumerate(POOL_WINDOWS):
            c = _pool_centered(zbuf, g, w, i, tm)
            y = _dot_nn(c.astype(BF16), w_ref[g]) * sc_ref[:, g * POOL_CH:(g + 1) * POOL_CH]
            p_ref[:, g * POOL_CH:(g + 1) * POOL_CH] = y.astype(BF16)

    return pl.pallas_call(
        body, name="pool_fwd", grid=(t // tm,),
        in_specs=[pl.BlockSpec((tm, pw), lambda i: (i, 1)),
                  pl.BlockSpec((POOL_HALO, pw), lambda i: (jnp.maximum(i * hb - 1, 0), 1)),
                  pl.BlockSpec(pool_w.shape, lambda i: (0, 0, 0)),
                  pl.BlockSpec((1, pw), lambda i: (0, 0))],
        out_specs=pl.BlockSpec((tm, pw), lambda i: (i, 0)),
        out_shape=jax.ShapeDtypeStruct((t, pw), BF16),
        scratch_shapes=[pltpu.VMEM((POOL_HALO + tm, pw), F32)],
        compiler_params=_cparams(),
    )(z, z, pool_w, pool_scale)


def _pool_bwd(dp, z, pool_w, pool_scale):
    t = z.shape[0]
    ng = len(POOL_WINDOWS)
    pw = ng * POOL_CH
    tm = min(512, t)
    hb = tm // POOL_HALO
    nt = t // tm

    def body(dp_ref, dpn_ref, z_ref, halo_ref, w_ref, sc_ref, dz_ref, dw_ref, dsc_ref, zbuf, dbuf):
        i = pl.program_id(0)
        _pool_load(zbuf, z_ref, halo_ref, i, tm)

        @pl.when(i == 0)
        def _():
            dw_ref[...] = jnp.zeros_like(dw_ref)
            dsc_ref[...] = jnp.zeros_like(dsc_ref)

        nxt_ok = (i < nt - 1).astype(F32)
        for g, w in enumerate(POOL_WINDOWS):
            lanes = pl.ds(g * POOL_CH, POOL_CH)
            cols = slice(g * POOL_CH, (g + 1) * POOL_CH)
            sc = sc_ref[:, cols]
            wg = w_ref[g]
            c = _pool_centered(zbuf, g, w, i, tm).astype(BF16)
            ypre = _dot_nn(c, wg)
            dpg = dp_ref[:, cols].astype(F32)
            dsc_ref[:, cols] += jnp.sum(dpg * ypre, axis=0, keepdims=True)
            dyb = (dpg * sc).astype(BF16)
            dw_ref[g] += _dot_tn(c, dyb)
            dd = _dot_nt(dyb, wg)
            dyn = (dpn_ref[:, cols].astype(F32) * sc).astype(BF16)
            ddn = _dot_nt(dyn, wg) * nxt_ok
            dbuf[pl.ds(0, tm), lanes] = dd / _pool_counts(i * tm, tm, w)
            dbuf[pl.ds(tm, POOL_HALO), lanes] = ddn / _pool_counts((i + 1) * tm, POOL_HALO, w)
            acc = -dd
            for s in range(w):
                acc = acc + dbuf[pl.ds(s, tm), lanes]
            dz_ref[:, cols] = acc

    return pl.pallas_call(
        body, name="pool_bwd", grid=(nt,),
        in_specs=[pl.BlockSpec((tm, pw), lambda i: (i, 0)),
                  pl.BlockSpec((POOL_HALO, pw), lambda i: (jnp.minimum((i + 1) * hb, t // POOL_HALO - 1), 0)),
                  pl.BlockSpec((tm, pw), lambda i: (i, 1)),
                  pl.BlockSpec((POOL_HALO, pw), lambda i: (jnp.maximum(i * hb - 1, 0), 1)),
                  pl.BlockSpec(pool_w.shape, lambda i: (0, 0, 0)),
                  pl.BlockSpec((1, pw), lambda i: (0, 0))],
        out_specs=[pl.BlockSpec((tm, pw), lambda i: (i, 0)),
                   pl.BlockSpec((ng, POOL_CH, POOL_CH), lambda i: (0, 0, 0)),
                   pl.BlockSpec((1, pw), lambda i: (0, 0))],
        out_shape=[jax.ShapeDtypeStruct((t, pw), F32), jax.ShapeDtypeStruct((ng, POOL_CH, POOL_CH), F32),
                   jax.ShapeDtypeStruct((1, pw), F32)],
        scratch_shapes=[pltpu.VMEM((POOL_HALO + tm, pw), F32), pltpu.VMEM((tm + POOL_HALO, pw), F32)],
        compiler_params=_cparams(),
    )(dp, dp, z, z, pool_w, pool_scale)


def _mla_bwd(dq_h, dk_h, dv_h, z, dz_pool, h1, dh2, mix_norm, w_in, q_norm, wq_t, kv_norm, wkv, pos, rope_tab):
    t, d = h1.shape
    tm = min(512, t)

    def body(dqh_ref, dkh_ref, dvh_ref, z_ref, dzp_ref, h_ref, dh2_ref, gm_ref, win_ref, gq_ref, wq_ref, gkv_ref,
             wkv_ref, pos_ref, tab_ref, dh1_ref, dq_ref, dkv_ref, dz_ref, dgq_ref, dgkv_ref, dgm_ref):
        i = pl.program_id(0)
        first = i == 0
        cos_t, sin_t = _rope_tables(pos_ref[...], tab_ref[...])
        dq_parts, dkv_parts = [], []
        dk_pe = jnp.zeros((tm, 128), F32)
        for hh in range(MLA_HEADS):
            dqh = dqh_ref[hh]
            dq_parts += [dqh[:, 0:NOPE], _rope_apply_t(dqh[:, NOPE:HEAD_PAD], cos_t, sin_t)]
            dkh = dkh_ref[hh]
            dkv_parts += [dkh[:, 0:NOPE], dvh_ref[hh]]
            dk_pe = dk_pe + dkh[:, NOPE:HEAD_PAD]
        dqb = jnp.concatenate(dq_parts, axis=-1).astype(BF16)
        dkvb = jnp.concatenate(dkv_parts, axis=-1).astype(BF16)
        dq_ref[...] = dqb
        dkv_ref[...] = dkvb
        z = z_ref[...]
        c_q = z[:, 0:Q_RANK]
        gq = gq_ref[...]
        _, rq = _rms_fwd(c_q, gq)
        dcq, dgq = _rms_bwd(_dot_nn(dqb, wq_ref[...]), c_q, gq, rq)
        c_kv = z[:, Q_RANK:Q_RANK + KV_RANK]
        gkv = gkv_ref[...]
        _, rkv = _rms_fwd(c_kv, gkv)
        dckv, dgkv = _rms_bwd(_dot_nt(dkvb, wkv_ref[...]), c_kv, gkv, rkv)
        dkr = _rope_apply_t(dk_pe, cos_t, sin_t)
        dzb = jnp.concatenate([dcq, dckv, dkr, dzp_ref[...]], axis=-1).astype(BF16)
        dz_ref[...] = dzb
        x = h_ref[...]
        gm = gm_ref[...]
        _, rm = _rms_fwd(x, gm)
        dx, dgm = _rms_bwd(_dot_nt(dzb, win_ref[...]), x, gm, rm)
        dh1_ref[...] = dh2_ref[...] + dx
        _accumulate(dgq_ref, dgq, first)
        _accumulate(dgkv_ref, dgkv, first)
        _accumulate(dgm_ref, dgm, first)

    full = lambda shape: pl.BlockSpec(shape, lambda i: (0,) * len(shape))
    row = lambda w: pl.BlockSpec((tm, w), lambda i: (i, 0))
    head = lambda w: pl.BlockSpec((MLA_HEADS, tm, w), lambda i: (0, i, 0))
    pw = len(POOL_WINDOWS) * POOL_CH
    return pl.pallas_call(
        body, name="mla_bwd", grid=(t // tm,),
        in_specs=[head(HEAD_PAD), head(HEAD_PAD), head(V_DIM), row(d), row(pw), row(d), row(d),
                  _resident((1, d)), _resident(w_in.shape), _resident((1, Q_RANK)), _resident(wq_t.shape),
                  _resident((1, KV_RANK)), _resident(wkv.shape), row(1), _resident(rope_tab.shape)],
        out_specs=[row(d), row(d), row(d), row(d), full((1, Q_RANK)), full((1, KV_RANK)), full((1, d))],
        out_shape=[jax.ShapeDtypeStruct((t, d), F32), jax.ShapeDtypeStruct((t, d), BF16),
                   jax.ShapeDtypeStruct((t, d), BF16), jax.ShapeDtypeStruct((t, d), BF16),
                   jax.ShapeDtypeStruct((1, Q_RANK), F32), jax.ShapeDtypeStruct((1, KV_RANK), F32),
                   jax.ShapeDtypeStruct((1, d), F32)],
        compiler_params=_cparams(),
    )(dq_h, dk_h, dv_h, z, dz_pool, h1, dh2, mix_norm, w_in, q_norm, wq_t, kv_norm, wkv, pos, rope_tab)


def _mem_kv(mem, mem_norm, wmkv):
    n, d = mem.shape

    def body(mem_ref, g_ref, w_ref, memn_ref, k_ref, v_ref):
        y, _ = _rms_fwd(mem_ref[...], g_ref[...])
        yb = y.astype(BF16)
        memn_ref[...] = yb
        for hh in range(MEM_HEADS):
            k_ref[hh] = _dot_nn(yb, w_ref[hh]).astype(BF16)
            v_ref[hh] = _dot_nn(yb, w_ref[MEM_HEADS + hh]).astype(BF16)

    return pl.pallas_call(
        body, name="mem_kv",
        out_shape=[jax.ShapeDtypeStruct((n, d), BF16), jax.ShapeDtypeStruct((MEM_HEADS, n, MEM_HD), BF16),
                   jax.ShapeDtypeStruct((MEM_HEADS, n, MEM_HD), BF16)],
        compiler_params=_cparams(),
    )(mem, mem_norm, wmkv)


def _mem_softmax(qb, km):
    s = _dot_nt(qb, km) * MEM_SCALE
    e = jnp.exp(s - jnp.max(s, axis=-1, keepdims=True))
    return e / jnp.sum(e, axis=-1, keepdims=True)


def _xattn_fwd(h1, a, p, w_out, g, wmq, km, vm, wmo, token=None):
    t, d = h1.shape
    tm = min(512, t)
    half = a.shape[1]

    def body(h_ref, a_ref, p_ref, wo_ref, g_ref, wmq_ref, km_ref, vm_ref, wmo_ref,
             h2_ref, h3_ref, hn_ref, q_ref, o_ref):
        h2 = h_ref[...] + _dot_nn(a_ref[...], wo_ref[0:half, :]) + _dot_nn(p_ref[...], wo_ref[half:2 * half, :])
        h2_ref[...] = h2
        hn, _ = _rms_fwd(h2, g_ref[...])
        hnb = hn.astype(BF16)
        hn_ref[...] = hnb
        qb = _dot_nn(hnb, wmq_ref[...]).astype(BF16)
        q_ref[...] = qb
        outs = []
        for hh in range(MEM_HEADS):
            pr = _mem_softmax(qb[:, hh * MEM_HD:(hh + 1) * MEM_HD], km_ref[hh])
            outs.append(_dot_nn(pr.astype(BF16), vm_ref[hh]))
        ob = jnp.concatenate(outs, axis=-1).astype(BF16)
        o_ref[...] = ob
        h3_ref[...] = h2 + _dot_nn(ob, wmo_ref[...])

    full = lambda shape: pl.BlockSpec(shape, lambda i: (0,) * len(shape))
    row = lambda w: pl.BlockSpec((tm, w), lambda i: (i, 0))
    return _call_after(
        token, body,
        [row(d), row(half), row(half), _resident(w_out.shape), _resident((1, d)), _resident(wmq.shape),
         _resident(km.shape), _resident(vm.shape), _resident(wmo.shape)],
        (h1, a, p, w_out, g, wmq, km, vm, wmo),
        name="xattn_fwd", grid=(t // tm,),
        out_specs=[row(d), row(d), row(d), row(d), row(d)],
        out_shape=[jax.ShapeDtypeStruct((t, d), F32), jax.ShapeDtypeStruct((t, d), F32),
                   jax.ShapeDtypeStruct((t, d), BF16), jax.ShapeDtypeStruct((t, d), BF16),
                   jax.ShapeDtypeStruct((t, d), BF16)],
        compiler_params=_cparams(),
    )


def _xattn_bwd(dh3, h2, qm, g, wmq, km, vm, wmo, w_out, token=None):
    t, d = h2.shape
    tm = min(512, t)
    half = d // 2

    def body(dh3_ref, h2_ref, q_ref, g_ref, wmq_ref, km_ref, vm_ref, wmo_ref, wo_ref,
             dh2_ref, dq_ref, da_ref, dp_ref, dk_ref, dv_ref, dg_ref):
        i = pl.program_id(0)
        first = i == 0

        @pl.when(first)
        def _():
            dk_ref[...] = jnp.zeros_like(dk_ref)
            dv_ref[...] = jnp.zeros_like(dv_ref)

        dh3 = dh3_ref[...]
        dob = _dot_nt(dh3.astype(BF16), wmo_ref[...]).astype(BF16)
        qb = q_ref[...]
        dq_parts = []
        for hh in range(MEM_HEADS):
            cols = slice(hh * MEM_HD, (hh + 1) * MEM_HD)
            kk, vv = km_ref[hh], vm_ref[hh]
            pr = _mem_softmax(qb[:, cols], kk)
            doh = dob[:, cols]
            dv_ref[hh] += _dot_tn(pr.astype(BF16), doh)
            dpp = _dot_nt(doh, vv)
            dsb = (pr * (dpp - jnp.sum(dpp * pr, axis=-1, keepdims=True)) * MEM_SCALE).astype(BF16)
            dq_parts.append(_dot_nn(dsb, kk))
            dk_ref[hh] += _dot_tn(dsb, qb[:, cols])
        dqb = jnp.concatenate(dq_parts, axis=-1).astype(BF16)
        dq_ref[...] = dqb
        x = h2_ref[...]
        gg = g_ref[...]
        _, r = _rms_fwd(x, gg)
        dx, dg = _rms_bwd(_dot_nt(dqb, wmq_ref[...]), x, gg, r)
        dh2 = dh3 + dx
        dh2_ref[...] = dh2
        dap = _dot_nt(dh2.astype(BF16), wo_ref[...])
        da_ref[...] = dap[:, 0:half].astype(BF16)
        dp_ref[...] = dap[:, half:d].astype(BF16)
        _accumulate(dg_ref, dg, first)

    full = lambda shape: pl.BlockSpec(shape, lambda i: (0,) * len(shape))
    row = lambda w: pl.BlockSpec((tm, w), lambda i: (i, 0))
    return _call_after(
        token, body,
        [row(d), row(d), row(d), _resident((1, d)), _resident(wmq.shape), _resident(km.shape), _resident(vm.shape),
         _resident(wmo.shape), _resident(w_out.shape)],
        (dh3, h2, qm, g, wmq, km, vm, wmo, w_out),
        name="xattn_bwd", grid=(t // tm,),
        out_specs=[row(d), row(d), row(half), row(half), full(km.shape), full(vm.shape), full((1, d))],
        out_shape=[jax.ShapeDtypeStruct((t, d), F32), jax.ShapeDtypeStruct((t, d), BF16),
                   jax.ShapeDtypeStruct((t, half), BF16), jax.ShapeDtypeStruct((t, half), BF16),
                   jax.ShapeDtypeStruct(km.shape, F32), jax.ShapeDtypeStruct(vm.shape, F32),
                   jax.ShapeDtypeStruct((1, d), F32)],
        compiler_params=_cparams(),
    )


def _mem_kv_bwd(dkm, dvm, memn, mem, mem_norm, wmkv):
    n, d = mem.shape

    def body(dk_ref, dv_ref, memn_ref, mem_ref, g_ref, w_ref, dw_ref, dg_ref):
        memn = memn_ref[...]
        dmemn = jnp.zeros((n, d), F32)
        for s in range(2 * MEM_HEADS):
            src = dk_ref[s] if s < MEM_HEADS else dv_ref[s - MEM_HEADS]
            db = src.astype(BF16)
            dw_ref[s] = _dot_tn(memn, db)
            dmemn = dmemn + _dot_nt(db, w_ref[s])
        x = mem_ref[...]
        gg = g_ref[...]
        _, r = _rms_fwd(x, gg)
        _, dg = _rms_bwd(dmemn, x, gg, r)
        dg_ref[...] = dg

    return pl.pallas_call(
        body, name="mem_kv_bwd",
        out_shape=[jax.ShapeDtypeStruct(wmkv.shape, F32), jax.ShapeDtypeStruct((1, d), F32)],
        compiler_params=_cparams(),
    )(dkm, dvm, memn, mem, mem_norm, wmkv)


MESH_ID = pl.DeviceIdType.MESH
ANY = pl.BlockSpec(memory_space=pl.ANY)


def _coords():
    return lax.axis_index("x"), lax.axis_index("y"), lax.axis_index("c")


def _other_chips(x, y):
    return [(1 - x, y), (x, 1 - y), (1 - x, 1 - y)]


def _core_reduce(g, tag):
    _, r, w = g.shape

    def body(g_ref, part_ref, own_sc, recv_sc, send_sems, recv_sems, local_sems):
        x, y, c = _coords()
        sent, local = [], []
        for chip in range(4):
            sent.append(pltpu.make_async_remote_copy(
                src_ref=g_ref.at[2 * chip + (1 - c)], dst_ref=recv_sc.at[chip],
                send_sem=send_sems.at[chip], recv_sem=recv_sems.at[chip],
                device_id=(x, y, 1 - c), device_id_type=MESH_ID))
            local.append(pltpu.make_async_copy(g_ref.at[2 * chip + c], own_sc.at[chip], local_sems.at[chip]))
        for cp in sent + local:
            cp.start()
        for chip in range(4):
            local[chip].wait()
            sent[chip].wait_recv()
            part_ref[chip] = (own_sc[chip].astype(F32) + recv_sc[chip].astype(F32)).astype(part_ref.dtype)
        for cp in sent:
            cp.wait_send()

    return pl.pallas_call(
        body, name="core_reduce_" + tag,
        out_shape=jax.ShapeDtypeStruct((4, r, w), g.dtype),
        in_specs=[ANY], out_specs=pl.BlockSpec(memory_space=pltpu.VMEM),
        scratch_shapes=[pltpu.VMEM((4, r, w), g.dtype), pltpu.VMEM((4, r, w), g.dtype),
                        pltpu.SemaphoreType.DMA((4,)), pltpu.SemaphoreType.DMA((4,)), pltpu.SemaphoreType.DMA((4,))],
        compiler_params=_cparams(),
    )(g)


HBM_SPEC = pl.BlockSpec(memory_space=pltpu.HBM)
SEM_SPEC = pl.BlockSpec(memory_space=pltpu.SEMAPHORE)
SPLIT_EFFECT = pltpu.SideEffectType.DATAFLOW_SIDE_EFFECTING


def _ici_refs(gather, src_ref, land_ref, j, px, py, slot_chip, c):
    if gather:
        return src_ref, land_ref.at[:, 4 * slot_chip[0] + 2 * slot_chip[1] + c]
    return src_ref.at[2 * px + py], land_ref.at[j]


def _ici_start(src, after, name, gather):
    r, w = src.shape[-2:]
    land_shape = (src.shape[0], N_DEV, r, w) if gather else (3, r, w)

    def body(src_ref, land_ref, after_ref, send_sems, recv_sems, src_thru, land_thru, token):
        x, y, c = _coords()
        for j, (px, py) in enumerate(_other_chips(x, y)):
            s_ref, d_ref = _ici_refs(gather, src_ref, land_ref, j, px, py, (x, y), c)
            pltpu.make_async_remote_copy(
                src_ref=s_ref, dst_ref=d_ref, send_sem=send_sems.at[j], recv_sem=recv_sems.at[j],
                device_id=(px, py, c), device_id_type=MESH_ID).start()
        token[...] = jnp.zeros_like(token)

    return pl.pallas_call(
        body, name=name,
        out_shape=(pltpu.SemaphoreType.DMA((3,)), pltpu.SemaphoreType.DMA((3,)), pltpu.HBM(src.shape, src.dtype),
                   pltpu.HBM(land_shape, src.dtype), jax.ShapeDtypeStruct((8, 128), F32)),
        in_specs=(HBM_SPEC, HBM_SPEC, ANY),
        out_specs=(SEM_SPEC, SEM_SPEC, HBM_SPEC, HBM_SPEC, pl.BlockSpec(memory_space=pltpu.VMEM)),
        input_output_aliases={0: 2, 1: 3},
        compiler_params=pltpu.CompilerParams(has_side_effects=SPLIT_EFFECT),
    )(pltpu.with_memory_space_constraint(src, pltpu.HBM),
      pltpu.with_memory_space_constraint(lax.empty(land_shape, src.dtype), pltpu.HBM), after)


def _ici_wait(started, after, name, gather):
    send_sems, recv_sems, src_thru, land_thru, _ = started

    def body(src_ref, land_ref, send_sems, recv_sems, after_ref, src_dead, got_ref):
        x, y, c = _coords()
        for j, (px, py) in enumerate(_other_chips(x, y)):
            s_ref, d_ref = _ici_refs(gather, src_ref, land_ref, j, px, py, (px, py), c)
            copy = pltpu.make_async_remote_copy(
                src_ref=s_ref, dst_ref=d_ref, send_sem=send_sems.at[j], recv_sem=recv_sems.at[j],
                device_id=(px, py, c), device_id_type=MESH_ID)
            copy.wait_send()
            copy.wait_recv()

    return pl.pallas_call(
        body, name=name,
        out_shape=(pltpu.HBM(src_thru.shape, src_thru.dtype), pltpu.HBM(land_thru.shape, land_thru.dtype)),
        in_specs=(HBM_SPEC, HBM_SPEC, SEM_SPEC, SEM_SPEC, ANY),
        out_specs=(HBM_SPEC, HBM_SPEC), input_output_aliases={0: 0, 1: 1},
        compiler_params=pltpu.CompilerParams(has_side_effects=SPLIT_EFFECT),
    )(src_thru, land_thru, send_sems, recv_sems, after)


def _neighbour(k, x, y):
    return (1 - x, y) if k == 0 else (x, 1 - y)


def _slot(ref, px, py, c):
    return ref.at[:, 4 * px + 2 * py + c]


def _near_start(src, after, name):
    land_shape = (src.shape[0], N_DEV) + src.shape[1:]

    def body(src_ref, land_ref, after_ref, send_sems, recv_sems, src_thru, land_thru, token):
        x, y, c = _coords()
        for k in range(2):
            px, py = _neighbour(k, x, y)
            pltpu.make_async_remote_copy(
                src_ref=src_ref, dst_ref=_slot(land_ref, x, y, c), send_sem=send_sems.at[k],
                recv_sem=recv_sems.at[k], device_id=(px, py, c), device_id_type=MESH_ID).start()
        token[...] = jnp.zeros_like(token)

    return pl.pallas_call(
        body, name=name,
        out_shape=(pltpu.SemaphoreType.DMA((2,)), pltpu.SemaphoreType.DMA((2,)), pltpu.HBM(src.shape, src.dtype),
                   pltpu.HBM(land_shape, src.dtype), jax.ShapeDtypeStruct((8, 128), F32)),
        in_specs=(HBM_SPEC, HBM_SPEC, ANY),
        out_specs=(SEM_SPEC, SEM_SPEC, HBM_SPEC, HBM_SPEC, pl.BlockSpec(memory_space=pltpu.VMEM)),
        input_output_aliases={0: 2, 1: 3},
        compiler_params=pltpu.CompilerParams(has_side_effects=SPLIT_EFFECT),
    )(pltpu.with_memory_space_constraint(src, pltpu.HBM),
      pltpu.with_memory_space_constraint(lax.empty(land_shape, src.dtype), pltpu.HBM), after)


def _near_wait(started, after, name):
    send_sems, recv_sems, src_thru, land_thru, _ = started

    def body(src_ref, land_ref, send_sems, recv_sems, after_ref, src_dead, got_ref):
        x, y, c = _coords()
        for k in range(2):
            px, py = _neighbour(k, x, y)
            copy = pltpu.make_async_remote_copy(
                src_ref=src_ref, dst_ref=_slot(land_ref, px, py, c), send_sem=send_sems.at[k],
                recv_sem=recv_sems.at[k], device_id=(px, py, c), device_id_type=MESH_ID)
            copy.wait_send()
            copy.wait_recv()

    return pl.pallas_call(
        body, name=name,
        out_shape=(pltpu.HBM(src_thru.shape, src_thru.dtype), pltpu.HBM(land_thru.shape, land_thru.dtype)),
        in_specs=(HBM_SPEC, HBM_SPEC, SEM_SPEC, SEM_SPEC, ANY),
        out_specs=(HBM_SPEC, HBM_SPEC), input_output_aliases={0: 0, 1: 1},
        compiler_params=pltpu.CompilerParams(has_side_effects=SPLIT_EFFECT),
    )(src_thru, land_thru, send_sems, recv_sems, after)


def _far_refs(land_ref, k, x, y, c, arriving):
    half = land_ref.shape[2] // 2
    rows = pl.ds(k * half, half)
    ox, oy = (1 - x, 1 - y) if arriving else _neighbour(k, x, y)
    return land_ref.at[:, 4 * ox + 2 * oy + c, rows]


def _far_start(land, after, name):
    def body(land_ref, after_ref, send_sems, recv_sems, land_thru, token):
        x, y, c = _coords()
        for k in range(2):
            block = _far_refs(land_ref, k, x, y, c, False)
            px, py = _neighbour(1 - k, x, y)
            pltpu.make_async_remote_copy(
                src_ref=block, dst_ref=block, send_sem=send_sems.at[k], recv_sem=recv_sems.at[k],
                device_id=(px, py, c), device_id_type=MESH_ID).start()
        token[...] = jnp.zeros_like(token)

    return pl.pallas_call(
        body, name=name,
        out_shape=(pltpu.SemaphoreType.DMA((2,)), pltpu.SemaphoreType.DMA((2,)),
                   pltpu.HBM(land.shape, land.dtype), jax.ShapeDtypeStruct((8, 128), F32)),
        in_specs=(HBM_SPEC, ANY),
        out_specs=(SEM_SPEC, SEM_SPEC, HBM_SPEC, pl.BlockSpec(memory_space=pltpu.VMEM)),
        input_output_aliases={0: 2},
        compiler_params=pltpu.CompilerParams(has_side_effects=SPLIT_EFFECT),
    )(pltpu.with_memory_space_constraint(land, pltpu.HBM), after)


def _far_wait(started, after, name):
    send_sems, recv_sems, land_thru, _ = started

    def body(land_ref, send_sems, recv_sems, after_ref, got_ref):
        x, y, c = _coords()
        for k in range(2):
            px, py = _neighbour(1 - k, x, y)
            copy = pltpu.make_async_remote_copy(
                src_ref=_far_refs(land_ref, k, x, y, c, False), dst_ref=_far_refs(land_ref, k, x, y, c, True),
                send_sem=send_sems.at[k], recv_sem=recv_sems.at[k], device_id=(px, py, c), device_id_type=MESH_ID)
            copy.wait_send()
            copy.wait_recv()

    return pl.pallas_call(
        body, name=name,
        out_shape=pltpu.HBM(land_thru.shape, land_thru.dtype),
        in_specs=(HBM_SPEC, SEM_SPEC, SEM_SPEC, ANY),
        out_specs=HBM_SPEC, input_output_aliases={0: 0},
        compiler_params=pltpu.CompilerParams(has_side_effects=SPLIT_EFFECT),
    )(land_thru, send_sems, recv_sems, after)


def _peer(k, x, y, c):
    return x ^ ((k >> 2) & 1), y ^ ((k >> 1) & 1), c ^ (k & 1)


def _peers_start(src, after, name):
    r, w = src.shape
    x, y, c = _coords()
    land = lax.dynamic_update_slice(jnp.zeros((N_DEV, r, w), src.dtype), src[None], (4 * x + 2 * y + c, 0, 0))

    def body(src_ref, land_ref, after_ref, send_sems, recv_sems, src_thru, land_thru, token):
        x, y, c = _coords()
        for k in range(1, N_DEV):
            pltpu.make_async_remote_copy(
                src_ref=src_ref, dst_ref=land_ref.at[4 * x + 2 * y + c],
                send_sem=send_sems.at[k - 1], recv_sem=recv_sems.at[k - 1],
                device_id=_peer(k, x, y, c), device_id_type=MESH_ID).start()
        token[...] = jnp.zeros_like(token)

    return pl.pallas_call(
        body, name=name,
        out_shape=(pltpu.SemaphoreType.DMA((N_DEV - 1,)), pltpu.SemaphoreType.DMA((N_DEV - 1,)),
                   pltpu.HBM(src.shape, src.dtype), pltpu.HBM(land.shape, src.dtype),
                   jax.ShapeDtypeStruct((8, 128), F32)),
        in_specs=(HBM_SPEC, HBM_SPEC, ANY),
        out_specs=(SEM_SPEC, SEM_SPEC, HBM_SPEC, HBM_SPEC, pl.BlockSpec(memory_space=pltpu.VMEM)),
        input_output_aliases={0: 2, 1: 3},
        compiler_params=pltpu.CompilerParams(has_side_effects=SPLIT_EFFECT),
    )(pltpu.with_memory_space_constraint(src, pltpu.HBM), pltpu.with_memory_space_constraint(land, pltpu.HBM), after)


def _peers_wait(started, after, name):
    send_sems, recv_sems, src_thru, land_thru, _ = started

    def body(src_ref, land_ref, send_sems, recv_sems, after_ref, src_dead, got_ref):
        x, y, c = _coords()
        for k in range(1, N_DEV):
            px, py, pc = _peer(k, x, y, c)
            copy = pltpu.make_async_remote_copy(
                src_ref=src_ref, dst_ref=land_ref.at[4 * px + 2 * py + pc],
                send_sem=send_sems.at[k - 1], recv_sem=recv_sems.at[k - 1],
                device_id=(px, py, pc), device_id_type=MESH_ID)
            copy.wait_send()
            copy.wait_recv()

    return pl.pallas_call(
        body, name=name,
        out_shape=(pltpu.HBM(src_thru.shape, src_thru.dtype), pltpu.HBM(land_thru.shape, land_thru.dtype)),
        in_specs=(HBM_SPEC, HBM_SPEC, SEM_SPEC, SEM_SPEC, ANY),
        out_specs=(HBM_SPEC, HBM_SPEC), input_output_aliases={0: 0, 1: 1},
        compiler_params=pltpu.CompilerParams(has_side_effects=SPLIT_EFFECT),
    )(src_thru, land_thru, send_sems, recv_sems, after)


def _share_refs(ref, k, x, y, c, sender_c):
    px, py = ([(x, y)] + _other_chips(x, y))[k]
    return ref.at[:, 4 * px + 2 * py + sender_c]


def _share_start(gathered, after, name):
    def body(g_ref, after_ref, send_sems, recv_sems, g_thru, token):
        x, y, c = _coords()
        for k in range(4):
            slot = _share_refs(g_ref, k, x, y, c, c)
            pltpu.make_async_remote_copy(
                src_ref=slot, dst_ref=slot, send_sem=send_sems.at[k], recv_sem=recv_sems.at[k],
                device_id=(x, y, 1 - c), device_id_type=MESH_ID).start()
        token[...] = jnp.zeros_like(token)

    return pl.pallas_call(
        body, name=name,
        out_shape=(pltpu.SemaphoreType.DMA((4,)), pltpu.SemaphoreType.DMA((4,)),
                   pltpu.HBM(gathered.shape, gathered.dtype), jax.ShapeDtypeStruct((8, 128), F32)),
        in_specs=(HBM_SPEC, ANY),
        out_specs=(SEM_SPEC, SEM_SPEC, HBM_SPEC, pl.BlockSpec(memory_space=pltpu.VMEM)),
        input_output_aliases={0: 2},
        compiler_params=pltpu.CompilerParams(has_side_effects=SPLIT_EFFECT),
    )(pltpu.with_memory_space_constraint(gathered, pltpu.HBM), after)


def _share_wait(started, after, name):
    send_sems, recv_sems, g_thru, _ = started

    def body(g_ref, send_sems, recv_sems, after_ref, got_ref):
        x, y, c = _coords()
        for k in range(4):
            copy = pltpu.make_async_remote_copy(
                src_ref=_share_refs(g_ref, k, x, y, c, c), dst_ref=_share_refs(g_ref, k, x, y, c, 1 - c),
                send_sem=send_sems.at[k], recv_sem=recv_sems.at[k],
                device_id=(x, y, 1 - c), device_id_type=MESH_ID)
            copy.wait_send()
            copy.wait_recv()

    return pl.pallas_call(
        body, name=name,
        out_shape=pltpu.HBM(g_thru.shape, g_thru.dtype),
        in_specs=(HBM_SPEC, SEM_SPEC, SEM_SPEC, ANY),
        out_specs=HBM_SPEC, input_output_aliases={0: 0},
        compiler_params=pltpu.CompilerParams(has_side_effects=SPLIT_EFFECT),
    )(g_thru, send_sems, recv_sems, after)


def _core_share(own, gathered, name):
    def body(own_ref, gin_ref, out_ref, stage, send_sems, recv_sems, local_sem):
        x, y, c = _coords()
        sibling = (x, y, 1 - c)
        chips = [(x, y)] + _other_chips(x, y)
        stage_in = pltpu.make_async_copy(own_ref, stage, local_sem)
        stage_in.start()
        sent, arriving = [], []
        for k, (px, py) in enumerate(chips):
            slot = out_ref.at[:, 4 * px + 2 * py + c]
            sent.append(pltpu.make_async_remote_copy(
                src_ref=own_ref if k == 0 else slot, dst_ref=slot,
                send_sem=send_sems.at[k], recv_sem=recv_sems.at[k], device_id=sibling, device_id_type=MESH_ID))
            arriving.append(pltpu.make_async_remote_copy(
                src_ref=own_ref, dst_ref=out_ref.at[:, 4 * px + 2 * py + (1 - c)],
                send_sem=send_sems.at[k], recv_sem=recv_sems.at[k], device_id=sibling, device_id_type=MESH_ID))
        for cp in sent:
            cp.start()
        stage_in.wait()
        stage_out = pltpu.make_async_copy(stage, out_ref.at[:, 4 * x + 2 * y + c], local_sem)
        stage_out.start()
        for cp in arriving:
            cp.wait_recv()
        for cp in sent:
            cp.wait_send()
        stage_out.wait()

    return pl.pallas_call(
        body, name=name,
        out_shape=jax.ShapeDtypeStruct(gathered.shape, own.dtype),
        in_specs=[ANY, ANY], out_specs=ANY, input_output_aliases={1: 0},
        scratch_shapes=[pltpu.VMEM(own.shape, own.dtype), pltpu.SemaphoreType.DMA((4,)),
                        pltpu.SemaphoreType.DMA((4,)), pltpu.SemaphoreType.DMA],
    )(own, gathered)


def _adamw(w, g, m, v):
    m = ADAM_B1 * m + (1.0 - ADAM_B1) * g
    v = ADAM_B2 * v + (1.0 - ADAM_B2) * (g * g)
    m_hat = m / ADAM_C1
    v_hat = v / ADAM_C2
    delta = -ADAM_LR * (m_hat / (jnp.sqrt(v_hat) + ADAM_EPS) + ADAM_WD * w)
    return delta, m, v


def _adam_big(units, chip_idx, tag, token):
    n = len(units)
    r, wd = units[0][2].shape
    tr, tw = _row_tile(r, 1024), 256

    def body(s_ref, tok_ref, *refs):
        for u in range(n):
            p_ref, l_ref, w_ref, m_ref, v_ref = refs[5 * u:5 * u + 5]
            g_ref, d_ref, mo_ref, vo_ref = refs[5 * n + 4 * u:5 * n + 4 * u + 4]
            g = p_ref[0].astype(F32)
            for j in range(3):
                g = g + l_ref[j].astype(F32)
            delta, mn, vn = _adamw(w_ref[...], g, m_ref[...], v_ref[...])
            g_ref[...] = g
            d_ref[...] = delta
            mo_ref[...] = mn
            vo_ref[...] = vn

    row = pl.BlockSpec((tr, tw), lambda i, j, s: (i, j))
    unit_specs = [pl.BlockSpec((1, tr, tw), lambda i, j, s: (s[0], i, j)),
                  pl.BlockSpec((3, tr, tw), lambda i, j, s: (0, i, j)), row, row, row]
    outs = pl.pallas_call(
        body, name="adam_big_" + tag,
        grid_spec=pltpu.PrefetchScalarGridSpec(
            num_scalar_prefetch=1, grid=(r // tr, wd // tw),
            in_specs=[pl.BlockSpec((8, 128), lambda i, j, s: (0, 0))] + unit_specs * n,
            out_specs=[row] * (4 * n)),
        out_shape=[jax.ShapeDtypeStruct((r, wd), F32)] * (4 * n),
        compiler_params=_cparams(),
    )(chip_idx, token, *[a for unit in units for a in unit])
    return [outs[4 * u:4 * u + 4] for u in range(n)]


def _adam_small(parts, w, m, v):
    _, r, wd = parts.shape

    def body(p_ref, w_ref, m_ref, v_ref, g_ref, d_ref, mo_ref, vo_ref):
        g = p_ref[0]
        for k in range(1, N_DEV):
            g = g + p_ref[k]
        delta, mn, vn = _adamw(w_ref[...], g, m_ref[...], v_ref[...])
        g_ref[...] = g
        d_ref[...] = delta
        mo_ref[...] = mn
        vo_ref[...] = vn

    return pl.pallas_call(
        body, name="adam_small",
        out_shape=[jax.ShapeDtypeStruct((r, wd), F32)] * 4,
        compiler_params=_cparams(),
    )(parts, w, m, v)


def _pad_rows(a, rows):
    return jnp.pad(a, ((0, rows - a.shape[0]), (0, 0)))


def _pad_w_in(w):
    cut = Q_RANK + KV_RANK + ROPE
    return jnp.concatenate([w[:, :cut], jnp.zeros((w.shape[0], 64), w.dtype), w[:, cut:]], axis=1)


def _unpad_w_in(w):
    cut = Q_RANK + KV_RANK + ROPE
    return jnp.concatenate([w[:, :cut], w[:, cut + 64:]], axis=1)


def _pack_mid(p):
    parts = [_pad_w_in(p["w_in"][0]), p["w_out"][0], p["w_mq"][0], p["w_mo"][0],
             p["w_mkv"][0].reshape(256, D_MODEL),
             _pad_rows(p["w_q_up"][0].T.reshape(24, D_MODEL), 32),
             p["w_kv_up"][0].reshape(16, D_MODEL)]
    return jnp.concatenate(parts, axis=0)


def _pack_ffn(w_gate, w_up, w_down, name):
    d, rows = w_gate.shape[1:]

    def body(g_ref, u_ref, d_ref, o_ref):
        eye = (lax.broadcasted_iota(jnp.int32, (d, d), 0) == lax.broadcasted_iota(jnp.int32, (d, d), 1)).astype(BF16)
        o_ref[0] = _dot_tn(g_ref[0].astype(BF16), eye).astype(BF16)
        o_ref[1] = _dot_tn(u_ref[0].astype(BF16), eye).astype(BF16)
        o_ref[2] = d_ref[0].astype(BF16)

    return pl.pallas_call(
        body, name=name, out_shape=jax.ShapeDtypeStruct((3, rows, d), BF16), compiler_params=_cparams(),
    )(w_gate, w_up, w_down)


def _pack_segments(p, group):
    if group == "mid":
        return _pack_mid(p)[None].astype(BF16)
    return _pack_ffn(p[group + "_w_gate"], p[group + "_w_up"], p[group + "_w_down"], "pack_" + group)


UNIT_WEIGHT = {"ffn1_g": ("ffn1_w_gate", True), "ffn1_u": ("ffn1_w_up", True), "ffn1_d": ("ffn1_w_down", False),
               "ffn2_g": ("ffn2_w_gate", True), "ffn2_u": ("ffn2_w_up", True), "ffn2_d": ("ffn2_w_down", False)}


def _pack_unit(p, unit):
    if unit == "mid":
        return _pack_mid(p)
    name, transposed = UNIT_WEIGHT[unit]
    return p[name][0].T if transposed else p[name][0]


def _unpack_unit(a, unit):
    if unit != "mid":
        name, transposed = UNIT_WEIGHT[unit]
        return {name: (a.T if transposed else a)[None]}
    seg = lambda n: a[SEG_OFF[n][0]:SEG_OFF[n][0] + SEG_OFF[n][1]]
    return {"w_in": _unpad_w_in(seg("w_in"))[None], "w_out": seg("w_out")[None], "w_mq": seg("w_mq")[None],
            "w_mo": seg("w_mo")[None], "w_mkv": seg("w_mkv").reshape(D_MODEL, 256)[None],
            "w_q_up": seg("w_q")[:24].reshape(96, Q_RANK).T[None],
            "w_kv_up": seg("w_kv").reshape(KV_RANK, 128)[None]}


def _unpack_gathered(full, group):
    if group != "mid":
        return {n: full[k].reshape(-1, D_MODEL) for k, (n, _) in enumerate(GROUP_SEGS[group])}
    full = full[0]
    seg = lambda n: full[:, SEG_OFF[n][0]:SEG_OFF[n][0] + SEG_OFF[n][1]]
    rows = lambda n: seg(n).reshape(-1, D_MODEL)
    wq_t = seg("w_q")[:, :24].reshape(MLA_HEADS, NOPE + ROPE, Q_RANK)
    wq_t = jnp.pad(wq_t, ((0, 0), (0, HEAD_PAD - NOPE - ROPE), (0, 0))).reshape(MLA_HEADS * HEAD_PAD, Q_RANK)
    wkv = seg("w_kv").reshape(N_DEV, KV_RANK, 128).transpose(1, 0, 2).reshape(KV_RANK, N_DEV * 128)
    return {"w_in": rows("w_in"), "w_out": rows("w_out"), "w_mq": rows("w_mq"), "w_mo": rows("w_mo"),
            "w_mkv": seg("w_mkv").reshape(N_DEV, D_MODEL, 256), "w_q": wq_t, "w_kv": wkv}


def _pack_grads(gr):
    blk = lambda a: a.reshape(N_DEV, -1, D_MODEL)
    dwq = gr["w_q"].reshape(MLA_HEADS, HEAD_PAD, Q_RANK)[:, :NOPE + ROPE].reshape(N_DEV, 24, D_MODEL)
    dwq = jnp.pad(dwq, ((0, 0), (0, 8), (0, 0)))
    dwkv = gr["w_kv"].reshape(KV_RANK, N_DEV, 128).transpose(1, 0, 2).reshape(N_DEV, 16, D_MODEL)
    parts = [blk(gr["w_in"]), blk(gr["w_out"]), blk(gr["w_mq"]), blk(gr["w_mo"]),
             gr["w_mkv"].reshape(N_DEV, 256, D_MODEL), dwq, dwkv]
    return jnp.concatenate([a.astype(BF16) for a in parts], axis=1)


def _pack_small(vals):
    parts = []
    for n, r in SMALL_ROWS:
        parts.append(_pad_rows(vals[n].reshape(-1, 128), r) if n in vals else jnp.zeros((r, 128), F32))
    return jnp.concatenate(parts, axis=0)


def _unpack_small(a, shapes):
    out = {}
    for n, shape in shapes.items():
        o = SMALL_OFF[n][0]
        out[n] = a[o:o + int(np.prod(shape)) // 128].reshape(shape)
    return out


BIG_NAMES = ("ffn1_w_gate", "ffn1_w_up", "ffn1_w_down", "w_in", "w_q_up", "w_kv_up", "w_out", "w_mq", "w_mkv",
             "w_mo", "ffn2_w_gate", "ffn2_w_up", "ffn2_w_down")
SMALL_NAMES = ("ffn1_norm", "mix_norm", "q_norm", "kv_norm", "pool_w", "pool_scale", "xattn_norm", "mem_norm",
               "ffn2_norm", "final_norm")
WEIGHT_ORDER = ("ffn1_norm", "ffn1_w_gate", "ffn1_w_up", "ffn1_w_down", "mix_norm", "w_in", "q_norm", "w_q_up",
                "kv_norm", "w_kv_up", "pool_w", "pool_scale", "w_out", "xattn_norm", "mem_norm", "w_mq", "w_mkv",
                "w_mo", "ffn2_norm", "ffn2_w_gate", "ffn2_w_up", "ffn2_w_down", "final_norm")


def _rope_table():
    lane = np.arange(128)
    freqs = (1.0 / (ROPE_BASE ** (np.arange(0, ROPE, 2, dtype=np.float32) / ROPE))).astype(np.float32)
    tab = np.zeros((8, 128), np.float32)
    tab[0] = np.where(lane < ROPE, freqs[lane % (ROPE // 2)], 0.0)
    tab[1] = np.where(lane < ROPE // 2, -1.0, np.where(lane < ROPE, 1.0, 0.0))
    return jnp.asarray(tab)


def kernel(x, mem, positions, ffn1_norm, ffn1_w_gate, ffn1_w_up, ffn1_w_down, mix_norm, w_in, q_norm, w_q_up, kv_norm, w_kv_up, pool_w, pool_scale, w_out, xattn_norm, mem_norm, w_mq, w_mkv, w_mo, ffn2_norm, ffn2_w_gate, ffn2_w_up, ffn2_w_down, final_norm, loss_target, m_ffn1_norm, m_ffn1_w_gate, m_ffn1_w_up, m_ffn1_w_down, m_mix_norm, m_w_in, m_q_norm, m_w_q_up, m_kv_norm, m_w_kv_up, m_pool_w, m_pool_scale, m_w_out, m_xattn_norm, m_mem_norm, m_w_mq, m_w_mkv, m_w_mo, m_ffn2_norm, m_ffn2_w_gate, m_ffn2_w_up, m_ffn2_w_down, m_final_norm, v_ffn1_norm, v_ffn1_w_gate, v_ffn1_w_up, v_ffn1_w_down, v_mix_norm, v_w_in, v_q_norm, v_w_q_up, v_kv_norm, v_w_kv_up, v_pool_w, v_pool_scale, v_w_out, v_xattn_norm, v_mem_norm, v_w_mq, v_w_mkv, v_w_mo, v_ffn2_norm, v_ffn2_w_gate, v_ffn2_w_up, v_ffn2_w_down, v_final_norm):
    wts = dict(ffn1_norm=ffn1_norm, ffn1_w_gate=ffn1_w_gate, ffn1_w_up=ffn1_w_up, ffn1_w_down=ffn1_w_down,
               mix_norm=mix_norm, w_in=w_in, q_norm=q_norm, w_q_up=w_q_up, kv_norm=kv_norm, w_kv_up=w_kv_up,
               pool_w=pool_w, pool_scale=pool_scale, w_out=w_out, xattn_norm=xattn_norm, mem_norm=mem_norm,
               w_mq=w_mq, w_mkv=w_mkv, w_mo=w_mo, ffn2_norm=ffn2_norm, ffn2_w_gate=ffn2_w_gate,
               ffn2_w_up=ffn2_w_up, ffn2_w_down=ffn2_w_down, final_norm=final_norm)
    mom = dict(ffn1_norm=m_ffn1_norm, ffn1_w_gate=m_ffn1_w_gate, ffn1_w_up=m_ffn1_w_up, ffn1_w_down=m_ffn1_w_down,
               mix_norm=m_mix_norm, w_in=m_w_in, q_norm=m_q_norm, w_q_up=m_w_q_up, kv_norm=m_kv_norm,
               w_kv_up=m_w_kv_up, pool_w=m_pool_w, pool_scale=m_pool_scale, w_out=m_w_out, xattn_norm=m_xattn_norm,
               mem_norm=m_mem_norm, w_mq=m_w_mq, w_mkv=m_w_mkv, w_mo=m_w_mo, ffn2_norm=m_ffn2_norm,
               ffn2_w_gate=m_ffn2_w_gate, ffn2_w_up=m_ffn2_w_up, ffn2_w_down=m_ffn2_w_down, final_norm=m_final_norm)
    var = dict(ffn1_norm=v_ffn1_norm, ffn1_w_gate=v_ffn1_w_gate, ffn1_w_up=v_ffn1_w_up, ffn1_w_down=v_ffn1_w_down,
               mix_norm=v_mix_norm, w_in=v_w_in, q_norm=v_q_norm, w_q_up=v_w_q_up, kv_norm=v_kv_norm,
               w_kv_up=v_w_kv_up, pool_w=v_pool_w, pool_scale=v_pool_scale, w_out=v_w_out, xattn_norm=v_xattn_norm,
               mem_norm=v_mem_norm, w_mq=v_w_mq, w_mkv=v_w_mkv, w_mo=v_w_mo, ffn2_norm=v_ffn2_norm,
               ffn2_w_gate=v_ffn2_w_gate, ffn2_w_up=v_ffn2_w_up, ffn2_w_down=v_ffn2_w_down, final_norm=v_final_norm)

    t = x.shape[1]
    xs = x[0]
    mems = mem[0]
    target = loss_target[0]
    pos = positions.reshape(t, 1)
    row = lambda a: a.reshape(1, -1)
    rope_tab = _rope_table()

    cx, cy, cc = _coords()
    chip_idx = (2 * cx + cy).astype(jnp.int32).reshape(1)

    wb = {}
    for grp in ("ffn1", "mid", "ffn2"):
        wb[grp] = _pack_segments(wts, grp)
        if grp == "ffn1":
            near_ffn1 = _near_start(wb["ffn1"], pos, "ag_ffn1_near_start")
    own_ffn1, land_ffn1 = _near_wait(near_ffn1, wb["ffn2"], "ag_ffn1_near_wait")
    far_ffn1 = _far_start(land_ffn1, own_ffn1, "ag_ffn1_far_start")
    land_ffn1 = _far_wait(far_ffn1, wb["mid"], "ag_ffn1_far_wait")
    full_ffn1 = _core_share(own_ffn1, land_ffn1, "ag_ffn1_share")
    fw = _unpack_gathered(full_ffn1, "ffn1")
    ag_mid = _ici_start(wb["mid"], full_ffn1, "ag_mid_start", True)
    g_ffn1, g_mix, g_q, g_kv = row(ffn1_norm), row(mix_norm), row(q_norm), row(kv_norm)
    g_x, g_mem, g_ffn2, g_fin = row(xattn_norm), row(mem_norm), row(ffn2_norm), row(final_norm)
    pool_wb = pool_w[0].astype(BF16)
    pool_sc = row(pool_scale)

    h1, n1, gate1, up1 = _ffn_fwd(xs, g_ffn1, fw["ffn1_g"], fw["ffn1_u"], fw["ffn1_d"], "ffn1_fwd", token=ag_mid[4])
    own_mid, land_mid = _ici_wait(ag_mid, h1, "ag_mid_wait", True)
    full_mid = _core_share(own_mid, land_mid, "ag_mid_share")
    fw.update(_unpack_gathered(full_mid, "mid"))
    ag_ffn2 = _ici_start(wb["ffn2"], full_mid, "ag_ffn2_start", True)
    u, z, qn, kvn, qh, kh, vh = _mix_prep(h1, g_mix, fw["w_in"], g_q, fw["w_q"], g_kv, fw["w_kv"], pos, rope_tab,
                                          token=ag_ffn2[4])
    a, lse = _attn_fwd(qh, kh, vh)
    p = _pool_fwd(z, pool_wb, pool_sc)
    memn, km, vm = _mem_kv(mems, g_mem, fw["w_mkv"])
    own_ffn2, land_ffn2 = _ici_wait(ag_ffn2, a, "ag_ffn2_wait", True)
    land_ffn2 = lax.dynamic_update_slice(land_ffn2, own_ffn2[:, None], (0, 4 * cx + 2 * cy + cc, 0, 0))
    share_ffn2 = _share_start(land_ffn2, a, "ag_ffn2_share_start")
    h2, h3, hn, qm, om = _xattn_fwd(h1, a, p, fw["w_out"], g_x, fw["w_mq"], km, vm, fw["w_mo"], token=share_ffn2[3])
    fw.update(_unpack_gathered(_share_wait(share_ffn2, h3, "ag_ffn2_share_wait"), "ffn2"))
    dh4, n2, gate2, up2, loss_part, dg_fin = _ffn_fwd(h3, g_ffn2, fw["ffn2_g"], fw["ffn2_u"], fw["ffn2_d"],
                                                      "ffn2_fwd", head=(target, g_fin))

    def reduce_start(g8, unit):
        part = _core_reduce(g8, unit)
        return _ici_start(part, g8, "rs_" + unit + "_start", False)

    def by_device(g):
        return g.reshape(N_DEV, -1, D_MODEL)

    rs = {}
    dh3, dgate2, dup2, act2, dg_ffn2 = _ffn_bwd_data(dh4, h3, g_ffn2, gate2, up2, fw["ffn2_g"], fw["ffn2_u"],
                                                     fw["ffn2_d"], "ffn2_bwd")
    rs["ffn2_g"] = reduce_start(by_device(_tn_matmul(dgate2, n2, "ffn2_dwg", tmm=1408, m=D_FF, out_dtype=BF16)), "ffn2_g")
    rs["ffn2_u"] = reduce_start(by_device(_tn_matmul(dup2, n2, "ffn2_dwu", tmm=1408, m=D_FF, out_dtype=BF16,
                                                     token=rs["ffn2_g"][4])), "ffn2_u")
    rs["ffn2_d"] = reduce_start(by_device(_tn_matmul(act2, dh4, "ffn2_dwd", scale=0.5, tmm=1408, m=D_FF, out_dtype=BF16,
                                                     token=rs["ffn2_u"][4])), "ffn2_d")
    dh2, dqm, da, dp, dkm, dvm, dg_x = _xattn_bwd(dh3, h2, qm, g_x, fw["w_mq"], km, vm, fw["w_mo"], fw["w_out"],
                                                  token=rs["ffn2_d"][4])
    gr = {}
    gr["w_mo"] = _tn_matmul(om, dh3, "dw_mo", out_dtype=BF16)
    gr["w_mq"] = _tn_matmul(hn, dqm, "dw_mq", out_dtype=BF16)
    gr["w_out"] = jnp.concatenate([_tn_matmul(a, dh2, "dw_out_a", out_dtype=BF16),
                                   _tn_matmul(p, dh2, "dw_out_p", out_dtype=BF16)], axis=0)
    gr["w_mkv"], dg_mem = _mem_kv_bwd(dkm, dvm, memn, mems, g_mem, fw["w_mkv"])
    dz_pool, d_pool_w, d_pool_sc = _pool_bwd(dp, z, pool_wb, pool_sc)
    dqh, dkh, dvh = _attn_bwd(qh, kh, vh, da, lse, _attn_delta(a, da))
    dh1, dq, dkv, dz, dg_q, dg_kv, dg_mix = _mla_bwd(dqh, dkh, dvh, z, dz_pool, h1, dh2, g_mix, fw["w_in"], g_q,
                                                     fw["w_q"], g_kv, fw["w_kv"], pos, rope_tab)
    gr["w_q"] = _tn_matmul(dq, qn, "dw_q", out_dtype=BF16)
    gr["w_kv"] = _tn_matmul(kvn, dkv, "dw_kv", out_dtype=BF16)
    gr["w_in"] = _tn_matmul(u, dz, "dw_in", out_dtype=BF16)
    g_mid = _pack_grads(gr)
    part_mid = _core_reduce(g_mid, "mid")
    got = {}
    after = part_mid
    for unit in ("ffn2_g", "ffn2_u", "ffn2_d"):
        got[unit] = _ici_wait(rs[unit], after, "rs_" + unit + "_wait", False)
        after = got[unit][1]
    rs["mid"] = _ici_start(part_mid, after, "rs_mid_start", False)
    dx, dgate1, dup1, act1, dg_ffn1 = _ffn_bwd_data(dh1, xs, g_ffn1, gate1, up1, fw["ffn1_g"], fw["ffn1_u"],
                                                    fw["ffn1_d"], "ffn1_bwd", token=rs["mid"][4])
    got["mid"] = _ici_wait(rs["mid"], dx, "rs_mid_wait", False)

    small_g = dict(ffn1_norm=dg_ffn1, mix_norm=dg_mix, q_norm=dg_q, kv_norm=dg_kv, pool_w=d_pool_w,
                   pool_scale=d_pool_sc, xattn_norm=dg_x, mem_norm=dg_mem, ffn2_norm=dg_ffn2, final_norm=dg_fin,
                   loss=loss_part)
    small_ag = _peers_start(_pack_small(small_g), got["mid"][1], "small_ag_start")
    rs["ffn1_g"] = reduce_start(by_device(_tn_matmul(dgate1, n1, "ffn1_dwg", tmm=1408, m=D_FF, out_dtype=BF16,
                                                     token=small_ag[4])), "ffn1_g")
    _, parts = _peers_wait(small_ag, rs["ffn1_g"][4], "small_ag_wait")
    small = _adam_small(parts, _pack_small({n: wts[n] for n in SMALL_NAMES}),
                        _pack_small({n: mom[n] for n in SMALL_NAMES}), _pack_small({n: var[n] for n in SMALL_NAMES}))
    small_sum = small[0]
    loss = small_sum[SMALL_OFF["loss"][0], 0]
    shapes = {n: wts[n].shape for n in SMALL_NAMES}
    small = [_unpack_small(s, shapes) for s in small]

    rs["ffn1_u"] = reduce_start(by_device(_tn_matmul(dup1, n1, "ffn1_dwu", tmm=1408, m=D_FF, out_dtype=BF16,
                                                     token=small_sum)), "ffn1_u")
    rs["ffn1_d"] = reduce_start(by_device(_tn_matmul(act1, dh1, "ffn1_dwd", scale=0.5, tmm=1408, m=D_FF, out_dtype=BF16,
                                                     token=rs["ffn1_u"][4])), "ffn1_d")

    big = {}

    def adam_units(names, token):
        units = [got[u] + (_pack_unit(wts, u), _pack_unit(mom, u), _pack_unit(var, u)) for u in names]
        res = _adam_big(units, chip_idx, "_".join(names), token)
        for u, four in zip(names, res):
            for k, packed in enumerate(four):
                big.setdefault(k, {}).update(_unpack_unit(packed, u))
        return res[-1][0]

    done = adam_units(["mid"], rs["ffn1_d"][4])
    done = adam_units(["ffn2_g", "ffn2_u", "ffn2_d"], done)
    got["ffn1_g"] = _ici_wait(rs["ffn1_g"], done, "rs_ffn1_g_wait", False)
    got["ffn1_u"] = _ici_wait(rs["ffn1_u"], got["ffn1_g"][1], "rs_ffn1_u_wait", False)
    done = adam_units(["ffn1_g", "ffn1_u"], done)
    got["ffn1_d"] = _ici_wait(rs["ffn1_d"], done, "rs_ffn1_d_wait", False)
    adam_units(["ffn1_d"], done)

    outs = [loss, dx[None]]
    for k in range(4):
        for n in WEIGHT_ORDER:
            outs.append(big[k][n] if n in BIG_NAMES else small[k][n])
    return tuple(outs)
```

```python
import numpy as np

import jax
import jax.numpy as jnp
from jax import lax
from jax.experimental import pallas as pl
from jax.experimental.pallas import tpu as pltpu

F32 = jnp.float32
BF16 = jnp.bfloat16

N_DEV = 8
D_MODEL = 1024
D_FF = 2816
MLA_HEADS = 4
NOPE = 128
ROPE = 64
HEAD_PAD = 256
V_DIM = 128
Q_RANK = 256
KV_RANK = 128
POOL_WINDOWS = (2, 4, 8, 16)
POOL_CH = 128
POOL_HALO = 16
N_MEM = 256
MEM_HEADS = 4
MEM_HD = 256
ROPE_BASE = 10000.0
RMS_EPS = 1e-6
ATTN_SCALE = (NOPE + ROPE) ** -0.5
MEM_SCALE = MEM_HD ** -0.5
NEG_BIG = -1e30

ADAM_LR = 0.001
ADAM_B1 = 0.9
ADAM_B2 = 0.999
ADAM_EPS = 1e-08
ADAM_WD = 0.01
ADAM_STEP = 10
ADAM_C1 = 1.0 - ADAM_B1 ** ADAM_STEP
ADAM_C2 = 1.0 - ADAM_B2 ** ADAM_STEP

VMEM_LIMIT_BYTES = 56 * 1024 * 1024
BF16_ROWS = 16

GROUP_SEGS = {
    "ffn1": (("ffn1_g", 352), ("ffn1_u", 352), ("ffn1_d", 352)),
    "mid": (("w_in", 128), ("w_out", 128), ("w_mq", 128), ("w_mo", 128), ("w_mkv", 256), ("w_q", 32), ("w_kv", 16)),
    "ffn2": (("ffn2_g", 352), ("ffn2_u", 352), ("ffn2_d", 352)),
}
SEG_OFF = {}
GROUP_ROWS = {}
for _g, _segs in GROUP_SEGS.items():
    _o = 0
    for _n, _r in _segs:
        SEG_OFF[_n] = (_o, _r)
        _o += _r
    GROUP_ROWS[_g] = _o

SMALL_ROWS = (("ffn1_norm", 8), ("mix_norm", 8), ("q_norm", 8), ("kv_norm", 8), ("pool_w", 512), ("pool_scale", 8),
              ("xattn_norm", 8), ("mem_norm", 8), ("ffn2_norm", 8), ("final_norm", 8), ("loss", 8))
SMALL_OFF = {}
_o = 0
for _n, _r in SMALL_ROWS:
    SMALL_OFF[_n] = (_o, _r)
    _o += _r


def _cparams(**kw):
    return pltpu.CompilerParams(vmem_limit_bytes=VMEM_LIMIT_BYTES, **kw)


def _row_tile(rows, limit):
    best = None
    for cand in range(BF16_ROWS, min(rows, limit) + 1, BF16_ROWS):
        if rows % cand == 0:
            best = cand
    assert best is not None, rows
    return best


def _dot_nn(a, b):
    return lax.dot_general(a, b, (((1,), (0,)), ((), ())), preferred_element_type=F32)


def _dot_nt(a, b):
    return lax.dot_general(a, b, (((1,), (1,)), ((), ())), preferred_element_type=F32)


def _dot_tn(a, b):
    return lax.dot_general(a, b, (((0,), (0,)), ((), ())), preferred_element_type=F32)


def _rms_fwd(x, g):
    r = lax.rsqrt(jnp.mean(x * x, axis=-1, keepdims=True) + RMS_EPS)
    return x * r * g, r


def _rms_bwd(dy, x, g, r):
    xhat = x * r
    dyg = dy * g
    dx = r * (dyg - xhat * jnp.mean(dyg * xhat, axis=-1, keepdims=True))
    dg = jnp.sum(dy * xhat, axis=0, keepdims=True)
    return dx, dg


def _accumulate(ref, val, first):
    if isinstance(first, bool):
        if first:
            ref[...] = val
        else:
            ref[...] += val
        return

    @pl.when(first)
    def _():
        ref[...] = val

    @pl.when(jnp.logical_not(first))
    def _():
        ref[...] += val


def _call_after(token, body, in_specs, args, **kw):
    if token is not None:
        inner = body
        body = lambda tok_ref, *refs: inner(*refs)
        in_specs = [pl.BlockSpec((8, 128), lambda *_: (0, 0))] + list(in_specs)
        args = (token,) + tuple(args)
    return pl.pallas_call(body, in_specs=in_specs, **kw)(*args)


def _resident(shape):
    return pl.BlockSpec(shape, lambda *_: (0,) * len(shape), pipeline_mode=pl.Buffered(1))


def _rope_tables(pos_col, tab):
    ang = pos_col.astype(F32) * tab[0:1, :]
    return jnp.cos(ang), jnp.sin(ang) * tab[1:2, :]


def _swap_halves(x):
    lane = lax.broadcasted_iota(jnp.int32, x.shape, 1)
    return jnp.where((lane % 64) < 32, pltpu.roll(x, 96, 1), pltpu.roll(x, 32, 1))


def _rope_apply(x, cos_t, sin_t):
    return x * cos_t + _swap_halves(x) * sin_t


def _rope_apply_t(dy, cos_t, sin_t):
    return dy * cos_t + _swap_halves(dy * sin_t)


def _ffn_fwd(h, g, w, name, token=None, head=None):
    t, d = h.shape
    f = w.shape[1]
    tm, tf = min(512, t), 256
    nf = f // tf
    n_in = 3 if head is None else 5

    def body(*refs):
        h_ref, g_ref, w_ref = refs[:3]
        ho_ref, n_ref, gate_ref, up_ref = refs[n_in:n_in + 4]
        nb_sc, acc_sc = refs[-2:]
        y, _ = _rms_fwd(h_ref[...], g_ref[...])
        nb = y.astype(BF16)
        nb_sc[...] = nb
        n_ref[...] = nb
        acc_sc[...] = jnp.zeros_like(acc_sc)

        def f_tile(j):
            rows = pl.ds(pl.multiple_of(j * tf, tf), tf)
            nb = nb_sc[...]
            gt = _dot_nt(nb, w_ref[0, rows, :])
            ut = _dot_nt(nb, w_ref[1, rows, :])
            gate_ref[j] = gt.astype(BF16)
            up_ref[j] = ut.astype(BF16)
            act = (gt * jax.nn.sigmoid(gt)) * ut
            return _dot_nn(act.astype(BF16), w_ref[2, rows, :])

        def pair(p, carry):
            acc_sc[...] += f_tile(2 * p) + f_tile(2 * p + 1)
            return carry

        lax.fori_loop(0, nf // 2, pair, 0)
        if nf % 2:
            acc_sc[...] += f_tile(nf - 1)
        ho = h_ref[...] + 0.5 * acc_sc[...]
        if head is None:
            ho_ref[...] = ho
            return
        t_ref, gf_ref = refs[3:5]
        loss_ref, dgf_ref = refs[n_in + 4:n_in + 6]
        gg = gf_ref[...]
        y, r = _rms_fwd(ho, gg)
        err = y - t_ref[...]
        part = 0.5 * jnp.sum(jnp.mean(err * err, axis=-1, keepdims=True), axis=0, keepdims=True)
        dx, dg = _rms_bwd(err * (1.0 / d), ho, gg, r)
        ho_ref[...] = dx
        first = pl.program_id(0) == 0
        _accumulate(loss_ref, jnp.broadcast_to(part, loss_ref.shape), first)
        _accumulate(dgf_ref, dg, first)

    row = pl.BlockSpec((tm, d), lambda i: (i, 0))
    tiles = pl.BlockSpec((nf, tm, tf), lambda i: (0, i, 0))
    in_specs = [row, _resident((1, d)), _resident(w.shape)]
    args = (h, g, w)
    out_specs = [row, row, tiles, tiles]
    out_shape = [jax.ShapeDtypeStruct((t, d), F32), jax.ShapeDtypeStruct((t, d), BF16),
                 jax.ShapeDtypeStruct((nf, t, tf), BF16), jax.ShapeDtypeStruct((nf, t, tf), BF16)]
    if head is not None:
        in_specs += [row, _resident((1, d))]
        args += tuple(head)
        out_specs += [pl.BlockSpec((8, 128), lambda i: (0, 0)), pl.BlockSpec((1, d), lambda i: (0, 0))]
        out_shape += [jax.ShapeDtypeStruct((8, 128), F32), jax.ShapeDtypeStruct((1, d), F32)]
    return _call_after(
        token, body, in_specs, args, name=name, grid=(t // tm,), out_specs=out_specs, out_shape=out_shape,
        scratch_shapes=[pltpu.VMEM((tm, d), BF16), pltpu.VMEM((tm, d), F32)],
        compiler_params=_cparams(),
    )


def _ffn_bwd_data(dho, h, g, gate, up, w, name, token=None):
    t, d = h.shape
    f = w.shape[1]
    tm, tf = min(1024, t), 256
    parts = 2 if tm % 512 == 0 else 1
    tp = tm // parts
    nf = f // tf
    npair, odd = nf // 2, nf % 2
    nsteps = npair + odd

    def body(dho_ref, h_ref, g_ref, gate_ref, up_ref, wg_ref, wu_ref, wd_ref,
             dh_ref, dgate_ref, dup_ref, act_ref, dg_ref, dhb_sc, acc_sc):
        i, j = pl.program_id(0), pl.program_id(1)

        @pl.when(j == 0)
        def _():
            dhb_sc[...] = (0.5 * dho_ref[...]).astype(BF16)
            acc_sc[...] = jnp.zeros_like(acc_sc)

        def slab(ntile):
            cols = pl.ds(0, ntile * tf)
            for r in range(parts):
                rows = pl.ds(r * tp, tp)
                gt = jnp.concatenate([gate_ref[k, rows, :] for k in range(ntile)], axis=-1).astype(F32)
                ut = jnp.concatenate([up_ref[k, rows, :] for k in range(ntile)], axis=-1).astype(F32)
                dact = _dot_nt(dhb_sc[rows, :], wd_ref[cols, :])
                sg = jax.nn.sigmoid(gt)
                silu = gt * sg
                dgb = (dact * ut * (sg * (1.0 + gt * (1.0 - sg)))).astype(BF16)
                dub = (dact * silu).astype(BF16)
                act_ref[rows, cols] = (silu * ut).astype(BF16)
                dgate_ref[rows, cols] = dgb
                dup_ref[rows, cols] = dub
                acc_sc[rows, :] += _dot_nn(dgb, wg_ref[cols, :]) + _dot_nn(dub, wu_ref[cols, :])

        pl.when(j < npair)(lambda: slab(2))
        if odd:
            pl.when(j == npair)(lambda: slab(1))

        @pl.when(j == nsteps - 1)
        def _():
            x = h_ref[...]
            gg = g_ref[...]
            _, r = _rms_fwd(x, gg)
            dx, dg = _rms_bwd(acc_sc[...], x, gg, r)
            dh_ref[...] = dho_ref[...] + dx
            _accumulate(dg_ref, dg, i == 0)

    row = pl.BlockSpec((tm, d), lambda i, j: (i, 0))
    acts = pl.BlockSpec((2, tm, tf), lambda i, j: (j, i, 0))
    weights = lambda k: pl.BlockSpec((None, 2 * tf, d), lambda i, j: (k, j, 0))
    outs = pl.BlockSpec((tm, 2 * tf), lambda i, j: (i, j))
    padded = jax.ShapeDtypeStruct((t, 2 * tf * nsteps), BF16)
    return _call_after(
        token, body,
        [row, row, pl.BlockSpec((1, d), lambda i, j: (0, 0)), acts, acts, weights(0), weights(1), weights(2)],
        (dho, h, g, gate, up, w, w, w),
        name=name, grid=(t // tm, nsteps),
        out_specs=[row, outs, outs, outs, pl.BlockSpec((1, d), lambda i, j: (0, 0))],
        out_shape=[jax.ShapeDtypeStruct((t, d), F32), padded, padded, padded, jax.ShapeDtypeStruct((1, d), F32)],
        scratch_shapes=[pltpu.VMEM((tm, d), BF16), pltpu.VMEM((tm, d), F32)],
        compiler_params=_cparams(),
    )


def _tn_matmul(a, b, name, scale=1.0, tmm=None, out_dtype=F32, token=None, m=None):
    t = a.shape[0]
    m = a.shape[1] if m is None else m
    n = b.shape[1]
    tmm = m if tmm is None else tmm
    tk = min(1024, t)
    nk = t // tk

    def product(a_ref, b_ref):
        prod = _dot_tn(a_ref[...].astype(BF16), b_ref[...].astype(BF16))
        return prod * scale if scale != 1.0 else prod

    def body_f32(a_ref, b_ref, o_ref):
        _accumulate(o_ref, product(a_ref, b_ref), pl.program_id(1) == 0)

    def body_cast(a_ref, b_ref, o_ref, acc_sc):
        k = pl.program_id(1)
        _accumulate(acc_sc, product(a_ref, b_ref), k == 0)

        @pl.when(k == nk - 1)
        def _():
            o_ref[...] = acc_sc[...].astype(out_dtype)

    direct = out_dtype == F32
    return _call_after(
        token, body_f32 if direct else body_cast,
        [pl.BlockSpec((tk, tmm), lambda i, k: (k, i)),
         pl.BlockSpec((tk, n), lambda i, k: (k, 0))],
        (a, b),
        name=name, grid=(m // tmm, nk),
        out_specs=pl.BlockSpec((tmm, n), lambda i, k: (i, 0)),
        out_shape=jax.ShapeDtypeStruct((m, n), out_dtype),
        scratch_shapes=[] if direct else [pltpu.VMEM((tmm, n), F32)],
        compiler_params=_cparams(),
    )


def _mix_prep(h1, mix_norm, w_in, q_norm, wq_t, kv_norm, wkv, pos, rope_tab, token=None):
    t, d = h1.shape
    tm = min(512, t)

    def body(h_ref, gm_ref, win_ref, gq_ref, wq_ref, gkv_ref, wkv_ref, pos_ref, tab_ref,
             u_ref, z_ref, qn_ref, kvn_ref, q_ref, k_ref, v_ref):
        u, _ = _rms_fwd(h_ref[...], gm_ref[...])
        ub = u.astype(BF16)
        u_ref[...] = ub
        z = _dot_nn(ub, win_ref[...])
        z_ref[...] = z
        cos_t, sin_t = _rope_tables(pos_ref[...], tab_ref[...])
        qn, _ = _rms_fwd(z[:, 0:Q_RANK], gq_ref[...])
        qnb = qn.astype(BF16)
        qn_ref[...] = qnb
        q = _dot_nt(qnb, wq_ref[...])
        kvn, _ = _rms_fwd(z[:, Q_RANK:Q_RANK + KV_RANK], gkv_ref[...])
        kvnb = kvn.astype(BF16)
        kvn_ref[...] = kvnb
        kv = _dot_nn(kvnb, wkv_ref[...])
        k_pe = _rope_apply(z[:, Q_RANK + KV_RANK:Q_RANK + KV_RANK + 128], cos_t, sin_t)
        ones = jnp.ones((tm, V_DIM), F32)
        for hh in range(MLA_HEADS):
            b = hh * HEAD_PAD
            q_pe = _rope_apply(q[:, b + NOPE:b + HEAD_PAD], cos_t, sin_t)
            q_ref[hh] = jnp.concatenate([q[:, b:b + NOPE], q_pe], axis=-1).astype(BF16)
            k_ref[hh] = jnp.concatenate([kv[:, b:b + NOPE], k_pe], axis=-1).astype(BF16)
            v_ref[hh] = jnp.concatenate([kv[:, b + NOPE:b + HEAD_PAD], ones], axis=-1).astype(BF16)

    full = lambda shape: pl.BlockSpec(shape, lambda i: (0,) * len(shape))
    return _call_after(
        token, body,
        [pl.BlockSpec((tm, d), lambda i: (i, 0)), _resident((1, d)), _resident(w_in.shape), _resident((1, Q_RANK)),
         _resident(wq_t.shape), _resident((1, KV_RANK)), _resident(wkv.shape),
         pl.BlockSpec((tm, 1), lambda i: (i, 0)), _resident(rope_tab.shape)],
        (h1, mix_norm, w_in, q_norm, wq_t, kv_norm, wkv, pos, rope_tab),
        name="mix_prep", grid=(t // tm,),
        out_specs=[pl.BlockSpec((tm, d), lambda i: (i, 0)),
                   pl.BlockSpec((tm, d), lambda i: (i, 0)),
                   pl.BlockSpec((tm, Q_RANK), lambda i: (i, 0)),
                   pl.BlockSpec((tm, KV_RANK), lambda i: (i, 0)),
                   pl.BlockSpec((MLA_HEADS, tm, HEAD_PAD), lambda i: (0, i, 0)),
                   pl.BlockSpec((MLA_HEADS, tm, HEAD_PAD), lambda i: (0, i, 0)),
                   pl.BlockSpec((MLA_HEADS, tm, 2 * V_DIM), lambda i: (0, i, 0))],
        out_shape=[jax.ShapeDtypeStruct((t, d), BF16), jax.ShapeDtypeStruct((t, d), F32),
                   jax.ShapeDtypeStruct((t, Q_RANK), BF16), jax.ShapeDtypeStruct((t, KV_RANK), BF16),
                   jax.ShapeDtypeStruct((MLA_HEADS, t, HEAD_PAD), BF16),
                   jax.ShapeDtypeStruct((MLA_HEADS, t, HEAD_PAD), BF16),
                   jax.ShapeDtypeStruct((MLA_HEADS, t, 2 * V_DIM), BF16)],
        compiler_params=_cparams(),
    )


def _causal_mask(s):
    row = lax.broadcasted_iota(jnp.int32, s.shape, 0)
    col = lax.broadcasted_iota(jnp.int32, s.shape, 1)
    return jnp.where(col <= row, s, NEG_BIG)


def _attn_fwd(q, k, v):
    nh, t, _ = q.shape
    tq = tk = min(512, t)
    nq, nk = t // tq, t // tk

    pairs = [(i, j) for i in range(nq) for j in range(i + 1)]
    qi = jnp.asarray(np.array([i for i, _ in pairs], np.int32))
    kj = jnp.asarray(np.array([j for _, j in pairs], np.int32))

    def body(qi_ref, kj_ref, q_ref, k_ref, v_ref, o_ref, lse_ref, m_sc, acc_sc):
        n = pl.program_id(0)
        i, j = qi_ref[n], kj_ref[n]

        @pl.when(j == 0)
        def _():
            m_sc[...] = jnp.full_like(m_sc, NEG_BIG)
            acc_sc[...] = jnp.zeros_like(acc_sc)

        def step(diagonal):
            for hh in range(nh):
                s = _dot_nt(q_ref[hh], k_ref[hh]) * ATTN_SCALE
                if diagonal:
                    s = _causal_mask(s)
                m_old = m_sc[hh]
                m_new = jnp.maximum(m_old, jnp.max(s, axis=-1, keepdims=True))
                p = jnp.exp(s - m_new).astype(BF16)
                acc_sc[hh] = jnp.exp(m_old - m_new) * acc_sc[hh] + _dot_nn(p, v_ref[hh])
                m_sc[hh] = m_new

        @pl.when(j < i)
        def _():
            step(False)

        @pl.when(j == i)
        def _():
            step(True)
            for hh in range(nh):
                acc = acc_sc[hh]
                l = acc[:, V_DIM:2 * V_DIM]
                o_ref[:, hh * V_DIM:(hh + 1) * V_DIM] = (acc[:, 0:V_DIM] / l).astype(BF16)
                lse_ref[hh] = m_sc[hh] + jnp.log(l[:, 0:1])

    q_map = lambda n, qi_ref, kj_ref: (0, qi_ref[n], 0)
    kv_map = lambda n, qi_ref, kj_ref: (0, kj_ref[n], 0)
    return pl.pallas_call(
        body, name="attn_fwd",
        grid_spec=pltpu.PrefetchScalarGridSpec(
            num_scalar_prefetch=2, grid=(len(pairs),),
            in_specs=[pl.BlockSpec((nh, tq, HEAD_PAD), q_map),
                      pl.BlockSpec((nh, tk, HEAD_PAD), kv_map),
                      pl.BlockSpec((nh, tk, 2 * V_DIM), kv_map)],
            out_specs=[pl.BlockSpec((tq, nh * V_DIM), lambda n, qi_ref, kj_ref: (qi_ref[n], 0)),
                       pl.BlockSpec((nh, tq, 1), q_map)],
            scratch_shapes=[pltpu.VMEM((nh, tq, 1), F32), pltpu.VMEM((nh, tq, 2 * V_DIM), F32)]),
        out_shape=[jax.ShapeDtypeStruct((t, nh * V_DIM), BF16), jax.ShapeDtypeStruct((nh, t, 1), F32)],
        compiler_params=_cparams(),
    )(qi, kj, q, k, v)


def _attn_delta(o, do):
    t, w = o.shape
    nh = w // V_DIM
    tm = min(512, t)

    def body(o_ref, do_ref, d_ref):
        prod = o_ref[...].astype(F32) * do_ref[...].astype(F32)
        for hh in range(nh):
            d_ref[hh] = jnp.sum(prod[:, hh * V_DIM:(hh + 1) * V_DIM], axis=-1, keepdims=True)

    return pl.pallas_call(
        body, name="attn_delta", grid=(t // tm,),
        in_specs=[pl.BlockSpec((tm, w), lambda i: (i, 0)), pl.BlockSpec((tm, w), lambda i: (i, 0))],
        out_specs=pl.BlockSpec((nh, tm, 1), lambda i: (0, i, 0)),
        out_shape=jax.ShapeDtypeStruct((nh, t, 1), F32),
        compiler_params=_cparams(),
    )(o, do)


ATTN_BWD_HEADS = 2


def _attn_bwd(q, k, v, do, lse, delta):
    nh, t, _ = q.shape
    hp = ATTN_BWD_HEADS
    tq = tk = min(512, t)
    nq, nk = t // tq, t // tk

    pairs = [(j, i) for j in range(nk) for i in range(j, nq)]
    kj = jnp.asarray(np.array([j for j, _ in pairs], np.int32))
    qi = jnp.asarray(np.array([i for _, i in pairs], np.int32))

    def body(kj_ref, qi_ref, q_ref, k_ref, v_ref, do_ref, lse_ref, dlt_ref, dq_ref, dk_ref, dv_ref):
        n = pl.program_id(1)
        j, i = kj_ref[n], qi_ref[n]

        @pl.when(n == 0)
        def _():
            dq_ref[...] = jnp.zeros_like(dq_ref)

        def step(diagonal):
            for hh in range(hp):
                qq, kk = q_ref[hh], k_ref[hh]
                dob = do_ref[:, hh * V_DIM:(hh + 1) * V_DIM]
                s = _dot_nt(qq, kk) * ATTN_SCALE
                if diagonal:
                    s = _causal_mask(s)
                p = jnp.exp(s - lse_ref[hh])
                dpp = _dot_nt(dob, v_ref[hh])
                dsb = (p * (dpp - dlt_ref[hh]) * ATTN_SCALE).astype(BF16)
                _accumulate(dv_ref.at[hh], _dot_tn(p.astype(BF16), dob), diagonal)
                _accumulate(dk_ref.at[hh], _dot_tn(dsb, qq), diagonal)
                dq_ref[hh, pl.ds(pl.multiple_of(i * tq, tq), tq), :] += _dot_nn(dsb, kk)

        @pl.when(i > j)
        def _():
            step(False)

        @pl.when(i == j)
        def _():
            step(True)

    q_map = lambda h, n, kj_ref, qi_ref: (h, qi_ref[n], 0)
    k_map = lambda h, n, kj_ref, qi_ref: (h, kj_ref[n], 0)
    return pl.pallas_call(
        body, name="attn_bwd",
        grid_spec=pltpu.PrefetchScalarGridSpec(
            num_scalar_prefetch=2, grid=(nh // hp, len(pairs)),
            in_specs=[pl.BlockSpec((hp, tq, HEAD_PAD), q_map),
                      pl.BlockSpec((hp, tk, HEAD_PAD), k_map),
                      pl.BlockSpec((hp, tk, V_DIM), k_map),
                      pl.BlockSpec((tq, hp * V_DIM), lambda h, n, kj_ref, qi_ref: (qi_ref[n], h)),
                      pl.BlockSpec((hp, tq, 1), q_map),
                      pl.BlockSpec((hp, tq, 1), q_map)],
            out_specs=[pl.BlockSpec((hp, t, HEAD_PAD), lambda h, n, kj_ref, qi_ref: (h, 0, 0)),
                       pl.BlockSpec((hp, tk, HEAD_PAD), k_map),
                       pl.BlockSpec((hp, tk, V_DIM), k_map)]),
        out_shape=[jax.ShapeDtypeStruct((nh, t, HEAD_PAD), F32), jax.ShapeDtypeStruct((nh, t, HEAD_PAD), F32),
                   jax.ShapeDtypeStruct((nh, t, V_DIM), F32)],
        compiler_params=_cparams(),
    )(kj, qi, q, k, v, do, lse, delta)


def _pool_counts(first_token, rows, w):
    tok = lax.broadcasted_iota(jnp.int32, (rows, POOL_CH), 0) + first_token
    return jnp.minimum(tok + 1, w).astype(F32)


def _pool_centered(zbuf, g, w, i, tm):
    lanes = pl.ds(g * POOL_CH, POOL_CH)
    cur = zbuf[pl.ds(POOL_HALO, tm), lanes]
    win = cur
    for s in range(1, w):
        win = win + zbuf[pl.ds(POOL_HALO - s, tm), lanes]
    return win / _pool_counts(i * tm, tm, w) - cur


def _pool_load(zbuf, z_ref, halo_ref, i, tm):
    @pl.when(i == 0)
    def _():
        zbuf[pl.ds(0, POOL_HALO), :] = jnp.zeros((POOL_HALO, zbuf.shape[1]), F32)

    @pl.when(i > 0)
    def _():
        zbuf[pl.ds(0, POOL_HALO), :] = halo_ref[...]

    zbuf[pl.ds(POOL_HALO, tm), :] = z_ref[...]


def _pool_fwd(z, pool_w, pool_scale):
    t = z.shape[0]
    pw = len(POOL_WINDOWS) * POOL_CH
    tm = min(512, t)
    hb = tm // POOL_HALO

    def body(z_ref, halo_ref, w_ref, sc_ref, p_ref, zbuf):
        i = pl.program_id(0)
        _pool_load(zbuf, z_ref, halo_ref, i, tm)
        for g, w in enumerate(POOL_WINDOWS):
            c = _pool_centered(zbuf, g, w, i, tm)
            y = _dot_nn(c.astype(BF16), w_ref[g]) * sc_ref[:, g * POOL_CH:(g + 1) * POOL_CH]
            p_ref[:, g * POOL_CH:(g + 1) * POOL_CH] = y.astype(BF16)

    return pl.pallas_call(
        body, name="pool_fwd", grid=(t // tm,),
        in_specs=[pl.BlockSpec((tm, pw), lambda i: (i, 1)),
                  pl.BlockSpec((POOL_HALO, pw), lambda i: (jnp.maximum(i * hb - 1, 0), 1)),
                  pl.BlockSpec(pool_w.shape, lambda i: (0, 0, 0)),
                  pl.BlockSpec((1, pw), lambda i: (0, 0))],
        out_specs=pl.BlockSpec((tm, pw), lambda i: (i, 0)),
        out_shape=jax.ShapeDtypeStruct((t, pw), BF16),
        scratch_shapes=[pltpu.VMEM((POOL_HALO + tm, pw), F32)],
        compiler_params=_cparams(),
    )(z, z, pool_w, pool_scale)


def _pool_bwd(dp, z, pool_w, pool_scale):
    t = z.shape[0]
    ng = len(POOL_WINDOWS)
    pw = ng * POOL_CH
    tm = min(512, t)
    hb = tm // POOL_HALO
    nt = t // tm

    def body(dp_ref, dpn_ref, z_ref, halo_ref, w_ref, sc_ref, dz_ref, dw_ref, dsc_ref, zbuf, dbuf):
        i = pl.program_id(0)
        _pool_load(zbuf, z_ref, halo_ref, i, tm)

        @pl.when(i == 0)
        def _():
            dw_ref[...] = jnp.zeros_like(dw_ref)
            dsc_ref[...] = jnp.zeros_like(dsc_ref)

        nxt_ok = (i < nt - 1).astype(F32)
        for g, w in enumerate(POOL_WINDOWS):
            lanes = pl.ds(g * POOL_CH, POOL_CH)
            cols = slice(g * POOL_CH, (g + 1) * POOL_CH)
            sc = sc_ref[:, cols]
            wg = w_ref[g]
            c = _pool_centered(zbuf, g, w, i, tm).astype(BF16)
            ypre = _dot_nn(c, wg)
            dpg = dp_ref[:, cols].astype(F32)
            dsc_ref[:, cols] += jnp.sum(dpg * ypre, axis=0, keepdims=True)
            dyb = (dpg * sc).astype(BF16)
            dw_ref[g] += _dot_tn(c, dyb)
            dd = _dot_nt(dyb, wg)
            dyn = (dpn_ref[:, cols].astype(F32) * sc).astype(BF16)
            ddn = _dot_nt(dyn, wg) * nxt_ok
            dbuf[pl.ds(0, tm), lanes] = dd / _pool_counts(i * tm, tm, w)
            dbuf[pl.ds(tm, POOL_HALO), lanes] = ddn / _pool_counts((i + 1) * tm, POOL_HALO, w)
            acc = -dd
            for s in range(w):
                acc = acc + dbuf[pl.ds(s, tm), lanes]
            dz_ref[:, cols] = acc

    return pl.pallas_call(
        body, name="pool_bwd", grid=(nt,),
        in_specs=[pl.BlockSpec((tm, pw), lambda i: (i, 0)),
                  pl.BlockSpec((POOL_HALO, pw), lambda i: (jnp.minimum((i + 1) * hb, t // POOL_HALO - 1), 0)),
                  pl.BlockSpec((tm, pw), lambda i: (i, 1)),
                  pl.BlockSpec((POOL_HALO, pw), lambda i: (jnp.maximum(i * hb - 1, 0), 1)),
                  pl.BlockSpec(pool_w.shape, lambda i: (0, 0, 0)),
                  pl.BlockSpec((1, pw), lambda i: (0, 0))],
        out_specs=[pl.BlockSpec((tm, pw), lambda i: (i, 0)),
                   pl.BlockSpec((ng, POOL_CH, POOL_CH), lambda i: (0, 0, 0)),
                   pl.BlockSpec((1, pw), lambda i: (0, 0))],
        out_shape=[jax.ShapeDtypeStruct((t, pw), F32), jax.ShapeDtypeStruct((ng, POOL_CH, POOL_CH), F32),
                   jax.ShapeDtypeStruct((1, pw), F32)],
        scratch_shapes=[pltpu.VMEM((POOL_HALO + tm, pw), F32), pltpu.VMEM((tm + POOL_HALO, pw), F32)],
        compiler_params=_cparams(),
    )(dp, dp, z, z, pool_w, pool_scale)


def _mla_bwd(dq_h, dk_h, dv_h, z, dz_pool, h1, dh2, mix_norm, w_in, q_norm, wq_t, kv_norm, wkv, pos, rope_tab):
    t, d = h1.shape
    tm = min(512, t)

    def body(dqh_ref, dkh_ref, dvh_ref, z_ref, dzp_ref, h_ref, dh2_ref, gm_ref, win_ref, gq_ref, wq_ref, gkv_ref,
             wkv_ref, pos_ref, tab_ref, dh1_ref, dq_ref, dkv_ref, dz_ref, dgq_ref, dgkv_ref, dgm_ref):
        i = pl.program_id(0)
        first = i == 0
        cos_t, sin_t = _rope_tables(pos_ref[...], tab_ref[...])
        dq_parts, dkv_parts = [], []
        dk_pe = jnp.zeros((tm, 128), F32)
        for hh in range(MLA_HEADS):
            dqh = dqh_ref[hh]
            dq_parts += [dqh[:, 0:NOPE], _rope_apply_t(dqh[:, NOPE:HEAD_PAD], cos_t, sin_t)]
            dkh = dkh_ref[hh]
            dkv_parts += [dkh[:, 0:NOPE], dvh_ref[hh]]
            dk_pe = dk_pe + dkh[:, NOPE:HEAD_PAD]
        dqb = jnp.concatenate(dq_parts, axis=-1).astype(BF16)
        dkvb = jnp.concatenate(dkv_parts, axis=-1).astype(BF16)
        dq_ref[...] = dqb
        dkv_ref[...] = dkvb
        z = z_ref[...]
        c_q = z[:, 0:Q_RANK]
        gq = gq_ref[...]
        _, rq = _rms_fwd(c_q, gq)
        dcq, dgq = _rms_bwd(_dot_nn(dqb, wq_ref[...]), c_q, gq, rq)
        c_kv = z[:, Q_RANK:Q_RANK + KV_RANK]
        gkv = gkv_ref[...]
        _, rkv = _rms_fwd(c_kv, gkv)
        dckv, dgkv = _rms_bwd(_dot_nt(dkvb, wkv_ref[...]), c_kv, gkv, rkv)
        dkr = _rope_apply_t(dk_pe, cos_t, sin_t)
        dzb = jnp.concatenate([dcq, dckv, dkr, dzp_ref[...]], axis=-1).astype(BF16)
        dz_ref[...] = dzb
        x = h_ref[...]
        gm = gm_ref[...]
        _, rm = _rms_fwd(x, gm)
        dx, dgm = _rms_bwd(_dot_nt(dzb, win_ref[...]), x, gm, rm)
        dh1_ref[...] = dh2_ref[...] + dx
        _accumulate(dgq_ref, dgq, first)
        _accumulate(dgkv_ref, dgkv, first)
        _accumulate(dgm_ref, dgm, first)

    full = lambda shape: pl.BlockSpec(shape, lambda i: (0,) * len(shape))
    row = lambda w: pl.BlockSpec((tm, w), lambda i: (i, 0))
    head = lambda w: pl.BlockSpec((MLA_HEADS, tm, w), lambda i: (0, i, 0))
    pw = len(POOL_WINDOWS) * POOL_CH
    return pl.pallas_call(
        body, name="mla_bwd", grid=(t // tm,),
        in_specs=[head(HEAD_PAD), head(HEAD_PAD), head(V_DIM), row(d), row(pw), row(d), row(d),
                  _resident((1, d)), _resident(w_in.shape), _resident((1, Q_RANK)), _resident(wq_t.shape),
                  _resident((1, KV_RANK)), _resident(wkv.shape), row(1), _resident(rope_tab.shape)],
        out_specs=[row(d), row(d), row(d), row(d), full((1, Q_RANK)), full((1, KV_RANK)), full((1, d))],
        out_shape=[jax.ShapeDtypeStruct((t, d), F32), jax.ShapeDtypeStruct((t, d), BF16),
                   jax.ShapeDtypeStruct((t, d), BF16), jax.ShapeDtypeStruct((t, d), BF16),
                   jax.ShapeDtypeStruct((1, Q_RANK), F32), jax.ShapeDtypeStruct((1, KV_RANK), F32),
                   jax.ShapeDtypeStruct((1, d), F32)],
        compiler_params=_cparams(),
    )(dq_h, dk_h, dv_h, z, dz_pool, h1, dh2, mix_norm, w_in, q_norm, wq_t, kv_norm, wkv, pos, rope_tab)


def _mem_kv(mem, mem_norm, wmkv):
    n, d = mem.shape

    def body(mem_ref, g_ref, w_ref, memn_ref, k_ref, v_ref):
        y, _ = _rms_fwd(mem_ref[...], g_ref[...])
        yb = y.astype(BF16)
        memn_ref[...] = yb
        for hh in range(MEM_HEADS):
            k_ref[hh] = _dot_nn(yb, w_ref[hh]).astype(BF16)
            v_ref[hh] = _dot_nn(yb, w_ref[MEM_HEADS + hh]).astype(BF16)

    return pl.pallas_call(
        body, name="mem_kv",
        out_shape=[jax.ShapeDtypeStruct((n, d), BF16), jax.ShapeDtypeStruct((MEM_HEADS, n, MEM_HD), BF16),
                   jax.ShapeDtypeStruct((MEM_HEADS, n, MEM_HD), BF16)],
        compiler_params=_cparams(),
    )(mem, mem_norm, wmkv)


def _mem_softmax(qb, km):
    s = _dot_nt(qb, km) * MEM_SCALE
    e = jnp.exp(s - jnp.max(s, axis=-1, keepdims=True))
    return e / jnp.sum(e, axis=-1, keepdims=True)


def _xattn_fwd(h1, a, p, w_out, g, wmq, km, vm, wmo, token=None):
    t, d = h1.shape
    tm = min(512, t)
    half = a.shape[1]

    def body(h_ref, a_ref, p_ref, wo_ref, g_ref, wmq_ref, km_ref, vm_ref, wmo_ref,
             h2_ref, h3_ref, hn_ref, q_ref, o_ref):
        h2 = h_ref[...] + _dot_nn(a_ref[...], wo_ref[0:half, :]) + _dot_nn(p_ref[...], wo_ref[half:2 * half, :])
        h2_ref[...] = h2
        hn, _ = _rms_fwd(h2, g_ref[...])
        hnb = hn.astype(BF16)
        hn_ref[...] = hnb
        qb = _dot_nn(hnb, wmq_ref[...]).astype(BF16)
        q_ref[...] = qb
        outs = []
        for hh in range(MEM_HEADS):
            pr = _mem_softmax(qb[:, hh * MEM_HD:(hh + 1) * MEM_HD], km_ref[hh])
            outs.append(_dot_nn(pr.astype(BF16), vm_ref[hh]))
        ob = jnp.concatenate(outs, axis=-1).astype(BF16)
        o_ref[...] = ob
        h3_ref[...] = h2 + _dot_nn(ob, wmo_ref[...])

    full = lambda shape: pl.BlockSpec(shape, lambda i: (0,) * len(shape))
    row = lambda w: pl.BlockSpec((tm, w), lambda i: (i, 0))
    return _call_after(
        token, body,
        [row(d), row(half), row(half), _resident(w_out.shape), _resident((1, d)), _resident(wmq.shape),
         _resident(km.shape), _resident(vm.shape), _resident(wmo.shape)],
        (h1, a, p, w_out, g, wmq, km, vm, wmo),
        name="xattn_fwd", grid=(t // tm,),
        out_specs=[row(d), row(d), row(d), row(d), row(d)],
        out_shape=[jax.ShapeDtypeStruct((t, d), F32), jax.ShapeDtypeStruct((t, d), F32),
                   jax.ShapeDtypeStruct((t, d), BF16), jax.ShapeDtypeStruct((t, d), BF16),
                   jax.ShapeDtypeStruct((t, d), BF16)],
        compiler_params=_cparams(),
    )


def _xattn_bwd(dh3, h2, qm, g, wmq, km, vm, wmo, w_out, token=None):
    t, d = h2.shape
    tm = min(512, t)
    half = d // 2

    def body(dh3_ref, h2_ref, q_ref, g_ref, wmq_ref, km_ref, vm_ref, wmo_ref, wo_ref,
             dh2_ref, dq_ref, da_ref, dp_ref, dk_ref, dv_ref, dg_ref):
        i = pl.program_id(0)
        first = i == 0

        @pl.when(first)
        def _():
            dk_ref[...] = jnp.zeros_like(dk_ref)
            dv_ref[...] = jnp.zeros_like(dv_ref)

        dh3 = dh3_ref[...]
        dob = _dot_nt(dh3.astype(BF16), wmo_ref[...]).astype(BF16)
        qb = q_ref[...]
        dq_parts = []
        for hh in range(MEM_HEADS):
            cols = slice(hh * MEM_HD, (hh + 1) * MEM_HD)
            kk, vv = km_ref[hh], vm_ref[hh]
            pr = _mem_softmax(qb[:, cols], kk)
            doh = dob[:, cols]
            dv_ref[hh] += _dot_tn(pr.astype(BF16), doh)
            dpp = _dot_nt(doh, vv)
            dsb = (pr * (dpp - jnp.sum(dpp * pr, axis=-1, keepdims=True)) * MEM_SCALE).astype(BF16)
            dq_parts.append(_dot_nn(dsb, kk))
            dk_ref[hh] += _dot_tn(dsb, qb[:, cols])
        dqb = jnp.concatenate(dq_parts, axis=-1).astype(BF16)
        dq_ref[...] = dqb
        x = h2_ref[...]
        gg = g_ref[...]
        _, r = _rms_fwd(x, gg)
        dx, dg = _rms_bwd(_dot_nt(dqb, wmq_ref[...]), x, gg, r)
        dh2 = dh3 + dx
        dh2_ref[...] = dh2
        dap = _dot_nt(dh2.astype(BF16), wo_ref[...])
        da_ref[...] = dap[:, 0:half].astype(BF16)
        dp_ref[...] = dap[:, half:d].astype(BF16)
        _accumulate(dg_ref, dg, first)

    full = lambda shape: pl.BlockSpec(shape, lambda i: (0,) * len(shape))
    row = lambda w: pl.BlockSpec((tm, w), lambda i: (i, 0))
    return _call_after(
        token, body,
        [row(d), row(d), row(d), _resident((1, d)), _resident(wmq.shape), _resident(km.shape), _resident(vm.shape),
         _resident(wmo.shape), _resident(w_out.shape)],
        (dh3, h2, qm, g, wmq, km, vm, wmo, w_out),
        name="xattn_bwd", grid=(t // tm,),
        out_specs=[row(d), row(d), row(half), row(half), full(km.shape), full(vm.shape), full((1, d))],
        out_shape=[jax.ShapeDtypeStruct((t, d), F32), jax.ShapeDtypeStruct((t, d), BF16),
                   jax.ShapeDtypeStruct((t, half), BF16), jax.ShapeDtypeStruct((t, half), BF16),
                   jax.ShapeDtypeStruct(km.shape, F32), jax.ShapeDtypeStruct(vm.shape, F32),
                   jax.ShapeDtypeStruct((1, d), F32)],
        compiler_params=_cparams(),
    )


def _mem_kv_bwd(dkm, dvm, memn, mem, mem_norm, wmkv):
    n, d = mem.shape

    def body(dk_ref, dv_ref, memn_ref, mem_ref, g_ref, w_ref, dw_ref, dg_ref):
        memn = memn_ref[...]
        dmemn = jnp.zeros((n, d), F32)
        for s in range(2 * MEM_HEADS):
            src = dk_ref[s] if s < MEM_HEADS else dv_ref[s - MEM_HEADS]
            db = src.astype(BF16)
            dw_ref[s] = _dot_tn(memn, db)
            dmemn = dmemn + _dot_nt(db, w_ref[s])
        x = mem_ref[...]
        gg = g_ref[...]
        _, r = _rms_fwd(x, gg)
        _, dg = _rms_bwd(dmemn, x, gg, r)
        dg_ref[...] = dg

    return pl.pallas_call(
        body, name="mem_kv_bwd",
        out_shape=[jax.ShapeDtypeStruct(wmkv.shape, F32), jax.ShapeDtypeStruct((1, d), F32)],
        compiler_params=_cparams(),
    )(dkm, dvm, memn, mem, mem_norm, wmkv)


MESH_ID = pl.DeviceIdType.MESH
ANY = pl.BlockSpec(memory_space=pl.ANY)


def _coords():
    return lax.axis_index("x"), lax.axis_index("y"), lax.axis_index("c")


def _other_chips(x, y):
    return [(1 - x, y), (x, 1 - y), (1 - x, 1 - y)]


def _core_reduce(g, tag):
    _, r, w = g.shape

    def body(g_ref, part_ref, own_sc, recv_sc, send_sems, recv_sems, local_sems):
        x, y, c = _coords()
        sent, local = [], []
        for chip in range(4):
            sent.append(pltpu.make_async_remote_copy(
                src_ref=g_ref.at[2 * chip + (1 - c)], dst_ref=recv_sc.at[chip],
                send_sem=send_sems.at[chip], recv_sem=recv_sems.at[chip],
                device_id=(x, y, 1 - c), device_id_type=MESH_ID))
            local.append(pltpu.make_async_copy(g_ref.at[2 * chip + c], own_sc.at[chip], local_sems.at[chip]))
        for cp in sent + local:
            cp.start()
        for chip in range(4):
            local[chip].wait()
            sent[chip].wait_recv()
            part_ref[chip] = (own_sc[chip].astype(F32) + recv_sc[chip].astype(F32)).astype(part_ref.dtype)
        for cp in sent:
            cp.wait_send()

    return pl.pallas_call(
        body, name="core_reduce_" + tag,
        out_shape=jax.ShapeDtypeStruct((4, r, w), g.dtype),
        in_specs=[ANY], out_specs=pl.BlockSpec(memory_space=pltpu.VMEM),
        scratch_shapes=[pltpu.VMEM((4, r, w), g.dtype), pltpu.VMEM((4, r, w), g.dtype),
                        pltpu.SemaphoreType.DMA((4,)), pltpu.SemaphoreType.DMA((4,)), pltpu.SemaphoreType.DMA((4,))],
        compiler_params=_cparams(),
    )(g)


HBM_SPEC = pl.BlockSpec(memory_space=pltpu.HBM)
SEM_SPEC = pl.BlockSpec(memory_space=pltpu.SEMAPHORE)
SPLIT_EFFECT = pltpu.SideEffectType.DATAFLOW_SIDE_EFFECTING


def _ici_refs(gather, src_ref, land_ref, j, px, py, slot_chip, c):
    if gather:
        return src_ref, land_ref.at[:, 4 * slot_chip[0] + 2 * slot_chip[1] + c]
    return src_ref.at[2 * px + py], land_ref.at[j]


def _ici_start(src, after, name, gather):
    r, w = src.shape[-2:]
    land_shape = (src.shape[0], N_DEV, r, w) if gather else (3, r, w)

    def body(src_ref, land_ref, after_ref, send_sems, recv_sems, src_thru, land_thru, token):
        x, y, c = _coords()
        for j, (px, py) in enumerate(_other_chips(x, y)):
            s_ref, d_ref = _ici_refs(gather, src_ref, land_ref, j, px, py, (x, y), c)
            pltpu.make_async_remote_copy(
                src_ref=s_ref, dst_ref=d_ref, send_sem=send_sems.at[j], recv_sem=recv_sems.at[j],
                device_id=(px, py, c), device_id_type=MESH_ID).start()
        token[...] = jnp.zeros_like(token)

    return pl.pallas_call(
        body, name=name,
        out_shape=(pltpu.SemaphoreType.DMA((3,)), pltpu.SemaphoreType.DMA((3,)), pltpu.HBM(src.shape, src.dtype),
                   pltpu.HBM(land_shape, src.dtype), jax.ShapeDtypeStruct((8, 128), F32)),
        in_specs=(HBM_SPEC, HBM_SPEC, ANY),
        out_specs=(SEM_SPEC, SEM_SPEC, HBM_SPEC, HBM_SPEC, pl.BlockSpec(memory_space=pltpu.VMEM)),
        input_output_aliases={0: 2, 1: 3},
        compiler_params=pltpu.CompilerParams(has_side_effects=SPLIT_EFFECT),
    )(pltpu.with_memory_space_constraint(src, pltpu.HBM),
      pltpu.with_memory_space_constraint(lax.empty(land_shape, src.dtype), pltpu.HBM), after)


def _ici_wait(started, after, name, gather):
    send_sems, recv_sems, src_thru, land_thru, _ = started

    def body(src_ref, land_ref, send_sems, recv_sems, after_ref, src_dead, got_ref):
        x, y, c = _coords()
        for j, (px, py) in enumerate(_other_chips(x, y)):
            s_ref, d_ref = _ici_refs(gather, src_ref, land_ref, j, px, py, (px, py), c)
            copy = pltpu.make_async_remote_copy(
                src_ref=s_ref, dst_ref=d_ref, send_sem=send_sems.at[j], recv_sem=recv_sems.at[j],
                device_id=(px, py, c), device_id_type=MESH_ID)
            copy.wait_send()
            copy.wait_recv()

    return pl.pallas_call(
        body, name=name,
        out_shape=(pltpu.HBM(src_thru.shape, src_thru.dtype), pltpu.HBM(land_thru.shape, land_thru.dtype)),
        in_specs=(HBM_SPEC, HBM_SPEC, SEM_SPEC, SEM_SPEC, ANY),
        out_specs=(HBM_SPEC, HBM_SPEC), input_output_aliases={0: 0, 1: 1},
        compiler_params=pltpu.CompilerParams(has_side_effects=SPLIT_EFFECT),
    )(src_thru, land_thru, send_sems, recv_sems, after)


def _neighbour(k, x, y):
    return (1 - x, y) if k == 0 else (x, 1 - y)


def _slot(ref, px, py, c):
    return ref.at[:, 4 * px + 2 * py + c]


def _near_start(src, after, name):
    land_shape = (src.shape[0], N_DEV) + src.shape[1:]

    def body(src_ref, land_ref, after_ref, send_sems, recv_sems, src_thru, land_thru, token):
        x, y, c = _coords()
        for k in range(2):
            px, py = _neighbour(k, x, y)
            pltpu.make_async_remote_copy(
                src_ref=src_ref, dst_ref=_slot(land_ref, x, y, c), send_sem=send_sems.at[k],
                recv_sem=recv_sems.at[k], device_id=(px, py, c), device_id_type=MESH_ID).start()
        token[...] = jnp.zeros_like(token)

    return pl.pallas_call(
        body, name=name,
        out_shape=(pltpu.SemaphoreType.DMA((2,)), pltpu.SemaphoreType.DMA((2,)), pltpu.HBM(src.shape, src.dtype),
                   pltpu.HBM(land_shape, src.dtype), jax.ShapeDtypeStruct((8, 128), F32)),
        in_specs=(HBM_SPEC, HBM_SPEC, ANY),
        out_specs=(SEM_SPEC, SEM_SPEC, HBM_SPEC, HBM_SPEC, pl.BlockSpec(memory_space=pltpu.VMEM)),
        input_output_aliases={0: 2, 1: 3},
        compiler_params=pltpu.CompilerParams(has_side_effects=SPLIT_EFFECT),
    )(pltpu.with_memory_space_constraint(src, pltpu.HBM),
      pltpu.with_memory_space_constraint(lax.empty(land_shape, src.dtype), pltpu.HBM), after)


def _near_wait(started, after, name):
    send_sems, recv_sems, src_thru, land_thru, _ = started

    def body(src_ref, land_ref, send_sems, recv_sems, after_ref, src_dead, got_ref):
        x, y, c = _coords()
        for k in range(2):
            px, py = _neighbour(k, x, y)
            copy = pltpu.make_async_remote_copy(
                src_ref=src_ref, dst_ref=_slot(land_ref, px, py, c), send_sem=send_sems.at[k],
                recv_sem=recv_sems.at[k], device_id=(px, py, c), device_id_type=MESH_ID)
            copy.wait_send()
            copy.wait_recv()

    return pl.pallas_call(
        body, name=name,
        out_shape=(pltpu.HBM(src_thru.shape, src_thru.dtype), pltpu.HBM(land_thru.shape, land_thru.dtype)),
        in_specs=(HBM_SPEC, HBM_SPEC, SEM_SPEC, SEM_SPEC, ANY),
        out_specs=(HBM_SPEC, HBM_SPEC), input_output_aliases={0: 0, 1: 1},
        compiler_params=pltpu.CompilerParams(has_side_effects=SPLIT_EFFECT),
    )(src_thru, land_thru, send_sems, recv_sems, after)


def _far_refs(land_ref, k, x, y, c, arriving):
    half = land_ref.shape[2] // 2
    rows = pl.ds(k * half, half)
    ox, oy = (1 - x, 1 - y) if arriving else _neighbour(k, x, y)
    return land_ref.at[:, 4 * ox + 2 * oy + c, rows]


def _far_start(land, after, name):
    def body(land_ref, after_ref, send_sems, recv_sems, land_thru, token):
        x, y, c = _coords()
        for k in range(2):
            block = _far_refs(land_ref, k, x, y, c, False)
            px, py = _neighbour(1 - k, x, y)
            pltpu.make_async_remote_copy(
                src_ref=block, dst_ref=block, send_sem=send_sems.at[k], recv_sem=recv_sems.at[k],
                device_id=(px, py, c), device_id_type=MESH_ID).start()
        token[...] = jnp.zeros_like(token)

    return pl.pallas_call(
        body, name=name,
        out_shape=(pltpu.SemaphoreType.DMA((2,)), pltpu.SemaphoreType.DMA((2,)),
                   pltpu.HBM(land.shape, land.dtype), jax.ShapeDtypeStruct((8, 128), F32)),
        in_specs=(HBM_SPEC, ANY),
        out_specs=(SEM_SPEC, SEM_SPEC, HBM_SPEC, pl.BlockSpec(memory_space=pltpu.VMEM)),
        input_output_aliases={0: 2},
        compiler_params=pltpu.CompilerParams(has_side_effects=SPLIT_EFFECT),
    )(pltpu.with_memory_space_constraint(land, pltpu.HBM), after)


def _far_wait(started, after, name):
    send_sems, recv_sems, land_thru, _ = started

    def body(land_ref, send_sems, recv_sems, after_ref, got_ref):
        x, y, c = _coords()
        for k in range(2):
            px, py = _neighbour(1 - k, x, y)
            copy = pltpu.make_async_remote_copy(
                src_ref=_far_refs(land_ref, k, x, y, c, False), dst_ref=_far_refs(land_ref, k, x, y, c, True),
                send_sem=send_sems.at[k], recv_sem=recv_sems.at[k], device_id=(px, py, c), device_id_type=MESH_ID)
            copy.wait_send()
            copy.wait_recv()

    return pl.pallas_call(
        body, name=name,
        out_shape=pltpu.HBM(land_thru.shape, land_thru.dtype),
        in_specs=(HBM_SPEC, SEM_SPEC, SEM_SPEC, ANY),
        out_specs=HBM_SPEC, input_output_aliases={0: 0},
        compiler_params=pltpu.CompilerParams(has_side_effects=SPLIT_EFFECT),
    )(land_thru, send_sems, recv_sems, after)


def _peer(k, x, y, c):
    return x ^ ((k >> 2) & 1), y ^ ((k >> 1) & 1), c ^ (k & 1)


def _peers_start(src, after, name):
    r, w = src.shape
    x, y, c = _coords()
    land = lax.dynamic_update_slice(jnp.zeros((N_DEV, r, w), src.dtype), src[None], (4 * x + 2 * y + c, 0, 0))

    def body(src_ref, land_ref, after_ref, send_sems, recv_sems, src_thru, land_thru, token):
        x, y, c = _coords()
        for k in range(1, N_DEV):
            pltpu.make_async_remote_copy(
                src_ref=src_ref, dst_ref=land_ref.at[4 * x + 2 * y + c],
                send_sem=send_sems.at[k - 1], recv_sem=recv_sems.at[k - 1],
                device_id=_peer(k, x, y, c), device_id_type=MESH_ID).start()
        token[...] = jnp.zeros_like(token)

    return pl.pallas_call(
        body, name=name,
        out_shape=(pltpu.SemaphoreType.DMA((N_DEV - 1,)), pltpu.SemaphoreType.DMA((N_DEV - 1,)),
                   pltpu.HBM(src.shape, src.dtype), pltpu.HBM(land.shape, src.dtype),
                   jax.ShapeDtypeStruct((8, 128), F32)),
        in_specs=(HBM_SPEC, HBM_SPEC, ANY),
        out_specs=(SEM_SPEC, SEM_SPEC, HBM_SPEC, HBM_SPEC, pl.BlockSpec(memory_space=pltpu.VMEM)),
        input_output_aliases={0: 2, 1: 3},
        compiler_params=pltpu.CompilerParams(has_side_effects=SPLIT_EFFECT),
    )(pltpu.with_memory_space_constraint(src, pltpu.HBM), pltpu.with_memory_space_constraint(land, pltpu.HBM), after)


def _peers_wait(started, after, name):
    send_sems, recv_sems, src_thru, land_thru, _ = started

    def body(src_ref, land_ref, send_sems, recv_sems, after_ref, src_dead, got_ref):
        x, y, c = _coords()
        for k in range(1, N_DEV):
            px, py, pc = _peer(k, x, y, c)
            copy = pltpu.make_async_remote_copy(
                src_ref=src_ref, dst_ref=land_ref.at[4 * px + 2 * py + pc],
                send_sem=send_sems.at[k - 1], recv_sem=recv_sems.at[k - 1],
                device_id=(px, py, pc), device_id_type=MESH_ID)
            copy.wait_send()
            copy.wait_recv()

    return pl.pallas_call(
        body, name=name,
        out_shape=(pltpu.HBM(src_thru.shape, src_thru.dtype), pltpu.HBM(land_thru.shape, land_thru.dtype)),
        in_specs=(HBM_SPEC, HBM_SPEC, SEM_SPEC, SEM_SPEC, ANY),
        out_specs=(HBM_SPEC, HBM_SPEC), input_output_aliases={0: 0, 1: 1},
        compiler_params=pltpu.CompilerParams(has_side_effects=SPLIT_EFFECT),
    )(src_thru, land_thru, send_sems, recv_sems, after)


def _share_refs(ref, k, x, y, c, sender_c):
    px, py = ([(x, y)] + _other_chips(x, y))[k]
    return ref.at[:, 4 * px + 2 * py + sender_c]


def _share_start(gathered, after, name):
    def body(g_ref, after_ref, send_sems, recv_sems, g_thru, token):
        x, y, c = _coords()
        for k in range(4):
            slot = _share_refs(g_ref, k, x, y, c, c)
            pltpu.make_async_remote_copy(
                src_ref=slot, dst_ref=slot, send_sem=send_sems.at[k], recv_sem=recv_sems.at[k],
                device_id=(x, y, 1 - c), device_id_type=MESH_ID).start()
        token[...] = jnp.zeros_like(token)

    return pl.pallas_call(
        body, name=name,
        out_shape=(pltpu.SemaphoreType.DMA((4,)), pltpu.SemaphoreType.DMA((4,)),
                   pltpu.HBM(gathered.shape, gathered.dtype), jax.ShapeDtypeStruct((8, 128), F32)),
        in_specs=(HBM_SPEC, ANY),
        out_specs=(SEM_SPEC, SEM_SPEC, HBM_SPEC, pl.BlockSpec(memory_space=pltpu.VMEM)),
        input_output_aliases={0: 2},
        compiler_params=pltpu.CompilerParams(has_side_effects=SPLIT_EFFECT),
    )(pltpu.with_memory_space_constraint(gathered, pltpu.HBM), after)


def _share_wait(started, after, name):
    send_sems, recv_sems, g_thru, _ = started

    def body(g_ref, send_sems, recv_sems, after_ref, got_ref):
        x, y, c = _coords()
        for k in range(4):
            copy = pltpu.make_async_remote_copy(
                src_ref=_share_refs(g_ref, k, x, y, c, c), dst_ref=_share_refs(g_ref, k, x, y, c, 1 - c),
                send_sem=send_sems.at[k], recv_sem=recv_sems.at[k],
                device_id=(x, y, 1 - c), device_id_type=MESH_ID)
            copy.wait_send()
            copy.wait_recv()

    return pl.pallas_call(
        body, name=name,
        out_shape=pltpu.HBM(g_thru.shape, g_thru.dtype),
        in_specs=(HBM_SPEC, SEM_SPEC, SEM_SPEC, ANY),
        out_specs=HBM_SPEC, input_output_aliases={0: 0},
        compiler_params=pltpu.CompilerParams(has_side_effects=SPLIT_EFFECT),
    )(g_thru, send_sems, recv_sems, after)


def _core_share(own, gathered, name):
    def body(own_ref, gin_ref, out_ref, stage, send_sems, recv_sems, local_sem):
        x, y, c = _coords()
        sibling = (x, y, 1 - c)
        chips = [(x, y)] + _other_chips(x, y)
        stage_in = pltpu.make_async_copy(own_ref, stage, local_sem)
        stage_in.start()
        sent, arriving = [], []
        for k, (px, py) in enumerate(chips):
            slot = out_ref.at[:, 4 * px + 2 * py + c]
            sent.append(pltpu.make_async_remote_copy(
                src_ref=own_ref if k == 0 else slot, dst_ref=slot,
                send_sem=send_sems.at[k], recv_sem=recv_sems.at[k], device_id=sibling, device_id_type=MESH_ID))
            arriving.append(pltpu.make_async_remote_copy(
                src_ref=own_ref, dst_ref=out_ref.at[:, 4 * px + 2 * py + (1 - c)],
                send_sem=send_sems.at[k], recv_sem=recv_sems.at[k], device_id=sibling, device_id_type=MESH_ID))
        for cp in sent:
            cp.start()
        stage_in.wait()
        stage_out = pltpu.make_async_copy(stage, out_ref.at[:, 4 * x + 2 * y + c], local_sem)
        stage_out.start()
        for cp in arriving:
            cp.wait_recv()
        for cp in sent:
            cp.wait_send()
        stage_out.wait()

    return pl.pallas_call(
        body, name=name,
        out_shape=jax.ShapeDtypeStruct(gathered.shape, own.dtype),
        in_specs=[ANY, ANY], out_specs=ANY, input_output_aliases={1: 0},
        scratch_shapes=[pltpu.VMEM(own.shape, own.dtype), pltpu.SemaphoreType.DMA((4,)),
                        pltpu.SemaphoreType.DMA((4,)), pltpu.SemaphoreType.DMA],
    )(own, gathered)


def _adamw(w, g, m, v):
    m = ADAM_B1 * m + (1.0 - ADAM_B1) * g
    v = ADAM_B2 * v + (1.0 - ADAM_B2) * (g * g)
    m_hat = m / ADAM_C1
    v_hat = v / ADAM_C2
    delta = -ADAM_LR * (m_hat / (jnp.sqrt(v_hat) + ADAM_EPS) + ADAM_WD * w)
    return delta, m, v


def _adam_big(units, chip_idx, tag, token):
    n = len(units)
    r, wd = units[0][2].shape
    tr, tw = _row_tile(r, 1024), 256

    def body(s_ref, tok_ref, *refs):
        for u in range(n):
            p_ref, l_ref, w_ref, m_ref, v_ref = refs[5 * u:5 * u + 5]
            g_ref, d_ref, mo_ref, vo_ref = refs[5 * n + 4 * u:5 * n + 4 * u + 4]
            g = p_ref[0].astype(F32)
            for j in range(3):
                g = g + l_ref[j].astype(F32)
            delta, mn, vn = _adamw(w_ref[...], g, m_ref[...], v_ref[...])
            g_ref[...] = g
            d_ref[...] = delta
            mo_ref[...] = mn
            vo_ref[...] = vn

    row = pl.BlockSpec((tr, tw), lambda i, j, s: (i, j))
    unit_specs = [pl.BlockSpec((1, tr, tw), lambda i, j, s: (s[0], i, j)),
                  pl.BlockSpec((3, tr, tw), lambda i, j, s: (0, i, j)), row, row, row]
    outs = pl.pallas_call(
        body, name="adam_big_" + tag,
        grid_spec=pltpu.PrefetchScalarGridSpec(
            num_scalar_prefetch=1, grid=(r // tr, wd // tw),
            in_specs=[pl.BlockSpec((8, 128), lambda i, j, s: (0, 0))] + unit_specs * n,
            out_specs=[row] * (4 * n)),
        out_shape=[jax.ShapeDtypeStruct((r, wd), F32)] * (4 * n),
        compiler_params=_cparams(),
    )(chip_idx, token, *[a for unit in units for a in unit])
    return [outs[4 * u:4 * u + 4] for u in range(n)]


def _adam_small(parts, w, m, v):
    _, r, wd = parts.shape

    def body(p_ref, w_ref, m_ref, v_ref, g_ref, d_ref, mo_ref, vo_ref):
        g = p_ref[0]
        for k in range(1, N_DEV):
            g = g + p_ref[k]
        delta, mn, vn = _adamw(w_ref[...], g, m_ref[...], v_ref[...])
        g_ref[...] = g
        d_ref[...] = delta
        mo_ref[...] = mn
        vo_ref[...] = vn

    return pl.pallas_call(
        body, name="adam_small",
        out_shape=[jax.ShapeDtypeStruct((r, wd), F32)] * 4,
        compiler_params=_cparams(),
    )(parts, w, m, v)


def _pad_rows(a, rows):
    return jnp.pad(a, ((0, rows - a.shape[0]), (0, 0)))


def _pad_w_in(w):
    cut = Q_RANK + KV_RANK + ROPE
    return jnp.concatenate([w[:, :cut], jnp.zeros((w.shape[0], 64), w.dtype), w[:, cut:]], axis=1)


def _unpad_w_in(w):
    cut = Q_RANK + KV_RANK + ROPE
    return jnp.concatenate([w[:, :cut], w[:, cut + 64:]], axis=1)


def _pack_mid(p):
    parts = [_pad_w_in(p["w_in"][0]), p["w_out"][0], p["w_mq"][0], p["w_mo"][0],
             p["w_mkv"][0].reshape(256, D_MODEL),
             _pad_rows(p["w_q_up"][0].T.reshape(24, D_MODEL), 32),
             p["w_kv_up"][0].reshape(16, D_MODEL)]
    return jnp.concatenate(parts, axis=0)


def _pack_ffn(w_gate, w_up, w_down, name):
    d, rows = w_gate.shape[1:]

    def body(g_ref, u_ref, d_ref, o_ref):
        eye = (lax.broadcasted_iota(jnp.int32, (d, d), 0) == lax.broadcasted_iota(jnp.int32, (d, d), 1)).astype(BF16)
        o_ref[0] = _dot_tn(g_ref[0].astype(BF16), eye).astype(BF16)
        o_ref[1] = _dot_tn(u_ref[0].astype(BF16), eye).astype(BF16)
        o_ref[2] = d_ref[0].astype(BF16)

    return pl.pallas_call(
        body, name=name, out_shape=jax.ShapeDtypeStruct((3, rows, d), BF16), compiler_params=_cparams(),
    )(w_gate, w_up, w_down)


def _pack_segments(p, group):
    if group == "mid":
        return _pack_mid(p)[None].astype(BF16)
    return _pack_ffn(p[group + "_w_gate"], p[group + "_w_up"], p[group + "_w_down"], "pack_" + group)


UNIT_WEIGHT = {"ffn1_g": ("ffn1_w_gate", True), "ffn1_u": ("ffn1_w_up", True), "ffn1_d": ("ffn1_w_down", False),
               "ffn2_g": ("ffn2_w_gate", True), "ffn2_u": ("ffn2_w_up", True), "ffn2_d": ("ffn2_w_down", False)}


def _pack_unit(p, unit):
    if unit == "mid":
        return _pack_mid(p)
    name, transposed = UNIT_WEIGHT[unit]
    return p[name][0].T if transposed else p[name][0]


def _unpack_unit(a, unit):
    if unit != "mid":
        name, transposed = UNIT_WEIGHT[unit]
        return {name: (a.T if transposed else a)[None]}
    seg = lambda n: a[SEG_OFF[n][0]:SEG_OFF[n][0] + SEG_OFF[n][1]]
    return {"w_in": _unpad_w_in(seg("w_in"))[None], "w_out": seg("w_out")[None], "w_mq": seg("w_mq")[None],
            "w_mo": seg("w_mo")[None], "w_mkv": seg("w_mkv").reshape(D_MODEL, 256)[None],
            "w_q_up": seg("w_q")[:24].reshape(96, Q_RANK).T[None],
            "w_kv_up": seg("w_kv").reshape(KV_RANK, 128)[None]}


def _unpack_gathered(full, group):
    if group != "mid":
        return {group: full.reshape(len(GROUP_SEGS[group]), -1, D_MODEL)}
    full = full[0]
    seg = lambda n: full[:, SEG_OFF[n][0]:SEG_OFF[n][0] + SEG_OFF[n][1]]
    rows = lambda n: seg(n).reshape(-1, D_MODEL)
    wq_t = seg("w_q")[:, :24].reshape(MLA_HEADS, NOPE + ROPE, Q_RANK)
    wq_t = jnp.pad(wq_t, ((0, 0), (0, HEAD_PAD - NOPE - ROPE), (0, 0))).reshape(MLA_HEADS * HEAD_PAD, Q_RANK)
    wkv = seg("w_kv").reshape(N_DEV, KV_RANK, 128).transpose(1, 0, 2).reshape(KV_RANK, N_DEV * 128)
    return {"w_in": rows("w_in"), "w_out": rows("w_out"), "w_mq": rows("w_mq"), "w_mo": rows("w_mo"),
            "w_mkv": seg("w_mkv").reshape(N_DEV, D_MODEL, 256), "w_q": wq_t, "w_kv": wkv}


def _pack_grads(gr):
    blk = lambda a: a.reshape(N_DEV, -1, D_MODEL)
    dwq = gr["w_q"].reshape(MLA_HEADS, HEAD_PAD, Q_RANK)[:, :NOPE + ROPE].reshape(N_DEV, 24, D_MODEL)
    dwq = jnp.pad(dwq, ((0, 0), (0, 8), (0, 0)))
    dwkv = gr["w_kv"].reshape(KV_RANK, N_DEV, 128).transpose(1, 0, 2).reshape(N_DEV, 16, D_MODEL)
    parts = [blk(gr["w_in"]), blk(gr["w_out"]), blk(gr["w_mq"]), blk(gr["w_mo"]),
             gr["w_mkv"].reshape(N_DEV, 256, D_MODEL), dwq, dwkv]
    return jnp.concatenate([a.astype(BF16) for a in parts], axis=1)


def _pack_small(vals):
    parts = []
    for n, r in SMALL_ROWS:
        parts.append(_pad_rows(vals[n].reshape(-1, 128), r) if n in vals else jnp.zeros((r, 128), F32))
    return jnp.concatenate(parts, axis=0)


def _unpack_small(a, shapes):
    out = {}
    for n, shape in shapes.items():
        o = SMALL_OFF[n][0]
        out[n] = a[o:o + int(np.prod(shape)) // 128].reshape(shape)
    return out


BIG_NAMES = ("ffn1_w_gate", "ffn1_w_up", "ffn1_w_down", "w_in", "w_q_up", "w_kv_up", "w_out", "w_mq", "w_mkv",
             "w_mo", "ffn2_w_gate", "ffn2_w_up", "ffn2_w_down")
SMALL_NAMES = ("ffn1_norm", "mix_norm", "q_norm", "kv_norm", "pool_w", "pool_scale", "xattn_norm", "mem_norm",
               "ffn2_norm", "final_norm")
WEIGHT_ORDER = ("ffn1_norm", "ffn1_w_gate", "ffn1_w_up", "ffn1_w_down", "mix_norm", "w_in", "q_norm", "w_q_up",
                "kv_norm", "w_kv_up", "pool_w", "pool_scale", "w_out", "xattn_norm", "mem_norm", "w_mq", "w_mkv",
                "w_mo", "ffn2_norm", "ffn2_w_gate", "ffn2_w_up", "ffn2_w_down", "final_norm")


def _rope_table():
    lane = np.arange(128)
    freqs = (1.0 / (ROPE_BASE ** (np.arange(0, ROPE, 2, dtype=np.float32) / ROPE))).astype(np.float32)
    tab = np.zeros((8, 128), np.float32)
    tab[0] = np.where(lane < ROPE, freqs[lane % (ROPE // 2)], 0.0)
    tab[1] = np.where(lane < ROPE // 2, -1.0, np.where(lane < ROPE, 1.0, 0.0))
    return jnp.asarray(tab)


def kernel(x, mem, positions, ffn1_norm, ffn1_w_gate, ffn1_w_up, ffn1_w_down, mix_norm, w_in, q_norm, w_q_up, kv_norm, w_kv_up, pool_w, pool_scale, w_out, xattn_norm, mem_norm, w_mq, w_mkv, w_mo, ffn2_norm, ffn2_w_gate, ffn2_w_up, ffn2_w_down, final_norm, loss_target, m_ffn1_norm, m_ffn1_w_gate, m_ffn1_w_up, m_ffn1_w_down, m_mix_norm, m_w_in, m_q_norm, m_w_q_up, m_kv_norm, m_w_kv_up, m_pool_w, m_pool_scale, m_w_out, m_xattn_norm, m_mem_norm, m_w_mq, m_w_mkv, m_w_mo, m_ffn2_norm, m_ffn2_w_gate, m_ffn2_w_up, m_ffn2_w_down, m_final_norm, v_ffn1_norm, v_ffn1_w_gate, v_ffn1_w_up, v_ffn1_w_down, v_mix_norm, v_w_in, v_q_norm, v_w_q_up, v_kv_norm, v_w_kv_up, v_pool_w, v_pool_scale, v_w_out, v_xattn_norm, v_mem_norm, v_w_mq, v_w_mkv, v_w_mo, v_ffn2_norm, v_ffn2_w_gate, v_ffn2_w_up, v_ffn2_w_down, v_final_norm):
    wts = dict(ffn1_norm=ffn1_norm, ffn1_w_gate=ffn1_w_gate, ffn1_w_up=ffn1_w_up, ffn1_w_down=ffn1_w_down,
               mix_norm=mix_norm, w_in=w_in, q_norm=q_norm, w_q_up=w_q_up, kv_norm=kv_norm, w_kv_up=w_kv_up,
               pool_w=pool_w, pool_scale=pool_scale, w_out=w_out, xattn_norm=xattn_norm, mem_norm=mem_norm,
               w_mq=w_mq, w_mkv=w_mkv, w_mo=w_mo, ffn2_norm=ffn2_norm, ffn2_w_gate=ffn2_w_gate,
               ffn2_w_up=ffn2_w_up, ffn2_w_down=ffn2_w_down, final_norm=final_norm)
    mom = dict(ffn1_norm=m_ffn1_norm, ffn1_w_gate=m_ffn1_w_gate, ffn1_w_up=m_ffn1_w_up, ffn1_w_down=m_ffn1_w_down,
               mix_norm=m_mix_norm, w_in=m_w_in, q_norm=m_q_norm, w_q_up=m_w_q_up, kv_norm=m_kv_norm,
               w_kv_up=m_w_kv_up, pool_w=m_pool_w, pool_scale=m_pool_scale, w_out=m_w_out, xattn_norm=m_xattn_norm,
               mem_norm=m_mem_norm, w_mq=m_w_mq, w_mkv=m_w_mkv, w_mo=m_w_mo, ffn2_norm=m_ffn2_norm,
               ffn2_w_gate=m_ffn2_w_gate, ffn2_w_up=m_ffn2_w_up, ffn2_w_down=m_ffn2_w_down, final_norm=m_final_norm)
    var = dict(ffn1_norm=v_ffn1_norm, ffn1_w_gate=v_ffn1_w_gate, ffn1_w_up=v_ffn1_w_up, ffn1_w_down=v_ffn1_w_down,
               mix_norm=v_mix_norm, w_in=v_w_in, q_norm=v_q_norm, w_q_up=v_w_q_up, kv_norm=v_kv_norm,
               w_kv_up=v_w_kv_up, pool_w=v_pool_w, pool_scale=v_pool_scale, w_out=v_w_out, xattn_norm=v_xattn_norm,
               mem_norm=v_mem_norm, w_mq=v_w_mq, w_mkv=v_w_mkv, w_mo=v_w_mo, ffn2_norm=v_ffn2_norm,
               ffn2_w_gate=v_ffn2_w_gate, ffn2_w_up=v_ffn2_w_up, ffn2_w_down=v_ffn2_w_down, final_norm=v_final_norm)

    t = x.shape[1]
    xs = x[0]
    mems = mem[0]
    target = loss_target[0]
    pos = positions.reshape(t, 1)
    row = lambda a: a.reshape(1, -1)
    rope_tab = _rope_table()

    cx, cy, cc = _coords()
    chip_idx = (2 * cx + cy).astype(jnp.int32).reshape(1)

    wb = {}
    for grp in ("ffn1", "mid", "ffn2"):
        wb[grp] = _pack_segments(wts, grp)
        if grp == "ffn1":
            near_ffn1 = _near_start(wb["ffn1"], pos, "ag_ffn1_near_start")
    own_ffn1, land_ffn1 = _near_wait(near_ffn1, wb["ffn2"], "ag_ffn1_near_wait")
    far_ffn1 = _far_start(land_ffn1, own_ffn1, "ag_ffn1_far_start")
    land_ffn1 = _far_wait(far_ffn1, wb["mid"], "ag_ffn1_far_wait")
    full_ffn1 = _core_share(own_ffn1, land_ffn1, "ag_ffn1_share")
    fw = _unpack_gathered(full_ffn1, "ffn1")
    ag_mid = _ici_start(wb["mid"], full_ffn1, "ag_mid_start", True)
    g_ffn1, g_mix, g_q, g_kv = row(ffn1_norm), row(mix_norm), row(q_norm), row(kv_norm)
    g_x, g_mem, g_ffn2, g_fin = row(xattn_norm), row(mem_norm), row(ffn2_norm), row(final_norm)
    pool_wb = pool_w[0].astype(BF16)
    pool_sc = row(pool_scale)

    h1, n1, gate1, up1 = _ffn_fwd(xs, g_ffn1, fw["ffn1"], "ffn1_fwd", token=ag_mid[4])
    own_mid, land_mid = _ici_wait(ag_mid, h1, "ag_mid_wait", True)
    full_mid = _core_share(own_mid, land_mid, "ag_mid_share")
    fw.update(_unpack_gathered(full_mid, "mid"))
    ag_ffn2 = _ici_start(wb["ffn2"], full_mid, "ag_ffn2_start", True)
    u, z, qn, kvn, qh, kh, vh = _mix_prep(h1, g_mix, fw["w_in"], g_q, fw["w_q"], g_kv, fw["w_kv"], pos, rope_tab,
                                          token=ag_ffn2[4])
    a, lse = _attn_fwd(qh, kh, vh)
    p = _pool_fwd(z, pool_wb, pool_sc)
    memn, km, vm = _mem_kv(mems, g_mem, fw["w_mkv"])
    own_ffn2, land_ffn2 = _ici_wait(ag_ffn2, a, "ag_ffn2_wait", True)
    land_ffn2 = lax.dynamic_update_slice(land_ffn2, own_ffn2[:, None], (0, 4 * cx + 2 * cy + cc, 0, 0))
    share_ffn2 = _share_start(land_ffn2, a, "ag_ffn2_share_start")
    h2, h3, hn, qm, om = _xattn_fwd(h1, a, p, fw["w_out"], g_x, fw["w_mq"], km, vm, fw["w_mo"], token=share_ffn2[3])
    fw.update(_unpack_gathered(_share_wait(share_ffn2, h3, "ag_ffn2_share_wait"), "ffn2"))
    dh4, n2, gate2, up2, loss_part, dg_fin = _ffn_fwd(h3, g_ffn2, fw["ffn2"],
                                                      "ffn2_fwd", head=(target, g_fin))

    def reduce_start(g8, unit):
        part = _core_reduce(g8, unit)
        return _ici_start(part, g8, "rs_" + unit + "_start", False)

    def by_device(g):
        return g.reshape(N_DEV, -1, D_MODEL)

    rs = {}
    dh3, dgate2, dup2, act2, dg_ffn2 = _ffn_bwd_data(dh4, h3, g_ffn2, gate2, up2, fw["ffn2"], "ffn2_bwd")
    rs["ffn2_g"] = reduce_start(by_device(_tn_matmul(dgate2, n2, "ffn2_dwg", tmm=1408, m=D_FF, out_dtype=BF16)), "ffn2_g")
    rs["ffn2_u"] = reduce_start(by_device(_tn_matmul(dup2, n2, "ffn2_dwu", tmm=1408, m=D_FF, out_dtype=BF16,
                                                     token=rs["ffn2_g"][4])), "ffn2_u")
    rs["ffn2_d"] = reduce_start(by_device(_tn_matmul(act2, dh4, "ffn2_dwd", scale=0.5, tmm=1408, m=D_FF, out_dtype=BF16,
                                                     token=rs["ffn2_u"][4])), "ffn2_d")
    dh2, dqm, da, dp, dkm, dvm, dg_x = _xattn_bwd(dh3, h2, qm, g_x, fw["w_mq"], km, vm, fw["w_mo"], fw["w_out"],
                                                  token=rs["ffn2_d"][4])
    gr = {}
    gr["w_mo"] = _tn_matmul(om, dh3, "dw_mo", out_dtype=BF16)
    gr["w_mq"] = _tn_matmul(hn, dqm, "dw_mq", out_dtype=BF16)
    gr["w_out"] = jnp.concatenate([_tn_matmul(a, dh2, "dw_out_a", out_dtype=BF16),
                                   _tn_matmul(p, dh2, "dw_out_p", out_dtype=BF16)], axis=0)
    gr["w_mkv"], dg_mem = _mem_kv_bwd(dkm, dvm, memn, mems, g_mem, fw["w_mkv"])
    dz_pool, d_pool_w, d_pool_sc = _pool_bwd(dp, z, pool_wb, pool_sc)
    dqh, dkh, dvh = _attn_bwd(qh, kh, vh, da, lse, _attn_delta(a, da))
    dh1, dq, dkv, dz, dg_q, dg_kv, dg_mix = _mla_bwd(dqh, dkh, dvh, z, dz_pool, h1, dh2, g_mix, fw["w_in"], g_q,
                                                     fw["w_q"], g_kv, fw["w_kv"], pos, rope_tab)
    gr["w_q"] = _tn_matmul(dq, qn, "dw_q", out_dtype=BF16)
    gr["w_kv"] = _tn_matmul(kvn, dkv, "dw_kv", out_dtype=BF16)
    gr["w_in"] = _tn_matmul(u, dz, "dw_in", out_dtype=BF16)
    g_mid = _pack_grads(gr)
    part_mid = _core_reduce(g_mid, "mid")
    got = {}
    after = part_mid
    for unit in ("ffn2_g", "ffn2_u", "ffn2_d"):
        got[unit] = _ici_wait(rs[unit], after, "rs_" + unit + "_wait", False)
        after = got[unit][1]
    rs["mid"] = _ici_start(part_mid, after, "rs_mid_start", False)
    dx, dgate1, dup1, act1, dg_ffn1 = _ffn_bwd_data(dh1, xs, g_ffn1, gate1, up1, fw["ffn1"], "ffn1_bwd", token=rs["mid"][4])
    got["mid"] = _ici_wait(rs["mid"], dx, "rs_mid_wait", False)

    small_g = dict(ffn1_norm=dg_ffn1, mix_norm=dg_mix, q_norm=dg_q, kv_norm=dg_kv, pool_w=d_pool_w,
                   pool_scale=d_pool_sc, xattn_norm=dg_x, mem_norm=dg_mem, ffn2_norm=dg_ffn2, final_norm=dg_fin,
                   loss=loss_part)
    small_ag = _peers_start(_pack_small(small_g), got["mid"][1], "small_ag_start")
    rs["ffn1_g"] = reduce_start(by_device(_tn_matmul(dgate1, n1, "ffn1_dwg", tmm=1408, m=D_FF, out_dtype=BF16,
                                                     token=small_ag[4])), "ffn1_g")
    _, parts = _peers_wait(small_ag, rs["ffn1_g"][4], "small_ag_wait")
    small = _adam_small(parts, _pack_small({n: wts[n] for n in SMALL_NAMES}),
                        _pack_small({n: mom[n] for n in SMALL_NAMES}), _pack_small({n: var[n] for n in SMALL_NAMES}))
    small_sum = small[0]
    loss = small_sum[SMALL_OFF["loss"][0], 0]
    shapes = {n: wts[n].shape for n in SMALL_NAMES}
    small = [_unpack_small(s, shapes) for s in small]

    rs["ffn1_u"] = reduce_start(by_device(_tn_matmul(dup1, n1, "ffn1_dwu", tmm=1408, m=D_FF, out_dtype=BF16,
                                                     token=small_sum)), "ffn1_u")
    rs["ffn1_d"] = reduce_start(by_device(_tn_matmul(act1, dh1, "ffn1_dwd", scale=0.5, tmm=1408, m=D_FF, out_dtype=BF16,
                                                     token=rs["ffn1_u"][4])), "ffn1_d")

    big = {}

    def adam_units(names, token):
        units = [got[u] + (_pack_unit(wts, u), _pack_unit(mom, u), _pack_unit(var, u)) for u in names]
        res = _adam_big(units, chip_idx, "_".join(names), token)
        for u, four in zip(names, res):
            for k, packed in enumerate(four):
                big.setdefault(k, {}).update(_unpack_unit(packed, u))
        return res[-1][0]

    done = adam_units(["mid"], rs["ffn1_d"][4])
    done = adam_units(["ffn2_g", "ffn2_u", "ffn2_d"], done)
    got["ffn1_g"] = _ici_wait(rs["ffn1_g"], done, "rs_ffn1_g_wait", False)
    got["ffn1_u"] = _ici_wait(rs["ffn1_u"], got["ffn1_g"][1], "rs_ffn1_u_wait", False)
    done = adam_units(["ffn1_g", "ffn1_u"], done)
    got["ffn1_d"] = _ici_wait(rs["ffn1_d"], done, "rs_ffn1_d_wait", False)
    adam_units(["ffn1_d"], done)

    outs = [loss, dx[None]]
    for k in range(4):
        for n in WEIGHT_ORDER:
            outs.append(big[k][n] if n in BIG_NAMES else small[k][n])
    return tuple(outs)
```

```python
import numpy as np

import jax
import jax.numpy as jnp
from jax import lax
from jax.experimental import pallas as pl
from jax.experimental.pallas import tpu as pltpu

F32 = jnp.float32
BF16 = jnp.bfloat16

N_DEV = 8
D_MODEL = 1024
D_FF = 2816
MLA_HEADS = 4
NOPE = 128
ROPE = 64
HEAD_PAD = 256
V_DIM = 128
Q_RANK = 256
KV_RANK = 128
POOL_WINDOWS = (2, 4, 8, 16)
POOL_CH = 128
POOL_HALO = 16
N_MEM = 256
MEM_HEADS = 4
MEM_HD = 256
ROPE_BASE = 10000.0
RMS_EPS = 1e-6
ATTN_SCALE = (NOPE + ROPE) ** -0.5
MEM_SCALE = MEM_HD ** -0.5
NEG_BIG = -1e30

ADAM_LR = 0.001
ADAM_B1 = 0.9
ADAM_B2 = 0.999
ADAM_EPS = 1e-08
ADAM_WD = 0.01
ADAM_STEP = 10
ADAM_C1 = 1.0 - ADAM_B1 ** ADAM_STEP
ADAM_C2 = 1.0 - ADAM_B2 ** ADAM_STEP

VMEM_LIMIT_BYTES = 56 * 1024 * 1024
BF16_ROWS = 16

GROUP_SEGS = {
    "ffn1": (("ffn1_g", 352), ("ffn1_u", 352), ("ffn1_d", 352)),
    "mid": (("w_in", 128), ("w_out", 128), ("w_mq", 128), ("w_mo", 128), ("w_mkv", 256), ("w_q", 32), ("w_kv", 16)),
    "ffn2": (("ffn2_g", 352), ("ffn2_u", 352), ("ffn2_d", 352)),
}
SEG_OFF = {}
GROUP_ROWS = {}
for _g, _segs in GROUP_SEGS.items():
    _o = 0
    for _n, _r in _segs:
        SEG_OFF[_n] = (_o, _r)
        _o += _r
    GROUP_ROWS[_g] = _o

SMALL_ROWS = (("ffn1_norm", 8), ("mix_norm", 8), ("q_norm", 8), ("kv_norm", 8), ("pool_w", 512), ("pool_scale", 8),
              ("xattn_norm", 8), ("mem_norm", 8), ("ffn2_norm", 8), ("final_norm", 8), ("loss", 8))
SMALL_OFF = {}
_o = 0
for _n, _r in SMALL_ROWS:
    SMALL_OFF[_n] = (_o, _r)
    _o += _r


def _cparams(**kw):
    return pltpu.CompilerParams(vmem_limit_bytes=VMEM_LIMIT_BYTES, **kw)


def _row_tile(rows, limit):
    best = None
    for cand in range(BF16_ROWS, min(rows, limit) + 1, BF16_ROWS):
        if rows % cand == 0:
            best = cand
    assert best is not None, rows
    return best


def _dot_nn(a, b):
    return lax.dot_general(a, b, (((1,), (0,)), ((), ())), preferred_element_type=F32)


def _dot_nt(a, b):
    return lax.dot_general(a, b, (((1,), (1,)), ((), ())), preferred_element_type=F32)


def _dot_tn(a, b):
    return lax.dot_general(a, b, (((0,), (0,)), ((), ())), preferred_element_type=F32)


def _rms_fwd(x, g):
    r = lax.rsqrt(jnp.mean(x * x, axis=-1, keepdims=True) + RMS_EPS)
    return x * r * g, r


def _rms_bwd(dy, x, g, r):
    xhat = x * r
    dyg = dy * g
    dx = r * (dyg - xhat * jnp.mean(dyg * xhat, axis=-1, keepdims=True))
    dg = jnp.sum(dy * xhat, axis=0, keepdims=True)
    return dx, dg


def _accumulate(ref, val, first):
    if isinstance(first, bool):
        if first:
            ref[...] = val
        else:
            ref[...] += val
        return

    @pl.when(first)
    def _():
        ref[...] = val

    @pl.when(jnp.logical_not(first))
    def _():
        ref[...] += val


def _call_after(token, body, in_specs, args, **kw):
    if token is not None:
        inner = body
        body = lambda tok_ref, *refs: inner(*refs)
        in_specs = [pl.BlockSpec((8, 128), lambda *_: (0, 0))] + list(in_specs)
        args = (token,) + tuple(args)
    return pl.pallas_call(body, in_specs=in_specs, **kw)(*args)


def _resident(shape):
    return pl.BlockSpec(shape, lambda *_: (0,) * len(shape), pipeline_mode=pl.Buffered(1))


def _rope_tables(pos_col, tab):
    ang = pos_col.astype(F32) * tab[0:1, :]
    return jnp.cos(ang), jnp.sin(ang) * tab[1:2, :]


def _swap_halves(x):
    lane = lax.broadcasted_iota(jnp.int32, x.shape, 1)
    return jnp.where((lane % 64) < 32, pltpu.roll(x, 96, 1), pltpu.roll(x, 32, 1))


def _rope_apply(x, cos_t, sin_t):
    return x * cos_t + _swap_halves(x) * sin_t


def _rope_apply_t(dy, cos_t, sin_t):
    return dy * cos_t + _swap_halves(dy * sin_t)


def _ffn_fwd(h, g, w, name, token=None, head=None):
    t, d = h.shape
    f = w.shape[1]
    tm, tf = min(512, t), 256
    nf = f // tf
    n_in = 3 if head is None else 5

    def body(*refs):
        h_ref, g_ref, w_ref = refs[:3]
        ho_ref, n_ref, gate_ref, up_ref = refs[n_in:n_in + 4]
        nb_sc, acc_sc = refs[-2:]
        y, _ = _rms_fwd(h_ref[...], g_ref[...])
        nb = y.astype(BF16)
        nb_sc[...] = nb
        n_ref[...] = nb
        acc_sc[...] = jnp.zeros_like(acc_sc)

        def f_tile(j):
            rows = pl.ds(pl.multiple_of(j * tf, tf), tf)
            nb = nb_sc[...]
            gt = _dot_nt(nb, w_ref[0, rows, :])
            ut = _dot_nt(nb, w_ref[1, rows, :])
            gate_ref[j] = gt.astype(BF16)
            up_ref[j] = ut.astype(BF16)
            act = (gt * jax.nn.sigmoid(gt)) * ut
            return _dot_nn(act.astype(BF16), w_ref[2, rows, :])

        def pair(p, carry):
            acc_sc[...] += f_tile(2 * p) + f_tile(2 * p + 1)
            return carry

        lax.fori_loop(0, nf // 2, pair, 0)
        if nf % 2:
            acc_sc[...] += f_tile(nf - 1)
        ho = h_ref[...] + 0.5 * acc_sc[...]
        if head is None:
            ho_ref[...] = ho
            return
        t_ref, gf_ref = refs[3:5]
        loss_ref, dgf_ref = refs[n_in + 4:n_in + 6]
        gg = gf_ref[...]
        y, r = _rms_fwd(ho, gg)
        err = y - t_ref[...]
        part = 0.5 * jnp.sum(jnp.mean(err * err, axis=-1, keepdims=True), axis=0, keepdims=True)
        dx, dg = _rms_bwd(err * (1.0 / d), ho, gg, r)
        ho_ref[...] = dx
        first = pl.program_id(0) == 0
        _accumulate(loss_ref, jnp.broadcast_to(part, loss_ref.shape), first)
        _accumulate(dgf_ref, dg, first)

    row = pl.BlockSpec((tm, d), lambda i: (i, 0))
    tiles = pl.BlockSpec((nf, tm, tf), lambda i: (0, i, 0))
    in_specs = [row, _resident((1, d)), _resident(w.shape)]
    args = (h, g, w)
    out_specs = [row, row, tiles, tiles]
    out_shape = [jax.ShapeDtypeStruct((t, d), F32), jax.ShapeDtypeStruct((t, d), BF16),
                 jax.ShapeDtypeStruct((nf, t, tf), BF16), jax.ShapeDtypeStruct((nf, t, tf), BF16)]
    if head is not None:
        in_specs += [row, _resident((1, d))]
        args += tuple(head)
        out_specs += [pl.BlockSpec((8, 128), lambda i: (0, 0)), pl.BlockSpec((1, d), lambda i: (0, 0))]
        out_shape += [jax.ShapeDtypeStruct((8, 128), F32), jax.ShapeDtypeStruct((1, d), F32)]
    return _call_after(
        token, body, in_specs, args, name=name, grid=(t // tm,), out_specs=out_specs, out_shape=out_shape,
        scratch_shapes=[pltpu.VMEM((tm, d), BF16), pltpu.VMEM((tm, d), F32)],
        compiler_params=_cparams(),
    )


def _ffn_bwd_data(dho, h, g, gate, up, w, name, token=None):
    t, d = h.shape
    f = w.shape[1]
    tm, tf = min(1024, t), 256
    parts = 2 if tm % 512 == 0 else 1
    tp = tm // parts
    nf = f // tf
    npair, odd = nf // 2, nf % 2
    nsteps = npair + odd

    def body(dho_ref, h_ref, g_ref, gate_ref, up_ref, wg_ref, wu_ref, wd_ref,
             dh_ref, dgate_ref, dup_ref, act_ref, dg_ref, dhb_sc, acc_sc):
        i, j = pl.program_id(0), pl.program_id(1)

        @pl.when(j == 0)
        def _():
            dhb_sc[...] = (0.5 * dho_ref[...]).astype(BF16)
            acc_sc[...] = jnp.zeros_like(acc_sc)

        def slab(ntile):
            cols = pl.ds(0, ntile * tf)
            for r in range(parts):
                rows = pl.ds(r * tp, tp)
                gt = jnp.concatenate([gate_ref[k, rows, :] for k in range(ntile)], axis=-1).astype(F32)
                ut = jnp.concatenate([up_ref[k, rows, :] for k in range(ntile)], axis=-1).astype(F32)
                dact = _dot_nt(dhb_sc[rows, :], wd_ref[cols, :])
                sg = jax.nn.sigmoid(gt)
                silu = gt * sg
                dgb = (dact * ut * (sg * (1.0 + gt * (1.0 - sg)))).astype(BF16)
                dub = (dact * silu).astype(BF16)
                act_ref[rows, cols] = (silu * ut).astype(BF16)
                dgate_ref[rows, cols] = dgb
                dup_ref[rows, cols] = dub
                acc_sc[rows, :] += _dot_nn(dgb, wg_ref[cols, :]) + _dot_nn(dub, wu_ref[cols, :])

        pl.when(j < npair)(lambda: slab(2))
        if odd:
            pl.when(j == npair)(lambda: slab(1))

        @pl.when(j == nsteps - 1)
        def _():
            x = h_ref[...]
            gg = g_ref[...]
            _, r = _rms_fwd(x, gg)
            dx, dg = _rms_bwd(acc_sc[...], x, gg, r)
            dh_ref[...] = dho_ref[...] + dx
            _accumulate(dg_ref, dg, i == 0)

    row = pl.BlockSpec((tm, d), lambda i, j: (i, 0))
    acts = pl.BlockSpec((2, tm, tf), lambda i, j: (j, i, 0))
    weights = lambda k: pl.BlockSpec((None, 2 * tf, d), lambda i, j: (k, j, 0))
    outs = pl.BlockSpec((tm, 2 * tf), lambda i, j: (i, j))
    padded = jax.ShapeDtypeStruct((t, 2 * tf * nsteps), BF16)
    return _call_after(
        token, body,
        [row, row, pl.BlockSpec((1, d), lambda i, j: (0, 0)), acts, acts, weights(0), weights(1), weights(2)],
        (dho, h, g, gate, up, w, w, w),
        name=name, grid=(t // tm, nsteps),
        out_specs=[row, outs, outs, outs, pl.BlockSpec((1, d), lambda i, j: (0, 0))],
        out_shape=[jax.ShapeDtypeStruct((t, d), F32), padded, padded, padded, jax.ShapeDtypeStruct((1, d), F32)],
        scratch_shapes=[pltpu.VMEM((tm, d), BF16), pltpu.VMEM((tm, d), F32)],
        compiler_params=_cparams(),
    )


def _tn_matmul(a, b, name, scale=1.0, tmm=None, out_dtype=F32, token=None, m=None):
    t = a.shape[0]
    m = a.shape[1] if m is None else m
    n = b.shape[1]
    tmm = m if tmm is None else tmm
    tk = min(1024, t)
    nk = t // tk

    def product(a_ref, b_ref):
        prod = _dot_tn(a_ref[...].astype(BF16), b_ref[...].astype(BF16))
        return prod * scale if scale != 1.0 else prod

    def body_f32(a_ref, b_ref, o_ref):
        _accumulate(o_ref, product(a_ref, b_ref), pl.program_id(1) == 0)

    def body_cast(a_ref, b_ref, o_ref, acc_sc):
        k = pl.program_id(1)
        _accumulate(acc_sc, product(a_ref, b_ref), k == 0)

        @pl.when(k == nk - 1)
        def _():
            o_ref[...] = acc_sc[...].astype(out_dtype)

    direct = out_dtype == F32
    return _call_after(
        token, body_f32 if direct else body_cast,
        [pl.BlockSpec((tk, tmm), lambda i, k: (k, i)),
         pl.BlockSpec((tk, n), lambda i, k: (k, 0))],
        (a, b),
        name=name, grid=(m // tmm, nk),
        out_specs=pl.BlockSpec((tmm, n), lambda i, k: (i, 0)),
        out_shape=jax.ShapeDtypeStruct((m, n), out_dtype),
        scratch_shapes=[] if direct else [pltpu.VMEM((tmm, n), F32)],
        compiler_params=_cparams(),
    )


def _mix_prep(h1, mix_norm, w_in, q_norm, wq_t, kv_norm, wkv, pos, rope_tab, token=None):
    t, d = h1.shape
    tm = min(512, t)

    def body(h_ref, gm_ref, win_ref, gq_ref, wq_ref, gkv_ref, wkv_ref, pos_ref, tab_ref,
             u_ref, z_ref, qn_ref, kvn_ref, q_ref, k_ref, v_ref):
        u, _ = _rms_fwd(h_ref[...], gm_ref[...])
        ub = u.astype(BF16)
        u_ref[...] = ub
        z = _dot_nn(ub, win_ref[...])
        z_ref[...] = z
        cos_t, sin_t = _rope_tables(pos_ref[...], tab_ref[...])
        qn, _ = _rms_fwd(z[:, 0:Q_RANK], gq_ref[...])
        qnb = qn.astype(BF16)
        qn_ref[...] = qnb
        q = _dot_nt(qnb, wq_ref[...])
        kvn, _ = _rms_fwd(z[:, Q_RANK:Q_RANK + KV_RANK], gkv_ref[...])
        kvnb = kvn.astype(BF16)
        kvn_ref[...] = kvnb
        kv = _dot_nn(kvnb, wkv_ref[...])
        k_pe = _rope_apply(z[:, Q_RANK + KV_RANK:Q_RANK + KV_RANK + 128], cos_t, sin_t)
        ones = jnp.ones((tm, V_DIM), F32)
        for hh in range(MLA_HEADS):
            b = hh * HEAD_PAD
            q_pe = _rope_apply(q[:, b + NOPE:b + HEAD_PAD], cos_t, sin_t)
            q_ref[hh] = jnp.concatenate([q[:, b:b + NOPE], q_pe], axis=-1).astype(BF16)
            k_ref[hh] = jnp.concatenate([kv[:, b:b + NOPE], k_pe], axis=-1).astype(BF16)
            v_ref[hh] = jnp.concatenate([kv[:, b + NOPE:b + HEAD_PAD], ones], axis=-1).astype(BF16)

    full = lambda shape: pl.BlockSpec(shape, lambda i: (0,) * len(shape))
    return _call_after(
        token, body,
        [pl.BlockSpec((tm, d), lambda i: (i, 0)), _resident((1, d)), _resident(w_in.shape), _resident((1, Q_RANK)),
         _resident(wq_t.shape), _resident((1, KV_RANK)), _resident(wkv.shape),
         pl.BlockSpec((tm, 1), lambda i: (i, 0)), _resident(rope_tab.shape)],
        (h1, mix_norm, w_in, q_norm, wq_t, kv_norm, wkv, pos, rope_tab),
        name="mix_prep", grid=(t // tm,),
        out_specs=[pl.BlockSpec((tm, d), lambda i: (i, 0)),
                   pl.BlockSpec((tm, d), lambda i: (i, 0)),
                   pl.BlockSpec((tm, Q_RANK), lambda i: (i, 0)),
                   pl.BlockSpec((tm, KV_RANK), lambda i: (i, 0)),
                   pl.BlockSpec((MLA_HEADS, tm, HEAD_PAD), lambda i: (0, i, 0)),
                   pl.BlockSpec((MLA_HEADS, tm, HEAD_PAD), lambda i: (0, i, 0)),
                   pl.BlockSpec((MLA_HEADS, tm, 2 * V_DIM), lambda i: (0, i, 0))],
        out_shape=[jax.ShapeDtypeStruct((t, d), BF16), jax.ShapeDtypeStruct((t, d), F32),
                   jax.ShapeDtypeStruct((t, Q_RANK), BF16), jax.ShapeDtypeStruct((t, KV_RANK), BF16),
                   jax.ShapeDtypeStruct((MLA_HEADS, t, HEAD_PAD), BF16),
                   jax.ShapeDtypeStruct((MLA_HEADS, t, HEAD_PAD), BF16),
                   jax.ShapeDtypeStruct((MLA_HEADS, t, 2 * V_DIM), BF16)],
        compiler_params=_cparams(),
    )


def _causal_mask(s):
    row = lax.broadcasted_iota(jnp.int32, s.shape, 0)
    col = lax.broadcasted_iota(jnp.int32, s.shape, 1)
    return jnp.where(col <= row, s, NEG_BIG)


def _attn_fwd(q, k, v):
    nh, t, _ = q.shape
    tq = tk = min(512, t)
    nq, nk = t // tq, t // tk

    pairs = [(i, j) for i in range(nq) for j in range(i + 1)]
    qi = jnp.asarray(np.array([i for i, _ in pairs], np.int32))
    kj = jnp.asarray(np.array([j for _, j in pairs], np.int32))

    def body(qi_ref, kj_ref, q_ref, k_ref, v_ref, o_ref, lse_ref, m_sc, acc_sc):
        n = pl.program_id(0)
        i, j = qi_ref[n], kj_ref[n]

        @pl.when(j == 0)
        def _():
            m_sc[...] = jnp.full_like(m_sc, NEG_BIG)
            acc_sc[...] = jnp.zeros_like(acc_sc)

        def step(diagonal):
            for hh in range(nh):
                s = _dot_nt(q_ref[hh], k_ref[hh]) * ATTN_SCALE
                if diagonal:
                    s = _causal_mask(s)
                m_old = m_sc[hh]
                m_new = jnp.maximum(m_old, jnp.max(s, axis=-1, keepdims=True))
                p = jnp.exp(s - m_new).astype(BF16)
                acc_sc[hh] = jnp.exp(m_old - m_new) * acc_sc[hh] + _dot_nn(p, v_ref[hh])
                m_sc[hh] = m_new

        @pl.when(j < i)
        def _():
            step(False)

        @pl.when(j == i)
        def _():
            step(True)
            for hh in range(nh):
                acc = acc_sc[hh]
                l = acc[:, V_DIM:2 * V_DIM]
                o_ref[:, hh * V_DIM:(hh + 1) * V_DIM] = (acc[:, 0:V_DIM] / l).astype(BF16)
                lse_ref[hh] = m_sc[hh] + jnp.log(l[:, 0:1])

    q_map = lambda n, qi_ref, kj_ref: (0, qi_ref[n], 0)
    kv_map = lambda n, qi_ref, kj_ref: (0, kj_ref[n], 0)
    return pl.pallas_call(
        body, name="attn_fwd",
        grid_spec=pltpu.PrefetchScalarGridSpec(
            num_scalar_prefetch=2, grid=(len(pairs),),
            in_specs=[pl.BlockSpec((nh, tq, HEAD_PAD), q_map),
                      pl.BlockSpec((nh, tk, HEAD_PAD), kv_map),
                      pl.BlockSpec((nh, tk, 2 * V_DIM), kv_map)],
            out_specs=[pl.BlockSpec((tq, nh * V_DIM), lambda n, qi_ref, kj_ref: (qi_ref[n], 0)),
                       pl.BlockSpec((nh, tq, 1), q_map)],
            scratch_shapes=[pltpu.VMEM((nh, tq, 1), F32), pltpu.VMEM((nh, tq, 2 * V_DIM), F32)]),
        out_shape=[jax.ShapeDtypeStruct((t, nh * V_DIM), BF16), jax.ShapeDtypeStruct((nh, t, 1), F32)],
        compiler_params=_cparams(),
    )(qi, kj, q, k, v)


def _attn_delta(o, do):
    t, w = o.shape
    nh = w // V_DIM
    tm = min(512, t)

    def body(o_ref, do_ref, d_ref):
        prod = o_ref[...].astype(F32) * do_ref[...].astype(F32)
        for hh in range(nh):
            d_ref[hh] = jnp.sum(prod[:, hh * V_DIM:(hh + 1) * V_DIM], axis=-1, keepdims=True)

    return pl.pallas_call(
        body, name="attn_delta", grid=(t // tm,),
        in_specs=[pl.BlockSpec((tm, w), lambda i: (i, 0)), pl.BlockSpec((tm, w), lambda i: (i, 0))],
        out_specs=pl.BlockSpec((nh, tm, 1), lambda i: (0, i, 0)),
        out_shape=jax.ShapeDtypeStruct((nh, t, 1), F32),
        compiler_params=_cparams(),
    )(o, do)


ATTN_BWD_HEADS = 2


def _attn_bwd(q, k, v, do, lse, delta):
    nh, t, _ = q.shape
    hp = ATTN_BWD_HEADS
    tq = tk = min(512, t)
    nq, nk = t // tq, t // tk

    pairs = [(j, i) for j in range(nk) for i in range(j, nq)]
    kj = jnp.asarray(np.array([j for j, _ in pairs], np.int32))
    qi = jnp.asarray(np.array([i for _, i in pairs], np.int32))

    def body(kj_ref, qi_ref, q_ref, k_ref, v_ref, do_ref, lse_ref, dlt_ref, dq_ref, dk_ref, dv_ref):
        n = pl.program_id(1)
        j, i = kj_ref[n], qi_ref[n]

        @pl.when(n == 0)
        def _():
            dq_ref[...] = jnp.zeros_like(dq_ref)

        def step(diagonal):
            for hh in range(hp):
                qq, kk = q_ref[hh], k_ref[hh]
                dob = do_ref[:, hh * V_DIM:(hh + 1) * V_DIM]
                s = _dot_nt(qq, kk) * ATTN_SCALE
                if diagonal:
                    s = _causal_mask(s)
                p = jnp.exp(s - lse_ref[hh])
                dpp = _dot_nt(dob, v_ref[hh])
                dsb = (p * (dpp - dlt_ref[hh]) * ATTN_SCALE).astype(BF16)
                _accumulate(dv_ref.at[hh], _dot_tn(p.astype(BF16), dob), diagonal)
                _accumulate(dk_ref.at[hh], _dot_tn(dsb, qq), diagonal)
                dq_ref[hh, pl.ds(pl.multiple_of(i * tq, tq), tq), :] += _dot_nn(dsb, kk)

        @pl.when(i > j)
        def _():
            step(False)

        @pl.when(i == j)
        def _():
            step(True)

    q_map = lambda h, n, kj_ref, qi_ref: (h, qi_ref[n], 0)
    k_map = lambda h, n, kj_ref, qi_ref: (h, kj_ref[n], 0)
    return pl.pallas_call(
        body, name="attn_bwd",
        grid_spec=pltpu.PrefetchScalarGridSpec(
            num_scalar_prefetch=2, grid=(nh // hp, len(pairs)),
            in_specs=[pl.BlockSpec((hp, tq, HEAD_PAD), q_map),
                      pl.BlockSpec((hp, tk, HEAD_PAD), k_map),
                      pl.BlockSpec((hp, tk, V_DIM), k_map),
                      pl.BlockSpec((tq, hp * V_DIM), lambda h, n, kj_ref, qi_ref: (qi_ref[n], h)),
                      pl.BlockSpec((hp, tq, 1), q_map),
                      pl.BlockSpec((hp, tq, 1), q_map)],
            out_specs=[pl.BlockSpec((hp, t, HEAD_PAD), lambda h, n, kj_ref, qi_ref: (h, 0, 0)),
                       pl.BlockSpec((hp, tk, HEAD_PAD), k_map),
                       pl.BlockSpec((hp, tk, V_DIM), k_map)]),
        out_shape=[jax.ShapeDtypeStruct((nh, t, HEAD_PAD), F32), jax.ShapeDtypeStruct((nh, t, HEAD_PAD), F32),
                   jax.ShapeDtypeStruct((nh, t, V_DIM), F32)],
        compiler_params=_cparams(),
    )(kj, qi, q, k, v, do, lse, delta)


def _pool_counts(first_token, rows, w):
    tok = lax.broadcasted_iota(jnp.int32, (rows, POOL_CH), 0) + first_token
    return jnp.minimum(tok + 1, w).astype(F32)


def _pool_centered(zbuf, g, w, i, tm):
    lanes = pl.ds(g * POOL_CH, POOL_CH)
    cur = zbuf[pl.ds(POOL_HALO, tm), lanes]
    win = cur
    for s in range(1, w):
        win = win + zbuf[pl.ds(POOL_HALO - s, tm), lanes]
    return win / _pool_counts(i * tm, tm, w) - cur


def _pool_load(zbuf, z_ref, halo_ref, i, tm):
    @pl.when(i == 0)
    def _():
        zbuf[pl.ds(0, POOL_HALO), :] = jnp.zeros((POOL_HALO, zbuf.shape[1]), F32)

    @pl.when(i > 0)
    def _():
        zbuf[pl.ds(0, POOL_HALO), :] = halo_ref[...]

    zbuf[pl.ds(POOL_HALO, tm), :] = z_ref[...]


def _pool_fwd(z, pool_w, pool_scale):
    t = z.shape[0]
    pw = len(POOL_WINDOWS) * POOL_CH
    tm = min(512, t)
    hb = tm // POOL_HALO

    def body(z_ref, halo_ref, w_ref, sc_ref, p_ref, zbuf):
        i = pl.program_id(0)
        _pool_load(zbuf, z_ref, halo_ref, i, tm)
        for g, w in enumerate(POOL_WINDOWS):
            c = _pool_centered(zbuf, g, w, i, tm)
            y = _dot_nn(c.astype(BF16), w_ref[g]) * sc_ref[:, g * POOL_CH:(g + 1) * POOL_CH]
            p_ref[:, g * POOL_CH:(g + 1) * POOL_CH] = y.astype(BF16)

    return pl.pallas_call(
        body, name="pool_fwd", grid=(t // tm,),
        in_specs=[pl.BlockSpec((tm, pw), lambda i: (i, 1)),
                  pl.BlockSpec((POOL_HALO, pw), lambda i: (jnp.maximum(i * hb - 1, 0), 1)),
                  pl.BlockSpec(pool_w.shape, lambda i: (0, 0, 0)),
                  pl.BlockSpec((1, pw), lambda i: (0, 0))],
        out_specs=pl.BlockSpec((tm, pw), lambda i: (i, 0)),
        out_shape=jax.ShapeDtypeStruct((t, pw), BF16),
        scratch_shapes=[pltpu.VMEM((POOL_HALO + tm, pw), F32)],
        compiler_params=_cparams(),
    )(z, z, pool_w, pool_scale)


def _pool_bwd(dp, z, pool_w, pool_scale):
    t = z.shape[0]
    ng = len(POOL_WINDOWS)
    pw = ng * POOL_CH
    tm = min(512, t)
    hb = tm // POOL_HALO
    nt = t // tm

    def body(dp_ref, dpn_ref, z_ref, halo_ref, w_ref, sc_ref, dz_ref, dw_ref, dsc_ref, zbuf, dbuf):
        i = pl.program_id(0)
        _pool_load(zbuf, z_ref, halo_ref, i, tm)

        @pl.when(i == 0)
        def _():
            dw_ref[...] = jnp.zeros_like(dw_ref)
            dsc_ref[...] = jnp.zeros_like(dsc_ref)

        nxt_ok = (i < nt - 1).astype(F32)
        for g, w in enumerate(POOL_WINDOWS):
            lanes = pl.ds(g * POOL_CH, POOL_CH)
            cols = slice(g * POOL_CH, (g + 1) * POOL_CH)
            sc = sc_ref[:, cols]
            wg = w_ref[g]
            c = _pool_centered(zbuf, g, w, i, tm).astype(BF16)
            ypre = _dot_nn(c, wg)
            dpg = dp_ref[:, cols].astype(F32)
            dsc_ref[:, cols] += jnp.sum(dpg * ypre, axis=0, keepdims=True)
            dyb = (dpg * sc).astype(BF16)
            dw_ref[g] += _dot_tn(c, dyb)
            dd = _dot_nt(dyb, wg)
            dyn = (dpn_ref[:, cols].astype(F32) * sc).astype(BF16)
            ddn = _dot_nt(dyn, wg) * nxt_ok
            dbuf[pl.ds(0, tm), lanes] = dd / _pool_counts(i * tm, tm, w)
            dbuf[pl.ds(tm, POOL_HALO), lanes] = ddn / _pool_counts((i + 1) * tm, POOL_HALO, w)
            acc = -dd
            for s in range(w):
                acc = acc + dbuf[pl.ds(s, tm), lanes]
            dz_ref[:, cols] = acc

    return pl.pallas_call(
        body, name="pool_bwd", grid=(nt,),
        in_specs=[pl.BlockSpec((tm, pw), lambda i: (i, 0)),
                  pl.BlockSpec((POOL_HALO, pw), lambda i: (jnp.minimum((i + 1) * hb, t // POOL_HALO - 1), 0)),
                  pl.BlockSpec((tm, pw), lambda i: (i, 1)),
                  pl.BlockSpec((POOL_HALO, pw), lambda i: (jnp.maximum(i * hb - 1, 0), 1)),
                  pl.BlockSpec(pool_w.shape, lambda i: (0, 0, 0)),
                  pl.BlockSpec((1, pw), lambda i: (0, 0))],
        out_specs=[pl.BlockSpec((tm, pw), lambda i: (i, 0)),
                   pl.BlockSpec((ng, POOL_CH, POOL_CH), lambda i: (0, 0, 0)),
                   pl.BlockSpec((1, pw), lambda i: (0, 0))],
        out_shape=[jax.ShapeDtypeStruct((t, pw), F32), jax.ShapeDtypeStruct((ng, POOL_CH, POOL_CH), F32),
                   jax.ShapeDtypeStruct((1, pw), F32)],
        scratch_shapes=[pltpu.VMEM((POOL_HALO + tm, pw), F32), pltpu.VMEM((tm + POOL_HALO, pw), F32)],
        compiler_params=_cparams(),
    )(dp, dp, z, z, pool_w, pool_scale)


def _mla_bwd(dq_h, dk_h, dv_h, z, dz_pool, h1, dh2, mix_norm, w_in, q_norm, wq_t, kv_norm, wkv, pos, rope_tab):
    t, d = h1.shape
    tm = min(512, t)

    def body(dqh_ref, dkh_ref, dvh_ref, z_ref, dzp_ref, h_ref, dh2_ref, gm_ref, win_ref, gq_ref, wq_ref, gkv_ref,
             wkv_ref, pos_ref, tab_ref, dh1_ref, dq_ref, dkv_ref, dz_ref, dgq_ref, dgkv_ref, dgm_ref):
        i = pl.program_id(0)
        first = i == 0
        cos_t, sin_t = _rope_tables(pos_ref[...], tab_ref[...])
        dq_parts, dkv_parts = [], []
        dk_pe = jnp.zeros((tm, 128), F32)
        for hh in range(MLA_HEADS):
            dqh = dqh_ref[hh]
            dq_parts += [dqh[:, 0:NOPE], _rope_apply_t(dqh[:, NOPE:HEAD_PAD], cos_t, sin_t)]
            dkh = dkh_ref[hh]
            dkv_parts += [dkh[:, 0:NOPE], dvh_ref[hh]]
            dk_pe = dk_pe + dkh[:, NOPE:HEAD_PAD]
        dqb = jnp.concatenate(dq_parts, axis=-1).astype(BF16)
        dkvb = jnp.concatenate(dkv_parts, axis=-1).astype(BF16)
        dq_ref[...] = dqb
        dkv_ref[...] = dkvb
        z = z_ref[...]
        c_q = z[:, 0:Q_RANK]
        gq = gq_ref[...]
        _, rq = _rms_fwd(c_q, gq)
        dcq, dgq = _rms_bwd(_dot_nn(dqb, wq_ref[...]), c_q, gq, rq)
        c_kv = z[:, Q_RANK:Q_RANK + KV_RANK]
        gkv = gkv_ref[...]
        _, rkv = _rms_fwd(c_kv, gkv)
        dckv, dgkv = _rms_bwd(_dot_nt(dkvb, wkv_ref[...]), c_kv, gkv, rkv)
        dkr = _rope_apply_t(dk_pe, cos_t, sin_t)
        dzb = jnp.concatenate([dcq, dckv, dkr, dzp_ref[...]], axis=-1).astype(BF16)
        dz_ref[...] = dzb
        x = h_ref[...]
        gm = gm_ref[...]
        _, rm = _rms_fwd(x, gm)
        dx, dgm = _rms_bwd(_dot_nt(dzb, win_ref[...]), x, gm, rm)
        dh1_ref[...] = dh2_ref[...] + dx
        _accumulate(dgq_ref, dgq, first)
        _accumulate(dgkv_ref, dgkv, first)
        _accumulate(dgm_ref, dgm, first)

    full = lambda shape: pl.BlockSpec(shape, lambda i: (0,) * len(shape))
    row = lambda w: pl.BlockSpec((tm, w), lambda i: (i, 0))
    head = lambda w: pl.BlockSpec((MLA_HEADS, tm, w), lambda i: (0, i, 0))
    pw = len(POOL_WINDOWS) * POOL_CH
    return pl.pallas_call(
        body, name="mla_bwd", grid=(t // tm,),
        in_specs=[head(HEAD_PAD), head(HEAD_PAD), head(V_DIM), row(d), row(pw), row(d), row(d),
                  _resident((1, d)), _resident(w_in.shape), _resident((1, Q_RANK)), _resident(wq_t.shape),
                  _resident((1, KV_RANK)), _resident(wkv.shape), row(1), _resident(rope_tab.shape)],
        out_specs=[row(d), row(d), row(d), row(d), full((1, Q_RANK)), full((1, KV_RANK)), full((1, d))],
        out_shape=[jax.ShapeDtypeStruct((t, d), F32), jax.ShapeDtypeStruct((t, d), BF16),
                   jax.ShapeDtypeStruct((t, d), BF16), jax.ShapeDtypeStruct((t, d), BF16),
                   jax.ShapeDtypeStruct((1, Q_RANK), F32), jax.ShapeDtypeStruct((1, KV_RANK), F32),
                   jax.ShapeDtypeStruct((1, d), F32)],
        compiler_params=_cparams(),
    )(dq_h, dk_h, dv_h, z, dz_pool, h1, dh2, mix_norm, w_in, q_norm, wq_t, kv_norm, wkv, pos, rope_tab)


def _mem_kv(mem, mem_norm, wmkv):
    n, d = mem.shape

    def body(mem_ref, g_ref, w_ref, memn_ref, k_ref, v_ref):
        y, _ = _rms_fwd(mem_ref[...], g_ref[...])
        yb = y.astype(BF16)
        memn_ref[...] = yb
        for hh in range(MEM_HEADS):
            k_ref[hh] = _dot_nn(yb, w_ref[hh]).astype(BF16)
            v_ref[hh] = _dot_nn(yb, w_ref[MEM_HEADS + hh]).astype(BF16)

    return pl.pallas_call(
        body, name="mem_kv",
        out_shape=[jax.ShapeDtypeStruct((n, d), BF16), jax.ShapeDtypeStruct((MEM_HEADS, n, MEM_HD), BF16),
                   jax.ShapeDtypeStruct((MEM_HEADS, n, MEM_HD), BF16)],
        compiler_params=_cparams(),
    )(mem, mem_norm, wmkv)


def _mem_softmax(qb, km):
    s = _dot_nt(qb, km) * MEM_SCALE
    e = jnp.exp(s - jnp.max(s, axis=-1, keepdims=True))
    return e / jnp.sum(e, axis=-1, keepdims=True)


def _xattn_fwd(h1, a, p, w_out, g, wmq, km, vm, wmo, token=None):
    t, d = h1.shape
    tm = min(512, t)
    half = a.shape[1]

    def body(h_ref, a_ref, p_ref, wo_ref, g_ref, wmq_ref, km_ref, vm_ref, wmo_ref,
             h2_ref, h3_ref, hn_ref, q_ref, o_ref):
        h2 = h_ref[...] + _dot_nn(a_ref[...], wo_ref[0:half, :]) + _dot_nn(p_ref[...], wo_ref[half:2 * half, :])
        h2_ref[...] = h2
        hn, _ = _rms_fwd(h2, g_ref[...])
        hnb = hn.astype(BF16)
        hn_ref[...] = hnb
        qb = _dot_nn(hnb, wmq_ref[...]).astype(BF16)
        q_ref[...] = qb
        outs = []
        for hh in range(MEM_HEADS):
            pr = _mem_softmax(qb[:, hh * MEM_HD:(hh + 1) * MEM_HD], km_ref[hh])
            outs.append(_dot_nn(pr.astype(BF16), vm_ref[hh]))
        ob = jnp.concatenate(outs, axis=-1).astype(BF16)
        o_ref[...] = ob
        h3_ref[...] = h2 + _dot_nn(ob, wmo_ref[...])

    full = lambda shape: pl.BlockSpec(shape, lambda i: (0,) * len(shape))
    row = lambda w: pl.BlockSpec((tm, w), lambda i: (i, 0))
    return _call_after(
        token, body,
        [row(d), row(half), row(half), _resident(w_out.shape), _resident((1, d)), _resident(wmq.shape),
         _resident(km.shape), _resident(vm.shape), _resident(wmo.shape)],
        (h1, a, p, w_out, g, wmq, km, vm, wmo),
        name="xattn_fwd", grid=(t // tm,),
        out_specs=[row(d), row(d), row(d), row(d), row(d)],
        out_shape=[jax.ShapeDtypeStruct((t, d), F32), jax.ShapeDtypeStruct((t, d), F32),
                   jax.ShapeDtypeStruct((t, d), BF16), jax.ShapeDtypeStruct((t, d), BF16),
                   jax.ShapeDtypeStruct((t, d), BF16)],
        compiler_params=_cparams(),
    )


def _xattn_bwd(dh3, h2, qm, g, wmq, km, vm, wmo, w_out, token=None):
    t, d = h2.shape
    tm = min(512, t)
    half = d // 2

    def body(dh3_ref, h2_ref, q_ref, g_ref, wmq_ref, km_ref, vm_ref, wmo_ref, wo_ref,
             dh2_ref, dq_ref, da_ref, dp_ref, dk_ref, dv_ref, dg_ref):
        i = pl.program_id(0)
        first = i == 0

        @pl.when(first)
        def _():
            dk_ref[...] = jnp.zeros_like(dk_ref)
            dv_ref[...] = jnp.zeros_like(dv_ref)

        dh3 = dh3_ref[...]
        dob = _dot_nt(dh3.astype(BF16), wmo_ref[...]).astype(BF16)
        qb = q_ref[...]
        dq_parts = []
        for hh in range(MEM_HEADS):
            cols = slice(hh * MEM_HD, (hh + 1) * MEM_HD)
            kk, vv = km_ref[hh], vm_ref[hh]
            pr = _mem_softmax(qb[:, cols], kk)
            doh = dob[:, cols]
            dv_ref[hh] += _dot_tn(pr.astype(BF16), doh)
            dpp = _dot_nt(doh, vv)
            dsb = (pr * (dpp - jnp.sum(dpp * pr, axis=-1, keepdims=True)) * MEM_SCALE).astype(BF16)
            dq_parts.append(_dot_nn(dsb, kk))
            dk_ref[hh] += _dot_tn(dsb, qb[:, cols])
        dqb = jnp.concatenate(dq_parts, axis=-1).astype(BF16)
        dq_ref[...] = dqb
        x = h2_ref[...]
        gg = g_ref[...]
        _, r = _rms_fwd(x, gg)
        dx, dg = _rms_bwd(_dot_nt(dqb, wmq_ref[...]), x, gg, r)
        dh2 = dh3 + dx
        dh2_ref[...] = dh2
        dap = _dot_nt(dh2.astype(BF16), wo_ref[...])
        da_ref[...] = dap[:, 0:half].astype(BF16)
        dp_ref[...] = dap[:, half:d].astype(BF16)
        _accumulate(dg_ref, dg, first)

    full = lambda shape: pl.BlockSpec(shape, lambda i: (0,) * len(shape))
    row = lambda w: pl.BlockSpec((tm, w), lambda i: (i, 0))
    return _call_after(
        token, body,
        [row(d), row(d), row(d), _resident((1, d)), _resident(wmq.shape), _resident(km.shape), _resident(vm.shape),
         _resident(wmo.shape), _resident(w_out.shape)],
        (dh3, h2, qm, g, wmq, km, vm, wmo, w_out),
        name="xattn_bwd", grid=(t // tm,),
        out_specs=[row(d), row(d), row(half), row(half), full(km.shape), full(vm.shape), full((1, d))],
        out_shape=[jax.ShapeDtypeStruct((t, d), F32), jax.ShapeDtypeStruct((t, d), BF16),
                   jax.ShapeDtypeStruct((t, half), BF16), jax.ShapeDtypeStruct((t, half), BF16),
                   jax.ShapeDtypeStruct(km.shape, F32), jax.ShapeDtypeStruct(vm.shape, F32),
                   jax.ShapeDtypeStruct((1, d), F32)],
        compiler_params=_cparams(),
    )


def _mem_kv_bwd(dkm, dvm, memn, mem, mem_norm, wmkv):
    n, d = mem.shape

    def body(dk_ref, dv_ref, memn_ref, mem_ref, g_ref, w_ref, dw_ref, dg_ref):
        memn = memn_ref[...]
        dmemn = jnp.zeros((n, d), F32)
        for s in range(2 * MEM_HEADS):
            src = dk_ref[s] if s < MEM_HEADS else dv_ref[s - MEM_HEADS]
            db = src.astype(BF16)
            dw_ref[s] = _dot_tn(memn, db)
            dmemn = dmemn + _dot_nt(db, w_ref[s])
        x = mem_ref[...]
        gg = g_ref[...]
        _, r = _rms_fwd(x, gg)
        _, dg = _rms_bwd(dmemn, x, gg, r)
        dg_ref[...] = dg

    return pl.pallas_call(
        body, name="mem_kv_bwd",
        out_shape=[jax.ShapeDtypeStruct(wmkv.shape, F32), jax.ShapeDtypeStruct((1, d), F32)],
        compiler_params=_cparams(),
    )(dkm, dvm, memn, mem, mem_norm, wmkv)


MESH_ID = pl.DeviceIdType.MESH
ANY = pl.BlockSpec(memory_space=pl.ANY)


def _coords():
    return lax.axis_index("x"), lax.axis_index("y"), lax.axis_index("c")


def _other_chips(x, y):
    return [(1 - x, y), (x, 1 - y), (1 - x, 1 - y)]


def _core_reduce(g, tag):
    _, r, w = g.shape

    def body(g_ref, part_ref, own_sc, recv_sc, send_sems, recv_sems, local_sems):
        x, y, c = _coords()
        sent, local = [], []
        for chip in range(4):
            sent.append(pltpu.make_async_remote_copy(
                src_ref=g_ref.at[2 * chip + (1 - c)], dst_ref=recv_sc.at[chip],
                send_sem=send_sems.at[chip], recv_sem=recv_sems.at[chip],
                device_id=(x, y, 1 - c), device_id_type=MESH_ID))
            local.append(pltpu.make_async_copy(g_ref.at[2 * chip + c], own_sc.at[chip], local_sems.at[chip]))
        for cp in sent + local:
            cp.start()
        for chip in range(4):
            local[chip].wait()
            sent[chip].wait_recv()
            part_ref[chip] = (own_sc[chip].astype(F32) + recv_sc[chip].astype(F32)).astype(part_ref.dtype)
        for cp in sent:
            cp.wait_send()

    return pl.pallas_call(
        body, name="core_reduce_" + tag,
        out_shape=jax.ShapeDtypeStruct((4, r, w), g.dtype),
        in_specs=[ANY], out_specs=pl.BlockSpec(memory_space=pltpu.VMEM),
        scratch_shapes=[pltpu.VMEM((4, r, w), g.dtype), pltpu.VMEM((4, r, w), g.dtype),
                        pltpu.SemaphoreType.DMA((4,)), pltpu.SemaphoreType.DMA((4,)), pltpu.SemaphoreType.DMA((4,))],
        compiler_params=_cparams(),
    )(g)


HBM_SPEC = pl.BlockSpec(memory_space=pltpu.HBM)
SEM_SPEC = pl.BlockSpec(memory_space=pltpu.SEMAPHORE)
SPLIT_EFFECT = pltpu.SideEffectType.DATAFLOW_SIDE_EFFECTING


def _ici_refs(gather, src_ref, land_ref, j, px, py, slot_chip, c):
    if gather:
        return src_ref, land_ref.at[:, 4 * slot_chip[0] + 2 * slot_chip[1] + c]
    return src_ref.at[2 * px + py], land_ref.at[j]


def _ici_start(src, after, name, gather):
    r, w = src.shape[-2:]
    land_shape = (src.shape[0], N_DEV, r, w) if gather else (3, r, w)

    def body(src_ref, land_ref, after_ref, send_sems, recv_sems, src_thru, land_thru, token):
        x, y, c = _coords()
        for j, (px, py) in enumerate(_other_chips(x, y)):
            s_ref, d_ref = _ici_refs(gather, src_ref, land_ref, j, px, py, (x, y), c)
            pltpu.make_async_remote_copy(
                src_ref=s_ref, dst_ref=d_ref, send_sem=send_sems.at[j], recv_sem=recv_sems.at[j],
                device_id=(px, py, c), device_id_type=MESH_ID).start()
        token[...] = jnp.zeros_like(token)

    return pl.pallas_call(
        body, name=name,
        out_shape=(pltpu.SemaphoreType.DMA((3,)), pltpu.SemaphoreType.DMA((3,)), pltpu.HBM(src.shape, src.dtype),
                   pltpu.HBM(land_shape, src.dtype), jax.ShapeDtypeStruct((8, 128), F32)),
        in_specs=(HBM_SPEC, HBM_SPEC, ANY),
        out_specs=(SEM_SPEC, SEM_SPEC, HBM_SPEC, HBM_SPEC, pl.BlockSpec(memory_space=pltpu.VMEM)),
        input_output_aliases={0: 2, 1: 3},
        compiler_params=pltpu.CompilerParams(has_side_effects=SPLIT_EFFECT),
    )(pltpu.with_memory_space_constraint(src, pltpu.HBM),
      pltpu.with_memory_space_constraint(lax.empty(land_shape, src.dtype), pltpu.HBM), after)


def _ici_wait(started, after, name, gather):
    send_sems, recv_sems, src_thru, land_thru, _ = started

    def body(src_ref, land_ref, send_sems, recv_sems, after_ref, src_dead, got_ref):
        x, y, c = _coords()
        for j, (px, py) in enumerate(_other_chips(x, y)):
            s_ref, d_ref = _ici_refs(gather, src_ref, land_ref, j, px, py, (px, py), c)
            copy = pltpu.make_async_remote_copy(
                src_ref=s_ref, dst_ref=d_ref, send_sem=send_sems.at[j], recv_sem=recv_sems.at[j],
                device_id=(px, py, c), device_id_type=MESH_ID)
            copy.wait_send()
            copy.wait_recv()

    return pl.pallas_call(
        body, name=name,
        out_shape=(pltpu.HBM(src_thru.shape, src_thru.dtype), pltpu.HBM(land_thru.shape, land_thru.dtype)),
        in_specs=(HBM_SPEC, HBM_SPEC, SEM_SPEC, SEM_SPEC, ANY),
        out_specs=(HBM_SPEC, HBM_SPEC), input_output_aliases={0: 0, 1: 1},
        compiler_params=pltpu.CompilerParams(has_side_effects=SPLIT_EFFECT),
    )(src_thru, land_thru, send_sems, recv_sems, after)


def _neighbour(k, x, y):
    return (1 - x, y) if k == 0 else (x, 1 - y)


def _slot(ref, px, py, c):
    return ref.at[:, 4 * px + 2 * py + c]


def _near_start(src, after, name):
    land_shape = (src.shape[0], N_DEV) + src.shape[1:]

    def body(src_ref, land_ref, after_ref, send_sems, recv_sems, src_thru, land_thru, token):
        x, y, c = _coords()
        for k in range(2):
            px, py = _neighbour(k, x, y)
            pltpu.make_async_remote_copy(
                src_ref=src_ref, dst_ref=_slot(land_ref, x, y, c), send_sem=send_sems.at[k],
                recv_sem=recv_sems.at[k], device_id=(px, py, c), device_id_type=MESH_ID).start()
        token[...] = jnp.zeros_like(token)

    return pl.pallas_call(
        body, name=name,
        out_shape=(pltpu.SemaphoreType.DMA((2,)), pltpu.SemaphoreType.DMA((2,)), pltpu.HBM(src.shape, src.dtype),
                   pltpu.HBM(land_shape, src.dtype), jax.ShapeDtypeStruct((8, 128), F32)),
        in_specs=(HBM_SPEC, HBM_SPEC, ANY),
        out_specs=(SEM_SPEC, SEM_SPEC, HBM_SPEC, HBM_SPEC, pl.BlockSpec(memory_space=pltpu.VMEM)),
        input_output_aliases={0: 2, 1: 3},
        compiler_params=pltpu.CompilerParams(has_side_effects=SPLIT_EFFECT),
    )(pltpu.with_memory_space_constraint(src, pltpu.HBM),
      pltpu.with_memory_space_constraint(lax.empty(land_shape, src.dtype), pltpu.HBM), after)


def _near_wait(started, after, name):
    send_sems, recv_sems, src_thru, land_thru, _ = started

    def body(src_ref, land_ref, send_sems, recv_sems, after_ref, src_dead, got_ref):
        x, y, c = _coords()
        for k in range(2):
            px, py = _neighbour(k, x, y)
            copy = pltpu.make_async_remote_copy(
                src_ref=src_ref, dst_ref=_slot(land_ref, px, py, c), send_sem=send_sems.at[k],
                recv_sem=recv_sems.at[k], device_id=(px, py, c), device_id_type=MESH_ID)
            copy.wait_send()
            copy.wait_recv()

    return pl.pallas_call(
        body, name=name,
        out_shape=(pltpu.HBM(src_thru.shape, src_thru.dtype), pltpu.HBM(land_thru.shape, land_thru.dtype)),
        in_specs=(HBM_SPEC, HBM_SPEC, SEM_SPEC, SEM_SPEC, ANY),
        out_specs=(HBM_SPEC, HBM_SPEC), input_output_aliases={0: 0, 1: 1},
        compiler_params=pltpu.CompilerParams(has_side_effects=SPLIT_EFFECT),
    )(src_thru, land_thru, send_sems, recv_sems, after)


def _far_refs(land_ref, k, x, y, c, arriving):
    half = land_ref.shape[2] // 2
    rows = pl.ds(k * half, half)
    ox, oy = (1 - x, 1 - y) if arriving else _neighbour(k, x, y)
    return land_ref.at[:, 4 * ox + 2 * oy + c, rows]


def _far_start(land, after, name):
    def body(land_ref, after_ref, send_sems, recv_sems, land_thru, token):
        x, y, c = _coords()
        for k in range(2):
            block = _far_refs(land_ref, k, x, y, c, False)
            px, py = _neighbour(1 - k, x, y)
            pltpu.make_async_remote_copy(
                src_ref=block, dst_ref=block, send_sem=send_sems.at[k], recv_sem=recv_sems.at[k],
                device_id=(px, py, c), device_id_type=MESH_ID).start()
        token[...] = jnp.zeros_like(token)

    return pl.pallas_call(
        body, name=name,
        out_shape=(pltpu.SemaphoreType.DMA((2,)), pltpu.SemaphoreType.DMA((2,)),
                   pltpu.HBM(land.shape, land.dtype), jax.ShapeDtypeStruct((8, 128), F32)),
        in_specs=(HBM_SPEC, ANY),
        out_specs=(SEM_SPEC, SEM_SPEC, HBM_SPEC, pl.BlockSpec(memory_space=pltpu.VMEM)),
        input_output_aliases={0: 2},
        compiler_params=pltpu.CompilerParams(has_side_effects=SPLIT_EFFECT),
    )(pltpu.with_memory_space_constraint(land, pltpu.HBM), after)


def _far_wait(started, after, name):
    send_sems, recv_sems, land_thru, _ = started

    def body(land_ref, send_sems, recv_sems, after_ref, got_ref):
        x, y, c = _coords()
        for k in range(2):
            px, py = _neighbour(1 - k, x, y)
            copy = pltpu.make_async_remote_copy(
                src_ref=_far_refs(land_ref, k, x, y, c, False), dst_ref=_far_refs(land_ref, k, x, y, c, True),
                send_sem=send_sems.at[k], recv_sem=recv_sems.at[k], device_id=(px, py, c), device_id_type=MESH_ID)
            copy.wait_send()
            copy.wait_recv()

    return pl.pallas_call(
        body, name=name,
        out_shape=pltpu.HBM(land_thru.shape, land_thru.dtype),
        in_specs=(HBM_SPEC, SEM_SPEC, SEM_SPEC, ANY),
        out_specs=HBM_SPEC, input_output_aliases={0: 0},
        compiler_params=pltpu.CompilerParams(has_side_effects=SPLIT_EFFECT),
    )(land_thru, send_sems, recv_sems, after)


def _peer(k, x, y, c):
    return x ^ ((k >> 2) & 1), y ^ ((k >> 1) & 1), c ^ (k & 1)


def _peers_start(src, after, name):
    r, w = src.shape
    x, y, c = _coords()
    land = lax.dynamic_update_slice(jnp.zeros((N_DEV, r, w), src.dtype), src[None], (4 * x + 2 * y + c, 0, 0))

    def body(src_ref, land_ref, after_ref, send_sems, recv_sems, src_thru, land_thru, token):
        x, y, c = _coords()
        for k in range(1, N_DEV):
            pltpu.make_async_remote_copy(
                src_ref=src_ref, dst_ref=land_ref.at[4 * x + 2 * y + c],
                send_sem=send_sems.at[k - 1], recv_sem=recv_sems.at[k - 1],
                device_id=_peer(k, x, y, c), device_id_type=MESH_ID).start()
        token[...] = jnp.zeros_like(token)

    return pl.pallas_call(
        body, name=name,
        out_shape=(pltpu.SemaphoreType.DMA((N_DEV - 1,)), pltpu.SemaphoreType.DMA((N_DEV - 1,)),
                   pltpu.HBM(src.shape, src.dtype), pltpu.HBM(land.shape, src.dtype),
                   jax.ShapeDtypeStruct((8, 128), F32)),
        in_specs=(HBM_SPEC, HBM_SPEC, ANY),
        out_specs=(SEM_SPEC, SEM_SPEC, HBM_SPEC, HBM_SPEC, pl.BlockSpec(memory_space=pltpu.VMEM)),
        input_output_aliases={0: 2, 1: 3},
        compiler_params=pltpu.CompilerParams(has_side_effects=SPLIT_EFFECT),
    )(pltpu.with_memory_space_constraint(src, pltpu.HBM), pltpu.with_memory_space_constraint(land, pltpu.HBM), after)


def _peers_wait(started, after, name):
    send_sems, recv_sems, src_thru, land_thru, _ = started

    def body(src_ref, land_ref, send_sems, recv_sems, after_ref, src_dead, got_ref):
        x, y, c = _coords()
        for k in range(1, N_DEV):
            px, py, pc = _peer(k, x, y, c)
            copy = pltpu.make_async_remote_copy(
                src_ref=src_ref, dst_ref=land_ref.at[4 * px + 2 * py + pc],
                send_sem=send_sems.at[k - 1], recv_sem=recv_sems.at[k - 1],
                device_id=(px, py, pc), device_id_type=MESH_ID)
            copy.wait_send()
            copy.wait_recv()

    return pl.pallas_call(
        body, name=name,
        out_shape=(pltpu.HBM(src_thru.shape, src_thru.dtype), pltpu.HBM(land_thru.shape, land_thru.dtype)),
        in_specs=(HBM_SPEC, HBM_SPEC, SEM_SPEC, SEM_SPEC, ANY),
        out_specs=(HBM_SPEC, HBM_SPEC), input_output_aliases={0: 0, 1: 1},
        compiler_params=pltpu.CompilerParams(has_side_effects=SPLIT_EFFECT),
    )(src_thru, land_thru, send_sems, recv_sems, after)


def _share_refs(ref, k, x, y, c, sender_c):
    px, py = ([(x, y)] + _other_chips(x, y))[k]
    return ref.at[:, 4 * px + 2 * py + sender_c]


def _share_start(gathered, after, name):
    def body(g_ref, after_ref, send_sems, recv_sems, g_thru, token):
        x, y, c = _coords()
        for k in range(4):
            slot = _share_refs(g_ref, k, x, y, c, c)
            pltpu.make_async_remote_copy(
                src_ref=slot, dst_ref=slot, send_sem=send_sems.at[k], recv_sem=recv_sems.at[k],
                device_id=(x, y, 1 - c), device_id_type=MESH_ID).start()
        token[...] = jnp.zeros_like(token)

    return pl.pallas_call(
        body, name=name,
        out_shape=(pltpu.SemaphoreType.DMA((4,)), pltpu.SemaphoreType.DMA((4,)),
                   pltpu.HBM(gathered.shape, gathered.dtype), jax.ShapeDtypeStruct((8, 128), F32)),
        in_specs=(HBM_SPEC, ANY),
        out_specs=(SEM_SPEC, SEM_SPEC, HBM_SPEC, pl.BlockSpec(memory_space=pltpu.VMEM)),
        input_output_aliases={0: 2},
        compiler_params=pltpu.CompilerParams(has_side_effects=SPLIT_EFFECT),
    )(pltpu.with_memory_space_constraint(gathered, pltpu.HBM), after)


def _share_wait(started, after, name):
    send_sems, recv_sems, g_thru, _ = started

    def body(g_ref, send_sems, recv_sems, after_ref, got_ref):
        x, y, c = _coords()
        for k in range(4):
            copy = pltpu.make_async_remote_copy(
                src_ref=_share_refs(g_ref, k, x, y, c, c), dst_ref=_share_refs(g_ref, k, x, y, c, 1 - c),
                send_sem=send_sems.at[k], recv_sem=recv_sems.at[k],
                device_id=(x, y, 1 - c), device_id_type=MESH_ID)
            copy.wait_send()
            copy.wait_recv()

    return pl.pallas_call(
        body, name=name,
        out_shape=pltpu.HBM(g_thru.shape, g_thru.dtype),
        in_specs=(HBM_SPEC, SEM_SPEC, SEM_SPEC, ANY),
        out_specs=HBM_SPEC, input_output_aliases={0: 0},
        compiler_params=pltpu.CompilerParams(has_side_effects=SPLIT_EFFECT),
    )(g_thru, send_sems, recv_sems, after)


def _core_share(own, gathered, name):
    def body(own_ref, gin_ref, out_ref, stage, send_sems, recv_sems, local_sem):
        x, y, c = _coords()
        sibling = (x, y, 1 - c)
        chips = [(x, y)] + _other_chips(x, y)
        stage_in = pltpu.make_async_copy(own_ref, stage, local_sem)
        stage_in.start()
        sent, arriving = [], []
        for k, (px, py) in enumerate(chips):
            slot = out_ref.at[:, 4 * px + 2 * py + c]
            sent.append(pltpu.make_async_remote_copy(
                src_ref=own_ref if k == 0 else slot, dst_ref=slot,
                send_sem=send_sems.at[k], recv_sem=recv_sems.at[k], device_id=sibling, device_id_type=MESH_ID))
            arriving.append(pltpu.make_async_remote_copy(
                src_ref=own_ref, dst_ref=out_ref.at[:, 4 * px + 2 * py + (1 - c)],
                send_sem=send_sems.at[k], recv_sem=recv_sems.at[k], device_id=sibling, device_id_type=MESH_ID))
        for cp in sent:
            cp.start()
        stage_in.wait()
        stage_out = pltpu.make_async_copy(stage, out_ref.at[:, 4 * x + 2 * y + c], local_sem)
        stage_out.start()
        for cp in arriving:
            cp.wait_recv()
        for cp in sent:
            cp.wait_send()
        stage_out.wait()

    return pl.pallas_call(
        body, name=name,
        out_shape=jax.ShapeDtypeStruct(gathered.shape, own.dtype),
        in_specs=[ANY, ANY], out_specs=ANY, input_output_aliases={1: 0},
        scratch_shapes=[pltpu.VMEM(own.shape, own.dtype), pltpu.SemaphoreType.DMA((4,)),
                        pltpu.SemaphoreType.DMA((4,)), pltpu.SemaphoreType.DMA],
    )(own, gathered)


def _adamw(w, g, m, v):
    m = ADAM_B1 * m + (1.0 - ADAM_B1) * g
    v = ADAM_B2 * v + (1.0 - ADAM_B2) * (g * g)
    m_hat = m / ADAM_C1
    v_hat = v / ADAM_C2
    delta = -ADAM_LR * (m_hat / (jnp.sqrt(v_hat) + ADAM_EPS) + ADAM_WD * w)
    return delta, m, v


def _adam_big(units, chip_idx, tag, token):
    n = len(units)
    r, wd = units[0][2].shape
    tr, tw = _row_tile(r, 1024), 256

    def body(s_ref, tok_ref, *refs):
        for u in range(n):
            p_ref, l_ref, w_ref, m_ref, v_ref = refs[5 * u:5 * u + 5]
            g_ref, d_ref, mo_ref, vo_ref = refs[5 * n + 4 * u:5 * n + 4 * u + 4]
            g = p_ref[0].astype(F32)
            for j in range(3):
                g = g + l_ref[j].astype(F32)
            delta, mn, vn = _adamw(w_ref[...], g, m_ref[...], v_ref[...])
            g_ref[...] = g
            d_ref[...] = delta
            mo_ref[...] = mn
            vo_ref[...] = vn

    row = pl.BlockSpec((tr, tw), lambda i, j, s: (i, j))
    unit_specs = [pl.BlockSpec((1, tr, tw), lambda i, j, s: (s[0], i, j)),
                  pl.BlockSpec((3, tr, tw), lambda i, j, s: (0, i, j)), row, row, row]
    outs = pl.pallas_call(
        body, name="adam_big_" + tag,
        grid_spec=pltpu.PrefetchScalarGridSpec(
            num_scalar_prefetch=1, grid=(r // tr, wd // tw),
            in_specs=[pl.BlockSpec((8, 128), lambda i, j, s: (0, 0))] + unit_specs * n,
            out_specs=[row] * (4 * n)),
        out_shape=[jax.ShapeDtypeStruct((r, wd), F32)] * (4 * n),
        compiler_params=_cparams(),
    )(chip_idx, token, *[a for unit in units for a in unit])
    return [outs[4 * u:4 * u + 4] for u in range(n)]


def _adam_small(parts, w, m, v):
    _, r, wd = parts.shape

    def body(p_ref, w_ref, m_ref, v_ref, g_ref, d_ref, mo_ref, vo_ref):
        g = p_ref[0]
        for k in range(1, N_DEV):
            g = g + p_ref[k]
        delta, mn, vn = _adamw(w_ref[...], g, m_ref[...], v_ref[...])
        g_ref[...] = g
        d_ref[...] = delta
        mo_ref[...] = mn
        vo_ref[...] = vn

    return pl.pallas_call(
        body, name="adam_small",
        out_shape=[jax.ShapeDtypeStruct((r, wd), F32)] * 4,
        compiler_params=_cparams(),
    )(parts, w, m, v)


def _pad_rows(a, rows):
    return jnp.pad(a, ((0, rows - a.shape[0]), (0, 0)))


def _pad_w_in(w):
    cut = Q_RANK + KV_RANK + ROPE
    return jnp.concatenate([w[:, :cut], jnp.zeros((w.shape[0], 64), w.dtype), w[:, cut:]], axis=1)


def _unpad_w_in(w):
    cut = Q_RANK + KV_RANK + ROPE
    return jnp.concatenate([w[:, :cut], w[:, cut + 64:]], axis=1)


def _pack_mid(p):
    parts = [_pad_w_in(p["w_in"][0]), p["w_out"][0], p["w_mq"][0], p["w_mo"][0],
             p["w_mkv"][0].reshape(256, D_MODEL),
             _pad_rows(p["w_q_up"][0].T.reshape(24, D_MODEL), 32),
             p["w_kv_up"][0].reshape(16, D_MODEL)]
    return jnp.concatenate(parts, axis=0)


def _pack_ffn(w_gate, w_up, w_down, name):
    d, rows = w_gate.shape[1:]

    def body(g_ref, u_ref, d_ref, o_ref):
        eye = (lax.broadcasted_iota(jnp.int32, (d, d), 0) == lax.broadcasted_iota(jnp.int32, (d, d), 1)).astype(BF16)
        o_ref[0] = _dot_tn(g_ref[0].astype(BF16), eye).astype(BF16)
        o_ref[1] = _dot_tn(u_ref[0].astype(BF16), eye).astype(BF16)
        o_ref[2] = d_ref[0].astype(BF16)

    return pl.pallas_call(
        body, name=name, out_shape=jax.ShapeDtypeStruct((3, rows, d), BF16), compiler_params=_cparams(),
    )(w_gate, w_up, w_down)


def _pack_segments(p, group):
    if group == "mid":
        return _pack_mid(p)[None].astype(BF16)
    return _pack_ffn(p[group + "_w_gate"], p[group + "_w_up"], p[group + "_w_down"], "pack_" + group)


UNIT_WEIGHT = {"ffn1_g": ("ffn1_w_gate", True), "ffn1_u": ("ffn1_w_up", True), "ffn1_d": ("ffn1_w_down", False),
               "ffn2_g": ("ffn2_w_gate", True), "ffn2_u": ("ffn2_w_up", True), "ffn2_d": ("ffn2_w_down", False)}


def _pack_unit(p, unit):
    if unit == "mid":
        return _pack_mid(p)
    name, transposed = UNIT_WEIGHT[unit]
    return p[name][0].T if transposed else p[name][0]


def _unpack_unit(a, unit):
    if unit != "mid":
        name, transposed = UNIT_WEIGHT[unit]
        return {name: (a.T if transposed else a)[None]}
    seg = lambda n: a[SEG_OFF[n][0]:SEG_OFF[n][0] + SEG_OFF[n][1]]
    return {"w_in": _unpad_w_in(seg("w_in"))[None], "w_out": seg("w_out")[None], "w_mq": seg("w_mq")[None],
            "w_mo": seg("w_mo")[None], "w_mkv": seg("w_mkv").reshape(D_MODEL, 256)[None],
            "w_q_up": seg("w_q")[:24].reshape(96, Q_RANK).T[None],
            "w_kv_up": seg("w_kv").reshape(KV_RANK, 128)[None]}


def _unpack_gathered(full, group):
    if group != "mid":
        return {group: full.reshape(len(GROUP_SEGS[group]), -1, D_MODEL)}
    full = full[0]
    seg = lambda n: full[:, SEG_OFF[n][0]:SEG_OFF[n][0] + SEG_OFF[n][1]]
    rows = lambda n: seg(n).reshape(-1, D_MODEL)
    wq_t = seg("w_q")[:, :24].reshape(MLA_HEADS, NOPE + ROPE, Q_RANK)
    wq_t = jnp.pad(wq_t, ((0, 0), (0, HEAD_PAD - NOPE - ROPE), (0, 0))).reshape(MLA_HEADS * HEAD_PAD, Q_RANK)
    wkv = seg("w_kv").reshape(N_DEV, KV_RANK, 128).transpose(1, 0, 2).reshape(KV_RANK, N_DEV * 128)
    return {"w_in": rows("w_in"), "w_out": rows("w_out"), "w_mq": rows("w_mq"), "w_mo": rows("w_mo"),
            "w_mkv": seg("w_mkv").reshape(N_DEV, D_MODEL, 256), "w_q": wq_t, "w_kv": wkv}


def _pack_grads(gr):
    blk = lambda a: a.reshape(N_DEV, -1, D_MODEL)
    dwq = gr["w_q"].reshape(MLA_HEADS, HEAD_PAD, Q_RANK)[:, :NOPE + ROPE].reshape(N_DEV, 24, D_MODEL)
    dwq = jnp.pad(dwq, ((0, 0), (0, 8), (0, 0)))
    dwkv = gr["w_kv"].reshape(KV_RANK, N_DEV, 128).transpose(1, 0, 2).reshape(N_DEV, 16, D_MODEL)
    parts = [blk(gr["w_in"]), blk(gr["w_out"]), blk(gr["w_mq"]), blk(gr["w_mo"]),
             gr["w_mkv"].reshape(N_DEV, 256, D_MODEL), dwq, dwkv]
    return jnp.concatenate([a.astype(BF16) for a in parts], axis=1)


def _pack_small(vals):
    parts = []
    for n, r in SMALL_ROWS:
        parts.append(_pad_rows(vals[n].reshape(-1, 128), r) if n in vals else jnp.zeros((r, 128), F32))
    return jnp.concatenate(parts, axis=0)


def _unpack_small(a, shapes):
    out = {}
    for n, shape in shapes.items():
        o = SMALL_OFF[n][0]
        out[n] = a[o:o + int(np.prod(shape)) // 128].reshape(shape)
    return out


BIG_NAMES = ("ffn1_w_gate", "ffn1_w_up", "ffn1_w_down", "w_in", "w_q_up", "w_kv_up", "w_out", "w_mq", "w_mkv",
             "w_mo", "ffn2_w_gate", "ffn2_w_up", "ffn2_w_down")
SMALL_NAMES = ("ffn1_norm", "mix_norm", "q_norm", "kv_norm", "pool_w", "pool_scale", "xattn_norm", "mem_norm",
               "ffn2_norm", "final_norm")
WEIGHT_ORDER = ("ffn1_norm", "ffn1_w_gate", "ffn1_w_up", "ffn1_w_down", "mix_norm", "w_in", "q_norm", "w_q_up",
                "kv_norm", "w_kv_up", "pool_w", "pool_scale", "w_out", "xattn_norm", "mem_norm", "w_mq", "w_mkv",
                "w_mo", "ffn2_norm", "ffn2_w_gate", "ffn2_w_up", "ffn2_w_down", "final_norm")


def _rope_table():
    lane = np.arange(128)
    freqs = (1.0 / (ROPE_BASE ** (np.arange(0, ROPE, 2, dtype=np.float32) / ROPE))).astype(np.float32)
    tab = np.zeros((8, 128), np.float32)
    tab[0] = np.where(lane < ROPE, freqs[lane % (ROPE // 2)], 0.0)
    tab[1] = np.where(lane < ROPE // 2, -1.0, np.where(lane < ROPE, 1.0, 0.0))
    return jnp.asarray(tab)


def kernel(x, mem, positions, ffn1_norm, ffn1_w_gate, ffn1_w_up, ffn1_w_down, mix_norm, w_in, q_norm, w_q_up, kv_norm, w_kv_up, pool_w, pool_scale, w_out, xattn_norm, mem_norm, w_mq, w_mkv, w_mo, ffn2_norm, ffn2_w_gate, ffn2_w_up, ffn2_w_down, final_norm, loss_target, m_ffn1_norm, m_ffn1_w_gate, m_ffn1_w_up, m_ffn1_w_down, m_mix_norm, m_w_in, m_q_norm, m_w_q_up, m_kv_norm, m_w_kv_up, m_pool_w, m_pool_scale, m_w_out, m_xattn_norm, m_mem_norm, m_w_mq, m_w_mkv, m_w_mo, m_ffn2_norm, m_ffn2_w_gate, m_ffn2_w_up, m_ffn2_w_down, m_final_norm, v_ffn1_norm, v_ffn1_w_gate, v_ffn1_w_up, v_ffn1_w_down, v_mix_norm, v_w_in, v_q_norm, v_w_q_up, v_kv_norm, v_w_kv_up, v_pool_w, v_pool_scale, v_w_out, v_xattn_norm, v_mem_norm, v_w_mq, v_w_mkv, v_w_mo, v_ffn2_norm, v_ffn2_w_gate, v_ffn2_w_up, v_ffn2_w_down, v_final_norm):
    wts = dict(ffn1_norm=ffn1_norm, ffn1_w_gate=ffn1_w_gate, ffn1_w_up=ffn1_w_up, ffn1_w_down=ffn1_w_down,
               mix_norm=mix_norm, w_in=w_in, q_norm=q_norm, w_q_up=w_q_up, kv_norm=kv_norm, w_kv_up=w_kv_up,
               pool_w=pool_w, pool_scale=pool_scale, w_out=w_out, xattn_norm=xattn_norm, mem_norm=mem_norm,
               w_mq=w_mq, w_mkv=w_mkv, w_mo=w_mo, ffn2_norm=ffn2_norm, ffn2_w_gate=ffn2_w_gate,
               ffn2_w_up=ffn2_w_up, ffn2_w_down=ffn2_w_down, final_norm=final_norm)
    mom = dict(ffn1_norm=m_ffn1_norm, ffn1_w_gate=m_ffn1_w_gate, ffn1_w_up=m_ffn1_w_up, ffn1_w_down=m_ffn1_w_down,
               mix_norm=m_mix_norm, w_in=m_w_in, q_norm=m_q_norm, w_q_up=m_w_q_up, kv_norm=m_kv_norm,
               w_kv_up=m_w_kv_up, pool_w=m_pool_w, pool_scale=m_pool_scale, w_out=m_w_out, xattn_norm=m_xattn_norm,
               mem_norm=m_mem_norm, w_mq=m_w_mq, w_mkv=m_w_mkv, w_mo=m_w_mo, ffn2_norm=m_ffn2_norm,
               ffn2_w_gate=m_ffn2_w_gate, ffn2_w_up=m_ffn2_w_up, ffn2_w_down=m_ffn2_w_down, final_norm=m_final_norm)
    var = dict(ffn1_norm=v_ffn1_norm, ffn1_w_gate=v_ffn1_w_gate, ffn1_w_up=v_ffn1_w_up, ffn1_w_down=v_ffn1_w_down,
               mix_norm=v_mix_norm, w_in=v_w_in, q_norm=v_q_norm, w_q_up=v_w_q_up, kv_norm=v_kv_norm,
               w_kv_up=v_w_kv_up, pool_w=v_pool_w, pool_scale=v_pool_scale, w_out=v_w_out, xattn_norm=v_xattn_norm,
               mem_norm=v_mem_norm, w_mq=v_w_mq, w_mkv=v_w_mkv, w_mo=v_w_mo, ffn2_norm=v_ffn2_norm,
               ffn2_w_gate=v_ffn2_w_gate, ffn2_w_up=v_ffn2_w_up, ffn2_w_down=v_ffn2_w_down, final_norm=v_final_norm)

    t = x.shape[1]
    xs = x[0]
    mems = mem[0]
    target = loss_target[0]
    pos = positions.reshape(t, 1)
    row = lambda a: a.reshape(1, -1)
    rope_tab = _rope_table()

    cx, cy, cc = _coords()
    chip_idx = (2 * cx + cy).astype(jnp.int32).reshape(1)

    wb = {}
    for grp in ("ffn1", "mid", "ffn2"):
        wb[grp] = _pack_segments(wts, grp)
        if grp == "ffn1":
            near_ffn1 = _near_start(wb["ffn1"], pos, "ag_ffn1_near_start")
    def packed_during(token, units, after):
        held = lax.optimization_barrier((token, [{n: p[n] for n in BIG_NAMES + SMALL_NAMES} for p in (wts, mom, var)]))[1]
        packs = {u: tuple(_pack_small({n: p[n] for n in SMALL_NAMES}) if u == "small" else _pack_unit(p, u) for p in held)
                 for u in units}
        return lax.optimization_barrier((after, packs))

    after, adam_in = packed_during(near_ffn1[4], ("ffn1_g", "ffn1_u", "ffn1_d", "ffn2_g", "ffn2_u", "ffn2_d"), wb["ffn2"])
    own_ffn1, land_ffn1 = _near_wait(near_ffn1, after, "ag_ffn1_near_wait")
    far_ffn1 = _far_start(land_ffn1, own_ffn1, "ag_ffn1_far_start")
    after, more = packed_during(far_ffn1[3], ("mid", "small"), wb["mid"])
    adam_in.update(more)
    land_ffn1 = _far_wait(far_ffn1, after, "ag_ffn1_far_wait")
    full_ffn1 = _core_share(own_ffn1, land_ffn1, "ag_ffn1_share")
    fw = _unpack_gathered(full_ffn1, "ffn1")
    ag_mid = _ici_start(wb["mid"], full_ffn1, "ag_mid_start", True)
    g_ffn1, g_mix, g_q, g_kv = row(ffn1_norm), row(mix_norm), row(q_norm), row(kv_norm)
    g_x, g_mem, g_ffn2, g_fin = row(xattn_norm), row(mem_norm), row(ffn2_norm), row(final_norm)
    pool_wb = pool_w[0].astype(BF16)
    pool_sc = row(pool_scale)

    h1, n1, gate1, up1 = _ffn_fwd(xs, g_ffn1, fw["ffn1"], "ffn1_fwd", token=ag_mid[4])
    own_mid, land_mid = _ici_wait(ag_mid, h1, "ag_mid_wait", True)
    full_mid = _core_share(own_mid, land_mid, "ag_mid_share")
    fw.update(_unpack_gathered(full_mid, "mid"))
    ag_ffn2 = _ici_start(wb["ffn2"], full_mid, "ag_ffn2_start", True)
    u, z, qn, kvn, qh, kh, vh = _mix_prep(h1, g_mix, fw["w_in"], g_q, fw["w_q"], g_kv, fw["w_kv"], pos, rope_tab,
                                          token=ag_ffn2[4])
    a, lse = _attn_fwd(qh, kh, vh)
    p = _pool_fwd(z, pool_wb, pool_sc)
    memn, km, vm = _mem_kv(mems, g_mem, fw["w_mkv"])
    own_ffn2, land_ffn2 = _ici_wait(ag_ffn2, a, "ag_ffn2_wait", True)
    land_ffn2 = lax.dynamic_update_slice(land_ffn2, own_ffn2[:, None], (0, 4 * cx + 2 * cy + cc, 0, 0))
    share_ffn2 = _share_start(land_ffn2, a, "ag_ffn2_share_start")
    h2, h3, hn, qm, om = _xattn_fwd(h1, a, p, fw["w_out"], g_x, fw["w_mq"], km, vm, fw["w_mo"], token=share_ffn2[3])
    fw.update(_unpack_gathered(_share_wait(share_ffn2, h3, "ag_ffn2_share_wait"), "ffn2"))
    dh4, n2, gate2, up2, loss_part, dg_fin = _ffn_fwd(h3, g_ffn2, fw["ffn2"],
                                                      "ffn2_fwd", head=(target, g_fin))

    def reduce_start(g8, unit):
        part = _core_reduce(g8, unit)
        return _ici_start(part, g8, "rs_" + unit + "_start", False)

    def by_device(g):
        return g.reshape(N_DEV, -1, D_MODEL)

    rs = {}
    dh3, dgate2, dup2, act2, dg_ffn2 = _ffn_bwd_data(dh4, h3, g_ffn2, gate2, up2, fw["ffn2"], "ffn2_bwd")
    rs["ffn2_g"] = reduce_start(by_device(_tn_matmul(dgate2, n2, "ffn2_dwg", tmm=1408, m=D_FF, out_dtype=BF16)), "ffn2_g")
    rs["ffn2_u"] = reduce_start(by_device(_tn_matmul(dup2, n2, "ffn2_dwu", tmm=1408, m=D_FF, out_dtype=BF16,
                                                     token=rs["ffn2_g"][4])), "ffn2_u")
    rs["ffn2_d"] = reduce_start(by_device(_tn_matmul(act2, dh4, "ffn2_dwd", scale=0.5, tmm=1408, m=D_FF, out_dtype=BF16,
                                                     token=rs["ffn2_u"][4])), "ffn2_d")
    dh2, dqm, da, dp, dkm, dvm, dg_x = _xattn_bwd(dh3, h2, qm, g_x, fw["w_mq"], km, vm, fw["w_mo"], fw["w_out"],
                                                  token=rs["ffn2_d"][4])
    gr = {}
    gr["w_mo"] = _tn_matmul(om, dh3, "dw_mo", out_dtype=BF16)
    gr["w_mq"] = _tn_matmul(hn, dqm, "dw_mq", out_dtype=BF16)
    gr["w_out"] = jnp.concatenate([_tn_matmul(a, dh2, "dw_out_a", out_dtype=BF16),
                                   _tn_matmul(p, dh2, "dw_out_p", out_dtype=BF16)], axis=0)
    gr["w_mkv"], dg_mem = _mem_kv_bwd(dkm, dvm, memn, mems, g_mem, fw["w_mkv"])
    dz_pool, d_pool_w, d_pool_sc = _pool_bwd(dp, z, pool_wb, pool_sc)
    dqh, dkh, dvh = _attn_bwd(qh, kh, vh, da, lse, _attn_delta(a, da))
    dh1, dq, dkv, dz, dg_q, dg_kv, dg_mix = _mla_bwd(dqh, dkh, dvh, z, dz_pool, h1, dh2, g_mix, fw["w_in"], g_q,
                                                     fw["w_q"], g_kv, fw["w_kv"], pos, rope_tab)
    gr["w_q"] = _tn_matmul(dq, qn, "dw_q", out_dtype=BF16)
    gr["w_kv"] = _tn_matmul(kvn, dkv, "dw_kv", out_dtype=BF16)
    gr["w_in"] = _tn_matmul(u, dz, "dw_in", out_dtype=BF16)
    g_mid = _pack_grads(gr)
    part_mid = _core_reduce(g_mid, "mid")
    got = {}
    after = part_mid
    for unit in ("ffn2_g", "ffn2_u", "ffn2_d"):
        got[unit] = _ici_wait(rs[unit], after, "rs_" + unit + "_wait", False)
        after = got[unit][1]
    rs["mid"] = _ici_start(part_mid, after, "rs_mid_start", False)
    dx, dgate1, dup1, act1, dg_ffn1 = _ffn_bwd_data(dh1, xs, g_ffn1, gate1, up1, fw["ffn1"], "ffn1_bwd", token=rs["mid"][4])
    got["mid"] = _ici_wait(rs["mid"], dx, "rs_mid_wait", False)

    small_g = dict(ffn1_norm=dg_ffn1, mix_norm=dg_mix, q_norm=dg_q, kv_norm=dg_kv, pool_w=d_pool_w,
                   pool_scale=d_pool_sc, xattn_norm=dg_x, mem_norm=dg_mem, ffn2_norm=dg_ffn2, final_norm=dg_fin,
                   loss=loss_part)
    small_ag = _peers_start(_pack_small(small_g), got["mid"][1], "small_ag_start")
    rs["ffn1_g"] = reduce_start(by_device(_tn_matmul(dgate1, n1, "ffn1_dwg", tmm=1408, m=D_FF, out_dtype=BF16,
                                                     token=small_ag[4])), "ffn1_g")
    _, parts = _peers_wait(small_ag, rs["ffn1_g"][4], "small_ag_wait")
    small = _adam_small(parts, *adam_in["small"])
    small_sum = small[0]
    loss = small_sum[SMALL_OFF["loss"][0], 0]
    shapes = {n: wts[n].shape for n in SMALL_NAMES}
    small = [_unpack_small(s, shapes) for s in small]

    rs["ffn1_u"] = reduce_start(by_device(_tn_matmul(dup1, n1, "ffn1_dwu", tmm=1408, m=D_FF, out_dtype=BF16,
                                                     token=small_sum)), "ffn1_u")
    rs["ffn1_d"] = reduce_start(by_device(_tn_matmul(act1, dh1, "ffn1_dwd", scale=0.5, tmm=1408, m=D_FF, out_dtype=BF16,
                                                     token=rs["ffn1_u"][4])), "ffn1_d")

    big = {}

    def adam_units(names, token):
        units = [got[u] + adam_in[u] for u in names]
        res = _adam_big(units, chip_idx, "_".join(names), token)
        for u, four in zip(names, res):
            for k, packed in enumerate(four):
                big.setdefault(k, {}).update(_unpack_unit(packed, u))
        return res[-1][0]

    done = adam_units(["mid"], rs["ffn1_d"][4])
    done = adam_units(["ffn2_g", "ffn2_u", "ffn2_d"], done)
    got["ffn1_g"] = _ici_wait(rs["ffn1_g"], done, "rs_ffn1_g_wait", False)
    got["ffn1_u"] = _ici_wait(rs["ffn1_u"], got["ffn1_g"][1], "rs_ffn1_u_wait", False)
    done = adam_units(["ffn1_g", "ffn1_u"], done)
    got["ffn1_d"] = _ici_wait(rs["ffn1_d"], done, "rs_ffn1_d_wait", False)
    adam_units(["ffn1_d"], done)

    outs = [loss, dx[None]]
    for k in range(4):
        for n in WEIGHT_ORDER:
            outs.append(big[k][n] if n in BIG_NAMES else small[k][n])
    return tuple(outs)
```

```python
import numpy as np

import jax
import jax.numpy as jnp
from jax import lax
from jax.experimental import pallas as pl
from jax.experimental.pallas import tpu as pltpu

F32 = jnp.float32
BF16 = jnp.bfloat16

N_DEV = 8
D_MODEL = 1024
D_FF = 2816
MLA_HEADS = 4
NOPE = 128
ROPE = 64
HEAD_PAD = 256
V_DIM = 128
Q_RANK = 256
KV_RANK = 128
POOL_WINDOWS = (2, 4, 8, 16)
POOL_CH = 128
POOL_HALO = 16
N_MEM = 256
MEM_HEADS = 4
MEM_HD = 256
ROPE_BASE = 10000.0
RMS_EPS = 1e-6
ATTN_SCALE = (NOPE + ROPE) ** -0.5
MEM_SCALE = MEM_HD ** -0.5
NEG_BIG = -1e30

ADAM_LR = 0.001
ADAM_B1 = 0.9
ADAM_B2 = 0.999
ADAM_EPS = 1e-08
ADAM_WD = 0.01
ADAM_STEP = 10
ADAM_C1 = 1.0 - ADAM_B1 ** ADAM_STEP
ADAM_C2 = 1.0 - ADAM_B2 ** ADAM_STEP

VMEM_LIMIT_BYTES = 56 * 1024 * 1024
BF16_ROWS = 16

GROUP_SEGS = {
    "ffn1": (("ffn1_g", 352), ("ffn1_u", 352), ("ffn1_d", 352)),
    "mid": (("w_in", 128), ("w_out", 128), ("w_mq", 128), ("w_mo", 128), ("w_mkv", 256), ("w_q", 32), ("w_kv", 16)),
    "ffn2": (("ffn2_g", 352), ("ffn2_u", 352), ("ffn2_d", 352)),
}
SEG_OFF = {}
GROUP_ROWS = {}
for _g, _segs in GROUP_SEGS.items():
    _o = 0
    for _n, _r in _segs:
        SEG_OFF[_n] = (_o, _r)
        _o += _r
    GROUP_ROWS[_g] = _o

SMALL_ROWS = (("ffn1_norm", 8), ("mix_norm", 8), ("q_norm", 8), ("kv_norm", 8), ("pool_w", 512), ("pool_scale", 8),
              ("xattn_norm", 8), ("mem_norm", 8), ("ffn2_norm", 8), ("final_norm", 8), ("loss", 8))
SMALL_OFF = {}
_o = 0
for _n, _r in SMALL_ROWS:
    SMALL_OFF[_n] = (_o, _r)
    _o += _r


def _cparams(**kw):
    return pltpu.CompilerParams(vmem_limit_bytes=VMEM_LIMIT_BYTES, **kw)


def _row_tile(rows, limit):
    best = None
    for cand in range(BF16_ROWS, min(rows, limit) + 1, BF16_ROWS):
        if rows % cand == 0:
            best = cand
    assert best is not None, rows
    return best


def _dot_nn(a, b):
    return lax.dot_general(a, b, (((1,), (0,)), ((), ())), preferred_element_type=F32)


def _dot_nt(a, b):
    return lax.dot_general(a, b, (((1,), (1,)), ((), ())), preferred_element_type=F32)


def _dot_tn(a, b):
    return lax.dot_general(a, b, (((0,), (0,)), ((), ())), preferred_element_type=F32)


def _rms_fwd(x, g):
    r = lax.rsqrt(jnp.mean(x * x, axis=-1, keepdims=True) + RMS_EPS)
    return x * r * g, r


def _rms_bwd(dy, x, g, r):
    xhat = x * r
    dyg = dy * g
    dx = r * (dyg - xhat * jnp.mean(dyg * xhat, axis=-1, keepdims=True))
    dg = jnp.sum(dy * xhat, axis=0, keepdims=True)
    return dx, dg


def _accumulate(ref, val, first):
    if isinstance(first, bool):
        if first:
            ref[...] = val
        else:
            ref[...] += val
        return

    @pl.when(first)
    def _():
        ref[...] = val

    @pl.when(jnp.logical_not(first))
    def _():
        ref[...] += val


def _call_after(token, body, in_specs, args, **kw):
    if token is not None:
        inner = body
        body = lambda tok_ref, *refs: inner(*refs)
        in_specs = [pl.BlockSpec((8, 128), lambda *_: (0, 0))] + list(in_specs)
        args = (token,) + tuple(args)
    return pl.pallas_call(body, in_specs=in_specs, **kw)(*args)


def _resident(shape):
    return pl.BlockSpec(shape, lambda *_: (0,) * len(shape), pipeline_mode=pl.Buffered(1))


def _rope_tables(pos_col, tab):
    ang = pos_col.astype(F32) * tab[0:1, :]
    return jnp.cos(ang), jnp.sin(ang) * tab[1:2, :]


def _swap_halves(x):
    lane = lax.broadcasted_iota(jnp.int32, x.shape, 1)
    return jnp.where((lane % 64) < 32, pltpu.roll(x, 96, 1), pltpu.roll(x, 32, 1))


def _rope_apply(x, cos_t, sin_t):
    return x * cos_t + _swap_halves(x) * sin_t


def _rope_apply_t(dy, cos_t, sin_t):
    return dy * cos_t + _swap_halves(dy * sin_t)


def _ffn_fwd(h, g, w, name, token=None, head=None):
    t, d = h.shape
    f = w.shape[1]
    tm, tf = min(512, t), 256
    nf = f // tf
    n_in = 3 if head is None else 5

    def body(*refs):
        h_ref, g_ref, w_ref = refs[:3]
        ho_ref, n_ref, gate_ref, up_ref = refs[n_in:n_in + 4]
        nb_sc, acc_sc = refs[-2:]
        y, _ = _rms_fwd(h_ref[...], g_ref[...])
        nb = y.astype(BF16)
        nb_sc[...] = nb
        n_ref[...] = nb
        acc_sc[...] = jnp.zeros_like(acc_sc)

        def f_tile(j):
            rows = pl.ds(pl.multiple_of(j * tf, tf), tf)
            nb = nb_sc[...]
            gt = _dot_nt(nb, w_ref[0, rows, :])
            ut = _dot_nt(nb, w_ref[1, rows, :])
            gate_ref[j] = gt.astype(BF16)
            up_ref[j] = ut.astype(BF16)
            act = (gt * jax.nn.sigmoid(gt)) * ut
            return _dot_nn(act.astype(BF16), w_ref[2, rows, :])

        def pair(p, carry):
            acc_sc[...] += f_tile(2 * p) + f_tile(2 * p + 1)
            return carry

        lax.fori_loop(0, nf // 2, pair, 0)
        if nf % 2:
            acc_sc[...] += f_tile(nf - 1)
        ho = h_ref[...] + 0.5 * acc_sc[...]
        if head is None:
            ho_ref[...] = ho
            return
        t_ref, gf_ref = refs[3:5]
        loss_ref, dgf_ref = refs[n_in + 4:n_in + 6]
        gg = gf_ref[...]
        y, r = _rms_fwd(ho, gg)
        err = y - t_ref[...]
        part = 0.5 * jnp.sum(jnp.mean(err * err, axis=-1, keepdims=True), axis=0, keepdims=True)
        dx, dg = _rms_bwd(err * (1.0 / d), ho, gg, r)
        ho_ref[...] = dx
        first = pl.program_id(0) == 0
        _accumulate(loss_ref, jnp.broadcast_to(part, loss_ref.shape), first)
        _accumulate(dgf_ref, dg, first)

    row = pl.BlockSpec((tm, d), lambda i: (i, 0))
    tiles = pl.BlockSpec((nf, tm, tf), lambda i: (0, i, 0))
    in_specs = [row, _resident((1, d)), _resident(w.shape)]
    args = (h, g, w)
    out_specs = [row, row, tiles, tiles]
    out_shape = [jax.ShapeDtypeStruct((t, d), F32), jax.ShapeDtypeStruct((t, d), BF16),
                 jax.ShapeDtypeStruct((nf, t, tf), BF16), jax.ShapeDtypeStruct((nf, t, tf), BF16)]
    if head is not None:
        in_specs += [row, _resident((1, d))]
        args += tuple(head)
        out_specs += [pl.BlockSpec((8, 128), lambda i: (0, 0)), pl.BlockSpec((1, d), lambda i: (0, 0))]
        out_shape += [jax.ShapeDtypeStruct((8, 128), F32), jax.ShapeDtypeStruct((1, d), F32)]
    return _call_after(
        token, body, in_specs, args, name=name, grid=(t // tm,), out_specs=out_specs, out_shape=out_shape,
        scratch_shapes=[pltpu.VMEM((tm, d), BF16), pltpu.VMEM((tm, d), F32)],
        compiler_params=_cparams(),
    )


def _ffn_bwd_data(dho, h, g, gate, up, w, name, token=None):
    t, d = h.shape
    f = w.shape[1]
    tm, tf = min(1024, t), 256
    parts = 2 if tm % 512 == 0 else 1
    tp = tm // parts
    nf = f // tf
    npair, odd = nf // 2, nf % 2
    nsteps = npair + odd

    def body(dho_ref, h_ref, g_ref, gate_ref, up_ref, wg_ref, wu_ref, wd_ref,
             dh_ref, dgate_ref, dup_ref, act_ref, dg_ref, dhb_sc, acc_sc):
        i, j = pl.program_id(0), pl.program_id(1)

        @pl.when(j == 0)
        def _():
            dhb_sc[...] = (0.5 * dho_ref[...]).astype(BF16)
            acc_sc[...] = jnp.zeros_like(acc_sc)

        def slab(ntile):
            cols = pl.ds(0, ntile * tf)
            for r in range(parts):
                rows = pl.ds(r * tp, tp)
                gt = jnp.concatenate([gate_ref[k, rows, :] for k in range(ntile)], axis=-1).astype(F32)
                ut = jnp.concatenate([up_ref[k, rows, :] for k in range(ntile)], axis=-1).astype(F32)
                dact = _dot_nt(dhb_sc[rows, :], wd_ref[cols, :])
                sg = jax.nn.sigmoid(gt)
                silu = gt * sg
                dgb = (dact * ut * (sg * (1.0 + gt * (1.0 - sg)))).astype(BF16)
                dub = (dact * silu).astype(BF16)
                act_ref[rows, cols] = (silu * ut).astype(BF16)
                dgate_ref[rows, cols] = dgb
                dup_ref[rows, cols] = dub
                acc_sc[rows, :] += _dot_nn(dgb, wg_ref[cols, :]) + _dot_nn(dub, wu_ref[cols, :])

        pl.when(j < npair)(lambda: slab(2))
        if odd:
            pl.when(j == npair)(lambda: slab(1))

        @pl.when(j == nsteps - 1)
        def _():
            x = h_ref[...]
            gg = g_ref[...]
            _, r = _rms_fwd(x, gg)
            dx, dg = _rms_bwd(acc_sc[...], x, gg, r)
            dh_ref[...] = dho_ref[...] + dx
            _accumulate(dg_ref, dg, i == 0)

    row = pl.BlockSpec((tm, d), lambda i, j: (i, 0))
    acts = pl.BlockSpec((2, tm, tf), lambda i, j: (j, i, 0))
    weights = lambda k: pl.BlockSpec((None, 2 * tf, d), lambda i, j: (k, j, 0))
    outs = pl.BlockSpec((tm, 2 * tf), lambda i, j: (i, j))
    padded = jax.ShapeDtypeStruct((t, 2 * tf * nsteps), BF16)
    return _call_after(
        token, body,
        [row, row, pl.BlockSpec((1, d), lambda i, j: (0, 0)), acts, acts, weights(0), weights(1), weights(2)],
        (dho, h, g, gate, up, w, w, w),
        name=name, grid=(t // tm, nsteps),
        out_specs=[row, outs, outs, outs, pl.BlockSpec((1, d), lambda i, j: (0, 0))],
        out_shape=[jax.ShapeDtypeStruct((t, d), F32), padded, padded, padded, jax.ShapeDtypeStruct((1, d), F32)],
        scratch_shapes=[pltpu.VMEM((tm, d), BF16), pltpu.VMEM((tm, d), F32)],
        compiler_params=_cparams(),
    )


def _tn_matmul(a, b, name, scale=1.0, tmm=None, out_dtype=F32, token=None, m=None):
    t = a.shape[0]
    m = a.shape[1] if m is None else m
    n = b.shape[1]
    tmm = m if tmm is None else tmm
    tk = min(1024, t)
    nk = t // tk

    def product(a_ref, b_ref):
        prod = _dot_tn(a_ref[...].astype(BF16), b_ref[...].astype(BF16))
        return prod * scale if scale != 1.0 else prod

    def body_f32(a_ref, b_ref, o_ref):
        _accumulate(o_ref, product(a_ref, b_ref), pl.program_id(1) == 0)

    def body_cast(a_ref, b_ref, o_ref, acc_sc):
        k = pl.program_id(1)
        _accumulate(acc_sc, product(a_ref, b_ref), k == 0)

        @pl.when(k == nk - 1)
        def _():
            o_ref[...] = acc_sc[...].astype(out_dtype)

    direct = out_dtype == F32
    return _call_after(
        token, body_f32 if direct else body_cast,
        [pl.BlockSpec((tk, tmm), lambda i, k: (k, i)),
         pl.BlockSpec((tk, n), lambda i, k: (k, 0))],
        (a, b),
        name=name, grid=(m // tmm, nk),
        out_specs=pl.BlockSpec((tmm, n), lambda i, k: (i, 0)),
        out_shape=jax.ShapeDtypeStruct((m, n), out_dtype),
        scratch_shapes=[] if direct else [pltpu.VMEM((tmm, n), F32)],
        compiler_params=_cparams(),
    )


def _mix_prep(h1, mix_norm, w_in, q_norm, wq_t, kv_norm, wkv, pos, rope_tab, token=None):
    t, d = h1.shape
    tm = min(512, t)

    def body(h_ref, gm_ref, win_ref, gq_ref, wq_ref, gkv_ref, wkv_ref, pos_ref, tab_ref,
             u_ref, z_ref, qn_ref, kvn_ref, q_ref, k_ref, v_ref):
        u, _ = _rms_fwd(h_ref[...], gm_ref[...])
        ub = u.astype(BF16)
        u_ref[...] = ub
        z = _dot_nn(ub, win_ref[...])
        z_ref[...] = z
        cos_t, sin_t = _rope_tables(pos_ref[...], tab_ref[...])
        qn, _ = _rms_fwd(z[:, 0:Q_RANK], gq_ref[...])
        qnb = qn.astype(BF16)
        qn_ref[...] = qnb
        q = _dot_nt(qnb, wq_ref[...])
        kvn, _ = _rms_fwd(z[:, Q_RANK:Q_RANK + KV_RANK], gkv_ref[...])
        kvnb = kvn.astype(BF16)
        kvn_ref[...] = kvnb
        kv = _dot_nn(kvnb, wkv_ref[...])
        k_pe = _rope_apply(z[:, Q_RANK + KV_RANK:Q_RANK + KV_RANK + 128], cos_t, sin_t)
        ones = jnp.ones((tm, V_DIM), F32)
        for hh in range(MLA_HEADS):
            b = hh * HEAD_PAD
            q_pe = _rope_apply(q[:, b + NOPE:b + HEAD_PAD], cos_t, sin_t)
            q_ref[hh] = jnp.concatenate([q[:, b:b + NOPE], q_pe], axis=-1).astype(BF16)
            k_ref[hh] = jnp.concatenate([kv[:, b:b + NOPE], k_pe], axis=-1).astype(BF16)
            v_ref[hh] = jnp.concatenate([kv[:, b + NOPE:b + HEAD_PAD], ones], axis=-1).astype(BF16)

    full = lambda shape: pl.BlockSpec(shape, lambda i: (0,) * len(shape))
    return _call_after(
        token, body,
        [pl.BlockSpec((tm, d), lambda i: (i, 0)), _resident((1, d)), _resident(w_in.shape), _resident((1, Q_RANK)),
         _resident(wq_t.shape), _resident((1, KV_RANK)), _resident(wkv.shape),
         pl.BlockSpec((tm, 1), lambda i: (i, 0)), _resident(rope_tab.shape)],
        (h1, mix_norm, w_in, q_norm, wq_t, kv_norm, wkv, pos, rope_tab),
        name="mix_prep", grid=(t // tm,),
        out_specs=[pl.BlockSpec((tm, d), lambda i: (i, 0)),
                   pl.BlockSpec((tm, d), lambda i: (i, 0)),
                   pl.BlockSpec((tm, Q_RANK), lambda i: (i, 0)),
                   pl.BlockSpec((tm, KV_RANK), lambda i: (i, 0)),
                   pl.BlockSpec((MLA_HEADS, tm, HEAD_PAD), lambda i: (0, i, 0)),
                   pl.BlockSpec((MLA_HEADS, tm, HEAD_PAD), lambda i: (0, i, 0)),
                   pl.BlockSpec((MLA_HEADS, tm, 2 * V_DIM), lambda i: (0, i, 0))],
        out_shape=[jax.ShapeDtypeStruct((t, d), BF16), jax.ShapeDtypeStruct((t, d), F32),
                   jax.ShapeDtypeStruct((t, Q_RANK), BF16), jax.ShapeDtypeStruct((t, KV_RANK), BF16),
                   jax.ShapeDtypeStruct((MLA_HEADS, t, HEAD_PAD), BF16),
                   jax.ShapeDtypeStruct((MLA_HEADS, t, HEAD_PAD), BF16),
                   jax.ShapeDtypeStruct((MLA_HEADS, t, 2 * V_DIM), BF16)],
        compiler_params=_cparams(),
    )


def _causal_mask(s):
    row = lax.broadcasted_iota(jnp.int32, s.shape, 0)
    col = lax.broadcasted_iota(jnp.int32, s.shape, 1)
    return jnp.where(col <= row, s, NEG_BIG)


def _attn_fwd(q, k, v):
    nh, t, _ = q.shape
    tq = tk = min(512, t)
    nq, nk = t // tq, t // tk

    pairs = [(i, j) for i in range(nq) for j in range(i + 1)]
    qi = jnp.asarray(np.array([i for i, _ in pairs], np.int32))
    kj = jnp.asarray(np.array([j for _, j in pairs], np.int32))

    def body(qi_ref, kj_ref, q_ref, k_ref, v_ref, o_ref, lse_ref, m_sc, acc_sc):
        n = pl.program_id(0)
        i, j = qi_ref[n], kj_ref[n]

        @pl.when(j == 0)
        def _():
            m_sc[...] = jnp.full_like(m_sc, NEG_BIG)
            acc_sc[...] = jnp.zeros_like(acc_sc)

        def step(diagonal):
            for hh in range(nh):
                s = _dot_nt(q_ref[hh], k_ref[hh]) * ATTN_SCALE
                if diagonal:
                    s = _causal_mask(s)
                m_old = m_sc[hh]
                m_new = jnp.maximum(m_old, jnp.max(s, axis=-1, keepdims=True))
                p = jnp.exp(s - m_new).astype(BF16)
                acc_sc[hh] = jnp.exp(m_old - m_new) * acc_sc[hh] + _dot_nn(p, v_ref[hh])
                m_sc[hh] = m_new

        @pl.when(j < i)
        def _():
            step(False)

        @pl.when(j == i)
        def _():
            step(True)
            for hh in range(nh):
                acc = acc_sc[hh]
                l = acc[:, V_DIM:2 * V_DIM]
                o_ref[:, hh * V_DIM:(hh + 1) * V_DIM] = (acc[:, 0:V_DIM] / l).astype(BF16)
                lse_ref[hh] = m_sc[hh] + jnp.log(l[:, 0:1])

    q_map = lambda n, qi_ref, kj_ref: (0, qi_ref[n], 0)
    kv_map = lambda n, qi_ref, kj_ref: (0, kj_ref[n], 0)
    return pl.pallas_call(
        body, name="attn_fwd",
        grid_spec=pltpu.PrefetchScalarGridSpec(
            num_scalar_prefetch=2, grid=(len(pairs),),
            in_specs=[pl.BlockSpec((nh, tq, HEAD_PAD), q_map),
                      pl.BlockSpec((nh, tk, HEAD_PAD), kv_map),
                      pl.BlockSpec((nh, tk, 2 * V_DIM), kv_map)],
            out_specs=[pl.BlockSpec((tq, nh * V_DIM), lambda n, qi_ref, kj_ref: (qi_ref[n], 0)),
                       pl.BlockSpec((nh, tq, 1), q_map)],
            scratch_shapes=[pltpu.VMEM((nh, tq, 1), F32), pltpu.VMEM((nh, tq, 2 * V_DIM), F32)]),
        out_shape=[jax.ShapeDtypeStruct((t, nh * V_DIM), BF16), jax.ShapeDtypeStruct((nh, t, 1), F32)],
        compiler_params=_cparams(),
    )(qi, kj, q, k, v)


def _attn_delta(o, do):
    t, w = o.shape
    nh = w // V_DIM
    tm = min(512, t)

    def body(o_ref, do_ref, d_ref):
        prod = o_ref[...].astype(F32) * do_ref[...].astype(F32)
        for hh in range(nh):
            d_ref[hh] = jnp.sum(prod[:, hh * V_DIM:(hh + 1) * V_DIM], axis=-1, keepdims=True)

    return pl.pallas_call(
        body, name="attn_delta", grid=(t // tm,),
        in_specs=[pl.BlockSpec((tm, w), lambda i: (i, 0)), pl.BlockSpec((tm, w), lambda i: (i, 0))],
        out_specs=pl.BlockSpec((nh, tm, 1), lambda i: (0, i, 0)),
        out_shape=jax.ShapeDtypeStruct((nh, t, 1), F32),
        compiler_params=_cparams(),
    )(o, do)


ATTN_BWD_HEADS = 2


def _attn_bwd(q, k, v, do, lse, delta):
    nh, t, _ = q.shape
    hp = ATTN_BWD_HEADS
    tq = tk = min(512, t)
    nq, nk = t // tq, t // tk

    pairs = [(j, i) for j in range(nk) for i in range(j, nq)]
    kj = jnp.asarray(np.array([j for j, _ in pairs], np.int32))
    qi = jnp.asarray(np.array([i for _, i in pairs], np.int32))

    def body(kj_ref, qi_ref, q_ref, k_ref, v_ref, do_ref, lse_ref, dlt_ref, dq_ref, dk_ref, dv_ref):
        n = pl.program_id(1)
        j, i = kj_ref[n], qi_ref[n]

        @pl.when(n == 0)
        def _():
            dq_ref[...] = jnp.zeros_like(dq_ref)

        def step(diagonal):
            for hh in range(hp):
                qq, kk = q_ref[hh], k_ref[hh]
                dob = do_ref[:, hh * V_DIM:(hh + 1) * V_DIM]
                s = _dot_nt(qq, kk) * ATTN_SCALE
                if diagonal:
                    s = _causal_mask(s)
                p = jnp.exp(s - lse_ref[hh])
                dpp = _dot_nt(dob, v_ref[hh])
                dsb = (p * (dpp - dlt_ref[hh]) * ATTN_SCALE).astype(BF16)
                _accumulate(dv_ref.at[hh], _dot_tn(p.astype(BF16), dob), diagonal)
                _accumulate(dk_ref.at[hh], _dot_tn(dsb, qq), diagonal)
                dq_ref[hh, pl.ds(pl.multiple_of(i * tq, tq), tq), :] += _dot_nn(dsb, kk)

        @pl.when(i > j)
        def _():
            step(False)

        @pl.when(i == j)
        def _():
            step(True)

    q_map = lambda h, n, kj_ref, qi_ref: (h, qi_ref[n], 0)
    k_map = lambda h, n, kj_ref, qi_ref: (h, kj_ref[n], 0)
    return pl.pallas_call(
        body, name="attn_bwd",
        grid_spec=pltpu.PrefetchScalarGridSpec(
            num_scalar_prefetch=2, grid=(nh // hp, len(pairs)),
            in_specs=[pl.BlockSpec((hp, tq, HEAD_PAD), q_map),
                      pl.BlockSpec((hp, tk, HEAD_PAD), k_map),
                      pl.BlockSpec((hp, tk, V_DIM), k_map),
                      pl.BlockSpec((tq, hp * V_DIM), lambda h, n, kj_ref, qi_ref: (qi_ref[n], h)),
                      pl.BlockSpec((hp, tq, 1), q_map),
                      pl.BlockSpec((hp, tq, 1), q_map)],
            out_specs=[pl.BlockSpec((hp, t, HEAD_PAD), lambda h, n, kj_ref, qi_ref: (h, 0, 0)),
                       pl.BlockSpec((hp, tk, HEAD_PAD), k_map),
                       pl.BlockSpec((hp, tk, V_DIM), k_map)]),
        out_shape=[jax.ShapeDtypeStruct((nh, t, HEAD_PAD), F32), jax.ShapeDtypeStruct((nh, t, HEAD_PAD), F32),
                   jax.ShapeDtypeStruct((nh, t, V_DIM), F32)],
        compiler_params=_cparams(),
    )(kj, qi, q, k, v, do, lse, delta)


def _pool_counts(first_token, rows, w):
    tok = lax.broadcasted_iota(jnp.int32, (rows, POOL_CH), 0) + first_token
    return jnp.minimum(tok + 1, w).astype(F32)


def _pool_centered(zbuf, g, w, i, tm):
    lanes = pl.ds(g * POOL_CH, POOL_CH)
    cur = zbuf[pl.ds(POOL_HALO, tm), lanes]
    win = cur
    for s in range(1, w):
        win = win + zbuf[pl.ds(POOL_HALO - s, tm), lanes]
    return win / _pool_counts(i * tm, tm, w) - cur


def _pool_load(zbuf, z_ref, halo_ref, i, tm):
    @pl.when(i == 0)
    def _():
        zbuf[pl.ds(0, POOL_HALO), :] = jnp.zeros((POOL_HALO, zbuf.shape[1]), F32)

    @pl.when(i > 0)
    def _():
        zbuf[pl.ds(0, POOL_HALO), :] = halo_ref[...]

    zbuf[pl.ds(POOL_HALO, tm), :] = z_ref[...]


def _pool_fwd(z, pool_w, pool_scale):
    t = z.shape[0]
    pw = len(POOL_WINDOWS) * POOL_CH
    tm = min(512, t)
    hb = tm // POOL_HALO

    def body(z_ref, halo_ref, w_ref, sc_ref, p_ref, zbuf):
        i = pl.program_id(0)
        _pool_load(zbuf, z_ref, halo_ref, i, tm)
        for g, w in enumerate(POOL_WINDOWS):
            c = _pool_centered(zbuf, g, w, i, tm)
            y = _dot_nn(c.astype(BF16), w_ref[g]) * sc_ref[:, g * POOL_CH:(g + 1) * POOL_CH]
            p_ref[:, g * POOL_CH:(g + 1) * POOL_CH] = y.astype(BF16)

    return pl.pallas_call(
        body, name="pool_fwd", grid=(t // tm,),
        in_specs=[pl.BlockSpec((tm, pw), lambda i: (i, 1)),
                  pl.BlockSpec((POOL_HALO, pw), lambda i: (jnp.maximum(i * hb - 1, 0), 1)),
                  pl.BlockSpec(pool_w.shape, lambda i: (0, 0, 0)),
                  pl.BlockSpec((1, pw), lambda i: (0, 0))],
        out_specs=pl.BlockSpec((tm, pw), lambda i: (i, 0)),
        out_shape=jax.ShapeDtypeStruct((t, pw), BF16),
        scratch_shapes=[pltpu.VMEM((POOL_HALO + tm, pw), F32)],
        compiler_params=_cparams(),
    )(z, z, pool_w, pool_scale)


def _pool_bwd(dp, z, pool_w, pool_scale):
    t = z.shape[0]
    ng = len(POOL_WINDOWS)
    pw = ng * POOL_CH
    tm = min(512, t)
    hb = tm // POOL_HALO
    nt = t // tm

    def body(dp_ref, dpn_ref, z_ref, halo_ref, w_ref, sc_ref, dz_ref, dw_ref, dsc_ref, zbuf, dbuf):
        i = pl.program_id(0)
        _pool_load(zbuf, z_ref, halo_ref, i, tm)

        @pl.when(i == 0)
        def _():
            dw_ref[...] = jnp.zeros_like(dw_ref)
            dsc_ref[...] = jnp.zeros_like(dsc_ref)

        nxt_ok = (i < nt - 1).astype(F32)
        for g, w in enumerate(POOL_WINDOWS):
            lanes = pl.ds(g * POOL_CH, POOL_CH)
            cols = slice(g * POOL_CH, (g + 1) * POOL_CH)
            sc = sc_ref[:, cols]
            wg = w_ref[g]
            c = _pool_centered(zbuf, g, w, i, tm).astype(BF16)
            ypre = _dot_nn(c, wg)
            dpg = dp_ref[:, cols].astype(F32)
            dsc_ref[:, cols] += jnp.sum(dpg * ypre, axis=0, keepdims=True)
            dyb = (dpg * sc).astype(BF16)
            dw_ref[g] += _dot_tn(c, dyb)
            dd = _dot_nt(dyb, wg)
            dyn = (dpn_ref[:, cols].astype(F32) * sc).astype(BF16)
            ddn = _dot_nt(dyn, wg) * nxt_ok
            dbuf[pl.ds(0, tm), lanes] = dd / _pool_counts(i * tm, tm, w)
            dbuf[pl.ds(tm, POOL_HALO), lanes] = ddn / _pool_counts((i + 1) * tm, POOL_HALO, w)
            acc = -dd
            for s in range(w):
                acc = acc + dbuf[pl.ds(s, tm), lanes]
            dz_ref[:, cols] = acc

    return pl.pallas_call(
        body, name="pool_bwd", grid=(nt,),
        in_specs=[pl.BlockSpec((tm, pw), lambda i: (i, 0)),
                  pl.BlockSpec((POOL_HALO, pw), lambda i: (jnp.minimum((i + 1) * hb, t // POOL_HALO - 1), 0)),
                  pl.BlockSpec((tm, pw), lambda i: (i, 1)),
                  pl.BlockSpec((POOL_HALO, pw), lambda i: (jnp.maximum(i * hb - 1, 0), 1)),
                  pl.BlockSpec(pool_w.shape, lambda i: (0, 0, 0)),
                  pl.BlockSpec((1, pw), lambda i: (0, 0))],
        out_specs=[pl.BlockSpec((tm, pw), lambda i: (i, 0)),
                   pl.BlockSpec((ng, POOL_CH, POOL_CH), lambda i: (0, 0, 0)),
                   pl.BlockSpec((1, pw), lambda i: (0, 0))],
        out_shape=[jax.ShapeDtypeStruct((t, pw), F32), jax.ShapeDtypeStruct((ng, POOL_CH, POOL_CH), F32),
                   jax.ShapeDtypeStruct((1, pw), F32)],
        scratch_shapes=[pltpu.VMEM((POOL_HALO + tm, pw), F32), pltpu.VMEM((tm + POOL_HALO, pw), F32)],
        compiler_params=_cparams(),
    )(dp, dp, z, z, pool_w, pool_scale)


def _mla_bwd(dq_h, dk_h, dv_h, z, dz_pool, h1, dh2, mix_norm, w_in, q_norm, wq_t, kv_norm, wkv, pos, rope_tab):
    t, d = h1.shape
    tm = min(512, t)

    def body(dqh_ref, dkh_ref, dvh_ref, z_ref, dzp_ref, h_ref, dh2_ref, gm_ref, win_ref, gq_ref, wq_ref, gkv_ref,
             wkv_ref, pos_ref, tab_ref, dh1_ref, dq_ref, dkv_ref, dz_ref, dgq_ref, dgkv_ref, dgm_ref):
        i = pl.program_id(0)
        first = i == 0
        cos_t, sin_t = _rope_tables(pos_ref[...], tab_ref[...])
        dq_parts, dkv_parts = [], []
        dk_pe = jnp.zeros((tm, 128), F32)
        for hh in range(MLA_HEADS):
            dqh = dqh_ref[hh]
            dq_parts += [dqh[:, 0:NOPE], _rope_apply_t(dqh[:, NOPE:HEAD_PAD], cos_t, sin_t)]
            dkh = dkh_ref[hh]
            dkv_parts += [dkh[:, 0:NOPE], dvh_ref[hh]]
            dk_pe = dk_pe + dkh[:, NOPE:HEAD_PAD]
        dqb = jnp.concatenate(dq_parts, axis=-1).astype(BF16)
        dkvb = jnp.concatenate(dkv_parts, axis=-1).astype(BF16)
        dq_ref[...] = dqb
        dkv_ref[...] = dkvb
        z = z_ref[...]
        c_q = z[:, 0:Q_RANK]
        gq = gq_ref[...]
        _, rq = _rms_fwd(c_q, gq)
        dcq, dgq = _rms_bwd(_dot_nn(dqb, wq_ref[...]), c_q, gq, rq)
        c_kv = z[:, Q_RANK:Q_RANK + KV_RANK]
        gkv = gkv_ref[...]
        _, rkv = _rms_fwd(c_kv, gkv)
        dckv, dgkv = _rms_bwd(_dot_nt(dkvb, wkv_ref[...]), c_kv, gkv, rkv)
        dkr = _rope_apply_t(dk_pe, cos_t, sin_t)
        dzb = jnp.concatenate([dcq, dckv, dkr, dzp_ref[...]], axis=-1).astype(BF16)
        dz_ref[...] = dzb
        x = h_ref[...]
        gm = gm_ref[...]
        _, rm = _rms_fwd(x, gm)
        dx, dgm = _rms_bwd(_dot_nt(dzb, win_ref[...]), x, gm, rm)
        dh1_ref[...] = dh2_ref[...] + dx
        _accumulate(dgq_ref, dgq, first)
        _accumulate(dgkv_ref, dgkv, first)
        _accumulate(dgm_ref, dgm, first)

    full = lambda shape: pl.BlockSpec(shape, lambda i: (0,) * len(shape))
    row = lambda w: pl.BlockSpec((tm, w), lambda i: (i, 0))
    head = lambda w: pl.BlockSpec((MLA_HEADS, tm, w), lambda i: (0, i, 0))
    pw = len(POOL_WINDOWS) * POOL_CH
    return pl.pallas_call(
        body, name="mla_bwd", grid=(t // tm,),
        in_specs=[head(HEAD_PAD), head(HEAD_PAD), head(V_DIM), row(d), row(pw), row(d), row(d),
                  _resident((1, d)), _resident(w_in.shape), _resident((1, Q_RANK)), _resident(wq_t.shape),
                  _resident((1, KV_RANK)), _resident(wkv.shape), row(1), _resident(rope_tab.shape)],
        out_specs=[row(d), row(d), row(d), row(d), full((1, Q_RANK)), full((1, KV_RANK)), full((1, d))],
        out_shape=[jax.ShapeDtypeStruct((t, d), F32), jax.ShapeDtypeStruct((t, d), BF16),
                   jax.ShapeDtypeStruct((t, d), BF16), jax.ShapeDtypeStruct((t, d), BF16),
                   jax.ShapeDtypeStruct((1, Q_RANK), F32), jax.ShapeDtypeStruct((1, KV_RANK), F32),
                   jax.ShapeDtypeStruct((1, d), F32)],
        compiler_params=_cparams(),
    )(dq_h, dk_h, dv_h, z, dz_pool, h1, dh2, mix_norm, w_in, q_norm, wq_t, kv_norm, wkv, pos, rope_tab)


def _mem_kv(mem, mem_norm, wmkv):
    n, d = mem.shape

    def body(mem_ref, g_ref, w_ref, memn_ref, k_ref, v_ref):
        y, _ = _rms_fwd(mem_ref[...], g_ref[...])
        yb = y.astype(BF16)
        memn_ref[...] = yb
        for hh in range(MEM_HEADS):
            k_ref[hh] = _dot_nn(yb, w_ref[hh]).astype(BF16)
            v_ref[hh] = _dot_nn(yb, w_ref[MEM_HEADS + hh]).astype(BF16)

    return pl.pallas_call(
        body, name="mem_kv",
        out_shape=[jax.ShapeDtypeStruct((n, d), BF16), jax.ShapeDtypeStruct((MEM_HEADS, n, MEM_HD), BF16),
                   jax.ShapeDtypeStruct((MEM_HEADS, n, MEM_HD), BF16)],
        compiler_params=_cparams(),
    )(mem, mem_norm, wmkv)


def _mem_softmax(qb, km):
    s = _dot_nt(qb, km) * MEM_SCALE
    e = jnp.exp(s - jnp.max(s, axis=-1, keepdims=True))
    return e / jnp.sum(e, axis=-1, keepdims=True)


def _xattn_fwd(h1, a, p, w_out, g, wmq, km, vm, wmo, token=None):
    t, d = h1.shape
    tm = min(512, t)
    half = a.shape[1]

    def body(h_ref, a_ref, p_ref, wo_ref, g_ref, wmq_ref, km_ref, vm_ref, wmo_ref,
             h2_ref, h3_ref, hn_ref, q_ref, o_ref):
        h2 = h_ref[...] + _dot_nn(a_ref[...], wo_ref[0:half, :]) + _dot_nn(p_ref[...], wo_ref[half:2 * half, :])
        h2_ref[...] = h2
        hn, _ = _rms_fwd(h2, g_ref[...])
        hnb = hn.astype(BF16)
        hn_ref[...] = hnb
        qb = _dot_nn(hnb, wmq_ref[...]).astype(BF16)
        q_ref[...] = qb
        outs = []
        for hh in range(MEM_HEADS):
            pr = _mem_softmax(qb[:, hh * MEM_HD:(hh + 1) * MEM_HD], km_ref[hh])
            outs.append(_dot_nn(pr.astype(BF16), vm_ref[hh]))
        ob = jnp.concatenate(outs, axis=-1).astype(BF16)
        o_ref[...] = ob
        h3_ref[...] = h2 + _dot_nn(ob, wmo_ref[...])

    full = lambda shape: pl.BlockSpec(shape, lambda i: (0,) * len(shape))
    row = lambda w: pl.BlockSpec((tm, w), lambda i: (i, 0))
    return _call_after(
        token, body,
        [row(d), row(half), row(half), _resident(w_out.shape), _resident((1, d)), _resident(wmq.shape),
         _resident(km.shape), _resident(vm.shape), _resident(wmo.shape)],
        (h1, a, p, w_out, g, wmq, km, vm, wmo),
        name="xattn_fwd", grid=(t // tm,),
        out_specs=[row(d), row(d), row(d), row(d), row(d)],
        out_shape=[jax.ShapeDtypeStruct((t, d), F32), jax.ShapeDtypeStruct((t, d), F32),
                   jax.ShapeDtypeStruct((t, d), BF16), jax.ShapeDtypeStruct((t, d), BF16),
                   jax.ShapeDtypeStruct((t, d), BF16)],
        compiler_params=_cparams(),
    )


def _xattn_bwd(dh3, h2, qm, g, wmq, km, vm, wmo, w_out, token=None):
    t, d = h2.shape
    tm = min(512, t)
    half = d // 2

    def body(dh3_ref, h2_ref, q_ref, g_ref, wmq_ref, km_ref, vm_ref, wmo_ref, wo_ref,
             dh2_ref, dq_ref, da_ref, dp_ref, dk_ref, dv_ref, dg_ref):
        i = pl.program_id(0)
        first = i == 0

        @pl.when(first)
        def _():
            dk_ref[...] = jnp.zeros_like(dk_ref)
            dv_ref[...] = jnp.zeros_like(dv_ref)

        dh3 = dh3_ref[...]
        dob = _dot_nt(dh3.astype(BF16), wmo_ref[...]).astype(BF16)
        qb = q_ref[...]
        dq_parts = []
        for hh in range(MEM_HEADS):
            cols = slice(hh * MEM_HD, (hh + 1) * MEM_HD)
            kk, vv = km_ref[hh], vm_ref[hh]
            pr = _mem_softmax(qb[:, cols], kk)
            doh = dob[:, cols]
            dv_ref[hh] += _dot_tn(pr.astype(BF16), doh)
            dpp = _dot_nt(doh, vv)
            dsb = (pr * (dpp - jnp.sum(dpp * pr, axis=-1, keepdims=True)) * MEM_SCALE).astype(BF16)
            dq_parts.append(_dot_nn(dsb, kk))
            dk_ref[hh] += _dot_tn(dsb, qb[:, cols])
        dqb = jnp.concatenate(dq_parts, axis=-1).astype(BF16)
        dq_ref[...] = dqb
        x = h2_ref[...]
        gg = g_ref[...]
        _, r = _rms_fwd(x, gg)
        dx, dg = _rms_bwd(_dot_nt(dqb, wmq_ref[...]), x, gg, r)
        dh2 = dh3 + dx
        dh2_ref[...] = dh2
        dap = _dot_nt(dh2.astype(BF16), wo_ref[...])
        da_ref[...] = dap[:, 0:half].astype(BF16)
        dp_ref[...] = dap[:, half:d].astype(BF16)
        _accumulate(dg_ref, dg, first)

    full = lambda shape: pl.BlockSpec(shape, lambda i: (0,) * len(shape))
    row = lambda w: pl.BlockSpec((tm, w), lambda i: (i, 0))
    return _call_after(
        token, body,
        [row(d), row(d), row(d), _resident((1, d)), _resident(wmq.shape), _resident(km.shape), _resident(vm.shape),
         _resident(wmo.shape), _resident(w_out.shape)],
        (dh3, h2, qm, g, wmq, km, vm, wmo, w_out),
        name="xattn_bwd", grid=(t // tm,),
        out_specs=[row(d), row(d), row(half), row(half), full(km.shape), full(vm.shape), full((1, d))],
        out_shape=[jax.ShapeDtypeStruct((t, d), F32), jax.ShapeDtypeStruct((t, d), BF16),
                   jax.ShapeDtypeStruct((t, half), BF16), jax.ShapeDtypeStruct((t, half), BF16),
                   jax.ShapeDtypeStruct(km.shape, F32), jax.ShapeDtypeStruct(vm.shape, F32),
                   jax.ShapeDtypeStruct((1, d), F32)],
        compiler_params=_cparams(),
    )


def _mem_kv_bwd(dkm, dvm, memn, mem, mem_norm, wmkv):
    n, d = mem.shape

    def body(dk_ref, dv_ref, memn_ref, mem_ref, g_ref, w_ref, dw_ref, dg_ref):
        memn = memn_ref[...]
        dmemn = jnp.zeros((n, d), F32)
        for s in range(2 * MEM_HEADS):
            src = dk_ref[s] if s < MEM_HEADS else dv_ref[s - MEM_HEADS]
            db = src.astype(BF16)
            dw_ref[s] = _dot_tn(memn, db)
            dmemn = dmemn + _dot_nt(db, w_ref[s])
        x = mem_ref[...]
        gg = g_ref[...]
        _, r = _rms_fwd(x, gg)
        _, dg = _rms_bwd(dmemn, x, gg, r)
        dg_ref[...] = dg

    return pl.pallas_call(
        body, name="mem_kv_bwd",
        out_shape=[jax.ShapeDtypeStruct(wmkv.shape, F32), jax.ShapeDtypeStruct((1, d), F32)],
        compiler_params=_cparams(),
    )(dkm, dvm, memn, mem, mem_norm, wmkv)


MESH_ID = pl.DeviceIdType.MESH
ANY = pl.BlockSpec(memory_space=pl.ANY)


def _coords():
    return lax.axis_index("x"), lax.axis_index("y"), lax.axis_index("c")


def _other_chips(x, y):
    return [(1 - x, y), (x, 1 - y), (1 - x, 1 - y)]


def _core_reduce(g, tag):
    _, r, w = g.shape

    def body(g_ref, part_ref, own_sc, recv_sc, send_sems, recv_sems, local_sems):
        x, y, c = _coords()
        sent, local = [], []
        for chip in range(4):
            sent.append(pltpu.make_async_remote_copy(
                src_ref=g_ref.at[2 * chip + (1 - c)], dst_ref=recv_sc.at[chip],
                send_sem=send_sems.at[chip], recv_sem=recv_sems.at[chip],
                device_id=(x, y, 1 - c), device_id_type=MESH_ID))
            local.append(pltpu.make_async_copy(g_ref.at[2 * chip + c], own_sc.at[chip], local_sems.at[chip]))
        for cp in sent + local:
            cp.start()
        for chip in range(4):
            local[chip].wait()
            sent[chip].wait_recv()
            part_ref[chip] = (own_sc[chip].astype(F32) + recv_sc[chip].astype(F32)).astype(part_ref.dtype)
        for cp in sent:
            cp.wait_send()

    return pl.pallas_call(
        body, name="core_reduce_" + tag,
        out_shape=jax.ShapeDtypeStruct((4, r, w), g.dtype),
        in_specs=[ANY], out_specs=pl.BlockSpec(memory_space=pltpu.VMEM),
        scratch_shapes=[pltpu.VMEM((4, r, w), g.dtype), pltpu.VMEM((4, r, w), g.dtype),
                        pltpu.SemaphoreType.DMA((4,)), pltpu.SemaphoreType.DMA((4,)), pltpu.SemaphoreType.DMA((4,))],
        compiler_params=_cparams(),
    )(g)


HBM_SPEC = pl.BlockSpec(memory_space=pltpu.HBM)
SEM_SPEC = pl.BlockSpec(memory_space=pltpu.SEMAPHORE)
SPLIT_EFFECT = pltpu.SideEffectType.DATAFLOW_SIDE_EFFECTING


def _ici_refs(gather, src_ref, land_ref, j, px, py, slot_chip, c):
    if gather:
        return src_ref, land_ref.at[:, 4 * slot_chip[0] + 2 * slot_chip[1] + c]
    return src_ref.at[2 * px + py], land_ref.at[j]


def _ici_start(src, after, name, gather):
    r, w = src.shape[-2:]
    land_shape = (src.shape[0], N_DEV, r, w) if gather else (3, r, w)

    def body(src_ref, land_ref, after_ref, send_sems, recv_sems, src_thru, land_thru, token):
        x, y, c = _coords()
        for j, (px, py) in enumerate(_other_chips(x, y)):
            s_ref, d_ref = _ici_refs(gather, src_ref, land_ref, j, px, py, (x, y), c)
            pltpu.make_async_remote_copy(
                src_ref=s_ref, dst_ref=d_ref, send_sem=send_sems.at[j], recv_sem=recv_sems.at[j],
                device_id=(px, py, c), device_id_type=MESH_ID).start()
        token[...] = jnp.zeros_like(token)

    return pl.pallas_call(
        body, name=name,
        out_shape=(pltpu.SemaphoreType.DMA((3,)), pltpu.SemaphoreType.DMA((3,)), pltpu.HBM(src.shape, src.dtype),
                   pltpu.HBM(land_shape, src.dtype), jax.ShapeDtypeStruct((8, 128), F32)),
        in_specs=(HBM_SPEC, HBM_SPEC, ANY),
        out_specs=(SEM_SPEC, SEM_SPEC, HBM_SPEC, HBM_SPEC, pl.BlockSpec(memory_space=pltpu.VMEM)),
        input_output_aliases={0: 2, 1: 3},
        compiler_params=pltpu.CompilerParams(has_side_effects=SPLIT_EFFECT),
    )(pltpu.with_memory_space_constraint(src, pltpu.HBM),
      pltpu.with_memory_space_constraint(lax.empty(land_shape, src.dtype), pltpu.HBM), after)


def _ici_wait(started, after, name, gather):
    send_sems, recv_sems, src_thru, land_thru, _ = started

    def body(src_ref, land_ref, send_sems, recv_sems, after_ref, src_dead, got_ref):
        x, y, c = _coords()
        for j, (px, py) in enumerate(_other_chips(x, y)):
            s_ref, d_ref = _ici_refs(gather, src_ref, land_ref, j, px, py, (px, py), c)
            copy = pltpu.make_async_remote_copy(
                src_ref=s_ref, dst_ref=d_ref, send_sem=send_sems.at[j], recv_sem=recv_sems.at[j],
                device_id=(px, py, c), device_id_type=MESH_ID)
            copy.wait_send()
            copy.wait_recv()

    return pl.pallas_call(
        body, name=name,
        out_shape=(pltpu.HBM(src_thru.shape, src_thru.dtype), pltpu.HBM(land_thru.shape, land_thru.dtype)),
        in_specs=(HBM_SPEC, HBM_SPEC, SEM_SPEC, SEM_SPEC, ANY),
        out_specs=(HBM_SPEC, HBM_SPEC), input_output_aliases={0: 0, 1: 1},
        compiler_params=pltpu.CompilerParams(has_side_effects=SPLIT_EFFECT),
    )(src_thru, land_thru, send_sems, recv_sems, after)


def _neighbour(k, x, y):
    return (1 - x, y) if k == 0 else (x, 1 - y)


def _slot(ref, px, py, c):
    return ref.at[:, 4 * px + 2 * py + c]


def _near_start(src, after, name):
    land_shape = (src.shape[0], N_DEV) + src.shape[1:]

    def body(src_ref, land_ref, after_ref, send_sems, recv_sems, src_thru, land_thru, token):
        x, y, c = _coords()
        for k in range(2):
            px, py = _neighbour(k, x, y)
            pltpu.make_async_remote_copy(
                src_ref=src_ref, dst_ref=_slot(land_ref, x, y, c), send_sem=send_sems.at[k],
                recv_sem=recv_sems.at[k], device_id=(px, py, c), device_id_type=MESH_ID).start()
        token[...] = jnp.zeros_like(token)

    return pl.pallas_call(
        body, name=name,
        out_shape=(pltpu.SemaphoreType.DMA((2,)), pltpu.SemaphoreType.DMA((2,)), pltpu.HBM(src.shape, src.dtype),
                   pltpu.HBM(land_shape, src.dtype), jax.ShapeDtypeStruct((8, 128), F32)),
        in_specs=(HBM_SPEC, HBM_SPEC, ANY),
        out_specs=(SEM_SPEC, SEM_SPEC, HBM_SPEC, HBM_SPEC, pl.BlockSpec(memory_space=pltpu.VMEM)),
        input_output_aliases={0: 2, 1: 3},
        compiler_params=pltpu.CompilerParams(has_side_effects=SPLIT_EFFECT),
    )(pltpu.with_memory_space_constraint(src, pltpu.HBM),
      pltpu.with_memory_space_constraint(lax.empty(land_shape, src.dtype), pltpu.HBM), after)


def _near_wait(started, after, name):
    send_sems, recv_sems, src_thru, land_thru, _ = started

    def body(src_ref, land_ref, send_sems, recv_sems, after_ref, src_dead, got_ref):
        x, y, c = _coords()
        for k in range(2):
            px, py = _neighbour(k, x, y)
            copy = pltpu.make_async_remote_copy(
                src_ref=src_ref, dst_ref=_slot(land_ref, px, py, c), send_sem=send_sems.at[k],
                recv_sem=recv_sems.at[k], device_id=(px, py, c), device_id_type=MESH_ID)
            copy.wait_send()
            copy.wait_recv()

    return pl.pallas_call(
        body, name=name,
        out_shape=(pltpu.HBM(src_thru.shape, src_thru.dtype), pltpu.HBM(land_thru.shape, land_thru.dtype)),
        in_specs=(HBM_SPEC, HBM_SPEC, SEM_SPEC, SEM_SPEC, ANY),
        out_specs=(HBM_SPEC, HBM_SPEC), input_output_aliases={0: 0, 1: 1},
        compiler_params=pltpu.CompilerParams(has_side_effects=SPLIT_EFFECT),
    )(src_thru, land_thru, send_sems, recv_sems, after)


def _far_refs(land_ref, k, x, y, c, arriving):
    half = land_ref.shape[2] // 2
    rows = pl.ds(k * half, half)
    ox, oy = (1 - x, 1 - y) if arriving else _neighbour(k, x, y)
    return land_ref.at[:, 4 * ox + 2 * oy + c, rows]


def _far_start(land, after, name):
    def body(land_ref, after_ref, send_sems, recv_sems, land_thru, token):
        x, y, c = _coords()
        for k in range(2):
            block = _far_refs(land_ref, k, x, y, c, False)
            px, py = _neighbour(1 - k, x, y)
            pltpu.make_async_remote_copy(
                src_ref=block, dst_ref=block, send_sem=send_sems.at[k], recv_sem=recv_sems.at[k],
                device_id=(px, py, c), device_id_type=MESH_ID).start()
        token[...] = jnp.zeros_like(token)

    return pl.pallas_call(
        body, name=name,
        out_shape=(pltpu.SemaphoreType.DMA((2,)), pltpu.SemaphoreType.DMA((2,)),
                   pltpu.HBM(land.shape, land.dtype), jax.ShapeDtypeStruct((8, 128), F32)),
        in_specs=(HBM_SPEC, ANY),
        out_specs=(SEM_SPEC, SEM_SPEC, HBM_SPEC, pl.BlockSpec(memory_space=pltpu.VMEM)),
        input_output_aliases={0: 2},
        compiler_params=pltpu.CompilerParams(has_side_effects=SPLIT_EFFECT),
    )(pltpu.with_memory_space_constraint(land, pltpu.HBM), after)


def _far_wait(started, after, name):
    send_sems, recv_sems, land_thru, _ = started

    def body(land_ref, send_sems, recv_sems, after_ref, got_ref):
        x, y, c = _coords()
        for k in range(2):
            px, py = _neighbour(1 - k, x, y)
            copy = pltpu.make_async_remote_copy(
                src_ref=_far_refs(land_ref, k, x, y, c, False), dst_ref=_far_refs(land_ref, k, x, y, c, True),
                send_sem=send_sems.at[k], recv_sem=recv_sems.at[k], device_id=(px, py, c), device_id_type=MESH_ID)
            copy.wait_send()
            copy.wait_recv()

    return pl.pallas_call(
        body, name=name,
        out_shape=pltpu.HBM(land_thru.shape, land_thru.dtype),
        in_specs=(HBM_SPEC, SEM_SPEC, SEM_SPEC, ANY),
        out_specs=HBM_SPEC, input_output_aliases={0: 0},
        compiler_params=pltpu.CompilerParams(has_side_effects=SPLIT_EFFECT),
    )(land_thru, send_sems, recv_sems, after)


def _peer(k, x, y, c):
    return x ^ ((k >> 2) & 1), y ^ ((k >> 1) & 1), c ^ (k & 1)


def _peers_start(src, after, name):
    r, w = src.shape
    x, y, c = _coords()
    land = lax.dynamic_update_slice(jnp.zeros((N_DEV, r, w), src.dtype), src[None], (4 * x + 2 * y + c, 0, 0))

    def body(src_ref, land_ref, after_ref, send_sems, recv_sems, src_thru, land_thru, token):
        x, y, c = _coords()
        for k in range(1, N_DEV):
            pltpu.make_async_remote_copy(
                src_ref=src_ref, dst_ref=land_ref.at[4 * x + 2 * y + c],
                send_sem=send_sems.at[k - 1], recv_sem=recv_sems.at[k - 1],
                device_id=_peer(k, x, y, c), device_id_type=MESH_ID).start()
        token[...] = jnp.zeros_like(token)

    return pl.pallas_call(
        body, name=name,
        out_shape=(pltpu.SemaphoreType.DMA((N_DEV - 1,)), pltpu.SemaphoreType.DMA((N_DEV - 1,)),
                   pltpu.HBM(src.shape, src.dtype), pltpu.HBM(land.shape, src.dtype),
                   jax.ShapeDtypeStruct((8, 128), F32)),
        in_specs=(HBM_SPEC, HBM_SPEC, ANY),
        out_specs=(SEM_SPEC, SEM_SPEC, HBM_SPEC, HBM_SPEC, pl.BlockSpec(memory_space=pltpu.VMEM)),
        input_output_aliases={0: 2, 1: 3},
        compiler_params=pltpu.CompilerParams(has_side_effects=SPLIT_EFFECT),
    )(pltpu.with_memory_space_constraint(src, pltpu.HBM), pltpu.with_memory_space_constraint(land, pltpu.HBM), after)


def _peers_wait(started, after, name):
    send_sems, recv_sems, src_thru, land_thru, _ = started

    def body(src_ref, land_ref, send_sems, recv_sems, after_ref, src_dead, got_ref):
        x, y, c = _coords()
        for k in range(1, N_DEV):
            px, py, pc = _peer(k, x, y, c)
            copy = pltpu.make_async_remote_copy(
                src_ref=src_ref, dst_ref=land_ref.at[4 * px + 2 * py + pc],
                send_sem=send_sems.at[k - 1], recv_sem=recv_sems.at[k - 1],
                device_id=(px, py, pc), device_id_type=MESH_ID)
            copy.wait_send()
            copy.wait_recv()

    return pl.pallas_call(
        body, name=name,
        out_shape=(pltpu.HBM(src_thru.shape, src_thru.dtype), pltpu.HBM(land_thru.shape, land_thru.dtype)),
        in_specs=(HBM_SPEC, HBM_SPEC, SEM_SPEC, SEM_SPEC, ANY),
        out_specs=(HBM_SPEC, HBM_SPEC), input_output_aliases={0: 0, 1: 1},
        compiler_params=pltpu.CompilerParams(has_side_effects=SPLIT_EFFECT),
    )(src_thru, land_thru, send_sems, recv_sems, after)


def _share_refs(ref, k, x, y, c, sender_c):
    px, py = ([(x, y)] + _other_chips(x, y))[k]
    return ref.at[:, 4 * px + 2 * py + sender_c]


def _share_start(gathered, after, name):
    def body(g_ref, after_ref, send_sems, recv_sems, g_thru, token):
        x, y, c = _coords()
        for k in range(4):
            slot = _share_refs(g_ref, k, x, y, c, c)
            pltpu.make_async_remote_copy(
                src_ref=slot, dst_ref=slot, send_sem=send_sems.at[k], recv_sem=recv_sems.at[k],
                device_id=(x, y, 1 - c), device_id_type=MESH_ID).start()
        token[...] = jnp.zeros_like(token)

    return pl.pallas_call(
        body, name=name,
        out_shape=(pltpu.SemaphoreType.DMA((4,)), pltpu.SemaphoreType.DMA((4,)),
                   pltpu.HBM(gathered.shape, gathered.dtype), jax.ShapeDtypeStruct((8, 128), F32)),
        in_specs=(HBM_SPEC, ANY),
        out_specs=(SEM_SPEC, SEM_SPEC, HBM_SPEC, pl.BlockSpec(memory_space=pltpu.VMEM)),
        input_output_aliases={0: 2},
        compiler_params=pltpu.CompilerParams(has_side_effects=SPLIT_EFFECT),
    )(pltpu.with_memory_space_constraint(gathered, pltpu.HBM), after)


def _share_wait(started, after, name):
    send_sems, recv_sems, g_thru, _ = started

    def body(g_ref, send_sems, recv_sems, after_ref, got_ref):
        x, y, c = _coords()
        for k in range(4):
            copy = pltpu.make_async_remote_copy(
                src_ref=_share_refs(g_ref, k, x, y, c, c), dst_ref=_share_refs(g_ref, k, x, y, c, 1 - c),
                send_sem=send_sems.at[k], recv_sem=recv_sems.at[k],
                device_id=(x, y, 1 - c), device_id_type=MESH_ID)
            copy.wait_send()
            copy.wait_recv()

    return pl.pallas_call(
        body, name=name,
        out_shape=pltpu.HBM(g_thru.shape, g_thru.dtype),
        in_specs=(HBM_SPEC, SEM_SPEC, SEM_SPEC, ANY),
        out_specs=HBM_SPEC, input_output_aliases={0: 0},
        compiler_params=pltpu.CompilerParams(has_side_effects=SPLIT_EFFECT),
    )(g_thru, send_sems, recv_sems, after)


def _core_share(own, gathered, name):
    def body(own_ref, gin_ref, out_ref, stage, send_sems, recv_sems, local_sem):
        x, y, c = _coords()
        sibling = (x, y, 1 - c)
        chips = [(x, y)] + _other_chips(x, y)
        stage_in = pltpu.make_async_copy(own_ref, stage, local_sem)
        stage_in.start()
        sent, arriving = [], []
        for k, (px, py) in enumerate(chips):
            slot = out_ref.at[:, 4 * px + 2 * py + c]
            sent.append(pltpu.make_async_remote_copy(
                src_ref=own_ref if k == 0 else slot, dst_ref=slot,
                send_sem=send_sems.at[k], recv_sem=recv_sems.at[k], device_id=sibling, device_id_type=MESH_ID))
            arriving.append(pltpu.make_async_remote_copy(
                src_ref=own_ref, dst_ref=out_ref.at[:, 4 * px + 2 * py + (1 - c)],
                send_sem=send_sems.at[k], recv_sem=recv_sems.at[k], device_id=sibling, device_id_type=MESH_ID))
        for cp in sent:
            cp.start()
        stage_in.wait()
        stage_out = pltpu.make_async_copy(stage, out_ref.at[:, 4 * x + 2 * y + c], local_sem)
        stage_out.start()
        for cp in arriving:
            cp.wait_recv()
        for cp in sent:
            cp.wait_send()
        stage_out.wait()

    return pl.pallas_call(
        body, name=name,
        out_shape=jax.ShapeDtypeStruct(gathered.shape, own.dtype),
        in_specs=[ANY, ANY], out_specs=ANY, input_output_aliases={1: 0},
        scratch_shapes=[pltpu.VMEM(own.shape, own.dtype), pltpu.SemaphoreType.DMA((4,)),
                        pltpu.SemaphoreType.DMA((4,)), pltpu.SemaphoreType.DMA],
    )(own, gathered)


def _adamw(w, g, m, v):
    m = ADAM_B1 * m + (1.0 - ADAM_B1) * g
    v = ADAM_B2 * v + (1.0 - ADAM_B2) * (g * g)
    m_hat = m / ADAM_C1
    v_hat = v / ADAM_C2
    delta = -ADAM_LR * (m_hat / (jnp.sqrt(v_hat) + ADAM_EPS) + ADAM_WD * w)
    return delta, m, v


def _adam_big(units, chip_idx, tag, token):
    n = len(units)
    r, wd = units[0][2].shape
    tr, tw = _row_tile(r, 1024), 256

    def body(s_ref, tok_ref, *refs):
        for u in range(n):
            p_ref, l_ref, w_ref, m_ref, v_ref = refs[5 * u:5 * u + 5]
            g_ref, d_ref, mo_ref, vo_ref = refs[5 * n + 4 * u:5 * n + 4 * u + 4]
            g = p_ref[0].astype(F32)
            for j in range(3):
                g = g + l_ref[j].astype(F32)
            delta, mn, vn = _adamw(w_ref[...], g, m_ref[...], v_ref[...])
            g_ref[...] = g
            d_ref[...] = delta
            mo_ref[...] = mn
            vo_ref[...] = vn

    row = pl.BlockSpec((tr, tw), lambda i, j, s: (i, j))
    unit_specs = [pl.BlockSpec((1, tr, tw), lambda i, j, s: (s[0], i, j)),
                  pl.BlockSpec((3, tr, tw), lambda i, j, s: (0, i, j)), row, row, row]
    outs = pl.pallas_call(
        body, name="adam_big_" + tag,
        grid_spec=pltpu.PrefetchScalarGridSpec(
            num_scalar_prefetch=1, grid=(r // tr, wd // tw),
            in_specs=[pl.BlockSpec((8, 128), lambda i, j, s: (0, 0))] + unit_specs * n,
            out_specs=[row] * (4 * n)),
        out_shape=[jax.ShapeDtypeStruct((r, wd), F32)] * (4 * n),
        compiler_params=_cparams(),
    )(chip_idx, token, *[a for unit in units for a in unit])
    return [outs[4 * u:4 * u + 4] for u in range(n)]


def _adam_small(parts, w, m, v):
    _, r, wd = parts.shape

    def body(p_ref, w_ref, m_ref, v_ref, g_ref, d_ref, mo_ref, vo_ref):
        g = p_ref[0]
        for k in range(1, N_DEV):
            g = g + p_ref[k]
        delta, mn, vn = _adamw(w_ref[...], g, m_ref[...], v_ref[...])
        g_ref[...] = g
        d_ref[...] = delta
        mo_ref[...] = mn
        vo_ref[...] = vn

    return pl.pallas_call(
        body, name="adam_small",
        out_shape=[jax.ShapeDtypeStruct((r, wd), F32)] * 4,
        compiler_params=_cparams(),
    )(parts, w, m, v)


def _pad_rows(a, rows):
    return jnp.pad(a, ((0, rows - a.shape[0]), (0, 0)))


def _pad_w_in(w):
    cut = Q_RANK + KV_RANK + ROPE
    return jnp.concatenate([w[:, :cut], jnp.zeros((w.shape[0], 64), w.dtype), w[:, cut:]], axis=1)


def _unpad_w_in(w):
    cut = Q_RANK + KV_RANK + ROPE
    return jnp.concatenate([w[:, :cut], w[:, cut + 64:]], axis=1)


def _pack_mid(p):
    parts = [_pad_w_in(p["w_in"][0]), p["w_out"][0], p["w_mq"][0], p["w_mo"][0],
             p["w_mkv"][0].reshape(256, D_MODEL),
             _pad_rows(p["w_q_up"][0].T.reshape(24, D_MODEL), 32),
             p["w_kv_up"][0].reshape(16, D_MODEL)]
    return jnp.concatenate(parts, axis=0)


def _pack_ffn(w_gate, w_up, w_down, name):
    d, rows = w_gate.shape[1:]

    def body(g_ref, u_ref, d_ref, o_ref):
        eye = (lax.broadcasted_iota(jnp.int32, (d, d), 0) == lax.broadcasted_iota(jnp.int32, (d, d), 1)).astype(BF16)
        o_ref[0] = _dot_tn(g_ref[0].astype(BF16), eye).astype(BF16)
        o_ref[1] = _dot_tn(u_ref[0].astype(BF16), eye).astype(BF16)
        o_ref[2] = d_ref[0].astype(BF16)

    return pl.pallas_call(
        body, name=name, out_shape=jax.ShapeDtypeStruct((3, rows, d), BF16), compiler_params=_cparams(),
    )(w_gate, w_up, w_down)


def _pack_segments(p, group):
    if group == "mid":
        return _pack_mid(p)[None].astype(BF16)
    return _pack_ffn(p[group + "_w_gate"], p[group + "_w_up"], p[group + "_w_down"], "pack_" + group)


UNIT_WEIGHT = {"ffn1_g": ("ffn1_w_gate", True), "ffn1_u": ("ffn1_w_up", True), "ffn1_d": ("ffn1_w_down", False),
               "ffn2_g": ("ffn2_w_gate", True), "ffn2_u": ("ffn2_w_up", True), "ffn2_d": ("ffn2_w_down", False)}


def _pack_unit(p, unit):
    if unit == "mid":
        return _pack_mid(p)
    name, transposed = UNIT_WEIGHT[unit]
    return p[name][0].T if transposed else p[name][0]


def _unpack_unit(a, unit):
    if unit != "mid":
        name, transposed = UNIT_WEIGHT[unit]
        return {name: (a.T if transposed else a)[None]}
    seg = lambda n: a[SEG_OFF[n][0]:SEG_OFF[n][0] + SEG_OFF[n][1]]
    return {"w_in": _unpad_w_in(seg("w_in"))[None], "w_out": seg("w_out")[None], "w_mq": seg("w_mq")[None],
            "w_mo": seg("w_mo")[None], "w_mkv": seg("w_mkv").reshape(D_MODEL, 256)[None],
            "w_q_up": seg("w_q")[:24].reshape(96, Q_RANK).T[None],
            "w_kv_up": seg("w_kv").reshape(KV_RANK, 128)[None]}


def _unpack_gathered(full, group):
    if group != "mid":
        return {group: full.reshape(len(GROUP_SEGS[group]), -1, D_MODEL)}
    full = full[0]
    seg = lambda n: full[:, SEG_OFF[n][0]:SEG_OFF[n][0] + SEG_OFF[n][1]]
    rows = lambda n: seg(n).reshape(-1, D_MODEL)
    wq_t = seg("w_q")[:, :24].reshape(MLA_HEADS, NOPE + ROPE, Q_RANK)
    wq_t = jnp.pad(wq_t, ((0, 0), (0, HEAD_PAD - NOPE - ROPE), (0, 0))).reshape(MLA_HEADS * HEAD_PAD, Q_RANK)
    wkv = seg("w_kv").reshape(N_DEV, KV_RANK, 128).transpose(1, 0, 2).reshape(KV_RANK, N_DEV * 128)
    return {"w_in": rows("w_in"), "w_out": rows("w_out"), "w_mq": rows("w_mq"), "w_mo": rows("w_mo"),
            "w_mkv": seg("w_mkv").reshape(N_DEV, D_MODEL, 256), "w_q": wq_t, "w_kv": wkv}


def _pack_grads(gr):
    blk = lambda a: a.reshape(N_DEV, -1, D_MODEL)
    dwq = gr["w_q"].reshape(MLA_HEADS, HEAD_PAD, Q_RANK)[:, :NOPE + ROPE].reshape(N_DEV, 24, D_MODEL)
    dwq = jnp.pad(dwq, ((0, 0), (0, 8), (0, 0)))
    dwkv = gr["w_kv"].reshape(KV_RANK, N_DEV, 128).transpose(1, 0, 2).reshape(N_DEV, 16, D_MODEL)
    parts = [blk(gr["w_in"]), blk(gr["w_out"]), blk(gr["w_mq"]), blk(gr["w_mo"]),
             gr["w_mkv"].reshape(N_DEV, 256, D_MODEL), dwq, dwkv]
    return jnp.concatenate([a.astype(BF16) for a in parts], axis=1)


def _pack_small(vals):
    parts = []
    for n, r in SMALL_ROWS:
        parts.append(_pad_rows(vals[n].reshape(-1, 128), r) if n in vals else jnp.zeros((r, 128), F32))
    return jnp.concatenate(parts, axis=0)


def _unpack_small(a, shapes):
    out = {}
    for n, shape in shapes.items():
        o = SMALL_OFF[n][0]
        out[n] = a[o:o + int(np.prod(shape)) // 128].reshape(shape)
    return out


BIG_NAMES = ("ffn1_w_gate", "ffn1_w_up", "ffn1_w_down", "w_in", "w_q_up", "w_kv_up", "w_out", "w_mq", "w_mkv",
             "w_mo", "ffn2_w_gate", "ffn2_w_up", "ffn2_w_down")
SMALL_NAMES = ("ffn1_norm", "mix_norm", "q_norm", "kv_norm", "pool_w", "pool_scale", "xattn_norm", "mem_norm",
               "ffn2_norm", "final_norm")
WEIGHT_ORDER = ("ffn1_norm", "ffn1_w_gate", "ffn1_w_up", "ffn1_w_down", "mix_norm", "w_in", "q_norm", "w_q_up",
                "kv_norm", "w_kv_up", "pool_w", "pool_scale", "w_out", "xattn_norm", "mem_norm", "w_mq", "w_mkv",
                "w_mo", "ffn2_norm", "ffn2_w_gate", "ffn2_w_up", "ffn2_w_down", "final_norm")


def _rope_table():
    lane = np.arange(128)
    freqs = (1.0 / (ROPE_BASE ** (np.arange(0, ROPE, 2, dtype=np.float32) / ROPE))).astype(np.float32)
    tab = np.zeros((8, 128), np.float32)
    tab[0] = np.where(lane < ROPE, freqs[lane % (ROPE // 2)], 0.0)
    tab[1] = np.where(lane < ROPE // 2, -1.0, np.where(lane < ROPE, 1.0, 0.0))
    return jnp.asarray(tab)


def kernel(x, mem, positions, ffn1_norm, ffn1_w_gate, ffn1_w_up, ffn1_w_down, mix_norm, w_in, q_norm, w_q_up, kv_norm, w_kv_up, pool_w, pool_scale, w_out, xattn_norm, mem_norm, w_mq, w_mkv, w_mo, ffn2_norm, ffn2_w_gate, ffn2_w_up, ffn2_w_down, final_norm, loss_target, m_ffn1_norm, m_ffn1_w_gate, m_ffn1_w_up, m_ffn1_w_down, m_mix_norm, m_w_in, m_q_norm, m_w_q_up, m_kv_norm, m_w_kv_up, m_pool_w, m_pool_scale, m_w_out, m_xattn_norm, m_mem_norm, m_w_mq, m_w_mkv, m_w_mo, m_ffn2_norm, m_ffn2_w_gate, m_ffn2_w_up, m_ffn2_w_down, m_final_norm, v_ffn1_norm, v_ffn1_w_gate, v_ffn1_w_up, v_ffn1_w_down, v_mix_norm, v_w_in, v_q_norm, v_w_q_up, v_kv_norm, v_w_kv_up, v_pool_w, v_pool_scale, v_w_out, v_xattn_norm, v_mem_norm, v_w_mq, v_w_mkv, v_w_mo, v_ffn2_norm, v_ffn2_w_gate, v_ffn2_w_up, v_ffn2_w_down, v_final_norm):
    wts = dict(ffn1_norm=ffn1_norm, ffn1_w_gate=ffn1_w_gate, ffn1_w_up=ffn1_w_up, ffn1_w_down=ffn1_w_down,
               mix_norm=mix_norm, w_in=w_in, q_norm=q_norm, w_q_up=w_q_up, kv_norm=kv_norm, w_kv_up=w_kv_up,
               pool_w=pool_w, pool_scale=pool_scale, w_out=w_out, xattn_norm=xattn_norm, mem_norm=mem_norm,
               w_mq=w_mq, w_mkv=w_mkv, w_mo=w_mo, ffn2_norm=ffn2_norm, ffn2_w_gate=ffn2_w_gate,
               ffn2_w_up=ffn2_w_up, ffn2_w_down=ffn2_w_down, final_norm=final_norm)
    mom = dict(ffn1_norm=m_ffn1_norm, ffn1_w_gate=m_ffn1_w_gate, ffn1_w_up=m_ffn1_w_up, ffn1_w_down=m_ffn1_w_down,
               mix_norm=m_mix_norm, w_in=m_w_in, q_norm=m_q_norm, w_q_up=m_w_q_up, kv_norm=m_kv_norm,
               w_kv_up=m_w_kv_up, pool_w=m_pool_w, pool_scale=m_pool_scale, w_out=m_w_out, xattn_norm=m_xattn_norm,
               mem_norm=m_mem_norm, w_mq=m_w_mq, w_mkv=m_w_mkv, w_mo=m_w_mo, ffn2_norm=m_ffn2_norm,
               ffn2_w_gate=m_ffn2_w_gate, ffn2_w_up=m_ffn2_w_up, ffn2_w_down=m_ffn2_w_down, final_norm=m_final_norm)
    var = dict(ffn1_norm=v_ffn1_norm, ffn1_w_gate=v_ffn1_w_gate, ffn1_w_up=v_ffn1_w_up, ffn1_w_down=v_ffn1_w_down,
               mix_norm=v_mix_norm, w_in=v_w_in, q_norm=v_q_norm, w_q_up=v_w_q_up, kv_norm=v_kv_norm,
               w_kv_up=v_w_kv_up, pool_w=v_pool_w, pool_scale=v_pool_scale, w_out=v_w_out, xattn_norm=v_xattn_norm,
               mem_norm=v_mem_norm, w_mq=v_w_mq, w_mkv=v_w_mkv, w_mo=v_w_mo, ffn2_norm=v_ffn2_norm,
               ffn2_w_gate=v_ffn2_w_gate, ffn2_w_up=v_ffn2_w_up, ffn2_w_down=v_ffn2_w_down, final_norm=v_final_norm)

    t = x.shape[1]
    xs = x[0]
    mems = mem[0]
    target = loss_target[0]
    pos = positions.reshape(t, 1)
    row = lambda a: a.reshape(1, -1)
    rope_tab = _rope_table()

    cx, cy, cc = _coords()
    chip_idx = (2 * cx + cy).astype(jnp.int32).reshape(1)

    wb = {}
    for grp in ("ffn1", "mid", "ffn2"):
        wb[grp] = _pack_segments(wts, grp)
        if grp == "ffn1":
            near_ffn1 = _near_start(wb["ffn1"], pos, "ag_ffn1_near_start")
    mid_names = ("w_in", "w_out", "w_mq", "w_mo", "w_mkv", "w_q_up", "w_kv_up")

    def packed_during(token, units, after):
        one = 1.0 + token[0, 0]
        packs = {}
        for u in units:
            names = SMALL_NAMES if u == "small" else mid_names if u == "mid" else UNIT_WEIGHT[u][:1]
            held = [{n: p[n] * one for n in names} for p in (wts, mom, var)]
            packs[u] = tuple(_pack_small(h) if u == "small" else _pack_unit(h, u) for h in held)
        return lax.optimization_barrier((after, packs))

    after, adam_in = packed_during(near_ffn1[4], ("ffn1_g", "ffn1_u", "ffn1_d", "ffn2_g", "ffn2_u", "ffn2_d"), wb["ffn2"])
    own_ffn1, land_ffn1 = _near_wait(near_ffn1, after, "ag_ffn1_near_wait")
    far_ffn1 = _far_start(land_ffn1, own_ffn1, "ag_ffn1_far_start")
    after, more = packed_during(far_ffn1[3], ("mid", "small"), wb["mid"])
    adam_in.update(more)
    land_ffn1 = _far_wait(far_ffn1, after, "ag_ffn1_far_wait")
    full_ffn1 = _core_share(own_ffn1, land_ffn1, "ag_ffn1_share")
    fw = _unpack_gathered(full_ffn1, "ffn1")
    ag_mid = _ici_start(wb["mid"], full_ffn1, "ag_mid_start", True)
    g_ffn1, g_mix, g_q, g_kv = row(ffn1_norm), row(mix_norm), row(q_norm), row(kv_norm)
    g_x, g_mem, g_ffn2, g_fin = row(xattn_norm), row(mem_norm), row(ffn2_norm), row(final_norm)
    pool_wb = pool_w[0].astype(BF16)
    pool_sc = row(pool_scale)

    h1, n1, gate1, up1 = _ffn_fwd(xs, g_ffn1, fw["ffn1"], "ffn1_fwd", token=ag_mid[4])
    own_mid, land_mid = _ici_wait(ag_mid, h1, "ag_mid_wait", True)
    full_mid = _core_share(own_mid, land_mid, "ag_mid_share")
    fw.update(_unpack_gathered(full_mid, "mid"))
    ag_ffn2 = _ici_start(wb["ffn2"], full_mid, "ag_ffn2_start", True)
    u, z, qn, kvn, qh, kh, vh = _mix_prep(h1, g_mix, fw["w_in"], g_q, fw["w_q"], g_kv, fw["w_kv"], pos, rope_tab,
                                          token=ag_ffn2[4])
    a, lse = _attn_fwd(qh, kh, vh)
    p = _pool_fwd(z, pool_wb, pool_sc)
    memn, km, vm = _mem_kv(mems, g_mem, fw["w_mkv"])
    own_ffn2, land_ffn2 = _ici_wait(ag_ffn2, a, "ag_ffn2_wait", True)
    land_ffn2 = lax.dynamic_update_slice(land_ffn2, own_ffn2[:, None], (0, 4 * cx + 2 * cy + cc, 0, 0))
    share_ffn2 = _share_start(land_ffn2, a, "ag_ffn2_share_start")
    h2, h3, hn, qm, om = _xattn_fwd(h1, a, p, fw["w_out"], g_x, fw["w_mq"], km, vm, fw["w_mo"], token=share_ffn2[3])
    fw.update(_unpack_gathered(_share_wait(share_ffn2, h3, "ag_ffn2_share_wait"), "ffn2"))
    dh4, n2, gate2, up2, loss_part, dg_fin = _ffn_fwd(h3, g_ffn2, fw["ffn2"],
                                                      "ffn2_fwd", head=(target, g_fin))

    def reduce_start(g8, unit):
        part = _core_reduce(g8, unit)
        return _ici_start(part, g8, "rs_" + unit + "_start", False)

    def by_device(g):
        return g.reshape(N_DEV, -1, D_MODEL)

    rs = {}
    dh3, dgate2, dup2, act2, dg_ffn2 = _ffn_bwd_data(dh4, h3, g_ffn2, gate2, up2, fw["ffn2"], "ffn2_bwd")
    rs["ffn2_g"] = reduce_start(by_device(_tn_matmul(dgate2, n2, "ffn2_dwg", tmm=1408, m=D_FF, out_dtype=BF16)), "ffn2_g")
    rs["ffn2_u"] = reduce_start(by_device(_tn_matmul(dup2, n2, "ffn2_dwu", tmm=1408, m=D_FF, out_dtype=BF16,
                                                     token=rs["ffn2_g"][4])), "ffn2_u")
    rs["ffn2_d"] = reduce_start(by_device(_tn_matmul(act2, dh4, "ffn2_dwd", scale=0.5, tmm=1408, m=D_FF, out_dtype=BF16,
                                                     token=rs["ffn2_u"][4])), "ffn2_d")
    dh2, dqm, da, dp, dkm, dvm, dg_x = _xattn_bwd(dh3, h2, qm, g_x, fw["w_mq"], km, vm, fw["w_mo"], fw["w_out"],
                                                  token=rs["ffn2_d"][4])
    gr = {}
    gr["w_mo"] = _tn_matmul(om, dh3, "dw_mo", out_dtype=BF16)
    gr["w_mq"] = _tn_matmul(hn, dqm, "dw_mq", out_dtype=BF16)
    gr["w_out"] = jnp.concatenate([_tn_matmul(a, dh2, "dw_out_a", out_dtype=BF16),
                                   _tn_matmul(p, dh2, "dw_out_p", out_dtype=BF16)], axis=0)
    gr["w_mkv"], dg_mem = _mem_kv_bwd(dkm, dvm, memn, mems, g_mem, fw["w_mkv"])
    dz_pool, d_pool_w, d_pool_sc = _pool_bwd(dp, z, pool_wb, pool_sc)
    dqh, dkh, dvh = _attn_bwd(qh, kh, vh, da, lse, _attn_delta(a, da))
    dh1, dq, dkv, dz, dg_q, dg_kv, dg_mix = _mla_bwd(dqh, dkh, dvh, z, dz_pool, h1, dh2, g_mix, fw["w_in"], g_q,
                                                     fw["w_q"], g_kv, fw["w_kv"], pos, rope_tab)
    gr["w_q"] = _tn_matmul(dq, qn, "dw_q", out_dtype=BF16)
    gr["w_kv"] = _tn_matmul(kvn, dkv, "dw_kv", out_dtype=BF16)
    gr["w_in"] = _tn_matmul(u, dz, "dw_in", out_dtype=BF16)
    g_mid = _pack_grads(gr)
    part_mid = _core_reduce(g_mid, "mid")
    got = {}
    after = part_mid
    for unit in ("ffn2_g", "ffn2_u", "ffn2_d"):
        got[unit] = _ici_wait(rs[unit], after, "rs_" + unit + "_wait", False)
        after = got[unit][1]
    rs["mid"] = _ici_start(part_mid, after, "rs_mid_start", False)
    dx, dgate1, dup1, act1, dg_ffn1 = _ffn_bwd_data(dh1, xs, g_ffn1, gate1, up1, fw["ffn1"], "ffn1_bwd", token=rs["mid"][4])
    got["mid"] = _ici_wait(rs["mid"], dx, "rs_mid_wait", False)

    small_g = dict(ffn1_norm=dg_ffn1, mix_norm=dg_mix, q_norm=dg_q, kv_norm=dg_kv, pool_w=d_pool_w,
                   pool_scale=d_pool_sc, xattn_norm=dg_x, mem_norm=dg_mem, ffn2_norm=dg_ffn2, final_norm=dg_fin,
                   loss=loss_part)
    small_ag = _peers_start(_pack_small(small_g), got["mid"][1], "small_ag_start")
    rs["ffn1_g"] = reduce_start(by_device(_tn_matmul(dgate1, n1, "ffn1_dwg", tmm=1408, m=D_FF, out_dtype=BF16,
                                                     token=small_ag[4])), "ffn1_g")
    _, parts = _peers_wait(small_ag, rs["ffn1_g"][4], "small_ag_wait")
    small = _adam_small(parts, *adam_in["small"])
    small_sum = small[0]
    loss = small_sum[SMALL_OFF["loss"][0], 0]
    shapes = {n: wts[n].shape for n in SMALL_NAMES}
    small = [_unpack_small(s, shapes) for s in small]

    rs["ffn1_u"] = reduce_start(by_device(_tn_matmul(dup1, n1, "ffn1_dwu", tmm=1408, m=D_FF, out_dtype=BF16,
                                                     token=small_sum)), "ffn1_u")
    rs["ffn1_d"] = reduce_start(by_device(_tn_matmul(act1, dh1, "ffn1_dwd", scale=0.5, tmm=1408, m=D_FF, out_dtype=BF16,
                                                     token=rs["ffn1_u"][4])), "ffn1_d")

    big = {}

    def adam_units(names, token):
        units = [got[u] + adam_in[u] for u in names]
        res = _adam_big(units, chip_idx, "_".join(names), token)
        for u, four in zip(names, res):
            for k, packed in enumerate(four):
                big.setdefault(k, {}).update(_unpack_unit(packed, u))
        return res[-1][0]

    done = adam_units(["mid"], rs["ffn1_d"][4])
    done = adam_units(["ffn2_g", "ffn2_u", "ffn2_d"], done)
    got["ffn1_g"] = _ici_wait(rs["ffn1_g"], done, "rs_ffn1_g_wait", False)
    got["ffn1_u"] = _ici_wait(rs["ffn1_u"], got["ffn1_g"][1], "rs_ffn1_u_wait", False)
    done = adam_units(["ffn1_g", "ffn1_u"], done)
    got["ffn1_d"] = _ici_wait(rs["ffn1_d"], done, "rs_ffn1_d_wait", False)
    adam_units(["ffn1_d"], done)

    outs = [loss, dx[None]]
    for k in range(4):
        for n in WEIGHT_ORDER:
            outs.append(big[k][n] if n in BIG_NAMES else small[k][n])
    return tuple(outs)
```

```python
import numpy as np

import jax
import jax.numpy as jnp
from jax import lax
from jax.experimental import pallas as pl
from jax.experimental.pallas import tpu as pltpu

F32 = jnp.float32
BF16 = jnp.bfloat16

N_DEV = 8
D_MODEL = 1024
D_FF = 2816
MLA_HEADS = 4
NOPE = 128
ROPE = 64
HEAD_PAD = 256
V_DIM = 128
Q_RANK = 256
KV_RANK = 128
POOL_WINDOWS = (2, 4, 8, 16)
POOL_CH = 128
POOL_HALO = 16
N_MEM = 256
MEM_HEADS = 4
MEM_HD = 256
ROPE_BASE = 10000.0
RMS_EPS = 1e-6
ATTN_SCALE = (NOPE + ROPE) ** -0.5
MEM_SCALE = MEM_HD ** -0.5
NEG_BIG = -1e30

ADAM_LR = 0.001
ADAM_B1 = 0.9
ADAM_B2 = 0.999
ADAM_EPS = 1e-08
ADAM_WD = 0.01
ADAM_STEP = 10
ADAM_C1 = 1.0 - ADAM_B1 ** ADAM_STEP
ADAM_C2 = 1.0 - ADAM_B2 ** ADAM_STEP

VMEM_LIMIT_BYTES = 56 * 1024 * 1024
BF16_ROWS = 16

GROUP_SEGS = {
    "ffn1": (("ffn1_g", 352), ("ffn1_u", 352), ("ffn1_d", 352)),
    "mid": (("w_in", 128), ("w_out", 128), ("w_mq", 128), ("w_mo", 128), ("w_mkv", 256), ("w_q", 32), ("w_kv", 16)),
    "ffn2": (("ffn2_g", 352), ("ffn2_u", 352), ("ffn2_d", 352)),
}
SEG_OFF = {}
GROUP_ROWS = {}
for _g, _segs in GROUP_SEGS.items():
    _o = 0
    for _n, _r in _segs:
        SEG_OFF[_n] = (_o, _r)
        _o += _r
    GROUP_ROWS[_g] = _o

SMALL_ROWS = (("ffn1_norm", 8), ("mix_norm", 8), ("q_norm", 8), ("kv_norm", 8), ("pool_w", 512), ("pool_scale", 8),
              ("xattn_norm", 8), ("mem_norm", 8), ("ffn2_norm", 8), ("final_norm", 8), ("loss", 8))
SMALL_OFF = {}
_o = 0
for _n, _r in SMALL_ROWS:
    SMALL_OFF[_n] = (_o, _r)
    _o += _r


def _cparams(**kw):
    return pltpu.CompilerParams(vmem_limit_bytes=VMEM_LIMIT_BYTES, **kw)


def _row_tile(rows, limit):
    best = None
    for cand in range(BF16_ROWS, min(rows, limit) + 1, BF16_ROWS):
        if rows % cand == 0:
            best = cand
    assert best is not None, rows
    return best


def _dot_nn(a, b):
    return lax.dot_general(a, b, (((1,), (0,)), ((), ())), preferred_element_type=F32)


def _dot_nt(a, b):
    return lax.dot_general(a, b, (((1,), (1,)), ((), ())), preferred_element_type=F32)


def _dot_tn(a, b):
    return lax.dot_general(a, b, (((0,), (0,)), ((), ())), preferred_element_type=F32)


def _rms_fwd(x, g):
    r = lax.rsqrt(jnp.mean(x * x, axis=-1, keepdims=True) + RMS_EPS)
    return x * r * g, r


def _rms_bwd(dy, x, g, r):
    xhat = x * r
    dyg = dy * g
    dx = r * (dyg - xhat * jnp.mean(dyg * xhat, axis=-1, keepdims=True))
    dg = jnp.sum(dy * xhat, axis=0, keepdims=True)
    return dx, dg


def _accumulate(ref, val, first):
    if isinstance(first, bool):
        if first:
            ref[...] = val
        else:
            ref[...] += val
        return

    @pl.when(first)
    def _():
        ref[...] = val

    @pl.when(jnp.logical_not(first))
    def _():
        ref[...] += val


def _call_after(token, body, in_specs, args, **kw):
    if token is not None:
        inner = body
        body = lambda tok_ref, *refs: inner(*refs)
        in_specs = [pl.BlockSpec((8, 128), lambda *_: (0, 0))] + list(in_specs)
        args = (token,) + tuple(args)
    return pl.pallas_call(body, in_specs=in_specs, **kw)(*args)


def _resident(shape):
    return pl.BlockSpec(shape, lambda *_: (0,) * len(shape), pipeline_mode=pl.Buffered(1))


def _rope_tables(pos_col, tab):
    ang = pos_col.astype(F32) * tab[0:1, :]
    return jnp.cos(ang), jnp.sin(ang) * tab[1:2, :]


def _swap_halves(x):
    lane = lax.broadcasted_iota(jnp.int32, x.shape, 1)
    return jnp.where((lane % 64) < 32, pltpu.roll(x, 96, 1), pltpu.roll(x, 32, 1))


def _rope_apply(x, cos_t, sin_t):
    return x * cos_t + _swap_halves(x) * sin_t


def _rope_apply_t(dy, cos_t, sin_t):
    return dy * cos_t + _swap_halves(dy * sin_t)


def _ffn_fwd(h, g, w, name, token=None, head=None):
    t, d = h.shape
    f = w.shape[1]
    tm, tf = min(512, t), 256
    nf = f // tf
    n_in = 3 if head is None else 5

    def body(*refs):
        h_ref, g_ref, w_ref = refs[:3]
        ho_ref, n_ref, gate_ref, up_ref = refs[n_in:n_in + 4]
        nb_sc, acc_sc = refs[-2:]
        y, _ = _rms_fwd(h_ref[...], g_ref[...])
        nb = y.astype(BF16)
        nb_sc[...] = nb
        n_ref[...] = nb
        acc_sc[...] = jnp.zeros_like(acc_sc)

        def f_tile(j):
            rows = pl.ds(pl.multiple_of(j * tf, tf), tf)
            nb = nb_sc[...]
            gt = _dot_nt(nb, w_ref[0, rows, :])
            ut = _dot_nt(nb, w_ref[1, rows, :])
            gate_ref[j] = gt.astype(BF16)
            up_ref[j] = ut.astype(BF16)
            act = (gt * jax.nn.sigmoid(gt)) * ut
            return _dot_nn(act.astype(BF16), w_ref[2, rows, :])

        def pair(p, carry):
            acc_sc[...] += f_tile(2 * p) + f_tile(2 * p + 1)
            return carry

        lax.fori_loop(0, nf // 2, pair, 0)
        if nf % 2:
            acc_sc[...] += f_tile(nf - 1)
        ho = h_ref[...] + 0.5 * acc_sc[...]
        if head is None:
            ho_ref[...] = ho
            return
        t_ref, gf_ref = refs[3:5]
        loss_ref, dgf_ref = refs[n_in + 4:n_in + 6]
        gg = gf_ref[...]
        y, r = _rms_fwd(ho, gg)
        err = y - t_ref[...]
        part = 0.5 * jnp.sum(jnp.mean(err * err, axis=-1, keepdims=True), axis=0, keepdims=True)
        dx, dg = _rms_bwd(err * (1.0 / d), ho, gg, r)
        ho_ref[...] = dx
        first = pl.program_id(0) == 0
        _accumulate(loss_ref, jnp.broadcast_to(part, loss_ref.shape), first)
        _accumulate(dgf_ref, dg, first)

    row = pl.BlockSpec((tm, d), lambda i: (i, 0))
    tiles = pl.BlockSpec((nf, tm, tf), lambda i: (0, i, 0))
    in_specs = [row, _resident((1, d)), _resident(w.shape)]
    args = (h, g, w)
    out_specs = [row, row, tiles, tiles]
    out_shape = [jax.ShapeDtypeStruct((t, d), F32), jax.ShapeDtypeStruct((t, d), BF16),
                 jax.ShapeDtypeStruct((nf, t, tf), BF16), jax.ShapeDtypeStruct((nf, t, tf), BF16)]
    if head is not None:
        in_specs += [row, _resident((1, d))]
        args += tuple(head)
        out_specs += [pl.BlockSpec((8, 128), lambda i: (0, 0)), pl.BlockSpec((1, d), lambda i: (0, 0))]
        out_shape += [jax.ShapeDtypeStruct((8, 128), F32), jax.ShapeDtypeStruct((1, d), F32)]
    return _call_after(
        token, body, in_specs, args, name=name, grid=(t // tm,), out_specs=out_specs, out_shape=out_shape,
        scratch_shapes=[pltpu.VMEM((tm, d), BF16), pltpu.VMEM((tm, d), F32)],
        compiler_params=_cparams(),
    )


def _ffn_bwd_data(dho, h, g, gate, up, w, name, token=None):
    t, d = h.shape
    f = w.shape[1]
    tm, tf = min(1024, t), 256
    parts = 2 if tm % 512 == 0 else 1
    tp = tm // parts
    nf = f // tf
    npair, odd = nf // 2, nf % 2
    nsteps = npair + odd

    def body(dho_ref, h_ref, g_ref, gate_ref, up_ref, wg_ref, wu_ref, wd_ref,
             dh_ref, dgate_ref, dup_ref, act_ref, dg_ref, dhb_sc, acc_sc):
        i, j = pl.program_id(0), pl.program_id(1)

        @pl.when(j == 0)
        def _():
            dhb_sc[...] = (0.5 * dho_ref[...]).astype(BF16)
            acc_sc[...] = jnp.zeros_like(acc_sc)

        def slab(ntile):
            cols = pl.ds(0, ntile * tf)
            for r in range(parts):
                rows = pl.ds(r * tp, tp)
                gt = jnp.concatenate([gate_ref[k, rows, :] for k in range(ntile)], axis=-1).astype(F32)
                ut = jnp.concatenate([up_ref[k, rows, :] for k in range(ntile)], axis=-1).astype(F32)
                dact = _dot_nt(dhb_sc[rows, :], wd_ref[cols, :])
                sg = jax.nn.sigmoid(gt)
                silu = gt * sg
                dgb = (dact * ut * (sg * (1.0 + gt * (1.0 - sg)))).astype(BF16)
                dub = (dact * silu).astype(BF16)
                act_ref[rows, cols] = (silu * ut).astype(BF16)
                dgate_ref[rows, cols] = dgb
                dup_ref[rows, cols] = dub
                acc_sc[rows, :] += _dot_nn(dgb, wg_ref[cols, :]) + _dot_nn(dub, wu_ref[cols, :])

        pl.when(j < npair)(lambda: slab(2))
        if odd:
            pl.when(j == npair)(lambda: slab(1))

        @pl.when(j == nsteps - 1)
        def _():
            x = h_ref[...]
            gg = g_ref[...]
            _, r = _rms_fwd(x, gg)
            dx, dg = _rms_bwd(acc_sc[...], x, gg, r)
            dh_ref[...] = dho_ref[...] + dx
            _accumulate(dg_ref, dg, i == 0)

    row = pl.BlockSpec((tm, d), lambda i, j: (i, 0))
    acts = pl.BlockSpec((2, tm, tf), lambda i, j: (j, i, 0))
    weights = lambda k: pl.BlockSpec((None, 2 * tf, d), lambda i, j: (k, j, 0))
    outs = pl.BlockSpec((tm, 2 * tf), lambda i, j: (i, j))
    padded = jax.ShapeDtypeStruct((t, 2 * tf * nsteps), BF16)
    return _call_after(
        token, body,
        [row, row, pl.BlockSpec((1, d), lambda i, j: (0, 0)), acts, acts, weights(0), weights(1), weights(2)],
        (dho, h, g, gate, up, w, w, w),
        name=name, grid=(t // tm, nsteps),
        out_specs=[row, outs, outs, outs, pl.BlockSpec((1, d), lambda i, j: (0, 0))],
        out_shape=[jax.ShapeDtypeStruct((t, d), F32), padded, padded, padded, jax.ShapeDtypeStruct((1, d), F32)],
        scratch_shapes=[pltpu.VMEM((tm, d), BF16), pltpu.VMEM((tm, d), F32)],
        compiler_params=_cparams(),
    )


def _tn_matmul(a, b, name, scale=1.0, tmm=None, out_dtype=F32, token=None, m=None):
    t = a.shape[0]
    m = a.shape[1] if m is None else m
    n = b.shape[1]
    tmm = m if tmm is None else tmm
    tk = min(1024, t)
    nk = t // tk

    def product(a_ref, b_ref):
        prod = _dot_tn(a_ref[...].astype(BF16), b_ref[...].astype(BF16))
        return prod * scale if scale != 1.0 else prod

    def body_f32(a_ref, b_ref, o_ref):
        _accumulate(o_ref, product(a_ref, b_ref), pl.program_id(1) == 0)

    def body_cast(a_ref, b_ref, o_ref, acc_sc):
        k = pl.program_id(1)
        _accumulate(acc_sc, product(a_ref, b_ref), k == 0)

        @pl.when(k == nk - 1)
        def _():
            o_ref[...] = acc_sc[...].astype(out_dtype)

    direct = out_dtype == F32
    return _call_after(
        token, body_f32 if direct else body_cast,
        [pl.BlockSpec((tk, tmm), lambda i, k: (k, i)),
         pl.BlockSpec((tk, n), lambda i, k: (k, 0))],
        (a, b),
        name=name, grid=(m // tmm, nk),
        out_specs=pl.BlockSpec((tmm, n), lambda i, k: (i, 0)),
        out_shape=jax.ShapeDtypeStruct((m, n), out_dtype),
        scratch_shapes=[] if direct else [pltpu.VMEM((tmm, n), F32)],
        compiler_params=_cparams(),
    )


def _mix_prep(h1, mix_norm, w_in, q_norm, wq_t, kv_norm, wkv, pos, rope_tab, token=None):
    t, d = h1.shape
    tm = min(512, t)

    def body(h_ref, gm_ref, win_ref, gq_ref, wq_ref, gkv_ref, wkv_ref, pos_ref, tab_ref,
             u_ref, z_ref, qn_ref, kvn_ref, q_ref, k_ref, v_ref):
        u, _ = _rms_fwd(h_ref[...], gm_ref[...])
        ub = u.astype(BF16)
        u_ref[...] = ub
        z = _dot_nn(ub, win_ref[...])
        z_ref[...] = z
        cos_t, sin_t = _rope_tables(pos_ref[...], tab_ref[...])
        qn, _ = _rms_fwd(z[:, 0:Q_RANK], gq_ref[...])
        qnb = qn.astype(BF16)
        qn_ref[...] = qnb
        q = _dot_nt(qnb, wq_ref[...])
        kvn, _ = _rms_fwd(z[:, Q_RANK:Q_RANK + KV_RANK], gkv_ref[...])
        kvnb = kvn.astype(BF16)
        kvn_ref[...] = kvnb
        kv = _dot_nn(kvnb, wkv_ref[...])
        k_pe = _rope_apply(z[:, Q_RANK + KV_RANK:Q_RANK + KV_RANK + 128], cos_t, sin_t)
        ones = jnp.ones((tm, V_DIM), F32)
        for hh in range(MLA_HEADS):
            b = hh * HEAD_PAD
            q_pe = _rope_apply(q[:, b + NOPE:b + HEAD_PAD], cos_t, sin_t)
            q_ref[hh] = jnp.concatenate([q[:, b:b + NOPE], q_pe], axis=-1).astype(BF16)
            k_ref[hh] = jnp.concatenate([kv[:, b:b + NOPE], k_pe], axis=-1).astype(BF16)
            v_ref[hh] = jnp.concatenate([kv[:, b + NOPE:b + HEAD_PAD], ones], axis=-1).astype(BF16)

    full = lambda shape: pl.BlockSpec(shape, lambda i: (0,) * len(shape))
    return _call_after(
        token, body,
        [pl.BlockSpec((tm, d), lambda i: (i, 0)), _resident((1, d)), _resident(w_in.shape), _resident((1, Q_RANK)),
         _resident(wq_t.shape), _resident((1, KV_RANK)), _resident(wkv.shape),
         pl.BlockSpec((tm, 1), lambda i: (i, 0)), _resident(rope_tab.shape)],
        (h1, mix_norm, w_in, q_norm, wq_t, kv_norm, wkv, pos, rope_tab),
        name="mix_prep", grid=(t // tm,),
        out_specs=[pl.BlockSpec((tm, d), lambda i: (i, 0)),
                   pl.BlockSpec((tm, d), lambda i: (i, 0)),
                   pl.BlockSpec((tm, Q_RANK), lambda i: (i, 0)),
                   pl.BlockSpec((tm, KV_RANK), lambda i: (i, 0)),
                   pl.BlockSpec((MLA_HEADS, tm, HEAD_PAD), lambda i: (0, i, 0)),
                   pl.BlockSpec((MLA_HEADS, tm, HEAD_PAD), lambda i: (0, i, 0)),
                   pl.BlockSpec((MLA_HEADS, tm, 2 * V_DIM), lambda i: (0, i, 0))],
        out_shape=[jax.ShapeDtypeStruct((t, d), BF16), jax.ShapeDtypeStruct((t, d), F32),
                   jax.ShapeDtypeStruct((t, Q_RANK), BF16), jax.ShapeDtypeStruct((t, KV_RANK), BF16),
                   jax.ShapeDtypeStruct((MLA_HEADS, t, HEAD_PAD), BF16),
                   jax.ShapeDtypeStruct((MLA_HEADS, t, HEAD_PAD), BF16),
                   jax.ShapeDtypeStruct((MLA_HEADS, t, 2 * V_DIM), BF16)],
        compiler_params=_cparams(),
    )


def _causal_mask(s):
    row = lax.broadcasted_iota(jnp.int32, s.shape, 0)
    col = lax.broadcasted_iota(jnp.int32, s.shape, 1)
    return jnp.where(col <= row, s, NEG_BIG)


def _attn_fwd(q, k, v):
    nh, t, _ = q.shape
    tq = tk = min(512, t)
    nq, nk = t // tq, t // tk

    pairs = [(i, j) for i in range(nq) for j in range(i + 1)]
    qi = jnp.asarray(np.array([i for i, _ in pairs], np.int32))
    kj = jnp.asarray(np.array([j for _, j in pairs], np.int32))

    def body(qi_ref, kj_ref, q_ref, k_ref, v_ref, o_ref, lse_ref, m_sc, acc_sc):
        n = pl.program_id(0)
        i, j = qi_ref[n], kj_ref[n]

        @pl.when(j == 0)
        def _():
            m_sc[...] = jnp.full_like(m_sc, NEG_BIG)
            acc_sc[...] = jnp.zeros_like(acc_sc)

        def step(diagonal):
            for hh in range(nh):
                s = _dot_nt(q_ref[hh], k_ref[hh]) * ATTN_SCALE
                if diagonal:
                    s = _causal_mask(s)
                m_old = m_sc[hh]
                m_new = jnp.maximum(m_old, jnp.max(s, axis=-1, keepdims=True))
                p = jnp.exp(s - m_new).astype(BF16)
                acc_sc[hh] = jnp.exp(m_old - m_new) * acc_sc[hh] + _dot_nn(p, v_ref[hh])
                m_sc[hh] = m_new

        @pl.when(j < i)
        def _():
            step(False)

        @pl.when(j == i)
        def _():
            step(True)
            for hh in range(nh):
                acc = acc_sc[hh]
                l = acc[:, V_DIM:2 * V_DIM]
                o_ref[:, hh * V_DIM:(hh + 1) * V_DIM] = (acc[:, 0:V_DIM] / l).astype(BF16)
                lse_ref[hh] = m_sc[hh] + jnp.log(l[:, 0:1])

    q_map = lambda n, qi_ref, kj_ref: (0, qi_ref[n], 0)
    kv_map = lambda n, qi_ref, kj_ref: (0, kj_ref[n], 0)
    return pl.pallas_call(
        body, name="attn_fwd",
        grid_spec=pltpu.PrefetchScalarGridSpec(
            num_scalar_prefetch=2, grid=(len(pairs),),
            in_specs=[pl.BlockSpec((nh, tq, HEAD_PAD), q_map),
                      pl.BlockSpec((nh, tk, HEAD_PAD), kv_map),
                      pl.BlockSpec((nh, tk, 2 * V_DIM), kv_map)],
            out_specs=[pl.BlockSpec((tq, nh * V_DIM), lambda n, qi_ref, kj_ref: (qi_ref[n], 0)),
                       pl.BlockSpec((nh, tq, 1), q_map)],
            scratch_shapes=[pltpu.VMEM((nh, tq, 1), F32), pltpu.VMEM((nh, tq, 2 * V_DIM), F32)]),
        out_shape=[jax.ShapeDtypeStruct((t, nh * V_DIM), BF16), jax.ShapeDtypeStruct((nh, t, 1), F32)],
        compiler_params=_cparams(),
    )(qi, kj, q, k, v)


def _attn_delta(o, do):
    t, w = o.shape
    nh = w // V_DIM
    tm = min(512, t)

    def body(o_ref, do_ref, d_ref):
        prod = o_ref[...].astype(F32) * do_ref[...].astype(F32)
        for hh in range(nh):
            d_ref[hh] = jnp.sum(prod[:, hh * V_DIM:(hh + 1) * V_DIM], axis=-1, keepdims=True)

    return pl.pallas_call(
        body, name="attn_delta", grid=(t // tm,),
        in_specs=[pl.BlockSpec((tm, w), lambda i: (i, 0)), pl.BlockSpec((tm, w), lambda i: (i, 0))],
        out_specs=pl.BlockSpec((nh, tm, 1), lambda i: (0, i, 0)),
        out_shape=jax.ShapeDtypeStruct((nh, t, 1), F32),
        compiler_params=_cparams(),
    )(o, do)


ATTN_BWD_HEADS = 2


def _attn_bwd(q, k, v, do, lse, delta):
    nh, t, _ = q.shape
    hp = ATTN_BWD_HEADS
    tq = tk = min(512, t)
    nq, nk = t // tq, t // tk

    pairs = [(j, i) for j in range(nk) for i in range(j, nq)]
    kj = jnp.asarray(np.array([j for j, _ in pairs], np.int32))
    qi = jnp.asarray(np.array([i for _, i in pairs], np.int32))

    def body(kj_ref, qi_ref, q_ref, k_ref, v_ref, do_ref, lse_ref, dlt_ref, dq_ref, dk_ref, dv_ref):
        n = pl.program_id(1)
        j, i = kj_ref[n], qi_ref[n]

        @pl.when(n == 0)
        def _():
            dq_ref[...] = jnp.zeros_like(dq_ref)

        def step(diagonal):
            for hh in range(hp):
                qq, kk = q_ref[hh], k_ref[hh]
                dob = do_ref[:, hh * V_DIM:(hh + 1) * V_DIM]
                s = _dot_nt(qq, kk) * ATTN_SCALE
                if diagonal:
                    s = _causal_mask(s)
                p = jnp.exp(s - lse_ref[hh])
                dpp = _dot_nt(dob, v_ref[hh])
                dsb = (p * (dpp - dlt_ref[hh]) * ATTN_SCALE).astype(BF16)
                _accumulate(dv_ref.at[hh], _dot_tn(p.astype(BF16), dob), diagonal)
                _accumulate(dk_ref.at[hh], _dot_tn(dsb, qq), diagonal)
                dq_ref[hh, pl.ds(pl.multiple_of(i * tq, tq), tq), :] += _dot_nn(dsb, kk)

        @pl.when(i > j)
        def _():
            step(False)

        @pl.when(i == j)
        def _():
            step(True)

    q_map = lambda h, n, kj_ref, qi_ref: (h, qi_ref[n], 0)
    k_map = lambda h, n, kj_ref, qi_ref: (h, kj_ref[n], 0)
    return pl.pallas_call(
        body, name="attn_bwd",
        grid_spec=pltpu.PrefetchScalarGridSpec(
            num_scalar_prefetch=2, grid=(nh // hp, len(pairs)),
            in_specs=[pl.BlockSpec((hp, tq, HEAD_PAD), q_map),
                      pl.BlockSpec((hp, tk, HEAD_PAD), k_map),
                      pl.BlockSpec((hp, tk, V_DIM), k_map),
                      pl.BlockSpec((tq, hp * V_DIM), lambda h, n, kj_ref, qi_ref: (qi_ref[n], h)),
                      pl.BlockSpec((hp, tq, 1), q_map),
                      pl.BlockSpec((hp, tq, 1), q_map)],
            out_specs=[pl.BlockSpec((hp, t, HEAD_PAD), lambda h, n, kj_ref, qi_ref: (h, 0, 0)),
                       pl.BlockSpec((hp, tk, HEAD_PAD), k_map),
                       pl.BlockSpec((hp, tk, V_DIM), k_map)]),
        out_shape=[jax.ShapeDtypeStruct((nh, t, HEAD_PAD), F32), jax.ShapeDtypeStruct((nh, t, HEAD_PAD), F32),
                   jax.ShapeDtypeStruct((nh, t, V_DIM), F32)],
        compiler_params=_cparams(),
    )(kj, qi, q, k, v, do, lse, delta)


def _pool_counts(first_token, rows, w):
    tok = lax.broadcasted_iota(jnp.int32, (rows, POOL_CH), 0) + first_token
    return jnp.minimum(tok + 1, w).astype(F32)


def _pool_centered(zbuf, g, w, i, tm):
    lanes = pl.ds(g * POOL_CH, POOL_CH)
    cur = zbuf[pl.ds(POOL_HALO, tm), lanes]
    win = cur
    for s in range(1, w):
        win = win + zbuf[pl.ds(POOL_HALO - s, tm), lanes]
    return win / _pool_counts(i * tm, tm, w) - cur


def _pool_load(zbuf, z_ref, halo_ref, i, tm):
    @pl.when(i == 0)
    def _():
        zbuf[pl.ds(0, POOL_HALO), :] = jnp.zeros((POOL_HALO, zbuf.shape[1]), F32)

    @pl.when(i > 0)
    def _():
        zbuf[pl.ds(0, POOL_HALO), :] = halo_ref[...]

    zbuf[pl.ds(POOL_HALO, tm), :] = z_ref[...]


def _pool_fwd(z, pool_w, pool_scale):
    t = z.shape[0]
    pw = len(POOL_WINDOWS) * POOL_CH
    tm = min(512, t)
    hb = tm // POOL_HALO

    def body(z_ref, halo_ref, w_ref, sc_ref, p_ref, zbuf):
        i = pl.program_id(0)
        _pool_load(zbuf, z_ref, halo_ref, i, tm)
        for g, w in enumerate(POOL_WINDOWS):
            c = _pool_centered(zbuf, g, w, i, tm)
            y = _dot_nn(c.astype(BF16), w_ref[g]) * sc_ref[:, g * POOL_CH:(g + 1) * POOL_CH]
            p_ref[:, g * POOL_CH:(g + 1) * POOL_CH] = y.astype(BF16)

    return pl.pallas_call(
        body, name="pool_fwd", grid=(t // tm,),
        in_specs=[pl.BlockSpec((tm, pw), lambda i: (i, 1)),
                  pl.BlockSpec((POOL_HALO, pw), lambda i: (jnp.maximum(i * hb - 1, 0), 1)),
                  pl.BlockSpec(pool_w.shape, lambda i: (0, 0, 0)),
                  pl.BlockSpec((1, pw), lambda i: (0, 0))],
        out_specs=pl.BlockSpec((tm, pw), lambda i: (i, 0)),
        out_shape=jax.ShapeDtypeStruct((t, pw), BF16),
        scratch_shapes=[pltpu.VMEM((POOL_HALO + tm, pw), F32)],
        compiler_params=_cparams(),
    )(z, z, pool_w, pool_scale)


def _pool_bwd(dp, z, pool_w, pool_scale):
    t = z.shape[0]
    ng = len(POOL_WINDOWS)
    pw = ng * POOL_CH
    tm = min(512, t)
    hb = tm // POOL_HALO
    nt = t // tm

    def body(dp_ref, dpn_ref, z_ref, halo_ref, w_ref, sc_ref, dz_ref, dw_ref, dsc_ref, zbuf, dbuf):
        i = pl.program_id(0)
        _pool_load(zbuf, z_ref, halo_ref, i, tm)

        @pl.when(i == 0)
        def _():
            dw_ref[...] = jnp.zeros_like(dw_ref)
            dsc_ref[...] = jnp.zeros_like(dsc_ref)

        nxt_ok = (i < nt - 1).astype(F32)
        for g, w in enumerate(POOL_WINDOWS):
            lanes = pl.ds(g * POOL_CH, POOL_CH)
            cols = slice(g * POOL_CH, (g + 1) * POOL_CH)
            sc = sc_ref[:, cols]
            wg = w_ref[g]
            c = _pool_centered(zbuf, g, w, i, tm).astype(BF16)
            ypre = _dot_nn(c, wg)
            dpg = dp_ref[:, cols].astype(F32)
            dsc_ref[:, cols] += jnp.sum(dpg * ypre, axis=0, keepdims=True)
            dyb = (dpg * sc).astype(BF16)
            dw_ref[g] += _dot_tn(c, dyb)
            dd = _dot_nt(dyb, wg)
            dyn = (dpn_ref[:, cols].astype(F32) * sc).astype(BF16)
            ddn = _dot_nt(dyn, wg) * nxt_ok
            dbuf[pl.ds(0, tm), lanes] = dd / _pool_counts(i * tm, tm, w)
            dbuf[pl.ds(tm, POOL_HALO), lanes] = ddn / _pool_counts((i + 1) * tm, POOL_HALO, w)
            acc = -dd
            for s in range(w):
                acc = acc + dbuf[pl.ds(s, tm), lanes]
            dz_ref[:, cols] = acc

    return pl.pallas_call(
        body, name="pool_bwd", grid=(nt,),
        in_specs=[pl.BlockSpec((tm, pw), lambda i: (i, 0)),
                  pl.BlockSpec((POOL_HALO, pw), lambda i: (jnp.minimum((i + 1) * hb, t // POOL_HALO - 1), 0)),
                  pl.BlockSpec((tm, pw), lambda i: (i, 1)),
                  pl.BlockSpec((POOL_HALO, pw), lambda i: (jnp.maximum(i * hb - 1, 0), 1)),
                  pl.BlockSpec(pool_w.shape, lambda i: (0, 0, 0)),
                  pl.BlockSpec((1, pw), lambda i: (0, 0))],
        out_specs=[pl.BlockSpec((tm, pw), lambda i: (i, 0)),
                   pl.BlockSpec((ng, POOL_CH, POOL_CH), lambda i: (0, 0, 0)),
                   pl.BlockSpec((1, pw), lambda i: (0, 0))],
        out_shape=[jax.ShapeDtypeStruct((t, pw), F32), jax.ShapeDtypeStruct((ng, POOL_CH, POOL_CH), F32),
                   jax.ShapeDtypeStruct((1, pw), F32)],
        scratch_shapes=[pltpu.VMEM((POOL_HALO + tm, pw), F32), pltpu.VMEM((tm + POOL_HALO, pw), F32)],
        compiler_params=_cparams(),
    )(dp, dp, z, z, pool_w, pool_scale)


def _mla_bwd(dq_h, dk_h, dv_h, z, dz_pool, h1, dh2, mix_norm, w_in, q_norm, wq_t, kv_norm, wkv, pos, rope_tab):
    t, d = h1.shape
    tm = min(512, t)

    def body(dqh_ref, dkh_ref, dvh_ref, z_ref, dzp_ref, h_ref, dh2_ref, gm_ref, win_ref, gq_ref, wq_ref, gkv_ref,
             wkv_ref, pos_ref, tab_ref, dh1_ref, dq_ref, dkv_ref, dz_ref, dgq_ref, dgkv_ref, dgm_ref):
        i = pl.program_id(0)
        first = i == 0
        cos_t, sin_t = _rope_tables(pos_ref[...], tab_ref[...])
        dq_parts, dkv_parts = [], []
        dk_pe = jnp.zeros((tm, 128), F32)
        for hh in range(MLA_HEADS):
            dqh = dqh_ref[hh]
            dq_parts += [dqh[:, 0:NOPE], _rope_apply_t(dqh[:, NOPE:HEAD_PAD], cos_t, sin_t)]
            dkh = dkh_ref[hh]
            dkv_parts += [dkh[:, 0:NOPE], dvh_ref[hh]]
            dk_pe = dk_pe + dkh[:, NOPE:HEAD_PAD]
        dqb = jnp.concatenate(dq_parts, axis=-1).astype(BF16)
        dkvb = jnp.concatenate(dkv_parts, axis=-1).astype(BF16)
        dq_ref[...] = dqb
        dkv_ref[...] = dkvb
        z = z_ref[...]
        c_q = z[:, 0:Q_RANK]
        gq = gq_ref[...]
        _, rq = _rms_fwd(c_q, gq)
        dcq, dgq = _rms_bwd(_dot_nn(dqb, wq_ref[...]), c_q, gq, rq)
        c_kv = z[:, Q_RANK:Q_RANK + KV_RANK]
        gkv = gkv_ref[...]
        _, rkv = _rms_fwd(c_kv, gkv)
        dckv, dgkv = _rms_bwd(_dot_nt(dkvb, wkv_ref[...]), c_kv, gkv, rkv)
        dkr = _rope_apply_t(dk_pe, cos_t, sin_t)
        dzb = jnp.concatenate([dcq, dckv, dkr, dzp_ref[...]], axis=-1).astype(BF16)
        dz_ref[...] = dzb
        x = h_ref[...]
        gm = gm_ref[...]
        _, rm = _rms_fwd(x, gm)
        dx, dgm = _rms_bwd(_dot_nt(dzb, win_ref[...]), x, gm, rm)
        dh1_ref[...] = dh2_ref[...] + dx
        _accumulate(dgq_ref, dgq, first)
        _accumulate(dgkv_ref, dgkv, first)
        _accumulate(dgm_ref, dgm, first)

    full = lambda shape: pl.BlockSpec(shape, lambda i: (0,) * len(shape))
    row = lambda w: pl.BlockSpec((tm, w), lambda i: (i, 0))
    head = lambda w: pl.BlockSpec((MLA_HEADS, tm, w), lambda i: (0, i, 0))
    pw = len(POOL_WINDOWS) * POOL_CH
    return pl.pallas_call(
        body, name="mla_bwd", grid=(t // tm,),
        in_specs=[head(HEAD_PAD), head(HEAD_PAD), head(V_DIM), row(d), row(pw), row(d), row(d),
                  _resident((1, d)), _resident(w_in.shape), _resident((1, Q_RANK)), _resident(wq_t.shape),
                  _resident((1, KV_RANK)), _resident(wkv.shape), row(1), _resident(rope_tab.shape)],
        out_specs=[row(d), row(d), row(d), row(d), full((1, Q_RANK)), full((1, KV_RANK)), full((1, d))],
        out_shape=[jax.ShapeDtypeStruct((t, d), F32), jax.ShapeDtypeStruct((t, d), BF16),
                   jax.ShapeDtypeStruct((t, d), BF16), jax.ShapeDtypeStruct((t, d), BF16),
                   jax.ShapeDtypeStruct((1, Q_RANK), F32), jax.ShapeDtypeStruct((1, KV_RANK), F32),
                   jax.ShapeDtypeStruct((1, d), F32)],
        compiler_params=_cparams(),
    )(dq_h, dk_h, dv_h, z, dz_pool, h1, dh2, mix_norm, w_in, q_norm, wq_t, kv_norm, wkv, pos, rope_tab)


def _mem_kv(mem, mem_norm, wmkv):
    n, d = mem.shape

    def body(mem_ref, g_ref, w_ref, memn_ref, k_ref, v_ref):
        y, _ = _rms_fwd(mem_ref[...], g_ref[...])
        yb = y.astype(BF16)
        memn_ref[...] = yb
        for hh in range(MEM_HEADS):
            k_ref[hh] = _dot_nn(yb, w_ref[hh]).astype(BF16)
            v_ref[hh] = _dot_nn(yb, w_ref[MEM_HEADS + hh]).astype(BF16)

    return pl.pallas_call(
        body, name="mem_kv",
        out_shape=[jax.ShapeDtypeStruct((n, d), BF16), jax.ShapeDtypeStruct((MEM_HEADS, n, MEM_HD), BF16),
                   jax.ShapeDtypeStruct((MEM_HEADS, n, MEM_HD), BF16)],
        compiler_params=_cparams(),
    )(mem, mem_norm, wmkv)


def _mem_softmax(qb, km):
    s = _dot_nt(qb, km) * MEM_SCALE
    e = jnp.exp(s - jnp.max(s, axis=-1, keepdims=True))
    return e / jnp.sum(e, axis=-1, keepdims=True)


def _xattn_fwd(h1, a, p, w_out, g, wmq, km, vm, wmo, token=None):
    t, d = h1.shape
    tm = min(512, t)
    half = a.shape[1]

    def body(h_ref, a_ref, p_ref, wo_ref, g_ref, wmq_ref, km_ref, vm_ref, wmo_ref,
             h2_ref, h3_ref, hn_ref, q_ref, o_ref):
        h2 = h_ref[...] + _dot_nn(a_ref[...], wo_ref[0:half, :]) + _dot_nn(p_ref[...], wo_ref[half:2 * half, :])
        h2_ref[...] = h2
        hn, _ = _rms_fwd(h2, g_ref[...])
        hnb = hn.astype(BF16)
        hn_ref[...] = hnb
        qb = _dot_nn(hnb, wmq_ref[...]).astype(BF16)
        q_ref[...] = qb
        outs = []
        for hh in range(MEM_HEADS):
            pr = _mem_softmax(qb[:, hh * MEM_HD:(hh + 1) * MEM_HD], km_ref[hh])
            outs.append(_dot_nn(pr.astype(BF16), vm_ref[hh]))
        ob = jnp.concatenate(outs, axis=-1).astype(BF16)
        o_ref[...] = ob
        h3_ref[...] = h2 + _dot_nn(ob, wmo_ref[...])

    full = lambda shape: pl.BlockSpec(shape, lambda i: (0,) * len(shape))
    row = lambda w: pl.BlockSpec((tm, w), lambda i: (i, 0))
    return _call_after(
        token, body,
        [row(d), row(half), row(half), _resident(w_out.shape), _resident((1, d)), _resident(wmq.shape),
         _resident(km.shape), _resident(vm.shape), _resident(wmo.shape)],
        (h1, a, p, w_out, g, wmq, km, vm, wmo),
        name="xattn_fwd", grid=(t // tm,),
        out_specs=[row(d), row(d), row(d), row(d), row(d)],
        out_shape=[jax.ShapeDtypeStruct((t, d), F32), jax.ShapeDtypeStruct((t, d), F32),
                   jax.ShapeDtypeStruct((t, d), BF16), jax.ShapeDtypeStruct((t, d), BF16),
                   jax.ShapeDtypeStruct((t, d), BF16)],
        compiler_params=_cparams(),
    )


def _xattn_bwd(dh3, h2, qm, g, wmq, km, vm, wmo, w_out, token=None):
    t, d = h2.shape
    tm = min(512, t)
    half = d // 2

    def body(dh3_ref, h2_ref, q_ref, g_ref, wmq_ref, km_ref, vm_ref, wmo_ref, wo_ref,
             dh2_ref, dq_ref, da_ref, dp_ref, dk_ref, dv_ref, dg_ref):
        i = pl.program_id(0)
        first = i == 0

        @pl.when(first)
        def _():
            dk_ref[...] = jnp.zeros_like(dk_ref)
            dv_ref[...] = jnp.zeros_like(dv_ref)

        dh3 = dh3_ref[...]
        dob = _dot_nt(dh3.astype(BF16), wmo_ref[...]).astype(BF16)
        qb = q_ref[...]
        dq_parts = []
        for hh in range(MEM_HEADS):
            cols = slice(hh * MEM_HD, (hh + 1) * MEM_HD)
            kk, vv = km_ref[hh], vm_ref[hh]
            pr = _mem_softmax(qb[:, cols], kk)
            doh = dob[:, cols]
            dv_ref[hh] += _dot_tn(pr.astype(BF16), doh)
            dpp = _dot_nt(doh, vv)
            dsb = (pr * (dpp - jnp.sum(dpp * pr, axis=-1, keepdims=True)) * MEM_SCALE).astype(BF16)
            dq_parts.append(_dot_nn(dsb, kk))
            dk_ref[hh] += _dot_tn(dsb, qb[:, cols])
        dqb = jnp.concatenate(dq_parts, axis=-1).astype(BF16)
        dq_ref[...] = dqb
        x = h2_ref[...]
        gg = g_ref[...]
        _, r = _rms_fwd(x, gg)
        dx, dg = _rms_bwd(_dot_nt(dqb, wmq_ref[...]), x, gg, r)
        dh2 = dh3 + dx
        dh2_ref[...] = dh2
        dap = _dot_nt(dh2.astype(BF16), wo_ref[...])
        da_ref[...] = dap[:, 0:half].astype(BF16)
        dp_ref[...] = dap[:, half:d].astype(BF16)
        _accumulate(dg_ref, dg, first)

    full = lambda shape: pl.BlockSpec(shape, lambda i: (0,) * len(shape))
    row = lambda w: pl.BlockSpec((tm, w), lambda i: (i, 0))
    return _call_after(
        token, body,
        [row(d), row(d), row(d), _resident((1, d)), _resident(wmq.shape), _resident(km.shape), _resident(vm.shape),
         _resident(wmo.shape), _resident(w_out.shape)],
        (dh3, h2, qm, g, wmq, km, vm, wmo, w_out),
        name="xattn_bwd", grid=(t // tm,),
        out_specs=[row(d), row(d), row(half), row(half), full(km.shape), full(vm.shape), full((1, d))],
        out_shape=[jax.ShapeDtypeStruct((t, d), F32), jax.ShapeDtypeStruct((t, d), BF16),
                   jax.ShapeDtypeStruct((t, half), BF16), jax.ShapeDtypeStruct((t, half), BF16),
                   jax.ShapeDtypeStruct(km.shape, F32), jax.ShapeDtypeStruct(vm.shape, F32),
                   jax.ShapeDtypeStruct((1, d), F32)],
        compiler_params=_cparams(),
    )


def _mem_kv_bwd(dkm, dvm, memn, mem, mem_norm, wmkv):
    n, d = mem.shape

    def body(dk_ref, dv_ref, memn_ref, mem_ref, g_ref, w_ref, dw_ref, dg_ref):
        memn = memn_ref[...]
        dmemn = jnp.zeros((n, d), F32)
        for s in range(2 * MEM_HEADS):
            src = dk_ref[s] if s < MEM_HEADS else dv_ref[s - MEM_HEADS]
            db = src.astype(BF16)
            dw_ref[s] = _dot_tn(memn, db)
            dmemn = dmemn + _dot_nt(db, w_ref[s])
        x = mem_ref[...]
        gg = g_ref[...]
        _, r = _rms_fwd(x, gg)
        _, dg = _rms_bwd(dmemn, x, gg, r)
        dg_ref[...] = dg

    return pl.pallas_call(
        body, name="mem_kv_bwd",
        out_shape=[jax.ShapeDtypeStruct(wmkv.shape, F32), jax.ShapeDtypeStruct((1, d), F32)],
        compiler_params=_cparams(),
    )(dkm, dvm, memn, mem, mem_norm, wmkv)


MESH_ID = pl.DeviceIdType.MESH
ANY = pl.BlockSpec(memory_space=pl.ANY)


def _coords():
    return lax.axis_index("x"), lax.axis_index("y"), lax.axis_index("c")


def _other_chips(x, y):
    return [(1 - x, y), (x, 1 - y), (1 - x, 1 - y)]


def _core_reduce(g, tag):
    _, r, w = g.shape

    def body(g_ref, part_ref, own_sc, recv_sc, send_sems, recv_sems, local_sems):
        x, y, c = _coords()
        sent, local = [], []
        for chip in range(4):
            sent.append(pltpu.make_async_remote_copy(
                src_ref=g_ref.at[2 * chip + (1 - c)], dst_ref=recv_sc.at[chip],
                send_sem=send_sems.at[chip], recv_sem=recv_sems.at[chip],
                device_id=(x, y, 1 - c), device_id_type=MESH_ID))
            local.append(pltpu.make_async_copy(g_ref.at[2 * chip + c], own_sc.at[chip], local_sems.at[chip]))
        for cp in sent + local:
            cp.start()
        for chip in range(4):
            local[chip].wait()
            sent[chip].wait_recv()
            part_ref[chip] = (own_sc[chip].astype(F32) + recv_sc[chip].astype(F32)).astype(part_ref.dtype)
        for cp in sent:
            cp.wait_send()

    return pl.pallas_call(
        body, name="core_reduce_" + tag,
        out_shape=jax.ShapeDtypeStruct((4, r, w), g.dtype),
        in_specs=[ANY], out_specs=pl.BlockSpec(memory_space=pltpu.VMEM),
        scratch_shapes=[pltpu.VMEM((4, r, w), g.dtype), pltpu.VMEM((4, r, w), g.dtype),
                        pltpu.SemaphoreType.DMA((4,)), pltpu.SemaphoreType.DMA((4,)), pltpu.SemaphoreType.DMA((4,))],
        compiler_params=_cparams(),
    )(g)


HBM_SPEC = pl.BlockSpec(memory_space=pltpu.HBM)
SEM_SPEC = pl.BlockSpec(memory_space=pltpu.SEMAPHORE)
SPLIT_EFFECT = pltpu.SideEffectType.DATAFLOW_SIDE_EFFECTING


def _ici_refs(gather, src_ref, land_ref, j, px, py, slot_chip, c):
    if gather:
        return src_ref, land_ref.at[:, 4 * slot_chip[0] + 2 * slot_chip[1] + c]
    return src_ref.at[2 * px + py], land_ref.at[j]


def _ici_start(src, after, name, gather):
    r, w = src.shape[-2:]
    land_shape = (src.shape[0], N_DEV, r, w) if gather else (3, r, w)

    def body(src_ref, land_ref, after_ref, send_sems, recv_sems, src_thru, land_thru, token):
        x, y, c = _coords()
        for j, (px, py) in enumerate(_other_chips(x, y)):
            s_ref, d_ref = _ici_refs(gather, src_ref, land_ref, j, px, py, (x, y), c)
            pltpu.make_async_remote_copy(
                src_ref=s_ref, dst_ref=d_ref, send_sem=send_sems.at[j], recv_sem=recv_sems.at[j],
                device_id=(px, py, c), device_id_type=MESH_ID).start()
        token[...] = jnp.zeros_like(token)

    return pl.pallas_call(
        body, name=name,
        out_shape=(pltpu.SemaphoreType.DMA((3,)), pltpu.SemaphoreType.DMA((3,)), pltpu.HBM(src.shape, src.dtype),
                   pltpu.HBM(land_shape, src.dtype), jax.ShapeDtypeStruct((8, 128), F32)),
        in_specs=(HBM_SPEC, HBM_SPEC, ANY),
        out_specs=(SEM_SPEC, SEM_SPEC, HBM_SPEC, HBM_SPEC, pl.BlockSpec(memory_space=pltpu.VMEM)),
        input_output_aliases={0: 2, 1: 3},
        compiler_params=pltpu.CompilerParams(has_side_effects=SPLIT_EFFECT),
    )(pltpu.with_memory_space_constraint(src, pltpu.HBM),
      pltpu.with_memory_space_constraint(lax.empty(land_shape, src.dtype), pltpu.HBM), after)


def _ici_wait(started, after, name, gather):
    send_sems, recv_sems, src_thru, land_thru, _ = started

    def body(src_ref, land_ref, send_sems, recv_sems, after_ref, src_dead, got_ref):
        x, y, c = _coords()
        for j, (px, py) in enumerate(_other_chips(x, y)):
            s_ref, d_ref = _ici_refs(gather, src_ref, land_ref, j, px, py, (px, py), c)
            copy = pltpu.make_async_remote_copy(
                src_ref=s_ref, dst_ref=d_ref, send_sem=send_sems.at[j], recv_sem=recv_sems.at[j],
                device_id=(px, py, c), device_id_type=MESH_ID)
            copy.wait_send()
            copy.wait_recv()

    return pl.pallas_call(
        body, name=name,
        out_shape=(pltpu.HBM(src_thru.shape, src_thru.dtype), pltpu.HBM(land_thru.shape, land_thru.dtype)),
        in_specs=(HBM_SPEC, HBM_SPEC, SEM_SPEC, SEM_SPEC, ANY),
        out_specs=(HBM_SPEC, HBM_SPEC), input_output_aliases={0: 0, 1: 1},
        compiler_params=pltpu.CompilerParams(has_side_effects=SPLIT_EFFECT),
    )(src_thru, land_thru, send_sems, recv_sems, after)


def _neighbour(k, x, y):
    return (1 - x, y) if k == 0 else (x, 1 - y)


def _slot(ref, px, py, c):
    return ref.at[:, 4 * px + 2 * py + c]


def _near_start(src, after, name):
    land_shape = (src.shape[0], N_DEV) + src.shape[1:]

    def body(src_ref, land_ref, after_ref, send_sems, recv_sems, src_thru, land_thru, token):
        x, y, c = _coords()
        for k in range(2):
            px, py = _neighbour(k, x, y)
            pltpu.make_async_remote_copy(
                src_ref=src_ref, dst_ref=_slot(land_ref, x, y, c), send_sem=send_sems.at[k],
                recv_sem=recv_sems.at[k], device_id=(px, py, c), device_id_type=MESH_ID).start()
        token[...] = jnp.zeros_like(token)

    return pl.pallas_call(
        body, name=name,
        out_shape=(pltpu.SemaphoreType.DMA((2,)), pltpu.SemaphoreType.DMA((2,)), pltpu.HBM(src.shape, src.dtype),
                   pltpu.HBM(land_shape, src.dtype), jax.ShapeDtypeStruct((8, 128), F32)),
        in_specs=(HBM_SPEC, HBM_SPEC, ANY),
        out_specs=(SEM_SPEC, SEM_SPEC, HBM_SPEC, HBM_SPEC, pl.BlockSpec(memory_space=pltpu.VMEM)),
        input_output_aliases={0: 2, 1: 3},
        compiler_params=pltpu.CompilerParams(has_side_effects=SPLIT_EFFECT),
    )(pltpu.with_memory_space_constraint(src, pltpu.HBM),
      pltpu.with_memory_space_constraint(lax.empty(land_shape, src.dtype), pltpu.HBM), after)


def _near_wait(started, after, name):
    send_sems, recv_sems, src_thru, land_thru, _ = started

    def body(src_ref, land_ref, send_sems, recv_sems, after_ref, src_dead, got_ref):
        x, y, c = _coords()
        for k in range(2):
            px, py = _neighbour(k, x, y)
            copy = pltpu.make_async_remote_copy(
                src_ref=src_ref, dst_ref=_slot(land_ref, px, py, c), send_sem=send_sems.at[k],
                recv_sem=recv_sems.at[k], device_id=(px, py, c), device_id_type=MESH_ID)
            copy.wait_send()
            copy.wait_recv()

    return pl.pallas_call(
        body, name=name,
        out_shape=(pltpu.HBM(src_thru.shape, src_thru.dtype), pltpu.HBM(land_thru.shape, land_thru.dtype)),
        in_specs=(HBM_SPEC, HBM_SPEC, SEM_SPEC, SEM_SPEC, ANY),
        out_specs=(HBM_SPEC, HBM_SPEC), input_output_aliases={0: 0, 1: 1},
        compiler_params=pltpu.CompilerParams(has_side_effects=SPLIT_EFFECT),
    )(src_thru, land_thru, send_sems, recv_sems, after)


def _far_refs(land_ref, k, x, y, c, arriving):
    half = land_ref.shape[2] // 2
    rows = pl.ds(k * half, half)
    ox, oy = (1 - x, 1 - y) if arriving else _neighbour(k, x, y)
    return land_ref.at[:, 4 * ox + 2 * oy + c, rows]


def _far_start(land, after, name):
    def body(land_ref, after_ref, send_sems, recv_sems, land_thru, token):
        x, y, c = _coords()
        for k in range(2):
            block = _far_refs(land_ref, k, x, y, c, False)
            px, py = _neighbour(1 - k, x, y)
            pltpu.make_async_remote_copy(
                src_ref=block, dst_ref=block, send_sem=send_sems.at[k], recv_sem=recv_sems.at[k],
                device_id=(px, py, c), device_id_type=MESH_ID).start()
        token[...] = jnp.zeros_like(token)

    return pl.pallas_call(
        body, name=name,
        out_shape=(pltpu.SemaphoreType.DMA((2,)), pltpu.SemaphoreType.DMA((2,)),
                   pltpu.HBM(land.shape, land.dtype), jax.ShapeDtypeStruct((8, 128), F32)),
        in_specs=(HBM_SPEC, ANY),
        out_specs=(SEM_SPEC, SEM_SPEC, HBM_SPEC, pl.BlockSpec(memory_space=pltpu.VMEM)),
        input_output_aliases={0: 2},
        compiler_params=pltpu.CompilerParams(has_side_effects=SPLIT_EFFECT),
    )(pltpu.with_memory_space_constraint(land, pltpu.HBM), after)


def _far_wait(started, after, name):
    send_sems, recv_sems, land_thru, _ = started

    def body(land_ref, send_sems, recv_sems, after_ref, got_ref):
        x, y, c = _coords()
        for k in range(2):
            px, py = _neighbour(1 - k, x, y)
            copy = pltpu.make_async_remote_copy(
                src_ref=_far_refs(land_ref, k, x, y, c, False), dst_ref=_far_refs(land_ref, k, x, y, c, True),
                send_sem=send_sems.at[k], recv_sem=recv_sems.at[k], device_id=(px, py, c), device_id_type=MESH_ID)
            copy.wait_send()
            copy.wait_recv()

    return pl.pallas_call(
        body, name=name,
        out_shape=pltpu.HBM(land_thru.shape, land_thru.dtype),
        in_specs=(HBM_SPEC, SEM_SPEC, SEM_SPEC, ANY),
        out_specs=HBM_SPEC, input_output_aliases={0: 0},
        compiler_params=pltpu.CompilerParams(has_side_effects=SPLIT_EFFECT),
    )(land_thru, send_sems, recv_sems, after)


def _peer(k, x, y, c):
    return x ^ ((k >> 2) & 1), y ^ ((k >> 1) & 1), c ^ (k & 1)


def _peers_start(src, after, name):
    r, w = src.shape
    x, y, c = _coords()
    land = lax.dynamic_update_slice(jnp.zeros((N_DEV, r, w), src.dtype), src[None], (4 * x + 2 * y + c, 0, 0))

    def body(src_ref, land_ref, after_ref, send_sems, recv_sems, src_thru, land_thru, token):
        x, y, c = _coords()
        for k in range(1, N_DEV):
            pltpu.make_async_remote_copy(
                src_ref=src_ref, dst_ref=land_ref.at[4 * x + 2 * y + c],
                send_sem=send_sems.at[k - 1], recv_sem=recv_sems.at[k - 1],
                device_id=_peer(k, x, y, c), device_id_type=MESH_ID).start()
        token[...] = jnp.zeros_like(token)

    return pl.pallas_call(
        body, name=name,
        out_shape=(pltpu.SemaphoreType.DMA((N_DEV - 1,)), pltpu.SemaphoreType.DMA((N_DEV - 1,)),
                   pltpu.HBM(src.shape, src.dtype), pltpu.HBM(land.shape, src.dtype),
                   jax.ShapeDtypeStruct((8, 128), F32)),
        in_specs=(HBM_SPEC, HBM_SPEC, ANY),
        out_specs=(SEM_SPEC, SEM_SPEC, HBM_SPEC, HBM_SPEC, pl.BlockSpec(memory_space=pltpu.VMEM)),
        input_output_aliases={0: 2, 1: 3},
        compiler_params=pltpu.CompilerParams(has_side_effects=SPLIT_EFFECT),
    )(pltpu.with_memory_space_constraint(src, pltpu.HBM), pltpu.with_memory_space_constraint(land, pltpu.HBM), after)


def _peers_wait(started, after, name):
    send_sems, recv_sems, src_thru, land_thru, _ = started

    def body(src_ref, land_ref, send_sems, recv_sems, after_ref, src_dead, got_ref):
        x, y, c = _coords()
        for k in range(1, N_DEV):
            px, py, pc = _peer(k, x, y, c)
            copy = pltpu.make_async_remote_copy(
                src_ref=src_ref, dst_ref=land_ref.at[4 * px + 2 * py + pc],
                send_sem=send_sems.at[k - 1], recv_sem=recv_sems.at[k - 1],
                device_id=(px, py, pc), device_id_type=MESH_ID)
            copy.wait_send()
            copy.wait_recv()

    return pl.pallas_call(
        body, name=name,
        out_shape=(pltpu.HBM(src_thru.shape, src_thru.dtype), pltpu.HBM(land_thru.shape, land_thru.dtype)),
        in_specs=(HBM_SPEC, HBM_SPEC, SEM_SPEC, SEM_SPEC, ANY),
        out_specs=(HBM_SPEC, HBM_SPEC), input_output_aliases={0: 0, 1: 1},
        compiler_params=pltpu.CompilerParams(has_side_effects=SPLIT_EFFECT),
    )(src_thru, land_thru, send_sems, recv_sems, after)


def _share_refs(ref, k, x, y, c, sender_c):
    px, py = ([(x, y)] + _other_chips(x, y))[k]
    return ref.at[:, 4 * px + 2 * py + sender_c]


def _share_start(gathered, after, name):
    def body(g_ref, after_ref, send_sems, recv_sems, g_thru, token):
        x, y, c = _coords()
        for k in range(4):
            slot = _share_refs(g_ref, k, x, y, c, c)
            pltpu.make_async_remote_copy(
                src_ref=slot, dst_ref=slot, send_sem=send_sems.at[k], recv_sem=recv_sems.at[k],
                device_id=(x, y, 1 - c), device_id_type=MESH_ID).start()
        token[...] = jnp.zeros_like(token)

    return pl.pallas_call(
        body, name=name,
        out_shape=(pltpu.SemaphoreType.DMA((4,)), pltpu.SemaphoreType.DMA((4,)),
                   pltpu.HBM(gathered.shape, gathered.dtype), jax.ShapeDtypeStruct((8, 128), F32)),
        in_specs=(HBM_SPEC, ANY),
        out_specs=(SEM_SPEC, SEM_SPEC, HBM_SPEC, pl.BlockSpec(memory_space=pltpu.VMEM)),
        input_output_aliases={0: 2},
        compiler_params=pltpu.CompilerParams(has_side_effects=SPLIT_EFFECT),
    )(pltpu.with_memory_space_constraint(gathered, pltpu.HBM), after)


def _share_wait(started, after, name):
    send_sems, recv_sems, g_thru, _ = started

    def body(g_ref, send_sems, recv_sems, after_ref, got_ref):
        x, y, c = _coords()
        for k in range(4):
            copy = pltpu.make_async_remote_copy(
                src_ref=_share_refs(g_ref, k, x, y, c, c), dst_ref=_share_refs(g_ref, k, x, y, c, 1 - c),
                send_sem=send_sems.at[k], recv_sem=recv_sems.at[k],
                device_id=(x, y, 1 - c), device_id_type=MESH_ID)
            copy.wait_send()
            copy.wait_recv()

    return pl.pallas_call(
        body, name=name,
        out_shape=pltpu.HBM(g_thru.shape, g_thru.dtype),
        in_specs=(HBM_SPEC, SEM_SPEC, SEM_SPEC, ANY),
        out_specs=HBM_SPEC, input_output_aliases={0: 0},
        compiler_params=pltpu.CompilerParams(has_side_effects=SPLIT_EFFECT),
    )(g_thru, send_sems, recv_sems, after)


def _core_share(own, gathered, name):
    def body(own_ref, gin_ref, out_ref, stage, send_sems, recv_sems, local_sem):
        x, y, c = _coords()
        sibling = (x, y, 1 - c)
        chips = [(x, y)] + _other_chips(x, y)
        stage_in = pltpu.make_async_copy(own_ref, stage, local_sem)
        stage_in.start()
        sent, arriving = [], []
        for k, (px, py) in enumerate(chips):
            slot = out_ref.at[:, 4 * px + 2 * py + c]
            sent.append(pltpu.make_async_remote_copy(
                src_ref=own_ref if k == 0 else slot, dst_ref=slot,
                send_sem=send_sems.at[k], recv_sem=recv_sems.at[k], device_id=sibling, device_id_type=MESH_ID))
            arriving.append(pltpu.make_async_remote_copy(
                src_ref=own_ref, dst_ref=out_ref.at[:, 4 * px + 2 * py + (1 - c)],
                send_sem=send_sems.at[k], recv_sem=recv_sems.at[k], device_id=sibling, device_id_type=MESH_ID))
        for cp in sent:
            cp.start()
        stage_in.wait()
        stage_out = pltpu.make_async_copy(stage, out_ref.at[:, 4 * x + 2 * y + c], local_sem)
        stage_out.start()
        for cp in arriving:
            cp.wait_recv()
        for cp in sent:
            cp.wait_send()
        stage_out.wait()

    return pl.pallas_call(
        body, name=name,
        out_shape=jax.ShapeDtypeStruct(gathered.shape, own.dtype),
        in_specs=[ANY, ANY], out_specs=ANY, input_output_aliases={1: 0},
        scratch_shapes=[pltpu.VMEM(own.shape, own.dtype), pltpu.SemaphoreType.DMA((4,)),
                        pltpu.SemaphoreType.DMA((4,)), pltpu.SemaphoreType.DMA],
    )(own, gathered)


def _adamw(w, g, m, v):
    m = ADAM_B1 * m + (1.0 - ADAM_B1) * g
    v = ADAM_B2 * v + (1.0 - ADAM_B2) * (g * g)
    m_hat = m / ADAM_C1
    v_hat = v / ADAM_C2
    delta = -ADAM_LR * (m_hat / (jnp.sqrt(v_hat) + ADAM_EPS) + ADAM_WD * w)
    return delta, m, v


def _adam_big(units, chip_idx, tag, token):
    n = len(units)
    r, wd = units[0][2].shape
    tr, tw = _row_tile(r, 1024), 256

    def body(s_ref, tok_ref, *refs):
        for u in range(n):
            p_ref, l_ref, w_ref, m_ref, v_ref = refs[5 * u:5 * u + 5]
            g_ref, d_ref, mo_ref, vo_ref = refs[5 * n + 4 * u:5 * n + 4 * u + 4]
            g = p_ref[0].astype(F32)
            for j in range(3):
                g = g + l_ref[j].astype(F32)
            delta, mn, vn = _adamw(w_ref[...], g, m_ref[...], v_ref[...])
            g_ref[...] = g
            d_ref[...] = delta
            mo_ref[...] = mn
            vo_ref[...] = vn

    row = pl.BlockSpec((tr, tw), lambda i, j, s: (i, j))
    unit_specs = [pl.BlockSpec((1, tr, tw), lambda i, j, s: (s[0], i, j)),
                  pl.BlockSpec((3, tr, tw), lambda i, j, s: (0, i, j)), row, row, row]
    outs = pl.pallas_call(
        body, name="adam_big_" + tag,
        grid_spec=pltpu.PrefetchScalarGridSpec(
            num_scalar_prefetch=1, grid=(r // tr, wd // tw),
            in_specs=[pl.BlockSpec((8, 128), lambda i, j, s: (0, 0))] + unit_specs * n,
            out_specs=[row] * (4 * n)),
        out_shape=[jax.ShapeDtypeStruct((r, wd), F32)] * (4 * n),
        compiler_params=_cparams(),
    )(chip_idx, token, *[a for unit in units for a in unit])
    return [outs[4 * u:4 * u + 4] for u in range(n)]


def _adam_small(parts, w, m, v):
    _, r, wd = parts.shape

    def body(p_ref, w_ref, m_ref, v_ref, g_ref, d_ref, mo_ref, vo_ref):
        g = p_ref[0]
        for k in range(1, N_DEV):
            g = g + p_ref[k]
        delta, mn, vn = _adamw(w_ref[...], g, m_ref[...], v_ref[...])
        g_ref[...] = g
        d_ref[...] = delta
        mo_ref[...] = mn
        vo_ref[...] = vn

    return pl.pallas_call(
        body, name="adam_small",
        out_shape=[jax.ShapeDtypeStruct((r, wd), F32)] * 4,
        compiler_params=_cparams(),
    )(parts, w, m, v)


def _pad_rows(a, rows):
    return jnp.pad(a, ((0, rows - a.shape[0]), (0, 0)))


def _pad_w_in(w):
    cut = Q_RANK + KV_RANK + ROPE
    return jnp.concatenate([w[:, :cut], jnp.zeros((w.shape[0], 64), w.dtype), w[:, cut:]], axis=1)


def _unpad_w_in(w):
    cut = Q_RANK + KV_RANK + ROPE
    return jnp.concatenate([w[:, :cut], w[:, cut + 64:]], axis=1)


def _pack_mid(p):
    parts = [_pad_w_in(p["w_in"][0]), p["w_out"][0], p["w_mq"][0], p["w_mo"][0],
             p["w_mkv"][0].reshape(256, D_MODEL),
             _pad_rows(p["w_q_up"][0].T.reshape(24, D_MODEL), 32),
             p["w_kv_up"][0].reshape(16, D_MODEL)]
    return jnp.concatenate(parts, axis=0)


def _pack_ffn(w_gate, w_up, w_down, name):
    d, rows = w_gate.shape[1:]

    def body(g_ref, u_ref, d_ref, o_ref):
        eye = (lax.broadcasted_iota(jnp.int32, (d, d), 0) == lax.broadcasted_iota(jnp.int32, (d, d), 1)).astype(BF16)
        o_ref[0] = _dot_tn(g_ref[0].astype(BF16), eye).astype(BF16)
        o_ref[1] = _dot_tn(u_ref[0].astype(BF16), eye).astype(BF16)
        o_ref[2] = d_ref[0].astype(BF16)

    return pl.pallas_call(
        body, name=name, out_shape=jax.ShapeDtypeStruct((3, rows, d), BF16), compiler_params=_cparams(),
    )(w_gate, w_up, w_down)


def _pack_segments(p, group):
    if group == "mid":
        return _pack_mid(p)[None].astype(BF16)
    return _pack_ffn(p[group + "_w_gate"], p[group + "_w_up"], p[group + "_w_down"], "pack_" + group)


UNIT_WEIGHT = {"ffn1_g": ("ffn1_w_gate", True), "ffn1_u": ("ffn1_w_up", True), "ffn1_d": ("ffn1_w_down", False),
               "ffn2_g": ("ffn2_w_gate", True), "ffn2_u": ("ffn2_w_up", True), "ffn2_d": ("ffn2_w_down", False)}


def _pack_unit(p, unit):
    if unit == "mid":
        return _pack_mid(p)
    name, transposed = UNIT_WEIGHT[unit]
    return p[name][0].T if transposed else p[name][0]


def _unpack_unit(a, unit):
    if unit != "mid":
        name, transposed = UNIT_WEIGHT[unit]
        return {name: (a.T if transposed else a)[None]}
    seg = lambda n: a[SEG_OFF[n][0]:SEG_OFF[n][0] + SEG_OFF[n][1]]
    return {"w_in": _unpad_w_in(seg("w_in"))[None], "w_out": seg("w_out")[None], "w_mq": seg("w_mq")[None],
            "w_mo": seg("w_mo")[None], "w_mkv": seg("w_mkv").reshape(D_MODEL, 256)[None],
            "w_q_up": seg("w_q")[:24].reshape(96, Q_RANK).T[None],
            "w_kv_up": seg("w_kv").reshape(KV_RANK, 128)[None]}


def _unpack_gathered(full, group):
    if group != "mid":
        return {group: full.reshape(len(GROUP_SEGS[group]), -1, D_MODEL)}
    full = full[0]
    seg = lambda n: full[:, SEG_OFF[n][0]:SEG_OFF[n][0] + SEG_OFF[n][1]]
    rows = lambda n: seg(n).reshape(-1, D_MODEL)
    wq_t = seg("w_q")[:, :24].reshape(MLA_HEADS, NOPE + ROPE, Q_RANK)
    wq_t = jnp.pad(wq_t, ((0, 0), (0, HEAD_PAD - NOPE - ROPE), (0, 0))).reshape(MLA_HEADS * HEAD_PAD, Q_RANK)
    wkv = seg("w_kv").reshape(N_DEV, KV_RANK, 128).transpose(1, 0, 2).reshape(KV_RANK, N_DEV * 128)
    return {"w_in": rows("w_in"), "w_out": rows("w_out"), "w_mq": rows("w_mq"), "w_mo": rows("w_mo"),
            "w_mkv": seg("w_mkv").reshape(N_DEV, D_MODEL, 256), "w_q": wq_t, "w_kv": wkv}


def _pack_grads(gr):
    blk = lambda a: a.reshape(N_DEV, -1, D_MODEL)
    dwq = gr["w_q"].reshape(MLA_HEADS, HEAD_PAD, Q_RANK)[:, :NOPE + ROPE].reshape(N_DEV, 24, D_MODEL)
    dwq = jnp.pad(dwq, ((0, 0), (0, 8), (0, 0)))
    dwkv = gr["w_kv"].reshape(KV_RANK, N_DEV, 128).transpose(1, 0, 2).reshape(N_DEV, 16, D_MODEL)
    parts = [blk(gr["w_in"]), blk(gr["w_out"]), blk(gr["w_mq"]), blk(gr["w_mo"]),
             gr["w_mkv"].reshape(N_DEV, 256, D_MODEL), dwq, dwkv]
    return jnp.concatenate([a.astype(BF16) for a in parts], axis=1)


def _pack_small(vals):
    parts = []
    for n, r in SMALL_ROWS:
        parts.append(_pad_rows(vals[n].reshape(-1, 128), r) if n in vals else jnp.zeros((r, 128), F32))
    return jnp.concatenate(parts, axis=0)


def _unpack_small(a, shapes):
    out = {}
    for n, shape in shapes.items():
        o = SMALL_OFF[n][0]
        out[n] = a[o:o + int(np.prod(shape)) // 128].reshape(shape)
    return out


BIG_NAMES = ("ffn1_w_gate", "ffn1_w_up", "ffn1_w_down", "w_in", "w_q_up", "w_kv_up", "w_out", "w_mq", "w_mkv",
             "w_mo", "ffn2_w_gate", "ffn2_w_up", "ffn2_w_down")
SMALL_NAMES = ("ffn1_norm", "mix_norm", "q_norm", "kv_norm", "pool_w", "pool_scale", "xattn_norm", "mem_norm",
               "ffn2_norm", "final_norm")
WEIGHT_ORDER = ("ffn1_norm", "ffn1_w_gate", "ffn1_w_up", "ffn1_w_down", "mix_norm", "w_in", "q_norm", "w_q_up",
                "kv_norm", "w_kv_up", "pool_w", "pool_scale", "w_out", "xattn_norm", "mem_norm", "w_mq", "w_mkv",
                "w_mo", "ffn2_norm", "ffn2_w_gate", "ffn2_w_up", "ffn2_w_down", "final_norm")


def _rope_table():
    lane = np.arange(128)
    freqs = (1.0 / (ROPE_BASE ** (np.arange(0, ROPE, 2, dtype=np.float32) / ROPE))).astype(np.float32)
    tab = np.zeros((8, 128), np.float32)
    tab[0] = np.where(lane < ROPE, freqs[lane % (ROPE // 2)], 0.0)
    tab[1] = np.where(lane < ROPE // 2, -1.0, np.where(lane < ROPE, 1.0, 0.0))
    return jnp.asarray(tab)


def kernel(x, mem, positions, ffn1_norm, ffn1_w_gate, ffn1_w_up, ffn1_w_down, mix_norm, w_in, q_norm, w_q_up, kv_norm, w_kv_up, pool_w, pool_scale, w_out, xattn_norm, mem_norm, w_mq, w_mkv, w_mo, ffn2_norm, ffn2_w_gate, ffn2_w_up, ffn2_w_down, final_norm, loss_target, m_ffn1_norm, m_ffn1_w_gate, m_ffn1_w_up, m_ffn1_w_down, m_mix_norm, m_w_in, m_q_norm, m_w_q_up, m_kv_norm, m_w_kv_up, m_pool_w, m_pool_scale, m_w_out, m_xattn_norm, m_mem_norm, m_w_mq, m_w_mkv, m_w_mo, m_ffn2_norm, m_ffn2_w_gate, m_ffn2_w_up, m_ffn2_w_down, m_final_norm, v_ffn1_norm, v_ffn1_w_gate, v_ffn1_w_up, v_ffn1_w_down, v_mix_norm, v_w_in, v_q_norm, v_w_q_up, v_kv_norm, v_w_kv_up, v_pool_w, v_pool_scale, v_w_out, v_xattn_norm, v_mem_norm, v_w_mq, v_w_mkv, v_w_mo, v_ffn2_norm, v_ffn2_w_gate, v_ffn2_w_up, v_ffn2_w_down, v_final_norm):
    wts = dict(ffn1_norm=ffn1_norm, ffn1_w_gate=ffn1_w_gate, ffn1_w_up=ffn1_w_up, ffn1_w_down=ffn1_w_down,
               mix_norm=mix_norm, w_in=w_in, q_norm=q_norm, w_q_up=w_q_up, kv_norm=kv_norm, w_kv_up=w_kv_up,
               pool_w=pool_w, pool_scale=pool_scale, w_out=w_out, xattn_norm=xattn_norm, mem_norm=mem_norm,
               w_mq=w_mq, w_mkv=w_mkv, w_mo=w_mo, ffn2_norm=ffn2_norm, ffn2_w_gate=ffn2_w_gate,
               ffn2_w_up=ffn2_w_up, ffn2_w_down=ffn2_w_down, final_norm=final_norm)
    mom = dict(ffn1_norm=m_ffn1_norm, ffn1_w_gate=m_ffn1_w_gate, ffn1_w_up=m_ffn1_w_up, ffn1_w_down=m_ffn1_w_down,
               mix_norm=m_mix_norm, w_in=m_w_in, q_norm=m_q_norm, w_q_up=m_w_q_up, kv_norm=m_kv_norm,
               w_kv_up=m_w_kv_up, pool_w=m_pool_w, pool_scale=m_pool_scale, w_out=m_w_out, xattn_norm=m_xattn_norm,
               mem_norm=m_mem_norm, w_mq=m_w_mq, w_mkv=m_w_mkv, w_mo=m_w_mo, ffn2_norm=m_ffn2_norm,
               ffn2_w_gate=m_ffn2_w_gate, ffn2_w_up=m_ffn2_w_up, ffn2_w_down=m_ffn2_w_down, final_norm=m_final_norm)
    var = dict(ffn1_norm=v_ffn1_norm, ffn1_w_gate=v_ffn1_w_gate, ffn1_w_up=v_ffn1_w_up, ffn1_w_down=v_ffn1_w_down,
               mix_norm=v_mix_norm, w_in=v_w_in, q_norm=v_q_norm, w_q_up=v_w_q_up, kv_norm=v_kv_norm,
               w_kv_up=v_w_kv_up, pool_w=v_pool_w, pool_scale=v_pool_scale, w_out=v_w_out, xattn_norm=v_xattn_norm,
               mem_norm=v_mem_norm, w_mq=v_w_mq, w_mkv=v_w_mkv, w_mo=v_w_mo, ffn2_norm=v_ffn2_norm,
               ffn2_w_gate=v_ffn2_w_gate, ffn2_w_up=v_ffn2_w_up, ffn2_w_down=v_ffn2_w_down, final_norm=v_final_norm)

    t = x.shape[1]
    xs = x[0]
    mems = mem[0]
    target = loss_target[0]
    pos = positions.reshape(t, 1)
    row = lambda a: a.reshape(1, -1)
    rope_tab = _rope_table()

    cx, cy, cc = _coords()
    chip_idx = (2 * cx + cy).astype(jnp.int32).reshape(1)

    wb = {}
    for grp in ("ffn1", "mid", "ffn2"):
        wb[grp] = _pack_segments(wts, grp)
        if grp == "ffn1":
            near_ffn1 = _near_start(wb["ffn1"], pos, "ag_ffn1_near_start")
    mid_names = ("w_in", "w_out", "w_mq", "w_mo", "w_mkv", "w_q_up", "w_kv_up")

    def packed_during(token, units, after):
        one = 1.0 + token[0, 0]
        packs = {}
        for u in units:
            names = SMALL_NAMES if u == "small" else mid_names if u == "mid" else UNIT_WEIGHT[u][:1]
            held = [{n: p[n] * one for n in names} for p in (wts, mom, var)]
            packs[u] = tuple(_pack_small(h) if u == "small" else _pack_unit(h, u) for h in held)
        return lax.optimization_barrier((after, packs))

    after, adam_in = packed_during(near_ffn1[4], ("ffn1_g", "ffn1_u", "ffn1_d", "ffn2_g", "ffn2_u", "ffn2_d", "mid"),
                                   wb["ffn2"])
    own_ffn1, land_ffn1 = _near_wait(near_ffn1, after, "ag_ffn1_near_wait")
    far_ffn1 = _far_start(land_ffn1, own_ffn1, "ag_ffn1_far_start")
    after, more = packed_during(far_ffn1[3], ("small",), wb["mid"])
    adam_in.update(more)
    land_ffn1 = _far_wait(far_ffn1, after, "ag_ffn1_far_wait")
    full_ffn1 = _core_share(own_ffn1, land_ffn1, "ag_ffn1_share")
    fw = _unpack_gathered(full_ffn1, "ffn1")
    ag_mid = _ici_start(wb["mid"], full_ffn1, "ag_mid_start", True)
    g_ffn1, g_mix, g_q, g_kv = row(ffn1_norm), row(mix_norm), row(q_norm), row(kv_norm)
    g_x, g_mem, g_ffn2, g_fin = row(xattn_norm), row(mem_norm), row(ffn2_norm), row(final_norm)
    pool_wb = pool_w[0].astype(BF16)
    pool_sc = row(pool_scale)

    h1, n1, gate1, up1 = _ffn_fwd(xs, g_ffn1, fw["ffn1"], "ffn1_fwd", token=ag_mid[4])
    own_mid, land_mid = _ici_wait(ag_mid, h1, "ag_mid_wait", True)
    full_mid = _core_share(own_mid, land_mid, "ag_mid_share")
    fw.update(_unpack_gathered(full_mid, "mid"))
    ag_ffn2 = _ici_start(wb["ffn2"], full_mid, "ag_ffn2_start", True)
    u, z, qn, kvn, qh, kh, vh = _mix_prep(h1, g_mix, fw["w_in"], g_q, fw["w_q"], g_kv, fw["w_kv"], pos, rope_tab,
                                          token=ag_ffn2[4])
    a, lse = _attn_fwd(qh, kh, vh)
    p = _pool_fwd(z, pool_wb, pool_sc)
    memn, km, vm = _mem_kv(mems, g_mem, fw["w_mkv"])
    own_ffn2, land_ffn2 = _ici_wait(ag_ffn2, a, "ag_ffn2_wait", True)
    land_ffn2 = lax.dynamic_update_slice(land_ffn2, own_ffn2[:, None], (0, 4 * cx + 2 * cy + cc, 0, 0))
    share_ffn2 = _share_start(land_ffn2, a, "ag_ffn2_share_start")
    h2, h3, hn, qm, om = _xattn_fwd(h1, a, p, fw["w_out"], g_x, fw["w_mq"], km, vm, fw["w_mo"], token=share_ffn2[3])
    fw.update(_unpack_gathered(_share_wait(share_ffn2, h3, "ag_ffn2_share_wait"), "ffn2"))
    dh4, n2, gate2, up2, loss_part, dg_fin = _ffn_fwd(h3, g_ffn2, fw["ffn2"],
                                                      "ffn2_fwd", head=(target, g_fin))

    def reduce_start(g8, unit):
        part = _core_reduce(g8, unit)
        return _ici_start(part, g8, "rs_" + unit + "_start", False)

    def by_device(g):
        return g.reshape(N_DEV, -1, D_MODEL)

    rs = {}
    dh3, dgate2, dup2, act2, dg_ffn2 = _ffn_bwd_data(dh4, h3, g_ffn2, gate2, up2, fw["ffn2"], "ffn2_bwd")
    rs["ffn2_g"] = reduce_start(by_device(_tn_matmul(dgate2, n2, "ffn2_dwg", tmm=1408, m=D_FF, out_dtype=BF16)), "ffn2_g")
    rs["ffn2_u"] = reduce_start(by_device(_tn_matmul(dup2, n2, "ffn2_dwu", tmm=1408, m=D_FF, out_dtype=BF16,
                                                     token=rs["ffn2_g"][4])), "ffn2_u")
    rs["ffn2_d"] = reduce_start(by_device(_tn_matmul(act2, dh4, "ffn2_dwd", scale=0.5, tmm=1408, m=D_FF, out_dtype=BF16,
                                                     token=rs["ffn2_u"][4])), "ffn2_d")
    dh2, dqm, da, dp, dkm, dvm, dg_x = _xattn_bwd(dh3, h2, qm, g_x, fw["w_mq"], km, vm, fw["w_mo"], fw["w_out"],
                                                  token=rs["ffn2_d"][4])
    gr = {}
    gr["w_mo"] = _tn_matmul(om, dh3, "dw_mo", out_dtype=BF16)
    gr["w_mq"] = _tn_matmul(hn, dqm, "dw_mq", out_dtype=BF16)
    gr["w_out"] = jnp.concatenate([_tn_matmul(a, dh2, "dw_out_a", out_dtype=BF16),
                                   _tn_matmul(p, dh2, "dw_out_p", out_dtype=BF16)], axis=0)
    gr["w_mkv"], dg_mem = _mem_kv_bwd(dkm, dvm, memn, mems, g_mem, fw["w_mkv"])
    dz_pool, d_pool_w, d_pool_sc = _pool_bwd(dp, z, pool_wb, pool_sc)
    dqh, dkh, dvh = _attn_bwd(qh, kh, vh, da, lse, _attn_delta(a, da))
    dh1, dq, dkv, dz, dg_q, dg_kv, dg_mix = _mla_bwd(dqh, dkh, dvh, z, dz_pool, h1, dh2, g_mix, fw["w_in"], g_q,
                                                     fw["w_q"], g_kv, fw["w_kv"], pos, rope_tab)
    gr["w_q"] = _tn_matmul(dq, qn, "dw_q", out_dtype=BF16)
    gr["w_kv"] = _tn_matmul(kvn, dkv, "dw_kv", out_dtype=BF16)
    gr["w_in"] = _tn_matmul(u, dz, "dw_in", out_dtype=BF16)
    g_mid = _pack_grads(gr)
    part_mid = _core_reduce(g_mid, "mid")
    got = {}
    after = part_mid
    for unit in ("ffn2_g", "ffn2_u", "ffn2_d"):
        got[unit] = _ici_wait(rs[unit], after, "rs_" + unit + "_wait", False)
        after = got[unit][1]
    rs["mid"] = _ici_start(part_mid, after, "rs_mid_start", False)
    dx, dgate1, dup1, act1, dg_ffn1 = _ffn_bwd_data(dh1, xs, g_ffn1, gate1, up1, fw["ffn1"], "ffn1_bwd", token=rs["mid"][4])
    got["mid"] = _ici_wait(rs["mid"], dx, "rs_mid_wait", False)

    small_g = dict(ffn1_norm=dg_ffn1, mix_norm=dg_mix, q_norm=dg_q, kv_norm=dg_kv, pool_w=d_pool_w,
                   pool_scale=d_pool_sc, xattn_norm=dg_x, mem_norm=dg_mem, ffn2_norm=dg_ffn2, final_norm=dg_fin,
                   loss=loss_part)
    small_ag = _peers_start(_pack_small(small_g), got["mid"][1], "small_ag_start")
    rs["ffn1_g"] = reduce_start(by_device(_tn_matmul(dgate1, n1, "ffn1_dwg", tmm=1408, m=D_FF, out_dtype=BF16,
                                                     token=small_ag[4])), "ffn1_g")
    _, parts = _peers_wait(small_ag, rs["ffn1_g"][4], "small_ag_wait")
    small = _adam_small(parts, *adam_in["small"])
    small_sum = small[0]
    loss = small_sum[SMALL_OFF["loss"][0], 0]
    shapes = {n: wts[n].shape for n in SMALL_NAMES}
    small = [_unpack_small(s, shapes) for s in small]

    rs["ffn1_u"] = reduce_start(by_device(_tn_matmul(dup1, n1, "ffn1_dwu", tmm=1408, m=D_FF, out_dtype=BF16,
                                                     token=small_sum)), "ffn1_u")
    rs["ffn1_d"] = reduce_start(by_device(_tn_matmul(act1, dh1, "ffn1_dwd", scale=0.5, tmm=1408, m=D_FF, out_dtype=BF16,
                                                     token=rs["ffn1_u"][4])), "ffn1_d")

    big = {}

    def adam_units(names, token):
        units = [got[u] + adam_in[u] for u in names]
        res = _adam_big(units, chip_idx, "_".join(names), token)
        for u, four in zip(names, res):
            for k, packed in enumerate(four):
                big.setdefault(k, {}).update(_unpack_unit(packed, u))
        return res[-1][0]

    done = adam_units(["mid"], rs["ffn1_d"][4])
    done = adam_units(["ffn2_g", "ffn2_u", "ffn2_d"], done)
    got["ffn1_g"] = _ici_wait(rs["ffn1_g"], done, "rs_ffn1_g_wait", False)
    got["ffn1_u"] = _ici_wait(rs["ffn1_u"], got["ffn1_g"][1], "rs_ffn1_u_wait", False)
    done = adam_units(["ffn1_g", "ffn1_u"], done)
    got["ffn1_d"] = _ici_wait(rs["ffn1_d"], done, "rs_ffn1_d_wait", False)
    adam_units(["ffn1_d"], done)

    outs = [loss, dx[None]]
    for k in range(4):
        for n in WEIGHT_ORDER:
            outs.append(big[k][n] if n in BIG_NAMES else small[k][n])
    return tuple(outs)
```

```python
import numpy as np

import jax
import jax.numpy as jnp
from jax import lax
from jax.experimental import pallas as pl
from jax.experimental.pallas import tpu as pltpu

F32 = jnp.float32
BF16 = jnp.bfloat16

N_DEV = 8
D_MODEL = 1024
D_FF = 2816
MLA_HEADS = 4
NOPE = 128
ROPE = 64
HEAD_PAD = 256
V_DIM = 128
Q_RANK = 256
KV_RANK = 128
POOL_WINDOWS = (2, 4, 8, 16)
POOL_CH = 128
POOL_HALO = 16
N_MEM = 256
MEM_HEADS = 4
MEM_HD = 256
ROPE_BASE = 10000.0
RMS_EPS = 1e-6
ATTN_SCALE = (NOPE + ROPE) ** -0.5
MEM_SCALE = MEM_HD ** -0.5
NEG_BIG = -1e30

ADAM_LR = 0.001
ADAM_B1 = 0.9
ADAM_B2 = 0.999
ADAM_EPS = 1e-08
ADAM_WD = 0.01
ADAM_STEP = 10
ADAM_C1 = 1.0 - ADAM_B1 ** ADAM_STEP
ADAM_C2 = 1.0 - ADAM_B2 ** ADAM_STEP

VMEM_LIMIT_BYTES = 56 * 1024 * 1024
BF16_ROWS = 16

GROUP_SEGS = {
    "ffn1": (("ffn1_g", 352), ("ffn1_u", 352), ("ffn1_d", 352)),
    "mid": (("w_in", 128), ("w_out", 128), ("w_mq", 128), ("w_mo", 128), ("w_mkv", 256), ("w_q", 32), ("w_kv", 16)),
    "ffn2": (("ffn2_g", 352), ("ffn2_u", 352), ("ffn2_d", 352)),
}
SEG_OFF = {}
GROUP_ROWS = {}
for _g, _segs in GROUP_SEGS.items():
    _o = 0
    for _n, _r in _segs:
        SEG_OFF[_n] = (_o, _r)
        _o += _r
    GROUP_ROWS[_g] = _o

SMALL_ROWS = (("ffn1_norm", 8), ("mix_norm", 8), ("q_norm", 8), ("kv_norm", 8), ("pool_w", 512), ("pool_scale", 8),
              ("xattn_norm", 8), ("mem_norm", 8), ("ffn2_norm", 8), ("final_norm", 8), ("loss", 8))
SMALL_OFF = {}
_o = 0
for _n, _r in SMALL_ROWS:
    SMALL_OFF[_n] = (_o, _r)
    _o += _r


def _cparams(**kw):
    return pltpu.CompilerParams(vmem_limit_bytes=VMEM_LIMIT_BYTES, **kw)


def _row_tile(rows, limit):
    best = None
    for cand in range(BF16_ROWS, min(rows, limit) + 1, BF16_ROWS):
        if rows % cand == 0:
            best = cand
    assert best is not None, rows
    return best


def _dot_nn(a, b):
    return lax.dot_general(a, b, (((1,), (0,)), ((), ())), preferred_element_type=F32)


def _dot_nt(a, b):
    return lax.dot_general(a, b, (((1,), (1,)), ((), ())), preferred_element_type=F32)


def _dot_tn(a, b):
    return lax.dot_general(a, b, (((0,), (0,)), ((), ())), preferred_element_type=F32)


def _rms_fwd(x, g):
    r = lax.rsqrt(jnp.mean(x * x, axis=-1, keepdims=True) + RMS_EPS)
    return x * r * g, r


def _rms_bwd(dy, x, g, r):
    xhat = x * r
    dyg = dy * g
    dx = r * (dyg - xhat * jnp.mean(dyg * xhat, axis=-1, keepdims=True))
    dg = jnp.sum(dy * xhat, axis=0, keepdims=True)
    return dx, dg


def _accumulate(ref, val, first):
    if isinstance(first, bool):
        if first:
            ref[...] = val
        else:
            ref[...] += val
        return

    @pl.when(first)
    def _():
        ref[...] = val

    @pl.when(jnp.logical_not(first))
    def _():
        ref[...] += val


def _call_after(token, body, in_specs, args, **kw):
    if token is not None:
        inner = body
        body = lambda tok_ref, *refs: inner(*refs)
        in_specs = [pl.BlockSpec((8, 128), lambda *_: (0, 0))] + list(in_specs)
        args = (token,) + tuple(args)
    return pl.pallas_call(body, in_specs=in_specs, **kw)(*args)


def _resident(shape):
    return pl.BlockSpec(shape, lambda *_: (0,) * len(shape), pipeline_mode=pl.Buffered(1))


def _rope_tables(pos_col, tab):
    ang = pos_col.astype(F32) * tab[0:1, :]
    return jnp.cos(ang), jnp.sin(ang) * tab[1:2, :]


def _swap_halves(x):
    lane = lax.broadcasted_iota(jnp.int32, x.shape, 1)
    return jnp.where((lane % 64) < 32, pltpu.roll(x, 96, 1), pltpu.roll(x, 32, 1))


def _rope_apply(x, cos_t, sin_t):
    return x * cos_t + _swap_halves(x) * sin_t


def _rope_apply_t(dy, cos_t, sin_t):
    return dy * cos_t + _swap_halves(dy * sin_t)


def _ffn_fwd(h, g, w, name, token=None, head=None):
    t, d = h.shape
    f = w.shape[1]
    tm, tf = min(512, t), 256
    nf = f // tf
    n_in = 3 if head is None else 5

    def body(*refs):
        h_ref, g_ref, w_ref = refs[:3]
        ho_ref, n_ref, gate_ref, up_ref = refs[n_in:n_in + 4]
        nb_sc, acc_sc = refs[-2:]
        y, _ = _rms_fwd(h_ref[...], g_ref[...])
        nb = y.astype(BF16)
        nb_sc[...] = nb
        n_ref[...] = nb
        acc_sc[...] = jnp.zeros_like(acc_sc)

        def f_tile(j):
            rows = pl.ds(pl.multiple_of(j * tf, tf), tf)
            nb = nb_sc[...]
            gt = _dot_nt(nb, w_ref[0, rows, :])
            ut = _dot_nt(nb, w_ref[1, rows, :])
            gate_ref[j] = gt.astype(BF16)
            up_ref[j] = ut.astype(BF16)
            act = (gt * jax.nn.sigmoid(gt)) * ut
            return _dot_nn(act.astype(BF16), w_ref[2, rows, :])

        def pair(p, carry):
            acc_sc[...] += f_tile(2 * p) + f_tile(2 * p + 1)
            return carry

        lax.fori_loop(0, nf // 2, pair, 0)
        if nf % 2:
            acc_sc[...] += f_tile(nf - 1)
        ho = h_ref[...] + 0.5 * acc_sc[...]
        if head is None:
            ho_ref[...] = ho
            return
        t_ref, gf_ref = refs[3:5]
        loss_ref, dgf_ref = refs[n_in + 4:n_in + 6]
        gg = gf_ref[...]
        y, r = _rms_fwd(ho, gg)
        err = y - t_ref[...]
        part = 0.5 * jnp.sum(jnp.mean(err * err, axis=-1, keepdims=True), axis=0, keepdims=True)
        dx, dg = _rms_bwd(err * (1.0 / d), ho, gg, r)
        ho_ref[...] = dx
        first = pl.program_id(0) == 0
        _accumulate(loss_ref, jnp.broadcast_to(part, loss_ref.shape), first)
        _accumulate(dgf_ref, dg, first)

    row = pl.BlockSpec((tm, d), lambda i: (i, 0))
    tiles = pl.BlockSpec((nf, tm, tf), lambda i: (0, i, 0))
    in_specs = [row, _resident((1, d)), _resident(w.shape)]
    args = (h, g, w)
    out_specs = [row, row, tiles, tiles]
    out_shape = [jax.ShapeDtypeStruct((t, d), F32), jax.ShapeDtypeStruct((t, d), BF16),
                 jax.ShapeDtypeStruct((nf, t, tf), BF16), jax.ShapeDtypeStruct((nf, t, tf), BF16)]
    if head is not None:
        in_specs += [row, _resident((1, d))]
        args += tuple(head)
        out_specs += [pl.BlockSpec((8, 128), lambda i: (0, 0)), pl.BlockSpec((1, d), lambda i: (0, 0))]
        out_shape += [jax.ShapeDtypeStruct((8, 128), F32), jax.ShapeDtypeStruct((1, d), F32)]
    return _call_after(
        token, body, in_specs, args, name=name, grid=(t // tm,), out_specs=out_specs, out_shape=out_shape,
        scratch_shapes=[pltpu.VMEM((tm, d), BF16), pltpu.VMEM((tm, d), F32)],
        compiler_params=_cparams(),
    )


def _ffn_bwd_data(dho, h, g, gate, up, w, name, token=None):
    t, d = h.shape
    f = w.shape[1]
    tm, tf = min(1024, t), 256
    parts = 2 if tm % 512 == 0 else 1
    tp = tm // parts
    nf = f // tf
    npair, odd = nf // 2, nf % 2
    nsteps = npair + odd

    def body(dho_ref, h_ref, g_ref, gate_ref, up_ref, wg_ref, wu_ref, wd_ref,
             dh_ref, dgate_ref, dup_ref, act_ref, dg_ref, dhb_sc, acc_sc):
        i, j = pl.program_id(0), pl.program_id(1)

        @pl.when(j == 0)
        def _():
            dhb_sc[...] = (0.5 * dho_ref[...]).astype(BF16)
            acc_sc[...] = jnp.zeros_like(acc_sc)

        def slab(ntile):
            cols = pl.ds(0, ntile * tf)
            for r in range(parts):
                rows = pl.ds(r * tp, tp)
                gt = jnp.concatenate([gate_ref[k, rows, :] for k in range(ntile)], axis=-1).astype(F32)
                ut = jnp.concatenate([up_ref[k, rows, :] for k in range(ntile)], axis=-1).astype(F32)
                dact = _dot_nt(dhb_sc[rows, :], wd_ref[cols, :])
                sg = jax.nn.sigmoid(gt)
                silu = gt * sg
                dgb = (dact * ut * (sg * (1.0 + gt * (1.0 - sg)))).astype(BF16)
                dub = (dact * silu).astype(BF16)
                act_ref[rows, cols] = (silu * ut).astype(BF16)
                dgate_ref[rows, cols] = dgb
                dup_ref[rows, cols] = dub
                acc_sc[rows, :] += _dot_nn(dgb, wg_ref[cols, :]) + _dot_nn(dub, wu_ref[cols, :])

        pl.when(j < npair)(lambda: slab(2))
        if odd:
            pl.when(j == npair)(lambda: slab(1))

        @pl.when(j == nsteps - 1)
        def _():
            x = h_ref[...]
            gg = g_ref[...]
            _, r = _rms_fwd(x, gg)
            dx, dg = _rms_bwd(acc_sc[...], x, gg, r)
            dh_ref[...] = dho_ref[...] + dx
            _accumulate(dg_ref, dg, i == 0)

    row = pl.BlockSpec((tm, d), lambda i, j: (i, 0))
    acts = pl.BlockSpec((2, tm, tf), lambda i, j: (j, i, 0))
    weights = lambda k: pl.BlockSpec((None, 2 * tf, d), lambda i, j: (k, j, 0))
    outs = pl.BlockSpec((tm, 2 * tf), lambda i, j: (i, j))
    padded = jax.ShapeDtypeStruct((t, 2 * tf * nsteps), BF16)
    return _call_after(
        token, body,
        [row, row, pl.BlockSpec((1, d), lambda i, j: (0, 0)), acts, acts, weights(0), weights(1), weights(2)],
        (dho, h, g, gate, up, w, w, w),
        name=name, grid=(t // tm, nsteps),
        out_specs=[row, outs, outs, outs, pl.BlockSpec((1, d), lambda i, j: (0, 0))],
        out_shape=[jax.ShapeDtypeStruct((t, d), F32), padded, padded, padded, jax.ShapeDtypeStruct((1, d), F32)],
        scratch_shapes=[pltpu.VMEM((tm, d), BF16), pltpu.VMEM((tm, d), F32)],
        compiler_params=_cparams(),
    )


def _tn_matmul(a, b, name, scale=1.0, tmm=None, out_dtype=F32, token=None, m=None):
    t = a.shape[0]
    m = a.shape[1] if m is None else m
    n = b.shape[1]
    tmm = m if tmm is None else tmm
    tk = min(1024, t)
    nk = t // tk

    def product(a_ref, b_ref):
        prod = _dot_tn(a_ref[...].astype(BF16), b_ref[...].astype(BF16))
        return prod * scale if scale != 1.0 else prod

    def body_f32(a_ref, b_ref, o_ref):
        _accumulate(o_ref, product(a_ref, b_ref), pl.program_id(1) == 0)

    def body_cast(a_ref, b_ref, o_ref, acc_sc):
        k = pl.program_id(1)
        _accumulate(acc_sc, product(a_ref, b_ref), k == 0)

        @pl.when(k == nk - 1)
        def _():
            o_ref[...] = acc_sc[...].astype(out_dtype)

    direct = out_dtype == F32
    return _call_after(
        token, body_f32 if direct else body_cast,
        [pl.BlockSpec((tk, tmm), lambda i, k: (k, i)),
         pl.BlockSpec((tk, n), lambda i, k: (k, 0))],
        (a, b),
        name=name, grid=(m // tmm, nk),
        out_specs=pl.BlockSpec((tmm, n), lambda i, k: (i, 0)),
        out_shape=jax.ShapeDtypeStruct((m, n), out_dtype),
        scratch_shapes=[] if direct else [pltpu.VMEM((tmm, n), F32)],
        compiler_params=_cparams(),
    )


def _mix_prep(h1, mix_norm, w_in, q_norm, wq_t, kv_norm, wkv, pos, rope_tab, token=None):
    t, d = h1.shape
    tm = min(512, t)

    def body(h_ref, gm_ref, win_ref, gq_ref, wq_ref, gkv_ref, wkv_ref, pos_ref, tab_ref,
             u_ref, z_ref, qn_ref, kvn_ref, q_ref, k_ref, v_ref):
        u, _ = _rms_fwd(h_ref[...], gm_ref[...])
        ub = u.astype(BF16)
        u_ref[...] = ub
        z = _dot_nn(ub, win_ref[...])
        z_ref[...] = z
        cos_t, sin_t = _rope_tables(pos_ref[...], tab_ref[...])
        qn, _ = _rms_fwd(z[:, 0:Q_RANK], gq_ref[...])
        qnb = qn.astype(BF16)
        qn_ref[...] = qnb
        q = _dot_nt(qnb, wq_ref[...])
        kvn, _ = _rms_fwd(z[:, Q_RANK:Q_RANK + KV_RANK], gkv_ref[...])
        kvnb = kvn.astype(BF16)
        kvn_ref[...] = kvnb
        kv = _dot_nn(kvnb, wkv_ref[...])
        k_pe = _rope_apply(z[:, Q_RANK + KV_RANK:Q_RANK + KV_RANK + 128], cos_t, sin_t)
        ones = jnp.ones((tm, V_DIM), F32)
        for hh in range(MLA_HEADS):
            b = hh * HEAD_PAD
            q_pe = _rope_apply(q[:, b + NOPE:b + HEAD_PAD], cos_t, sin_t)
            q_ref[hh] = jnp.concatenate([q[:, b:b + NOPE], q_pe], axis=-1).astype(BF16)
            k_ref[hh] = jnp.concatenate([kv[:, b:b + NOPE], k_pe], axis=-1).astype(BF16)
            v_ref[hh] = jnp.concatenate([kv[:, b + NOPE:b + HEAD_PAD], ones], axis=-1).astype(BF16)

    full = lambda shape: pl.BlockSpec(shape, lambda i: (0,) * len(shape))
    return _call_after(
        token, body,
        [pl.BlockSpec((tm, d), lambda i: (i, 0)), _resident((1, d)), _resident(w_in.shape), _resident((1, Q_RANK)),
         _resident(wq_t.shape), _resident((1, KV_RANK)), _resident(wkv.shape),
         pl.BlockSpec((tm, 1), lambda i: (i, 0)), _resident(rope_tab.shape)],
        (h1, mix_norm, w_in, q_norm, wq_t, kv_norm, wkv, pos, rope_tab),
        name="mix_prep", grid=(t // tm,),
        out_specs=[pl.BlockSpec((tm, d), lambda i: (i, 0)),
                   pl.BlockSpec((tm, d), lambda i: (i, 0)),
                   pl.BlockSpec((tm, Q_RANK), lambda i: (i, 0)),
                   pl.BlockSpec((tm, KV_RANK), lambda i: (i, 0)),
                   pl.BlockSpec((MLA_HEADS, tm, HEAD_PAD), lambda i: (0, i, 0)),
                   pl.BlockSpec((MLA_HEADS, tm, HEAD_PAD), lambda i: (0, i, 0)),
                   pl.BlockSpec((MLA_HEADS, tm, 2 * V_DIM), lambda i: (0, i, 0))],
        out_shape=[jax.ShapeDtypeStruct((t, d), BF16), jax.ShapeDtypeStruct((t, d), F32),
                   jax.ShapeDtypeStruct((t, Q_RANK), BF16), jax.ShapeDtypeStruct((t, KV_RANK), BF16),
                   jax.ShapeDtypeStruct((MLA_HEADS, t, HEAD_PAD), BF16),
                   jax.ShapeDtypeStruct((MLA_HEADS, t, HEAD_PAD), BF16),
                   jax.ShapeDtypeStruct((MLA_HEADS, t, 2 * V_DIM), BF16)],
        compiler_params=_cparams(),
    )


def _causal_mask(s):
    row = lax.broadcasted_iota(jnp.int32, s.shape, 0)
    col = lax.broadcasted_iota(jnp.int32, s.shape, 1)
    return jnp.where(col <= row, s, NEG_BIG)


def _attn_fwd(q, k, v):
    nh, t, _ = q.shape
    tq = tk = min(512, t)
    nq, nk = t // tq, t // tk

    pairs = [(i, j) for i in range(nq) for j in range(i + 1)]
    qi = jnp.asarray(np.array([i for i, _ in pairs], np.int32))
    kj = jnp.asarray(np.array([j for _, j in pairs], np.int32))

    def body(qi_ref, kj_ref, q_ref, k_ref, v_ref, o_ref, lse_ref, m_sc, acc_sc):
        n = pl.program_id(0)
        i, j = qi_ref[n], kj_ref[n]

        @pl.when(j == 0)
        def _():
            m_sc[...] = jnp.full_like(m_sc, NEG_BIG)
            acc_sc[...] = jnp.zeros_like(acc_sc)

        def step(diagonal):
            for hh in range(nh):
                s = _dot_nt(q_ref[hh], k_ref[hh]) * ATTN_SCALE
                if diagonal:
                    s = _causal_mask(s)
                m_old = m_sc[hh]
                m_new = jnp.maximum(m_old, jnp.max(s, axis=-1, keepdims=True))
                p = jnp.exp(s - m_new).astype(BF16)
                acc_sc[hh] = jnp.exp(m_old - m_new) * acc_sc[hh] + _dot_nn(p, v_ref[hh])
                m_sc[hh] = m_new

        @pl.when(j < i)
        def _():
            step(False)

        @pl.when(j == i)
        def _():
            step(True)
            for hh in range(nh):
                acc = acc_sc[hh]
                l = acc[:, V_DIM:2 * V_DIM]
                o_ref[:, hh * V_DIM:(hh + 1) * V_DIM] = (acc[:, 0:V_DIM] / l).astype(BF16)
                lse_ref[hh] = m_sc[hh] + jnp.log(l[:, 0:1])

    q_map = lambda n, qi_ref, kj_ref: (0, qi_ref[n], 0)
    kv_map = lambda n, qi_ref, kj_ref: (0, kj_ref[n], 0)
    return pl.pallas_call(
        body, name="attn_fwd",
        grid_spec=pltpu.PrefetchScalarGridSpec(
            num_scalar_prefetch=2, grid=(len(pairs),),
            in_specs=[pl.BlockSpec((nh, tq, HEAD_PAD), q_map),
                      pl.BlockSpec((nh, tk, HEAD_PAD), kv_map),
                      pl.BlockSpec((nh, tk, 2 * V_DIM), kv_map)],
            out_specs=[pl.BlockSpec((tq, nh * V_DIM), lambda n, qi_ref, kj_ref: (qi_ref[n], 0)),
                       pl.BlockSpec((nh, tq, 1), q_map)],
            scratch_shapes=[pltpu.VMEM((nh, tq, 1), F32), pltpu.VMEM((nh, tq, 2 * V_DIM), F32)]),
        out_shape=[jax.ShapeDtypeStruct((t, nh * V_DIM), BF16), jax.ShapeDtypeStruct((nh, t, 1), F32)],
        compiler_params=_cparams(),
    )(qi, kj, q, k, v)


def _attn_delta(o, do):
    t, w = o.shape
    nh = w // V_DIM
    tm = min(512, t)

    def body(o_ref, do_ref, d_ref):
        prod = o_ref[...].astype(F32) * do_ref[...].astype(F32)
        for hh in range(nh):
            d_ref[hh] = jnp.sum(prod[:, hh * V_DIM:(hh + 1) * V_DIM], axis=-1, keepdims=True)

    return pl.pallas_call(
        body, name="attn_delta", grid=(t // tm,),
        in_specs=[pl.BlockSpec((tm, w), lambda i: (i, 0)), pl.BlockSpec((tm, w), lambda i: (i, 0))],
        out_specs=pl.BlockSpec((nh, tm, 1), lambda i: (0, i, 0)),
        out_shape=jax.ShapeDtypeStruct((nh, t, 1), F32),
        compiler_params=_cparams(),
    )(o, do)


ATTN_BWD_HEADS = 2


def _attn_bwd(q, k, v, do, lse, delta):
    nh, t, _ = q.shape
    hp = ATTN_BWD_HEADS
    tq = tk = min(512, t)
    nq, nk = t // tq, t // tk

    pairs = [(j, i) for j in range(nk) for i in range(j, nq)]
    kj = jnp.asarray(np.array([j for j, _ in pairs], np.int32))
    qi = jnp.asarray(np.array([i for _, i in pairs], np.int32))

    def body(kj_ref, qi_ref, q_ref, k_ref, v_ref, do_ref, lse_ref, dlt_ref, dq_ref, dk_ref, dv_ref):
        n = pl.program_id(1)
        j, i = kj_ref[n], qi_ref[n]

        @pl.when(n == 0)
        def _():
            dq_ref[...] = jnp.zeros_like(dq_ref)

        def step(diagonal):
            for hh in range(hp):
                qq, kk = q_ref[hh], k_ref[hh]
                dob = do_ref[:, hh * V_DIM:(hh + 1) * V_DIM]
                s = _dot_nt(qq, kk) * ATTN_SCALE
                if diagonal:
                    s = _causal_mask(s)
                p = jnp.exp(s - lse_ref[hh])
                dpp = _dot_nt(dob, v_ref[hh])
                dsb = (p * (dpp - dlt_ref[hh]) * ATTN_SCALE).astype(BF16)
                _accumulate(dv_ref.at[hh], _dot_tn(p.astype(BF16), dob), diagonal)
                _accumulate(dk_ref.at[hh], _dot_tn(dsb, qq), diagonal)
                dq_ref[hh, pl.ds(pl.multiple_of(i * tq, tq), tq), :] += _dot_nn(dsb, kk)

        @pl.when(i > j)
        def _():
            step(False)

        @pl.when(i == j)
        def _():
            step(True)

    q_map = lambda h, n, kj_ref, qi_ref: (h, qi_ref[n], 0)
    k_map = lambda h, n, kj_ref, qi_ref: (h, kj_ref[n], 0)
    return pl.pallas_call(
        body, name="attn_bwd",
        grid_spec=pltpu.PrefetchScalarGridSpec(
            num_scalar_prefetch=2, grid=(nh // hp, len(pairs)),
            in_specs=[pl.BlockSpec((hp, tq, HEAD_PAD), q_map),
                      pl.BlockSpec((hp, tk, HEAD_PAD), k_map),
                      pl.BlockSpec((hp, tk, V_DIM), k_map),
                      pl.BlockSpec((tq, hp * V_DIM), lambda h, n, kj_ref, qi_ref: (qi_ref[n], h)),
                      pl.BlockSpec((hp, tq, 1), q_map),
                      pl.BlockSpec((hp, tq, 1), q_map)],
            out_specs=[pl.BlockSpec((hp, t, HEAD_PAD), lambda h, n, kj_ref, qi_ref: (h, 0, 0)),
                       pl.BlockSpec((hp, tk, HEAD_PAD), k_map),
                       pl.BlockSpec((hp, tk, V_DIM), k_map)]),
        out_shape=[jax.ShapeDtypeStruct((nh, t, HEAD_PAD), F32), jax.ShapeDtypeStruct((nh, t, HEAD_PAD), F32),
                   jax.ShapeDtypeStruct((nh, t, V_DIM), F32)],
        compiler_params=_cparams(),
    )(kj, qi, q, k, v, do, lse, delta)


def _pool_counts(first_token, rows, w):
    tok = lax.broadcasted_iota(jnp.int32, (rows, POOL_CH), 0) + first_token
    return jnp.minimum(tok + 1, w).astype(F32)


def _pool_centered(zbuf, g, w, i, tm):
    lanes = pl.ds(g * POOL_CH, POOL_CH)
    cur = zbuf[pl.ds(POOL_HALO, tm), lanes]
    win = cur
    for s in range(1, w):
        win = win + zbuf[pl.ds(POOL_HALO - s, tm), lanes]
    return win / _pool_counts(i * tm, tm, w) - cur


def _pool_load(zbuf, z_ref, halo_ref, i, tm):
    @pl.when(i == 0)
    def _():
        zbuf[pl.ds(0, POOL_HALO), :] = jnp.zeros((POOL_HALO, zbuf.shape[1]), F32)

    @pl.when(i > 0)
    def _():
        zbuf[pl.ds(0, POOL_HALO), :] = halo_ref[...]

    zbuf[pl.ds(POOL_HALO, tm), :] = z_ref[...]


def _pool_fwd(z, pool_w, pool_scale):
    t = z.shape[0]
    pw = len(POOL_WINDOWS) * POOL_CH
    tm = min(512, t)
    hb = tm // POOL_HALO

    def body(z_ref, halo_ref, w_ref, sc_ref, p_ref, zbuf):
        i = pl.program_id(0)
        _pool_load(zbuf, z_ref, halo_ref, i, tm)
        for g, w in enumerate(POOL_WINDOWS):
            c = _pool_centered(zbuf, g, w, i, tm)
            y = _dot_nn(c.astype(BF16), w_ref[g]) * sc_ref[:, g * POOL_CH:(g + 1) * POOL_CH]
            p_ref[:, g * POOL_CH:(g + 1) * POOL_CH] = y.astype(BF16)

    return pl.pallas_call(
        body, name="pool_fwd", grid=(t // tm,),
        in_specs=[pl.BlockSpec((tm, pw), lambda i: (i, 1)),
                  pl.BlockSpec((POOL_HALO, pw), lambda i: (jnp.maximum(i * hb - 1, 0), 1)),
                  pl.BlockSpec(pool_w.shape, lambda i: (0, 0, 0)),
                  pl.BlockSpec((1, pw), lambda i: (0, 0))],
        out_specs=pl.BlockSpec((tm, pw), lambda i: (i, 0)),
        out_shape=jax.ShapeDtypeStruct((t, pw), BF16),
        scratch_shapes=[pltpu.VMEM((POOL_HALO + tm, pw), F32)],
        compiler_params=_cparams(),
    )(z, z, pool_w, pool_scale)


def _pool_bwd(dp, z, pool_w, pool_scale):
    t = z.shape[0]
    ng = len(POOL_WINDOWS)
    pw = ng * POOL_CH
    tm = min(512, t)
    hb = tm // POOL_HALO
    nt = t // tm

    def body(dp_ref, dpn_ref, z_ref, halo_ref, w_ref, sc_ref, dz_ref, dw_ref, dsc_ref, zbuf, dbuf):
        i = pl.program_id(0)
        _pool_load(zbuf, z_ref, halo_ref, i, tm)

        @pl.when(i == 0)
        def _():
            dw_ref[...] = jnp.zeros_like(dw_ref)
            dsc_ref[...] = jnp.zeros_like(dsc_ref)

        nxt_ok = (i < nt - 1).astype(F32)
        for g, w in enumerate(POOL_WINDOWS):
            lanes = pl.ds(g * POOL_CH, POOL_CH)
            cols = slice(g * POOL_CH, (g + 1) * POOL_CH)
            sc = sc_ref[:, cols]
            wg = w_ref[g]
            c = _pool_centered(zbuf, g, w, i, tm).astype(BF16)
            ypre = _dot_nn(c, wg)
            dpg = dp_ref[:, cols].astype(F32)
            dsc_ref[:, cols] += jnp.sum(dpg * ypre, axis=0, keepdims=True)
            dyb = (dpg * sc).astype(BF16)
            dw_ref[g] += _dot_tn(c, dyb)
            dd = _dot_nt(dyb, wg)
            dyn = (dpn_ref[:, cols].astype(F32) * sc).astype(BF16)
            ddn = _dot_nt(dyn, wg) * nxt_ok
            dbuf[pl.ds(0, tm), lanes] = dd / _pool_counts(i * tm, tm, w)
            dbuf[pl.ds(tm, POOL_HALO), lanes] = ddn / _pool_counts((i + 1) * tm, POOL_HALO, w)
            acc = -dd
            for s in range(w):
                acc = acc + dbuf[pl.ds(s, tm), lanes]
            dz_ref[:, cols] = acc

    return pl.pallas_call(
        body, name="pool_bwd", grid=(nt,),
        in_specs=[pl.BlockSpec((tm, pw), lambda i: (i, 0)),
                  pl.BlockSpec((POOL_HALO, pw), lambda i: (jnp.minimum((i + 1) * hb, t // POOL_HALO - 1), 0)),
                  pl.BlockSpec((tm, pw), lambda i: (i, 1)),
                  pl.BlockSpec((POOL_HALO, pw), lambda i: (jnp.maximum(i * hb - 1, 0), 1)),
                  pl.BlockSpec(pool_w.shape, lambda i: (0, 0, 0)),
                  pl.BlockSpec((1, pw), lambda i: (0, 0))],
        out_specs=[pl.BlockSpec((tm, pw), lambda i: (i, 0)),
                   pl.BlockSpec((ng, POOL_CH, POOL_CH), lambda i: (0, 0, 0)),
                   pl.BlockSpec((1, pw), lambda i: (0, 0))],
        out_shape=[jax.ShapeDtypeStruct((t, pw), F32), jax.ShapeDtypeStruct((ng, POOL_CH, POOL_CH), F32),
                   jax.ShapeDtypeStruct((1, pw), F32)],
        scratch_shapes=[pltpu.VMEM((POOL_HALO + tm, pw), F32), pltpu.VMEM((tm + POOL_HALO, pw), F32)],
        compiler_params=_cparams(),
    )(dp, dp, z, z, pool_w, pool_scale)


def _mla_bwd(dq_h, dk_h, dv_h, z, dz_pool, h1, dh2, mix_norm, w_in, q_norm, wq_t, kv_norm, wkv, pos, rope_tab):
    t, d = h1.shape
    tm = min(512, t)

    def body(dqh_ref, dkh_ref, dvh_ref, z_ref, dzp_ref, h_ref, dh2_ref, gm_ref, win_ref, gq_ref, wq_ref, gkv_ref,
             wkv_ref, pos_ref, tab_ref, dh1_ref, dq_ref, dkv_ref, dz_ref, dgq_ref, dgkv_ref, dgm_ref):
        i = pl.program_id(0)
        first = i == 0
        cos_t, sin_t = _rope_tables(pos_ref[...], tab_ref[...])
        dq_parts, dkv_parts = [], []
        dk_pe = jnp.zeros((tm, 128), F32)
        for hh in range(MLA_HEADS):
            dqh = dqh_ref[hh]
            dq_parts += [dqh[:, 0:NOPE], _rope_apply_t(dqh[:, NOPE:HEAD_PAD], cos_t, sin_t)]
            dkh = dkh_ref[hh]
            dkv_parts += [dkh[:, 0:NOPE], dvh_ref[hh]]
            dk_pe = dk_pe + dkh[:, NOPE:HEAD_PAD]
        dqb = jnp.concatenate(dq_parts, axis=-1).astype(BF16)
        dkvb = jnp.concatenate(dkv_parts, axis=-1).astype(BF16)
        dq_ref[...] = dqb
        dkv_ref[...] = dkvb
        z = z_ref[...]
        c_q = z[:, 0:Q_RANK]
        gq = gq_ref[...]
        _, rq = _rms_fwd(c_q, gq)
        dcq, dgq = _rms_bwd(_dot_nn(dqb, wq_ref[...]), c_q, gq, rq)
        c_kv = z[:, Q_RANK:Q_RANK + KV_RANK]
        gkv = gkv_ref[...]
        _, rkv = _rms_fwd(c_kv, gkv)
        dckv, dgkv = _rms_bwd(_dot_nt(dkvb, wkv_ref[...]), c_kv, gkv, rkv)
        dkr = _rope_apply_t(dk_pe, cos_t, sin_t)
        dzb = jnp.concatenate([dcq, dckv, dkr, dzp_ref[...]], axis=-1).astype(BF16)
        dz_ref[...] = dzb
        x = h_ref[...]
        gm = gm_ref[...]
        _, rm = _rms_fwd(x, gm)
        dx, dgm = _rms_bwd(_dot_nt(dzb, win_ref[...]), x, gm, rm)
        dh1_ref[...] = dh2_ref[...] + dx
        _accumulate(dgq_ref, dgq, first)
        _accumulate(dgkv_ref, dgkv, first)
        _accumulate(dgm_ref, dgm, first)

    full = lambda shape: pl.BlockSpec(shape, lambda i: (0,) * len(shape))
    row = lambda w: pl.BlockSpec((tm, w), lambda i: (i, 0))
    head = lambda w: pl.BlockSpec((MLA_HEADS, tm, w), lambda i: (0, i, 0))
    pw = len(POOL_WINDOWS) * POOL_CH
    return pl.pallas_call(
        body, name="mla_bwd", grid=(t // tm,),
        in_specs=[head(HEAD_PAD), head(HEAD_PAD), head(V_DIM), row(d), row(pw), row(d), row(d),
                  _resident((1, d)), _resident(w_in.shape), _resident((1, Q_RANK)), _resident(wq_t.shape),
                  _resident((1, KV_RANK)), _resident(wkv.shape), row(1), _resident(rope_tab.shape)],
        out_specs=[row(d), row(d), row(d), row(d), full((1, Q_RANK)), full((1, KV_RANK)), full((1, d))],
        out_shape=[jax.ShapeDtypeStruct((t, d), F32), jax.ShapeDtypeStruct((t, d), BF16),
                   jax.ShapeDtypeStruct((t, d), BF16), jax.ShapeDtypeStruct((t, d), BF16),
                   jax.ShapeDtypeStruct((1, Q_RANK), F32), jax.ShapeDtypeStruct((1, KV_RANK), F32),
                   jax.ShapeDtypeStruct((1, d), F32)],
        compiler_params=_cparams(),
    )(dq_h, dk_h, dv_h, z, dz_pool, h1, dh2, mix_norm, w_in, q_norm, wq_t, kv_norm, wkv, pos, rope_tab)


def _mem_kv(mem, mem_norm, wmkv):
    n, d = mem.shape

    def body(mem_ref, g_ref, w_ref, memn_ref, k_ref, v_ref):
        y, _ = _rms_fwd(mem_ref[...], g_ref[...])
        yb = y.astype(BF16)
        memn_ref[...] = yb
        for hh in range(MEM_HEADS):
            k_ref[hh] = _dot_nn(yb, w_ref[hh]).astype(BF16)
            v_ref[hh] = _dot_nn(yb, w_ref[MEM_HEADS + hh]).astype(BF16)

    return pl.pallas_call(
        body, name="mem_kv",
        out_shape=[jax.ShapeDtypeStruct((n, d), BF16), jax.ShapeDtypeStruct((MEM_HEADS, n, MEM_HD), BF16),
                   jax.ShapeDtypeStruct((MEM_HEADS, n, MEM_HD), BF16)],
        compiler_params=_cparams(),
    )(mem, mem_norm, wmkv)


def _mem_softmax(qb, km):
    s = _dot_nt(qb, km) * MEM_SCALE
    e = jnp.exp(s - jnp.max(s, axis=-1, keepdims=True))
    return e / jnp.sum(e, axis=-1, keepdims=True)


def _xattn_fwd(h1, a, p, w_out, g, wmq, km, vm, wmo, token=None):
    t, d = h1.shape
    tm = min(512, t)
    half = a.shape[1]

    def body(h_ref, a_ref, p_ref, wo_ref, g_ref, wmq_ref, km_ref, vm_ref, wmo_ref,
             h2_ref, h3_ref, hn_ref, q_ref, o_ref):
        h2 = h_ref[...] + _dot_nn(a_ref[...], wo_ref[0:half, :]) + _dot_nn(p_ref[...], wo_ref[half:2 * half, :])
        h2_ref[...] = h2
        hn, _ = _rms_fwd(h2, g_ref[...])
        hnb = hn.astype(BF16)
        hn_ref[...] = hnb
        qb = _dot_nn(hnb, wmq_ref[...]).astype(BF16)
        q_ref[...] = qb
        outs = []
        for hh in range(MEM_HEADS):
            pr = _mem_softmax(qb[:, hh * MEM_HD:(hh + 1) * MEM_HD], km_ref[hh])
            outs.append(_dot_nn(pr.astype(BF16), vm_ref[hh]))
        ob = jnp.concatenate(outs, axis=-1).astype(BF16)
        o_ref[...] = ob
        h3_ref[...] = h2 + _dot_nn(ob, wmo_ref[...])

    full = lambda shape: pl.BlockSpec(shape, lambda i: (0,) * len(shape))
    row = lambda w: pl.BlockSpec((tm, w), lambda i: (i, 0))
    return _call_after(
        token, body,
        [row(d), row(half), row(half), _resident(w_out.shape), _resident((1, d)), _resident(wmq.shape),
         _resident(km.shape), _resident(vm.shape), _resident(wmo.shape)],
        (h1, a, p, w_out, g, wmq, km, vm, wmo),
        name="xattn_fwd", grid=(t // tm,),
        out_specs=[row(d), row(d), row(d), row(d), row(d)],
        out_shape=[jax.ShapeDtypeStruct((t, d), F32), jax.ShapeDtypeStruct((t, d), F32),
                   jax.ShapeDtypeStruct((t, d), BF16), jax.ShapeDtypeStruct((t, d), BF16),
                   jax.ShapeDtypeStruct((t, d), BF16)],
        compiler_params=_cparams(),
    )


def _xattn_bwd(dh3, h2, qm, g, wmq, km, vm, wmo, w_out, token=None):
    t, d = h2.shape
    tm = min(512, t)
    half = d // 2

    def body(dh3_ref, h2_ref, q_ref, g_ref, wmq_ref, km_ref, vm_ref, wmo_ref, wo_ref,
             dh2_ref, dq_ref, da_ref, dp_ref, dk_ref, dv_ref, dg_ref):
        i = pl.program_id(0)
        first = i == 0

        @pl.when(first)
        def _():
            dk_ref[...] = jnp.zeros_like(dk_ref)
            dv_ref[...] = jnp.zeros_like(dv_ref)

        dh3 = dh3_ref[...]
        dob = _dot_nt(dh3.astype(BF16), wmo_ref[...]).astype(BF16)
        qb = q_ref[...]
        dq_parts = []
        for hh in range(MEM_HEADS):
            cols = slice(hh * MEM_HD, (hh + 1) * MEM_HD)
            kk, vv = km_ref[hh], vm_ref[hh]
            pr = _mem_softmax(qb[:, cols], kk)
            doh = dob[:, cols]
            dv_ref[hh] += _dot_tn(pr.astype(BF16), doh)
            dpp = _dot_nt(doh, vv)
            dsb = (pr * (dpp - jnp.sum(dpp * pr, axis=-1, keepdims=True)) * MEM_SCALE).astype(BF16)
            dq_parts.append(_dot_nn(dsb, kk))
            dk_ref[hh] += _dot_tn(dsb, qb[:, cols])
        dqb = jnp.concatenate(dq_parts, axis=-1).astype(BF16)
        dq_ref[...] = dqb
        x = h2_ref[...]
        gg = g_ref[...]
        _, r = _rms_fwd(x, gg)
        dx, dg = _rms_bwd(_dot_nt(dqb, wmq_ref[...]), x, gg, r)
        dh2 = dh3 + dx
        dh2_ref[...] = dh2
        dap = _dot_nt(dh2.astype(BF16), wo_ref[...])
        da_ref[...] = dap[:, 0:half].astype(BF16)
        dp_ref[...] = dap[:, half:d].astype(BF16)
        _accumulate(dg_ref, dg, first)

    full = lambda shape: pl.BlockSpec(shape, lambda i: (0,) * len(shape))
    row = lambda w: pl.BlockSpec((tm, w), lambda i: (i, 0))
    return _call_after(
        token, body,
        [row(d), row(d), row(d), _resident((1, d)), _resident(wmq.shape), _resident(km.shape), _resident(vm.shape),
         _resident(wmo.shape), _resident(w_out.shape)],
        (dh3, h2, qm, g, wmq, km, vm, wmo, w_out),
        name="xattn_bwd", grid=(t // tm,),
        out_specs=[row(d), row(d), row(half), row(half), full(km.shape), full(vm.shape), full((1, d))],
        out_shape=[jax.ShapeDtypeStruct((t, d), F32), jax.ShapeDtypeStruct((t, d), BF16),
                   jax.ShapeDtypeStruct((t, half), BF16), jax.ShapeDtypeStruct((t, half), BF16),
                   jax.ShapeDtypeStruct(km.shape, F32), jax.ShapeDtypeStruct(vm.shape, F32),
                   jax.ShapeDtypeStruct((1, d), F32)],
        compiler_params=_cparams(),
    )


def _mem_kv_bwd(dkm, dvm, memn, mem, mem_norm, wmkv):
    n, d = mem.shape

    def body(dk_ref, dv_ref, memn_ref, mem_ref, g_ref, w_ref, dw_ref, dg_ref):
        memn = memn_ref[...]
        dmemn = jnp.zeros((n, d), F32)
        for s in range(2 * MEM_HEADS):
            src = dk_ref[s] if s < MEM_HEADS else dv_ref[s - MEM_HEADS]
            db = src.astype(BF16)
            dw_ref[s] = _dot_tn(memn, db)
            dmemn = dmemn + _dot_nt(db, w_ref[s])
        x = mem_ref[...]
        gg = g_ref[...]
        _, r = _rms_fwd(x, gg)
        _, dg = _rms_bwd(dmemn, x, gg, r)
        dg_ref[...] = dg

    return pl.pallas_call(
        body, name="mem_kv_bwd",
        out_shape=[jax.ShapeDtypeStruct(wmkv.shape, F32), jax.ShapeDtypeStruct((1, d), F32)],
        compiler_params=_cparams(),
    )(dkm, dvm, memn, mem, mem_norm, wmkv)


MESH_ID = pl.DeviceIdType.MESH
ANY = pl.BlockSpec(memory_space=pl.ANY)


def _coords():
    return lax.axis_index("x"), lax.axis_index("y"), lax.axis_index("c")


def _other_chips(x, y):
    return [(1 - x, y), (x, 1 - y), (1 - x, 1 - y)]


def _core_reduce(g, tag):
    _, r, w = g.shape

    def body(g_ref, part_ref, own_sc, recv_sc, send_sems, recv_sems, local_sems):
        x, y, c = _coords()
        sent, local = [], []
        for chip in range(4):
            sent.append(pltpu.make_async_remote_copy(
                src_ref=g_ref.at[2 * chip + (1 - c)], dst_ref=recv_sc.at[chip],
                send_sem=send_sems.at[chip], recv_sem=recv_sems.at[chip],
                device_id=(x, y, 1 - c), device_id_type=MESH_ID))
            local.append(pltpu.make_async_copy(g_ref.at[2 * chip + c], own_sc.at[chip], local_sems.at[chip]))
        for cp in sent + local:
            cp.start()
        for chip in range(4):
            local[chip].wait()
            sent[chip].wait_recv()
            part_ref[chip] = (own_sc[chip].astype(F32) + recv_sc[chip].astype(F32)).astype(part_ref.dtype)
        for cp in sent:
            cp.wait_send()

    return pl.pallas_call(
        body, name="core_reduce_" + tag,
        out_shape=jax.ShapeDtypeStruct((4, r, w), g.dtype),
        in_specs=[ANY], out_specs=pl.BlockSpec(memory_space=pltpu.VMEM),
        scratch_shapes=[pltpu.VMEM((4, r, w), g.dtype), pltpu.VMEM((4, r, w), g.dtype),
                        pltpu.SemaphoreType.DMA((4,)), pltpu.SemaphoreType.DMA((4,)), pltpu.SemaphoreType.DMA((4,))],
        compiler_params=_cparams(),
    )(g)


HBM_SPEC = pl.BlockSpec(memory_space=pltpu.HBM)
SEM_SPEC = pl.BlockSpec(memory_space=pltpu.SEMAPHORE)
SPLIT_EFFECT = pltpu.SideEffectType.DATAFLOW_SIDE_EFFECTING


def _ici_refs(gather, src_ref, land_ref, j, px, py, slot_chip, c):
    if gather:
        return src_ref, land_ref.at[:, 4 * slot_chip[0] + 2 * slot_chip[1] + c]
    return src_ref.at[2 * px + py], land_ref.at[j]


def _ici_start(src, after, name, gather):
    r, w = src.shape[-2:]
    land_shape = (src.shape[0], N_DEV, r, w) if gather else (3, r, w)

    def body(src_ref, land_ref, after_ref, send_sems, recv_sems, src_thru, land_thru, token):
        x, y, c = _coords()
        for j, (px, py) in enumerate(_other_chips(x, y)):
            s_ref, d_ref = _ici_refs(gather, src_ref, land_ref, j, px, py, (x, y), c)
            pltpu.make_async_remote_copy(
                src_ref=s_ref, dst_ref=d_ref, send_sem=send_sems.at[j], recv_sem=recv_sems.at[j],
                device_id=(px, py, c), device_id_type=MESH_ID).start()
        token[...] = jnp.zeros_like(token)

    return pl.pallas_call(
        body, name=name,
        out_shape=(pltpu.SemaphoreType.DMA((3,)), pltpu.SemaphoreType.DMA((3,)), pltpu.HBM(src.shape, src.dtype),
                   pltpu.HBM(land_shape, src.dtype), jax.ShapeDtypeStruct((8, 128), F32)),
        in_specs=(HBM_SPEC, HBM_SPEC, ANY),
        out_specs=(SEM_SPEC, SEM_SPEC, HBM_SPEC, HBM_SPEC, pl.BlockSpec(memory_space=pltpu.VMEM)),
        input_output_aliases={0: 2, 1: 3},
        compiler_params=pltpu.CompilerParams(has_side_effects=SPLIT_EFFECT),
    )(pltpu.with_memory_space_constraint(src, pltpu.HBM),
      pltpu.with_memory_space_constraint(lax.empty(land_shape, src.dtype), pltpu.HBM), after)


def _ici_wait(started, after, name, gather):
    send_sems, recv_sems, src_thru, land_thru, _ = started

    def body(src_ref, land_ref, send_sems, recv_sems, after_ref, src_dead, got_ref):
        x, y, c = _coords()
        for j, (px, py) in enumerate(_other_chips(x, y)):
            s_ref, d_ref = _ici_refs(gather, src_ref, land_ref, j, px, py, (px, py), c)
            copy = pltpu.make_async_remote_copy(
                src_ref=s_ref, dst_ref=d_ref, send_sem=send_sems.at[j], recv_sem=recv_sems.at[j],
                device_id=(px, py, c), device_id_type=MESH_ID)
            copy.wait_send()
            copy.wait_recv()

    return pl.pallas_call(
        body, name=name,
        out_shape=(pltpu.HBM(src_thru.shape, src_thru.dtype), pltpu.HBM(land_thru.shape, land_thru.dtype)),
        in_specs=(HBM_SPEC, HBM_SPEC, SEM_SPEC, SEM_SPEC, ANY),
        out_specs=(HBM_SPEC, HBM_SPEC), input_output_aliases={0: 0, 1: 1},
        compiler_params=pltpu.CompilerParams(has_side_effects=SPLIT_EFFECT),
    )(src_thru, land_thru, send_sems, recv_sems, after)


def _neighbour(k, x, y):
    return (1 - x, y) if k == 0 else (x, 1 - y)


def _slot(ref, px, py, c):
    return ref.at[:, 4 * px + 2 * py + c]


def _near_start(src, after, name):
    land_shape = (src.shape[0], N_DEV) + src.shape[1:]

    def body(src_ref, land_ref, after_ref, send_sems, recv_sems, src_thru, land_thru, token):
        x, y, c = _coords()
        for k in range(2):
            px, py = _neighbour(k, x, y)
            pltpu.make_async_remote_copy(
                src_ref=src_ref, dst_ref=_slot(land_ref, x, y, c), send_sem=send_sems.at[k],
                recv_sem=recv_sems.at[k], device_id=(px, py, c), device_id_type=MESH_ID).start()
        token[...] = jnp.zeros_like(token)

    return pl.pallas_call(
        body, name=name,
        out_shape=(pltpu.SemaphoreType.DMA((2,)), pltpu.SemaphoreType.DMA((2,)), pltpu.HBM(src.shape, src.dtype),
                   pltpu.HBM(land_shape, src.dtype), jax.ShapeDtypeStruct((8, 128), F32)),
        in_specs=(HBM_SPEC, HBM_SPEC, ANY),
        out_specs=(SEM_SPEC, SEM_SPEC, HBM_SPEC, HBM_SPEC, pl.BlockSpec(memory_space=pltpu.VMEM)),
        input_output_aliases={0: 2, 1: 3},
        compiler_params=pltpu.CompilerParams(has_side_effects=SPLIT_EFFECT),
    )(pltpu.with_memory_space_constraint(src, pltpu.HBM),
      pltpu.with_memory_space_constraint(lax.empty(land_shape, src.dtype), pltpu.HBM), after)


def _near_wait(started, after, name):
    send_sems, recv_sems, src_thru, land_thru, _ = started

    def body(src_ref, land_ref, send_sems, recv_sems, after_ref, src_dead, got_ref):
        x, y, c = _coords()
        for k in range(2):
            px, py = _neighbour(k, x, y)
            copy = pltpu.make_async_remote_copy(
                src_ref=src_ref, dst_ref=_slot(land_ref, px, py, c), send_sem=send_sems.at[k],
                recv_sem=recv_sems.at[k], device_id=(px, py, c), device_id_type=MESH_ID)
            copy.wait_send()
            copy.wait_recv()

    return pl.pallas_call(
        body, name=name,
        out_shape=(pltpu.HBM(src_thru.shape, src_thru.dtype), pltpu.HBM(land_thru.shape, land_thru.dtype)),
        in_specs=(HBM_SPEC, HBM_SPEC, SEM_SPEC, SEM_SPEC, ANY),
        out_specs=(HBM_SPEC, HBM_SPEC), input_output_aliases={0: 0, 1: 1},
        compiler_params=pltpu.CompilerParams(has_side_effects=SPLIT_EFFECT),
    )(src_thru, land_thru, send_sems, recv_sems, after)


def _far_refs(land_ref, k, x, y, c, arriving):
    half = land_ref.shape[2] // 2
    rows = pl.ds(k * half, half)
    ox, oy = (1 - x, 1 - y) if arriving else _neighbour(k, x, y)
    return land_ref.at[:, 4 * ox + 2 * oy + c, rows]


def _far_start(land, after, name):
    def body(land_ref, after_ref, send_sems, recv_sems, land_thru, token):
        x, y, c = _coords()
        for k in range(2):
            block = _far_refs(land_ref, k, x, y, c, False)
            px, py = _neighbour(1 - k, x, y)
            pltpu.make_async_remote_copy(
                src_ref=block, dst_ref=block, send_sem=send_sems.at[k], recv_sem=recv_sems.at[k],
                device_id=(px, py, c), device_id_type=MESH_ID).start()
        token[...] = jnp.zeros_like(token)

    return pl.pallas_call(
        body, name=name,
        out_shape=(pltpu.SemaphoreType.DMA((2,)), pltpu.SemaphoreType.DMA((2,)),
                   pltpu.HBM(land.shape, land.dtype), jax.ShapeDtypeStruct((8, 128), F32)),
        in_specs=(HBM_SPEC, ANY),
        out_specs=(SEM_SPEC, SEM_SPEC, HBM_SPEC, pl.BlockSpec(memory_space=pltpu.VMEM)),
        input_output_aliases={0: 2},
        compiler_params=pltpu.CompilerParams(has_side_effects=SPLIT_EFFECT),
    )(pltpu.with_memory_space_constraint(land, pltpu.HBM), after)


def _far_wait(started, after, name):
    send_sems, recv_sems, land_thru, _ = started

    def body(land_ref, send_sems, recv_sems, after_ref, got_ref):
        x, y, c = _coords()
        for k in range(2):
            px, py = _neighbour(1 - k, x, y)
            copy = pltpu.make_async_remote_copy(
                src_ref=_far_refs(land_ref, k, x, y, c, False), dst_ref=_far_refs(land_ref, k, x, y, c, True),
                send_sem=send_sems.at[k], recv_sem=recv_sems.at[k], device_id=(px, py, c), device_id_type=MESH_ID)
            copy.wait_send()
            copy.wait_recv()

    return pl.pallas_call(
        body, name=name,
        out_shape=pltpu.HBM(land_thru.shape, land_thru.dtype),
        in_specs=(HBM_SPEC, SEM_SPEC, SEM_SPEC, ANY),
        out_specs=HBM_SPEC, input_output_aliases={0: 0},
        compiler_params=pltpu.CompilerParams(has_side_effects=SPLIT_EFFECT),
    )(land_thru, send_sems, recv_sems, after)


def _peer(k, x, y, c):
    return x ^ ((k >> 2) & 1), y ^ ((k >> 1) & 1), c ^ (k & 1)


def _peers_start(src, after, name):
    r, w = src.shape
    x, y, c = _coords()
    land = lax.dynamic_update_slice(jnp.zeros((N_DEV, r, w), src.dtype), src[None], (4 * x + 2 * y + c, 0, 0))

    def body(src_ref, land_ref, after_ref, send_sems, recv_sems, src_thru, land_thru, token):
        x, y, c = _coords()
        for k in range(1, N_DEV):
            pltpu.make_async_remote_copy(
                src_ref=src_ref, dst_ref=land_ref.at[4 * x + 2 * y + c],
                send_sem=send_sems.at[k - 1], recv_sem=recv_sems.at[k - 1],
                device_id=_peer(k, x, y, c), device_id_type=MESH_ID).start()
        token[...] = jnp.zeros_like(token)

    return pl.pallas_call(
        body, name=name,
        out_shape=(pltpu.SemaphoreType.DMA((N_DEV - 1,)), pltpu.SemaphoreType.DMA((N_DEV - 1,)),
                   pltpu.HBM(src.shape, src.dtype), pltpu.HBM(land.shape, src.dtype),
                   jax.ShapeDtypeStruct((8, 128), F32)),
        in_specs=(HBM_SPEC, HBM_SPEC, ANY),
        out_specs=(SEM_SPEC, SEM_SPEC, HBM_SPEC, HBM_SPEC, pl.BlockSpec(memory_space=pltpu.VMEM)),
        input_output_aliases={0: 2, 1: 3},
        compiler_params=pltpu.CompilerParams(has_side_effects=SPLIT_EFFECT),
    )(pltpu.with_memory_space_constraint(src, pltpu.HBM), pltpu.with_memory_space_constraint(land, pltpu.HBM), after)


def _peers_wait(started, after, name):
    send_sems, recv_sems, src_thru, land_thru, _ = started

    def body(src_ref, land_ref, send_sems, recv_sems, after_ref, src_dead, got_ref):
        x, y, c = _coords()
        for k in range(1, N_DEV):
            px, py, pc = _peer(k, x, y, c)
            copy = pltpu.make_async_remote_copy(
                src_ref=src_ref, dst_ref=land_ref.at[4 * px + 2 * py + pc],
                send_sem=send_sems.at[k - 1], recv_sem=recv_sems.at[k - 1],
                device_id=(px, py, pc), device_id_type=MESH_ID)
            copy.wait_send()
            copy.wait_recv()

    return pl.pallas_call(
        body, name=name,
        out_shape=(pltpu.HBM(src_thru.shape, src_thru.dtype), pltpu.HBM(land_thru.shape, land_thru.dtype)),
        in_specs=(HBM_SPEC, HBM_SPEC, SEM_SPEC, SEM_SPEC, ANY),
        out_specs=(HBM_SPEC, HBM_SPEC), input_output_aliases={0: 0, 1: 1},
        compiler_params=pltpu.CompilerParams(has_side_effects=SPLIT_EFFECT),
    )(src_thru, land_thru, send_sems, recv_sems, after)


def _share_refs(ref, k, x, y, c, sender_c):
    px, py = ([(x, y)] + _other_chips(x, y))[k]
    return ref.at[:, 4 * px + 2 * py + sender_c]


def _share_start(gathered, after, name):
    def body(g_ref, after_ref, send_sems, recv_sems, g_thru, token):
        x, y, c = _coords()
        for k in range(4):
            slot = _share_refs(g_ref, k, x, y, c, c)
            pltpu.make_async_remote_copy(
                src_ref=slot, dst_ref=slot, send_sem=send_sems.at[k], recv_sem=recv_sems.at[k],
                device_id=(x, y, 1 - c), device_id_type=MESH_ID).start()
        token[...] = jnp.zeros_like(token)

    return pl.pallas_call(
        body, name=name,
        out_shape=(pltpu.SemaphoreType.DMA((4,)), pltpu.SemaphoreType.DMA((4,)),
                   pltpu.HBM(gathered.shape, gathered.dtype), jax.ShapeDtypeStruct((8, 128), F32)),
        in_specs=(HBM_SPEC, ANY),
        out_specs=(SEM_SPEC, SEM_SPEC, HBM_SPEC, pl.BlockSpec(memory_space=pltpu.VMEM)),
        input_output_aliases={0: 2},
        compiler_params=pltpu.CompilerParams(has_side_effects=SPLIT_EFFECT),
    )(pltpu.with_memory_space_constraint(gathered, pltpu.HBM), after)


def _share_wait(started, after, name):
    send_sems, recv_sems, g_thru, _ = started

    def body(g_ref, send_sems, recv_sems, after_ref, got_ref):
        x, y, c = _coords()
        for k in range(4):
            copy = pltpu.make_async_remote_copy(
                src_ref=_share_refs(g_ref, k, x, y, c, c), dst_ref=_share_refs(g_ref, k, x, y, c, 1 - c),
                send_sem=send_sems.at[k], recv_sem=recv_sems.at[k],
                device_id=(x, y, 1 - c), device_id_type=MESH_ID)
            copy.wait_send()
            copy.wait_recv()

    return pl.pallas_call(
        body, name=name,
        out_shape=pltpu.HBM(g_thru.shape, g_thru.dtype),
        in_specs=(HBM_SPEC, SEM_SPEC, SEM_SPEC, ANY),
        out_specs=HBM_SPEC, input_output_aliases={0: 0},
        compiler_params=pltpu.CompilerParams(has_side_effects=SPLIT_EFFECT),
    )(g_thru, send_sems, recv_sems, after)


def _core_share(own, gathered, name):
    def body(own_ref, gin_ref, out_ref, stage, send_sems, recv_sems, local_sem):
        x, y, c = _coords()
        sibling = (x, y, 1 - c)
        chips = [(x, y)] + _other_chips(x, y)
        stage_in = pltpu.make_async_copy(own_ref, stage, local_sem)
        stage_in.start()
        sent, arriving = [], []
        for k, (px, py) in enumerate(chips):
            slot = out_ref.at[:, 4 * px + 2 * py + c]
            sent.append(pltpu.make_async_remote_copy(
                src_ref=own_ref if k == 0 else slot, dst_ref=slot,
                send_sem=send_sems.at[k], recv_sem=recv_sems.at[k], device_id=sibling, device_id_type=MESH_ID))
            arriving.append(pltpu.make_async_remote_copy(
                src_ref=own_ref, dst_ref=out_ref.at[:, 4 * px + 2 * py + (1 - c)],
                send_sem=send_sems.at[k], recv_sem=recv_sems.at[k], device_id=sibling, device_id_type=MESH_ID))
        for cp in sent:
            cp.start()
        stage_in.wait()
        stage_out = pltpu.make_async_copy(stage, out_ref.at[:, 4 * x + 2 * y + c], local_sem)
        stage_out.start()
        for cp in arriving:
            cp.wait_recv()
        for cp in sent:
            cp.wait_send()
        stage_out.wait()

    return pl.pallas_call(
        body, name=name,
        out_shape=jax.ShapeDtypeStruct(gathered.shape, own.dtype),
        in_specs=[ANY, ANY], out_specs=ANY, input_output_aliases={1: 0},
        scratch_shapes=[pltpu.VMEM(own.shape, own.dtype), pltpu.SemaphoreType.DMA((4,)),
                        pltpu.SemaphoreType.DMA((4,)), pltpu.SemaphoreType.DMA],
    )(own, gathered)


def _adamw(w, g, m, v):
    m = ADAM_B1 * m + (1.0 - ADAM_B1) * g
    v = ADAM_B2 * v + (1.0 - ADAM_B2) * (g * g)
    m_hat = m / ADAM_C1
    v_hat = v / ADAM_C2
    delta = -ADAM_LR * (m_hat / (jnp.sqrt(v_hat) + ADAM_EPS) + ADAM_WD * w)
    return delta, m, v


def _adam_big(units, chip_idx, tag, token):
    n = len(units)
    r, wd = units[0][2].shape
    tr, tw = _row_tile(r, 1024), 256

    def body(s_ref, tok_ref, *refs):
        for u in range(n):
            p_ref, l_ref, w_ref, m_ref, v_ref = refs[5 * u:5 * u + 5]
            g_ref, d_ref, mo_ref, vo_ref = refs[5 * n + 4 * u:5 * n + 4 * u + 4]
            g = p_ref[0].astype(F32)
            for j in range(3):
                g = g + l_ref[j].astype(F32)
            delta, mn, vn = _adamw(w_ref[...], g, m_ref[...], v_ref[...])
            g_ref[...] = g
            d_ref[...] = delta
            mo_ref[...] = mn
            vo_ref[...] = vn

    row = pl.BlockSpec((tr, tw), lambda i, j, s: (i, j))
    unit_specs = [pl.BlockSpec((1, tr, tw), lambda i, j, s: (s[0], i, j)),
                  pl.BlockSpec((3, tr, tw), lambda i, j, s: (0, i, j)), row, row, row]
    outs = pl.pallas_call(
        body, name="adam_big_" + tag,
        grid_spec=pltpu.PrefetchScalarGridSpec(
            num_scalar_prefetch=1, grid=(r // tr, wd // tw),
            in_specs=[pl.BlockSpec((8, 128), lambda i, j, s: (0, 0))] + unit_specs * n,
            out_specs=[row] * (4 * n)),
        out_shape=[jax.ShapeDtypeStruct((r, wd), F32)] * (4 * n),
        compiler_params=_cparams(),
    )(chip_idx, token, *[a for unit in units for a in unit])
    return [outs[4 * u:4 * u + 4] for u in range(n)]


def _adam_small(parts, w, m, v):
    _, r, wd = parts.shape

    def body(p_ref, w_ref, m_ref, v_ref, g_ref, d_ref, mo_ref, vo_ref):
        g = p_ref[0]
        for k in range(1, N_DEV):
            g = g + p_ref[k]
        delta, mn, vn = _adamw(w_ref[...], g, m_ref[...], v_ref[...])
        g_ref[...] = g
        d_ref[...] = delta
        mo_ref[...] = mn
        vo_ref[...] = vn

    return pl.pallas_call(
        body, name="adam_small",
        out_shape=[jax.ShapeDtypeStruct((r, wd), F32)] * 4,
        compiler_params=_cparams(),
    )(parts, w, m, v)


def _pad_rows(a, rows):
    return jnp.pad(a, ((0, rows - a.shape[0]), (0, 0)))


def _pad_w_in(w):
    cut = Q_RANK + KV_RANK + ROPE
    return jnp.concatenate([w[:, :cut], jnp.zeros((w.shape[0], 64), w.dtype), w[:, cut:]], axis=1)


def _unpad_w_in(w):
    cut = Q_RANK + KV_RANK + ROPE
    return jnp.concatenate([w[:, :cut], w[:, cut + 64:]], axis=1)


def _pack_mid(p):
    parts = [_pad_w_in(p["w_in"][0]), p["w_out"][0], p["w_mq"][0], p["w_mo"][0],
             p["w_mkv"][0].reshape(256, D_MODEL),
             _pad_rows(p["w_q_up"][0].T.reshape(24, D_MODEL), 32),
             p["w_kv_up"][0].reshape(16, D_MODEL)]
    return jnp.concatenate(parts, axis=0)


def _pack_ffn(w_gate, w_up, w_down, name):
    d, rows = w_gate.shape[1:]

    def body(g_ref, u_ref, d_ref, o_ref):
        eye = (lax.broadcasted_iota(jnp.int32, (d, d), 0) == lax.broadcasted_iota(jnp.int32, (d, d), 1)).astype(BF16)
        o_ref[0] = _dot_tn(g_ref[0].astype(BF16), eye).astype(BF16)
        o_ref[1] = _dot_tn(u_ref[0].astype(BF16), eye).astype(BF16)
        o_ref[2] = d_ref[0].astype(BF16)

    return pl.pallas_call(
        body, name=name, out_shape=jax.ShapeDtypeStruct((3, rows, d), BF16), compiler_params=_cparams(),
    )(w_gate, w_up, w_down)


def _pack_segments(p, group):
    if group == "mid":
        return _pack_mid(p)[None].astype(BF16)
    return _pack_ffn(p[group + "_w_gate"], p[group + "_w_up"], p[group + "_w_down"], "pack_" + group)


UNIT_WEIGHT = {"ffn1_g": ("ffn1_w_gate", True), "ffn1_u": ("ffn1_w_up", True), "ffn1_d": ("ffn1_w_down", False),
               "ffn2_g": ("ffn2_w_gate", True), "ffn2_u": ("ffn2_w_up", True), "ffn2_d": ("ffn2_w_down", False)}


def _pack_unit(p, unit):
    if unit == "mid":
        return _pack_mid(p)
    name, transposed = UNIT_WEIGHT[unit]
    return p[name][0].T if transposed else p[name][0]


def _unpack_unit(a, unit):
    if unit != "mid":
        name, transposed = UNIT_WEIGHT[unit]
        return {name: (a.T if transposed else a)[None]}
    seg = lambda n: a[SEG_OFF[n][0]:SEG_OFF[n][0] + SEG_OFF[n][1]]
    return {"w_in": _unpad_w_in(seg("w_in"))[None], "w_out": seg("w_out")[None], "w_mq": seg("w_mq")[None],
            "w_mo": seg("w_mo")[None], "w_mkv": seg("w_mkv").reshape(D_MODEL, 256)[None],
            "w_q_up": seg("w_q")[:24].reshape(96, Q_RANK).T[None],
            "w_kv_up": seg("w_kv").reshape(KV_RANK, 128)[None]}


def _unpack_gathered(full, group):
    if group != "mid":
        return {group: full.reshape(len(GROUP_SEGS[group]), -1, D_MODEL)}
    full = full[0]
    seg = lambda n: full[:, SEG_OFF[n][0]:SEG_OFF[n][0] + SEG_OFF[n][1]]
    rows = lambda n: seg(n).reshape(-1, D_MODEL)
    wq_t = seg("w_q")[:, :24].reshape(MLA_HEADS, NOPE + ROPE, Q_RANK)
    wq_t = jnp.pad(wq_t, ((0, 0), (0, HEAD_PAD - NOPE - ROPE), (0, 0))).reshape(MLA_HEADS * HEAD_PAD, Q_RANK)
    wkv = seg("w_kv").reshape(N_DEV, KV_RANK, 128).transpose(1, 0, 2).reshape(KV_RANK, N_DEV * 128)
    return {"w_in": rows("w_in"), "w_out": rows("w_out"), "w_mq": rows("w_mq"), "w_mo": rows("w_mo"),
            "w_mkv": seg("w_mkv").reshape(N_DEV, D_MODEL, 256), "w_q": wq_t, "w_kv": wkv}


def _pack_grads(gr):
    blk = lambda a: a.reshape(N_DEV, -1, D_MODEL)
    dwq = gr["w_q"].reshape(MLA_HEADS, HEAD_PAD, Q_RANK)[:, :NOPE + ROPE].reshape(N_DEV, 24, D_MODEL)
    dwq = jnp.pad(dwq, ((0, 0), (0, 8), (0, 0)))
    dwkv = gr["w_kv"].reshape(KV_RANK, N_DEV, 128).transpose(1, 0, 2).reshape(N_DEV, 16, D_MODEL)
    parts = [blk(gr["w_in"]), blk(gr["w_out"]), blk(gr["w_mq"]), blk(gr["w_mo"]),
             gr["w_mkv"].reshape(N_DEV, 256, D_MODEL), dwq, dwkv]
    return jnp.concatenate([a.astype(BF16) for a in parts], axis=1)


def _pack_small(vals):
    parts = []
    for n, r in SMALL_ROWS:
        parts.append(_pad_rows(vals[n].reshape(-1, 128), r) if n in vals else jnp.zeros((r, 128), F32))
    return jnp.concatenate(parts, axis=0)


def _unpack_small(a, shapes):
    out = {}
    for n, shape in shapes.items():
        o = SMALL_OFF[n][0]
        out[n] = a[o:o + int(np.prod(shape)) // 128].reshape(shape)
    return out


BIG_NAMES = ("ffn1_w_gate", "ffn1_w_up", "ffn1_w_down", "w_in", "w_q_up", "w_kv_up", "w_out", "w_mq", "w_mkv",
             "w_mo", "ffn2_w_gate", "ffn2_w_up", "ffn2_w_down")
SMALL_NAMES = ("ffn1_norm", "mix_norm", "q_norm", "kv_norm", "pool_w", "pool_scale", "xattn_norm", "mem_norm",
               "ffn2_norm", "final_norm")
WEIGHT_ORDER = ("ffn1_norm", "ffn1_w_gate", "ffn1_w_up", "ffn1_w_down", "mix_norm", "w_in", "q_norm", "w_q_up",
                "kv_norm", "w_kv_up", "pool_w", "pool_scale", "w_out", "xattn_norm", "mem_norm", "w_mq", "w_mkv",
                "w_mo", "ffn2_norm", "ffn2_w_gate", "ffn2_w_up", "ffn2_w_down", "final_norm")


def _rope_table():
    lane = np.arange(128)
    freqs = (1.0 / (ROPE_BASE ** (np.arange(0, ROPE, 2, dtype=np.float32) / ROPE))).astype(np.float32)
    tab = np.zeros((8, 128), np.float32)
    tab[0] = np.where(lane < ROPE, freqs[lane % (ROPE // 2)], 0.0)
    tab[1] = np.where(lane < ROPE // 2, -1.0, np.where(lane < ROPE, 1.0, 0.0))
    return jnp.asarray(tab)


def kernel(x, mem, positions, ffn1_norm, ffn1_w_gate, ffn1_w_up, ffn1_w_down, mix_norm, w_in, q_norm, w_q_up, kv_norm, w_kv_up, pool_w, pool_scale, w_out, xattn_norm, mem_norm, w_mq, w_mkv, w_mo, ffn2_norm, ffn2_w_gate, ffn2_w_up, ffn2_w_down, final_norm, loss_target, m_ffn1_norm, m_ffn1_w_gate, m_ffn1_w_up, m_ffn1_w_down, m_mix_norm, m_w_in, m_q_norm, m_w_q_up, m_kv_norm, m_w_kv_up, m_pool_w, m_pool_scale, m_w_out, m_xattn_norm, m_mem_norm, m_w_mq, m_w_mkv, m_w_mo, m_ffn2_norm, m_ffn2_w_gate, m_ffn2_w_up, m_ffn2_w_down, m_final_norm, v_ffn1_norm, v_ffn1_w_gate, v_ffn1_w_up, v_ffn1_w_down, v_mix_norm, v_w_in, v_q_norm, v_w_q_up, v_kv_norm, v_w_kv_up, v_pool_w, v_pool_scale, v_w_out, v_xattn_norm, v_mem_norm, v_w_mq, v_w_mkv, v_w_mo, v_ffn2_norm, v_ffn2_w_gate, v_ffn2_w_up, v_ffn2_w_down, v_final_norm):
    wts = dict(ffn1_norm=ffn1_norm, ffn1_w_gate=ffn1_w_gate, ffn1_w_up=ffn1_w_up, ffn1_w_down=ffn1_w_down,
               mix_norm=mix_norm, w_in=w_in, q_norm=q_norm, w_q_up=w_q_up, kv_norm=kv_norm, w_kv_up=w_kv_up,
               pool_w=pool_w, pool_scale=pool_scale, w_out=w_out, xattn_norm=xattn_norm, mem_norm=mem_norm,
               w_mq=w_mq, w_mkv=w_mkv, w_mo=w_mo, ffn2_norm=ffn2_norm, ffn2_w_gate=ffn2_w_gate,
               ffn2_w_up=ffn2_w_up, ffn2_w_down=ffn2_w_down, final_norm=final_norm)
    mom = dict(ffn1_norm=m_ffn1_norm, ffn1_w_gate=m_ffn1_w_gate, ffn1_w_up=m_ffn1_w_up, ffn1_w_down=m_ffn1_w_down,
               mix_norm=m_mix_norm, w_in=m_w_in, q_norm=m_q_norm, w_q_up=m_w_q_up, kv_norm=m_kv_norm,
               w_kv_up=m_w_kv_up, pool_w=m_pool_w, pool_scale=m_pool_scale, w_out=m_w_out, xattn_norm=m_xattn_norm,
               mem_norm=m_mem_norm, w_mq=m_w_mq, w_mkv=m_w_mkv, w_mo=m_w_mo, ffn2_norm=m_ffn2_norm,
               ffn2_w_gate=m_ffn2_w_gate, ffn2_w_up=m_ffn2_w_up, ffn2_w_down=m_ffn2_w_down, final_norm=m_final_norm)
    var = dict(ffn1_norm=v_ffn1_norm, ffn1_w_gate=v_ffn1_w_gate, ffn1_w_up=v_ffn1_w_up, ffn1_w_down=v_ffn1_w_down,
               mix_norm=v_mix_norm, w_in=v_w_in, q_norm=v_q_norm, w_q_up=v_w_q_up, kv_norm=v_kv_norm,
               w_kv_up=v_w_kv_up, pool_w=v_pool_w, pool_scale=v_pool_scale, w_out=v_w_out, xattn_norm=v_xattn_norm,
               mem_norm=v_mem_norm, w_mq=v_w_mq, w_mkv=v_w_mkv, w_mo=v_w_mo, ffn2_norm=v_ffn2_norm,
               ffn2_w_gate=v_ffn2_w_gate, ffn2_w_up=v_ffn2_w_up, ffn2_w_down=v_ffn2_w_down, final_norm=v_final_norm)

    t = x.shape[1]
    xs = x[0]
    mems = mem[0]
    target = loss_target[0]
    pos = positions.reshape(t, 1)
    row = lambda a: a.reshape(1, -1)
    rope_tab = _rope_table()

    cx, cy, cc = _coords()
    chip_idx = (2 * cx + cy).astype(jnp.int32).reshape(1)

    wb = {}
    for grp in ("ffn1", "mid", "ffn2"):
        wb[grp] = _pack_segments(wts, grp)
        if grp == "ffn1":
            near_ffn1 = _near_start(wb["ffn1"], pos, "ag_ffn1_near_start")
    mid_names = ("w_in", "w_out", "w_mq", "w_mo", "w_mkv", "w_q_up", "w_kv_up")

    states = (wts, mom, var)

    def packed_during(token, units, after, states):
        one = 1.0 + token[0, 0]
        packs = {}
        for u in units:
            names = SMALL_NAMES if u == "small" else mid_names if u == "mid" else UNIT_WEIGHT[u][:1]
            held = [{n: p[n] * one for n in names} for p in states]
            packs[u] = tuple(_pack_small(h) if u == "small" else _pack_unit(h, u) for h in held)
        return lax.optimization_barrier((after, packs))

    after, adam_in = packed_during(near_ffn1[4], ("ffn1_g", "ffn1_u", "ffn1_d", "ffn2_g", "ffn2_u", "ffn2_d"),
                                   wb["ffn2"], states)
    after, mid_w = packed_during(near_ffn1[4], ("mid",), after, states[:1])
    own_ffn1, land_ffn1 = _near_wait(near_ffn1, after, "ag_ffn1_near_wait")
    far_ffn1 = _far_start(land_ffn1, own_ffn1, "ag_ffn1_far_start")
    after, more = packed_during(far_ffn1[3], ("small",), wb["mid"], states)
    after, mid_mv = packed_during(far_ffn1[3], ("mid",), after, states[1:])
    adam_in.update(more)
    adam_in["mid"] = mid_w["mid"] + mid_mv["mid"]
    land_ffn1 = _far_wait(far_ffn1, after, "ag_ffn1_far_wait")
    full_ffn1 = _core_share(own_ffn1, land_ffn1, "ag_ffn1_share")
    fw = _unpack_gathered(full_ffn1, "ffn1")
    ag_mid = _ici_start(wb["mid"], full_ffn1, "ag_mid_start", True)
    g_ffn1, g_mix, g_q, g_kv = row(ffn1_norm), row(mix_norm), row(q_norm), row(kv_norm)
    g_x, g_mem, g_ffn2, g_fin = row(xattn_norm), row(mem_norm), row(ffn2_norm), row(final_norm)
    pool_wb = pool_w[0].astype(BF16)
    pool_sc = row(pool_scale)

    h1, n1, gate1, up1 = _ffn_fwd(xs, g_ffn1, fw["ffn1"], "ffn1_fwd", token=ag_mid[4])
    own_mid, land_mid = _ici_wait(ag_mid, h1, "ag_mid_wait", True)
    full_mid = _core_share(own_mid, land_mid, "ag_mid_share")
    fw.update(_unpack_gathered(full_mid, "mid"))
    ag_ffn2 = _ici_start(wb["ffn2"], full_mid, "ag_ffn2_start", True)
    u, z, qn, kvn, qh, kh, vh = _mix_prep(h1, g_mix, fw["w_in"], g_q, fw["w_q"], g_kv, fw["w_kv"], pos, rope_tab,
                                          token=ag_ffn2[4])
    a, lse = _attn_fwd(qh, kh, vh)
    p = _pool_fwd(z, pool_wb, pool_sc)
    memn, km, vm = _mem_kv(mems, g_mem, fw["w_mkv"])
    own_ffn2, land_ffn2 = _ici_wait(ag_ffn2, a, "ag_ffn2_wait", True)
    land_ffn2 = lax.dynamic_update_slice(land_ffn2, own_ffn2[:, None], (0, 4 * cx + 2 * cy + cc, 0, 0))
    share_ffn2 = _share_start(land_ffn2, a, "ag_ffn2_share_start")
    h2, h3, hn, qm, om = _xattn_fwd(h1, a, p, fw["w_out"], g_x, fw["w_mq"], km, vm, fw["w_mo"], token=share_ffn2[3])
    fw.update(_unpack_gathered(_share_wait(share_ffn2, h3, "ag_ffn2_share_wait"), "ffn2"))
    dh4, n2, gate2, up2, loss_part, dg_fin = _ffn_fwd(h3, g_ffn2, fw["ffn2"],
                                                      "ffn2_fwd", head=(target, g_fin))

    def reduce_start(g8, unit):
        part = _core_reduce(g8, unit)
        return _ici_start(part, g8, "rs_" + unit + "_start", False)

    def by_device(g):
        return g.reshape(N_DEV, -1, D_MODEL)

    rs = {}
    dh3, dgate2, dup2, act2, dg_ffn2 = _ffn_bwd_data(dh4, h3, g_ffn2, gate2, up2, fw["ffn2"], "ffn2_bwd")
    rs["ffn2_g"] = reduce_start(by_device(_tn_matmul(dgate2, n2, "ffn2_dwg", tmm=1408, m=D_FF, out_dtype=BF16)), "ffn2_g")
    rs["ffn2_u"] = reduce_start(by_device(_tn_matmul(dup2, n2, "ffn2_dwu", tmm=1408, m=D_FF, out_dtype=BF16,
                                                     token=rs["ffn2_g"][4])), "ffn2_u")
    rs["ffn2_d"] = reduce_start(by_device(_tn_matmul(act2, dh4, "ffn2_dwd", scale=0.5, tmm=1408, m=D_FF, out_dtype=BF16,
                                                     token=rs["ffn2_u"][4])), "ffn2_d")
    dh2, dqm, da, dp, dkm, dvm, dg_x = _xattn_bwd(dh3, h2, qm, g_x, fw["w_mq"], km, vm, fw["w_mo"], fw["w_out"],
                                                  token=rs["ffn2_d"][4])
    gr = {}
    gr["w_mo"] = _tn_matmul(om, dh3, "dw_mo", out_dtype=BF16)
    gr["w_mq"] = _tn_matmul(hn, dqm, "dw_mq", out_dtype=BF16)
    gr["w_out"] = jnp.concatenate([_tn_matmul(a, dh2, "dw_out_a", out_dtype=BF16),
                                   _tn_matmul(p, dh2, "dw_out_p", out_dtype=BF16)], axis=0)
    gr["w_mkv"], dg_mem = _mem_kv_bwd(dkm, dvm, memn, mems, g_mem, fw["w_mkv"])
    dz_pool, d_pool_w, d_pool_sc = _pool_bwd(dp, z, pool_wb, pool_sc)
    dqh, dkh, dvh = _attn_bwd(qh, kh, vh, da, lse, _attn_delta(a, da))
    dh1, dq, dkv, dz, dg_q, dg_kv, dg_mix = _mla_bwd(dqh, dkh, dvh, z, dz_pool, h1, dh2, g_mix, fw["w_in"], g_q,
                                                     fw["w_q"], g_kv, fw["w_kv"], pos, rope_tab)
    gr["w_q"] = _tn_matmul(dq, qn, "dw_q", out_dtype=BF16)
    gr["w_kv"] = _tn_matmul(kvn, dkv, "dw_kv", out_dtype=BF16)
    gr["w_in"] = _tn_matmul(u, dz, "dw_in", out_dtype=BF16)
    g_mid = _pack_grads(gr)
    part_mid = _core_reduce(g_mid, "mid")
    got = {}
    after = part_mid
    for unit in ("ffn2_g", "ffn2_u", "ffn2_d"):
        got[unit] = _ici_wait(rs[unit], after, "rs_" + unit + "_wait", False)
        after = got[unit][1]
    rs["mid"] = _ici_start(part_mid, after, "rs_mid_start", False)
    dx, dgate1, dup1, act1, dg_ffn1 = _ffn_bwd_data(dh1, xs, g_ffn1, gate1, up1, fw["ffn1"], "ffn1_bwd", token=rs["mid"][4])
    got["mid"] = _ici_wait(rs["mid"], dx, "rs_mid_wait", False)

    small_g = dict(ffn1_norm=dg_ffn1, mix_norm=dg_mix, q_norm=dg_q, kv_norm=dg_kv, pool_w=d_pool_w,
                   pool_scale=d_pool_sc, xattn_norm=dg_x, mem_norm=dg_mem, ffn2_norm=dg_ffn2, final_norm=dg_fin,
                   loss=loss_part)
    small_ag = _peers_start(_pack_small(small_g), got["mid"][1], "small_ag_start")
    rs["ffn1_g"] = reduce_start(by_device(_tn_matmul(dgate1, n1, "ffn1_dwg", tmm=1408, m=D_FF, out_dtype=BF16,
                                                     token=small_ag[4])), "ffn1_g")
    _, parts = _peers_wait(small_ag, rs["ffn1_g"][4], "small_ag_wait")
    small = _adam_small(parts, *adam_in["small"])
    small_sum = small[0]
    loss = small_sum[SMALL_OFF["loss"][0], 0]
    shapes = {n: wts[n].shape for n in SMALL_NAMES}
    small = [_unpack_small(s, shapes) for s in small]

    rs["ffn1_u"] = reduce_start(by_device(_tn_matmul(dup1, n1, "ffn1_dwu", tmm=1408, m=D_FF, out_dtype=BF16,
                                                     token=small_sum)), "ffn1_u")
    rs["ffn1_d"] = reduce_start(by_device(_tn_matmul(act1, dh1, "ffn1_dwd", scale=0.5, tmm=1408, m=D_FF, out_dtype=BF16,
                                                     token=rs["ffn1_u"][4])), "ffn1_d")

    big = {}

    def adam_units(names, token):
        units = [got[u] + adam_in[u] for u in names]
        res = _adam_big(units, chip_idx, "_".join(names), token)
        for u, four in zip(names, res):
            for k, packed in enumerate(four):
                big.setdefault(k, {}).update(_unpack_unit(packed, u))
        return res[-1][0]

    done = adam_units(["mid"], rs["ffn1_d"][4])
    done = adam_units(["ffn2_g", "ffn2_u", "ffn2_d"], done)
    got["ffn1_g"] = _ici_wait(rs["ffn1_g"], done, "rs_ffn1_g_wait", False)
    got["ffn1_u"] = _ici_wait(rs["ffn1_u"], got["ffn1_g"][1], "rs_ffn1_u_wait", False)
    done = adam_units(["ffn1_g", "ffn1_u"], done)
    got["ffn1_d"] = _ici_wait(rs["ffn1_d"], done, "rs_ffn1_d_wait", False)
    adam_units(["ffn1_d"], done)

    outs = [loss, dx[None]]
    for k in range(4):
        for n in WEIGHT_ORDER:
            outs.append(big[k][n] if n in BIG_NAMES else small[k][n])
    return tuple(outs)
```

```python
import numpy as np

import jax
import jax.numpy as jnp
from jax import lax
from jax.experimental import pallas as pl
from jax.experimental.pallas import tpu as pltpu

F32 = jnp.float32
BF16 = jnp.bfloat16

N_DEV = 8
D_MODEL = 1024
D_FF = 2816
MLA_HEADS = 4
NOPE = 128
ROPE = 64
HEAD_PAD = 256
V_DIM = 128
Q_RANK = 256
KV_RANK = 128
POOL_WINDOWS = (2, 4, 8, 16)
POOL_CH = 128
POOL_HALO = 16
N_MEM = 256
MEM_HEADS = 4
MEM_HD = 256
ROPE_BASE = 10000.0
RMS_EPS = 1e-6
ATTN_SCALE = (NOPE + ROPE) ** -0.5
MEM_SCALE = MEM_HD ** -0.5
NEG_BIG = -1e30

ADAM_LR = 0.001
ADAM_B1 = 0.9
ADAM_B2 = 0.999
ADAM_EPS = 1e-08
ADAM_WD = 0.01
ADAM_STEP = 10
ADAM_C1 = 1.0 - ADAM_B1 ** ADAM_STEP
ADAM_C2 = 1.0 - ADAM_B2 ** ADAM_STEP

VMEM_LIMIT_BYTES = 56 * 1024 * 1024
BF16_ROWS = 16

GROUP_SEGS = {
    "ffn1": (("ffn1_g", 352), ("ffn1_u", 352), ("ffn1_d", 352)),
    "mid": (("w_in", 128), ("w_out", 128), ("w_mq", 128), ("w_mo", 128), ("w_mkv", 256), ("w_q", 32), ("w_kv", 16)),
    "ffn2": (("ffn2_g", 352), ("ffn2_u", 352), ("ffn2_d", 352)),
}
SEG_OFF = {}
GROUP_ROWS = {}
for _g, _segs in GROUP_SEGS.items():
    _o = 0
    for _n, _r in _segs:
        SEG_OFF[_n] = (_o, _r)
        _o += _r
    GROUP_ROWS[_g] = _o

SMALL_ROWS = (("ffn1_norm", 8), ("mix_norm", 8), ("q_norm", 8), ("kv_norm", 8), ("pool_w", 512), ("pool_scale", 8),
              ("xattn_norm", 8), ("mem_norm", 8), ("ffn2_norm", 8), ("final_norm", 8), ("loss", 8))
SMALL_OFF = {}
_o = 0
for _n, _r in SMALL_ROWS:
    SMALL_OFF[_n] = (_o, _r)
    _o += _r


def _cparams(**kw):
    return pltpu.CompilerParams(vmem_limit_bytes=VMEM_LIMIT_BYTES, **kw)


def _row_tile(rows, limit):
    best = None
    for cand in range(BF16_ROWS, min(rows, limit) + 1, BF16_ROWS):
        if rows % cand == 0:
            best = cand
    assert best is not None, rows
    return best


def _dot_nn(a, b):
    return lax.dot_general(a, b, (((1,), (0,)), ((), ())), preferred_element_type=F32)


def _dot_nt(a, b):
    return lax.dot_general(a, b, (((1,), (1,)), ((), ())), preferred_element_type=F32)


def _dot_tn(a, b):
    return lax.dot_general(a, b, (((0,), (0,)), ((), ())), preferred_element_type=F32)


def _rms_fwd(x, g):
    r = lax.rsqrt(jnp.mean(x * x, axis=-1, keepdims=True) + RMS_EPS)
    return x * r * g, r


def _rms_bwd(dy, x, g, r):
    xhat = x * r
    dyg = dy * g
    dx = r * (dyg - xhat * jnp.mean(dyg * xhat, axis=-1, keepdims=True))
    dg = jnp.sum(dy * xhat, axis=0, keepdims=True)
    return dx, dg


def _accumulate(ref, val, first):
    if isinstance(first, bool):
        if first:
            ref[...] = val
        else:
            ref[...] += val
        return

    @pl.when(first)
    def _():
        ref[...] = val

    @pl.when(jnp.logical_not(first))
    def _():
        ref[...] += val


def _call_after(token, body, in_specs, args, **kw):
    if token is not None:
        inner = body
        body = lambda tok_ref, *refs: inner(*refs)
        in_specs = [pl.BlockSpec((8, 128), lambda *_: (0, 0))] + list(in_specs)
        args = (token,) + tuple(args)
    return pl.pallas_call(body, in_specs=in_specs, **kw)(*args)


def _resident(shape):
    return pl.BlockSpec(shape, lambda *_: (0,) * len(shape), pipeline_mode=pl.Buffered(1))


def _rope_tables(pos_col, tab):
    ang = pos_col.astype(F32) * tab[0:1, :]
    return jnp.cos(ang), jnp.sin(ang) * tab[1:2, :]


def _swap_halves(x):
    lane = lax.broadcasted_iota(jnp.int32, x.shape, 1)
    return jnp.where((lane % 64) < 32, pltpu.roll(x, 96, 1), pltpu.roll(x, 32, 1))


def _rope_apply(x, cos_t, sin_t):
    return x * cos_t + _swap_halves(x) * sin_t


def _rope_apply_t(dy, cos_t, sin_t):
    return dy * cos_t + _swap_halves(dy * sin_t)


def _ffn_fwd(h, g, w, name, token=None, head=None):
    t, d = h.shape
    f = w.shape[1]
    tm, tf = min(512, t), 256
    nf = f // tf
    n_in = 3 if head is None else 5

    def body(*refs):
        h_ref, g_ref, w_ref = refs[:3]
        ho_ref, n_ref, gate_ref, up_ref = refs[n_in:n_in + 4]
        nb_sc, acc_sc = refs[-2:]
        y, _ = _rms_fwd(h_ref[...], g_ref[...])
        nb = y.astype(BF16)
        nb_sc[...] = nb
        n_ref[...] = nb
        acc_sc[...] = jnp.zeros_like(acc_sc)

        def f_tile(j):
            rows = pl.ds(pl.multiple_of(j * tf, tf), tf)
            nb = nb_sc[...]
            gt = _dot_nt(nb, w_ref[0, rows, :])
            ut = _dot_nt(nb, w_ref[1, rows, :])
            gate_ref[j] = gt.astype(BF16)
            up_ref[j] = ut.astype(BF16)
            act = (gt * jax.nn.sigmoid(gt)) * ut
            return _dot_nn(act.astype(BF16), w_ref[2, rows, :])

        def pair(p, carry):
            acc_sc[...] += f_tile(2 * p) + f_tile(2 * p + 1)
            return carry

        lax.fori_loop(0, nf // 2, pair, 0)
        if nf % 2:
            acc_sc[...] += f_tile(nf - 1)
        ho = h_ref[...] + 0.5 * acc_sc[...]
        if head is None:
            ho_ref[...] = ho
            return
        t_ref, gf_ref = refs[3:5]
        loss_ref, dgf_ref = refs[n_in + 4:n_in + 6]
        gg = gf_ref[...]
        y, r = _rms_fwd(ho, gg)
        err = y - t_ref[...]
        part = 0.5 * jnp.sum(jnp.mean(err * err, axis=-1, keepdims=True), axis=0, keepdims=True)
        dx, dg = _rms_bwd(err * (1.0 / d), ho, gg, r)
        ho_ref[...] = dx
        first = pl.program_id(0) == 0
        _accumulate(loss_ref, jnp.broadcast_to(part, loss_ref.shape), first)
        _accumulate(dgf_ref, dg, first)

    row = pl.BlockSpec((tm, d), lambda i: (i, 0))
    tiles = pl.BlockSpec((nf, tm, tf), lambda i: (0, i, 0))
    in_specs = [row, _resident((1, d)), _resident(w.shape)]
    args = (h, g, w)
    out_specs = [row, row, tiles, tiles]
    out_shape = [jax.ShapeDtypeStruct((t, d), F32), jax.ShapeDtypeStruct((t, d), BF16),
                 jax.ShapeDtypeStruct((nf, t, tf), BF16), jax.ShapeDtypeStruct((nf, t, tf), BF16)]
    if head is not None:
        in_specs += [row, _resident((1, d))]
        args += tuple(head)
        out_specs += [pl.BlockSpec((8, 128), lambda i: (0, 0)), pl.BlockSpec((1, d), lambda i: (0, 0))]
        out_shape += [jax.ShapeDtypeStruct((8, 128), F32), jax.ShapeDtypeStruct((1, d), F32)]
    return _call_after(
        token, body, in_specs, args, name=name, grid=(t // tm,), out_specs=out_specs, out_shape=out_shape,
        scratch_shapes=[pltpu.VMEM((tm, d), BF16), pltpu.VMEM((tm, d), F32)],
        compiler_params=_cparams(),
    )


def _ffn_bwd_data(dho, h, g, gate, up, w, name, token=None):
    t, d = h.shape
    f = w.shape[1]
    tm, tf = min(1024, t), 256
    parts = 2 if tm % 512 == 0 else 1
    tp = tm // parts
    nf = f // tf
    npair, odd = nf // 2, nf % 2
    nsteps = npair + odd

    def body(dho_ref, h_ref, g_ref, gate_ref, up_ref, wg_ref, wu_ref, wd_ref,
             dh_ref, dgate_ref, dup_ref, act_ref, dg_ref, dhb_sc, acc_sc):
        i, j = pl.program_id(0), pl.program_id(1)

        @pl.when(j == 0)
        def _():
            dhb_sc[...] = (0.5 * dho_ref[...]).astype(BF16)
            acc_sc[...] = jnp.zeros_like(acc_sc)

        def slab(ntile):
            cols = pl.ds(0, ntile * tf)
            for r in range(parts):
                rows = pl.ds(r * tp, tp)
                gt = jnp.concatenate([gate_ref[k, rows, :] for k in range(ntile)], axis=-1).astype(F32)
                ut = jnp.concatenate([up_ref[k, rows, :] for k in range(ntile)], axis=-1).astype(F32)
                dact = _dot_nt(dhb_sc[rows, :], wd_ref[cols, :])
                sg = jax.nn.sigmoid(gt)
                silu = gt * sg
                dgb = (dact * ut * (sg * (1.0 + gt * (1.0 - sg)))).astype(BF16)
                dub = (dact * silu).astype(BF16)
                act_ref[rows, cols] = (silu * ut).astype(BF16)
                dgate_ref[rows, cols] = dgb
                dup_ref[rows, cols] = dub
                acc_sc[rows, :] += _dot_nn(dgb, wg_ref[cols, :]) + _dot_nn(dub, wu_ref[cols, :])

        pl.when(j < npair)(lambda: slab(2))
        if odd:
            pl.when(j == npair)(lambda: slab(1))

        @pl.when(j == nsteps - 1)
        def _():
            x = h_ref[...]
            gg = g_ref[...]
            _, r = _rms_fwd(x, gg)
            dx, dg = _rms_bwd(acc_sc[...], x, gg, r)
            dh_ref[...] = dho_ref[...] + dx
            _accumulate(dg_ref, dg, i == 0)

    row = pl.BlockSpec((tm, d), lambda i, j: (i, 0))
    acts = pl.BlockSpec((2, tm, tf), lambda i, j: (j, i, 0))
    weights = lambda k: pl.BlockSpec((None, 2 * tf, d), lambda i, j: (k, j, 0))
    outs = pl.BlockSpec((tm, 2 * tf), lambda i, j: (i, j))
    padded = jax.ShapeDtypeStruct((t, 2 * tf * nsteps), BF16)
    return _call_after(
        token, body,
        [row, row, pl.BlockSpec((1, d), lambda i, j: (0, 0)), acts, acts, weights(0), weights(1), weights(2)],
        (dho, h, g, gate, up, w, w, w),
        name=name, grid=(t // tm, nsteps),
        out_specs=[row, outs, outs, outs, pl.BlockSpec((1, d), lambda i, j: (0, 0))],
        out_shape=[jax.ShapeDtypeStruct((t, d), F32), padded, padded, padded, jax.ShapeDtypeStruct((1, d), F32)],
        scratch_shapes=[pltpu.VMEM((tm, d), BF16), pltpu.VMEM((tm, d), F32)],
        compiler_params=_cparams(),
    )


def _tn_matmul(a, b, name, scale=1.0, tmm=None, out_dtype=F32, token=None, m=None):
    t = a.shape[0]
    m = a.shape[1] if m is None else m
    n = b.shape[1]
    tmm = m if tmm is None else tmm
    tk = min(1024, t)
    nk = t // tk

    def product(a_ref, b_ref):
        prod = _dot_tn(a_ref[...].astype(BF16), b_ref[...].astype(BF16))
        return prod * scale if scale != 1.0 else prod

    def body_f32(a_ref, b_ref, o_ref):
        _accumulate(o_ref, product(a_ref, b_ref), pl.program_id(1) == 0)

    def body_cast(a_ref, b_ref, o_ref, acc_sc):
        k = pl.program_id(1)
        _accumulate(acc_sc, product(a_ref, b_ref), k == 0)

        @pl.when(k == nk - 1)
        def _():
            o_ref[...] = acc_sc[...].astype(out_dtype)

    direct = out_dtype == F32
    return _call_after(
        token, body_f32 if direct else body_cast,
        [pl.BlockSpec((tk, tmm), lambda i, k: (k, i)),
         pl.BlockSpec((tk, n), lambda i, k: (k, 0))],
        (a, b),
        name=name, grid=(m // tmm, nk),
        out_specs=pl.BlockSpec((tmm, n), lambda i, k: (i, 0)),
        out_shape=jax.ShapeDtypeStruct((m, n), out_dtype),
        scratch_shapes=[] if direct else [pltpu.VMEM((tmm, n), F32)],
        compiler_params=_cparams(),
    )


def _mix_prep(h1, mix_norm, w_in, q_norm, wq_t, kv_norm, wkv, pos, rope_tab, token=None):
    t, d = h1.shape
    tm = min(512, t)

    def body(h_ref, gm_ref, win_ref, gq_ref, wq_ref, gkv_ref, wkv_ref, pos_ref, tab_ref,
             u_ref, z_ref, qn_ref, kvn_ref, q_ref, k_ref, v_ref):
        u, _ = _rms_fwd(h_ref[...], gm_ref[...])
        ub = u.astype(BF16)
        u_ref[...] = ub
        z = _dot_nn(ub, win_ref[...])
        z_ref[...] = z
        cos_t, sin_t = _rope_tables(pos_ref[...], tab_ref[...])
        qn, _ = _rms_fwd(z[:, 0:Q_RANK], gq_ref[...])
        qnb = qn.astype(BF16)
        qn_ref[...] = qnb
        q = _dot_nt(qnb, wq_ref[...])
        kvn, _ = _rms_fwd(z[:, Q_RANK:Q_RANK + KV_RANK], gkv_ref[...])
        kvnb = kvn.astype(BF16)
        kvn_ref[...] = kvnb
        kv = _dot_nn(kvnb, wkv_ref[...])
        k_pe = _rope_apply(z[:, Q_RANK + KV_RANK:Q_RANK + KV_RANK + 128], cos_t, sin_t)
        ones = jnp.ones((tm, V_DIM), F32)
        for hh in range(MLA_HEADS):
            b = hh * HEAD_PAD
            q_pe = _rope_apply(q[:, b + NOPE:b + HEAD_PAD], cos_t, sin_t)
            q_ref[hh] = jnp.concatenate([q[:, b:b + NOPE], q_pe], axis=-1).astype(BF16)
            k_ref[hh] = jnp.concatenate([kv[:, b:b + NOPE], k_pe], axis=-1).astype(BF16)
            v_ref[hh] = jnp.concatenate([kv[:, b + NOPE:b + HEAD_PAD], ones], axis=-1).astype(BF16)

    full = lambda shape: pl.BlockSpec(shape, lambda i: (0,) * len(shape))
    return _call_after(
        token, body,
        [pl.BlockSpec((tm, d), lambda i: (i, 0)), _resident((1, d)), _resident(w_in.shape), _resident((1, Q_RANK)),
         _resident(wq_t.shape), _resident((1, KV_RANK)), _resident(wkv.shape),
         pl.BlockSpec((tm, 1), lambda i: (i, 0)), _resident(rope_tab.shape)],
        (h1, mix_norm, w_in, q_norm, wq_t, kv_norm, wkv, pos, rope_tab),
        name="mix_prep", grid=(t // tm,),
        out_specs=[pl.BlockSpec((tm, d), lambda i: (i, 0)),
                   pl.BlockSpec((tm, d), lambda i: (i, 0)),
                   pl.BlockSpec((tm, Q_RANK), lambda i: (i, 0)),
                   pl.BlockSpec((tm, KV_RANK), lambda i: (i, 0)),
                   pl.BlockSpec((MLA_HEADS, tm, HEAD_PAD), lambda i: (0, i, 0)),
                   pl.BlockSpec((MLA_HEADS, tm, HEAD_PAD), lambda i: (0, i, 0)),
                   pl.BlockSpec((MLA_HEADS, tm, 2 * V_DIM), lambda i: (0, i, 0))],
        out_shape=[jax.ShapeDtypeStruct((t, d), BF16), jax.ShapeDtypeStruct((t, d), F32),
                   jax.ShapeDtypeStruct((t, Q_RANK), BF16), jax.ShapeDtypeStruct((t, KV_RANK), BF16),
                   jax.ShapeDtypeStruct((MLA_HEADS, t, HEAD_PAD), BF16),
                   jax.ShapeDtypeStruct((MLA_HEADS, t, HEAD_PAD), BF16),
                   jax.ShapeDtypeStruct((MLA_HEADS, t, 2 * V_DIM), BF16)],
        compiler_params=_cparams(),
    )


def _causal_mask(s):
    row = lax.broadcasted_iota(jnp.int32, s.shape, 0)
    col = lax.broadcasted_iota(jnp.int32, s.shape, 1)
    return jnp.where(col <= row, s, NEG_BIG)


def _attn_fwd(q, k, v):
    nh, t, _ = q.shape
    tq = tk = min(512, t)
    nq, nk = t // tq, t // tk

    pairs = [(i, j) for i in range(nq) for j in range(i + 1)]
    qi = jnp.asarray(np.array([i for i, _ in pairs], np.int32))
    kj = jnp.asarray(np.array([j for _, j in pairs], np.int32))

    def body(qi_ref, kj_ref, q_ref, k_ref, v_ref, o_ref, lse_ref, m_sc, acc_sc):
        n = pl.program_id(0)
        i, j = qi_ref[n], kj_ref[n]

        @pl.when(j == 0)
        def _():
            m_sc[...] = jnp.full_like(m_sc, NEG_BIG)
            acc_sc[...] = jnp.zeros_like(acc_sc)

        def step(diagonal):
            for hh in range(nh):
                s = _dot_nt(q_ref[hh], k_ref[hh]) * ATTN_SCALE
                if diagonal:
                    s = _causal_mask(s)
                m_old = m_sc[hh]
                m_new = jnp.maximum(m_old, jnp.max(s, axis=-1, keepdims=True))
                p = jnp.exp(s - m_new).astype(BF16)
                acc_sc[hh] = jnp.exp(m_old - m_new) * acc_sc[hh] + _dot_nn(p, v_ref[hh])
                m_sc[hh] = m_new

        @pl.when(j < i)
        def _():
            step(False)

        @pl.when(j == i)
        def _():
            step(True)
            for hh in range(nh):
                acc = acc_sc[hh]
                l = acc[:, V_DIM:2 * V_DIM]
                o_ref[:, hh * V_DIM:(hh + 1) * V_DIM] = (acc[:, 0:V_DIM] / l).astype(BF16)
                lse_ref[hh] = m_sc[hh] + jnp.log(l[:, 0:1])

    q_map = lambda n, qi_ref, kj_ref: (0, qi_ref[n], 0)
    kv_map = lambda n, qi_ref, kj_ref: (0, kj_ref[n], 0)
    return pl.pallas_call(
        body, name="attn_fwd",
        grid_spec=pltpu.PrefetchScalarGridSpec(
            num_scalar_prefetch=2, grid=(len(pairs),),
            in_specs=[pl.BlockSpec((nh, tq, HEAD_PAD), q_map),
                      pl.BlockSpec((nh, tk, HEAD_PAD), kv_map),
                      pl.BlockSpec((nh, tk, 2 * V_DIM), kv_map)],
            out_specs=[pl.BlockSpec((tq, nh * V_DIM), lambda n, qi_ref, kj_ref: (qi_ref[n], 0)),
                       pl.BlockSpec((nh, tq, 1), q_map)],
            scratch_shapes=[pltpu.VMEM((nh, tq, 1), F32), pltpu.VMEM((nh, tq, 2 * V_DIM), F32)]),
        out_shape=[jax.ShapeDtypeStruct((t, nh * V_DIM), BF16), jax.ShapeDtypeStruct((nh, t, 1), F32)],
        compiler_params=_cparams(),
    )(qi, kj, q, k, v)


def _attn_delta(o, do):
    t, w = o.shape
    nh = w // V_DIM
    tm = min(512, t)

    def body(o_ref, do_ref, d_ref):
        prod = o_ref[...].astype(F32) * do_ref[...].astype(F32)
        for hh in range(nh):
            d_ref[hh] = jnp.sum(prod[:, hh * V_DIM:(hh + 1) * V_DIM], axis=-1, keepdims=True)

    return pl.pallas_call(
        body, name="attn_delta", grid=(t // tm,),
        in_specs=[pl.BlockSpec((tm, w), lambda i: (i, 0)), pl.BlockSpec((tm, w), lambda i: (i, 0))],
        out_specs=pl.BlockSpec((nh, tm, 1), lambda i: (0, i, 0)),
        out_shape=jax.ShapeDtypeStruct((nh, t, 1), F32),
        compiler_params=_cparams(),
    )(o, do)


ATTN_BWD_HEADS = 2


def _attn_bwd(q, k, v, do, lse, delta):
    nh, t, _ = q.shape
    hp = ATTN_BWD_HEADS
    tq = tk = min(512, t)
    nq, nk = t // tq, t // tk

    pairs = [(j, i) for j in range(nk) for i in range(j, nq)]
    kj = jnp.asarray(np.array([j for j, _ in pairs], np.int32))
    qi = jnp.asarray(np.array([i for _, i in pairs], np.int32))

    def body(kj_ref, qi_ref, q_ref, k_ref, v_ref, do_ref, lse_ref, dlt_ref, dq_ref, dk_ref, dv_ref):
        n = pl.program_id(1)
        j, i = kj_ref[n], qi_ref[n]

        @pl.when(n == 0)
        def _():
            dq_ref[...] = jnp.zeros_like(dq_ref)

        def step(diagonal):
            for hh in range(hp):
                qq, kk = q_ref[hh], k_ref[hh]
                dob = do_ref[:, hh * V_DIM:(hh + 1) * V_DIM]
                s = _dot_nt(qq, kk) * ATTN_SCALE
                if diagonal:
                    s = _causal_mask(s)
                p = jnp.exp(s - lse_ref[hh])
                dpp = _dot_nt(dob, v_ref[hh])
                dsb = (p * (dpp - dlt_ref[hh]) * ATTN_SCALE).astype(BF16)
                _accumulate(dv_ref.at[hh], _dot_tn(p.astype(BF16), dob), diagonal)
                _accumulate(dk_ref.at[hh], _dot_tn(dsb, qq), diagonal)
                dq_ref[hh, pl.ds(pl.multiple_of(i * tq, tq), tq), :] += _dot_nn(dsb, kk)

        @pl.when(i > j)
        def _():
            step(False)

        @pl.when(i == j)
        def _():
            step(True)

    q_map = lambda h, n, kj_ref, qi_ref: (h, qi_ref[n], 0)
    k_map = lambda h, n, kj_ref, qi_ref: (h, kj_ref[n], 0)
    return pl.pallas_call(
        body, name="attn_bwd",
        grid_spec=pltpu.PrefetchScalarGridSpec(
            num_scalar_prefetch=2, grid=(nh // hp, len(pairs)),
            in_specs=[pl.BlockSpec((hp, tq, HEAD_PAD), q_map),
                      pl.BlockSpec((hp, tk, HEAD_PAD), k_map),
                      pl.BlockSpec((hp, tk, V_DIM), k_map),
                      pl.BlockSpec((tq, hp * V_DIM), lambda h, n, kj_ref, qi_ref: (qi_ref[n], h)),
                      pl.BlockSpec((hp, tq, 1), q_map),
                      pl.BlockSpec((hp, tq, 1), q_map)],
            out_specs=[pl.BlockSpec((hp, t, HEAD_PAD), lambda h, n, kj_ref, qi_ref: (h, 0, 0)),
                       pl.BlockSpec((hp, tk, HEAD_PAD), k_map),
                       pl.BlockSpec((hp, tk, V_DIM), k_map)]),
        out_shape=[jax.ShapeDtypeStruct((nh, t, HEAD_PAD), F32), jax.ShapeDtypeStruct((nh, t, HEAD_PAD), F32),
                   jax.ShapeDtypeStruct((nh, t, V_DIM), F32)],
        compiler_params=_cparams(),
    )(kj, qi, q, k, v, do, lse, delta)


def _pool_counts(first_token, rows, w):
    tok = lax.broadcasted_iota(jnp.int32, (rows, POOL_CH), 0) + first_token
    return jnp.minimum(tok + 1, w).astype(F32)


def _pool_centered(zbuf, g, w, i, tm):
    lanes = pl.ds(g * POOL_CH, POOL_CH)
    cur = zbuf[pl.ds(POOL_HALO, tm), lanes]
    win = cur
    for s in range(1, w):
        win = win + zbuf[pl.ds(POOL_HALO - s, tm), lanes]
    return win / _pool_counts(i * tm, tm, w) - cur


def _pool_load(zbuf, z_ref, halo_ref, i, tm):
    @pl.when(i == 0)
    def _():
        zbuf[pl.ds(0, POOL_HALO), :] = jnp.zeros((POOL_HALO, zbuf.shape[1]), F32)

    @pl.when(i > 0)
    def _():
        zbuf[pl.ds(0, POOL_HALO), :] = halo_ref[...]

    zbuf[pl.ds(POOL_HALO, tm), :] = z_ref[...]


def _pool_fwd(z, pool_w, pool_scale):
    t = z.shape[0]
    pw = len(POOL_WINDOWS) * POOL_CH
    tm = min(512, t)
    hb = tm // POOL_HALO

    def body(z_ref, halo_ref, w_ref, sc_ref, p_ref, zbuf):
        i = pl.program_id(0)
        _pool_load(zbuf, z_ref, halo_ref, i, tm)
        for g, w in enumerate(POOL_WINDOWS):
            c = _pool_centered(zbuf, g, w, i, tm)
            y = _dot_nn(c.astype(BF16), w_ref[g]) * sc_ref[:, g * POOL_CH:(g + 1) * POOL_CH]
            p_ref[:, g * POOL_CH:(g + 1) * POOL_CH] = y.astype(BF16)

    return pl.pallas_call(
        body, name="pool_fwd", grid=(t // tm,),
        in_specs=[pl.BlockSpec((tm, pw), lambda i: (i, 1)),
                  pl.BlockSpec((POOL_HALO, pw), lambda i: (jnp.maximum(i * hb - 1, 0), 1)),
                  pl.BlockSpec(pool_w.shape, lambda i: (0, 0, 0)),
                  pl.BlockSpec((1, pw), lambda i: (0, 0))],
        out_specs=pl.BlockSpec((tm, pw), lambda i: (i, 0)),
        out_shape=jax.ShapeDtypeStruct((t, pw), BF16),
        scratch_shapes=[pltpu.VMEM((POOL_HALO + tm, pw), F32)],
        compiler_params=_cparams(),
    )(z, z, pool_w, pool_scale)


def _pool_bwd(dp, z, pool_w, pool_scale):
    t = z.shape[0]
    ng = len(POOL_WINDOWS)
    pw = ng * POOL_CH
    tm = min(512, t)
    hb = tm // POOL_HALO
    nt = t // tm

    def body(dp_ref, dpn_ref, z_ref, halo_ref, w_ref, sc_ref, dz_ref, dw_ref, dsc_ref, zbuf, dbuf):
        i = pl.program_id(0)
        _pool_load(zbuf, z_ref, halo_ref, i, tm)

        @pl.when(i == 0)
        def _():
            dw_ref[...] = jnp.zeros_like(dw_ref)
            dsc_ref[...] = jnp.zeros_like(dsc_ref)

        nxt_ok = (i < nt - 1).astype(F32)
        for g, w in enumerate(POOL_WINDOWS):
            lanes = pl.ds(g * POOL_CH, POOL_CH)
            cols = slice(g * POOL_CH, (g + 1) * POOL_CH)
            sc = sc_ref[:, cols]
            wg = w_ref[g]
            c = _pool_centered(zbuf, g, w, i, tm).astype(BF16)
            ypre = _dot_nn(c, wg)
            dpg = dp_ref[:, cols].astype(F32)
            dsc_ref[:, cols] += jnp.sum(dpg * ypre, axis=0, keepdims=True)
            dyb = (dpg * sc).astype(BF16)
            dw_ref[g] += _dot_tn(c, dyb)
            dd = _dot_nt(dyb, wg)
            dyn = (dpn_ref[:, cols].astype(F32) * sc).astype(BF16)
            ddn = _dot_nt(dyn, wg) * nxt_ok
            dbuf[pl.ds(0, tm), lanes] = dd / _pool_counts(i * tm, tm, w)
            dbuf[pl.ds(tm, POOL_HALO), lanes] = ddn / _pool_counts((i + 1) * tm, POOL_HALO, w)
            acc = -dd
            for s in range(w):
                acc = acc + dbuf[pl.ds(s, tm), lanes]
            dz_ref[:, cols] = acc

    return pl.pallas_call(
        body, name="pool_bwd", grid=(nt,),
        in_specs=[pl.BlockSpec((tm, pw), lambda i: (i, 0)),
                  pl.BlockSpec((POOL_HALO, pw), lambda i: (jnp.minimum((i + 1) * hb, t // POOL_HALO - 1), 0)),
                  pl.BlockSpec((tm, pw), lambda i: (i, 1)),
                  pl.BlockSpec((POOL_HALO, pw), lambda i: (jnp.maximum(i * hb - 1, 0), 1)),
                  pl.BlockSpec(pool_w.shape, lambda i: (0, 0, 0)),
                  pl.BlockSpec((1, pw), lambda i: (0, 0))],
        out_specs=[pl.BlockSpec((tm, pw), lambda i: (i, 0)),
                   pl.BlockSpec((ng, POOL_CH, POOL_CH), lambda i: (0, 0, 0)),
                   pl.BlockSpec((1, pw), lambda i: (0, 0))],
        out_shape=[jax.ShapeDtypeStruct((t, pw), F32), jax.ShapeDtypeStruct((ng, POOL_CH, POOL_CH), F32),
                   jax.ShapeDtypeStruct((1, pw), F32)],
        scratch_shapes=[pltpu.VMEM((POOL_HALO + tm, pw), F32), pltpu.VMEM((tm + POOL_HALO, pw), F32)],
        compiler_params=_cparams(),
    )(dp, dp, z, z, pool_w, pool_scale)


def _mla_bwd(dq_h, dk_h, dv_h, z, dz_pool, h1, dh2, mix_norm, w_in, q_norm, wq_t, kv_norm, wkv, pos, rope_tab):
    t, d = h1.shape
    tm = min(512, t)

    def body(dqh_ref, dkh_ref, dvh_ref, z_ref, dzp_ref, h_ref, dh2_ref, gm_ref, win_ref, gq_ref, wq_ref, gkv_ref,
             wkv_ref, pos_ref, tab_ref, dh1_ref, dq_ref, dkv_ref, dz_ref, dgq_ref, dgkv_ref, dgm_ref):
        i = pl.program_id(0)
        first = i == 0
        cos_t, sin_t = _rope_tables(pos_ref[...], tab_ref[...])
        dq_parts, dkv_parts = [], []
        dk_pe = jnp.zeros((tm, 128), F32)
        for hh in range(MLA_HEADS):
            dqh = dqh_ref[hh]
            dq_parts += [dqh[:, 0:NOPE], _rope_apply_t(dqh[:, NOPE:HEAD_PAD], cos_t, sin_t)]
            dkh = dkh_ref[hh]
            dkv_parts += [dkh[:, 0:NOPE], dvh_ref[hh]]
            dk_pe = dk_pe + dkh[:, NOPE:HEAD_PAD]
        dqb = jnp.concatenate(dq_parts, axis=-1).astype(BF16)
        dkvb = jnp.concatenate(dkv_parts, axis=-1).astype(BF16)
        dq_ref[...] = dqb
        dkv_ref[...] = dkvb
        z = z_ref[...]
        c_q = z[:, 0:Q_RANK]
        gq = gq_ref[...]
        _, rq = _rms_fwd(c_q, gq)
        dcq, dgq = _rms_bwd(_dot_nn(dqb, wq_ref[...]), c_q, gq, rq)
        c_kv = z[:, Q_RANK:Q_RANK + KV_RANK]
        gkv = gkv_ref[...]
        _, rkv = _rms_fwd(c_kv, gkv)
        dckv, dgkv = _rms_bwd(_dot_nt(dkvb, wkv_ref[...]), c_kv, gkv, rkv)
        dkr = _rope_apply_t(dk_pe, cos_t, sin_t)
        dzb = jnp.concatenate([dcq, dckv, dkr, dzp_ref[...]], axis=-1).astype(BF16)
        dz_ref[...] = dzb
        x = h_ref[...]
        gm = gm_ref[...]
        _, rm = _rms_fwd(x, gm)
        dx, dgm = _rms_bwd(_dot_nt(dzb, win_ref[...]), x, gm, rm)
        dh1_ref[...] = dh2_ref[...] + dx
        _accumulate(dgq_ref, dgq, first)
        _accumulate(dgkv_ref, dgkv, first)
        _accumulate(dgm_ref, dgm, first)

    full = lambda shape: pl.BlockSpec(shape, lambda i: (0,) * len(shape))
    row = lambda w: pl.BlockSpec((tm, w), lambda i: (i, 0))
    head = lambda w: pl.BlockSpec((MLA_HEADS, tm, w), lambda i: (0, i, 0))
    pw = len(POOL_WINDOWS) * POOL_CH
    return pl.pallas_call(
        body, name="mla_bwd", grid=(t // tm,),
        in_specs=[head(HEAD_PAD), head(HEAD_PAD), head(V_DIM), row(d), row(pw), row(d), row(d),
                  _resident((1, d)), _resident(w_in.shape), _resident((1, Q_RANK)), _resident(wq_t.shape),
                  _resident((1, KV_RANK)), _resident(wkv.shape), row(1), _resident(rope_tab.shape)],
        out_specs=[row(d), row(d), row(d), row(d), full((1, Q_RANK)), full((1, KV_RANK)), full((1, d))],
        out_shape=[jax.ShapeDtypeStruct((t, d), F32), jax.ShapeDtypeStruct((t, d), BF16),
                   jax.ShapeDtypeStruct((t, d), BF16), jax.ShapeDtypeStruct((t, d), BF16),
                   jax.ShapeDtypeStruct((1, Q_RANK), F32), jax.ShapeDtypeStruct((1, KV_RANK), F32),
                   jax.ShapeDtypeStruct((1, d), F32)],
        compiler_params=_cparams(),
    )(dq_h, dk_h, dv_h, z, dz_pool, h1, dh2, mix_norm, w_in, q_norm, wq_t, kv_norm, wkv, pos, rope_tab)


def _mem_kv(mem, mem_norm, wmkv):
    n, d = mem.shape

    def body(mem_ref, g_ref, w_ref, memn_ref, k_ref, v_ref):
        y, _ = _rms_fwd(mem_ref[...], g_ref[...])
        yb = y.astype(BF16)
        memn_ref[...] = yb
        for hh in range(MEM_HEADS):
            k_ref[hh] = _dot_nn(yb, w_ref[hh]).astype(BF16)
            v_ref[hh] = _dot_nn(yb, w_ref[MEM_HEADS + hh]).astype(BF16)

    return pl.pallas_call(
        body, name="mem_kv",
        out_shape=[jax.ShapeDtypeStruct((n, d), BF16), jax.ShapeDtypeStruct((MEM_HEADS, n, MEM_HD), BF16),
                   jax.ShapeDtypeStruct((MEM_HEADS, n, MEM_HD), BF16)],
        compiler_params=_cparams(),
    )(mem, mem_norm, wmkv)


def _mem_softmax(qb, km):
    s = _dot_nt(qb, km) * MEM_SCALE
    e = jnp.exp(s - jnp.max(s, axis=-1, keepdims=True))
    return e / jnp.sum(e, axis=-1, keepdims=True)


def _xattn_fwd(h1, a, p, w_out, g, wmq, km, vm, wmo, token=None):
    t, d = h1.shape
    tm = min(512, t)
    half = a.shape[1]

    def body(h_ref, a_ref, p_ref, wo_ref, g_ref, wmq_ref, km_ref, vm_ref, wmo_ref,
             h2_ref, h3_ref, hn_ref, q_ref, o_ref):
        h2 = h_ref[...] + _dot_nn(a_ref[...], wo_ref[0:half, :]) + _dot_nn(p_ref[...], wo_ref[half:2 * half, :])
        h2_ref[...] = h2
        hn, _ = _rms_fwd(h2, g_ref[...])
        hnb = hn.astype(BF16)
        hn_ref[...] = hnb
        qb = _dot_nn(hnb, wmq_ref[...]).astype(BF16)
        q_ref[...] = qb
        outs = []
        for hh in range(MEM_HEADS):
            pr = _mem_softmax(qb[:, hh * MEM_HD:(hh + 1) * MEM_HD], km_ref[hh])
            outs.append(_dot_nn(pr.astype(BF16), vm_ref[hh]))
        ob = jnp.concatenate(outs, axis=-1).astype(BF16)
        o_ref[...] = ob
        h3_ref[...] = h2 + _dot_nn(ob, wmo_ref[...])

    full = lambda shape: pl.BlockSpec(shape, lambda i: (0,) * len(shape))
    row = lambda w: pl.BlockSpec((tm, w), lambda i: (i, 0))
    return _call_after(
        token, body,
        [row(d), row(half), row(half), _resident(w_out.shape), _resident((1, d)), _resident(wmq.shape),
         _resident(km.shape), _resident(vm.shape), _resident(wmo.shape)],
        (h1, a, p, w_out, g, wmq, km, vm, wmo),
        name="xattn_fwd", grid=(t // tm,),
        out_specs=[row(d), row(d), row(d), row(d), row(d)],
        out_shape=[jax.ShapeDtypeStruct((t, d), F32), jax.ShapeDtypeStruct((t, d), F32),
                   jax.ShapeDtypeStruct((t, d), BF16), jax.ShapeDtypeStruct((t, d), BF16),
                   jax.ShapeDtypeStruct((t, d), BF16)],
        compiler_params=_cparams(),
    )


def _xattn_bwd(dh3, h2, qm, g, wmq, km, vm, wmo, w_out, token=None):
    t, d = h2.shape
    tm = min(512, t)
    half = d // 2

    def body(dh3_ref, h2_ref, q_ref, g_ref, wmq_ref, km_ref, vm_ref, wmo_ref, wo_ref,
             dh2_ref, dq_ref, da_ref, dp_ref, dk_ref, dv_ref, dg_ref):
        i = pl.program_id(0)
        first = i == 0

        @pl.when(first)
        def _():
            dk_ref[...] = jnp.zeros_like(dk_ref)
            dv_ref[...] = jnp.zeros_like(dv_ref)

        dh3 = dh3_ref[...]
        dob = _dot_nt(dh3.astype(BF16), wmo_ref[...]).astype(BF16)
        qb = q_ref[...]
        dq_parts = []
        for hh in range(MEM_HEADS):
            cols = slice(hh * MEM_HD, (hh + 1) * MEM_HD)
            kk, vv = km_ref[hh], vm_ref[hh]
            pr = _mem_softmax(qb[:, cols], kk)
            doh = dob[:, cols]
            dv_ref[hh] += _dot_tn(pr.astype(BF16), doh)
            dpp = _dot_nt(doh, vv)
            dsb = (pr * (dpp - jnp.sum(dpp * pr, axis=-1, keepdims=True)) * MEM_SCALE).astype(BF16)
            dq_parts.append(_dot_nn(dsb, kk))
            dk_ref[hh] += _dot_tn(dsb, qb[:, cols])
        dqb = jnp.concatenate(dq_parts, axis=-1).astype(BF16)
        dq_ref[...] = dqb
        x = h2_ref[...]
        gg = g_ref[...]
        _, r = _rms_fwd(x, gg)
        dx, dg = _rms_bwd(_dot_nt(dqb, wmq_ref[...]), x, gg, r)
        dh2 = dh3 + dx
        dh2_ref[...] = dh2
        dap = _dot_nt(dh2.astype(BF16), wo_ref[...])
        da_ref[...] = dap[:, 0:half].astype(BF16)
        dp_ref[...] = dap[:, half:d].astype(BF16)
        _accumulate(dg_ref, dg, first)

    full = lambda shape: pl.BlockSpec(shape, lambda i: (0,) * len(shape))
    row = lambda w: pl.BlockSpec((tm, w), lambda i: (i, 0))
    return _call_after(
        token, body,
        [row(d), row(d), row(d), _resident((1, d)), _resident(wmq.shape), _resident(km.shape), _resident(vm.shape),
         _resident(wmo.shape), _resident(w_out.shape)],
        (dh3, h2, qm, g, wmq, km, vm, wmo, w_out),
        name="xattn_bwd", grid=(t // tm,),
        out_specs=[row(d), row(d), row(half), row(half), full(km.shape), full(vm.shape), full((1, d))],
        out_shape=[jax.ShapeDtypeStruct((t, d), F32), jax.ShapeDtypeStruct((t, d), BF16),
                   jax.ShapeDtypeStruct((t, half), BF16), jax.ShapeDtypeStruct((t, half), BF16),
                   jax.ShapeDtypeStruct(km.shape, F32), jax.ShapeDtypeStruct(vm.shape, F32),
                   jax.ShapeDtypeStruct((1, d), F32)],
        compiler_params=_cparams(),
    )


def _mem_kv_bwd(dkm, dvm, memn, mem, mem_norm, wmkv):
    n, d = mem.shape

    def body(dk_ref, dv_ref, memn_ref, mem_ref, g_ref, w_ref, dw_ref, dg_ref):
        memn = memn_ref[...]
        dmemn = jnp.zeros((n, d), F32)
        for s in range(2 * MEM_HEADS):
            src = dk_ref[s] if s < MEM_HEADS else dv_ref[s - MEM_HEADS]
            db = src.astype(BF16)
            dw_ref[s] = _dot_tn(memn, db)
            dmemn = dmemn + _dot_nt(db, w_ref[s])
        x = mem_ref[...]
        gg = g_ref[...]
        _, r = _rms_fwd(x, gg)
        _, dg = _rms_bwd(dmemn, x, gg, r)
        dg_ref[...] = dg

    return pl.pallas_call(
        body, name="mem_kv_bwd",
        out_shape=[jax.ShapeDtypeStruct(wmkv.shape, F32), jax.ShapeDtypeStruct((1, d), F32)],
        compiler_params=_cparams(),
    )(dkm, dvm, memn, mem, mem_norm, wmkv)


MESH_ID = pl.DeviceIdType.MESH
ANY = pl.BlockSpec(memory_space=pl.ANY)


def _coords():
    return lax.axis_index("x"), lax.axis_index("y"), lax.axis_index("c")


def _other_chips(x, y):
    return [(1 - x, y), (x, 1 - y), (1 - x, 1 - y)]


def _core_reduce(g, tag):
    _, r, w = g.shape

    def body(g_ref, part_ref, own_sc, recv_sc, send_sems, recv_sems, local_sems):
        x, y, c = _coords()
        sent, local = [], []
        for chip in range(4):
            sent.append(pltpu.make_async_remote_copy(
                src_ref=g_ref.at[2 * chip + (1 - c)], dst_ref=recv_sc.at[chip],
                send_sem=send_sems.at[chip], recv_sem=recv_sems.at[chip],
                device_id=(x, y, 1 - c), device_id_type=MESH_ID))
            local.append(pltpu.make_async_copy(g_ref.at[2 * chip + c], own_sc.at[chip], local_sems.at[chip]))
        for cp in sent + local:
            cp.start()
        for chip in range(4):
            local[chip].wait()
            sent[chip].wait_recv()
            part_ref[chip] = (own_sc[chip].astype(F32) + recv_sc[chip].astype(F32)).astype(part_ref.dtype)
        for cp in sent:
            cp.wait_send()

    return pl.pallas_call(
        body, name="core_reduce_" + tag,
        out_shape=jax.ShapeDtypeStruct((4, r, w), g.dtype),
        in_specs=[ANY], out_specs=pl.BlockSpec(memory_space=pltpu.VMEM),
        scratch_shapes=[pltpu.VMEM((4, r, w), g.dtype), pltpu.VMEM((4, r, w), g.dtype),
                        pltpu.SemaphoreType.DMA((4,)), pltpu.SemaphoreType.DMA((4,)), pltpu.SemaphoreType.DMA((4,))],
        compiler_params=_cparams(),
    )(g)


HBM_SPEC = pl.BlockSpec(memory_space=pltpu.HBM)
SEM_SPEC = pl.BlockSpec(memory_space=pltpu.SEMAPHORE)
SPLIT_EFFECT = pltpu.SideEffectType.DATAFLOW_SIDE_EFFECTING


def _ici_refs(gather, src_ref, land_ref, j, px, py, slot_chip, c):
    if gather:
        return src_ref, land_ref.at[:, 4 * slot_chip[0] + 2 * slot_chip[1] + c]
    return src_ref.at[2 * px + py], land_ref.at[j]


def _ici_start(src, after, name, gather):
    r, w = src.shape[-2:]
    land_shape = (src.shape[0], N_DEV, r, w) if gather else (3, r, w)

    def body(src_ref, land_ref, after_ref, send_sems, recv_sems, src_thru, land_thru, token):
        x, y, c = _coords()
        for j, (px, py) in enumerate(_other_chips(x, y)):
            s_ref, d_ref = _ici_refs(gather, src_ref, land_ref, j, px, py, (x, y), c)
            pltpu.make_async_remote_copy(
                src_ref=s_ref, dst_ref=d_ref, send_sem=send_sems.at[j], recv_sem=recv_sems.at[j],
                device_id=(px, py, c), device_id_type=MESH_ID).start()
        token[...] = jnp.zeros_like(token)

    return pl.pallas_call(
        body, name=name,
        out_shape=(pltpu.SemaphoreType.DMA((3,)), pltpu.SemaphoreType.DMA((3,)), pltpu.HBM(src.shape, src.dtype),
                   pltpu.HBM(land_shape, src.dtype), jax.ShapeDtypeStruct((8, 128), F32)),
        in_specs=(HBM_SPEC, HBM_SPEC, ANY),
        out_specs=(SEM_SPEC, SEM_SPEC, HBM_SPEC, HBM_SPEC, pl.BlockSpec(memory_space=pltpu.VMEM)),
        input_output_aliases={0: 2, 1: 3},
        compiler_params=pltpu.CompilerParams(has_side_effects=SPLIT_EFFECT),
    )(pltpu.with_memory_space_constraint(src, pltpu.HBM),
      pltpu.with_memory_space_constraint(lax.empty(land_shape, src.dtype), pltpu.HBM), after)


def _ici_wait(started, after, name, gather):
    send_sems, recv_sems, src_thru, land_thru, _ = started

    def body(src_ref, land_ref, send_sems, recv_sems, after_ref, src_dead, got_ref):
        x, y, c = _coords()
        for j, (px, py) in enumerate(_other_chips(x, y)):
            s_ref, d_ref = _ici_refs(gather, src_ref, land_ref, j, px, py, (px, py), c)
            copy = pltpu.make_async_remote_copy(
                src_ref=s_ref, dst_ref=d_ref, send_sem=send_sems.at[j], recv_sem=recv_sems.at[j],
                device_id=(px, py, c), device_id_type=MESH_ID)
            copy.wait_send()
            copy.wait_recv()

    return pl.pallas_call(
        body, name=name,
        out_shape=(pltpu.HBM(src_thru.shape, src_thru.dtype), pltpu.HBM(land_thru.shape, land_thru.dtype)),
        in_specs=(HBM_SPEC, HBM_SPEC, SEM_SPEC, SEM_SPEC, ANY),
        out_specs=(HBM_SPEC, HBM_SPEC), input_output_aliases={0: 0, 1: 1},
        compiler_params=pltpu.CompilerParams(has_side_effects=SPLIT_EFFECT),
    )(src_thru, land_thru, send_sems, recv_sems, after)


def _neighbour(k, x, y):
    return (1 - x, y) if k == 0 else (x, 1 - y)


def _slot(ref, px, py, c):
    return ref.at[:, 4 * px + 2 * py + c]


def _near_start(src, after, name):
    land_shape = (src.shape[0], N_DEV) + src.shape[1:]

    def body(src_ref, land_ref, after_ref, send_sems, recv_sems, src_thru, land_thru, token):
        x, y, c = _coords()
        for k in range(2):
            px, py = _neighbour(k, x, y)
            pltpu.make_async_remote_copy(
                src_ref=src_ref, dst_ref=_slot(land_ref, x, y, c), send_sem=send_sems.at[k],
                recv_sem=recv_sems.at[k], device_id=(px, py, c), device_id_type=MESH_ID).start()
        token[...] = jnp.zeros_like(token)

    return pl.pallas_call(
        body, name=name,
        out_shape=(pltpu.SemaphoreType.DMA((2,)), pltpu.SemaphoreType.DMA((2,)), pltpu.HBM(src.shape, src.dtype),
                   pltpu.HBM(land_shape, src.dtype), jax.ShapeDtypeStruct((8, 128), F32)),
        in_specs=(HBM_SPEC, HBM_SPEC, ANY),
        out_specs=(SEM_SPEC, SEM_SPEC, HBM_SPEC, HBM_SPEC, pl.BlockSpec(memory_space=pltpu.VMEM)),
        input_output_aliases={0: 2, 1: 3},
        compiler_params=pltpu.CompilerParams(has_side_effects=SPLIT_EFFECT),
    )(pltpu.with_memory_space_constraint(src, pltpu.HBM),
      pltpu.with_memory_space_constraint(lax.empty(land_shape, src.dtype), pltpu.HBM), after)


def _near_wait(started, after, name):
    send_sems, recv_sems, src_thru, land_thru, _ = started

    def body(src_ref, land_ref, send_sems, recv_sems, after_ref, src_dead, got_ref):
        x, y, c = _coords()
        for k in range(2):
            px, py = _neighbour(k, x, y)
            copy = pltpu.make_async_remote_copy(
                src_ref=src_ref, dst_ref=_slot(land_ref, px, py, c), send_sem=send_sems.at[k],
                recv_sem=recv_sems.at[k], device_id=(px, py, c), device_id_type=MESH_ID)
            copy.wait_send()
            copy.wait_recv()

    return pl.pallas_call(
        body, name=name,
        out_shape=(pltpu.HBM(src_thru.shape, src_thru.dtype), pltpu.HBM(land_thru.shape, land_thru.dtype)),
        in_specs=(HBM_SPEC, HBM_SPEC, SEM_SPEC, SEM_SPEC, ANY),
        out_specs=(HBM_SPEC, HBM_SPEC), input_output_aliases={0: 0, 1: 1},
        compiler_params=pltpu.CompilerParams(has_side_effects=SPLIT_EFFECT),
    )(src_thru, land_thru, send_sems, recv_sems, after)


def _far_refs(land_ref, k, x, y, c, arriving):
    half = land_ref.shape[2] // 2
    rows = pl.ds(k * half, half)
    ox, oy = (1 - x, 1 - y) if arriving else _neighbour(k, x, y)
    return land_ref.at[:, 4 * ox + 2 * oy + c, rows]


def _far_start(land, after, name):
    def body(land_ref, after_ref, send_sems, recv_sems, land_thru, token):
        x, y, c = _coords()
        for k in range(2):
            block = _far_refs(land_ref, k, x, y, c, False)
            px, py = _neighbour(1 - k, x, y)
            pltpu.make_async_remote_copy(
                src_ref=block, dst_ref=block, send_sem=send_sems.at[k], recv_sem=recv_sems.at[k],
                device_id=(px, py, c), device_id_type=MESH_ID).start()
        token[...] = jnp.zeros_like(token)

    return pl.pallas_call(
        body, name=name,
        out_shape=(pltpu.SemaphoreType.DMA((2,)), pltpu.SemaphoreType.DMA((2,)),
                   pltpu.HBM(land.shape, land.dtype), jax.ShapeDtypeStruct((8, 128), F32)),
        in_specs=(HBM_SPEC, ANY),
        out_specs=(SEM_SPEC, SEM_SPEC, HBM_SPEC, pl.BlockSpec(memory_space=pltpu.VMEM)),
        input_output_aliases={0: 2},
        compiler_params=pltpu.CompilerParams(has_side_effects=SPLIT_EFFECT),
    )(pltpu.with_memory_space_constraint(land, pltpu.HBM), after)


def _far_wait(started, after, name):
    send_sems, recv_sems, land_thru, _ = started

    def body(land_ref, send_sems, recv_sems, after_ref, got_ref):
        x, y, c = _coords()
        for k in range(2):
            px, py = _neighbour(1 - k, x, y)
            copy = pltpu.make_async_remote_copy(
                src_ref=_far_refs(land_ref, k, x, y, c, False), dst_ref=_far_refs(land_ref, k, x, y, c, True),
                send_sem=send_sems.at[k], recv_sem=recv_sems.at[k], device_id=(px, py, c), device_id_type=MESH_ID)
            copy.wait_send()
            copy.wait_recv()

    return pl.pallas_call(
        body, name=name,
        out_shape=pltpu.HBM(land_thru.shape, land_thru.dtype),
        in_specs=(HBM_SPEC, SEM_SPEC, SEM_SPEC, ANY),
        out_specs=HBM_SPEC, input_output_aliases={0: 0},
        compiler_params=pltpu.CompilerParams(has_side_effects=SPLIT_EFFECT),
    )(land_thru, send_sems, recv_sems, after)


def _peer(k, x, y, c):
    return x ^ ((k >> 2) & 1), y ^ ((k >> 1) & 1), c ^ (k & 1)


def _peers_start(src, after, name):
    r, w = src.shape
    x, y, c = _coords()
    land = lax.dynamic_update_slice(jnp.zeros((N_DEV, r, w), src.dtype), src[None], (4 * x + 2 * y + c, 0, 0))

    def body(src_ref, land_ref, after_ref, send_sems, recv_sems, src_thru, land_thru, token):
        x, y, c = _coords()
        for k in range(1, N_DEV):
            pltpu.make_async_remote_copy(
                src_ref=src_ref, dst_ref=land_ref.at[4 * x + 2 * y + c],
                send_sem=send_sems.at[k - 1], recv_sem=recv_sems.at[k - 1],
                device_id=_peer(k, x, y, c), device_id_type=MESH_ID).start()
        token[...] = jnp.zeros_like(token)

    return pl.pallas_call(
        body, name=name,
        out_shape=(pltpu.SemaphoreType.DMA((N_DEV - 1,)), pltpu.SemaphoreType.DMA((N_DEV - 1,)),
                   pltpu.HBM(src.shape, src.dtype), pltpu.HBM(land.shape, src.dtype),
                   jax.ShapeDtypeStruct((8, 128), F32)),
        in_specs=(HBM_SPEC, HBM_SPEC, ANY),
        out_specs=(SEM_SPEC, SEM_SPEC, HBM_SPEC, HBM_SPEC, pl.BlockSpec(memory_space=pltpu.VMEM)),
        input_output_aliases={0: 2, 1: 3},
        compiler_params=pltpu.CompilerParams(has_side_effects=SPLIT_EFFECT),
    )(pltpu.with_memory_space_constraint(src, pltpu.HBM), pltpu.with_memory_space_constraint(land, pltpu.HBM), after)


def _peers_wait(started, after, name):
    send_sems, recv_sems, src_thru, land_thru, _ = started

    def body(src_ref, land_ref, send_sems, recv_sems, after_ref, src_dead, got_ref):
        x, y, c = _coords()
        for k in range(1, N_DEV):
            px, py, pc = _peer(k, x, y, c)
            copy = pltpu.make_async_remote_copy(
                src_ref=src_ref, dst_ref=land_ref.at[4 * px + 2 * py + pc],
                send_sem=send_sems.at[k - 1], recv_sem=recv_sems.at[k - 1],
                device_id=(px, py, pc), device_id_type=MESH_ID)
            copy.wait_send()
            copy.wait_recv()

    return pl.pallas_call(
        body, name=name,
        out_shape=(pltpu.HBM(src_thru.shape, src_thru.dtype), pltpu.HBM(land_thru.shape, land_thru.dtype)),
        in_specs=(HBM_SPEC, HBM_SPEC, SEM_SPEC, SEM_SPEC, ANY),
        out_specs=(HBM_SPEC, HBM_SPEC), input_output_aliases={0: 0, 1: 1},
        compiler_params=pltpu.CompilerParams(has_side_effects=SPLIT_EFFECT),
    )(src_thru, land_thru, send_sems, recv_sems, after)


def _share_refs(ref, k, x, y, c, sender_c):
    px, py = ([(x, y)] + _other_chips(x, y))[k]
    return ref.at[:, 4 * px + 2 * py + sender_c]


def _share_start(gathered, after, name):
    def body(g_ref, after_ref, send_sems, recv_sems, g_thru, token):
        x, y, c = _coords()
        for k in range(4):
            slot = _share_refs(g_ref, k, x, y, c, c)
            pltpu.make_async_remote_copy(
                src_ref=slot, dst_ref=slot, send_sem=send_sems.at[k], recv_sem=recv_sems.at[k],
                device_id=(x, y, 1 - c), device_id_type=MESH_ID).start()
        token[...] = jnp.zeros_like(token)

    return pl.pallas_call(
        body, name=name,
        out_shape=(pltpu.SemaphoreType.DMA((4,)), pltpu.SemaphoreType.DMA((4,)),
                   pltpu.HBM(gathered.shape, gathered.dtype), jax.ShapeDtypeStruct((8, 128), F32)),
        in_specs=(HBM_SPEC, ANY),
        out_specs=(SEM_SPEC, SEM_SPEC, HBM_SPEC, pl.BlockSpec(memory_space=pltpu.VMEM)),
        input_output_aliases={0: 2},
        compiler_params=pltpu.CompilerParams(has_side_effects=SPLIT_EFFECT),
    )(pltpu.with_memory_space_constraint(gathered, pltpu.HBM), after)


def _share_wait(started, after, name):
    send_sems, recv_sems, g_thru, _ = started

    def body(g_ref, send_sems, recv_sems, after_ref, got_ref):
        x, y, c = _coords()
        for k in range(4):
            copy = pltpu.make_async_remote_copy(
                src_ref=_share_refs(g_ref, k, x, y, c, c), dst_ref=_share_refs(g_ref, k, x, y, c, 1 - c),
                send_sem=send_sems.at[k], recv_sem=recv_sems.at[k],
                device_id=(x, y, 1 - c), device_id_type=MESH_ID)
            copy.wait_send()
            copy.wait_recv()

    return pl.pallas_call(
        body, name=name,
        out_shape=pltpu.HBM(g_thru.shape, g_thru.dtype),
        in_specs=(HBM_SPEC, SEM_SPEC, SEM_SPEC, ANY),
        out_specs=HBM_SPEC, input_output_aliases={0: 0},
        compiler_params=pltpu.CompilerParams(has_side_effects=SPLIT_EFFECT),
    )(g_thru, send_sems, recv_sems, after)


def _core_share(own, gathered, name):
    def body(own_ref, gin_ref, out_ref, stage, send_sems, recv_sems, local_sem):
        x, y, c = _coords()
        sibling = (x, y, 1 - c)
        chips = [(x, y)] + _other_chips(x, y)
        stage_in = pltpu.make_async_copy(own_ref, stage, local_sem)
        stage_in.start()
        sent, arriving = [], []
        for k, (px, py) in enumerate(chips):
            slot = out_ref.at[:, 4 * px + 2 * py + c]
            sent.append(pltpu.make_async_remote_copy(
                src_ref=own_ref if k == 0 else slot, dst_ref=slot,
                send_sem=send_sems.at[k], recv_sem=recv_sems.at[k], device_id=sibling, device_id_type=MESH_ID))
            arriving.append(pltpu.make_async_remote_copy(
                src_ref=own_ref, dst_ref=out_ref.at[:, 4 * px + 2 * py + (1 - c)],
                send_sem=send_sems.at[k], recv_sem=recv_sems.at[k], device_id=sibling, device_id_type=MESH_ID))
        for cp in sent:
            cp.start()
        stage_in.wait()
        stage_out = pltpu.make_async_copy(stage, out_ref.at[:, 4 * x + 2 * y + c], local_sem)
        stage_out.start()
        for cp in arriving:
            cp.wait_recv()
        for cp in sent:
            cp.wait_send()
        stage_out.wait()

    return pl.pallas_call(
        body, name=name,
        out_shape=jax.ShapeDtypeStruct(gathered.shape, own.dtype),
        in_specs=[ANY, ANY], out_specs=ANY, input_output_aliases={1: 0},
        scratch_shapes=[pltpu.VMEM(own.shape, own.dtype), pltpu.SemaphoreType.DMA((4,)),
                        pltpu.SemaphoreType.DMA((4,)), pltpu.SemaphoreType.DMA],
    )(own, gathered)


def _adamw(w, g, m, v):
    m = ADAM_B1 * m + (1.0 - ADAM_B1) * g
    v = ADAM_B2 * v + (1.0 - ADAM_B2) * (g * g)
    m_hat = m / ADAM_C1
    v_hat = v / ADAM_C2
    delta = -ADAM_LR * (m_hat / (jnp.sqrt(v_hat) + ADAM_EPS) + ADAM_WD * w)
    return delta, m, v


def _adam_big(units, chip_idx, tag, token):
    n = len(units)
    r, wd = units[0][2].shape
    tr, tw = _row_tile(r, 1024), 256

    def body(s_ref, tok_ref, *refs):
        for u in range(n):
            p_ref, l_ref, w_ref, m_ref, v_ref = refs[5 * u:5 * u + 5]
            g_ref, d_ref, mo_ref, vo_ref = refs[5 * n + 4 * u:5 * n + 4 * u + 4]
            g = p_ref[0].astype(F32)
            for j in range(3):
                g = g + l_ref[j].astype(F32)
            delta, mn, vn = _adamw(w_ref[...], g, m_ref[...], v_ref[...])
            g_ref[...] = g
            d_ref[...] = delta
            mo_ref[...] = mn
            vo_ref[...] = vn

    row = pl.BlockSpec((tr, tw), lambda i, j, s: (i, j))
    unit_specs = [pl.BlockSpec((1, tr, tw), lambda i, j, s: (s[0], i, j)),
                  pl.BlockSpec((3, tr, tw), lambda i, j, s: (0, i, j)), row, row, row]
    outs = pl.pallas_call(
        body, name="adam_big_" + tag,
        grid_spec=pltpu.PrefetchScalarGridSpec(
            num_scalar_prefetch=1, grid=(r // tr, wd // tw),
            in_specs=[pl.BlockSpec((8, 128), lambda i, j, s: (0, 0))] + unit_specs * n,
            out_specs=[row] * (4 * n)),
        out_shape=[jax.ShapeDtypeStruct((r, wd), F32)] * (4 * n),
        compiler_params=_cparams(),
    )(chip_idx, token, *[a for unit in units for a in unit])
    return [outs[4 * u:4 * u + 4] for u in range(n)]


def _adam_small(parts, w, m, v):
    _, r, wd = parts.shape

    def body(p_ref, w_ref, m_ref, v_ref, g_ref, d_ref, mo_ref, vo_ref):
        g = p_ref[0]
        for k in range(1, N_DEV):
            g = g + p_ref[k]
        delta, mn, vn = _adamw(w_ref[...], g, m_ref[...], v_ref[...])
        g_ref[...] = g
        d_ref[...] = delta
        mo_ref[...] = mn
        vo_ref[...] = vn

    return pl.pallas_call(
        body, name="adam_small",
        out_shape=[jax.ShapeDtypeStruct((r, wd), F32)] * 4,
        compiler_params=_cparams(),
    )(parts, w, m, v)


def _pad_rows(a, rows):
    return jnp.pad(a, ((0, rows - a.shape[0]), (0, 0)))


def _pad_w_in(w):
    cut = Q_RANK + KV_RANK + ROPE
    return jnp.concatenate([w[:, :cut], jnp.zeros((w.shape[0], 64), w.dtype), w[:, cut:]], axis=1)


def _unpad_w_in(w):
    cut = Q_RANK + KV_RANK + ROPE
    return jnp.concatenate([w[:, :cut], w[:, cut + 64:]], axis=1)


def _pack_mid(p):
    parts = [_pad_w_in(p["w_in"][0]), p["w_out"][0], p["w_mq"][0], p["w_mo"][0],
             p["w_mkv"][0].reshape(256, D_MODEL),
             _pad_rows(p["w_q_up"][0].T.reshape(24, D_MODEL), 32),
             p["w_kv_up"][0].reshape(16, D_MODEL)]
    return jnp.concatenate(parts, axis=0)


def _pack_ffn(w_gate, w_up, w_down, name):
    d, rows = w_gate.shape[1:]

    def body(g_ref, u_ref, d_ref, o_ref):
        eye = (lax.broadcasted_iota(jnp.int32, (d, d), 0) == lax.broadcasted_iota(jnp.int32, (d, d), 1)).astype(BF16)
        o_ref[0] = _dot_tn(g_ref[0].astype(BF16), eye).astype(BF16)
        o_ref[1] = _dot_tn(u_ref[0].astype(BF16), eye).astype(BF16)
        o_ref[2] = d_ref[0].astype(BF16)

    return pl.pallas_call(
        body, name=name, out_shape=jax.ShapeDtypeStruct((3, rows, d), BF16), compiler_params=_cparams(),
    )(w_gate, w_up, w_down)


def _pack_segments(p, group):
    if group == "mid":
        return _pack_mid(p)[None].astype(BF16)
    return _pack_ffn(p[group + "_w_gate"], p[group + "_w_up"], p[group + "_w_down"], "pack_" + group)


UNIT_WEIGHT = {"ffn1_g": ("ffn1_w_gate", True), "ffn1_u": ("ffn1_w_up", True), "ffn1_d": ("ffn1_w_down", False),
               "ffn2_g": ("ffn2_w_gate", True), "ffn2_u": ("ffn2_w_up", True), "ffn2_d": ("ffn2_w_down", False)}


def _pack_unit(p, unit):
    if unit == "mid":
        return _pack_mid(p)
    name, transposed = UNIT_WEIGHT[unit]
    return p[name][0].T if transposed else p[name][0]


def _unpack_unit(a, unit):
    if unit != "mid":
        name, transposed = UNIT_WEIGHT[unit]
        return {name: (a.T if transposed else a)[None]}
    seg = lambda n: a[SEG_OFF[n][0]:SEG_OFF[n][0] + SEG_OFF[n][1]]
    return {"w_in": _unpad_w_in(seg("w_in"))[None], "w_out": seg("w_out")[None], "w_mq": seg("w_mq")[None],
            "w_mo": seg("w_mo")[None], "w_mkv": seg("w_mkv").reshape(D_MODEL, 256)[None],
            "w_q_up": seg("w_q")[:24].reshape(96, Q_RANK).T[None],
            "w_kv_up": seg("w_kv").reshape(KV_RANK, 128)[None]}


def _unpack_gathered(full, group):
    if group != "mid":
        return {group: full.reshape(len(GROUP_SEGS[group]), -1, D_MODEL)}
    full = full[0]
    seg = lambda n: full[:, SEG_OFF[n][0]:SEG_OFF[n][0] + SEG_OFF[n][1]]
    rows = lambda n: seg(n).reshape(-1, D_MODEL)
    wq_t = seg("w_q")[:, :24].reshape(MLA_HEADS, NOPE + ROPE, Q_RANK)
    wq_t = jnp.pad(wq_t, ((0, 0), (0, HEAD_PAD - NOPE - ROPE), (0, 0))).reshape(MLA_HEADS * HEAD_PAD, Q_RANK)
    wkv = seg("w_kv").reshape(N_DEV, KV_RANK, 128).transpose(1, 0, 2).reshape(KV_RANK, N_DEV * 128)
    return {"w_in": rows("w_in"), "w_out": rows("w_out"), "w_mq": rows("w_mq"), "w_mo": rows("w_mo"),
            "w_mkv": seg("w_mkv").reshape(N_DEV, D_MODEL, 256), "w_q": wq_t, "w_kv": wkv}


def _pack_grads(gr):
    blk = lambda a: a.reshape(N_DEV, -1, D_MODEL)
    dwq = gr["w_q"].reshape(MLA_HEADS, HEAD_PAD, Q_RANK)[:, :NOPE + ROPE].reshape(N_DEV, 24, D_MODEL)
    dwq = jnp.pad(dwq, ((0, 0), (0, 8), (0, 0)))
    dwkv = gr["w_kv"].reshape(KV_RANK, N_DEV, 128).transpose(1, 0, 2).reshape(N_DEV, 16, D_MODEL)
    parts = [blk(gr["w_in"]), blk(gr["w_out"]), blk(gr["w_mq"]), blk(gr["w_mo"]),
             gr["w_mkv"].reshape(N_DEV, 256, D_MODEL), dwq, dwkv]
    return jnp.concatenate([a.astype(BF16) for a in parts], axis=1)


def _pack_small(vals):
    parts = []
    for n, r in SMALL_ROWS:
        parts.append(_pad_rows(vals[n].reshape(-1, 128), r) if n in vals else jnp.zeros((r, 128), F32))
    return jnp.concatenate(parts, axis=0)


def _unpack_small(a, shapes):
    out = {}
    for n, shape in shapes.items():
        o = SMALL_OFF[n][0]
        out[n] = a[o:o + int(np.prod(shape)) // 128].reshape(shape)
    return out


BIG_NAMES = ("ffn1_w_gate", "ffn1_w_up", "ffn1_w_down", "w_in", "w_q_up", "w_kv_up", "w_out", "w_mq", "w_mkv",
             "w_mo", "ffn2_w_gate", "ffn2_w_up", "ffn2_w_down")
SMALL_NAMES = ("ffn1_norm", "mix_norm", "q_norm", "kv_norm", "pool_w", "pool_scale", "xattn_norm", "mem_norm",
               "ffn2_norm", "final_norm")
WEIGHT_ORDER = ("ffn1_norm", "ffn1_w_gate", "ffn1_w_up", "ffn1_w_down", "mix_norm", "w_in", "q_norm", "w_q_up",
                "kv_norm", "w_kv_up", "pool_w", "pool_scale", "w_out", "xattn_norm", "mem_norm", "w_mq", "w_mkv",
                "w_mo", "ffn2_norm", "ffn2_w_gate", "ffn2_w_up", "ffn2_w_down", "final_norm")


def _rope_table():
    lane = np.arange(128)
    freqs = (1.0 / (ROPE_BASE ** (np.arange(0, ROPE, 2, dtype=np.float32) / ROPE))).astype(np.float32)
    tab = np.zeros((8, 128), np.float32)
    tab[0] = np.where(lane < ROPE, freqs[lane % (ROPE // 2)], 0.0)
    tab[1] = np.where(lane < ROPE // 2, -1.0, np.where(lane < ROPE, 1.0, 0.0))
    return jnp.asarray(tab)


def kernel(x, mem, positions, ffn1_norm, ffn1_w_gate, ffn1_w_up, ffn1_w_down, mix_norm, w_in, q_norm, w_q_up, kv_norm, w_kv_up, pool_w, pool_scale, w_out, xattn_norm, mem_norm, w_mq, w_mkv, w_mo, ffn2_norm, ffn2_w_gate, ffn2_w_up, ffn2_w_down, final_norm, loss_target, m_ffn1_norm, m_ffn1_w_gate, m_ffn1_w_up, m_ffn1_w_down, m_mix_norm, m_w_in, m_q_norm, m_w_q_up, m_kv_norm, m_w_kv_up, m_pool_w, m_pool_scale, m_w_out, m_xattn_norm, m_mem_norm, m_w_mq, m_w_mkv, m_w_mo, m_ffn2_norm, m_ffn2_w_gate, m_ffn2_w_up, m_ffn2_w_down, m_final_norm, v_ffn1_norm, v_ffn1_w_gate, v_ffn1_w_up, v_ffn1_w_down, v_mix_norm, v_w_in, v_q_norm, v_w_q_up, v_kv_norm, v_w_kv_up, v_pool_w, v_pool_scale, v_w_out, v_xattn_norm, v_mem_norm, v_w_mq, v_w_mkv, v_w_mo, v_ffn2_norm, v_ffn2_w_gate, v_ffn2_w_up, v_ffn2_w_down, v_final_norm):
    wts = dict(ffn1_norm=ffn1_norm, ffn1_w_gate=ffn1_w_gate, ffn1_w_up=ffn1_w_up, ffn1_w_down=ffn1_w_down,
               mix_norm=mix_norm, w_in=w_in, q_norm=q_norm, w_q_up=w_q_up, kv_norm=kv_norm, w_kv_up=w_kv_up,
               pool_w=pool_w, pool_scale=pool_scale, w_out=w_out, xattn_norm=xattn_norm, mem_norm=mem_norm,
               w_mq=w_mq, w_mkv=w_mkv, w_mo=w_mo, ffn2_norm=ffn2_norm, ffn2_w_gate=ffn2_w_gate,
               ffn2_w_up=ffn2_w_up, ffn2_w_down=ffn2_w_down, final_norm=final_norm)
    mom = dict(ffn1_norm=m_ffn1_norm, ffn1_w_gate=m_ffn1_w_gate, ffn1_w_up=m_ffn1_w_up, ffn1_w_down=m_ffn1_w_down,
               mix_norm=m_mix_norm, w_in=m_w_in, q_norm=m_q_norm, w_q_up=m_w_q_up, kv_norm=m_kv_norm,
               w_kv_up=m_w_kv_up, pool_w=m_pool_w, pool_scale=m_pool_scale, w_out=m_w_out, xattn_norm=m_xattn_norm,
               mem_norm=m_mem_norm, w_mq=m_w_mq, w_mkv=m_w_mkv, w_mo=m_w_mo, ffn2_norm=m_ffn2_norm,
               ffn2_w_gate=m_ffn2_w_gate, ffn2_w_up=m_ffn2_w_up, ffn2_w_down=m_ffn2_w_down, final_norm=m_final_norm)
    var = dict(ffn1_norm=v_ffn1_norm, ffn1_w_gate=v_ffn1_w_gate, ffn1_w_up=v_ffn1_w_up, ffn1_w_down=v_ffn1_w_down,
               mix_norm=v_mix_norm, w_in=v_w_in, q_norm=v_q_norm, w_q_up=v_w_q_up, kv_norm=v_kv_norm,
               w_kv_up=v_w_kv_up, pool_w=v_pool_w, pool_scale=v_pool_scale, w_out=v_w_out, xattn_norm=v_xattn_norm,
               mem_norm=v_mem_norm, w_mq=v_w_mq, w_mkv=v_w_mkv, w_mo=v_w_mo, ffn2_norm=v_ffn2_norm,
               ffn2_w_gate=v_ffn2_w_gate, ffn2_w_up=v_ffn2_w_up, ffn2_w_down=v_ffn2_w_down, final_norm=v_final_norm)

    t = x.shape[1]
    xs = x[0]
    mems = mem[0]
    target = loss_target[0]
    pos = positions.reshape(t, 1)
    row = lambda a: a.reshape(1, -1)
    rope_tab = _rope_table()

    cx, cy, cc = _coords()
    chip_idx = (2 * cx + cy).astype(jnp.int32).reshape(1)

    wb = {}
    for grp in ("ffn1", "mid", "ffn2"):
        wb[grp] = _pack_segments(wts, grp)
        if grp == "ffn1":
            near_ffn1 = _near_start(wb["ffn1"], pos, "ag_ffn1_near_start")
    mid_names = ("w_in", "w_out", "w_mq", "w_mo", "w_mkv", "w_q_up", "w_kv_up")

    states = (wts, mom, var)

    def packed_during(token, units, after, states):
        one = 1.0 + token[0, 0]
        packs = {}
        for u in units:
            names = SMALL_NAMES if u == "small" else mid_names if u == "mid" else UNIT_WEIGHT[u][:1]
            held = [{n: p[n] * one for n in names} for p in states]
            packs[u] = tuple(_pack_small(h) if u == "small" else _pack_unit(h, u) for h in held)
        return lax.optimization_barrier((after, packs))

    after, adam_in = packed_during(near_ffn1[4], ("ffn1_g", "ffn1_u", "ffn1_d", "ffn2_g", "ffn2_u", "ffn2_d"),
                                   wb["ffn2"], states)
    after, mid_w = packed_during(near_ffn1[4], ("mid",), after, states[:1])
    after, more = packed_during(near_ffn1[4], ("small",), after, states)
    own_ffn1, land_ffn1 = _near_wait(near_ffn1, after, "ag_ffn1_near_wait")
    far_ffn1 = _far_start(land_ffn1, own_ffn1, "ag_ffn1_far_start")
    after, mid_mv = packed_during(far_ffn1[3], ("mid",), wb["mid"], states[1:])
    adam_in.update(more)
    adam_in["mid"] = mid_w["mid"] + mid_mv["mid"]
    land_ffn1 = _far_wait(far_ffn1, after, "ag_ffn1_far_wait")
    full_ffn1 = _core_share(own_ffn1, land_ffn1, "ag_ffn1_share")
    fw = _unpack_gathered(full_ffn1, "ffn1")
    ag_mid = _ici_start(wb["mid"], full_ffn1, "ag_mid_start", True)
    g_ffn1, g_mix, g_q, g_kv = row(ffn1_norm), row(mix_norm), row(q_norm), row(kv_norm)
    g_x, g_mem, g_ffn2, g_fin = row(xattn_norm), row(mem_norm), row(ffn2_norm), row(final_norm)
    pool_wb = pool_w[0].astype(BF16)
    pool_sc = row(pool_scale)

    h1, n1, gate1, up1 = _ffn_fwd(xs, g_ffn1, fw["ffn1"], "ffn1_fwd", token=ag_mid[4])
    own_mid, land_mid = _ici_wait(ag_mid, h1, "ag_mid_wait", True)
    full_mid = _core_share(own_mid, land_mid, "ag_mid_share")
    fw.update(_unpack_gathered(full_mid, "mid"))
    ag_ffn2 = _ici_start(wb["ffn2"], full_mid, "ag_ffn2_start", True)
    u, z, qn, kvn, qh, kh, vh = _mix_prep(h1, g_mix, fw["w_in"], g_q, fw["w_q"], g_kv, fw["w_kv"], pos, rope_tab,
                                          token=ag_ffn2[4])
    a, lse = _attn_fwd(qh, kh, vh)
    p = _pool_fwd(z, pool_wb, pool_sc)
    memn, km, vm = _mem_kv(mems, g_mem, fw["w_mkv"])
    own_ffn2, land_ffn2 = _ici_wait(ag_ffn2, a, "ag_ffn2_wait", True)
    land_ffn2 = lax.dynamic_update_slice(land_ffn2, own_ffn2[:, None], (0, 4 * cx + 2 * cy + cc, 0, 0))
    share_ffn2 = _share_start(land_ffn2, a, "ag_ffn2_share_start")
    h2, h3, hn, qm, om = _xattn_fwd(h1, a, p, fw["w_out"], g_x, fw["w_mq"], km, vm, fw["w_mo"], token=share_ffn2[3])
    fw.update(_unpack_gathered(_share_wait(share_ffn2, h3, "ag_ffn2_share_wait"), "ffn2"))
    dh4, n2, gate2, up2, loss_part, dg_fin = _ffn_fwd(h3, g_ffn2, fw["ffn2"],
                                                      "ffn2_fwd", head=(target, g_fin))

    def reduce_start(g8, unit):
        part = _core_reduce(g8, unit)
        return _ici_start(part, g8, "rs_" + unit + "_start", False)

    def by_device(g):
        return g.reshape(N_DEV, -1, D_MODEL)

    rs = {}
    dh3, dgate2, dup2, act2, dg_ffn2 = _ffn_bwd_data(dh4, h3, g_ffn2, gate2, up2, fw["ffn2"], "ffn2_bwd")
    rs["ffn2_g"] = reduce_start(by_device(_tn_matmul(dgate2, n2, "ffn2_dwg", tmm=1408, m=D_FF, out_dtype=BF16)), "ffn2_g")
    rs["ffn2_u"] = reduce_start(by_device(_tn_matmul(dup2, n2, "ffn2_dwu", tmm=1408, m=D_FF, out_dtype=BF16,
                                                     token=rs["ffn2_g"][4])), "ffn2_u")
    rs["ffn2_d"] = reduce_start(by_device(_tn_matmul(act2, dh4, "ffn2_dwd", scale=0.5, tmm=1408, m=D_FF, out_dtype=BF16,
                                                     token=rs["ffn2_u"][4])), "ffn2_d")
    dh2, dqm, da, dp, dkm, dvm, dg_x = _xattn_bwd(dh3, h2, qm, g_x, fw["w_mq"], km, vm, fw["w_mo"], fw["w_out"],
                                                  token=rs["ffn2_d"][4])
    gr = {}
    gr["w_mo"] = _tn_matmul(om, dh3, "dw_mo", out_dtype=BF16)
    gr["w_mq"] = _tn_matmul(hn, dqm, "dw_mq", out_dtype=BF16)
    gr["w_out"] = jnp.concatenate([_tn_matmul(a, dh2, "dw_out_a", out_dtype=BF16),
                                   _tn_matmul(p, dh2, "dw_out_p", out_dtype=BF16)], axis=0)
    gr["w_mkv"], dg_mem = _mem_kv_bwd(dkm, dvm, memn, mems, g_mem, fw["w_mkv"])
    dz_pool, d_pool_w, d_pool_sc = _pool_bwd(dp, z, pool_wb, pool_sc)
    dqh, dkh, dvh = _attn_bwd(qh, kh, vh, da, lse, _attn_delta(a, da))
    dh1, dq, dkv, dz, dg_q, dg_kv, dg_mix = _mla_bwd(dqh, dkh, dvh, z, dz_pool, h1, dh2, g_mix, fw["w_in"], g_q,
                                                     fw["w_q"], g_kv, fw["w_kv"], pos, rope_tab)
    gr["w_q"] = _tn_matmul(dq, qn, "dw_q", out_dtype=BF16)
    gr["w_kv"] = _tn_matmul(kvn, dkv, "dw_kv", out_dtype=BF16)
    gr["w_in"] = _tn_matmul(u, dz, "dw_in", out_dtype=BF16)
    g_mid = _pack_grads(gr)
    part_mid = _core_reduce(g_mid, "mid")
    got = {}
    after = part_mid
    for unit in ("ffn2_g", "ffn2_u", "ffn2_d"):
        got[unit] = _ici_wait(rs[unit], after, "rs_" + unit + "_wait", False)
        after = got[unit][1]
    rs["mid"] = _ici_start(part_mid, after, "rs_mid_start", False)
    dx, dgate1, dup1, act1, dg_ffn1 = _ffn_bwd_data(dh1, xs, g_ffn1, gate1, up1, fw["ffn1"], "ffn1_bwd", token=rs["mid"][4])
    got["mid"] = _ici_wait(rs["mid"], dx, "rs_mid_wait", False)

    small_g = dict(ffn1_norm=dg_ffn1, mix_norm=dg_mix, q_norm=dg_q, kv_norm=dg_kv, pool_w=d_pool_w,
                   pool_scale=d_pool_sc, xattn_norm=dg_x, mem_norm=dg_mem, ffn2_norm=dg_ffn2, final_norm=dg_fin,
                   loss=loss_part)
    small_ag = _peers_start(_pack_small(small_g), got["mid"][1], "small_ag_start")
    rs["ffn1_g"] = reduce_start(by_device(_tn_matmul(dgate1, n1, "ffn1_dwg", tmm=1408, m=D_FF, out_dtype=BF16,
                                                     token=small_ag[4])), "ffn1_g")
    _, parts = _peers_wait(small_ag, rs["ffn1_g"][4], "small_ag_wait")
    small = _adam_small(parts, *adam_in["small"])
    small_sum = small[0]
    loss = small_sum[SMALL_OFF["loss"][0], 0]
    shapes = {n: wts[n].shape for n in SMALL_NAMES}
    small = [_unpack_small(s, shapes) for s in small]

    rs["ffn1_u"] = reduce_start(by_device(_tn_matmul(dup1, n1, "ffn1_dwu", tmm=1408, m=D_FF, out_dtype=BF16,
                                                     token=small_sum)), "ffn1_u")
    rs["ffn1_d"] = reduce_start(by_device(_tn_matmul(act1, dh1, "ffn1_dwd", scale=0.5, tmm=1408, m=D_FF, out_dtype=BF16,
                                                     token=rs["ffn1_u"][4])), "ffn1_d")

    big = {}

    def adam_units(names, token):
        units = [got[u] + adam_in[u] for u in names]
        res = _adam_big(units, chip_idx, "_".join(names), token)
        for u, four in zip(names, res):
            for k, packed in enumerate(four):
                big.setdefault(k, {}).update(_unpack_unit(packed, u))
        return res[-1][0]

    done = adam_units(["mid"], rs["ffn1_d"][4])
    done = adam_units(["ffn2_g", "ffn2_u", "ffn2_d"], done)
    got["ffn1_g"] = _ici_wait(rs["ffn1_g"], done, "rs_ffn1_g_wait", False)
    got["ffn1_u"] = _ici_wait(rs["ffn1_u"], got["ffn1_g"][1], "rs_ffn1_u_wait", False)
    done = adam_units(["ffn1_g", "ffn1_u"], done)
    got["ffn1_d"] = _ici_wait(rs["ffn1_d"], done, "rs_ffn1_d_wait", False)
    adam_units(["ffn1_d"], done)

    outs = [loss, dx[None]]
    for k in range(4):
        for n in WEIGHT_ORDER:
            outs.append(big[k][n] if n in BIG_NAMES else small[k][n])
    return tuple(outs)
```

```python
import numpy as np

import jax
import jax.numpy as jnp
from jax import lax
from jax.experimental import pallas as pl
from jax.experimental.pallas import tpu as pltpu

F32 = jnp.float32
BF16 = jnp.bfloat16

N_DEV = 8
D_MODEL = 1024
D_FF = 2816
MLA_HEADS = 4
NOPE = 128
ROPE = 64
HEAD_PAD = 256
V_DIM = 128
Q_RANK = 256
KV_RANK = 128
POOL_WINDOWS = (2, 4, 8, 16)
POOL_CH = 128
POOL_HALO = 16
N_MEM = 256
MEM_HEADS = 4
MEM_HD = 256
ROPE_BASE = 10000.0
RMS_EPS = 1e-6
ATTN_SCALE = (NOPE + ROPE) ** -0.5
MEM_SCALE = MEM_HD ** -0.5
NEG_BIG = -1e30

ADAM_LR = 0.001
ADAM_B1 = 0.9
ADAM_B2 = 0.999
ADAM_EPS = 1e-08
ADAM_WD = 0.01
ADAM_STEP = 10
ADAM_C1 = 1.0 - ADAM_B1 ** ADAM_STEP
ADAM_C2 = 1.0 - ADAM_B2 ** ADAM_STEP

VMEM_LIMIT_BYTES = 56 * 1024 * 1024
BF16_ROWS = 16

GROUP_SEGS = {
    "ffn1": (("ffn1_g", 352), ("ffn1_u", 352), ("ffn1_d", 352)),
    "mid": (("w_in", 128), ("w_out", 128), ("w_mq", 128), ("w_mo", 128), ("w_mkv", 256), ("w_q", 32), ("w_kv", 16)),
    "ffn2": (("ffn2_g", 352), ("ffn2_u", 352), ("ffn2_d", 352)),
}
SEG_OFF = {}
GROUP_ROWS = {}
for _g, _segs in GROUP_SEGS.items():
    _o = 0
    for _n, _r in _segs:
        SEG_OFF[_n] = (_o, _r)
        _o += _r
    GROUP_ROWS[_g] = _o

SMALL_ROWS = (("ffn1_norm", 8), ("mix_norm", 8), ("q_norm", 8), ("kv_norm", 8), ("pool_w", 512), ("pool_scale", 8),
              ("xattn_norm", 8), ("mem_norm", 8), ("ffn2_norm", 8), ("final_norm", 8), ("loss", 8))
SMALL_OFF = {}
_o = 0
for _n, _r in SMALL_ROWS:
    SMALL_OFF[_n] = (_o, _r)
    _o += _r


def _cparams(**kw):
    return pltpu.CompilerParams(vmem_limit_bytes=VMEM_LIMIT_BYTES, **kw)


def _row_tile(rows, limit):
    best = None
    for cand in range(BF16_ROWS, min(rows, limit) + 1, BF16_ROWS):
        if rows % cand == 0:
            best = cand
    assert best is not None, rows
    return best


def _dot_nn(a, b):
    return lax.dot_general(a, b, (((1,), (0,)), ((), ())), preferred_element_type=F32)


def _dot_nt(a, b):
    return lax.dot_general(a, b, (((1,), (1,)), ((), ())), preferred_element_type=F32)


def _dot_tn(a, b):
    return lax.dot_general(a, b, (((0,), (0,)), ((), ())), preferred_element_type=F32)


def _rms_fwd(x, g):
    r = lax.rsqrt(jnp.mean(x * x, axis=-1, keepdims=True) + RMS_EPS)
    return x * r * g, r


def _rms_bwd(dy, x, g, r):
    xhat = x * r
    dyg = dy * g
    dx = r * (dyg - xhat * jnp.mean(dyg * xhat, axis=-1, keepdims=True))
    dg = jnp.sum(dy * xhat, axis=0, keepdims=True)
    return dx, dg


def _accumulate(ref, val, first):
    if isinstance(first, bool):
        if first:
            ref[...] = val
        else:
            ref[...] += val
        return

    @pl.when(first)
    def _():
        ref[...] = val

    @pl.when(jnp.logical_not(first))
    def _():
        ref[...] += val


def _call_after(token, body, in_specs, args, **kw):
    if token is not None:
        inner = body
        body = lambda tok_ref, *refs: inner(*refs)
        in_specs = [pl.BlockSpec((8, 128), lambda *_: (0, 0))] + list(in_specs)
        args = (token,) + tuple(args)
    return pl.pallas_call(body, in_specs=in_specs, **kw)(*args)


def _resident(shape):
    return pl.BlockSpec(shape, lambda *_: (0,) * len(shape), pipeline_mode=pl.Buffered(1))


def _rope_tables(pos_col, tab):
    ang = pos_col.astype(F32) * tab[0:1, :]
    return jnp.cos(ang), jnp.sin(ang) * tab[1:2, :]


def _swap_halves(x):
    lane = lax.broadcasted_iota(jnp.int32, x.shape, 1)
    return jnp.where((lane % 64) < 32, pltpu.roll(x, 96, 1), pltpu.roll(x, 32, 1))


def _rope_apply(x, cos_t, sin_t):
    return x * cos_t + _swap_halves(x) * sin_t


def _rope_apply_t(dy, cos_t, sin_t):
    return dy * cos_t + _swap_halves(dy * sin_t)


def _ffn_fwd(h, g, w, name, token=None, head=None):
    t, d = h.shape
    f = w.shape[1]
    tm, tf = min(512, t), 256
    nf = f // tf
    n_in = 3 if head is None else 5

    def body(*refs):
        h_ref, g_ref, w_ref = refs[:3]
        ho_ref, n_ref, gate_ref, up_ref = refs[n_in:n_in + 4]
        nb_sc, acc_sc = refs[-2:]
        y, _ = _rms_fwd(h_ref[...], g_ref[...])
        nb = y.astype(BF16)
        nb_sc[...] = nb
        n_ref[...] = nb
        acc_sc[...] = jnp.zeros_like(acc_sc)

        def f_slab(j, ntile):
            rows = pl.ds(pl.multiple_of(j * tf, tf), ntile * tf)
            nb = nb_sc[...]
            gt = _dot_nt(nb, w_ref[0, rows, :])
            ut = _dot_nt(nb, w_ref[1, rows, :])
            for k in range(ntile):
                gate_ref[j + k] = gt[:, k * tf:(k + 1) * tf].astype(BF16)
                up_ref[j + k] = ut[:, k * tf:(k + 1) * tf].astype(BF16)
            act = (gt * jax.nn.sigmoid(gt)) * ut
            return _dot_nn(act.astype(BF16), w_ref[2, rows, :])

        def pair(p, carry):
            acc_sc[...] += f_slab(2 * p, 2)
            return carry

        lax.fori_loop(0, nf // 2, pair, 0, unroll=True)
        if nf % 2:
            acc_sc[...] += f_slab(nf - 1, 1)
        ho = h_ref[...] + 0.5 * acc_sc[...]
        if head is None:
            ho_ref[...] = ho
            return
        t_ref, gf_ref = refs[3:5]
        loss_ref, dgf_ref = refs[n_in + 4:n_in + 6]
        gg = gf_ref[...]
        y, r = _rms_fwd(ho, gg)
        err = y - t_ref[...]
        part = 0.5 * jnp.sum(jnp.mean(err * err, axis=-1, keepdims=True), axis=0, keepdims=True)
        dx, dg = _rms_bwd(err * (1.0 / d), ho, gg, r)
        ho_ref[...] = dx
        first = pl.program_id(0) == 0
        _accumulate(loss_ref, jnp.broadcast_to(part, loss_ref.shape), first)
        _accumulate(dgf_ref, dg, first)

    row = pl.BlockSpec((tm, d), lambda i: (i, 0))
    tiles = pl.BlockSpec((nf, tm, tf), lambda i: (0, i, 0))
    in_specs = [row, _resident((1, d)), _resident(w.shape)]
    args = (h, g, w)
    out_specs = [row, row, tiles, tiles]
    out_shape = [jax.ShapeDtypeStruct((t, d), F32), jax.ShapeDtypeStruct((t, d), BF16),
                 jax.ShapeDtypeStruct((nf, t, tf), BF16), jax.ShapeDtypeStruct((nf, t, tf), BF16)]
    if head is not None:
        in_specs += [row, _resident((1, d))]
        args += tuple(head)
        out_specs += [pl.BlockSpec((8, 128), lambda i: (0, 0)), pl.BlockSpec((1, d), lambda i: (0, 0))]
        out_shape += [jax.ShapeDtypeStruct((8, 128), F32), jax.ShapeDtypeStruct((1, d), F32)]
    return _call_after(
        token, body, in_specs, args, name=name, grid=(t // tm,), out_specs=out_specs, out_shape=out_shape,
        scratch_shapes=[pltpu.VMEM((tm, d), BF16), pltpu.VMEM((tm, d), F32)],
        compiler_params=_cparams(),
    )


def _ffn_bwd_data(dho, h, g, gate, up, w, name, token=None):
    t, d = h.shape
    f = w.shape[1]
    tm, tf = min(1024, t), 256
    parts = 2 if tm % 512 == 0 else 1
    tp = tm // parts
    nf = f // tf
    npair, odd = nf // 2, nf % 2
    nsteps = npair + odd

    def body(dho_ref, h_ref, g_ref, gate_ref, up_ref, wg_ref, wu_ref, wd_ref,
             dh_ref, dgate_ref, dup_ref, act_ref, dg_ref, dhb_sc, acc_sc):
        i, j = pl.program_id(0), pl.program_id(1)

        @pl.when(j == 0)
        def _():
            dhb_sc[...] = (0.5 * dho_ref[...]).astype(BF16)
            acc_sc[...] = jnp.zeros_like(acc_sc)

        def slab(ntile):
            cols = pl.ds(0, ntile * tf)
            for r in range(parts):
                rows = pl.ds(r * tp, tp)
                gt = jnp.concatenate([gate_ref[k, rows, :] for k in range(ntile)], axis=-1).astype(F32)
                ut = jnp.concatenate([up_ref[k, rows, :] for k in range(ntile)], axis=-1).astype(F32)
                dact = _dot_nt(dhb_sc[rows, :], wd_ref[cols, :])
                sg = jax.nn.sigmoid(gt)
                silu = gt * sg
                dgb = (dact * ut * (sg * (1.0 + gt * (1.0 - sg)))).astype(BF16)
                dub = (dact * silu).astype(BF16)
                act_ref[rows, cols] = (silu * ut).astype(BF16)
                dgate_ref[rows, cols] = dgb
                dup_ref[rows, cols] = dub
                acc_sc[rows, :] += _dot_nn(dgb, wg_ref[cols, :]) + _dot_nn(dub, wu_ref[cols, :])

        pl.when(j < npair)(lambda: slab(2))
        if odd:
            pl.when(j == npair)(lambda: slab(1))

        @pl.when(j == nsteps - 1)
        def _():
            x = h_ref[...]
            gg = g_ref[...]
            _, r = _rms_fwd(x, gg)
            dx, dg = _rms_bwd(acc_sc[...], x, gg, r)
            dh_ref[...] = dho_ref[...] + dx
            _accumulate(dg_ref, dg, i == 0)

    row = pl.BlockSpec((tm, d), lambda i, j: (i, 0))
    acts = pl.BlockSpec((2, tm, tf), lambda i, j: (j, i, 0))
    weights = lambda k: pl.BlockSpec((None, 2 * tf, d), lambda i, j: (k, j, 0))
    outs = pl.BlockSpec((tm, 2 * tf), lambda i, j: (i, j))
    padded = jax.ShapeDtypeStruct((t, 2 * tf * nsteps), BF16)
    return _call_after(
        token, body,
        [row, row, pl.BlockSpec((1, d), lambda i, j: (0, 0)), acts, acts, weights(0), weights(1), weights(2)],
        (dho, h, g, gate, up, w, w, w),
        name=name, grid=(t // tm, nsteps),
        out_specs=[row, outs, outs, outs, pl.BlockSpec((1, d), lambda i, j: (0, 0))],
        out_shape=[jax.ShapeDtypeStruct((t, d), F32), padded, padded, padded, jax.ShapeDtypeStruct((1, d), F32)],
        scratch_shapes=[pltpu.VMEM((tm, d), BF16), pltpu.VMEM((tm, d), F32)],
        compiler_params=_cparams(),
    )


def _tn_matmul(a, b, name, scale=1.0, tmm=None, out_dtype=F32, token=None, m=None):
    t = a.shape[0]
    m = a.shape[1] if m is None else m
    n = b.shape[1]
    tmm = m if tmm is None else tmm
    tk = min(1024, t)
    nk = t // tk

    def product(a_ref, b_ref):
        prod = _dot_tn(a_ref[...].astype(BF16), b_ref[...].astype(BF16))
        return prod * scale if scale != 1.0 else prod

    def body_f32(a_ref, b_ref, o_ref):
        _accumulate(o_ref, product(a_ref, b_ref), pl.program_id(1) == 0)

    def body_cast(a_ref, b_ref, o_ref, acc_sc):
        k = pl.program_id(1)
        _accumulate(acc_sc, product(a_ref, b_ref), k == 0)

        @pl.when(k == nk - 1)
        def _():
            o_ref[...] = acc_sc[...].astype(out_dtype)

    direct = out_dtype == F32
    return _call_after(
        token, body_f32 if direct else body_cast,
        [pl.BlockSpec((tk, tmm), lambda i, k: (k, i)),
         pl.BlockSpec((tk, n), lambda i, k: (k, 0))],
        (a, b),
        name=name, grid=(m // tmm, nk),
        out_specs=pl.BlockSpec((tmm, n), lambda i, k: (i, 0)),
        out_shape=jax.ShapeDtypeStruct((m, n), out_dtype),
        scratch_shapes=[] if direct else [pltpu.VMEM((tmm, n), F32)],
        compiler_params=_cparams(),
    )


def _mix_prep(h1, mix_norm, w_in, q_norm, wq_t, kv_norm, wkv, pos, rope_tab, token=None):
    t, d = h1.shape
    tm = min(512, t)

    def body(h_ref, gm_ref, win_ref, gq_ref, wq_ref, gkv_ref, wkv_ref, pos_ref, tab_ref,
             u_ref, z_ref, qn_ref, kvn_ref, q_ref, k_ref, v_ref):
        u, _ = _rms_fwd(h_ref[...], gm_ref[...])
        ub = u.astype(BF16)
        u_ref[...] = ub
        z = _dot_nn(ub, win_ref[...])
        z_ref[...] = z
        cos_t, sin_t = _rope_tables(pos_ref[...], tab_ref[...])
        qn, _ = _rms_fwd(z[:, 0:Q_RANK], gq_ref[...])
        qnb = qn.astype(BF16)
        qn_ref[...] = qnb
        q = _dot_nt(qnb, wq_ref[...])
        kvn, _ = _rms_fwd(z[:, Q_RANK:Q_RANK + KV_RANK], gkv_ref[...])
        kvnb = kvn.astype(BF16)
        kvn_ref[...] = kvnb
        kv = _dot_nn(kvnb, wkv_ref[...])
        k_pe = _rope_apply(z[:, Q_RANK + KV_RANK:Q_RANK + KV_RANK + 128], cos_t, sin_t)
        ones = jnp.ones((tm, V_DIM), F32)
        for hh in range(MLA_HEADS):
            b = hh * HEAD_PAD
            q_pe = _rope_apply(q[:, b + NOPE:b + HEAD_PAD], cos_t, sin_t)
            q_ref[hh] = jnp.concatenate([q[:, b:b + NOPE], q_pe], axis=-1).astype(BF16)
            k_ref[hh] = jnp.concatenate([kv[:, b:b + NOPE], k_pe], axis=-1).astype(BF16)
            v_ref[hh] = jnp.concatenate([kv[:, b + NOPE:b + HEAD_PAD], ones], axis=-1).astype(BF16)

    full = lambda shape: pl.BlockSpec(shape, lambda i: (0,) * len(shape))
    return _call_after(
        token, body,
        [pl.BlockSpec((tm, d), lambda i: (i, 0)), _resident((1, d)), _resident(w_in.shape), _resident((1, Q_RANK)),
         _resident(wq_t.shape), _resident((1, KV_RANK)), _resident(wkv.shape),
         pl.BlockSpec((tm, 1), lambda i: (i, 0)), _resident(rope_tab.shape)],
        (h1, mix_norm, w_in, q_norm, wq_t, kv_norm, wkv, pos, rope_tab),
        name="mix_prep", grid=(t // tm,),
        out_specs=[pl.BlockSpec((tm, d), lambda i: (i, 0)),
                   pl.BlockSpec((tm, d), lambda i: (i, 0)),
                   pl.BlockSpec((tm, Q_RANK), lambda i: (i, 0)),
                   pl.BlockSpec((tm, KV_RANK), lambda i: (i, 0)),
                   pl.BlockSpec((MLA_HEADS, tm, HEAD_PAD), lambda i: (0, i, 0)),
                   pl.BlockSpec((MLA_HEADS, tm, HEAD_PAD), lambda i: (0, i, 0)),
                   pl.BlockSpec((MLA_HEADS, tm, 2 * V_DIM), lambda i: (0, i, 0))],
        out_shape=[jax.ShapeDtypeStruct((t, d), BF16), jax.ShapeDtypeStruct((t, d), F32),
                   jax.ShapeDtypeStruct((t, Q_RANK), BF16), jax.ShapeDtypeStruct((t, KV_RANK), BF16),
                   jax.ShapeDtypeStruct((MLA_HEADS, t, HEAD_PAD), BF16),
                   jax.ShapeDtypeStruct((MLA_HEADS, t, HEAD_PAD), BF16),
                   jax.ShapeDtypeStruct((MLA_HEADS, t, 2 * V_DIM), BF16)],
        compiler_params=_cparams(),
    )


def _causal_mask(s):
    row = lax.broadcasted_iota(jnp.int32, s.shape, 0)
    col = lax.broadcasted_iota(jnp.int32, s.shape, 1)
    return jnp.where(col <= row, s, NEG_BIG)


def _attn_fwd(q, k, v):
    nh, t, _ = q.shape
    tq = tk = min(512, t)
    nq, nk = t // tq, t // tk

    pairs = [(i, j) for i in range(nq) for j in range(i + 1)]
    qi = jnp.asarray(np.array([i for i, _ in pairs], np.int32))
    kj = jnp.asarray(np.array([j for _, j in pairs], np.int32))

    def body(qi_ref, kj_ref, q_ref, k_ref, v_ref, o_ref, lse_ref, m_sc, acc_sc):
        n = pl.program_id(0)
        i, j = qi_ref[n], kj_ref[n]

        @pl.when(j == 0)
        def _():
            m_sc[...] = jnp.full_like(m_sc, NEG_BIG)
            acc_sc[...] = jnp.zeros_like(acc_sc)

        def step(diagonal):
            for hh in range(nh):
                s = _dot_nt(q_ref[hh], k_ref[hh]) * ATTN_SCALE
                if diagonal:
                    s = _causal_mask(s)
                m_old = m_sc[hh]
                m_new = jnp.maximum(m_old, jnp.max(s, axis=-1, keepdims=True))
                p = jnp.exp(s - m_new).astype(BF16)
                acc_sc[hh] = jnp.exp(m_old - m_new) * acc_sc[hh] + _dot_nn(p, v_ref[hh])
                m_sc[hh] = m_new

        @pl.when(j < i)
        def _():
            step(False)

        @pl.when(j == i)
        def _():
            step(True)
            for hh in range(nh):
                acc = acc_sc[hh]
                l = acc[:, V_DIM:2 * V_DIM]
                o_ref[:, hh * V_DIM:(hh + 1) * V_DIM] = (acc[:, 0:V_DIM] / l).astype(BF16)
                lse_ref[hh] = m_sc[hh] + jnp.log(l[:, 0:1])

    q_map = lambda n, qi_ref, kj_ref: (0, qi_ref[n], 0)
    kv_map = lambda n, qi_ref, kj_ref: (0, kj_ref[n], 0)
    return pl.pallas_call(
        body, name="attn_fwd",
        grid_spec=pltpu.PrefetchScalarGridSpec(
            num_scalar_prefetch=2, grid=(len(pairs),),
            in_specs=[pl.BlockSpec((nh, tq, HEAD_PAD), q_map),
                      pl.BlockSpec((nh, tk, HEAD_PAD), kv_map),
                      pl.BlockSpec((nh, tk, 2 * V_DIM), kv_map)],
            out_specs=[pl.BlockSpec((tq, nh * V_DIM), lambda n, qi_ref, kj_ref: (qi_ref[n], 0)),
                       pl.BlockSpec((nh, tq, 1), q_map)],
            scratch_shapes=[pltpu.VMEM((nh, tq, 1), F32), pltpu.VMEM((nh, tq, 2 * V_DIM), F32)]),
        out_shape=[jax.ShapeDtypeStruct((t, nh * V_DIM), BF16), jax.ShapeDtypeStruct((nh, t, 1), F32)],
        compiler_params=_cparams(),
    )(qi, kj, q, k, v)


def _attn_delta(o, do):
    t, w = o.shape
    nh = w // V_DIM
    tm = min(512, t)

    def body(o_ref, do_ref, d_ref):
        prod = o_ref[...].astype(F32) * do_ref[...].astype(F32)
        for hh in range(nh):
            d_ref[hh] = jnp.sum(prod[:, hh * V_DIM:(hh + 1) * V_DIM], axis=-1, keepdims=True)

    return pl.pallas_call(
        body, name="attn_delta", grid=(t // tm,),
        in_specs=[pl.BlockSpec((tm, w), lambda i: (i, 0)), pl.BlockSpec((tm, w), lambda i: (i, 0))],
        out_specs=pl.BlockSpec((nh, tm, 1), lambda i: (0, i, 0)),
        out_shape=jax.ShapeDtypeStruct((nh, t, 1), F32),
        compiler_params=_cparams(),
    )(o, do)


ATTN_BWD_HEADS = 2


def _attn_bwd(q, k, v, do, lse, delta):
    nh, t, _ = q.shape
    hp = ATTN_BWD_HEADS
    tq = tk = min(512, t)
    nq, nk = t // tq, t // tk

    pairs = [(j, i) for j in range(nk) for i in range(j, nq)]
    kj = jnp.asarray(np.array([j for j, _ in pairs], np.int32))
    qi = jnp.asarray(np.array([i for _, i in pairs], np.int32))

    def body(kj_ref, qi_ref, q_ref, k_ref, v_ref, do_ref, lse_ref, dlt_ref, dq_ref, dk_ref, dv_ref):
        n = pl.program_id(1)
        j, i = kj_ref[n], qi_ref[n]

        @pl.when(n == 0)
        def _():
            dq_ref[...] = jnp.zeros_like(dq_ref)

        def step(diagonal):
            for hh in range(hp):
                qq, kk = q_ref[hh], k_ref[hh]
                dob = do_ref[:, hh * V_DIM:(hh + 1) * V_DIM]
                s = _dot_nt(qq, kk) * ATTN_SCALE
                if diagonal:
                    s = _causal_mask(s)
                p = jnp.exp(s - lse_ref[hh])
                dpp = _dot_nt(dob, v_ref[hh])
                dsb = (p * (dpp - dlt_ref[hh]) * ATTN_SCALE).astype(BF16)
                _accumulate(dv_ref.at[hh], _dot_tn(p.astype(BF16), dob), diagonal)
                _accumulate(dk_ref.at[hh], _dot_tn(dsb, qq), diagonal)
                dq_ref[hh, pl.ds(pl.multiple_of(i * tq, tq), tq), :] += _dot_nn(dsb, kk)

        @pl.when(i > j)
        def _():
            step(False)

        @pl.when(i == j)
        def _():
            step(True)

    q_map = lambda h, n, kj_ref, qi_ref: (h, qi_ref[n], 0)
    k_map = lambda h, n, kj_ref, qi_ref: (h, kj_ref[n], 0)
    return pl.pallas_call(
        body, name="attn_bwd",
        grid_spec=pltpu.PrefetchScalarGridSpec(
            num_scalar_prefetch=2, grid=(nh // hp, len(pairs)),
            in_specs=[pl.BlockSpec((hp, tq, HEAD_PAD), q_map),
                      pl.BlockSpec((hp, tk, HEAD_PAD), k_map),
                      pl.BlockSpec((hp, tk, V_DIM), k_map),
                      pl.BlockSpec((tq, hp * V_DIM), lambda h, n, kj_ref, qi_ref: (qi_ref[n], h)),
                      pl.BlockSpec((hp, tq, 1), q_map),
                      pl.BlockSpec((hp, tq, 1), q_map)],
            out_specs=[pl.BlockSpec((hp, t, HEAD_PAD), lambda h, n, kj_ref, qi_ref: (h, 0, 0)),
                       pl.BlockSpec((hp, tk, HEAD_PAD), k_map),
                       pl.BlockSpec((hp, tk, V_DIM), k_map)]),
        out_shape=[jax.ShapeDtypeStruct((nh, t, HEAD_PAD), F32), jax.ShapeDtypeStruct((nh, t, HEAD_PAD), F32),
                   jax.ShapeDtypeStruct((nh, t, V_DIM), F32)],
        compiler_params=_cparams(),
    )(kj, qi, q, k, v, do, lse, delta)


def _pool_counts(first_token, rows, w):
    tok = lax.broadcasted_iota(jnp.int32, (rows, POOL_CH), 0) + first_token
    return jnp.minimum(tok + 1, w).astype(F32)


def _pool_centered(zbuf, g, w, i, tm):
    lanes = pl.ds(g * POOL_CH, POOL_CH)
    cur = zbuf[pl.ds(POOL_HALO, tm), lanes]
    win = cur
    for s in range(1, w):
        win = win + zbuf[pl.ds(POOL_HALO - s, tm), lanes]
    return win / _pool_counts(i * tm, tm, w) - cur


def _pool_load(zbuf, z_ref, halo_ref, i, tm):
    @pl.when(i == 0)
    def _():
        zbuf[pl.ds(0, POOL_HALO), :] = jnp.zeros((POOL_HALO, zbuf.shape[1]), F32)

    @pl.when(i > 0)
    def _():
        zbuf[pl.ds(0, POOL_HALO), :] = halo_ref[...]

    zbuf[pl.ds(POOL_HALO, tm), :] = z_ref[...]


def _pool_fwd(z, pool_w, pool_scale):
    t = z.shape[0]
    pw = len(POOL_WINDOWS) * POOL_CH
    tm = min(512, t)
    hb = tm // POOL_HALO

    def body(z_ref, halo_ref, w_ref, sc_ref, p_ref, zbuf):
        i = pl.program_id(0)
        _pool_load(zbuf, z_ref, halo_ref, i, tm)
        for g, w in enumerate(POOL_WINDOWS):
            c = _pool_centered(zbuf, g, w, i, tm)
            y = _dot_nn(c.astype(BF16), w_ref[g]) * sc_ref[:, g * POOL_CH:(g + 1) * POOL_CH]
            p_ref[:, g * POOL_CH:(g + 1) * POOL_CH] = y.astype(BF16)

    return pl.pallas_call(
        body, name="pool_fwd", grid=(t // tm,),
        in_specs=[pl.BlockSpec((tm, pw), lambda i: (i, 1)),
                  pl.BlockSpec((POOL_HALO, pw), lambda i: (jnp.maximum(i * hb - 1, 0), 1)),
                  pl.BlockSpec(pool_w.shape, lambda i: (0, 0, 0)),
                  pl.BlockSpec((1, pw), lambda i: (0, 0))],
        out_specs=pl.BlockSpec((tm, pw), lambda i: (i, 0)),
        out_shape=jax.ShapeDtypeStruct((t, pw), BF16),
        scratch_shapes=[pltpu.VMEM((POOL_HALO + tm, pw), F32)],
        compiler_params=_cparams(),
    )(z, z, pool_w, pool_scale)


def _pool_bwd(dp, z, pool_w, pool_scale):
    t = z.shape[0]
    ng = len(POOL_WINDOWS)
    pw = ng * POOL_CH
    tm = min(512, t)
    hb = tm // POOL_HALO
    nt = t // tm

    def body(dp_ref, dpn_ref, z_ref, halo_ref, w_ref, sc_ref, dz_ref, dw_ref, dsc_ref, zbuf, dbuf):
        i = pl.program_id(0)
        _pool_load(zbuf, z_ref, halo_ref, i, tm)

        @pl.when(i == 0)
        def _():
            dw_ref[...] = jnp.zeros_like(dw_ref)
            dsc_ref[...] = jnp.zeros_like(dsc_ref)

        nxt_ok = (i < nt - 1).astype(F32)
        for g, w in enumerate(POOL_WINDOWS):
            lanes = pl.ds(g * POOL_CH, POOL_CH)
            cols = slice(g * POOL_CH, (g + 1) * POOL_CH)
            sc = sc_ref[:, cols]
            wg = w_ref[g]
            c = _pool_centered(zbuf, g, w, i, tm).astype(BF16)
            ypre = _dot_nn(c, wg)
            dpg = dp_ref[:, cols].astype(F32)
            dsc_ref[:, cols] += jnp.sum(dpg * ypre, axis=0, keepdims=True)
            dyb = (dpg * sc).astype(BF16)
            dw_ref[g] += _dot_tn(c, dyb)
            dd = _dot_nt(dyb, wg)
            dyn = (dpn_ref[:, cols].astype(F32) * sc).astype(BF16)
            ddn = _dot_nt(dyn, wg) * nxt_ok
            dbuf[pl.ds(0, tm), lanes] = dd / _pool_counts(i * tm, tm, w)
            dbuf[pl.ds(tm, POOL_HALO), lanes] = ddn / _pool_counts((i + 1) * tm, POOL_HALO, w)
            acc = -dd
            for s in range(w):
                acc = acc + dbuf[pl.ds(s, tm), lanes]
            dz_ref[:, cols] = acc

    return pl.pallas_call(
        body, name="pool_bwd", grid=(nt,),
        in_specs=[pl.BlockSpec((tm, pw), lambda i: (i, 0)),
                  pl.BlockSpec((POOL_HALO, pw), lambda i: (jnp.minimum((i + 1) * hb, t // POOL_HALO - 1), 0)),
                  pl.BlockSpec((tm, pw), lambda i: (i, 1)),
                  pl.BlockSpec((POOL_HALO, pw), lambda i: (jnp.maximum(i * hb - 1, 0), 1)),
                  pl.BlockSpec(pool_w.shape, lambda i: (0, 0, 0)),
                  pl.BlockSpec((1, pw), lambda i: (0, 0))],
        out_specs=[pl.BlockSpec((tm, pw), lambda i: (i, 0)),
                   pl.BlockSpec((ng, POOL_CH, POOL_CH), lambda i: (0, 0, 0)),
                   pl.BlockSpec((1, pw), lambda i: (0, 0))],
        out_shape=[jax.ShapeDtypeStruct((t, pw), F32), jax.ShapeDtypeStruct((ng, POOL_CH, POOL_CH), F32),
                   jax.ShapeDtypeStruct((1, pw), F32)],
        scratch_shapes=[pltpu.VMEM((POOL_HALO + tm, pw), F32), pltpu.VMEM((tm + POOL_HALO, pw), F32)],
        compiler_params=_cparams(),
    )(dp, dp, z, z, pool_w, pool_scale)


def _mla_bwd(dq_h, dk_h, dv_h, z, dz_pool, h1, dh2, mix_norm, w_in, q_norm, wq_t, kv_norm, wkv, pos, rope_tab):
    t, d = h1.shape
    tm = min(512, t)

    def body(dqh_ref, dkh_ref, dvh_ref, z_ref, dzp_ref, h_ref, dh2_ref, gm_ref, win_ref, gq_ref, wq_ref, gkv_ref,
             wkv_ref, pos_ref, tab_ref, dh1_ref, dq_ref, dkv_ref, dz_ref, dgq_ref, dgkv_ref, dgm_ref):
        i = pl.program_id(0)
        first = i == 0
        cos_t, sin_t = _rope_tables(pos_ref[...], tab_ref[...])
        dq_parts, dkv_parts = [], []
        dk_pe = jnp.zeros((tm, 128), F32)
        for hh in range(MLA_HEADS):
            dqh = dqh_ref[hh]
            dq_parts += [dqh[:, 0:NOPE], _rope_apply_t(dqh[:, NOPE:HEAD_PAD], cos_t, sin_t)]
            dkh = dkh_ref[hh]
            dkv_parts += [dkh[:, 0:NOPE], dvh_ref[hh]]
            dk_pe = dk_pe + dkh[:, NOPE:HEAD_PAD]
        dqb = jnp.concatenate(dq_parts, axis=-1).astype(BF16)
        dkvb = jnp.concatenate(dkv_parts, axis=-1).astype(BF16)
        dq_ref[...] = dqb
        dkv_ref[...] = dkvb
        z = z_ref[...]
        c_q = z[:, 0:Q_RANK]
        gq = gq_ref[...]
        _, rq = _rms_fwd(c_q, gq)
        dcq, dgq = _rms_bwd(_dot_nn(dqb, wq_ref[...]), c_q, gq, rq)
        c_kv = z[:, Q_RANK:Q_RANK + KV_RANK]
        gkv = gkv_ref[...]
        _, rkv = _rms_fwd(c_kv, gkv)
        dckv, dgkv = _rms_bwd(_dot_nt(dkvb, wkv_ref[...]), c_kv, gkv, rkv)
        dkr = _rope_apply_t(dk_pe, cos_t, sin_t)
        dzb = jnp.concatenate([dcq, dckv, dkr, dzp_ref[...]], axis=-1).astype(BF16)
        dz_ref[...] = dzb
        x = h_ref[...]
        gm = gm_ref[...]
        _, rm = _rms_fwd(x, gm)
        dx, dgm = _rms_bwd(_dot_nt(dzb, win_ref[...]), x, gm, rm)
        dh1_ref[...] = dh2_ref[...] + dx
        _accumulate(dgq_ref, dgq, first)
        _accumulate(dgkv_ref, dgkv, first)
        _accumulate(dgm_ref, dgm, first)

    full = lambda shape: pl.BlockSpec(shape, lambda i: (0,) * len(shape))
    row = lambda w: pl.BlockSpec((tm, w), lambda i: (i, 0))
    head = lambda w: pl.BlockSpec((MLA_HEADS, tm, w), lambda i: (0, i, 0))
    pw = len(POOL_WINDOWS) * POOL_CH
    return pl.pallas_call(
        body, name="mla_bwd", grid=(t // tm,),
        in_specs=[head(HEAD_PAD), head(HEAD_PAD), head(V_DIM), row(d), row(pw), row(d), row(d),
                  _resident((1, d)), _resident(w_in.shape), _resident((1, Q_RANK)), _resident(wq_t.shape),
                  _resident((1, KV_RANK)), _resident(wkv.shape), row(1), _resident(rope_tab.shape)],
        out_specs=[row(d), row(d), row(d), row(d), full((1, Q_RANK)), full((1, KV_RANK)), full((1, d))],
        out_shape=[jax.ShapeDtypeStruct((t, d), F32), jax.ShapeDtypeStruct((t, d), BF16),
                   jax.ShapeDtypeStruct((t, d), BF16), jax.ShapeDtypeStruct((t, d), BF16),
                   jax.ShapeDtypeStruct((1, Q_RANK), F32), jax.ShapeDtypeStruct((1, KV_RANK), F32),
                   jax.ShapeDtypeStruct((1, d), F32)],
        compiler_params=_cparams(),
    )(dq_h, dk_h, dv_h, z, dz_pool, h1, dh2, mix_norm, w_in, q_norm, wq_t, kv_norm, wkv, pos, rope_tab)


def _mem_kv(mem, mem_norm, wmkv):
    n, d = mem.shape

    def body(mem_ref, g_ref, w_ref, memn_ref, k_ref, v_ref):
        y, _ = _rms_fwd(mem_ref[...], g_ref[...])
        yb = y.astype(BF16)
        memn_ref[...] = yb
        for hh in range(MEM_HEADS):
            k_ref[hh] = _dot_nn(yb, w_ref[hh]).astype(BF16)
            v_ref[hh] = _dot_nn(yb, w_ref[MEM_HEADS + hh]).astype(BF16)

    return pl.pallas_call(
        body, name="mem_kv",
        out_shape=[jax.ShapeDtypeStruct((n, d), BF16), jax.ShapeDtypeStruct((MEM_HEADS, n, MEM_HD), BF16),
                   jax.ShapeDtypeStruct((MEM_HEADS, n, MEM_HD), BF16)],
        compiler_params=_cparams(),
    )(mem, mem_norm, wmkv)


def _mem_softmax(qb, km):
    s = _dot_nt(qb, km) * MEM_SCALE
    e = jnp.exp(s - jnp.max(s, axis=-1, keepdims=True))
    return e / jnp.sum(e, axis=-1, keepdims=True)


def _xattn_fwd(h1, a, p, w_out, g, wmq, km, vm, wmo, token=None):
    t, d = h1.shape
    tm = min(512, t)
    half = a.shape[1]

    def body(h_ref, a_ref, p_ref, wo_ref, g_ref, wmq_ref, km_ref, vm_ref, wmo_ref,
             h2_ref, h3_ref, hn_ref, q_ref, o_ref):
        h2 = h_ref[...] + _dot_nn(a_ref[...], wo_ref[0:half, :]) + _dot_nn(p_ref[...], wo_ref[half:2 * half, :])
        h2_ref[...] = h2
        hn, _ = _rms_fwd(h2, g_ref[...])
        hnb = hn.astype(BF16)
        hn_ref[...] = hnb
        qb = _dot_nn(hnb, wmq_ref[...]).astype(BF16)
        q_ref[...] = qb
        outs = []
        for hh in range(MEM_HEADS):
            pr = _mem_softmax(qb[:, hh * MEM_HD:(hh + 1) * MEM_HD], km_ref[hh])
            outs.append(_dot_nn(pr.astype(BF16), vm_ref[hh]))
        ob = jnp.concatenate(outs, axis=-1).astype(BF16)
        o_ref[...] = ob
        h3_ref[...] = h2 + _dot_nn(ob, wmo_ref[...])

    full = lambda shape: pl.BlockSpec(shape, lambda i: (0,) * len(shape))
    row = lambda w: pl.BlockSpec((tm, w), lambda i: (i, 0))
    return _call_after(
        token, body,
        [row(d), row(half), row(half), _resident(w_out.shape), _resident((1, d)), _resident(wmq.shape),
         _resident(km.shape), _resident(vm.shape), _resident(wmo.shape)],
        (h1, a, p, w_out, g, wmq, km, vm, wmo),
        name="xattn_fwd", grid=(t // tm,),
        out_specs=[row(d), row(d), row(d), row(d), row(d)],
        out_shape=[jax.ShapeDtypeStruct((t, d), F32), jax.ShapeDtypeStruct((t, d), F32),
                   jax.ShapeDtypeStruct((t, d), BF16), jax.ShapeDtypeStruct((t, d), BF16),
                   jax.ShapeDtypeStruct((t, d), BF16)],
        compiler_params=_cparams(),
    )


def _xattn_bwd(dh3, h2, qm, g, wmq, km, vm, wmo, w_out, token=None):
    t, d = h2.shape
    tm = min(512, t)
    half = d // 2

    def body(dh3_ref, h2_ref, q_ref, g_ref, wmq_ref, km_ref, vm_ref, wmo_ref, wo_ref,
             dh2_ref, dq_ref, da_ref, dp_ref, dk_ref, dv_ref, dg_ref):
        i = pl.program_id(0)
        first = i == 0

        @pl.when(first)
        def _():
            dk_ref[...] = jnp.zeros_like(dk_ref)
            dv_ref[...] = jnp.zeros_like(dv_ref)

        dh3 = dh3_ref[...]
        dob = _dot_nt(dh3.astype(BF16), wmo_ref[...]).astype(BF16)
        qb = q_ref[...]
        dq_parts = []
        for hh in range(MEM_HEADS):
            cols = slice(hh * MEM_HD, (hh + 1) * MEM_HD)
            kk, vv = km_ref[hh], vm_ref[hh]
            pr = _mem_softmax(qb[:, cols], kk)
            doh = dob[:, cols]
            dv_ref[hh] += _dot_tn(pr.astype(BF16), doh)
            dpp = _dot_nt(doh, vv)
            dsb = (pr * (dpp - jnp.sum(dpp * pr, axis=-1, keepdims=True)) * MEM_SCALE).astype(BF16)
            dq_parts.append(_dot_nn(dsb, kk))
            dk_ref[hh] += _dot_tn(dsb, qb[:, cols])
        dqb = jnp.concatenate(dq_parts, axis=-1).astype(BF16)
        dq_ref[...] = dqb
        x = h2_ref[...]
        gg = g_ref[...]
        _, r = _rms_fwd(x, gg)
        dx, dg = _rms_bwd(_dot_nt(dqb, wmq_ref[...]), x, gg, r)
        dh2 = dh3 + dx
        dh2_ref[...] = dh2
        dap = _dot_nt(dh2.astype(BF16), wo_ref[...])
        da_ref[...] = dap[:, 0:half].astype(BF16)
        dp_ref[...] = dap[:, half:d].astype(BF16)
        _accumulate(dg_ref, dg, first)

    full = lambda shape: pl.BlockSpec(shape, lambda i: (0,) * len(shape))
    row = lambda w: pl.BlockSpec((tm, w), lambda i: (i, 0))
    return _call_after(
        token, body,
        [row(d), row(d), row(d), _resident((1, d)), _resident(wmq.shape), _resident(km.shape), _resident(vm.shape),
         _resident(wmo.shape), _resident(w_out.shape)],
        (dh3, h2, qm, g, wmq, km, vm, wmo, w_out),
        name="xattn_bwd", grid=(t // tm,),
        out_specs=[row(d), row(d), row(half), row(half), full(km.shape), full(vm.shape), full((1, d))],
        out_shape=[jax.ShapeDtypeStruct((t, d), F32), jax.ShapeDtypeStruct((t, d), BF16),
                   jax.ShapeDtypeStruct((t, half), BF16), jax.ShapeDtypeStruct((t, half), BF16),
                   jax.ShapeDtypeStruct(km.shape, F32), jax.ShapeDtypeStruct(vm.shape, F32),
                   jax.ShapeDtypeStruct((1, d), F32)],
        compiler_params=_cparams(),
    )


def _mem_kv_bwd(dkm, dvm, memn, mem, mem_norm, wmkv):
    n, d = mem.shape

    def body(dk_ref, dv_ref, memn_ref, mem_ref, g_ref, w_ref, dw_ref, dg_ref):
        memn = memn_ref[...]
        dmemn = jnp.zeros((n, d), F32)
        for s in range(2 * MEM_HEADS):
            src = dk_ref[s] if s < MEM_HEADS else dv_ref[s - MEM_HEADS]
            db = src.astype(BF16)
            dw_ref[s] = _dot_tn(memn, db)
            dmemn = dmemn + _dot_nt(db, w_ref[s])
        x = mem_ref[...]
        gg = g_ref[...]
        _, r = _rms_fwd(x, gg)
        _, dg = _rms_bwd(dmemn, x, gg, r)
        dg_ref[...] = dg

    return pl.pallas_call(
        body, name="mem_kv_bwd",
        out_shape=[jax.ShapeDtypeStruct(wmkv.shape, F32), jax.ShapeDtypeStruct((1, d), F32)],
        compiler_params=_cparams(),
    )(dkm, dvm, memn, mem, mem_norm, wmkv)


MESH_ID = pl.DeviceIdType.MESH
ANY = pl.BlockSpec(memory_space=pl.ANY)


def _coords():
    return lax.axis_index("x"), lax.axis_index("y"), lax.axis_index("c")


def _other_chips(x, y):
    return [(1 - x, y), (x, 1 - y), (1 - x, 1 - y)]


def _core_reduce(g, tag):
    _, r, w = g.shape

    def body(g_ref, part_ref, own_sc, recv_sc, send_sems, recv_sems, local_sems):
        x, y, c = _coords()
        sent, local = [], []
        for chip in range(4):
            sent.append(pltpu.make_async_remote_copy(
                src_ref=g_ref.at[2 * chip + (1 - c)], dst_ref=recv_sc.at[chip],
                send_sem=send_sems.at[chip], recv_sem=recv_sems.at[chip],
                device_id=(x, y, 1 - c), device_id_type=MESH_ID))
            local.append(pltpu.make_async_copy(g_ref.at[2 * chip + c], own_sc.at[chip], local_sems.at[chip]))
        for cp in sent + local:
            cp.start()
        for chip in range(4):
            local[chip].wait()
            sent[chip].wait_recv()
            part_ref[chip] = (own_sc[chip].astype(F32) + recv_sc[chip].astype(F32)).astype(part_ref.dtype)
        for cp in sent:
            cp.wait_send()

    return pl.pallas_call(
        body, name="core_reduce_" + tag,
        out_shape=jax.ShapeDtypeStruct((4, r, w), g.dtype),
        in_specs=[ANY], out_specs=pl.BlockSpec(memory_space=pltpu.VMEM),
        scratch_shapes=[pltpu.VMEM((4, r, w), g.dtype), pltpu.VMEM((4, r, w), g.dtype),
                        pltpu.SemaphoreType.DMA((4,)), pltpu.SemaphoreType.DMA((4,)), pltpu.SemaphoreType.DMA((4,))],
        compiler_params=_cparams(),
    )(g)


HBM_SPEC = pl.BlockSpec(memory_space=pltpu.HBM)
SEM_SPEC = pl.BlockSpec(memory_space=pltpu.SEMAPHORE)
SPLIT_EFFECT = pltpu.SideEffectType.DATAFLOW_SIDE_EFFECTING


def _ici_refs(gather, src_ref, land_ref, j, px, py, slot_chip, c):
    if gather:
        return src_ref, land_ref.at[:, 4 * slot_chip[0] + 2 * slot_chip[1] + c]
    return src_ref.at[2 * px + py], land_ref.at[j]


def _ici_start(src, after, name, gather):
    r, w = src.shape[-2:]
    land_shape = (src.shape[0], N_DEV, r, w) if gather else (3, r, w)

    def body(src_ref, land_ref, after_ref, send_sems, recv_sems, src_thru, land_thru, token):
        x, y, c = _coords()
        for j, (px, py) in enumerate(_other_chips(x, y)):
            s_ref, d_ref = _ici_refs(gather, src_ref, land_ref, j, px, py, (x, y), c)
            pltpu.make_async_remote_copy(
                src_ref=s_ref, dst_ref=d_ref, send_sem=send_sems.at[j], recv_sem=recv_sems.at[j],
                device_id=(px, py, c), device_id_type=MESH_ID).start()
        token[...] = jnp.zeros_like(token)

    return pl.pallas_call(
        body, name=name,
        out_shape=(pltpu.SemaphoreType.DMA((3,)), pltpu.SemaphoreType.DMA((3,)), pltpu.HBM(src.shape, src.dtype),
                   pltpu.HBM(land_shape, src.dtype), jax.ShapeDtypeStruct((8, 128), F32)),
        in_specs=(HBM_SPEC, HBM_SPEC, ANY),
        out_specs=(SEM_SPEC, SEM_SPEC, HBM_SPEC, HBM_SPEC, pl.BlockSpec(memory_space=pltpu.VMEM)),
        input_output_aliases={0: 2, 1: 3},
        compiler_params=pltpu.CompilerParams(has_side_effects=SPLIT_EFFECT),
    )(pltpu.with_memory_space_constraint(src, pltpu.HBM),
      pltpu.with_memory_space_constraint(lax.empty(land_shape, src.dtype), pltpu.HBM), after)


def _ici_wait(started, after, name, gather):
    send_sems, recv_sems, src_thru, land_thru, _ = started

    def body(src_ref, land_ref, send_sems, recv_sems, after_ref, src_dead, got_ref):
        x, y, c = _coords()
        for j, (px, py) in enumerate(_other_chips(x, y)):
            s_ref, d_ref = _ici_refs(gather, src_ref, land_ref, j, px, py, (px, py), c)
            copy = pltpu.make_async_remote_copy(
                src_ref=s_ref, dst_ref=d_ref, send_sem=send_sems.at[j], recv_sem=recv_sems.at[j],
                device_id=(px, py, c), device_id_type=MESH_ID)
            copy.wait_send()
            copy.wait_recv()

    return pl.pallas_call(
        body, name=name,
        out_shape=(pltpu.HBM(src_thru.shape, src_thru.dtype), pltpu.HBM(land_thru.shape, land_thru.dtype)),
        in_specs=(HBM_SPEC, HBM_SPEC, SEM_SPEC, SEM_SPEC, ANY),
        out_specs=(HBM_SPEC, HBM_SPEC), input_output_aliases={0: 0, 1: 1},
        compiler_params=pltpu.CompilerParams(has_side_effects=SPLIT_EFFECT),
    )(src_thru, land_thru, send_sems, recv_sems, after)


def _neighbour(k, x, y):
    return (1 - x, y) if k == 0 else (x, 1 - y)


def _slot(ref, px, py, c):
    return ref.at[:, 4 * px + 2 * py + c]


def _near_start(src, after, name):
    land_shape = (src.shape[0], N_DEV) + src.shape[1:]

    def body(src_ref, land_ref, after_ref, send_sems, recv_sems, src_thru, land_thru, token):
        x, y, c = _coords()
        for k in range(2):
            px, py = _neighbour(k, x, y)
            pltpu.make_async_remote_copy(
                src_ref=src_ref, dst_ref=_slot(land_ref, x, y, c), send_sem=send_sems.at[k],
                recv_sem=recv_sems.at[k], device_id=(px, py, c), device_id_type=MESH_ID).start()
        token[...] = jnp.zeros_like(token)

    return pl.pallas_call(
        body, name=name,
        out_shape=(pltpu.SemaphoreType.DMA((2,)), pltpu.SemaphoreType.DMA((2,)), pltpu.HBM(src.shape, src.dtype),
                   pltpu.HBM(land_shape, src.dtype), jax.ShapeDtypeStruct((8, 128), F32)),
        in_specs=(HBM_SPEC, HBM_SPEC, ANY),
        out_specs=(SEM_SPEC, SEM_SPEC, HBM_SPEC, HBM_SPEC, pl.BlockSpec(memory_space=pltpu.VMEM)),
        input_output_aliases={0: 2, 1: 3},
        compiler_params=pltpu.CompilerParams(has_side_effects=SPLIT_EFFECT),
    )(pltpu.with_memory_space_constraint(src, pltpu.HBM),
      pltpu.with_memory_space_constraint(lax.empty(land_shape, src.dtype), pltpu.HBM), after)


def _near_wait(started, after, name):
    send_sems, recv_sems, src_thru, land_thru, _ = started

    def body(src_ref, land_ref, send_sems, recv_sems, after_ref, src_dead, got_ref):
        x, y, c = _coords()
        for k in range(2):
            px, py = _neighbour(k, x, y)
            copy = pltpu.make_async_remote_copy(
                src_ref=src_ref, dst_ref=_slot(land_ref, px, py, c), send_sem=send_sems.at[k],
                recv_sem=recv_sems.at[k], device_id=(px, py, c), device_id_type=MESH_ID)
            copy.wait_send()
            copy.wait_recv()

    return pl.pallas_call(
        body, name=name,
        out_shape=(pltpu.HBM(src_thru.shape, src_thru.dtype), pltpu.HBM(land_thru.shape, land_thru.dtype)),
        in_specs=(HBM_SPEC, HBM_SPEC, SEM_SPEC, SEM_SPEC, ANY),
        out_specs=(HBM_SPEC, HBM_SPEC), input_output_aliases={0: 0, 1: 1},
        compiler_params=pltpu.CompilerParams(has_side_effects=SPLIT_EFFECT),
    )(src_thru, land_thru, send_sems, recv_sems, after)


def _far_refs(land_ref, k, x, y, c, arriving):
    half = land_ref.shape[2] // 2
    rows = pl.ds(k * half, half)
    ox, oy = (1 - x, 1 - y) if arriving else _neighbour(k, x, y)
    return land_ref.at[:, 4 * ox + 2 * oy + c, rows]


def _far_start(land, after, name):
    def body(land_ref, after_ref, send_sems, recv_sems, land_thru, token):
        x, y, c = _coords()
        for k in range(2):
            block = _far_refs(land_ref, k, x, y, c, False)
            px, py = _neighbour(1 - k, x, y)
            pltpu.make_async_remote_copy(
                src_ref=block, dst_ref=block, send_sem=send_sems.at[k], recv_sem=recv_sems.at[k],
                device_id=(px, py, c), device_id_type=MESH_ID).start()
        token[...] = jnp.zeros_like(token)

    return pl.pallas_call(
        body, name=name,
        out_shape=(pltpu.SemaphoreType.DMA((2,)), pltpu.SemaphoreType.DMA((2,)),
                   pltpu.HBM(land.shape, land.dtype), jax.ShapeDtypeStruct((8, 128), F32)),
        in_specs=(HBM_SPEC, ANY),
        out_specs=(SEM_SPEC, SEM_SPEC, HBM_SPEC, pl.BlockSpec(memory_space=pltpu.VMEM)),
        input_output_aliases={0: 2},
        compiler_params=pltpu.CompilerParams(has_side_effects=SPLIT_EFFECT),
    )(pltpu.with_memory_space_constraint(land, pltpu.HBM), after)


def _far_wait(started, after, name):
    send_sems, recv_sems, land_thru, _ = started

    def body(land_ref, send_sems, recv_sems, after_ref, got_ref):
        x, y, c = _coords()
        for k in range(2):
            px, py = _neighbour(1 - k, x, y)
            copy = pltpu.make_async_remote_copy(
                src_ref=_far_refs(land_ref, k, x, y, c, False), dst_ref=_far_refs(land_ref, k, x, y, c, True),
                send_sem=send_sems.at[k], recv_sem=recv_sems.at[k], device_id=(px, py, c), device_id_type=MESH_ID)
            copy.wait_send()
            copy.wait_recv()

    return pl.pallas_call(
        body, name=name,
        out_shape=pltpu.HBM(land_thru.shape, land_thru.dtype),
        in_specs=(HBM_SPEC, SEM_SPEC, SEM_SPEC, ANY),
        out_specs=HBM_SPEC, input_output_aliases={0: 0},
        compiler_params=pltpu.CompilerParams(has_side_effects=SPLIT_EFFECT),
    )(land_thru, send_sems, recv_sems, after)


def _peer(k, x, y, c):
    return x ^ ((k >> 2) & 1), y ^ ((k >> 1) & 1), c ^ (k & 1)


def _peers_start(src, after, name):
    r, w = src.shape
    x, y, c = _coords()
    land = lax.dynamic_update_slice(jnp.zeros((N_DEV, r, w), src.dtype), src[None], (4 * x + 2 * y + c, 0, 0))

    def body(src_ref, land_ref, after_ref, send_sems, recv_sems, src_thru, land_thru, token):
        x, y, c = _coords()
        for k in range(1, N_DEV):
            pltpu.make_async_remote_copy(
                src_ref=src_ref, dst_ref=land_ref.at[4 * x + 2 * y + c],
                send_sem=send_sems.at[k - 1], recv_sem=recv_sems.at[k - 1],
                device_id=_peer(k, x, y, c), device_id_type=MESH_ID).start()
        token[...] = jnp.zeros_like(token)

    return pl.pallas_call(
        body, name=name,
        out_shape=(pltpu.SemaphoreType.DMA((N_DEV - 1,)), pltpu.SemaphoreType.DMA((N_DEV - 1,)),
                   pltpu.HBM(src.shape, src.dtype), pltpu.HBM(land.shape, src.dtype),
                   jax.ShapeDtypeStruct((8, 128), F32)),
        in_specs=(HBM_SPEC, HBM_SPEC, ANY),
        out_specs=(SEM_SPEC, SEM_SPEC, HBM_SPEC, HBM_SPEC, pl.BlockSpec(memory_space=pltpu.VMEM)),
        input_output_aliases={0: 2, 1: 3},
        compiler_params=pltpu.CompilerParams(has_side_effects=SPLIT_EFFECT),
    )(pltpu.with_memory_space_constraint(src, pltpu.HBM), pltpu.with_memory_space_constraint(land, pltpu.HBM), after)


def _peers_wait(started, after, name):
    send_sems, recv_sems, src_thru, land_thru, _ = started

    def body(src_ref, land_ref, send_sems, recv_sems, after_ref, src_dead, got_ref):
        x, y, c = _coords()
        for k in range(1, N_DEV):
            px, py, pc = _peer(k, x, y, c)
            copy = pltpu.make_async_remote_copy(
                src_ref=src_ref, dst_ref=land_ref.at[4 * px + 2 * py + pc],
                send_sem=send_sems.at[k - 1], recv_sem=recv_sems.at[k - 1],
                device_id=(px, py, pc), device_id_type=MESH_ID)
            copy.wait_send()
            copy.wait_recv()

    return pl.pallas_call(
        body, name=name,
        out_shape=(pltpu.HBM(src_thru.shape, src_thru.dtype), pltpu.HBM(land_thru.shape, land_thru.dtype)),
        in_specs=(HBM_SPEC, HBM_SPEC, SEM_SPEC, SEM_SPEC, ANY),
        out_specs=(HBM_SPEC, HBM_SPEC), input_output_aliases={0: 0, 1: 1},
        compiler_params=pltpu.CompilerParams(has_side_effects=SPLIT_EFFECT),
    )(src_thru, land_thru, send_sems, recv_sems, after)


def _share_refs(ref, k, x, y, c, sender_c):
    px, py = ([(x, y)] + _other_chips(x, y))[k]
    return ref.at[:, 4 * px + 2 * py + sender_c]


def _share_start(gathered, after, name):
    def body(g_ref, after_ref, send_sems, recv_sems, g_thru, token):
        x, y, c = _coords()
        for k in range(4):
            slot = _share_refs(g_ref, k, x, y, c, c)
            pltpu.make_async_remote_copy(
                src_ref=slot, dst_ref=slot, send_sem=send_sems.at[k], recv_sem=recv_sems.at[k],
                device_id=(x, y, 1 - c), device_id_type=MESH_ID).start()
        token[...] = jnp.zeros_like(token)

    return pl.pallas_call(
        body, name=name,
        out_shape=(pltpu.SemaphoreType.DMA((4,)), pltpu.SemaphoreType.DMA((4,)),
                   pltpu.HBM(gathered.shape, gathered.dtype), jax.ShapeDtypeStruct((8, 128), F32)),
        in_specs=(HBM_SPEC, ANY),
        out_specs=(SEM_SPEC, SEM_SPEC, HBM_SPEC, pl.BlockSpec(memory_space=pltpu.VMEM)),
        input_output_aliases={0: 2},
        compiler_params=pltpu.CompilerParams(has_side_effects=SPLIT_EFFECT),
    )(pltpu.with_memory_space_constraint(gathered, pltpu.HBM), after)


def _share_wait(started, after, name):
    send_sems, recv_sems, g_thru, _ = started

    def body(g_ref, send_sems, recv_sems, after_ref, got_ref):
        x, y, c = _coords()
        for k in range(4):
            copy = pltpu.make_async_remote_copy(
                src_ref=_share_refs(g_ref, k, x, y, c, c), dst_ref=_share_refs(g_ref, k, x, y, c, 1 - c),
                send_sem=send_sems.at[k], recv_sem=recv_sems.at[k],
                device_id=(x, y, 1 - c), device_id_type=MESH_ID)
            copy.wait_send()
            copy.wait_recv()

    return pl.pallas_call(
        body, name=name,
        out_shape=pltpu.HBM(g_thru.shape, g_thru.dtype),
        in_specs=(HBM_SPEC, SEM_SPEC, SEM_SPEC, ANY),
        out_specs=HBM_SPEC, input_output_aliases={0: 0},
        compiler_params=pltpu.CompilerParams(has_side_effects=SPLIT_EFFECT),
    )(g_thru, send_sems, recv_sems, after)


def _core_share(own, gathered, name):
    def body(own_ref, gin_ref, out_ref, stage, send_sems, recv_sems, local_sem):
        x, y, c = _coords()
        sibling = (x, y, 1 - c)
        chips = [(x, y)] + _other_chips(x, y)
        stage_in = pltpu.make_async_copy(own_ref, stage, local_sem)
        stage_in.start()
        sent, arriving = [], []
        for k, (px, py) in enumerate(chips):
            slot = out_ref.at[:, 4 * px + 2 * py + c]
            sent.append(pltpu.make_async_remote_copy(
                src_ref=own_ref if k == 0 else slot, dst_ref=slot,
                send_sem=send_sems.at[k], recv_sem=recv_sems.at[k], device_id=sibling, device_id_type=MESH_ID))
            arriving.append(pltpu.make_async_remote_copy(
                src_ref=own_ref, dst_ref=out_ref.at[:, 4 * px + 2 * py + (1 - c)],
                send_sem=send_sems.at[k], recv_sem=recv_sems.at[k], device_id=sibling, device_id_type=MESH_ID))
        for cp in sent:
            cp.start()
        stage_in.wait()
        stage_out = pltpu.make_async_copy(stage, out_ref.at[:, 4 * x + 2 * y + c], local_sem)
        stage_out.start()
        for cp in arriving:
            cp.wait_recv()
        for cp in sent:
            cp.wait_send()
        stage_out.wait()

    return pl.pallas_call(
        body, name=name,
        out_shape=jax.ShapeDtypeStruct(gathered.shape, own.dtype),
        in_specs=[ANY, ANY], out_specs=ANY, input_output_aliases={1: 0},
        scratch_shapes=[pltpu.VMEM(own.shape, own.dtype), pltpu.SemaphoreType.DMA((4,)),
                        pltpu.SemaphoreType.DMA((4,)), pltpu.SemaphoreType.DMA],
    )(own, gathered)


def _adamw(w, g, m, v):
    m = ADAM_B1 * m + (1.0 - ADAM_B1) * g
    v = ADAM_B2 * v + (1.0 - ADAM_B2) * (g * g)
    m_hat = m / ADAM_C1
    v_hat = v / ADAM_C2
    delta = -ADAM_LR * (m_hat / (jnp.sqrt(v_hat) + ADAM_EPS) + ADAM_WD * w)
    return delta, m, v


def _adam_big(units, chip_idx, tag, token):
    n = len(units)
    r, wd = units[0][2].shape
    tr, tw = _row_tile(r, 1024), 256

    def body(s_ref, tok_ref, *refs):
        for u in range(n):
            p_ref, l_ref, w_ref, m_ref, v_ref = refs[5 * u:5 * u + 5]
            g_ref, d_ref, mo_ref, vo_ref = refs[5 * n + 4 * u:5 * n + 4 * u + 4]
            g = p_ref[0].astype(F32)
            for j in range(3):
                g = g + l_ref[j].astype(F32)
            delta, mn, vn = _adamw(w_ref[...], g, m_ref[...], v_ref[...])
            g_ref[...] = g
            d_ref[...] = delta
            mo_ref[...] = mn
            vo_ref[...] = vn

    row = pl.BlockSpec((tr, tw), lambda i, j, s: (i, j))
    unit_specs = [pl.BlockSpec((1, tr, tw), lambda i, j, s: (s[0], i, j)),
                  pl.BlockSpec((3, tr, tw), lambda i, j, s: (0, i, j)), row, row, row]
    outs = pl.pallas_call(
        body, name="adam_big_" + tag,
        grid_spec=pltpu.PrefetchScalarGridSpec(
            num_scalar_prefetch=1, grid=(r // tr, wd // tw),
            in_specs=[pl.BlockSpec((8, 128), lambda i, j, s: (0, 0))] + unit_specs * n,
            out_specs=[row] * (4 * n)),
        out_shape=[jax.ShapeDtypeStruct((r, wd), F32)] * (4 * n),
        compiler_params=_cparams(),
    )(chip_idx, token, *[a for unit in units for a in unit])
    return [outs[4 * u:4 * u + 4] for u in range(n)]


def _adam_small(parts, w, m, v):
    _, r, wd = parts.shape

    def body(p_ref, w_ref, m_ref, v_ref, g_ref, d_ref, mo_ref, vo_ref):
        g = p_ref[0]
        for k in range(1, N_DEV):
            g = g + p_ref[k]
        delta, mn, vn = _adamw(w_ref[...], g, m_ref[...], v_ref[...])
        g_ref[...] = g
        d_ref[...] = delta
        mo_ref[...] = mn
        vo_ref[...] = vn

    return pl.pallas_call(
        body, name="adam_small",
        out_shape=[jax.ShapeDtypeStruct((r, wd), F32)] * 4,
        compiler_params=_cparams(),
    )(parts, w, m, v)


def _pad_rows(a, rows):
    return jnp.pad(a, ((0, rows - a.shape[0]), (0, 0)))


def _pad_w_in(w):
    cut = Q_RANK + KV_RANK + ROPE
    return jnp.concatenate([w[:, :cut], jnp.zeros((w.shape[0], 64), w.dtype), w[:, cut:]], axis=1)


def _unpad_w_in(w):
    cut = Q_RANK + KV_RANK + ROPE
    return jnp.concatenate([w[:, :cut], w[:, cut + 64:]], axis=1)


def _pack_mid(p):
    parts = [_pad_w_in(p["w_in"][0]), p["w_out"][0], p["w_mq"][0], p["w_mo"][0],
             p["w_mkv"][0].reshape(256, D_MODEL),
             _pad_rows(p["w_q_up"][0].T.reshape(24, D_MODEL), 32),
             p["w_kv_up"][0].reshape(16, D_MODEL)]
    return jnp.concatenate(parts, axis=0)


def _pack_ffn(w_gate, w_up, w_down, name):
    d, rows = w_gate.shape[1:]

    def body(g_ref, u_ref, d_ref, o_ref):
        eye = (lax.broadcasted_iota(jnp.int32, (d, d), 0) == lax.broadcasted_iota(jnp.int32, (d, d), 1)).astype(BF16)
        o_ref[0] = _dot_tn(g_ref[0].astype(BF16), eye).astype(BF16)
        o_ref[1] = _dot_tn(u_ref[0].astype(BF16), eye).astype(BF16)
        o_ref[2] = d_ref[0].astype(BF16)

    return pl.pallas_call(
        body, name=name, out_shape=jax.ShapeDtypeStruct((3, rows, d), BF16), compiler_params=_cparams(),
    )(w_gate, w_up, w_down)


def _pack_segments(p, group):
    if group == "mid":
        return _pack_mid(p)[None].astype(BF16)
    return _pack_ffn(p[group + "_w_gate"], p[group + "_w_up"], p[group + "_w_down"], "pack_" + group)


UNIT_WEIGHT = {"ffn1_g": ("ffn1_w_gate", True), "ffn1_u": ("ffn1_w_up", True), "ffn1_d": ("ffn1_w_down", False),
               "ffn2_g": ("ffn2_w_gate", True), "ffn2_u": ("ffn2_w_up", True), "ffn2_d": ("ffn2_w_down", False)}


def _pack_unit(p, unit):
    if unit == "mid":
        return _pack_mid(p)
    name, transposed = UNIT_WEIGHT[unit]
    return p[name][0].T if transposed else p[name][0]


def _unpack_unit(a, unit):
    if unit != "mid":
        name, transposed = UNIT_WEIGHT[unit]
        return {name: (a.T if transposed else a)[None]}
    seg = lambda n: a[SEG_OFF[n][0]:SEG_OFF[n][0] + SEG_OFF[n][1]]
    return {"w_in": _unpad_w_in(seg("w_in"))[None], "w_out": seg("w_out")[None], "w_mq": seg("w_mq")[None],
            "w_mo": seg("w_mo")[None], "w_mkv": seg("w_mkv").reshape(D_MODEL, 256)[None],
            "w_q_up": seg("w_q")[:24].reshape(96, Q_RANK).T[None],
            "w_kv_up": seg("w_kv").reshape(KV_RANK, 128)[None]}


def _unpack_gathered(full, group):
    if group != "mid":
        return {group: full.reshape(len(GROUP_SEGS[group]), -1, D_MODEL)}
    full = full[0]
    seg = lambda n: full[:, SEG_OFF[n][0]:SEG_OFF[n][0] + SEG_OFF[n][1]]
    rows = lambda n: seg(n).reshape(-1, D_MODEL)
    wq_t = seg("w_q")[:, :24].reshape(MLA_HEADS, NOPE + ROPE, Q_RANK)
    wq_t = jnp.pad(wq_t, ((0, 0), (0, HEAD_PAD - NOPE - ROPE), (0, 0))).reshape(MLA_HEADS * HEAD_PAD, Q_RANK)
    wkv = seg("w_kv").reshape(N_DEV, KV_RANK, 128).transpose(1, 0, 2).reshape(KV_RANK, N_DEV * 128)
    return {"w_in": rows("w_in"), "w_out": rows("w_out"), "w_mq": rows("w_mq"), "w_mo": rows("w_mo"),
            "w_mkv": seg("w_mkv").reshape(N_DEV, D_MODEL, 256), "w_q": wq_t, "w_kv": wkv}


def _pack_grads(gr):
    blk = lambda a: a.reshape(N_DEV, -1, D_MODEL)
    dwq = gr["w_q"].reshape(MLA_HEADS, HEAD_PAD, Q_RANK)[:, :NOPE + ROPE].reshape(N_DEV, 24, D_MODEL)
    dwq = jnp.pad(dwq, ((0, 0), (0, 8), (0, 0)))
    dwkv = gr["w_kv"].reshape(KV_RANK, N_DEV, 128).transpose(1, 0, 2).reshape(N_DEV, 16, D_MODEL)
    parts = [blk(gr["w_in"]), blk(gr["w_out"]), blk(gr["w_mq"]), blk(gr["w_mo"]),
             gr["w_mkv"].reshape(N_DEV, 256, D_MODEL), dwq, dwkv]
    return jnp.concatenate([a.astype(BF16) for a in parts], axis=1)


def _pack_small(vals):
    parts = []
    for n, r in SMALL_ROWS:
        parts.append(_pad_rows(vals[n].reshape(-1, 128), r) if n in vals else jnp.zeros((r, 128), F32))
    return jnp.concatenate(parts, axis=0)


def _unpack_small(a, shapes):
    out = {}
    for n, shape in shapes.items():
        o = SMALL_OFF[n][0]
        out[n] = a[o:o + int(np.prod(shape)) // 128].reshape(shape)
    return out


BIG_NAMES = ("ffn1_w_gate", "ffn1_w_up", "ffn1_w_down", "w_in", "w_q_up", "w_kv_up", "w_out", "w_mq", "w_mkv",
             "w_mo", "ffn2_w_gate", "ffn2_w_up", "ffn2_w_down")
SMALL_NAMES = ("ffn1_norm", "mix_norm", "q_norm", "kv_norm", "pool_w", "pool_scale", "xattn_norm", "mem_norm",
               "ffn2_norm", "final_norm")
WEIGHT_ORDER = ("ffn1_norm", "ffn1_w_gate", "ffn1_w_up", "ffn1_w_down", "mix_norm", "w_in", "q_norm", "w_q_up",
                "kv_norm", "w_kv_up", "pool_w", "pool_scale", "w_out", "xattn_norm", "mem_norm", "w_mq", "w_mkv",
                "w_mo", "ffn2_norm", "ffn2_w_gate", "ffn2_w_up", "ffn2_w_down", "final_norm")


def _rope_table():
    lane = np.arange(128)
    freqs = (1.0 / (ROPE_BASE ** (np.arange(0, ROPE, 2, dtype=np.float32) / ROPE))).astype(np.float32)
    tab = np.zeros((8, 128), np.float32)
    tab[0] = np.where(lane < ROPE, freqs[lane % (ROPE // 2)], 0.0)
    tab[1] = np.where(lane < ROPE // 2, -1.0, np.where(lane < ROPE, 1.0, 0.0))
    return jnp.asarray(tab)


def kernel(x, mem, positions, ffn1_norm, ffn1_w_gate, ffn1_w_up, ffn1_w_down, mix_norm, w_in, q_norm, w_q_up, kv_norm, w_kv_up, pool_w, pool_scale, w_out, xattn_norm, mem_norm, w_mq, w_mkv, w_mo, ffn2_norm, ffn2_w_gate, ffn2_w_up, ffn2_w_down, final_norm, loss_target, m_ffn1_norm, m_ffn1_w_gate, m_ffn1_w_up, m_ffn1_w_down, m_mix_norm, m_w_in, m_q_norm, m_w_q_up, m_kv_norm, m_w_kv_up, m_pool_w, m_pool_scale, m_w_out, m_xattn_norm, m_mem_norm, m_w_mq, m_w_mkv, m_w_mo, m_ffn2_norm, m_ffn2_w_gate, m_ffn2_w_up, m_ffn2_w_down, m_final_norm, v_ffn1_norm, v_ffn1_w_gate, v_ffn1_w_up, v_ffn1_w_down, v_mix_norm, v_w_in, v_q_norm, v_w_q_up, v_kv_norm, v_w_kv_up, v_pool_w, v_pool_scale, v_w_out, v_xattn_norm, v_mem_norm, v_w_mq, v_w_mkv, v_w_mo, v_ffn2_norm, v_ffn2_w_gate, v_ffn2_w_up, v_ffn2_w_down, v_final_norm):
    wts = dict(ffn1_norm=ffn1_norm, ffn1_w_gate=ffn1_w_gate, ffn1_w_up=ffn1_w_up, ffn1_w_down=ffn1_w_down,
               mix_norm=mix_norm, w_in=w_in, q_norm=q_norm, w_q_up=w_q_up, kv_norm=kv_norm, w_kv_up=w_kv_up,
               pool_w=pool_w, pool_scale=pool_scale, w_out=w_out, xattn_norm=xattn_norm, mem_norm=mem_norm,
               w_mq=w_mq, w_mkv=w_mkv, w_mo=w_mo, ffn2_norm=ffn2_norm, ffn2_w_gate=ffn2_w_gate,
               ffn2_w_up=ffn2_w_up, ffn2_w_down=ffn2_w_down, final_norm=final_norm)
    mom = dict(ffn1_norm=m_ffn1_norm, ffn1_w_gate=m_ffn1_w_gate, ffn1_w_up=m_ffn1_w_up, ffn1_w_down=m_ffn1_w_down,
               mix_norm=m_mix_norm, w_in=m_w_in, q_norm=m_q_norm, w_q_up=m_w_q_up, kv_norm=m_kv_norm,
               w_kv_up=m_w_kv_up, pool_w=m_pool_w, pool_scale=m_pool_scale, w_out=m_w_out, xattn_norm=m_xattn_norm,
               mem_norm=m_mem_norm, w_mq=m_w_mq, w_mkv=m_w_mkv, w_mo=m_w_mo, ffn2_norm=m_ffn2_norm,
               ffn2_w_gate=m_ffn2_w_gate, ffn2_w_up=m_ffn2_w_up, ffn2_w_down=m_ffn2_w_down, final_norm=m_final_norm)
    var = dict(ffn1_norm=v_ffn1_norm, ffn1_w_gate=v_ffn1_w_gate, ffn1_w_up=v_ffn1_w_up, ffn1_w_down=v_ffn1_w_down,
               mix_norm=v_mix_norm, w_in=v_w_in, q_norm=v_q_norm, w_q_up=v_w_q_up, kv_norm=v_kv_norm,
               w_kv_up=v_w_kv_up, pool_w=v_pool_w, pool_scale=v_pool_scale, w_out=v_w_out, xattn_norm=v_xattn_norm,
               mem_norm=v_mem_norm, w_mq=v_w_mq, w_mkv=v_w_mkv, w_mo=v_w_mo, ffn2_norm=v_ffn2_norm,
               ffn2_w_gate=v_ffn2_w_gate, ffn2_w_up=v_ffn2_w_up, ffn2_w_down=v_ffn2_w_down, final_norm=v_final_norm)

    t = x.shape[1]
    xs = x[0]
    mems = mem[0]
    target = loss_target[0]
    pos = positions.reshape(t, 1)
    row = lambda a: a.reshape(1, -1)
    rope_tab = _rope_table()

    cx, cy, cc = _coords()
    chip_idx = (2 * cx + cy).astype(jnp.int32).reshape(1)

    wb = {}
    for grp in ("ffn1", "mid", "ffn2"):
        wb[grp] = _pack_segments(wts, grp)
        if grp == "ffn1":
            near_ffn1 = _near_start(wb["ffn1"], pos, "ag_ffn1_near_start")
    mid_names = ("w_in", "w_out", "w_mq", "w_mo", "w_mkv", "w_q_up", "w_kv_up")

    states = (wts, mom, var)

    def packed_during(token, units, after, states):
        one = 1.0 + token[0, 0]
        packs = {}
        for u in units:
            names = SMALL_NAMES if u == "small" else mid_names if u == "mid" else UNIT_WEIGHT[u][:1]
            held = [{n: p[n] * one for n in names} for p in states]
            packs[u] = tuple(_pack_small(h) if u == "small" else _pack_unit(h, u) for h in held)
        return lax.optimization_barrier((after, packs))

    after, adam_in = packed_during(near_ffn1[4], ("ffn1_g", "ffn1_u", "ffn1_d", "ffn2_g", "ffn2_u", "ffn2_d"),
                                   wb["ffn2"], states)
    after, mid_w = packed_during(near_ffn1[4], ("mid",), after, states[:1])
    own_ffn1, land_ffn1 = _near_wait(near_ffn1, after, "ag_ffn1_near_wait")
    far_ffn1 = _far_start(land_ffn1, own_ffn1, "ag_ffn1_far_start")
    after, more = packed_during(far_ffn1[3], ("small",), wb["mid"], states)
    after, mid_mv = packed_during(far_ffn1[3], ("mid",), after, states[1:])
    adam_in.update(more)
    adam_in["mid"] = mid_w["mid"] + mid_mv["mid"]
    land_ffn1 = _far_wait(far_ffn1, after, "ag_ffn1_far_wait")
    full_ffn1 = _core_share(own_ffn1, land_ffn1, "ag_ffn1_share")
    fw = _unpack_gathered(full_ffn1, "ffn1")
    ag_mid = _ici_start(wb["mid"], full_ffn1, "ag_mid_start", True)
    g_ffn1, g_mix, g_q, g_kv = row(ffn1_norm), row(mix_norm), row(q_norm), row(kv_norm)
    g_x, g_mem, g_ffn2, g_fin = row(xattn_norm), row(mem_norm), row(ffn2_norm), row(final_norm)
    pool_wb = pool_w[0].astype(BF16)
    pool_sc = row(pool_scale)

    h1, n1, gate1, up1 = _ffn_fwd(xs, g_ffn1, fw["ffn1"], "ffn1_fwd", token=ag_mid[4])
    own_mid, land_mid = _ici_wait(ag_mid, h1, "ag_mid_wait", True)
    full_mid = _core_share(own_mid, land_mid, "ag_mid_share")
    fw.update(_unpack_gathered(full_mid, "mid"))
    ag_ffn2 = _ici_start(wb["ffn2"], full_mid, "ag_ffn2_start", True)
    u, z, qn, kvn, qh, kh, vh = _mix_prep(h1, g_mix, fw["w_in"], g_q, fw["w_q"], g_kv, fw["w_kv"], pos, rope_tab,
                                          token=ag_ffn2[4])
    a, lse = _attn_fwd(qh, kh, vh)
    p = _pool_fwd(z, pool_wb, pool_sc)
    memn, km, vm = _mem_kv(mems, g_mem, fw["w_mkv"])
    own_ffn2, land_ffn2 = _ici_wait(ag_ffn2, a, "ag_ffn2_wait", True)
    land_ffn2 = lax.dynamic_update_slice(land_ffn2, own_ffn2[:, None], (0, 4 * cx + 2 * cy + cc, 0, 0))
    share_ffn2 = _share_start(land_ffn2, a, "ag_ffn2_share_start")
    h2, h3, hn, qm, om = _xattn_fwd(h1, a, p, fw["w_out"], g_x, fw["w_mq"], km, vm, fw["w_mo"], token=share_ffn2[3])
    fw.update(_unpack_gathered(_share_wait(share_ffn2, h3, "ag_ffn2_share_wait"), "ffn2"))
    dh4, n2, gate2, up2, loss_part, dg_fin = _ffn_fwd(h3, g_ffn2, fw["ffn2"],
                                                      "ffn2_fwd", head=(target, g_fin))

    def reduce_start(g8, unit):
        part = _core_reduce(g8, unit)
        return _ici_start(part, g8, "rs_" + unit + "_start", False)

    def by_device(g):
        return g.reshape(N_DEV, -1, D_MODEL)

    rs = {}
    dh3, dgate2, dup2, act2, dg_ffn2 = _ffn_bwd_data(dh4, h3, g_ffn2, gate2, up2, fw["ffn2"], "ffn2_bwd")
    rs["ffn2_g"] = reduce_start(by_device(_tn_matmul(dgate2, n2, "ffn2_dwg", tmm=1408, m=D_FF, out_dtype=BF16)), "ffn2_g")
    rs["ffn2_u"] = reduce_start(by_device(_tn_matmul(dup2, n2, "ffn2_dwu", tmm=1408, m=D_FF, out_dtype=BF16,
                                                     token=rs["ffn2_g"][4])), "ffn2_u")
    rs["ffn2_d"] = reduce_start(by_device(_tn_matmul(act2, dh4, "ffn2_dwd", scale=0.5, tmm=1408, m=D_FF, out_dtype=BF16,
                                                     token=rs["ffn2_u"][4])), "ffn2_d")
    dh2, dqm, da, dp, dkm, dvm, dg_x = _xattn_bwd(dh3, h2, qm, g_x, fw["w_mq"], km, vm, fw["w_mo"], fw["w_out"],
                                                  token=rs["ffn2_d"][4])
    gr = {}
    gr["w_mo"] = _tn_matmul(om, dh3, "dw_mo", out_dtype=BF16)
    gr["w_mq"] = _tn_matmul(hn, dqm, "dw_mq", out_dtype=BF16)
    gr["w_out"] = jnp.concatenate([_tn_matmul(a, dh2, "dw_out_a", out_dtype=BF16),
                                   _tn_matmul(p, dh2, "dw_out_p", out_dtype=BF16)], axis=0)
    gr["w_mkv"], dg_mem = _mem_kv_bwd(dkm, dvm, memn, mems, g_mem, fw["w_mkv"])
    dz_pool, d_pool_w, d_pool_sc = _pool_bwd(dp, z, pool_wb, pool_sc)
    dqh, dkh, dvh = _attn_bwd(qh, kh, vh, da, lse, _attn_delta(a, da))
    dh1, dq, dkv, dz, dg_q, dg_kv, dg_mix = _mla_bwd(dqh, dkh, dvh, z, dz_pool, h1, dh2, g_mix, fw["w_in"], g_q,
                                                     fw["w_q"], g_kv, fw["w_kv"], pos, rope_tab)
    gr["w_q"] = _tn_matmul(dq, qn, "dw_q", out_dtype=BF16)
    gr["w_kv"] = _tn_matmul(kvn, dkv, "dw_kv", out_dtype=BF16)
    gr["w_in"] = _tn_matmul(u, dz, "dw_in", out_dtype=BF16)
    g_mid = _pack_grads(gr)
    part_mid = _core_reduce(g_mid, "mid")
    got = {}
    after = part_mid
    for unit in ("ffn2_g", "ffn2_u", "ffn2_d"):
        got[unit] = _ici_wait(rs[unit], after, "rs_" + unit + "_wait", False)
        after = got[unit][1]
    rs["mid"] = _ici_start(part_mid, after, "rs_mid_start", False)
    dx, dgate1, dup1, act1, dg_ffn1 = _ffn_bwd_data(dh1, xs, g_ffn1, gate1, up1, fw["ffn1"], "ffn1_bwd", token=rs["mid"][4])
    got["mid"] = _ici_wait(rs["mid"], dx, "rs_mid_wait", False)

    small_g = dict(ffn1_norm=dg_ffn1, mix_norm=dg_mix, q_norm=dg_q, kv_norm=dg_kv, pool_w=d_pool_w,
                   pool_scale=d_pool_sc, xattn_norm=dg_x, mem_norm=dg_mem, ffn2_norm=dg_ffn2, final_norm=dg_fin,
                   loss=loss_part)
    small_ag = _peers_start(_pack_small(small_g), got["mid"][1], "small_ag_start")
    rs["ffn1_g"] = reduce_start(by_device(_tn_matmul(dgate1, n1, "ffn1_dwg", tmm=1408, m=D_FF, out_dtype=BF16,
                                                     token=small_ag[4])), "ffn1_g")
    _, parts = _peers_wait(small_ag, rs["ffn1_g"][4], "small_ag_wait")
    small = _adam_small(parts, *adam_in["small"])
    small_sum = small[0]
    loss = small_sum[SMALL_OFF["loss"][0], 0]
    shapes = {n: wts[n].shape for n in SMALL_NAMES}
    small = [_unpack_small(s, shapes) for s in small]

    rs["ffn1_u"] = reduce_start(by_device(_tn_matmul(dup1, n1, "ffn1_dwu", tmm=1408, m=D_FF, out_dtype=BF16,
                                                     token=small_sum)), "ffn1_u")
    rs["ffn1_d"] = reduce_start(by_device(_tn_matmul(act1, dh1, "ffn1_dwd", scale=0.5, tmm=1408, m=D_FF, out_dtype=BF16,
                                                     token=rs["ffn1_u"][4])), "ffn1_d")

    big = {}

    def adam_units(names, token):
        units = [got[u] + adam_in[u] for u in names]
        res = _adam_big(units, chip_idx, "_".join(names), token)
        for u, four in zip(names, res):
            for k, packed in enumerate(four):
                big.setdefault(k, {}).update(_unpack_unit(packed, u))
        return res[-1][0]

    done = adam_units(["mid"], rs["ffn1_d"][4])
    done = adam_units(["ffn2_g", "ffn2_u", "ffn2_d"], done)
    got["ffn1_g"] = _ici_wait(rs["ffn1_g"], done, "rs_ffn1_g_wait", False)
    got["ffn1_u"] = _ici_wait(rs["ffn1_u"], got["ffn1_g"][1], "rs_ffn1_u_wait", False)
    done = adam_units(["ffn1_g", "ffn1_u"], done)
    got["ffn1_d"] = _ici_wait(rs["ffn1_d"], done, "rs_ffn1_d_wait", False)
    adam_units(["ffn1_d"], done)

    outs = [loss, dx[None]]
    for k in range(4):
        for n in WEIGHT_ORDER:
            outs.append(big[k][n] if n in BIG_NAMES else small[k][n])
    return tuple(outs)
```

```python
import numpy as np

import jax
import jax.numpy as jnp
from jax import lax
from jax.experimental import pallas as pl
from jax.experimental.pallas import tpu as pltpu

F32 = jnp.float32
BF16 = jnp.bfloat16

N_DEV = 8
D_MODEL = 1024
D_FF = 2816
MLA_HEADS = 4
NOPE = 128
ROPE = 64
HEAD_PAD = 256
V_DIM = 128
Q_RANK = 256
KV_RANK = 128
POOL_WINDOWS = (2, 4, 8, 16)
POOL_CH = 128
POOL_HALO = 16
N_MEM = 256
MEM_HEADS = 4
MEM_HD = 256
ROPE_BASE = 10000.0
RMS_EPS = 1e-6
ATTN_SCALE = (NOPE + ROPE) ** -0.5
MEM_SCALE = MEM_HD ** -0.5
NEG_BIG = -1e30

ADAM_LR = 0.001
ADAM_B1 = 0.9
ADAM_B2 = 0.999
ADAM_EPS = 1e-08
ADAM_WD = 0.01
ADAM_STEP = 10
ADAM_C1 = 1.0 - ADAM_B1 ** ADAM_STEP
ADAM_C2 = 1.0 - ADAM_B2 ** ADAM_STEP

VMEM_LIMIT_BYTES = 56 * 1024 * 1024
BF16_ROWS = 16

GROUP_SEGS = {
    "ffn1": (("ffn1_g", 352), ("ffn1_u", 352), ("ffn1_d", 352)),
    "mid": (("w_in", 128), ("w_out", 128), ("w_mq", 128), ("w_mo", 128), ("w_mkv", 256), ("w_q", 32), ("w_kv", 16)),
    "ffn2": (("ffn2_g", 352), ("ffn2_u", 352), ("ffn2_d", 352)),
}
SEG_OFF = {}
GROUP_ROWS = {}
for _g, _segs in GROUP_SEGS.items():
    _o = 0
    for _n, _r in _segs:
        SEG_OFF[_n] = (_o, _r)
        _o += _r
    GROUP_ROWS[_g] = _o

SMALL_ROWS = (("ffn1_norm", 8), ("mix_norm", 8), ("q_norm", 8), ("kv_norm", 8), ("pool_w", 512), ("pool_scale", 8),
              ("xattn_norm", 8), ("mem_norm", 8), ("ffn2_norm", 8), ("final_norm", 8), ("loss", 8))
SMALL_OFF = {}
_o = 0
for _n, _r in SMALL_ROWS:
    SMALL_OFF[_n] = (_o, _r)
    _o += _r


def _cparams(**kw):
    return pltpu.CompilerParams(vmem_limit_bytes=VMEM_LIMIT_BYTES, **kw)


def _row_tile(rows, limit):
    best = None
    for cand in range(BF16_ROWS, min(rows, limit) + 1, BF16_ROWS):
        if rows % cand == 0:
            best = cand
    assert best is not None, rows
    return best


def _dot_nn(a, b):
    return lax.dot_general(a, b, (((1,), (0,)), ((), ())), preferred_element_type=F32)


def _dot_nt(a, b):
    return lax.dot_general(a, b, (((1,), (1,)), ((), ())), preferred_element_type=F32)


def _dot_tn(a, b):
    return lax.dot_general(a, b, (((0,), (0,)), ((), ())), preferred_element_type=F32)


def _rms_fwd(x, g):
    r = lax.rsqrt(jnp.mean(x * x, axis=-1, keepdims=True) + RMS_EPS)
    return x * r * g, r


def _rms_bwd(dy, x, g, r):
    xhat = x * r
    dyg = dy * g
    dx = r * (dyg - xhat * jnp.mean(dyg * xhat, axis=-1, keepdims=True))
    dg = jnp.sum(dy * xhat, axis=0, keepdims=True)
    return dx, dg


def _accumulate(ref, val, first):
    if isinstance(first, bool):
        if first:
            ref[...] = val
        else:
            ref[...] += val
        return

    @pl.when(first)
    def _():
        ref[...] = val

    @pl.when(jnp.logical_not(first))
    def _():
        ref[...] += val


def _call_after(token, body, in_specs, args, **kw):
    if token is not None:
        inner = body
        body = lambda tok_ref, *refs: inner(*refs)
        in_specs = [pl.BlockSpec((8, 128), lambda *_: (0, 0))] + list(in_specs)
        args = (token,) + tuple(args)
    return pl.pallas_call(body, in_specs=in_specs, **kw)(*args)


def _resident(shape):
    return pl.BlockSpec(shape, lambda *_: (0,) * len(shape), pipeline_mode=pl.Buffered(1))


def _rope_tables(pos_col, tab):
    ang = pos_col.astype(F32) * tab[0:1, :]
    return jnp.cos(ang), jnp.sin(ang) * tab[1:2, :]


def _swap_halves(x):
    lane = lax.broadcasted_iota(jnp.int32, x.shape, 1)
    return jnp.where((lane % 64) < 32, pltpu.roll(x, 96, 1), pltpu.roll(x, 32, 1))


def _rope_apply(x, cos_t, sin_t):
    return x * cos_t + _swap_halves(x) * sin_t


def _rope_apply_t(dy, cos_t, sin_t):
    return dy * cos_t + _swap_halves(dy * sin_t)


def _ffn_fwd(h, g, w, name, token=None, head=None):
    t, d = h.shape
    f = w.shape[1]
    tm, tf = min(512, t), 256
    nf = f // tf
    n_in = 3 if head is None else 5

    def body(*refs):
        h_ref, g_ref, w_ref = refs[:3]
        ho_ref, n_ref, gate_ref, up_ref = refs[n_in:n_in + 4]
        nb_sc, acc_sc = refs[-2:]
        y, _ = _rms_fwd(h_ref[...], g_ref[...])
        nb = y.astype(BF16)
        nb_sc[...] = nb
        n_ref[...] = nb
        acc_sc[...] = jnp.zeros_like(acc_sc)

        def f_slab(j, ntile):
            rows = pl.ds(pl.multiple_of(j * tf, tf), ntile * tf)
            nb = nb_sc[...]
            gt = _dot_nt(nb, w_ref[0, rows, :])
            ut = _dot_nt(nb, w_ref[1, rows, :])
            for k in range(ntile):
                gate_ref[j + k] = gt[:, k * tf:(k + 1) * tf].astype(BF16)
                up_ref[j + k] = ut[:, k * tf:(k + 1) * tf].astype(BF16)
            act = (gt * jax.nn.sigmoid(gt)) * ut
            return _dot_nn(act.astype(BF16), w_ref[2, rows, :])

        def pair(p, carry):
            acc_sc[...] += f_slab(2 * p, 2)
            return carry

        lax.fori_loop(0, nf // 2, pair, 0, unroll=True)
        if nf % 2:
            acc_sc[...] += f_slab(nf - 1, 1)
        ho = h_ref[...] + 0.5 * acc_sc[...]
        if head is None:
            ho_ref[...] = ho
            return
        t_ref, gf_ref = refs[3:5]
        loss_ref, dgf_ref = refs[n_in + 4:n_in + 6]
        gg = gf_ref[...]
        y, r = _rms_fwd(ho, gg)
        err = y - t_ref[...]
        part = 0.5 * jnp.sum(jnp.mean(err * err, axis=-1, keepdims=True), axis=0, keepdims=True)
        dx, dg = _rms_bwd(err * (1.0 / d), ho, gg, r)
        ho_ref[...] = dx
        first = pl.program_id(0) == 0
        _accumulate(loss_ref, jnp.broadcast_to(part, loss_ref.shape), first)
        _accumulate(dgf_ref, dg, first)

    row = pl.BlockSpec((tm, d), lambda i: (i, 0))
    tiles = pl.BlockSpec((nf, tm, tf), lambda i: (0, i, 0))
    in_specs = [row, _resident((1, d)), _resident(w.shape)]
    args = (h, g, w)
    out_specs = [row, row, tiles, tiles]
    out_shape = [jax.ShapeDtypeStruct((t, d), F32), jax.ShapeDtypeStruct((t, d), BF16),
                 jax.ShapeDtypeStruct((nf, t, tf), BF16), jax.ShapeDtypeStruct((nf, t, tf), BF16)]
    if head is not None:
        in_specs += [row, _resident((1, d))]
        args += tuple(head)
        out_specs += [pl.BlockSpec((8, 128), lambda i: (0, 0)), pl.BlockSpec((1, d), lambda i: (0, 0))]
        out_shape += [jax.ShapeDtypeStruct((8, 128), F32), jax.ShapeDtypeStruct((1, d), F32)]
    return _call_after(
        token, body, in_specs, args, name=name, grid=(t // tm,), out_specs=out_specs, out_shape=out_shape,
        scratch_shapes=[pltpu.VMEM((tm, d), BF16), pltpu.VMEM((tm, d), F32)],
        compiler_params=_cparams(),
    )


def _ffn_bwd_data(dho, h, g, gate, up, w, name, token=None):
    t, d = h.shape
    f = w.shape[1]
    tm, tf = min(1024, t), 256
    parts = 2 if tm % 512 == 0 else 1
    tp = tm // parts
    nf = f // tf
    npair, odd = nf // 2, nf % 2
    nsteps = npair + odd

    def body(dho_ref, h_ref, g_ref, gate_ref, up_ref, wg_ref, wu_ref, wd_ref,
             dh_ref, dgate_ref, dup_ref, act_ref, dg_ref, dhb_sc, acc_sc):
        i, j = pl.program_id(0), pl.program_id(1)

        @pl.when(j == 0)
        def _():
            dhb_sc[...] = (0.5 * dho_ref[...]).astype(BF16)
            acc_sc[...] = jnp.zeros_like(acc_sc)

        def slab(ntile):
            cols = pl.ds(0, ntile * tf)
            for r in range(parts):
                rows = pl.ds(r * tp, tp)
                gt = jnp.concatenate([gate_ref[k, rows, :] for k in range(ntile)], axis=-1).astype(F32)
                ut = jnp.concatenate([up_ref[k, rows, :] for k in range(ntile)], axis=-1).astype(F32)
                dact = _dot_nt(dhb_sc[rows, :], wd_ref[cols, :])
                sg = jax.nn.sigmoid(gt)
                silu = gt * sg
                dgb = (dact * ut * (sg * (1.0 + gt * (1.0 - sg)))).astype(BF16)
                dub = (dact * silu).astype(BF16)
                act_ref[rows, cols] = (silu * ut).astype(BF16)
                dgate_ref[rows, cols] = dgb
                dup_ref[rows, cols] = dub
                acc_sc[rows, :] += _dot_nn(dgb, wg_ref[cols, :]) + _dot_nn(dub, wu_ref[cols, :])

        pl.when(j < npair)(lambda: slab(2))
        if odd:
            pl.when(j == npair)(lambda: slab(1))

        @pl.when(j == nsteps - 1)
        def _():
            x = h_ref[...]
            gg = g_ref[...]
            _, r = _rms_fwd(x, gg)
            dx, dg = _rms_bwd(acc_sc[...], x, gg, r)
            dh_ref[...] = dho_ref[...] + dx
            _accumulate(dg_ref, dg, i == 0)

    row = pl.BlockSpec((tm, d), lambda i, j: (i, 0))
    acts = pl.BlockSpec((2, tm, tf), lambda i, j: (j, i, 0))
    weights = lambda k: pl.BlockSpec((None, 2 * tf, d), lambda i, j: (k, j, 0))
    outs = pl.BlockSpec((tm, 2 * tf), lambda i, j: (i, j))
    padded = jax.ShapeDtypeStruct((t, 2 * tf * nsteps), BF16)
    return _call_after(
        token, body,
        [row, row, pl.BlockSpec((1, d), lambda i, j: (0, 0)), acts, acts, weights(0), weights(1), weights(2)],
        (dho, h, g, gate, up, w, w, w),
        name=name, grid=(t // tm, nsteps),
        out_specs=[row, outs, outs, outs, pl.BlockSpec((1, d), lambda i, j: (0, 0))],
        out_shape=[jax.ShapeDtypeStruct((t, d), F32), padded, padded, padded, jax.ShapeDtypeStruct((1, d), F32)],
        scratch_shapes=[pltpu.VMEM((tm, d), BF16), pltpu.VMEM((tm, d), F32)],
        compiler_params=_cparams(),
    )


def _tn_matmul(a, b, name, scale=1.0, tmm=None, out_dtype=F32, token=None, m=None):
    t = a.shape[0]
    m = a.shape[1] if m is None else m
    n = b.shape[1]
    tmm = m if tmm is None else tmm
    tk = min(1024, t)
    nk = t // tk

    def product(a_ref, b_ref):
        prod = _dot_tn(a_ref[...].astype(BF16), b_ref[...].astype(BF16))
        return prod * scale if scale != 1.0 else prod

    def body_f32(a_ref, b_ref, o_ref):
        _accumulate(o_ref, product(a_ref, b_ref), pl.program_id(1) == 0)

    def body_cast(a_ref, b_ref, o_ref, acc_sc):
        k = pl.program_id(1)
        _accumulate(acc_sc, product(a_ref, b_ref), k == 0)

        @pl.when(k == nk - 1)
        def _():
            o_ref[...] = acc_sc[...].astype(out_dtype)

    direct = out_dtype == F32
    return _call_after(
        token, body_f32 if direct else body_cast,
        [pl.BlockSpec((tk, tmm), lambda i, k: (k, i)),
         pl.BlockSpec((tk, n), lambda i, k: (k, 0))],
        (a, b),
        name=name, grid=(m // tmm, nk),
        out_specs=pl.BlockSpec((tmm, n), lambda i, k: (i, 0)),
        out_shape=jax.ShapeDtypeStruct((m, n), out_dtype),
        scratch_shapes=[] if direct else [pltpu.VMEM((tmm, n), F32)],
        compiler_params=_cparams(),
    )


def _mix_prep(h1, mix_norm, w_in, q_norm, wq_t, kv_norm, wkv, pos, rope_tab, token=None):
    t, d = h1.shape
    tm = min(512, t)

    def body(h_ref, gm_ref, win_ref, gq_ref, wq_ref, gkv_ref, wkv_ref, pos_ref, tab_ref,
             u_ref, z_ref, qn_ref, kvn_ref, q_ref, k_ref, v_ref):
        u, _ = _rms_fwd(h_ref[...], gm_ref[...])
        ub = u.astype(BF16)
        u_ref[...] = ub
        z = _dot_nn(ub, win_ref[...])
        z_ref[...] = z
        cos_t, sin_t = _rope_tables(pos_ref[...], tab_ref[...])
        qn, _ = _rms_fwd(z[:, 0:Q_RANK], gq_ref[...])
        qnb = qn.astype(BF16)
        qn_ref[...] = qnb
        q = _dot_nt(qnb, wq_ref[...])
        kvn, _ = _rms_fwd(z[:, Q_RANK:Q_RANK + KV_RANK], gkv_ref[...])
        kvnb = kvn.astype(BF16)
        kvn_ref[...] = kvnb
        kv = _dot_nn(kvnb, wkv_ref[...])
        k_pe = _rope_apply(z[:, Q_RANK + KV_RANK:Q_RANK + KV_RANK + 128], cos_t, sin_t)
        ones = jnp.ones((tm, V_DIM), F32)
        for hh in range(MLA_HEADS):
            b = hh * HEAD_PAD
            q_pe = _rope_apply(q[:, b + NOPE:b + HEAD_PAD], cos_t, sin_t)
            q_ref[hh] = jnp.concatenate([q[:, b:b + NOPE], q_pe], axis=-1).astype(BF16)
            k_ref[hh] = jnp.concatenate([kv[:, b:b + NOPE], k_pe], axis=-1).astype(BF16)
            v_ref[hh] = jnp.concatenate([kv[:, b + NOPE:b + HEAD_PAD], ones], axis=-1).astype(BF16)

    full = lambda shape: pl.BlockSpec(shape, lambda i: (0,) * len(shape))
    return _call_after(
        token, body,
        [pl.BlockSpec((tm, d), lambda i: (i, 0)), _resident((1, d)), _resident(w_in.shape), _resident((1, Q_RANK)),
         _resident(wq_t.shape), _resident((1, KV_RANK)), _resident(wkv.shape),
         pl.BlockSpec((tm, 1), lambda i: (i, 0)), _resident(rope_tab.shape)],
        (h1, mix_norm, w_in, q_norm, wq_t, kv_norm, wkv, pos, rope_tab),
        name="mix_prep", grid=(t // tm,),
        out_specs=[pl.BlockSpec((tm, d), lambda i: (i, 0)),
                   pl.BlockSpec((tm, d), lambda i: (i, 0)),
                   pl.BlockSpec((tm, Q_RANK), lambda i: (i, 0)),
                   pl.BlockSpec((tm, KV_RANK), lambda i: (i, 0)),
                   pl.BlockSpec((MLA_HEADS, tm, HEAD_PAD), lambda i: (0, i, 0)),
                   pl.BlockSpec((MLA_HEADS, tm, HEAD_PAD), lambda i: (0, i, 0)),
                   pl.BlockSpec((MLA_HEADS, tm, 2 * V_DIM), lambda i: (0, i, 0))],
        out_shape=[jax.ShapeDtypeStruct((t, d), BF16), jax.ShapeDtypeStruct((t, d), F32),
                   jax.ShapeDtypeStruct((t, Q_RANK), BF16), jax.ShapeDtypeStruct((t, KV_RANK), BF16),
                   jax.ShapeDtypeStruct((MLA_HEADS, t, HEAD_PAD), BF16),
                   jax.ShapeDtypeStruct((MLA_HEADS, t, HEAD_PAD), BF16),
                   jax.ShapeDtypeStruct((MLA_HEADS, t, 2 * V_DIM), BF16)],
        compiler_params=_cparams(),
    )


def _causal_mask(s):
    row = lax.broadcasted_iota(jnp.int32, s.shape, 0)
    col = lax.broadcasted_iota(jnp.int32, s.shape, 1)
    return jnp.where(col <= row, s, NEG_BIG)


def _attn_fwd(q, k, v):
    nh, t, _ = q.shape
    tq = tk = min(512, t)
    nq, nk = t // tq, t // tk

    pairs = [(i, j) for i in range(nq) for j in range(i + 1)]
    qi = jnp.asarray(np.array([i for i, _ in pairs], np.int32))
    kj = jnp.asarray(np.array([j for _, j in pairs], np.int32))

    def body(qi_ref, kj_ref, q_ref, k_ref, v_ref, o_ref, lse_ref, m_sc, acc_sc):
        n = pl.program_id(0)
        i, j = qi_ref[n], kj_ref[n]

        @pl.when(j == 0)
        def _():
            m_sc[...] = jnp.full_like(m_sc, NEG_BIG)
            acc_sc[...] = jnp.zeros_like(acc_sc)

        def step(diagonal):
            groups = 1 if diagonal else 2
            for hh in range(nh):
                for r in range(groups):
                    rows = pl.ds(r * (tq // groups), tq // groups)
                    s = _dot_nt(q_ref[hh, rows, :], k_ref[hh]) * ATTN_SCALE
                    if diagonal:
                        s = _causal_mask(s)
                    m_old = m_sc[hh, rows, :]
                    m_new = jnp.maximum(m_old, jnp.max(s, axis=-1, keepdims=True))
                    p = jnp.exp(s - m_new).astype(BF16)
                    acc_sc[hh, rows, :] = jnp.exp(m_old - m_new) * acc_sc[hh, rows, :] + _dot_nn(p, v_ref[hh])
                    m_sc[hh, rows, :] = m_new

        @pl.when(j < i)
        def _():
            step(False)

        @pl.when(j == i)
        def _():
            step(True)
            for hh in range(nh):
                acc = acc_sc[hh]
                l = acc[:, V_DIM:2 * V_DIM]
                o_ref[:, hh * V_DIM:(hh + 1) * V_DIM] = (acc[:, 0:V_DIM] / l).astype(BF16)
                lse_ref[hh] = m_sc[hh] + jnp.log(l[:, 0:1])

    q_map = lambda n, qi_ref, kj_ref: (0, qi_ref[n], 0)
    kv_map = lambda n, qi_ref, kj_ref: (0, kj_ref[n], 0)
    return pl.pallas_call(
        body, name="attn_fwd",
        grid_spec=pltpu.PrefetchScalarGridSpec(
            num_scalar_prefetch=2, grid=(len(pairs),),
            in_specs=[pl.BlockSpec((nh, tq, HEAD_PAD), q_map),
                      pl.BlockSpec((nh, tk, HEAD_PAD), kv_map),
                      pl.BlockSpec((nh, tk, 2 * V_DIM), kv_map)],
            out_specs=[pl.BlockSpec((tq, nh * V_DIM), lambda n, qi_ref, kj_ref: (qi_ref[n], 0)),
                       pl.BlockSpec((nh, tq, 1), q_map)],
            scratch_shapes=[pltpu.VMEM((nh, tq, 1), F32), pltpu.VMEM((nh, tq, 2 * V_DIM), F32)]),
        out_shape=[jax.ShapeDtypeStruct((t, nh * V_DIM), BF16), jax.ShapeDtypeStruct((nh, t, 1), F32)],
        compiler_params=_cparams(),
    )(qi, kj, q, k, v)


def _attn_delta(o, do):
    t, w = o.shape
    nh = w // V_DIM
    tm = min(512, t)

    def body(o_ref, do_ref, d_ref):
        prod = o_ref[...].astype(F32) * do_ref[...].astype(F32)
        for hh in range(nh):
            d_ref[hh] = jnp.sum(prod[:, hh * V_DIM:(hh + 1) * V_DIM], axis=-1, keepdims=True)

    return pl.pallas_call(
        body, name="attn_delta", grid=(t // tm,),
        in_specs=[pl.BlockSpec((tm, w), lambda i: (i, 0)), pl.BlockSpec((tm, w), lambda i: (i, 0))],
        out_specs=pl.BlockSpec((nh, tm, 1), lambda i: (0, i, 0)),
        out_shape=jax.ShapeDtypeStruct((nh, t, 1), F32),
        compiler_params=_cparams(),
    )(o, do)


ATTN_BWD_HEADS = 2


def _attn_bwd(q, k, v, do, lse, delta):
    nh, t, _ = q.shape
    hp = ATTN_BWD_HEADS
    tq = tk = min(512, t)
    nq, nk = t // tq, t // tk

    pairs = [(j, i) for j in range(nk) for i in range(j, nq)]
    kj = jnp.asarray(np.array([j for j, _ in pairs], np.int32))
    qi = jnp.asarray(np.array([i for _, i in pairs], np.int32))

    def body(kj_ref, qi_ref, q_ref, k_ref, v_ref, do_ref, lse_ref, dlt_ref, dq_ref, dk_ref, dv_ref):
        n = pl.program_id(1)
        j, i = kj_ref[n], qi_ref[n]

        @pl.when(n == 0)
        def _():
            dq_ref[...] = jnp.zeros_like(dq_ref)

        def step(diagonal):
            for hh in range(hp):
                qq, kk = q_ref[hh], k_ref[hh]
                dob = do_ref[:, hh * V_DIM:(hh + 1) * V_DIM]
                s = _dot_nt(qq, kk) * ATTN_SCALE
                if diagonal:
                    s = _causal_mask(s)
                p = jnp.exp(s - lse_ref[hh])
                dpp = _dot_nt(dob, v_ref[hh])
                dsb = (p * (dpp - dlt_ref[hh]) * ATTN_SCALE).astype(BF16)
                _accumulate(dv_ref.at[hh], _dot_tn(p.astype(BF16), dob), diagonal)
                _accumulate(dk_ref.at[hh], _dot_tn(dsb, qq), diagonal)
                dq_ref[hh, pl.ds(pl.multiple_of(i * tq, tq), tq), :] += _dot_nn(dsb, kk)

        @pl.when(i > j)
        def _():
            step(False)

        @pl.when(i == j)
        def _():
            step(True)

    q_map = lambda h, n, kj_ref, qi_ref: (h, qi_ref[n], 0)
    k_map = lambda h, n, kj_ref, qi_ref: (h, kj_ref[n], 0)
    return pl.pallas_call(
        body, name="attn_bwd",
        grid_spec=pltpu.PrefetchScalarGridSpec(
            num_scalar_prefetch=2, grid=(nh // hp, len(pairs)),
            in_specs=[pl.BlockSpec((hp, tq, HEAD_PAD), q_map),
                      pl.BlockSpec((hp, tk, HEAD_PAD), k_map),
                      pl.BlockSpec((hp, tk, V_DIM), k_map),
                      pl.BlockSpec((tq, hp * V_DIM), lambda h, n, kj_ref, qi_ref: (qi_ref[n], h)),
                      pl.BlockSpec((hp, tq, 1), q_map),
                      pl.BlockSpec((hp, tq, 1), q_map)],
            out_specs=[pl.BlockSpec((hp, t, HEAD_PAD), lambda h, n, kj_ref, qi_ref: (h, 0, 0)),
                       pl.BlockSpec((hp, tk, HEAD_PAD), k_map),
                       pl.BlockSpec((hp, tk, V_DIM), k_map)]),
        out_shape=[jax.ShapeDtypeStruct((nh, t, HEAD_PAD), F32), jax.ShapeDtypeStruct((nh, t, HEAD_PAD), F32),
                   jax.ShapeDtypeStruct((nh, t, V_DIM), F32)],
        compiler_params=_cparams(),
    )(kj, qi, q, k, v, do, lse, delta)


def _pool_counts(first_token, rows, w):
    tok = lax.broadcasted_iota(jnp.int32, (rows, POOL_CH), 0) + first_token
    return jnp.minimum(tok + 1, w).astype(F32)


def _pool_centered(zbuf, g, w, i, tm):
    lanes = pl.ds(g * POOL_CH, POOL_CH)
    cur = zbuf[pl.ds(POOL_HALO, tm), lanes]
    win = cur
    for s in range(1, w):
        win = win + zbuf[pl.ds(POOL_HALO - s, tm), lanes]
    return win / _pool_counts(i * tm, tm, w) - cur


def _pool_load(zbuf, z_ref, halo_ref, i, tm):
    @pl.when(i == 0)
    def _():
        zbuf[pl.ds(0, POOL_HALO), :] = jnp.zeros((POOL_HALO, zbuf.shape[1]), F32)

    @pl.when(i > 0)
    def _():
        zbuf[pl.ds(0, POOL_HALO), :] = halo_ref[...]

    zbuf[pl.ds(POOL_HALO, tm), :] = z_ref[...]


def _pool_fwd(z, pool_w, pool_scale):
    t = z.shape[0]
    pw = len(POOL_WINDOWS) * POOL_CH
    tm = min(512, t)
    hb = tm // POOL_HALO

    def body(z_ref, halo_ref, w_ref, sc_ref, p_ref, zbuf):
        i = pl.program_id(0)
        _pool_load(zbuf, z_ref, halo_ref, i, tm)
        for g, w in enumerate(POOL_WINDOWS):
            c = _pool_centered(zbuf, g, w, i, tm)
            y = _dot_nn(c.astype(BF16), w_ref[g]) * sc_ref[:, g * POOL_CH:(g + 1) * POOL_CH]
            p_ref[:, g * POOL_CH:(g + 1) * POOL_CH] = y.astype(BF16)

    return pl.pallas_call(
        body, name="pool_fwd", grid=(t // tm,),
        in_specs=[pl.BlockSpec((tm, pw), lambda i: (i, 1)),
                  pl.BlockSpec((POOL_HALO, pw), lambda i: (jnp.maximum(i * hb - 1, 0), 1)),
                  pl.BlockSpec(pool_w.shape, lambda i: (0, 0, 0)),
                  pl.BlockSpec((1, pw), lambda i: (0, 0))],
        out_specs=pl.BlockSpec((tm, pw), lambda i: (i, 0)),
        out_shape=jax.ShapeDtypeStruct((t, pw), BF16),
        scratch_shapes=[pltpu.VMEM((POOL_HALO + tm, pw), F32)],
        compiler_params=_cparams(),
    )(z, z, pool_w, pool_scale)


def _pool_bwd(dp, z, pool_w, pool_scale):
    t = z.shape[0]
    ng = len(POOL_WINDOWS)
    pw = ng * POOL_CH
    tm = min(512, t)
    hb = tm // POOL_HALO
    nt = t // tm

    def body(dp_ref, dpn_ref, z_ref, halo_ref, w_ref, sc_ref, dz_ref, dw_ref, dsc_ref, zbuf, dbuf):
        i = pl.program_id(0)
        _pool_load(zbuf, z_ref, halo_ref, i, tm)

        @pl.when(i == 0)
        def _():
            dw_ref[...] = jnp.zeros_like(dw_ref)
            dsc_ref[...] = jnp.zeros_like(dsc_ref)

        nxt_ok = (i < nt - 1).astype(F32)
        for g, w in enumerate(POOL_WINDOWS):
            lanes = pl.ds(g * POOL_CH, POOL_CH)
            cols = slice(g * POOL_CH, (g + 1) * POOL_CH)
            sc = sc_ref[:, cols]
            wg = w_ref[g]
            c = _pool_centered(zbuf, g, w, i, tm).astype(BF16)
            ypre = _dot_nn(c, wg)
            dpg = dp_ref[:, cols].astype(F32)
            dsc_ref[:, cols] += jnp.sum(dpg * ypre, axis=0, keepdims=True)
            dyb = (dpg * sc).astype(BF16)
            dw_ref[g] += _dot_tn(c, dyb)
            dd = _dot_nt(dyb, wg)
            dyn = (dpn_ref[:, cols].astype(F32) * sc).astype(BF16)
            ddn = _dot_nt(dyn, wg) * nxt_ok
            dbuf[pl.ds(0, tm), lanes] = dd / _pool_counts(i * tm, tm, w)
            dbuf[pl.ds(tm, POOL_HALO), lanes] = ddn / _pool_counts((i + 1) * tm, POOL_HALO, w)
            acc = -dd
            for s in range(w):
                acc = acc + dbuf[pl.ds(s, tm), lanes]
            dz_ref[:, cols] = acc

    return pl.pallas_call(
        body, name="pool_bwd", grid=(nt,),
        in_specs=[pl.BlockSpec((tm, pw), lambda i: (i, 0)),
                  pl.BlockSpec((POOL_HALO, pw), lambda i: (jnp.minimum((i + 1) * hb, t // POOL_HALO - 1), 0)),
                  pl.BlockSpec((tm, pw), lambda i: (i, 1)),
                  pl.BlockSpec((POOL_HALO, pw), lambda i: (jnp.maximum(i * hb - 1, 0), 1)),
                  pl.BlockSpec(pool_w.shape, lambda i: (0, 0, 0)),
                  pl.BlockSpec((1, pw), lambda i: (0, 0))],
        out_specs=[pl.BlockSpec((tm, pw), lambda i: (i, 0)),
                   pl.BlockSpec((ng, POOL_CH, POOL_CH), lambda i: (0, 0, 0)),
                   pl.BlockSpec((1, pw), lambda i: (0, 0))],
        out_shape=[jax.ShapeDtypeStruct((t, pw), F32), jax.ShapeDtypeStruct((ng, POOL_CH, POOL_CH), F32),
                   jax.ShapeDtypeStruct((1, pw), F32)],
        scratch_shapes=[pltpu.VMEM((POOL_HALO + tm, pw), F32), pltpu.VMEM((tm + POOL_HALO, pw), F32)],
        compiler_params=_cparams(),
    )(dp, dp, z, z, pool_w, pool_scale)


def _mla_bwd(dq_h, dk_h, dv_h, z, dz_pool, h1, dh2, mix_norm, w_in, q_norm, wq_t, kv_norm, wkv, pos, rope_tab):
    t, d = h1.shape
    tm = min(512, t)

    def body(dqh_ref, dkh_ref, dvh_ref, z_ref, dzp_ref, h_ref, dh2_ref, gm_ref, win_ref, gq_ref, wq_ref, gkv_ref,
             wkv_ref, pos_ref, tab_ref, dh1_ref, dq_ref, dkv_ref, dz_ref, dgq_ref, dgkv_ref, dgm_ref):
        i = pl.program_id(0)
        first = i == 0
        cos_t, sin_t = _rope_tables(pos_ref[...], tab_ref[...])
        dq_parts, dkv_parts = [], []
        dk_pe = jnp.zeros((tm, 128), F32)
        for hh in range(MLA_HEADS):
            dqh = dqh_ref[hh]
            dq_parts += [dqh[:, 0:NOPE], _rope_apply_t(dqh[:, NOPE:HEAD_PAD], cos_t, sin_t)]
            dkh = dkh_ref[hh]
            dkv_parts += [dkh[:, 0:NOPE], dvh_ref[hh]]
            dk_pe = dk_pe + dkh[:, NOPE:HEAD_PAD]
        dqb = jnp.concatenate(dq_parts, axis=-1).astype(BF16)
        dkvb = jnp.concatenate(dkv_parts, axis=-1).astype(BF16)
        dq_ref[...] = dqb
        dkv_ref[...] = dkvb
        z = z_ref[...]
        c_q = z[:, 0:Q_RANK]
        gq = gq_ref[...]
        _, rq = _rms_fwd(c_q, gq)
        dcq, dgq = _rms_bwd(_dot_nn(dqb, wq_ref[...]), c_q, gq, rq)
        c_kv = z[:, Q_RANK:Q_RANK + KV_RANK]
        gkv = gkv_ref[...]
        _, rkv = _rms_fwd(c_kv, gkv)
        dckv, dgkv = _rms_bwd(_dot_nt(dkvb, wkv_ref[...]), c_kv, gkv, rkv)
        dkr = _rope_apply_t(dk_pe, cos_t, sin_t)
        dzb = jnp.concatenate([dcq, dckv, dkr, dzp_ref[...]], axis=-1).astype(BF16)
        dz_ref[...] = dzb
        x = h_ref[...]
        gm = gm_ref[...]
        _, rm = _rms_fwd(x, gm)
        dx, dgm = _rms_bwd(_dot_nt(dzb, win_ref[...]), x, gm, rm)
        dh1_ref[...] = dh2_ref[...] + dx
        _accumulate(dgq_ref, dgq, first)
        _accumulate(dgkv_ref, dgkv, first)
        _accumulate(dgm_ref, dgm, first)

    full = lambda shape: pl.BlockSpec(shape, lambda i: (0,) * len(shape))
    row = lambda w: pl.BlockSpec((tm, w), lambda i: (i, 0))
    head = lambda w: pl.BlockSpec((MLA_HEADS, tm, w), lambda i: (0, i, 0))
    pw = len(POOL_WINDOWS) * POOL_CH
    return pl.pallas_call(
        body, name="mla_bwd", grid=(t // tm,),
        in_specs=[head(HEAD_PAD), head(HEAD_PAD), head(V_DIM), row(d), row(pw), row(d), row(d),
                  _resident((1, d)), _resident(w_in.shape), _resident((1, Q_RANK)), _resident(wq_t.shape),
                  _resident((1, KV_RANK)), _resident(wkv.shape), row(1), _resident(rope_tab.shape)],
        out_specs=[row(d), row(d), row(d), row(d), full((1, Q_RANK)), full((1, KV_RANK)), full((1, d))],
        out_shape=[jax.ShapeDtypeStruct((t, d), F32), jax.ShapeDtypeStruct((t, d), BF16),
                   jax.ShapeDtypeStruct((t, d), BF16), jax.ShapeDtypeStruct((t, d), BF16),
                   jax.ShapeDtypeStruct((1, Q_RANK), F32), jax.ShapeDtypeStruct((1, KV_RANK), F32),
                   jax.ShapeDtypeStruct((1, d), F32)],
        compiler_params=_cparams(),
    )(dq_h, dk_h, dv_h, z, dz_pool, h1, dh2, mix_norm, w_in, q_norm, wq_t, kv_norm, wkv, pos, rope_tab)


def _mem_kv(mem, mem_norm, wmkv):
    n, d = mem.shape

    def body(mem_ref, g_ref, w_ref, memn_ref, k_ref, v_ref):
        y, _ = _rms_fwd(mem_ref[...], g_ref[...])
        yb = y.astype(BF16)
        memn_ref[...] = yb
        for hh in range(MEM_HEADS):
            k_ref[hh] = _dot_nn(yb, w_ref[hh]).astype(BF16)
            v_ref[hh] = _dot_nn(yb, w_ref[MEM_HEADS + hh]).astype(BF16)

    return pl.pallas_call(
        body, name="mem_kv",
        out_shape=[jax.ShapeDtypeStruct((n, d), BF16), jax.ShapeDtypeStruct((MEM_HEADS, n, MEM_HD), BF16),
                   jax.ShapeDtypeStruct((MEM_HEADS, n, MEM_HD), BF16)],
        compiler_params=_cparams(),
    )(mem, mem_norm, wmkv)


def _mem_softmax(qb, km):
    s = _dot_nt(qb, km) * MEM_SCALE
    e = jnp.exp(s - jnp.max(s, axis=-1, keepdims=True))
    return e / jnp.sum(e, axis=-1, keepdims=True)


def _xattn_fwd(h1, a, p, w_out, g, wmq, km, vm, wmo, token=None):
    t, d = h1.shape
    tm = min(512, t)
    half = a.shape[1]

    def body(h_ref, a_ref, p_ref, wo_ref, g_ref, wmq_ref, km_ref, vm_ref, wmo_ref,
             h2_ref, h3_ref, hn_ref, q_ref, o_ref):
        h2 = h_ref[...] + _dot_nn(a_ref[...], wo_ref[0:half, :]) + _dot_nn(p_ref[...], wo_ref[half:2 * half, :])
        h2_ref[...] = h2
        hn, _ = _rms_fwd(h2, g_ref[...])
        hnb = hn.astype(BF16)
        hn_ref[...] = hnb
        qb = _dot_nn(hnb, wmq_ref[...]).astype(BF16)
        q_ref[...] = qb
        outs = []
        for hh in range(MEM_HEADS):
            pr = _mem_softmax(qb[:, hh * MEM_HD:(hh + 1) * MEM_HD], km_ref[hh])
            outs.append(_dot_nn(pr.astype(BF16), vm_ref[hh]))
        ob = jnp.concatenate(outs, axis=-1).astype(BF16)
        o_ref[...] = ob
        h3_ref[...] = h2 + _dot_nn(ob, wmo_ref[...])

    full = lambda shape: pl.BlockSpec(shape, lambda i: (0,) * len(shape))
    row = lambda w: pl.BlockSpec((tm, w), lambda i: (i, 0))
    return _call_after(
        token, body,
        [row(d), row(half), row(half), _resident(w_out.shape), _resident((1, d)), _resident(wmq.shape),
         _resident(km.shape), _resident(vm.shape), _resident(wmo.shape)],
        (h1, a, p, w_out, g, wmq, km, vm, wmo),
        name="xattn_fwd", grid=(t // tm,),
        out_specs=[row(d), row(d), row(d), row(d), row(d)],
        out_shape=[jax.ShapeDtypeStruct((t, d), F32), jax.ShapeDtypeStruct((t, d), F32),
                   jax.ShapeDtypeStruct((t, d), BF16), jax.ShapeDtypeStruct((t, d), BF16),
                   jax.ShapeDtypeStruct((t, d), BF16)],
        compiler_params=_cparams(),
    )


def _xattn_bwd(dh3, h2, qm, g, wmq, km, vm, wmo, w_out, token=None):
    t, d = h2.shape
    tm = min(512, t)
    half = d // 2

    def body(dh3_ref, h2_ref, q_ref, g_ref, wmq_ref, km_ref, vm_ref, wmo_ref, wo_ref,
             dh2_ref, dq_ref, da_ref, dp_ref, dk_ref, dv_ref, dg_ref):
        i = pl.program_id(0)
        first = i == 0

        @pl.when(first)
        def _():
            dk_ref[...] = jnp.zeros_like(dk_ref)
            dv_ref[...] = jnp.zeros_like(dv_ref)

        dh3 = dh3_ref[...]
        dob = _dot_nt(dh3.astype(BF16), wmo_ref[...]).astype(BF16)
        qb = q_ref[...]
        dq_parts = []
        for hh in range(MEM_HEADS):
            cols = slice(hh * MEM_HD, (hh + 1) * MEM_HD)
            kk, vv = km_ref[hh], vm_ref[hh]
            pr = _mem_softmax(qb[:, cols], kk)
            doh = dob[:, cols]
            dv_ref[hh] += _dot_tn(pr.astype(BF16), doh)
            dpp = _dot_nt(doh, vv)
            dsb = (pr * (dpp - jnp.sum(dpp * pr, axis=-1, keepdims=True)) * MEM_SCALE).astype(BF16)
            dq_parts.append(_dot_nn(dsb, kk))
            dk_ref[hh] += _dot_tn(dsb, qb[:, cols])
        dqb = jnp.concatenate(dq_parts, axis=-1).astype(BF16)
        dq_ref[...] = dqb
        x = h2_ref[...]
        gg = g_ref[...]
        _, r = _rms_fwd(x, gg)
        dx, dg = _rms_bwd(_dot_nt(dqb, wmq_ref[...]), x, gg, r)
        dh2 = dh3 + dx
        dh2_ref[...] = dh2
        dap = _dot_nt(dh2.astype(BF16), wo_ref[...])
        da_ref[...] = dap[:, 0:half].astype(BF16)
        dp_ref[...] = dap[:, half:d].astype(BF16)
        _accumulate(dg_ref, dg, first)

    full = lambda shape: pl.BlockSpec(shape, lambda i: (0,) * len(shape))
    row = lambda w: pl.BlockSpec((tm, w), lambda i: (i, 0))
    return _call_after(
        token, body,
        [row(d), row(d), row(d), _resident((1, d)), _resident(wmq.shape), _resident(km.shape), _resident(vm.shape),
         _resident(wmo.shape), _resident(w_out.shape)],
        (dh3, h2, qm, g, wmq, km, vm, wmo, w_out),
        name="xattn_bwd", grid=(t // tm,),
        out_specs=[row(d), row(d), row(half), row(half), full(km.shape), full(vm.shape), full((1, d))],
        out_shape=[jax.ShapeDtypeStruct((t, d), F32), jax.ShapeDtypeStruct((t, d), BF16),
                   jax.ShapeDtypeStruct((t, half), BF16), jax.ShapeDtypeStruct((t, half), BF16),
                   jax.ShapeDtypeStruct(km.shape, F32), jax.ShapeDtypeStruct(vm.shape, F32),
                   jax.ShapeDtypeStruct((1, d), F32)],
        compiler_params=_cparams(),
    )


def _mem_kv_bwd(dkm, dvm, memn, mem, mem_norm, wmkv):
    n, d = mem.shape

    def body(dk_ref, dv_ref, memn_ref, mem_ref, g_ref, w_ref, dw_ref, dg_ref):
        memn = memn_ref[...]
        dmemn = jnp.zeros((n, d), F32)
        for s in range(2 * MEM_HEADS):
            src = dk_ref[s] if s < MEM_HEADS else dv_ref[s - MEM_HEADS]
            db = src.astype(BF16)
            dw_ref[s] = _dot_tn(memn, db)
            dmemn = dmemn + _dot_nt(db, w_ref[s])
        x = mem_ref[...]
        gg = g_ref[...]
        _, r = _rms_fwd(x, gg)
        _, dg = _rms_bwd(dmemn, x, gg, r)
        dg_ref[...] = dg

    return pl.pallas_call(
        body, name="mem_kv_bwd",
        out_shape=[jax.ShapeDtypeStruct(wmkv.shape, F32), jax.ShapeDtypeStruct((1, d), F32)],
        compiler_params=_cparams(),
    )(dkm, dvm, memn, mem, mem_norm, wmkv)


MESH_ID = pl.DeviceIdType.MESH
ANY = pl.BlockSpec(memory_space=pl.ANY)


def _coords():
    return lax.axis_index("x"), lax.axis_index("y"), lax.axis_index("c")


def _other_chips(x, y):
    return [(1 - x, y), (x, 1 - y), (1 - x, 1 - y)]


def _core_reduce(g, tag):
    _, r, w = g.shape

    def body(g_ref, part_ref, own_sc, recv_sc, send_sems, recv_sems, local_sems):
        x, y, c = _coords()
        sent, local = [], []
        for chip in range(4):
            sent.append(pltpu.make_async_remote_copy(
                src_ref=g_ref.at[2 * chip + (1 - c)], dst_ref=recv_sc.at[chip],
                send_sem=send_sems.at[chip], recv_sem=recv_sems.at[chip],
                device_id=(x, y, 1 - c), device_id_type=MESH_ID))
            local.append(pltpu.make_async_copy(g_ref.at[2 * chip + c], own_sc.at[chip], local_sems.at[chip]))
        for cp in sent + local:
            cp.start()
        for chip in range(4):
            local[chip].wait()
            sent[chip].wait_recv()
            part_ref[chip] = (own_sc[chip].astype(F32) + recv_sc[chip].astype(F32)).astype(part_ref.dtype)
        for cp in sent:
            cp.wait_send()

    return pl.pallas_call(
        body, name="core_reduce_" + tag,
        out_shape=jax.ShapeDtypeStruct((4, r, w), g.dtype),
        in_specs=[ANY], out_specs=pl.BlockSpec(memory_space=pltpu.VMEM),
        scratch_shapes=[pltpu.VMEM((4, r, w), g.dtype), pltpu.VMEM((4, r, w), g.dtype),
                        pltpu.SemaphoreType.DMA((4,)), pltpu.SemaphoreType.DMA((4,)), pltpu.SemaphoreType.DMA((4,))],
        compiler_params=_cparams(),
    )(g)


HBM_SPEC = pl.BlockSpec(memory_space=pltpu.HBM)
SEM_SPEC = pl.BlockSpec(memory_space=pltpu.SEMAPHORE)
SPLIT_EFFECT = pltpu.SideEffectType.DATAFLOW_SIDE_EFFECTING


def _ici_refs(gather, src_ref, land_ref, j, px, py, slot_chip, c):
    if gather:
        return src_ref, land_ref.at[:, 4 * slot_chip[0] + 2 * slot_chip[1] + c]
    return src_ref.at[2 * px + py], land_ref.at[j]


def _ici_start(src, after, name, gather):
    r, w = src.shape[-2:]
    land_shape = (src.shape[0], N_DEV, r, w) if gather else (3, r, w)

    def body(src_ref, land_ref, after_ref, send_sems, recv_sems, src_thru, land_thru, token):
        x, y, c = _coords()
        for j, (px, py) in enumerate(_other_chips(x, y)):
            s_ref, d_ref = _ici_refs(gather, src_ref, land_ref, j, px, py, (x, y), c)
            pltpu.make_async_remote_copy(
                src_ref=s_ref, dst_ref=d_ref, send_sem=send_sems.at[j], recv_sem=recv_sems.at[j],
                device_id=(px, py, c), device_id_type=MESH_ID).start()
        token[...] = jnp.zeros_like(token)

    return pl.pallas_call(
        body, name=name,
        out_shape=(pltpu.SemaphoreType.DMA((3,)), pltpu.SemaphoreType.DMA((3,)), pltpu.HBM(src.shape, src.dtype),
                   pltpu.HBM(land_shape, src.dtype), jax.ShapeDtypeStruct((8, 128), F32)),
        in_specs=(HBM_SPEC, HBM_SPEC, ANY),
        out_specs=(SEM_SPEC, SEM_SPEC, HBM_SPEC, HBM_SPEC, pl.BlockSpec(memory_space=pltpu.VMEM)),
        input_output_aliases={0: 2, 1: 3},
        compiler_params=pltpu.CompilerParams(has_side_effects=SPLIT_EFFECT),
    )(pltpu.with_memory_space_constraint(src, pltpu.HBM),
      pltpu.with_memory_space_constraint(lax.empty(land_shape, src.dtype), pltpu.HBM), after)


def _ici_wait(started, after, name, gather):
    send_sems, recv_sems, src_thru, land_thru, _ = started

    def body(src_ref, land_ref, send_sems, recv_sems, after_ref, src_dead, got_ref):
        x, y, c = _coords()
        for j, (px, py) in enumerate(_other_chips(x, y)):
            s_ref, d_ref = _ici_refs(gather, src_ref, land_ref, j, px, py, (px, py), c)
            copy = pltpu.make_async_remote_copy(
                src_ref=s_ref, dst_ref=d_ref, send_sem=send_sems.at[j], recv_sem=recv_sems.at[j],
                device_id=(px, py, c), device_id_type=MESH_ID)
            copy.wait_send()
            copy.wait_recv()

    return pl.pallas_call(
        body, name=name,
        out_shape=(pltpu.HBM(src_thru.shape, src_thru.dtype), pltpu.HBM(land_thru.shape, land_thru.dtype)),
        in_specs=(HBM_SPEC, HBM_SPEC, SEM_SPEC, SEM_SPEC, ANY),
        out_specs=(HBM_SPEC, HBM_SPEC), input_output_aliases={0: 0, 1: 1},
        compiler_params=pltpu.CompilerParams(has_side_effects=SPLIT_EFFECT),
    )(src_thru, land_thru, send_sems, recv_sems, after)


def _neighbour(k, x, y):
    return (1 - x, y) if k == 0 else (x, 1 - y)


def _slot(ref, px, py, c):
    return ref.at[:, 4 * px + 2 * py + c]


def _near_start(src, after, name):
    land_shape = (src.shape[0], N_DEV) + src.shape[1:]

    def body(src_ref, land_ref, after_ref, send_sems, recv_sems, src_thru, land_thru, token):
        x, y, c = _coords()
        for k in range(2):
            px, py = _neighbour(k, x, y)
            pltpu.make_async_remote_copy(
                src_ref=src_ref, dst_ref=_slot(land_ref, x, y, c), send_sem=send_sems.at[k],
                recv_sem=recv_sems.at[k], device_id=(px, py, c), device_id_type=MESH_ID).start()
        token[...] = jnp.zeros_like(token)

    return pl.pallas_call(
        body, name=name,
        out_shape=(pltpu.SemaphoreType.DMA((2,)), pltpu.SemaphoreType.DMA((2,)), pltpu.HBM(src.shape, src.dtype),
                   pltpu.HBM(land_shape, src.dtype), jax.ShapeDtypeStruct((8, 128), F32)),
        in_specs=(HBM_SPEC, HBM_SPEC, ANY),
        out_specs=(SEM_SPEC, SEM_SPEC, HBM_SPEC, HBM_SPEC, pl.BlockSpec(memory_space=pltpu.VMEM)),
        input_output_aliases={0: 2, 1: 3},
        compiler_params=pltpu.CompilerParams(has_side_effects=SPLIT_EFFECT),
    )(pltpu.with_memory_space_constraint(src, pltpu.HBM),
      pltpu.with_memory_space_constraint(lax.empty(land_shape, src.dtype), pltpu.HBM), after)


def _near_wait(started, after, name):
    send_sems, recv_sems, src_thru, land_thru, _ = started

    def body(src_ref, land_ref, send_sems, recv_sems, after_ref, src_dead, got_ref):
        x, y, c = _coords()
        for k in range(2):
            px, py = _neighbour(k, x, y)
            copy = pltpu.make_async_remote_copy(
                src_ref=src_ref, dst_ref=_slot(land_ref, px, py, c), send_sem=send_sems.at[k],
                recv_sem=recv_sems.at[k], device_id=(px, py, c), device_id_type=MESH_ID)
            copy.wait_send()
            copy.wait_recv()

    return pl.pallas_call(
        body, name=name,
        out_shape=(pltpu.HBM(src_thru.shape, src_thru.dtype), pltpu.HBM(land_thru.shape, land_thru.dtype)),
        in_specs=(HBM_SPEC, HBM_SPEC, SEM_SPEC, SEM_SPEC, ANY),
        out_specs=(HBM_SPEC, HBM_SPEC), input_output_aliases={0: 0, 1: 1},
        compiler_params=pltpu.CompilerParams(has_side_effects=SPLIT_EFFECT),
    )(src_thru, land_thru, send_sems, recv_sems, after)


def _far_refs(land_ref, k, x, y, c, arriving):
    half = land_ref.shape[2] // 2
    rows = pl.ds(k * half, half)
    ox, oy = (1 - x, 1 - y) if arriving else _neighbour(k, x, y)
    return land_ref.at[:, 4 * ox + 2 * oy + c, rows]


def _far_start(land, after, name):
    def body(land_ref, after_ref, send_sems, recv_sems, land_thru, token):
        x, y, c = _coords()
        for k in range(2):
            block = _far_refs(land_ref, k, x, y, c, False)
            px, py = _neighbour(1 - k, x, y)
            pltpu.make_async_remote_copy(
                src_ref=block, dst_ref=block, send_sem=send_sems.at[k], recv_sem=recv_sems.at[k],
                device_id=(px, py, c), device_id_type=MESH_ID).start()
        token[...] = jnp.zeros_like(token)

    return pl.pallas_call(
        body, name=name,
        out_shape=(pltpu.SemaphoreType.DMA((2,)), pltpu.SemaphoreType.DMA((2,)),
                   pltpu.HBM(land.shape, land.dtype), jax.ShapeDtypeStruct((8, 128), F32)),
        in_specs=(HBM_SPEC, ANY),
        out_specs=(SEM_SPEC, SEM_SPEC, HBM_SPEC, pl.BlockSpec(memory_space=pltpu.VMEM)),
        input_output_aliases={0: 2},
        compiler_params=pltpu.CompilerParams(has_side_effects=SPLIT_EFFECT),
    )(pltpu.with_memory_space_constraint(land, pltpu.HBM), after)


def _far_wait(started, after, name):
    send_sems, recv_sems, land_thru, _ = started

    def body(land_ref, send_sems, recv_sems, after_ref, got_ref):
        x, y, c = _coords()
        for k in range(2):
            px, py = _neighbour(1 - k, x, y)
            copy = pltpu.make_async_remote_copy(
                src_ref=_far_refs(land_ref, k, x, y, c, False), dst_ref=_far_refs(land_ref, k, x, y, c, True),
                send_sem=send_sems.at[k], recv_sem=recv_sems.at[k], device_id=(px, py, c), device_id_type=MESH_ID)
            copy.wait_send()
            copy.wait_recv()

    return pl.pallas_call(
        body, name=name,
        out_shape=pltpu.HBM(land_thru.shape, land_thru.dtype),
        in_specs=(HBM_SPEC, SEM_SPEC, SEM_SPEC, ANY),
        out_specs=HBM_SPEC, input_output_aliases={0: 0},
        compiler_params=pltpu.CompilerParams(has_side_effects=SPLIT_EFFECT),
    )(land_thru, send_sems, recv_sems, after)


def _peer(k, x, y, c):
    return x ^ ((k >> 2) & 1), y ^ ((k >> 1) & 1), c ^ (k & 1)


def _peers_start(src, after, name):
    r, w = src.shape
    x, y, c = _coords()
    land = lax.dynamic_update_slice(jnp.zeros((N_DEV, r, w), src.dtype), src[None], (4 * x + 2 * y + c, 0, 0))

    def body(src_ref, land_ref, after_ref, send_sems, recv_sems, src_thru, land_thru, token):
        x, y, c = _coords()
        for k in range(1, N_DEV):
            pltpu.make_async_remote_copy(
                src_ref=src_ref, dst_ref=land_ref.at[4 * x + 2 * y + c],
                send_sem=send_sems.at[k - 1], recv_sem=recv_sems.at[k - 1],
                device_id=_peer(k, x, y, c), device_id_type=MESH_ID).start()
        token[...] = jnp.zeros_like(token)

    return pl.pallas_call(
        body, name=name,
        out_shape=(pltpu.SemaphoreType.DMA((N_DEV - 1,)), pltpu.SemaphoreType.DMA((N_DEV - 1,)),
                   pltpu.HBM(src.shape, src.dtype), pltpu.HBM(land.shape, src.dtype),
                   jax.ShapeDtypeStruct((8, 128), F32)),
        in_specs=(HBM_SPEC, HBM_SPEC, ANY),
        out_specs=(SEM_SPEC, SEM_SPEC, HBM_SPEC, HBM_SPEC, pl.BlockSpec(memory_space=pltpu.VMEM)),
        input_output_aliases={0: 2, 1: 3},
        compiler_params=pltpu.CompilerParams(has_side_effects=SPLIT_EFFECT),
    )(pltpu.with_memory_space_constraint(src, pltpu.HBM), pltpu.with_memory_space_constraint(land, pltpu.HBM), after)


def _peers_wait(started, after, name):
    send_sems, recv_sems, src_thru, land_thru, _ = started

    def body(src_ref, land_ref, send_sems, recv_sems, after_ref, src_dead, got_ref):
        x, y, c = _coords()
        for k in range(1, N_DEV):
            px, py, pc = _peer(k, x, y, c)
            copy = pltpu.make_async_remote_copy(
                src_ref=src_ref, dst_ref=land_ref.at[4 * px + 2 * py + pc],
                send_sem=send_sems.at[k - 1], recv_sem=recv_sems.at[k - 1],
                device_id=(px, py, pc), device_id_type=MESH_ID)
            copy.wait_send()
            copy.wait_recv()

    return pl.pallas_call(
        body, name=name,
        out_shape=(pltpu.HBM(src_thru.shape, src_thru.dtype), pltpu.HBM(land_thru.shape, land_thru.dtype)),
        in_specs=(HBM_SPEC, HBM_SPEC, SEM_SPEC, SEM_SPEC, ANY),
        out_specs=(HBM_SPEC, HBM_SPEC), input_output_aliases={0: 0, 1: 1},
        compiler_params=pltpu.CompilerParams(has_side_effects=SPLIT_EFFECT),
    )(src_thru, land_thru, send_sems, recv_sems, after)


def _share_refs(ref, k, x, y, c, sender_c):
    px, py = ([(x, y)] + _other_chips(x, y))[k]
    return ref.at[:, 4 * px + 2 * py + sender_c]


def _share_start(gathered, after, name):
    def body(g_ref, after_ref, send_sems, recv_sems, g_thru, token):
        x, y, c = _coords()
        for k in range(4):
            slot = _share_refs(g_ref, k, x, y, c, c)
            pltpu.make_async_remote_copy(
                src_ref=slot, dst_ref=slot, send_sem=send_sems.at[k], recv_sem=recv_sems.at[k],
                device_id=(x, y, 1 - c), device_id_type=MESH_ID).start()
        token[...] = jnp.zeros_like(token)

    return pl.pallas_call(
        body, name=name,
        out_shape=(pltpu.SemaphoreType.DMA((4,)), pltpu.SemaphoreType.DMA((4,)),
                   pltpu.HBM(gathered.shape, gathered.dtype), jax.ShapeDtypeStruct((8, 128), F32)),
        in_specs=(HBM_SPEC, ANY),
        out_specs=(SEM_SPEC, SEM_SPEC, HBM_SPEC, pl.BlockSpec(memory_space=pltpu.VMEM)),
        input_output_aliases={0: 2},
        compiler_params=pltpu.CompilerParams(has_side_effects=SPLIT_EFFECT),
    )(pltpu.with_memory_space_constraint(gathered, pltpu.HBM), after)


def _share_wait(started, after, name):
    send_sems, recv_sems, g_thru, _ = started

    def body(g_ref, send_sems, recv_sems, after_ref, got_ref):
        x, y, c = _coords()
        for k in range(4):
            copy = pltpu.make_async_remote_copy(
                src_ref=_share_refs(g_ref, k, x, y, c, c), dst_ref=_share_refs(g_ref, k, x, y, c, 1 - c),
                send_sem=send_sems.at[k], recv_sem=recv_sems.at[k],
                device_id=(x, y, 1 - c), device_id_type=MESH_ID)
            copy.wait_send()
            copy.wait_recv()

    return pl.pallas_call(
        body, name=name,
        out_shape=pltpu.HBM(g_thru.shape, g_thru.dtype),
        in_specs=(HBM_SPEC, SEM_SPEC, SEM_SPEC, ANY),
        out_specs=HBM_SPEC, input_output_aliases={0: 0},
        compiler_params=pltpu.CompilerParams(has_side_effects=SPLIT_EFFECT),
    )(g_thru, send_sems, recv_sems, after)


def _core_share(own, gathered, name):
    def body(own_ref, gin_ref, out_ref, stage, send_sems, recv_sems, local_sem):
        x, y, c = _coords()
        sibling = (x, y, 1 - c)
        chips = [(x, y)] + _other_chips(x, y)
        stage_in = pltpu.make_async_copy(own_ref, stage, local_sem)
        stage_in.start()
        sent, arriving = [], []
        for k, (px, py) in enumerate(chips):
            slot = out_ref.at[:, 4 * px + 2 * py + c]
            sent.append(pltpu.make_async_remote_copy(
                src_ref=own_ref if k == 0 else slot, dst_ref=slot,
                send_sem=send_sems.at[k], recv_sem=recv_sems.at[k], device_id=sibling, device_id_type=MESH_ID))
            arriving.append(pltpu.make_async_remote_copy(
                src_ref=own_ref, dst_ref=out_ref.at[:, 4 * px + 2 * py + (1 - c)],
                send_sem=send_sems.at[k], recv_sem=recv_sems.at[k], device_id=sibling, device_id_type=MESH_ID))
        for cp in sent:
            cp.start()
        stage_in.wait()
        stage_out = pltpu.make_async_copy(stage, out_ref.at[:, 4 * x + 2 * y + c], local_sem)
        stage_out.start()
        for cp in arriving:
            cp.wait_recv()
        for cp in sent:
            cp.wait_send()
        stage_out.wait()

    return pl.pallas_call(
        body, name=name,
        out_shape=jax.ShapeDtypeStruct(gathered.shape, own.dtype),
        in_specs=[ANY, ANY], out_specs=ANY, input_output_aliases={1: 0},
        scratch_shapes=[pltpu.VMEM(own.shape, own.dtype), pltpu.SemaphoreType.DMA((4,)),
                        pltpu.SemaphoreType.DMA((4,)), pltpu.SemaphoreType.DMA],
    )(own, gathered)


def _adamw(w, g, m, v):
    m = ADAM_B1 * m + (1.0 - ADAM_B1) * g
    v = ADAM_B2 * v + (1.0 - ADAM_B2) * (g * g)
    m_hat = m / ADAM_C1
    v_hat = v / ADAM_C2
    delta = -ADAM_LR * (m_hat / (jnp.sqrt(v_hat) + ADAM_EPS) + ADAM_WD * w)
    return delta, m, v


def _adam_big(units, chip_idx, tag, token):
    n = len(units)
    r, wd = units[0][2].shape
    tr, tw = _row_tile(r, 1024), 256

    def body(s_ref, tok_ref, *refs):
        for u in range(n):
            p_ref, l_ref, w_ref, m_ref, v_ref = refs[5 * u:5 * u + 5]
            g_ref, d_ref, mo_ref, vo_ref = refs[5 * n + 4 * u:5 * n + 4 * u + 4]
            g = p_ref[0].astype(F32)
            for j in range(3):
                g = g + l_ref[j].astype(F32)
            delta, mn, vn = _adamw(w_ref[...], g, m_ref[...], v_ref[...])
            g_ref[...] = g
            d_ref[...] = delta
            mo_ref[...] = mn
            vo_ref[...] = vn

    row = pl.BlockSpec((tr, tw), lambda i, j, s: (i, j))
    unit_specs = [pl.BlockSpec((1, tr, tw), lambda i, j, s: (s[0], i, j)),
                  pl.BlockSpec((3, tr, tw), lambda i, j, s: (0, i, j)), row, row, row]
    outs = pl.pallas_call(
        body, name="adam_big_" + tag,
        grid_spec=pltpu.PrefetchScalarGridSpec(
            num_scalar_prefetch=1, grid=(r // tr, wd // tw),
            in_specs=[pl.BlockSpec((8, 128), lambda i, j, s: (0, 0))] + unit_specs * n,
            out_specs=[row] * (4 * n)),
        out_shape=[jax.ShapeDtypeStruct((r, wd), F32)] * (4 * n),
        compiler_params=_cparams(),
    )(chip_idx, token, *[a for unit in units for a in unit])
    return [outs[4 * u:4 * u + 4] for u in range(n)]


def _adam_small(parts, w, m, v):
    _, r, wd = parts.shape

    def body(p_ref, w_ref, m_ref, v_ref, g_ref, d_ref, mo_ref, vo_ref):
        g = p_ref[0]
        for k in range(1, N_DEV):
            g = g + p_ref[k]
        delta, mn, vn = _adamw(w_ref[...], g, m_ref[...], v_ref[...])
        g_ref[...] = g
        d_ref[...] = delta
        mo_ref[...] = mn
        vo_ref[...] = vn

    return pl.pallas_call(
        body, name="adam_small",
        out_shape=[jax.ShapeDtypeStruct((r, wd), F32)] * 4,
        compiler_params=_cparams(),
    )(parts, w, m, v)


def _pad_rows(a, rows):
    return jnp.pad(a, ((0, rows - a.shape[0]), (0, 0)))


def _pad_w_in(w):
    cut = Q_RANK + KV_RANK + ROPE
    return jnp.concatenate([w[:, :cut], jnp.zeros((w.shape[0], 64), w.dtype), w[:, cut:]], axis=1)


def _unpad_w_in(w):
    cut = Q_RANK + KV_RANK + ROPE
    return jnp.concatenate([w[:, :cut], w[:, cut + 64:]], axis=1)


def _pack_mid(p):
    parts = [_pad_w_in(p["w_in"][0]), p["w_out"][0], p["w_mq"][0], p["w_mo"][0],
             p["w_mkv"][0].reshape(256, D_MODEL),
             _pad_rows(p["w_q_up"][0].T.reshape(24, D_MODEL), 32),
             p["w_kv_up"][0].reshape(16, D_MODEL)]
    return jnp.concatenate(parts, axis=0)


def _pack_ffn(w_gate, w_up, w_down, name):
    d, rows = w_gate.shape[1:]

    def body(g_ref, u_ref, d_ref, o_ref):
        eye = (lax.broadcasted_iota(jnp.int32, (d, d), 0) == lax.broadcasted_iota(jnp.int32, (d, d), 1)).astype(BF16)
        o_ref[0] = _dot_tn(g_ref[0].astype(BF16), eye).astype(BF16)
        o_ref[1] = _dot_tn(u_ref[0].astype(BF16), eye).astype(BF16)
        o_ref[2] = d_ref[0].astype(BF16)

    return pl.pallas_call(
        body, name=name, out_shape=jax.ShapeDtypeStruct((3, rows, d), BF16), compiler_params=_cparams(),
    )(w_gate, w_up, w_down)


def _pack_segments(p, group):
    if group == "mid":
        return _pack_mid(p)[None].astype(BF16)
    return _pack_ffn(p[group + "_w_gate"], p[group + "_w_up"], p[group + "_w_down"], "pack_" + group)


UNIT_WEIGHT = {"ffn1_g": ("ffn1_w_gate", True), "ffn1_u": ("ffn1_w_up", True), "ffn1_d": ("ffn1_w_down", False),
               "ffn2_g": ("ffn2_w_gate", True), "ffn2_u": ("ffn2_w_up", True), "ffn2_d": ("ffn2_w_down", False)}


def _pack_unit(p, unit):
    if unit == "mid":
        return _pack_mid(p)
    name, transposed = UNIT_WEIGHT[unit]
    return p[name][0].T if transposed else p[name][0]


def _unpack_unit(a, unit):
    if unit != "mid":
        name, transposed = UNIT_WEIGHT[unit]
        return {name: (a.T if transposed else a)[None]}
    seg = lambda n: a[SEG_OFF[n][0]:SEG_OFF[n][0] + SEG_OFF[n][1]]
    return {"w_in": _unpad_w_in(seg("w_in"))[None], "w_out": seg("w_out")[None], "w_mq": seg("w_mq")[None],
            "w_mo": seg("w_mo")[None], "w_mkv": seg("w_mkv").reshape(D_MODEL, 256)[None],
            "w_q_up": seg("w_q")[:24].reshape(96, Q_RANK).T[None],
            "w_kv_up": seg("w_kv").reshape(KV_RANK, 128)[None]}


def _unpack_gathered(full, group):
    if group != "mid":
        return {group: full.reshape(len(GROUP_SEGS[group]), -1, D_MODEL)}
    full = full[0]
    seg = lambda n: full[:, SEG_OFF[n][0]:SEG_OFF[n][0] + SEG_OFF[n][1]]
    rows = lambda n: seg(n).reshape(-1, D_MODEL)
    wq_t = seg("w_q")[:, :24].reshape(MLA_HEADS, NOPE + ROPE, Q_RANK)
    wq_t = jnp.pad(wq_t, ((0, 0), (0, HEAD_PAD - NOPE - ROPE), (0, 0))).reshape(MLA_HEADS * HEAD_PAD, Q_RANK)
    wkv = seg("w_kv").reshape(N_DEV, KV_RANK, 128).transpose(1, 0, 2).reshape(KV_RANK, N_DEV * 128)
    return {"w_in": rows("w_in"), "w_out": rows("w_out"), "w_mq": rows("w_mq"), "w_mo": rows("w_mo"),
            "w_mkv": seg("w_mkv").reshape(N_DEV, D_MODEL, 256), "w_q": wq_t, "w_kv": wkv}


def _pack_grads(gr):
    blk = lambda a: a.reshape(N_DEV, -1, D_MODEL)
    dwq = gr["w_q"].reshape(MLA_HEADS, HEAD_PAD, Q_RANK)[:, :NOPE + ROPE].reshape(N_DEV, 24, D_MODEL)
    dwq = jnp.pad(dwq, ((0, 0), (0, 8), (0, 0)))
    dwkv = gr["w_kv"].reshape(KV_RANK, N_DEV, 128).transpose(1, 0, 2).reshape(N_DEV, 16, D_MODEL)
    parts = [blk(gr["w_in"]), blk(gr["w_out"]), blk(gr["w_mq"]), blk(gr["w_mo"]),
             gr["w_mkv"].reshape(N_DEV, 256, D_MODEL), dwq, dwkv]
    return jnp.concatenate([a.astype(BF16) for a in parts], axis=1)


def _pack_small(vals):
    parts = []
    for n, r in SMALL_ROWS:
        parts.append(_pad_rows(vals[n].reshape(-1, 128), r) if n in vals else jnp.zeros((r, 128), F32))
    return jnp.concatenate(parts, axis=0)


def _unpack_small(a, shapes):
    out = {}
    for n, shape in shapes.items():
        o = SMALL_OFF[n][0]
        out[n] = a[o:o + int(np.prod(shape)) // 128].reshape(shape)
    return out


BIG_NAMES = ("ffn1_w_gate", "ffn1_w_up", "ffn1_w_down", "w_in", "w_q_up", "w_kv_up", "w_out", "w_mq", "w_mkv",
             "w_mo", "ffn2_w_gate", "ffn2_w_up", "ffn2_w_down")
SMALL_NAMES = ("ffn1_norm", "mix_norm", "q_norm", "kv_norm", "pool_w", "pool_scale", "xattn_norm", "mem_norm",
               "ffn2_norm", "final_norm")
WEIGHT_ORDER = ("ffn1_norm", "ffn1_w_gate", "ffn1_w_up", "ffn1_w_down", "mix_norm", "w_in", "q_norm", "w_q_up",
                "kv_norm", "w_kv_up", "pool_w", "pool_scale", "w_out", "xattn_norm", "mem_norm", "w_mq", "w_mkv",
                "w_mo", "ffn2_norm", "ffn2_w_gate", "ffn2_w_up", "ffn2_w_down", "final_norm")


def _rope_table():
    lane = np.arange(128)
    freqs = (1.0 / (ROPE_BASE ** (np.arange(0, ROPE, 2, dtype=np.float32) / ROPE))).astype(np.float32)
    tab = np.zeros((8, 128), np.float32)
    tab[0] = np.where(lane < ROPE, freqs[lane % (ROPE // 2)], 0.0)
    tab[1] = np.where(lane < ROPE // 2, -1.0, np.where(lane < ROPE, 1.0, 0.0))
    return jnp.asarray(tab)


def kernel(x, mem, positions, ffn1_norm, ffn1_w_gate, ffn1_w_up, ffn1_w_down, mix_norm, w_in, q_norm, w_q_up, kv_norm, w_kv_up, pool_w, pool_scale, w_out, xattn_norm, mem_norm, w_mq, w_mkv, w_mo, ffn2_norm, ffn2_w_gate, ffn2_w_up, ffn2_w_down, final_norm, loss_target, m_ffn1_norm, m_ffn1_w_gate, m_ffn1_w_up, m_ffn1_w_down, m_mix_norm, m_w_in, m_q_norm, m_w_q_up, m_kv_norm, m_w_kv_up, m_pool_w, m_pool_scale, m_w_out, m_xattn_norm, m_mem_norm, m_w_mq, m_w_mkv, m_w_mo, m_ffn2_norm, m_ffn2_w_gate, m_ffn2_w_up, m_ffn2_w_down, m_final_norm, v_ffn1_norm, v_ffn1_w_gate, v_ffn1_w_up, v_ffn1_w_down, v_mix_norm, v_w_in, v_q_norm, v_w_q_up, v_kv_norm, v_w_kv_up, v_pool_w, v_pool_scale, v_w_out, v_xattn_norm, v_mem_norm, v_w_mq, v_w_mkv, v_w_mo, v_ffn2_norm, v_ffn2_w_gate, v_ffn2_w_up, v_ffn2_w_down, v_final_norm):
    wts = dict(ffn1_norm=ffn1_norm, ffn1_w_gate=ffn1_w_gate, ffn1_w_up=ffn1_w_up, ffn1_w_down=ffn1_w_down,
               mix_norm=mix_norm, w_in=w_in, q_norm=q_norm, w_q_up=w_q_up, kv_norm=kv_norm, w_kv_up=w_kv_up,
               pool_w=pool_w, pool_scale=pool_scale, w_out=w_out, xattn_norm=xattn_norm, mem_norm=mem_norm,
               w_mq=w_mq, w_mkv=w_mkv, w_mo=w_mo, ffn2_norm=ffn2_norm, ffn2_w_gate=ffn2_w_gate,
               ffn2_w_up=ffn2_w_up, ffn2_w_down=ffn2_w_down, final_norm=final_norm)
    mom = dict(ffn1_norm=m_ffn1_norm, ffn1_w_gate=m_ffn1_w_gate, ffn1_w_up=m_ffn1_w_up, ffn1_w_down=m_ffn1_w_down,
               mix_norm=m_mix_norm, w_in=m_w_in, q_norm=m_q_norm, w_q_up=m_w_q_up, kv_norm=m_kv_norm,
               w_kv_up=m_w_kv_up, pool_w=m_pool_w, pool_scale=m_pool_scale, w_out=m_w_out, xattn_norm=m_xattn_norm,
               mem_norm=m_mem_norm, w_mq=m_w_mq, w_mkv=m_w_mkv, w_mo=m_w_mo, ffn2_norm=m_ffn2_norm,
               ffn2_w_gate=m_ffn2_w_gate, ffn2_w_up=m_ffn2_w_up, ffn2_w_down=m_ffn2_w_down, final_norm=m_final_norm)
    var = dict(ffn1_norm=v_ffn1_norm, ffn1_w_gate=v_ffn1_w_gate, ffn1_w_up=v_ffn1_w_up, ffn1_w_down=v_ffn1_w_down,
               mix_norm=v_mix_norm, w_in=v_w_in, q_norm=v_q_norm, w_q_up=v_w_q_up, kv_norm=v_kv_norm,
               w_kv_up=v_w_kv_up, pool_w=v_pool_w, pool_scale=v_pool_scale, w_out=v_w_out, xattn_norm=v_xattn_norm,
               mem_norm=v_mem_norm, w_mq=v_w_mq, w_mkv=v_w_mkv, w_mo=v_w_mo, ffn2_norm=v_ffn2_norm,
               ffn2_w_gate=v_ffn2_w_gate, ffn2_w_up=v_ffn2_w_up, ffn2_w_down=v_ffn2_w_down, final_norm=v_final_norm)

    t = x.shape[1]
    xs = x[0]
    mems = mem[0]
    target = loss_target[0]
    pos = positions.reshape(t, 1)
    row = lambda a: a.reshape(1, -1)
    rope_tab = _rope_table()

    cx, cy, cc = _coords()
    chip_idx = (2 * cx + cy).astype(jnp.int32).reshape(1)

    wb = {}
    for grp in ("ffn1", "mid", "ffn2"):
        wb[grp] = _pack_segments(wts, grp)
        if grp == "ffn1":
            near_ffn1 = _near_start(wb["ffn1"], pos, "ag_ffn1_near_start")
    mid_names = ("w_in", "w_out", "w_mq", "w_mo", "w_mkv", "w_q_up", "w_kv_up")

    states = (wts, mom, var)

    def packed_during(token, units, after, states):
        one = 1.0 + token[0, 0]
        packs = {}
        for u in units:
            names = SMALL_NAMES if u == "small" else mid_names if u == "mid" else UNIT_WEIGHT[u][:1]
            held = [{n: p[n] * one for n in names} for p in states]
            packs[u] = tuple(_pack_small(h) if u == "small" else _pack_unit(h, u) for h in held)
        return lax.optimization_barrier((after, packs))

    after, adam_in = packed_during(near_ffn1[4], ("ffn1_g", "ffn1_u", "ffn1_d", "ffn2_g", "ffn2_u", "ffn2_d"),
                                   wb["ffn2"], states)
    after, mid_w = packed_during(near_ffn1[4], ("mid",), after, states[:1])
    own_ffn1, land_ffn1 = _near_wait(near_ffn1, after, "ag_ffn1_near_wait")
    far_ffn1 = _far_start(land_ffn1, own_ffn1, "ag_ffn1_far_start")
    after, more = packed_during(far_ffn1[3], ("small",), wb["mid"], states)
    after, mid_mv = packed_during(far_ffn1[3], ("mid",), after, states[1:])
    adam_in.update(more)
    adam_in["mid"] = mid_w["mid"] + mid_mv["mid"]
    land_ffn1 = _far_wait(far_ffn1, after, "ag_ffn1_far_wait")
    full_ffn1 = _core_share(own_ffn1, land_ffn1, "ag_ffn1_share")
    fw = _unpack_gathered(full_ffn1, "ffn1")
    ag_mid = _ici_start(wb["mid"], full_ffn1, "ag_mid_start", True)
    g_ffn1, g_mix, g_q, g_kv = row(ffn1_norm), row(mix_norm), row(q_norm), row(kv_norm)
    g_x, g_mem, g_ffn2, g_fin = row(xattn_norm), row(mem_norm), row(ffn2_norm), row(final_norm)
    pool_wb = pool_w[0].astype(BF16)
    pool_sc = row(pool_scale)

    h1, n1, gate1, up1 = _ffn_fwd(xs, g_ffn1, fw["ffn1"], "ffn1_fwd", token=ag_mid[4])
    own_mid, land_mid = _ici_wait(ag_mid, h1, "ag_mid_wait", True)
    full_mid = _core_share(own_mid, land_mid, "ag_mid_share")
    fw.update(_unpack_gathered(full_mid, "mid"))
    ag_ffn2 = _ici_start(wb["ffn2"], full_mid, "ag_ffn2_start", True)
    u, z, qn, kvn, qh, kh, vh = _mix_prep(h1, g_mix, fw["w_in"], g_q, fw["w_q"], g_kv, fw["w_kv"], pos, rope_tab,
                                          token=ag_ffn2[4])
    a, lse = _attn_fwd(qh, kh, vh)
    p = _pool_fwd(z, pool_wb, pool_sc)
    memn, km, vm = _mem_kv(mems, g_mem, fw["w_mkv"])
    own_ffn2, land_ffn2 = _ici_wait(ag_ffn2, a, "ag_ffn2_wait", True)
    land_ffn2 = lax.dynamic_update_slice(land_ffn2, own_ffn2[:, None], (0, 4 * cx + 2 * cy + cc, 0, 0))
    share_ffn2 = _share_start(land_ffn2, a, "ag_ffn2_share_start")
    h2, h3, hn, qm, om = _xattn_fwd(h1, a, p, fw["w_out"], g_x, fw["w_mq"], km, vm, fw["w_mo"], token=share_ffn2[3])
    fw.update(_unpack_gathered(_share_wait(share_ffn2, h3, "ag_ffn2_share_wait"), "ffn2"))
    dh4, n2, gate2, up2, loss_part, dg_fin = _ffn_fwd(h3, g_ffn2, fw["ffn2"],
                                                      "ffn2_fwd", head=(target, g_fin))

    def reduce_start(g8, unit):
        part = _core_reduce(g8, unit)
        return _ici_start(part, g8, "rs_" + unit + "_start", False)

    def by_device(g):
        return g.reshape(N_DEV, -1, D_MODEL)

    rs = {}
    dh3, dgate2, dup2, act2, dg_ffn2 = _ffn_bwd_data(dh4, h3, g_ffn2, gate2, up2, fw["ffn2"], "ffn2_bwd")
    rs["ffn2_g"] = reduce_start(by_device(_tn_matmul(dgate2, n2, "ffn2_dwg", tmm=1408, m=D_FF, out_dtype=BF16)), "ffn2_g")
    rs["ffn2_u"] = reduce_start(by_device(_tn_matmul(dup2, n2, "ffn2_dwu", tmm=1408, m=D_FF, out_dtype=BF16,
                                                     token=rs["ffn2_g"][4])), "ffn2_u")
    rs["ffn2_d"] = reduce_start(by_device(_tn_matmul(act2, dh4, "ffn2_dwd", scale=0.5, tmm=1408, m=D_FF, out_dtype=BF16,
                                                     token=rs["ffn2_u"][4])), "ffn2_d")
    dh2, dqm, da, dp, dkm, dvm, dg_x = _xattn_bwd(dh3, h2, qm, g_x, fw["w_mq"], km, vm, fw["w_mo"], fw["w_out"],
                                                  token=rs["ffn2_d"][4])
    gr = {}
    gr["w_mo"] = _tn_matmul(om, dh3, "dw_mo", out_dtype=BF16)
    gr["w_mq"] = _tn_matmul(hn, dqm, "dw_mq", out_dtype=BF16)
    gr["w_out"] = jnp.concatenate([_tn_matmul(a, dh2, "dw_out_a", out_dtype=BF16),
                                   _tn_matmul(p, dh2, "dw_out_p", out_dtype=BF16)], axis=0)
    gr["w_mkv"], dg_mem = _mem_kv_bwd(dkm, dvm, memn, mems, g_mem, fw["w_mkv"])
    dz_pool, d_pool_w, d_pool_sc = _pool_bwd(dp, z, pool_wb, pool_sc)
    dqh, dkh, dvh = _attn_bwd(qh, kh, vh, da, lse, _attn_delta(a, da))
    dh1, dq, dkv, dz, dg_q, dg_kv, dg_mix = _mla_bwd(dqh, dkh, dvh, z, dz_pool, h1, dh2, g_mix, fw["w_in"], g_q,
                                                     fw["w_q"], g_kv, fw["w_kv"], pos, rope_tab)
    gr["w_q"] = _tn_matmul(dq, qn, "dw_q", out_dtype=BF16)
    gr["w_kv"] = _tn_matmul(kvn, dkv, "dw_kv", out_dtype=BF16)
    gr["w_in"] = _tn_matmul(u, dz, "dw_in", out_dtype=BF16)
    g_mid = _pack_grads(gr)
    part_mid = _core_reduce(g_mid, "mid")
    got = {}
    after = part_mid
    for unit in ("ffn2_g", "ffn2_u", "ffn2_d"):
        got[unit] = _ici_wait(rs[unit], after, "rs_" + unit + "_wait", False)
        after = got[unit][1]
    rs["mid"] = _ici_start(part_mid, after, "rs_mid_start", False)
    dx, dgate1, dup1, act1, dg_ffn1 = _ffn_bwd_data(dh1, xs, g_ffn1, gate1, up1, fw["ffn1"], "ffn1_bwd", token=rs["mid"][4])
    got["mid"] = _ici_wait(rs["mid"], dx, "rs_mid_wait", False)

    small_g = dict(ffn1_norm=dg_ffn1, mix_norm=dg_mix, q_norm=dg_q, kv_norm=dg_kv, pool_w=d_pool_w,
                   pool_scale=d_pool_sc, xattn_norm=dg_x, mem_norm=dg_mem, ffn2_norm=dg_ffn2, final_norm=dg_fin,
                   loss=loss_part)
    small_ag = _peers_start(_pack_small(small_g), got["mid"][1], "small_ag_start")
    rs["ffn1_g"] = reduce_start(by_device(_tn_matmul(dgate1, n1, "ffn1_dwg", tmm=1408, m=D_FF, out_dtype=BF16,
                                                     token=small_ag[4])), "ffn1_g")
    _, parts = _peers_wait(small_ag, rs["ffn1_g"][4], "small_ag_wait")
    small = _adam_small(parts, *adam_in["small"])
    small_sum = small[0]
    loss = small_sum[SMALL_OFF["loss"][0], 0]
    shapes = {n: wts[n].shape for n in SMALL_NAMES}
    small = [_unpack_small(s, shapes) for s in small]

    rs["ffn1_u"] = reduce_start(by_device(_tn_matmul(dup1, n1, "ffn1_dwu", tmm=1408, m=D_FF, out_dtype=BF16,
                                                     token=small_sum)), "ffn1_u")
    rs["ffn1_d"] = reduce_start(by_device(_tn_matmul(act1, dh1, "ffn1_dwd", scale=0.5, tmm=1408, m=D_FF, out_dtype=BF16,
                                                     token=rs["ffn1_u"][4])), "ffn1_d")

    big = {}

    def adam_units(names, token):
        units = [got[u] + adam_in[u] for u in names]
        res = _adam_big(units, chip_idx, "_".join(names), token)
        for u, four in zip(names, res):
            for k, packed in enumerate(four):
                big.setdefault(k, {}).update(_unpack_unit(packed, u))
        return res[-1][0]

    done = adam_units(["mid"], rs["ffn1_d"][4])
    done = adam_units(["ffn2_g", "ffn2_u", "ffn2_d"], done)
    got["ffn1_g"] = _ici_wait(rs["ffn1_g"], done, "rs_ffn1_g_wait", False)
    got["ffn1_u"] = _ici_wait(rs["ffn1_u"], got["ffn1_g"][1], "rs_ffn1_u_wait", False)
    done = adam_units(["ffn1_g", "ffn1_u"], done)
    got["ffn1_d"] = _ici_wait(rs["ffn1_d"], done, "rs_ffn1_d_wait", False)
    adam_units(["ffn1_d"], done)

    outs = [loss, dx[None]]
    for k in range(4):
        for n in WEIGHT_ORDER:
            outs.append(big[k][n] if n in BIG_NAMES else small[k][n])
    return tuple(outs)
```
